```python
import math
import jax, jax.numpy as jnp
from jax import lax
import numpy as np

D_MODEL = 1024
BATCH = 32
SEQ = 2048
DEPTH = 1

N_HEADS = 8
HEAD_DIM = 64
ATT_WIDTH = N_HEADS * HEAD_DIM
PATTERNS = ((128, 1), (512, 4), (2048, 16))
SSM_GROUPS = 16
SSM_GROUP_CH = 16
SSM_WIDTH = SSM_GROUPS * SSM_GROUP_CH
SSM_STATE = 64
D_FF = 2048
CONV_W = 3
IN_WIDTH = 3 * ATT_WIDTH + SSM_WIDTH + 2 * D_MODEL
EPS = 1e-6
NEG_INF = -1e30

kernel_name = "gated_dilated_attn_s5_hybrid_block"


def _rmsnorm(x, g):
    x32 = x.astype(jnp.float32)
    y = x32 * lax.rsqrt(jnp.mean(x32 * x32, axis=-1, keepdims=True) + EPS)
    return y.astype(x.dtype) * g


def _modulate(x, g, shift, scale):
    return _rmsnorm(x, g) * (1 + scale[:, None, :]) + shift[:, None, :]


def _alibi_slopes():
    return np.array([2.0 ** (-8.0 * (h + 1) / N_HEADS) for h in range(N_HEADS)], dtype=np.float32)


def _dilated_window_attention(q, k, v, slopes, window, dilation):
    b, s, h, hd = q.shape
    w = window // dilation
    L = s // dilation
    nb = -(-L // w)
    Lp = nb * w
    X = b * dilation

    def to_sub(t):
        t = t.reshape(b, L, dilation, h, hd).transpose(0, 2, 3, 1, 4)
        return t.reshape(X, h, L, hd)

    qs, ks, vs = to_sub(q), to_sub(k), to_sub(v)
    qb = jnp.pad(qs, ((0, 0), (0, 0), (0, Lp - L), (0, 0))).reshape(X, h, nb, w, hd)
    kp = jnp.pad(ks, ((0, 0), (0, 0), (w, Lp - L), (0, 0)))
    vp = jnp.pad(vs, ((0, 0), (0, 0), (w, Lp - L), (0, 0)))
    kb = jnp.concatenate([kp[:, :, :Lp].reshape(X, h, nb, w, hd),
                          kp[:, :, w:].reshape(X, h, nb, w, hd)], axis=3)
    vb = jnp.concatenate([vp[:, :, :Lp].reshape(X, h, nb, w, hd),
                          vp[:, :, w:].reshape(X, h, nb, w, hd)], axis=3)

    a_idx = np.arange(w)[:, None]
    j_idx = np.arange(2 * w)[None, :]
    dist = (w + a_idx - j_idx).astype(np.float32)
    kpos = np.arange(nb)[:, None, None] * w - w + j_idx[None]
    valid = (dist[None] >= 0) & (dist[None] <= w) & (kpos >= 0)
    bias = -(slopes[:, None, None] * dilation) * dist[None]

    scale = HEAD_DIM ** -0.5
    sc = jnp.einsum('xhnqd,xhnkd->xhnqk', qb, kb).astype(jnp.float32) * scale
    sc = jnp.where(valid, sc + bias[:, None], NEG_INF)
    m = jnp.max(sc, axis=-1, keepdims=True)
    p = jnp.exp(sc - m)
    den = jnp.sum(p, axis=-1, keepdims=True)
    o = jnp.einsum('xhnqk,xhnkd->xhnqd', p, vb.astype(jnp.float32)) / den
    lse = (m + jnp.log(den))[..., 0]

    o = o.reshape(X, h, Lp, hd)[:, :, :L].reshape(b, dilation, h, L, hd)
    o = o.transpose(0, 3, 1, 2, 4).reshape(b, s, h, hd)
    lse = lse.reshape(X, h, Lp)[:, :, :L].reshape(b, dilation, h, L)
    lse = lse.transpose(0, 3, 1, 2).reshape(b, s, h)
    return o, lse


def _s5_branch(u, a_re, a_im, log_dt, b_re, b_im, c_re, c_im, d_skip, w_glu, b_glu):
    f32 = jnp.float32
    bsz, s, _ = u.shape
    lr, li = a_re.astype(f32), a_im.astype(f32)
    dt = jnp.exp(log_dt.astype(f32))[:, None]
    mag = jnp.exp(lr * dt)
    ang = li * dt
    ab_re, ab_im = mag * jnp.cos(ang), mag * jnp.sin(ang)
    nr, ni = ab_re - 1.0, ab_im
    den = lr * lr + li * li
    f_re = (nr * lr + ni * li) / den
    f_im = (ni * lr - nr * li) / den
    br, bi = b_re.astype(f32), b_im.astype(f32)
    bb_re = f_re[..., None] * br - f_im[..., None] * bi
    bb_im = f_re[..., None] * bi + f_im[..., None] * br

    ug = u.astype(f32).reshape(bsz, s, SSM_GROUPS, SSM_GROUP_CH)
    bu_re = jnp.einsum('bsgc,gnc->bsgn', ug, bb_re)
    bu_im = jnp.einsum('bsgc,gnc->bsgn', ug, bb_im)
    a_re_t = jnp.broadcast_to(ab_re, bu_re.shape)
    a_im_t = jnp.broadcast_to(ab_im, bu_re.shape)

    def combine(left, right):
        ar1, ai1, xr1, xi1 = left
        ar2, ai2, xr2, xi2 = right
        return (ar2 * ar1 - ai2 * ai1,
                ar2 * ai1 + ai2 * ar1,
                ar2 * xr1 - ai2 * xi1 + xr2,
                ar2 * xi1 + ai2 * xr1 + xi2)

    _, _, xr, xi = lax.associative_scan(combine, (a_re_t, a_im_t, bu_re, bu_im), axis=1)
    y = (jnp.einsum('bsgn,gcn->bsgc', xr, c_re.astype(f32))
         - jnp.einsum('bsgn,gcn->bsgc', xi, c_im.astype(f32))
         + d_skip.astype(f32).reshape(SSM_GROUPS, SSM_GROUP_CH) * ug)
    y = y.reshape(bsz, s, SSM_WIDTH).astype(u.dtype)
    y = jax.nn.gelu(y)
    return y * jax.nn.sigmoid(y @ w_glu + b_glu)


def _hybrid_mixer(u, w_in, b_gate, a_re, a_im, log_dt, b_re, b_im, c_re, c_im,
                  d_skip, w_glu, b_glu, w_proj_att, w_proj_ssm, w_out):
    bsz, s, _ = u.shape
    proj = u @ w_in
    q, k, v, us, g_att, g_ssm = jnp.split(
        proj, [ATT_WIDTH, 2 * ATT_WIDTH, 3 * ATT_WIDTH, 3 * ATT_WIDTH + SSM_WIDTH,
               3 * ATT_WIDTH + SSM_WIDTH + D_MODEL], axis=-1)
    q = q.reshape(bsz, s, N_HEADS, HEAD_DIM)
    k = k.reshape(bsz, s, N_HEADS, HEAD_DIM)
    v = v.reshape(bsz, s, N_HEADS, HEAD_DIM)

    slopes = _alibi_slopes()
    outs, lses = [], []
    for window, dilation in PATTERNS:
        o, lse = _dilated_window_attention(q, k, v, slopes, window, dilation)
        outs.append(o)
        lses.append(lse)
    wts = jax.nn.softmax(jnp.stack(lses, axis=0), axis=0)
    o_att = jnp.sum(wts[..., None] * jnp.stack(outs, axis=0), axis=0)
    o_att = o_att.reshape(bsz, s, ATT_WIDTH).astype(u.dtype)
    y_att = o_att @ w_proj_att

    y_ssm = _s5_branch(us, a_re, a_im, log_dt, b_re, b_im, c_re, c_im,
                       d_skip, w_glu, b_glu) @ w_proj_ssm

    gb_att, gb_ssm = jnp.split(b_gate, 2, axis=-1)
    merged = jax.nn.sigmoid(g_att + gb_att) * y_att + jax.nn.sigmoid(g_ssm + gb_ssm) * y_ssm
    return merged @ w_out


def _conv_ffn(u, w_up, w_conv, b_conv, w_down):
    s = u.shape[1]
    a, val = jnp.split(u @ w_up, 2, axis=-1)
    ap = jnp.pad(a, ((0, 0), (CONV_W - 1, 0), (0, 0)))
    conv = b_conv
    for j in range(CONV_W):
        conv = conv + w_conv[j] * ap[:, CONV_W - 1 - j:CONV_W - 1 - j + s]
    return (jax.nn.silu(conv) * val) @ w_down


def _fwd_setup_inputs(seed: int = 0) -> dict:
    key = jax.random.key(seed)
    ks = jax.random.split(key, 32)
    f32 = jnp.float32
    L, D, G, N, C = DEPTH, D_MODEL, SSM_GROUPS, SSM_STATE, SSM_GROUP_CH
    nrm = lambda k, shape, sc: jax.random.normal(k, shape, f32) * sc
    inp = {}
    inp["x"] = nrm(ks[0], (BATCH, SEQ, D), 1.0)
    inp["c"] = nrm(ks[1], (BATCH, D), 1.0)
    inp["w_ada"] = nrm(ks[2], (L, D, 6 * D), 0.5 * D ** -0.5)
    inp["b_ada"] = nrm(ks[3], (L, 6 * D), 0.02)
    inp["g_mix"] = 1.0 + nrm(ks[4], (L, D), 0.02)
    inp["w_in"] = nrm(ks[5], (L, D, IN_WIDTH), D ** -0.5)
    inp["b_gate"] = nrm(ks[6], (L, 2 * D), 0.02)
    inp["a_re"] = -0.5 + nrm(ks[7], (L, G, N), 0.01)
    inp["a_im"] = jnp.pi * jnp.arange(N, dtype=f32)[None, None, :] + nrm(ks[8], (L, G, N), 0.01)
    inp["log_dt"] = jax.random.uniform(ks[9], (L, G), f32, math.log(1e-3), math.log(1e-1))
    inp["b_re"] = nrm(ks[10], (L, G, N, C), (2 * C) ** -0.5)
    inp["b_im"] = nrm(ks[11], (L, G, N, C), (2 * C) ** -0.5)
    inp["c_re"] = nrm(ks[12], (L, G, C, N), (2 * N) ** -0.5)
    inp["c_im"] = nrm(ks[13], (L, G, C, N), (2 * N) ** -0.5)
    inp["d_skip"] = nrm(ks[14], (L, SSM_WIDTH), 1.0)
    inp["w_glu"] = nrm(ks[15], (L, SSM_WIDTH, SSM_WIDTH), SSM_WIDTH ** -0.5)
    inp["b_glu"] = nrm(ks[16], (L, SSM_WIDTH), 0.02)
    inp["w_proj_att"] = nrm(ks[17], (L, ATT_WIDTH, D), ATT_WIDTH ** -0.5)
    inp["w_proj_ssm"] = nrm(ks[18], (L, SSM_WIDTH, D), SSM_WIDTH ** -0.5)
    inp["w_out"] = nrm(ks[19], (L, D, D), D ** -0.5)
    inp["g_ffn"] = 1.0 + nrm(ks[20], (L, D), 0.02)
    inp["w_up"] = nrm(ks[21], (L, D, 2 * D_FF), D ** -0.5)
    inp["w_conv"] = nrm(ks[22], (L, CONV_W, D_FF), CONV_W ** -0.5)
    inp["b_conv"] = nrm(ks[23], (L, D_FF), 0.02)
    inp["w_down"] = nrm(ks[24], (L, D_FF, D), D_FF ** -0.5)
    inp["g_final"] = 1.0 + nrm(ks[25], (D,), 0.02)
    return inp


def _fwd_reference(x, c, w_ada, b_ada, g_mix, w_in, b_gate, a_re, a_im, log_dt, b_re, b_im,
              c_re, c_im, d_skip, w_glu, b_glu, w_proj_att, w_proj_ssm, w_out,
              g_ffn, w_up, w_conv, b_conv, w_down, g_final):
    h = x
    c_act = jax.nn.silu(c)
    for l in range(DEPTH):
        mod = c_act @ w_ada[l] + b_ada[l]
        sh1, sc1, gt1, sh2, sc2, gt2 = jnp.split(mod, 6, axis=-1)
        u = _modulate(h, g_mix[l], sh1, sc1)
        h = h + gt1[:, None, :] * _hybrid_mixer(
            u, w_in[l], b_gate[l], a_re[l], a_im[l], log_dt[l], b_re[l], b_im[l],
            c_re[l], c_im[l], d_skip[l], w_glu[l], b_glu[l],
            w_proj_att[l], w_proj_ssm[l], w_out[l])
        u = _modulate(h, g_ffn[l], sh2, sc2)
        h = h + gt2[:, None, :] * _conv_ffn(u, w_up[l], w_conv[l], b_conv[l], w_down[l])
    return _rmsnorm(h, g_final)


import jax as _jax
import jax.numpy as _jnp

TWIN_FORMAT = 'train_step'
FWD_PARAMS = ['x', 'c', 'w_ada', 'b_ada', 'g_mix', 'w_in', 'b_gate', 'a_re', 'a_im', 'log_dt', 'b_re', 'b_im', 'c_re', 'c_im', 'd_skip', 'w_glu', 'b_glu', 'w_proj_att', 'w_proj_ssm', 'w_out', 'g_ffn', 'w_up', 'w_conv', 'b_conv', 'w_down', 'g_final']
TWIN_WEIGHTS = ['w_ada', 'b_ada', 'g_mix', 'w_in', 'b_gate', 'a_re', 'a_im', 'log_dt', 'b_re', 'b_im', 'c_re', 'c_im', 'd_skip', 'w_glu', 'b_glu', 'w_proj_att', 'w_proj_ssm', 'w_out', 'g_ffn', 'w_up', 'w_conv', 'b_conv', 'w_down', 'g_final']
TWIN_DIFF_INPUT = 'x'
TWIN_INPUTS = ['x', 'c', 'w_ada', 'b_ada', 'g_mix', 'w_in', 'b_gate', 'a_re', 'a_im', 'log_dt', 'b_re', 'b_im', 'c_re', 'c_im', 'd_skip', 'w_glu', 'b_glu', 'w_proj_att', 'w_proj_ssm', 'w_out', 'g_ffn', 'w_up', 'w_conv', 'b_conv', 'w_down', 'g_final', 'loss_target', 'm_w_ada', 'm_b_ada', 'm_g_mix', 'm_w_in', 'm_b_gate', 'm_a_re', 'm_a_im', 'm_log_dt', 'm_b_re', 'm_b_im', 'm_c_re', 'm_c_im', 'm_d_skip', 'm_w_glu', 'm_b_glu', 'm_w_proj_att', 'm_w_proj_ssm', 'm_w_out', 'm_g_ffn', 'm_w_up', 'm_w_conv', 'm_b_conv', 'm_w_down', 'm_g_final', 'v_w_ada', 'v_b_ada', 'v_g_mix', 'v_w_in', 'v_b_gate', 'v_a_re', 'v_a_im', 'v_log_dt', 'v_b_re', 'v_b_im', 'v_c_re', 'v_c_im', 'v_d_skip', 'v_w_glu', 'v_b_glu', 'v_w_proj_att', 'v_w_proj_ssm', 'v_w_out', 'v_g_ffn', 'v_w_up', 'v_w_conv', 'v_b_conv', 'v_w_down', 'v_g_final']
TWIN_OUTPUTS = ['loss', 'grad_x', 'grad_w_ada', 'grad_b_ada', 'grad_g_mix', 'grad_w_in', 'grad_b_gate', 'grad_a_re', 'grad_a_im', 'grad_log_dt', 'grad_b_re', 'grad_b_im', 'grad_c_re', 'grad_c_im', 'grad_d_skip', 'grad_w_glu', 'grad_b_glu', 'grad_w_proj_att', 'grad_w_proj_ssm', 'grad_w_out', 'grad_g_ffn', 'grad_w_up', 'grad_w_conv', 'grad_b_conv', 'grad_w_down', 'grad_g_final', 'delta_w_ada', 'delta_b_ada', 'delta_g_mix', 'delta_w_in', 'delta_b_gate', 'delta_a_re', 'delta_a_im', 'delta_log_dt', 'delta_b_re', 'delta_b_im', 'delta_c_re', 'delta_c_im', 'delta_d_skip', 'delta_w_glu', 'delta_b_glu', 'delta_w_proj_att', 'delta_w_proj_ssm', 'delta_w_out', 'delta_g_ffn', 'delta_w_up', 'delta_w_conv', 'delta_b_conv', 'delta_w_down', 'delta_g_final', 'new_m_w_ada', 'new_m_b_ada', 'new_m_g_mix', 'new_m_w_in', 'new_m_b_gate', 'new_m_a_re', 'new_m_a_im', 'new_m_log_dt', 'new_m_b_re', 'new_m_b_im', 'new_m_c_re', 'new_m_c_im', 'new_m_d_skip', 'new_m_w_glu', 'new_m_b_glu', 'new_m_w_proj_att', 'new_m_w_proj_ssm', 'new_m_w_out', 'new_m_g_ffn', 'new_m_w_up', 'new_m_w_conv', 'new_m_b_conv', 'new_m_w_down', 'new_m_g_final', 'new_v_w_ada', 'new_v_b_ada', 'new_v_g_mix', 'new_v_w_in', 'new_v_b_gate', 'new_v_a_re', 'new_v_a_im', 'new_v_log_dt', 'new_v_b_re', 'new_v_b_im', 'new_v_c_re', 'new_v_c_im', 'new_v_d_skip', 'new_v_w_glu', 'new_v_b_glu', 'new_v_w_proj_att', 'new_v_w_proj_ssm', 'new_v_w_out', 'new_v_g_ffn', 'new_v_w_up', 'new_v_w_conv', 'new_v_b_conv', 'new_v_w_down', 'new_v_g_final']
TWIN_LEAF_KINDS = {'loss': 'loss', 'grad_x': 'grad_x', 'grad_w_ada': 'grad_w', 'grad_b_ada': 'grad_w', 'grad_g_mix': 'grad_w', 'grad_w_in': 'grad_w', 'grad_b_gate': 'grad_w', 'grad_a_re': 'grad_w', 'grad_a_im': 'grad_w', 'grad_log_dt': 'grad_w', 'grad_b_re': 'grad_w', 'grad_b_im': 'grad_w', 'grad_c_re': 'grad_w', 'grad_c_im': 'grad_w', 'grad_d_skip': 'grad_w', 'grad_w_glu': 'grad_w', 'grad_b_glu': 'grad_w', 'grad_w_proj_att': 'grad_w', 'grad_w_proj_ssm': 'grad_w', 'grad_w_out': 'grad_w', 'grad_g_ffn': 'grad_w', 'grad_w_up': 'grad_w', 'grad_w_conv': 'grad_w', 'grad_b_conv': 'grad_w', 'grad_w_down': 'grad_w', 'grad_g_final': 'grad_w', 'delta_w_ada': 'delta_w', 'delta_b_ada': 'delta_w', 'delta_g_mix': 'delta_w', 'delta_w_in': 'delta_w', 'delta_b_gate': 'delta_w', 'delta_a_re': 'delta_w', 'delta_a_im': 'delta_w', 'delta_log_dt': 'delta_w', 'delta_b_re': 'delta_w', 'delta_b_im': 'delta_w', 'delta_c_re': 'delta_w', 'delta_c_im': 'delta_w', 'delta_d_skip': 'delta_w', 'delta_w_glu': 'delta_w', 'delta_b_glu': 'delta_w', 'delta_w_proj_att': 'delta_w', 'delta_w_proj_ssm': 'delta_w', 'delta_w_out': 'delta_w', 'delta_g_ffn': 'delta_w', 'delta_w_up': 'delta_w', 'delta_w_conv': 'delta_w', 'delta_b_conv': 'delta_w', 'delta_w_down': 'delta_w', 'delta_g_final': 'delta_w', 'new_m_w_ada': 'new_m', 'new_m_b_ada': 'new_m', 'new_m_g_mix': 'new_m', 'new_m_w_in': 'new_m', 'new_m_b_gate': 'new_m', 'new_m_a_re': 'new_m', 'new_m_a_im': 'new_m', 'new_m_log_dt': 'new_m', 'new_m_b_re': 'new_m', 'new_m_b_im': 'new_m', 'new_m_c_re': 'new_m', 'new_m_c_im': 'new_m', 'new_m_d_skip': 'new_m', 'new_m_w_glu': 'new_m', 'new_m_b_glu': 'new_m', 'new_m_w_proj_att': 'new_m', 'new_m_w_proj_ssm': 'new_m', 'new_m_w_out': 'new_m', 'new_m_g_ffn': 'new_m', 'new_m_w_up': 'new_m', 'new_m_w_conv': 'new_m', 'new_m_b_conv': 'new_m', 'new_m_w_down': 'new_m', 'new_m_g_final': 'new_m', 'new_v_w_ada': 'new_v', 'new_v_b_ada': 'new_v', 'new_v_g_mix': 'new_v', 'new_v_w_in': 'new_v', 'new_v_b_gate': 'new_v', 'new_v_a_re': 'new_v', 'new_v_a_im': 'new_v', 'new_v_log_dt': 'new_v', 'new_v_b_re': 'new_v', 'new_v_b_im': 'new_v', 'new_v_c_re': 'new_v', 'new_v_c_im': 'new_v', 'new_v_d_skip': 'new_v', 'new_v_w_glu': 'new_v', 'new_v_b_glu': 'new_v', 'new_v_w_proj_att': 'new_v', 'new_v_w_proj_ssm': 'new_v', 'new_v_w_out': 'new_v', 'new_v_g_ffn': 'new_v', 'new_v_w_up': 'new_v', 'new_v_w_conv': 'new_v', 'new_v_b_conv': 'new_v', 'new_v_w_down': 'new_v', 'new_v_g_final': 'new_v'}


def _forward(args):
    return _fwd_reference(*[args[k] for k in FWD_PARAMS])


def _output_shape():
    out = _jax.eval_shape(lambda: _forward(_fwd_setup_inputs(0)))
    return out.shape, out.dtype

N_MICROBATCH = 1
ADAM_LR = 0.001
ADAM_B1 = 0.9
ADAM_B2 = 0.999
ADAM_EPS = 1e-08
ADAM_WD = 0.01
ADAM_STEP = 10
PER_EXAMPLE_BATCH_AXIS = {'x': 0, 'c': 0, 'loss_target': 0}
SHARED_INPUTS = []
_WEIGHT_DTYPES = {'w_ada': _jnp.float32, 'b_ada': _jnp.float32, 'g_mix': _jnp.float32, 'w_in': _jnp.float32, 'b_gate': _jnp.float32, 'a_re': _jnp.float32, 'a_im': _jnp.float32, 'log_dt': _jnp.float32, 'b_re': _jnp.float32, 'b_im': _jnp.float32, 'c_re': _jnp.float32, 'c_im': _jnp.float32, 'd_skip': _jnp.float32, 'w_glu': _jnp.float32, 'b_glu': _jnp.float32, 'w_proj_att': _jnp.float32, 'w_proj_ssm': _jnp.float32, 'w_out': _jnp.float32, 'g_ffn': _jnp.float32, 'w_up': _jnp.float32, 'w_conv': _jnp.float32, 'b_conv': _jnp.float32, 'w_down': _jnp.float32, 'g_final': _jnp.float32}
MOMENT_SCALE = {'w_ada': 6.173370e-02, 'b_ada': 9.684659e-02, 'g_mix': 3.363063e-02, 'w_in': 1.875633e-02, 'b_gate': 7.549804e-03, 'a_re': 4.501251e-03, 'a_im': 4.107068e-03, 'log_dt': 2.709125e+00, 'b_re': 1.612596e-03, 'b_im': 1.466372e-03, 'c_re': 3.047710e-03, 'c_im': 3.067778e-03, 'd_skip': 3.598856e-02, 'w_glu': 1.045098e-02, 'b_glu': 1.480533e-02, 'w_proj_att': 2.272677e-02, 'w_proj_ssm': 1.628189e-02, 'w_out': 2.774116e-02, 'g_ffn': 7.802314e-02, 'w_up': 3.948419e-02, 'w_conv': 4.001203e-02, 'b_conv': 3.547408e-02, 'w_down': 5.495872e-02, 'g_final': 6.392431e+01}


def _to_microbatches(a, axis):
    t = _jnp.moveaxis(a, axis, 0)
    t = t.reshape((N_MICROBATCH, t.shape[0] // N_MICROBATCH) + t.shape[1:])
    return _jnp.moveaxis(t, 1, axis + 1)


def setup_inputs(seed: int = 0) -> dict:
    inp = _fwd_setup_inputs(seed)
    key = _jax.random.fold_in(_jax.random.key(seed), 7919)
    shape, _ = _output_shape()
    out = dict(inp)
    out["loss_target"] = _jax.random.normal(_jax.random.fold_in(key, 0), shape, _jnp.float32)
    for i, name in enumerate(TWIN_WEIGHTS):
        w = inp[name].astype(_jnp.float32)
        if MOMENT_SCALE is None:
            s = _jnp.sqrt(_jnp.mean(_jnp.square(w)) + 1e-30)
        else:
            s = MOMENT_SCALE[name]
        km, kv = _jax.random.split(_jax.random.fold_in(key, i + 1))
        out[name] = w
        out["m_" + name] = s * _jax.random.normal(km, w.shape, _jnp.float32)
        out["v_" + name] = (s * s) * _jax.random.uniform(kv, w.shape, _jnp.float32, 0.5, 1.5)
    if N_MICROBATCH > 1:
        for name, axis in PER_EXAMPLE_BATCH_AXIS.items():
            out[name] = _to_microbatches(out[name], axis)
    return {'x': out['x'], 'c': out['c'], 'w_ada': out['w_ada'], 'b_ada': out['b_ada'], 'g_mix': out['g_mix'], 'w_in': out['w_in'], 'b_gate': out['b_gate'], 'a_re': out['a_re'], 'a_im': out['a_im'], 'log_dt': out['log_dt'], 'b_re': out['b_re'], 'b_im': out['b_im'], 'c_re': out['c_re'], 'c_im': out['c_im'], 'd_skip': out['d_skip'], 'w_glu': out['w_glu'], 'b_glu': out['b_glu'], 'w_proj_att': out['w_proj_att'], 'w_proj_ssm': out['w_proj_ssm'], 'w_out': out['w_out'], 'g_ffn': out['g_ffn'], 'w_up': out['w_up'], 'w_conv': out['w_conv'], 'b_conv': out['b_conv'], 'w_down': out['w_down'], 'g_final': out['g_final'], 'loss_target': out['loss_target'], 'm_w_ada': out['m_w_ada'], 'm_b_ada': out['m_b_ada'], 'm_g_mix': out['m_g_mix'], 'm_w_in': out['m_w_in'], 'm_b_gate': out['m_b_gate'], 'm_a_re': out['m_a_re'], 'm_a_im': out['m_a_im'], 'm_log_dt': out['m_log_dt'], 'm_b_re': out['m_b_re'], 'm_b_im': out['m_b_im'], 'm_c_re': out['m_c_re'], 'm_c_im': out['m_c_im'], 'm_d_skip': out['m_d_skip'], 'm_w_glu': out['m_w_glu'], 'm_b_glu': out['m_b_glu'], 'm_w_proj_att': out['m_w_proj_att'], 'm_w_proj_ssm': out['m_w_proj_ssm'], 'm_w_out': out['m_w_out'], 'm_g_ffn': out['m_g_ffn'], 'm_w_up': out['m_w_up'], 'm_w_conv': out['m_w_conv'], 'm_b_conv': out['m_b_conv'], 'm_w_down': out['m_w_down'], 'm_g_final': out['m_g_final'], 'v_w_ada': out['v_w_ada'], 'v_b_ada': out['v_b_ada'], 'v_g_mix': out['v_g_mix'], 'v_w_in': out['v_w_in'], 'v_b_gate': out['v_b_gate'], 'v_a_re': out['v_a_re'], 'v_a_im': out['v_a_im'], 'v_log_dt': out['v_log_dt'], 'v_b_re': out['v_b_re'], 'v_b_im': out['v_b_im'], 'v_c_re': out['v_c_re'], 'v_c_im': out['v_c_im'], 'v_d_skip': out['v_d_skip'], 'v_w_glu': out['v_w_glu'], 'v_b_glu': out['v_b_glu'], 'v_w_proj_att': out['v_w_proj_att'], 'v_w_proj_ssm': out['v_w_proj_ssm'], 'v_w_out': out['v_w_out'], 'v_g_ffn': out['v_g_ffn'], 'v_w_up': out['v_w_up'], 'v_w_conv': out['v_w_conv'], 'v_b_conv': out['v_b_conv'], 'v_w_down': out['v_w_down'], 'v_g_final': out['v_g_final']}


def _loss(weights, diff, rest, loss_target):
    with _jax.named_scope("forward"):
        args = {**rest, TWIN_DIFF_INPUT: diff, **{k: w.astype(_WEIGHT_DTYPES[k]) for k, w in weights.items()}}
        y = _forward(args)
    with _jax.named_scope("loss_head"):
        err = _jnp.square(y.astype(_jnp.float32) - loss_target)
        return 0.5 * _jnp.sum(_jnp.mean(err, axis=-1)) if err.ndim else 0.5 * err


def _adamw(w, g, m, v):
    m = ADAM_B1 * m + (1.0 - ADAM_B1) * g
    v = ADAM_B2 * v + (1.0 - ADAM_B2) * _jnp.square(g)
    m_hat = m / (1.0 - ADAM_B1 ** ADAM_STEP)
    v_hat = v / (1.0 - ADAM_B2 ** ADAM_STEP)
    delta = -ADAM_LR * (m_hat / (_jnp.sqrt(v_hat) + ADAM_EPS) + ADAM_WD * w)
    return delta, m, v


def reference(x, c, w_ada, b_ada, g_mix, w_in, b_gate, a_re, a_im, log_dt, b_re, b_im, c_re, c_im, d_skip, w_glu, b_glu, w_proj_att, w_proj_ssm, w_out, g_ffn, w_up, w_conv, b_conv, w_down, g_final, loss_target, m_w_ada, m_b_ada, m_g_mix, m_w_in, m_b_gate, m_a_re, m_a_im, m_log_dt, m_b_re, m_b_im, m_c_re, m_c_im, m_d_skip, m_w_glu, m_b_glu, m_w_proj_att, m_w_proj_ssm, m_w_out, m_g_ffn, m_w_up, m_w_conv, m_b_conv, m_w_down, m_g_final, v_w_ada, v_b_ada, v_g_mix, v_w_in, v_b_gate, v_a_re, v_a_im, v_log_dt, v_b_re, v_b_im, v_c_re, v_c_im, v_d_skip, v_w_glu, v_b_glu, v_w_proj_att, v_w_proj_ssm, v_w_out, v_g_ffn, v_w_up, v_w_conv, v_b_conv, v_w_down, v_g_final):
    given = dict(x=x, c=c, w_ada=w_ada, b_ada=b_ada, g_mix=g_mix, w_in=w_in, b_gate=b_gate, a_re=a_re, a_im=a_im, log_dt=log_dt, b_re=b_re, b_im=b_im, c_re=c_re, c_im=c_im, d_skip=d_skip, w_glu=w_glu, b_glu=b_glu, w_proj_att=w_proj_att, w_proj_ssm=w_proj_ssm, w_out=w_out, g_ffn=g_ffn, w_up=w_up, w_conv=w_conv, b_conv=b_conv, w_down=w_down, g_final=g_final, loss_target=loss_target, m_w_ada=m_w_ada, m_b_ada=m_b_ada, m_g_mix=m_g_mix, m_w_in=m_w_in, m_b_gate=m_b_gate, m_a_re=m_a_re, m_a_im=m_a_im, m_log_dt=m_log_dt, m_b_re=m_b_re, m_b_im=m_b_im, m_c_re=m_c_re, m_c_im=m_c_im, m_d_skip=m_d_skip, m_w_glu=m_w_glu, m_b_glu=m_b_glu, m_w_proj_att=m_w_proj_att, m_w_proj_ssm=m_w_proj_ssm, m_w_out=m_w_out, m_g_ffn=m_g_ffn, m_w_up=m_w_up, m_w_conv=m_w_conv, m_b_conv=m_b_conv, m_w_down=m_w_down, m_g_final=m_g_final, v_w_ada=v_w_ada, v_b_ada=v_b_ada, v_g_mix=v_g_mix, v_w_in=v_w_in, v_b_gate=v_b_gate, v_a_re=v_a_re, v_a_im=v_a_im, v_log_dt=v_log_dt, v_b_re=v_b_re, v_b_im=v_b_im, v_c_re=v_c_re, v_c_im=v_c_im, v_d_skip=v_d_skip, v_w_glu=v_w_glu, v_b_glu=v_b_glu, v_w_proj_att=v_w_proj_att, v_w_proj_ssm=v_w_proj_ssm, v_w_out=v_w_out, v_g_ffn=v_g_ffn, v_w_up=v_w_up, v_w_conv=v_w_conv, v_b_conv=v_b_conv, v_w_down=v_w_down, v_g_final=v_g_final)
    weights = {n: given[n] for n in TWIN_WEIGHTS}
    shared = {n: given[n] for n in SHARED_INPUTS}
    per_example = {n: given[n] for n in ['x', 'c']}
    grad_fn = _jax.value_and_grad(_loss, argnums=(0, 1))

    def one_microbatch(ex, loss_target):
        ex = dict(ex)
        diff = ex.pop(TWIN_DIFF_INPUT)
        return grad_fn(weights, diff, {**shared, **ex}, loss_target)

    if N_MICROBATCH == 1:
        loss, (grad_w, grad_x) = one_microbatch(per_example, given["loss_target"])
    else:
        def body(carry, xs):
            loss_sum, grad_sum = carry
            l_k, (gw_k, gx_k) = one_microbatch(xs[0], xs[1])
            with _jax.named_scope("update"):
                return (loss_sum + l_k, _jax.tree.map(_jnp.add, grad_sum, gw_k)), gx_k

        init = (_jnp.zeros((), _jnp.float32), _jax.tree.map(_jnp.zeros_like, weights))
        (loss, grad_w), grad_x = _jax.lax.scan(body, init, (per_example, given["loss_target"]))
    with _jax.named_scope("update"):
        delta_w, new_m, new_v = {}, {}, {}
        for n in TWIN_WEIGHTS:
            delta_w[n], new_m[n], new_v[n] = _adamw(weights[n], grad_w[n], given["m_" + n], given["v_" + n])
    return (loss, grad_x, *[grad_w[n] for n in TWIN_WEIGHTS], *[delta_w[n] for n in TWIN_WEIGHTS],
            *[new_m[n] for n in TWIN_WEIGHTS], *[new_v[n] for n in TWIN_WEIGHTS])
```

```python
import functools
import math

import jax
import jax.numpy as jnp
from jax import lax
from jax.experimental import pallas as pl
from jax.experimental.pallas import tpu as pltpu

F32 = jnp.float32
BF16 = jnp.bfloat16

N_DEV = 8
D_MODEL = 1024
N_HEADS = 8
HEAD_DIM = 64
ATT_WIDTH = N_HEADS * HEAD_DIM
DILATIONS = (1, 4, 16)
WIN = 128
SSM_GROUPS = 16
SSM_GROUP_CH = 16
SSM_WIDTH = SSM_GROUPS * SSM_GROUP_CH
SSM_STATE = 64
SSM_COLS = SSM_GROUPS * SSM_STATE
D_FF = 2048
EPS = 1e-6
NEG_INF = -1e30
ADAM_LR, ADAM_B1, ADAM_B2, ADAM_EPS, ADAM_WD, ADAM_STEP = 0.001, 0.9, 0.999, 1e-08, 0.01, 10

V7X_VMEM_LIMIT = 56 * 1024 * 1024
LANES = 128


def _params(n_grid):
    return pltpu.CompilerParams(dimension_semantics=("arbitrary",) * n_grid,
                                vmem_limit_bytes=V7X_VMEM_LIMIT)


def _tile(n, pref):
    if n <= pref:
        return n
    t = (pref // LANES) * LANES
    while t > 0:
        if n % t == 0:
            return t
        t -= LANES
    return n


def _matmul(a, b, *, ta=False, tb=False, out_dtype=F32, name, tm=512, tn=1024, tk=1024):
    if ta:
        K, M = a.shape
    else:
        M, K = a.shape
    if tb:
        N, K2 = b.shape
    else:
        K2, N = b.shape
    assert K == K2, (a.shape, b.shape)
    tm, tn, tk = _tile(M, tm), _tile(N, tn), _tile(K, tk)
    nk = K // tk
    dn = (((0,) if ta else (1,), (1,) if tb else (0,)), ((), ()))

    def body(a_ref, b_ref, o_ref, acc_ref):
        k = pl.program_id(2)

        @pl.when(k == 0)
        def _():
            acc_ref[...] = jnp.zeros_like(acc_ref)

        acc_ref[...] += lax.dot_general(a_ref[...].astype(BF16), b_ref[...].astype(BF16), dn,
                                        preferred_element_type=F32)

        @pl.when(k == nk - 1)
        def _():
            o_ref[...] = acc_ref[...].astype(o_ref.dtype)

    a_spec = (pl.BlockSpec((tk, tm), lambda i, j, k: (k, i)) if ta
              else pl.BlockSpec((tm, tk), lambda i, j, k: (i, k)))
    b_spec = (pl.BlockSpec((tn, tk), lambda i, j, k: (j, k)) if tb
              else pl.BlockSpec((tk, tn), lambda i, j, k: (k, j)))
    return pl.pallas_call(
        body, name=name, grid=(M // tm, N // tn, nk),
        in_specs=[a_spec, b_spec],
        out_specs=pl.BlockSpec((tm, tn), lambda i, j, k: (i, j)),
        out_shape=jax.ShapeDtypeStruct((M, N), out_dtype),
        scratch_shapes=[pltpu.VMEM((tm, tn), F32)],
        compiler_params=_params(3),
    )(a, b)


def _rowwise(fn, rows, bvecs, consts, out_rows, out_b, out_g, *, ts, name):
    B, S = rows[0][0].shape[:2]
    nin = len(rows) + len(bvecs) + len(consts)
    nr, nb, ng = len(out_rows), len(out_b), len(out_g)

    def body(*refs):
        b = pl.program_id(0)
        s = pl.program_id(1)
        outs = fn(*[r[...] for r in refs[:nin]])
        if not isinstance(outs, (tuple, list)):
            outs = (outs,)
        orefs = refs[nin:]
        for i in range(nr):
            orefs[i][...] = outs[i].astype(orefs[i].dtype)
        for i in range(nb):
            ref = orefs[nr + i]

            @pl.when(s == 0)
            def _(ref=ref):
                ref[...] = jnp.zeros_like(ref)

            ref[...] += outs[nr + i]
        for i in range(ng):
            ref = orefs[nr + nb + i]

            @pl.when((s == 0) & (b == 0))
            def _(ref=ref):
                ref[...] = jnp.zeros_like(ref)

            ref[...] += outs[nr + nb + i]

    in_specs = ([pl.BlockSpec((None, ts, cb), lambda b, s, ci=ci: (b, s, ci)) for (_, cb, ci) in rows]
                + [pl.BlockSpec((None, 1, cb), lambda b, s, ci=ci: (b, 0, ci)) for (_, cb, ci) in bvecs]
                + [pl.BlockSpec(a.shape, lambda b, s: (0, 0)) for a in consts])
    out_shape = ([jax.ShapeDtypeStruct((B, S, c), dt) for (c, dt) in out_rows]
                 + [jax.ShapeDtypeStruct((B, 1, c), F32) for c in out_b]
                 + [jax.ShapeDtypeStruct(rc, F32) for rc in out_g])
    out_specs = ([pl.BlockSpec((None, ts, c), lambda b, s: (b, s, 0)) for (c, _) in out_rows]
                 + [pl.BlockSpec((None, 1, c), lambda b, s: (b, 0, 0)) for c in out_b]
                 + [pl.BlockSpec(rc, lambda b, s: (0, 0)) for rc in out_g])
    args = [a for (a, _, _) in rows] + [a for (a, _, _) in bvecs] + list(consts)
    return pl.pallas_call(
        body, name=name, grid=(B, S // ts), in_specs=in_specs, out_specs=out_specs,
        out_shape=out_shape, compiler_params=_params(2),
    )(*args)


def _col_sum(v):
    return jnp.sum(v, axis=0, keepdims=True)


def _rms_scale(h):
    return lax.rsqrt(jnp.mean(h * h, axis=-1, keepdims=True) + EPS)


def _sigmoid(v):
    return 1.0 / (1.0 + jnp.exp(-v))


def _to_pattern_order(dst_ref, src_ref, d, seq):
    sub = seq // d
    for r in range(d):
        dst_ref[r * sub:(r + 1) * sub, :] = src_ref[pl.ds(r, sub, stride=d), :].astype(dst_ref.dtype)


def _block_scores(q, k_cur, k_prev, slope_d, has_prev, masks):
    dcur, dprev, mcur, mprev = masks
    nt = (((1,), (1,)), ((), ()))
    scale = HEAD_DIM ** -0.5
    sc = lax.dot_general(q, k_cur, nt, preferred_element_type=F32) * scale - slope_d * dcur
    sp = lax.dot_general(q, k_prev, nt, preferred_element_type=F32) * scale - slope_d * dprev
    sc = jnp.where(mcur, sc, NEG_INF)
    sp = jnp.where(mprev & has_prev, sp, NEG_INF)
    return sc, sp


def _masks():
    row = lax.broadcasted_iota(jnp.int32, (WIN, WIN), 0)
    col = lax.broadcasted_iota(jnp.int32, (WIN, WIN), 1)
    dcur = (row - col).astype(F32)
    return dcur, dcur + float(WIN), row >= col, col >= row


def _attention_fwd(qkv, slopes):
    B, S, _ = qkv.shape
    n_blk = S // WIN
    n_pair = N_HEADS // 2

    def body(slopes_ref, q_ref, k_ref, v_ref, o_ref, lse_ref, qp, kp, vp, acc, st, acc_n, st_n):
        hp = pl.program_id(1)
        masks = _masks()
        st[...] = jnp.zeros_like(st)
        for p, d in enumerate(DILATIONS):
            nb = n_blk // d
            _to_pattern_order(qp, q_ref, d, S)
            _to_pattern_order(kp, k_ref, d, S)
            _to_pattern_order(vp, v_ref, d, S)

            def block(i, carry, p=p, d=d, nb=nb):
                base = pl.multiple_of(i * WIN, WIN)
                pbase = pl.multiple_of(jnp.maximum(i - 1, 0) * WIN, WIN)
                has_prev = (i % nb) > 0
                for h in range(2):
                    lo = h * HEAD_DIM
                    slope_d = slopes_ref[2 * hp + h] * float(d)
                    q = qp[pl.ds(base, WIN), lo:lo + HEAD_DIM]
                    sc, sp = _block_scores(q, kp[pl.ds(base, WIN), lo:lo + HEAD_DIM],
                                           kp[pl.ds(pbase, WIN), lo:lo + HEAD_DIM], slope_d, has_prev, masks)
                    m = jnp.maximum(jnp.max(sc, axis=1, keepdims=True), jnp.max(sp, axis=1, keepdims=True))
                    pc = jnp.exp(sc - m)
                    pp = jnp.exp(sp - m)
                    l = jnp.sum(pc, axis=1, keepdims=True) + jnp.sum(pp, axis=1, keepdims=True)
                    a = (jnp.dot(pc.astype(BF16), vp[pl.ds(base, WIN), lo:lo + HEAD_DIM], preferred_element_type=F32)
                         + jnp.dot(pp.astype(BF16), vp[pl.ds(pbase, WIN), lo:lo + HEAD_DIM], preferred_element_type=F32))
                    acc[p, pl.ds(base, WIN), lo:lo + HEAD_DIM] = a
                    st[p, pl.ds(base, WIN), 2 * h:2 * h + 1] = m
                    st[p, pl.ds(base, WIN), 2 * h + 1:2 * h + 2] = l
                return carry

            lax.fori_loop(0, n_blk, block, 0)
            sub = S // d
            for r in range(d):
                acc_n[p, pl.ds(r, sub, stride=d), :] = acc[p, r * sub:(r + 1) * sub, :]
                st_n[p, pl.ds(r, sub, stride=d), :] = st[p, r * sub:(r + 1) * sub, :]

        chunk = 256

        def merge(i, carry):
            base = pl.multiple_of(i * chunk, chunk)
            rows = pl.ds(base, chunk)
            for h in range(2):
                lo = h * HEAD_DIM
                ms = [st_n[p, rows, 2 * h:2 * h + 1] for p in range(3)]
                ls = [st_n[p, rows, 2 * h + 1:2 * h + 2] for p in range(3)]
                m = jnp.maximum(jnp.maximum(ms[0], ms[1]), ms[2])
                ws = [jnp.exp(mp - m) for mp in ms]
                l = ws[0] * ls[0] + ws[1] * ls[1] + ws[2] * ls[2]
                o = (ws[0] * acc_n[0, rows, lo:lo + HEAD_DIM] + ws[1] * acc_n[1, rows, lo:lo + HEAD_DIM]
                     + ws[2] * acc_n[2, rows, lo:lo + HEAD_DIM]) / l
                o_ref[rows, lo:lo + HEAD_DIM] = o.astype(o_ref.dtype)
                lse_ref[rows, h:h + 1] = m + jnp.log(l)
            return carry

        lax.fori_loop(0, S // chunk, merge, 0)

    blk = lambda off: pl.BlockSpec((None, S, LANES), lambda b, hp, off=off: (b, 0, off + hp))
    return pl.pallas_call(
        body, name="attention_fwd", grid=(B, n_pair),
        in_specs=[pl.BlockSpec(memory_space=pltpu.SMEM), blk(0), blk(n_pair), blk(2 * n_pair)],
        out_specs=[pl.BlockSpec((None, S, LANES), lambda b, hp: (b, 0, hp)),
                   pl.BlockSpec((None, None, S, 2), lambda b, hp: (b, hp, 0, 0))],
        out_shape=[jax.ShapeDtypeStruct((B, S, ATT_WIDTH), BF16),
                   jax.ShapeDtypeStruct((B, n_pair, S, 2), F32)],
        scratch_shapes=[pltpu.VMEM((S, LANES), BF16), pltpu.VMEM((S, LANES), BF16), pltpu.VMEM((S, LANES), BF16),
                        pltpu.VMEM((3, S, LANES), F32), pltpu.VMEM((3, S, LANES), F32),
                        pltpu.VMEM((3, S, LANES), F32), pltpu.VMEM((3, S, LANES), F32)],
        compiler_params=_params(2),
    )(slopes, qkv, qkv, qkv)


def _attention_bwd(qkv, o, do, lse, slopes):
    B, S, _ = qkv.shape
    n_blk = S // WIN
    n_pair = N_HEADS // 2
    scale = HEAD_DIM ** -0.5

    def body(slopes_ref, q_ref, k_ref, v_ref, o_ref, do_ref, lse_ref, dq_ref, dk_ref, dv_ref,
             qp, kp, vp, dop, aux, auxp, dqp, dkp, dvp, dqn, dkn, dvn):
        hp = pl.program_id(1)
        masks = _masks()
        prod = do_ref[...].astype(F32) * o_ref[...].astype(F32)
        aux[...] = jnp.zeros_like(aux)
        for h in range(2):
            aux[:, 2 * h:2 * h + 1] = lse_ref[:, h:h + 1]
            aux[:, 2 * h + 1:2 * h + 2] = jnp.sum(prod[:, h * HEAD_DIM:(h + 1) * HEAD_DIM], axis=1, keepdims=True)
        dqn[...] = jnp.zeros_like(dqn)
        dkn[...] = jnp.zeros_like(dkn)
        dvn[...] = jnp.zeros_like(dvn)
        for p, d in enumerate(DILATIONS):
            nb = n_blk // d
            _to_pattern_order(qp, q_ref, d, S)
            _to_pattern_order(kp, k_ref, d, S)
            _to_pattern_order(vp, v_ref, d, S)
            _to_pattern_order(dop, do_ref, d, S)
            _to_pattern_order(auxp, aux, d, S)

            def block(i, carry, d=d, nb=nb):
                base = pl.multiple_of(i * WIN, WIN)
                pbase = pl.multiple_of(jnp.maximum(i - 1, 0) * WIN, WIN)
                cur = pl.ds(base, WIN)
                prev = pl.ds(pbase, WIN)
                has_prev = (i % nb) > 0
                nt = (((1,), (1,)), ((), ()))
                tn = (((0,), (0,)), ((), ()))
                for h in range(2):
                    lo = h * HEAD_DIM
                    hs = slice(lo, lo + HEAD_DIM)
                    slope_d = slopes_ref[2 * hp + h] * float(d)
                    q = qp[cur, hs]
                    kc, kv = kp[cur, hs], kp[prev, hs]
                    vc, vv = vp[cur, hs], vp[prev, hs]
                    dov = dop[cur, hs]
                    sc, sp = _block_scores(q, kc, kv, slope_d, has_prev, masks)
                    lse_b = auxp[cur, 2 * h:2 * h + 1]
                    delta = auxp[cur, 2 * h + 1:2 * h + 2]
                    pc = jnp.exp(sc - lse_b)
                    pp = jnp.exp(sp - lse_b)
                    dpc = lax.dot_general(dov, vc, nt, preferred_element_type=F32)
                    dpp = lax.dot_general(dov, vv, nt, preferred_element_type=F32)
                    dsc = (pc * (dpc - delta) * scale).astype(BF16)
                    dsp = (pp * (dpp - delta) * scale).astype(BF16)
                    pcb, ppb = pc.astype(BF16), pp.astype(BF16)
                    dqp[cur, hs] = (jnp.dot(dsc, kc, preferred_element_type=F32)
                                    + jnp.dot(dsp, kv, preferred_element_type=F32))
                    dkp[cur, hs] = lax.dot_general(dsc, q, tn, preferred_element_type=F32)
                    dvp[cur, hs] = lax.dot_general(pcb, dov, tn, preferred_element_type=F32)
                    dkp[prev, hs] += lax.dot_general(dsp, q, tn, preferred_element_type=F32)
                    dvp[prev, hs] += lax.dot_general(ppb, dov, tn, preferred_element_type=F32)
                return carry

            lax.fori_loop(0, n_blk, block, 0)
            sub = S // d
            for r in range(d):
                rows = pl.ds(r, sub, stride=d)
                dqn[rows, :] += dqp[r * sub:(r + 1) * sub, :]
                dkn[rows, :] += dkp[r * sub:(r + 1) * sub, :]
                dvn[rows, :] += dvp[r * sub:(r + 1) * sub, :]
        dq_ref[...] = dqn[...].astype(dq_ref.dtype)
        dk_ref[...] = dkn[...].astype(dk_ref.dtype)
        dv_ref[...] = dvn[...].astype(dv_ref.dtype)

    blk = lambda off: pl.BlockSpec((None, S, LANES), lambda b, hp, off=off: (b, 0, off + hp))
    big = lambda dt: pltpu.VMEM((S, LANES), dt)
    outs = pl.pallas_call(
        body, name="attention_bwd", grid=(B, n_pair),
        in_specs=[pl.BlockSpec(memory_space=pltpu.SMEM), blk(0), blk(n_pair), blk(2 * n_pair), blk(0), blk(0),
                  pl.BlockSpec((None, None, S, 2), lambda b, hp: (b, hp, 0, 0))],
        out_specs=[blk(0), blk(0), blk(0)],
        out_shape=[jax.ShapeDtypeStruct((B, S, ATT_WIDTH), BF16)] * 3,
        scratch_shapes=[big(BF16), big(BF16), big(BF16), big(BF16), big(F32), big(F32),
                        big(F32), big(F32), big(F32), big(F32), big(F32), big(F32)],
        compiler_params=_params(2),
    )(slopes, qkv, qkv, qkv, o, do, lse)
    return outs


SCAN_COLS = 256
SCAN_ROWS = 8


def _rows_to_tile(rows):
    rid = lax.broadcasted_iota(jnp.int32, (SCAN_ROWS, rows[0].shape[1]), 0)
    tile = jnp.broadcast_to(rows[0], rid.shape)
    for k in range(1, SCAN_ROWS):
        tile = jnp.where(rid == k, rows[k], tile)
    return tile


def _scan_fwd(bu, a_row):
    B, S, _ = bu.shape
    nc = SSM_COLS // SCAN_COLS
    nt = S // SCAN_ROWS

    def body(bur_ref, bui_ref, ar_ref, ai_ref, xr_ref, xi_ref):
        ar, ai = ar_ref[...], ai_ref[...]

        def tile(i, carry):
            xr, xi = carry
            out_r, out_i = [], []
            for k in range(SCAN_ROWS):
                xr, xi = (ar * xr - ai * xi + bur_ref[i, k:k + 1, :], ar * xi + ai * xr + bui_ref[i, k:k + 1, :])
                out_r.append(xr)
                out_i.append(xi)
            xr_ref[i] = _rows_to_tile(out_r)
            xi_ref[i] = _rows_to_tile(out_i)
            return xr, xi

        zero = jnp.zeros((1, SCAN_COLS), F32)
        lax.fori_loop(0, nt, tile, (zero, zero))

    col = lambda off: pl.BlockSpec((None, nt, SCAN_ROWS, SCAN_COLS), lambda b, j, off=off: (b, 0, 0, off + j))
    par = lambda off: pl.BlockSpec((1, SCAN_COLS), lambda b, j, off=off: (0, off + j))
    bu4 = bu.reshape(B, nt, SCAN_ROWS, 2 * SSM_COLS)
    xr, xi = pl.pallas_call(
        body, name="s5_scan_fwd", grid=(B, nc),
        in_specs=[col(0), col(nc), par(0), par(nc)],
        out_specs=[col(0), col(0)],
        out_shape=[jax.ShapeDtypeStruct((B, nt, SCAN_ROWS, SSM_COLS), F32)] * 2,
        compiler_params=_params(2),
    )(bu4, bu4, a_row, a_row)
    return xr.reshape(B, S, SSM_COLS), xi.reshape(B, S, SSM_COLS)


def _scan_bwd(dxs, xr, xi, a_row):
    B, S, _ = dxs.shape
    nc = SSM_COLS // SCAN_COLS
    nt = S // SCAN_ROWS

    def body(dr_ref, di_ref, xr_ref, xi_ref, ar_ref, ai_ref, lr_ref, li_ref, gar_ref, gai_ref):
        b = pl.program_id(1)
        ar, ai = ar_ref[...], ai_ref[...]
        rid = lax.broadcasted_iota(jnp.int32, (SCAN_ROWS, SCAN_COLS), 0)

        @pl.when(b == 0)
        def _():
            gar_ref[...] = jnp.zeros_like(gar_ref)
            gai_ref[...] = jnp.zeros_like(gai_ref)

        def tile(j, carry):
            lr, li, accr, acci = carry
            i = nt - 1 - j
            out_r, out_i = [None] * SCAN_ROWS, [None] * SCAN_ROWS
            for k in reversed(range(SCAN_ROWS)):
                lr, li = (dr_ref[i, k:k + 1, :] + ar * lr + ai * li, di_ref[i, k:k + 1, :] + ar * li - ai * lr)
                out_r[k] = lr
                out_i[k] = li
            lam_r = _rows_to_tile(out_r)
            lam_i = _rows_to_tile(out_i)
            lr_ref[i] = lam_r
            li_ref[i] = lam_i
            ip = jnp.maximum(i - 1, 0)
            keep = (i > 0).astype(F32)
            xpr = jnp.where(rid == 0, xr_ref[ip, SCAN_ROWS - 1:SCAN_ROWS, :] * keep, pltpu.roll(xr_ref[i], 1, 0))
            xpi = jnp.where(rid == 0, xi_ref[ip, SCAN_ROWS - 1:SCAN_ROWS, :] * keep, pltpu.roll(xi_ref[i], 1, 0))
            accr = accr + lam_r * xpr + lam_i * xpi
            acci = acci + lam_i * xpr - lam_r * xpi
            return lr, li, accr, acci

        z1 = jnp.zeros((1, SCAN_COLS), F32)
        z8 = jnp.zeros((SCAN_ROWS, SCAN_COLS), F32)
        _, _, accr, acci = lax.fori_loop(0, nt, tile, (z1, z1, z8, z8))
        gar_ref[...] += _col_sum(accr)
        gai_ref[...] += _col_sum(acci)

    col = lambda off: pl.BlockSpec((None, nt, SCAN_ROWS, SCAN_COLS), lambda j, b, off=off: (b, 0, 0, off + j))
    par = lambda off: pl.BlockSpec((1, SCAN_COLS), lambda j, b, off=off: (0, off + j))
    t4 = lambda a: a.reshape(B, nt, SCAN_ROWS, a.shape[-1])
    lam_r, lam_i, g_ar, g_ai = pl.pallas_call(
        body, name="s5_scan_bwd", grid=(nc, B),
        in_specs=[col(0), col(nc), col(0), col(0), par(0), par(nc)],
        out_specs=[col(0), col(0), par(0), par(0)],
        out_shape=[jax.ShapeDtypeStruct((B, nt, SCAN_ROWS, SSM_COLS), F32)] * 2
        + [jax.ShapeDtypeStruct((1, SSM_COLS), F32)] * 2,
        compiler_params=_params(2),
    )(t4(dxs), t4(dxs), t4(xr), t4(xi), a_row, a_row)
    return lam_r.reshape(B, S, SSM_COLS), lam_i.reshape(B, S, SSM_COLS), g_ar, g_ai


def _s5_discretise(lr, li, log_dt):
    dt = jnp.exp(log_dt)
    mag = jnp.exp(lr * dt)
    ang = li * dt
    ab_re, ab_im = mag * jnp.cos(ang), mag * jnp.sin(ang)
    nr, ni = ab_re - 1.0, ab_im
    den = lr * lr + li * li
    f_re = (nr * lr + ni * li) / den
    f_im = (ni * lr - nr * li) / den
    return dt, ab_re, ab_im, nr, ni, den, f_re, f_im


def _s5_params(a_re, a_im, log_dt):
    def body(lr_ref, li_ref, ld_ref, abr, abi, fr, fi):
        _, ab_re, ab_im, _, _, _, f_re, f_im = _s5_discretise(lr_ref[...], li_ref[...], ld_ref[...])
        abr[...] = ab_re
        abi[...] = ab_im
        fr[...] = f_re
        fi[...] = f_im

    return pl.pallas_call(body, name="s5_params",
                          out_shape=[jax.ShapeDtypeStruct(a_re.shape, F32)] * 4)(a_re, a_im, log_dt)


def _s5_input_matrix(f_re, f_im, b_re, b_im):
    def body(fr, fi, br, bi, o_re, o_im):
        o_re[...] = fr[...] * br[...] - fi[...] * bi[...]
        o_im[...] = fr[...] * bi[...] + fi[...] * br[...]

    return pl.pallas_call(body, name="s5_input_matrix",
                          out_shape=[jax.ShapeDtypeStruct(b_re.shape, F32)] * 2)(f_re, f_im, b_re, b_im)


def _s5_input_matrix_bwd(f_re, f_im, b_re, b_im, g_re, g_im):
    def body(fr, fi, br, bi, gr, gi, dbr, dbi, dfr, dfi):
        dbr[...] = fr[...] * gr[...] + fi[...] * gi[...]
        dbi[...] = fr[...] * gi[...] - fi[...] * gr[...]
        dfr[...] = jnp.sum(br[...] * gr[...] + bi[...] * gi[...], axis=1, keepdims=True)
        dfi[...] = jnp.sum(br[...] * gi[...] - bi[...] * gr[...], axis=1, keepdims=True)

    return pl.pallas_call(
        body, name="s5_input_matrix_bwd",
        out_shape=[jax.ShapeDtypeStruct(b_re.shape, F32)] * 2 + [jax.ShapeDtypeStruct(f_re.shape, F32)] * 2,
    )(f_re, f_im, b_re, b_im, g_re, g_im)


def _s5_params_bwd(a_re, a_im, log_dt, g_ab_re, g_ab_im, d_f_re, d_f_im):
    def body(lr_ref, li_ref, ld_ref, gar, gai, dfr, dfi, o_lr, o_li, o_ld):
        lr, li = lr_ref[...], li_ref[...]
        dt, ab_re, ab_im, nr, ni, den, f_re, f_im = _s5_discretise(lr, li, ld_ref[...])
        d_fr, d_fi = dfr[...], dfi[...]
        d_nr = (d_fr * lr - d_fi * li) / den
        d_ni = (d_fr * li + d_fi * lr) / den
        common = (d_fr * f_re + d_fi * f_im) * 2.0 / den
        d_lr = (d_fr * nr + d_fi * ni) / den - common * lr
        d_li = (d_fr * ni - d_fi * nr) / den - common * li
        d_abr = gar[...] + d_nr
        d_abi = gai[...] + d_ni
        d_mag_mag = d_abr * ab_re + d_abi * ab_im
        d_ang = d_abi * ab_re - d_abr * ab_im
        o_lr[...] = d_lr + d_mag_mag * dt
        o_li[...] = d_li + d_ang * dt
        o_ld[...] = jnp.sum(d_mag_mag * lr + d_ang * li, axis=1, keepdims=True) * dt

    return pl.pallas_call(
        body, name="s5_params_bwd",
        out_shape=[jax.ShapeDtypeStruct(a_re.shape, F32)] * 2 + [jax.ShapeDtypeStruct(log_dt.shape, F32)],
    )(a_re, a_im, log_dt, g_ab_re, g_ab_im, d_f_re, d_f_im)


CONV_COLS = 256


def _shift_down(v, j, row):
    return jnp.where(row >= j, pltpu.roll(v, j, 0), 0.0)


def _shift_up(v, j, row, seq):
    return jnp.where(row < seq - j, pltpu.roll(v, seq - j, 0), 0.0)


def _conv_fwd(up, w_conv, b_conv):
    B, S, _ = up.shape
    nj = D_FF // CONV_COLS

    def body(up_ref, w_ref, b_ref, ff_ref):
        a = up_ref[:, :CONV_COLS]
        val = up_ref[:, CONV_COLS:]
        row = lax.broadcasted_iota(jnp.int32, a.shape, 0)
        w0, w1, w2 = w_ref[0:1, :], w_ref[1:2, :], w_ref[2:3, :]
        conv = b_ref[...] + w0 * a + w1 * _shift_down(a, 1, row) + w2 * _shift_down(a, 2, row)
        ff_ref[...] = (conv * _sigmoid(conv) * val).astype(ff_ref.dtype)

    return pl.pallas_call(
        body, name="conv_gate_fwd", grid=(B, nj),
        in_specs=[pl.BlockSpec((None, S, 2 * CONV_COLS), lambda b, j: (b, 0, j)),
                  pl.BlockSpec((3, CONV_COLS), lambda b, j: (0, j)),
                  pl.BlockSpec((1, CONV_COLS), lambda b, j: (0, j))],
        out_specs=pl.BlockSpec((None, S, CONV_COLS), lambda b, j: (b, 0, j)),
        out_shape=jax.ShapeDtypeStruct((B, S, D_FF), BF16),
        compiler_params=_params(2),
    )(up, w_conv, b_conv)


def _conv_bwd(up, d_ff, w_conv, b_conv):
    B, S, _ = up.shape
    nj = D_FF // CONV_COLS

    def body(up_ref, dff_ref, w_ref, b_ref, dup_ref, dw_ref, db_ref):
        b = pl.program_id(1)
        a = up_ref[:, :CONV_COLS]
        val = up_ref[:, CONV_COLS:]
        row = lax.broadcasted_iota(jnp.int32, a.shape, 0)
        w0, w1, w2 = w_ref[0:1, :], w_ref[1:2, :], w_ref[2:3, :]
        a1, a2 = _shift_down(a, 1, row), _shift_down(a, 2, row)
        conv = b_ref[...] + w0 * a + w1 * a1 + w2 * a2
        sg = _sigmoid(conv)
        dff = dff_ref[...].astype(F32)
        d_val = dff * conv * sg
        dc = dff * val * (sg * (1.0 + conv * (1.0 - sg)))
        d_a = w0 * dc + w1 * _shift_up(dc, 1, row, S) + w2 * _shift_up(dc, 2, row, S)
        dup_ref[:, :CONV_COLS] = d_a.astype(dup_ref.dtype)
        dup_ref[:, CONV_COLS:] = d_val.astype(dup_ref.dtype)

        @pl.when(b == 0)
        def _():
            dw_ref[...] = jnp.zeros_like(dw_ref)
            db_ref[...] = jnp.zeros_like(db_ref)

        dw_ref[0:1, :] += _col_sum(dc * a)
        dw_ref[1:2, :] += _col_sum(dc * a1)
        dw_ref[2:3, :] += _col_sum(dc * a2)
        db_ref[...] += _col_sum(dc)

    return pl.pallas_call(
        body, name="conv_gate_bwd", grid=(nj, B),
        in_specs=[pl.BlockSpec((None, S, 2 * CONV_COLS), lambda j, b: (b, 0, j)),
                  pl.BlockSpec((None, S, CONV_COLS), lambda j, b: (b, 0, j)),
                  pl.BlockSpec((3, CONV_COLS), lambda j, b: (0, j)),
                  pl.BlockSpec((1, CONV_COLS), lambda j, b: (0, j))],
        out_specs=[pl.BlockSpec((None, S, 2 * CONV_COLS), lambda j, b: (b, 0, j)),
                   pl.BlockSpec((3, CONV_COLS), lambda j, b: (0, j)),
                   pl.BlockSpec((1, CONV_COLS), lambda j, b: (0, j))],
        out_shape=[jax.ShapeDtypeStruct((B, S, 2 * D_FF), BF16), jax.ShapeDtypeStruct((3, D_FF), F32),
                   jax.ShapeDtypeStruct((1, D_FF), F32)],
        compiler_params=_params(2),
    )(up, d_ff, w_conv, b_conv)


def _ada_fwd(c_all, w_ada, b_ada):
    def body(c_ref, w_ref, b_ref, o_ref):
        cv = c_ref[...]
        act = (cv * _sigmoid(cv)).astype(BF16)
        o_ref[...] = jnp.dot(act, w_ref[...].astype(BF16), preferred_element_type=F32) + b_ref[...]

    return pl.pallas_call(body, name="ada_fwd",
                          out_shape=jax.ShapeDtypeStruct((c_all.shape[0], w_ada.shape[1]), F32),
                          compiler_params=pltpu.CompilerParams(vmem_limit_bytes=V7X_VMEM_LIMIT))(c_all, w_ada, b_ada)


def _ada_bwd(c_all, dmod_all, dmod_cols):
    def body(c_ref, dm_ref, dmc_ref, dw_ref, db_ref):
        cv = c_ref[...]
        act = (cv * _sigmoid(cv)).astype(BF16)
        dw_ref[...] = lax.dot_general(act, dmc_ref[...].astype(BF16), (((0,), (0,)), ((), ())),
                                      preferred_element_type=F32)
        db_ref[...] = _col_sum(dm_ref[...])

    return pl.pallas_call(
        body, name="ada_bwd",
        out_shape=[jax.ShapeDtypeStruct((c_all.shape[1], dmod_cols.shape[1]), F32),
                   jax.ShapeDtypeStruct((1, dmod_all.shape[1]), F32)],
        compiler_params=pltpu.CompilerParams(vmem_limit_bytes=V7X_VMEM_LIMIT))(c_all, dmod_all, dmod_cols)


def _adamw(w, m, v, g_parts, name):
    R, C = w.shape
    P = g_parts.shape[0]
    tr = R
    for cand in (256, 128, 64, 32, 16, 8):
        if R % cand == 0 and cand * C * 4 * (P + 7) * 2 <= V7X_VMEM_LIMIT // 2:
            tr = cand
            break
    c1 = 1.0 / (1.0 - ADAM_B1 ** ADAM_STEP)
    c2 = 1.0 / (1.0 - ADAM_B2 ** ADAM_STEP)

    def body(w_ref, m_ref, v_ref, g_ref, og, od, om, ov):
        g = g_ref[0]
        for p in range(1, P):
            g = g + g_ref[p]
        m_new = ADAM_B1 * m_ref[...] + (1.0 - ADAM_B1) * g
        v_new = ADAM_B2 * v_ref[...] + (1.0 - ADAM_B2) * (g * g)
        og[...] = g
        om[...] = m_new
        ov[...] = v_new
        od[...] = -ADAM_LR * ((m_new * c1) / (jnp.sqrt(v_new * c2) + ADAM_EPS) + ADAM_WD * w_ref[...])

    spec = pl.BlockSpec((tr, C), lambda i: (i, 0))
    return pl.pallas_call(
        body, name=name, grid=(R // tr,),
        in_specs=[spec, spec, spec, pl.BlockSpec((P, tr, C), lambda i: (0, i, 0))],
        out_specs=[spec] * 4, out_shape=[jax.ShapeDtypeStruct((R, C), F32)] * 4,
        compiler_params=_params(1),
    )(w, m, v, g_parts)


def _sum_parts(parts, loss_rows):
    P, R, C = parts.shape
    lo, hi = loss_rows

    def body(p_ref, o_ref, loss_ref):
        t = p_ref[0]
        for p in range(1, P):
            t = t + p_ref[p]
        o_ref[...] = t
        tot = jnp.sum(jnp.sum(o_ref[lo:hi, :], axis=1, keepdims=True), axis=0, keepdims=True)
        loss_ref[...] = jnp.broadcast_to(tot, loss_ref.shape)

    return pl.pallas_call(body, name="sum_small_grads",
                          out_shape=[jax.ShapeDtypeStruct((R, C), F32), jax.ShapeDtypeStruct((1, LANES), F32)],
                          compiler_params=pltpu.CompilerParams(vmem_limit_bytes=V7X_VMEM_LIMIT))(parts)


def _exchange(items, name):
    n = len(items)
    MESH = pl.DeviceIdType.MESH

    def body(*refs):
        src, dst = refs[:n], refs[n:2 * n]
        send_sems, recv_sems, local_sems = refs[2 * n:]
        x, y, c = lax.axis_index("x"), lax.axis_index("y"), lax.axis_index("c")
        me = 4 * x + 2 * y + c
        started = []
        for it, (_, per_peer) in enumerate(items):
            own = pltpu.make_async_copy(src[it].at[me] if per_peer else src[it], dst[it].at[me], local_sems.at[it])
            own.start()
            started.append(own)
        sends, recvs = [], []
        for k in range(1, N_DEV):
            px = 1 - x if k & 4 else x
            py = 1 - y if k & 2 else y
            pc = 1 - c if k & 1 else c
            peer = 4 * px + 2 * py + pc
            for it, (_, per_peer) in enumerate(items):
                s = src[it].at[peer] if per_peer else src[it]
                cp = pltpu.make_async_remote_copy(src_ref=s, dst_ref=dst[it].at[me], send_sem=send_sems.at[it, k - 1],
                                                  recv_sem=recv_sems.at[it, k - 1], device_id=(px, py, pc),
                                                  device_id_type=MESH)
                cp.start()
                sends.append(cp)
                recvs.append(pltpu.make_async_remote_copy(
                    src_ref=s, dst_ref=dst[it].at[peer], send_sem=send_sems.at[it, k - 1],
                    recv_sem=recv_sems.at[it, k - 1], device_id=(px, py, pc), device_id_type=MESH))
        for cp in recvs:
            cp.wait_recv()
        for cp in sends:
            cp.wait_send()
        for cp in started:
            cp.wait()

    any_spec = pl.BlockSpec(memory_space=pl.ANY)
    out_shape = []
    for a, per_peer in items:
        shp = a.shape if per_peer else (N_DEV,) + a.shape
        out_shape.append(jax.ShapeDtypeStruct(shp, a.dtype))
    return pl.pallas_call(
        body, name=name, in_specs=[any_spec] * n, out_specs=[any_spec] * n, out_shape=out_shape,
        scratch_shapes=[pltpu.SemaphoreType.DMA((n, N_DEV - 1)), pltpu.SemaphoreType.DMA((n, N_DEV - 1)),
                        pltpu.SemaphoreType.DMA((n,))],
    )(*[a for a, _ in items])


def _gelu_tanh(y):
    k = math.sqrt(2.0 / math.pi)
    t = jnp.tanh(k * (y + 0.044715 * y * y * y))
    return 0.5 * y * (1.0 + t), t


def _local_step(x, mod, target, W, P):
    B, S, D = x.shape
    T = B * S
    TS = 512
    flat = lambda a: a.reshape(T, a.shape[-1])
    unflat = lambda a: a.reshape(B, S, a.shape[-1])
    mod_col = lambda i: (mod, D, i)

    def f_modnorm(xv, sc, sh, g):
        return (xv * _rms_scale(xv) * g) * (1.0 + sc) + sh

    (u1,) = _rowwise(f_modnorm, [(x, D, 0)], [mod_col(1), mod_col(0)], [P["g_mix"]],
                     [(D, BF16)], [], [], ts=TS, name="modnorm_mix")
    u1f = flat(u1)
    qkv = unflat(_matmul(u1f, W["w_qkv"], name="proj_qkv"))
    us = unflat(_matmul(u1f, W["w_us"], name="proj_ssm_in"))
    gates = unflat(_matmul(u1f, W["w_gates"], name="proj_gates"))

    o_att, lse = _attention_fwd(qkv, P["slopes"])
    y_att = unflat(_matmul(flat(o_att), W["w_proj_att"], name="proj_att"))

    bu = unflat(_matmul(flat(us), P["bb_big"], name="s5_bu"))
    xr, xi = _scan_fwd(bu, P["a_row"])
    xs = jnp.concatenate([xr, xi], axis=-1)
    y_mm = unflat(_matmul(flat(xs), P["cc_big"], name="s5_readout"))

    def f_glu(ymm, usv, dsk, wg, bg):
        yv = ymm + dsk * usv
        ge, _ = _gelu_tanh(yv)
        pre = jnp.dot(ge.astype(BF16), wg, preferred_element_type=F32) + bg
        return yv, ge * _sigmoid(pre)

    y_s5, z = _rowwise(f_glu, [(y_mm, SSM_WIDTH, 0), (us, SSM_WIDTH, 0)], [], [P["d_skip"], W["w_glu"], P["b_glu"]],
                       [(SSM_WIDTH, F32), (SSM_WIDTH, BF16)], [], [], ts=TS, name="s5_glu")
    y_ssm = unflat(_matmul(flat(z), W["w_proj_ssm"], name="proj_ssm"))

    def f_merge(ga, gs, ya, ys, bga, bgs):
        return _sigmoid(ga + bga) * ya + _sigmoid(gs + bgs) * ys

    bga, bgs = P["b_gate"][:, :D], P["b_gate"][:, D:]
    (merged,) = _rowwise(f_merge, [(gates, D, 0), (gates, D, 1), (y_att, D, 0), (y_ssm, D, 0)], [], [bga, bgs],
                         [(D, BF16)], [], [], ts=TS, name="gate_merge")
    mix = unflat(_matmul(flat(merged), W["w_out"], name="proj_out"))

    def f_res_modnorm(xv, mx, gt, sc, sh, g):
        h = xv + gt * mx
        return h, (h * _rms_scale(h) * g) * (1.0 + sc) + sh

    h1, u2 = _rowwise(f_res_modnorm, [(x, D, 0), (mix, D, 0)], [mod_col(2), mod_col(4), mod_col(3)], [P["g_ffn"]],
                      [(D, F32), (D, BF16)], [], [], ts=TS, name="residual_modnorm_ffn")
    up = unflat(_matmul(flat(u2), W["w_up"], name="ffn_up"))
    ff = _conv_fwd(up, P["w_conv"], P["b_conv"])
    down = unflat(_matmul(flat(ff), W["w_down"], name="ffn_down"))

    def f_head(h1v, dn, tg, gt, g):
        h2 = h1v + gt * dn
        r = _rms_scale(h2)
        nh = h2 * r
        e = nh * g - tg
        dy = e * (1.0 / D)
        gy = dy * g
        dh = r * (gy - nh * jnp.mean(gy * nh, axis=-1, keepdims=True))
        return (dh, dh * gt, _col_sum(dh * dn), _col_sum(dy * nh), _col_sum(e * e) * (0.5 / D))

    dh2, d_down, d_gt2, d_g_final, loss_cols = _rowwise(
        f_head, [(h1, D, 0), (down, D, 0), (target, D, 0)], [mod_col(5)], [P["g_final"]],
        [(D, F32), (D, BF16)], [D], [(1, D), (1, D)], ts=TS, name="head_loss")

    d_downf = flat(d_down)
    d_ff = unflat(_matmul(d_downf, W["w_down"], tb=True, out_dtype=BF16, name="ffn_down_dx"))
    d_w_down = _matmul(flat(ff), d_downf, ta=True, name="ffn_down_dw")
    d_up, d_w_conv, d_b_conv = _conv_bwd(up, d_ff, P["w_conv"], P["b_conv"])
    d_upf = flat(d_up)
    d_u2 = unflat(_matmul(d_upf, W["w_up"], tb=True, name="ffn_up_dx"))
    d_w_up = _matmul(flat(u2), d_upf, ta=True, name="ffn_up_dw")

    def f_modnorm_bwd(du, h, dres, mx, sc, gt, g):
        r = _rms_scale(h)
        nh = h * r
        dn = du * (1.0 + sc)
        gy = dn * g
        dh = dres + r * (gy - nh * jnp.mean(gy * nh, axis=-1, keepdims=True))
        return (dh, dh * gt, _col_sum(du), _col_sum(du * nh * g), _col_sum(dh * mx), _col_sum(dn * nh))

    dh1, d_mix, d_sh2, d_sc2, d_gt1, d_g_ffn = _rowwise(
        f_modnorm_bwd, [(d_u2, D, 0), (h1, D, 0), (dh2, D, 0), (mix, D, 0)], [mod_col(4), mod_col(2)], [P["g_ffn"]],
        [(D, F32), (D, BF16)], [D, D, D], [(1, D)], ts=TS, name="modnorm_ffn_bwd")

    d_mixf = flat(d_mix)
    d_merged = unflat(_matmul(d_mixf, W["w_out"], tb=True, name="proj_out_dx"))
    d_w_out = _matmul(flat(merged), d_mixf, ta=True, name="proj_out_dw")

    def f_merge_bwd(dm, ga, gs, ya, ys, bga_, bgs_):
        sa, ss = _sigmoid(ga + bga_), _sigmoid(gs + bgs_)
        dga = dm * ya * sa * (1.0 - sa)
        dgs = dm * ys * ss * (1.0 - ss)
        return dm * sa, dm * ss, dga, dgs, _col_sum(dga), _col_sum(dgs)

    d_y_att, d_y_ssm, d_ga, d_gs, d_bga, d_bgs = _rowwise(
        f_merge_bwd, [(d_merged, D, 0), (gates, D, 0), (gates, D, 1), (y_att, D, 0), (y_ssm, D, 0)], [], [bga, bgs],
        [(D, BF16)] * 4, [], [(1, D), (1, D)], ts=TS, name="gate_merge_bwd")

    d_yaf, d_ysf = flat(d_y_att), flat(d_y_ssm)
    d_o_att = unflat(_matmul(d_yaf, W["w_proj_att"], tb=True, name="proj_att_dx"))
    d_w_proj_att = _matmul(flat(o_att), d_yaf, ta=True, name="proj_att_dw")
    d_z = unflat(_matmul(d_ysf, W["w_proj_ssm"], tb=True, name="proj_ssm_dx"))
    d_w_proj_ssm = _matmul(flat(z), d_ysf, ta=True, name="proj_ssm_dw")

    def f_glu_bwd(yv, dz, usv, dsk, wg, bg):
        ge, t = _gelu_tanh(yv)
        pre = jnp.dot(ge.astype(BF16), wg, preferred_element_type=F32) + bg
        sg = _sigmoid(pre)
        dpre = dz * ge * sg * (1.0 - sg)
        dge = dz * sg + lax.dot_general(dpre.astype(BF16), wg, (((1,), (1,)), ((), ())), preferred_element_type=F32)
        k = math.sqrt(2.0 / math.pi)
        dgelu = 0.5 * (1.0 + t) + 0.5 * yv * (1.0 - t * t) * k * (1.0 + 3.0 * 0.044715 * yv * yv)
        dy = dge * dgelu
        dwg = lax.dot_general(ge.astype(BF16), dpre.astype(BF16), (((0,), (0,)), ((), ())), preferred_element_type=F32)
        return dy, dy * dsk, dwg, _col_sum(dpre), _col_sum(dy * usv)

    d_y_s5, d_us_skip, d_w_glu, d_b_glu, d_d_skip = _rowwise(
        f_glu_bwd, [(y_s5, SSM_WIDTH, 0), (d_z, SSM_WIDTH, 0), (us, SSM_WIDTH, 0)], [],
        [P["d_skip"], W["w_glu"], P["b_glu"]],
        [(SSM_WIDTH, BF16), (SSM_WIDTH, F32)], [], [(SSM_WIDTH, SSM_WIDTH), (1, SSM_WIDTH), (1, SSM_WIDTH)],
        ts=TS, name="s5_glu_bwd")
    d_ysf2 = flat(d_y_s5)
    dxs = unflat(_matmul(d_ysf2, P["cc_big"], tb=True, name="s5_readout_dx"))
    d_cc = _matmul(flat(xs), d_ysf2, ta=True, name="s5_readout_dw")
    lam_r, lam_i, g_ab_re, g_ab_im = _scan_bwd(dxs, xr, xi, P["a_row"])
    lam = flat(jnp.concatenate([lam_r, lam_i], axis=-1).astype(BF16))
    d_us_mm = unflat(_matmul(lam, P["bb_big"], tb=True, name="s5_bu_dx"))
    d_bb = _matmul(flat(us), lam, ta=True, name="s5_bu_dw")

    d_q, d_k, d_v = _attention_bwd(qkv, o_att, d_o_att, lse, P["slopes"])

    def f_add(a, b_):
        return a + b_

    (d_us,) = _rowwise(f_add, [(d_us_mm, SSM_WIDTH, 0), (d_us_skip, SSM_WIDTH, 0)], [], [],
                       [(SSM_WIDTH, BF16)], [], [], ts=TS, name="s5_input_grad")
    d_qkvf = flat(jnp.concatenate([d_q, d_k, d_v], axis=-1))
    d_usf = flat(d_us)
    d_gatesf = flat(jnp.concatenate([d_ga, d_gs], axis=-1))
    d_u1 = (_matmul(d_qkvf, W["w_qkv"], tb=True, name="proj_qkv_dx"),
            _matmul(d_usf, W["w_us"], tb=True, name="proj_ssm_in_dx"),
            _matmul(d_gatesf, W["w_gates"], tb=True, name="proj_gates_dx"))
    d_w_in = jnp.concatenate([_matmul(u1f, d_qkvf, ta=True, name="proj_qkv_dw"),
                              _matmul(u1f, d_usf, ta=True, name="proj_ssm_in_dw"),
                              _matmul(u1f, d_gatesf, ta=True, name="proj_gates_dw")], axis=1)

    def f_modnorm_bwd_in(du0, du1, du2, h, dres, sc, g):
        du = du0 + du1 + du2
        r = _rms_scale(h)
        nh = h * r
        dn = du * (1.0 + sc)
        gy = dn * g
        dh = dres + r * (gy - nh * jnp.mean(gy * nh, axis=-1, keepdims=True))
        return (dh, _col_sum(du), _col_sum(du * nh * g), _col_sum(dn * nh))

    grad_x, d_sh1, d_sc1, d_g_mix = _rowwise(
        f_modnorm_bwd_in, [(unflat(d_u1[0]), D, 0), (unflat(d_u1[1]), D, 0), (unflat(d_u1[2]), D, 0), (x, D, 0),
                           (dh1, D, 0)], [mod_col(1)], [P["g_mix"]],
        [(D, F32)], [D, D], [(1, D)], ts=TS, name="modnorm_mix_bwd")

    d_mod = jnp.concatenate([d_sh1, d_sc1, d_gt1, d_sh2, d_sc2, d_gt2], axis=-1)
    big = dict(w_in=d_w_in, w_glu=d_w_glu, w_proj_att=d_w_proj_att, w_proj_ssm=d_w_proj_ssm, w_out=d_w_out,
               w_up_pairs=d_w_up, w_conv=d_w_conv, w_down=d_w_down)
    small = dict(g_mix=d_g_mix, b_gate=jnp.concatenate([d_bga, d_bgs], axis=1), g_ab_re=g_ab_re, g_ab_im=g_ab_im,
                 d_bb=d_bb, d_cc=d_cc, d_skip=d_d_skip, b_glu=d_b_glu, g_ffn=d_g_ffn, b_conv=d_b_conv,
                 g_final=d_g_final, loss_cols=loss_cols)
    return grad_x, d_mod, big, small


def _block_diag_in(bb):
    t = bb.reshape(SSM_GROUPS, SSM_STATE, SSM_GROUP_CH)
    eye = jnp.eye(SSM_GROUPS, dtype=bb.dtype)
    return jnp.einsum("gnc,gh->gchn", t, eye).reshape(SSM_WIDTH, SSM_COLS)


def _block_diag_out(cm):
    eye = jnp.eye(SSM_GROUPS, dtype=cm.dtype)
    return jnp.einsum("gcn,gh->gnhc", cm, eye).reshape(SSM_COLS, SSM_WIDTH)


def _diag_blocks_in(m):
    t = m.reshape(SSM_GROUPS, SSM_GROUP_CH, SSM_GROUPS, SSM_STATE)
    idx = jnp.arange(SSM_GROUPS)
    return t[idx, :, idx, :].transpose(0, 2, 1).reshape(SSM_COLS, SSM_GROUP_CH)


def _diag_blocks_out(m):
    t = m.reshape(SSM_GROUPS, SSM_STATE, SSM_GROUPS, SSM_GROUP_CH)
    idx = jnp.arange(SSM_GROUPS)
    return t[idx, :, idx, :].transpose(0, 2, 1)


def _pair_columns(w):
    lead = w.shape[:-1]
    a = w[..., :D_FF].reshape(lead + (D_FF // CONV_COLS, 1, CONV_COLS))
    val = w[..., D_FF:].reshape(lead + (D_FF // CONV_COLS, 1, CONV_COLS))
    return jnp.concatenate([a, val], axis=-2).reshape(lead + (2 * D_FF,))


def _unpair_columns(w):
    lead = w.shape[:-1]
    t = w.reshape(lead + (D_FF // CONV_COLS, 2, CONV_COLS))
    return jnp.concatenate([t[..., 0, :].reshape(lead + (D_FF,)), t[..., 1, :].reshape(lead + (D_FF,))], axis=-1)


def _cols_to_slots(g):
    R = g.shape[0]
    return g.reshape(R, N_DEV, g.shape[1] // N_DEV).transpose(1, 0, 2)


def _slots_to_cols(g):
    return g.transpose(1, 0, 2).reshape(g.shape[1], N_DEV * g.shape[2])


SMALL_ORDER = ("b_ada", "g_mix", "b_gate", "a_re", "a_im", "log_dt", "b_re", "b_im", "c_re", "c_im", "d_skip",
               "b_glu", "g_ffn", "b_conv", "g_final")


def _pack(arrs):
    pieces, offs, row = [], [], 0
    for a in arrs:
        f = a.reshape(-1).astype(F32)
        n = f.shape[0]
        rows = -(-n // LANES)
        pieces.append(jnp.pad(f, (0, rows * LANES - n)))
        offs.append((row, n))
        row += rows
    return jnp.concatenate(pieces).reshape(row, LANES), offs


def _unpack(packed, offs, shapes):
    flat = packed.reshape(-1)
    return [flat[r * LANES:r * LANES + n].reshape(s) for (r, n), s in zip(offs, shapes)]


def kernel(x, c, w_ada, b_ada, g_mix, w_in, b_gate, a_re, a_im, log_dt, b_re, b_im, c_re, c_im, d_skip, w_glu, b_glu, w_proj_att, w_proj_ssm, w_out, g_ffn, w_up, w_conv, b_conv, w_down, g_final, loss_target, m_w_ada, m_b_ada, m_g_mix, m_w_in, m_b_gate, m_a_re, m_a_im, m_log_dt, m_b_re, m_b_im, m_c_re, m_c_im, m_d_skip, m_w_glu, m_b_glu, m_w_proj_att, m_w_proj_ssm, m_w_out, m_g_ffn, m_w_up, m_w_conv, m_b_conv, m_w_down, m_g_final, v_w_ada, v_b_ada, v_g_mix, v_w_in, v_b_gate, v_a_re, v_a_im, v_log_dt, v_b_re, v_b_im, v_c_re, v_c_im, v_d_skip, v_w_glu, v_b_glu, v_w_proj_att, v_w_proj_ssm, v_w_out, v_g_ffn, v_w_up, v_w_conv, v_b_conv, v_w_down, v_g_final):
    args = dict(locals())
    B, S, D = x.shape
    me = 4 * lax.axis_index("x") + 2 * lax.axis_index("y") + lax.axis_index("c")
    bf = lambda w: w[0].astype(BF16)

    gathered = _exchange([(c, False), (bf(w_in), False), (bf(w_glu), False), (bf(w_proj_att), False),
                          (bf(w_proj_ssm), False), (bf(w_out), False), (bf(w_up), False), (w_conv[0], False),
                          (bf(w_down), False)], name="gather_weights")
    c_all = gathered[0].reshape(N_DEV * B, D)
    w_in_full = _slots_to_cols(gathered[1])
    n_qkv = 3 * ATT_WIDTH
    W = dict(w_qkv=w_in_full[:, :n_qkv], w_us=w_in_full[:, n_qkv:n_qkv + SSM_WIDTH],
             w_gates=w_in_full[:, n_qkv + SSM_WIDTH:], w_glu=gathered[2].reshape(SSM_WIDTH, SSM_WIDTH),
             w_proj_att=_slots_to_cols(gathered[3]), w_proj_ssm=_slots_to_cols(gathered[4]),
             w_out=gathered[5].reshape(D, D), w_up=_pair_columns(_slots_to_cols(gathered[6])),
             w_down=gathered[8].reshape(D_FF, D))
    w_conv_full = _slots_to_cols(gathered[7])

    n_ada = w_ada.shape[2]
    b_ada_cols = lax.dynamic_slice(b_ada, (0, me * n_ada), (1, n_ada))
    mod_part = _ada_fwd(c_all, w_ada[0], b_ada_cols)
    (mod_slots,) = _exchange([(mod_part.reshape(N_DEV, B, n_ada), True)], name="scatter_modulation")
    mod = mod_slots.transpose(1, 0, 2).reshape(B, 1, 6 * D)

    ab_re, ab_im, f_re, f_im = _s5_params(a_re[0], a_im[0], log_dt[0].reshape(SSM_GROUPS, 1))
    col = lambda a: a.reshape(SSM_COLS, 1)
    b_re2, b_im2 = b_re[0].reshape(SSM_COLS, SSM_GROUP_CH), b_im[0].reshape(SSM_COLS, SSM_GROUP_CH)
    bb_re, bb_im = _s5_input_matrix(col(f_re), col(f_im), b_re2, b_im2)
    slopes = jnp.asarray([2.0 ** (-8.0 * (h + 1) / N_HEADS) for h in range(N_HEADS)], F32)
    P = dict(g_mix=g_mix, g_ffn=g_ffn, g_final=g_final.reshape(1, D), b_gate=b_gate, d_skip=d_skip, b_glu=b_glu,
             b_conv=b_conv, w_conv=w_conv_full, slopes=slopes,
             a_row=jnp.concatenate([ab_re.reshape(1, SSM_COLS), ab_im.reshape(1, SSM_COLS)], axis=1),
             bb_big=jnp.concatenate([_block_diag_in(bb_re), _block_diag_in(bb_im)], axis=1),
             cc_big=jnp.concatenate([_block_diag_out(c_re[0]), -_block_diag_out(c_im[0])], axis=0))

    grad_x, d_mod, big, small = _local_step(x, mod, loss_target, W, P)

    small_list = [small["loss_cols"], small["g_mix"], small["b_gate"], small["g_ab_re"], small["g_ab_im"],
                  _diag_blocks_in(small["d_bb"][:, :SSM_COLS]), _diag_blocks_in(small["d_bb"][:, SSM_COLS:]),
                  _diag_blocks_out(small["d_cc"][:SSM_COLS]), -_diag_blocks_out(small["d_cc"][SSM_COLS:]),
                  small["g_ffn"], small["b_conv"], small["g_final"], small["d_skip"], small["b_glu"]]
    small_packed, small_offs = _pack(small_list)
    d_w_up_full = _unpair_columns(big["w_up_pairs"])
    sent = _exchange([(_cols_to_slots(big["w_in"]), True),
                      (big["w_glu"].reshape(N_DEV, SSM_WIDTH // N_DEV, SSM_WIDTH), True),
                      (_cols_to_slots(big["w_proj_att"]), True), (_cols_to_slots(big["w_proj_ssm"]), True),
                      (big["w_out"].reshape(N_DEV, D // N_DEV, D), True), (_cols_to_slots(d_w_up_full), True),
                      (_cols_to_slots(big["w_conv"]), True),
                      (big["w_down"].reshape(N_DEV, D_FF // N_DEV, D), True),
                      (small_packed, False), (d_mod.reshape(B, 6 * D), False)], name="exchange_gradients")

    out = {}

    def update(name, parts):
        w2 = args[name][0]
        g, dl, mn, vn = _adamw(w2, args["m_" + name][0], args["v_" + name][0], parts, name="adamw_" + name)
        for key, val in (("grad_", g), ("delta_", dl), ("new_m_", mn), ("new_v_", vn)):
            out[key + name] = val[None]

    for name, parts in (("w_in", sent[0]), ("w_glu", sent[1]), ("w_proj_att", sent[2]), ("w_proj_ssm", sent[3]),
                        ("w_out", sent[4]), ("w_up", sent[5]), ("w_conv", sent[6]), ("w_down", sent[7])):
        update(name, parts)

    dmod_all = sent[9].reshape(N_DEV * B, 6 * D)
    dmod_cols = lax.dynamic_slice(dmod_all, (0, me * n_ada), (N_DEV * B, n_ada))
    d_w_ada, d_b_ada = _ada_bwd(c_all, dmod_all, dmod_cols)
    update("w_ada", d_w_ada[None])

    loss_row, loss_n = small_offs[0]
    small_sum, loss_vec = _sum_parts(sent[8], (loss_row, loss_row + loss_n // LANES))
    shapes = [(1, D), (1, D), (1, 2 * D), (SSM_GROUPS, SSM_STATE), (SSM_GROUPS, SSM_STATE), (SSM_COLS, SSM_GROUP_CH),
              (SSM_COLS, SSM_GROUP_CH), (1, SSM_GROUPS, SSM_GROUP_CH, SSM_STATE),
              (1, SSM_GROUPS, SSM_GROUP_CH, SSM_STATE), (1, D), (1, D_FF), (D,), (1, SSM_WIDTH), (1, SSM_WIDTH)]
    (_, s_g_mix, s_b_gate, s_ab_re, s_ab_im, s_bb_re, s_bb_im, s_c_re, s_c_im, s_g_ffn, s_b_conv, s_g_final,
     s_d_skip, s_b_glu) = _unpack(small_sum, small_offs, shapes)
    d_b_re2, d_b_im2, d_f_re, d_f_im = _s5_input_matrix_bwd(col(f_re), col(f_im), b_re2, b_im2, s_bb_re, s_bb_im)
    d_a_re, d_a_im, d_log_dt = _s5_params_bwd(a_re[0], a_im[0], log_dt[0].reshape(SSM_GROUPS, 1), s_ab_re, s_ab_im,
                                              d_f_re.reshape(SSM_GROUPS, SSM_STATE),
                                              d_f_im.reshape(SSM_GROUPS, SSM_STATE))
    grads_small = dict(b_ada=d_b_ada, g_mix=s_g_mix, b_gate=s_b_gate, a_re=d_a_re[None], a_im=d_a_im[None],
                       log_dt=d_log_dt.reshape(1, SSM_GROUPS), b_re=d_b_re2.reshape(b_re.shape),
                       b_im=d_b_im2.reshape(b_im.shape), c_re=s_c_re, c_im=s_c_im, d_skip=s_d_skip, b_glu=s_b_glu,
                       g_ffn=s_g_ffn, b_conv=s_b_conv, g_final=s_g_final)
    w_pack, offs = _pack([args[n] for n in SMALL_ORDER])
    m_pack, _ = _pack([args["m_" + n] for n in SMALL_ORDER])
    v_pack, _ = _pack([args["v_" + n] for n in SMALL_ORDER])
    g_pack, _ = _pack([grads_small[n] for n in SMALL_ORDER])
    res = _adamw(w_pack, m_pack, v_pack, g_pack[None], name="adamw_small")
    shapes_small = [args[n].shape for n in SMALL_ORDER]
    for key, packed in zip(("grad_", "delta_", "new_m_", "new_v_"), res):
        for n, val in zip(SMALL_ORDER, _unpack(packed, offs, shapes_small)):
            out[key + n] = val

    order = ["w_ada", "b_ada", "g_mix", "w_in", "b_gate", "a_re", "a_im", "log_dt", "b_re", "b_im", "c_re", "c_im",
             "d_skip", "w_glu", "b_glu", "w_proj_att", "w_proj_ssm", "w_out", "g_ffn", "w_up", "w_conv", "b_conv",
             "w_down", "g_final"]
    loss = loss_vec[0, 0]
    return (loss, grad_x, *[out[k + n] for k in ("grad_", "delta_", "new_m_", "new_v_") for n in order])
```

```python
import math

import jax
import jax.numpy as jnp
from jax import lax
from jax.experimental import pallas as pl
from jax.experimental.pallas import tpu as pltpu

F32 = jnp.float32
BF16 = jnp.bfloat16

N_DEV = 8
D_MODEL = 1024
N_HEADS = 8
HEAD_DIM = 64
ATT_WIDTH = N_HEADS * HEAD_DIM
DILATIONS = (1, 4, 16)
WIN = 128
SSM_GROUPS = 16
SSM_GROUP_CH = 16
SSM_WIDTH = SSM_GROUPS * SSM_GROUP_CH
SSM_STATE = 64
SSM_COLS = SSM_GROUPS * SSM_STATE
D_FF = 2048
EPS = 1e-6
NEG_INF = -1e30
ADAM_LR, ADAM_B1, ADAM_B2, ADAM_EPS, ADAM_WD, ADAM_STEP = 0.001, 0.9, 0.999, 1e-08, 0.01, 10

V7X_VMEM_LIMIT = 56 * 1024 * 1024
LANES = 128


def _params(n_grid):
    return pltpu.CompilerParams(dimension_semantics=("arbitrary",) * n_grid,
                                vmem_limit_bytes=V7X_VMEM_LIMIT)


def _tile(n, pref):
    if n <= pref:
        return n
    t = (pref // LANES) * LANES
    while t > 0:
        if n % t == 0:
            return t
        t -= LANES
    return n


def _matmul(a, b, *, ta=False, tb=False, out_dtype=F32, name, tm=512, tn=1024, tk=1024):
    if ta:
        K, M = a.shape
    else:
        M, K = a.shape
    if tb:
        N, K2 = b.shape
    else:
        K2, N = b.shape
    assert K == K2, (a.shape, b.shape)
    tm, tn, tk = _tile(M, tm), _tile(N, tn), _tile(K, tk)
    nk = K // tk
    dn = (((0,) if ta else (1,), (1,) if tb else (0,)), ((), ()))

    def body(a_ref, b_ref, o_ref, acc_ref):
        k = pl.program_id(2)

        @pl.when(k == 0)
        def _():
            acc_ref[...] = jnp.zeros_like(acc_ref)

        acc_ref[...] += lax.dot_general(a_ref[...].astype(BF16), b_ref[...].astype(BF16), dn,
                                        preferred_element_type=F32)

        @pl.when(k == nk - 1)
        def _():
            o_ref[...] = acc_ref[...].astype(o_ref.dtype)

    a_spec = (pl.BlockSpec((tk, tm), lambda i, j, k: (k, i)) if ta
              else pl.BlockSpec((tm, tk), lambda i, j, k: (i, k)))
    b_spec = (pl.BlockSpec((tn, tk), lambda i, j, k: (j, k)) if tb
              else pl.BlockSpec((tk, tn), lambda i, j, k: (k, j)))
    return pl.pallas_call(
        body, name=name, grid=(M // tm, N // tn, nk),
        in_specs=[a_spec, b_spec],
        out_specs=pl.BlockSpec((tm, tn), lambda i, j, k: (i, j)),
        out_shape=jax.ShapeDtypeStruct((M, N), out_dtype),
        scratch_shapes=[pltpu.VMEM((tm, tn), F32)],
        compiler_params=_params(3),
    )(a, b)


def _rowwise(fn, rows, bvecs, consts, out_rows, out_b, out_g, *, ts, name):
    B, S = rows[0][0].shape[:2]
    nin = len(rows) + len(bvecs) + len(consts)
    nr, nb, ng = len(out_rows), len(out_b), len(out_g)

    def body(*refs):
        b = pl.program_id(0)
        s = pl.program_id(1)
        outs = fn(*[r[...] for r in refs[:nin]])
        if not isinstance(outs, (tuple, list)):
            outs = (outs,)
        orefs = refs[nin:]
        for i in range(nr):
            orefs[i][...] = outs[i].astype(orefs[i].dtype)
        for i in range(nb):
            ref = orefs[nr + i]

            @pl.when(s == 0)
            def _(ref=ref):
                ref[...] = jnp.zeros_like(ref)

            ref[...] += outs[nr + i]
        for i in range(ng):
            ref = orefs[nr + nb + i]

            @pl.when((s == 0) & (b == 0))
            def _(ref=ref):
                ref[...] = jnp.zeros_like(ref)

            ref[...] += outs[nr + nb + i]

    in_specs = ([pl.BlockSpec((None, ts, cb), lambda b, s, ci=ci: (b, s, ci)) for (_, cb, ci) in rows]
                + [pl.BlockSpec((None, 1, cb), lambda b, s, ci=ci: (b, 0, ci)) for (_, cb, ci) in bvecs]
                + [pl.BlockSpec(a.shape, lambda b, s: (0, 0)) for a in consts])
    out_shape = ([jax.ShapeDtypeStruct((B, S, c), dt) for (c, dt) in out_rows]
                 + [jax.ShapeDtypeStruct((B, 1, c), F32) for c in out_b]
                 + [jax.ShapeDtypeStruct(rc, F32) for rc in out_g])
    out_specs = ([pl.BlockSpec((None, ts, c), lambda b, s: (b, s, 0)) for (c, _) in out_rows]
                 + [pl.BlockSpec((None, 1, c), lambda b, s: (b, 0, 0)) for c in out_b]
                 + [pl.BlockSpec(rc, lambda b, s: (0, 0)) for rc in out_g])
    args = [a for (a, _, _) in rows] + [a for (a, _, _) in bvecs] + list(consts)
    return pl.pallas_call(
        body, name=name, grid=(B, S // ts), in_specs=in_specs, out_specs=out_specs,
        out_shape=out_shape, compiler_params=_params(2),
    )(*args)


def _col_sum(v):
    return jnp.sum(v, axis=0, keepdims=True)


def _rms_scale(h):
    return lax.rsqrt(jnp.mean(h * h, axis=-1, keepdims=True) + EPS)


def _sigmoid(v):
    return 1.0 / (1.0 + jnp.exp(-v))


ATT_SCALE = HEAD_DIM ** -0.5
COPY_ROWS = 256
_NT = (((1,), (1,)), ((), ()))
_TN = (((0,), (0,)), ((), ()))


def _row_chunks(d, seq):
    sub = seq // d
    out = []
    for r in range(d):
        for c0 in range(0, sub, COPY_ROWS):
            n = min(COPY_ROWS, sub - c0)
            out.append((pl.ds(r + c0 * d, n, stride=d), r * sub + c0, n))
    return out


def _fill_bias(bias_ref, slopes_ref, hp, d, key_major):
    shape = (2 * WIN, WIN) if key_major else (WIN, 2 * WIN)
    qi = lax.broadcasted_iota(jnp.int32, shape, 1 if key_major else 0)
    kj = lax.broadcasted_iota(jnp.int32, shape, 0 if key_major else 1)
    dist = WIN + qi - kj
    valid = (dist >= 0) & (dist <= WIN)
    distf = dist.astype(F32)
    for h in range(2):
        slope_d = slopes_ref[2 * hp + h] * float(d)
        bias_ref[h] = jnp.where(valid, -(slope_d * distf), NEG_INF)


def _for_blocks(d, n_blk, block):
    nb = n_blk // d
    if nb == 1:
        def only(i, c):
            block(i, False)
            return c
        lax.fori_loop(0, n_blk, only, 0)
        return

    def sub_sequence(r, c):
        i0 = r * nb
        block(i0, False)

        def later(n, c2):
            block(i0 + n, True)
            return c2

        lax.fori_loop(1, nb, later, 0)
        return c

    lax.fori_loop(0, d, sub_sequence, 0)


def _zero_once(refs):
    @pl.when((pl.program_id(0) == 0) & (pl.program_id(1) == 0))
    def _():
        for r in refs:
            r[...] = jnp.zeros_like(r)


def _qkv_spec(seq, j):
    return pl.BlockSpec((None, seq, LANES), lambda b, hp: (b, 0, 3 * hp + j))


def _attention_fwd(qkv, slopes):
    B, S, _ = qkv.shape
    n_blk = S // WIN
    n_pair = N_HEADS // 2

    def body(slopes_ref, q_ref, k_ref, v_ref, o_ref, lse_ref, qp, kp, vp, bias, acc, st, acc_n, st_n):
        hp = pl.program_id(1)
        st[...] = jnp.zeros_like(st)
        st_n[...] = jnp.zeros_like(st_n)
        _zero_once((qp, kp, vp))
        for p, d in enumerate(DILATIONS):
            chunks = _row_chunks(d, S)
            for src, dst, n in chunks:
                qr, kr, vr = q_ref[src, :], k_ref[src, :], v_ref[src, :]
                for h in range(2):
                    hs = slice(h * HEAD_DIM, (h + 1) * HEAD_DIM)
                    qp[h, dst:dst + n, :HEAD_DIM] = (qr[:, hs] * ATT_SCALE).astype(BF16)
                    kp[h, WIN + dst:WIN + dst + n, :HEAD_DIM] = kr[:, hs].astype(BF16)
                    vp[h, WIN + dst:WIN + dst + n, :HEAD_DIM] = vr[:, hs].astype(BF16)
            _fill_bias(bias, slopes_ref, hp, d, key_major=False)
            acc_t, st_t = (acc_n, st_n) if d == 1 else (acc, st)

            def block(i, with_prev, p=p, acc_t=acc_t, st_t=st_t):
                cur = pl.ds(pl.multiple_of(i * WIN, WIN), WIN)
                if with_prev:
                    keys = pl.ds(pl.multiple_of(i * WIN, WIN), 2 * WIN)
                else:
                    keys = pl.ds(pl.multiple_of((i + 1) * WIN, WIN), WIN)
                for h in range(2):
                    bias_h = bias[h] if with_prev else bias[h, :, WIN:]
                    s = lax.dot_general(qp[h, cur, :], kp[h, keys, :], _NT, preferred_element_type=F32) + bias_h
                    m = jnp.max(jnp.maximum(s[:, :WIN], s[:, WIN:]) if with_prev else s, axis=1, keepdims=True)
                    e = jnp.exp(s - m)
                    l = jnp.sum(e[:, :WIN] + e[:, WIN:] if with_prev else e, axis=1, keepdims=True)
                    acc_t[p, h, cur, :] = jnp.dot(e.astype(BF16), vp[h, keys, :], preferred_element_type=F32)
                    st_t[p, cur, 2 * h:2 * h + 1] = m
                    st_t[p, cur, 2 * h + 1:2 * h + 2] = l

            _for_blocks(d, n_blk, block)
            if d > 1:
                for src, dst, n in chunks:
                    st_n[p, src, :] = st[p, dst:dst + n, :]
                    for h in range(2):
                        acc_n[p, h, src, :] = acc[p, h, dst:dst + n, :]

        chunk = 256

        def merge(i, carry):
            rows = pl.ds(pl.multiple_of(i * chunk, chunk), chunk)
            for h in range(2):
                lo = h * HEAD_DIM
                ms = [st_n[p, rows, 2 * h:2 * h + 1] for p in range(3)]
                ls = [st_n[p, rows, 2 * h + 1:2 * h + 2] for p in range(3)]
                m = jnp.maximum(jnp.maximum(ms[0], ms[1]), ms[2])
                ws = [jnp.exp(mp - m) for mp in ms]
                l = ws[0] * ls[0] + ws[1] * ls[1] + ws[2] * ls[2]
                o = (ws[0] * acc_n[0, h, rows, :HEAD_DIM] + ws[1] * acc_n[1, h, rows, :HEAD_DIM]
                     + ws[2] * acc_n[2, h, rows, :HEAD_DIM]) / l
                o_ref[rows, lo:lo + HEAD_DIM] = o.astype(o_ref.dtype)
                lse_ref[rows, h:h + 1] = m + jnp.log(l)
            return carry

        lax.fori_loop(0, S // chunk, merge, 0)

    return pl.pallas_call(
        body, name="attention_fwd", grid=(B, n_pair),
        in_specs=[pl.BlockSpec(memory_space=pltpu.SMEM), _qkv_spec(S, 0), _qkv_spec(S, 1), _qkv_spec(S, 2)],
        out_specs=[pl.BlockSpec((None, S, LANES), lambda b, hp: (b, 0, hp)),
                   pl.BlockSpec((None, None, S, 2), lambda b, hp: (b, hp, 0, 0))],
        out_shape=[jax.ShapeDtypeStruct((B, S, ATT_WIDTH), BF16),
                   jax.ShapeDtypeStruct((B, n_pair, S, 2), F32)],
        scratch_shapes=[pltpu.VMEM((2, S, LANES), BF16), pltpu.VMEM((2, S + WIN, LANES), BF16),
                        pltpu.VMEM((2, S + WIN, LANES), BF16), pltpu.VMEM((2, WIN, 2 * WIN), F32),
                        pltpu.VMEM((3, 2, S, LANES), F32), pltpu.VMEM((3, S, LANES), F32),
                        pltpu.VMEM((3, 2, S, LANES), F32), pltpu.VMEM((3, S, LANES), F32)],
        compiler_params=_params(2),
    )(slopes, qkv, qkv, qkv)


def _attention_bwd(qkv, o, do, lse, slopes):
    B, S, _ = qkv.shape
    n_blk = S // WIN
    n_pair = N_HEADS // 2

    def body(slopes_ref, q_ref, k_ref, v_ref, o_ref, do_ref, lse_ref, dx_ref,
             qp, kp, vp, dop, aux, auxp, aux_t, bias_t, dqp, dkp, dvp, dq_n, dk_n, dv_n):
        hp = pl.program_id(1)
        aux[...] = jnp.zeros_like(aux)
        for c0 in range(0, S, COPY_ROWS):
            rows = slice(c0, c0 + COPY_ROWS)
            prod = do_ref[rows, :] * o_ref[rows, :].astype(F32)
            for h in range(2):
                aux[rows, 2 * h:2 * h + 1] = lse_ref[rows, h:h + 1]
                aux[rows, 2 * h + 1:2 * h + 2] = jnp.sum(prod[:, h * HEAD_DIM:(h + 1) * HEAD_DIM], axis=1,
                                                         keepdims=True)
        dq_n[...] = jnp.zeros_like(dq_n)
        dk_n[...] = jnp.zeros_like(dk_n)
        dv_n[...] = jnp.zeros_like(dv_n)
        _zero_once((qp, kp, vp, dop))
        for p, d in enumerate(DILATIONS):
            chunks = _row_chunks(d, S)
            for src, dst, n in chunks:
                qr, kr, vr, dor = q_ref[src, :], k_ref[src, :], v_ref[src, :], do_ref[src, :]
                auxp[dst:dst + n, :] = aux[src, :]
                for h in range(2):
                    hs = slice(h * HEAD_DIM, (h + 1) * HEAD_DIM)
                    qp[h, dst:dst + n, :HEAD_DIM] = (qr[:, hs] * ATT_SCALE).astype(BF16)
                    kp[h, WIN + dst:WIN + dst + n, :HEAD_DIM] = kr[:, hs].astype(BF16)
                    vp[h, WIN + dst:WIN + dst + n, :HEAD_DIM] = vr[:, hs].astype(BF16)
                    dop[h, dst:dst + n, :HEAD_DIM] = dor[:, hs].astype(BF16)
            for i in range(n_blk):
                aux_t[i] = auxp[i * WIN:(i + 1) * WIN, :].T[0:8, :]
            _fill_bias(bias_t, slopes_ref, hp, d, key_major=True)
            dkp[...] = jnp.zeros_like(dkp)
            dvp[...] = jnp.zeros_like(dvp)

            def block(i, with_prev):
                cur = pl.ds(pl.multiple_of(i * WIN, WIN), WIN)
                if with_prev:
                    keys = pl.ds(pl.multiple_of(i * WIN, WIN), 2 * WIN)
                else:
                    keys = pl.ds(pl.multiple_of((i + 1) * WIN, WIN), WIN)
                for h in range(2):
                    bias_h = bias_t[h] if with_prev else bias_t[h, WIN:, :]
                    q, dov, kc, vc = qp[h, cur, :], dop[h, cur, :], kp[h, keys, :], vp[h, keys, :]
                    lse_row = aux_t[i, 2 * h:2 * h + 1, :]
                    delta_row = aux_t[i, 2 * h + 1:2 * h + 2, :]
                    s_t = lax.dot_general(kc, q, _NT, preferred_element_type=F32) + bias_h
                    p_t = jnp.exp(s_t - lse_row)
                    dp_t = lax.dot_general(vc, dov, _NT, preferred_element_type=F32)
                    ds_t = (p_t * (dp_t - delta_row)).astype(BF16)
                    dvp[h, keys, :] += jnp.dot(p_t.astype(BF16), dov, preferred_element_type=F32)
                    dkp[h, keys, :] += jnp.dot(ds_t, q, preferred_element_type=F32)
                    dqp[h, cur, :] = lax.dot_general(ds_t, kc, _TN, preferred_element_type=F32)

            _for_blocks(d, n_blk, block)
            for src, dst, n in chunks:
                for h in range(2):
                    dq_n[h, src, :] += dqp[h, dst:dst + n, :]
                    dk_n[h, src, :] += dkp[h, WIN + dst:WIN + dst + n, :]
                    dv_n[h, src, :] += dvp[h, WIN + dst:WIN + dst + n, :]
        for c0 in range(0, S, COPY_ROWS):
            rows = slice(c0, c0 + COPY_ROWS)
            for h in range(2):
                lo = h * HEAD_DIM
                dx_ref[rows, lo:lo + HEAD_DIM] = (dq_n[h, rows, :HEAD_DIM] * ATT_SCALE).astype(dx_ref.dtype)
                dx_ref[rows, LANES + lo:LANES + lo + HEAD_DIM] = dk_n[h, rows, :HEAD_DIM].astype(dx_ref.dtype)
                dx_ref[rows, 2 * LANES + lo:2 * LANES + lo + HEAD_DIM] = dv_n[h, rows, :HEAD_DIM].astype(dx_ref.dtype)

    pair = lambda width: pl.BlockSpec((None, S, width), lambda b, hp: (b, 0, hp))
    head = lambda rows, dt: pltpu.VMEM((2, rows, LANES), dt)
    return pl.pallas_call(
        body, name="attention_bwd", grid=(B, n_pair),
        in_specs=[pl.BlockSpec(memory_space=pltpu.SMEM), _qkv_spec(S, 0), _qkv_spec(S, 1), _qkv_spec(S, 2),
                  pair(LANES), pair(LANES),
                  pl.BlockSpec((None, None, S, 2), lambda b, hp: (b, hp, 0, 0))],
        out_specs=pair(3 * LANES),
        out_shape=jax.ShapeDtypeStruct((B, S, 3 * ATT_WIDTH), BF16),
        scratch_shapes=[head(S, BF16), head(S + WIN, BF16), head(S + WIN, BF16), head(S, BF16),
                        pltpu.VMEM((S, LANES), F32), pltpu.VMEM((S, LANES), F32), pltpu.VMEM((n_blk, 8, WIN), F32),
                        pltpu.VMEM((2, 2 * WIN, WIN), F32),
                        head(S, F32), head(S + WIN, F32), head(S + WIN, F32), head(S, F32), head(S, F32), head(S, F32)],
        compiler_params=_params(2),
    )(slopes, qkv, qkv, qkv, o, do, lse)


SCAN_COLS = 256
SCAN_ROWS = 8


def _rows_to_tile(rows):
    rid = lax.broadcasted_iota(jnp.int32, (SCAN_ROWS, rows[0].shape[1]), 0)
    tile = jnp.broadcast_to(rows[0], rid.shape)
    for k in range(1, SCAN_ROWS):
        tile = jnp.where(rid == k, rows[k], tile)
    return tile


def _scan_fwd(bu, a_row):
    B, S, _ = bu.shape
    nc = SSM_COLS // SCAN_COLS
    nt = S // SCAN_ROWS
    RE, IM = slice(0, SCAN_COLS), slice(SCAN_COLS, 2 * SCAN_COLS)

    def body(bu_ref, a_ref, xs_ref):
        ar, ai = a_ref[:, RE], a_ref[:, IM]

        def tile(i, carry):
            xr, xi = carry
            out_r, out_i = [], []
            for k in range(SCAN_ROWS):
                xr, xi = (ar * xr - ai * xi + bu_ref[i, k:k + 1, RE], ar * xi + ai * xr + bu_ref[i, k:k + 1, IM])
                out_r.append(xr)
                out_i.append(xi)
            xs_ref[i, :, RE] = _rows_to_tile(out_r)
            xs_ref[i, :, IM] = _rows_to_tile(out_i)
            return xr, xi

        zero = jnp.zeros((1, SCAN_COLS), F32)
        lax.fori_loop(0, nt, tile, (zero, zero))

    col = pl.BlockSpec((None, nt, SCAN_ROWS, 2 * SCAN_COLS), lambda b, j: (b, 0, 0, j))
    xs = pl.pallas_call(
        body, name="s5_scan_fwd", grid=(B, nc),
        in_specs=[col, pl.BlockSpec((1, 2 * SCAN_COLS), lambda b, j: (0, j))],
        out_specs=col,
        out_shape=jax.ShapeDtypeStruct((B, nt, SCAN_ROWS, 2 * SSM_COLS), F32),
        compiler_params=_params(2),
    )(bu.reshape(B, nt, SCAN_ROWS, 2 * SSM_COLS), a_row)
    return xs.reshape(B, S, 2 * SSM_COLS)


def _scan_bwd(dxs, xs, a_row):
    B, S, _ = dxs.shape
    nc = SSM_COLS // SCAN_COLS
    nt = S // SCAN_ROWS
    RE, IM = slice(0, SCAN_COLS), slice(SCAN_COLS, 2 * SCAN_COLS)

    def body(d_ref, x_ref, a_ref, lam_ref, ga_ref):
        b = pl.program_id(1)
        ar, ai = a_ref[:, RE], a_ref[:, IM]
        rid = lax.broadcasted_iota(jnp.int32, (SCAN_ROWS, SCAN_COLS), 0)

        @pl.when(b == 0)
        def _():
            ga_ref[...] = jnp.zeros_like(ga_ref)

        def tile(j, carry):
            lr, li, accr, acci = carry
            i = nt - 1 - j
            out_r, out_i = [None] * SCAN_ROWS, [None] * SCAN_ROWS
            for k in reversed(range(SCAN_ROWS)):
                lr, li = (d_ref[i, k:k + 1, RE] + ar * lr + ai * li, d_ref[i, k:k + 1, IM] + ar * li - ai * lr)
                out_r[k] = lr
                out_i[k] = li
            lam_r = _rows_to_tile(out_r)
            lam_i = _rows_to_tile(out_i)
            lam_ref[i, :, RE] = lam_r
            lam_ref[i, :, IM] = lam_i
            ip = jnp.maximum(i - 1, 0)
            keep = (i > 0).astype(F32)
            last = slice(SCAN_ROWS - 1, SCAN_ROWS)
            xpr = jnp.where(rid == 0, x_ref[ip, last, RE] * keep, pltpu.roll(x_ref[i, :, RE], 1, 0))
            xpi = jnp.where(rid == 0, x_ref[ip, last, IM] * keep, pltpu.roll(x_ref[i, :, IM], 1, 0))
            accr = accr + lam_r * xpr + lam_i * xpi
            acci = acci + lam_i * xpr - lam_r * xpi
            return lr, li, accr, acci

        z1 = jnp.zeros((1, SCAN_COLS), F32)
        z8 = jnp.zeros((SCAN_ROWS, SCAN_COLS), F32)
        _, _, accr, acci = lax.fori_loop(0, nt, tile, (z1, z1, z8, z8))
        ga_ref[:, RE] += _col_sum(accr)
        ga_ref[:, IM] += _col_sum(acci)

    col = pl.BlockSpec((None, nt, SCAN_ROWS, 2 * SCAN_COLS), lambda j, b: (b, 0, 0, j))
    par = pl.BlockSpec((1, 2 * SCAN_COLS), lambda j, b: (0, j))
    t4 = lambda a: a.reshape(B, nt, SCAN_ROWS, 2 * SSM_COLS)
    lam, g_a = pl.pallas_call(
        body, name="s5_scan_bwd", grid=(nc, B),
        in_specs=[col, col, par], out_specs=[col, par],
        out_shape=[jax.ShapeDtypeStruct((B, nt, SCAN_ROWS, 2 * SSM_COLS), F32),
                   jax.ShapeDtypeStruct((1, 2 * SSM_COLS), F32)],
        compiler_params=_params(2),
    )(t4(dxs), t4(xs), a_row)
    return lam.reshape(B, S, 2 * SSM_COLS), g_a


def _s5_discretise(lr, li, log_dt):
    dt = jnp.exp(log_dt)
    mag = jnp.exp(lr * dt)
    ang = li * dt
    ab_re, ab_im = mag * jnp.cos(ang), mag * jnp.sin(ang)
    nr, ni = ab_re - 1.0, ab_im
    den = lr * lr + li * li
    f_re = (nr * lr + ni * li) / den
    f_im = (ni * lr - nr * li) / den
    return dt, ab_re, ab_im, nr, ni, den, f_re, f_im


def _s5_params(a_re, a_im, log_dt):
    def body(lr_ref, li_ref, ld_ref, abr, abi, fr, fi):
        _, ab_re, ab_im, _, _, _, f_re, f_im = _s5_discretise(lr_ref[...], li_ref[...], ld_ref[...])
        abr[...] = ab_re
        abi[...] = ab_im
        fr[...] = f_re
        fi[...] = f_im

    return pl.pallas_call(body, name="s5_params",
                          out_shape=[jax.ShapeDtypeStruct(a_re.shape, F32)] * 4)(a_re, a_im, log_dt)


def _s5_input_matrix(f_re, f_im, b_re, b_im):
    def body(fr, fi, br, bi, o_re, o_im):
        o_re[...] = fr[...] * br[...] - fi[...] * bi[...]
        o_im[...] = fr[...] * bi[...] + fi[...] * br[...]

    return pl.pallas_call(body, name="s5_input_matrix",
                          out_shape=[jax.ShapeDtypeStruct(b_re.shape, F32)] * 2)(f_re, f_im, b_re, b_im)


def _s5_input_matrix_bwd(f_re, f_im, b_re, b_im, g_re, g_im):
    def body(fr, fi, br, bi, gr, gi, dbr, dbi, dfr, dfi):
        dbr[...] = fr[...] * gr[...] + fi[...] * gi[...]
        dbi[...] = fr[...] * gi[...] - fi[...] * gr[...]
        dfr[...] = jnp.sum(br[...] * gr[...] + bi[...] * gi[...], axis=1, keepdims=True)
        dfi[...] = jnp.sum(br[...] * gi[...] - bi[...] * gr[...], axis=1, keepdims=True)

    return pl.pallas_call(
        body, name="s5_input_matrix_bwd",
        out_shape=[jax.ShapeDtypeStruct(b_re.shape, F32)] * 2 + [jax.ShapeDtypeStruct(f_re.shape, F32)] * 2,
    )(f_re, f_im, b_re, b_im, g_re, g_im)


def _s5_params_bwd(a_re, a_im, log_dt, g_ab_re, g_ab_im, d_f_re, d_f_im):
    def body(lr_ref, li_ref, ld_ref, gar, gai, dfr, dfi, o_lr, o_li, o_ld):
        lr, li = lr_ref[...], li_ref[...]
        dt, ab_re, ab_im, nr, ni, den, f_re, f_im = _s5_discretise(lr, li, ld_ref[...])
        d_fr, d_fi = dfr[...], dfi[...]
        d_nr = (d_fr * lr - d_fi * li) / den
        d_ni = (d_fr * li + d_fi * lr) / den
        common = (d_fr * f_re + d_fi * f_im) * 2.0 / den
        d_lr = (d_fr * nr + d_fi * ni) / den - common * lr
        d_li = (d_fr * ni - d_fi * nr) / den - common * li
        d_abr = gar[...] + d_nr
        d_abi = gai[...] + d_ni
        d_mag_mag = d_abr * ab_re + d_abi * ab_im
        d_ang = d_abi * ab_re - d_abr * ab_im
        o_lr[...] = d_lr + d_mag_mag * dt
        o_li[...] = d_li + d_ang * dt
        o_ld[...] = jnp.sum(d_mag_mag * lr + d_ang * li, axis=1, keepdims=True) * dt

    return pl.pallas_call(
        body, name="s5_params_bwd",
        out_shape=[jax.ShapeDtypeStruct(a_re.shape, F32)] * 2 + [jax.ShapeDtypeStruct(log_dt.shape, F32)],
    )(a_re, a_im, log_dt, g_ab_re, g_ab_im, d_f_re, d_f_im)


CONV_COLS = 256


def _shift_down(v, j, row):
    return jnp.where(row >= j, pltpu.roll(v, j, 0), 0.0)


def _shift_up(v, j, row, seq):
    return jnp.where(row < seq - j, pltpu.roll(v, seq - j, 0), 0.0)


def _conv_fwd(up, w_conv, b_conv):
    B, S, _ = up.shape
    nj = D_FF // CONV_COLS

    def body(up_ref, w_ref, b_ref, ff_ref):
        a = up_ref[:, :CONV_COLS]
        val = up_ref[:, CONV_COLS:]
        row = lax.broadcasted_iota(jnp.int32, a.shape, 0)
        w0, w1, w2 = w_ref[0:1, :], w_ref[1:2, :], w_ref[2:3, :]
        conv = b_ref[...] + w0 * a + w1 * _shift_down(a, 1, row) + w2 * _shift_down(a, 2, row)
        ff_ref[...] = (conv * _sigmoid(conv) * val).astype(ff_ref.dtype)

    return pl.pallas_call(
        body, name="conv_gate_fwd", grid=(B, nj),
        in_specs=[pl.BlockSpec((None, S, 2 * CONV_COLS), lambda b, j: (b, 0, j)),
                  pl.BlockSpec((3, CONV_COLS), lambda b, j: (0, j)),
                  pl.BlockSpec((1, CONV_COLS), lambda b, j: (0, j))],
        out_specs=pl.BlockSpec((None, S, CONV_COLS), lambda b, j: (b, 0, j)),
        out_shape=jax.ShapeDtypeStruct((B, S, D_FF), BF16),
        compiler_params=_params(2),
    )(up, w_conv, b_conv)


def _conv_bwd(up, d_ff, w_conv, b_conv):
    B, S, _ = up.shape
    nj = D_FF // CONV_COLS

    def body(up_ref, dff_ref, w_ref, b_ref, dup_ref, dw_ref, db_ref):
        b = pl.program_id(1)
        a = up_ref[:, :CONV_COLS]
        val = up_ref[:, CONV_COLS:]
        row = lax.broadcasted_iota(jnp.int32, a.shape, 0)
        w0, w1, w2 = w_ref[0:1, :], w_ref[1:2, :], w_ref[2:3, :]
        a1, a2 = _shift_down(a, 1, row), _shift_down(a, 2, row)
        conv = b_ref[...] + w0 * a + w1 * a1 + w2 * a2
        sg = _sigmoid(conv)
        dff = dff_ref[...].astype(F32)
        d_val = dff * conv * sg
        dc = dff * val * (sg * (1.0 + conv * (1.0 - sg)))
        d_a = w0 * dc + w1 * _shift_up(dc, 1, row, S) + w2 * _shift_up(dc, 2, row, S)
        dup_ref[:, :CONV_COLS] = d_a.astype(dup_ref.dtype)
        dup_ref[:, CONV_COLS:] = d_val.astype(dup_ref.dtype)

        @pl.when(b == 0)
        def _():
            dw_ref[...] = jnp.zeros_like(dw_ref)
            db_ref[...] = jnp.zeros_like(db_ref)

        dw_ref[0:1, :] += _col_sum(dc * a)
        dw_ref[1:2, :] += _col_sum(dc * a1)
        dw_ref[2:3, :] += _col_sum(dc * a2)
        db_ref[...] += _col_sum(dc)

    return pl.pallas_call(
        body, name="conv_gate_bwd", grid=(nj, B),
        in_specs=[pl.BlockSpec((None, S, 2 * CONV_COLS), lambda j, b: (b, 0, j)),
                  pl.BlockSpec((None, S, CONV_COLS), lambda j, b: (b, 0, j)),
                  pl.BlockSpec((3, CONV_COLS), lambda j, b: (0, j)),
                  pl.BlockSpec((1, CONV_COLS), lambda j, b: (0, j))],
        out_specs=[pl.BlockSpec((None, S, 2 * CONV_COLS), lambda j, b: (b, 0, j)),
                   pl.BlockSpec((3, CONV_COLS), lambda j, b: (0, j)),
                   pl.BlockSpec((1, CONV_COLS), lambda j, b: (0, j))],
        out_shape=[jax.ShapeDtypeStruct((B, S, 2 * D_FF), BF16), jax.ShapeDtypeStruct((3, D_FF), F32),
                   jax.ShapeDtypeStruct((1, D_FF), F32)],
        compiler_params=_params(2),
    )(up, d_ff, w_conv, b_conv)


def _ada_fwd(c_all, w_ada, b_ada):
    def body(c_ref, w_ref, b_ref, o_ref):
        cv = c_ref[...]
        act = (cv * _sigmoid(cv)).astype(BF16)
        o_ref[...] = jnp.dot(act, w_ref[...].astype(BF16), preferred_element_type=F32) + b_ref[...]

    return pl.pallas_call(body, name="ada_fwd",
                          out_shape=jax.ShapeDtypeStruct((c_all.shape[0], w_ada.shape[1]), F32),
                          compiler_params=pltpu.CompilerParams(vmem_limit_bytes=V7X_VMEM_LIMIT))(c_all, w_ada, b_ada)


def _ada_bwd(c_all, dmod_all, dmod_cols):
    def body(c_ref, dm_ref, dmc_ref, dw_ref, db_ref):
        cv = c_ref[...]
        act = (cv * _sigmoid(cv)).astype(BF16)
        dw_ref[...] = lax.dot_general(act, dmc_ref[...].astype(BF16), _TN, preferred_element_type=F32)
        db_ref[...] = _col_sum(dm_ref[...])

    return pl.pallas_call(
        body, name="ada_bwd",
        out_shape=[jax.ShapeDtypeStruct((c_all.shape[1], dmod_cols.shape[1]), F32),
                   jax.ShapeDtypeStruct((1, dmod_all.shape[1]), F32)],
        compiler_params=pltpu.CompilerParams(vmem_limit_bytes=V7X_VMEM_LIMIT))(c_all, dmod_all, dmod_cols)


def _adamw(w, m, v, g_parts, name):
    R, C = w.shape
    P = g_parts.shape[0]
    tr = R
    for cand in (256, 128, 64, 32, 16, 8):
        if R % cand == 0 and cand * C * 4 * (P + 7) * 2 <= V7X_VMEM_LIMIT // 2:
            tr = cand
            break
    c1 = 1.0 / (1.0 - ADAM_B1 ** ADAM_STEP)
    c2 = 1.0 / (1.0 - ADAM_B2 ** ADAM_STEP)

    def body(w_ref, m_ref, v_ref, g_ref, og, od, om, ov):
        g = g_ref[0].astype(F32)
        for p in range(1, P):
            g = g + g_ref[p].astype(F32)
        m_new = ADAM_B1 * m_ref[...] + (1.0 - ADAM_B1) * g
        v_new = ADAM_B2 * v_ref[...] + (1.0 - ADAM_B2) * (g * g)
        og[...] = g
        om[...] = m_new
        ov[...] = v_new
        od[...] = -ADAM_LR * ((m_new * c1) / (jnp.sqrt(v_new * c2) + ADAM_EPS) + ADAM_WD * w_ref[...])

    spec = pl.BlockSpec((tr, C), lambda i: (i, 0))
    return pl.pallas_call(
        body, name=name, grid=(R // tr,),
        in_specs=[spec, spec, spec, pl.BlockSpec((P, tr, C), lambda i: (0, i, 0))],
        out_specs=[spec] * 4, out_shape=[jax.ShapeDtypeStruct((R, C), F32)] * 4,
        compiler_params=_params(1),
    )(w, m, v, g_parts)


def _sum_parts(parts, loss_rows):
    P, R, C = parts.shape
    lo, hi = loss_rows

    def body(p_ref, o_ref, loss_ref):
        t = p_ref[0]
        for p in range(1, P):
            t = t + p_ref[p]
        o_ref[...] = t
        tot = jnp.sum(jnp.sum(o_ref[lo:hi, :], axis=1, keepdims=True), axis=0, keepdims=True)
        loss_ref[...] = jnp.broadcast_to(tot, loss_ref.shape)

    return pl.pallas_call(body, name="sum_small_grads",
                          out_shape=[jax.ShapeDtypeStruct((R, C), F32), jax.ShapeDtypeStruct((1, LANES), F32)],
                          compiler_params=pltpu.CompilerParams(vmem_limit_bytes=V7X_VMEM_LIMIT))(parts)


def _exchange(items, name):
    n = len(items)
    MESH = pl.DeviceIdType.MESH

    def body(*refs):
        src, dst = refs[:n], refs[n:2 * n]
        send_sems, recv_sems, local_sems = refs[2 * n:]
        x, y, c = lax.axis_index("x"), lax.axis_index("y"), lax.axis_index("c")
        me = 4 * x + 2 * y + c
        started = []
        for it, (_, per_peer) in enumerate(items):
            own = pltpu.make_async_copy(src[it].at[me] if per_peer else src[it], dst[it].at[me], local_sems.at[it])
            own.start()
            started.append(own)
        sends, recvs = [], []
        for k in range(1, N_DEV):
            px = 1 - x if k & 4 else x
            py = 1 - y if k & 2 else y
            pc = 1 - c if k & 1 else c
            peer = 4 * px + 2 * py + pc
            for it, (_, per_peer) in enumerate(items):
                s = src[it].at[peer] if per_peer else src[it]
                cp = pltpu.make_async_remote_copy(src_ref=s, dst_ref=dst[it].at[me], send_sem=send_sems.at[it, k - 1],
                                                  recv_sem=recv_sems.at[it, k - 1], device_id=(px, py, pc),
                                                  device_id_type=MESH)
                cp.start()
                sends.append(cp)
                recvs.append(pltpu.make_async_remote_copy(
                    src_ref=s, dst_ref=dst[it].at[peer], send_sem=send_sems.at[it, k - 1],
                    recv_sem=recv_sems.at[it, k - 1], device_id=(px, py, pc), device_id_type=MESH))
        for cp in recvs:
            cp.wait_recv()
        for cp in sends:
            cp.wait_send()
        for cp in started:
            cp.wait()

    any_spec = pl.BlockSpec(memory_space=pl.ANY)
    out_shape = []
    for a, per_peer in items:
        shp = a.shape if per_peer else (N_DEV,) + a.shape
        out_shape.append(jax.ShapeDtypeStruct(shp, a.dtype))
    return pl.pallas_call(
        body, name=name, in_specs=[any_spec] * n, out_specs=[any_spec] * n, out_shape=out_shape,
        scratch_shapes=[pltpu.SemaphoreType.DMA((n, N_DEV - 1)), pltpu.SemaphoreType.DMA((n, N_DEV - 1)),
                        pltpu.SemaphoreType.DMA((n,))],
    )(*[a for a, _ in items])


def _gelu_tanh(y):
    k = math.sqrt(2.0 / math.pi)
    t = jnp.tanh(k * (y + 0.044715 * y * y * y))
    return 0.5 * y * (1.0 + t), t


def _local_step(x, mod, target, W, P):
    B, S, D = x.shape
    T = B * S
    TS = 512
    flat = lambda a: a.reshape(T, a.shape[-1])
    unflat = lambda a: a.reshape(B, S, a.shape[-1])
    mod_col = lambda i: (mod, D, i)

    def f_modnorm(xv, sc, sh, g):
        return (xv * _rms_scale(xv) * g) * (1.0 + sc) + sh

    (u1,) = _rowwise(f_modnorm, [(x, D, 0)], [mod_col(1), mod_col(0)], [P["g_mix"]],
                     [(D, BF16)], [], [], ts=TS, name="modnorm_mix")
    u1f = flat(u1)
    qkv = unflat(_matmul(u1f, W["w_qkv"], name="proj_qkv"))
    us = unflat(_matmul(u1f, W["w_us"], name="proj_ssm_in"))
    gates = unflat(_matmul(u1f, W["w_gates"], name="proj_gates"))

    o_att, lse = _attention_fwd(qkv, P["slopes"])
    y_att = unflat(_matmul(flat(o_att), W["w_proj_att"], name="proj_att"))

    bu = unflat(_matmul(flat(us), P["bb_big"], name="s5_bu"))
    xs = _scan_fwd(bu, P["a_row"])
    y_mm = unflat(_matmul(flat(xs), P["cc_big"], name="s5_readout"))

    def f_glu(ymm, usv, dsk, wg, bg):
        yv = ymm + dsk * usv
        ge, _ = _gelu_tanh(yv)
        pre = jnp.dot(ge.astype(BF16), wg, preferred_element_type=F32) + bg
        return yv, ge * _sigmoid(pre)

    y_s5, z = _rowwise(f_glu, [(y_mm, SSM_WIDTH, 0), (us, SSM_WIDTH, 0)], [], [P["d_skip"], W["w_glu"], P["b_glu"]],
                       [(SSM_WIDTH, F32), (SSM_WIDTH, BF16)], [], [], ts=TS, name="s5_glu")
    y_ssm = unflat(_matmul(flat(z), W["w_proj_ssm"], name="proj_ssm"))

    def f_merge(ga, gs, ya, ys, bga, bgs):
        return _sigmoid(ga + bga) * ya + _sigmoid(gs + bgs) * ys

    bga, bgs = P["b_gate"][:, :D], P["b_gate"][:, D:]
    (merged,) = _rowwise(f_merge, [(gates, D, 0), (gates, D, 1), (y_att, D, 0), (y_ssm, D, 0)], [], [bga, bgs],
                         [(D, BF16)], [], [], ts=TS, name="gate_merge")
    mix = unflat(_matmul(flat(merged), W["w_out"], name="proj_out"))

    def f_res_modnorm(xv, mx, gt, sc, sh, g):
        h = xv + gt * mx
        return h, (h * _rms_scale(h) * g) * (1.0 + sc) + sh

    h1, u2 = _rowwise(f_res_modnorm, [(x, D, 0), (mix, D, 0)], [mod_col(2), mod_col(4), mod_col(3)], [P["g_ffn"]],
                      [(D, F32), (D, BF16)], [], [], ts=TS, name="residual_modnorm_ffn")
    up = unflat(_matmul(flat(u2), W["w_up"], name="ffn_up"))
    ff = _conv_fwd(up, P["w_conv"], P["b_conv"])
    down = unflat(_matmul(flat(ff), W["w_down"], name="ffn_down"))

    def f_head(h1v, dn, tg, gt, g):
        h2 = h1v + gt * dn
        r = _rms_scale(h2)
        nh = h2 * r
        e = nh * g - tg
        dy = e * (1.0 / D)
        gy = dy * g
        dh = r * (gy - nh * jnp.mean(gy * nh, axis=-1, keepdims=True))
        return (dh, dh * gt, _col_sum(dh * dn), _col_sum(dy * nh), _col_sum(e * e) * (0.5 / D))

    dh2, d_down, d_gt2, d_g_final, loss_cols = _rowwise(
        f_head, [(h1, D, 0), (down, D, 0), (target, D, 0)], [mod_col(5)], [P["g_final"]],
        [(D, F32), (D, BF16)], [D], [(1, D), (1, D)], ts=TS, name="head_loss")

    d_downf = flat(d_down)
    d_ff = unflat(_matmul(d_downf, W["w_down"], tb=True, out_dtype=BF16, name="ffn_down_dx"))
    d_w_down = _matmul(flat(ff), d_downf, ta=True, out_dtype=BF16, name="ffn_down_dw")
    d_up, d_w_conv, d_b_conv = _conv_bwd(up, d_ff, P["w_conv"], P["b_conv"])
    d_upf = flat(d_up)
    d_u2 = unflat(_matmul(d_upf, W["w_up"], tb=True, name="ffn_up_dx"))
    d_w_up = _matmul(flat(u2), d_upf, ta=True, out_dtype=BF16, name="ffn_up_dw")

    def f_modnorm_bwd(du, h, dres, mx, sc, gt, g):
        r = _rms_scale(h)
        nh = h * r
        dn = du * (1.0 + sc)
        gy = dn * g
        dh = dres + r * (gy - nh * jnp.mean(gy * nh, axis=-1, keepdims=True))
        return (dh, dh * gt, _col_sum(du), _col_sum(du * nh * g), _col_sum(dh * mx), _col_sum(dn * nh))

    dh1, d_mix, d_sh2, d_sc2, d_gt1, d_g_ffn = _rowwise(
        f_modnorm_bwd, [(d_u2, D, 0), (h1, D, 0), (dh2, D, 0), (mix, D, 0)], [mod_col(4), mod_col(2)], [P["g_ffn"]],
        [(D, F32), (D, BF16)], [D, D, D], [(1, D)], ts=TS, name="modnorm_ffn_bwd")

    d_mixf = flat(d_mix)
    d_merged = unflat(_matmul(d_mixf, W["w_out"], tb=True, name="proj_out_dx"))
    d_w_out = _matmul(flat(merged), d_mixf, ta=True, out_dtype=BF16, name="proj_out_dw")

    def f_merge_bwd(dm, ga, gs, ya, ys, bga_, bgs_):
        sa, ss = _sigmoid(ga + bga_), _sigmoid(gs + bgs_)
        dga = dm * ya * sa * (1.0 - sa)
        dgs = dm * ys * ss * (1.0 - ss)
        return dm * sa, dm * ss, jnp.concatenate([dga, dgs], axis=1), _col_sum(dga), _col_sum(dgs)

    d_y_att, d_y_ssm, d_gates, d_bga, d_bgs = _rowwise(
        f_merge_bwd, [(d_merged, D, 0), (gates, D, 0), (gates, D, 1), (y_att, D, 0), (y_ssm, D, 0)], [], [bga, bgs],
        [(D, BF16), (D, BF16), (2 * D, BF16)], [], [(1, D), (1, D)], ts=TS, name="gate_merge_bwd")

    d_yaf, d_ysf = flat(d_y_att), flat(d_y_ssm)
    d_o_att = unflat(_matmul(d_yaf, W["w_proj_att"], tb=True, name="proj_att_dx"))
    d_w_proj_att = _matmul(flat(o_att), d_yaf, ta=True, out_dtype=BF16, name="proj_att_dw")
    d_z = unflat(_matmul(d_ysf, W["w_proj_ssm"], tb=True, name="proj_ssm_dx"))
    d_w_proj_ssm = _matmul(flat(z), d_ysf, ta=True, out_dtype=BF16, name="proj_ssm_dw")

    def f_glu_bwd(yv, dz, usv, dsk, wg, bg):
        ge, t = _gelu_tanh(yv)
        pre = jnp.dot(ge.astype(BF16), wg, preferred_element_type=F32) + bg
        sg = _sigmoid(pre)
        dpre = dz * ge * sg * (1.0 - sg)
        dge = dz * sg + lax.dot_general(dpre.astype(BF16), wg, _NT, preferred_element_type=F32)
        k = math.sqrt(2.0 / math.pi)
        dgelu = 0.5 * (1.0 + t) + 0.5 * yv * (1.0 - t * t) * k * (1.0 + 3.0 * 0.044715 * yv * yv)
        dy = dge * dgelu
        dwg = lax.dot_general(ge.astype(BF16), dpre.astype(BF16), _TN, preferred_element_type=F32)
        return dy, dy * dsk, dwg, _col_sum(dpre), _col_sum(dy * usv)

    d_y_s5, d_us_skip, d_w_glu, d_b_glu, d_d_skip = _rowwise(
        f_glu_bwd, [(y_s5, SSM_WIDTH, 0), (d_z, SSM_WIDTH, 0), (us, SSM_WIDTH, 0)], [],
        [P["d_skip"], W["w_glu"], P["b_glu"]],
        [(SSM_WIDTH, BF16), (SSM_WIDTH, F32)], [], [(SSM_WIDTH, SSM_WIDTH), (1, SSM_WIDTH), (1, SSM_WIDTH)],
        ts=TS, name="s5_glu_bwd")
    d_ysf2 = flat(d_y_s5)
    dxs = unflat(_matmul(d_ysf2, P["cc_big"], tb=True, name="s5_readout_dx"))
    d_cc = _matmul(flat(xs), d_ysf2, ta=True, name="s5_readout_dw")
    lam, g_ab = _scan_bwd(dxs, xs, P["a_row"])
    lam = flat(lam)
    d_us_mm = unflat(_matmul(lam, P["bb_big"], tb=True, name="s5_bu_dx"))
    d_bb = _matmul(flat(us), lam, ta=True, name="s5_bu_dw")

    d_qkv = _attention_bwd(qkv, o_att, d_o_att, lse, P["slopes"])

    def f_add(a, b_):
        return a + b_

    (d_us,) = _rowwise(f_add, [(d_us_mm, SSM_WIDTH, 0), (d_us_skip, SSM_WIDTH, 0)], [], [],
                       [(SSM_WIDTH, BF16)], [], [], ts=TS, name="s5_input_grad")
    d_qkvf = flat(d_qkv)
    d_usf = flat(d_us)
    d_gatesf = flat(d_gates)
    d_u1 = (_matmul(d_qkvf, W["w_qkv"], tb=True, name="proj_qkv_dx"),
            _matmul(d_usf, W["w_us"], tb=True, name="proj_ssm_in_dx"),
            _matmul(d_gatesf, W["w_gates"], tb=True, name="proj_gates_dx"))
    d_w_in = jnp.concatenate(
        [_unpair_qkv_columns(_matmul(u1f, d_qkvf, ta=True, out_dtype=BF16, name="proj_qkv_dw")),
         _matmul(u1f, d_usf, ta=True, out_dtype=BF16, name="proj_ssm_in_dw"),
         _matmul(u1f, d_gatesf, ta=True, out_dtype=BF16, name="proj_gates_dw")], axis=1)

    def f_modnorm_bwd_in(du0, du1, du2, h, dres, sc, g):
        du = du0 + du1 + du2
        r = _rms_scale(h)
        nh = h * r
        dn = du * (1.0 + sc)
        gy = dn * g
        dh = dres + r * (gy - nh * jnp.mean(gy * nh, axis=-1, keepdims=True))
        return (dh, _col_sum(du), _col_sum(du * nh * g), _col_sum(dn * nh))

    grad_x, d_sh1, d_sc1, d_g_mix = _rowwise(
        f_modnorm_bwd_in, [(unflat(d_u1[0]), D, 0), (unflat(d_u1[1]), D, 0), (unflat(d_u1[2]), D, 0), (x, D, 0),
                           (dh1, D, 0)], [mod_col(1)], [P["g_mix"]],
        [(D, F32)], [D, D], [(1, D)], ts=TS, name="modnorm_mix_bwd")

    d_mod = jnp.concatenate([d_sh1, d_sc1, d_gt1, d_sh2, d_sc2, d_gt2], axis=-1)
    big = dict(w_in=d_w_in, w_glu=d_w_glu, w_proj_att=d_w_proj_att, w_proj_ssm=d_w_proj_ssm, w_out=d_w_out,
               w_up_pairs=d_w_up, w_conv=d_w_conv, w_down=d_w_down)
    g_ab_re, g_ab_im = _deinterleave(g_ab)
    d_bb_re, d_bb_im = _deinterleave(d_bb)
    d_cc_re, d_cc_im = (t.T for t in _deinterleave(d_cc.T))
    small = dict(g_mix=d_g_mix, b_gate=jnp.concatenate([d_bga, d_bgs], axis=1), g_ab_re=g_ab_re, g_ab_im=g_ab_im,
                 d_bb_re=d_bb_re, d_bb_im=d_bb_im, d_cc_re=d_cc_re, d_cc_im=d_cc_im, d_skip=d_d_skip,
                 b_glu=d_b_glu, g_ffn=d_g_ffn, b_conv=d_b_conv, g_final=d_g_final, loss_cols=loss_cols)
    return grad_x, d_mod, big, small


def _block_diag_in(bb):
    t = bb.reshape(SSM_GROUPS, SSM_STATE, SSM_GROUP_CH)
    eye = jnp.eye(SSM_GROUPS, dtype=bb.dtype)
    return jnp.einsum("gnc,gh->gchn", t, eye).reshape(SSM_WIDTH, SSM_COLS)


def _block_diag_out(cm):
    eye = jnp.eye(SSM_GROUPS, dtype=cm.dtype)
    return jnp.einsum("gcn,gh->gnhc", cm, eye).reshape(SSM_COLS, SSM_WIDTH)


def _diag_blocks_in(m):
    t = m.reshape(SSM_GROUPS, SSM_GROUP_CH, SSM_GROUPS, SSM_STATE)
    idx = jnp.arange(SSM_GROUPS)
    return t[idx, :, idx, :].transpose(0, 2, 1).reshape(SSM_COLS, SSM_GROUP_CH)


def _diag_blocks_out(m):
    t = m.reshape(SSM_GROUPS, SSM_STATE, SSM_GROUPS, SSM_GROUP_CH)
    idx = jnp.arange(SSM_GROUPS)
    return t[idx, :, idx, :].transpose(0, 2, 1)


def _pair_columns(w):
    lead = w.shape[:-1]
    a = w[..., :D_FF].reshape(lead + (D_FF // CONV_COLS, 1, CONV_COLS))
    val = w[..., D_FF:].reshape(lead + (D_FF // CONV_COLS, 1, CONV_COLS))
    return jnp.concatenate([a, val], axis=-2).reshape(lead + (2 * D_FF,))


def _unpair_columns(w):
    lead = w.shape[:-1]
    t = w.reshape(lead + (D_FF // CONV_COLS, 2, CONV_COLS))
    return jnp.concatenate([t[..., 0, :].reshape(lead + (D_FF,)), t[..., 1, :].reshape(lead + (D_FF,))], axis=-1)


def _pair_qkv_columns(w):
    lead = w.shape[:-1]
    return w.reshape(lead + (3, N_HEADS // 2, LANES)).swapaxes(-3, -2).reshape(lead + (3 * ATT_WIDTH,))


def _unpair_qkv_columns(w):
    lead = w.shape[:-1]
    return w.reshape(lead + (N_HEADS // 2, 3, LANES)).swapaxes(-3, -2).reshape(lead + (3 * ATT_WIDTH,))


def _interleave(re, im):
    lead = re.shape[:-1]
    g = lambda a: a.reshape(lead + (SSM_COLS // SCAN_COLS, 1, SCAN_COLS))
    return jnp.concatenate([g(re), g(im)], axis=-2).reshape(lead + (2 * SSM_COLS,))


def _deinterleave(x):
    lead = x.shape[:-1]
    t = x.reshape(lead + (SSM_COLS // SCAN_COLS, 2, SCAN_COLS))
    return t[..., 0, :].reshape(lead + (SSM_COLS,)), t[..., 1, :].reshape(lead + (SSM_COLS,))


def _cols_to_slots(g):
    R = g.shape[0]
    return g.reshape(R, N_DEV, g.shape[1] // N_DEV).transpose(1, 0, 2)


def _slots_to_cols(g):
    return g.transpose(1, 0, 2).reshape(g.shape[1], N_DEV * g.shape[2])


SMALL_ORDER = ("b_ada", "g_mix", "b_gate", "a_re", "a_im", "log_dt", "b_re", "b_im", "c_re", "c_im", "d_skip",
               "b_glu", "g_ffn", "b_conv", "g_final")


def _pack(arrs):
    pieces, offs, row = [], [], 0
    for a in arrs:
        f = a.reshape(-1).astype(F32)
        n = f.shape[0]
        rows = -(-n // LANES)
        pieces.append(jnp.pad(f, (0, rows * LANES - n)))
        offs.append((row, n))
        row += rows
    return jnp.concatenate(pieces).reshape(row, LANES), offs


def _unpack(packed, offs, shapes):
    flat = packed.reshape(-1)
    return [flat[r * LANES:r * LANES + n].reshape(s) for (r, n), s in zip(offs, shapes)]


def kernel(x, c, w_ada, b_ada, g_mix, w_in, b_gate, a_re, a_im, log_dt, b_re, b_im, c_re, c_im, d_skip, w_glu, b_glu, w_proj_att, w_proj_ssm, w_out, g_ffn, w_up, w_conv, b_conv, w_down, g_final, loss_target, m_w_ada, m_b_ada, m_g_mix, m_w_in, m_b_gate, m_a_re, m_a_im, m_log_dt, m_b_re, m_b_im, m_c_re, m_c_im, m_d_skip, m_w_glu, m_b_glu, m_w_proj_att, m_w_proj_ssm, m_w_out, m_g_ffn, m_w_up, m_w_conv, m_b_conv, m_w_down, m_g_final, v_w_ada, v_b_ada, v_g_mix, v_w_in, v_b_gate, v_a_re, v_a_im, v_log_dt, v_b_re, v_b_im, v_c_re, v_c_im, v_d_skip, v_w_glu, v_b_glu, v_w_proj_att, v_w_proj_ssm, v_w_out, v_g_ffn, v_w_up, v_w_conv, v_b_conv, v_w_down, v_g_final):
    args = dict(locals())
    B, S, D = x.shape
    me = 4 * lax.axis_index("x") + 2 * lax.axis_index("y") + lax.axis_index("c")
    bf = lambda w: w[0].astype(BF16)

    gathered = _exchange([(c, False), (bf(w_in), False), (bf(w_glu), False), (bf(w_proj_att), False),
                          (bf(w_proj_ssm), False), (bf(w_out), False), (bf(w_up), False), (w_conv[0], False),
                          (bf(w_down), False)], name="gather_weights")
    c_all = gathered[0].reshape(N_DEV * B, D)
    w_in_full = _slots_to_cols(gathered[1])
    n_qkv = 3 * ATT_WIDTH
    W = dict(w_qkv=_pair_qkv_columns(w_in_full[:, :n_qkv]), w_us=w_in_full[:, n_qkv:n_qkv + SSM_WIDTH],
             w_gates=w_in_full[:, n_qkv + SSM_WIDTH:], w_glu=gathered[2].reshape(SSM_WIDTH, SSM_WIDTH),
             w_proj_att=_slots_to_cols(gathered[3]), w_proj_ssm=_slots_to_cols(gathered[4]),
             w_out=gathered[5].reshape(D, D), w_up=_pair_columns(_slots_to_cols(gathered[6])),
             w_down=gathered[8].reshape(D_FF, D))
    w_conv_full = _slots_to_cols(gathered[7])

    n_ada = w_ada.shape[2]
    b_ada_cols = lax.dynamic_slice(b_ada, (0, me * n_ada), (1, n_ada))
    mod_part = _ada_fwd(c_all, w_ada[0], b_ada_cols)
    (mod_slots,) = _exchange([(mod_part.reshape(N_DEV, B, n_ada), True)], name="scatter_modulation")
    mod = mod_slots.transpose(1, 0, 2).reshape(B, 1, 6 * D)

    ab_re, ab_im, f_re, f_im = _s5_params(a_re[0], a_im[0], log_dt[0].reshape(SSM_GROUPS, 1))
    col = lambda a: a.reshape(SSM_COLS, 1)
    b_re2, b_im2 = b_re[0].reshape(SSM_COLS, SSM_GROUP_CH), b_im[0].reshape(SSM_COLS, SSM_GROUP_CH)
    bb_re, bb_im = _s5_input_matrix(col(f_re), col(f_im), b_re2, b_im2)
    slopes = jnp.asarray([2.0 ** (-8.0 * (h + 1) / N_HEADS) for h in range(N_HEADS)], F32)
    P = dict(g_mix=g_mix, g_ffn=g_ffn, g_final=g_final.reshape(1, D), b_gate=b_gate, d_skip=d_skip, b_glu=b_glu,
             b_conv=b_conv, w_conv=w_conv_full, slopes=slopes,
             a_row=_interleave(ab_re.reshape(1, SSM_COLS), ab_im.reshape(1, SSM_COLS)),
             bb_big=_interleave(_block_diag_in(bb_re), _block_diag_in(bb_im)),
             cc_big=_interleave(_block_diag_out(c_re[0]).T, -_block_diag_out(c_im[0]).T).T)

    grad_x, d_mod, big, small = _local_step(x, mod, loss_target, W, P)

    small_list = [small["loss_cols"], small["g_mix"], small["b_gate"], small["g_ab_re"], small["g_ab_im"],
                  _diag_blocks_in(small["d_bb_re"]), _diag_blocks_in(small["d_bb_im"]),
                  _diag_blocks_out(small["d_cc_re"]), -_diag_blocks_out(small["d_cc_im"]),
                  small["g_ffn"], small["b_conv"], small["g_final"], small["d_skip"], small["b_glu"]]
    small_packed, small_offs = _pack(small_list)
    d_w_up_full = _unpair_columns(big["w_up_pairs"])
    sent = _exchange([(_cols_to_slots(big["w_in"]), True),
                      (big["w_glu"].astype(BF16).reshape(N_DEV, SSM_WIDTH // N_DEV, SSM_WIDTH), True),
                      (_cols_to_slots(big["w_proj_att"]), True), (_cols_to_slots(big["w_proj_ssm"]), True),
                      (big["w_out"].reshape(N_DEV, D // N_DEV, D), True), (_cols_to_slots(d_w_up_full), True),
                      (_cols_to_slots(big["w_conv"].astype(BF16)), True),
                      (big["w_down"].reshape(N_DEV, D_FF // N_DEV, D), True),
                      (small_packed, False), (d_mod.reshape(B, 6 * D), False)], name="exchange_gradients")

    out = {}

    def update(name, parts):
        w2 = args[name][0]
        g, dl, mn, vn = _adamw(w2, args["m_" + name][0], args["v_" + name][0], parts, name="adamw_" + name)
        for key, val in (("grad_", g), ("delta_", dl), ("new_m_", mn), ("new_v_", vn)):
            out[key + name] = val[None]

    for name, parts in (("w_in", sent[0]), ("w_glu", sent[1]), ("w_proj_att", sent[2]), ("w_proj_ssm", sent[3]),
                        ("w_out", sent[4]), ("w_up", sent[5]), ("w_conv", sent[6]), ("w_down", sent[7])):
        update(name, parts)

    dmod_all = sent[9].reshape(N_DEV * B, 6 * D)
    dmod_cols = lax.dynamic_slice(dmod_all, (0, me * n_ada), (N_DEV * B, n_ada))
    d_w_ada, d_b_ada = _ada_bwd(c_all, dmod_all, dmod_cols)
    update("w_ada", d_w_ada[None])

    loss_row, loss_n = small_offs[0]
    small_sum, loss_vec = _sum_parts(sent[8], (loss_row, loss_row + loss_n // LANES))
    shapes = [(1, D), (1, D), (1, 2 * D), (SSM_GROUPS, SSM_STATE), (SSM_GROUPS, SSM_STATE), (SSM_COLS, SSM_GROUP_CH),
              (SSM_COLS, SSM_GROUP_CH), (1, SSM_GROUPS, SSM_GROUP_CH, SSM_STATE),
              (1, SSM_GROUPS, SSM_GROUP_CH, SSM_STATE), (1, D), (1, D_FF), (D,), (1, SSM_WIDTH), (1, SSM_WIDTH)]
    (_, s_g_mix, s_b_gate, s_ab_re, s_ab_im, s_bb_re, s_bb_im, s_c_re, s_c_im, s_g_ffn, s_b_conv, s_g_final,
     s_d_skip, s_b_glu) = _unpack(small_sum, small_offs, shapes)
    d_b_re2, d_b_im2, d_f_re, d_f_im = _s5_input_matrix_bwd(col(f_re), col(f_im), b_re2, b_im2, s_bb_re, s_bb_im)
    d_a_re, d_a_im, d_log_dt = _s5_params_bwd(a_re[0], a_im[0], log_dt[0].reshape(SSM_GROUPS, 1), s_ab_re, s_ab_im,
                                              d_f_re.reshape(SSM_GROUPS, SSM_STATE),
                                              d_f_im.reshape(SSM_GROUPS, SSM_STATE))
    grads_small = dict(b_ada=d_b_ada, g_mix=s_g_mix, b_gate=s_b_gate, a_re=d_a_re[None], a_im=d_a_im[None],
                       log_dt=d_log_dt.reshape(1, SSM_GROUPS), b_re=d_b_re2.reshape(b_re.shape),
                       b_im=d_b_im2.reshape(b_im.shape), c_re=s_c_re, c_im=s_c_im, d_skip=s_d_skip, b_glu=s_b_glu,
                       g_ffn=s_g_ffn, b_conv=s_b_conv, g_final=s_g_final)
    w_pack, offs = _pack([args[n] for n in SMALL_ORDER])
    m_pack, _ = _pack([args["m_" + n] for n in SMALL_ORDER])
    v_pack, _ = _pack([args["v_" + n] for n in SMALL_ORDER])
    g_pack, _ = _pack([grads_small[n] for n in SMALL_ORDER])
    res = _adamw(w_pack, m_pack, v_pack, g_pack[None], name="adamw_small")
    shapes_small = [args[n].shape for n in SMALL_ORDER]
    for key, packed in zip(("grad_", "delta_", "new_m_", "new_v_"), res):
        for n, val in zip(SMALL_ORDER, _unpack(packed, offs, shapes_small)):
            out[key + n] = val

    order = ["w_ada", "b_ada", "g_mix", "w_in", "b_gate", "a_re", "a_im", "log_dt", "b_re", "b_im", "c_re", "c_im",
             "d_skip", "w_glu", "b_glu", "w_proj_att", "w_proj_ssm", "w_out", "g_ffn", "w_up", "w_conv", "b_conv",
             "w_down", "g_final"]
    loss = loss_vec[0, 0]
    return (loss, grad_x, *[out[k + n] for k in ("grad_", "delta_", "new_m_", "new_v_") for n in order])
```

```python
import math

import jax
import jax.numpy as jnp
from jax import lax
from jax.experimental import pallas as pl
from jax.experimental.pallas import tpu as pltpu

F32 = jnp.float32
BF16 = jnp.bfloat16

N_DEV = 8
D_MODEL = 1024
N_HEADS = 8
HEAD_DIM = 64
ATT_WIDTH = N_HEADS * HEAD_DIM
DILATIONS = (1, 4, 16)
WIN = 128
SSM_GROUPS = 16
SSM_GROUP_CH = 16
SSM_WIDTH = SSM_GROUPS * SSM_GROUP_CH
SSM_STATE = 64
SSM_COLS = SSM_GROUPS * SSM_STATE
D_FF = 2048
EPS = 1e-6
NEG_INF = -1e30
ADAM_LR, ADAM_B1, ADAM_B2, ADAM_EPS, ADAM_WD, ADAM_STEP = 0.001, 0.9, 0.999, 1e-08, 0.01, 10

V7X_VMEM_LIMIT = 56 * 1024 * 1024
LANES = 128


def _params(n_grid):
    return pltpu.CompilerParams(dimension_semantics=("arbitrary",) * n_grid,
                                vmem_limit_bytes=V7X_VMEM_LIMIT)


def _tile(n, pref):
    if n <= pref:
        return n
    t = (pref // LANES) * LANES
    while t > 0:
        if n % t == 0:
            return t
        t -= LANES
    return n


def _matmul(a, b, *, ta=False, tb=False, out_dtype=F32, name, tm=512, tn=1024, tk=1024):
    if ta:
        K, M = a.shape
    else:
        M, K = a.shape
    if tb:
        N, K2 = b.shape
    else:
        K2, N = b.shape
    assert K == K2, (a.shape, b.shape)
    tm, tn, tk = _tile(M, tm), _tile(N, tn), _tile(K, tk)
    nk = K // tk
    dn = (((0,) if ta else (1,), (1,) if tb else (0,)), ((), ()))

    def body(a_ref, b_ref, o_ref, acc_ref):
        k = pl.program_id(2)

        @pl.when(k == 0)
        def _():
            acc_ref[...] = jnp.zeros_like(acc_ref)

        acc_ref[...] += lax.dot_general(a_ref[...].astype(BF16), b_ref[...].astype(BF16), dn,
                                        preferred_element_type=F32)

        @pl.when(k == nk - 1)
        def _():
            o_ref[...] = acc_ref[...].astype(o_ref.dtype)

    a_spec = (pl.BlockSpec((tk, tm), lambda i, j, k: (k, i)) if ta
              else pl.BlockSpec((tm, tk), lambda i, j, k: (i, k)))
    b_spec = (pl.BlockSpec((tn, tk), lambda i, j, k: (j, k)) if tb
              else pl.BlockSpec((tk, tn), lambda i, j, k: (k, j)))
    return pl.pallas_call(
        body, name=name, grid=(M // tm, N // tn, nk),
        in_specs=[a_spec, b_spec],
        out_specs=pl.BlockSpec((tm, tn), lambda i, j, k: (i, j)),
        out_shape=jax.ShapeDtypeStruct((M, N), out_dtype),
        scratch_shapes=[pltpu.VMEM((tm, tn), F32)],
        compiler_params=_params(3),
    )(a, b)


def _rowwise(fn, rows, bvecs, consts, out_rows, out_b, out_g, *, ts, name):
    B, S = rows[0][0].shape[:2]
    nin = len(rows) + len(bvecs) + len(consts)
    nr, nb, ng = len(out_rows), len(out_b), len(out_g)

    def body(*refs):
        b = pl.program_id(0)
        s = pl.program_id(1)
        outs = fn(*[r[...] for r in refs[:nin]])
        if not isinstance(outs, (tuple, list)):
            outs = (outs,)
        orefs = refs[nin:]
        for i in range(nr):
            orefs[i][...] = outs[i].astype(orefs[i].dtype)
        for i in range(nb):
            ref = orefs[nr + i]

            @pl.when(s == 0)
            def _(ref=ref):
                ref[...] = jnp.zeros_like(ref)

            ref[...] += outs[nr + i]
        for i in range(ng):
            ref = orefs[nr + nb + i]

            @pl.when((s == 0) & (b == 0))
            def _(ref=ref):
                ref[...] = jnp.zeros_like(ref)

            ref[...] += outs[nr + nb + i]

    in_specs = ([pl.BlockSpec((None, ts, cb), lambda b, s, ci=ci: (b, s, ci)) for (_, cb, ci) in rows]
                + [pl.BlockSpec((None, 1, cb), lambda b, s, ci=ci: (b, 0, ci)) for (_, cb, ci) in bvecs]
                + [pl.BlockSpec(a.shape, lambda b, s: (0, 0)) for a in consts])
    out_shape = ([jax.ShapeDtypeStruct((B, S, c), dt) for (c, dt) in out_rows]
                 + [jax.ShapeDtypeStruct((B, 1, c), F32) for c in out_b]
                 + [jax.ShapeDtypeStruct(rc, F32) for rc in out_g])
    out_specs = ([pl.BlockSpec((None, ts, c), lambda b, s: (b, s, 0)) for (c, _) in out_rows]
                 + [pl.BlockSpec((None, 1, c), lambda b, s: (b, 0, 0)) for c in out_b]
                 + [pl.BlockSpec(rc, lambda b, s: (0, 0)) for rc in out_g])
    args = [a for (a, _, _) in rows] + [a for (a, _, _) in bvecs] + list(consts)
    return pl.pallas_call(
        body, name=name, grid=(B, S // ts), in_specs=in_specs, out_specs=out_specs,
        out_shape=out_shape, compiler_params=_params(2),
    )(*args)


def _col_sum(v):
    return jnp.sum(v, axis=0, keepdims=True)


def _rms_scale(h):
    return lax.rsqrt(jnp.mean(h * h, axis=-1, keepdims=True) + EPS)


def _sigmoid(v):
    return 1.0 / (1.0 + jnp.exp(-v))


ATT_SCALE = HEAD_DIM ** -0.5
COPY_ROWS = 256
_NT = (((1,), (1,)), ((), ()))
_TN = (((0,), (0,)), ((), ()))


def _row_chunks(d, seq):
    sub = seq // d
    out = []
    for r in range(d):
        for c0 in range(0, sub, COPY_ROWS):
            n = min(COPY_ROWS, sub - c0)
            out.append((pl.ds(r + c0 * d, n, stride=d), r * sub + c0, n))
    return out


ATT_UNROLL = 2
KEYS = 2 * WIN


def _zero_once(refs):
    @pl.when((pl.program_id(0) == 0) & (pl.program_id(1) == 0))
    def _():
        for r in refs:
            r[...] = jnp.zeros_like(r)


def _pair_bias(bias_ref, slopes_ref, hp, d, key_major):
    shape = (KEYS, WIN) if key_major else (WIN, KEYS)
    qi = lax.broadcasted_iota(jnp.int32, shape, 1 if key_major else 0)
    kj = lax.broadcasted_iota(jnp.int32, shape, 0 if key_major else 1)
    dist = WIN + qi - kj
    valid = (dist >= 0) & (dist <= WIN)
    distf = dist.astype(F32)
    for h in range(2):
        slope_d = slopes_ref[2 * hp + h] * float(d)
        with_prev = jnp.where(valid, -(slope_d * distf), NEG_INF)
        no_prev = jnp.where(kj >= WIN, with_prev, NEG_INF)
        span = slice(h * KEYS, (h + 1) * KEYS)
        if key_major:
            bias_ref[1, span, :] = with_prev
            bias_ref[0, span, :] = no_prev
        else:
            bias_ref[1, :, span] = with_prev
            bias_ref[0, :, span] = no_prev


def _stack_heads(v):
    first = lax.broadcasted_iota(jnp.int32, v.shape, 1) < HEAD_DIM
    zero = jnp.zeros_like(v)
    return jnp.concatenate([jnp.where(first, v, zero), jnp.where(first, zero, v)], axis=0)


def _per_head(c0, c1, n):
    return jnp.where(lax.broadcasted_iota(jnp.int32, (n, LANES), 1) < HEAD_DIM, c0, c1)


def _qkv_spec(seq, j):
    return pl.BlockSpec((None, seq, LANES), lambda b, hp: (b, 0, 3 * hp + j))


def _attention_fwd(qkv, slopes):
    B, S, _ = qkv.shape
    n_blk = S // WIN
    n_pair = N_HEADS // 2

    def body(slopes_ref, q_ref, k_ref, v_ref, o_ref, lse_ref, qp, kp, vp, bias, acc, mx, sm, acc_n, mx_n, sm_n):
        hp = pl.program_id(1)
        _zero_once((kp, vp))
        for p, d in enumerate(DILATIONS):
            nb = n_blk // d
            chunks = _row_chunks(d, S)
            for src, dst, n in chunks:
                qp[dst:dst + n, :] = (q_ref[src, :] * ATT_SCALE).astype(BF16)
                kp[WIN + dst:WIN + dst + n, :] = k_ref[src, :].astype(BF16)
                vp[WIN + dst:WIN + dst + n, :] = v_ref[src, :].astype(BF16)
            _pair_bias(bias, slopes_ref, hp, d, key_major=False)
            acc_t, mx_t, sm_t = (acc_n, mx_n, sm_n) if d == 1 else (acc, mx, sm)

            def block(i, carry, p=p, nb=nb, acc_t=acc_t, mx_t=mx_t, sm_t=sm_t):
                cur = pl.ds(pl.multiple_of(i * WIN, WIN), WIN)
                keys = pl.ds(pl.multiple_of(i * WIN, WIN), KEYS)
                flag = ((i % nb) > 0).astype(jnp.int32)
                s = lax.dot_general(qp[cur, :], _stack_heads(kp[keys, :]), _NT, preferred_element_type=F32)
                s = s + bias[flag]
                es, ms, ls = [], [], []
                for h in range(2):
                    sh = s[:, h * KEYS:(h + 1) * KEYS]
                    m = jnp.max(jnp.maximum(sh[:, :WIN], sh[:, WIN:]), axis=1, keepdims=True)
                    e = jnp.exp(sh - m)
                    es.append(e.astype(BF16))
                    ms.append(m)
                    ls.append(jnp.sum(e[:, :WIN] + e[:, WIN:], axis=1, keepdims=True))
                acc_t[p, cur, :] = jnp.dot(jnp.concatenate(es, axis=1), _stack_heads(vp[keys, :]),
                                           preferred_element_type=F32)
                mx_t[p, cur, :] = _per_head(ms[0], ms[1], WIN)
                sm_t[p, cur, :] = _per_head(ls[0], ls[1], WIN)
                return carry

            lax.fori_loop(0, n_blk, block, 0, unroll=ATT_UNROLL)
            if d > 1:
                for src, dst, n in chunks:
                    acc_n[p, src, :] = acc[p, dst:dst + n, :]
                    mx_n[p, src, :] = mx[p, dst:dst + n, :]
                    sm_n[p, src, :] = sm[p, dst:dst + n, :]

        chunk = 256

        def merge(i, carry):
            rows = pl.ds(pl.multiple_of(i * chunk, chunk), chunk)
            ms = [mx_n[p, rows, :] for p in range(3)]
            m = jnp.maximum(jnp.maximum(ms[0], ms[1]), ms[2])
            ws = [jnp.exp(mp - m) for mp in ms]
            l = ws[0] * sm_n[0, rows, :] + ws[1] * sm_n[1, rows, :] + ws[2] * sm_n[2, rows, :]
            o = (ws[0] * acc_n[0, rows, :] + ws[1] * acc_n[1, rows, :] + ws[2] * acc_n[2, rows, :]) / l
            o_ref[rows, :] = o.astype(o_ref.dtype)
            lse = m + jnp.log(l)
            for h in range(2):
                lse_ref[rows, h:h + 1] = lse[:, h * HEAD_DIM:h * HEAD_DIM + 1]
            return carry

        lax.fori_loop(0, S // chunk, merge, 0)

    return pl.pallas_call(
        body, name="attention_fwd", grid=(B, n_pair),
        in_specs=[pl.BlockSpec(memory_space=pltpu.SMEM), _qkv_spec(S, 0), _qkv_spec(S, 1), _qkv_spec(S, 2)],
        out_specs=[pl.BlockSpec((None, S, LANES), lambda b, hp: (b, 0, hp)),
                   pl.BlockSpec((None, None, S, 2), lambda b, hp: (b, hp, 0, 0))],
        out_shape=[jax.ShapeDtypeStruct((B, S, ATT_WIDTH), BF16),
                   jax.ShapeDtypeStruct((B, n_pair, S, 2), F32)],
        scratch_shapes=[pltpu.VMEM((S, LANES), BF16), pltpu.VMEM((S + WIN, LANES), BF16),
                        pltpu.VMEM((S + WIN, LANES), BF16), pltpu.VMEM((2, WIN, 2 * KEYS), F32)]
        + [pltpu.VMEM((3, S, LANES), F32)] * 6,
        compiler_params=_params(2),
    )(slopes, qkv, qkv, qkv)


def _attention_bwd(qkv, o, do, lse, slopes):
    B, S, _ = qkv.shape
    n_blk = S // WIN
    n_pair = N_HEADS // 2

    def body(slopes_ref, q_ref, k_ref, v_ref, o_ref, do_ref, lse_ref, dx_ref,
             qp, dop, kp, vp, aux, auxp, aux_t, bias_t, dqp, dkp, dvp, dq_n, dk_n, dv_n):
        hp = pl.program_id(1)
        aux[...] = jnp.zeros_like(aux)
        for c0 in range(0, S, COPY_ROWS):
            rows = slice(c0, c0 + COPY_ROWS)
            prod = do_ref[rows, :] * o_ref[rows, :].astype(F32)
            for h in range(2):
                aux[rows, 2 * h:2 * h + 1] = lse_ref[rows, h:h + 1]
                aux[rows, 2 * h + 1:2 * h + 2] = jnp.sum(prod[:, h * HEAD_DIM:(h + 1) * HEAD_DIM], axis=1,
                                                         keepdims=True)
        dq_n[...] = jnp.zeros_like(dq_n)
        dk_n[...] = jnp.zeros_like(dk_n)
        dv_n[...] = jnp.zeros_like(dv_n)
        _zero_once((kp, vp))
        for p, d in enumerate(DILATIONS):
            nb = n_blk // d
            chunks = _row_chunks(d, S)
            for src, dst, n in chunks:
                auxp[dst:dst + n, :] = aux[src, :]
                qp[dst:dst + n, :] = (q_ref[src, :] * ATT_SCALE).astype(BF16)
                dop[dst:dst + n, :] = do_ref[src, :].astype(BF16)
                kp[WIN + dst:WIN + dst + n, :] = k_ref[src, :].astype(BF16)
                vp[WIN + dst:WIN + dst + n, :] = v_ref[src, :].astype(BF16)
            for i in range(n_blk):
                aux_t[i] = auxp[i * WIN:(i + 1) * WIN, :].T[0:8, :]
            _pair_bias(bias_t, slopes_ref, hp, d, key_major=True)
            dkp[...] = jnp.zeros_like(dkp)
            dvp[...] = jnp.zeros_like(dvp)

            def block(i, carry, nb=nb):
                cur = pl.ds(pl.multiple_of(i * WIN, WIN), WIN)
                keys = pl.ds(pl.multiple_of(i * WIN, WIN), KEYS)
                flag = ((i % nb) > 0).astype(jnp.int32)
                q2, do2 = qp[cur, :], dop[cur, :]
                kc = _stack_heads(kp[keys, :])
                s_t = lax.dot_general(kc, q2, _NT, preferred_element_type=F32) + bias_t[flag]
                dp_t = lax.dot_general(_stack_heads(vp[keys, :]), do2, _NT, preferred_element_type=F32)
                ps, dss = [], []
                for h in range(2):
                    span = slice(h * KEYS, (h + 1) * KEYS)
                    p_t = jnp.exp(s_t[span, :] - aux_t[i, 2 * h:2 * h + 1, :])
                    ds_t = p_t * (dp_t[span, :] - aux_t[i, 2 * h + 1:2 * h + 2, :])
                    ps.append(p_t.astype(BF16))
                    dss.append(ds_t.astype(BF16))
                dvp[keys, :] += jnp.dot(jnp.concatenate(ps, axis=1), _stack_heads(do2), preferred_element_type=F32)
                dkp[keys, :] += jnp.dot(jnp.concatenate(dss, axis=1), _stack_heads(q2), preferred_element_type=F32)
                dqp[cur, :] = lax.dot_general(jnp.concatenate(dss, axis=0), kc, _TN, preferred_element_type=F32)
                return carry

            lax.fori_loop(0, n_blk, block, 0)
            for src, dst, n in chunks:
                dq_n[src, :] += dqp[dst:dst + n, :]
                dk_n[src, :] += dkp[WIN + dst:WIN + dst + n, :]
                dv_n[src, :] += dvp[WIN + dst:WIN + dst + n, :]
        for c0 in range(0, S, COPY_ROWS):
            rows = slice(c0, c0 + COPY_ROWS)
            dx_ref[rows, 0:LANES] = (dq_n[rows, :] * ATT_SCALE).astype(dx_ref.dtype)
            dx_ref[rows, LANES:2 * LANES] = dk_n[rows, :].astype(dx_ref.dtype)
            dx_ref[rows, 2 * LANES:3 * LANES] = dv_n[rows, :].astype(dx_ref.dtype)

    pair = lambda width: pl.BlockSpec((None, S, width), lambda b, hp: (b, 0, hp))
    vm = lambda shape, dt: pltpu.VMEM(shape, dt)
    return pl.pallas_call(
        body, name="attention_bwd", grid=(B, n_pair),
        in_specs=[pl.BlockSpec(memory_space=pltpu.SMEM), _qkv_spec(S, 0), _qkv_spec(S, 1), _qkv_spec(S, 2),
                  pair(LANES), pair(LANES), pl.BlockSpec((None, None, S, 2), lambda b, hp: (b, hp, 0, 0))],
        out_specs=pair(3 * LANES),
        out_shape=jax.ShapeDtypeStruct((B, S, 3 * ATT_WIDTH), BF16),
        scratch_shapes=[vm((S, LANES), BF16), vm((S, LANES), BF16),
                        vm((S + WIN, LANES), BF16), vm((S + WIN, LANES), BF16),
                        vm((S, LANES), F32), vm((S, LANES), F32), vm((n_blk, 8, WIN), F32),
                        vm((2, 2 * KEYS, WIN), F32),
                        vm((S, LANES), F32), vm((S + WIN, LANES), F32), vm((S + WIN, LANES), F32),
                        vm((S, LANES), F32), vm((S, LANES), F32), vm((S, LANES), F32)],
        compiler_params=_params(2),
    )(slopes, qkv, qkv, qkv, o, do, lse)


SCAN_COLS = 256
SCAN_ROWS = 8


def _rows_to_tile(rows):
    rid = lax.broadcasted_iota(jnp.int32, (SCAN_ROWS, rows[0].shape[1]), 0)
    tile = jnp.broadcast_to(rows[0], rid.shape)
    for k in range(1, SCAN_ROWS):
        tile = jnp.where(rid == k, rows[k], tile)
    return tile


def _scan_fwd(bu, a_row):
    B, S, _ = bu.shape
    nc = SSM_COLS // SCAN_COLS
    nt = S // SCAN_ROWS
    RE, IM = slice(0, SCAN_COLS), slice(SCAN_COLS, 2 * SCAN_COLS)

    def body(bu_ref, a_ref, xs_ref):
        ar, ai = a_ref[:, RE], a_ref[:, IM]

        def tile(i, carry):
            xr, xi = carry
            out_r, out_i = [], []
            for k in range(SCAN_ROWS):
                xr, xi = (ar * xr - ai * xi + bu_ref[i, k:k + 1, RE], ar * xi + ai * xr + bu_ref[i, k:k + 1, IM])
                out_r.append(xr)
                out_i.append(xi)
            xs_ref[i, :, RE] = _rows_to_tile(out_r)
            xs_ref[i, :, IM] = _rows_to_tile(out_i)
            return xr, xi

        zero = jnp.zeros((1, SCAN_COLS), F32)
        lax.fori_loop(0, nt, tile, (zero, zero))

    col = pl.BlockSpec((None, nt, SCAN_ROWS, 2 * SCAN_COLS), lambda b, j: (b, 0, 0, j))
    xs = pl.pallas_call(
        body, name="s5_scan_fwd", grid=(B, nc),
        in_specs=[col, pl.BlockSpec((1, 2 * SCAN_COLS), lambda b, j: (0, j))],
        out_specs=col,
        out_shape=jax.ShapeDtypeStruct((B, nt, SCAN_ROWS, 2 * SSM_COLS), F32),
        compiler_params=_params(2),
    )(bu.reshape(B, nt, SCAN_ROWS, 2 * SSM_COLS), a_row)
    return xs.reshape(B, S, 2 * SSM_COLS)


def _scan_bwd(dxs, xs, a_row):
    B, S, _ = dxs.shape
    nc = SSM_COLS // SCAN_COLS
    nt = S // SCAN_ROWS
    RE, IM = slice(0, SCAN_COLS), slice(SCAN_COLS, 2 * SCAN_COLS)

    def body(d_ref, x_ref, a_ref, lam_ref, ga_ref):
        b = pl.program_id(1)
        ar, ai = a_ref[:, RE], a_ref[:, IM]
        rid = lax.broadcasted_iota(jnp.int32, (SCAN_ROWS, SCAN_COLS), 0)

        @pl.when(b == 0)
        def _():
            ga_ref[...] = jnp.zeros_like(ga_ref)

        def tile(j, carry):
            lr, li, accr, acci = carry
            i = nt - 1 - j
            out_r, out_i = [None] * SCAN_ROWS, [None] * SCAN_ROWS
            for k in reversed(range(SCAN_ROWS)):
                lr, li = (d_ref[i, k:k + 1, RE] + ar * lr + ai * li, d_ref[i, k:k + 1, IM] + ar * li - ai * lr)
                out_r[k] = lr
                out_i[k] = li
            lam_r = _rows_to_tile(out_r)
            lam_i = _rows_to_tile(out_i)
            lam_ref[i, :, RE] = lam_r
            lam_ref[i, :, IM] = lam_i
            ip = jnp.maximum(i - 1, 0)
            keep = (i > 0).astype(F32)
            last = slice(SCAN_ROWS - 1, SCAN_ROWS)
            xpr = jnp.where(rid == 0, x_ref[ip, last, RE] * keep, pltpu.roll(x_ref[i, :, RE], 1, 0))
            xpi = jnp.where(rid == 0, x_ref[ip, last, IM] * keep, pltpu.roll(x_ref[i, :, IM], 1, 0))
            accr = accr + lam_r * xpr + lam_i * xpi
            acci = acci + lam_i * xpr - lam_r * xpi
            return lr, li, accr, acci

        z1 = jnp.zeros((1, SCAN_COLS), F32)
        z8 = jnp.zeros((SCAN_ROWS, SCAN_COLS), F32)
        _, _, accr, acci = lax.fori_loop(0, nt, tile, (z1, z1, z8, z8))
        ga_ref[:, RE] += _col_sum(accr)
        ga_ref[:, IM] += _col_sum(acci)

    col = pl.BlockSpec((None, nt, SCAN_ROWS, 2 * SCAN_COLS), lambda j, b: (b, 0, 0, j))
    par = pl.BlockSpec((1, 2 * SCAN_COLS), lambda j, b: (0, j))
    t4 = lambda a: a.reshape(B, nt, SCAN_ROWS, 2 * SSM_COLS)
    lam, g_a = pl.pallas_call(
        body, name="s5_scan_bwd", grid=(nc, B),
        in_specs=[col, col, par], out_specs=[col, par],
        out_shape=[jax.ShapeDtypeStruct((B, nt, SCAN_ROWS, 2 * SSM_COLS), F32),
                   jax.ShapeDtypeStruct((1, 2 * SSM_COLS), F32)],
        compiler_params=_params(2),
    )(t4(dxs), t4(xs), a_row)
    return lam.reshape(B, S, 2 * SSM_COLS), g_a


def _s5_discretise(lr, li, log_dt):
    dt = jnp.exp(log_dt)
    mag = jnp.exp(lr * dt)
    ang = li * dt
    ab_re, ab_im = mag * jnp.cos(ang), mag * jnp.sin(ang)
    nr, ni = ab_re - 1.0, ab_im
    den = lr * lr + li * li
    f_re = (nr * lr + ni * li) / den
    f_im = (ni * lr - nr * li) / den
    return dt, ab_re, ab_im, nr, ni, den, f_re, f_im


def _s5_params(a_re, a_im, log_dt):
    def body(lr_ref, li_ref, ld_ref, abr, abi, fr, fi):
        _, ab_re, ab_im, _, _, _, f_re, f_im = _s5_discretise(lr_ref[...], li_ref[...], ld_ref[...])
        abr[...] = ab_re
        abi[...] = ab_im
        fr[...] = f_re
        fi[...] = f_im

    return pl.pallas_call(body, name="s5_params",
                          out_shape=[jax.ShapeDtypeStruct(a_re.shape, F32)] * 4)(a_re, a_im, log_dt)


def _s5_input_matrix(f_re, f_im, b_re, b_im):
    def body(fr, fi, br, bi, o_re, o_im):
        o_re[...] = fr[...] * br[...] - fi[...] * bi[...]
        o_im[...] = fr[...] * bi[...] + fi[...] * br[...]

    return pl.pallas_call(body, name="s5_input_matrix",
                          out_shape=[jax.ShapeDtypeStruct(b_re.shape, F32)] * 2)(f_re, f_im, b_re, b_im)


def _s5_input_matrix_bwd(f_re, f_im, b_re, b_im, g_re, g_im):
    def body(fr, fi, br, bi, gr, gi, dbr, dbi, dfr, dfi):
        dbr[...] = fr[...] * gr[...] + fi[...] * gi[...]
        dbi[...] = fr[...] * gi[...] - fi[...] * gr[...]
        dfr[...] = jnp.sum(br[...] * gr[...] + bi[...] * gi[...], axis=1, keepdims=True)
        dfi[...] = jnp.sum(br[...] * gi[...] - bi[...] * gr[...], axis=1, keepdims=True)

    return pl.pallas_call(
        body, name="s5_input_matrix_bwd",
        out_shape=[jax.ShapeDtypeStruct(b_re.shape, F32)] * 2 + [jax.ShapeDtypeStruct(f_re.shape, F32)] * 2,
    )(f_re, f_im, b_re, b_im, g_re, g_im)


def _s5_params_bwd(a_re, a_im, log_dt, g_ab_re, g_ab_im, d_f_re, d_f_im):
    def body(lr_ref, li_ref, ld_ref, gar, gai, dfr, dfi, o_lr, o_li, o_ld):
        lr, li = lr_ref[...], li_ref[...]
        dt, ab_re, ab_im, nr, ni, den, f_re, f_im = _s5_discretise(lr, li, ld_ref[...])
        d_fr, d_fi = dfr[...], dfi[...]
        d_nr = (d_fr * lr - d_fi * li) / den
        d_ni = (d_fr * li + d_fi * lr) / den
        common = (d_fr * f_re + d_fi * f_im) * 2.0 / den
        d_lr = (d_fr * nr + d_fi * ni) / den - common * lr
        d_li = (d_fr * ni - d_fi * nr) / den - common * li
        d_abr = gar[...] + d_nr
        d_abi = gai[...] + d_ni
        d_mag_mag = d_abr * ab_re + d_abi * ab_im
        d_ang = d_abi * ab_re - d_abr * ab_im
        o_lr[...] = d_lr + d_mag_mag * dt
        o_li[...] = d_li + d_ang * dt
        o_ld[...] = jnp.sum(d_mag_mag * lr + d_ang * li, axis=1, keepdims=True) * dt

    return pl.pallas_call(
        body, name="s5_params_bwd",
        out_shape=[jax.ShapeDtypeStruct(a_re.shape, F32)] * 2 + [jax.ShapeDtypeStruct(log_dt.shape, F32)],
    )(a_re, a_im, log_dt, g_ab_re, g_ab_im, d_f_re, d_f_im)


CONV_COLS = 256


def _shift_down(v, j, row):
    return jnp.where(row >= j, pltpu.roll(v, j, 0), 0.0)


def _shift_up(v, j, row, seq):
    return jnp.where(row < seq - j, pltpu.roll(v, seq - j, 0), 0.0)


def _conv_fwd(up, w_conv, b_conv):
    B, S, _ = up.shape
    nj = D_FF // CONV_COLS

    def body(up_ref, w_ref, b_ref, ff_ref):
        a = up_ref[:, :CONV_COLS]
        val = up_ref[:, CONV_COLS:]
        row = lax.broadcasted_iota(jnp.int32, a.shape, 0)
        w0, w1, w2 = w_ref[0:1, :], w_ref[1:2, :], w_ref[2:3, :]
        conv = b_ref[...] + w0 * a + w1 * _shift_down(a, 1, row) + w2 * _shift_down(a, 2, row)
        ff_ref[...] = (conv * _sigmoid(conv) * val).astype(ff_ref.dtype)

    return pl.pallas_call(
        body, name="conv_gate_fwd", grid=(B, nj),
        in_specs=[pl.BlockSpec((None, S, 2 * CONV_COLS), lambda b, j: (b, 0, j)),
                  pl.BlockSpec((3, CONV_COLS), lambda b, j: (0, j)),
                  pl.BlockSpec((1, CONV_COLS), lambda b, j: (0, j))],
        out_specs=pl.BlockSpec((None, S, CONV_COLS), lambda b, j: (b, 0, j)),
        out_shape=jax.ShapeDtypeStruct((B, S, D_FF), BF16),
        compiler_params=_params(2),
    )(up, w_conv, b_conv)


def _conv_bwd(up, d_ff, w_conv, b_conv):
    B, S, _ = up.shape
    nj = D_FF // CONV_COLS

    def body(up_ref, dff_ref, w_ref, b_ref, dup_ref, dw_ref, db_ref):
        b = pl.program_id(1)
        a = up_ref[:, :CONV_COLS]
        val = up_ref[:, CONV_COLS:]
        row = lax.broadcasted_iota(jnp.int32, a.shape, 0)
        w0, w1, w2 = w_ref[0:1, :], w_ref[1:2, :], w_ref[2:3, :]
        a1, a2 = _shift_down(a, 1, row), _shift_down(a, 2, row)
        conv = b_ref[...] + w0 * a + w1 * a1 + w2 * a2
        sg = _sigmoid(conv)
        dff = dff_ref[...].astype(F32)
        d_val = dff * conv * sg
        dc = dff * val * (sg * (1.0 + conv * (1.0 - sg)))
        d_a = w0 * dc + w1 * _shift_up(dc, 1, row, S) + w2 * _shift_up(dc, 2, row, S)
        dup_ref[:, :CONV_COLS] = d_a.astype(dup_ref.dtype)
        dup_ref[:, CONV_COLS:] = d_val.astype(dup_ref.dtype)

        @pl.when(b == 0)
        def _():
            dw_ref[...] = jnp.zeros_like(dw_ref)
            db_ref[...] = jnp.zeros_like(db_ref)

        dw_ref[0:1, :] += _col_sum(dc * a)
        dw_ref[1:2, :] += _col_sum(dc * a1)
        dw_ref[2:3, :] += _col_sum(dc * a2)
        db_ref[...] += _col_sum(dc)

    return pl.pallas_call(
        body, name="conv_gate_bwd", grid=(nj, B),
        in_specs=[pl.BlockSpec((None, S, 2 * CONV_COLS), lambda j, b: (b, 0, j)),
                  pl.BlockSpec((None, S, CONV_COLS), lambda j, b: (b, 0, j)),
                  pl.BlockSpec((3, CONV_COLS), lambda j, b: (0, j)),
                  pl.BlockSpec((1, CONV_COLS), lambda j, b: (0, j))],
        out_specs=[pl.BlockSpec((None, S, 2 * CONV_COLS), lambda j, b: (b, 0, j)),
                   pl.BlockSpec((3, CONV_COLS), lambda j, b: (0, j)),
                   pl.BlockSpec((1, CONV_COLS), lambda j, b: (0, j))],
        out_shape=[jax.ShapeDtypeStruct((B, S, 2 * D_FF), BF16), jax.ShapeDtypeStruct((3, D_FF), F32),
                   jax.ShapeDtypeStruct((1, D_FF), F32)],
        compiler_params=_params(2),
    )(up, d_ff, w_conv, b_conv)


def _ada_fwd(c_all, w_ada, b_ada):
    def body(c_ref, w_ref, b_ref, o_ref):
        cv = c_ref[...]
        act = (cv * _sigmoid(cv)).astype(BF16)
        o_ref[...] = jnp.dot(act, w_ref[...].astype(BF16), preferred_element_type=F32) + b_ref[...]

    return pl.pallas_call(body, name="ada_fwd",
                          out_shape=jax.ShapeDtypeStruct((c_all.shape[0], w_ada.shape[1]), F32),
                          compiler_params=pltpu.CompilerParams(vmem_limit_bytes=V7X_VMEM_LIMIT))(c_all, w_ada, b_ada)


def _ada_bwd(c_all, dmod_all, dmod_cols):
    def body(c_ref, dm_ref, dmc_ref, dw_ref, db_ref):
        cv = c_ref[...]
        act = (cv * _sigmoid(cv)).astype(BF16)
        dw_ref[...] = lax.dot_general(act, dmc_ref[...].astype(BF16), _TN, preferred_element_type=F32)
        db_ref[...] = _col_sum(dm_ref[...])

    return pl.pallas_call(
        body, name="ada_bwd",
        out_shape=[jax.ShapeDtypeStruct((c_all.shape[1], dmod_cols.shape[1]), F32),
                   jax.ShapeDtypeStruct((1, dmod_all.shape[1]), F32)],
        compiler_params=pltpu.CompilerParams(vmem_limit_bytes=V7X_VMEM_LIMIT))(c_all, dmod_all, dmod_cols)


def _adamw(w, m, v, g_parts, name):
    R, C = w.shape
    P = g_parts.shape[0]
    tr = R
    for cand in (256, 128, 64, 32, 16, 8):
        if R % cand == 0 and cand * C * 4 * (P + 7) * 2 <= V7X_VMEM_LIMIT // 2:
            tr = cand
            break
    c1 = 1.0 / (1.0 - ADAM_B1 ** ADAM_STEP)
    c2 = 1.0 / (1.0 - ADAM_B2 ** ADAM_STEP)

    def body(w_ref, m_ref, v_ref, g_ref, og, od, om, ov):
        g = g_ref[0].astype(F32)
        for p in range(1, P):
            g = g + g_ref[p].astype(F32)
        m_new = ADAM_B1 * m_ref[...] + (1.0 - ADAM_B1) * g
        v_new = ADAM_B2 * v_ref[...] + (1.0 - ADAM_B2) * (g * g)
        og[...] = g
        om[...] = m_new
        ov[...] = v_new
        od[...] = -ADAM_LR * ((m_new * c1) / (jnp.sqrt(v_new * c2) + ADAM_EPS) + ADAM_WD * w_ref[...])

    spec = pl.BlockSpec((tr, C), lambda i: (i, 0))
    return pl.pallas_call(
        body, name=name, grid=(R // tr,),
        in_specs=[spec, spec, spec, pl.BlockSpec((P, tr, C), lambda i: (0, i, 0))],
        out_specs=[spec] * 4, out_shape=[jax.ShapeDtypeStruct((R, C), F32)] * 4,
        compiler_params=_params(1),
    )(w, m, v, g_parts)


def _sum_parts(parts, loss_rows):
    P, R, C = parts.shape
    lo, hi = loss_rows

    def body(p_ref, o_ref, loss_ref):
        t = p_ref[0]
        for p in range(1, P):
            t = t + p_ref[p]
        o_ref[...] = t
        tot = jnp.sum(jnp.sum(o_ref[lo:hi, :], axis=1, keepdims=True), axis=0, keepdims=True)
        loss_ref[...] = jnp.broadcast_to(tot, loss_ref.shape)

    return pl.pallas_call(body, name="sum_small_grads",
                          out_shape=[jax.ShapeDtypeStruct((R, C), F32), jax.ShapeDtypeStruct((1, LANES), F32)],
                          compiler_params=pltpu.CompilerParams(vmem_limit_bytes=V7X_VMEM_LIMIT))(parts)


def _exchange(items, name):
    n = len(items)
    MESH = pl.DeviceIdType.MESH

    def body(*refs):
        src, dst = refs[:n], refs[n:2 * n]
        send_sems, recv_sems, local_sems = refs[2 * n:]
        x, y, c = lax.axis_index("x"), lax.axis_index("y"), lax.axis_index("c")
        me = 4 * x + 2 * y + c
        started = []
        for it, (_, per_peer) in enumerate(items):
            own = pltpu.make_async_copy(src[it].at[me] if per_peer else src[it], dst[it].at[me], local_sems.at[it])
            own.start()
            started.append(own)
        sends, recvs = [], []
        for k in range(1, N_DEV):
            px = 1 - x if k & 4 else x
            py = 1 - y if k & 2 else y
            pc = 1 - c if k & 1 else c
            peer = 4 * px + 2 * py + pc
            for it, (_, per_peer) in enumerate(items):
                s = src[it].at[peer] if per_peer else src[it]
                cp = pltpu.make_async_remote_copy(src_ref=s, dst_ref=dst[it].at[me], send_sem=send_sems.at[it, k - 1],
                                                  recv_sem=recv_sems.at[it, k - 1], device_id=(px, py, pc),
                                                  device_id_type=MESH)
                cp.start()
                sends.append(cp)
                recvs.append(pltpu.make_async_remote_copy(
                    src_ref=s, dst_ref=dst[it].at[peer], send_sem=send_sems.at[it, k - 1],
                    recv_sem=recv_sems.at[it, k - 1], device_id=(px, py, pc), device_id_type=MESH))
        for cp in recvs:
            cp.wait_recv()
        for cp in sends:
            cp.wait_send()
        for cp in started:
            cp.wait()

    any_spec = pl.BlockSpec(memory_space=pl.ANY)
    out_shape = []
    for a, per_peer in items:
        shp = a.shape if per_peer else (N_DEV,) + a.shape
        out_shape.append(jax.ShapeDtypeStruct(shp, a.dtype))
    return pl.pallas_call(
        body, name=name, in_specs=[any_spec] * n, out_specs=[any_spec] * n, out_shape=out_shape,
        scratch_shapes=[pltpu.SemaphoreType.DMA((n, N_DEV - 1)), pltpu.SemaphoreType.DMA((n, N_DEV - 1)),
                        pltpu.SemaphoreType.DMA((n,))],
    )(*[a for a, _ in items])


def _remote(src, dst, send_sem, recv_sem, device):
    return pltpu.make_async_remote_copy(src_ref=src, dst_ref=dst, send_sem=send_sem, recv_sem=recv_sem,
                                        device_id=device, device_id_type=pl.DeviceIdType.MESH)


def _mesh_place():
    x, y, c = lax.axis_index("x"), lax.axis_index("y"), lax.axis_index("c")
    other_chips = [(1 - x, y), (x, 1 - y), (1 - x, 1 - y)]
    return x, y, c, (x, y, 1 - c), other_chips


def _gather_all(items, name):
    n = len(items)

    def body(*refs):
        src, dst = refs[:n], refs[n:2 * n]
        send_sems, recv_sems, local_sems = refs[2 * n:]
        x, y, c, sibling, chips = _mesh_place()
        slot = lambda px, py, pc: 4 * px + 2 * py + pc
        me = slot(x, y, c)
        own = [pltpu.make_async_copy(src[it], dst[it].at[me], local_sems.at[it]) for it in range(n)]
        first = []
        for it in range(n):
            first.append(_remote(src[it], dst[it].at[me], send_sems.at[it, 0], recv_sems.at[it, 0], sibling))
            for j, chip in enumerate(chips):
                first.append(_remote(src[it], dst[it].at[me], send_sems.at[it, 1 + j], recv_sems.at[it, 1 + j],
                                     (*chip, c)))
        for cp in own + first:
            cp.start()
        passed = []
        for j, chip in enumerate(chips):
            blk = slot(*chip, c)
            for it in range(n):
                _remote(src[it], dst[it].at[blk], send_sems.at[it, 1 + j], recv_sems.at[it, 1 + j],
                        (*chip, c)).wait_recv()
                fwd = _remote(dst[it].at[blk], dst[it].at[blk], send_sems.at[it, 4 + j], recv_sems.at[it, 4 + j],
                              sibling)
                fwd.start()
                passed.append(fwd)
        for it in range(n):
            _remote(src[it], dst[it].at[slot(x, y, 1 - c)], send_sems.at[it, 0], recv_sems.at[it, 0],
                    sibling).wait_recv()
        for j, chip in enumerate(chips):
            for it in range(n):
                _remote(src[it], dst[it].at[slot(*chip, 1 - c)], send_sems.at[it, 4 + j], recv_sems.at[it, 4 + j],
                        sibling).wait_recv()
        for cp in first + passed:
            cp.wait_send()
        for cp in own:
            cp.wait()

    any_spec = pl.BlockSpec(memory_space=pl.ANY)
    return pl.pallas_call(
        body, name=name, in_specs=[any_spec] * n, out_specs=[any_spec] * n,
        out_shape=[jax.ShapeDtypeStruct((N_DEV,) + a.shape, a.dtype) for a in items],
        scratch_shapes=[pltpu.SemaphoreType.DMA((n, 7)), pltpu.SemaphoreType.DMA((n, 7)),
                        pltpu.SemaphoreType.DMA((n,))],
    )(*items)


N_CHIPS = N_DEV // 2


def _sibling_swap(g, name):
    def body(g_ref, got_ref, mine_ref, send_sems, recv_sems, local_sems):
        x, y, c, sibling, _ = _mesh_place()
        sends, keeps = [], []
        for chip in range(N_CHIPS):
            sends.append(_remote(g_ref.at[2 * chip + 1 - c], got_ref.at[chip], send_sems.at[chip],
                                 recv_sems.at[chip], sibling))
            keeps.append(pltpu.make_async_copy(g_ref.at[2 * chip + c], mine_ref.at[chip], local_sems.at[chip]))
        for cp in sends + keeps:
            cp.start()
        for cp in sends:
            cp.wait_recv()
        for cp in sends:
            cp.wait_send()
        for cp in keeps:
            cp.wait()

    any_spec = pl.BlockSpec(memory_space=pl.ANY)
    shape = jax.ShapeDtypeStruct((N_CHIPS,) + g.shape[1:], g.dtype)
    return pl.pallas_call(
        body, name=name, in_specs=[any_spec], out_specs=[any_spec, any_spec], out_shape=[shape, shape],
        scratch_shapes=[pltpu.SemaphoreType.DMA((N_CHIPS,))] * 3,
    )(g)


def _chip_scatter(s, name):
    def body(s_ref, out_ref, send_sems, recv_sems, local_sem):
        x, y, c, _, _ = _mesh_place()
        my_chip = 2 * x + y
        keep = pltpu.make_async_copy(s_ref.at[my_chip], out_ref.at[my_chip], local_sem)
        keep.start()
        sends, recvs = [], []
        for k in range(1, N_CHIPS):
            px = 1 - x if k & 2 else x
            py = 1 - y if k & 1 else y
            peer_chip = 2 * px + py
            cp = _remote(s_ref.at[peer_chip], out_ref.at[my_chip], send_sems.at[k - 1], recv_sems.at[k - 1],
                         (px, py, c))
            cp.start()
            sends.append(cp)
            recvs.append(_remote(s_ref.at[peer_chip], out_ref.at[peer_chip], send_sems.at[k - 1],
                                 recv_sems.at[k - 1], (px, py, c)))
        for cp in recvs:
            cp.wait_recv()
        for cp in sends:
            cp.wait_send()
        keep.wait()

    any_spec = pl.BlockSpec(memory_space=pl.ANY)
    return pl.pallas_call(
        body, name=name, in_specs=[any_spec], out_specs=any_spec,
        out_shape=jax.ShapeDtypeStruct(s.shape, s.dtype),
        scratch_shapes=[pltpu.SemaphoreType.DMA((N_CHIPS - 1,)), pltpu.SemaphoreType.DMA((N_CHIPS - 1,)),
                        pltpu.SemaphoreType.DMA],
    )(s)


PACK_TILES = 8


def _add_pairs(a, b):
    n, R, C = a.shape
    tr = R // PACK_TILES

    def body(a_ref, b_ref, o_ref):
        o_ref[...] = (a_ref[...].astype(F32) + b_ref[...].astype(F32)).astype(o_ref.dtype)

    spec = pl.BlockSpec((None, tr, C), lambda i, j: (i, j, 0))
    return pl.pallas_call(body, name="add_sibling_partials", grid=(n, PACK_TILES), in_specs=[spec, spec],
                          out_specs=spec, out_shape=jax.ShapeDtypeStruct(a.shape, a.dtype),
                          compiler_params=_params(2))(a, b)


def _gelu_tanh(y):
    k = math.sqrt(2.0 / math.pi)
    t = jnp.tanh(k * (y + 0.044715 * y * y * y))
    return 0.5 * y * (1.0 + t), t


def _local_step(x, mod, target, W, P):
    B, S, D = x.shape
    T = B * S
    TS = 512
    flat = lambda a: a.reshape(T, a.shape[-1])
    unflat = lambda a: a.reshape(B, S, a.shape[-1])
    mod_col = lambda i: (mod, D, i)

    def f_modnorm(xv, sc, sh, g):
        return (xv * _rms_scale(xv) * g) * (1.0 + sc) + sh

    (u1,) = _rowwise(f_modnorm, [(x, D, 0)], [mod_col(1), mod_col(0)], [P["g_mix"]],
                     [(D, BF16)], [], [], ts=TS, name="modnorm_mix")
    u1f = flat(u1)
    qkv = unflat(_matmul(u1f, W["w_qkv"], name="proj_qkv"))
    us = unflat(_matmul(u1f, W["w_us"], name="proj_ssm_in"))
    gates = unflat(_matmul(u1f, W["w_gates"], name="proj_gates"))

    o_att, lse = _attention_fwd(qkv, P["slopes"])
    y_att = unflat(_matmul(flat(o_att), W["w_proj_att"], name="proj_att"))

    bu = unflat(_matmul(flat(us), P["bb_big"], name="s5_bu"))
    xs = _scan_fwd(bu, P["a_row"])
    y_mm = unflat(_matmul(flat(xs), P["cc_big"], name="s5_readout"))

    def f_glu(ymm, usv, dsk, wg, bg):
        yv = ymm + dsk * usv
        ge, _ = _gelu_tanh(yv)
        pre = jnp.dot(ge.astype(BF16), wg, preferred_element_type=F32) + bg
        return yv, ge * _sigmoid(pre)

    y_s5, z = _rowwise(f_glu, [(y_mm, SSM_WIDTH, 0), (us, SSM_WIDTH, 0)], [], [P["d_skip"], W["w_glu"], P["b_glu"]],
                       [(SSM_WIDTH, F32), (SSM_WIDTH, BF16)], [], [], ts=TS, name="s5_glu")
    y_ssm = unflat(_matmul(flat(z), W["w_proj_ssm"], name="proj_ssm"))

    def f_merge(ga, gs, ya, ys, bga, bgs):
        return _sigmoid(ga + bga) * ya + _sigmoid(gs + bgs) * ys

    bga, bgs = P["b_gate"][:, :D], P["b_gate"][:, D:]
    (merged,) = _rowwise(f_merge, [(gates, D, 0), (gates, D, 1), (y_att, D, 0), (y_ssm, D, 0)], [], [bga, bgs],
                         [(D, BF16)], [], [], ts=TS, name="gate_merge")
    mix = unflat(_matmul(flat(merged), W["w_out"], name="proj_out"))

    def f_res_modnorm(xv, mx, gt, sc, sh, g):
        h = xv + gt * mx
        return h, (h * _rms_scale(h) * g) * (1.0 + sc) + sh

    h1, u2 = _rowwise(f_res_modnorm, [(x, D, 0), (mix, D, 0)], [mod_col(2), mod_col(4), mod_col(3)], [P["g_ffn"]],
                      [(D, F32), (D, BF16)], [], [], ts=TS, name="residual_modnorm_ffn")
    up = unflat(_matmul(flat(u2), W["w_up"], name="ffn_up"))
    ff = _conv_fwd(up, P["w_conv"], P["b_conv"])
    down = unflat(_matmul(flat(ff), W["w_down"], name="ffn_down"))

    def f_head(h1v, dn, tg, gt, g):
        h2 = h1v + gt * dn
        r = _rms_scale(h2)
        nh = h2 * r
        e = nh * g - tg
        dy = e * (1.0 / D)
        gy = dy * g
        dh = r * (gy - nh * jnp.mean(gy * nh, axis=-1, keepdims=True))
        return (dh, dh * gt, _col_sum(dh * dn), _col_sum(dy * nh), _col_sum(e * e) * (0.5 / D))

    dh2, d_down, d_gt2, d_g_final, loss_cols = _rowwise(
        f_head, [(h1, D, 0), (down, D, 0), (target, D, 0)], [mod_col(5)], [P["g_final"]],
        [(D, F32), (D, BF16)], [D], [(1, D), (1, D)], ts=TS, name="head_loss")

    d_downf = flat(d_down)
    d_ff = unflat(_matmul(d_downf, W["w_down"], tb=True, out_dtype=BF16, name="ffn_down_dx"))
    d_w_down = _matmul(flat(ff), d_downf, ta=True, out_dtype=BF16, name="ffn_down_dw")
    d_up, d_w_conv, d_b_conv = _conv_bwd(up, d_ff, P["w_conv"], P["b_conv"])
    d_upf = flat(d_up)
    d_u2 = unflat(_matmul(d_upf, W["w_up"], tb=True, name="ffn_up_dx"))
    d_w_up = _matmul(flat(u2), d_upf, ta=True, out_dtype=BF16, name="ffn_up_dw")

    def f_modnorm_bwd(du, h, dres, mx, sc, gt, g):
        r = _rms_scale(h)
        nh = h * r
        dn = du * (1.0 + sc)
        gy = dn * g
        dh = dres + r * (gy - nh * jnp.mean(gy * nh, axis=-1, keepdims=True))
        return (dh, dh * gt, _col_sum(du), _col_sum(du * nh * g), _col_sum(dh * mx), _col_sum(dn * nh))

    dh1, d_mix, d_sh2, d_sc2, d_gt1, d_g_ffn = _rowwise(
        f_modnorm_bwd, [(d_u2, D, 0), (h1, D, 0), (dh2, D, 0), (mix, D, 0)], [mod_col(4), mod_col(2)], [P["g_ffn"]],
        [(D, F32), (D, BF16)], [D, D, D], [(1, D)], ts=TS, name="modnorm_ffn_bwd")

    d_mixf = flat(d_mix)
    d_merged = unflat(_matmul(d_mixf, W["w_out"], tb=True, name="proj_out_dx"))
    d_w_out = _matmul(flat(merged), d_mixf, ta=True, out_dtype=BF16, name="proj_out_dw")

    def f_merge_bwd(dm, ga, gs, ya, ys, bga_, bgs_):
        sa, ss = _sigmoid(ga + bga_), _sigmoid(gs + bgs_)
        dga = dm * ya * sa * (1.0 - sa)
        dgs = dm * ys * ss * (1.0 - ss)
        return dm * sa, dm * ss, jnp.concatenate([dga, dgs], axis=1), _col_sum(dga), _col_sum(dgs)

    d_y_att, d_y_ssm, d_gates, d_bga, d_bgs = _rowwise(
        f_merge_bwd, [(d_merged, D, 0), (gates, D, 0), (gates, D, 1), (y_att, D, 0), (y_ssm, D, 0)], [], [bga, bgs],
        [(D, BF16), (D, BF16), (2 * D, BF16)], [], [(1, D), (1, D)], ts=TS, name="gate_merge_bwd")

    d_yaf, d_ysf = flat(d_y_att), flat(d_y_ssm)
    d_o_att = unflat(_matmul(d_yaf, W["w_proj_att"], tb=True, name="proj_att_dx"))
    d_w_proj_att = _matmul(flat(o_att), d_yaf, ta=True, out_dtype=BF16, name="proj_att_dw")
    d_z = unflat(_matmul(d_ysf, W["w_proj_ssm"], tb=True, name="proj_ssm_dx"))
    d_w_proj_ssm = _matmul(flat(z), d_ysf, ta=True, out_dtype=BF16, name="proj_ssm_dw")

    def f_glu_bwd(yv, dz, usv, dsk, wg, bg):
        ge, t = _gelu_tanh(yv)
        pre = jnp.dot(ge.astype(BF16), wg, preferred_element_type=F32) + bg
        sg = _sigmoid(pre)
        dpre = dz * ge * sg * (1.0 - sg)
        dge = dz * sg + lax.dot_general(dpre.astype(BF16), wg, _NT, preferred_element_type=F32)
        k = math.sqrt(2.0 / math.pi)
        dgelu = 0.5 * (1.0 + t) + 0.5 * yv * (1.0 - t * t) * k * (1.0 + 3.0 * 0.044715 * yv * yv)
        dy = dge * dgelu
        dwg = lax.dot_general(ge.astype(BF16), dpre.astype(BF16), _TN, preferred_element_type=F32)
        return dy, dy * dsk, dwg, _col_sum(dpre), _col_sum(dy * usv)

    d_y_s5, d_us_skip, d_w_glu, d_b_glu, d_d_skip = _rowwise(
        f_glu_bwd, [(y_s5, SSM_WIDTH, 0), (d_z, SSM_WIDTH, 0), (us, SSM_WIDTH, 0)], [],
        [P["d_skip"], W["w_glu"], P["b_glu"]],
        [(SSM_WIDTH, BF16), (SSM_WIDTH, F32)], [], [(SSM_WIDTH, SSM_WIDTH), (1, SSM_WIDTH), (1, SSM_WIDTH)],
        ts=TS, name="s5_glu_bwd")
    d_ysf2 = flat(d_y_s5)
    dxs = unflat(_matmul(d_ysf2, P["cc_big"], tb=True, name="s5_readout_dx"))
    d_cc = _matmul(flat(xs), d_ysf2, ta=True, name="s5_readout_dw")
    lam, g_ab = _scan_bwd(dxs, xs, P["a_row"])
    lam = flat(lam)
    d_us_mm = unflat(_matmul(lam, P["bb_big"], tb=True, name="s5_bu_dx"))
    d_bb = _matmul(flat(us), lam, ta=True, name="s5_bu_dw")

    d_qkv = _attention_bwd(qkv, o_att, d_o_att, lse, P["slopes"])

    def f_add(a, b_):
        return a + b_

    (d_us,) = _rowwise(f_add, [(d_us_mm, SSM_WIDTH, 0), (d_us_skip, SSM_WIDTH, 0)], [], [],
                       [(SSM_WIDTH, BF16)], [], [], ts=TS, name="s5_input_grad")
    d_qkvf = flat(d_qkv)
    d_usf = flat(d_us)
    d_gatesf = flat(d_gates)
    d_u1 = (_matmul(d_qkvf, W["w_qkv"], tb=True, name="proj_qkv_dx"),
            _matmul(d_usf, W["w_us"], tb=True, name="proj_ssm_in_dx"),
            _matmul(d_gatesf, W["w_gates"], tb=True, name="proj_gates_dx"))
    d_w_in = jnp.concatenate(
        [_unpair_qkv_columns(_matmul(u1f, d_qkvf, ta=True, out_dtype=BF16, name="proj_qkv_dw")),
         _matmul(u1f, d_usf, ta=True, out_dtype=BF16, name="proj_ssm_in_dw"),
         _matmul(u1f, d_gatesf, ta=True, out_dtype=BF16, name="proj_gates_dw")], axis=1)

    def f_modnorm_bwd_in(du0, du1, du2, h, dres, sc, g):
        du = du0 + du1 + du2
        r = _rms_scale(h)
        nh = h * r
        dn = du * (1.0 + sc)
        gy = dn * g
        dh = dres + r * (gy - nh * jnp.mean(gy * nh, axis=-1, keepdims=True))
        return (dh, _col_sum(du), _col_sum(du * nh * g), _col_sum(dn * nh))

    grad_x, d_sh1, d_sc1, d_g_mix = _rowwise(
        f_modnorm_bwd_in, [(unflat(d_u1[0]), D, 0), (unflat(d_u1[1]), D, 0), (unflat(d_u1[2]), D, 0), (x, D, 0),
                           (dh1, D, 0)], [mod_col(1)], [P["g_mix"]],
        [(D, F32)], [D, D], [(1, D)], ts=TS, name="modnorm_mix_bwd")

    d_mod = jnp.concatenate([d_sh1, d_sc1, d_gt1, d_sh2, d_sc2, d_gt2], axis=-1)
    big = dict(w_in=d_w_in, w_glu=d_w_glu, w_proj_att=d_w_proj_att, w_proj_ssm=d_w_proj_ssm, w_out=d_w_out,
               w_up_pairs=d_w_up, w_conv=d_w_conv, w_down=d_w_down)
    g_ab_re, g_ab_im = _deinterleave(g_ab)
    d_bb_re, d_bb_im = _deinterleave(d_bb)
    d_cc_re, d_cc_im = (t.T for t in _deinterleave(d_cc.T))
    small = dict(g_mix=d_g_mix, b_gate=jnp.concatenate([d_bga, d_bgs], axis=1), g_ab_re=g_ab_re, g_ab_im=g_ab_im,
                 d_bb_re=d_bb_re, d_bb_im=d_bb_im, d_cc_re=d_cc_re, d_cc_im=d_cc_im, d_skip=d_d_skip,
                 b_glu=d_b_glu, g_ffn=d_g_ffn, b_conv=d_b_conv, g_final=d_g_final, loss_cols=loss_cols)
    return grad_x, d_mod, big, small


def _block_diag_in(bb):
    t = bb.reshape(SSM_GROUPS, SSM_STATE, SSM_GROUP_CH)
    eye = jnp.eye(SSM_GROUPS, dtype=bb.dtype)
    return jnp.einsum("gnc,gh->gchn", t, eye).reshape(SSM_WIDTH, SSM_COLS)


def _block_diag_out(cm):
    eye = jnp.eye(SSM_GROUPS, dtype=cm.dtype)
    return jnp.einsum("gcn,gh->gnhc", cm, eye).reshape(SSM_COLS, SSM_WIDTH)


def _diag_blocks_in(m):
    t = m.reshape(SSM_GROUPS, SSM_GROUP_CH, SSM_GROUPS, SSM_STATE)
    idx = jnp.arange(SSM_GROUPS)
    return t[idx, :, idx, :].transpose(0, 2, 1).reshape(SSM_COLS, SSM_GROUP_CH)


def _diag_blocks_out(m):
    t = m.reshape(SSM_GROUPS, SSM_STATE, SSM_GROUPS, SSM_GROUP_CH)
    idx = jnp.arange(SSM_GROUPS)
    return t[idx, :, idx, :].transpose(0, 2, 1)


def _pair_columns(w):
    lead = w.shape[:-1]
    a = w[..., :D_FF].reshape(lead + (D_FF // CONV_COLS, 1, CONV_COLS))
    val = w[..., D_FF:].reshape(lead + (D_FF // CONV_COLS, 1, CONV_COLS))
    return jnp.concatenate([a, val], axis=-2).reshape(lead + (2 * D_FF,))


def _unpair_columns(w):
    lead = w.shape[:-1]
    t = w.reshape(lead + (D_FF // CONV_COLS, 2, CONV_COLS))
    return jnp.concatenate([t[..., 0, :].reshape(lead + (D_FF,)), t[..., 1, :].reshape(lead + (D_FF,))], axis=-1)


def _pair_qkv_columns(w):
    lead = w.shape[:-1]
    return w.reshape(lead + (3, N_HEADS // 2, LANES)).swapaxes(-3, -2).reshape(lead + (3 * ATT_WIDTH,))


def _unpair_qkv_columns(w):
    lead = w.shape[:-1]
    return w.reshape(lead + (N_HEADS // 2, 3, LANES)).swapaxes(-3, -2).reshape(lead + (3 * ATT_WIDTH,))


def _interleave(re, im):
    lead = re.shape[:-1]
    g = lambda a: a.reshape(lead + (SSM_COLS // SCAN_COLS, 1, SCAN_COLS))
    return jnp.concatenate([g(re), g(im)], axis=-2).reshape(lead + (2 * SSM_COLS,))


def _deinterleave(x):
    lead = x.shape[:-1]
    t = x.reshape(lead + (SSM_COLS // SCAN_COLS, 2, SCAN_COLS))
    return t[..., 0, :].reshape(lead + (SSM_COLS,)), t[..., 1, :].reshape(lead + (SSM_COLS,))


def _cols_to_slots(g):
    R = g.shape[0]
    return g.reshape(R, N_DEV, g.shape[1] // N_DEV).transpose(1, 0, 2)


def _slots_to_cols(g):
    return g.transpose(1, 0, 2).reshape(g.shape[1], N_DEV * g.shape[2])


SMALL_ORDER = ("b_ada", "g_mix", "b_gate", "a_re", "a_im", "log_dt", "b_re", "b_im", "c_re", "c_im", "d_skip",
               "b_glu", "g_ffn", "b_conv", "g_final")


def _pack(arrs):
    pieces, offs, row = [], [], 0
    for a in arrs:
        f = a.reshape(-1).astype(F32)
        n = f.shape[0]
        rows = -(-n // LANES)
        pieces.append(jnp.pad(f, (0, rows * LANES - n)))
        offs.append((row, n))
        row += rows
    return jnp.concatenate(pieces).reshape(row, LANES), offs


def _unpack(packed, offs, shapes):
    flat = packed.reshape(-1)
    return [flat[r * LANES:r * LANES + n].reshape(s) for (r, n), s in zip(offs, shapes)]


def _pack_slots(parts):
    n = parts[0].shape[0]
    flat = [p.reshape(n, -1) for p in parts]
    offs, total = [], 0
    for f in flat:
        offs.append((total, f.shape[1]))
        total += f.shape[1]
    unit = LANES * 16 * PACK_TILES
    padded = -(-total // unit) * unit
    buf = jnp.concatenate(flat + [jnp.zeros((n, padded - total), flat[0].dtype)], axis=1)
    return buf.reshape(n, padded // LANES, LANES), offs


def kernel(x, c, w_ada, b_ada, g_mix, w_in, b_gate, a_re, a_im, log_dt, b_re, b_im, c_re, c_im, d_skip, w_glu, b_glu, w_proj_att, w_proj_ssm, w_out, g_ffn, w_up, w_conv, b_conv, w_down, g_final, loss_target, m_w_ada, m_b_ada, m_g_mix, m_w_in, m_b_gate, m_a_re, m_a_im, m_log_dt, m_b_re, m_b_im, m_c_re, m_c_im, m_d_skip, m_w_glu, m_b_glu, m_w_proj_att, m_w_proj_ssm, m_w_out, m_g_ffn, m_w_up, m_w_conv, m_b_conv, m_w_down, m_g_final, v_w_ada, v_b_ada, v_g_mix, v_w_in, v_b_gate, v_a_re, v_a_im, v_log_dt, v_b_re, v_b_im, v_c_re, v_c_im, v_d_skip, v_w_glu, v_b_glu, v_w_proj_att, v_w_proj_ssm, v_w_out, v_g_ffn, v_w_up, v_w_conv, v_b_conv, v_w_down, v_g_final):
    args = dict(locals())
    B, S, D = x.shape
    me = 4 * lax.axis_index("x") + 2 * lax.axis_index("y") + lax.axis_index("c")
    bf = lambda w: w[0].astype(BF16)

    gathered = _gather_all([c, bf(w_in), bf(w_glu), bf(w_proj_att), bf(w_proj_ssm), bf(w_out), bf(w_up), w_conv[0],
                            bf(w_down)], name="gather_weights")
    c_all = gathered[0].reshape(N_DEV * B, D)
    w_in_full = _slots_to_cols(gathered[1])
    n_qkv = 3 * ATT_WIDTH
    W = dict(w_qkv=_pair_qkv_columns(w_in_full[:, :n_qkv]), w_us=w_in_full[:, n_qkv:n_qkv + SSM_WIDTH],
             w_gates=w_in_full[:, n_qkv + SSM_WIDTH:], w_glu=gathered[2].reshape(SSM_WIDTH, SSM_WIDTH),
             w_proj_att=_slots_to_cols(gathered[3]), w_proj_ssm=_slots_to_cols(gathered[4]),
             w_out=gathered[5].reshape(D, D), w_up=_pair_columns(_slots_to_cols(gathered[6])),
             w_down=gathered[8].reshape(D_FF, D))
    w_conv_full = _slots_to_cols(gathered[7])

    n_ada = w_ada.shape[2]
    b_ada_cols = lax.dynamic_slice(b_ada, (0, me * n_ada), (1, n_ada))
    mod_part = _ada_fwd(c_all, w_ada[0], b_ada_cols)
    (mod_slots,) = _exchange([(mod_part.reshape(N_DEV, B, n_ada), True)], name="scatter_modulation")
    mod = mod_slots.transpose(1, 0, 2).reshape(B, 1, 6 * D)

    ab_re, ab_im, f_re, f_im = _s5_params(a_re[0], a_im[0], log_dt[0].reshape(SSM_GROUPS, 1))
    col = lambda a: a.reshape(SSM_COLS, 1)
    b_re2, b_im2 = b_re[0].reshape(SSM_COLS, SSM_GROUP_CH), b_im[0].reshape(SSM_COLS, SSM_GROUP_CH)
    bb_re, bb_im = _s5_input_matrix(col(f_re), col(f_im), b_re2, b_im2)
    slopes = jnp.asarray([2.0 ** (-8.0 * (h + 1) / N_HEADS) for h in range(N_HEADS)], F32)
    P = dict(g_mix=g_mix, g_ffn=g_ffn, g_final=g_final.reshape(1, D), b_gate=b_gate, d_skip=d_skip, b_glu=b_glu,
             b_conv=b_conv, w_conv=w_conv_full, slopes=slopes,
             a_row=_interleave(ab_re.reshape(1, SSM_COLS), ab_im.reshape(1, SSM_COLS)),
             bb_big=_interleave(_block_diag_in(bb_re), _block_diag_in(bb_im)),
             cc_big=_interleave(_block_diag_out(c_re[0]).T, -_block_diag_out(c_im[0]).T).T)

    grad_x, d_mod, big, small = _local_step(x, mod, loss_target, W, P)

    small_list = [small["loss_cols"], small["g_mix"], small["b_gate"], small["g_ab_re"], small["g_ab_im"],
                  _diag_blocks_in(small["d_bb_re"]), _diag_blocks_in(small["d_bb_im"]),
                  _diag_blocks_out(small["d_cc_re"]), -_diag_blocks_out(small["d_cc_im"]),
                  small["g_ffn"], small["b_conv"], small["g_final"], small["d_skip"], small["b_glu"]]
    small_packed, small_offs = _pack(small_list)
    d_w_up_full = _unpair_columns(big["w_up_pairs"])
    sharded = ("w_in", "w_glu", "w_proj_att", "w_proj_ssm", "w_out", "w_up", "w_conv", "w_down")
    slots = [_cols_to_slots(big["w_in"]), big["w_glu"].astype(BF16).reshape(N_DEV, SSM_WIDTH // N_DEV, SSM_WIDTH),
             _cols_to_slots(big["w_proj_att"]), _cols_to_slots(big["w_proj_ssm"]),
             big["w_out"].reshape(N_DEV, D // N_DEV, D), _cols_to_slots(d_w_up_full),
             _cols_to_slots(big["w_conv"].astype(BF16)), big["w_down"].reshape(N_DEV, D_FF // N_DEV, D)]
    packed, pack_offs = _pack_slots(slots)
    from_sibling, mine = _sibling_swap(packed, name="swap_sibling_gradients")
    chip_sums = _chip_scatter(_add_pairs(mine, from_sibling), name="scatter_chip_gradients")
    chip_sums = chip_sums.reshape(N_CHIPS, -1)
    small_all, dmod_slots = _gather_all([small_packed, d_mod.reshape(B, 6 * D)], name="gather_small_gradients")

    out = {}

    def update(name, parts):
        w2 = args[name][0]
        g, dl, mn, vn = _adamw(w2, args["m_" + name][0], args["v_" + name][0], parts, name="adamw_" + name)
        for key, val in (("grad_", g), ("delta_", dl), ("new_m_", mn), ("new_v_", vn)):
            out[key + name] = val[None]

    for name, (off, cnt) in zip(sharded, pack_offs):
        update(name, chip_sums[:, off:off + cnt].reshape((N_CHIPS,) + args[name].shape[1:]))

    dmod_all = dmod_slots.reshape(N_DEV * B, 6 * D)
    dmod_cols = lax.dynamic_slice(dmod_all, (0, me * n_ada), (N_DEV * B, n_ada))
    d_w_ada, d_b_ada = _ada_bwd(c_all, dmod_all, dmod_cols)
    update("w_ada", d_w_ada[None])

    loss_row, loss_n = small_offs[0]
    small_sum, loss_vec = _sum_parts(small_all, (loss_row, loss_row + loss_n // LANES))
    shapes = [(1, D), (1, D), (1, 2 * D), (SSM_GROUPS, SSM_STATE), (SSM_GROUPS, SSM_STATE), (SSM_COLS, SSM_GROUP_CH),
              (SSM_COLS, SSM_GROUP_CH), (1, SSM_GROUPS, SSM_GROUP_CH, SSM_STATE),
              (1, SSM_GROUPS, SSM_GROUP_CH, SSM_STATE), (1, D), (1, D_FF), (D,), (1, SSM_WIDTH), (1, SSM_WIDTH)]
    (_, s_g_mix, s_b_gate, s_ab_re, s_ab_im, s_bb_re, s_bb_im, s_c_re, s_c_im, s_g_ffn, s_b_conv, s_g_final,
     s_d_skip, s_b_glu) = _unpack(small_sum, small_offs, shapes)
    d_b_re2, d_b_im2, d_f_re, d_f_im = _s5_input_matrix_bwd(col(f_re), col(f_im), b_re2, b_im2, s_bb_re, s_bb_im)
    d_a_re, d_a_im, d_log_dt = _s5_params_bwd(a_re[0], a_im[0], log_dt[0].reshape(SSM_GROUPS, 1), s_ab_re, s_ab_im,
                                              d_f_re.reshape(SSM_GROUPS, SSM_STATE),
                                              d_f_im.reshape(SSM_GROUPS, SSM_STATE))
    grads_small = dict(b_ada=d_b_ada, g_mix=s_g_mix, b_gate=s_b_gate, a_re=d_a_re[None], a_im=d_a_im[None],
                       log_dt=d_log_dt.reshape(1, SSM_GROUPS), b_re=d_b_re2.reshape(b_re.shape),
                       b_im=d_b_im2.reshape(b_im.shape), c_re=s_c_re, c_im=s_c_im, d_skip=s_d_skip, b_glu=s_b_glu,
                       g_ffn=s_g_ffn, b_conv=s_b_conv, g_final=s_g_final)
    w_pack, offs = _pack([args[n] for n in SMALL_ORDER])
    m_pack, _ = _pack([args["m_" + n] for n in SMALL_ORDER])
    v_pack, _ = _pack([args["v_" + n] for n in SMALL_ORDER])
    g_pack, _ = _pack([grads_small[n] for n in SMALL_ORDER])
    res = _adamw(w_pack, m_pack, v_pack, g_pack[None], name="adamw_small")
    shapes_small = [args[n].shape for n in SMALL_ORDER]
    for key, packed in zip(("grad_", "delta_", "new_m_", "new_v_"), res):
        for n, val in zip(SMALL_ORDER, _unpack(packed, offs, shapes_small)):
            out[key + n] = val

    order = ["w_ada", "b_ada", "g_mix", "w_in", "b_gate", "a_re", "a_im", "log_dt", "b_re", "b_im", "c_re", "c_im",
             "d_skip", "w_glu", "b_glu", "w_proj_att", "w_proj_ssm", "w_out", "g_ffn", "w_up", "w_conv", "b_conv",
             "w_down", "g_final"]
    loss = loss_vec[0, 0]
    return (loss, grad_x, *[out[k + n] for k in ("grad_", "delta_", "new_m_", "new_v_") for n in order])
```

```python
import math

import jax
import jax.numpy as jnp
from jax import lax
from jax.experimental import pallas as pl
from jax.experimental.pallas import tpu as pltpu

F32 = jnp.float32
BF16 = jnp.bfloat16

N_DEV = 8
D_MODEL = 1024
N_HEADS = 8
HEAD_DIM = 64
ATT_WIDTH = N_HEADS * HEAD_DIM
DILATIONS = (1, 4, 16)
WIN = 128
SSM_GROUPS = 16
SSM_GROUP_CH = 16
SSM_WIDTH = SSM_GROUPS * SSM_GROUP_CH
SSM_STATE = 64
SSM_COLS = SSM_GROUPS * SSM_STATE
D_FF = 2048
EPS = 1e-6
NEG_INF = -1e30
ADAM_LR, ADAM_B1, ADAM_B2, ADAM_EPS, ADAM_WD, ADAM_STEP = 0.001, 0.9, 0.999, 1e-08, 0.01, 10

V7X_VMEM_LIMIT = 56 * 1024 * 1024
LANES = 128


def _params(n_grid):
    return pltpu.CompilerParams(dimension_semantics=("arbitrary",) * n_grid,
                                vmem_limit_bytes=V7X_VMEM_LIMIT)


def _tile(n, pref):
    if n <= pref:
        return n
    t = (pref // LANES) * LANES
    while t > 0:
        if n % t == 0:
            return t
        t -= LANES
    return n


def _matmul(a, b, *, ta=False, tb=False, out_dtype=F32, name):
    if ta:
        K, M = a.shape
    else:
        M, K = a.shape
    if tb:
        N, K2 = b.shape
    else:
        K2, N = b.shape
    assert K == K2, (a.shape, b.shape)
    if ta:
        tm, tn, tk = _tile(M, 1024), _tile(N, 2048), _tile(K, 512)
    else:
        tm, tk = _tile(M, 512), _tile(K, 4096)
        tn = _tile(N, 2048 if K <= 2048 else 1024)
    nk = K // tk
    dn = (((0,) if ta else (1,), (1,) if tb else (0,)), ((), ()))

    def body(a_ref, b_ref, o_ref, acc_ref):
        k = pl.program_id(2)
        part = lax.dot_general(a_ref[...].astype(BF16), b_ref[...].astype(BF16), dn, preferred_element_type=F32)
        if nk == 1:
            o_ref[...] = part.astype(o_ref.dtype)
            return

        @pl.when(k == 0)
        def _():
            acc_ref[...] = jnp.zeros_like(acc_ref)

        acc_ref[...] += part

        @pl.when(k == nk - 1)
        def _():
            o_ref[...] = acc_ref[...].astype(o_ref.dtype)

    a_spec = (pl.BlockSpec((tk, tm), lambda j, i, k: (k, i)) if ta
              else pl.BlockSpec((tm, tk), lambda j, i, k: (i, k)))
    b_spec = (pl.BlockSpec((tn, tk), lambda j, i, k: (j, k)) if tb
              else pl.BlockSpec((tk, tn), lambda j, i, k: (k, j)))
    return pl.pallas_call(
        body, name=name, grid=(N // tn, M // tm, nk),
        in_specs=[a_spec, b_spec],
        out_specs=pl.BlockSpec((tm, tn), lambda j, i, k: (i, j)),
        out_shape=jax.ShapeDtypeStruct((M, N), out_dtype),
        scratch_shapes=[pltpu.VMEM((tm, tn) if nk > 1 else (8, LANES), F32)],
        compiler_params=_params(3),
    )(a, b)


def _rowwise(fn, rows, bvecs, consts, out_rows, out_b, out_g, *, ts, name):
    B, S = rows[0][0].shape[:2]
    nin = len(rows) + len(bvecs) + len(consts)
    nr, nb, ng = len(out_rows), len(out_b), len(out_g)

    def body(*refs):
        b = pl.program_id(0)
        s = pl.program_id(1)
        vals = [r[...] for r in refs[:nin]]
        vals[:len(rows)] = [v.astype(F32) for v in vals[:len(rows)]]
        outs = fn(*vals)
        if not isinstance(outs, (tuple, list)):
            outs = (outs,)
        orefs = refs[nin:]
        for i in range(nr):
            orefs[i][...] = outs[i].astype(orefs[i].dtype)
        for i in range(nb):
            ref = orefs[nr + i]

            @pl.when(s == 0)
            def _(ref=ref):
                ref[...] = jnp.zeros_like(ref)

            ref[...] += outs[nr + i]
        for i in range(ng):
            ref = orefs[nr + nb + i]

            @pl.when((s == 0) & (b == 0))
            def _(ref=ref):
                ref[...] = jnp.zeros_like(ref)

            ref[...] += outs[nr + nb + i]

    in_specs = ([pl.BlockSpec((None, ts, cb), lambda b, s, ci=ci: (b, s, ci)) for (_, cb, ci) in rows]
                + [pl.BlockSpec((None, 1, cb), lambda b, s, ci=ci: (b, 0, ci)) for (_, cb, ci) in bvecs]
                + [pl.BlockSpec(a.shape, lambda b, s: (0, 0)) for a in consts])
    out_shape = ([jax.ShapeDtypeStruct((B, S, c), dt) for (c, dt) in out_rows]
                 + [jax.ShapeDtypeStruct((B, 1, c), F32) for c in out_b]
                 + [jax.ShapeDtypeStruct(rc, F32) for rc in out_g])
    out_specs = ([pl.BlockSpec((None, ts, c), lambda b, s: (b, s, 0)) for (c, _) in out_rows]
                 + [pl.BlockSpec((None, 1, c), lambda b, s: (b, 0, 0)) for c in out_b]
                 + [pl.BlockSpec(rc, lambda b, s: (0, 0)) for rc in out_g])
    args = [a for (a, _, _) in rows] + [a for (a, _, _) in bvecs] + list(consts)
    return pl.pallas_call(
        body, name=name, grid=(B, S // ts), in_specs=in_specs, out_specs=out_specs,
        out_shape=out_shape, compiler_params=_params(2),
    )(*args)


def _col_sum(v):
    return jnp.sum(v, axis=0, keepdims=True)


def _rms_scale(h):
    return lax.rsqrt(jnp.mean(h * h, axis=-1, keepdims=True) + EPS)


def _sigmoid(v):
    return 1.0 / (1.0 + jnp.exp(-v))


ATT_SCALE = HEAD_DIM ** -0.5
COPY_ROWS = 256
_NT = (((1,), (1,)), ((), ()))
_TN = (((0,), (0,)), ((), ()))


def _row_chunks(d, seq):
    sub = seq // d
    out = []
    for r in range(d):
        for c0 in range(0, sub, COPY_ROWS):
            n = min(COPY_ROWS, sub - c0)
            out.append((pl.ds(r + c0 * d, n, stride=d), r * sub + c0, n))
    return out


ATT_UNROLL = 2
KEYS = 2 * WIN


def _zero_once(refs):
    @pl.when((pl.program_id(0) == 0) & (pl.program_id(1) == 0))
    def _():
        for r in refs:
            r[...] = jnp.zeros_like(r)


def _pair_bias(bias_ref, slopes_ref, hp, d, key_major):
    shape = (KEYS, WIN) if key_major else (WIN, KEYS)
    qi = lax.broadcasted_iota(jnp.int32, shape, 1 if key_major else 0)
    kj = lax.broadcasted_iota(jnp.int32, shape, 0 if key_major else 1)
    dist = WIN + qi - kj
    valid = (dist >= 0) & (dist <= WIN)
    distf = dist.astype(F32)
    for h in range(2):
        slope_d = slopes_ref[2 * hp + h] * float(d)
        with_prev = jnp.where(valid, -(slope_d * distf), NEG_INF)
        no_prev = jnp.where(kj >= WIN, with_prev, NEG_INF)
        span = slice(h * KEYS, (h + 1) * KEYS)
        if key_major:
            bias_ref[1, span, :] = with_prev
            bias_ref[0, span, :] = no_prev
        else:
            bias_ref[1, :, span] = with_prev
            bias_ref[0, :, span] = no_prev


def _stack_heads(v):
    first = lax.broadcasted_iota(jnp.int32, v.shape, 1) < HEAD_DIM
    zero = jnp.zeros_like(v)
    return jnp.concatenate([jnp.where(first, v, zero), jnp.where(first, zero, v)], axis=0)


def _per_head(c0, c1, n):
    return jnp.where(lax.broadcasted_iota(jnp.int32, (n, LANES), 1) < HEAD_DIM, c0, c1)


def _qkv_spec(seq, j):
    return pl.BlockSpec((None, seq, LANES), lambda b, hp: (b, 0, 3 * hp + j))


def _attention_fwd(qkv, slopes):
    B, S, _ = qkv.shape
    n_blk = S // WIN
    n_pair = N_HEADS // 2

    def body(slopes_ref, q_ref, k_ref, v_ref, o_ref, lse_ref, qp, kp, vp, bias, acc, mx, sm, acc_n, mx_n, sm_n):
        hp = pl.program_id(1)
        _zero_once((kp, vp))
        for p, d in enumerate(DILATIONS):
            nb = n_blk // d
            chunks = _row_chunks(d, S)
            for src, dst, n in chunks:
                qp[dst:dst + n, :] = (q_ref[src, :] * ATT_SCALE).astype(BF16)
                kp[WIN + dst:WIN + dst + n, :] = k_ref[src, :].astype(BF16)
                vp[WIN + dst:WIN + dst + n, :] = v_ref[src, :].astype(BF16)
            _pair_bias(bias, slopes_ref, hp, d, key_major=False)
            acc_t, mx_t, sm_t = (acc_n, mx_n, sm_n) if d == 1 else (acc, mx, sm)

            def block(i, carry, p=p, nb=nb, acc_t=acc_t, mx_t=mx_t, sm_t=sm_t):
                cur = pl.ds(pl.multiple_of(i * WIN, WIN), WIN)
                keys = pl.ds(pl.multiple_of(i * WIN, WIN), KEYS)
                flag = ((i % nb) > 0).astype(jnp.int32)
                s = lax.dot_general(qp[cur, :], _stack_heads(kp[keys, :]), _NT, preferred_element_type=F32)
                s = s + bias[flag]
                es, ms, ls = [], [], []
                for h in range(2):
                    sh = s[:, h * KEYS:(h + 1) * KEYS]
                    m = jnp.max(jnp.maximum(sh[:, :WIN], sh[:, WIN:]), axis=1, keepdims=True)
                    e = jnp.exp(sh - m)
                    es.append(e.astype(BF16))
                    ms.append(m)
                    ls.append(jnp.sum(e[:, :WIN] + e[:, WIN:], axis=1, keepdims=True))
                acc_t[p, cur, :] = jnp.dot(jnp.concatenate(es, axis=1), _stack_heads(vp[keys, :]),
                                           preferred_element_type=F32)
                mx_t[p, cur, :] = _per_head(ms[0], ms[1], WIN)
                sm_t[p, cur, :] = _per_head(ls[0], ls[1], WIN)
                return carry

            lax.fori_loop(0, n_blk, block, 0, unroll=ATT_UNROLL)
            if d > 1:
                for src, dst, n in chunks:
                    acc_n[p, src, :] = acc[p, dst:dst + n, :]
                    mx_n[p, src, :] = mx[p, dst:dst + n, :]
                    sm_n[p, src, :] = sm[p, dst:dst + n, :]

        chunk = 256

        def merge(i, carry):
            rows = pl.ds(pl.multiple_of(i * chunk, chunk), chunk)
            ms = [mx_n[p, rows, :] for p in range(3)]
            m = jnp.maximum(jnp.maximum(ms[0], ms[1]), ms[2])
            ws = [jnp.exp(mp - m) for mp in ms]
            l = ws[0] * sm_n[0, rows, :] + ws[1] * sm_n[1, rows, :] + ws[2] * sm_n[2, rows, :]
            o = (ws[0] * acc_n[0, rows, :] + ws[1] * acc_n[1, rows, :] + ws[2] * acc_n[2, rows, :]) / l
            o_ref[rows, :] = o.astype(o_ref.dtype)
            lse = m + jnp.log(l)
            for h in range(2):
                lse_ref[rows, h:h + 1] = lse[:, h * HEAD_DIM:h * HEAD_DIM + 1]
            return carry

        lax.fori_loop(0, S // chunk, merge, 0)

    return pl.pallas_call(
        body, name="attention_fwd", grid=(B, n_pair),
        in_specs=[pl.BlockSpec(memory_space=pltpu.SMEM), _qkv_spec(S, 0), _qkv_spec(S, 1), _qkv_spec(S, 2)],
        out_specs=[pl.BlockSpec((None, S, LANES), lambda b, hp: (b, 0, hp)),
                   pl.BlockSpec((None, None, S, 2), lambda b, hp: (b, hp, 0, 0))],
        out_shape=[jax.ShapeDtypeStruct((B, S, ATT_WIDTH), BF16),
                   jax.ShapeDtypeStruct((B, n_pair, S, 2), F32)],
        scratch_shapes=[pltpu.VMEM((S, LANES), BF16), pltpu.VMEM((S + WIN, LANES), BF16),
                        pltpu.VMEM((S + WIN, LANES), BF16), pltpu.VMEM((2, WIN, 2 * KEYS), F32)]
        + [pltpu.VMEM((3, S, LANES), F32)] * 6,
        compiler_params=_params(2),
    )(slopes, qkv, qkv, qkv)


def _attention_bwd(qkv, o, do, lse, slopes):
    B, S, _ = qkv.shape
    n_blk = S // WIN
    n_pair = N_HEADS // 2

    def body(slopes_ref, q_ref, k_ref, v_ref, o_ref, do_ref, lse_ref, dx_ref,
             qp, dop, kp, vp, aux, auxp, aux_t, bias_t, dqp, dkp, dvp, dq_n, dk_n, dv_n):
        hp = pl.program_id(1)
        aux[...] = jnp.zeros_like(aux)
        for c0 in range(0, S, COPY_ROWS):
            rows = slice(c0, c0 + COPY_ROWS)
            prod = do_ref[rows, :] * o_ref[rows, :].astype(F32)
            for h in range(2):
                aux[rows, 2 * h:2 * h + 1] = lse_ref[rows, h:h + 1]
                aux[rows, 2 * h + 1:2 * h + 2] = jnp.sum(prod[:, h * HEAD_DIM:(h + 1) * HEAD_DIM], axis=1,
                                                         keepdims=True)
        dq_n[...] = jnp.zeros_like(dq_n)
        dk_n[...] = jnp.zeros_like(dk_n)
        dv_n[...] = jnp.zeros_like(dv_n)
        _zero_once((kp, vp))
        for p, d in enumerate(DILATIONS):
            nb = n_blk // d
            chunks = _row_chunks(d, S)
            for src, dst, n in chunks:
                auxp[dst:dst + n, :] = aux[src, :]
                qp[dst:dst + n, :] = (q_ref[src, :] * ATT_SCALE).astype(BF16)
                dop[dst:dst + n, :] = do_ref[src, :].astype(BF16)
                kp[WIN + dst:WIN + dst + n, :] = k_ref[src, :].astype(BF16)
                vp[WIN + dst:WIN + dst + n, :] = v_ref[src, :].astype(BF16)
            for i in range(n_blk):
                aux_t[i] = auxp[i * WIN:(i + 1) * WIN, :].T[0:8, :]
            _pair_bias(bias_t, slopes_ref, hp, d, key_major=True)
            dkp[...] = jnp.zeros_like(dkp)
            dvp[...] = jnp.zeros_like(dvp)

            def block(i, carry, nb=nb):
                cur = pl.ds(pl.multiple_of(i * WIN, WIN), WIN)
                keys = pl.ds(pl.multiple_of(i * WIN, WIN), KEYS)
                flag = ((i % nb) > 0).astype(jnp.int32)
                q2, do2 = qp[cur, :], dop[cur, :]
                kc = _stack_heads(kp[keys, :])
                s_t = lax.dot_general(kc, q2, _NT, preferred_element_type=F32) + bias_t[flag]
                dp_t = lax.dot_general(_stack_heads(vp[keys, :]), do2, _NT, preferred_element_type=F32)
                ps, dss = [], []
                for h in range(2):
                    span = slice(h * KEYS, (h + 1) * KEYS)
                    p_t = jnp.exp(s_t[span, :] - aux_t[i, 2 * h:2 * h + 1, :])
                    ds_t = p_t * (dp_t[span, :] - aux_t[i, 2 * h + 1:2 * h + 2, :])
                    ps.append(p_t.astype(BF16))
                    dss.append(ds_t.astype(BF16))
                dvp[keys, :] += jnp.dot(jnp.concatenate(ps, axis=1), _stack_heads(do2), preferred_element_type=F32)
                dkp[keys, :] += jnp.dot(jnp.concatenate(dss, axis=1), _stack_heads(q2), preferred_element_type=F32)
                dqp[cur, :] = lax.dot_general(jnp.concatenate(dss, axis=0), kc, _TN, preferred_element_type=F32)
                return carry

            lax.fori_loop(0, n_blk, block, 0)
            for src, dst, n in chunks:
                dq_n[src, :] += dqp[dst:dst + n, :]
                dk_n[src, :] += dkp[WIN + dst:WIN + dst + n, :]
                dv_n[src, :] += dvp[WIN + dst:WIN + dst + n, :]
        for c0 in range(0, S, COPY_ROWS):
            rows = slice(c0, c0 + COPY_ROWS)
            dx_ref[rows, 0:LANES] = (dq_n[rows, :] * ATT_SCALE).astype(dx_ref.dtype)
            dx_ref[rows, LANES:2 * LANES] = dk_n[rows, :].astype(dx_ref.dtype)
            dx_ref[rows, 2 * LANES:3 * LANES] = dv_n[rows, :].astype(dx_ref.dtype)

    pair = lambda width: pl.BlockSpec((None, S, width), lambda b, hp: (b, 0, hp))
    vm = lambda shape, dt: pltpu.VMEM(shape, dt)
    return pl.pallas_call(
        body, name="attention_bwd", grid=(B, n_pair),
        in_specs=[pl.BlockSpec(memory_space=pltpu.SMEM), _qkv_spec(S, 0), _qkv_spec(S, 1), _qkv_spec(S, 2),
                  pair(LANES), pair(LANES), pl.BlockSpec((None, None, S, 2), lambda b, hp: (b, hp, 0, 0))],
        out_specs=pair(3 * LANES),
        out_shape=jax.ShapeDtypeStruct((B, S, 3 * ATT_WIDTH), BF16),
        scratch_shapes=[vm((S, LANES), BF16), vm((S, LANES), BF16),
                        vm((S + WIN, LANES), BF16), vm((S + WIN, LANES), BF16),
                        vm((S, LANES), F32), vm((S, LANES), F32), vm((n_blk, 8, WIN), F32),
                        vm((2, 2 * KEYS, WIN), F32),
                        vm((S, LANES), F32), vm((S + WIN, LANES), F32), vm((S + WIN, LANES), F32),
                        vm((S, LANES), F32), vm((S, LANES), F32), vm((S, LANES), F32)],
        compiler_params=_params(2),
    )(slopes, qkv, qkv, qkv, o, do, lse)


SCAN_COLS = 256
SCAN_ROWS = 8


def _rows_to_tile(rows):
    rid = lax.broadcasted_iota(jnp.int32, (SCAN_ROWS, rows[0].shape[1]), 0)
    tile = jnp.broadcast_to(rows[0], rid.shape)
    for k in range(1, SCAN_ROWS):
        tile = jnp.where(rid == k, rows[k], tile)
    return tile


def _scan_fwd(bu, a_row):
    B, S, _ = bu.shape
    nc = SSM_COLS // SCAN_COLS
    nt = S // SCAN_ROWS
    RE, IM = slice(0, SCAN_COLS), slice(SCAN_COLS, 2 * SCAN_COLS)

    def body(bu_ref, a_ref, xs_ref):
        ar, ai = a_ref[:, RE], a_ref[:, IM]

        def tile(i, carry):
            xr, xi = carry
            out_r, out_i = [], []
            for k in range(SCAN_ROWS):
                xr, xi = (ar * xr - ai * xi + bu_ref[i, k:k + 1, RE], ar * xi + ai * xr + bu_ref[i, k:k + 1, IM])
                out_r.append(xr)
                out_i.append(xi)
            xs_ref[i, :, RE] = _rows_to_tile(out_r)
            xs_ref[i, :, IM] = _rows_to_tile(out_i)
            return xr, xi

        zero = jnp.zeros((1, SCAN_COLS), F32)
        lax.fori_loop(0, nt, tile, (zero, zero))

    col = pl.BlockSpec((None, nt, SCAN_ROWS, 2 * SCAN_COLS), lambda b, j: (b, 0, 0, j))
    xs = pl.pallas_call(
        body, name="s5_scan_fwd", grid=(B, nc),
        in_specs=[col, pl.BlockSpec((1, 2 * SCAN_COLS), lambda b, j: (0, j))],
        out_specs=col,
        out_shape=jax.ShapeDtypeStruct((B, nt, SCAN_ROWS, 2 * SSM_COLS), F32),
        compiler_params=_params(2),
    )(bu.reshape(B, nt, SCAN_ROWS, 2 * SSM_COLS), a_row)
    return xs.reshape(B, S, 2 * SSM_COLS)


def _scan_bwd(dxs, xs, a_row):
    B, S, _ = dxs.shape
    nc = SSM_COLS // SCAN_COLS
    nt = S // SCAN_ROWS
    RE, IM = slice(0, SCAN_COLS), slice(SCAN_COLS, 2 * SCAN_COLS)

    def body(d_ref, x_ref, a_ref, lam_ref, ga_ref):
        b = pl.program_id(1)
        ar, ai = a_ref[:, RE], a_ref[:, IM]
        rid = lax.broadcasted_iota(jnp.int32, (SCAN_ROWS, SCAN_COLS), 0)

        @pl.when(b == 0)
        def _():
            ga_ref[...] = jnp.zeros_like(ga_ref)

        def tile(j, carry):
            lr, li, accr, acci = carry
            i = nt - 1 - j
            out_r, out_i = [None] * SCAN_ROWS, [None] * SCAN_ROWS
            for k in reversed(range(SCAN_ROWS)):
                lr, li = (d_ref[i, k:k + 1, RE] + ar * lr + ai * li, d_ref[i, k:k + 1, IM] + ar * li - ai * lr)
                out_r[k] = lr
                out_i[k] = li
            lam_r = _rows_to_tile(out_r)
            lam_i = _rows_to_tile(out_i)
            lam_ref[i, :, RE] = lam_r
            lam_ref[i, :, IM] = lam_i
            ip = jnp.maximum(i - 1, 0)
            keep = (i > 0).astype(F32)
            last = slice(SCAN_ROWS - 1, SCAN_ROWS)
            xpr = jnp.where(rid == 0, x_ref[ip, last, RE] * keep, pltpu.roll(x_ref[i, :, RE], 1, 0))
            xpi = jnp.where(rid == 0, x_ref[ip, last, IM] * keep, pltpu.roll(x_ref[i, :, IM], 1, 0))
            accr = accr + lam_r * xpr + lam_i * xpi
            acci = acci + lam_i * xpr - lam_r * xpi
            return lr, li, accr, acci

        z1 = jnp.zeros((1, SCAN_COLS), F32)
        z8 = jnp.zeros((SCAN_ROWS, SCAN_COLS), F32)
        _, _, accr, acci = lax.fori_loop(0, nt, tile, (z1, z1, z8, z8))
        ga_ref[:, RE] += _col_sum(accr)
        ga_ref[:, IM] += _col_sum(acci)

    col = pl.BlockSpec((None, nt, SCAN_ROWS, 2 * SCAN_COLS), lambda j, b: (b, 0, 0, j))
    par = pl.BlockSpec((1, 2 * SCAN_COLS), lambda j, b: (0, j))
    t4 = lambda a: a.reshape(B, nt, SCAN_ROWS, 2 * SSM_COLS)
    lam, g_a = pl.pallas_call(
        body, name="s5_scan_bwd", grid=(nc, B),
        in_specs=[col, col, par], out_specs=[col, par],
        out_shape=[jax.ShapeDtypeStruct((B, nt, SCAN_ROWS, 2 * SSM_COLS), F32),
                   jax.ShapeDtypeStruct((1, 2 * SSM_COLS), F32)],
        compiler_params=_params(2),
    )(t4(dxs), t4(xs), a_row)
    return lam.reshape(B, S, 2 * SSM_COLS), g_a


def _s5_discretise(lr, li, log_dt):
    dt = jnp.exp(log_dt)
    mag = jnp.exp(lr * dt)
    ang = li * dt
    ab_re, ab_im = mag * jnp.cos(ang), mag * jnp.sin(ang)
    nr, ni = ab_re - 1.0, ab_im
    den = lr * lr + li * li
    f_re = (nr * lr + ni * li) / den
    f_im = (ni * lr - nr * li) / den
    return dt, ab_re, ab_im, nr, ni, den, f_re, f_im


def _s5_params(a_re, a_im, log_dt):
    def body(lr_ref, li_ref, ld_ref, abr, abi, fr, fi):
        _, ab_re, ab_im, _, _, _, f_re, f_im = _s5_discretise(lr_ref[...], li_ref[...], ld_ref[...])
        abr[...] = ab_re
        abi[...] = ab_im
        fr[...] = f_re
        fi[...] = f_im

    return pl.pallas_call(body, name="s5_params",
                          out_shape=[jax.ShapeDtypeStruct(a_re.shape, F32)] * 4)(a_re, a_im, log_dt)


def _s5_input_matrix(f_re, f_im, b_re, b_im):
    def body(fr, fi, br, bi, o_re, o_im):
        o_re[...] = fr[...] * br[...] - fi[...] * bi[...]
        o_im[...] = fr[...] * bi[...] + fi[...] * br[...]

    return pl.pallas_call(body, name="s5_input_matrix",
                          out_shape=[jax.ShapeDtypeStruct(b_re.shape, F32)] * 2)(f_re, f_im, b_re, b_im)


def _s5_input_matrix_bwd(f_re, f_im, b_re, b_im, g_re, g_im):
    def body(fr, fi, br, bi, gr, gi, dbr, dbi, dfr, dfi):
        dbr[...] = fr[...] * gr[...] + fi[...] * gi[...]
        dbi[...] = fr[...] * gi[...] - fi[...] * gr[...]
        dfr[...] = jnp.sum(br[...] * gr[...] + bi[...] * gi[...], axis=1, keepdims=True)
        dfi[...] = jnp.sum(br[...] * gi[...] - bi[...] * gr[...], axis=1, keepdims=True)

    return pl.pallas_call(
        body, name="s5_input_matrix_bwd",
        out_shape=[jax.ShapeDtypeStruct(b_re.shape, F32)] * 2 + [jax.ShapeDtypeStruct(f_re.shape, F32)] * 2,
    )(f_re, f_im, b_re, b_im, g_re, g_im)


def _s5_params_bwd(a_re, a_im, log_dt, g_ab_re, g_ab_im, d_f_re, d_f_im):
    def body(lr_ref, li_ref, ld_ref, gar, gai, dfr, dfi, o_lr, o_li, o_ld):
        lr, li = lr_ref[...], li_ref[...]
        dt, ab_re, ab_im, nr, ni, den, f_re, f_im = _s5_discretise(lr, li, ld_ref[...])
        d_fr, d_fi = dfr[...], dfi[...]
        d_nr = (d_fr * lr - d_fi * li) / den
        d_ni = (d_fr * li + d_fi * lr) / den
        common = (d_fr * f_re + d_fi * f_im) * 2.0 / den
        d_lr = (d_fr * nr + d_fi * ni) / den - common * lr
        d_li = (d_fr * ni - d_fi * nr) / den - common * li
        d_abr = gar[...] + d_nr
        d_abi = gai[...] + d_ni
        d_mag_mag = d_abr * ab_re + d_abi * ab_im
        d_ang = d_abi * ab_re - d_abr * ab_im
        o_lr[...] = d_lr + d_mag_mag * dt
        o_li[...] = d_li + d_ang * dt
        o_ld[...] = jnp.sum(d_mag_mag * lr + d_ang * li, axis=1, keepdims=True) * dt

    return pl.pallas_call(
        body, name="s5_params_bwd",
        out_shape=[jax.ShapeDtypeStruct(a_re.shape, F32)] * 2 + [jax.ShapeDtypeStruct(log_dt.shape, F32)],
    )(a_re, a_im, log_dt, g_ab_re, g_ab_im, d_f_re, d_f_im)


CONV_COLS = 256


def _shift_down(v, j, row):
    return jnp.where(row >= j, pltpu.roll(v, j, 0), 0.0)


def _shift_up(v, j, row, seq):
    return jnp.where(row < seq - j, pltpu.roll(v, seq - j, 0), 0.0)


def _conv_fwd(up, w_conv, b_conv):
    B, S, _ = up.shape
    nj = D_FF // CONV_COLS

    def body(up_ref, w_ref, b_ref, ff_ref):
        a = up_ref[:, :CONV_COLS].astype(F32)
        val = up_ref[:, CONV_COLS:].astype(F32)
        row = lax.broadcasted_iota(jnp.int32, a.shape, 0)
        w0, w1, w2 = w_ref[0:1, :], w_ref[1:2, :], w_ref[2:3, :]
        conv = b_ref[...] + w0 * a + w1 * _shift_down(a, 1, row) + w2 * _shift_down(a, 2, row)
        ff_ref[...] = (conv * _sigmoid(conv) * val).astype(ff_ref.dtype)

    return pl.pallas_call(
        body, name="conv_gate_fwd", grid=(B, nj),
        in_specs=[pl.BlockSpec((None, S, 2 * CONV_COLS), lambda b, j: (b, 0, j)),
                  pl.BlockSpec((3, CONV_COLS), lambda b, j: (0, j)),
                  pl.BlockSpec((1, CONV_COLS), lambda b, j: (0, j))],
        out_specs=pl.BlockSpec((None, S, CONV_COLS), lambda b, j: (b, 0, j)),
        out_shape=jax.ShapeDtypeStruct((B, S, D_FF), BF16),
        compiler_params=_params(2),
    )(up, w_conv, b_conv)


def _conv_bwd(up, d_ff, w_conv, b_conv):
    B, S, _ = up.shape
    nj = D_FF // CONV_COLS

    def body(up_ref, dff_ref, w_ref, b_ref, dup_ref, dw_ref, db_ref):
        b = pl.program_id(1)
        a = up_ref[:, :CONV_COLS].astype(F32)
        val = up_ref[:, CONV_COLS:].astype(F32)
        row = lax.broadcasted_iota(jnp.int32, a.shape, 0)
        w0, w1, w2 = w_ref[0:1, :], w_ref[1:2, :], w_ref[2:3, :]
        a1, a2 = _shift_down(a, 1, row), _shift_down(a, 2, row)
        conv = b_ref[...] + w0 * a + w1 * a1 + w2 * a2
        sg = _sigmoid(conv)
        dff = dff_ref[...].astype(F32)
        d_val = dff * conv * sg
        dc = dff * val * (sg * (1.0 + conv * (1.0 - sg)))
        d_a = w0 * dc + w1 * _shift_up(dc, 1, row, S) + w2 * _shift_up(dc, 2, row, S)
        dup_ref[:, :CONV_COLS] = d_a.astype(dup_ref.dtype)
        dup_ref[:, CONV_COLS:] = d_val.astype(dup_ref.dtype)

        @pl.when(b == 0)
        def _():
            dw_ref[...] = jnp.zeros_like(dw_ref)
            db_ref[...] = jnp.zeros_like(db_ref)

        dw_ref[0:1, :] += _col_sum(dc * a)
        dw_ref[1:2, :] += _col_sum(dc * a1)
        dw_ref[2:3, :] += _col_sum(dc * a2)
        db_ref[...] += _col_sum(dc)

    return pl.pallas_call(
        body, name="conv_gate_bwd", grid=(nj, B),
        in_specs=[pl.BlockSpec((None, S, 2 * CONV_COLS), lambda j, b: (b, 0, j)),
                  pl.BlockSpec((None, S, CONV_COLS), lambda j, b: (b, 0, j)),
                  pl.BlockSpec((3, CONV_COLS), lambda j, b: (0, j)),
                  pl.BlockSpec((1, CONV_COLS), lambda j, b: (0, j))],
        out_specs=[pl.BlockSpec((None, S, 2 * CONV_COLS), lambda j, b: (b, 0, j)),
                   pl.BlockSpec((3, CONV_COLS), lambda j, b: (0, j)),
                   pl.BlockSpec((1, CONV_COLS), lambda j, b: (0, j))],
        out_shape=[jax.ShapeDtypeStruct((B, S, 2 * D_FF), BF16), jax.ShapeDtypeStruct((3, D_FF), F32),
                   jax.ShapeDtypeStruct((1, D_FF), F32)],
        compiler_params=_params(2),
    )(up, d_ff, w_conv, b_conv)


def _ada_fwd(c_all, w_ada, b_ada):
    def body(c_ref, w_ref, b_ref, o_ref):
        cv = c_ref[...]
        act = (cv * _sigmoid(cv)).astype(BF16)
        o_ref[...] = jnp.dot(act, w_ref[...].astype(BF16), preferred_element_type=F32) + b_ref[...]

    return pl.pallas_call(body, name="ada_fwd",
                          out_shape=jax.ShapeDtypeStruct((c_all.shape[0], w_ada.shape[1]), F32),
                          compiler_params=pltpu.CompilerParams(vmem_limit_bytes=V7X_VMEM_LIMIT))(c_all, w_ada, b_ada)


def _ada_bwd(c_all, dmod_all, dmod_cols):
    def body(c_ref, dm_ref, dmc_ref, dw_ref, db_ref):
        cv = c_ref[...]
        act = (cv * _sigmoid(cv)).astype(BF16)
        dw_ref[...] = lax.dot_general(act, dmc_ref[...].astype(BF16), _TN, preferred_element_type=F32)
        db_ref[...] = _col_sum(dm_ref[...])

    return pl.pallas_call(
        body, name="ada_bwd",
        out_shape=[jax.ShapeDtypeStruct((c_all.shape[1], dmod_cols.shape[1]), F32),
                   jax.ShapeDtypeStruct((1, dmod_all.shape[1]), F32)],
        compiler_params=pltpu.CompilerParams(vmem_limit_bytes=V7X_VMEM_LIMIT))(c_all, dmod_all, dmod_cols)


def _adamw(w, m, v, g_parts, name):
    R, C = w.shape
    P = g_parts.shape[0]
    tr = R
    for cand in (256, 128, 64, 32, 16, 8):
        if R % cand == 0 and cand * C * 4 * (P + 7) * 2 <= V7X_VMEM_LIMIT // 2:
            tr = cand
            break
    c1 = 1.0 / (1.0 - ADAM_B1 ** ADAM_STEP)
    c2 = 1.0 / (1.0 - ADAM_B2 ** ADAM_STEP)

    def body(w_ref, m_ref, v_ref, g_ref, og, od, om, ov):
        g = g_ref[0].astype(F32)
        for p in range(1, P):
            g = g + g_ref[p].astype(F32)
        m_new = ADAM_B1 * m_ref[...] + (1.0 - ADAM_B1) * g
        v_new = ADAM_B2 * v_ref[...] + (1.0 - ADAM_B2) * (g * g)
        og[...] = g
        om[...] = m_new
        ov[...] = v_new
        od[...] = -ADAM_LR * ((m_new * c1) / (jnp.sqrt(v_new * c2) + ADAM_EPS) + ADAM_WD * w_ref[...])

    spec = pl.BlockSpec((tr, C), lambda i: (i, 0))
    return pl.pallas_call(
        body, name=name, grid=(R // tr,),
        in_specs=[spec, spec, spec, pl.BlockSpec((P, tr, C), lambda i: (0, i, 0))],
        out_specs=[spec] * 4, out_shape=[jax.ShapeDtypeStruct((R, C), F32)] * 4,
        compiler_params=_params(1),
    )(w, m, v, g_parts)


def _sum_parts(parts, loss_rows):
    P, R, C = parts.shape
    lo, hi = loss_rows

    def body(p_ref, o_ref, loss_ref):
        t = p_ref[0]
        for p in range(1, P):
            t = t + p_ref[p]
        o_ref[...] = t
        tot = jnp.sum(jnp.sum(o_ref[lo:hi, :], axis=1, keepdims=True), axis=0, keepdims=True)
        loss_ref[...] = jnp.broadcast_to(tot, loss_ref.shape)

    return pl.pallas_call(body, name="sum_small_grads",
                          out_shape=[jax.ShapeDtypeStruct((R, C), F32), jax.ShapeDtypeStruct((1, LANES), F32)],
                          compiler_params=pltpu.CompilerParams(vmem_limit_bytes=V7X_VMEM_LIMIT))(parts)


def _exchange(items, name):
    n = len(items)
    MESH = pl.DeviceIdType.MESH

    def body(*refs):
        src, dst = refs[:n], refs[n:2 * n]
        send_sems, recv_sems, local_sems = refs[2 * n:]
        x, y, c = lax.axis_index("x"), lax.axis_index("y"), lax.axis_index("c")
        me = 4 * x + 2 * y + c
        started = []
        for it, (_, per_peer) in enumerate(items):
            own = pltpu.make_async_copy(src[it].at[me] if per_peer else src[it], dst[it].at[me], local_sems.at[it])
            own.start()
            started.append(own)
        sends, recvs = [], []
        for k in range(1, N_DEV):
            px = 1 - x if k & 4 else x
            py = 1 - y if k & 2 else y
            pc = 1 - c if k & 1 else c
            peer = 4 * px + 2 * py + pc
            for it, (_, per_peer) in enumerate(items):
                s = src[it].at[peer] if per_peer else src[it]
                cp = pltpu.make_async_remote_copy(src_ref=s, dst_ref=dst[it].at[me], send_sem=send_sems.at[it, k - 1],
                                                  recv_sem=recv_sems.at[it, k - 1], device_id=(px, py, pc),
                                                  device_id_type=MESH)
                cp.start()
                sends.append(cp)
                recvs.append(pltpu.make_async_remote_copy(
                    src_ref=s, dst_ref=dst[it].at[peer], send_sem=send_sems.at[it, k - 1],
                    recv_sem=recv_sems.at[it, k - 1], device_id=(px, py, pc), device_id_type=MESH))
        for cp in recvs:
            cp.wait_recv()
        for cp in sends:
            cp.wait_send()
        for cp in started:
            cp.wait()

    any_spec = pl.BlockSpec(memory_space=pl.ANY)
    out_shape = []
    for a, per_peer in items:
        shp = a.shape if per_peer else (N_DEV,) + a.shape
        out_shape.append(jax.ShapeDtypeStruct(shp, a.dtype))
    return pl.pallas_call(
        body, name=name, in_specs=[any_spec] * n, out_specs=[any_spec] * n, out_shape=out_shape,
        scratch_shapes=[pltpu.SemaphoreType.DMA((n, N_DEV - 1)), pltpu.SemaphoreType.DMA((n, N_DEV - 1)),
                        pltpu.SemaphoreType.DMA((n,))],
    )(*[a for a, _ in items])


def _remote(src, dst, send_sem, recv_sem, device):
    return pltpu.make_async_remote_copy(src_ref=src, dst_ref=dst, send_sem=send_sem, recv_sem=recv_sem,
                                        device_id=device, device_id_type=pl.DeviceIdType.MESH)


def _mesh_place():
    x, y, c = lax.axis_index("x"), lax.axis_index("y"), lax.axis_index("c")
    other_chips = [(1 - x, y), (x, 1 - y), (1 - x, 1 - y)]
    return x, y, c, (x, y, 1 - c), other_chips


def _gather_all(items, name):
    n = len(items)

    def body(*refs):
        src, dst = refs[:n], refs[n:2 * n]
        send_sems, recv_sems, local_sems = refs[2 * n:]
        x, y, c, sibling, chips = _mesh_place()
        slot = lambda px, py, pc: 4 * px + 2 * py + pc
        me = slot(x, y, c)
        own = [pltpu.make_async_copy(src[it], dst[it].at[me], local_sems.at[it]) for it in range(n)]
        first = []
        for it in range(n):
            first.append(_remote(src[it], dst[it].at[me], send_sems.at[it, 0], recv_sems.at[it, 0], sibling))
            for j, chip in enumerate(chips):
                first.append(_remote(src[it], dst[it].at[me], send_sems.at[it, 1 + j], recv_sems.at[it, 1 + j],
                                     (*chip, c)))
        for cp in own + first:
            cp.start()
        passed = []
        for j, chip in enumerate(chips):
            blk = slot(*chip, c)
            for it in range(n):
                _remote(src[it], dst[it].at[blk], send_sems.at[it, 1 + j], recv_sems.at[it, 1 + j],
                        (*chip, c)).wait_recv()
                fwd = _remote(dst[it].at[blk], dst[it].at[blk], send_sems.at[it, 4 + j], recv_sems.at[it, 4 + j],
                              sibling)
                fwd.start()
                passed.append(fwd)
        for it in range(n):
            _remote(src[it], dst[it].at[slot(x, y, 1 - c)], send_sems.at[it, 0], recv_sems.at[it, 0],
                    sibling).wait_recv()
        for j, chip in enumerate(chips):
            for it in range(n):
                _remote(src[it], dst[it].at[slot(*chip, 1 - c)], send_sems.at[it, 4 + j], recv_sems.at[it, 4 + j],
                        sibling).wait_recv()
        for cp in first + passed:
            cp.wait_send()
        for cp in own:
            cp.wait()

    any_spec = pl.BlockSpec(memory_space=pl.ANY)
    return pl.pallas_call(
        body, name=name, in_specs=[any_spec] * n, out_specs=[any_spec] * n,
        out_shape=[jax.ShapeDtypeStruct((N_DEV,) + a.shape, a.dtype) for a in items],
        scratch_shapes=[pltpu.SemaphoreType.DMA((n, 7)), pltpu.SemaphoreType.DMA((n, 7)),
                        pltpu.SemaphoreType.DMA((n,))],
    )(*items)


N_CHIPS = N_DEV // 2


def _sibling_swap(g, name):
    def body(g_ref, got_ref, send_sems, recv_sems):
        x, y, c, sibling, _ = _mesh_place()
        sends = [_remote(g_ref.at[2 * chip + 1 - c], got_ref.at[chip], send_sems.at[chip], recv_sems.at[chip],
                         sibling) for chip in range(N_CHIPS)]
        for cp in sends:
            cp.start()
        for cp in sends:
            cp.wait_recv()
        for cp in sends:
            cp.wait_send()

    any_spec = pl.BlockSpec(memory_space=pl.ANY)
    return pl.pallas_call(
        body, name=name, in_specs=[any_spec], out_specs=any_spec,
        out_shape=jax.ShapeDtypeStruct((N_CHIPS,) + g.shape[1:], g.dtype),
        scratch_shapes=[pltpu.SemaphoreType.DMA((N_CHIPS,))] * 2,
    )(g)


def _chip_scatter(s, name):
    def body(s_ref, out_ref, send_sems, recv_sems, local_sem):
        x, y, c, _, _ = _mesh_place()
        my_chip = 2 * x + y
        keep = pltpu.make_async_copy(s_ref.at[my_chip], out_ref.at[my_chip], local_sem)
        keep.start()
        sends, recvs = [], []
        for k in range(1, N_CHIPS):
            px = 1 - x if k & 2 else x
            py = 1 - y if k & 1 else y
            peer_chip = 2 * px + py
            cp = _remote(s_ref.at[peer_chip], out_ref.at[my_chip], send_sems.at[k - 1], recv_sems.at[k - 1],
                         (px, py, c))
            cp.start()
            sends.append(cp)
            recvs.append(_remote(s_ref.at[peer_chip], out_ref.at[peer_chip], send_sems.at[k - 1],
                                 recv_sems.at[k - 1], (px, py, c)))
        for cp in recvs:
            cp.wait_recv()
        for cp in sends:
            cp.wait_send()
        keep.wait()

    any_spec = pl.BlockSpec(memory_space=pl.ANY)
    return pl.pallas_call(
        body, name=name, in_specs=[any_spec], out_specs=any_spec,
        out_shape=jax.ShapeDtypeStruct(s.shape, s.dtype),
        scratch_shapes=[pltpu.SemaphoreType.DMA((N_CHIPS - 1,)), pltpu.SemaphoreType.DMA((N_CHIPS - 1,)),
                        pltpu.SemaphoreType.DMA],
    )(s)


PACK_TILES = 8


def _add_pairs(g, got):
    n, R, C = got.shape
    tr = R // PACK_TILES

    def body(core_ref, a_ref, b_ref, o_ref):
        o_ref[...] = (a_ref[...].astype(F32) + b_ref[...].astype(F32)).astype(o_ref.dtype)

    spec = pl.BlockSpec((None, tr, C), lambda i, j, core: (i, j, 0))
    grid_spec = pltpu.PrefetchScalarGridSpec(
        num_scalar_prefetch=1, grid=(n, PACK_TILES),
        in_specs=[pl.BlockSpec((None, tr, C), lambda i, j, core: (2 * i + core[0], j, 0)), spec], out_specs=spec)
    core = lax.axis_index("c").astype(jnp.int32).reshape(1)
    return pl.pallas_call(body, name="add_sibling_partials", grid_spec=grid_spec,
                          out_shape=jax.ShapeDtypeStruct(got.shape, got.dtype),
                          compiler_params=_params(2))(core, g, got)


def _gelu_tanh(y):
    k = math.sqrt(2.0 / math.pi)
    t = jnp.tanh(k * (y + 0.044715 * y * y * y))
    return 0.5 * y * (1.0 + t), t


def _local_step(x, mod, target, W, P):
    B, S, D = x.shape
    T = B * S
    TS = 512
    flat = lambda a: a.reshape(T, a.shape[-1])
    unflat = lambda a: a.reshape(B, S, a.shape[-1])
    mod_col = lambda i: (mod, D, i)

    def f_modnorm(xv, sc, sh, g):
        return (xv * _rms_scale(xv) * g) * (1.0 + sc) + sh

    (u1,) = _rowwise(f_modnorm, [(x, D, 0)], [mod_col(1), mod_col(0)], [P["g_mix"]],
                     [(D, BF16)], [], [], ts=TS, name="modnorm_mix")
    u1f = flat(u1)
    qkv = unflat(_matmul(u1f, W["w_qkv"], name="proj_qkv"))
    us = unflat(_matmul(u1f, W["w_us"], name="proj_ssm_in"))
    gates = unflat(_matmul(u1f, W["w_gates"], out_dtype=BF16, name="proj_gates"))

    o_att, lse = _attention_fwd(qkv, P["slopes"])
    y_att = unflat(_matmul(flat(o_att), W["w_proj_att"], out_dtype=BF16, name="proj_att"))

    bu = unflat(_matmul(flat(us), P["bb_big"], name="s5_bu"))
    xs = _scan_fwd(bu, P["a_row"])
    y_mm = unflat(_matmul(flat(xs), P["cc_big"], name="s5_readout"))

    def f_glu(ymm, usv, dsk, wg, bg):
        yv = ymm + dsk * usv
        ge, _ = _gelu_tanh(yv)
        pre = jnp.dot(ge.astype(BF16), wg, preferred_element_type=F32) + bg
        return yv, ge * _sigmoid(pre)

    y_s5, z = _rowwise(f_glu, [(y_mm, SSM_WIDTH, 0), (us, SSM_WIDTH, 0)], [], [P["d_skip"], W["w_glu"], P["b_glu"]],
                       [(SSM_WIDTH, F32), (SSM_WIDTH, BF16)], [], [], ts=TS, name="s5_glu")
    y_ssm = unflat(_matmul(flat(z), W["w_proj_ssm"], out_dtype=BF16, name="proj_ssm"))

    def f_merge(ga, gs, ya, ys, bga, bgs):
        return _sigmoid(ga + bga) * ya + _sigmoid(gs + bgs) * ys

    bga, bgs = P["b_gate"][:, :D], P["b_gate"][:, D:]
    (merged,) = _rowwise(f_merge, [(gates, D, 0), (gates, D, 1), (y_att, D, 0), (y_ssm, D, 0)], [], [bga, bgs],
                         [(D, BF16)], [], [], ts=TS, name="gate_merge")
    mix = unflat(_matmul(flat(merged), W["w_out"], name="proj_out"))

    def f_res_modnorm(xv, mx, gt, sc, sh, g):
        h = xv + gt * mx
        return h, (h * _rms_scale(h) * g) * (1.0 + sc) + sh

    h1, u2 = _rowwise(f_res_modnorm, [(x, D, 0), (mix, D, 0)], [mod_col(2), mod_col(4), mod_col(3)], [P["g_ffn"]],
                      [(D, F32), (D, BF16)], [], [], ts=TS, name="residual_modnorm_ffn")
    up = unflat(_matmul(flat(u2), W["w_up"], out_dtype=BF16, name="ffn_up"))
    ff = _conv_fwd(up, P["w_conv"], P["b_conv"])
    down = unflat(_matmul(flat(ff), W["w_down"], name="ffn_down"))

    def f_head(h1v, dn, tg, gt, g):
        h2 = h1v + gt * dn
        r = _rms_scale(h2)
        nh = h2 * r
        e = nh * g - tg
        dy = e * (1.0 / D)
        gy = dy * g
        dh = r * (gy - nh * jnp.mean(gy * nh, axis=-1, keepdims=True))
        return (dh, dh * gt, _col_sum(dh * dn), _col_sum(dy * nh), _col_sum(e * e) * (0.5 / D))

    dh2, d_down, d_gt2, d_g_final, loss_cols = _rowwise(
        f_head, [(h1, D, 0), (down, D, 0), (target, D, 0)], [mod_col(5)], [P["g_final"]],
        [(D, F32), (D, BF16)], [D], [(1, D), (1, D)], ts=TS, name="head_loss")

    d_downf = flat(d_down)
    d_ff = unflat(_matmul(d_downf, W["w_down"], tb=True, out_dtype=BF16, name="ffn_down_dx"))
    d_w_down = _matmul(flat(ff), d_downf, ta=True, out_dtype=BF16, name="ffn_down_dw")
    d_up, d_w_conv, d_b_conv = _conv_bwd(up, d_ff, P["w_conv"], P["b_conv"])
    d_upf = flat(d_up)
    d_u2 = unflat(_matmul(d_upf, W["w_up"], tb=True, out_dtype=BF16, name="ffn_up_dx"))
    d_w_up = _matmul(flat(u2), d_upf, ta=True, out_dtype=BF16, name="ffn_up_dw")

    def f_modnorm_bwd(du, h, dres, mx, sc, gt, g):
        r = _rms_scale(h)
        nh = h * r
        dn = du * (1.0 + sc)
        gy = dn * g
        dh = dres + r * (gy - nh * jnp.mean(gy * nh, axis=-1, keepdims=True))
        return (dh, dh * gt, _col_sum(du), _col_sum(du * nh * g), _col_sum(dh * mx), _col_sum(dn * nh))

    dh1, d_mix, d_sh2, d_sc2, d_gt1, d_g_ffn = _rowwise(
        f_modnorm_bwd, [(d_u2, D, 0), (h1, D, 0), (dh2, D, 0), (mix, D, 0)], [mod_col(4), mod_col(2)], [P["g_ffn"]],
        [(D, F32), (D, BF16)], [D, D, D], [(1, D)], ts=TS, name="modnorm_ffn_bwd")

    d_mixf = flat(d_mix)
    d_merged = unflat(_matmul(d_mixf, W["w_out"], tb=True, out_dtype=BF16, name="proj_out_dx"))
    d_w_out = _matmul(flat(merged), d_mixf, ta=True, out_dtype=BF16, name="proj_out_dw")

    def f_merge_bwd(dm, ga, gs, ya, ys, bga_, bgs_):
        sa, ss = _sigmoid(ga + bga_), _sigmoid(gs + bgs_)
        dga = dm * ya * sa * (1.0 - sa)
        dgs = dm * ys * ss * (1.0 - ss)
        return dm * sa, dm * ss, jnp.concatenate([dga, dgs], axis=1), _col_sum(dga), _col_sum(dgs)

    d_y_att, d_y_ssm, d_gates, d_bga, d_bgs = _rowwise(
        f_merge_bwd, [(d_merged, D, 0), (gates, D, 0), (gates, D, 1), (y_att, D, 0), (y_ssm, D, 0)], [], [bga, bgs],
        [(D, BF16), (D, BF16), (2 * D, BF16)], [], [(1, D), (1, D)], ts=TS, name="gate_merge_bwd")

    d_yaf, d_ysf = flat(d_y_att), flat(d_y_ssm)
    d_o_att = unflat(_matmul(d_yaf, W["w_proj_att"], tb=True, name="proj_att_dx"))
    d_w_proj_att = _matmul(flat(o_att), d_yaf, ta=True, out_dtype=BF16, name="proj_att_dw")
    d_z = unflat(_matmul(d_ysf, W["w_proj_ssm"], tb=True, out_dtype=BF16, name="proj_ssm_dx"))
    d_w_proj_ssm = _matmul(flat(z), d_ysf, ta=True, out_dtype=BF16, name="proj_ssm_dw")

    def f_glu_bwd(yv, dz, usv, dsk, wg, bg):
        ge, t = _gelu_tanh(yv)
        pre = jnp.dot(ge.astype(BF16), wg, preferred_element_type=F32) + bg
        sg = _sigmoid(pre)
        dpre = dz * ge * sg * (1.0 - sg)
        dge = dz * sg + lax.dot_general(dpre.astype(BF16), wg, _NT, preferred_element_type=F32)
        k = math.sqrt(2.0 / math.pi)
        dgelu = 0.5 * (1.0 + t) + 0.5 * yv * (1.0 - t * t) * k * (1.0 + 3.0 * 0.044715 * yv * yv)
        dy = dge * dgelu
        dwg = lax.dot_general(ge.astype(BF16), dpre.astype(BF16), _TN, preferred_element_type=F32)
        return dy, dy * dsk, dwg, _col_sum(dpre), _col_sum(dy * usv)

    d_y_s5, d_us_skip, d_w_glu, d_b_glu, d_d_skip = _rowwise(
        f_glu_bwd, [(y_s5, SSM_WIDTH, 0), (d_z, SSM_WIDTH, 0), (us, SSM_WIDTH, 0)], [],
        [P["d_skip"], W["w_glu"], P["b_glu"]],
        [(SSM_WIDTH, BF16), (SSM_WIDTH, F32)], [], [(SSM_WIDTH, SSM_WIDTH), (1, SSM_WIDTH), (1, SSM_WIDTH)],
        ts=TS, name="s5_glu_bwd")
    d_ysf2 = flat(d_y_s5)
    dxs = unflat(_matmul(d_ysf2, P["cc_big"], tb=True, name="s5_readout_dx"))
    d_cc = _matmul(flat(xs), d_ysf2, ta=True, name="s5_readout_dw")
    lam, g_ab = _scan_bwd(dxs, xs, P["a_row"])
    lam = flat(lam)
    d_us_mm = unflat(_matmul(lam, P["bb_big"], tb=True, name="s5_bu_dx"))
    d_bb = _matmul(flat(us), lam, ta=True, name="s5_bu_dw")

    d_qkv = _attention_bwd(qkv, o_att, d_o_att, lse, P["slopes"])

    def f_add(a, b_):
        return a + b_

    (d_us,) = _rowwise(f_add, [(d_us_mm, SSM_WIDTH, 0), (d_us_skip, SSM_WIDTH, 0)], [], [],
                       [(SSM_WIDTH, BF16)], [], [], ts=TS, name="s5_input_grad")
    d_qkvf = flat(d_qkv)
    d_usf = flat(d_us)
    d_gatesf = flat(d_gates)
    d_u1 = (_matmul(d_qkvf, W["w_qkv"], tb=True, name="proj_qkv_dx"),
            _matmul(d_usf, W["w_us"], tb=True, name="proj_ssm_in_dx"),
            _matmul(d_gatesf, W["w_gates"], tb=True, name="proj_gates_dx"))
    d_w_in = jnp.concatenate(
        [_unpair_qkv_columns(_matmul(u1f, d_qkvf, ta=True, out_dtype=BF16, name="proj_qkv_dw")),
         _matmul(u1f, d_usf, ta=True, out_dtype=BF16, name="proj_ssm_in_dw"),
         _matmul(u1f, d_gatesf, ta=True, out_dtype=BF16, name="proj_gates_dw")], axis=1)

    def f_modnorm_bwd_in(du0, du1, du2, h, dres, sc, g):
        du = du0 + du1 + du2
        r = _rms_scale(h)
        nh = h * r
        dn = du * (1.0 + sc)
        gy = dn * g
        dh = dres + r * (gy - nh * jnp.mean(gy * nh, axis=-1, keepdims=True))
        return (dh, _col_sum(du), _col_sum(du * nh * g), _col_sum(dn * nh))

    grad_x, d_sh1, d_sc1, d_g_mix = _rowwise(
        f_modnorm_bwd_in, [(unflat(d_u1[0]), D, 0), (unflat(d_u1[1]), D, 0), (unflat(d_u1[2]), D, 0), (x, D, 0),
                           (dh1, D, 0)], [mod_col(1)], [P["g_mix"]],
        [(D, F32)], [D, D], [(1, D)], ts=TS, name="modnorm_mix_bwd")

    d_mod = jnp.concatenate([d_sh1, d_sc1, d_gt1, d_sh2, d_sc2, d_gt2], axis=-1)
    big = dict(w_in=d_w_in, w_glu=d_w_glu, w_proj_att=d_w_proj_att, w_proj_ssm=d_w_proj_ssm, w_out=d_w_out,
               w_up_pairs=d_w_up, w_conv=d_w_conv, w_down=d_w_down)
    g_ab_re, g_ab_im = _deinterleave(g_ab)
    d_bb_re, d_bb_im = _deinterleave(d_bb)
    d_cc_re, d_cc_im = (t.T for t in _deinterleave(d_cc.T))
    small = dict(g_mix=d_g_mix, b_gate=jnp.concatenate([d_bga, d_bgs], axis=1), g_ab_re=g_ab_re, g_ab_im=g_ab_im,
                 d_bb_re=d_bb_re, d_bb_im=d_bb_im, d_cc_re=d_cc_re, d_cc_im=d_cc_im, d_skip=d_d_skip,
                 b_glu=d_b_glu, g_ffn=d_g_ffn, b_conv=d_b_conv, g_final=d_g_final, loss_cols=loss_cols)
    return grad_x, d_mod, big, small


def _block_diag_in(bb):
    t = bb.reshape(SSM_GROUPS, SSM_STATE, SSM_GROUP_CH)
    eye = jnp.eye(SSM_GROUPS, dtype=bb.dtype)
    return jnp.einsum("gnc,gh->gchn", t, eye).reshape(SSM_WIDTH, SSM_COLS)


def _block_diag_out(cm):
    eye = jnp.eye(SSM_GROUPS, dtype=cm.dtype)
    return jnp.einsum("gcn,gh->gnhc", cm, eye).reshape(SSM_COLS, SSM_WIDTH)


def _diag_blocks_in(m):
    t = m.reshape(SSM_GROUPS, SSM_GROUP_CH, SSM_GROUPS, SSM_STATE)
    idx = jnp.arange(SSM_GROUPS)
    return t[idx, :, idx, :].transpose(0, 2, 1).reshape(SSM_COLS, SSM_GROUP_CH)


def _diag_blocks_out(m):
    t = m.reshape(SSM_GROUPS, SSM_STATE, SSM_GROUPS, SSM_GROUP_CH)
    idx = jnp.arange(SSM_GROUPS)
    return t[idx, :, idx, :].transpose(0, 2, 1)


def _pair_columns(w):
    lead = w.shape[:-1]
    a = w[..., :D_FF].reshape(lead + (D_FF // CONV_COLS, 1, CONV_COLS))
    val = w[..., D_FF:].reshape(lead + (D_FF // CONV_COLS, 1, CONV_COLS))
    return jnp.concatenate([a, val], axis=-2).reshape(lead + (2 * D_FF,))


def _unpair_columns(w):
    lead = w.shape[:-1]
    t = w.reshape(lead + (D_FF // CONV_COLS, 2, CONV_COLS))
    return jnp.concatenate([t[..., 0, :].reshape(lead + (D_FF,)), t[..., 1, :].reshape(lead + (D_FF,))], axis=-1)


def _pair_qkv_columns(w):
    lead = w.shape[:-1]
    return w.reshape(lead + (3, N_HEADS // 2, LANES)).swapaxes(-3, -2).reshape(lead + (3 * ATT_WIDTH,))


def _unpair_qkv_columns(w):
    lead = w.shape[:-1]
    return w.reshape(lead + (N_HEADS // 2, 3, LANES)).swapaxes(-3, -2).reshape(lead + (3 * ATT_WIDTH,))


def _interleave(re, im):
    lead = re.shape[:-1]
    g = lambda a: a.reshape(lead + (SSM_COLS // SCAN_COLS, 1, SCAN_COLS))
    return jnp.concatenate([g(re), g(im)], axis=-2).reshape(lead + (2 * SSM_COLS,))


def _deinterleave(x):
    lead = x.shape[:-1]
    t = x.reshape(lead + (SSM_COLS // SCAN_COLS, 2, SCAN_COLS))
    return t[..., 0, :].reshape(lead + (SSM_COLS,)), t[..., 1, :].reshape(lead + (SSM_COLS,))


def _cols_to_slots(g):
    R = g.shape[0]
    return g.reshape(R, N_DEV, g.shape[1] // N_DEV).transpose(1, 0, 2)


def _slots_to_cols(g):
    return g.transpose(1, 0, 2).reshape(g.shape[1], N_DEV * g.shape[2])


SMALL_ORDER = ("b_ada", "g_mix", "b_gate", "a_re", "a_im", "log_dt", "b_re", "b_im", "c_re", "c_im", "d_skip",
               "b_glu", "g_ffn", "b_conv", "g_final")


def _pack(arrs):
    pieces, offs, row = [], [], 0
    for a in arrs:
        f = a.reshape(-1).astype(F32)
        n = f.shape[0]
        rows = -(-n // LANES)
        pieces.append(jnp.pad(f, (0, rows * LANES - n)))
        offs.append((row, n))
        row += rows
    return jnp.concatenate(pieces).reshape(row, LANES), offs


def _unpack(packed, offs, shapes):
    flat = packed.reshape(-1)
    return [flat[r * LANES:r * LANES + n].reshape(s) for (r, n), s in zip(offs, shapes)]


def _pack_slots(parts):
    n = parts[0].shape[0]
    flat = [p.reshape(n, -1) for p in parts]
    offs, total = [], 0
    for f in flat:
        offs.append((total, f.shape[1]))
        total += f.shape[1]
    unit = LANES * 16 * PACK_TILES
    padded = -(-total // unit) * unit
    buf = jnp.concatenate(flat + [jnp.zeros((n, padded - total), flat[0].dtype)], axis=1)
    return buf.reshape(n, padded // LANES, LANES), offs


def kernel(x, c, w_ada, b_ada, g_mix, w_in, b_gate, a_re, a_im, log_dt, b_re, b_im, c_re, c_im, d_skip, w_glu, b_glu, w_proj_att, w_proj_ssm, w_out, g_ffn, w_up, w_conv, b_conv, w_down, g_final, loss_target, m_w_ada, m_b_ada, m_g_mix, m_w_in, m_b_gate, m_a_re, m_a_im, m_log_dt, m_b_re, m_b_im, m_c_re, m_c_im, m_d_skip, m_w_glu, m_b_glu, m_w_proj_att, m_w_proj_ssm, m_w_out, m_g_ffn, m_w_up, m_w_conv, m_b_conv, m_w_down, m_g_final, v_w_ada, v_b_ada, v_g_mix, v_w_in, v_b_gate, v_a_re, v_a_im, v_log_dt, v_b_re, v_b_im, v_c_re, v_c_im, v_d_skip, v_w_glu, v_b_glu, v_w_proj_att, v_w_proj_ssm, v_w_out, v_g_ffn, v_w_up, v_w_conv, v_b_conv, v_w_down, v_g_final):
    args = dict(locals())
    B, S, D = x.shape
    me = 4 * lax.axis_index("x") + 2 * lax.axis_index("y") + lax.axis_index("c")
    bf = lambda w: w[0].astype(BF16)

    gathered = _gather_all([c, bf(w_in), bf(w_glu), bf(w_proj_att), bf(w_proj_ssm), bf(w_out), bf(w_up), w_conv[0],
                            bf(w_down)], name="gather_weights")
    c_all = gathered[0].reshape(N_DEV * B, D)
    w_in_full = _slots_to_cols(gathered[1])
    n_qkv = 3 * ATT_WIDTH
    W = dict(w_qkv=_pair_qkv_columns(w_in_full[:, :n_qkv]), w_us=w_in_full[:, n_qkv:n_qkv + SSM_WIDTH],
             w_gates=w_in_full[:, n_qkv + SSM_WIDTH:], w_glu=gathered[2].reshape(SSM_WIDTH, SSM_WIDTH),
             w_proj_att=_slots_to_cols(gathered[3]), w_proj_ssm=_slots_to_cols(gathered[4]),
             w_out=gathered[5].reshape(D, D), w_up=_pair_columns(_slots_to_cols(gathered[6])),
             w_down=gathered[8].reshape(D_FF, D))
    w_conv_full = _slots_to_cols(gathered[7])

    n_ada = w_ada.shape[2]
    b_ada_cols = lax.dynamic_slice(b_ada, (0, me * n_ada), (1, n_ada))
    mod_part = _ada_fwd(c_all, w_ada[0], b_ada_cols)
    (mod_slots,) = _exchange([(mod_part.reshape(N_DEV, B, n_ada), True)], name="scatter_modulation")
    mod = mod_slots.transpose(1, 0, 2).reshape(B, 1, 6 * D)

    ab_re, ab_im, f_re, f_im = _s5_params(a_re[0], a_im[0], log_dt[0].reshape(SSM_GROUPS, 1))
    col = lambda a: a.reshape(SSM_COLS, 1)
    b_re2, b_im2 = b_re[0].reshape(SSM_COLS, SSM_GROUP_CH), b_im[0].reshape(SSM_COLS, SSM_GROUP_CH)
    bb_re, bb_im = _s5_input_matrix(col(f_re), col(f_im), b_re2, b_im2)
    slopes = jnp.asarray([2.0 ** (-8.0 * (h + 1) / N_HEADS) for h in range(N_HEADS)], F32)
    P = dict(g_mix=g_mix, g_ffn=g_ffn, g_final=g_final.reshape(1, D), b_gate=b_gate, d_skip=d_skip, b_glu=b_glu,
             b_conv=b_conv, w_conv=w_conv_full, slopes=slopes,
             a_row=_interleave(ab_re.reshape(1, SSM_COLS), ab_im.reshape(1, SSM_COLS)),
             bb_big=_interleave(_block_diag_in(bb_re), _block_diag_in(bb_im)),
             cc_big=_interleave(_block_diag_out(c_re[0]).T, -_block_diag_out(c_im[0]).T).T)

    grad_x, d_mod, big, small = _local_step(x, mod, loss_target, W, P)

    small_list = [small["loss_cols"], small["g_mix"], small["b_gate"], small["g_ab_re"], small["g_ab_im"],
                  _diag_blocks_in(small["d_bb_re"]), _diag_blocks_in(small["d_bb_im"]),
                  _diag_blocks_out(small["d_cc_re"]), -_diag_blocks_out(small["d_cc_im"]),
                  small["g_ffn"], small["b_conv"], small["g_final"], small["d_skip"], small["b_glu"]]
    small_packed, small_offs = _pack(small_list)
    d_w_up_full = _unpair_columns(big["w_up_pairs"])
    sharded = ("w_in", "w_glu", "w_proj_att", "w_proj_ssm", "w_out", "w_up", "w_conv", "w_down")
    slots = [_cols_to_slots(big["w_in"]), big["w_glu"].astype(BF16).reshape(N_DEV, SSM_WIDTH // N_DEV, SSM_WIDTH),
             _cols_to_slots(big["w_proj_att"]), _cols_to_slots(big["w_proj_ssm"]),
             big["w_out"].reshape(N_DEV, D // N_DEV, D), _cols_to_slots(d_w_up_full),
             _cols_to_slots(big["w_conv"].astype(BF16)), big["w_down"].reshape(N_DEV, D_FF // N_DEV, D)]
    packed, pack_offs = _pack_slots(slots)
    from_sibling = _sibling_swap(packed, name="swap_sibling_gradients")
    chip_sums = _chip_scatter(_add_pairs(packed, from_sibling), name="scatter_chip_gradients")
    chip_sums = chip_sums.reshape(N_CHIPS, -1)
    small_all, dmod_slots = _gather_all([small_packed, d_mod.reshape(B, 6 * D)], name="gather_small_gradients")

    out = {}

    def update(name, parts):
        w2 = args[name][0]
        g, dl, mn, vn = _adamw(w2, args["m_" + name][0], args["v_" + name][0], parts, name="adamw_" + name)
        for key, val in (("grad_", g), ("delta_", dl), ("new_m_", mn), ("new_v_", vn)):
            out[key + name] = val[None]

    for name, (off, cnt) in zip(sharded, pack_offs):
        update(name, chip_sums[:, off:off + cnt].reshape((N_CHIPS,) + args[name].shape[1:]))

    dmod_all = dmod_slots.reshape(N_DEV * B, 6 * D)
    dmod_cols = lax.dynamic_slice(dmod_all, (0, me * n_ada), (N_DEV * B, n_ada))
    d_w_ada, d_b_ada = _ada_bwd(c_all, dmod_all, dmod_cols)
    update("w_ada", d_w_ada[None])

    loss_row, loss_n = small_offs[0]
    small_sum, loss_vec = _sum_parts(small_all, (loss_row, loss_row + loss_n // LANES))
    shapes = [(1, D), (1, D), (1, 2 * D), (SSM_GROUPS, SSM_STATE), (SSM_GROUPS, SSM_STATE), (SSM_COLS, SSM_GROUP_CH),
              (SSM_COLS, SSM_GROUP_CH), (1, SSM_GROUPS, SSM_GROUP_CH, SSM_STATE),
              (1, SSM_GROUPS, SSM_GROUP_CH, SSM_STATE), (1, D), (1, D_FF), (D,), (1, SSM_WIDTH), (1, SSM_WIDTH)]
    (_, s_g_mix, s_b_gate, s_ab_re, s_ab_im, s_bb_re, s_bb_im, s_c_re, s_c_im, s_g_ffn, s_b_conv, s_g_final,
     s_d_skip, s_b_glu) = _unpack(small_sum, small_offs, shapes)
    d_b_re2, d_b_im2, d_f_re, d_f_im = _s5_input_matrix_bwd(col(f_re), col(f_im), b_re2, b_im2, s_bb_re, s_bb_im)
    d_a_re, d_a_im, d_log_dt = _s5_params_bwd(a_re[0], a_im[0], log_dt[0].reshape(SSM_GROUPS, 1), s_ab_re, s_ab_im,
                                              d_f_re.reshape(SSM_GROUPS, SSM_STATE),
                                              d_f_im.reshape(SSM_GROUPS, SSM_STATE))
    grads_small = dict(b_ada=d_b_ada, g_mix=s_g_mix, b_gate=s_b_gate, a_re=d_a_re[None], a_im=d_a_im[None],
                       log_dt=d_log_dt.reshape(1, SSM_GROUPS), b_re=d_b_re2.reshape(b_re.shape),
                       b_im=d_b_im2.reshape(b_im.shape), c_re=s_c_re, c_im=s_c_im, d_skip=s_d_skip, b_glu=s_b_glu,
                       g_ffn=s_g_ffn, b_conv=s_b_conv, g_final=s_g_final)
    w_pack, offs = _pack([args[n] for n in SMALL_ORDER])
    m_pack, _ = _pack([args["m_" + n] for n in SMALL_ORDER])
    v_pack, _ = _pack([args["v_" + n] for n in SMALL_ORDER])
    g_pack, _ = _pack([grads_small[n] for n in SMALL_ORDER])
    res = _adamw(w_pack, m_pack, v_pack, g_pack[None], name="adamw_small")
    shapes_small = [args[n].shape for n in SMALL_ORDER]
    for key, packed in zip(("grad_", "delta_", "new_m_", "new_v_"), res):
        for n, val in zip(SMALL_ORDER, _unpack(packed, offs, shapes_small)):
            out[key + n] = val

    order = ["w_ada", "b_ada", "g_mix", "w_in", "b_gate", "a_re", "a_im", "log_dt", "b_re", "b_im", "c_re", "c_im",
             "d_skip", "w_glu", "b_glu", "w_proj_att", "w_proj_ssm", "w_out", "g_ffn", "w_up", "w_conv", "b_conv",
             "w_down", "g_final"]
    loss = loss_vec[0, 0]
    return (loss, grad_x, *[out[k + n] for k in ("grad_", "delta_", "new_m_", "new_v_") for n in order])
```

```python
import math

import jax
import jax.numpy as jnp
from jax import lax
from jax.experimental import pallas as pl
from jax.experimental.pallas import tpu as pltpu

F32 = jnp.float32
BF16 = jnp.bfloat16

N_DEV = 8
D_MODEL = 1024
N_HEADS = 8
HEAD_DIM = 64
ATT_WIDTH = N_HEADS * HEAD_DIM
DILATIONS = (1, 4, 16)
WIN = 128
SSM_GROUPS = 16
SSM_GROUP_CH = 16
SSM_WIDTH = SSM_GROUPS * SSM_GROUP_CH
SSM_STATE = 64
SSM_COLS = SSM_GROUPS * SSM_STATE
D_FF = 2048
EPS = 1e-6
NEG_INF = -1e30
ADAM_LR, ADAM_B1, ADAM_B2, ADAM_EPS, ADAM_WD, ADAM_STEP = 0.001, 0.9, 0.999, 1e-08, 0.01, 10

V7X_VMEM_LIMIT = 56 * 1024 * 1024
LANES = 128


def _params(n_grid):
    return pltpu.CompilerParams(dimension_semantics=("arbitrary",) * n_grid,
                                vmem_limit_bytes=V7X_VMEM_LIMIT)


def _tile(n, pref):
    if n <= pref:
        return n
    t = (pref // LANES) * LANES
    while t > 0:
        if n % t == 0:
            return t
        t -= LANES
    return n


def _matmul(a, b, *, ta=False, tb=False, out_dtype=F32, name):
    if ta:
        K, M = a.shape
    else:
        M, K = a.shape
    if tb:
        N, K2 = b.shape
    else:
        K2, N = b.shape
    assert K == K2, (a.shape, b.shape)
    if ta:
        tm, tn, tk = _tile(M, 1024), _tile(N, 2048), _tile(K, 512)
    else:
        tm, tk = _tile(M, 512), _tile(K, 4096)
        tn = _tile(N, 2048 if K <= 2048 else 1024)
    nk = K // tk
    dn = (((0,) if ta else (1,), (1,) if tb else (0,)), ((), ()))

    def body(a_ref, b_ref, o_ref, acc_ref):
        k = pl.program_id(2)
        part = lax.dot_general(a_ref[...].astype(BF16), b_ref[...].astype(BF16), dn, preferred_element_type=F32)
        if nk == 1:
            o_ref[...] = part.astype(o_ref.dtype)
            return

        @pl.when(k == 0)
        def _():
            acc_ref[...] = jnp.zeros_like(acc_ref)

        acc_ref[...] += part

        @pl.when(k == nk - 1)
        def _():
            o_ref[...] = acc_ref[...].astype(o_ref.dtype)

    a_spec = (pl.BlockSpec((tk, tm), lambda j, i, k: (k, i)) if ta
              else pl.BlockSpec((tm, tk), lambda j, i, k: (i, k)))
    b_spec = (pl.BlockSpec((tn, tk), lambda j, i, k: (j, k)) if tb
              else pl.BlockSpec((tk, tn), lambda j, i, k: (k, j)))
    return pl.pallas_call(
        body, name=name, grid=(N // tn, M // tm, nk),
        in_specs=[a_spec, b_spec],
        out_specs=pl.BlockSpec((tm, tn), lambda j, i, k: (i, j)),
        out_shape=jax.ShapeDtypeStruct((M, N), out_dtype),
        scratch_shapes=[pltpu.VMEM((tm, tn) if nk > 1 else (8, LANES), F32)],
        compiler_params=_params(3),
    )(a, b)


HALF = 256
UP_SLOTS = N_DEV // 2
UP_GROUP = 4 * HALF


def _group_weight(w_ref):
    return jnp.concatenate([w_ref[0, :, :HALF], w_ref[1, :, :HALF], w_ref[0, :, HALF:], w_ref[1, :, HALF:]], axis=1)


def _up_weight_spec(K, index):
    return pl.BlockSpec((2, None, K, 2 * HALF), index)


def _up_fwd(a, w3, name):
    M, K = a.shape
    tm = _tile(M, 512)

    def body(a_ref, w_ref, o_ref):
        o_ref[...] = jnp.dot(a_ref[...].astype(BF16), _group_weight(w_ref),
                             preferred_element_type=F32).astype(o_ref.dtype)

    return pl.pallas_call(
        body, name=name, grid=(UP_SLOTS, M // tm),
        in_specs=[pl.BlockSpec((tm, K), lambda j, i: (i, 0)), _up_weight_spec(K, lambda j, i: (0, j, 0, 0))],
        out_specs=pl.BlockSpec((tm, UP_GROUP), lambda j, i: (i, j)),
        out_shape=jax.ShapeDtypeStruct((M, UP_SLOTS * UP_GROUP), BF16), compiler_params=_params(2),
    )(a, w3.reshape(2, UP_SLOTS, K, 2 * HALF))


def _up_dx(d, w3, name):
    M = d.shape[0]
    K = w3.shape[1]
    tm = _tile(M, 1024)

    def body(d_ref, w_ref, o_ref, acc_ref):
        j = pl.program_id(1)

        @pl.when(j == 0)
        def _():
            acc_ref[...] = jnp.zeros_like(acc_ref)

        acc_ref[...] += lax.dot_general(d_ref[...], _group_weight(w_ref), _NT, preferred_element_type=F32)

        @pl.when(j == UP_SLOTS - 1)
        def _():
            o_ref[...] = acc_ref[...].astype(o_ref.dtype)

    return pl.pallas_call(
        body, name=name, grid=(M // tm, UP_SLOTS),
        in_specs=[pl.BlockSpec((tm, UP_GROUP), lambda i, j: (i, j)), _up_weight_spec(K, lambda i, j: (0, j, 0, 0))],
        out_specs=pl.BlockSpec((tm, K), lambda i, j: (i, 0)),
        out_shape=jax.ShapeDtypeStruct((M, K), BF16), scratch_shapes=[pltpu.VMEM((tm, K), F32)],
        compiler_params=_params(2),
    )(d, w3.reshape(2, UP_SLOTS, K, 2 * HALF))


def _up_dw(a, d, name):
    M, K = a.shape
    tk = _tile(M, 512)
    nk = M // tk

    def body(a_ref, d_ref, o_ref, acc_ref):
        k = pl.program_id(1)

        @pl.when(k == 0)
        def _():
            acc_ref[...] = jnp.zeros_like(acc_ref)

        acc_ref[...] += lax.dot_general(a_ref[...], d_ref[...], _TN, preferred_element_type=F32)

        @pl.when(k == nk - 1)
        def _():
            for half in range(2):
                for part in range(2):
                    lo = (2 * half + part) * HALF
                    o_ref[part, :, half * HALF:(half + 1) * HALF] = acc_ref[:, lo:lo + HALF].astype(o_ref.dtype)

    out = pl.pallas_call(
        body, name=name, grid=(UP_SLOTS, nk),
        in_specs=[pl.BlockSpec((tk, K), lambda j, k: (k, 0)), pl.BlockSpec((tk, UP_GROUP), lambda j, k: (k, j))],
        out_specs=_up_weight_spec(K, lambda j, k: (0, j, 0, 0)),
        out_shape=jax.ShapeDtypeStruct((2, UP_SLOTS, K, 2 * HALF), BF16),
        scratch_shapes=[pltpu.VMEM((K, UP_GROUP), F32)], compiler_params=_params(2),
    )(a, d)
    return out.reshape(N_DEV, K, 2 * HALF)


def _rowwise(fn, rows, bvecs, consts, out_rows, out_b, out_g, *, ts, name):
    B, S = rows[0][0].shape[:2]
    nin = len(rows) + len(bvecs) + len(consts)
    nr, nb, ng = len(out_rows), len(out_b), len(out_g)

    def body(*refs):
        b = pl.program_id(0)
        s = pl.program_id(1)
        vals = [r[...] for r in refs[:nin]]
        vals[:len(rows)] = [v.astype(F32) for v in vals[:len(rows)]]
        outs = fn(*vals)
        if not isinstance(outs, (tuple, list)):
            outs = (outs,)
        orefs = refs[nin:]
        for i in range(nr):
            orefs[i][...] = outs[i].astype(orefs[i].dtype)
        for i in range(nb):
            ref = orefs[nr + i]

            @pl.when(s == 0)
            def _(ref=ref):
                ref[...] = jnp.zeros_like(ref)

            ref[...] += outs[nr + i]
        for i in range(ng):
            ref = orefs[nr + nb + i]

            @pl.when((s == 0) & (b == 0))
            def _(ref=ref):
                ref[...] = jnp.zeros_like(ref)

            ref[...] += outs[nr + nb + i]

    in_specs = ([pl.BlockSpec((None, ts, cb), lambda b, s, ci=ci: (b, s, ci)) for (_, cb, ci) in rows]
                + [pl.BlockSpec((None, 1, cb), lambda b, s, ci=ci: (b, 0, ci)) for (_, cb, ci) in bvecs]
                + [pl.BlockSpec(a.shape, lambda b, s: (0, 0)) for a in consts])
    out_shape = ([jax.ShapeDtypeStruct((B, S, c), dt) for (c, dt) in out_rows]
                 + [jax.ShapeDtypeStruct((B, 1, c), F32) for c in out_b]
                 + [jax.ShapeDtypeStruct(rc, F32) for rc in out_g])
    out_specs = ([pl.BlockSpec((None, ts, c), lambda b, s: (b, s, 0)) for (c, _) in out_rows]
                 + [pl.BlockSpec((None, 1, c), lambda b, s: (b, 0, 0)) for c in out_b]
                 + [pl.BlockSpec(rc, lambda b, s: (0, 0)) for rc in out_g])
    args = [a for (a, _, _) in rows] + [a for (a, _, _) in bvecs] + list(consts)
    return pl.pallas_call(
        body, name=name, grid=(B, S // ts), in_specs=in_specs, out_specs=out_specs,
        out_shape=out_shape, compiler_params=_params(2),
    )(*args)


def _col_sum(v):
    return jnp.sum(v, axis=0, keepdims=True)


def _rms_scale(h):
    return lax.rsqrt(jnp.mean(h * h, axis=-1, keepdims=True) + EPS)


def _sigmoid(v):
    return 1.0 / (1.0 + jnp.exp(-v))


ATT_SCALE = HEAD_DIM ** -0.5
COPY_ROWS = 256
_NT = (((1,), (1,)), ((), ()))
_TN = (((0,), (0,)), ((), ()))


def _row_chunks(d, seq):
    sub = seq // d
    out = []
    for r in range(d):
        for c0 in range(0, sub, COPY_ROWS):
            n = min(COPY_ROWS, sub - c0)
            out.append((pl.ds(r + c0 * d, n, stride=d), r * sub + c0, n))
    return out


ATT_UNROLL = 8
KEYS = 2 * WIN


def _zero_once(refs):
    @pl.when((pl.program_id(0) == 0) & (pl.program_id(1) == 0))
    def _():
        for r in refs:
            r[...] = jnp.zeros_like(r)


def _pair_bias(bias_ref, slopes_ref, hp, d, key_major):
    shape = (KEYS, WIN) if key_major else (WIN, KEYS)
    qi = lax.broadcasted_iota(jnp.int32, shape, 1 if key_major else 0)
    kj = lax.broadcasted_iota(jnp.int32, shape, 0 if key_major else 1)
    dist = WIN + qi - kj
    valid = (dist >= 0) & (dist <= WIN)
    distf = dist.astype(F32)
    for h in range(2):
        slope_d = slopes_ref[2 * hp + h] * float(d)
        with_prev = jnp.where(valid, -(slope_d * distf), NEG_INF)
        no_prev = jnp.where(kj >= WIN, with_prev, NEG_INF)
        span = slice(h * KEYS, (h + 1) * KEYS)
        if key_major:
            bias_ref[1, span, :] = with_prev
            bias_ref[0, span, :] = no_prev
        else:
            bias_ref[1, :, span] = with_prev
            bias_ref[0, :, span] = no_prev


def _stack_heads(v):
    first = lax.broadcasted_iota(jnp.int32, v.shape, 1) < HEAD_DIM
    zero = jnp.zeros_like(v)
    return jnp.concatenate([jnp.where(first, v, zero), jnp.where(first, zero, v)], axis=0)


def _per_head(c0, c1, n):
    return jnp.where(lax.broadcasted_iota(jnp.int32, (n, LANES), 1) < HEAD_DIM, c0, c1)


def _qkv_spec(seq, j):
    return pl.BlockSpec((None, seq, LANES), lambda b, hp: (b, 0, 3 * hp + j))


def _attention_fwd(qkv, slopes):
    B, S, _ = qkv.shape
    n_blk = S // WIN
    n_pair = N_HEADS // 2

    def body(slopes_ref, q_ref, k_ref, v_ref, o_ref, lse_ref, qp, kp, vp, bias, acc, mx, sm, acc_n, mx_n, sm_n):
        hp = pl.program_id(1)
        _zero_once((kp, vp))
        for p, d in enumerate(DILATIONS):
            nb = n_blk // d
            chunks = _row_chunks(d, S)
            for src, dst, n in chunks:
                qp[dst:dst + n, :] = (q_ref[src, :] * ATT_SCALE).astype(BF16)
                kp[WIN + dst:WIN + dst + n, :] = k_ref[src, :].astype(BF16)
                vp[WIN + dst:WIN + dst + n, :] = v_ref[src, :].astype(BF16)
            _pair_bias(bias, slopes_ref, hp, d, key_major=False)
            acc_t, mx_t, sm_t = (acc_n, mx_n, sm_n) if d == 1 else (acc, mx, sm)

            def block(i, carry, p=p, nb=nb, acc_t=acc_t, mx_t=mx_t, sm_t=sm_t):
                cur = pl.ds(pl.multiple_of(i * WIN, WIN), WIN)
                keys = pl.ds(pl.multiple_of(i * WIN, WIN), KEYS)
                flag = ((i % nb) > 0).astype(jnp.int32)
                s = lax.dot_general(qp[cur, :], _stack_heads(kp[keys, :]), _NT, preferred_element_type=F32)
                s = s + bias[flag]
                es, ms, ls = [], [], []
                for h in range(2):
                    sh = s[:, h * KEYS:(h + 1) * KEYS]
                    m = jnp.max(jnp.maximum(sh[:, :WIN], sh[:, WIN:]), axis=1, keepdims=True)
                    e = jnp.exp(sh - m)
                    es.append(e.astype(BF16))
                    ms.append(m)
                    ls.append(jnp.sum(e[:, :WIN] + e[:, WIN:], axis=1, keepdims=True))
                acc_t[p, cur, :] = jnp.dot(jnp.concatenate(es, axis=1), _stack_heads(vp[keys, :]),
                                           preferred_element_type=F32)
                mx_t[p, cur, :] = _per_head(ms[0], ms[1], WIN)
                sm_t[p, cur, :] = _per_head(ls[0], ls[1], WIN)
                return carry

            lax.fori_loop(0, n_blk, block, 0, unroll=ATT_UNROLL)
            if d > 1:
                for src, dst, n in chunks:
                    acc_n[p, src, :] = acc[p, dst:dst + n, :]
                    mx_n[p, src, :] = mx[p, dst:dst + n, :]
                    sm_n[p, src, :] = sm[p, dst:dst + n, :]

        chunk = 256

        def merge(i, carry):
            rows = pl.ds(pl.multiple_of(i * chunk, chunk), chunk)
            ms = [mx_n[p, rows, :] for p in range(3)]
            m = jnp.maximum(jnp.maximum(ms[0], ms[1]), ms[2])
            ws = [jnp.exp(mp - m) for mp in ms]
            l = ws[0] * sm_n[0, rows, :] + ws[1] * sm_n[1, rows, :] + ws[2] * sm_n[2, rows, :]
            o = (ws[0] * acc_n[0, rows, :] + ws[1] * acc_n[1, rows, :] + ws[2] * acc_n[2, rows, :]) / l
            o_ref[rows, :] = o.astype(o_ref.dtype)
            lse = m + jnp.log(l)
            for h in range(2):
                lse_ref[rows, h:h + 1] = lse[:, h * HEAD_DIM:h * HEAD_DIM + 1]
            return carry

        lax.fori_loop(0, S // chunk, merge, 0)

    return pl.pallas_call(
        body, name="attention_fwd", grid=(B, n_pair),
        in_specs=[pl.BlockSpec(memory_space=pltpu.SMEM), _qkv_spec(S, 0), _qkv_spec(S, 1), _qkv_spec(S, 2)],
        out_specs=[pl.BlockSpec((None, S, LANES), lambda b, hp: (b, 0, hp)),
                   pl.BlockSpec((None, None, S, 2), lambda b, hp: (b, hp, 0, 0))],
        out_shape=[jax.ShapeDtypeStruct((B, S, ATT_WIDTH), BF16),
                   jax.ShapeDtypeStruct((B, n_pair, S, 2), F32)],
        scratch_shapes=[pltpu.VMEM((S, LANES), BF16), pltpu.VMEM((S + WIN, LANES), BF16),
                        pltpu.VMEM((S + WIN, LANES), BF16), pltpu.VMEM((2, WIN, 2 * KEYS), F32)]
        + [pltpu.VMEM((3, S, LANES), F32)] * 6,
        compiler_params=_params(2),
    )(slopes, qkv, qkv, qkv)


def _attention_bwd(qkv, o, do, lse, slopes):
    B, S, _ = qkv.shape
    n_blk = S // WIN
    n_pair = N_HEADS // 2

    def body(slopes_ref, q_ref, k_ref, v_ref, o_ref, do_ref, lse_ref, dx_ref,
             qp, dop, kp, vp, aux, auxp, aux_t, bias_t, dqp, dvk, dq_n, dk_n, dv_n):
        hp = pl.program_id(1)
        aux[...] = jnp.zeros_like(aux)
        for c0 in range(0, S, COPY_ROWS):
            rows = slice(c0, c0 + COPY_ROWS)
            prod = do_ref[rows, :] * o_ref[rows, :].astype(F32)
            for h in range(2):
                aux[rows, 2 * h:2 * h + 1] = lse_ref[rows, h:h + 1]
                aux[rows, 2 * h + 1:2 * h + 2] = jnp.sum(prod[:, h * HEAD_DIM:(h + 1) * HEAD_DIM], axis=1,
                                                         keepdims=True)
        dq_n[...] = jnp.zeros_like(dq_n)
        dk_n[...] = jnp.zeros_like(dk_n)
        dv_n[...] = jnp.zeros_like(dv_n)
        _zero_once((kp, vp))
        for p, d in enumerate(DILATIONS):
            nb = n_blk // d
            chunks = _row_chunks(d, S)
            for src, dst, n in chunks:
                auxp[dst:dst + n, :] = aux[src, :]
                qp[dst:dst + n, :] = (q_ref[src, :] * ATT_SCALE).astype(BF16)
                dop[dst:dst + n, :] = do_ref[src, :].astype(BF16)
                kp[WIN + dst:WIN + dst + n, :] = k_ref[src, :].astype(BF16)
                vp[WIN + dst:WIN + dst + n, :] = v_ref[src, :].astype(BF16)
            for i in range(n_blk):
                aux_t[i] = auxp[i * WIN:(i + 1) * WIN, :].T[0:8, :]
            _pair_bias(bias_t, slopes_ref, hp, d, key_major=True)
            dvk[...] = jnp.zeros_like(dvk)

            def block(i, carry, nb=nb):
                cur = pl.ds(pl.multiple_of(i * WIN, WIN), WIN)
                keys = pl.ds(pl.multiple_of(i * WIN, WIN), KEYS)
                flag = ((i % nb) > 0).astype(jnp.int32)
                q2, do2 = qp[cur, :], dop[cur, :]
                kc = _stack_heads(kp[keys, :])
                s_t = lax.dot_general(kc, q2, _NT, preferred_element_type=F32) + bias_t[flag]
                dp_t = lax.dot_general(_stack_heads(vp[keys, :]), do2, _NT, preferred_element_type=F32)
                ps, dss = [], []
                for h in range(2):
                    span = slice(h * KEYS, (h + 1) * KEYS)
                    p_t = jnp.exp(s_t[span, :] - aux_t[i, 2 * h:2 * h + 1, :])
                    ds_t = p_t * (dp_t[span, :] - aux_t[i, 2 * h + 1:2 * h + 2, :])
                    ps.append(p_t.astype(BF16))
                    dss.append(ds_t.astype(BF16))
                do_rows, q_rows = _stack_heads(do2), _stack_heads(q2)
                zr = jnp.zeros_like(do_rows)
                rhs = jnp.concatenate([jnp.concatenate([do_rows, zr], axis=1),
                                       jnp.concatenate([zr, q_rows], axis=1)], axis=0)
                dvk[keys, :] += jnp.dot(jnp.concatenate(ps + dss, axis=1), rhs, preferred_element_type=F32)
                dqp[cur, :] = lax.dot_general(jnp.concatenate(dss, axis=0), kc, _TN, preferred_element_type=F32)
                return carry

            lax.fori_loop(0, n_blk, block, 0, unroll=ATT_UNROLL)
            for src, dst, n in chunks:
                dq_n[src, :] += dqp[dst:dst + n, :]
                dv_n[src, :] += dvk[WIN + dst:WIN + dst + n, :LANES]
                dk_n[src, :] += dvk[WIN + dst:WIN + dst + n, LANES:]
        for c0 in range(0, S, COPY_ROWS):
            rows = slice(c0, c0 + COPY_ROWS)
            dx_ref[rows, 0:LANES] = (dq_n[rows, :] * ATT_SCALE).astype(dx_ref.dtype)
            dx_ref[rows, LANES:2 * LANES] = dk_n[rows, :].astype(dx_ref.dtype)
            dx_ref[rows, 2 * LANES:3 * LANES] = dv_n[rows, :].astype(dx_ref.dtype)

    pair = lambda width: pl.BlockSpec((None, S, width), lambda b, hp: (b, 0, hp))
    vm = lambda shape, dt: pltpu.VMEM(shape, dt)
    return pl.pallas_call(
        body, name="attention_bwd", grid=(B, n_pair),
        in_specs=[pl.BlockSpec(memory_space=pltpu.SMEM), _qkv_spec(S, 0), _qkv_spec(S, 1), _qkv_spec(S, 2),
                  pair(LANES), pair(LANES), pl.BlockSpec((None, None, S, 2), lambda b, hp: (b, hp, 0, 0))],
        out_specs=pair(3 * LANES),
        out_shape=jax.ShapeDtypeStruct((B, S, 3 * ATT_WIDTH), BF16),
        scratch_shapes=[vm((S, LANES), BF16), vm((S, LANES), BF16),
                        vm((S + WIN, LANES), BF16), vm((S + WIN, LANES), BF16),
                        vm((S, LANES), F32), vm((S, LANES), F32), vm((n_blk, 8, WIN), F32),
                        vm((2, 2 * KEYS, WIN), F32),
                        vm((S, LANES), F32), vm((S + WIN, 2 * LANES), F32),
                        vm((S, LANES), F32), vm((S, LANES), F32), vm((S, LANES), F32)],
        compiler_params=_params(2),
    )(slopes, qkv, qkv, qkv, o, do, lse)


SCAN_COLS = 256
SCAN_ROWS = 8


def _rows_to_tile(rows):
    rid = lax.broadcasted_iota(jnp.int32, (SCAN_ROWS, rows[0].shape[1]), 0)
    tile = jnp.broadcast_to(rows[0], rid.shape)
    for k in range(1, SCAN_ROWS):
        tile = jnp.where(rid == k, rows[k], tile)
    return tile


def _scan_fwd(bu, a_row):
    B, S, _ = bu.shape
    nc = SSM_COLS // SCAN_COLS
    nt = S // SCAN_ROWS
    RE, IM = slice(0, SCAN_COLS), slice(SCAN_COLS, 2 * SCAN_COLS)

    def body(bu_ref, a_ref, xs_ref):
        ar, ai = a_ref[:, RE], a_ref[:, IM]

        def tile(i, carry):
            xr, xi = carry
            out_r, out_i = [], []
            for k in range(SCAN_ROWS):
                xr, xi = (ar * xr - ai * xi + bu_ref[i, k:k + 1, RE], ar * xi + ai * xr + bu_ref[i, k:k + 1, IM])
                out_r.append(xr)
                out_i.append(xi)
            xs_ref[i, :, RE] = _rows_to_tile(out_r)
            xs_ref[i, :, IM] = _rows_to_tile(out_i)
            return xr, xi

        zero = jnp.zeros((1, SCAN_COLS), F32)
        lax.fori_loop(0, nt, tile, (zero, zero))

    col = pl.BlockSpec((None, nt, SCAN_ROWS, 2 * SCAN_COLS), lambda b, j: (b, 0, 0, j))
    xs = pl.pallas_call(
        body, name="s5_scan_fwd", grid=(B, nc),
        in_specs=[col, pl.BlockSpec((1, 2 * SCAN_COLS), lambda b, j: (0, j))],
        out_specs=col,
        out_shape=jax.ShapeDtypeStruct((B, nt, SCAN_ROWS, 2 * SSM_COLS), F32),
        compiler_params=_params(2),
    )(bu.reshape(B, nt, SCAN_ROWS, 2 * SSM_COLS), a_row)
    return xs.reshape(B, S, 2 * SSM_COLS)


def _scan_bwd(dxs, xs, a_row):
    B, S, _ = dxs.shape
    nc = SSM_COLS // SCAN_COLS
    nt = S // SCAN_ROWS
    RE, IM = slice(0, SCAN_COLS), slice(SCAN_COLS, 2 * SCAN_COLS)

    def body(d_ref, x_ref, a_ref, lam_ref, ga_ref):
        b = pl.program_id(1)
        ar, ai = a_ref[:, RE], a_ref[:, IM]
        rid = lax.broadcasted_iota(jnp.int32, (SCAN_ROWS, SCAN_COLS), 0)

        @pl.when(b == 0)
        def _():
            ga_ref[...] = jnp.zeros_like(ga_ref)

        def tile(j, carry):
            lr, li, accr, acci = carry
            i = nt - 1 - j
            out_r, out_i = [None] * SCAN_ROWS, [None] * SCAN_ROWS
            for k in reversed(range(SCAN_ROWS)):
                lr, li = (d_ref[i, k:k + 1, RE] + ar * lr + ai * li, d_ref[i, k:k + 1, IM] + ar * li - ai * lr)
                out_r[k] = lr
                out_i[k] = li
            lam_r = _rows_to_tile(out_r)
            lam_i = _rows_to_tile(out_i)
            lam_ref[i, :, RE] = lam_r
            lam_ref[i, :, IM] = lam_i
            ip = jnp.maximum(i - 1, 0)
            keep = (i > 0).astype(F32)
            last = slice(SCAN_ROWS - 1, SCAN_ROWS)
            xpr = jnp.where(rid == 0, x_ref[ip, last, RE] * keep, pltpu.roll(x_ref[i, :, RE], 1, 0))
            xpi = jnp.where(rid == 0, x_ref[ip, last, IM] * keep, pltpu.roll(x_ref[i, :, IM], 1, 0))
            accr = accr + lam_r * xpr + lam_i * xpi
            acci = acci + lam_i * xpr - lam_r * xpi
            return lr, li, accr, acci

        z1 = jnp.zeros((1, SCAN_COLS), F32)
        z8 = jnp.zeros((SCAN_ROWS, SCAN_COLS), F32)
        _, _, accr, acci = lax.fori_loop(0, nt, tile, (z1, z1, z8, z8))
        ga_ref[:, RE] += _col_sum(accr)
        ga_ref[:, IM] += _col_sum(acci)

    col = pl.BlockSpec((None, nt, SCAN_ROWS, 2 * SCAN_COLS), lambda j, b: (b, 0, 0, j))
    par = pl.BlockSpec((1, 2 * SCAN_COLS), lambda j, b: (0, j))
    t4 = lambda a: a.reshape(B, nt, SCAN_ROWS, 2 * SSM_COLS)
    lam, g_a = pl.pallas_call(
        body, name="s5_scan_bwd", grid=(nc, B),
        in_specs=[col, col, par], out_specs=[col, par],
        out_shape=[jax.ShapeDtypeStruct((B, nt, SCAN_ROWS, 2 * SSM_COLS), F32),
                   jax.ShapeDtypeStruct((1, 2 * SSM_COLS), F32)],
        compiler_params=_params(2),
    )(t4(dxs), t4(xs), a_row)
    return lam.reshape(B, S, 2 * SSM_COLS), g_a


def _s5_discretise(lr, li, log_dt):
    dt = jnp.exp(log_dt)
    mag = jnp.exp(lr * dt)
    ang = li * dt
    ab_re, ab_im = mag * jnp.cos(ang), mag * jnp.sin(ang)
    nr, ni = ab_re - 1.0, ab_im
    den = lr * lr + li * li
    f_re = (nr * lr + ni * li) / den
    f_im = (ni * lr - nr * li) / den
    return dt, ab_re, ab_im, nr, ni, den, f_re, f_im


def _s5_params(a_re, a_im, log_dt):
    def body(lr_ref, li_ref, ld_ref, abr, abi, fr, fi):
        _, ab_re, ab_im, _, _, _, f_re, f_im = _s5_discretise(lr_ref[...], li_ref[...], ld_ref[...])
        abr[...] = ab_re
        abi[...] = ab_im
        fr[...] = f_re
        fi[...] = f_im

    return pl.pallas_call(body, name="s5_params",
                          out_shape=[jax.ShapeDtypeStruct(a_re.shape, F32)] * 4)(a_re, a_im, log_dt)


def _s5_input_matrix(f_re, f_im, b_re, b_im):
    def body(fr, fi, br, bi, o_re, o_im):
        o_re[...] = fr[...] * br[...] - fi[...] * bi[...]
        o_im[...] = fr[...] * bi[...] + fi[...] * br[...]

    return pl.pallas_call(body, name="s5_input_matrix",
                          out_shape=[jax.ShapeDtypeStruct(b_re.shape, F32)] * 2)(f_re, f_im, b_re, b_im)


def _s5_input_matrix_bwd(f_re, f_im, b_re, b_im, g_re, g_im):
    def body(fr, fi, br, bi, gr, gi, dbr, dbi, dfr, dfi):
        dbr[...] = fr[...] * gr[...] + fi[...] * gi[...]
        dbi[...] = fr[...] * gi[...] - fi[...] * gr[...]
        dfr[...] = jnp.sum(br[...] * gr[...] + bi[...] * gi[...], axis=1, keepdims=True)
        dfi[...] = jnp.sum(br[...] * gi[...] - bi[...] * gr[...], axis=1, keepdims=True)

    return pl.pallas_call(
        body, name="s5_input_matrix_bwd",
        out_shape=[jax.ShapeDtypeStruct(b_re.shape, F32)] * 2 + [jax.ShapeDtypeStruct(f_re.shape, F32)] * 2,
    )(f_re, f_im, b_re, b_im, g_re, g_im)


def _s5_params_bwd(a_re, a_im, log_dt, g_ab_re, g_ab_im, d_f_re, d_f_im):
    def body(lr_ref, li_ref, ld_ref, gar, gai, dfr, dfi, o_lr, o_li, o_ld):
        lr, li = lr_ref[...], li_ref[...]
        dt, ab_re, ab_im, nr, ni, den, f_re, f_im = _s5_discretise(lr, li, ld_ref[...])
        d_fr, d_fi = dfr[...], dfi[...]
        d_nr = (d_fr * lr - d_fi * li) / den
        d_ni = (d_fr * li + d_fi * lr) / den
        common = (d_fr * f_re + d_fi * f_im) * 2.0 / den
        d_lr = (d_fr * nr + d_fi * ni) / den - common * lr
        d_li = (d_fr * ni - d_fi * nr) / den - common * li
        d_abr = gar[...] + d_nr
        d_abi = gai[...] + d_ni
        d_mag_mag = d_abr * ab_re + d_abi * ab_im
        d_ang = d_abi * ab_re - d_abr * ab_im
        o_lr[...] = d_lr + d_mag_mag * dt
        o_li[...] = d_li + d_ang * dt
        o_ld[...] = jnp.sum(d_mag_mag * lr + d_ang * li, axis=1, keepdims=True) * dt

    return pl.pallas_call(
        body, name="s5_params_bwd",
        out_shape=[jax.ShapeDtypeStruct(a_re.shape, F32)] * 2 + [jax.ShapeDtypeStruct(log_dt.shape, F32)],
    )(a_re, a_im, log_dt, g_ab_re, g_ab_im, d_f_re, d_f_im)


CONV_COLS = 256


def _shift_down(v, j, row):
    return jnp.where(row >= j, pltpu.roll(v, j, 0), 0.0)


def _shift_up(v, j, row, seq):
    return jnp.where(row < seq - j, pltpu.roll(v, seq - j, 0), 0.0)


def _conv_fwd(up, w_conv, b_conv):
    B, S, _ = up.shape
    nj = D_FF // CONV_COLS

    def body(up_ref, w_ref, b_ref, ff_ref):
        a = up_ref[:, :CONV_COLS].astype(F32)
        val = up_ref[:, CONV_COLS:].astype(F32)
        row = lax.broadcasted_iota(jnp.int32, a.shape, 0)
        w0, w1, w2 = w_ref[0:1, :], w_ref[1:2, :], w_ref[2:3, :]
        conv = b_ref[...] + w0 * a + w1 * _shift_down(a, 1, row) + w2 * _shift_down(a, 2, row)
        ff_ref[...] = (conv * _sigmoid(conv) * val).astype(ff_ref.dtype)

    return pl.pallas_call(
        body, name="conv_gate_fwd", grid=(B, nj),
        in_specs=[pl.BlockSpec((None, S, 2 * CONV_COLS), lambda b, j: (b, 0, j)),
                  pl.BlockSpec((3, CONV_COLS), lambda b, j: (0, j)),
                  pl.BlockSpec((1, CONV_COLS), lambda b, j: (0, j))],
        out_specs=pl.BlockSpec((None, S, CONV_COLS), lambda b, j: (b, 0, j)),
        out_shape=jax.ShapeDtypeStruct((B, S, D_FF), BF16),
        compiler_params=_params(2),
    )(up, w_conv, b_conv)


def _conv_bwd(up, d_ff, w_conv, b_conv):
    B, S, _ = up.shape
    nj = D_FF // CONV_COLS

    def body(up_ref, dff_ref, w_ref, b_ref, dup_ref, dw_ref, db_ref):
        b = pl.program_id(1)
        a = up_ref[:, :CONV_COLS].astype(F32)
        val = up_ref[:, CONV_COLS:].astype(F32)
        row = lax.broadcasted_iota(jnp.int32, a.shape, 0)
        w0, w1, w2 = w_ref[0:1, :], w_ref[1:2, :], w_ref[2:3, :]
        a1, a2 = _shift_down(a, 1, row), _shift_down(a, 2, row)
        conv = b_ref[...] + w0 * a + w1 * a1 + w2 * a2
        sg = _sigmoid(conv)
        dff = dff_ref[...].astype(F32)
        d_val = dff * conv * sg
        dc = dff * val * (sg * (1.0 + conv * (1.0 - sg)))
        d_a = w0 * dc + w1 * _shift_up(dc, 1, row, S) + w2 * _shift_up(dc, 2, row, S)
        dup_ref[:, :CONV_COLS] = d_a.astype(dup_ref.dtype)
        dup_ref[:, CONV_COLS:] = d_val.astype(dup_ref.dtype)

        @pl.when(b == 0)
        def _():
            dw_ref[...] = jnp.zeros_like(dw_ref)
            db_ref[...] = jnp.zeros_like(db_ref)

        dw_ref[0:1, :] += _col_sum(dc * a)
        dw_ref[1:2, :] += _col_sum(dc * a1)
        dw_ref[2:3, :] += _col_sum(dc * a2)
        db_ref[...] += _col_sum(dc)

    return pl.pallas_call(
        body, name="conv_gate_bwd", grid=(nj, B),
        in_specs=[pl.BlockSpec((None, S, 2 * CONV_COLS), lambda j, b: (b, 0, j)),
                  pl.BlockSpec((None, S, CONV_COLS), lambda j, b: (b, 0, j)),
                  pl.BlockSpec((3, CONV_COLS), lambda j, b: (0, j)),
                  pl.BlockSpec((1, CONV_COLS), lambda j, b: (0, j))],
        out_specs=[pl.BlockSpec((None, S, 2 * CONV_COLS), lambda j, b: (b, 0, j)),
                   pl.BlockSpec((3, CONV_COLS), lambda j, b: (0, j)),
                   pl.BlockSpec((1, CONV_COLS), lambda j, b: (0, j))],
        out_shape=[jax.ShapeDtypeStruct((B, S, 2 * D_FF), BF16), jax.ShapeDtypeStruct((3, D_FF), F32),
                   jax.ShapeDtypeStruct((1, D_FF), F32)],
        compiler_params=_params(2),
    )(up, d_ff, w_conv, b_conv)


def _ada_fwd(c_all, w_ada, b_ada):
    def body(c_ref, w_ref, b_ref, o_ref):
        cv = c_ref[...]
        act = (cv * _sigmoid(cv)).astype(BF16)
        o_ref[...] = jnp.dot(act, w_ref[...].astype(BF16), preferred_element_type=F32) + b_ref[...]

    return pl.pallas_call(body, name="ada_fwd",
                          out_shape=jax.ShapeDtypeStruct((c_all.shape[0], w_ada.shape[1]), F32),
                          compiler_params=pltpu.CompilerParams(vmem_limit_bytes=V7X_VMEM_LIMIT))(c_all, w_ada, b_ada)


def _ada_bwd(c_all, dmod_all, dmod_cols):
    def body(c_ref, dm_ref, dmc_ref, dw_ref, db_ref):
        cv = c_ref[...]
        act = (cv * _sigmoid(cv)).astype(BF16)
        dw_ref[...] = lax.dot_general(act, dmc_ref[...].astype(BF16), _TN, preferred_element_type=F32)
        db_ref[...] = _col_sum(dm_ref[...])

    return pl.pallas_call(
        body, name="ada_bwd",
        out_shape=[jax.ShapeDtypeStruct((c_all.shape[1], dmod_cols.shape[1]), F32),
                   jax.ShapeDtypeStruct((1, dmod_all.shape[1]), F32)],
        compiler_params=pltpu.CompilerParams(vmem_limit_bytes=V7X_VMEM_LIMIT))(c_all, dmod_all, dmod_cols)


def _adamw(w, m, v, g_parts, name):
    R, C = w.shape
    P = g_parts.shape[0]
    tr = R
    for cand in (256, 128, 64, 32, 16, 8):
        if R % cand == 0 and cand * C * 4 * (P + 7) * 2 <= V7X_VMEM_LIMIT // 2:
            tr = cand
            break
    c1 = 1.0 / (1.0 - ADAM_B1 ** ADAM_STEP)
    c2 = 1.0 / (1.0 - ADAM_B2 ** ADAM_STEP)

    def body(w_ref, m_ref, v_ref, g_ref, og, od, om, ov):
        g = g_ref[0].astype(F32)
        for p in range(1, P):
            g = g + g_ref[p].astype(F32)
        m_new = ADAM_B1 * m_ref[...] + (1.0 - ADAM_B1) * g
        v_new = ADAM_B2 * v_ref[...] + (1.0 - ADAM_B2) * (g * g)
        og[...] = g
        om[...] = m_new
        ov[...] = v_new
        od[...] = -ADAM_LR * ((m_new * c1) / (jnp.sqrt(v_new * c2) + ADAM_EPS) + ADAM_WD * w_ref[...])

    spec = pl.BlockSpec((tr, C), lambda i: (i, 0))
    return pl.pallas_call(
        body, name=name, grid=(R // tr,),
        in_specs=[spec, spec, spec, pl.BlockSpec((P, tr, C), lambda i: (0, i, 0))],
        out_specs=[spec] * 4, out_shape=[jax.ShapeDtypeStruct((R, C), F32)] * 4,
        compiler_params=_params(1),
    )(w, m, v, g_parts)


def _sum_parts(parts, loss_rows):
    P, R, C = parts.shape
    lo, hi = loss_rows

    def body(p_ref, o_ref, loss_ref):
        t = p_ref[0]
        for p in range(1, P):
            t = t + p_ref[p]
        o_ref[...] = t
        tot = jnp.sum(jnp.sum(o_ref[lo:hi, :], axis=1, keepdims=True), axis=0, keepdims=True)
        loss_ref[...] = jnp.broadcast_to(tot, loss_ref.shape)

    return pl.pallas_call(body, name="sum_small_grads",
                          out_shape=[jax.ShapeDtypeStruct((R, C), F32), jax.ShapeDtypeStruct((1, LANES), F32)],
                          compiler_params=pltpu.CompilerParams(vmem_limit_bytes=V7X_VMEM_LIMIT))(parts)


def _exchange(items, name):
    n = len(items)
    MESH = pl.DeviceIdType.MESH

    def body(*refs):
        src, dst = refs[:n], refs[n:2 * n]
        send_sems, recv_sems, local_sems = refs[2 * n:]
        x, y, c = lax.axis_index("x"), lax.axis_index("y"), lax.axis_index("c")
        me = 4 * x + 2 * y + c
        started = []
        for it, (_, per_peer) in enumerate(items):
            own = pltpu.make_async_copy(src[it].at[me] if per_peer else src[it], dst[it].at[me], local_sems.at[it])
            own.start()
            started.append(own)
        sends, recvs = [], []
        for k in range(1, N_DEV):
            px = 1 - x if k & 4 else x
            py = 1 - y if k & 2 else y
            pc = 1 - c if k & 1 else c
            peer = 4 * px + 2 * py + pc
            for it, (_, per_peer) in enumerate(items):
                s = src[it].at[peer] if per_peer else src[it]
                cp = pltpu.make_async_remote_copy(src_ref=s, dst_ref=dst[it].at[me], send_sem=send_sems.at[it, k - 1],
                                                  recv_sem=recv_sems.at[it, k - 1], device_id=(px, py, pc),
                                                  device_id_type=MESH)
                cp.start()
                sends.append(cp)
                recvs.append(pltpu.make_async_remote_copy(
                    src_ref=s, dst_ref=dst[it].at[peer], send_sem=send_sems.at[it, k - 1],
                    recv_sem=recv_sems.at[it, k - 1], device_id=(px, py, pc), device_id_type=MESH))
        for cp in recvs:
            cp.wait_recv()
        for cp in sends:
            cp.wait_send()
        for cp in started:
            cp.wait()

    any_spec = pl.BlockSpec(memory_space=pl.ANY)
    out_shape = []
    for a, per_peer in items:
        shp = a.shape if per_peer else (N_DEV,) + a.shape
        out_shape.append(jax.ShapeDtypeStruct(shp, a.dtype))
    return pl.pallas_call(
        body, name=name, in_specs=[any_spec] * n, out_specs=[any_spec] * n, out_shape=out_shape,
        scratch_shapes=[pltpu.SemaphoreType.DMA((n, N_DEV - 1)), pltpu.SemaphoreType.DMA((n, N_DEV - 1)),
                        pltpu.SemaphoreType.DMA((n,))],
    )(*[a for a, _ in items])


def _remote(src, dst, send_sem, recv_sem, device):
    return pltpu.make_async_remote_copy(src_ref=src, dst_ref=dst, send_sem=send_sem, recv_sem=recv_sem,
                                        device_id=device, device_id_type=pl.DeviceIdType.MESH)


def _mesh_place():
    x, y, c = lax.axis_index("x"), lax.axis_index("y"), lax.axis_index("c")
    other_chips = [(1 - x, y), (x, 1 - y), (1 - x, 1 - y)]
    return x, y, c, (x, y, 1 - c), other_chips


def _gather_all(items, name):
    n = len(items)

    def body(*refs):
        src, dst = refs[:n], refs[n:2 * n]
        send_sems, recv_sems, local_sems = refs[2 * n:]
        x, y, c, sibling, chips = _mesh_place()
        slot = lambda px, py, pc: 4 * px + 2 * py + pc
        me = slot(x, y, c)
        own = [pltpu.make_async_copy(src[it], dst[it].at[me], local_sems.at[it]) for it in range(n)]
        first = []
        for it in range(n):
            first.append(_remote(src[it], dst[it].at[me], send_sems.at[it, 0], recv_sems.at[it, 0], sibling))
            for j, chip in enumerate(chips):
                first.append(_remote(src[it], dst[it].at[me], send_sems.at[it, 1 + j], recv_sems.at[it, 1 + j],
                                     (*chip, c)))
        for cp in own + first:
            cp.start()
        passed = []
        for j, chip in enumerate(chips):
            blk = slot(*chip, c)
            for it in range(n):
                _remote(src[it], dst[it].at[blk], send_sems.at[it, 1 + j], recv_sems.at[it, 1 + j],
                        (*chip, c)).wait_recv()
                fwd = _remote(dst[it].at[blk], dst[it].at[blk], send_sems.at[it, 4 + j], recv_sems.at[it, 4 + j],
                              sibling)
                fwd.start()
                passed.append(fwd)
        for it in range(n):
            _remote(src[it], dst[it].at[slot(x, y, 1 - c)], send_sems.at[it, 0], recv_sems.at[it, 0],
                    sibling).wait_recv()
        for j, chip in enumerate(chips):
            for it in range(n):
                _remote(src[it], dst[it].at[slot(*chip, 1 - c)], send_sems.at[it, 4 + j], recv_sems.at[it, 4 + j],
                        sibling).wait_recv()
        for cp in first + passed:
            cp.wait_send()
        for cp in own:
            cp.wait()

    any_spec = pl.BlockSpec(memory_space=pl.ANY)
    return pl.pallas_call(
        body, name=name, in_specs=[any_spec] * n, out_specs=[any_spec] * n,
        out_shape=[jax.ShapeDtypeStruct((N_DEV,) + a.shape, a.dtype) for a in items],
        scratch_shapes=[pltpu.SemaphoreType.DMA((n, 7)), pltpu.SemaphoreType.DMA((n, 7)),
                        pltpu.SemaphoreType.DMA((n,))],
    )(*items)


N_CHIPS = N_DEV // 2


def _sibling_swap(items, name):
    n = len(items)

    def body(*refs):
        g, got = refs[:n], refs[n:2 * n]
        send_sems, recv_sems = refs[2 * n:]
        x, y, c, sibling, _ = _mesh_place()
        sends = [_remote(g[it].at[2 * chip + 1 - c], got[it].at[chip], send_sems.at[it, chip],
                         recv_sems.at[it, chip], sibling) for it in range(n) for chip in range(N_CHIPS)]
        for cp in sends:
            cp.start()
        for cp in sends:
            cp.wait_recv()
        for cp in sends:
            cp.wait_send()

    any_spec = pl.BlockSpec(memory_space=pl.ANY)
    return pl.pallas_call(
        body, name=name, in_specs=[any_spec] * n, out_specs=[any_spec] * n,
        out_shape=[jax.ShapeDtypeStruct((N_CHIPS,) + a.shape[1:], a.dtype) for a in items],
        scratch_shapes=[pltpu.SemaphoreType.DMA((n, N_CHIPS))] * 2,
    )(*items)


def _add_pairs(items, gots):
    n = len(items)

    def body(core_ref, *refs):
        for it in range(n):
            a_ref, b_ref, o_ref = refs[it], refs[n + it], refs[2 * n + it]
            o_ref[...] = (a_ref[...].astype(F32) + b_ref[...].astype(F32)).astype(o_ref.dtype)

    mine = [pl.BlockSpec((None,) + a.shape[1:], lambda i, core: (2 * i + core[0], 0, 0)) for a in items]
    plain = [pl.BlockSpec((None,) + a.shape[1:], lambda i, core: (i, 0, 0)) for a in gots]
    grid_spec = pltpu.PrefetchScalarGridSpec(num_scalar_prefetch=1, grid=(N_CHIPS,), in_specs=mine + plain,
                                             out_specs=plain)
    core = lax.axis_index("c").astype(jnp.int32).reshape(1)
    return pl.pallas_call(body, name="add_sibling_partials", grid_spec=grid_spec,
                          out_shape=[jax.ShapeDtypeStruct(a.shape, a.dtype) for a in gots],
                          compiler_params=_params(1))(core, *items, *gots)


def _chip_scatter(items, name):
    n = len(items)

    def body(*refs):
        s, out = refs[:n], refs[n:2 * n]
        send_sems, recv_sems, local_sems = refs[2 * n:]
        x, y, c, _, _ = _mesh_place()
        my_chip = 2 * x + y
        keeps = [pltpu.make_async_copy(s[it].at[my_chip], out[it].at[my_chip], local_sems.at[it]) for it in range(n)]
        for cp in keeps:
            cp.start()
        sends, recvs = [], []
        for k in range(1, N_CHIPS):
            px = 1 - x if k & 2 else x
            py = 1 - y if k & 1 else y
            peer_chip = 2 * px + py
            for it in range(n):
                cp = _remote(s[it].at[peer_chip], out[it].at[my_chip], send_sems.at[it, k - 1],
                             recv_sems.at[it, k - 1], (px, py, c))
                cp.start()
                sends.append(cp)
                recvs.append(_remote(s[it].at[peer_chip], out[it].at[peer_chip], send_sems.at[it, k - 1],
                                     recv_sems.at[it, k - 1], (px, py, c)))
        for cp in recvs:
            cp.wait_recv()
        for cp in sends:
            cp.wait_send()
        for cp in keeps:
            cp.wait()

    any_spec = pl.BlockSpec(memory_space=pl.ANY)
    return pl.pallas_call(
        body, name=name, in_specs=[any_spec] * n, out_specs=[any_spec] * n,
        out_shape=[jax.ShapeDtypeStruct(a.shape, a.dtype) for a in items],
        scratch_shapes=[pltpu.SemaphoreType.DMA((n, N_CHIPS - 1)), pltpu.SemaphoreType.DMA((n, N_CHIPS - 1)),
                        pltpu.SemaphoreType.DMA((n,))],
    )(*items)


def _gelu_tanh(y):
    k = math.sqrt(2.0 / math.pi)
    t = jnp.tanh(k * (y + 0.044715 * y * y * y))
    return 0.5 * y * (1.0 + t), t


def _local_step(x, mod, target, W, P):
    B, S, D = x.shape
    T = B * S
    TS = 512
    flat = lambda a: a.reshape(T, a.shape[-1])
    unflat = lambda a: a.reshape(B, S, a.shape[-1])
    mod_col = lambda i: (mod, D, i)

    def f_modnorm(xv, sc, sh, g):
        return (xv * _rms_scale(xv) * g) * (1.0 + sc) + sh

    (u1,) = _rowwise(f_modnorm, [(x, D, 0)], [mod_col(1), mod_col(0)], [P["g_mix"]],
                     [(D, BF16)], [], [], ts=TS, name="modnorm_mix")
    u1f = flat(u1)
    qkv = unflat(_matmul(u1f, W["w_qkv"], name="proj_qkv"))
    us = unflat(_matmul(u1f, W["w_us"], name="proj_ssm_in"))
    gates = unflat(_matmul(u1f, W["w_gates"], out_dtype=BF16, name="proj_gates"))

    o_att, lse = _attention_fwd(qkv, P["slopes"])
    y_att = unflat(_matmul(flat(o_att), W["w_proj_att"], out_dtype=BF16, name="proj_att"))

    bu = unflat(_matmul(flat(us), P["bb_big"], name="s5_bu"))
    xs = _scan_fwd(bu, P["a_row"])
    y_mm = unflat(_matmul(flat(xs), P["cc_big"], name="s5_readout"))

    def f_glu(ymm, usv, dsk, wg, bg):
        yv = ymm + dsk * usv
        ge, _ = _gelu_tanh(yv)
        pre = jnp.dot(ge.astype(BF16), wg, preferred_element_type=F32) + bg
        return yv, ge * _sigmoid(pre)

    y_s5, z = _rowwise(f_glu, [(y_mm, SSM_WIDTH, 0), (us, SSM_WIDTH, 0)], [], [P["d_skip"], W["w_glu"], P["b_glu"]],
                       [(SSM_WIDTH, F32), (SSM_WIDTH, BF16)], [], [], ts=TS, name="s5_glu")
    y_ssm = unflat(_matmul(flat(z), W["w_proj_ssm"], out_dtype=BF16, name="proj_ssm"))

    def f_merge(ga, gs, ya, ys, bga, bgs):
        return _sigmoid(ga + bga) * ya + _sigmoid(gs + bgs) * ys

    bga, bgs = P["b_gate"][:, :D], P["b_gate"][:, D:]
    (merged,) = _rowwise(f_merge, [(gates, D, 0), (gates, D, 1), (y_att, D, 0), (y_ssm, D, 0)], [], [bga, bgs],
                         [(D, BF16)], [], [], ts=TS, name="gate_merge")
    mix = unflat(_matmul(flat(merged), W["w_out"], name="proj_out"))

    def f_res_modnorm(xv, mx, gt, sc, sh, g):
        h = xv + gt * mx
        return h, (h * _rms_scale(h) * g) * (1.0 + sc) + sh

    h1, u2 = _rowwise(f_res_modnorm, [(x, D, 0), (mix, D, 0)], [mod_col(2), mod_col(4), mod_col(3)], [P["g_ffn"]],
                      [(D, F32), (D, BF16)], [], [], ts=TS, name="residual_modnorm_ffn")
    up = unflat(_up_fwd(flat(u2), W["w_up"], name="ffn_up"))
    ff = _conv_fwd(up, P["w_conv"], P["b_conv"])
    down = unflat(_matmul(flat(ff), W["w_down"], name="ffn_down"))

    def f_head(h1v, dn, tg, gt, g):
        h2 = h1v + gt * dn
        r = _rms_scale(h2)
        nh = h2 * r
        e = nh * g - tg
        dy = e * (1.0 / D)
        gy = dy * g
        dh = r * (gy - nh * jnp.mean(gy * nh, axis=-1, keepdims=True))
        return (dh, dh * gt, _col_sum(dh * dn), _col_sum(dy * nh), _col_sum(e * e) * (0.5 / D))

    dh2, d_down, d_gt2, d_g_final, loss_cols = _rowwise(
        f_head, [(h1, D, 0), (down, D, 0), (target, D, 0)], [mod_col(5)], [P["g_final"]],
        [(D, F32), (D, BF16)], [D], [(1, D), (1, D)], ts=TS, name="head_loss")

    d_downf = flat(d_down)
    d_ff = unflat(_matmul(d_downf, W["w_down"], tb=True, out_dtype=BF16, name="ffn_down_dx"))
    d_w_down = _matmul(flat(ff), d_downf, ta=True, out_dtype=BF16, name="ffn_down_dw")
    d_up, d_w_conv, d_b_conv = _conv_bwd(up, d_ff, P["w_conv"], P["b_conv"])
    d_upf = flat(d_up)
    d_u2 = unflat(_up_dx(d_upf, W["w_up"], name="ffn_up_dx"))
    d_w_up = _up_dw(flat(u2), d_upf, name="ffn_up_dw")

    def f_modnorm_bwd(du, h, dres, mx, sc, gt, g):
        r = _rms_scale(h)
        nh = h * r
        dn = du * (1.0 + sc)
        gy = dn * g
        dh = dres + r * (gy - nh * jnp.mean(gy * nh, axis=-1, keepdims=True))
        return (dh, dh * gt, _col_sum(du), _col_sum(du * nh * g), _col_sum(dh * mx), _col_sum(dn * nh))

    dh1, d_mix, d_sh2, d_sc2, d_gt1, d_g_ffn = _rowwise(
        f_modnorm_bwd, [(d_u2, D, 0), (h1, D, 0), (dh2, D, 0), (mix, D, 0)], [mod_col(4), mod_col(2)], [P["g_ffn"]],
        [(D, F32), (D, BF16)], [D, D, D], [(1, D)], ts=TS, name="modnorm_ffn_bwd")

    d_mixf = flat(d_mix)
    d_merged = unflat(_matmul(d_mixf, W["w_out"], tb=True, out_dtype=BF16, name="proj_out_dx"))
    d_w_out = _matmul(flat(merged), d_mixf, ta=True, out_dtype=BF16, name="proj_out_dw")

    def f_merge_bwd(dm, ga, gs, ya, ys, bga_, bgs_):
        sa, ss = _sigmoid(ga + bga_), _sigmoid(gs + bgs_)
        dga = dm * ya * sa * (1.0 - sa)
        dgs = dm * ys * ss * (1.0 - ss)
        return dm * sa, dm * ss, jnp.concatenate([dga, dgs], axis=1), _col_sum(dga), _col_sum(dgs)

    d_y_att, d_y_ssm, d_gates, d_bga, d_bgs = _rowwise(
        f_merge_bwd, [(d_merged, D, 0), (gates, D, 0), (gates, D, 1), (y_att, D, 0), (y_ssm, D, 0)], [], [bga, bgs],
        [(D, BF16), (D, BF16), (2 * D, BF16)], [], [(1, D), (1, D)], ts=TS, name="gate_merge_bwd")

    d_yaf, d_ysf = flat(d_y_att), flat(d_y_ssm)
    d_o_att = unflat(_matmul(d_yaf, W["w_proj_att"], tb=True, name="proj_att_dx"))
    d_w_proj_att = _matmul(flat(o_att), d_yaf, ta=True, out_dtype=BF16, name="proj_att_dw")
    d_z = unflat(_matmul(d_ysf, W["w_proj_ssm"], tb=True, out_dtype=BF16, name="proj_ssm_dx"))
    d_w_proj_ssm = _matmul(flat(z), d_ysf, ta=True, out_dtype=BF16, name="proj_ssm_dw")

    def f_glu_bwd(yv, dz, usv, dsk, wg, bg):
        ge, t = _gelu_tanh(yv)
        pre = jnp.dot(ge.astype(BF16), wg, preferred_element_type=F32) + bg
        sg = _sigmoid(pre)
        dpre = dz * ge * sg * (1.0 - sg)
        dge = dz * sg + lax.dot_general(dpre.astype(BF16), wg, _NT, preferred_element_type=F32)
        k = math.sqrt(2.0 / math.pi)
        dgelu = 0.5 * (1.0 + t) + 0.5 * yv * (1.0 - t * t) * k * (1.0 + 3.0 * 0.044715 * yv * yv)
        dy = dge * dgelu
        dwg = lax.dot_general(ge.astype(BF16), dpre.astype(BF16), _TN, preferred_element_type=F32)
        return dy, dy * dsk, dwg, _col_sum(dpre), _col_sum(dy * usv)

    d_y_s5, d_us_skip, d_w_glu, d_b_glu, d_d_skip = _rowwise(
        f_glu_bwd, [(y_s5, SSM_WIDTH, 0), (d_z, SSM_WIDTH, 0), (us, SSM_WIDTH, 0)], [],
        [P["d_skip"], W["w_glu"], P["b_glu"]],
        [(SSM_WIDTH, BF16), (SSM_WIDTH, F32)], [], [(SSM_WIDTH, SSM_WIDTH), (1, SSM_WIDTH), (1, SSM_WIDTH)],
        ts=TS, name="s5_glu_bwd")
    d_ysf2 = flat(d_y_s5)
    dxs = unflat(_matmul(d_ysf2, P["cc_big"], tb=True, name="s5_readout_dx"))
    d_cc = _matmul(flat(xs), d_ysf2, ta=True, name="s5_readout_dw")
    lam, g_ab = _scan_bwd(dxs, xs, P["a_row"])
    lam = flat(lam)
    d_us_mm = unflat(_matmul(lam, P["bb_big"], tb=True, name="s5_bu_dx"))
    d_bb = _matmul(flat(us), lam, ta=True, name="s5_bu_dw")

    d_qkv = _attention_bwd(qkv, o_att, d_o_att, lse, P["slopes"])

    def f_add(a, b_):
        return a + b_

    (d_us,) = _rowwise(f_add, [(d_us_mm, SSM_WIDTH, 0), (d_us_skip, SSM_WIDTH, 0)], [], [],
                       [(SSM_WIDTH, BF16)], [], [], ts=TS, name="s5_input_grad")
    d_qkvf = flat(d_qkv)
    d_usf = flat(d_us)
    d_gatesf = flat(d_gates)
    d_u1 = (_matmul(d_qkvf, W["w_qkv"], tb=True, out_dtype=BF16, name="proj_qkv_dx"),
            _matmul(d_usf, W["w_us"], tb=True, out_dtype=BF16, name="proj_ssm_in_dx"),
            _matmul(d_gatesf, W["w_gates"], tb=True, out_dtype=BF16, name="proj_gates_dx"))
    d_w_in = jnp.concatenate(
        [_unpair_qkv_columns(_matmul(u1f, d_qkvf, ta=True, out_dtype=BF16, name="proj_qkv_dw")),
         _matmul(u1f, d_usf, ta=True, out_dtype=BF16, name="proj_ssm_in_dw"),
         _matmul(u1f, d_gatesf, ta=True, out_dtype=BF16, name="proj_gates_dw")], axis=1)

    def f_modnorm_bwd_in(du0, du1, du2, h, dres, sc, g):
        du = du0 + du1 + du2
        r = _rms_scale(h)
        nh = h * r
        dn = du * (1.0 + sc)
        gy = dn * g
        dh = dres + r * (gy - nh * jnp.mean(gy * nh, axis=-1, keepdims=True))
        return (dh, _col_sum(du), _col_sum(du * nh * g), _col_sum(dn * nh))

    grad_x, d_sh1, d_sc1, d_g_mix = _rowwise(
        f_modnorm_bwd_in, [(unflat(d_u1[0]), D, 0), (unflat(d_u1[1]), D, 0), (unflat(d_u1[2]), D, 0), (x, D, 0),
                           (dh1, D, 0)], [mod_col(1)], [P["g_mix"]],
        [(D, F32)], [D, D], [(1, D)], ts=TS, name="modnorm_mix_bwd")

    d_mod = jnp.concatenate([d_sh1, d_sc1, d_gt1, d_sh2, d_sc2, d_gt2], axis=-1)
    big = dict(w_in=d_w_in, w_glu=d_w_glu, w_proj_att=d_w_proj_att, w_proj_ssm=d_w_proj_ssm, w_out=d_w_out,
               w_up_slots=d_w_up, w_conv=d_w_conv, w_down=d_w_down)
    g_ab_re, g_ab_im = _deinterleave(g_ab)
    d_bb_re, d_bb_im = _deinterleave(d_bb)
    d_cc_re, d_cc_im = (t.T for t in _deinterleave(d_cc.T))
    small = dict(g_mix=d_g_mix, b_gate=jnp.concatenate([d_bga, d_bgs], axis=1), g_ab_re=g_ab_re, g_ab_im=g_ab_im,
                 d_bb_re=d_bb_re, d_bb_im=d_bb_im, d_cc_re=d_cc_re, d_cc_im=d_cc_im, d_skip=d_d_skip,
                 b_glu=d_b_glu, g_ffn=d_g_ffn, b_conv=d_b_conv, g_final=d_g_final, loss_cols=loss_cols)
    return grad_x, d_mod, big, small


def _block_diag_in(bb):
    t = bb.reshape(SSM_GROUPS, SSM_STATE, SSM_GROUP_CH)
    eye = jnp.eye(SSM_GROUPS, dtype=bb.dtype)
    return jnp.einsum("gnc,gh->gchn", t, eye).reshape(SSM_WIDTH, SSM_COLS)


def _block_diag_out(cm):
    eye = jnp.eye(SSM_GROUPS, dtype=cm.dtype)
    return jnp.einsum("gcn,gh->gnhc", cm, eye).reshape(SSM_COLS, SSM_WIDTH)


def _diag_blocks_in(m):
    t = m.reshape(SSM_GROUPS, SSM_GROUP_CH, SSM_GROUPS, SSM_STATE)
    idx = jnp.arange(SSM_GROUPS)
    return t[idx, :, idx, :].transpose(0, 2, 1).reshape(SSM_COLS, SSM_GROUP_CH)


def _diag_blocks_out(m):
    t = m.reshape(SSM_GROUPS, SSM_STATE, SSM_GROUPS, SSM_GROUP_CH)
    idx = jnp.arange(SSM_GROUPS)
    return t[idx, :, idx, :].transpose(0, 2, 1)


def _pair_qkv_columns(w):
    lead = w.shape[:-1]
    return w.reshape(lead + (3, N_HEADS // 2, LANES)).swapaxes(-3, -2).reshape(lead + (3 * ATT_WIDTH,))


def _unpair_qkv_columns(w):
    lead = w.shape[:-1]
    return w.reshape(lead + (N_HEADS // 2, 3, LANES)).swapaxes(-3, -2).reshape(lead + (3 * ATT_WIDTH,))


def _interleave(re, im):
    lead = re.shape[:-1]
    g = lambda a: a.reshape(lead + (SSM_COLS // SCAN_COLS, 1, SCAN_COLS))
    return jnp.concatenate([g(re), g(im)], axis=-2).reshape(lead + (2 * SSM_COLS,))


def _deinterleave(x):
    lead = x.shape[:-1]
    t = x.reshape(lead + (SSM_COLS // SCAN_COLS, 2, SCAN_COLS))
    return t[..., 0, :].reshape(lead + (SSM_COLS,)), t[..., 1, :].reshape(lead + (SSM_COLS,))


def _cols_to_slots(g):
    R = g.shape[0]
    return g.reshape(R, N_DEV, g.shape[1] // N_DEV).transpose(1, 0, 2)


def _slots_to_cols(g):
    return g.transpose(1, 0, 2).reshape(g.shape[1], N_DEV * g.shape[2])


SMALL_ORDER = ("b_ada", "g_mix", "b_gate", "a_re", "a_im", "log_dt", "b_re", "b_im", "c_re", "c_im", "d_skip",
               "b_glu", "g_ffn", "b_conv", "g_final")


def _pack(arrs):
    pieces, offs, row = [], [], 0
    for a in arrs:
        f = a.reshape(-1).astype(F32)
        n = f.shape[0]
        rows = -(-n // LANES)
        pieces.append(jnp.pad(f, (0, rows * LANES - n)))
        offs.append((row, n))
        row += rows
    return jnp.concatenate(pieces).reshape(row, LANES), offs


def _unpack(packed, offs, shapes):
    flat = packed.reshape(-1)
    return [flat[r * LANES:r * LANES + n].reshape(s) for (r, n), s in zip(offs, shapes)]


def kernel(x, c, w_ada, b_ada, g_mix, w_in, b_gate, a_re, a_im, log_dt, b_re, b_im, c_re, c_im, d_skip, w_glu, b_glu, w_proj_att, w_proj_ssm, w_out, g_ffn, w_up, w_conv, b_conv, w_down, g_final, loss_target, m_w_ada, m_b_ada, m_g_mix, m_w_in, m_b_gate, m_a_re, m_a_im, m_log_dt, m_b_re, m_b_im, m_c_re, m_c_im, m_d_skip, m_w_glu, m_b_glu, m_w_proj_att, m_w_proj_ssm, m_w_out, m_g_ffn, m_w_up, m_w_conv, m_b_conv, m_w_down, m_g_final, v_w_ada, v_b_ada, v_g_mix, v_w_in, v_b_gate, v_a_re, v_a_im, v_log_dt, v_b_re, v_b_im, v_c_re, v_c_im, v_d_skip, v_w_glu, v_b_glu, v_w_proj_att, v_w_proj_ssm, v_w_out, v_g_ffn, v_w_up, v_w_conv, v_b_conv, v_w_down, v_g_final):
    args = dict(locals())
    B, S, D = x.shape
    me = 4 * lax.axis_index("x") + 2 * lax.axis_index("y") + lax.axis_index("c")
    bf = lambda w: w[0].astype(BF16)

    gathered = _gather_all([c, bf(w_in), bf(w_glu), bf(w_proj_att), bf(w_proj_ssm), bf(w_out), bf(w_up), w_conv[0],
                            bf(w_down)], name="gather_weights")
    c_all = gathered[0].reshape(N_DEV * B, D)
    w_in_full = _slots_to_cols(gathered[1])
    n_qkv = 3 * ATT_WIDTH
    W = dict(w_qkv=_pair_qkv_columns(w_in_full[:, :n_qkv]), w_us=w_in_full[:, n_qkv:n_qkv + SSM_WIDTH],
             w_gates=w_in_full[:, n_qkv + SSM_WIDTH:], w_glu=gathered[2].reshape(SSM_WIDTH, SSM_WIDTH),
             w_proj_att=_slots_to_cols(gathered[3]), w_proj_ssm=_slots_to_cols(gathered[4]),
             w_out=gathered[5].reshape(D, D), w_up=gathered[6],
             w_down=gathered[8].reshape(D_FF, D))
    w_conv_full = _slots_to_cols(gathered[7])

    n_ada = w_ada.shape[2]
    b_ada_cols = lax.dynamic_slice(b_ada, (0, me * n_ada), (1, n_ada))
    mod_part = _ada_fwd(c_all, w_ada[0], b_ada_cols)
    (mod_slots,) = _exchange([(mod_part.reshape(N_DEV, B, n_ada), True)], name="scatter_modulation")
    mod = mod_slots.transpose(1, 0, 2).reshape(B, 1, 6 * D)

    ab_re, ab_im, f_re, f_im = _s5_params(a_re[0], a_im[0], log_dt[0].reshape(SSM_GROUPS, 1))
    col = lambda a: a.reshape(SSM_COLS, 1)
    b_re2, b_im2 = b_re[0].reshape(SSM_COLS, SSM_GROUP_CH), b_im[0].reshape(SSM_COLS, SSM_GROUP_CH)
    bb_re, bb_im = _s5_input_matrix(col(f_re), col(f_im), b_re2, b_im2)
    slopes = jnp.asarray([2.0 ** (-8.0 * (h + 1) / N_HEADS) for h in range(N_HEADS)], F32)
    P = dict(g_mix=g_mix, g_ffn=g_ffn, g_final=g_final.reshape(1, D), b_gate=b_gate, d_skip=d_skip, b_glu=b_glu,
             b_conv=b_conv, w_conv=w_conv_full, slopes=slopes,
             a_row=_interleave(ab_re.reshape(1, SSM_COLS), ab_im.reshape(1, SSM_COLS)),
             bb_big=_interleave(_block_diag_in(bb_re), _block_diag_in(bb_im)),
             cc_big=_interleave(_block_diag_out(c_re[0]).T, -_block_diag_out(c_im[0]).T).T)

    grad_x, d_mod, big, small = _local_step(x, mod, loss_target, W, P)

    small_list = [small["loss_cols"], small["g_mix"], small["b_gate"], small["g_ab_re"], small["g_ab_im"],
                  _diag_blocks_in(small["d_bb_re"]), _diag_blocks_in(small["d_bb_im"]),
                  _diag_blocks_out(small["d_cc_re"]), -_diag_blocks_out(small["d_cc_im"]),
                  small["g_ffn"], small["b_conv"], small["g_final"], small["d_skip"], small["b_glu"]]
    small_packed, small_offs = _pack(small_list)
    sharded = ("w_in", "w_glu", "w_proj_att", "w_proj_ssm", "w_out", "w_up", "w_conv", "w_down")
    slots = [_cols_to_slots(big["w_in"]), big["w_glu"].astype(BF16).reshape(N_DEV, SSM_WIDTH // N_DEV, SSM_WIDTH),
             _cols_to_slots(big["w_proj_att"]), _cols_to_slots(big["w_proj_ssm"]),
             big["w_out"].reshape(N_DEV, D // N_DEV, D), big["w_up_slots"],
             _cols_to_slots(big["w_conv"].astype(BF16)), big["w_down"].reshape(N_DEV, D_FF // N_DEV, D)]
    from_sibling = _sibling_swap(slots, name="swap_sibling_gradients")
    chip_sums = _chip_scatter(_add_pairs(slots, from_sibling), name="scatter_chip_gradients")
    small_all, dmod_slots = _gather_all([small_packed, d_mod.reshape(B, 6 * D)], name="gather_small_gradients")

    out = {}

    def update(name, parts):
        w2 = args[name][0]
        g, dl, mn, vn = _adamw(w2, args["m_" + name][0], args["v_" + name][0], parts, name="adamw_" + name)
        for key, val in (("grad_", g), ("delta_", dl), ("new_m_", mn), ("new_v_", vn)):
            out[key + name] = val[None]

    for name, parts in zip(sharded, chip_sums):
        update(name, parts)

    dmod_all = dmod_slots.reshape(N_DEV * B, 6 * D)
    dmod_cols = lax.dynamic_slice(dmod_all, (0, me * n_ada), (N_DEV * B, n_ada))
    d_w_ada, d_b_ada = _ada_bwd(c_all, dmod_all, dmod_cols)
    update("w_ada", d_w_ada[None])

    loss_row, loss_n = small_offs[0]
    small_sum, loss_vec = _sum_parts(small_all, (loss_row, loss_row + loss_n // LANES))
    shapes = [(1, D), (1, D), (1, 2 * D), (SSM_GROUPS, SSM_STATE), (SSM_GROUPS, SSM_STATE), (SSM_COLS, SSM_GROUP_CH),
              (SSM_COLS, SSM_GROUP_CH), (1, SSM_GROUPS, SSM_GROUP_CH, SSM_STATE),
              (1, SSM_GROUPS, SSM_GROUP_CH, SSM_STATE), (1, D), (1, D_FF), (D,), (1, SSM_WIDTH), (1, SSM_WIDTH)]
    (_, s_g_mix, s_b_gate, s_ab_re, s_ab_im, s_bb_re, s_bb_im, s_c_re, s_c_im, s_g_ffn, s_b_conv, s_g_final,
     s_d_skip, s_b_glu) = _unpack(small_sum, small_offs, shapes)
    d_b_re2, d_b_im2, d_f_re, d_f_im = _s5_input_matrix_bwd(col(f_re), col(f_im), b_re2, b_im2, s_bb_re, s_bb_im)
    d_a_re, d_a_im, d_log_dt = _s5_params_bwd(a_re[0], a_im[0], log_dt[0].reshape(SSM_GROUPS, 1), s_ab_re, s_ab_im,
                                              d_f_re.reshape(SSM_GROUPS, SSM_STATE),
                                              d_f_im.reshape(SSM_GROUPS, SSM_STATE))
    grads_small = dict(b_ada=d_b_ada, g_mix=s_g_mix, b_gate=s_b_gate, a_re=d_a_re[None], a_im=d_a_im[None],
                       log_dt=d_log_dt.reshape(1, SSM_GROUPS), b_re=d_b_re2.reshape(b_re.shape),
                       b_im=d_b_im2.reshape(b_im.shape), c_re=s_c_re, c_im=s_c_im, d_skip=s_d_skip, b_glu=s_b_glu,
                       g_ffn=s_g_ffn, b_conv=s_b_conv, g_final=s_g_final)
    w_pack, offs = _pack([args[n] for n in SMALL_ORDER])
    m_pack, _ = _pack([args["m_" + n] for n in SMALL_ORDER])
    v_pack, _ = _pack([args["v_" + n] for n in SMALL_ORDER])
    g_pack, _ = _pack([grads_small[n] for n in SMALL_ORDER])
    res = _adamw(w_pack, m_pack, v_pack, g_pack[None], name="adamw_small")
    shapes_small = [args[n].shape for n in SMALL_ORDER]
    for key, packed in zip(("grad_", "delta_", "new_m_", "new_v_"), res):
        for n, val in zip(SMALL_ORDER, _unpack(packed, offs, shapes_small)):
            out[key + n] = val

    order = ["w_ada", "b_ada", "g_mix", "w_in", "b_gate", "a_re", "a_im", "log_dt", "b_re", "b_im", "c_re", "c_im",
             "d_skip", "w_glu", "b_glu", "w_proj_att", "w_proj_ssm", "w_out", "g_ffn", "w_up", "w_conv", "b_conv",
             "w_down", "g_final"]
    loss = loss_vec[0, 0]
    return (loss, grad_x, *[out[k + n] for k in ("grad_", "delta_", "new_m_", "new_v_") for n in order])
```

```python
import math

import jax
import jax.numpy as jnp
from jax import lax
from jax.experimental import pallas as pl
from jax.experimental.pallas import tpu as pltpu

F32 = jnp.float32
BF16 = jnp.bfloat16

N_DEV = 8
D_MODEL = 1024
N_HEADS = 8
HEAD_DIM = 64
ATT_WIDTH = N_HEADS * HEAD_DIM
DILATIONS = (1, 4, 16)
WIN = 128
SSM_GROUPS = 16
SSM_GROUP_CH = 16
SSM_WIDTH = SSM_GROUPS * SSM_GROUP_CH
SSM_STATE = 64
SSM_COLS = SSM_GROUPS * SSM_STATE
D_FF = 2048
EPS = 1e-6
NEG_INF = -1e30
ADAM_LR, ADAM_B1, ADAM_B2, ADAM_EPS, ADAM_WD, ADAM_STEP = 0.001, 0.9, 0.999, 1e-08, 0.01, 10

V7X_VMEM_LIMIT = 56 * 1024 * 1024
LANES = 128


def _params(n_grid):
    return pltpu.CompilerParams(dimension_semantics=("arbitrary",) * n_grid,
                                vmem_limit_bytes=V7X_VMEM_LIMIT)


def _tile(n, pref):
    if n <= pref:
        return n
    t = (pref // LANES) * LANES
    while t > 0:
        if n % t == 0:
            return t
        t -= LANES
    return n


def _matmul(a, b, *, ta=False, tb=False, out_dtype=F32, name):
    if ta:
        K, M = a.shape
    else:
        M, K = a.shape
    if tb:
        N, K2 = b.shape
    else:
        K2, N = b.shape
    assert K == K2, (a.shape, b.shape)
    if ta:
        tm, tn, tk = _tile(M, 1024), _tile(N, 2048), _tile(K, 512)
    else:
        tm, tk = _tile(M, 512), _tile(K, 4096)
        tn = _tile(N, 2048 if K <= 2048 else 1024)
    nk = K // tk
    dn = (((0,) if ta else (1,), (1,) if tb else (0,)), ((), ()))

    def body(a_ref, b_ref, o_ref, acc_ref):
        k = pl.program_id(2)
        part = lax.dot_general(a_ref[...].astype(BF16), b_ref[...].astype(BF16), dn, preferred_element_type=F32)
        if nk == 1:
            o_ref[...] = part.astype(o_ref.dtype)
            return

        @pl.when(k == 0)
        def _():
            acc_ref[...] = jnp.zeros_like(acc_ref)

        acc_ref[...] += part

        @pl.when(k == nk - 1)
        def _():
            o_ref[...] = acc_ref[...].astype(o_ref.dtype)

    a_spec = (pl.BlockSpec((tk, tm), lambda j, i, k: (k, i)) if ta
              else pl.BlockSpec((tm, tk), lambda j, i, k: (i, k)))
    b_spec = (pl.BlockSpec((tn, tk), lambda j, i, k: (j, k)) if tb
              else pl.BlockSpec((tk, tn), lambda j, i, k: (k, j)))
    return pl.pallas_call(
        body, name=name, grid=(N // tn, M // tm, nk),
        in_specs=[a_spec, b_spec],
        out_specs=pl.BlockSpec((tm, tn), lambda j, i, k: (i, j)),
        out_shape=jax.ShapeDtypeStruct((M, N), out_dtype),
        scratch_shapes=[pltpu.VMEM((tm, tn) if nk > 1 else (8, LANES), F32)],
        compiler_params=_params(3),
    )(a, b)


HALF = 256
UP_SLOTS = N_DEV // 2
UP_GROUP = 4 * HALF


def _group_weight(w_ref):
    return jnp.concatenate([w_ref[0, :, :HALF], w_ref[1, :, :HALF], w_ref[0, :, HALF:], w_ref[1, :, HALF:]], axis=1)


def _up_weight_spec(K, index):
    return pl.BlockSpec((2, None, K, 2 * HALF), index)


def _up_fwd(a, w3, name):
    M, K = a.shape
    tm = _tile(M, 1024)

    def body(a_ref, w_ref, o_ref):
        o_ref[...] = jnp.dot(a_ref[...].astype(BF16), _group_weight(w_ref),
                             preferred_element_type=F32).astype(o_ref.dtype)

    return pl.pallas_call(
        body, name=name, grid=(UP_SLOTS, M // tm),
        in_specs=[pl.BlockSpec((tm, K), lambda j, i: (i, 0)), _up_weight_spec(K, lambda j, i: (0, j, 0, 0))],
        out_specs=pl.BlockSpec((tm, UP_GROUP), lambda j, i: (i, j)),
        out_shape=jax.ShapeDtypeStruct((M, UP_SLOTS * UP_GROUP), BF16), compiler_params=_params(2),
    )(a, w3.reshape(2, UP_SLOTS, K, 2 * HALF))


def _up_dx(d, w3, name):
    M = d.shape[0]
    K = w3.shape[1]
    tm = _tile(M, 1024)

    def body(d_ref, w_ref, o_ref, acc_ref):
        j = pl.program_id(1)

        @pl.when(j == 0)
        def _():
            acc_ref[...] = jnp.zeros_like(acc_ref)

        acc_ref[...] += lax.dot_general(d_ref[...], _group_weight(w_ref), _NT, preferred_element_type=F32)

        @pl.when(j == UP_SLOTS - 1)
        def _():
            o_ref[...] = acc_ref[...].astype(o_ref.dtype)

    return pl.pallas_call(
        body, name=name, grid=(M // tm, UP_SLOTS),
        in_specs=[pl.BlockSpec((tm, UP_GROUP), lambda i, j: (i, j)), _up_weight_spec(K, lambda i, j: (0, j, 0, 0))],
        out_specs=pl.BlockSpec((tm, K), lambda i, j: (i, 0)),
        out_shape=jax.ShapeDtypeStruct((M, K), BF16), scratch_shapes=[pltpu.VMEM((tm, K), F32)],
        compiler_params=_params(2),
    )(d, w3.reshape(2, UP_SLOTS, K, 2 * HALF))


def _up_dw(a, d, name):
    M, K = a.shape
    tk = _tile(M, 512)
    nk = M // tk

    def body(a_ref, d_ref, o_ref, acc_ref):
        k = pl.program_id(1)

        @pl.when(k == 0)
        def _():
            acc_ref[...] = jnp.zeros_like(acc_ref)

        acc_ref[...] += lax.dot_general(a_ref[...], d_ref[...], _TN, preferred_element_type=F32)

        @pl.when(k == nk - 1)
        def _():
            for half in range(2):
                for part in range(2):
                    lo = (2 * half + part) * HALF
                    o_ref[part, :, half * HALF:(half + 1) * HALF] = acc_ref[:, lo:lo + HALF].astype(o_ref.dtype)

    out = pl.pallas_call(
        body, name=name, grid=(UP_SLOTS, nk),
        in_specs=[pl.BlockSpec((tk, K), lambda j, k: (k, 0)), pl.BlockSpec((tk, UP_GROUP), lambda j, k: (k, j))],
        out_specs=_up_weight_spec(K, lambda j, k: (0, j, 0, 0)),
        out_shape=jax.ShapeDtypeStruct((2, UP_SLOTS, K, 2 * HALF), BF16),
        scratch_shapes=[pltpu.VMEM((K, UP_GROUP), F32)], compiler_params=_params(2),
    )(a, d)
    return out.reshape(N_DEV, K, 2 * HALF)


def _rowwise(fn, rows, bvecs, consts, out_rows, out_b, out_g, *, ts, name):
    B, S = rows[0][0].shape[:2]
    nin = len(rows) + len(bvecs) + len(consts)
    nr, nb, ng = len(out_rows), len(out_b), len(out_g)

    def body(*refs):
        b = pl.program_id(0)
        s = pl.program_id(1)
        vals = [r[...] for r in refs[:nin]]
        vals[:len(rows)] = [v.astype(F32) for v in vals[:len(rows)]]
        outs = fn(*vals)
        if not isinstance(outs, (tuple, list)):
            outs = (outs,)
        orefs = refs[nin:]
        for i in range(nr):
            orefs[i][...] = outs[i].astype(orefs[i].dtype)
        for i in range(nb):
            ref = orefs[nr + i]

            @pl.when(s == 0)
            def _(ref=ref):
                ref[...] = jnp.zeros_like(ref)

            ref[...] += outs[nr + i]
        for i in range(ng):
            ref = orefs[nr + nb + i]

            @pl.when((s == 0) & (b == 0))
            def _(ref=ref):
                ref[...] = jnp.zeros_like(ref)

            ref[...] += outs[nr + nb + i]

    in_specs = ([pl.BlockSpec((None, ts, cb), lambda b, s, ci=ci: (b, s, ci)) for (_, cb, ci) in rows]
                + [pl.BlockSpec((None, 1, cb), lambda b, s, ci=ci: (b, 0, ci)) for (_, cb, ci) in bvecs]
                + [pl.BlockSpec(a.shape, lambda b, s: (0, 0)) for a in consts])
    out_shape = ([jax.ShapeDtypeStruct((B, S, c), dt) for (c, dt) in out_rows]
                 + [jax.ShapeDtypeStruct((B, 1, c), F32) for c in out_b]
                 + [jax.ShapeDtypeStruct(rc, F32) for rc in out_g])
    out_specs = ([pl.BlockSpec((None, ts, c), lambda b, s: (b, s, 0)) for (c, _) in out_rows]
                 + [pl.BlockSpec((None, 1, c), lambda b, s: (b, 0, 0)) for c in out_b]
                 + [pl.BlockSpec(rc, lambda b, s: (0, 0)) for rc in out_g])
    args = [a for (a, _, _) in rows] + [a for (a, _, _) in bvecs] + list(consts)
    return pl.pallas_call(
        body, name=name, grid=(B, S // ts), in_specs=in_specs, out_specs=out_specs,
        out_shape=out_shape, compiler_params=_params(2),
    )(*args)


def _col_sum(v):
    return jnp.sum(v, axis=0, keepdims=True)


def _rms_scale(h):
    return lax.rsqrt(jnp.mean(h * h, axis=-1, keepdims=True) + EPS)


def _sigmoid(v):
    return 1.0 / (1.0 + jnp.exp(-v))


ATT_SCALE = HEAD_DIM ** -0.5
COPY_ROWS = 256
_NT = (((1,), (1,)), ((), ()))
_TN = (((0,), (0,)), ((), ()))


def _row_chunks(d, seq):
    sub = seq // d
    out = []
    for r in range(d):
        for c0 in range(0, sub, COPY_ROWS):
            n = min(COPY_ROWS, sub - c0)
            out.append((pl.ds(r + c0 * d, n, stride=d), r * sub + c0, n))
    return out


ATT_UNROLL = 8
KEYS = 2 * WIN


def _zero_once(refs):
    @pl.when((pl.program_id(0) == 0) & (pl.program_id(1) == 0))
    def _():
        for r in refs:
            r[...] = jnp.zeros_like(r)


def _pair_bias(bias_ref, slopes_ref, hp, d, key_major):
    shape = (KEYS, WIN) if key_major else (WIN, KEYS)
    qi = lax.broadcasted_iota(jnp.int32, shape, 1 if key_major else 0)
    kj = lax.broadcasted_iota(jnp.int32, shape, 0 if key_major else 1)
    dist = WIN + qi - kj
    valid = (dist >= 0) & (dist <= WIN)
    distf = dist.astype(F32)
    for h in range(2):
        slope_d = slopes_ref[2 * hp + h] * float(d)
        with_prev = jnp.where(valid, -(slope_d * distf), NEG_INF)
        no_prev = jnp.where(kj >= WIN, with_prev, NEG_INF)
        span = slice(h * KEYS, (h + 1) * KEYS)
        if key_major:
            bias_ref[1, span, :] = with_prev
            bias_ref[0, span, :] = no_prev
        else:
            bias_ref[1, :, span] = with_prev
            bias_ref[0, :, span] = no_prev


def _stack_heads(v):
    first = lax.broadcasted_iota(jnp.int32, v.shape, 1) < HEAD_DIM
    zero = jnp.zeros_like(v)
    return jnp.concatenate([jnp.where(first, v, zero), jnp.where(first, zero, v)], axis=0)


def _per_head(c0, c1, n):
    return jnp.where(lax.broadcasted_iota(jnp.int32, (n, LANES), 1) < HEAD_DIM, c0, c1)


def _qkv_spec(seq, j):
    return pl.BlockSpec((None, seq, LANES), lambda b, hp: (b, 0, 3 * hp + j))


def _attention_fwd(qkv, slopes):
    B, S, _ = qkv.shape
    n_blk = S // WIN
    n_pair = N_HEADS // 2

    def body(slopes_ref, q_ref, k_ref, v_ref, o_ref, lse_ref, qp, kp, vp, bias, acc, mx, sm, acc_n, mx_n, sm_n):
        hp = pl.program_id(1)
        _zero_once((kp, vp))
        for p, d in enumerate(DILATIONS):
            nb = n_blk // d
            chunks = _row_chunks(d, S)
            for src, dst, n in chunks:
                qp[dst:dst + n, :] = (q_ref[src, :] * ATT_SCALE).astype(BF16)
                kp[WIN + dst:WIN + dst + n, :] = k_ref[src, :].astype(BF16)
                vp[WIN + dst:WIN + dst + n, :] = v_ref[src, :].astype(BF16)
            _pair_bias(bias, slopes_ref, hp, d, key_major=False)
            acc_t, mx_t, sm_t = (acc_n, mx_n, sm_n) if d == 1 else (acc, mx, sm)

            def block(i, carry, p=p, nb=nb, acc_t=acc_t, mx_t=mx_t, sm_t=sm_t):
                cur = pl.ds(pl.multiple_of(i * WIN, WIN), WIN)
                keys = pl.ds(pl.multiple_of(i * WIN, WIN), KEYS)
                flag = ((i % nb) > 0).astype(jnp.int32)
                s = lax.dot_general(qp[cur, :], _stack_heads(kp[keys, :]), _NT, preferred_element_type=F32)
                s = s + bias[flag]
                es, ms, ls = [], [], []
                for h in range(2):
                    sh = s[:, h * KEYS:(h + 1) * KEYS]
                    m = jnp.max(jnp.maximum(sh[:, :WIN], sh[:, WIN:]), axis=1, keepdims=True)
                    e = jnp.exp(sh - m)
                    es.append(e.astype(BF16))
                    ms.append(m)
                    ls.append(jnp.sum(e[:, :WIN] + e[:, WIN:], axis=1, keepdims=True))
                acc_t[p, cur, :] = jnp.dot(jnp.concatenate(es, axis=1), _stack_heads(vp[keys, :]),
                                           preferred_element_type=F32)
                mx_t[p, cur, :] = _per_head(ms[0], ms[1], WIN)
                sm_t[p, cur, :] = _per_head(ls[0], ls[1], WIN)
                return carry

            lax.fori_loop(0, n_blk, block, 0, unroll=ATT_UNROLL)
            if d > 1:
                for src, dst, n in chunks:
                    acc_n[p, src, :] = acc[p, dst:dst + n, :]
                    mx_n[p, src, :] = mx[p, dst:dst + n, :]
                    sm_n[p, src, :] = sm[p, dst:dst + n, :]

        chunk = 256

        def merge(i, carry):
            rows = pl.ds(pl.multiple_of(i * chunk, chunk), chunk)
            ms = [mx_n[p, rows, :] for p in range(3)]
            m = jnp.maximum(jnp.maximum(ms[0], ms[1]), ms[2])
            ws = [jnp.exp(mp - m) for mp in ms]
            l = ws[0] * sm_n[0, rows, :] + ws[1] * sm_n[1, rows, :] + ws[2] * sm_n[2, rows, :]
            o = (ws[0] * acc_n[0, rows, :] + ws[1] * acc_n[1, rows, :] + ws[2] * acc_n[2, rows, :]) / l
            o_ref[rows, :] = o.astype(o_ref.dtype)
            lse = m + jnp.log(l)
            for h in range(2):
                lse_ref[rows, h:h + 1] = lse[:, h * HEAD_DIM:h * HEAD_DIM + 1]
            return carry

        lax.fori_loop(0, S // chunk, merge, 0)

    return pl.pallas_call(
        body, name="attention_fwd", grid=(B, n_pair),
        in_specs=[pl.BlockSpec(memory_space=pltpu.SMEM), _qkv_spec(S, 0), _qkv_spec(S, 1), _qkv_spec(S, 2)],
        out_specs=[pl.BlockSpec((None, S, LANES), lambda b, hp: (b, 0, hp)),
                   pl.BlockSpec((None, None, S, 2), lambda b, hp: (b, hp, 0, 0))],
        out_shape=[jax.ShapeDtypeStruct((B, S, ATT_WIDTH), BF16),
                   jax.ShapeDtypeStruct((B, n_pair, S, 2), F32)],
        scratch_shapes=[pltpu.VMEM((S, LANES), BF16), pltpu.VMEM((S + WIN, LANES), BF16),
                        pltpu.VMEM((S + WIN, LANES), BF16), pltpu.VMEM((2, WIN, 2 * KEYS), F32)]
        + [pltpu.VMEM((3, S, LANES), F32)] * 6,
        compiler_params=_params(2),
    )(slopes, qkv, qkv, qkv)


def _attention_bwd(qkv, o, do, lse, slopes):
    B, S, _ = qkv.shape
    n_blk = S // WIN
    n_pair = N_HEADS // 2

    def body(slopes_ref, q_ref, k_ref, v_ref, o_ref, do_ref, lse_ref, dx_ref,
             qp, dop, kp, vp, aux, auxp, aux_t, bias_t, dqp, dvk, dq_n, dk_n, dv_n):
        hp = pl.program_id(1)
        aux[...] = jnp.zeros_like(aux)
        for c0 in range(0, S, COPY_ROWS):
            rows = slice(c0, c0 + COPY_ROWS)
            prod = do_ref[rows, :] * o_ref[rows, :].astype(F32)
            for h in range(2):
                aux[rows, 2 * h:2 * h + 1] = lse_ref[rows, h:h + 1]
                aux[rows, 2 * h + 1:2 * h + 2] = jnp.sum(prod[:, h * HEAD_DIM:(h + 1) * HEAD_DIM], axis=1,
                                                         keepdims=True)
        dq_n[...] = jnp.zeros_like(dq_n)
        dk_n[...] = jnp.zeros_like(dk_n)
        dv_n[...] = jnp.zeros_like(dv_n)
        _zero_once((kp, vp))
        for p, d in enumerate(DILATIONS):
            nb = n_blk // d
            chunks = _row_chunks(d, S)
            for src, dst, n in chunks:
                auxp[dst:dst + n, :] = aux[src, :]
                qp[dst:dst + n, :] = (q_ref[src, :] * ATT_SCALE).astype(BF16)
                dop[dst:dst + n, :] = do_ref[src, :].astype(BF16)
                kp[WIN + dst:WIN + dst + n, :] = k_ref[src, :].astype(BF16)
                vp[WIN + dst:WIN + dst + n, :] = v_ref[src, :].astype(BF16)
            for i in range(n_blk):
                aux_t[i] = auxp[i * WIN:(i + 1) * WIN, :].T[0:8, :]
            _pair_bias(bias_t, slopes_ref, hp, d, key_major=True)
            dvk[...] = jnp.zeros_like(dvk)

            def block(i, carry, nb=nb):
                cur = pl.ds(pl.multiple_of(i * WIN, WIN), WIN)
                keys = pl.ds(pl.multiple_of(i * WIN, WIN), KEYS)
                flag = ((i % nb) > 0).astype(jnp.int32)
                q2, do2 = qp[cur, :], dop[cur, :]
                kc = _stack_heads(kp[keys, :])
                s_t = lax.dot_general(kc, q2, _NT, preferred_element_type=F32) + bias_t[flag]
                dp_t = lax.dot_general(_stack_heads(vp[keys, :]), do2, _NT, preferred_element_type=F32)
                ps, dss = [], []
                for h in range(2):
                    span = slice(h * KEYS, (h + 1) * KEYS)
                    p_t = jnp.exp(s_t[span, :] - aux_t[i, 2 * h:2 * h + 1, :])
                    ds_t = p_t * (dp_t[span, :] - aux_t[i, 2 * h + 1:2 * h + 2, :])
                    ps.append(p_t.astype(BF16))
                    dss.append(ds_t.astype(BF16))
                do_rows, q_rows = _stack_heads(do2), _stack_heads(q2)
                zr = jnp.zeros_like(do_rows)
                rhs = jnp.concatenate([jnp.concatenate([do_rows, zr], axis=1),
                                       jnp.concatenate([zr, q_rows], axis=1)], axis=0)
                dvk[keys, :] += jnp.dot(jnp.concatenate(ps + dss, axis=1), rhs, preferred_element_type=F32)
                dqp[cur, :] = lax.dot_general(jnp.concatenate(dss, axis=0), kc, _TN, preferred_element_type=F32)
                return carry

            lax.fori_loop(0, n_blk, block, 0, unroll=ATT_UNROLL)
            for src, dst, n in chunks:
                dq_n[src, :] += dqp[dst:dst + n, :]
                dv_n[src, :] += dvk[WIN + dst:WIN + dst + n, :LANES]
                dk_n[src, :] += dvk[WIN + dst:WIN + dst + n, LANES:]
        for c0 in range(0, S, COPY_ROWS):
            rows = slice(c0, c0 + COPY_ROWS)
            dx_ref[rows, 0:LANES] = (dq_n[rows, :] * ATT_SCALE).astype(dx_ref.dtype)
            dx_ref[rows, LANES:2 * LANES] = dk_n[rows, :].astype(dx_ref.dtype)
            dx_ref[rows, 2 * LANES:3 * LANES] = dv_n[rows, :].astype(dx_ref.dtype)

    pair = lambda width: pl.BlockSpec((None, S, width), lambda b, hp: (b, 0, hp))
    vm = lambda shape, dt: pltpu.VMEM(shape, dt)
    return pl.pallas_call(
        body, name="attention_bwd", grid=(B, n_pair),
        in_specs=[pl.BlockSpec(memory_space=pltpu.SMEM), _qkv_spec(S, 0), _qkv_spec(S, 1), _qkv_spec(S, 2),
                  pair(LANES), pair(LANES), pl.BlockSpec((None, None, S, 2), lambda b, hp: (b, hp, 0, 0))],
        out_specs=pair(3 * LANES),
        out_shape=jax.ShapeDtypeStruct((B, S, 3 * ATT_WIDTH), BF16),
        scratch_shapes=[vm((S, LANES), BF16), vm((S, LANES), BF16),
                        vm((S + WIN, LANES), BF16), vm((S + WIN, LANES), BF16),
                        vm((S, LANES), F32), vm((S, LANES), F32), vm((n_blk, 8, WIN), F32),
                        vm((2, 2 * KEYS, WIN), F32),
                        vm((S, LANES), F32), vm((S + WIN, 2 * LANES), F32),
                        vm((S, LANES), F32), vm((S, LANES), F32), vm((S, LANES), F32)],
        compiler_params=_params(2),
    )(slopes, qkv, qkv, qkv, o, do, lse)


SCAN_COLS = 256
SCAN_ROWS = 8


def _rows_to_tile(rows):
    rid = lax.broadcasted_iota(jnp.int32, (SCAN_ROWS, rows[0].shape[1]), 0)
    tile = jnp.broadcast_to(rows[0], rid.shape)
    for k in range(1, SCAN_ROWS):
        tile = jnp.where(rid == k, rows[k], tile)
    return tile


SCAN_UNROLL = 4


def _complex_powers(ar, ai, n):
    out = [(ar, ai)]
    for _ in range(n - 1):
        pr, pi = out[-1]
        out.append((pr * ar - pi * ai, pr * ai + pi * ar))
    return out


def _round_multipliers(powers, rid, reverse):
    out = []
    for s in (1, 2, 4):
        keep = (rid < SCAN_ROWS - s) if reverse else (rid >= s)
        out.append((jnp.where(keep, powers[s - 1][0], 0.0), jnp.where(keep, powers[s - 1][1], 0.0)))
    return out


def _tile_scan(xr, xi, multipliers, reverse):
    for s, (mr, mi) in zip((1, 2, 4), multipliers):
        shift = SCAN_ROWS - s if reverse else s
        sr, si = pltpu.roll(xr, shift, 0), pltpu.roll(xi, shift, 0)
        xr, xi = xr + (mr * sr - mi * si), xi + (mr * si + mi * sr)
    return xr, xi


def _scan_fwd(bu, a_row):
    B, S, _ = bu.shape
    groups = 2
    nc = SSM_COLS // (groups * SCAN_COLS)
    nt = S // SCAN_ROWS
    LAST = slice(SCAN_ROWS - 1, SCAN_ROWS)

    def body(bu_ref, a_ref, xs_ref):
        rid = lax.broadcasted_iota(jnp.int32, (SCAN_ROWS, SCAN_COLS), 0)
        consts = []
        for g in range(groups):
            re = slice(2 * g * SCAN_COLS, (2 * g + 1) * SCAN_COLS)
            im = slice((2 * g + 1) * SCAN_COLS, (2 * g + 2) * SCAN_COLS)
            powers = _complex_powers(a_ref[:, re], a_ref[:, im], SCAN_ROWS)
            carry_mult = (_rows_to_tile([p[0] for p in powers]), _rows_to_tile([p[1] for p in powers]))
            consts.append((re, im, carry_mult, _round_multipliers(powers, rid, reverse=False)))

        def tile(i, carry):
            out = []
            for (re, im, (cr_t, ci_t), rounds), (cr, ci) in zip(consts, carry):
                xr, xi = _tile_scan(bu_ref[i, :, re], bu_ref[i, :, im], rounds, reverse=False)
                xs_ref[i, :, re] = xr + (cr_t * cr - ci_t * ci)
                xs_ref[i, :, im] = xi + (cr_t * ci + ci_t * cr)
                out.append((xs_ref[i, LAST, re], xs_ref[i, LAST, im]))
            return tuple(out)

        zero = jnp.zeros((1, SCAN_COLS), F32)
        lax.fori_loop(0, nt, tile, ((zero, zero),) * groups, unroll=SCAN_UNROLL)

    col = pl.BlockSpec((None, nt, SCAN_ROWS, 2 * groups * SCAN_COLS), lambda b, j: (b, 0, 0, j))
    xs = pl.pallas_call(
        body, name="s5_scan_fwd", grid=(B, nc),
        in_specs=[col, pl.BlockSpec((1, 2 * groups * SCAN_COLS), lambda b, j: (0, j))],
        out_specs=col,
        out_shape=jax.ShapeDtypeStruct((B, nt, SCAN_ROWS, 2 * SSM_COLS), F32),
        compiler_params=_params(2),
    )(bu.reshape(B, nt, SCAN_ROWS, 2 * SSM_COLS), a_row)
    return xs.reshape(B, S, 2 * SSM_COLS)


def _scan_bwd(dxs, xs, a_row):
    B, S, _ = dxs.shape
    nc = SSM_COLS // SCAN_COLS
    nt = S // SCAN_ROWS
    RE, IM = slice(0, SCAN_COLS), slice(SCAN_COLS, 2 * SCAN_COLS)
    FIRST, LAST = slice(0, 1), slice(SCAN_ROWS - 1, SCAN_ROWS)

    def body(d_ref, x_ref, a_ref, lam_ref, ga_ref):
        b = pl.program_id(1)
        powers = _complex_powers(a_ref[:, RE], -a_ref[:, IM], SCAN_ROWS)
        rid = lax.broadcasted_iota(jnp.int32, (SCAN_ROWS, SCAN_COLS), 0)
        cr_t = _rows_to_tile([powers[SCAN_ROWS - 1 - r][0] for r in range(SCAN_ROWS)])
        ci_t = _rows_to_tile([powers[SCAN_ROWS - 1 - r][1] for r in range(SCAN_ROWS)])
        rounds = _round_multipliers(powers, rid, reverse=True)

        @pl.when(b == 0)
        def _():
            ga_ref[...] = jnp.zeros_like(ga_ref)

        def tile(j, carry):
            cr, ci, accr, acci = carry
            i = nt - 1 - j
            lr, li = _tile_scan(d_ref[i, :, RE], d_ref[i, :, IM], rounds, reverse=True)
            lam_r = lr + (cr_t * cr - ci_t * ci)
            lam_i = li + (cr_t * ci + ci_t * cr)
            lam_ref[i, :, RE] = lam_r
            lam_ref[i, :, IM] = lam_i
            ip = jnp.maximum(i - 1, 0)
            keep = (i > 0).astype(F32)
            xpr = jnp.where(rid == 0, x_ref[ip, LAST, RE] * keep, pltpu.roll(x_ref[i, :, RE], 1, 0))
            xpi = jnp.where(rid == 0, x_ref[ip, LAST, IM] * keep, pltpu.roll(x_ref[i, :, IM], 1, 0))
            accr = accr + lam_r * xpr + lam_i * xpi
            acci = acci + lam_i * xpr - lam_r * xpi
            return lam_ref[i, FIRST, RE], lam_ref[i, FIRST, IM], accr, acci

        z1 = jnp.zeros((1, SCAN_COLS), F32)
        z8 = jnp.zeros((SCAN_ROWS, SCAN_COLS), F32)
        _, _, accr, acci = lax.fori_loop(0, nt, tile, (z1, z1, z8, z8), unroll=SCAN_UNROLL)
        ga_ref[:, RE] += _col_sum(accr)
        ga_ref[:, IM] += _col_sum(acci)

    col = pl.BlockSpec((None, nt, SCAN_ROWS, 2 * SCAN_COLS), lambda j, b: (b, 0, 0, j))
    par = pl.BlockSpec((1, 2 * SCAN_COLS), lambda j, b: (0, j))
    t4 = lambda a: a.reshape(B, nt, SCAN_ROWS, 2 * SSM_COLS)
    lam, g_a = pl.pallas_call(
        body, name="s5_scan_bwd", grid=(nc, B),
        in_specs=[col, col, par], out_specs=[col, par],
        out_shape=[jax.ShapeDtypeStruct((B, nt, SCAN_ROWS, 2 * SSM_COLS), F32),
                   jax.ShapeDtypeStruct((1, 2 * SSM_COLS), F32)],
        compiler_params=_params(2),
    )(t4(dxs), t4(xs), a_row)
    return lam.reshape(B, S, 2 * SSM_COLS), g_a


def _s5_discretise(lr, li, log_dt):
    dt = jnp.exp(log_dt)
    mag = jnp.exp(lr * dt)
    ang = li * dt
    ab_re, ab_im = mag * jnp.cos(ang), mag * jnp.sin(ang)
    nr, ni = ab_re - 1.0, ab_im
    den = lr * lr + li * li
    f_re = (nr * lr + ni * li) / den
    f_im = (ni * lr - nr * li) / den
    return dt, ab_re, ab_im, nr, ni, den, f_re, f_im


def _s5_params(a_re, a_im, log_dt):
    def body(lr_ref, li_ref, ld_ref, abr, abi, fr, fi):
        _, ab_re, ab_im, _, _, _, f_re, f_im = _s5_discretise(lr_ref[...], li_ref[...], ld_ref[...])
        abr[...] = ab_re
        abi[...] = ab_im
        fr[...] = f_re
        fi[...] = f_im

    return pl.pallas_call(body, name="s5_params",
                          out_shape=[jax.ShapeDtypeStruct(a_re.shape, F32)] * 4)(a_re, a_im, log_dt)


def _s5_input_matrix(f_re, f_im, b_re, b_im):
    def body(fr, fi, br, bi, o_re, o_im):
        o_re[...] = fr[...] * br[...] - fi[...] * bi[...]
        o_im[...] = fr[...] * bi[...] + fi[...] * br[...]

    return pl.pallas_call(body, name="s5_input_matrix",
                          out_shape=[jax.ShapeDtypeStruct(b_re.shape, F32)] * 2)(f_re, f_im, b_re, b_im)


def _s5_input_matrix_bwd(f_re, f_im, b_re, b_im, g_re, g_im):
    def body(fr, fi, br, bi, gr, gi, dbr, dbi, dfr, dfi):
        dbr[...] = fr[...] * gr[...] + fi[...] * gi[...]
        dbi[...] = fr[...] * gi[...] - fi[...] * gr[...]
        dfr[...] = jnp.sum(br[...] * gr[...] + bi[...] * gi[...], axis=1, keepdims=True)
        dfi[...] = jnp.sum(br[...] * gi[...] - bi[...] * gr[...], axis=1, keepdims=True)

    return pl.pallas_call(
        body, name="s5_input_matrix_bwd",
        out_shape=[jax.ShapeDtypeStruct(b_re.shape, F32)] * 2 + [jax.ShapeDtypeStruct(f_re.shape, F32)] * 2,
    )(f_re, f_im, b_re, b_im, g_re, g_im)


def _s5_params_bwd(a_re, a_im, log_dt, g_ab_re, g_ab_im, d_f_re, d_f_im):
    def body(lr_ref, li_ref, ld_ref, gar, gai, dfr, dfi, o_lr, o_li, o_ld):
        lr, li = lr_ref[...], li_ref[...]
        dt, ab_re, ab_im, nr, ni, den, f_re, f_im = _s5_discretise(lr, li, ld_ref[...])
        d_fr, d_fi = dfr[...], dfi[...]
        d_nr = (d_fr * lr - d_fi * li) / den
        d_ni = (d_fr * li + d_fi * lr) / den
        common = (d_fr * f_re + d_fi * f_im) * 2.0 / den
        d_lr = (d_fr * nr + d_fi * ni) / den - common * lr
        d_li = (d_fr * ni - d_fi * nr) / den - common * li
        d_abr = gar[...] + d_nr
        d_abi = gai[...] + d_ni
        d_mag_mag = d_abr * ab_re + d_abi * ab_im
        d_ang = d_abi * ab_re - d_abr * ab_im
        o_lr[...] = d_lr + d_mag_mag * dt
        o_li[...] = d_li + d_ang * dt
        o_ld[...] = jnp.sum(d_mag_mag * lr + d_ang * li, axis=1, keepdims=True) * dt

    return pl.pallas_call(
        body, name="s5_params_bwd",
        out_shape=[jax.ShapeDtypeStruct(a_re.shape, F32)] * 2 + [jax.ShapeDtypeStruct(log_dt.shape, F32)],
    )(a_re, a_im, log_dt, g_ab_re, g_ab_im, d_f_re, d_f_im)


CONV_COLS = 256


def _shift_down(v, j, row):
    return jnp.where(row >= j, pltpu.roll(v, j, 0), 0.0)


def _shift_up(v, j, row, seq):
    return jnp.where(row < seq - j, pltpu.roll(v, seq - j, 0), 0.0)


def _conv_fwd(up, w_conv, b_conv):
    B, S, _ = up.shape
    nj = D_FF // CONV_COLS

    def body(up_ref, w_ref, b_ref, ff_ref):
        a = up_ref[:, :CONV_COLS].astype(F32)
        val = up_ref[:, CONV_COLS:].astype(F32)
        row = lax.broadcasted_iota(jnp.int32, a.shape, 0)
        w0, w1, w2 = w_ref[0:1, :], w_ref[1:2, :], w_ref[2:3, :]
        conv = b_ref[...] + w0 * a + w1 * _shift_down(a, 1, row) + w2 * _shift_down(a, 2, row)
        ff_ref[...] = (conv * _sigmoid(conv) * val).astype(ff_ref.dtype)

    return pl.pallas_call(
        body, name="conv_gate_fwd", grid=(B, nj),
        in_specs=[pl.BlockSpec((None, S, 2 * CONV_COLS), lambda b, j: (b, 0, j)),
                  pl.BlockSpec((3, CONV_COLS), lambda b, j: (0, j)),
                  pl.BlockSpec((1, CONV_COLS), lambda b, j: (0, j))],
        out_specs=pl.BlockSpec((None, S, CONV_COLS), lambda b, j: (b, 0, j)),
        out_shape=jax.ShapeDtypeStruct((B, S, D_FF), BF16),
        compiler_params=_params(2),
    )(up, w_conv, b_conv)


def _conv_bwd(up, d_ff, w_conv, b_conv):
    B, S, _ = up.shape
    nj = D_FF // CONV_COLS

    def body(up_ref, dff_ref, w_ref, b_ref, dup_ref, dw_ref, db_ref):
        b = pl.program_id(1)
        a = up_ref[:, :CONV_COLS].astype(F32)
        val = up_ref[:, CONV_COLS:].astype(F32)
        row = lax.broadcasted_iota(jnp.int32, a.shape, 0)
        w0, w1, w2 = w_ref[0:1, :], w_ref[1:2, :], w_ref[2:3, :]
        a1, a2 = _shift_down(a, 1, row), _shift_down(a, 2, row)
        conv = b_ref[...] + w0 * a + w1 * a1 + w2 * a2
        sg = _sigmoid(conv)
        dff = dff_ref[...].astype(F32)
        d_val = dff * conv * sg
        dc = dff * val * (sg * (1.0 + conv * (1.0 - sg)))
        d_a = w0 * dc + w1 * _shift_up(dc, 1, row, S) + w2 * _shift_up(dc, 2, row, S)
        dup_ref[:, :CONV_COLS] = d_a.astype(dup_ref.dtype)
        dup_ref[:, CONV_COLS:] = d_val.astype(dup_ref.dtype)

        @pl.when(b == 0)
        def _():
            dw_ref[...] = jnp.zeros_like(dw_ref)
            db_ref[...] = jnp.zeros_like(db_ref)

        dw_ref[0:1, :] += _col_sum(dc * a)
        dw_ref[1:2, :] += _col_sum(dc * a1)
        dw_ref[2:3, :] += _col_sum(dc * a2)
        db_ref[...] += _col_sum(dc)

    return pl.pallas_call(
        body, name="conv_gate_bwd", grid=(nj, B),
        in_specs=[pl.BlockSpec((None, S, 2 * CONV_COLS), lambda j, b: (b, 0, j)),
                  pl.BlockSpec((None, S, CONV_COLS), lambda j, b: (b, 0, j)),
                  pl.BlockSpec((3, CONV_COLS), lambda j, b: (0, j)),
                  pl.BlockSpec((1, CONV_COLS), lambda j, b: (0, j))],
        out_specs=[pl.BlockSpec((None, S, 2 * CONV_COLS), lambda j, b: (b, 0, j)),
                   pl.BlockSpec((3, CONV_COLS), lambda j, b: (0, j)),
                   pl.BlockSpec((1, CONV_COLS), lambda j, b: (0, j))],
        out_shape=[jax.ShapeDtypeStruct((B, S, 2 * D_FF), BF16), jax.ShapeDtypeStruct((3, D_FF), F32),
                   jax.ShapeDtypeStruct((1, D_FF), F32)],
        compiler_params=_params(2),
    )(up, d_ff, w_conv, b_conv)


def _ada_fwd(c_all, w_ada, b_ada):
    def body(c_ref, w_ref, b_ref, o_ref):
        cv = c_ref[...]
        act = (cv * _sigmoid(cv)).astype(BF16)
        o_ref[...] = jnp.dot(act, w_ref[...].astype(BF16), preferred_element_type=F32) + b_ref[...]

    return pl.pallas_call(body, name="ada_fwd",
                          out_shape=jax.ShapeDtypeStruct((c_all.shape[0], w_ada.shape[1]), F32),
                          compiler_params=pltpu.CompilerParams(vmem_limit_bytes=V7X_VMEM_LIMIT))(c_all, w_ada, b_ada)


def _ada_bwd(c_all, dmod_all, dmod_cols):
    def body(c_ref, dm_ref, dmc_ref, dw_ref, db_ref):
        cv = c_ref[...]
        act = (cv * _sigmoid(cv)).astype(BF16)
        dw_ref[...] = lax.dot_general(act, dmc_ref[...].astype(BF16), _TN, preferred_element_type=F32)
        db_ref[...] = _col_sum(dm_ref[...])

    return pl.pallas_call(
        body, name="ada_bwd",
        out_shape=[jax.ShapeDtypeStruct((c_all.shape[1], dmod_cols.shape[1]), F32),
                   jax.ShapeDtypeStruct((1, dmod_all.shape[1]), F32)],
        compiler_params=pltpu.CompilerParams(vmem_limit_bytes=V7X_VMEM_LIMIT))(c_all, dmod_all, dmod_cols)


def _adamw(w, m, v, g_parts, name):
    R, C = w.shape
    P = g_parts.shape[0]
    tr = R
    for cand in (256, 128, 64, 32, 16, 8):
        if R % cand == 0 and cand * C * 4 * (P + 7) * 2 <= V7X_VMEM_LIMIT // 2:
            tr = cand
            break
    c1 = 1.0 / (1.0 - ADAM_B1 ** ADAM_STEP)
    c2 = 1.0 / (1.0 - ADAM_B2 ** ADAM_STEP)

    def body(w_ref, m_ref, v_ref, g_ref, og, od, om, ov):
        g = g_ref[0].astype(F32)
        for p in range(1, P):
            g = g + g_ref[p].astype(F32)
        m_new = ADAM_B1 * m_ref[...] + (1.0 - ADAM_B1) * g
        v_new = ADAM_B2 * v_ref[...] + (1.0 - ADAM_B2) * (g * g)
        og[...] = g
        om[...] = m_new
        ov[...] = v_new
        od[...] = -ADAM_LR * ((m_new * c1) / (jnp.sqrt(v_new * c2) + ADAM_EPS) + ADAM_WD * w_ref[...])

    spec = pl.BlockSpec((tr, C), lambda i: (i, 0))
    return pl.pallas_call(
        body, name=name, grid=(R // tr,),
        in_specs=[spec, spec, spec, pl.BlockSpec((P, tr, C), lambda i: (0, i, 0))],
        out_specs=[spec] * 4, out_shape=[jax.ShapeDtypeStruct((R, C), F32)] * 4,
        compiler_params=_params(1),
    )(w, m, v, g_parts)


def _sum_parts(parts, loss_rows):
    P, R, C = parts.shape
    lo, hi = loss_rows

    def body(p_ref, o_ref, loss_ref):
        t = p_ref[0]
        for p in range(1, P):
            t = t + p_ref[p]
        o_ref[...] = t
        tot = jnp.sum(jnp.sum(o_ref[lo:hi, :], axis=1, keepdims=True), axis=0, keepdims=True)
        loss_ref[...] = jnp.broadcast_to(tot, loss_ref.shape)

    return pl.pallas_call(body, name="sum_small_grads",
                          out_shape=[jax.ShapeDtypeStruct((R, C), F32), jax.ShapeDtypeStruct((1, LANES), F32)],
                          compiler_params=pltpu.CompilerParams(vmem_limit_bytes=V7X_VMEM_LIMIT))(parts)


def _exchange(items, name):
    n = len(items)
    MESH = pl.DeviceIdType.MESH

    def body(*refs):
        src, dst = refs[:n], refs[n:2 * n]
        send_sems, recv_sems, local_sems = refs[2 * n:]
        x, y, c = lax.axis_index("x"), lax.axis_index("y"), lax.axis_index("c")
        me = 4 * x + 2 * y + c
        started = []
        for it, (_, per_peer) in enumerate(items):
            own = pltpu.make_async_copy(src[it].at[me] if per_peer else src[it], dst[it].at[me], local_sems.at[it])
            own.start()
            started.append(own)
        sends, recvs = [], []
        for k in range(1, N_DEV):
            px = 1 - x if k & 4 else x
            py = 1 - y if k & 2 else y
            pc = 1 - c if k & 1 else c
            peer = 4 * px + 2 * py + pc
            for it, (_, per_peer) in enumerate(items):
                s = src[it].at[peer] if per_peer else src[it]
                cp = pltpu.make_async_remote_copy(src_ref=s, dst_ref=dst[it].at[me], send_sem=send_sems.at[it, k - 1],
                                                  recv_sem=recv_sems.at[it, k - 1], device_id=(px, py, pc),
                                                  device_id_type=MESH)
                cp.start()
                sends.append(cp)
                recvs.append(pltpu.make_async_remote_copy(
                    src_ref=s, dst_ref=dst[it].at[peer], send_sem=send_sems.at[it, k - 1],
                    recv_sem=recv_sems.at[it, k - 1], device_id=(px, py, pc), device_id_type=MESH))
        for cp in recvs:
            cp.wait_recv()
        for cp in sends:
            cp.wait_send()
        for cp in started:
            cp.wait()

    any_spec = pl.BlockSpec(memory_space=pl.ANY)
    out_shape = []
    for a, per_peer in items:
        shp = a.shape if per_peer else (N_DEV,) + a.shape
        out_shape.append(jax.ShapeDtypeStruct(shp, a.dtype))
    return pl.pallas_call(
        body, name=name, in_specs=[any_spec] * n, out_specs=[any_spec] * n, out_shape=out_shape,
        scratch_shapes=[pltpu.SemaphoreType.DMA((n, N_DEV - 1)), pltpu.SemaphoreType.DMA((n, N_DEV - 1)),
                        pltpu.SemaphoreType.DMA((n,))],
    )(*[a for a, _ in items])


def _remote(src, dst, send_sem, recv_sem, device):
    return pltpu.make_async_remote_copy(src_ref=src, dst_ref=dst, send_sem=send_sem, recv_sem=recv_sem,
                                        device_id=device, device_id_type=pl.DeviceIdType.MESH)


def _mesh_place():
    x, y, c = lax.axis_index("x"), lax.axis_index("y"), lax.axis_index("c")
    other_chips = [(1 - x, y), (x, 1 - y), (1 - x, 1 - y)]
    return x, y, c, (x, y, 1 - c), other_chips


def _gather_all(items, name):
    n = len(items)

    def body(*refs):
        src, dst = refs[:n], refs[n:2 * n]
        send_sems, recv_sems, local_sems = refs[2 * n:]
        x, y, c, sibling, chips = _mesh_place()
        slot = lambda px, py, pc: 4 * px + 2 * py + pc
        me = slot(x, y, c)
        own = [pltpu.make_async_copy(src[it], dst[it].at[me], local_sems.at[it]) for it in range(n)]
        first = []
        for it in range(n):
            first.append(_remote(src[it], dst[it].at[me], send_sems.at[it, 0], recv_sems.at[it, 0], sibling))
            for j, chip in enumerate(chips):
                first.append(_remote(src[it], dst[it].at[me], send_sems.at[it, 1 + j], recv_sems.at[it, 1 + j],
                                     (*chip, c)))
        for cp in own + first:
            cp.start()
        passed = []
        for j, chip in enumerate(chips):
            blk = slot(*chip, c)
            for it in range(n):
                _remote(src[it], dst[it].at[blk], send_sems.at[it, 1 + j], recv_sems.at[it, 1 + j],
                        (*chip, c)).wait_recv()
                fwd = _remote(dst[it].at[blk], dst[it].at[blk], send_sems.at[it, 4 + j], recv_sems.at[it, 4 + j],
                              sibling)
                fwd.start()
                passed.append(fwd)
        for it in range(n):
            _remote(src[it], dst[it].at[slot(x, y, 1 - c)], send_sems.at[it, 0], recv_sems.at[it, 0],
                    sibling).wait_recv()
        for j, chip in enumerate(chips):
            for it in range(n):
                _remote(src[it], dst[it].at[slot(*chip, 1 - c)], send_sems.at[it, 4 + j], recv_sems.at[it, 4 + j],
                        sibling).wait_recv()
        for cp in first + passed:
            cp.wait_send()
        for cp in own:
            cp.wait()

    any_spec = pl.BlockSpec(memory_space=pl.ANY)
    return pl.pallas_call(
        body, name=name, in_specs=[any_spec] * n, out_specs=[any_spec] * n,
        out_shape=[jax.ShapeDtypeStruct((N_DEV,) + a.shape, a.dtype) for a in items],
        scratch_shapes=[pltpu.SemaphoreType.DMA((n, 7)), pltpu.SemaphoreType.DMA((n, 7)),
                        pltpu.SemaphoreType.DMA((n,))],
    )(*items)


N_CHIPS = N_DEV // 2


def _sibling_swap(items, name):
    n = len(items)

    def body(*refs):
        g, got = refs[:n], refs[n:2 * n]
        send_sems, recv_sems = refs[2 * n:]
        x, y, c, sibling, _ = _mesh_place()
        sends = [_remote(g[it].at[2 * chip + 1 - c], got[it].at[chip], send_sems.at[it, chip],
                         recv_sems.at[it, chip], sibling) for it in range(n) for chip in range(N_CHIPS)]
        for cp in sends:
            cp.start()
        for cp in sends:
            cp.wait_recv()
        for cp in sends:
            cp.wait_send()

    any_spec = pl.BlockSpec(memory_space=pl.ANY)
    return pl.pallas_call(
        body, name=name, in_specs=[any_spec] * n, out_specs=[any_spec] * n,
        out_shape=[jax.ShapeDtypeStruct((N_CHIPS,) + a.shape[1:], a.dtype) for a in items],
        scratch_shapes=[pltpu.SemaphoreType.DMA((n, N_CHIPS))] * 2,
    )(*items)


def _add_pairs(items, gots):
    n = len(items)

    def body(core_ref, *refs):
        for it in range(n):
            a_ref, b_ref, o_ref = refs[it], refs[n + it], refs[2 * n + it]
            o_ref[...] = (a_ref[...].astype(F32) + b_ref[...].astype(F32)).astype(o_ref.dtype)

    mine = [pl.BlockSpec((None,) + a.shape[1:], lambda i, core: (2 * i + core[0], 0, 0)) for a in items]
    plain = [pl.BlockSpec((None,) + a.shape[1:], lambda i, core: (i, 0, 0)) for a in gots]
    grid_spec = pltpu.PrefetchScalarGridSpec(num_scalar_prefetch=1, grid=(N_CHIPS,), in_specs=mine + plain,
                                             out_specs=plain)
    core = lax.axis_index("c").astype(jnp.int32).reshape(1)
    return pl.pallas_call(body, name="add_sibling_partials", grid_spec=grid_spec,
                          out_shape=[jax.ShapeDtypeStruct(a.shape, a.dtype) for a in gots],
                          compiler_params=_params(1))(core, *items, *gots)


def _chip_scatter(items, name):
    n = len(items)

    def body(*refs):
        s, out = refs[:n], refs[n:2 * n]
        send_sems, recv_sems, local_sems = refs[2 * n:]
        x, y, c, _, _ = _mesh_place()
        my_chip = 2 * x + y
        keeps = [pltpu.make_async_copy(s[it].at[my_chip], out[it].at[my_chip], local_sems.at[it]) for it in range(n)]
        for cp in keeps:
            cp.start()
        sends, recvs = [], []
        for k in range(1, N_CHIPS):
            px = 1 - x if k & 2 else x
            py = 1 - y if k & 1 else y
            peer_chip = 2 * px + py
            for it in range(n):
                cp = _remote(s[it].at[peer_chip], out[it].at[my_chip], send_sems.at[it, k - 1],
                             recv_sems.at[it, k - 1], (px, py, c))
                cp.start()
                sends.append(cp)
                recvs.append(_remote(s[it].at[peer_chip], out[it].at[peer_chip], send_sems.at[it, k - 1],
                                     recv_sems.at[it, k - 1], (px, py, c)))
        for cp in recvs:
            cp.wait_recv()
        for cp in sends:
            cp.wait_send()
        for cp in keeps:
            cp.wait()

    any_spec = pl.BlockSpec(memory_space=pl.ANY)
    return pl.pallas_call(
        body, name=name, in_specs=[any_spec] * n, out_specs=[any_spec] * n,
        out_shape=[jax.ShapeDtypeStruct(a.shape, a.dtype) for a in items],
        scratch_shapes=[pltpu.SemaphoreType.DMA((n, N_CHIPS - 1)), pltpu.SemaphoreType.DMA((n, N_CHIPS - 1)),
                        pltpu.SemaphoreType.DMA((n,))],
    )(*items)


def _gelu_tanh(y):
    k = math.sqrt(2.0 / math.pi)
    t = jnp.tanh(k * (y + 0.044715 * y * y * y))
    return 0.5 * y * (1.0 + t), t


def _local_step(x, mod, target, W, P):
    B, S, D = x.shape
    T = B * S
    TS = 512
    flat = lambda a: a.reshape(T, a.shape[-1])
    unflat = lambda a: a.reshape(B, S, a.shape[-1])
    mod_col = lambda i: (mod, D, i)

    def f_modnorm(xv, sc, sh, g):
        return (xv * _rms_scale(xv) * g) * (1.0 + sc) + sh

    (u1,) = _rowwise(f_modnorm, [(x, D, 0)], [mod_col(1), mod_col(0)], [P["g_mix"]],
                     [(D, BF16)], [], [], ts=TS, name="modnorm_mix")
    u1f = flat(u1)
    qkv = unflat(_matmul(u1f, W["w_qkv"], name="proj_qkv"))
    us = unflat(_matmul(u1f, W["w_us"], name="proj_ssm_in"))
    gates = unflat(_matmul(u1f, W["w_gates"], out_dtype=BF16, name="proj_gates"))

    o_att, lse = _attention_fwd(qkv, P["slopes"])
    y_att = unflat(_matmul(flat(o_att), W["w_proj_att"], out_dtype=BF16, name="proj_att"))

    bu = unflat(_matmul(flat(us), P["bb_big"], name="s5_bu"))
    xs = _scan_fwd(bu, P["a_row"])
    y_mm = unflat(_matmul(flat(xs), P["cc_big"], name="s5_readout"))

    def f_glu(ymm, usv, dsk, wg, bg):
        yv = ymm + dsk * usv
        ge, _ = _gelu_tanh(yv)
        pre = jnp.dot(ge.astype(BF16), wg, preferred_element_type=F32) + bg
        return yv, ge * _sigmoid(pre)

    y_s5, z = _rowwise(f_glu, [(y_mm, SSM_WIDTH, 0), (us, SSM_WIDTH, 0)], [], [P["d_skip"], W["w_glu"], P["b_glu"]],
                       [(SSM_WIDTH, F32), (SSM_WIDTH, BF16)], [], [], ts=TS, name="s5_glu")
    y_ssm = unflat(_matmul(flat(z), W["w_proj_ssm"], out_dtype=BF16, name="proj_ssm"))

    def f_merge(ga, gs, ya, ys, bga, bgs):
        return _sigmoid(ga + bga) * ya + _sigmoid(gs + bgs) * ys

    bga, bgs = P["b_gate"][:, :D], P["b_gate"][:, D:]
    (merged,) = _rowwise(f_merge, [(gates, D, 0), (gates, D, 1), (y_att, D, 0), (y_ssm, D, 0)], [], [bga, bgs],
                         [(D, BF16)], [], [], ts=TS, name="gate_merge")
    mix = unflat(_matmul(flat(merged), W["w_out"], name="proj_out"))

    def f_res_modnorm(xv, mx, gt, sc, sh, g):
        h = xv + gt * mx
        return h, (h * _rms_scale(h) * g) * (1.0 + sc) + sh

    h1, u2 = _rowwise(f_res_modnorm, [(x, D, 0), (mix, D, 0)], [mod_col(2), mod_col(4), mod_col(3)], [P["g_ffn"]],
                      [(D, F32), (D, BF16)], [], [], ts=TS, name="residual_modnorm_ffn")
    up = unflat(_up_fwd(flat(u2), W["w_up"], name="ffn_up"))
    ff = _conv_fwd(up, P["w_conv"], P["b_conv"])
    down = unflat(_matmul(flat(ff), W["w_down"], name="ffn_down"))

    def f_head(h1v, dn, tg, gt, g):
        h2 = h1v + gt * dn
        r = _rms_scale(h2)
        nh = h2 * r
        e = nh * g - tg
        dy = e * (1.0 / D)
        gy = dy * g
        dh = r * (gy - nh * jnp.mean(gy * nh, axis=-1, keepdims=True))
        return (dh, dh * gt, _col_sum(dh * dn), _col_sum(dy * nh), _col_sum(e * e) * (0.5 / D))

    dh2, d_down, d_gt2, d_g_final, loss_cols = _rowwise(
        f_head, [(h1, D, 0), (down, D, 0), (target, D, 0)], [mod_col(5)], [P["g_final"]],
        [(D, F32), (D, BF16)], [D], [(1, D), (1, D)], ts=TS, name="head_loss")

    d_downf = flat(d_down)
    d_ff = unflat(_matmul(d_downf, W["w_down"], tb=True, out_dtype=BF16, name="ffn_down_dx"))
    d_w_down = _matmul(flat(ff), d_downf, ta=True, out_dtype=BF16, name="ffn_down_dw")
    d_up, d_w_conv, d_b_conv = _conv_bwd(up, d_ff, P["w_conv"], P["b_conv"])
    d_upf = flat(d_up)
    d_u2 = unflat(_up_dx(d_upf, W["w_up"], name="ffn_up_dx"))
    d_w_up = _up_dw(flat(u2), d_upf, name="ffn_up_dw")

    def f_modnorm_bwd(du, h, dres, mx, sc, gt, g):
        r = _rms_scale(h)
        nh = h * r
        dn = du * (1.0 + sc)
        gy = dn * g
        dh = dres + r * (gy - nh * jnp.mean(gy * nh, axis=-1, keepdims=True))
        return (dh, dh * gt, _col_sum(du), _col_sum(du * nh * g), _col_sum(dh * mx), _col_sum(dn * nh))

    dh1, d_mix, d_sh2, d_sc2, d_gt1, d_g_ffn = _rowwise(
        f_modnorm_bwd, [(d_u2, D, 0), (h1, D, 0), (dh2, D, 0), (mix, D, 0)], [mod_col(4), mod_col(2)], [P["g_ffn"]],
        [(D, F32), (D, BF16)], [D, D, D], [(1, D)], ts=TS, name="modnorm_ffn_bwd")

    d_mixf = flat(d_mix)
    d_merged = unflat(_matmul(d_mixf, W["w_out"], tb=True, out_dtype=BF16, name="proj_out_dx"))
    d_w_out = _matmul(flat(merged), d_mixf, ta=True, out_dtype=BF16, name="proj_out_dw")

    def f_merge_bwd(dm, ga, gs, ya, ys, bga_, bgs_):
        sa, ss = _sigmoid(ga + bga_), _sigmoid(gs + bgs_)
        dga = dm * ya * sa * (1.0 - sa)
        dgs = dm * ys * ss * (1.0 - ss)
        return dm * sa, dm * ss, jnp.concatenate([dga, dgs], axis=1), _col_sum(dga), _col_sum(dgs)

    d_y_att, d_y_ssm, d_gates, d_bga, d_bgs = _rowwise(
        f_merge_bwd, [(d_merged, D, 0), (gates, D, 0), (gates, D, 1), (y_att, D, 0), (y_ssm, D, 0)], [], [bga, bgs],
        [(D, BF16), (D, BF16), (2 * D, BF16)], [], [(1, D), (1, D)], ts=TS, name="gate_merge_bwd")

    d_yaf, d_ysf = flat(d_y_att), flat(d_y_ssm)
    d_o_att = unflat(_matmul(d_yaf, W["w_proj_att"], tb=True, name="proj_att_dx"))
    d_w_proj_att = _matmul(flat(o_att), d_yaf, ta=True, out_dtype=BF16, name="proj_att_dw")
    d_z = unflat(_matmul(d_ysf, W["w_proj_ssm"], tb=True, out_dtype=BF16, name="proj_ssm_dx"))
    d_w_proj_ssm = _matmul(flat(z), d_ysf, ta=True, out_dtype=BF16, name="proj_ssm_dw")

    def f_glu_bwd(yv, dz, usv, dsk, wg, bg):
        ge, t = _gelu_tanh(yv)
        pre = jnp.dot(ge.astype(BF16), wg, preferred_element_type=F32) + bg
        sg = _sigmoid(pre)
        dpre = dz * ge * sg * (1.0 - sg)
        dge = dz * sg + lax.dot_general(dpre.astype(BF16), wg, _NT, preferred_element_type=F32)
        k = math.sqrt(2.0 / math.pi)
        dgelu = 0.5 * (1.0 + t) + 0.5 * yv * (1.0 - t * t) * k * (1.0 + 3.0 * 0.044715 * yv * yv)
        dy = dge * dgelu
        dwg = lax.dot_general(ge.astype(BF16), dpre.astype(BF16), _TN, preferred_element_type=F32)
        return dy, dy * dsk, dwg, _col_sum(dpre), _col_sum(dy * usv)

    d_y_s5, d_us_skip, d_w_glu, d_b_glu, d_d_skip = _rowwise(
        f_glu_bwd, [(y_s5, SSM_WIDTH, 0), (d_z, SSM_WIDTH, 0), (us, SSM_WIDTH, 0)], [],
        [P["d_skip"], W["w_glu"], P["b_glu"]],
        [(SSM_WIDTH, BF16), (SSM_WIDTH, F32)], [], [(SSM_WIDTH, SSM_WIDTH), (1, SSM_WIDTH), (1, SSM_WIDTH)],
        ts=TS, name="s5_glu_bwd")
    d_ysf2 = flat(d_y_s5)
    dxs = unflat(_matmul(d_ysf2, P["cc_big"], tb=True, name="s5_readout_dx"))
    d_cc = _matmul(flat(xs), d_ysf2, ta=True, name="s5_readout_dw")
    lam, g_ab = _scan_bwd(dxs, xs, P["a_row"])
    lam = flat(lam)
    d_us_mm = unflat(_matmul(lam, P["bb_big"], tb=True, name="s5_bu_dx"))
    d_bb = _matmul(flat(us), lam, ta=True, name="s5_bu_dw")

    d_qkv = _attention_bwd(qkv, o_att, d_o_att, lse, P["slopes"])

    def f_add(a, b_):
        return a + b_

    (d_us,) = _rowwise(f_add, [(d_us_mm, SSM_WIDTH, 0), (d_us_skip, SSM_WIDTH, 0)], [], [],
                       [(SSM_WIDTH, BF16)], [], [], ts=TS, name="s5_input_grad")
    d_qkvf = flat(d_qkv)
    d_usf = flat(d_us)
    d_gatesf = flat(d_gates)
    d_u1 = (_matmul(d_qkvf, W["w_qkv"], tb=True, out_dtype=BF16, name="proj_qkv_dx"),
            _matmul(d_usf, W["w_us"], tb=True, out_dtype=BF16, name="proj_ssm_in_dx"),
            _matmul(d_gatesf, W["w_gates"], tb=True, out_dtype=BF16, name="proj_gates_dx"))
    d_w_in = jnp.concatenate(
        [_unpair_qkv_columns(_matmul(u1f, d_qkvf, ta=True, out_dtype=BF16, name="proj_qkv_dw")),
         _matmul(u1f, d_usf, ta=True, out_dtype=BF16, name="proj_ssm_in_dw"),
         _matmul(u1f, d_gatesf, ta=True, out_dtype=BF16, name="proj_gates_dw")], axis=1)

    def f_modnorm_bwd_in(du0, du1, du2, h, dres, sc, g):
        du = du0 + du1 + du2
        r = _rms_scale(h)
        nh = h * r
        dn = du * (1.0 + sc)
        gy = dn * g
        dh = dres + r * (gy - nh * jnp.mean(gy * nh, axis=-1, keepdims=True))
        return (dh, _col_sum(du), _col_sum(du * nh * g), _col_sum(dn * nh))

    grad_x, d_sh1, d_sc1, d_g_mix = _rowwise(
        f_modnorm_bwd_in, [(unflat(d_u1[0]), D, 0), (unflat(d_u1[1]), D, 0), (unflat(d_u1[2]), D, 0), (x, D, 0),
                           (dh1, D, 0)], [mod_col(1)], [P["g_mix"]],
        [(D, F32)], [D, D], [(1, D)], ts=TS, name="modnorm_mix_bwd")

    d_mod = jnp.concatenate([d_sh1, d_sc1, d_gt1, d_sh2, d_sc2, d_gt2], axis=-1)
    big = dict(w_in=d_w_in, w_glu=d_w_glu, w_proj_att=d_w_proj_att, w_proj_ssm=d_w_proj_ssm, w_out=d_w_out,
               w_up_slots=d_w_up, w_conv=d_w_conv, w_down=d_w_down)
    g_ab_re, g_ab_im = _deinterleave(g_ab)
    d_bb_re, d_bb_im = _deinterleave(d_bb)
    d_cc_re, d_cc_im = (t.T for t in _deinterleave(d_cc.T))
    small = dict(g_mix=d_g_mix, b_gate=jnp.concatenate([d_bga, d_bgs], axis=1), g_ab_re=g_ab_re, g_ab_im=g_ab_im,
                 d_bb_re=d_bb_re, d_bb_im=d_bb_im, d_cc_re=d_cc_re, d_cc_im=d_cc_im, d_skip=d_d_skip,
                 b_glu=d_b_glu, g_ffn=d_g_ffn, b_conv=d_b_conv, g_final=d_g_final, loss_cols=loss_cols)
    return grad_x, d_mod, big, small


def _block_diag_in(bb):
    t = bb.reshape(SSM_GROUPS, SSM_STATE, SSM_GROUP_CH)
    eye = jnp.eye(SSM_GROUPS, dtype=bb.dtype)
    return jnp.einsum("gnc,gh->gchn", t, eye).reshape(SSM_WIDTH, SSM_COLS)


def _block_diag_out(cm):
    eye = jnp.eye(SSM_GROUPS, dtype=cm.dtype)
    return jnp.einsum("gcn,gh->gnhc", cm, eye).reshape(SSM_COLS, SSM_WIDTH)


def _diag_blocks_in(m):
    t = m.reshape(SSM_GROUPS, SSM_GROUP_CH, SSM_GROUPS, SSM_STATE)
    idx = jnp.arange(SSM_GROUPS)
    return t[idx, :, idx, :].transpose(0, 2, 1).reshape(SSM_COLS, SSM_GROUP_CH)


def _diag_blocks_out(m):
    t = m.reshape(SSM_GROUPS, SSM_STATE, SSM_GROUPS, SSM_GROUP_CH)
    idx = jnp.arange(SSM_GROUPS)
    return t[idx, :, idx, :].transpose(0, 2, 1)


def _pair_qkv_columns(w):
    lead = w.shape[:-1]
    return w.reshape(lead + (3, N_HEADS // 2, LANES)).swapaxes(-3, -2).reshape(lead + (3 * ATT_WIDTH,))


def _unpair_qkv_columns(w):
    lead = w.shape[:-1]
    return w.reshape(lead + (N_HEADS // 2, 3, LANES)).swapaxes(-3, -2).reshape(lead + (3 * ATT_WIDTH,))


def _interleave(re, im):
    lead = re.shape[:-1]
    g = lambda a: a.reshape(lead + (SSM_COLS // SCAN_COLS, 1, SCAN_COLS))
    return jnp.concatenate([g(re), g(im)], axis=-2).reshape(lead + (2 * SSM_COLS,))


def _deinterleave(x):
    lead = x.shape[:-1]
    t = x.reshape(lead + (SSM_COLS // SCAN_COLS, 2, SCAN_COLS))
    return t[..., 0, :].reshape(lead + (SSM_COLS,)), t[..., 1, :].reshape(lead + (SSM_COLS,))


def _cols_to_slots(g):
    R = g.shape[0]
    return g.reshape(R, N_DEV, g.shape[1] // N_DEV).transpose(1, 0, 2)


def _slots_to_cols(g):
    return g.transpose(1, 0, 2).reshape(g.shape[1], N_DEV * g.shape[2])


SMALL_ORDER = ("b_ada", "g_mix", "b_gate", "a_re", "a_im", "log_dt", "b_re", "b_im", "c_re", "c_im", "d_skip",
               "b_glu", "g_ffn", "b_conv", "g_final")


def _pack(arrs):
    pieces, offs, row = [], [], 0
    for a in arrs:
        f = a.reshape(-1).astype(F32)
        n = f.shape[0]
        rows = -(-n // LANES)
        pieces.append(jnp.pad(f, (0, rows * LANES - n)))
        offs.append((row, n))
        row += rows
    return jnp.concatenate(pieces).reshape(row, LANES), offs


def _unpack(packed, offs, shapes):
    flat = packed.reshape(-1)
    return [flat[r * LANES:r * LANES + n].reshape(s) for (r, n), s in zip(offs, shapes)]


def kernel(x, c, w_ada, b_ada, g_mix, w_in, b_gate, a_re, a_im, log_dt, b_re, b_im, c_re, c_im, d_skip, w_glu, b_glu, w_proj_att, w_proj_ssm, w_out, g_ffn, w_up, w_conv, b_conv, w_down, g_final, loss_target, m_w_ada, m_b_ada, m_g_mix, m_w_in, m_b_gate, m_a_re, m_a_im, m_log_dt, m_b_re, m_b_im, m_c_re, m_c_im, m_d_skip, m_w_glu, m_b_glu, m_w_proj_att, m_w_proj_ssm, m_w_out, m_g_ffn, m_w_up, m_w_conv, m_b_conv, m_w_down, m_g_final, v_w_ada, v_b_ada, v_g_mix, v_w_in, v_b_gate, v_a_re, v_a_im, v_log_dt, v_b_re, v_b_im, v_c_re, v_c_im, v_d_skip, v_w_glu, v_b_glu, v_w_proj_att, v_w_proj_ssm, v_w_out, v_g_ffn, v_w_up, v_w_conv, v_b_conv, v_w_down, v_g_final):
    args = dict(locals())
    B, S, D = x.shape
    me = 4 * lax.axis_index("x") + 2 * lax.axis_index("y") + lax.axis_index("c")
    bf = lambda w: w[0].astype(BF16)

    gathered = _gather_all([c, bf(w_in), bf(w_glu), bf(w_proj_att), bf(w_proj_ssm), bf(w_out), bf(w_up), w_conv[0],
                            bf(w_down)], name="gather_weights")
    c_all = gathered[0].reshape(N_DEV * B, D)
    w_in_full = _slots_to_cols(gathered[1])
    n_qkv = 3 * ATT_WIDTH
    W = dict(w_qkv=_pair_qkv_columns(w_in_full[:, :n_qkv]), w_us=w_in_full[:, n_qkv:n_qkv + SSM_WIDTH],
             w_gates=w_in_full[:, n_qkv + SSM_WIDTH:], w_glu=gathered[2].reshape(SSM_WIDTH, SSM_WIDTH),
             w_proj_att=_slots_to_cols(gathered[3]), w_proj_ssm=_slots_to_cols(gathered[4]),
             w_out=gathered[5].reshape(D, D), w_up=gathered[6],
             w_down=gathered[8].reshape(D_FF, D))
    w_conv_full = _slots_to_cols(gathered[7])

    n_ada = w_ada.shape[2]
    b_ada_cols = lax.dynamic_slice(b_ada, (0, me * n_ada), (1, n_ada))
    mod_part = _ada_fwd(c_all, w_ada[0], b_ada_cols)
    (mod_slots,) = _exchange([(mod_part.reshape(N_DEV, B, n_ada), True)], name="scatter_modulation")
    mod = mod_slots.transpose(1, 0, 2).reshape(B, 1, 6 * D)

    ab_re, ab_im, f_re, f_im = _s5_params(a_re[0], a_im[0], log_dt[0].reshape(SSM_GROUPS, 1))
    col = lambda a: a.reshape(SSM_COLS, 1)
    b_re2, b_im2 = b_re[0].reshape(SSM_COLS, SSM_GROUP_CH), b_im[0].reshape(SSM_COLS, SSM_GROUP_CH)
    bb_re, bb_im = _s5_input_matrix(col(f_re), col(f_im), b_re2, b_im2)
    slopes = jnp.asarray([2.0 ** (-8.0 * (h + 1) / N_HEADS) for h in range(N_HEADS)], F32)
    P = dict(g_mix=g_mix, g_ffn=g_ffn, g_final=g_final.reshape(1, D), b_gate=b_gate, d_skip=d_skip, b_glu=b_glu,
             b_conv=b_conv, w_conv=w_conv_full, slopes=slopes,
             a_row=_interleave(ab_re.reshape(1, SSM_COLS), ab_im.reshape(1, SSM_COLS)),
             bb_big=_interleave(_block_diag_in(bb_re), _block_diag_in(bb_im)),
             cc_big=_interleave(_block_diag_out(c_re[0]).T, -_block_diag_out(c_im[0]).T).T)

    grad_x, d_mod, big, small = _local_step(x, mod, loss_target, W, P)

    small_list = [small["loss_cols"], small["g_mix"], small["b_gate"], small["g_ab_re"], small["g_ab_im"],
                  _diag_blocks_in(small["d_bb_re"]), _diag_blocks_in(small["d_bb_im"]),
                  _diag_blocks_out(small["d_cc_re"]), -_diag_blocks_out(small["d_cc_im"]),
                  small["g_ffn"], small["b_conv"], small["g_final"], small["d_skip"], small["b_glu"]]
    small_packed, small_offs = _pack(small_list)
    sharded = ("w_in", "w_glu", "w_proj_att", "w_proj_ssm", "w_out", "w_up", "w_conv", "w_down")
    slots = [_cols_to_slots(big["w_in"]), big["w_glu"].astype(BF16).reshape(N_DEV, SSM_WIDTH // N_DEV, SSM_WIDTH),
             _cols_to_slots(big["w_proj_att"]), _cols_to_slots(big["w_proj_ssm"]),
             big["w_out"].reshape(N_DEV, D // N_DEV, D), big["w_up_slots"],
             _cols_to_slots(big["w_conv"].astype(BF16)), big["w_down"].reshape(N_DEV, D_FF // N_DEV, D)]
    from_sibling = _sibling_swap(slots, name="swap_sibling_gradients")
    chip_sums = _chip_scatter(_add_pairs(slots, from_sibling), name="scatter_chip_gradients")
    small_all, dmod_slots = _gather_all([small_packed, d_mod.reshape(B, 6 * D)], name="gather_small_gradients")

    out = {}

    def update(name, parts):
        w2 = args[name][0]
        g, dl, mn, vn = _adamw(w2, args["m_" + name][0], args["v_" + name][0], parts, name="adamw_" + name)
        for key, val in (("grad_", g), ("delta_", dl), ("new_m_", mn), ("new_v_", vn)):
            out[key + name] = val[None]

    for name, parts in zip(sharded, chip_sums):
        update(name, parts)

    dmod_all = dmod_slots.reshape(N_DEV * B, 6 * D)
    dmod_cols = lax.dynamic_slice(dmod_all, (0, me * n_ada), (N_DEV * B, n_ada))
    d_w_ada, d_b_ada = _ada_bwd(c_all, dmod_all, dmod_cols)
    update("w_ada", d_w_ada[None])

    loss_row, loss_n = small_offs[0]
    small_sum, loss_vec = _sum_parts(small_all, (loss_row, loss_row + loss_n // LANES))
    shapes = [(1, D), (1, D), (1, 2 * D), (SSM_GROUPS, SSM_STATE), (SSM_GROUPS, SSM_STATE), (SSM_COLS, SSM_GROUP_CH),
              (SSM_COLS, SSM_GROUP_CH), (1, SSM_GROUPS, SSM_GROUP_CH, SSM_STATE),
              (1, SSM_GROUPS, SSM_GROUP_CH, SSM_STATE), (1, D), (1, D_FF), (D,), (1, SSM_WIDTH), (1, SSM_WIDTH)]
    (_, s_g_mix, s_b_gate, s_ab_re, s_ab_im, s_bb_re, s_bb_im, s_c_re, s_c_im, s_g_ffn, s_b_conv, s_g_final,
     s_d_skip, s_b_glu) = _unpack(small_sum, small_offs, shapes)
    d_b_re2, d_b_im2, d_f_re, d_f_im = _s5_input_matrix_bwd(col(f_re), col(f_im), b_re2, b_im2, s_bb_re, s_bb_im)
    d_a_re, d_a_im, d_log_dt = _s5_params_bwd(a_re[0], a_im[0], log_dt[0].reshape(SSM_GROUPS, 1), s_ab_re, s_ab_im,
                                              d_f_re.reshape(SSM_GROUPS, SSM_STATE),
                                              d_f_im.reshape(SSM_GROUPS, SSM_STATE))
    grads_small = dict(b_ada=d_b_ada, g_mix=s_g_mix, b_gate=s_b_gate, a_re=d_a_re[None], a_im=d_a_im[None],
                       log_dt=d_log_dt.reshape(1, SSM_GROUPS), b_re=d_b_re2.reshape(b_re.shape),
                       b_im=d_b_im2.reshape(b_im.shape), c_re=s_c_re, c_im=s_c_im, d_skip=s_d_skip, b_glu=s_b_glu,
                       g_ffn=s_g_ffn, b_conv=s_b_conv, g_final=s_g_final)
    w_pack, offs = _pack([args[n] for n in SMALL_ORDER])
    m_pack, _ = _pack([args["m_" + n] for n in SMALL_ORDER])
    v_pack, _ = _pack([args["v_" + n] for n in SMALL_ORDER])
    g_pack, _ = _pack([grads_small[n] for n in SMALL_ORDER])
    res = _adamw(w_pack, m_pack, v_pack, g_pack[None], name="adamw_small")
    shapes_small = [args[n].shape for n in SMALL_ORDER]
    for key, packed in zip(("grad_", "delta_", "new_m_", "new_v_"), res):
        for n, val in zip(SMALL_ORDER, _unpack(packed, offs, shapes_small)):
            out[key + n] = val

    order = ["w_ada", "b_ada", "g_mix", "w_in", "b_gate", "a_re", "a_im", "log_dt", "b_re", "b_im", "c_re", "c_im",
             "d_skip", "w_glu", "b_glu", "w_proj_att", "w_proj_ssm", "w_out", "g_ffn", "w_up", "w_conv", "b_conv",
             "w_down", "g_final"]
    loss = loss_vec[0, 0]
    return (loss, grad_x, *[out[k + n] for k in ("grad_", "delta_", "new_m_", "new_v_") for n in order])
```

```python
import math

import jax
import jax.numpy as jnp
from jax import lax
from jax.experimental import pallas as pl
from jax.experimental.pallas import tpu as pltpu

F32 = jnp.float32
BF16 = jnp.bfloat16

N_DEV = 8
D_MODEL = 1024
N_HEADS = 8
HEAD_DIM = 64
ATT_WIDTH = N_HEADS * HEAD_DIM
DILATIONS = (1, 4, 16)
WIN = 128
SSM_GROUPS = 16
SSM_GROUP_CH = 16
SSM_WIDTH = SSM_GROUPS * SSM_GROUP_CH
SSM_STATE = 64
SSM_COLS = SSM_GROUPS * SSM_STATE
D_FF = 2048
EPS = 1e-6
NEG_INF = -1e30
ADAM_LR, ADAM_B1, ADAM_B2, ADAM_EPS, ADAM_WD, ADAM_STEP = 0.001, 0.9, 0.999, 1e-08, 0.01, 10

V7X_VMEM_LIMIT = 56 * 1024 * 1024
LANES = 128


def _params(n_grid):
    return pltpu.CompilerParams(dimension_semantics=("arbitrary",) * n_grid,
                                vmem_limit_bytes=V7X_VMEM_LIMIT)


def _tile(n, pref):
    if n <= pref:
        return n
    t = (pref // LANES) * LANES
    while t > 0:
        if n % t == 0:
            return t
        t -= LANES
    return n


def _matmul(a, b, *, ta=False, tb=False, out_dtype=F32, name):
    if ta:
        K, M = a.shape
    else:
        M, K = a.shape
    if tb:
        N, K2 = b.shape
    else:
        K2, N = b.shape
    assert K == K2, (a.shape, b.shape)
    if ta:
        tm, tn, tk = _tile(M, 1024), _tile(N, 2048), _tile(K, 512)
    else:
        tm, tk = _tile(M, 512), _tile(K, 4096)
        tn = _tile(N, 2048 if K <= 2048 else 1024)
    nk = K // tk
    dn = (((0,) if ta else (1,), (1,) if tb else (0,)), ((), ()))

    def body(a_ref, b_ref, o_ref, acc_ref):
        k = pl.program_id(2)
        part = lax.dot_general(a_ref[...].astype(BF16), b_ref[...].astype(BF16), dn, preferred_element_type=F32)
        if nk == 1:
            o_ref[...] = part.astype(o_ref.dtype)
            return

        @pl.when(k == 0)
        def _():
            acc_ref[...] = jnp.zeros_like(acc_ref)

        acc_ref[...] += part

        @pl.when(k == nk - 1)
        def _():
            o_ref[...] = acc_ref[...].astype(o_ref.dtype)

    a_spec = (pl.BlockSpec((tk, tm), lambda j, i, k: (k, i)) if ta
              else pl.BlockSpec((tm, tk), lambda j, i, k: (i, k)))
    b_spec = (pl.BlockSpec((tn, tk), lambda j, i, k: (j, k)) if tb
              else pl.BlockSpec((tk, tn), lambda j, i, k: (k, j)))
    return pl.pallas_call(
        body, name=name, grid=(N // tn, M // tm, nk),
        in_specs=[a_spec, b_spec],
        out_specs=pl.BlockSpec((tm, tn), lambda j, i, k: (i, j)),
        out_shape=jax.ShapeDtypeStruct((M, N), out_dtype),
        scratch_shapes=[pltpu.VMEM((tm, tn) if nk > 1 else (8, LANES), F32)],
        compiler_params=_params(3),
    )(a, b)


HALF = 256
UP_SLOTS = N_DEV // 2
UP_GROUP = 4 * HALF


def _group_weight(w_ref):
    return jnp.concatenate([w_ref[0, :, :HALF], w_ref[1, :, :HALF], w_ref[0, :, HALF:], w_ref[1, :, HALF:]], axis=1)


def _up_weight_spec(K, index):
    return pl.BlockSpec((2, None, K, 2 * HALF), index)


def _up_fwd(a, w3, name):
    M, K = a.shape
    tm = _tile(M, 1024)

    def body(a_ref, w_ref, o_ref):
        o_ref[...] = jnp.dot(a_ref[...].astype(BF16), _group_weight(w_ref),
                             preferred_element_type=F32).astype(o_ref.dtype)

    return pl.pallas_call(
        body, name=name, grid=(UP_SLOTS, M // tm),
        in_specs=[pl.BlockSpec((tm, K), lambda j, i: (i, 0)), _up_weight_spec(K, lambda j, i: (0, j, 0, 0))],
        out_specs=pl.BlockSpec((tm, UP_GROUP), lambda j, i: (i, j)),
        out_shape=jax.ShapeDtypeStruct((M, UP_SLOTS * UP_GROUP), BF16), compiler_params=_params(2),
    )(a, w3.reshape(2, UP_SLOTS, K, 2 * HALF))


def _up_dx(d, w3, name):
    M = d.shape[0]
    K = w3.shape[1]
    tm = _tile(M, 1024)

    def body(d_ref, w_ref, o_ref, acc_ref):
        j = pl.program_id(1)

        @pl.when(j == 0)
        def _():
            acc_ref[...] = jnp.zeros_like(acc_ref)

        acc_ref[...] += lax.dot_general(d_ref[...], _group_weight(w_ref), _NT, preferred_element_type=F32)

        @pl.when(j == UP_SLOTS - 1)
        def _():
            o_ref[...] = acc_ref[...].astype(o_ref.dtype)

    return pl.pallas_call(
        body, name=name, grid=(M // tm, UP_SLOTS),
        in_specs=[pl.BlockSpec((tm, UP_GROUP), lambda i, j: (i, j)), _up_weight_spec(K, lambda i, j: (0, j, 0, 0))],
        out_specs=pl.BlockSpec((tm, K), lambda i, j: (i, 0)),
        out_shape=jax.ShapeDtypeStruct((M, K), BF16), scratch_shapes=[pltpu.VMEM((tm, K), F32)],
        compiler_params=_params(2),
    )(d, w3.reshape(2, UP_SLOTS, K, 2 * HALF))


def _up_dw(a, d, name):
    M, K = a.shape
    tk = _tile(M, 512)
    nk = M // tk

    def body(a_ref, d_ref, o_ref, acc_ref):
        k = pl.program_id(1)

        @pl.when(k == 0)
        def _():
            acc_ref[...] = jnp.zeros_like(acc_ref)

        acc_ref[...] += lax.dot_general(a_ref[...], d_ref[...], _TN, preferred_element_type=F32)

        @pl.when(k == nk - 1)
        def _():
            for half in range(2):
                for part in range(2):
                    lo = (2 * half + part) * HALF
                    o_ref[part, :, half * HALF:(half + 1) * HALF] = acc_ref[:, lo:lo + HALF].astype(o_ref.dtype)

    out = pl.pallas_call(
        body, name=name, grid=(UP_SLOTS, nk),
        in_specs=[pl.BlockSpec((tk, K), lambda j, k: (k, 0)), pl.BlockSpec((tk, UP_GROUP), lambda j, k: (k, j))],
        out_specs=_up_weight_spec(K, lambda j, k: (0, j, 0, 0)),
        out_shape=jax.ShapeDtypeStruct((2, UP_SLOTS, K, 2 * HALF), BF16),
        scratch_shapes=[pltpu.VMEM((K, UP_GROUP), F32)], compiler_params=_params(2),
    )(a, d)
    return out.reshape(N_DEV, K, 2 * HALF)


def _rowwise(fn, rows, bvecs, consts, out_rows, out_b, out_g, *, ts, name):
    B, S = rows[0][0].shape[:2]
    nin = len(rows) + len(bvecs) + len(consts)
    nr, nb, ng = len(out_rows), len(out_b), len(out_g)

    def body(*refs):
        b = pl.program_id(0)
        s = pl.program_id(1)
        vals = [r[...] for r in refs[:nin]]
        vals[:len(rows)] = [v.astype(F32) for v in vals[:len(rows)]]
        outs = fn(*vals)
        if not isinstance(outs, (tuple, list)):
            outs = (outs,)
        orefs = refs[nin:]
        for i in range(nr):
            orefs[i][...] = outs[i].astype(orefs[i].dtype)
        for i in range(nb):
            ref = orefs[nr + i]

            @pl.when(s == 0)
            def _(ref=ref):
                ref[...] = jnp.zeros_like(ref)

            ref[...] += outs[nr + i]
        for i in range(ng):
            ref = orefs[nr + nb + i]

            @pl.when((s == 0) & (b == 0))
            def _(ref=ref):
                ref[...] = jnp.zeros_like(ref)

            ref[...] += outs[nr + nb + i]

    in_specs = ([pl.BlockSpec((None, ts, cb), lambda b, s, ci=ci: (b, s, ci)) for (_, cb, ci) in rows]
                + [pl.BlockSpec((None, 1, cb), lambda b, s, ci=ci: (b, 0, ci)) for (_, cb, ci) in bvecs]
                + [pl.BlockSpec(a.shape, lambda b, s: (0, 0)) for a in consts])
    out_shape = ([jax.ShapeDtypeStruct((B, S, c), dt) for (c, dt) in out_rows]
                 + [jax.ShapeDtypeStruct((B, 1, c), F32) for c in out_b]
                 + [jax.ShapeDtypeStruct(rc, F32) for rc in out_g])
    out_specs = ([pl.BlockSpec((None, ts, c), lambda b, s: (b, s, 0)) for (c, _) in out_rows]
                 + [pl.BlockSpec((None, 1, c), lambda b, s: (b, 0, 0)) for c in out_b]
                 + [pl.BlockSpec(rc, lambda b, s: (0, 0)) for rc in out_g])
    args = [a for (a, _, _) in rows] + [a for (a, _, _) in bvecs] + list(consts)
    return pl.pallas_call(
        body, name=name, grid=(B, S // ts), in_specs=in_specs, out_specs=out_specs,
        out_shape=out_shape, compiler_params=_params(2),
    )(*args)


def _col_sum(v):
    return jnp.sum(v, axis=0, keepdims=True)


def _rms_scale(h):
    return lax.rsqrt(jnp.mean(h * h, axis=-1, keepdims=True) + EPS)


def _sigmoid(v):
    return 1.0 / (1.0 + jnp.exp(-v))


ATT_SCALE = HEAD_DIM ** -0.5
COPY_ROWS = 256
_NT = (((1,), (1,)), ((), ()))
_TN = (((0,), (0,)), ((), ()))


def _row_chunks(d, seq):
    sub = seq // d
    out = []
    for r in range(d):
        for c0 in range(0, sub, COPY_ROWS):
            n = min(COPY_ROWS, sub - c0)
            out.append((pl.ds(r + c0 * d, n, stride=d), r * sub + c0, n))
    return out


ATT_UNROLL = 8
KEYS = 2 * WIN


def _zero_once(refs):
    @pl.when((pl.program_id(0) == 0) & (pl.program_id(1) == 0))
    def _():
        for r in refs:
            r[...] = jnp.zeros_like(r)


def _pair_bias(bias_ref, slopes_ref, hp, d, key_major):
    shape = (KEYS, WIN) if key_major else (WIN, KEYS)
    qi = lax.broadcasted_iota(jnp.int32, shape, 1 if key_major else 0)
    kj = lax.broadcasted_iota(jnp.int32, shape, 0 if key_major else 1)
    dist = WIN + qi - kj
    valid = (dist >= 0) & (dist <= WIN)
    distf = dist.astype(F32)
    for h in range(2):
        slope_d = slopes_ref[2 * hp + h] * float(d)
        with_prev = jnp.where(valid, -(slope_d * distf), NEG_INF)
        no_prev = jnp.where(kj >= WIN, with_prev, NEG_INF)
        span = slice(h * KEYS, (h + 1) * KEYS)
        if key_major:
            bias_ref[1, span, :] = with_prev
            bias_ref[0, span, :] = no_prev
        else:
            bias_ref[1, :, span] = with_prev
            bias_ref[0, :, span] = no_prev


def _stack_heads(v):
    first = lax.broadcasted_iota(jnp.int32, v.shape, 1) < HEAD_DIM
    zero = jnp.zeros_like(v)
    return jnp.concatenate([jnp.where(first, v, zero), jnp.where(first, zero, v)], axis=0)


def _per_head(c0, c1, n):
    return jnp.where(lax.broadcasted_iota(jnp.int32, (n, LANES), 1) < HEAD_DIM, c0, c1)


def _qkv_spec(seq, j):
    return pl.BlockSpec((None, seq, LANES), lambda b, hp: (b, 0, 3 * hp + j))


def _attention_fwd(qkv, slopes):
    B, S, _ = qkv.shape
    n_blk = S // WIN
    n_pair = N_HEADS // 2

    def body(slopes_ref, q_ref, k_ref, v_ref, o_ref, lse_ref, qp, kp, vp, bias, acc, mx, sm, acc_n, mx_n, sm_n):
        hp = pl.program_id(1)
        _zero_once((kp, vp))
        for p, d in enumerate(DILATIONS):
            nb = n_blk // d
            chunks = _row_chunks(d, S)
            for src, dst, n in chunks:
                qp[dst:dst + n, :] = (q_ref[src, :] * ATT_SCALE).astype(BF16)
                kp[WIN + dst:WIN + dst + n, :] = k_ref[src, :].astype(BF16)
                vp[WIN + dst:WIN + dst + n, :] = v_ref[src, :].astype(BF16)
            _pair_bias(bias, slopes_ref, hp, d, key_major=False)
            acc_t, mx_t, sm_t = (acc_n, mx_n, sm_n) if d == 1 else (acc, mx, sm)

            def block(i, carry, p=p, nb=nb, acc_t=acc_t, mx_t=mx_t, sm_t=sm_t):
                cur = pl.ds(pl.multiple_of(i * WIN, WIN), WIN)
                keys = pl.ds(pl.multiple_of(i * WIN, WIN), KEYS)
                flag = ((i % nb) > 0).astype(jnp.int32)
                s = lax.dot_general(qp[cur, :], _stack_heads(kp[keys, :]), _NT, preferred_element_type=F32)
                s = s + bias[flag]
                es, ms, ls = [], [], []
                for h in range(2):
                    sh = s[:, h * KEYS:(h + 1) * KEYS]
                    m = jnp.max(jnp.maximum(sh[:, :WIN], sh[:, WIN:]), axis=1, keepdims=True)
                    e = jnp.exp(sh - m)
                    es.append(e.astype(BF16))
                    ms.append(m)
                    ls.append(jnp.sum(e[:, :WIN] + e[:, WIN:], axis=1, keepdims=True))
                acc_t[p, cur, :] = jnp.dot(jnp.concatenate(es, axis=1), _stack_heads(vp[keys, :]),
                                           preferred_element_type=F32)
                mx_t[p, cur, :] = _per_head(ms[0], ms[1], WIN)
                sm_t[p, cur, :] = _per_head(ls[0], ls[1], WIN)
                return carry

            lax.fori_loop(0, n_blk, block, 0, unroll=ATT_UNROLL)
            if d > 1:
                for src, dst, n in chunks:
                    acc_n[p, src, :] = acc[p, dst:dst + n, :]
                    mx_n[p, src, :] = mx[p, dst:dst + n, :]
                    sm_n[p, src, :] = sm[p, dst:dst + n, :]

        chunk = 256

        def merge(i, carry):
            rows = pl.ds(pl.multiple_of(i * chunk, chunk), chunk)
            ms = [mx_n[p, rows, :] for p in range(3)]
            m = jnp.maximum(jnp.maximum(ms[0], ms[1]), ms[2])
            ws = [jnp.exp(mp - m) for mp in ms]
            l = ws[0] * sm_n[0, rows, :] + ws[1] * sm_n[1, rows, :] + ws[2] * sm_n[2, rows, :]
            o = (ws[0] * acc_n[0, rows, :] + ws[1] * acc_n[1, rows, :] + ws[2] * acc_n[2, rows, :]) / l
            o_ref[rows, :] = o.astype(o_ref.dtype)
            lse = m + jnp.log(l)
            for h in range(2):
                lse_ref[rows, h:h + 1] = lse[:, h * HEAD_DIM:h * HEAD_DIM + 1]
            return carry

        lax.fori_loop(0, S // chunk, merge, 0)

    return pl.pallas_call(
        body, name="attention_fwd", grid=(B, n_pair),
        in_specs=[pl.BlockSpec(memory_space=pltpu.SMEM), _qkv_spec(S, 0), _qkv_spec(S, 1), _qkv_spec(S, 2)],
        out_specs=[pl.BlockSpec((None, S, LANES), lambda b, hp: (b, 0, hp)),
                   pl.BlockSpec((None, None, S, 2), lambda b, hp: (b, hp, 0, 0))],
        out_shape=[jax.ShapeDtypeStruct((B, S, ATT_WIDTH), BF16),
                   jax.ShapeDtypeStruct((B, n_pair, S, 2), F32)],
        scratch_shapes=[pltpu.VMEM((S, LANES), BF16), pltpu.VMEM((S + WIN, LANES), BF16),
                        pltpu.VMEM((S + WIN, LANES), BF16), pltpu.VMEM((2, WIN, 2 * KEYS), F32)]
        + [pltpu.VMEM((3, S, LANES), F32)] * 6,
        compiler_params=_params(2),
    )(slopes, qkv, qkv, qkv)


def _attention_bwd(qkv, o, do, lse, slopes):
    B, S, _ = qkv.shape
    n_blk = S // WIN
    n_pair = N_HEADS // 2

    def body(slopes_ref, q_ref, k_ref, v_ref, o_ref, do_ref, lse_ref, dx_ref,
             qp, dop, kp, vp, aux, auxp, aux_t, bias_t, dqp, dvk, dq_n, dk_n, dv_n):
        hp = pl.program_id(1)
        aux[...] = jnp.zeros_like(aux)
        for c0 in range(0, S, COPY_ROWS):
            rows = slice(c0, c0 + COPY_ROWS)
            prod = do_ref[rows, :] * o_ref[rows, :].astype(F32)
            for h in range(2):
                aux[rows, 2 * h:2 * h + 1] = lse_ref[rows, h:h + 1]
                aux[rows, 2 * h + 1:2 * h + 2] = jnp.sum(prod[:, h * HEAD_DIM:(h + 1) * HEAD_DIM], axis=1,
                                                         keepdims=True)
        dq_n[...] = jnp.zeros_like(dq_n)
        dk_n[...] = jnp.zeros_like(dk_n)
        dv_n[...] = jnp.zeros_like(dv_n)
        _zero_once((kp, vp))
        for p, d in enumerate(DILATIONS):
            nb = n_blk // d
            chunks = _row_chunks(d, S)
            for src, dst, n in chunks:
                auxp[dst:dst + n, :] = aux[src, :]
                qp[dst:dst + n, :] = (q_ref[src, :] * ATT_SCALE).astype(BF16)
                dop[dst:dst + n, :] = do_ref[src, :].astype(BF16)
                kp[WIN + dst:WIN + dst + n, :] = k_ref[src, :].astype(BF16)
                vp[WIN + dst:WIN + dst + n, :] = v_ref[src, :].astype(BF16)
            for i in range(n_blk):
                aux_t[i] = auxp[i * WIN:(i + 1) * WIN, :].T[0:8, :]
            _pair_bias(bias_t, slopes_ref, hp, d, key_major=True)
            dvk[...] = jnp.zeros_like(dvk)

            def block(i, carry, nb=nb):
                cur = pl.ds(pl.multiple_of(i * WIN, WIN), WIN)
                keys = pl.ds(pl.multiple_of(i * WIN, WIN), KEYS)
                flag = ((i % nb) > 0).astype(jnp.int32)
                q2, do2 = qp[cur, :], dop[cur, :]
                kc = _stack_heads(kp[keys, :])
                s_t = lax.dot_general(kc, q2, _NT, preferred_element_type=F32) + bias_t[flag]
                dp_t = lax.dot_general(_stack_heads(vp[keys, :]), do2, _NT, preferred_element_type=F32)
                ps, dss = [], []
                for h in range(2):
                    span = slice(h * KEYS, (h + 1) * KEYS)
                    p_t = jnp.exp(s_t[span, :] - aux_t[i, 2 * h:2 * h + 1, :])
                    ds_t = p_t * (dp_t[span, :] - aux_t[i, 2 * h + 1:2 * h + 2, :])
                    ps.append(p_t.astype(BF16))
                    dss.append(ds_t.astype(BF16))
                do_rows, q_rows = _stack_heads(do2), _stack_heads(q2)
                zr = jnp.zeros_like(do_rows)
                rhs = jnp.concatenate([jnp.concatenate([do_rows, zr], axis=1),
                                       jnp.concatenate([zr, q_rows], axis=1)], axis=0)
                dvk[keys, :] += jnp.dot(jnp.concatenate(ps + dss, axis=1), rhs, preferred_element_type=F32)
                dqp[cur, :] = lax.dot_general(jnp.concatenate(dss, axis=0), kc, _TN, preferred_element_type=F32)
                return carry

            lax.fori_loop(0, n_blk, block, 0, unroll=ATT_UNROLL)
            for src, dst, n in chunks:
                dq_n[src, :] += dqp[dst:dst + n, :]
                dv_n[src, :] += dvk[WIN + dst:WIN + dst + n, :LANES]
                dk_n[src, :] += dvk[WIN + dst:WIN + dst + n, LANES:]
        for c0 in range(0, S, COPY_ROWS):
            rows = slice(c0, c0 + COPY_ROWS)
            dx_ref[rows, 0:LANES] = (dq_n[rows, :] * ATT_SCALE).astype(dx_ref.dtype)
            dx_ref[rows, LANES:2 * LANES] = dk_n[rows, :].astype(dx_ref.dtype)
            dx_ref[rows, 2 * LANES:3 * LANES] = dv_n[rows, :].astype(dx_ref.dtype)

    pair = lambda width: pl.BlockSpec((None, S, width), lambda b, hp: (b, 0, hp))
    vm = lambda shape, dt: pltpu.VMEM(shape, dt)
    return pl.pallas_call(
        body, name="attention_bwd", grid=(B, n_pair),
        in_specs=[pl.BlockSpec(memory_space=pltpu.SMEM), _qkv_spec(S, 0), _qkv_spec(S, 1), _qkv_spec(S, 2),
                  pair(LANES), pair(LANES), pl.BlockSpec((None, None, S, 2), lambda b, hp: (b, hp, 0, 0))],
        out_specs=pair(3 * LANES),
        out_shape=jax.ShapeDtypeStruct((B, S, 3 * ATT_WIDTH), BF16),
        scratch_shapes=[vm((S, LANES), BF16), vm((S, LANES), BF16),
                        vm((S + WIN, LANES), BF16), vm((S + WIN, LANES), BF16),
                        vm((S, LANES), F32), vm((S, LANES), F32), vm((n_blk, 8, WIN), F32),
                        vm((2, 2 * KEYS, WIN), F32),
                        vm((S, LANES), F32), vm((S + WIN, 2 * LANES), F32),
                        vm((S, LANES), F32), vm((S, LANES), F32), vm((S, LANES), F32)],
        compiler_params=_params(2),
    )(slopes, qkv, qkv, qkv, o, do, lse)


SCAN_COLS = 256
SCAN_ROWS = 8


def _rows_to_tile(rows):
    rid = lax.broadcasted_iota(jnp.int32, (SCAN_ROWS, rows[0].shape[1]), 0)
    tile = jnp.broadcast_to(rows[0], rid.shape)
    for k in range(1, SCAN_ROWS):
        tile = jnp.where(rid == k, rows[k], tile)
    return tile


SCAN_UNROLL = 4


def _complex_powers(ar, ai, n):
    out = [(ar, ai)]
    for _ in range(n - 1):
        pr, pi = out[-1]
        out.append((pr * ar - pi * ai, pr * ai + pi * ar))
    return out


def _round_multipliers(powers, rid, reverse):
    out = []
    for s in (1, 2, 4):
        keep = (rid < SCAN_ROWS - s) if reverse else (rid >= s)
        out.append((jnp.where(keep, powers[s - 1][0], 0.0), jnp.where(keep, powers[s - 1][1], 0.0)))
    return out


def _tile_scan(xr, xi, multipliers, reverse):
    for s, (mr, mi) in zip((1, 2, 4), multipliers):
        shift = SCAN_ROWS - s if reverse else s
        sr, si = pltpu.roll(xr, shift, 0), pltpu.roll(xi, shift, 0)
        xr, xi = xr + (mr * sr - mi * si), xi + (mr * si + mi * sr)
    return xr, xi


def _scan_fwd(bu, a_row):
    B, S, _ = bu.shape
    groups = 2
    nc = SSM_COLS // (groups * SCAN_COLS)
    nt = S // SCAN_ROWS
    LAST = slice(SCAN_ROWS - 1, SCAN_ROWS)

    def body(bu_ref, a_ref, xs_ref):
        rid = lax.broadcasted_iota(jnp.int32, (SCAN_ROWS, SCAN_COLS), 0)
        consts = []
        for g in range(groups):
            re = slice(2 * g * SCAN_COLS, (2 * g + 1) * SCAN_COLS)
            im = slice((2 * g + 1) * SCAN_COLS, (2 * g + 2) * SCAN_COLS)
            powers = _complex_powers(a_ref[:, re], a_ref[:, im], SCAN_ROWS)
            carry_mult = (_rows_to_tile([p[0] for p in powers]), _rows_to_tile([p[1] for p in powers]))
            consts.append((re, im, carry_mult, _round_multipliers(powers, rid, reverse=False)))

        def tile(i, carry):
            out = []
            for (re, im, (cr_t, ci_t), rounds), (cr, ci) in zip(consts, carry):
                xr, xi = _tile_scan(bu_ref[i, :, re], bu_ref[i, :, im], rounds, reverse=False)
                xs_ref[i, :, re] = xr + (cr_t * cr - ci_t * ci)
                xs_ref[i, :, im] = xi + (cr_t * ci + ci_t * cr)
                out.append((xs_ref[i, LAST, re], xs_ref[i, LAST, im]))
            return tuple(out)

        zero = jnp.zeros((1, SCAN_COLS), F32)
        lax.fori_loop(0, nt, tile, ((zero, zero),) * groups, unroll=SCAN_UNROLL)

    col = pl.BlockSpec((None, nt, SCAN_ROWS, 2 * groups * SCAN_COLS), lambda b, j: (b, 0, 0, j))
    xs = pl.pallas_call(
        body, name="s5_scan_fwd", grid=(B, nc),
        in_specs=[col, pl.BlockSpec((1, 2 * groups * SCAN_COLS), lambda b, j: (0, j))],
        out_specs=col,
        out_shape=jax.ShapeDtypeStruct((B, nt, SCAN_ROWS, 2 * SSM_COLS), F32),
        compiler_params=_params(2),
    )(bu.reshape(B, nt, SCAN_ROWS, 2 * SSM_COLS), a_row)
    return xs.reshape(B, S, 2 * SSM_COLS)


def _scan_bwd(dxs, xs, a_row):
    B, S, _ = dxs.shape
    nc = SSM_COLS // SCAN_COLS
    nt = S // SCAN_ROWS
    RE, IM = slice(0, SCAN_COLS), slice(SCAN_COLS, 2 * SCAN_COLS)
    FIRST, LAST = slice(0, 1), slice(SCAN_ROWS - 1, SCAN_ROWS)

    def body(d_ref, x_ref, a_ref, lam_ref, ga_ref):
        b = pl.program_id(1)
        powers = _complex_powers(a_ref[:, RE], -a_ref[:, IM], SCAN_ROWS)
        rid = lax.broadcasted_iota(jnp.int32, (SCAN_ROWS, SCAN_COLS), 0)
        cr_t = _rows_to_tile([powers[SCAN_ROWS - 1 - r][0] for r in range(SCAN_ROWS)])
        ci_t = _rows_to_tile([powers[SCAN_ROWS - 1 - r][1] for r in range(SCAN_ROWS)])
        rounds = _round_multipliers(powers, rid, reverse=True)

        @pl.when(b == 0)
        def _():
            ga_ref[...] = jnp.zeros_like(ga_ref)

        def tile(j, carry):
            cr, ci, accr, acci = carry
            i = nt - 1 - j
            lr, li = _tile_scan(d_ref[i, :, RE], d_ref[i, :, IM], rounds, reverse=True)
            lam_r = lr + (cr_t * cr - ci_t * ci)
            lam_i = li + (cr_t * ci + ci_t * cr)
            lam_ref[i, :, RE] = lam_r
            lam_ref[i, :, IM] = lam_i
            ip = jnp.maximum(i - 1, 0)
            keep = (i > 0).astype(F32)
            xpr = jnp.where(rid == 0, x_ref[ip, LAST, RE] * keep, pltpu.roll(x_ref[i, :, RE], 1, 0))
            xpi = jnp.where(rid == 0, x_ref[ip, LAST, IM] * keep, pltpu.roll(x_ref[i, :, IM], 1, 0))
            accr = accr + lam_r * xpr + lam_i * xpi
            acci = acci + lam_i * xpr - lam_r * xpi
            return lam_ref[i, FIRST, RE], lam_ref[i, FIRST, IM], accr, acci

        z1 = jnp.zeros((1, SCAN_COLS), F32)
        z8 = jnp.zeros((SCAN_ROWS, SCAN_COLS), F32)
        _, _, accr, acci = lax.fori_loop(0, nt, tile, (z1, z1, z8, z8), unroll=SCAN_UNROLL)
        ga_ref[:, RE] += _col_sum(accr)
        ga_ref[:, IM] += _col_sum(acci)

    col = pl.BlockSpec((None, nt, SCAN_ROWS, 2 * SCAN_COLS), lambda j, b: (b, 0, 0, j))
    par = pl.BlockSpec((1, 2 * SCAN_COLS), lambda j, b: (0, j))
    t4 = lambda a: a.reshape(B, nt, SCAN_ROWS, 2 * SSM_COLS)
    lam, g_a = pl.pallas_call(
        body, name="s5_scan_bwd", grid=(nc, B),
        in_specs=[col, col, par], out_specs=[col, par],
        out_shape=[jax.ShapeDtypeStruct((B, nt, SCAN_ROWS, 2 * SSM_COLS), F32),
                   jax.ShapeDtypeStruct((1, 2 * SSM_COLS), F32)],
        compiler_params=_params(2),
    )(t4(dxs), t4(xs), a_row)
    return lam.reshape(B, S, 2 * SSM_COLS), g_a


def _s5_discretise(lr, li, log_dt):
    dt = jnp.exp(log_dt)
    mag = jnp.exp(lr * dt)
    ang = li * dt
    ab_re, ab_im = mag * jnp.cos(ang), mag * jnp.sin(ang)
    nr, ni = ab_re - 1.0, ab_im
    den = lr * lr + li * li
    f_re = (nr * lr + ni * li) / den
    f_im = (ni * lr - nr * li) / den
    return dt, ab_re, ab_im, nr, ni, den, f_re, f_im


def _s5_params(a_re, a_im, log_dt):
    def body(lr_ref, li_ref, ld_ref, abr, abi, fr, fi):
        _, ab_re, ab_im, _, _, _, f_re, f_im = _s5_discretise(lr_ref[...], li_ref[...], ld_ref[...])
        abr[...] = ab_re
        abi[...] = ab_im
        fr[...] = f_re
        fi[...] = f_im

    return pl.pallas_call(body, name="s5_params",
                          out_shape=[jax.ShapeDtypeStruct(a_re.shape, F32)] * 4)(a_re, a_im, log_dt)


def _s5_input_matrix(f_re, f_im, b_re, b_im):
    def body(fr, fi, br, bi, o_re, o_im):
        o_re[...] = fr[...] * br[...] - fi[...] * bi[...]
        o_im[...] = fr[...] * bi[...] + fi[...] * br[...]

    return pl.pallas_call(body, name="s5_input_matrix",
                          out_shape=[jax.ShapeDtypeStruct(b_re.shape, F32)] * 2)(f_re, f_im, b_re, b_im)


def _s5_input_matrix_bwd(f_re, f_im, b_re, b_im, g_re, g_im):
    def body(fr, fi, br, bi, gr, gi, dbr, dbi, dfr, dfi):
        dbr[...] = fr[...] * gr[...] + fi[...] * gi[...]
        dbi[...] = fr[...] * gi[...] - fi[...] * gr[...]
        dfr[...] = jnp.sum(br[...] * gr[...] + bi[...] * gi[...], axis=1, keepdims=True)
        dfi[...] = jnp.sum(br[...] * gi[...] - bi[...] * gr[...], axis=1, keepdims=True)

    return pl.pallas_call(
        body, name="s5_input_matrix_bwd",
        out_shape=[jax.ShapeDtypeStruct(b_re.shape, F32)] * 2 + [jax.ShapeDtypeStruct(f_re.shape, F32)] * 2,
    )(f_re, f_im, b_re, b_im, g_re, g_im)


def _s5_params_bwd(a_re, a_im, log_dt, g_ab_re, g_ab_im, d_f_re, d_f_im):
    def body(lr_ref, li_ref, ld_ref, gar, gai, dfr, dfi, o_lr, o_li, o_ld):
        lr, li = lr_ref[...], li_ref[...]
        dt, ab_re, ab_im, nr, ni, den, f_re, f_im = _s5_discretise(lr, li, ld_ref[...])
        d_fr, d_fi = dfr[...], dfi[...]
        d_nr = (d_fr * lr - d_fi * li) / den
        d_ni = (d_fr * li + d_fi * lr) / den
        common = (d_fr * f_re + d_fi * f_im) * 2.0 / den
        d_lr = (d_fr * nr + d_fi * ni) / den - common * lr
        d_li = (d_fr * ni - d_fi * nr) / den - common * li
        d_abr = gar[...] + d_nr
        d_abi = gai[...] + d_ni
        d_mag_mag = d_abr * ab_re + d_abi * ab_im
        d_ang = d_abi * ab_re - d_abr * ab_im
        o_lr[...] = d_lr + d_mag_mag * dt
        o_li[...] = d_li + d_ang * dt
        o_ld[...] = jnp.sum(d_mag_mag * lr + d_ang * li, axis=1, keepdims=True) * dt

    return pl.pallas_call(
        body, name="s5_params_bwd",
        out_shape=[jax.ShapeDtypeStruct(a_re.shape, F32)] * 2 + [jax.ShapeDtypeStruct(log_dt.shape, F32)],
    )(a_re, a_im, log_dt, g_ab_re, g_ab_im, d_f_re, d_f_im)


CONV_COLS = 256


def _shift_down(v, j, row):
    return jnp.where(row >= j, pltpu.roll(v, j, 0), 0.0)


def _shift_up(v, j, row, seq):
    return jnp.where(row < seq - j, pltpu.roll(v, seq - j, 0), 0.0)


def _conv_fwd(up, w_conv, b_conv):
    B, S, _ = up.shape
    nj = D_FF // CONV_COLS

    def body(up_ref, w_ref, b_ref, ff_ref):
        a = up_ref[:, :CONV_COLS].astype(F32)
        val = up_ref[:, CONV_COLS:].astype(F32)
        row = lax.broadcasted_iota(jnp.int32, a.shape, 0)
        w0, w1, w2 = w_ref[0:1, :], w_ref[1:2, :], w_ref[2:3, :]
        conv = b_ref[...] + w0 * a + w1 * _shift_down(a, 1, row) + w2 * _shift_down(a, 2, row)
        ff_ref[...] = (conv * _sigmoid(conv) * val).astype(ff_ref.dtype)

    return pl.pallas_call(
        body, name="conv_gate_fwd", grid=(B, nj),
        in_specs=[pl.BlockSpec((None, S, 2 * CONV_COLS), lambda b, j: (b, 0, j)),
                  pl.BlockSpec((3, CONV_COLS), lambda b, j: (0, j)),
                  pl.BlockSpec((1, CONV_COLS), lambda b, j: (0, j))],
        out_specs=pl.BlockSpec((None, S, CONV_COLS), lambda b, j: (b, 0, j)),
        out_shape=jax.ShapeDtypeStruct((B, S, D_FF), BF16),
        compiler_params=_params(2),
    )(up, w_conv, b_conv)


def _conv_bwd(up, d_ff, w_conv, b_conv):
    B, S, _ = up.shape
    nj = D_FF // CONV_COLS

    def body(up_ref, dff_ref, w_ref, b_ref, dup_ref, dw_ref, db_ref):
        b = pl.program_id(1)
        a = up_ref[:, :CONV_COLS].astype(F32)
        val = up_ref[:, CONV_COLS:].astype(F32)
        row = lax.broadcasted_iota(jnp.int32, a.shape, 0)
        w0, w1, w2 = w_ref[0:1, :], w_ref[1:2, :], w_ref[2:3, :]
        a1, a2 = _shift_down(a, 1, row), _shift_down(a, 2, row)
        conv = b_ref[...] + w0 * a + w1 * a1 + w2 * a2
        sg = _sigmoid(conv)
        dff = dff_ref[...].astype(F32)
        d_val = dff * conv * sg
        dc = dff * val * (sg * (1.0 + conv * (1.0 - sg)))
        d_a = w0 * dc + w1 * _shift_up(dc, 1, row, S) + w2 * _shift_up(dc, 2, row, S)
        dup_ref[:, :CONV_COLS] = d_a.astype(dup_ref.dtype)
        dup_ref[:, CONV_COLS:] = d_val.astype(dup_ref.dtype)

        @pl.when(b == 0)
        def _():
            dw_ref[...] = jnp.zeros_like(dw_ref)
            db_ref[...] = jnp.zeros_like(db_ref)

        dw_ref[0:1, :] += _col_sum(dc * a)
        dw_ref[1:2, :] += _col_sum(dc * a1)
        dw_ref[2:3, :] += _col_sum(dc * a2)
        db_ref[...] += _col_sum(dc)

    return pl.pallas_call(
        body, name="conv_gate_bwd", grid=(nj, B),
        in_specs=[pl.BlockSpec((None, S, 2 * CONV_COLS), lambda j, b: (b, 0, j)),
                  pl.BlockSpec((None, S, CONV_COLS), lambda j, b: (b, 0, j)),
                  pl.BlockSpec((3, CONV_COLS), lambda j, b: (0, j)),
                  pl.BlockSpec((1, CONV_COLS), lambda j, b: (0, j))],
        out_specs=[pl.BlockSpec((None, S, 2 * CONV_COLS), lambda j, b: (b, 0, j)),
                   pl.BlockSpec((3, CONV_COLS), lambda j, b: (0, j)),
                   pl.BlockSpec((1, CONV_COLS), lambda j, b: (0, j))],
        out_shape=[jax.ShapeDtypeStruct((B, S, 2 * D_FF), BF16), jax.ShapeDtypeStruct((3, D_FF), F32),
                   jax.ShapeDtypeStruct((1, D_FF), F32)],
        compiler_params=_params(2),
    )(up, d_ff, w_conv, b_conv)


def _ada_fwd(c_all, w_ada, b_ada):
    def body(c_ref, w_ref, b_ref, o_ref):
        cv = c_ref[...]
        act = (cv * _sigmoid(cv)).astype(BF16)
        o_ref[...] = jnp.dot(act, w_ref[...].astype(BF16), preferred_element_type=F32) + b_ref[...]

    return pl.pallas_call(body, name="ada_fwd",
                          out_shape=jax.ShapeDtypeStruct((c_all.shape[0], w_ada.shape[1]), F32),
                          compiler_params=pltpu.CompilerParams(vmem_limit_bytes=V7X_VMEM_LIMIT))(c_all, w_ada, b_ada)


def _ada_bwd(c_all, dmod_all, dmod_cols):
    def body(c_ref, dm_ref, dmc_ref, dw_ref, db_ref):
        cv = c_ref[...]
        act = (cv * _sigmoid(cv)).astype(BF16)
        dw_ref[...] = lax.dot_general(act, dmc_ref[...].astype(BF16), _TN, preferred_element_type=F32)
        db_ref[...] = _col_sum(dm_ref[...])

    return pl.pallas_call(
        body, name="ada_bwd",
        out_shape=[jax.ShapeDtypeStruct((c_all.shape[1], dmod_cols.shape[1]), F32),
                   jax.ShapeDtypeStruct((1, dmod_all.shape[1]), F32)],
        compiler_params=pltpu.CompilerParams(vmem_limit_bytes=V7X_VMEM_LIMIT))(c_all, dmod_all, dmod_cols)


def _adamw(w, m, v, g_parts, name, own=None):
    R, C = w.shape
    P = g_parts.shape[0]
    tr = R
    for cand in (256, 128, 64, 32, 16, 8):
        if R % cand == 0 and cand * C * 4 * (P + 8) * 2 <= V7X_VMEM_LIMIT // 2:
            tr = cand
            break
    c1 = 1.0 / (1.0 - ADAM_B1 ** ADAM_STEP)
    c2 = 1.0 / (1.0 - ADAM_B2 ** ADAM_STEP)

    def update(w_ref, m_ref, v_ref, g, og, od, om, ov):
        m_new = ADAM_B1 * m_ref[...] + (1.0 - ADAM_B1) * g
        v_new = ADAM_B2 * v_ref[...] + (1.0 - ADAM_B2) * (g * g)
        og[...] = g
        om[...] = m_new
        ov[...] = v_new
        od[...] = -ADAM_LR * ((m_new * c1) / (jnp.sqrt(v_new * c2) + ADAM_EPS) + ADAM_WD * w_ref[...])

    def total(g_ref):
        g = g_ref[0].astype(F32)
        for p in range(1, P):
            g = g + g_ref[p].astype(F32)
        return g

    out_shape = [jax.ShapeDtypeStruct((R, C), F32)] * 4
    if own is None:
        def body(w_ref, m_ref, v_ref, g_ref, og, od, om, ov):
            update(w_ref, m_ref, v_ref, total(g_ref), og, od, om, ov)

        spec = pl.BlockSpec((tr, C), lambda i: (i, 0))
        return pl.pallas_call(
            body, name=name, grid=(R // tr,),
            in_specs=[spec, spec, spec, pl.BlockSpec((P, tr, C), lambda i: (0, i, 0))],
            out_specs=[spec] * 4, out_shape=out_shape, compiler_params=_params(1),
        )(w, m, v, g_parts)

    slots, me = own

    def body_own(me_ref, w_ref, m_ref, v_ref, g_ref, own_ref, og, od, om, ov):
        update(w_ref, m_ref, v_ref, total(g_ref) + own_ref[...].astype(F32), og, od, om, ov)

    spec = pl.BlockSpec((tr, C), lambda i, me_ref: (i, 0))
    grid_spec = pltpu.PrefetchScalarGridSpec(
        num_scalar_prefetch=1, grid=(R // tr,),
        in_specs=[spec, spec, spec, pl.BlockSpec((P, tr, C), lambda i, me_ref: (0, i, 0)),
                  pl.BlockSpec((None, tr, C), lambda i, me_ref: (me_ref[0], i, 0))],
        out_specs=[spec] * 4)
    return pl.pallas_call(body_own, name=name, grid_spec=grid_spec, out_shape=out_shape,
                          compiler_params=_params(1))(me, w, m, v, g_parts, slots)


def _sum_parts(parts, loss_rows):
    P, R, C = parts.shape
    lo, hi = loss_rows

    def body(p_ref, o_ref, loss_ref):
        t = p_ref[0]
        for p in range(1, P):
            t = t + p_ref[p]
        o_ref[...] = t
        tot = jnp.sum(jnp.sum(o_ref[lo:hi, :], axis=1, keepdims=True), axis=0, keepdims=True)
        loss_ref[...] = jnp.broadcast_to(tot, loss_ref.shape)

    return pl.pallas_call(body, name="sum_small_grads",
                          out_shape=[jax.ShapeDtypeStruct((R, C), F32), jax.ShapeDtypeStruct((1, LANES), F32)],
                          compiler_params=pltpu.CompilerParams(vmem_limit_bytes=V7X_VMEM_LIMIT))(parts)


def _exchange(items, name):
    n = len(items)
    MESH = pl.DeviceIdType.MESH

    def body(*refs):
        src, dst = refs[:n], refs[n:2 * n]
        send_sems, recv_sems, local_sems = refs[2 * n:]
        x, y, c = lax.axis_index("x"), lax.axis_index("y"), lax.axis_index("c")
        me = 4 * x + 2 * y + c
        started = []
        for it, (_, per_peer) in enumerate(items):
            own = pltpu.make_async_copy(src[it].at[me] if per_peer else src[it], dst[it].at[me], local_sems.at[it])
            own.start()
            started.append(own)
        sends, recvs = [], []
        for k in range(1, N_DEV):
            px = 1 - x if k & 4 else x
            py = 1 - y if k & 2 else y
            pc = 1 - c if k & 1 else c
            peer = 4 * px + 2 * py + pc
            for it, (_, per_peer) in enumerate(items):
                s = src[it].at[peer] if per_peer else src[it]
                cp = pltpu.make_async_remote_copy(src_ref=s, dst_ref=dst[it].at[me], send_sem=send_sems.at[it, k - 1],
                                                  recv_sem=recv_sems.at[it, k - 1], device_id=(px, py, pc),
                                                  device_id_type=MESH)
                cp.start()
                sends.append(cp)
                recvs.append(pltpu.make_async_remote_copy(
                    src_ref=s, dst_ref=dst[it].at[peer], send_sem=send_sems.at[it, k - 1],
                    recv_sem=recv_sems.at[it, k - 1], device_id=(px, py, pc), device_id_type=MESH))
        for cp in recvs:
            cp.wait_recv()
        for cp in sends:
            cp.wait_send()
        for cp in started:
            cp.wait()

    any_spec = pl.BlockSpec(memory_space=pl.ANY)
    out_shape = []
    for a, per_peer in items:
        shp = a.shape if per_peer else (N_DEV,) + a.shape
        out_shape.append(jax.ShapeDtypeStruct(shp, a.dtype))
    return pl.pallas_call(
        body, name=name, in_specs=[any_spec] * n, out_specs=[any_spec] * n, out_shape=out_shape,
        scratch_shapes=[pltpu.SemaphoreType.DMA((n, N_DEV - 1)), pltpu.SemaphoreType.DMA((n, N_DEV - 1)),
                        pltpu.SemaphoreType.DMA((n,))],
    )(*[a for a, _ in items])


def _remote(src, dst, send_sem, recv_sem, device):
    return pltpu.make_async_remote_copy(src_ref=src, dst_ref=dst, send_sem=send_sem, recv_sem=recv_sem,
                                        device_id=device, device_id_type=pl.DeviceIdType.MESH)


def _mesh_place():
    x, y, c = lax.axis_index("x"), lax.axis_index("y"), lax.axis_index("c")
    other_chips = [(1 - x, y), (x, 1 - y), (1 - x, 1 - y)]
    return x, y, c, (x, y, 1 - c), other_chips


def _gather_all(items, name):
    n = len(items)

    def body(*refs):
        src, dst = refs[:n], refs[n:2 * n]
        send_sems, recv_sems, local_sems = refs[2 * n:]
        x, y, c, sibling, chips = _mesh_place()
        slot = lambda px, py, pc: 4 * px + 2 * py + pc
        me = slot(x, y, c)
        own = [pltpu.make_async_copy(src[it], dst[it].at[me], local_sems.at[it]) for it in range(n)]
        first = []
        for it in range(n):
            first.append(_remote(src[it], dst[it].at[me], send_sems.at[it, 0], recv_sems.at[it, 0], sibling))
            for j, chip in enumerate(chips):
                first.append(_remote(src[it], dst[it].at[me], send_sems.at[it, 1 + j], recv_sems.at[it, 1 + j],
                                     (*chip, c)))
        for cp in own + first:
            cp.start()
        passed = []
        for j, chip in enumerate(chips):
            blk = slot(*chip, c)
            for it in range(n):
                _remote(src[it], dst[it].at[blk], send_sems.at[it, 1 + j], recv_sems.at[it, 1 + j],
                        (*chip, c)).wait_recv()
                fwd = _remote(dst[it].at[blk], dst[it].at[blk], send_sems.at[it, 4 + j], recv_sems.at[it, 4 + j],
                              sibling)
                fwd.start()
                passed.append(fwd)
        for it in range(n):
            _remote(src[it], dst[it].at[slot(x, y, 1 - c)], send_sems.at[it, 0], recv_sems.at[it, 0],
                    sibling).wait_recv()
        for j, chip in enumerate(chips):
            for it in range(n):
                _remote(src[it], dst[it].at[slot(*chip, 1 - c)], send_sems.at[it, 4 + j], recv_sems.at[it, 4 + j],
                        sibling).wait_recv()
        for cp in first + passed:
            cp.wait_send()
        for cp in own:
            cp.wait()

    any_spec = pl.BlockSpec(memory_space=pl.ANY)
    return pl.pallas_call(
        body, name=name, in_specs=[any_spec] * n, out_specs=[any_spec] * n,
        out_shape=[jax.ShapeDtypeStruct((N_DEV,) + a.shape, a.dtype) for a in items],
        scratch_shapes=[pltpu.SemaphoreType.DMA((n, 7)), pltpu.SemaphoreType.DMA((n, 7)),
                        pltpu.SemaphoreType.DMA((n,))],
    )(*items)


N_CHIPS = N_DEV // 2


def _sibling_swap(items, name):
    n = len(items)

    def body(*refs):
        g, got = refs[:n], refs[n:2 * n]
        send_sems, recv_sems = refs[2 * n:]
        x, y, c, sibling, _ = _mesh_place()
        sends = [_remote(g[it].at[2 * chip + 1 - c], got[it].at[chip], send_sems.at[it, chip],
                         recv_sems.at[it, chip], sibling) for it in range(n) for chip in range(N_CHIPS)]
        for cp in sends:
            cp.start()
        for cp in sends:
            cp.wait_recv()
        for cp in sends:
            cp.wait_send()

    any_spec = pl.BlockSpec(memory_space=pl.ANY)
    return pl.pallas_call(
        body, name=name, in_specs=[any_spec] * n, out_specs=[any_spec] * n,
        out_shape=[jax.ShapeDtypeStruct((N_CHIPS,) + a.shape[1:], a.dtype) for a in items],
        scratch_shapes=[pltpu.SemaphoreType.DMA((n, N_CHIPS))] * 2,
    )(*items)


def _add_pairs(items, gots):
    n = len(items)

    def body(core_ref, *refs):
        for it in range(n):
            a_ref, b_ref, o_ref = refs[it], refs[n + it], refs[2 * n + it]
            o_ref[...] = (a_ref[...].astype(F32) + b_ref[...].astype(F32)).astype(o_ref.dtype)

    mine = [pl.BlockSpec((None,) + a.shape[1:], lambda i, core: (2 * i + core[0], 0, 0)) for a in items]
    plain = [pl.BlockSpec((None,) + a.shape[1:], lambda i, core: (i, 0, 0)) for a in gots]
    grid_spec = pltpu.PrefetchScalarGridSpec(num_scalar_prefetch=1, grid=(N_CHIPS,), in_specs=mine + plain,
                                             out_specs=plain)
    core = lax.axis_index("c").astype(jnp.int32).reshape(1)
    return pl.pallas_call(body, name="add_sibling_partials", grid_spec=grid_spec,
                          out_shape=[jax.ShapeDtypeStruct(a.shape, a.dtype) for a in gots],
                          compiler_params=_params(1))(core, *items, *gots)


def _chip_scatter(items, name):
    n = len(items)

    def body(*refs):
        s, out = refs[:n], refs[n:2 * n]
        send_sems, recv_sems, local_sems = refs[2 * n:]
        x, y, c, _, _ = _mesh_place()
        my_chip = 2 * x + y
        keeps = [pltpu.make_async_copy(s[it].at[my_chip], out[it].at[my_chip], local_sems.at[it]) for it in range(n)]
        for cp in keeps:
            cp.start()
        sends, recvs = [], []
        for k in range(1, N_CHIPS):
            px = 1 - x if k & 2 else x
            py = 1 - y if k & 1 else y
            peer_chip = 2 * px + py
            for it in range(n):
                cp = _remote(s[it].at[peer_chip], out[it].at[my_chip], send_sems.at[it, k - 1],
                             recv_sems.at[it, k - 1], (px, py, c))
                cp.start()
                sends.append(cp)
                recvs.append(_remote(s[it].at[peer_chip], out[it].at[peer_chip], send_sems.at[it, k - 1],
                                     recv_sems.at[it, k - 1], (px, py, c)))
        for cp in recvs:
            cp.wait_recv()
        for cp in sends:
            cp.wait_send()
        for cp in keeps:
            cp.wait()

    any_spec = pl.BlockSpec(memory_space=pl.ANY)
    return pl.pallas_call(
        body, name=name, in_specs=[any_spec] * n, out_specs=[any_spec] * n,
        out_shape=[jax.ShapeDtypeStruct(a.shape, a.dtype) for a in items],
        scratch_shapes=[pltpu.SemaphoreType.DMA((n, N_CHIPS - 1)), pltpu.SemaphoreType.DMA((n, N_CHIPS - 1)),
                        pltpu.SemaphoreType.DMA((n,))],
    )(*items)


def _peers():
    x, y, c = lax.axis_index("x"), lax.axis_index("y"), lax.axis_index("c")
    out = []
    for k in range(1, N_DEV):
        px = 1 - x if k & 4 else x
        py = 1 - y if k & 2 else y
        pc = 1 - c if k & 1 else c
        out.append((k, (px, py, pc), 4 * px + 2 * py + pc))
    return 4 * x + 2 * y + c, out


def _scatter_start(items, name):
    n = len(items)

    def body(*refs):
        src, land = refs[:n], refs[n:2 * n]
        send_sems, recv_sems = refs[2 * n:3 * n], refs[3 * n:4 * n]
        token = refs[-1]
        me, peers = _peers()
        for k, peer, slot in peers:
            for it in range(n):
                _remote(src[it].at[slot], land[it].at[k - 1], send_sems[it], recv_sems[it], peer).start()
        token[...] = jnp.zeros_like(token)

    hbm = pl.BlockSpec(memory_space=pltpu.HBM)
    sem = pl.BlockSpec(memory_space=pltpu.SEMAPHORE)
    land_shapes = [(N_DEV - 1,) + a.shape[1:] for a in items]
    lands = [jnp.zeros(shp, a.dtype) for shp, a in zip(land_shapes, items)]
    outs = pl.pallas_call(
        body, name=name,
        out_shape=(*[pltpu.SemaphoreType.DMA(())] * (2 * n),
                   *[pltpu.HBM(a.shape, a.dtype) for a in items],
                   *[pltpu.HBM(shp, a.dtype) for shp, a in zip(land_shapes, items)],
                   jax.ShapeDtypeStruct((8, LANES), F32)),
        in_specs=[hbm] * (2 * n),
        out_specs=(*[sem] * (2 * n), *[hbm] * (2 * n), pl.BlockSpec(memory_space=pltpu.VMEM)),
        input_output_aliases={i: 2 * n + i for i in range(2 * n)},
        compiler_params=pltpu.CompilerParams(has_side_effects=pltpu.SideEffectType.DATAFLOW_SIDE_EFFECTING),
    )(*[pltpu.with_memory_space_constraint(a, pltpu.HBM) for a in list(items) + lands])
    return (list(outs[:n]), list(outs[n:2 * n]), list(outs[2 * n:3 * n]), list(outs[3 * n:4 * n]), outs[-1])


def _scatter_wait(send_sems, recv_sems, items, lands, after, name):
    n = len(items)

    def body(*refs):
        land = refs[n:2 * n]
        send_sems, recv_sems = refs[2 * n:3 * n], refs[3 * n:4 * n]
        me, peers = _peers()
        for it in range(n):
            cp = _remote(land[it], land[it], send_sems[it], recv_sems[it], peers[0][1])
            cp.wait_send()
            cp.wait_recv()

    hbm = pl.BlockSpec(memory_space=pltpu.HBM)
    sem = pl.BlockSpec(memory_space=pltpu.SEMAPHORE)
    outs = pl.pallas_call(
        body, name=name,
        out_shape=tuple(pltpu.HBM(a.shape, a.dtype) for a in list(items) + list(lands)),
        in_specs=[hbm] * (2 * n) + [sem] * (2 * n) + [pl.BlockSpec(memory_space=pl.ANY)],
        out_specs=tuple([hbm] * (2 * n)),
        input_output_aliases={i: i for i in range(2 * n)},
        compiler_params=pltpu.CompilerParams(has_side_effects=pltpu.SideEffectType.DATAFLOW_SIDE_EFFECTING),
    )(*items, *lands, *send_sems, *recv_sems, after)
    return list(outs[:n]), list(outs[n:])


def _gelu_tanh(y):
    k = math.sqrt(2.0 / math.pi)
    t = jnp.tanh(k * (y + 0.044715 * y * y * y))
    return 0.5 * y * (1.0 + t), t


def _local_step(x, mod, target, W, P, send_early):
    B, S, D = x.shape
    T = B * S
    TS = 512
    flat = lambda a: a.reshape(T, a.shape[-1])
    unflat = lambda a: a.reshape(B, S, a.shape[-1])
    mod_col = lambda i: (mod, D, i)

    def f_modnorm(xv, sc, sh, g):
        return (xv * _rms_scale(xv) * g) * (1.0 + sc) + sh

    (u1,) = _rowwise(f_modnorm, [(x, D, 0)], [mod_col(1), mod_col(0)], [P["g_mix"]],
                     [(D, BF16)], [], [], ts=TS, name="modnorm_mix")
    u1f = flat(u1)
    qkv = unflat(_matmul(u1f, W["w_qkv"], name="proj_qkv"))
    us = unflat(_matmul(u1f, W["w_us"], name="proj_ssm_in"))
    gates = unflat(_matmul(u1f, W["w_gates"], out_dtype=BF16, name="proj_gates"))

    o_att, lse = _attention_fwd(qkv, P["slopes"])
    y_att = unflat(_matmul(flat(o_att), W["w_proj_att"], out_dtype=BF16, name="proj_att"))

    bu = unflat(_matmul(flat(us), P["bb_big"], name="s5_bu"))
    xs = _scan_fwd(bu, P["a_row"])
    y_mm = unflat(_matmul(flat(xs), P["cc_big"], name="s5_readout"))

    def f_glu(ymm, usv, dsk, wg, bg):
        yv = ymm + dsk * usv
        ge, _ = _gelu_tanh(yv)
        pre = jnp.dot(ge.astype(BF16), wg, preferred_element_type=F32) + bg
        return yv, ge * _sigmoid(pre)

    y_s5, z = _rowwise(f_glu, [(y_mm, SSM_WIDTH, 0), (us, SSM_WIDTH, 0)], [], [P["d_skip"], W["w_glu"], P["b_glu"]],
                       [(SSM_WIDTH, F32), (SSM_WIDTH, BF16)], [], [], ts=TS, name="s5_glu")
    y_ssm = unflat(_matmul(flat(z), W["w_proj_ssm"], out_dtype=BF16, name="proj_ssm"))

    def f_merge(ga, gs, ya, ys, bga, bgs):
        return _sigmoid(ga + bga) * ya + _sigmoid(gs + bgs) * ys

    bga, bgs = P["b_gate"][:, :D], P["b_gate"][:, D:]
    (merged,) = _rowwise(f_merge, [(gates, D, 0), (gates, D, 1), (y_att, D, 0), (y_ssm, D, 0)], [], [bga, bgs],
                         [(D, BF16)], [], [], ts=TS, name="gate_merge")
    mix = unflat(_matmul(flat(merged), W["w_out"], name="proj_out"))

    def f_res_modnorm(xv, mx, gt, sc, sh, g):
        h = xv + gt * mx
        return h, (h * _rms_scale(h) * g) * (1.0 + sc) + sh

    h1, u2 = _rowwise(f_res_modnorm, [(x, D, 0), (mix, D, 0)], [mod_col(2), mod_col(4), mod_col(3)], [P["g_ffn"]],
                      [(D, F32), (D, BF16)], [], [], ts=TS, name="residual_modnorm_ffn")
    up = unflat(_up_fwd(flat(u2), W["w_up"], name="ffn_up"))
    ff = _conv_fwd(up, P["w_conv"], P["b_conv"])
    down = unflat(_matmul(flat(ff), W["w_down"], name="ffn_down"))

    def f_head(h1v, dn, tg, gt, g):
        h2 = h1v + gt * dn
        r = _rms_scale(h2)
        nh = h2 * r
        e = nh * g - tg
        dy = e * (1.0 / D)
        gy = dy * g
        dh = r * (gy - nh * jnp.mean(gy * nh, axis=-1, keepdims=True))
        return (dh, dh * gt, _col_sum(dh * dn), _col_sum(dy * nh), _col_sum(e * e) * (0.5 / D))

    dh2, d_down, d_gt2, d_g_final, loss_cols = _rowwise(
        f_head, [(h1, D, 0), (down, D, 0), (target, D, 0)], [mod_col(5)], [P["g_final"]],
        [(D, F32), (D, BF16)], [D], [(1, D), (1, D)], ts=TS, name="head_loss")

    d_downf = flat(d_down)
    d_ff = unflat(_matmul(d_downf, W["w_down"], tb=True, out_dtype=BF16, name="ffn_down_dx"))
    d_w_down = _matmul(flat(ff), d_downf, ta=True, out_dtype=BF16, name="ffn_down_dw")
    d_up, d_w_conv, d_b_conv = _conv_bwd(up, d_ff, P["w_conv"], P["b_conv"])
    d_upf = flat(d_up)
    d_u2 = unflat(_up_dx(d_upf, W["w_up"], name="ffn_up_dx"))
    d_w_up = _up_dw(flat(u2), d_upf, name="ffn_up_dw")
    token = send_early(dict(w_down=d_w_down.reshape(N_DEV, D_FF // N_DEV, D), w_up=d_w_up))
    g_ffn_after = P["g_ffn"] + token[0:1, 0:1]

    def f_modnorm_bwd(du, h, dres, mx, sc, gt, g):
        r = _rms_scale(h)
        nh = h * r
        dn = du * (1.0 + sc)
        gy = dn * g
        dh = dres + r * (gy - nh * jnp.mean(gy * nh, axis=-1, keepdims=True))
        return (dh, dh * gt, _col_sum(du), _col_sum(du * nh * g), _col_sum(dh * mx), _col_sum(dn * nh))

    dh1, d_mix, d_sh2, d_sc2, d_gt1, d_g_ffn = _rowwise(
        f_modnorm_bwd, [(d_u2, D, 0), (h1, D, 0), (dh2, D, 0), (mix, D, 0)], [mod_col(4), mod_col(2)], [g_ffn_after],
        [(D, F32), (D, BF16)], [D, D, D], [(1, D)], ts=TS, name="modnorm_ffn_bwd")

    d_mixf = flat(d_mix)
    d_merged = unflat(_matmul(d_mixf, W["w_out"], tb=True, out_dtype=BF16, name="proj_out_dx"))
    d_w_out = _matmul(flat(merged), d_mixf, ta=True, out_dtype=BF16, name="proj_out_dw")

    def f_merge_bwd(dm, ga, gs, ya, ys, bga_, bgs_):
        sa, ss = _sigmoid(ga + bga_), _sigmoid(gs + bgs_)
        dga = dm * ya * sa * (1.0 - sa)
        dgs = dm * ys * ss * (1.0 - ss)
        return dm * sa, dm * ss, jnp.concatenate([dga, dgs], axis=1), _col_sum(dga), _col_sum(dgs)

    d_y_att, d_y_ssm, d_gates, d_bga, d_bgs = _rowwise(
        f_merge_bwd, [(d_merged, D, 0), (gates, D, 0), (gates, D, 1), (y_att, D, 0), (y_ssm, D, 0)], [], [bga, bgs],
        [(D, BF16), (D, BF16), (2 * D, BF16)], [], [(1, D), (1, D)], ts=TS, name="gate_merge_bwd")

    d_yaf, d_ysf = flat(d_y_att), flat(d_y_ssm)
    d_o_att = unflat(_matmul(d_yaf, W["w_proj_att"], tb=True, name="proj_att_dx"))
    d_w_proj_att = _matmul(flat(o_att), d_yaf, ta=True, out_dtype=BF16, name="proj_att_dw")
    d_z = unflat(_matmul(d_ysf, W["w_proj_ssm"], tb=True, out_dtype=BF16, name="proj_ssm_dx"))
    d_w_proj_ssm = _matmul(flat(z), d_ysf, ta=True, out_dtype=BF16, name="proj_ssm_dw")

    def f_glu_bwd(yv, dz, usv, dsk, wg, bg):
        ge, t = _gelu_tanh(yv)
        pre = jnp.dot(ge.astype(BF16), wg, preferred_element_type=F32) + bg
        sg = _sigmoid(pre)
        dpre = dz * ge * sg * (1.0 - sg)
        dge = dz * sg + lax.dot_general(dpre.astype(BF16), wg, _NT, preferred_element_type=F32)
        k = math.sqrt(2.0 / math.pi)
        dgelu = 0.5 * (1.0 + t) + 0.5 * yv * (1.0 - t * t) * k * (1.0 + 3.0 * 0.044715 * yv * yv)
        dy = dge * dgelu
        dwg = lax.dot_general(ge.astype(BF16), dpre.astype(BF16), _TN, preferred_element_type=F32)
        return dy, dy * dsk, dwg, _col_sum(dpre), _col_sum(dy * usv)

    d_y_s5, d_us_skip, d_w_glu, d_b_glu, d_d_skip = _rowwise(
        f_glu_bwd, [(y_s5, SSM_WIDTH, 0), (d_z, SSM_WIDTH, 0), (us, SSM_WIDTH, 0)], [],
        [P["d_skip"], W["w_glu"], P["b_glu"]],
        [(SSM_WIDTH, BF16), (SSM_WIDTH, F32)], [], [(SSM_WIDTH, SSM_WIDTH), (1, SSM_WIDTH), (1, SSM_WIDTH)],
        ts=TS, name="s5_glu_bwd")
    d_ysf2 = flat(d_y_s5)
    dxs = unflat(_matmul(d_ysf2, P["cc_big"], tb=True, name="s5_readout_dx"))
    d_cc = _matmul(flat(xs), d_ysf2, ta=True, name="s5_readout_dw")
    lam, g_ab = _scan_bwd(dxs, xs, P["a_row"])
    lam = flat(lam)
    d_us_mm = unflat(_matmul(lam, P["bb_big"], tb=True, name="s5_bu_dx"))
    d_bb = _matmul(flat(us), lam, ta=True, name="s5_bu_dw")

    token = send_early(dict(
        w_out=d_w_out.reshape(N_DEV, D // N_DEV, D), w_proj_att=_cols_to_slots(d_w_proj_att),
        w_proj_ssm=_cols_to_slots(d_w_proj_ssm),
        w_glu=d_w_glu.astype(BF16).reshape(N_DEV, SSM_WIDTH // N_DEV, SSM_WIDTH),
        w_conv=_cols_to_slots(d_w_conv.astype(BF16))))
    d_qkv = _attention_bwd(qkv, o_att, d_o_att, lse, P["slopes"] + token[0, 0])

    def f_add(a, b_):
        return a + b_

    (d_us,) = _rowwise(f_add, [(d_us_mm, SSM_WIDTH, 0), (d_us_skip, SSM_WIDTH, 0)], [], [],
                       [(SSM_WIDTH, BF16)], [], [], ts=TS, name="s5_input_grad")
    d_qkvf = flat(d_qkv)
    d_usf = flat(d_us)
    d_gatesf = flat(d_gates)
    d_u1 = (_matmul(d_qkvf, W["w_qkv"], tb=True, out_dtype=BF16, name="proj_qkv_dx"),
            _matmul(d_usf, W["w_us"], tb=True, out_dtype=BF16, name="proj_ssm_in_dx"),
            _matmul(d_gatesf, W["w_gates"], tb=True, out_dtype=BF16, name="proj_gates_dx"))
    d_w_in = jnp.concatenate(
        [_unpair_qkv_columns(_matmul(u1f, d_qkvf, ta=True, out_dtype=BF16, name="proj_qkv_dw")),
         _matmul(u1f, d_usf, ta=True, out_dtype=BF16, name="proj_ssm_in_dw"),
         _matmul(u1f, d_gatesf, ta=True, out_dtype=BF16, name="proj_gates_dw")], axis=1)

    def f_modnorm_bwd_in(du0, du1, du2, h, dres, sc, g):
        du = du0 + du1 + du2
        r = _rms_scale(h)
        nh = h * r
        dn = du * (1.0 + sc)
        gy = dn * g
        dh = dres + r * (gy - nh * jnp.mean(gy * nh, axis=-1, keepdims=True))
        return (dh, _col_sum(du), _col_sum(du * nh * g), _col_sum(dn * nh))

    grad_x, d_sh1, d_sc1, d_g_mix = _rowwise(
        f_modnorm_bwd_in, [(unflat(d_u1[0]), D, 0), (unflat(d_u1[1]), D, 0), (unflat(d_u1[2]), D, 0), (x, D, 0),
                           (dh1, D, 0)], [mod_col(1)], [P["g_mix"]],
        [(D, F32)], [D, D], [(1, D)], ts=TS, name="modnorm_mix_bwd")

    d_mod = jnp.concatenate([d_sh1, d_sc1, d_gt1, d_sh2, d_sc2, d_gt2], axis=-1)
    big = dict(w_in=d_w_in)
    g_ab_re, g_ab_im = _deinterleave(g_ab)
    d_bb_re, d_bb_im = _deinterleave(d_bb)
    d_cc_re, d_cc_im = (t.T for t in _deinterleave(d_cc.T))
    small = dict(g_mix=d_g_mix, b_gate=jnp.concatenate([d_bga, d_bgs], axis=1), g_ab_re=g_ab_re, g_ab_im=g_ab_im,
                 d_bb_re=d_bb_re, d_bb_im=d_bb_im, d_cc_re=d_cc_re, d_cc_im=d_cc_im, d_skip=d_d_skip,
                 b_glu=d_b_glu, g_ffn=d_g_ffn, b_conv=d_b_conv, g_final=d_g_final, loss_cols=loss_cols)
    return grad_x, d_mod, big, small


def _block_diag_in(bb):
    t = bb.reshape(SSM_GROUPS, SSM_STATE, SSM_GROUP_CH)
    eye = jnp.eye(SSM_GROUPS, dtype=bb.dtype)
    return jnp.einsum("gnc,gh->gchn", t, eye).reshape(SSM_WIDTH, SSM_COLS)


def _block_diag_out(cm):
    eye = jnp.eye(SSM_GROUPS, dtype=cm.dtype)
    return jnp.einsum("gcn,gh->gnhc", cm, eye).reshape(SSM_COLS, SSM_WIDTH)


def _diag_blocks_in(m):
    t = m.reshape(SSM_GROUPS, SSM_GROUP_CH, SSM_GROUPS, SSM_STATE)
    idx = jnp.arange(SSM_GROUPS)
    return t[idx, :, idx, :].transpose(0, 2, 1).reshape(SSM_COLS, SSM_GROUP_CH)


def _diag_blocks_out(m):
    t = m.reshape(SSM_GROUPS, SSM_STATE, SSM_GROUPS, SSM_GROUP_CH)
    idx = jnp.arange(SSM_GROUPS)
    return t[idx, :, idx, :].transpose(0, 2, 1)


def _pair_qkv_columns(w):
    lead = w.shape[:-1]
    return w.reshape(lead + (3, N_HEADS // 2, LANES)).swapaxes(-3, -2).reshape(lead + (3 * ATT_WIDTH,))


def _unpair_qkv_columns(w):
    lead = w.shape[:-1]
    return w.reshape(lead + (N_HEADS // 2, 3, LANES)).swapaxes(-3, -2).reshape(lead + (3 * ATT_WIDTH,))


def _interleave(re, im):
    lead = re.shape[:-1]
    g = lambda a: a.reshape(lead + (SSM_COLS // SCAN_COLS, 1, SCAN_COLS))
    return jnp.concatenate([g(re), g(im)], axis=-2).reshape(lead + (2 * SSM_COLS,))


def _deinterleave(x):
    lead = x.shape[:-1]
    t = x.reshape(lead + (SSM_COLS // SCAN_COLS, 2, SCAN_COLS))
    return t[..., 0, :].reshape(lead + (SSM_COLS,)), t[..., 1, :].reshape(lead + (SSM_COLS,))


def _cols_to_slots(g):
    R = g.shape[0]
    return g.reshape(R, N_DEV, g.shape[1] // N_DEV).transpose(1, 0, 2)


def _slots_to_cols(g):
    return g.transpose(1, 0, 2).reshape(g.shape[1], N_DEV * g.shape[2])


SMALL_ORDER = ("b_ada", "g_mix", "b_gate", "a_re", "a_im", "log_dt", "b_re", "b_im", "c_re", "c_im", "d_skip",
               "b_glu", "g_ffn", "b_conv", "g_final")


def _pack(arrs):
    pieces, offs, row = [], [], 0
    for a in arrs:
        f = a.reshape(-1).astype(F32)
        n = f.shape[0]
        rows = -(-n // LANES)
        pieces.append(jnp.pad(f, (0, rows * LANES - n)))
        offs.append((row, n))
        row += rows
    return jnp.concatenate(pieces).reshape(row, LANES), offs


def _unpack(packed, offs, shapes):
    flat = packed.reshape(-1)
    return [flat[r * LANES:r * LANES + n].reshape(s) for (r, n), s in zip(offs, shapes)]


def kernel(x, c, w_ada, b_ada, g_mix, w_in, b_gate, a_re, a_im, log_dt, b_re, b_im, c_re, c_im, d_skip, w_glu, b_glu, w_proj_att, w_proj_ssm, w_out, g_ffn, w_up, w_conv, b_conv, w_down, g_final, loss_target, m_w_ada, m_b_ada, m_g_mix, m_w_in, m_b_gate, m_a_re, m_a_im, m_log_dt, m_b_re, m_b_im, m_c_re, m_c_im, m_d_skip, m_w_glu, m_b_glu, m_w_proj_att, m_w_proj_ssm, m_w_out, m_g_ffn, m_w_up, m_w_conv, m_b_conv, m_w_down, m_g_final, v_w_ada, v_b_ada, v_g_mix, v_w_in, v_b_gate, v_a_re, v_a_im, v_log_dt, v_b_re, v_b_im, v_c_re, v_c_im, v_d_skip, v_w_glu, v_b_glu, v_w_proj_att, v_w_proj_ssm, v_w_out, v_g_ffn, v_w_up, v_w_conv, v_b_conv, v_w_down, v_g_final):
    args = dict(locals())
    B, S, D = x.shape
    me = 4 * lax.axis_index("x") + 2 * lax.axis_index("y") + lax.axis_index("c")
    bf = lambda w: w[0].astype(BF16)

    gathered = _gather_all([c, bf(w_in), bf(w_glu), bf(w_proj_att), bf(w_proj_ssm), bf(w_out), bf(w_up), w_conv[0],
                            bf(w_down)], name="gather_weights")
    c_all = gathered[0].reshape(N_DEV * B, D)
    w_in_full = _slots_to_cols(gathered[1])
    n_qkv = 3 * ATT_WIDTH
    W = dict(w_qkv=_pair_qkv_columns(w_in_full[:, :n_qkv]), w_us=w_in_full[:, n_qkv:n_qkv + SSM_WIDTH],
             w_gates=w_in_full[:, n_qkv + SSM_WIDTH:], w_glu=gathered[2].reshape(SSM_WIDTH, SSM_WIDTH),
             w_proj_att=_slots_to_cols(gathered[3]), w_proj_ssm=_slots_to_cols(gathered[4]),
             w_out=gathered[5].reshape(D, D), w_up=gathered[6],
             w_down=gathered[8].reshape(D_FF, D))
    w_conv_full = _slots_to_cols(gathered[7])

    n_ada = w_ada.shape[2]
    b_ada_cols = lax.dynamic_slice(b_ada, (0, me * n_ada), (1, n_ada))
    mod_part = _ada_fwd(c_all, w_ada[0], b_ada_cols)
    (mod_slots,) = _exchange([(mod_part.reshape(N_DEV, B, n_ada), True)], name="scatter_modulation")
    mod = mod_slots.transpose(1, 0, 2).reshape(B, 1, 6 * D)

    ab_re, ab_im, f_re, f_im = _s5_params(a_re[0], a_im[0], log_dt[0].reshape(SSM_GROUPS, 1))
    col = lambda a: a.reshape(SSM_COLS, 1)
    b_re2, b_im2 = b_re[0].reshape(SSM_COLS, SSM_GROUP_CH), b_im[0].reshape(SSM_COLS, SSM_GROUP_CH)
    bb_re, bb_im = _s5_input_matrix(col(f_re), col(f_im), b_re2, b_im2)
    slopes = jnp.asarray([2.0 ** (-8.0 * (h + 1) / N_HEADS) for h in range(N_HEADS)], F32)
    P = dict(g_mix=g_mix, g_ffn=g_ffn, g_final=g_final.reshape(1, D), b_gate=b_gate, d_skip=d_skip, b_glu=b_glu,
             b_conv=b_conv, w_conv=w_conv_full, slopes=slopes,
             a_row=_interleave(ab_re.reshape(1, SSM_COLS), ab_im.reshape(1, SSM_COLS)),
             bb_big=_interleave(_block_diag_in(bb_re), _block_diag_in(bb_im)),
             cc_big=_interleave(_block_diag_out(c_re[0]).T, -_block_diag_out(c_im[0]).T).T)

    in_flight = []

    def send_early(grads):
        names = list(grads)
        handles = _scatter_start([grads[n] for n in names], name="start_gradients_%d" % len(in_flight))
        in_flight.append((names,) + handles[:4])
        return handles[4]

    grad_x, d_mod, big, small = _local_step(x, mod, loss_target, W, P, send_early)

    small_list = [small["loss_cols"], small["g_mix"], small["b_gate"], small["g_ab_re"], small["g_ab_im"],
                  _diag_blocks_in(small["d_bb_re"]), _diag_blocks_in(small["d_bb_im"]),
                  _diag_blocks_out(small["d_cc_re"]), -_diag_blocks_out(small["d_cc_im"]),
                  small["g_ffn"], small["b_conv"], small["g_final"], small["d_skip"], small["b_glu"]]
    small_packed, small_offs = _pack(small_list)
    slots = [_cols_to_slots(big["w_in"])]
    from_sibling = _sibling_swap(slots, name="swap_sibling_gradients")
    chip_sums = _chip_scatter(_add_pairs(slots, from_sibling), name="scatter_chip_gradients")
    small_all, dmod_slots = _gather_all([small_packed, d_mod.reshape(B, 6 * D)], name="gather_small_gradients")

    out = {}

    def update(name, parts, own=None):
        w2 = args[name][0]
        g, dl, mn, vn = _adamw(w2, args["m_" + name][0], args["v_" + name][0], parts, name="adamw_" + name, own=own)
        for key, val in (("grad_", g), ("delta_", dl), ("new_m_", mn), ("new_v_", vn)):
            out[key + name] = val[None]

    update("w_in", chip_sums[0])
    my_slot = me.astype(jnp.int32).reshape(1)
    for i, (names, send_sems, recv_sems, sent, lands) in enumerate(in_flight):
        sent, lands = _scatter_wait(send_sems, recv_sems, sent, lands, grad_x, name="wait_gradients_%d" % i)
        for name, own_slots, landed in zip(names, sent, lands):
            update(name, landed, own=(own_slots, my_slot))

    dmod_all = dmod_slots.reshape(N_DEV * B, 6 * D)
    dmod_cols = lax.dynamic_slice(dmod_all, (0, me * n_ada), (N_DEV * B, n_ada))
    d_w_ada, d_b_ada = _ada_bwd(c_all, dmod_all, dmod_cols)
    update("w_ada", d_w_ada[None])

    loss_row, loss_n = small_offs[0]
    small_sum, loss_vec = _sum_parts(small_all, (loss_row, loss_row + loss_n // LANES))
    shapes = [(1, D), (1, D), (1, 2 * D), (SSM_GROUPS, SSM_STATE), (SSM_GROUPS, SSM_STATE), (SSM_COLS, SSM_GROUP_CH),
              (SSM_COLS, SSM_GROUP_CH), (1, SSM_GROUPS, SSM_GROUP_CH, SSM_STATE),
              (1, SSM_GROUPS, SSM_GROUP_CH, SSM_STATE), (1, D), (1, D_FF), (D,), (1, SSM_WIDTH), (1, SSM_WIDTH)]
    (_, s_g_mix, s_b_gate, s_ab_re, s_ab_im, s_bb_re, s_bb_im, s_c_re, s_c_im, s_g_ffn, s_b_conv, s_g_final,
     s_d_skip, s_b_glu) = _unpack(small_sum, small_offs, shapes)
    d_b_re2, d_b_im2, d_f_re, d_f_im = _s5_input_matrix_bwd(col(f_re), col(f_im), b_re2, b_im2, s_bb_re, s_bb_im)
    d_a_re, d_a_im, d_log_dt = _s5_params_bwd(a_re[0], a_im[0], log_dt[0].reshape(SSM_GROUPS, 1), s_ab_re, s_ab_im,
                                              d_f_re.reshape(SSM_GROUPS, SSM_STATE),
                                              d_f_im.reshape(SSM_GROUPS, SSM_STATE))
    grads_small = dict(b_ada=d_b_ada, g_mix=s_g_mix, b_gate=s_b_gate, a_re=d_a_re[None], a_im=d_a_im[None],
                       log_dt=d_log_dt.reshape(1, SSM_GROUPS), b_re=d_b_re2.reshape(b_re.shape),
                       b_im=d_b_im2.reshape(b_im.shape), c_re=s_c_re, c_im=s_c_im, d_skip=s_d_skip, b_glu=s_b_glu,
                       g_ffn=s_g_ffn, b_conv=s_b_conv, g_final=s_g_final)
    w_pack, offs = _pack([args[n] for n in SMALL_ORDER])
    m_pack, _ = _pack([args["m_" + n] for n in SMALL_ORDER])
    v_pack, _ = _pack([args["v_" + n] for n in SMALL_ORDER])
    g_pack, _ = _pack([grads_small[n] for n in SMALL_ORDER])
    res = _adamw(w_pack, m_pack, v_pack, g_pack[None], name="adamw_small")
    shapes_small = [args[n].shape for n in SMALL_ORDER]
    for key, packed in zip(("grad_", "delta_", "new_m_", "new_v_"), res):
        for n, val in zip(SMALL_ORDER, _unpack(packed, offs, shapes_small)):
            out[key + n] = val

    order = ["w_ada", "b_ada", "g_mix", "w_in", "b_gate", "a_re", "a_im", "log_dt", "b_re", "b_im", "c_re", "c_im",
             "d_skip", "w_glu", "b_glu", "w_proj_att", "w_proj_ssm", "w_out", "g_ffn", "w_up", "w_conv", "b_conv",
             "w_down", "g_final"]
    loss = loss_vec[0, 0]
    return (loss, grad_x, *[out[k + n] for k in ("grad_", "delta_", "new_m_", "new_v_") for n in order])
```

```python
import math

import jax
import jax.numpy as jnp
from jax import lax
from jax.experimental import pallas as pl
from jax.experimental.pallas import tpu as pltpu

F32 = jnp.float32
BF16 = jnp.bfloat16

N_DEV = 8
D_MODEL = 1024
N_HEADS = 8
HEAD_DIM = 64
ATT_WIDTH = N_HEADS * HEAD_DIM
DILATIONS = (1, 4, 16)
WIN = 128
SSM_GROUPS = 16
SSM_GROUP_CH = 16
SSM_WIDTH = SSM_GROUPS * SSM_GROUP_CH
SSM_STATE = 64
SSM_COLS = SSM_GROUPS * SSM_STATE
D_FF = 2048
EPS = 1e-6
NEG_INF = -1e30
ADAM_LR, ADAM_B1, ADAM_B2, ADAM_EPS, ADAM_WD, ADAM_STEP = 0.001, 0.9, 0.999, 1e-08, 0.01, 10

V7X_VMEM_LIMIT = 56 * 1024 * 1024
LANES = 128


def _params(n_grid):
    return pltpu.CompilerParams(dimension_semantics=("arbitrary",) * n_grid,
                                vmem_limit_bytes=V7X_VMEM_LIMIT)


def _tile(n, pref):
    if n <= pref:
        return n
    t = (pref // LANES) * LANES
    while t > 0:
        if n % t == 0:
            return t
        t -= LANES
    return n


def _matmul(a, b, *, ta=False, tb=False, out_dtype=F32, name):
    if ta:
        K, M = a.shape
    else:
        M, K = a.shape
    if tb:
        N, K2 = b.shape
    else:
        K2, N = b.shape
    assert K == K2, (a.shape, b.shape)
    if ta:
        tm, tn, tk = _tile(M, 1024), _tile(N, 2048), _tile(K, 512)
    else:
        tm, tk = _tile(M, 512), _tile(K, 4096)
        tn = _tile(N, 2048 if K <= 2048 else 1024)
    nk = K // tk
    dn = (((0,) if ta else (1,), (1,) if tb else (0,)), ((), ()))

    def body(a_ref, b_ref, o_ref, acc_ref):
        k = pl.program_id(2)
        part = lax.dot_general(a_ref[...].astype(BF16), b_ref[...].astype(BF16), dn, preferred_element_type=F32)
        if nk == 1:
            o_ref[...] = part.astype(o_ref.dtype)
            return

        @pl.when(k == 0)
        def _():
            acc_ref[...] = jnp.zeros_like(acc_ref)

        acc_ref[...] += part

        @pl.when(k == nk - 1)
        def _():
            o_ref[...] = acc_ref[...].astype(o_ref.dtype)

    a_spec = (pl.BlockSpec((tk, tm), lambda j, i, k: (k, i)) if ta
              else pl.BlockSpec((tm, tk), lambda j, i, k: (i, k)))
    b_spec = (pl.BlockSpec((tn, tk), lambda j, i, k: (j, k)) if tb
              else pl.BlockSpec((tk, tn), lambda j, i, k: (k, j)))
    return pl.pallas_call(
        body, name=name, grid=(N // tn, M // tm, nk),
        in_specs=[a_spec, b_spec],
        out_specs=pl.BlockSpec((tm, tn), lambda j, i, k: (i, j)),
        out_shape=jax.ShapeDtypeStruct((M, N), out_dtype),
        scratch_shapes=[pltpu.VMEM((tm, tn) if nk > 1 else (8, LANES), F32)],
        compiler_params=_params(3),
    )(a, b)


HALF = 256
UP_SLOTS = N_DEV // 2
UP_GROUP = 4 * HALF


def _group_weight(w_ref):
    return jnp.concatenate([w_ref[0, :, :HALF], w_ref[1, :, :HALF], w_ref[0, :, HALF:], w_ref[1, :, HALF:]], axis=1)


def _up_weight_spec(K, index):
    return pl.BlockSpec((2, None, K, 2 * HALF), index)


def _up_fwd(a, w3, name):
    M, K = a.shape
    tm = _tile(M, 1024)

    def body(a_ref, w_ref, o_ref):
        o_ref[...] = jnp.dot(a_ref[...].astype(BF16), _group_weight(w_ref),
                             preferred_element_type=F32).astype(o_ref.dtype)

    return pl.pallas_call(
        body, name=name, grid=(UP_SLOTS, M // tm),
        in_specs=[pl.BlockSpec((tm, K), lambda j, i: (i, 0)), _up_weight_spec(K, lambda j, i: (0, j, 0, 0))],
        out_specs=pl.BlockSpec((tm, UP_GROUP), lambda j, i: (i, j)),
        out_shape=jax.ShapeDtypeStruct((M, UP_SLOTS * UP_GROUP), BF16), compiler_params=_params(2),
    )(a, w3.reshape(2, UP_SLOTS, K, 2 * HALF))


def _up_dx(d, w3, name):
    M = d.shape[0]
    K = w3.shape[1]
    tm = _tile(M, 1024)

    def body(d_ref, w_ref, o_ref, acc_ref):
        j = pl.program_id(1)

        @pl.when(j == 0)
        def _():
            acc_ref[...] = jnp.zeros_like(acc_ref)

        acc_ref[...] += lax.dot_general(d_ref[...], _group_weight(w_ref), _NT, preferred_element_type=F32)

        @pl.when(j == UP_SLOTS - 1)
        def _():
            o_ref[...] = acc_ref[...].astype(o_ref.dtype)

    return pl.pallas_call(
        body, name=name, grid=(M // tm, UP_SLOTS),
        in_specs=[pl.BlockSpec((tm, UP_GROUP), lambda i, j: (i, j)), _up_weight_spec(K, lambda i, j: (0, j, 0, 0))],
        out_specs=pl.BlockSpec((tm, K), lambda i, j: (i, 0)),
        out_shape=jax.ShapeDtypeStruct((M, K), BF16), scratch_shapes=[pltpu.VMEM((tm, K), F32)],
        compiler_params=_params(2),
    )(d, w3.reshape(2, UP_SLOTS, K, 2 * HALF))


def _up_dw(a, d, name):
    M, K = a.shape
    tk = _tile(M, 512)
    nk = M // tk

    def body(a_ref, d_ref, o_ref, acc_ref):
        k = pl.program_id(1)

        @pl.when(k == 0)
        def _():
            acc_ref[...] = jnp.zeros_like(acc_ref)

        acc_ref[...] += lax.dot_general(a_ref[...], d_ref[...], _TN, preferred_element_type=F32)

        @pl.when(k == nk - 1)
        def _():
            for half in range(2):
                for part in range(2):
                    lo = (2 * half + part) * HALF
                    o_ref[part, :, half * HALF:(half + 1) * HALF] = acc_ref[:, lo:lo + HALF].astype(o_ref.dtype)

    out = pl.pallas_call(
        body, name=name, grid=(UP_SLOTS, nk),
        in_specs=[pl.BlockSpec((tk, K), lambda j, k: (k, 0)), pl.BlockSpec((tk, UP_GROUP), lambda j, k: (k, j))],
        out_specs=_up_weight_spec(K, lambda j, k: (0, j, 0, 0)),
        out_shape=jax.ShapeDtypeStruct((2, UP_SLOTS, K, 2 * HALF), BF16),
        scratch_shapes=[pltpu.VMEM((K, UP_GROUP), F32)], compiler_params=_params(2),
    )(a, d)
    return out.reshape(N_DEV, K, 2 * HALF)


def _rowwise(fn, rows, bvecs, consts, out_rows, out_b, out_g, *, ts, name):
    B, S = rows[0][0].shape[:2]
    nin = len(rows) + len(bvecs) + len(consts)
    nr, nb, ng = len(out_rows), len(out_b), len(out_g)

    def body(*refs):
        b = pl.program_id(0)
        s = pl.program_id(1)
        vals = [r[...] for r in refs[:nin]]
        vals[:len(rows)] = [v.astype(F32) for v in vals[:len(rows)]]
        outs = fn(*vals)
        if not isinstance(outs, (tuple, list)):
            outs = (outs,)
        orefs = refs[nin:]
        for i in range(nr):
            orefs[i][...] = outs[i].astype(orefs[i].dtype)
        for i in range(nb):
            ref = orefs[nr + i]

            @pl.when(s == 0)
            def _(ref=ref):
                ref[...] = jnp.zeros_like(ref)

            ref[...] += outs[nr + i]
        for i in range(ng):
            ref = orefs[nr + nb + i]

            @pl.when((s == 0) & (b == 0))
            def _(ref=ref):
                ref[...] = jnp.zeros_like(ref)

            ref[...] += outs[nr + nb + i]

    in_specs = ([pl.BlockSpec((None, ts, cb), lambda b, s, ci=ci: (b, s, ci)) for (_, cb, ci) in rows]
                + [pl.BlockSpec((None, 1, cb), lambda b, s, ci=ci: (b, 0, ci)) for (_, cb, ci) in bvecs]
                + [pl.BlockSpec(a.shape, lambda b, s: (0, 0)) for a in consts])
    out_shape = ([jax.ShapeDtypeStruct((B, S, c), dt) for (c, dt) in out_rows]
                 + [jax.ShapeDtypeStruct((B, 1, c), F32) for c in out_b]
                 + [jax.ShapeDtypeStruct(rc, F32) for rc in out_g])
    out_specs = ([pl.BlockSpec((None, ts, c), lambda b, s: (b, s, 0)) for (c, _) in out_rows]
                 + [pl.BlockSpec((None, 1, c), lambda b, s: (b, 0, 0)) for c in out_b]
                 + [pl.BlockSpec(rc, lambda b, s: (0, 0)) for rc in out_g])
    args = [a for (a, _, _) in rows] + [a for (a, _, _) in bvecs] + list(consts)
    return pl.pallas_call(
        body, name=name, grid=(B, S // ts), in_specs=in_specs, out_specs=out_specs,
        out_shape=out_shape, compiler_params=_params(2),
    )(*args)


def _col_sum(v):
    return jnp.sum(v, axis=0, keepdims=True)


def _rms_scale(h):
    return lax.rsqrt(jnp.mean(h * h, axis=-1, keepdims=True) + EPS)


def _sigmoid(v):
    return 1.0 / (1.0 + jnp.exp(-v))


ATT_SCALE = HEAD_DIM ** -0.5
COPY_ROWS = 256
_NT = (((1,), (1,)), ((), ()))
_TN = (((0,), (0,)), ((), ()))


def _row_chunks(d, seq):
    sub = seq // d
    out = []
    for r in range(d):
        for c0 in range(0, sub, COPY_ROWS):
            n = min(COPY_ROWS, sub - c0)
            out.append((pl.ds(r + c0 * d, n, stride=d), r * sub + c0, n))
    return out


ATT_UNROLL = 8
KEYS = 2 * WIN


def _zero_once(refs):
    @pl.when((pl.program_id(0) == 0) & (pl.program_id(1) == 0))
    def _():
        for r in refs:
            r[...] = jnp.zeros_like(r)


def _pair_bias(bias_ref, slopes_ref, hp, d, key_major):
    shape = (KEYS, WIN) if key_major else (WIN, KEYS)
    qi = lax.broadcasted_iota(jnp.int32, shape, 1 if key_major else 0)
    kj = lax.broadcasted_iota(jnp.int32, shape, 0 if key_major else 1)
    dist = WIN + qi - kj
    valid = (dist >= 0) & (dist <= WIN)
    distf = dist.astype(F32)
    for h in range(2):
        slope_d = slopes_ref[2 * hp + h] * float(d)
        with_prev = jnp.where(valid, -(slope_d * distf), NEG_INF)
        no_prev = jnp.where(kj >= WIN, with_prev, NEG_INF)
        span = slice(h * KEYS, (h + 1) * KEYS)
        if key_major:
            bias_ref[1, span, :] = with_prev
            bias_ref[0, span, :] = no_prev
        else:
            bias_ref[1, :, span] = with_prev
            bias_ref[0, :, span] = no_prev


def _stack_heads(v):
    first = lax.broadcasted_iota(jnp.int32, v.shape, 1) < HEAD_DIM
    zero = jnp.zeros_like(v)
    return jnp.concatenate([jnp.where(first, v, zero), jnp.where(first, zero, v)], axis=0)


def _per_head(c0, c1, n):
    return jnp.where(lax.broadcasted_iota(jnp.int32, (n, LANES), 1) < HEAD_DIM, c0, c1)


def _qkv_spec(seq, j):
    return pl.BlockSpec((None, seq, LANES), lambda b, hp: (b, 0, 3 * hp + j))


def _attention_fwd(qkv, slopes):
    B, S, _ = qkv.shape
    n_blk = S // WIN
    n_pair = N_HEADS // 2

    def body(slopes_ref, q_ref, k_ref, v_ref, o_ref, lse_ref, qp, kp, vp, bias, acc, mx, sm, acc_n, mx_n, sm_n):
        hp = pl.program_id(1)
        _zero_once((kp, vp))
        for p, d in enumerate(DILATIONS):
            nb = n_blk // d
            chunks = _row_chunks(d, S)
            for src, dst, n in chunks:
                qp[dst:dst + n, :] = (q_ref[src, :] * ATT_SCALE).astype(BF16)
                kp[WIN + dst:WIN + dst + n, :] = k_ref[src, :].astype(BF16)
                vp[WIN + dst:WIN + dst + n, :] = v_ref[src, :].astype(BF16)
            _pair_bias(bias, slopes_ref, hp, d, key_major=False)
            acc_t, mx_t, sm_t = (acc_n, mx_n, sm_n) if d == 1 else (acc, mx, sm)

            def block(i, carry, p=p, nb=nb, acc_t=acc_t, mx_t=mx_t, sm_t=sm_t):
                cur = pl.ds(pl.multiple_of(i * WIN, WIN), WIN)
                keys = pl.ds(pl.multiple_of(i * WIN, WIN), KEYS)
                flag = ((i % nb) > 0).astype(jnp.int32)
                s = lax.dot_general(qp[cur, :], _stack_heads(kp[keys, :]), _NT, preferred_element_type=F32)
                s = s + bias[flag]
                es, ms, ls = [], [], []
                for h in range(2):
                    sh = s[:, h * KEYS:(h + 1) * KEYS]
                    m = jnp.max(jnp.maximum(sh[:, :WIN], sh[:, WIN:]), axis=1, keepdims=True)
                    e = jnp.exp(sh - m)
                    es.append(e.astype(BF16))
                    ms.append(m)
                    ls.append(jnp.sum(e[:, :WIN] + e[:, WIN:], axis=1, keepdims=True))
                acc_t[p, cur, :] = jnp.dot(jnp.concatenate(es, axis=1), _stack_heads(vp[keys, :]),
                                           preferred_element_type=F32)
                mx_t[p, cur, :] = _per_head(ms[0], ms[1], WIN)
                sm_t[p, cur, :] = _per_head(ls[0], ls[1], WIN)
                return carry

            lax.fori_loop(0, n_blk, block, 0, unroll=ATT_UNROLL)
            if d > 1:
                for src, dst, n in chunks:
                    acc_n[p, src, :] = acc[p, dst:dst + n, :]
                    mx_n[p, src, :] = mx[p, dst:dst + n, :]
                    sm_n[p, src, :] = sm[p, dst:dst + n, :]

        chunk = 256

        def merge(i, carry):
            rows = pl.ds(pl.multiple_of(i * chunk, chunk), chunk)
            ms = [mx_n[p, rows, :] for p in range(3)]
            m = jnp.maximum(jnp.maximum(ms[0], ms[1]), ms[2])
            ws = [jnp.exp(mp - m) for mp in ms]
            l = ws[0] * sm_n[0, rows, :] + ws[1] * sm_n[1, rows, :] + ws[2] * sm_n[2, rows, :]
            o = (ws[0] * acc_n[0, rows, :] + ws[1] * acc_n[1, rows, :] + ws[2] * acc_n[2, rows, :]) / l
            o_ref[rows, :] = o.astype(o_ref.dtype)
            lse = m + jnp.log(l)
            for h in range(2):
                lse_ref[rows, h:h + 1] = lse[:, h * HEAD_DIM:h * HEAD_DIM + 1]
            return carry

        lax.fori_loop(0, S // chunk, merge, 0)

    return pl.pallas_call(
        body, name="attention_fwd", grid=(B, n_pair),
        in_specs=[pl.BlockSpec(memory_space=pltpu.SMEM), _qkv_spec(S, 0), _qkv_spec(S, 1), _qkv_spec(S, 2)],
        out_specs=[pl.BlockSpec((None, S, LANES), lambda b, hp: (b, 0, hp)),
                   pl.BlockSpec((None, None, S, 2), lambda b, hp: (b, hp, 0, 0))],
        out_shape=[jax.ShapeDtypeStruct((B, S, ATT_WIDTH), BF16),
                   jax.ShapeDtypeStruct((B, n_pair, S, 2), F32)],
        scratch_shapes=[pltpu.VMEM((S, LANES), BF16), pltpu.VMEM((S + WIN, LANES), BF16),
                        pltpu.VMEM((S + WIN, LANES), BF16), pltpu.VMEM((2, WIN, 2 * KEYS), F32)]
        + [pltpu.VMEM((3, S, LANES), F32)] * 6,
        compiler_params=_params(2),
    )(slopes, qkv, qkv, qkv)


def _attention_bwd(qkv, o, do, lse, slopes):
    B, S, _ = qkv.shape
    n_blk = S // WIN
    n_pair = N_HEADS // 2

    def body(slopes_ref, q_ref, k_ref, v_ref, o_ref, do_ref, lse_ref, dx_ref,
             qp, dop, kp, vp, aux, auxp, aux_t, bias_t, dqp, dvk, dq_n, dk_n, dv_n):
        hp = pl.program_id(1)
        aux[...] = jnp.zeros_like(aux)
        for c0 in range(0, S, COPY_ROWS):
            rows = slice(c0, c0 + COPY_ROWS)
            prod = do_ref[rows, :] * o_ref[rows, :].astype(F32)
            for h in range(2):
                aux[rows, 2 * h:2 * h + 1] = lse_ref[rows, h:h + 1]
                aux[rows, 2 * h + 1:2 * h + 2] = jnp.sum(prod[:, h * HEAD_DIM:(h + 1) * HEAD_DIM], axis=1,
                                                         keepdims=True)
        dq_n[...] = jnp.zeros_like(dq_n)
        dk_n[...] = jnp.zeros_like(dk_n)
        dv_n[...] = jnp.zeros_like(dv_n)
        _zero_once((kp, vp))
        for p, d in enumerate(DILATIONS):
            nb = n_blk // d
            chunks = _row_chunks(d, S)
            for src, dst, n in chunks:
                auxp[dst:dst + n, :] = aux[src, :]
                qp[dst:dst + n, :] = (q_ref[src, :] * ATT_SCALE).astype(BF16)
                dop[dst:dst + n, :] = do_ref[src, :].astype(BF16)
                kp[WIN + dst:WIN + dst + n, :] = k_ref[src, :].astype(BF16)
                vp[WIN + dst:WIN + dst + n, :] = v_ref[src, :].astype(BF16)
            for i in range(n_blk):
                aux_t[i] = auxp[i * WIN:(i + 1) * WIN, :].T[0:8, :]
            _pair_bias(bias_t, slopes_ref, hp, d, key_major=True)
            dvk[...] = jnp.zeros_like(dvk)

            def block(i, carry, nb=nb):
                cur = pl.ds(pl.multiple_of(i * WIN, WIN), WIN)
                keys = pl.ds(pl.multiple_of(i * WIN, WIN), KEYS)
                flag = ((i % nb) > 0).astype(jnp.int32)
                q2, do2 = qp[cur, :], dop[cur, :]
                kc = _stack_heads(kp[keys, :])
                s_t = lax.dot_general(kc, q2, _NT, preferred_element_type=F32) + bias_t[flag]
                dp_t = lax.dot_general(_stack_heads(vp[keys, :]), do2, _NT, preferred_element_type=F32)
                ps, dss = [], []
                for h in range(2):
                    span = slice(h * KEYS, (h + 1) * KEYS)
                    p_t = jnp.exp(s_t[span, :] - aux_t[i, 2 * h:2 * h + 1, :])
                    ds_t = p_t * (dp_t[span, :] - aux_t[i, 2 * h + 1:2 * h + 2, :])
                    ps.append(p_t.astype(BF16))
                    dss.append(ds_t.astype(BF16))
                do_rows, q_rows = _stack_heads(do2), _stack_heads(q2)
                zr = jnp.zeros_like(do_rows)
                rhs = jnp.concatenate([jnp.concatenate([do_rows, zr], axis=1),
                                       jnp.concatenate([zr, q_rows], axis=1)], axis=0)
                dvk[keys, :] += jnp.dot(jnp.concatenate(ps + dss, axis=1), rhs, preferred_element_type=F32)
                dqp[cur, :] = lax.dot_general(jnp.concatenate(dss, axis=0), kc, _TN, preferred_element_type=F32)
                return carry

            lax.fori_loop(0, n_blk, block, 0, unroll=ATT_UNROLL)
            for src, dst, n in chunks:
                dq_n[src, :] += dqp[dst:dst + n, :]
                dv_n[src, :] += dvk[WIN + dst:WIN + dst + n, :LANES]
                dk_n[src, :] += dvk[WIN + dst:WIN + dst + n, LANES:]
        for c0 in range(0, S, COPY_ROWS):
            rows = slice(c0, c0 + COPY_ROWS)
            dx_ref[rows, 0:LANES] = (dq_n[rows, :] * ATT_SCALE).astype(dx_ref.dtype)
            dx_ref[rows, LANES:2 * LANES] = dk_n[rows, :].astype(dx_ref.dtype)
            dx_ref[rows, 2 * LANES:3 * LANES] = dv_n[rows, :].astype(dx_ref.dtype)

    pair = lambda width: pl.BlockSpec((None, S, width), lambda b, hp: (b, 0, hp))
    vm = lambda shape, dt: pltpu.VMEM(shape, dt)
    return pl.pallas_call(
        body, name="attention_bwd", grid=(B, n_pair),
        in_specs=[pl.BlockSpec(memory_space=pltpu.SMEM), _qkv_spec(S, 0), _qkv_spec(S, 1), _qkv_spec(S, 2),
                  pair(LANES), pair(LANES), pl.BlockSpec((None, None, S, 2), lambda b, hp: (b, hp, 0, 0))],
        out_specs=pair(3 * LANES),
        out_shape=jax.ShapeDtypeStruct((B, S, 3 * ATT_WIDTH), BF16),
        scratch_shapes=[vm((S, LANES), BF16), vm((S, LANES), BF16),
                        vm((S + WIN, LANES), BF16), vm((S + WIN, LANES), BF16),
                        vm((S, LANES), F32), vm((S, LANES), F32), vm((n_blk, 8, WIN), F32),
                        vm((2, 2 * KEYS, WIN), F32),
                        vm((S, LANES), F32), vm((S + WIN, 2 * LANES), F32),
                        vm((S, LANES), F32), vm((S, LANES), F32), vm((S, LANES), F32)],
        compiler_params=_params(2),
    )(slopes, qkv, qkv, qkv, o, do, lse)


SCAN_COLS = 256
SCAN_ROWS = 8


def _rows_to_tile(rows):
    rid = lax.broadcasted_iota(jnp.int32, (SCAN_ROWS, rows[0].shape[1]), 0)
    tile = jnp.broadcast_to(rows[0], rid.shape)
    for k in range(1, SCAN_ROWS):
        tile = jnp.where(rid == k, rows[k], tile)
    return tile


SCAN_UNROLL = 4


def _complex_powers(ar, ai, n):
    out = [(ar, ai)]
    for _ in range(n - 1):
        pr, pi = out[-1]
        out.append((pr * ar - pi * ai, pr * ai + pi * ar))
    return out


def _round_multipliers(powers, rid, reverse):
    out = []
    for s in (1, 2, 4):
        keep = (rid < SCAN_ROWS - s) if reverse else (rid >= s)
        out.append((jnp.where(keep, powers[s - 1][0], 0.0), jnp.where(keep, powers[s - 1][1], 0.0)))
    return out


def _tile_scan(xr, xi, multipliers, reverse):
    for s, (mr, mi) in zip((1, 2, 4), multipliers):
        shift = SCAN_ROWS - s if reverse else s
        sr, si = pltpu.roll(xr, shift, 0), pltpu.roll(xi, shift, 0)
        xr, xi = xr + (mr * sr - mi * si), xi + (mr * si + mi * sr)
    return xr, xi


def _scan_fwd(bu, a_row):
    B, S, _ = bu.shape
    groups = 2
    nc = SSM_COLS // (groups * SCAN_COLS)
    nt = S // SCAN_ROWS
    LAST = slice(SCAN_ROWS - 1, SCAN_ROWS)

    def body(bu_ref, a_ref, xs_ref):
        rid = lax.broadcasted_iota(jnp.int32, (SCAN_ROWS, SCAN_COLS), 0)
        consts = []
        for g in range(groups):
            re = slice(2 * g * SCAN_COLS, (2 * g + 1) * SCAN_COLS)
            im = slice((2 * g + 1) * SCAN_COLS, (2 * g + 2) * SCAN_COLS)
            powers = _complex_powers(a_ref[:, re], a_ref[:, im], SCAN_ROWS)
            carry_mult = (_rows_to_tile([p[0] for p in powers]), _rows_to_tile([p[1] for p in powers]))
            consts.append((re, im, carry_mult, _round_multipliers(powers, rid, reverse=False)))

        def tile(i, carry):
            out = []
            for (re, im, (cr_t, ci_t), rounds), (cr, ci) in zip(consts, carry):
                xr, xi = _tile_scan(bu_ref[i, :, re], bu_ref[i, :, im], rounds, reverse=False)
                xs_ref[i, :, re] = xr + (cr_t * cr - ci_t * ci)
                xs_ref[i, :, im] = xi + (cr_t * ci + ci_t * cr)
                out.append((xs_ref[i, LAST, re], xs_ref[i, LAST, im]))
            return tuple(out)

        zero = jnp.zeros((1, SCAN_COLS), F32)
        lax.fori_loop(0, nt, tile, ((zero, zero),) * groups, unroll=SCAN_UNROLL)

    col = pl.BlockSpec((None, nt, SCAN_ROWS, 2 * groups * SCAN_COLS), lambda b, j: (b, 0, 0, j))
    xs = pl.pallas_call(
        body, name="s5_scan_fwd", grid=(B, nc),
        in_specs=[col, pl.BlockSpec((1, 2 * groups * SCAN_COLS), lambda b, j: (0, j))],
        out_specs=col,
        out_shape=jax.ShapeDtypeStruct((B, nt, SCAN_ROWS, 2 * SSM_COLS), F32),
        compiler_params=_params(2),
    )(bu.reshape(B, nt, SCAN_ROWS, 2 * SSM_COLS), a_row)
    return xs.reshape(B, S, 2 * SSM_COLS)


def _scan_bwd(dxs, xs, a_row):
    B, S, _ = dxs.shape
    nc = SSM_COLS // SCAN_COLS
    nt = S // SCAN_ROWS
    RE, IM = slice(0, SCAN_COLS), slice(SCAN_COLS, 2 * SCAN_COLS)
    FIRST, LAST = slice(0, 1), slice(SCAN_ROWS - 1, SCAN_ROWS)

    def body(d_ref, x_ref, a_ref, lam_ref, ga_ref):
        b = pl.program_id(1)
        powers = _complex_powers(a_ref[:, RE], -a_ref[:, IM], SCAN_ROWS)
        rid = lax.broadcasted_iota(jnp.int32, (SCAN_ROWS, SCAN_COLS), 0)
        cr_t = _rows_to_tile([powers[SCAN_ROWS - 1 - r][0] for r in range(SCAN_ROWS)])
        ci_t = _rows_to_tile([powers[SCAN_ROWS - 1 - r][1] for r in range(SCAN_ROWS)])
        rounds = _round_multipliers(powers, rid, reverse=True)

        @pl.when(b == 0)
        def _():
            ga_ref[...] = jnp.zeros_like(ga_ref)

        def tile(j, carry):
            cr, ci, accr, acci = carry
            i = nt - 1 - j
            lr, li = _tile_scan(d_ref[i, :, RE], d_ref[i, :, IM], rounds, reverse=True)
            lam_r = lr + (cr_t * cr - ci_t * ci)
            lam_i = li + (cr_t * ci + ci_t * cr)
            lam_ref[i, :, RE] = lam_r
            lam_ref[i, :, IM] = lam_i
            ip = jnp.maximum(i - 1, 0)
            keep = (i > 0).astype(F32)
            xpr = jnp.where(rid == 0, x_ref[ip, LAST, RE] * keep, pltpu.roll(x_ref[i, :, RE], 1, 0))
            xpi = jnp.where(rid == 0, x_ref[ip, LAST, IM] * keep, pltpu.roll(x_ref[i, :, IM], 1, 0))
            accr = accr + lam_r * xpr + lam_i * xpi
            acci = acci + lam_i * xpr - lam_r * xpi
            return lam_ref[i, FIRST, RE], lam_ref[i, FIRST, IM], accr, acci

        z1 = jnp.zeros((1, SCAN_COLS), F32)
        z8 = jnp.zeros((SCAN_ROWS, SCAN_COLS), F32)
        _, _, accr, acci = lax.fori_loop(0, nt, tile, (z1, z1, z8, z8), unroll=SCAN_UNROLL)
        ga_ref[:, RE] += _col_sum(accr)
        ga_ref[:, IM] += _col_sum(acci)

    col = pl.BlockSpec((None, nt, SCAN_ROWS, 2 * SCAN_COLS), lambda j, b: (b, 0, 0, j))
    par = pl.BlockSpec((1, 2 * SCAN_COLS), lambda j, b: (0, j))
    t4 = lambda a: a.reshape(B, nt, SCAN_ROWS, 2 * SSM_COLS)
    lam, g_a = pl.pallas_call(
        body, name="s5_scan_bwd", grid=(nc, B),
        in_specs=[col, col, par], out_specs=[col, par],
        out_shape=[jax.ShapeDtypeStruct((B, nt, SCAN_ROWS, 2 * SSM_COLS), F32),
                   jax.ShapeDtypeStruct((1, 2 * SSM_COLS), F32)],
        compiler_params=_params(2),
    )(t4(dxs), t4(xs), a_row)
    return lam.reshape(B, S, 2 * SSM_COLS), g_a


def _s5_discretise(lr, li, log_dt):
    dt = jnp.exp(log_dt)
    mag = jnp.exp(lr * dt)
    ang = li * dt
    ab_re, ab_im = mag * jnp.cos(ang), mag * jnp.sin(ang)
    nr, ni = ab_re - 1.0, ab_im
    den = lr * lr + li * li
    f_re = (nr * lr + ni * li) / den
    f_im = (ni * lr - nr * li) / den
    return dt, ab_re, ab_im, nr, ni, den, f_re, f_im


def _s5_params(a_re, a_im, log_dt):
    def body(lr_ref, li_ref, ld_ref, abr, abi, fr, fi):
        _, ab_re, ab_im, _, _, _, f_re, f_im = _s5_discretise(lr_ref[...], li_ref[...], ld_ref[...])
        abr[...] = ab_re
        abi[...] = ab_im
        fr[...] = f_re
        fi[...] = f_im

    return pl.pallas_call(body, name="s5_params",
                          out_shape=[jax.ShapeDtypeStruct(a_re.shape, F32)] * 4)(a_re, a_im, log_dt)


def _s5_input_matrix(f_re, f_im, b_re, b_im):
    def body(fr, fi, br, bi, o_re, o_im):
        o_re[...] = fr[...] * br[...] - fi[...] * bi[...]
        o_im[...] = fr[...] * bi[...] + fi[...] * br[...]

    return pl.pallas_call(body, name="s5_input_matrix",
                          out_shape=[jax.ShapeDtypeStruct(b_re.shape, F32)] * 2)(f_re, f_im, b_re, b_im)


def _s5_input_matrix_bwd(f_re, f_im, b_re, b_im, g_re, g_im):
    def body(fr, fi, br, bi, gr, gi, dbr, dbi, dfr, dfi):
        dbr[...] = fr[...] * gr[...] + fi[...] * gi[...]
        dbi[...] = fr[...] * gi[...] - fi[...] * gr[...]
        dfr[...] = jnp.sum(br[...] * gr[...] + bi[...] * gi[...], axis=1, keepdims=True)
        dfi[...] = jnp.sum(br[...] * gi[...] - bi[...] * gr[...], axis=1, keepdims=True)

    return pl.pallas_call(
        body, name="s5_input_matrix_bwd",
        out_shape=[jax.ShapeDtypeStruct(b_re.shape, F32)] * 2 + [jax.ShapeDtypeStruct(f_re.shape, F32)] * 2,
    )(f_re, f_im, b_re, b_im, g_re, g_im)


def _s5_params_bwd(a_re, a_im, log_dt, g_ab_re, g_ab_im, d_f_re, d_f_im):
    def body(lr_ref, li_ref, ld_ref, gar, gai, dfr, dfi, o_lr, o_li, o_ld):
        lr, li = lr_ref[...], li_ref[...]
        dt, ab_re, ab_im, nr, ni, den, f_re, f_im = _s5_discretise(lr, li, ld_ref[...])
        d_fr, d_fi = dfr[...], dfi[...]
        d_nr = (d_fr * lr - d_fi * li) / den
        d_ni = (d_fr * li + d_fi * lr) / den
        common = (d_fr * f_re + d_fi * f_im) * 2.0 / den
        d_lr = (d_fr * nr + d_fi * ni) / den - common * lr
        d_li = (d_fr * ni - d_fi * nr) / den - common * li
        d_abr = gar[...] + d_nr
        d_abi = gai[...] + d_ni
        d_mag_mag = d_abr * ab_re + d_abi * ab_im
        d_ang = d_abi * ab_re - d_abr * ab_im
        o_lr[...] = d_lr + d_mag_mag * dt
        o_li[...] = d_li + d_ang * dt
        o_ld[...] = jnp.sum(d_mag_mag * lr + d_ang * li, axis=1, keepdims=True) * dt

    return pl.pallas_call(
        body, name="s5_params_bwd",
        out_shape=[jax.ShapeDtypeStruct(a_re.shape, F32)] * 2 + [jax.ShapeDtypeStruct(log_dt.shape, F32)],
    )(a_re, a_im, log_dt, g_ab_re, g_ab_im, d_f_re, d_f_im)


CONV_COLS = 256


def _shift_down(v, j, row):
    return jnp.where(row >= j, pltpu.roll(v, j, 0), 0.0)


def _shift_up(v, j, row, seq):
    return jnp.where(row < seq - j, pltpu.roll(v, seq - j, 0), 0.0)


def _conv_fwd(up, w_conv, b_conv):
    B, S, _ = up.shape
    nj = D_FF // CONV_COLS

    def body(up_ref, w_ref, b_ref, ff_ref):
        a = up_ref[:, :CONV_COLS].astype(F32)
        val = up_ref[:, CONV_COLS:].astype(F32)
        row = lax.broadcasted_iota(jnp.int32, a.shape, 0)
        w0, w1, w2 = w_ref[0:1, :], w_ref[1:2, :], w_ref[2:3, :]
        conv = b_ref[...] + w0 * a + w1 * _shift_down(a, 1, row) + w2 * _shift_down(a, 2, row)
        ff_ref[...] = (conv * _sigmoid(conv) * val).astype(ff_ref.dtype)

    return pl.pallas_call(
        body, name="conv_gate_fwd", grid=(B, nj),
        in_specs=[pl.BlockSpec((None, S, 2 * CONV_COLS), lambda b, j: (b, 0, j)),
                  pl.BlockSpec((3, CONV_COLS), lambda b, j: (0, j)),
                  pl.BlockSpec((1, CONV_COLS), lambda b, j: (0, j))],
        out_specs=pl.BlockSpec((None, S, CONV_COLS), lambda b, j: (b, 0, j)),
        out_shape=jax.ShapeDtypeStruct((B, S, D_FF), BF16),
        compiler_params=_params(2),
    )(up, w_conv, b_conv)


def _conv_bwd(up, d_ff, w_conv, b_conv):
    B, S, _ = up.shape
    nj = D_FF // CONV_COLS

    def body(up_ref, dff_ref, w_ref, b_ref, dup_ref, dw_ref, db_ref):
        b = pl.program_id(1)
        a = up_ref[:, :CONV_COLS].astype(F32)
        val = up_ref[:, CONV_COLS:].astype(F32)
        row = lax.broadcasted_iota(jnp.int32, a.shape, 0)
        w0, w1, w2 = w_ref[0:1, :], w_ref[1:2, :], w_ref[2:3, :]
        a1, a2 = _shift_down(a, 1, row), _shift_down(a, 2, row)
        conv = b_ref[...] + w0 * a + w1 * a1 + w2 * a2
        sg = _sigmoid(conv)
        dff = dff_ref[...].astype(F32)
        d_val = dff * conv * sg
        dc = dff * val * (sg * (1.0 + conv * (1.0 - sg)))
        d_a = w0 * dc + w1 * _shift_up(dc, 1, row, S) + w2 * _shift_up(dc, 2, row, S)
        dup_ref[:, :CONV_COLS] = d_a.astype(dup_ref.dtype)
        dup_ref[:, CONV_COLS:] = d_val.astype(dup_ref.dtype)

        @pl.when(b == 0)
        def _():
            dw_ref[...] = jnp.zeros_like(dw_ref)
            db_ref[...] = jnp.zeros_like(db_ref)

        dw_ref[0:1, :] += _col_sum(dc * a)
        dw_ref[1:2, :] += _col_sum(dc * a1)
        dw_ref[2:3, :] += _col_sum(dc * a2)
        db_ref[...] += _col_sum(dc)

    return pl.pallas_call(
        body, name="conv_gate_bwd", grid=(nj, B),
        in_specs=[pl.BlockSpec((None, S, 2 * CONV_COLS), lambda j, b: (b, 0, j)),
                  pl.BlockSpec((None, S, CONV_COLS), lambda j, b: (b, 0, j)),
                  pl.BlockSpec((3, CONV_COLS), lambda j, b: (0, j)),
                  pl.BlockSpec((1, CONV_COLS), lambda j, b: (0, j))],
        out_specs=[pl.BlockSpec((None, S, 2 * CONV_COLS), lambda j, b: (b, 0, j)),
                   pl.BlockSpec((3, CONV_COLS), lambda j, b: (0, j)),
                   pl.BlockSpec((1, CONV_COLS), lambda j, b: (0, j))],
        out_shape=[jax.ShapeDtypeStruct((B, S, 2 * D_FF), BF16), jax.ShapeDtypeStruct((3, D_FF), F32),
                   jax.ShapeDtypeStruct((1, D_FF), F32)],
        compiler_params=_params(2),
    )(up, d_ff, w_conv, b_conv)


def _ada_fwd(c_all, w_ada, b_ada):
    def body(c_ref, w_ref, b_ref, o_ref):
        cv = c_ref[...]
        act = (cv * _sigmoid(cv)).astype(BF16)
        o_ref[...] = jnp.dot(act, w_ref[...].astype(BF16), preferred_element_type=F32) + b_ref[...]

    return pl.pallas_call(body, name="ada_fwd",
                          out_shape=jax.ShapeDtypeStruct((c_all.shape[0], w_ada.shape[1]), F32),
                          compiler_params=pltpu.CompilerParams(vmem_limit_bytes=V7X_VMEM_LIMIT))(c_all, w_ada, b_ada)


def _ada_bwd(c_all, dmod_all, dmod_cols):
    def body(c_ref, dm_ref, dmc_ref, dw_ref, db_ref):
        cv = c_ref[...]
        act = (cv * _sigmoid(cv)).astype(BF16)
        dw_ref[...] = lax.dot_general(act, dmc_ref[...].astype(BF16), _TN, preferred_element_type=F32)
        db_ref[...] = _col_sum(dm_ref[...])

    return pl.pallas_call(
        body, name="ada_bwd",
        out_shape=[jax.ShapeDtypeStruct((c_all.shape[1], dmod_cols.shape[1]), F32),
                   jax.ShapeDtypeStruct((1, dmod_all.shape[1]), F32)],
        compiler_params=pltpu.CompilerParams(vmem_limit_bytes=V7X_VMEM_LIMIT))(c_all, dmod_all, dmod_cols)


def _adamw(w, m, v, g_parts, name, own=None):
    R, C = w.shape
    P = g_parts.shape[0]
    tr = R
    for cand in (256, 128, 64, 32, 16, 8):
        if R % cand == 0 and cand * C * 4 * (P + 8) * 2 <= V7X_VMEM_LIMIT // 2:
            tr = cand
            break
    c1 = 1.0 / (1.0 - ADAM_B1 ** ADAM_STEP)
    c2 = 1.0 / (1.0 - ADAM_B2 ** ADAM_STEP)

    def update(w_ref, m_ref, v_ref, g, og, od, om, ov):
        m_new = ADAM_B1 * m_ref[...] + (1.0 - ADAM_B1) * g
        v_new = ADAM_B2 * v_ref[...] + (1.0 - ADAM_B2) * (g * g)
        og[...] = g
        om[...] = m_new
        ov[...] = v_new
        od[...] = -ADAM_LR * ((m_new * c1) / (jnp.sqrt(v_new * c2) + ADAM_EPS) + ADAM_WD * w_ref[...])

    def total(g_ref):
        g = g_ref[0].astype(F32)
        for p in range(1, P):
            g = g + g_ref[p].astype(F32)
        return g

    out_shape = [jax.ShapeDtypeStruct((R, C), F32)] * 4
    if own is None:
        def body(w_ref, m_ref, v_ref, g_ref, og, od, om, ov):
            update(w_ref, m_ref, v_ref, total(g_ref), og, od, om, ov)

        spec = pl.BlockSpec((tr, C), lambda i: (i, 0))
        return pl.pallas_call(
            body, name=name, grid=(R // tr,),
            in_specs=[spec, spec, spec, pl.BlockSpec((P, tr, C), lambda i: (0, i, 0))],
            out_specs=[spec] * 4, out_shape=out_shape, compiler_params=_params(1),
        )(w, m, v, g_parts)

    slots, me = own

    def body_own(me_ref, w_ref, m_ref, v_ref, g_ref, own_ref, og, od, om, ov):
        update(w_ref, m_ref, v_ref, total(g_ref) + own_ref[...].astype(F32), og, od, om, ov)

    spec = pl.BlockSpec((tr, C), lambda i, me_ref: (i, 0))
    grid_spec = pltpu.PrefetchScalarGridSpec(
        num_scalar_prefetch=1, grid=(R // tr,),
        in_specs=[spec, spec, spec, pl.BlockSpec((P, tr, C), lambda i, me_ref: (0, i, 0)),
                  pl.BlockSpec((None, tr, C), lambda i, me_ref: (me_ref[0], i, 0))],
        out_specs=[spec] * 4)
    return pl.pallas_call(body_own, name=name, grid_spec=grid_spec, out_shape=out_shape,
                          compiler_params=_params(1))(me, w, m, v, g_parts, slots)


def _sum_parts(parts, loss_rows):
    P, R, C = parts.shape
    lo, hi = loss_rows

    def body(p_ref, o_ref, loss_ref):
        t = p_ref[0]
        for p in range(1, P):
            t = t + p_ref[p]
        o_ref[...] = t
        tot = jnp.sum(jnp.sum(o_ref[lo:hi, :], axis=1, keepdims=True), axis=0, keepdims=True)
        loss_ref[...] = jnp.broadcast_to(tot, loss_ref.shape)

    return pl.pallas_call(body, name="sum_small_grads",
                          out_shape=[jax.ShapeDtypeStruct((R, C), F32), jax.ShapeDtypeStruct((1, LANES), F32)],
                          compiler_params=pltpu.CompilerParams(vmem_limit_bytes=V7X_VMEM_LIMIT))(parts)


def _exchange(items, name):
    n = len(items)
    MESH = pl.DeviceIdType.MESH

    def body(*refs):
        src, dst = refs[:n], refs[n:2 * n]
        send_sems, recv_sems, local_sems = refs[2 * n:]
        x, y, c = lax.axis_index("x"), lax.axis_index("y"), lax.axis_index("c")
        me = 4 * x + 2 * y + c
        started = []
        for it, (_, per_peer) in enumerate(items):
            own = pltpu.make_async_copy(src[it].at[me] if per_peer else src[it], dst[it].at[me], local_sems.at[it])
            own.start()
            started.append(own)
        sends, recvs = [], []
        for k in range(1, N_DEV):
            px = 1 - x if k & 4 else x
            py = 1 - y if k & 2 else y
            pc = 1 - c if k & 1 else c
            peer = 4 * px + 2 * py + pc
            for it, (_, per_peer) in enumerate(items):
                s = src[it].at[peer] if per_peer else src[it]
                cp = pltpu.make_async_remote_copy(src_ref=s, dst_ref=dst[it].at[me], send_sem=send_sems.at[it, k - 1],
                                                  recv_sem=recv_sems.at[it, k - 1], device_id=(px, py, pc),
                                                  device_id_type=MESH)
                cp.start()
                sends.append(cp)
                recvs.append(pltpu.make_async_remote_copy(
                    src_ref=s, dst_ref=dst[it].at[peer], send_sem=send_sems.at[it, k - 1],
                    recv_sem=recv_sems.at[it, k - 1], device_id=(px, py, pc), device_id_type=MESH))
        for cp in recvs:
            cp.wait_recv()
        for cp in sends:
            cp.wait_send()
        for cp in started:
            cp.wait()

    any_spec = pl.BlockSpec(memory_space=pl.ANY)
    out_shape = []
    for a, per_peer in items:
        shp = a.shape if per_peer else (N_DEV,) + a.shape
        out_shape.append(jax.ShapeDtypeStruct(shp, a.dtype))
    return pl.pallas_call(
        body, name=name, in_specs=[any_spec] * n, out_specs=[any_spec] * n, out_shape=out_shape,
        scratch_shapes=[pltpu.SemaphoreType.DMA((n, N_DEV - 1)), pltpu.SemaphoreType.DMA((n, N_DEV - 1)),
                        pltpu.SemaphoreType.DMA((n,))],
    )(*[a for a, _ in items])


def _remote(src, dst, send_sem, recv_sem, device):
    return pltpu.make_async_remote_copy(src_ref=src, dst_ref=dst, send_sem=send_sem, recv_sem=recv_sem,
                                        device_id=device, device_id_type=pl.DeviceIdType.MESH)


def _mesh_place():
    x, y, c = lax.axis_index("x"), lax.axis_index("y"), lax.axis_index("c")
    other_chips = [(1 - x, y), (x, 1 - y), (1 - x, 1 - y)]
    return x, y, c, (x, y, 1 - c), other_chips


def _gather_all(items, name):
    n = len(items)

    def body(*refs):
        src, dst = refs[:n], refs[n:2 * n]
        send_sems, recv_sems, local_sems = refs[2 * n:]
        x, y, c, sibling, chips = _mesh_place()
        slot = lambda px, py, pc: 4 * px + 2 * py + pc
        me = slot(x, y, c)
        own = [pltpu.make_async_copy(src[it], dst[it].at[me], local_sems.at[it]) for it in range(n)]
        first = []
        for it in range(n):
            first.append(_remote(src[it], dst[it].at[me], send_sems.at[it, 0], recv_sems.at[it, 0], sibling))
            for j, chip in enumerate(chips):
                first.append(_remote(src[it], dst[it].at[me], send_sems.at[it, 1 + j], recv_sems.at[it, 1 + j],
                                     (*chip, c)))
        for cp in own + first:
            cp.start()
        passed = []
        for j, chip in enumerate(chips):
            blk = slot(*chip, c)
            for it in range(n):
                _remote(src[it], dst[it].at[blk], send_sems.at[it, 1 + j], recv_sems.at[it, 1 + j],
                        (*chip, c)).wait_recv()
                fwd = _remote(dst[it].at[blk], dst[it].at[blk], send_sems.at[it, 4 + j], recv_sems.at[it, 4 + j],
                              sibling)
                fwd.start()
                passed.append(fwd)
        for it in range(n):
            _remote(src[it], dst[it].at[slot(x, y, 1 - c)], send_sems.at[it, 0], recv_sems.at[it, 0],
                    sibling).wait_recv()
        for j, chip in enumerate(chips):
            for it in range(n):
                _remote(src[it], dst[it].at[slot(*chip, 1 - c)], send_sems.at[it, 4 + j], recv_sems.at[it, 4 + j],
                        sibling).wait_recv()
        for cp in first + passed:
            cp.wait_send()
        for cp in own:
            cp.wait()

    any_spec = pl.BlockSpec(memory_space=pl.ANY)
    return pl.pallas_call(
        body, name=name, in_specs=[any_spec] * n, out_specs=[any_spec] * n,
        out_shape=[jax.ShapeDtypeStruct((N_DEV,) + a.shape, a.dtype) for a in items],
        scratch_shapes=[pltpu.SemaphoreType.DMA((n, 7)), pltpu.SemaphoreType.DMA((n, 7)),
                        pltpu.SemaphoreType.DMA((n,))],
    )(*items)


def _peers():
    x, y, c = lax.axis_index("x"), lax.axis_index("y"), lax.axis_index("c")
    out = []
    for k in range(1, N_DEV):
        px = 1 - x if k & 4 else x
        py = 1 - y if k & 2 else y
        pc = 1 - c if k & 1 else c
        out.append((k, (px, py, pc), 4 * px + 2 * py + pc))
    return 4 * x + 2 * y + c, out


def _exchange_start(items, name, gather):
    n = len(items)

    def body(*refs):
        src, land = refs[:n], refs[n:2 * n]
        send_sems, recv_sems = refs[2 * n:3 * n], refs[3 * n:4 * n]
        token = refs[-1]
        me, peers = _peers()
        for k, peer, slot in peers:
            for it in range(n):
                _remote(src[it] if gather else src[it].at[slot], land[it].at[me], send_sems[it], recv_sems[it],
                        peer).start()
        token[...] = jnp.zeros_like(token)

    hbm = pl.BlockSpec(memory_space=pltpu.HBM)
    sem = pl.BlockSpec(memory_space=pltpu.SEMAPHORE)
    land_shapes = [(N_DEV,) + (a.shape if gather else a.shape[1:]) for a in items]
    lands = [jnp.zeros(shp, a.dtype) for shp, a in zip(land_shapes, items)]
    outs = pl.pallas_call(
        body, name=name,
        out_shape=(*[pltpu.SemaphoreType.DMA(())] * (2 * n),
                   *[pltpu.HBM(a.shape, a.dtype) for a in items],
                   *[pltpu.HBM(shp, a.dtype) for shp, a in zip(land_shapes, items)],
                   jax.ShapeDtypeStruct((8, LANES), F32)),
        in_specs=[hbm] * (2 * n),
        out_specs=(*[sem] * (2 * n), *[hbm] * (2 * n), pl.BlockSpec(memory_space=pltpu.VMEM)),
        input_output_aliases={i: 2 * n + i for i in range(2 * n)},
        compiler_params=pltpu.CompilerParams(has_side_effects=pltpu.SideEffectType.DATAFLOW_SIDE_EFFECTING),
    )(*[pltpu.with_memory_space_constraint(a, pltpu.HBM) for a in list(items) + lands])
    return (list(outs[:n]), list(outs[n:2 * n]), list(outs[2 * n:3 * n]), list(outs[3 * n:4 * n]), outs[-1])


def _exchange_wait(send_sems, recv_sems, items, lands, after, name):
    n = len(items)

    def body(*refs):
        land = refs[n:2 * n]
        send_sems, recv_sems = refs[2 * n:3 * n], refs[3 * n:4 * n]
        me, peers = _peers()
        for it in range(n):
            seven = land[it].at[pl.ds(0, N_DEV - 1)]
            cp = _remote(seven, seven, send_sems[it], recv_sems[it], peers[0][1])
            cp.wait_send()
            cp.wait_recv()

    hbm = pl.BlockSpec(memory_space=pltpu.HBM)
    sem = pl.BlockSpec(memory_space=pltpu.SEMAPHORE)
    outs = pl.pallas_call(
        body, name=name,
        out_shape=tuple(pltpu.HBM(a.shape, a.dtype) for a in list(items) + list(lands)),
        in_specs=[hbm] * (2 * n) + [sem] * (2 * n) + [pl.BlockSpec(memory_space=pl.ANY)],
        out_specs=tuple([hbm] * (2 * n)),
        input_output_aliases={i: i for i in range(2 * n)},
        compiler_params=pltpu.CompilerParams(has_side_effects=pltpu.SideEffectType.DATAFLOW_SIDE_EFFECTING),
    )(*items, *lands, *send_sems, *recv_sems, after)
    return list(outs[:n]), list(outs[n:])


def _gelu_tanh(y):
    k = math.sqrt(2.0 / math.pi)
    t = jnp.tanh(k * (y + 0.044715 * y * y * y))
    return 0.5 * y * (1.0 + t), t


def _local_step(x, mod, target, W, late_weights, P, send_early):
    B, S, D = x.shape
    T = B * S
    TS = 512
    flat = lambda a: a.reshape(T, a.shape[-1])
    unflat = lambda a: a.reshape(B, S, a.shape[-1])
    mod_col = lambda i: (mod, D, i)

    def f_modnorm(xv, sc, sh, g):
        return (xv * _rms_scale(xv) * g) * (1.0 + sc) + sh

    (u1,) = _rowwise(f_modnorm, [(x, D, 0)], [mod_col(1), mod_col(0)], [P["g_mix"]],
                     [(D, BF16)], [], [], ts=TS, name="modnorm_mix")
    u1f = flat(u1)
    qkv = unflat(_matmul(u1f, W["w_qkv"], name="proj_qkv"))
    us = unflat(_matmul(u1f, W["w_us"], name="proj_ssm_in"))
    gates = unflat(_matmul(u1f, W["w_gates"], out_dtype=BF16, name="proj_gates"))

    o_att, lse = _attention_fwd(qkv, P["slopes"])
    more_w, more_p = late_weights(o_att)
    W, P = {**W, **more_w}, {**P, **more_p}
    y_att = unflat(_matmul(flat(o_att), W["w_proj_att"], out_dtype=BF16, name="proj_att"))

    bu = unflat(_matmul(flat(us), P["bb_big"], name="s5_bu"))
    xs = _scan_fwd(bu, P["a_row"])
    y_mm = unflat(_matmul(flat(xs), P["cc_big"], name="s5_readout"))

    def f_glu(ymm, usv, dsk, wg, bg):
        yv = ymm + dsk * usv
        ge, _ = _gelu_tanh(yv)
        pre = jnp.dot(ge.astype(BF16), wg, preferred_element_type=F32) + bg
        return yv, ge * _sigmoid(pre)

    y_s5, z = _rowwise(f_glu, [(y_mm, SSM_WIDTH, 0), (us, SSM_WIDTH, 0)], [], [P["d_skip"], W["w_glu"], P["b_glu"]],
                       [(SSM_WIDTH, F32), (SSM_WIDTH, BF16)], [], [], ts=TS, name="s5_glu")
    y_ssm = unflat(_matmul(flat(z), W["w_proj_ssm"], out_dtype=BF16, name="proj_ssm"))

    def f_merge(ga, gs, ya, ys, bga, bgs):
        return _sigmoid(ga + bga) * ya + _sigmoid(gs + bgs) * ys

    bga, bgs = P["b_gate"][:, :D], P["b_gate"][:, D:]
    (merged,) = _rowwise(f_merge, [(gates, D, 0), (gates, D, 1), (y_att, D, 0), (y_ssm, D, 0)], [], [bga, bgs],
                         [(D, BF16)], [], [], ts=TS, name="gate_merge")
    mix = unflat(_matmul(flat(merged), W["w_out"], name="proj_out"))

    def f_res_modnorm(xv, mx, gt, sc, sh, g):
        h = xv + gt * mx
        return h, (h * _rms_scale(h) * g) * (1.0 + sc) + sh

    h1, u2 = _rowwise(f_res_modnorm, [(x, D, 0), (mix, D, 0)], [mod_col(2), mod_col(4), mod_col(3)], [P["g_ffn"]],
                      [(D, F32), (D, BF16)], [], [], ts=TS, name="residual_modnorm_ffn")
    up = unflat(_up_fwd(flat(u2), W["w_up"], name="ffn_up"))
    ff = _conv_fwd(up, P["w_conv"], P["b_conv"])
    down = unflat(_matmul(flat(ff), W["w_down"], name="ffn_down"))

    def f_head(h1v, dn, tg, gt, g):
        h2 = h1v + gt * dn
        r = _rms_scale(h2)
        nh = h2 * r
        e = nh * g - tg
        dy = e * (1.0 / D)
        gy = dy * g
        dh = r * (gy - nh * jnp.mean(gy * nh, axis=-1, keepdims=True))
        return (dh, dh * gt, _col_sum(dh * dn), _col_sum(dy * nh), _col_sum(e * e) * (0.5 / D))

    dh2, d_down, d_gt2, d_g_final, loss_cols = _rowwise(
        f_head, [(h1, D, 0), (down, D, 0), (target, D, 0)], [mod_col(5)], [P["g_final"]],
        [(D, F32), (D, BF16)], [D], [(1, D), (1, D)], ts=TS, name="head_loss")

    d_downf = flat(d_down)
    d_ff = unflat(_matmul(d_downf, W["w_down"], tb=True, out_dtype=BF16, name="ffn_down_dx"))
    d_w_down = _matmul(flat(ff), d_downf, ta=True, out_dtype=BF16, name="ffn_down_dw")
    d_up, d_w_conv, d_b_conv = _conv_bwd(up, d_ff, P["w_conv"], P["b_conv"])
    d_upf = flat(d_up)
    d_u2 = unflat(_up_dx(d_upf, W["w_up"], name="ffn_up_dx"))
    d_w_up = _up_dw(flat(u2), d_upf, name="ffn_up_dw")
    token = send_early(dict(w_down=d_w_down.reshape(N_DEV, D_FF // N_DEV, D), w_up=d_w_up))
    g_ffn_after = P["g_ffn"] + token[0:1, 0:1]

    def f_modnorm_bwd(du, h, dres, mx, sc, gt, g):
        r = _rms_scale(h)
        nh = h * r
        dn = du * (1.0 + sc)
        gy = dn * g
        dh = dres + r * (gy - nh * jnp.mean(gy * nh, axis=-1, keepdims=True))
        return (dh, dh * gt, _col_sum(du), _col_sum(du * nh * g), _col_sum(dh * mx), _col_sum(dn * nh))

    dh1, d_mix, d_sh2, d_sc2, d_gt1, d_g_ffn = _rowwise(
        f_modnorm_bwd, [(d_u2, D, 0), (h1, D, 0), (dh2, D, 0), (mix, D, 0)], [mod_col(4), mod_col(2)], [g_ffn_after],
        [(D, F32), (D, BF16)], [D, D, D], [(1, D)], ts=TS, name="modnorm_ffn_bwd")

    d_mixf = flat(d_mix)
    d_merged = unflat(_matmul(d_mixf, W["w_out"], tb=True, out_dtype=BF16, name="proj_out_dx"))
    d_w_out = _matmul(flat(merged), d_mixf, ta=True, out_dtype=BF16, name="proj_out_dw")

    def f_merge_bwd(dm, ga, gs, ya, ys, bga_, bgs_):
        sa, ss = _sigmoid(ga + bga_), _sigmoid(gs + bgs_)
        dga = dm * ya * sa * (1.0 - sa)
        dgs = dm * ys * ss * (1.0 - ss)
        return dm * sa, dm * ss, jnp.concatenate([dga, dgs], axis=1), _col_sum(dga), _col_sum(dgs)

    d_y_att, d_y_ssm, d_gates, d_bga, d_bgs = _rowwise(
        f_merge_bwd, [(d_merged, D, 0), (gates, D, 0), (gates, D, 1), (y_att, D, 0), (y_ssm, D, 0)], [], [bga, bgs],
        [(D, BF16), (D, BF16), (2 * D, BF16)], [], [(1, D), (1, D)], ts=TS, name="gate_merge_bwd")

    d_yaf, d_ysf = flat(d_y_att), flat(d_y_ssm)
    d_o_att = unflat(_matmul(d_yaf, W["w_proj_att"], tb=True, name="proj_att_dx"))
    d_w_proj_att = _matmul(flat(o_att), d_yaf, ta=True, out_dtype=BF16, name="proj_att_dw")
    d_z = unflat(_matmul(d_ysf, W["w_proj_ssm"], tb=True, out_dtype=BF16, name="proj_ssm_dx"))
    d_w_proj_ssm = _matmul(flat(z), d_ysf, ta=True, out_dtype=BF16, name="proj_ssm_dw")

    def f_glu_bwd(yv, dz, usv, dsk, wg, bg):
        ge, t = _gelu_tanh(yv)
        pre = jnp.dot(ge.astype(BF16), wg, preferred_element_type=F32) + bg
        sg = _sigmoid(pre)
        dpre = dz * ge * sg * (1.0 - sg)
        dge = dz * sg + lax.dot_general(dpre.astype(BF16), wg, _NT, preferred_element_type=F32)
        k = math.sqrt(2.0 / math.pi)
        dgelu = 0.5 * (1.0 + t) + 0.5 * yv * (1.0 - t * t) * k * (1.0 + 3.0 * 0.044715 * yv * yv)
        dy = dge * dgelu
        dwg = lax.dot_general(ge.astype(BF16), dpre.astype(BF16), _TN, preferred_element_type=F32)
        return dy, dy * dsk, dwg, _col_sum(dpre), _col_sum(dy * usv)

    d_y_s5, d_us_skip, d_w_glu, d_b_glu, d_d_skip = _rowwise(
        f_glu_bwd, [(y_s5, SSM_WIDTH, 0), (d_z, SSM_WIDTH, 0), (us, SSM_WIDTH, 0)], [],
        [P["d_skip"], W["w_glu"], P["b_glu"]],
        [(SSM_WIDTH, BF16), (SSM_WIDTH, F32)], [], [(SSM_WIDTH, SSM_WIDTH), (1, SSM_WIDTH), (1, SSM_WIDTH)],
        ts=TS, name="s5_glu_bwd")
    d_ysf2 = flat(d_y_s5)
    dxs = unflat(_matmul(d_ysf2, P["cc_big"], tb=True, name="s5_readout_dx"))
    d_cc = _matmul(flat(xs), d_ysf2, ta=True, name="s5_readout_dw")
    lam, g_ab = _scan_bwd(dxs, xs, P["a_row"])
    lam = flat(lam)
    d_us_mm = unflat(_matmul(lam, P["bb_big"], tb=True, name="s5_bu_dx"))
    d_bb = _matmul(flat(us), lam, ta=True, name="s5_bu_dw")

    token = send_early(dict(
        w_out=d_w_out.reshape(N_DEV, D // N_DEV, D), w_proj_att=_cols_to_slots(d_w_proj_att),
        w_proj_ssm=_cols_to_slots(d_w_proj_ssm),
        w_glu=d_w_glu.astype(BF16).reshape(N_DEV, SSM_WIDTH // N_DEV, SSM_WIDTH),
        w_conv=_cols_to_slots(d_w_conv.astype(BF16))))
    d_qkv = _attention_bwd(qkv, o_att, d_o_att, lse, P["slopes"] + token[0, 0])

    def f_add(a, b_):
        return a + b_

    (d_us,) = _rowwise(f_add, [(d_us_mm, SSM_WIDTH, 0), (d_us_skip, SSM_WIDTH, 0)], [], [],
                       [(SSM_WIDTH, BF16)], [], [], ts=TS, name="s5_input_grad")
    d_qkvf = flat(d_qkv)
    d_usf = flat(d_us)
    d_gatesf = flat(d_gates)
    d_w_in = jnp.concatenate(
        [_unpair_qkv_columns(_matmul(u1f, d_qkvf, ta=True, out_dtype=BF16, name="proj_qkv_dw")),
         _matmul(u1f, d_usf, ta=True, out_dtype=BF16, name="proj_ssm_in_dw"),
         _matmul(u1f, d_gatesf, ta=True, out_dtype=BF16, name="proj_gates_dw")], axis=1)
    token = send_early(dict(w_in=_cols_to_slots(d_w_in)))
    d_u1 = (_matmul(d_qkvf, W["w_qkv"], tb=True, out_dtype=BF16, name="proj_qkv_dx"),
            _matmul(d_usf, W["w_us"], tb=True, out_dtype=BF16, name="proj_ssm_in_dx"),
            _matmul(d_gatesf, W["w_gates"], tb=True, out_dtype=BF16, name="proj_gates_dx"))

    def f_modnorm_bwd_in(du0, du1, du2, h, dres, sc, g):
        du = du0 + du1 + du2
        r = _rms_scale(h)
        nh = h * r
        dn = du * (1.0 + sc)
        gy = dn * g
        dh = dres + r * (gy - nh * jnp.mean(gy * nh, axis=-1, keepdims=True))
        return (dh, _col_sum(du), _col_sum(du * nh * g), _col_sum(dn * nh))

    grad_x, d_sh1, d_sc1, d_g_mix = _rowwise(
        f_modnorm_bwd_in, [(unflat(d_u1[0]), D, 0), (unflat(d_u1[1]), D, 0), (unflat(d_u1[2]), D, 0), (x, D, 0),
                           (dh1, D, 0)], [mod_col(1)], [P["g_mix"] + token[0:1, 0:1]],
        [(D, F32)], [D, D], [(1, D)], ts=TS, name="modnorm_mix_bwd")

    d_mod = jnp.concatenate([d_sh1, d_sc1, d_gt1, d_sh2, d_sc2, d_gt2], axis=-1)
    g_ab_re, g_ab_im = _deinterleave(g_ab)
    d_bb_re, d_bb_im = _deinterleave(d_bb)
    d_cc_re, d_cc_im = (t.T for t in _deinterleave(d_cc.T))
    small = dict(g_mix=d_g_mix, b_gate=jnp.concatenate([d_bga, d_bgs], axis=1), g_ab_re=g_ab_re, g_ab_im=g_ab_im,
                 d_bb_re=d_bb_re, d_bb_im=d_bb_im, d_cc_re=d_cc_re, d_cc_im=d_cc_im, d_skip=d_d_skip,
                 b_glu=d_b_glu, g_ffn=d_g_ffn, b_conv=d_b_conv, g_final=d_g_final, loss_cols=loss_cols)
    return grad_x, d_mod, small


def _block_diag_in(bb):
    t = bb.reshape(SSM_GROUPS, SSM_STATE, SSM_GROUP_CH)
    eye = jnp.eye(SSM_GROUPS, dtype=bb.dtype)
    return jnp.einsum("gnc,gh->gchn", t, eye).reshape(SSM_WIDTH, SSM_COLS)


def _block_diag_out(cm):
    eye = jnp.eye(SSM_GROUPS, dtype=cm.dtype)
    return jnp.einsum("gcn,gh->gnhc", cm, eye).reshape(SSM_COLS, SSM_WIDTH)


def _diag_blocks_in(m):
    t = m.reshape(SSM_GROUPS, SSM_GROUP_CH, SSM_GROUPS, SSM_STATE)
    idx = jnp.arange(SSM_GROUPS)
    return t[idx, :, idx, :].transpose(0, 2, 1).reshape(SSM_COLS, SSM_GROUP_CH)


def _diag_blocks_out(m):
    t = m.reshape(SSM_GROUPS, SSM_STATE, SSM_GROUPS, SSM_GROUP_CH)
    idx = jnp.arange(SSM_GROUPS)
    return t[idx, :, idx, :].transpose(0, 2, 1)


def _pair_qkv_columns(w):
    lead = w.shape[:-1]
    return w.reshape(lead + (3, N_HEADS // 2, LANES)).swapaxes(-3, -2).reshape(lead + (3 * ATT_WIDTH,))


def _unpair_qkv_columns(w):
    lead = w.shape[:-1]
    return w.reshape(lead + (N_HEADS // 2, 3, LANES)).swapaxes(-3, -2).reshape(lead + (3 * ATT_WIDTH,))


def _interleave(re, im):
    lead = re.shape[:-1]
    g = lambda a: a.reshape(lead + (SSM_COLS // SCAN_COLS, 1, SCAN_COLS))
    return jnp.concatenate([g(re), g(im)], axis=-2).reshape(lead + (2 * SSM_COLS,))


def _deinterleave(x):
    lead = x.shape[:-1]
    t = x.reshape(lead + (SSM_COLS // SCAN_COLS, 2, SCAN_COLS))
    return t[..., 0, :].reshape(lead + (SSM_COLS,)), t[..., 1, :].reshape(lead + (SSM_COLS,))


def _cols_to_slots(g):
    R = g.shape[0]
    return g.reshape(R, N_DEV, g.shape[1] // N_DEV).transpose(1, 0, 2)


def _slots_to_cols(g):
    return g.transpose(1, 0, 2).reshape(g.shape[1], N_DEV * g.shape[2])


SMALL_ORDER = ("b_ada", "g_mix", "b_gate", "a_re", "a_im", "log_dt", "b_re", "b_im", "c_re", "c_im", "d_skip",
               "b_glu", "g_ffn", "b_conv", "g_final")


def _pack(arrs):
    pieces, offs, row = [], [], 0
    for a in arrs:
        f = a.reshape(-1).astype(F32)
        n = f.shape[0]
        rows = -(-n // LANES)
        pieces.append(jnp.pad(f, (0, rows * LANES - n)))
        offs.append((row, n))
        row += rows
    return jnp.concatenate(pieces).reshape(row, LANES), offs


def _unpack(packed, offs, shapes):
    flat = packed.reshape(-1)
    return [flat[r * LANES:r * LANES + n].reshape(s) for (r, n), s in zip(offs, shapes)]


def kernel(x, c, w_ada, b_ada, g_mix, w_in, b_gate, a_re, a_im, log_dt, b_re, b_im, c_re, c_im, d_skip, w_glu, b_glu, w_proj_att, w_proj_ssm, w_out, g_ffn, w_up, w_conv, b_conv, w_down, g_final, loss_target, m_w_ada, m_b_ada, m_g_mix, m_w_in, m_b_gate, m_a_re, m_a_im, m_log_dt, m_b_re, m_b_im, m_c_re, m_c_im, m_d_skip, m_w_glu, m_b_glu, m_w_proj_att, m_w_proj_ssm, m_w_out, m_g_ffn, m_w_up, m_w_conv, m_b_conv, m_w_down, m_g_final, v_w_ada, v_b_ada, v_g_mix, v_w_in, v_b_gate, v_a_re, v_a_im, v_log_dt, v_b_re, v_b_im, v_c_re, v_c_im, v_d_skip, v_w_glu, v_b_glu, v_w_proj_att, v_w_proj_ssm, v_w_out, v_g_ffn, v_w_up, v_w_conv, v_b_conv, v_w_down, v_g_final):
    args = dict(locals())
    B, S, D = x.shape
    me = 4 * lax.axis_index("x") + 2 * lax.axis_index("y") + lax.axis_index("c")
    bf = lambda w: w[0].astype(BF16)

    first = [c, bf(w_in)]
    later = [bf(w_glu), bf(w_proj_att), bf(w_proj_ssm), bf(w_out), bf(w_up), w_conv[0], bf(w_down)]
    first_sems = _exchange_start(first, "start_weights_0", gather=True)
    later_sems = _exchange_start(later, "start_weights_1", gather=True)

    def gathered_slots(handles, own, after, name):
        _, lands = _exchange_wait(*handles[:4], after, name=name)
        return [lax.dynamic_update_index_in_dim(land, a, me, 0) for land, a in zip(lands, own)]

    c_slots, w_in_slots = gathered_slots(first_sems, first, later_sems[4], "wait_weights_0")
    c_all = c_slots.reshape(N_DEV * B, D)
    w_in_full = _slots_to_cols(w_in_slots)
    n_qkv = 3 * ATT_WIDTH
    W = dict(w_qkv=_pair_qkv_columns(w_in_full[:, :n_qkv]), w_us=w_in_full[:, n_qkv:n_qkv + SSM_WIDTH],
             w_gates=w_in_full[:, n_qkv + SSM_WIDTH:])

    def late_weights(after):
        g = gathered_slots(later_sems, later, after, "wait_weights_1")
        more_w = dict(w_glu=g[0].reshape(SSM_WIDTH, SSM_WIDTH), w_proj_att=_slots_to_cols(g[1]),
                      w_proj_ssm=_slots_to_cols(g[2]), w_out=g[3].reshape(D, D), w_up=g[4],
                      w_down=g[6].reshape(D_FF, D))
        return more_w, dict(w_conv=_slots_to_cols(g[5]))

    n_ada = w_ada.shape[2]
    b_ada_cols = lax.dynamic_slice(b_ada, (0, me * n_ada), (1, n_ada))
    mod_part = _ada_fwd(c_all, w_ada[0], b_ada_cols)
    (mod_slots,) = _exchange([(mod_part.reshape(N_DEV, B, n_ada), True)], name="scatter_modulation")
    mod = mod_slots.transpose(1, 0, 2).reshape(B, 1, 6 * D)

    ab_re, ab_im, f_re, f_im = _s5_params(a_re[0], a_im[0], log_dt[0].reshape(SSM_GROUPS, 1))
    col = lambda a: a.reshape(SSM_COLS, 1)
    b_re2, b_im2 = b_re[0].reshape(SSM_COLS, SSM_GROUP_CH), b_im[0].reshape(SSM_COLS, SSM_GROUP_CH)
    bb_re, bb_im = _s5_input_matrix(col(f_re), col(f_im), b_re2, b_im2)
    slopes = jnp.asarray([2.0 ** (-8.0 * (h + 1) / N_HEADS) for h in range(N_HEADS)], F32)
    P = dict(g_mix=g_mix, g_ffn=g_ffn, g_final=g_final.reshape(1, D), b_gate=b_gate, d_skip=d_skip, b_glu=b_glu,
             b_conv=b_conv, slopes=slopes,
             a_row=_interleave(ab_re.reshape(1, SSM_COLS), ab_im.reshape(1, SSM_COLS)),
             bb_big=_interleave(_block_diag_in(bb_re), _block_diag_in(bb_im)),
             cc_big=_interleave(_block_diag_out(c_re[0]).T, -_block_diag_out(c_im[0]).T).T)

    in_flight = []

    def send_early(grads):
        names = list(grads)
        handles = _exchange_start([grads[n] for n in names], "start_gradients_%d" % len(in_flight), gather=False)
        in_flight.append((names,) + handles[:4])
        return handles[4]

    grad_x, d_mod, small = _local_step(x, mod, loss_target, W, late_weights, P, send_early)

    small_list = [small["loss_cols"], small["g_mix"], small["b_gate"], small["g_ab_re"], small["g_ab_im"],
                  _diag_blocks_in(small["d_bb_re"]), _diag_blocks_in(small["d_bb_im"]),
                  _diag_blocks_out(small["d_cc_re"]), -_diag_blocks_out(small["d_cc_im"]),
                  small["g_ffn"], small["b_conv"], small["g_final"], small["d_skip"], small["b_glu"]]
    small_packed, small_offs = _pack(small_list)
    small_all, dmod_slots = _gather_all([small_packed, d_mod.reshape(B, 6 * D)], name="gather_small_gradients")

    out = {}

    def update(name, parts, own=None):
        w2 = args[name][0]
        g, dl, mn, vn = _adamw(w2, args["m_" + name][0], args["v_" + name][0], parts, name="adamw_" + name, own=own)
        for key, val in (("grad_", g), ("delta_", dl), ("new_m_", mn), ("new_v_", vn)):
            out[key + name] = val[None]

    my_slot = me.astype(jnp.int32).reshape(1)
    for i, (names, send_sems, recv_sems, sent, lands) in enumerate(in_flight):
        sent, lands = _exchange_wait(send_sems, recv_sems, sent, lands, dmod_slots, name="wait_gradients_%d" % i)
        for name, own_slots, landed in zip(names, sent, lands):
            update(name, landed, own=(own_slots, my_slot))

    dmod_all = dmod_slots.reshape(N_DEV * B, 6 * D)
    dmod_cols = lax.dynamic_slice(dmod_all, (0, me * n_ada), (N_DEV * B, n_ada))
    d_w_ada, d_b_ada = _ada_bwd(c_all, dmod_all, dmod_cols)
    update("w_ada", d_w_ada[None])

    loss_row, loss_n = small_offs[0]
    small_sum, loss_vec = _sum_parts(small_all, (loss_row, loss_row + loss_n // LANES))
    shapes = [(1, D), (1, D), (1, 2 * D), (SSM_GROUPS, SSM_STATE), (SSM_GROUPS, SSM_STATE), (SSM_COLS, SSM_GROUP_CH),
              (SSM_COLS, SSM_GROUP_CH), (1, SSM_GROUPS, SSM_GROUP_CH, SSM_STATE),
              (1, SSM_GROUPS, SSM_GROUP_CH, SSM_STATE), (1, D), (1, D_FF), (D,), (1, SSM_WIDTH), (1, SSM_WIDTH)]
    (_, s_g_mix, s_b_gate, s_ab_re, s_ab_im, s_bb_re, s_bb_im, s_c_re, s_c_im, s_g_ffn, s_b_conv, s_g_final,
     s_d_skip, s_b_glu) = _unpack(small_sum, small_offs, shapes)
    d_b_re2, d_b_im2, d_f_re, d_f_im = _s5_input_matrix_bwd(col(f_re), col(f_im), b_re2, b_im2, s_bb_re, s_bb_im)
    d_a_re, d_a_im, d_log_dt = _s5_params_bwd(a_re[0], a_im[0], log_dt[0].reshape(SSM_GROUPS, 1), s_ab_re, s_ab_im,
                                              d_f_re.reshape(SSM_GROUPS, SSM_STATE),
                                              d_f_im.reshape(SSM_GROUPS, SSM_STATE))
    grads_small = dict(b_ada=d_b_ada, g_mix=s_g_mix, b_gate=s_b_gate, a_re=d_a_re[None], a_im=d_a_im[None],
                       log_dt=d_log_dt.reshape(1, SSM_GROUPS), b_re=d_b_re2.reshape(b_re.shape),
                       b_im=d_b_im2.reshape(b_im.shape), c_re=s_c_re, c_im=s_c_im, d_skip=s_d_skip, b_glu=s_b_glu,
                       g_ffn=s_g_ffn, b_conv=s_b_conv, g_final=s_g_final)
    w_pack, offs = _pack([args[n] for n in SMALL_ORDER])
    m_pack, _ = _pack([args["m_" + n] for n in SMALL_ORDER])
    v_pack, _ = _pack([args["v_" + n] for n in SMALL_ORDER])
    g_pack, _ = _pack([grads_small[n] for n in SMALL_ORDER])
    res = _adamw(w_pack, m_pack, v_pack, g_pack[None], name="adamw_small")
    shapes_small = [args[n].shape for n in SMALL_ORDER]
    for key, packed in zip(("grad_", "delta_", "new_m_", "new_v_"), res):
        for n, val in zip(SMALL_ORDER, _unpack(packed, offs, shapes_small)):
            out[key + n] = val

    order = ["w_ada", "b_ada", "g_mix", "w_in", "b_gate", "a_re", "a_im", "log_dt", "b_re", "b_im", "c_re", "c_im",
             "d_skip", "w_glu", "b_glu", "w_proj_att", "w_proj_ssm", "w_out", "g_ffn", "w_up", "w_conv", "b_conv",
             "w_down", "g_final"]
    loss = loss_vec[0, 0]
    return (loss, grad_x, *[out[k + n] for k in ("grad_", "delta_", "new_m_", "new_v_") for n in order])
```

```python
import math

import jax
import jax.numpy as jnp
from jax import lax
from jax.experimental import pallas as pl
from jax.experimental.pallas import tpu as pltpu

F32 = jnp.float32
BF16 = jnp.bfloat16

N_DEV = 8
D_MODEL = 1024
N_HEADS = 8
HEAD_DIM = 64
ATT_WIDTH = N_HEADS * HEAD_DIM
DILATIONS = (1, 4, 16)
WIN = 128
SSM_GROUPS = 16
SSM_GROUP_CH = 16
SSM_WIDTH = SSM_GROUPS * SSM_GROUP_CH
SSM_STATE = 64
SSM_COLS = SSM_GROUPS * SSM_STATE
D_FF = 2048
EPS = 1e-6
NEG_INF = -1e30
ADAM_LR, ADAM_B1, ADAM_B2, ADAM_EPS, ADAM_WD, ADAM_STEP = 0.001, 0.9, 0.999, 1e-08, 0.01, 10

V7X_VMEM_LIMIT = 56 * 1024 * 1024
LANES = 128


def _params(n_grid):
    return pltpu.CompilerParams(dimension_semantics=("arbitrary",) * n_grid,
                                vmem_limit_bytes=V7X_VMEM_LIMIT)


def _tile(n, pref):
    if n <= pref:
        return n
    t = (pref // LANES) * LANES
    while t > 0:
        if n % t == 0:
            return t
        t -= LANES
    return n


def _matmul(a, b, *, ta=False, tb=False, out_dtype=F32, name):
    if ta:
        K, M = a.shape
    else:
        M, K = a.shape
    if tb:
        N, K2 = b.shape
    else:
        K2, N = b.shape
    assert K == K2, (a.shape, b.shape)
    if ta:
        tm, tn, tk = _tile(M, 1024), _tile(N, 2048), _tile(K, 512)
    else:
        tm, tk = _tile(M, 512), _tile(K, 4096)
        tn = _tile(N, 2048 if K <= 2048 else 1024)
    nk = K // tk
    dn = (((0,) if ta else (1,), (1,) if tb else (0,)), ((), ()))

    def body(a_ref, b_ref, o_ref, acc_ref):
        k = pl.program_id(2)
        part = lax.dot_general(a_ref[...].astype(BF16), b_ref[...].astype(BF16), dn, preferred_element_type=F32)
        if nk == 1:
            o_ref[...] = part.astype(o_ref.dtype)
            return

        @pl.when(k == 0)
        def _():
            acc_ref[...] = jnp.zeros_like(acc_ref)

        acc_ref[...] += part

        @pl.when(k == nk - 1)
        def _():
            o_ref[...] = acc_ref[...].astype(o_ref.dtype)

    a_spec = (pl.BlockSpec((tk, tm), lambda j, i, k: (k, i)) if ta
              else pl.BlockSpec((tm, tk), lambda j, i, k: (i, k)))
    b_spec = (pl.BlockSpec((tn, tk), lambda j, i, k: (j, k)) if tb
              else pl.BlockSpec((tk, tn), lambda j, i, k: (k, j)))
    return pl.pallas_call(
        body, name=name, grid=(N // tn, M // tm, nk),
        in_specs=[a_spec, b_spec],
        out_specs=pl.BlockSpec((tm, tn), lambda j, i, k: (i, j)),
        out_shape=jax.ShapeDtypeStruct((M, N), out_dtype),
        scratch_shapes=[pltpu.VMEM((tm, tn) if nk > 1 else (8, LANES), F32)],
        compiler_params=_params(3),
    )(a, b)


HALF = 256
UP_SLOTS = N_DEV // 2
UP_GROUP = 4 * HALF


def _group_weight(w_ref):
    return jnp.concatenate([w_ref[0, :, :HALF], w_ref[1, :, :HALF], w_ref[0, :, HALF:], w_ref[1, :, HALF:]], axis=1)


def _up_weight_spec(K, index):
    return pl.BlockSpec((2, None, K, 2 * HALF), index)


def _up_fwd(a, w3, name):
    M, K = a.shape
    tm = _tile(M, 1024)

    def body(a_ref, w_ref, o_ref):
        o_ref[...] = jnp.dot(a_ref[...].astype(BF16), _group_weight(w_ref),
                             preferred_element_type=F32).astype(o_ref.dtype)

    return pl.pallas_call(
        body, name=name, grid=(UP_SLOTS, M // tm),
        in_specs=[pl.BlockSpec((tm, K), lambda j, i: (i, 0)), _up_weight_spec(K, lambda j, i: (0, j, 0, 0))],
        out_specs=pl.BlockSpec((tm, UP_GROUP), lambda j, i: (i, j)),
        out_shape=jax.ShapeDtypeStruct((M, UP_SLOTS * UP_GROUP), BF16), compiler_params=_params(2),
    )(a, w3.reshape(2, UP_SLOTS, K, 2 * HALF))


def _up_dx(d, w3, name):
    M = d.shape[0]
    K = w3.shape[1]
    tm = _tile(M, 1024)

    def body(d_ref, w_ref, o_ref, acc_ref):
        j = pl.program_id(1)

        @pl.when(j == 0)
        def _():
            acc_ref[...] = jnp.zeros_like(acc_ref)

        acc_ref[...] += lax.dot_general(d_ref[...], _group_weight(w_ref), _NT, preferred_element_type=F32)

        @pl.when(j == UP_SLOTS - 1)
        def _():
            o_ref[...] = acc_ref[...].astype(o_ref.dtype)

    return pl.pallas_call(
        body, name=name, grid=(M // tm, UP_SLOTS),
        in_specs=[pl.BlockSpec((tm, UP_GROUP), lambda i, j: (i, j)), _up_weight_spec(K, lambda i, j: (0, j, 0, 0))],
        out_specs=pl.BlockSpec((tm, K), lambda i, j: (i, 0)),
        out_shape=jax.ShapeDtypeStruct((M, K), BF16), scratch_shapes=[pltpu.VMEM((tm, K), F32)],
        compiler_params=_params(2),
    )(d, w3.reshape(2, UP_SLOTS, K, 2 * HALF))


def _up_dw(a, d, name):
    M, K = a.shape
    tk = _tile(M, 512)
    nk = M // tk

    def body(a_ref, d_ref, o_ref, acc_ref):
        k = pl.program_id(1)

        @pl.when(k == 0)
        def _():
            acc_ref[...] = jnp.zeros_like(acc_ref)

        acc_ref[...] += lax.dot_general(a_ref[...], d_ref[...], _TN, preferred_element_type=F32)

        @pl.when(k == nk - 1)
        def _():
            for half in range(2):
                for part in range(2):
                    lo = (2 * half + part) * HALF
                    o_ref[part, :, half * HALF:(half + 1) * HALF] = acc_ref[:, lo:lo + HALF].astype(o_ref.dtype)

    out = pl.pallas_call(
        body, name=name, grid=(UP_SLOTS, nk),
        in_specs=[pl.BlockSpec((tk, K), lambda j, k: (k, 0)), pl.BlockSpec((tk, UP_GROUP), lambda j, k: (k, j))],
        out_specs=_up_weight_spec(K, lambda j, k: (0, j, 0, 0)),
        out_shape=jax.ShapeDtypeStruct((2, UP_SLOTS, K, 2 * HALF), BF16),
        scratch_shapes=[pltpu.VMEM((K, UP_GROUP), F32)], compiler_params=_params(2),
    )(a, d)
    return out.reshape(N_DEV, K, 2 * HALF)


def _rowwise(fn, rows, bvecs, consts, out_rows, out_b, out_g, *, ts, name):
    B, S = rows[0][0].shape[:2]
    nin = len(rows) + len(bvecs) + len(consts)
    nr, nb, ng = len(out_rows), len(out_b), len(out_g)

    def body(*refs):
        b = pl.program_id(0)
        s = pl.program_id(1)
        vals = [r[...] for r in refs[:nin]]
        vals[:len(rows)] = [v.astype(F32) for v in vals[:len(rows)]]
        outs = fn(*vals)
        if not isinstance(outs, (tuple, list)):
            outs = (outs,)
        orefs = refs[nin:]
        for i in range(nr):
            orefs[i][...] = outs[i].astype(orefs[i].dtype)
        for i in range(nb):
            ref = orefs[nr + i]

            @pl.when(s == 0)
            def _(ref=ref):
                ref[...] = jnp.zeros_like(ref)

            ref[...] += outs[nr + i]
        for i in range(ng):
            ref = orefs[nr + nb + i]

            @pl.when((s == 0) & (b == 0))
            def _(ref=ref):
                ref[...] = jnp.zeros_like(ref)

            ref[...] += outs[nr + nb + i]

    in_specs = ([pl.BlockSpec((None, ts, cb), lambda b, s, ci=ci: (b, s, ci)) for (_, cb, ci) in rows]
                + [pl.BlockSpec((None, 1, cb), lambda b, s, ci=ci: (b, 0, ci)) for (_, cb, ci) in bvecs]
                + [pl.BlockSpec(a.shape, lambda b, s: (0, 0)) for a in consts])
    out_shape = ([jax.ShapeDtypeStruct((B, S, c), dt) for (c, dt) in out_rows]
                 + [jax.ShapeDtypeStruct((B, 1, c), F32) for c in out_b]
                 + [jax.ShapeDtypeStruct(rc, F32) for rc in out_g])
    out_specs = ([pl.BlockSpec((None, ts, c), lambda b, s: (b, s, 0)) for (c, _) in out_rows]
                 + [pl.BlockSpec((None, 1, c), lambda b, s: (b, 0, 0)) for c in out_b]
                 + [pl.BlockSpec(rc, lambda b, s: (0, 0)) for rc in out_g])
    args = [a for (a, _, _) in rows] + [a for (a, _, _) in bvecs] + list(consts)
    return pl.pallas_call(
        body, name=name, grid=(B, S // ts), in_specs=in_specs, out_specs=out_specs,
        out_shape=out_shape, compiler_params=_params(2),
    )(*args)


def _col_sum(v):
    return jnp.sum(v, axis=0, keepdims=True)


def _rms_scale(h):
    return lax.rsqrt(jnp.mean(h * h, axis=-1, keepdims=True) + EPS)


def _sigmoid(v):
    return 1.0 / (1.0 + jnp.exp(-v))


ATT_SCALE = HEAD_DIM ** -0.5
COPY_ROWS = 256
_NT = (((1,), (1,)), ((), ()))
_TN = (((0,), (0,)), ((), ()))


def _row_chunks(d, seq):
    sub = seq // d
    out = []
    for r in range(d):
        for c0 in range(0, sub, COPY_ROWS):
            n = min(COPY_ROWS, sub - c0)
            out.append((pl.ds(r + c0 * d, n, stride=d), r * sub + c0, n))
    return out


ATT_UNROLL = 8
KEYS = 2 * WIN


def _zero_once(refs):
    @pl.when((pl.program_id(0) == 0) & (pl.program_id(1) == 0))
    def _():
        for r in refs:
            r[...] = jnp.zeros_like(r)


def _pair_bias(bias_ref, slopes_ref, hp, d, key_major):
    shape = (KEYS, WIN) if key_major else (WIN, KEYS)
    qi = lax.broadcasted_iota(jnp.int32, shape, 1 if key_major else 0)
    kj = lax.broadcasted_iota(jnp.int32, shape, 0 if key_major else 1)
    dist = WIN + qi - kj
    valid = (dist >= 0) & (dist <= WIN)
    distf = dist.astype(F32)
    for h in range(2):
        slope_d = slopes_ref[2 * hp + h] * float(d)
        with_prev = jnp.where(valid, -(slope_d * distf), NEG_INF)
        no_prev = jnp.where(kj >= WIN, with_prev, NEG_INF)
        span = slice(h * KEYS, (h + 1) * KEYS)
        if key_major:
            bias_ref[1, span, :] = with_prev
            bias_ref[0, span, :] = no_prev
        else:
            bias_ref[1, :, span] = with_prev
            bias_ref[0, :, span] = no_prev


def _stack_heads(v):
    first = lax.broadcasted_iota(jnp.int32, v.shape, 1) < HEAD_DIM
    zero = jnp.zeros_like(v)
    return jnp.concatenate([jnp.where(first, v, zero), jnp.where(first, zero, v)], axis=0)


def _per_head(c0, c1, n):
    return jnp.where(lax.broadcasted_iota(jnp.int32, (n, LANES), 1) < HEAD_DIM, c0, c1)


def _qkv_spec(seq, j):
    return pl.BlockSpec((None, seq, LANES), lambda b, hp: (b, 0, 3 * hp + j))


def _attention_fwd(qkv, slopes):
    B, S, _ = qkv.shape
    n_blk = S // WIN
    n_pair = N_HEADS // 2

    def body(slopes_ref, q_ref, k_ref, v_ref, o_ref, lse_ref, qp, kp, vp, bias, acc, mx, sm, acc_n, mx_n, sm_n):
        hp = pl.program_id(1)
        _zero_once((kp, vp))
        for p, d in enumerate(DILATIONS):
            nb = n_blk // d
            chunks = _row_chunks(d, S)
            for src, dst, n in chunks:
                qp[dst:dst + n, :] = (q_ref[src, :] * ATT_SCALE).astype(BF16)
                kp[WIN + dst:WIN + dst + n, :] = k_ref[src, :].astype(BF16)
                vp[WIN + dst:WIN + dst + n, :] = v_ref[src, :].astype(BF16)
            _pair_bias(bias, slopes_ref, hp, d, key_major=False)
            acc_t, mx_t, sm_t = (acc_n, mx_n, sm_n) if d == 1 else (acc, mx, sm)

            def block(i, carry, p=p, nb=nb, acc_t=acc_t, mx_t=mx_t, sm_t=sm_t):
                cur = pl.ds(pl.multiple_of(i * WIN, WIN), WIN)
                keys = pl.ds(pl.multiple_of(i * WIN, WIN), KEYS)
                flag = ((i % nb) > 0).astype(jnp.int32)
                s = lax.dot_general(qp[cur, :], _stack_heads(kp[keys, :]), _NT, preferred_element_type=F32)
                s = s + bias[flag]
                es, ms, ls = [], [], []
                for h in range(2):
                    sh = s[:, h * KEYS:(h + 1) * KEYS]
                    m = jnp.max(jnp.maximum(sh[:, :WIN], sh[:, WIN:]), axis=1, keepdims=True)
                    e = jnp.exp(sh - m)
                    es.append(e.astype(BF16))
                    ms.append(m)
                    ls.append(jnp.sum(e[:, :WIN] + e[:, WIN:], axis=1, keepdims=True))
                acc_t[p, cur, :] = jnp.dot(jnp.concatenate(es, axis=1), _stack_heads(vp[keys, :]),
                                           preferred_element_type=F32)
                mx_t[p, cur, :] = _per_head(ms[0], ms[1], WIN)
                sm_t[p, cur, :] = _per_head(ls[0], ls[1], WIN)
                return carry

            lax.fori_loop(0, n_blk, block, 0, unroll=ATT_UNROLL)
            if d > 1:
                for src, dst, n in chunks:
                    acc_n[p, src, :] = acc[p, dst:dst + n, :]
                    mx_n[p, src, :] = mx[p, dst:dst + n, :]
                    sm_n[p, src, :] = sm[p, dst:dst + n, :]

        chunk = 256

        def merge(i, carry):
            rows = pl.ds(pl.multiple_of(i * chunk, chunk), chunk)
            ms = [mx_n[p, rows, :] for p in range(3)]
            m = jnp.maximum(jnp.maximum(ms[0], ms[1]), ms[2])
            ws = [jnp.exp(mp - m) for mp in ms]
            l = ws[0] * sm_n[0, rows, :] + ws[1] * sm_n[1, rows, :] + ws[2] * sm_n[2, rows, :]
            o = (ws[0] * acc_n[0, rows, :] + ws[1] * acc_n[1, rows, :] + ws[2] * acc_n[2, rows, :]) / l
            o_ref[rows, :] = o.astype(o_ref.dtype)
            lse = m + jnp.log(l)
            for h in range(2):
                lse_ref[rows, h:h + 1] = lse[:, h * HEAD_DIM:h * HEAD_DIM + 1]
            return carry

        lax.fori_loop(0, S // chunk, merge, 0)

    return pl.pallas_call(
        body, name="attention_fwd", grid=(B, n_pair),
        in_specs=[pl.BlockSpec(memory_space=pltpu.SMEM), _qkv_spec(S, 0), _qkv_spec(S, 1), _qkv_spec(S, 2)],
        out_specs=[pl.BlockSpec((None, S, LANES), lambda b, hp: (b, 0, hp)),
                   pl.BlockSpec((None, None, S, 2), lambda b, hp: (b, hp, 0, 0))],
        out_shape=[jax.ShapeDtypeStruct((B, S, ATT_WIDTH), BF16),
                   jax.ShapeDtypeStruct((B, n_pair, S, 2), F32)],
        scratch_shapes=[pltpu.VMEM((S, LANES), BF16), pltpu.VMEM((S + WIN, LANES), BF16),
                        pltpu.VMEM((S + WIN, LANES), BF16), pltpu.VMEM((2, WIN, 2 * KEYS), F32)]
        + [pltpu.VMEM((3, S, LANES), F32)] * 6,
        compiler_params=_params(2),
    )(slopes, qkv, qkv, qkv)


def _attention_bwd(qkv, o, do, lse, slopes):
    B, S, _ = qkv.shape
    n_blk = S // WIN
    n_pair = N_HEADS // 2

    def body(slopes_ref, q_ref, k_ref, v_ref, o_ref, do_ref, lse_ref, dx_ref,
             qp, dop, kp, vp, aux, auxp, aux_t, bias_t, dqp, dvk, dq_n, dk_n, dv_n):
        hp = pl.program_id(1)
        aux[...] = jnp.zeros_like(aux)
        for c0 in range(0, S, COPY_ROWS):
            rows = slice(c0, c0 + COPY_ROWS)
            prod = do_ref[rows, :] * o_ref[rows, :].astype(F32)
            for h in range(2):
                aux[rows, 2 * h:2 * h + 1] = lse_ref[rows, h:h + 1]
                aux[rows, 2 * h + 1:2 * h + 2] = jnp.sum(prod[:, h * HEAD_DIM:(h + 1) * HEAD_DIM], axis=1,
                                                         keepdims=True)
        dq_n[...] = jnp.zeros_like(dq_n)
        dk_n[...] = jnp.zeros_like(dk_n)
        dv_n[...] = jnp.zeros_like(dv_n)
        _zero_once((kp, vp))
        for p, d in enumerate(DILATIONS):
            nb = n_blk // d
            chunks = _row_chunks(d, S)
            for src, dst, n in chunks:
                auxp[dst:dst + n, :] = aux[src, :]
                qp[dst:dst + n, :] = (q_ref[src, :] * ATT_SCALE).astype(BF16)
                dop[dst:dst + n, :] = do_ref[src, :].astype(BF16)
                kp[WIN + dst:WIN + dst + n, :] = k_ref[src, :].astype(BF16)
                vp[WIN + dst:WIN + dst + n, :] = v_ref[src, :].astype(BF16)
            for i in range(n_blk):
                aux_t[i] = auxp[i * WIN:(i + 1) * WIN, :].T[0:8, :]
            _pair_bias(bias_t, slopes_ref, hp, d, key_major=True)
            dvk[...] = jnp.zeros_like(dvk)

            def block(i, carry, nb=nb):
                cur = pl.ds(pl.multiple_of(i * WIN, WIN), WIN)
                keys = pl.ds(pl.multiple_of(i * WIN, WIN), KEYS)
                flag = ((i % nb) > 0).astype(jnp.int32)
                q2, do2 = qp[cur, :], dop[cur, :]
                kc = _stack_heads(kp[keys, :])
                s_t = lax.dot_general(kc, q2, _NT, preferred_element_type=F32) + bias_t[flag]
                dp_t = lax.dot_general(_stack_heads(vp[keys, :]), do2, _NT, preferred_element_type=F32)
                ps, dss = [], []
                for h in range(2):
                    span = slice(h * KEYS, (h + 1) * KEYS)
                    p_t = jnp.exp(s_t[span, :] - aux_t[i, 2 * h:2 * h + 1, :])
                    ds_t = p_t * (dp_t[span, :] - aux_t[i, 2 * h + 1:2 * h + 2, :])
                    ps.append(p_t.astype(BF16))
                    dss.append(ds_t.astype(BF16))
                do_rows, q_rows = _stack_heads(do2), _stack_heads(q2)
                zr = jnp.zeros_like(do_rows)
                rhs = jnp.concatenate([jnp.concatenate([do_rows, zr], axis=1),
                                       jnp.concatenate([zr, q_rows], axis=1)], axis=0)
                dvk[keys, :] += jnp.dot(jnp.concatenate(ps + dss, axis=1), rhs, preferred_element_type=F32)
                dqp[cur, :] = lax.dot_general(jnp.concatenate(dss, axis=0), kc, _TN, preferred_element_type=F32)
                return carry

            lax.fori_loop(0, n_blk, block, 0, unroll=ATT_UNROLL)
            for src, dst, n in chunks:
                dq_n[src, :] += dqp[dst:dst + n, :]
                dv_n[src, :] += dvk[WIN + dst:WIN + dst + n, :LANES]
                dk_n[src, :] += dvk[WIN + dst:WIN + dst + n, LANES:]
        for c0 in range(0, S, COPY_ROWS):
            rows = slice(c0, c0 + COPY_ROWS)
            dx_ref[rows, 0:LANES] = (dq_n[rows, :] * ATT_SCALE).astype(dx_ref.dtype)
            dx_ref[rows, LANES:2 * LANES] = dk_n[rows, :].astype(dx_ref.dtype)
            dx_ref[rows, 2 * LANES:3 * LANES] = dv_n[rows, :].astype(dx_ref.dtype)

    pair = lambda width: pl.BlockSpec((None, S, width), lambda b, hp: (b, 0, hp))
    vm = lambda shape, dt: pltpu.VMEM(shape, dt)
    return pl.pallas_call(
        body, name="attention_bwd", grid=(B, n_pair),
        in_specs=[pl.BlockSpec(memory_space=pltpu.SMEM), _qkv_spec(S, 0), _qkv_spec(S, 1), _qkv_spec(S, 2),
                  pair(LANES), pair(LANES), pl.BlockSpec((None, None, S, 2), lambda b, hp: (b, hp, 0, 0))],
        out_specs=pair(3 * LANES),
        out_shape=jax.ShapeDtypeStruct((B, S, 3 * ATT_WIDTH), BF16),
        scratch_shapes=[vm((S, LANES), BF16), vm((S, LANES), BF16),
                        vm((S + WIN, LANES), BF16), vm((S + WIN, LANES), BF16),
                        vm((S, LANES), F32), vm((S, LANES), F32), vm((n_blk, 8, WIN), F32),
                        vm((2, 2 * KEYS, WIN), F32),
                        vm((S, LANES), F32), vm((S + WIN, 2 * LANES), F32),
                        vm((S, LANES), F32), vm((S, LANES), F32), vm((S, LANES), F32)],
        compiler_params=_params(2),
    )(slopes, qkv, qkv, qkv, o, do, lse)


SCAN_COLS = 256
SCAN_ROWS = 8


def _rows_to_tile(rows):
    rid = lax.broadcasted_iota(jnp.int32, (SCAN_ROWS, rows[0].shape[1]), 0)
    tile = jnp.broadcast_to(rows[0], rid.shape)
    for k in range(1, SCAN_ROWS):
        tile = jnp.where(rid == k, rows[k], tile)
    return tile


SCAN_UNROLL = 4


def _complex_powers(ar, ai, n):
    out = [(ar, ai)]
    for _ in range(n - 1):
        pr, pi = out[-1]
        out.append((pr * ar - pi * ai, pr * ai + pi * ar))
    return out


def _round_multipliers(powers, rid, reverse):
    out = []
    for s in (1, 2, 4):
        keep = (rid < SCAN_ROWS - s) if reverse else (rid >= s)
        out.append((jnp.where(keep, powers[s - 1][0], 0.0), jnp.where(keep, powers[s - 1][1], 0.0)))
    return out


def _tile_scan(xr, xi, multipliers, reverse):
    for s, (mr, mi) in zip((1, 2, 4), multipliers):
        shift = SCAN_ROWS - s if reverse else s
        sr, si = pltpu.roll(xr, shift, 0), pltpu.roll(xi, shift, 0)
        xr, xi = xr + (mr * sr - mi * si), xi + (mr * si + mi * sr)
    return xr, xi


def _scan_fwd(bu, a_row):
    B, S, _ = bu.shape
    groups = 2
    nc = SSM_COLS // (groups * SCAN_COLS)
    nt = S // SCAN_ROWS
    LAST = slice(SCAN_ROWS - 1, SCAN_ROWS)

    def body(bu_ref, a_ref, xs_ref):
        rid = lax.broadcasted_iota(jnp.int32, (SCAN_ROWS, SCAN_COLS), 0)
        consts = []
        for g in range(groups):
            re = slice(2 * g * SCAN_COLS, (2 * g + 1) * SCAN_COLS)
            im = slice((2 * g + 1) * SCAN_COLS, (2 * g + 2) * SCAN_COLS)
            powers = _complex_powers(a_ref[:, re], a_ref[:, im], SCAN_ROWS)
            carry_mult = (_rows_to_tile([p[0] for p in powers]), _rows_to_tile([p[1] for p in powers]))
            consts.append((re, im, carry_mult, _round_multipliers(powers, rid, reverse=False)))

        def tile(i, carry):
            out = []
            for (re, im, (cr_t, ci_t), rounds), (cr, ci) in zip(consts, carry):
                xr, xi = _tile_scan(bu_ref[i, :, re], bu_ref[i, :, im], rounds, reverse=False)
                xs_ref[i, :, re] = xr + (cr_t * cr - ci_t * ci)
                xs_ref[i, :, im] = xi + (cr_t * ci + ci_t * cr)
                out.append((xs_ref[i, LAST, re], xs_ref[i, LAST, im]))
            return tuple(out)

        zero = jnp.zeros((1, SCAN_COLS), F32)
        lax.fori_loop(0, nt, tile, ((zero, zero),) * groups, unroll=SCAN_UNROLL)

    col = pl.BlockSpec((None, nt, SCAN_ROWS, 2 * groups * SCAN_COLS), lambda b, j: (b, 0, 0, j))
    xs = pl.pallas_call(
        body, name="s5_scan_fwd", grid=(B, nc),
        in_specs=[col, pl.BlockSpec((1, 2 * groups * SCAN_COLS), lambda b, j: (0, j))],
        out_specs=col,
        out_shape=jax.ShapeDtypeStruct((B, nt, SCAN_ROWS, 2 * SSM_COLS), F32),
        compiler_params=_params(2),
    )(bu.reshape(B, nt, SCAN_ROWS, 2 * SSM_COLS), a_row)
    return xs.reshape(B, S, 2 * SSM_COLS)


def _scan_bwd(dxs, xs, a_row):
    B, S, _ = dxs.shape
    nc = SSM_COLS // SCAN_COLS
    nt = S // SCAN_ROWS
    RE, IM = slice(0, SCAN_COLS), slice(SCAN_COLS, 2 * SCAN_COLS)
    FIRST, LAST = slice(0, 1), slice(SCAN_ROWS - 1, SCAN_ROWS)

    def body(d_ref, x_ref, a_ref, lam_ref, ga_ref):
        b = pl.program_id(1)
        powers = _complex_powers(a_ref[:, RE], -a_ref[:, IM], SCAN_ROWS)
        rid = lax.broadcasted_iota(jnp.int32, (SCAN_ROWS, SCAN_COLS), 0)
        cr_t = _rows_to_tile([powers[SCAN_ROWS - 1 - r][0] for r in range(SCAN_ROWS)])
        ci_t = _rows_to_tile([powers[SCAN_ROWS - 1 - r][1] for r in range(SCAN_ROWS)])
        rounds = _round_multipliers(powers, rid, reverse=True)

        @pl.when(b == 0)
        def _():
            ga_ref[...] = jnp.zeros_like(ga_ref)

        def tile(j, carry):
            cr, ci, accr, acci = carry
            i = nt - 1 - j
            lr, li = _tile_scan(d_ref[i, :, RE], d_ref[i, :, IM], rounds, reverse=True)
            lam_r = lr + (cr_t * cr - ci_t * ci)
            lam_i = li + (cr_t * ci + ci_t * cr)
            lam_ref[i, :, RE] = lam_r
            lam_ref[i, :, IM] = lam_i
            ip = jnp.maximum(i - 1, 0)
            keep = (i > 0).astype(F32)
            xpr = jnp.where(rid == 0, x_ref[ip, LAST, RE] * keep, pltpu.roll(x_ref[i, :, RE], 1, 0))
            xpi = jnp.where(rid == 0, x_ref[ip, LAST, IM] * keep, pltpu.roll(x_ref[i, :, IM], 1, 0))
            accr = accr + lam_r * xpr + lam_i * xpi
            acci = acci + lam_i * xpr - lam_r * xpi
            return lam_ref[i, FIRST, RE], lam_ref[i, FIRST, IM], accr, acci

        z1 = jnp.zeros((1, SCAN_COLS), F32)
        z8 = jnp.zeros((SCAN_ROWS, SCAN_COLS), F32)
        _, _, accr, acci = lax.fori_loop(0, nt, tile, (z1, z1, z8, z8), unroll=SCAN_UNROLL)
        ga_ref[:, RE] += _col_sum(accr)
        ga_ref[:, IM] += _col_sum(acci)

    col = pl.BlockSpec((None, nt, SCAN_ROWS, 2 * SCAN_COLS), lambda j, b: (b, 0, 0, j))
    par = pl.BlockSpec((1, 2 * SCAN_COLS), lambda j, b: (0, j))
    t4 = lambda a: a.reshape(B, nt, SCAN_ROWS, 2 * SSM_COLS)
    lam, g_a = pl.pallas_call(
        body, name="s5_scan_bwd", grid=(nc, B),
        in_specs=[col, col, par], out_specs=[col, par],
        out_shape=[jax.ShapeDtypeStruct((B, nt, SCAN_ROWS, 2 * SSM_COLS), F32),
                   jax.ShapeDtypeStruct((1, 2 * SSM_COLS), F32)],
        compiler_params=_params(2),
    )(t4(dxs), t4(xs), a_row)
    return lam.reshape(B, S, 2 * SSM_COLS), g_a


def _s5_discretise(lr, li, log_dt):
    dt = jnp.exp(log_dt)
    mag = jnp.exp(lr * dt)
    ang = li * dt
    ab_re, ab_im = mag * jnp.cos(ang), mag * jnp.sin(ang)
    nr, ni = ab_re - 1.0, ab_im
    den = lr * lr + li * li
    f_re = (nr * lr + ni * li) / den
    f_im = (ni * lr - nr * li) / den
    return dt, ab_re, ab_im, nr, ni, den, f_re, f_im


def _s5_params(a_re, a_im, log_dt):
    def body(lr_ref, li_ref, ld_ref, abr, abi, fr, fi):
        _, ab_re, ab_im, _, _, _, f_re, f_im = _s5_discretise(lr_ref[...], li_ref[...], ld_ref[...])
        abr[...] = ab_re
        abi[...] = ab_im
        fr[...] = f_re
        fi[...] = f_im

    return pl.pallas_call(body, name="s5_params",
                          out_shape=[jax.ShapeDtypeStruct(a_re.shape, F32)] * 4)(a_re, a_im, log_dt)


def _s5_input_matrix(f_re, f_im, b_re, b_im):
    def body(fr, fi, br, bi, o_re, o_im):
        o_re[...] = fr[...] * br[...] - fi[...] * bi[...]
        o_im[...] = fr[...] * bi[...] + fi[...] * br[...]

    return pl.pallas_call(body, name="s5_input_matrix",
                          out_shape=[jax.ShapeDtypeStruct(b_re.shape, F32)] * 2)(f_re, f_im, b_re, b_im)


def _s5_input_matrix_bwd(f_re, f_im, b_re, b_im, g_re, g_im):
    def body(fr, fi, br, bi, gr, gi, dbr, dbi, dfr, dfi):
        dbr[...] = fr[...] * gr[...] + fi[...] * gi[...]
        dbi[...] = fr[...] * gi[...] - fi[...] * gr[...]
        dfr[...] = jnp.sum(br[...] * gr[...] + bi[...] * gi[...], axis=1, keepdims=True)
        dfi[...] = jnp.sum(br[...] * gi[...] - bi[...] * gr[...], axis=1, keepdims=True)

    return pl.pallas_call(
        body, name="s5_input_matrix_bwd",
        out_shape=[jax.ShapeDtypeStruct(b_re.shape, F32)] * 2 + [jax.ShapeDtypeStruct(f_re.shape, F32)] * 2,
    )(f_re, f_im, b_re, b_im, g_re, g_im)


def _s5_params_bwd(a_re, a_im, log_dt, g_ab_re, g_ab_im, d_f_re, d_f_im):
    def body(lr_ref, li_ref, ld_ref, gar, gai, dfr, dfi, o_lr, o_li, o_ld):
        lr, li = lr_ref[...], li_ref[...]
        dt, ab_re, ab_im, nr, ni, den, f_re, f_im = _s5_discretise(lr, li, ld_ref[...])
        d_fr, d_fi = dfr[...], dfi[...]
        d_nr = (d_fr * lr - d_fi * li) / den
        d_ni = (d_fr * li + d_fi * lr) / den
        common = (d_fr * f_re + d_fi * f_im) * 2.0 / den
        d_lr = (d_fr * nr + d_fi * ni) / den - common * lr
        d_li = (d_fr * ni - d_fi * nr) / den - common * li
        d_abr = gar[...] + d_nr
        d_abi = gai[...] + d_ni
        d_mag_mag = d_abr * ab_re + d_abi * ab_im
        d_ang = d_abi * ab_re - d_abr * ab_im
        o_lr[...] = d_lr + d_mag_mag * dt
        o_li[...] = d_li + d_ang * dt
        o_ld[...] = jnp.sum(d_mag_mag * lr + d_ang * li, axis=1, keepdims=True) * dt

    return pl.pallas_call(
        body, name="s5_params_bwd",
        out_shape=[jax.ShapeDtypeStruct(a_re.shape, F32)] * 2 + [jax.ShapeDtypeStruct(log_dt.shape, F32)],
    )(a_re, a_im, log_dt, g_ab_re, g_ab_im, d_f_re, d_f_im)


CONV_COLS = 256


def _shift_down(v, j, row):
    return jnp.where(row >= j, pltpu.roll(v, j, 0), 0.0)


def _shift_up(v, j, row, seq):
    return jnp.where(row < seq - j, pltpu.roll(v, seq - j, 0), 0.0)


def _conv_fwd(up, w_conv, b_conv):
    B, S, _ = up.shape
    nj = D_FF // CONV_COLS

    def body(up_ref, w_ref, b_ref, ff_ref):
        a = up_ref[:, :CONV_COLS].astype(F32)
        val = up_ref[:, CONV_COLS:].astype(F32)
        row = lax.broadcasted_iota(jnp.int32, a.shape, 0)
        w0, w1, w2 = w_ref[0:1, :], w_ref[1:2, :], w_ref[2:3, :]
        conv = b_ref[...] + w0 * a + w1 * _shift_down(a, 1, row) + w2 * _shift_down(a, 2, row)
        ff_ref[...] = (conv * _sigmoid(conv) * val).astype(ff_ref.dtype)

    return pl.pallas_call(
        body, name="conv_gate_fwd", grid=(B, nj),
        in_specs=[pl.BlockSpec((None, S, 2 * CONV_COLS), lambda b, j: (b, 0, j)),
                  pl.BlockSpec((3, CONV_COLS), lambda b, j: (0, j)),
                  pl.BlockSpec((1, CONV_COLS), lambda b, j: (0, j))],
        out_specs=pl.BlockSpec((None, S, CONV_COLS), lambda b, j: (b, 0, j)),
        out_shape=jax.ShapeDtypeStruct((B, S, D_FF), BF16),
        compiler_params=_params(2),
    )(up, w_conv, b_conv)


def _conv_bwd(up, d_ff, w_conv, b_conv):
    B, S, _ = up.shape
    nj = D_FF // CONV_COLS

    def body(up_ref, dff_ref, w_ref, b_ref, dup_ref, dw_ref, db_ref):
        b = pl.program_id(1)
        a = up_ref[:, :CONV_COLS].astype(F32)
        val = up_ref[:, CONV_COLS:].astype(F32)
        row = lax.broadcasted_iota(jnp.int32, a.shape, 0)
        w0, w1, w2 = w_ref[0:1, :], w_ref[1:2, :], w_ref[2:3, :]
        a1, a2 = _shift_down(a, 1, row), _shift_down(a, 2, row)
        conv = b_ref[...] + w0 * a + w1 * a1 + w2 * a2
        sg = _sigmoid(conv)
        dff = dff_ref[...].astype(F32)
        d_val = dff * conv * sg
        dc = dff * val * (sg * (1.0 + conv * (1.0 - sg)))
        d_a = w0 * dc + w1 * _shift_up(dc, 1, row, S) + w2 * _shift_up(dc, 2, row, S)
        dup_ref[:, :CONV_COLS] = d_a.astype(dup_ref.dtype)
        dup_ref[:, CONV_COLS:] = d_val.astype(dup_ref.dtype)

        @pl.when(b == 0)
        def _():
            dw_ref[...] = jnp.zeros_like(dw_ref)
            db_ref[...] = jnp.zeros_like(db_ref)

        dw_ref[0:1, :] += _col_sum(dc * a)
        dw_ref[1:2, :] += _col_sum(dc * a1)
        dw_ref[2:3, :] += _col_sum(dc * a2)
        db_ref[...] += _col_sum(dc)

    return pl.pallas_call(
        body, name="conv_gate_bwd", grid=(nj, B),
        in_specs=[pl.BlockSpec((None, S, 2 * CONV_COLS), lambda j, b: (b, 0, j)),
                  pl.BlockSpec((None, S, CONV_COLS), lambda j, b: (b, 0, j)),
                  pl.BlockSpec((3, CONV_COLS), lambda j, b: (0, j)),
                  pl.BlockSpec((1, CONV_COLS), lambda j, b: (0, j))],
        out_specs=[pl.BlockSpec((None, S, 2 * CONV_COLS), lambda j, b: (b, 0, j)),
                   pl.BlockSpec((3, CONV_COLS), lambda j, b: (0, j)),
                   pl.BlockSpec((1, CONV_COLS), lambda j, b: (0, j))],
        out_shape=[jax.ShapeDtypeStruct((B, S, 2 * D_FF), BF16), jax.ShapeDtypeStruct((3, D_FF), F32),
                   jax.ShapeDtypeStruct((1, D_FF), F32)],
        compiler_params=_params(2),
    )(up, d_ff, w_conv, b_conv)


def _ada_fwd(c_all, w_ada, b_ada):
    def body(c_ref, w_ref, b_ref, o_ref):
        cv = c_ref[...]
        act = (cv * _sigmoid(cv)).astype(BF16)
        o_ref[...] = jnp.dot(act, w_ref[...].astype(BF16), preferred_element_type=F32) + b_ref[...]

    return pl.pallas_call(body, name="ada_fwd",
                          out_shape=jax.ShapeDtypeStruct((c_all.shape[0], w_ada.shape[1]), F32),
                          compiler_params=pltpu.CompilerParams(vmem_limit_bytes=V7X_VMEM_LIMIT))(c_all, w_ada, b_ada)


def _ada_bwd(c_all, dmod_all, dmod_cols):
    def body(c_ref, dm_ref, dmc_ref, dw_ref, db_ref):
        cv = c_ref[...]
        act = (cv * _sigmoid(cv)).astype(BF16)
        dw_ref[...] = lax.dot_general(act, dmc_ref[...].astype(BF16), _TN, preferred_element_type=F32)
        db_ref[...] = _col_sum(dm_ref[...])

    return pl.pallas_call(
        body, name="ada_bwd",
        out_shape=[jax.ShapeDtypeStruct((c_all.shape[1], dmod_cols.shape[1]), F32),
                   jax.ShapeDtypeStruct((1, dmod_all.shape[1]), F32)],
        compiler_params=pltpu.CompilerParams(vmem_limit_bytes=V7X_VMEM_LIMIT))(c_all, dmod_all, dmod_cols)


def _adamw(w, m, v, g_parts, name, own=None):
    R, C = w.shape
    P = g_parts.shape[0]
    tr = R
    for cand in (256, 128, 64, 32, 16, 8):
        if R % cand == 0 and cand * C * 4 * (P + 8) * 2 <= V7X_VMEM_LIMIT // 2:
            tr = cand
            break
    c1 = 1.0 / (1.0 - ADAM_B1 ** ADAM_STEP)
    c2 = 1.0 / (1.0 - ADAM_B2 ** ADAM_STEP)

    def update(w_ref, m_ref, v_ref, g, og, od, om, ov):
        m_new = ADAM_B1 * m_ref[...] + (1.0 - ADAM_B1) * g
        v_new = ADAM_B2 * v_ref[...] + (1.0 - ADAM_B2) * (g * g)
        og[...] = g
        om[...] = m_new
        ov[...] = v_new
        od[...] = -ADAM_LR * ((m_new * c1) / (jnp.sqrt(v_new * c2) + ADAM_EPS) + ADAM_WD * w_ref[...])

    def total(g_ref):
        g = g_ref[0].astype(F32)
        for p in range(1, P):
            g = g + g_ref[p].astype(F32)
        return g

    out_shape = [jax.ShapeDtypeStruct((R, C), F32)] * 4
    if own is None:
        def body(w_ref, m_ref, v_ref, g_ref, og, od, om, ov):
            update(w_ref, m_ref, v_ref, total(g_ref), og, od, om, ov)

        spec = pl.BlockSpec((tr, C), lambda i: (i, 0))
        return pl.pallas_call(
            body, name=name, grid=(R // tr,),
            in_specs=[spec, spec, spec, pl.BlockSpec((P, tr, C), lambda i: (0, i, 0))],
            out_specs=[spec] * 4, out_shape=out_shape, compiler_params=_params(1),
        )(w, m, v, g_parts)

    slots, me = own

    def body_own(me_ref, w_ref, m_ref, v_ref, g_ref, own_ref, og, od, om, ov):
        g = own_ref[...].astype(F32)
        for p in range(P):
            g = g + jnp.where(me_ref[0] == p, 0.0, g_ref[p].astype(F32))
        update(w_ref, m_ref, v_ref, g, og, od, om, ov)

    spec = pl.BlockSpec((tr, C), lambda i, me_ref: (i, 0))
    grid_spec = pltpu.PrefetchScalarGridSpec(
        num_scalar_prefetch=1, grid=(R // tr,),
        in_specs=[spec, spec, spec, pl.BlockSpec((P, tr, C), lambda i, me_ref: (0, i, 0)),
                  pl.BlockSpec((None, tr, C), lambda i, me_ref: (me_ref[0], i, 0))],
        out_specs=[spec] * 4)
    return pl.pallas_call(body_own, name=name, grid_spec=grid_spec, out_shape=out_shape,
                          compiler_params=_params(1))(me, w, m, v, g_parts, slots)


def _sum_parts(parts, loss_rows):
    P, R, C = parts.shape
    lo, hi = loss_rows

    def body(p_ref, o_ref, loss_ref):
        t = p_ref[0]
        for p in range(1, P):
            t = t + p_ref[p]
        o_ref[...] = t
        tot = jnp.sum(jnp.sum(o_ref[lo:hi, :], axis=1, keepdims=True), axis=0, keepdims=True)
        loss_ref[...] = jnp.broadcast_to(tot, loss_ref.shape)

    return pl.pallas_call(body, name="sum_small_grads",
                          out_shape=[jax.ShapeDtypeStruct((R, C), F32), jax.ShapeDtypeStruct((1, LANES), F32)],
                          compiler_params=pltpu.CompilerParams(vmem_limit_bytes=V7X_VMEM_LIMIT))(parts)


def _exchange(items, name):
    n = len(items)
    MESH = pl.DeviceIdType.MESH

    def body(*refs):
        src, dst = refs[:n], refs[n:2 * n]
        send_sems, recv_sems, local_sems = refs[2 * n:]
        x, y, c = lax.axis_index("x"), lax.axis_index("y"), lax.axis_index("c")
        me = 4 * x + 2 * y + c
        started = []
        for it, (_, per_peer) in enumerate(items):
            own = pltpu.make_async_copy(src[it].at[me] if per_peer else src[it], dst[it].at[me], local_sems.at[it])
            own.start()
            started.append(own)
        sends, recvs = [], []
        for k in range(1, N_DEV):
            px = 1 - x if k & 4 else x
            py = 1 - y if k & 2 else y
            pc = 1 - c if k & 1 else c
            peer = 4 * px + 2 * py + pc
            for it, (_, per_peer) in enumerate(items):
                s = src[it].at[peer] if per_peer else src[it]
                cp = pltpu.make_async_remote_copy(src_ref=s, dst_ref=dst[it].at[me], send_sem=send_sems.at[it, k - 1],
                                                  recv_sem=recv_sems.at[it, k - 1], device_id=(px, py, pc),
                                                  device_id_type=MESH)
                cp.start()
                sends.append(cp)
                recvs.append(pltpu.make_async_remote_copy(
                    src_ref=s, dst_ref=dst[it].at[peer], send_sem=send_sems.at[it, k - 1],
                    recv_sem=recv_sems.at[it, k - 1], device_id=(px, py, pc), device_id_type=MESH))
        for cp in recvs:
            cp.wait_recv()
        for cp in sends:
            cp.wait_send()
        for cp in started:
            cp.wait()

    any_spec = pl.BlockSpec(memory_space=pl.ANY)
    out_shape = []
    for a, per_peer in items:
        shp = a.shape if per_peer else (N_DEV,) + a.shape
        out_shape.append(jax.ShapeDtypeStruct(shp, a.dtype))
    return pl.pallas_call(
        body, name=name, in_specs=[any_spec] * n, out_specs=[any_spec] * n, out_shape=out_shape,
        scratch_shapes=[pltpu.SemaphoreType.DMA((n, N_DEV - 1)), pltpu.SemaphoreType.DMA((n, N_DEV - 1)),
                        pltpu.SemaphoreType.DMA((n,))],
    )(*[a for a, _ in items])


def _remote(src, dst, send_sem, recv_sem, device):
    return pltpu.make_async_remote_copy(src_ref=src, dst_ref=dst, send_sem=send_sem, recv_sem=recv_sem,
                                        device_id=device, device_id_type=pl.DeviceIdType.MESH)


def _mesh_place():
    x, y, c = lax.axis_index("x"), lax.axis_index("y"), lax.axis_index("c")
    other_chips = [(1 - x, y), (x, 1 - y), (1 - x, 1 - y)]
    return x, y, c, (x, y, 1 - c), other_chips


def _gather_all(items, name):
    n = len(items)

    def body(*refs):
        src, dst = refs[:n], refs[n:2 * n]
        send_sems, recv_sems, local_sems = refs[2 * n:]
        x, y, c, sibling, chips = _mesh_place()
        slot = lambda px, py, pc: 4 * px + 2 * py + pc
        me = slot(x, y, c)
        own = [pltpu.make_async_copy(src[it], dst[it].at[me], local_sems.at[it]) for it in range(n)]
        first = []
        for it in range(n):
            first.append(_remote(src[it], dst[it].at[me], send_sems.at[it, 0], recv_sems.at[it, 0], sibling))
            for j, chip in enumerate(chips):
                first.append(_remote(src[it], dst[it].at[me], send_sems.at[it, 1 + j], recv_sems.at[it, 1 + j],
                                     (*chip, c)))
        for cp in own + first:
            cp.start()
        passed = []
        for j, chip in enumerate(chips):
            blk = slot(*chip, c)
            for it in range(n):
                _remote(src[it], dst[it].at[blk], send_sems.at[it, 1 + j], recv_sems.at[it, 1 + j],
                        (*chip, c)).wait_recv()
                fwd = _remote(dst[it].at[blk], dst[it].at[blk], send_sems.at[it, 4 + j], recv_sems.at[it, 4 + j],
                              sibling)
                fwd.start()
                passed.append(fwd)
        for it in range(n):
            _remote(src[it], dst[it].at[slot(x, y, 1 - c)], send_sems.at[it, 0], recv_sems.at[it, 0],
                    sibling).wait_recv()
        for j, chip in enumerate(chips):
            for it in range(n):
                _remote(src[it], dst[it].at[slot(*chip, 1 - c)], send_sems.at[it, 4 + j], recv_sems.at[it, 4 + j],
                        sibling).wait_recv()
        for cp in first + passed:
            cp.wait_send()
        for cp in own:
            cp.wait()

    any_spec = pl.BlockSpec(memory_space=pl.ANY)
    return pl.pallas_call(
        body, name=name, in_specs=[any_spec] * n, out_specs=[any_spec] * n,
        out_shape=[jax.ShapeDtypeStruct((N_DEV,) + a.shape, a.dtype) for a in items],
        scratch_shapes=[pltpu.SemaphoreType.DMA((n, 7)), pltpu.SemaphoreType.DMA((n, 7)),
                        pltpu.SemaphoreType.DMA((n,))],
    )(*items)


def _peers():
    x, y, c = lax.axis_index("x"), lax.axis_index("y"), lax.axis_index("c")
    out = []
    for k in range(1, N_DEV):
        px = 1 - x if k & 4 else x
        py = 1 - y if k & 2 else y
        pc = 1 - c if k & 1 else c
        out.append((k, (px, py, pc), 4 * px + 2 * py + pc))
    return 4 * x + 2 * y + c, out


def _exchange_start(items, name, gather, carry=()):
    n, m = len(items), len(carry)

    def body(*refs):
        src, land = refs[:n], refs[n:2 * n]
        first_out = 2 * n + m
        send_sems, recv_sems = refs[first_out:first_out + n], refs[first_out + n:first_out + 2 * n]
        token = refs[-1]
        me, peers = _peers()
        for k, peer, slot in peers:
            for it in range(n):
                _remote(src[it] if gather else src[it].at[slot], land[it].at[me], send_sems[it], recv_sems[it],
                        peer).start()
        token[...] = jnp.zeros_like(token)

    hbm = pl.BlockSpec(memory_space=pltpu.HBM)
    sem = pl.BlockSpec(memory_space=pltpu.SEMAPHORE)
    land_shapes = [(N_DEV,) + (a.shape if gather else a.shape[1:]) for a in items]
    lands = [lax.empty(shp, a.dtype) for shp, a in zip(land_shapes, items)]
    through = list(items) + lands + list(carry)
    outs = pl.pallas_call(
        body, name=name,
        out_shape=(*[pltpu.SemaphoreType.DMA(())] * (2 * n), *[pltpu.HBM(a.shape, a.dtype) for a in through],
                   jax.ShapeDtypeStruct((8, LANES), F32)),
        in_specs=[hbm] * len(through),
        out_specs=(*[sem] * (2 * n), *[hbm] * len(through), pl.BlockSpec(memory_space=pltpu.VMEM)),
        input_output_aliases={i: 2 * n + i for i in range(len(through))},
        compiler_params=pltpu.CompilerParams(has_side_effects=pltpu.SideEffectType.DATAFLOW_SIDE_EFFECTING),
    )(*[pltpu.with_memory_space_constraint(a, pltpu.HBM) for a in through])
    return (list(outs[:n]), list(outs[n:2 * n]), list(outs[2 * n:3 * n]), list(outs[3 * n:4 * n]), outs[-1],
            list(outs[4 * n:4 * n + m]))


def _exchange_wait(send_sems, recv_sems, items, lands, after, name):
    n = len(items)

    def body(*refs):
        land = refs[n:2 * n]
        send_sems, recv_sems = refs[2 * n:3 * n], refs[3 * n:4 * n]
        me, peers = _peers()
        for it in range(n):
            seven = land[it].at[pl.ds(0, N_DEV - 1)]
            cp = _remote(seven, seven, send_sems[it], recv_sems[it], peers[0][1])
            cp.wait_send()
            cp.wait_recv()

    hbm = pl.BlockSpec(memory_space=pltpu.HBM)
    sem = pl.BlockSpec(memory_space=pltpu.SEMAPHORE)
    outs = pl.pallas_call(
        body, name=name,
        out_shape=tuple(pltpu.HBM(a.shape, a.dtype) for a in list(items) + list(lands)),
        in_specs=[hbm] * (2 * n) + [sem] * (2 * n) + [pl.BlockSpec(memory_space=pl.ANY)],
        out_specs=tuple([hbm] * (2 * n)),
        input_output_aliases={i: i for i in range(2 * n)},
        compiler_params=pltpu.CompilerParams(has_side_effects=pltpu.SideEffectType.DATAFLOW_SIDE_EFFECTING),
    )(*items, *lands, *send_sems, *recv_sems, after)
    return list(outs[:n]), list(outs[n:])


def _gelu_tanh(y):
    k = math.sqrt(2.0 / math.pi)
    t = jnp.tanh(k * (y + 0.044715 * y * y * y))
    return 0.5 * y * (1.0 + t), t


def _local_step(x, mod, target, W, late_weights, P, send_early):
    B, S, D = x.shape
    T = B * S
    TS = 512
    flat = lambda a: a.reshape(T, a.shape[-1])
    unflat = lambda a: a.reshape(B, S, a.shape[-1])
    mod_col = lambda i: (mod, D, i)

    def f_modnorm(xv, sc, sh, g):
        return (xv * _rms_scale(xv) * g) * (1.0 + sc) + sh

    (u1,) = _rowwise(f_modnorm, [(x, D, 0)], [mod_col(1), mod_col(0)], [P["g_mix"]],
                     [(D, BF16)], [], [], ts=TS, name="modnorm_mix")
    u1f = flat(u1)
    qkv = unflat(_matmul(u1f, W["w_qkv"], name="proj_qkv"))
    us = unflat(_matmul(u1f, W["w_us"], name="proj_ssm_in"))
    gates = unflat(_matmul(u1f, W["w_gates"], out_dtype=BF16, name="proj_gates"))

    o_att, lse = _attention_fwd(qkv, P["slopes"])
    more_w, more_p = late_weights(o_att)
    W, P = {**W, **more_w}, {**P, **more_p}
    y_att = unflat(_matmul(flat(o_att), W["w_proj_att"], out_dtype=BF16, name="proj_att"))

    bu = unflat(_matmul(flat(us), P["bb_big"], name="s5_bu"))
    xs = _scan_fwd(bu, P["a_row"])
    y_mm = unflat(_matmul(flat(xs), P["cc_big"], name="s5_readout"))

    def f_glu(ymm, usv, dsk, wg, bg):
        yv = ymm + dsk * usv
        ge, _ = _gelu_tanh(yv)
        pre = jnp.dot(ge.astype(BF16), wg, preferred_element_type=F32) + bg
        return yv, ge * _sigmoid(pre)

    y_s5, z = _rowwise(f_glu, [(y_mm, SSM_WIDTH, 0), (us, SSM_WIDTH, 0)], [], [P["d_skip"], W["w_glu"], P["b_glu"]],
                       [(SSM_WIDTH, F32), (SSM_WIDTH, BF16)], [], [], ts=TS, name="s5_glu")
    y_ssm = unflat(_matmul(flat(z), W["w_proj_ssm"], out_dtype=BF16, name="proj_ssm"))

    def f_merge(ga, gs, ya, ys, bga, bgs):
        return _sigmoid(ga + bga) * ya + _sigmoid(gs + bgs) * ys

    bga, bgs = P["b_gate"][:, :D], P["b_gate"][:, D:]
    (merged,) = _rowwise(f_merge, [(gates, D, 0), (gates, D, 1), (y_att, D, 0), (y_ssm, D, 0)], [], [bga, bgs],
                         [(D, BF16)], [], [], ts=TS, name="gate_merge")
    mix = unflat(_matmul(flat(merged), W["w_out"], name="proj_out"))

    def f_res_modnorm(xv, mx, gt, sc, sh, g):
        h = xv + gt * mx
        return h, (h * _rms_scale(h) * g) * (1.0 + sc) + sh

    h1, u2 = _rowwise(f_res_modnorm, [(x, D, 0), (mix, D, 0)], [mod_col(2), mod_col(4), mod_col(3)], [P["g_ffn"]],
                      [(D, F32), (D, BF16)], [], [], ts=TS, name="residual_modnorm_ffn")
    up = unflat(_up_fwd(flat(u2), W["w_up"], name="ffn_up"))
    ff = _conv_fwd(up, P["w_conv"], P["b_conv"])
    down = unflat(_matmul(flat(ff), W["w_down"], name="ffn_down"))

    def f_head(h1v, dn, tg, gt, g):
        h2 = h1v + gt * dn
        r = _rms_scale(h2)
        nh = h2 * r
        e = nh * g - tg
        dy = e * (1.0 / D)
        gy = dy * g
        dh = r * (gy - nh * jnp.mean(gy * nh, axis=-1, keepdims=True))
        return (dh, dh * gt, _col_sum(dh * dn), _col_sum(dy * nh), _col_sum(e * e) * (0.5 / D))

    dh2, d_down, d_gt2, d_g_final, loss_cols = _rowwise(
        f_head, [(h1, D, 0), (down, D, 0), (target, D, 0)], [mod_col(5)], [P["g_final"]],
        [(D, F32), (D, BF16)], [D], [(1, D), (1, D)], ts=TS, name="head_loss")

    d_downf = flat(d_down)
    d_ff = unflat(_matmul(d_downf, W["w_down"], tb=True, out_dtype=BF16, name="ffn_down_dx"))
    d_w_down = _matmul(flat(ff), d_downf, ta=True, out_dtype=BF16, name="ffn_down_dw")
    d_up, d_w_conv, d_b_conv = _conv_bwd(up, d_ff, P["w_conv"], P["b_conv"])
    d_upf = flat(d_up)
    d_u2 = unflat(_up_dx(d_upf, W["w_up"], name="ffn_up_dx"))
    d_w_up = _up_dw(flat(u2), d_upf, name="ffn_up_dw")
    token, _ = send_early(dict(w_down=d_w_down.reshape(N_DEV, D_FF // N_DEV, D), w_up=d_w_up))
    g_ffn_after = P["g_ffn"] + token[0:1, 0:1]

    def f_modnorm_bwd(du, h, dres, mx, sc, gt, g):
        r = _rms_scale(h)
        nh = h * r
        dn = du * (1.0 + sc)
        gy = dn * g
        dh = dres + r * (gy - nh * jnp.mean(gy * nh, axis=-1, keepdims=True))
        return (dh, dh * gt, _col_sum(du), _col_sum(du * nh * g), _col_sum(dh * mx), _col_sum(dn * nh))

    dh1, d_mix, d_sh2, d_sc2, d_gt1, d_g_ffn = _rowwise(
        f_modnorm_bwd, [(d_u2, D, 0), (h1, D, 0), (dh2, D, 0), (mix, D, 0)], [mod_col(4), mod_col(2)], [g_ffn_after],
        [(D, F32), (D, BF16)], [D, D, D], [(1, D)], ts=TS, name="modnorm_ffn_bwd")

    d_mixf = flat(d_mix)
    d_merged = unflat(_matmul(d_mixf, W["w_out"], tb=True, out_dtype=BF16, name="proj_out_dx"))
    d_w_out = _matmul(flat(merged), d_mixf, ta=True, out_dtype=BF16, name="proj_out_dw")

    def f_merge_bwd(dm, ga, gs, ya, ys, bga_, bgs_):
        sa, ss = _sigmoid(ga + bga_), _sigmoid(gs + bgs_)
        dga = dm * ya * sa * (1.0 - sa)
        dgs = dm * ys * ss * (1.0 - ss)
        return dm * sa, dm * ss, jnp.concatenate([dga, dgs], axis=1), _col_sum(dga), _col_sum(dgs)

    d_y_att, d_y_ssm, d_gates, d_bga, d_bgs = _rowwise(
        f_merge_bwd, [(d_merged, D, 0), (gates, D, 0), (gates, D, 1), (y_att, D, 0), (y_ssm, D, 0)], [], [bga, bgs],
        [(D, BF16), (D, BF16), (2 * D, BF16)], [], [(1, D), (1, D)], ts=TS, name="gate_merge_bwd")

    d_yaf, d_ysf = flat(d_y_att), flat(d_y_ssm)
    d_o_att = unflat(_matmul(d_yaf, W["w_proj_att"], tb=True, name="proj_att_dx"))
    d_w_proj_att = _matmul(flat(o_att), d_yaf, ta=True, out_dtype=BF16, name="proj_att_dw")
    d_z = unflat(_matmul(d_ysf, W["w_proj_ssm"], tb=True, out_dtype=BF16, name="proj_ssm_dx"))
    d_w_proj_ssm = _matmul(flat(z), d_ysf, ta=True, out_dtype=BF16, name="proj_ssm_dw")

    def f_glu_bwd(yv, dz, usv, dsk, wg, bg):
        ge, t = _gelu_tanh(yv)
        pre = jnp.dot(ge.astype(BF16), wg, preferred_element_type=F32) + bg
        sg = _sigmoid(pre)
        dpre = dz * ge * sg * (1.0 - sg)
        dge = dz * sg + lax.dot_general(dpre.astype(BF16), wg, _NT, preferred_element_type=F32)
        k = math.sqrt(2.0 / math.pi)
        dgelu = 0.5 * (1.0 + t) + 0.5 * yv * (1.0 - t * t) * k * (1.0 + 3.0 * 0.044715 * yv * yv)
        dy = dge * dgelu
        dwg = lax.dot_general(ge.astype(BF16), dpre.astype(BF16), _TN, preferred_element_type=F32)
        return dy, dy * dsk, dwg, _col_sum(dpre), _col_sum(dy * usv)

    d_y_s5, d_us_skip, d_w_glu, d_b_glu, d_d_skip = _rowwise(
        f_glu_bwd, [(y_s5, SSM_WIDTH, 0), (d_z, SSM_WIDTH, 0), (us, SSM_WIDTH, 0)], [],
        [P["d_skip"], W["w_glu"], P["b_glu"]],
        [(SSM_WIDTH, BF16), (SSM_WIDTH, F32)], [], [(SSM_WIDTH, SSM_WIDTH), (1, SSM_WIDTH), (1, SSM_WIDTH)],
        ts=TS, name="s5_glu_bwd")
    d_ysf2 = flat(d_y_s5)
    dxs = unflat(_matmul(d_ysf2, P["cc_big"], tb=True, name="s5_readout_dx"))
    d_cc = _matmul(flat(xs), d_ysf2, ta=True, name="s5_readout_dw")
    lam, g_ab = _scan_bwd(dxs, xs, P["a_row"])
    lam = flat(lam)
    d_us_mm = unflat(_matmul(lam, P["bb_big"], tb=True, name="s5_bu_dx"))
    d_bb = _matmul(flat(us), lam, ta=True, name="s5_bu_dw")

    token, _ = send_early(dict(
        w_out=d_w_out.reshape(N_DEV, D // N_DEV, D), w_proj_att=_cols_to_slots(d_w_proj_att),
        w_proj_ssm=_cols_to_slots(d_w_proj_ssm),
        w_glu=d_w_glu.astype(BF16).reshape(N_DEV, SSM_WIDTH // N_DEV, SSM_WIDTH),
        w_conv=_cols_to_slots(d_w_conv.astype(BF16))))
    d_qkv = _attention_bwd(qkv, o_att, d_o_att, lse, P["slopes"] + token[0, 0])

    def f_add(a, b_):
        return a + b_

    (d_us,) = _rowwise(f_add, [(d_us_mm, SSM_WIDTH, 0), (d_us_skip, SSM_WIDTH, 0)], [], [],
                       [(SSM_WIDTH, BF16)], [], [], ts=TS, name="s5_input_grad")
    d_qkvf = flat(d_qkv)
    d_usf = flat(d_us)
    d_gatesf = flat(d_gates)
    d_w_in = jnp.concatenate(
        [_unpair_qkv_columns(_matmul(u1f, d_qkvf, ta=True, out_dtype=BF16, name="proj_qkv_dw")),
         _matmul(u1f, d_usf, ta=True, out_dtype=BF16, name="proj_ssm_in_dw"),
         _matmul(u1f, d_gatesf, ta=True, out_dtype=BF16, name="proj_gates_dw")], axis=1)
    token, (w_qkv, w_us, w_gates) = send_early(dict(w_in=_cols_to_slots(d_w_in)),
                                               carry=[W["w_qkv"], W["w_us"], W["w_gates"]])
    d_u1 = (_matmul(d_qkvf, w_qkv, tb=True, out_dtype=BF16, name="proj_qkv_dx"),
            _matmul(d_usf, w_us, tb=True, out_dtype=BF16, name="proj_ssm_in_dx"),
            _matmul(d_gatesf, w_gates, tb=True, out_dtype=BF16, name="proj_gates_dx"))

    def f_modnorm_bwd_in(du0, du1, du2, h, dres, sc, g):
        du = du0 + du1 + du2
        r = _rms_scale(h)
        nh = h * r
        dn = du * (1.0 + sc)
        gy = dn * g
        dh = dres + r * (gy - nh * jnp.mean(gy * nh, axis=-1, keepdims=True))
        return (dh, _col_sum(du), _col_sum(du * nh * g), _col_sum(dn * nh))

    grad_x, d_sh1, d_sc1, d_g_mix = _rowwise(
        f_modnorm_bwd_in, [(unflat(d_u1[0]), D, 0), (unflat(d_u1[1]), D, 0), (unflat(d_u1[2]), D, 0), (x, D, 0),
                           (dh1, D, 0)], [mod_col(1)], [P["g_mix"] + token[0:1, 0:1]],
        [(D, F32)], [D, D], [(1, D)], ts=TS, name="modnorm_mix_bwd")

    d_mod = jnp.concatenate([d_sh1, d_sc1, d_gt1, d_sh2, d_sc2, d_gt2], axis=-1)
    g_ab_re, g_ab_im = _deinterleave(g_ab)
    d_bb_re, d_bb_im = _deinterleave(d_bb)
    d_cc_re, d_cc_im = (t.T for t in _deinterleave(d_cc.T))
    small = dict(g_mix=d_g_mix, b_gate=jnp.concatenate([d_bga, d_bgs], axis=1), g_ab_re=g_ab_re, g_ab_im=g_ab_im,
                 d_bb_re=d_bb_re, d_bb_im=d_bb_im, d_cc_re=d_cc_re, d_cc_im=d_cc_im, d_skip=d_d_skip,
                 b_glu=d_b_glu, g_ffn=d_g_ffn, b_conv=d_b_conv, g_final=d_g_final, loss_cols=loss_cols)
    return grad_x, d_mod, small


def _block_diag_in(bb):
    t = bb.reshape(SSM_GROUPS, SSM_STATE, SSM_GROUP_CH)
    eye = jnp.eye(SSM_GROUPS, dtype=bb.dtype)
    return jnp.einsum("gnc,gh->gchn", t, eye).reshape(SSM_WIDTH, SSM_COLS)


def _block_diag_out(cm):
    eye = jnp.eye(SSM_GROUPS, dtype=cm.dtype)
    return jnp.einsum("gcn,gh->gnhc", cm, eye).reshape(SSM_COLS, SSM_WIDTH)


def _diag_blocks_in(m):
    t = m.reshape(SSM_GROUPS, SSM_GROUP_CH, SSM_GROUPS, SSM_STATE)
    idx = jnp.arange(SSM_GROUPS)
    return t[idx, :, idx, :].transpose(0, 2, 1).reshape(SSM_COLS, SSM_GROUP_CH)


def _diag_blocks_out(m):
    t = m.reshape(SSM_GROUPS, SSM_STATE, SSM_GROUPS, SSM_GROUP_CH)
    idx = jnp.arange(SSM_GROUPS)
    return t[idx, :, idx, :].transpose(0, 2, 1)


def _pair_qkv_columns(w):
    lead = w.shape[:-1]
    return w.reshape(lead + (3, N_HEADS // 2, LANES)).swapaxes(-3, -2).reshape(lead + (3 * ATT_WIDTH,))


def _unpair_qkv_columns(w):
    lead = w.shape[:-1]
    return w.reshape(lead + (N_HEADS // 2, 3, LANES)).swapaxes(-3, -2).reshape(lead + (3 * ATT_WIDTH,))


def _interleave(re, im):
    lead = re.shape[:-1]
    g = lambda a: a.reshape(lead + (SSM_COLS // SCAN_COLS, 1, SCAN_COLS))
    return jnp.concatenate([g(re), g(im)], axis=-2).reshape(lead + (2 * SSM_COLS,))


def _deinterleave(x):
    lead = x.shape[:-1]
    t = x.reshape(lead + (SSM_COLS // SCAN_COLS, 2, SCAN_COLS))
    return t[..., 0, :].reshape(lead + (SSM_COLS,)), t[..., 1, :].reshape(lead + (SSM_COLS,))


def _cols_to_slots(g):
    R = g.shape[0]
    return g.reshape(R, N_DEV, g.shape[1] // N_DEV).transpose(1, 0, 2)


def _slots_to_cols(g):
    return g.transpose(1, 0, 2).reshape(g.shape[1], N_DEV * g.shape[2])


SMALL_ORDER = ("b_ada", "g_mix", "b_gate", "a_re", "a_im", "log_dt", "b_re", "b_im", "c_re", "c_im", "d_skip",
               "b_glu", "g_ffn", "b_conv", "g_final")


def _pack(arrs):
    pieces, offs, row = [], [], 0
    for a in arrs:
        f = a.reshape(-1).astype(F32)
        n = f.shape[0]
        rows = -(-n // LANES)
        pieces.append(jnp.pad(f, (0, rows * LANES - n)))
        offs.append((row, n))
        row += rows
    return jnp.concatenate(pieces).reshape(row, LANES), offs


def _unpack(packed, offs, shapes):
    flat = packed.reshape(-1)
    return [flat[r * LANES:r * LANES + n].reshape(s) for (r, n), s in zip(offs, shapes)]


def kernel(x, c, w_ada, b_ada, g_mix, w_in, b_gate, a_re, a_im, log_dt, b_re, b_im, c_re, c_im, d_skip, w_glu, b_glu, w_proj_att, w_proj_ssm, w_out, g_ffn, w_up, w_conv, b_conv, w_down, g_final, loss_target, m_w_ada, m_b_ada, m_g_mix, m_w_in, m_b_gate, m_a_re, m_a_im, m_log_dt, m_b_re, m_b_im, m_c_re, m_c_im, m_d_skip, m_w_glu, m_b_glu, m_w_proj_att, m_w_proj_ssm, m_w_out, m_g_ffn, m_w_up, m_w_conv, m_b_conv, m_w_down, m_g_final, v_w_ada, v_b_ada, v_g_mix, v_w_in, v_b_gate, v_a_re, v_a_im, v_log_dt, v_b_re, v_b_im, v_c_re, v_c_im, v_d_skip, v_w_glu, v_b_glu, v_w_proj_att, v_w_proj_ssm, v_w_out, v_g_ffn, v_w_up, v_w_conv, v_b_conv, v_w_down, v_g_final):
    args = dict(locals())
    B, S, D = x.shape
    me = 4 * lax.axis_index("x") + 2 * lax.axis_index("y") + lax.axis_index("c")
    bf = lambda w: w[0].astype(BF16)

    c_slots, w_in_slots = _gather_all([c, bf(w_in)], name="gather_first_weights")
    c_all = c_slots.reshape(N_DEV * B, D)
    w_in_full = _slots_to_cols(w_in_slots)
    n_qkv = 3 * ATT_WIDTH
    W = dict(w_qkv=_pair_qkv_columns(w_in_full[:, :n_qkv]), w_us=w_in_full[:, n_qkv:n_qkv + SSM_WIDTH],
             w_gates=w_in_full[:, n_qkv + SSM_WIDTH:])

    n_ada = w_ada.shape[2]
    b_ada_cols = lax.dynamic_slice(b_ada, (0, me * n_ada), (1, n_ada))
    mod_part = _ada_fwd(c_all, w_ada[0], b_ada_cols)
    (mod_slots,) = _exchange([(mod_part.reshape(N_DEV, B, n_ada), True)], name="scatter_modulation")
    mod = mod_slots.transpose(1, 0, 2).reshape(B, 1, 6 * D)

    later = [bf(w_glu), bf(w_proj_att), bf(w_proj_ssm), bf(w_out), bf(w_up), w_conv[0], bf(w_down)]
    later_sems = _exchange_start(later, "start_later_weights", gather=True, carry=[mod])
    (mod,) = later_sems[5]

    def late_weights(after):
        _, lands = _exchange_wait(*later_sems[:4], after, name="wait_later_weights")
        g = [lax.dynamic_update_index_in_dim(land, a, me, 0) for land, a in zip(lands, later)]
        more_w = dict(w_glu=g[0].reshape(SSM_WIDTH, SSM_WIDTH), w_proj_att=_slots_to_cols(g[1]),
                      w_proj_ssm=_slots_to_cols(g[2]), w_out=g[3].reshape(D, D), w_up=g[4],
                      w_down=g[6].reshape(D_FF, D))
        return more_w, dict(w_conv=_slots_to_cols(g[5]))

    ab_re, ab_im, f_re, f_im = _s5_params(a_re[0], a_im[0], log_dt[0].reshape(SSM_GROUPS, 1))
    col = lambda a: a.reshape(SSM_COLS, 1)
    b_re2, b_im2 = b_re[0].reshape(SSM_COLS, SSM_GROUP_CH), b_im[0].reshape(SSM_COLS, SSM_GROUP_CH)
    bb_re, bb_im = _s5_input_matrix(col(f_re), col(f_im), b_re2, b_im2)
    slopes = jnp.asarray([2.0 ** (-8.0 * (h + 1) / N_HEADS) for h in range(N_HEADS)], F32)
    P = dict(g_mix=g_mix, g_ffn=g_ffn, g_final=g_final.reshape(1, D), b_gate=b_gate, d_skip=d_skip, b_glu=b_glu,
             b_conv=b_conv, slopes=slopes,
             a_row=_interleave(ab_re.reshape(1, SSM_COLS), ab_im.reshape(1, SSM_COLS)),
             bb_big=_interleave(_block_diag_in(bb_re), _block_diag_in(bb_im)),
             cc_big=_interleave(_block_diag_out(c_re[0]).T, -_block_diag_out(c_im[0]).T).T)

    in_flight = []

    def send_early(grads, carry=()):
        names = list(grads)
        handles = _exchange_start([grads[n] for n in names], "start_gradients_%d" % len(in_flight), gather=False,
                                  carry=carry)
        in_flight.append((names,) + handles[:4])
        return handles[4], handles[5]

    grad_x, d_mod, small = _local_step(x, mod, loss_target, W, late_weights, P, send_early)

    small_list = [small["loss_cols"], small["g_mix"], small["b_gate"], small["g_ab_re"], small["g_ab_im"],
                  _diag_blocks_in(small["d_bb_re"]), _diag_blocks_in(small["d_bb_im"]),
                  _diag_blocks_out(small["d_cc_re"]), -_diag_blocks_out(small["d_cc_im"]),
                  small["g_ffn"], small["b_conv"], small["g_final"], small["d_skip"], small["b_glu"]]
    small_packed, small_offs = _pack(small_list)
    small_all, dmod_slots = _gather_all([small_packed, d_mod.reshape(B, 6 * D)], name="gather_small_gradients")

    out = {}

    def update(name, parts, own=None):
        w2 = args[name][0]
        g, dl, mn, vn = _adamw(w2, args["m_" + name][0], args["v_" + name][0], parts, name="adamw_" + name, own=own)
        for key, val in (("grad_", g), ("delta_", dl), ("new_m_", mn), ("new_v_", vn)):
            out[key + name] = val[None]

    my_slot = me.astype(jnp.int32).reshape(1)
    for i, (names, send_sems, recv_sems, sent, lands) in enumerate(in_flight):
        sent, lands = _exchange_wait(send_sems, recv_sems, sent, lands, dmod_slots, name="wait_gradients_%d" % i)
        for name, own_slots, landed in zip(names, sent, lands):
            update(name, landed, own=(own_slots, my_slot))

    dmod_all = dmod_slots.reshape(N_DEV * B, 6 * D)
    dmod_cols = lax.dynamic_slice(dmod_all, (0, me * n_ada), (N_DEV * B, n_ada))
    d_w_ada, d_b_ada = _ada_bwd(c_all, dmod_all, dmod_cols)
    update("w_ada", d_w_ada[None])

    loss_row, loss_n = small_offs[0]
    small_sum, loss_vec = _sum_parts(small_all, (loss_row, loss_row + loss_n // LANES))
    shapes = [(1, D), (1, D), (1, 2 * D), (SSM_GROUPS, SSM_STATE), (SSM_GROUPS, SSM_STATE), (SSM_COLS, SSM_GROUP_CH),
              (SSM_COLS, SSM_GROUP_CH), (1, SSM_GROUPS, SSM_GROUP_CH, SSM_STATE),
              (1, SSM_GROUPS, SSM_GROUP_CH, SSM_STATE), (1, D), (1, D_FF), (D,), (1, SSM_WIDTH), (1, SSM_WIDTH)]
    (_, s_g_mix, s_b_gate, s_ab_re, s_ab_im, s_bb_re, s_bb_im, s_c_re, s_c_im, s_g_ffn, s_b_conv, s_g_final,
     s_d_skip, s_b_glu) = _unpack(small_sum, small_offs, shapes)
    d_b_re2, d_b_im2, d_f_re, d_f_im = _s5_input_matrix_bwd(col(f_re), col(f_im), b_re2, b_im2, s_bb_re, s_bb_im)
    d_a_re, d_a_im, d_log_dt = _s5_params_bwd(a_re[0], a_im[0], log_dt[0].reshape(SSM_GROUPS, 1), s_ab_re, s_ab_im,
                                              d_f_re.reshape(SSM_GROUPS, SSM_STATE),
                                              d_f_im.reshape(SSM_GROUPS, SSM_STATE))
    grads_small = dict(b_ada=d_b_ada, g_mix=s_g_mix, b_gate=s_b_gate, a_re=d_a_re[None], a_im=d_a_im[None],
                       log_dt=d_log_dt.reshape(1, SSM_GROUPS), b_re=d_b_re2.reshape(b_re.shape),
                       b_im=d_b_im2.reshape(b_im.shape), c_re=s_c_re, c_im=s_c_im, d_skip=s_d_skip, b_glu=s_b_glu,
                       g_ffn=s_g_ffn, b_conv=s_b_conv, g_final=s_g_final)
    w_pack, offs = _pack([args[n] for n in SMALL_ORDER])
    m_pack, _ = _pack([args["m_" + n] for n in SMALL_ORDER])
    v_pack, _ = _pack([args["v_" + n] for n in SMALL_ORDER])
    g_pack, _ = _pack([grads_small[n] for n in SMALL_ORDER])
    res = _adamw(w_pack, m_pack, v_pack, g_pack[None], name="adamw_small")
    shapes_small = [args[n].shape for n in SMALL_ORDER]
    for key, packed in zip(("grad_", "delta_", "new_m_", "new_v_"), res):
        for n, val in zip(SMALL_ORDER, _unpack(packed, offs, shapes_small)):
            out[key + n] = val

    order = ["w_ada", "b_ada", "g_mix", "w_in", "b_gate", "a_re", "a_im", "log_dt", "b_re", "b_im", "c_re", "c_im",
             "d_skip", "w_glu", "b_glu", "w_proj_att", "w_proj_ssm", "w_out", "g_ffn", "w_up", "w_conv", "b_conv",
             "w_down", "g_final"]
    loss = loss_vec[0, 0]
    return (loss, grad_x, *[out[k + n] for k in ("grad_", "delta_", "new_m_", "new_v_") for n in order])
```

```python
import math

import jax
import jax.numpy as jnp
from jax import lax
from jax.experimental import pallas as pl
from jax.experimental.pallas import tpu as pltpu

F32 = jnp.float32
BF16 = jnp.bfloat16

N_DEV = 8
D_MODEL = 1024
N_HEADS = 8
HEAD_DIM = 64
ATT_WIDTH = N_HEADS * HEAD_DIM
DILATIONS = (1, 4, 16)
WIN = 128
SSM_GROUPS = 16
SSM_GROUP_CH = 16
SSM_WIDTH = SSM_GROUPS * SSM_GROUP_CH
SSM_STATE = 64
SSM_COLS = SSM_GROUPS * SSM_STATE
D_FF = 2048
EPS = 1e-6
NEG_INF = -1e30
ADAM_LR, ADAM_B1, ADAM_B2, ADAM_EPS, ADAM_WD, ADAM_STEP = 0.001, 0.9, 0.999, 1e-08, 0.01, 10

V7X_VMEM_LIMIT = 56 * 1024 * 1024
LANES = 128


def _params(n_grid):
    return pltpu.CompilerParams(dimension_semantics=("arbitrary",) * n_grid,
                                vmem_limit_bytes=V7X_VMEM_LIMIT)


def _tile(n, pref):
    if n <= pref:
        return n
    t = (pref // LANES) * LANES
    while t > 0:
        if n % t == 0:
            return t
        t -= LANES
    return n


def _matmul(a, b, *, ta=False, tb=False, out_dtype=F32, name):
    if ta:
        K, M = a.shape
    else:
        M, K = a.shape
    if tb:
        N, K2 = b.shape
    else:
        K2, N = b.shape
    assert K == K2, (a.shape, b.shape)
    if ta:
        tm, tn, tk = _tile(M, 1024), _tile(N, 2048), _tile(K, 512)
    else:
        tm, tk = _tile(M, 512), _tile(K, 4096)
        tn = _tile(N, 2048 if K <= 2048 else 1024)
    nk = K // tk
    dn = (((0,) if ta else (1,), (1,) if tb else (0,)), ((), ()))

    def body(a_ref, b_ref, o_ref, acc_ref):
        k = pl.program_id(2)
        part = lax.dot_general(a_ref[...].astype(BF16), b_ref[...].astype(BF16), dn, preferred_element_type=F32)
        if nk == 1:
            o_ref[...] = part.astype(o_ref.dtype)
            return

        @pl.when(k == 0)
        def _():
            acc_ref[...] = jnp.zeros_like(acc_ref)

        acc_ref[...] += part

        @pl.when(k == nk - 1)
        def _():
            o_ref[...] = acc_ref[...].astype(o_ref.dtype)

    a_spec = (pl.BlockSpec((tk, tm), lambda j, i, k: (k, i)) if ta
              else pl.BlockSpec((tm, tk), lambda j, i, k: (i, k)))
    b_spec = (pl.BlockSpec((tn, tk), lambda j, i, k: (j, k)) if tb
              else pl.BlockSpec((tk, tn), lambda j, i, k: (k, j)))
    return pl.pallas_call(
        body, name=name, grid=(N // tn, M // tm, nk),
        in_specs=[a_spec, b_spec],
        out_specs=pl.BlockSpec((tm, tn), lambda j, i, k: (i, j)),
        out_shape=jax.ShapeDtypeStruct((M, N), out_dtype),
        scratch_shapes=[pltpu.VMEM((tm, tn) if nk > 1 else (8, LANES), F32)],
        compiler_params=_params(3),
    )(a, b)


HALF = 256
UP_SLOTS = N_DEV // 2
UP_GROUP = 4 * HALF


def _group_weight(w_ref):
    return jnp.concatenate([w_ref[0, :, :HALF], w_ref[1, :, :HALF], w_ref[0, :, HALF:], w_ref[1, :, HALF:]], axis=1)


def _up_weight_spec(K, index):
    return pl.BlockSpec((2, None, K, 2 * HALF), index)


def _up_fwd(a, w3, name):
    M, K = a.shape
    tm = _tile(M, 1024)

    def body(a_ref, w_ref, o_ref):
        o_ref[...] = jnp.dot(a_ref[...].astype(BF16), _group_weight(w_ref),
                             preferred_element_type=F32).astype(o_ref.dtype)

    return pl.pallas_call(
        body, name=name, grid=(UP_SLOTS, M // tm),
        in_specs=[pl.BlockSpec((tm, K), lambda j, i: (i, 0)), _up_weight_spec(K, lambda j, i: (0, j, 0, 0))],
        out_specs=pl.BlockSpec((tm, UP_GROUP), lambda j, i: (i, j)),
        out_shape=jax.ShapeDtypeStruct((M, UP_SLOTS * UP_GROUP), BF16), compiler_params=_params(2),
    )(a, w3.reshape(2, UP_SLOTS, K, 2 * HALF))


def _up_dx(d, w3, name):
    M = d.shape[0]
    K = w3.shape[1]
    tm = _tile(M, 1024)

    def body(d_ref, w_ref, o_ref, acc_ref):
        j = pl.program_id(1)

        @pl.when(j == 0)
        def _():
            acc_ref[...] = jnp.zeros_like(acc_ref)

        acc_ref[...] += lax.dot_general(d_ref[...], _group_weight(w_ref), _NT, preferred_element_type=F32)

        @pl.when(j == UP_SLOTS - 1)
        def _():
            o_ref[...] = acc_ref[...].astype(o_ref.dtype)

    return pl.pallas_call(
        body, name=name, grid=(M // tm, UP_SLOTS),
        in_specs=[pl.BlockSpec((tm, UP_GROUP), lambda i, j: (i, j)), _up_weight_spec(K, lambda i, j: (0, j, 0, 0))],
        out_specs=pl.BlockSpec((tm, K), lambda i, j: (i, 0)),
        out_shape=jax.ShapeDtypeStruct((M, K), BF16), scratch_shapes=[pltpu.VMEM((tm, K), F32)],
        compiler_params=_params(2),
    )(d, w3.reshape(2, UP_SLOTS, K, 2 * HALF))


def _up_dw(a, d, name):
    M, K = a.shape
    tk = _tile(M, 512)
    nk = M // tk

    def body(a_ref, d_ref, o_ref, acc_ref):
        k = pl.program_id(1)

        @pl.when(k == 0)
        def _():
            acc_ref[...] = jnp.zeros_like(acc_ref)

        acc_ref[...] += lax.dot_general(a_ref[...], d_ref[...], _TN, preferred_element_type=F32)

        @pl.when(k == nk - 1)
        def _():
            for half in range(2):
                for part in range(2):
                    lo = (2 * half + part) * HALF
                    o_ref[part, :, half * HALF:(half + 1) * HALF] = acc_ref[:, lo:lo + HALF].astype(o_ref.dtype)

    out = pl.pallas_call(
        body, name=name, grid=(UP_SLOTS, nk),
        in_specs=[pl.BlockSpec((tk, K), lambda j, k: (k, 0)), pl.BlockSpec((tk, UP_GROUP), lambda j, k: (k, j))],
        out_specs=_up_weight_spec(K, lambda j, k: (0, j, 0, 0)),
        out_shape=jax.ShapeDtypeStruct((2, UP_SLOTS, K, 2 * HALF), BF16),
        scratch_shapes=[pltpu.VMEM((K, UP_GROUP), F32)], compiler_params=_params(2),
    )(a, d)
    return out.reshape(N_DEV, K, 2 * HALF)


def _rowwise(fn, rows, bvecs, consts, out_rows, out_b, out_g, *, ts, name):
    B, S = rows[0][0].shape[:2]
    nin = len(rows) + len(bvecs) + len(consts)
    nr, nb, ng = len(out_rows), len(out_b), len(out_g)

    def body(*refs):
        b = pl.program_id(0)
        s = pl.program_id(1)
        vals = [r[...] for r in refs[:nin]]
        vals[:len(rows)] = [v.astype(F32) for v in vals[:len(rows)]]
        outs = fn(*vals)
        if not isinstance(outs, (tuple, list)):
            outs = (outs,)
        orefs = refs[nin:]
        for i in range(nr):
            orefs[i][...] = outs[i].astype(orefs[i].dtype)
        for i in range(nb):
            ref = orefs[nr + i]

            @pl.when(s == 0)
            def _(ref=ref):
                ref[...] = jnp.zeros_like(ref)

            ref[...] += outs[nr + i]
        for i in range(ng):
            ref = orefs[nr + nb + i]

            @pl.when((s == 0) & (b == 0))
            def _(ref=ref):
                ref[...] = jnp.zeros_like(ref)

            ref[...] += outs[nr + nb + i]

    in_specs = ([pl.BlockSpec((None, ts, cb), lambda b, s, ci=ci: (b, s, ci)) for (_, cb, ci) in rows]
                + [pl.BlockSpec((None, 1, cb), lambda b, s, ci=ci: (b, 0, ci)) for (_, cb, ci) in bvecs]
                + [pl.BlockSpec(a.shape, lambda b, s: (0, 0)) for a in consts])
    out_shape = ([jax.ShapeDtypeStruct((B, S, c), dt) for (c, dt) in out_rows]
                 + [jax.ShapeDtypeStruct((B, 1, c), F32) for c in out_b]
                 + [jax.ShapeDtypeStruct(rc, F32) for rc in out_g])
    out_specs = ([pl.BlockSpec((None, ts, c), lambda b, s: (b, s, 0)) for (c, _) in out_rows]
                 + [pl.BlockSpec((None, 1, c), lambda b, s: (b, 0, 0)) for c in out_b]
                 + [pl.BlockSpec(rc, lambda b, s: (0, 0)) for rc in out_g])
    args = [a for (a, _, _) in rows] + [a for (a, _, _) in bvecs] + list(consts)
    return pl.pallas_call(
        body, name=name, grid=(B, S // ts), in_specs=in_specs, out_specs=out_specs,
        out_shape=out_shape, compiler_params=_params(2),
    )(*args)


def _col_sum(v):
    return jnp.sum(v, axis=0, keepdims=True)


def _rms_scale(h):
    return lax.rsqrt(jnp.mean(h * h, axis=-1, keepdims=True) + EPS)


def _sigmoid(v):
    return 1.0 / (1.0 + jnp.exp(-v))


ATT_SCALE = HEAD_DIM ** -0.5
COPY_ROWS = 256
_NT = (((1,), (1,)), ((), ()))
_TN = (((0,), (0,)), ((), ()))


def _row_chunks(d, seq):
    sub = seq // d
    out = []
    for r in range(d):
        for c0 in range(0, sub, COPY_ROWS):
            n = min(COPY_ROWS, sub - c0)
            out.append((pl.ds(r + c0 * d, n, stride=d), r * sub + c0, n))
    return out


ATT_UNROLL = 8
KEYS = 2 * WIN


def _zero_once(refs):
    @pl.when((pl.program_id(0) == 0) & (pl.program_id(1) == 0))
    def _():
        for r in refs:
            r[...] = jnp.zeros_like(r)


def _pair_bias(bias_ref, slopes_ref, hp, d, key_major):
    shape = (KEYS, WIN) if key_major else (WIN, KEYS)
    qi = lax.broadcasted_iota(jnp.int32, shape, 1 if key_major else 0)
    kj = lax.broadcasted_iota(jnp.int32, shape, 0 if key_major else 1)
    dist = WIN + qi - kj
    valid = (dist >= 0) & (dist <= WIN)
    distf = dist.astype(F32)
    for h in range(2):
        slope_d = slopes_ref[2 * hp + h] * float(d)
        with_prev = jnp.where(valid, -(slope_d * distf), NEG_INF)
        no_prev = jnp.where(kj >= WIN, with_prev, NEG_INF)
        span = slice(h * KEYS, (h + 1) * KEYS)
        if key_major:
            bias_ref[1, span, :] = with_prev
            bias_ref[0, span, :] = no_prev
        else:
            bias_ref[1, :, span] = with_prev
            bias_ref[0, :, span] = no_prev


def _stack_heads(v):
    first = lax.broadcasted_iota(jnp.int32, v.shape, 1) < HEAD_DIM
    zero = jnp.zeros_like(v)
    return jnp.concatenate([jnp.where(first, v, zero), jnp.where(first, zero, v)], axis=0)


def _per_head(c0, c1, n):
    return jnp.where(lax.broadcasted_iota(jnp.int32, (n, LANES), 1) < HEAD_DIM, c0, c1)


def _qkv_spec(seq, j):
    return pl.BlockSpec((None, seq, LANES), lambda b, hp: (b, 0, 3 * hp + j))


def _attention_fwd(qkv, slopes):
    B, S, _ = qkv.shape
    n_blk = S // WIN
    n_pair = N_HEADS // 2

    def body(slopes_ref, q_ref, k_ref, v_ref, o_ref, lse_ref, qp, kp, vp, bias, acc, mx, sm, acc_n, mx_n, sm_n):
        hp = pl.program_id(1)
        _zero_once((kp, vp))
        for p, d in enumerate(DILATIONS):
            nb = n_blk // d
            chunks = _row_chunks(d, S)
            for src, dst, n in chunks:
                qp[dst:dst + n, :] = (q_ref[src, :] * ATT_SCALE).astype(BF16)
                kp[WIN + dst:WIN + dst + n, :] = k_ref[src, :].astype(BF16)
                vp[WIN + dst:WIN + dst + n, :] = v_ref[src, :].astype(BF16)
            _pair_bias(bias, slopes_ref, hp, d, key_major=False)
            acc_t, mx_t, sm_t = (acc_n, mx_n, sm_n) if d == 1 else (acc, mx, sm)

            def block(i, carry, p=p, nb=nb, acc_t=acc_t, mx_t=mx_t, sm_t=sm_t):
                cur = pl.ds(pl.multiple_of(i * WIN, WIN), WIN)
                keys = pl.ds(pl.multiple_of(i * WIN, WIN), KEYS)
                flag = ((i % nb) > 0).astype(jnp.int32)
                s = lax.dot_general(qp[cur, :], _stack_heads(kp[keys, :]), _NT, preferred_element_type=F32)
                s = s + bias[flag]
                es, ms, ls = [], [], []
                for h in range(2):
                    sh = s[:, h * KEYS:(h + 1) * KEYS]
                    m = jnp.max(jnp.maximum(sh[:, :WIN], sh[:, WIN:]), axis=1, keepdims=True)
                    e = jnp.exp(sh - m)
                    es.append(e.astype(BF16))
                    ms.append(m)
                    ls.append(jnp.sum(e[:, :WIN] + e[:, WIN:], axis=1, keepdims=True))
                acc_t[p, cur, :] = jnp.dot(jnp.concatenate(es, axis=1), _stack_heads(vp[keys, :]),
                                           preferred_element_type=F32)
                mx_t[p, cur, :] = _per_head(ms[0], ms[1], WIN)
                sm_t[p, cur, :] = _per_head(ls[0], ls[1], WIN)
                return carry

            lax.fori_loop(0, n_blk, block, 0, unroll=ATT_UNROLL)
            if d > 1:
                for src, dst, n in chunks:
                    acc_n[p, src, :] = acc[p, dst:dst + n, :]
                    mx_n[p, src, :] = mx[p, dst:dst + n, :]
                    sm_n[p, src, :] = sm[p, dst:dst + n, :]

        chunk = 256

        def merge(i, carry):
            rows = pl.ds(pl.multiple_of(i * chunk, chunk), chunk)
            ms = [mx_n[p, rows, :] for p in range(3)]
            m = jnp.maximum(jnp.maximum(ms[0], ms[1]), ms[2])
            ws = [jnp.exp(mp - m) for mp in ms]
            l = ws[0] * sm_n[0, rows, :] + ws[1] * sm_n[1, rows, :] + ws[2] * sm_n[2, rows, :]
            o = (ws[0] * acc_n[0, rows, :] + ws[1] * acc_n[1, rows, :] + ws[2] * acc_n[2, rows, :]) / l
            o_ref[rows, :] = o.astype(o_ref.dtype)
            lse = m + jnp.log(l)
            for h in range(2):
                lse_ref[rows, h:h + 1] = lse[:, h * HEAD_DIM:h * HEAD_DIM + 1]
            return carry

        lax.fori_loop(0, S // chunk, merge, 0)

    return pl.pallas_call(
        body, name="attention_fwd", grid=(B, n_pair),
        in_specs=[pl.BlockSpec(memory_space=pltpu.SMEM), _qkv_spec(S, 0), _qkv_spec(S, 1), _qkv_spec(S, 2)],
        out_specs=[pl.BlockSpec((None, S, LANES), lambda b, hp: (b, 0, hp)),
                   pl.BlockSpec((None, None, S, 2), lambda b, hp: (b, hp, 0, 0))],
        out_shape=[jax.ShapeDtypeStruct((B, S, ATT_WIDTH), BF16),
                   jax.ShapeDtypeStruct((B, n_pair, S, 2), F32)],
        scratch_shapes=[pltpu.VMEM((S, LANES), BF16), pltpu.VMEM((S + WIN, LANES), BF16),
                        pltpu.VMEM((S + WIN, LANES), BF16), pltpu.VMEM((2, WIN, 2 * KEYS), F32)]
        + [pltpu.VMEM((3, S, LANES), F32)] * 6,
        compiler_params=_params(2),
    )(slopes, qkv, qkv, qkv)


def _attention_bwd(qkv, o, do, lse, slopes):
    B, S, _ = qkv.shape
    n_blk = S // WIN
    n_pair = N_HEADS // 2

    def body(slopes_ref, q_ref, k_ref, v_ref, o_ref, do_ref, lse_ref, dx_ref,
             qp, dop, kp, vp, aux, auxp, aux_t, bias_t, dqp, dvk, dq_n, dk_n, dv_n):
        hp = pl.program_id(1)
        aux[...] = jnp.zeros_like(aux)
        for c0 in range(0, S, COPY_ROWS):
            rows = slice(c0, c0 + COPY_ROWS)
            prod = do_ref[rows, :] * o_ref[rows, :].astype(F32)
            for h in range(2):
                aux[rows, 2 * h:2 * h + 1] = lse_ref[rows, h:h + 1]
                aux[rows, 2 * h + 1:2 * h + 2] = jnp.sum(prod[:, h * HEAD_DIM:(h + 1) * HEAD_DIM], axis=1,
                                                         keepdims=True)
        dq_n[...] = jnp.zeros_like(dq_n)
        dk_n[...] = jnp.zeros_like(dk_n)
        dv_n[...] = jnp.zeros_like(dv_n)
        _zero_once((kp, vp))
        for p, d in enumerate(DILATIONS):
            nb = n_blk // d
            chunks = _row_chunks(d, S)
            for src, dst, n in chunks:
                auxp[dst:dst + n, :] = aux[src, :]
                qp[dst:dst + n, :] = (q_ref[src, :] * ATT_SCALE).astype(BF16)
                dop[dst:dst + n, :] = do_ref[src, :].astype(BF16)
                kp[WIN + dst:WIN + dst + n, :] = k_ref[src, :].astype(BF16)
                vp[WIN + dst:WIN + dst + n, :] = v_ref[src, :].astype(BF16)
            for i in range(n_blk):
                aux_t[i] = auxp[i * WIN:(i + 1) * WIN, :].T[0:8, :]
            _pair_bias(bias_t, slopes_ref, hp, d, key_major=True)
            dvk[...] = jnp.zeros_like(dvk)

            def block(i, carry, nb=nb):
                cur = pl.ds(pl.multiple_of(i * WIN, WIN), WIN)
                keys = pl.ds(pl.multiple_of(i * WIN, WIN), KEYS)
                flag = ((i % nb) > 0).astype(jnp.int32)
                q2, do2 = qp[cur, :], dop[cur, :]
                kc = _stack_heads(kp[keys, :])
                s_t = lax.dot_general(kc, q2, _NT, preferred_element_type=F32) + bias_t[flag]
                dp_t = lax.dot_general(_stack_heads(vp[keys, :]), do2, _NT, preferred_element_type=F32)
                ps, dss = [], []
                for h in range(2):
                    span = slice(h * KEYS, (h + 1) * KEYS)
                    p_t = jnp.exp(s_t[span, :] - aux_t[i, 2 * h:2 * h + 1, :])
                    ds_t = p_t * (dp_t[span, :] - aux_t[i, 2 * h + 1:2 * h + 2, :])
                    ps.append(p_t.astype(BF16))
                    dss.append(ds_t.astype(BF16))
                do_rows, q_rows = _stack_heads(do2), _stack_heads(q2)
                zr = jnp.zeros_like(do_rows)
                rhs = jnp.concatenate([jnp.concatenate([do_rows, zr], axis=1),
                                       jnp.concatenate([zr, q_rows], axis=1)], axis=0)
                dvk[keys, :] += jnp.dot(jnp.concatenate(ps + dss, axis=1), rhs, preferred_element_type=F32)
                dqp[cur, :] = lax.dot_general(jnp.concatenate(dss, axis=0), kc, _TN, preferred_element_type=F32)
                return carry

            lax.fori_loop(0, n_blk, block, 0, unroll=ATT_UNROLL)
            for src, dst, n in chunks:
                dq_n[src, :] += dqp[dst:dst + n, :]
                dv_n[src, :] += dvk[WIN + dst:WIN + dst + n, :LANES]
                dk_n[src, :] += dvk[WIN + dst:WIN + dst + n, LANES:]
        for c0 in range(0, S, COPY_ROWS):
            rows = slice(c0, c0 + COPY_ROWS)
            dx_ref[rows, 0:LANES] = (dq_n[rows, :] * ATT_SCALE).astype(dx_ref.dtype)
            dx_ref[rows, LANES:2 * LANES] = dk_n[rows, :].astype(dx_ref.dtype)
            dx_ref[rows, 2 * LANES:3 * LANES] = dv_n[rows, :].astype(dx_ref.dtype)

    pair = lambda width: pl.BlockSpec((None, S, width), lambda b, hp: (b, 0, hp))
    vm = lambda shape, dt: pltpu.VMEM(shape, dt)
    return pl.pallas_call(
        body, name="attention_bwd", grid=(B, n_pair),
        in_specs=[pl.BlockSpec(memory_space=pltpu.SMEM), _qkv_spec(S, 0), _qkv_spec(S, 1), _qkv_spec(S, 2),
                  pair(LANES), pair(LANES), pl.BlockSpec((None, None, S, 2), lambda b, hp: (b, hp, 0, 0))],
        out_specs=pair(3 * LANES),
        out_shape=jax.ShapeDtypeStruct((B, S, 3 * ATT_WIDTH), BF16),
        scratch_shapes=[vm((S, LANES), BF16), vm((S, LANES), BF16),
                        vm((S + WIN, LANES), BF16), vm((S + WIN, LANES), BF16),
                        vm((S, LANES), F32), vm((S, LANES), F32), vm((n_blk, 8, WIN), F32),
                        vm((2, 2 * KEYS, WIN), F32),
                        vm((S, LANES), F32), vm((S + WIN, 2 * LANES), F32),
                        vm((S, LANES), F32), vm((S, LANES), F32), vm((S, LANES), F32)],
        compiler_params=_params(2),
    )(slopes, qkv, qkv, qkv, o, do, lse)


SCAN_COLS = 256
SCAN_ROWS = 8


def _rows_to_tile(rows):
    rid = lax.broadcasted_iota(jnp.int32, (SCAN_ROWS, rows[0].shape[1]), 0)
    tile = jnp.broadcast_to(rows[0], rid.shape)
    for k in range(1, SCAN_ROWS):
        tile = jnp.where(rid == k, rows[k], tile)
    return tile


SCAN_UNROLL = 4


def _complex_powers(ar, ai, n):
    out = [(ar, ai)]
    for _ in range(n - 1):
        pr, pi = out[-1]
        out.append((pr * ar - pi * ai, pr * ai + pi * ar))
    return out


def _round_multipliers(powers, rid, reverse):
    out = []
    for s in (1, 2, 4):
        keep = (rid < SCAN_ROWS - s) if reverse else (rid >= s)
        out.append((jnp.where(keep, powers[s - 1][0], 0.0), jnp.where(keep, powers[s - 1][1], 0.0)))
    return out


def _tile_scan(xr, xi, multipliers, reverse):
    for s, (mr, mi) in zip((1, 2, 4), multipliers):
        shift = SCAN_ROWS - s if reverse else s
        sr, si = pltpu.roll(xr, shift, 0), pltpu.roll(xi, shift, 0)
        xr, xi = xr + (mr * sr - mi * si), xi + (mr * si + mi * sr)
    return xr, xi


SCAN_CHUNK = 256


def _scan_fwd(us, bb_big, a_row, cc_big):
    B, S, _ = us.shape
    groups = 2
    width = 2 * groups * SCAN_COLS
    nc = 2 * SSM_COLS // width
    nt = S // SCAN_ROWS
    tiles = SCAN_CHUNK // SCAN_ROWS
    LAST = slice(SCAN_ROWS - 1, SCAN_ROWS)

    def body(us_ref, bb_ref, a_ref, cc_ref, xs_ref, y_ref, bu_ref):
        bb = bb_ref[...].astype(BF16)
        for c in range(S // SCAN_CHUNK):
            part = jnp.dot(us_ref[c * SCAN_CHUNK:(c + 1) * SCAN_CHUNK, :].astype(BF16), bb,
                           preferred_element_type=F32)
            bu_ref[c * tiles:(c + 1) * tiles] = part.reshape(tiles, SCAN_ROWS, width)
        rid = lax.broadcasted_iota(jnp.int32, (SCAN_ROWS, SCAN_COLS), 0)
        consts = []
        for g in range(groups):
            re = slice(2 * g * SCAN_COLS, (2 * g + 1) * SCAN_COLS)
            im = slice((2 * g + 1) * SCAN_COLS, (2 * g + 2) * SCAN_COLS)
            powers = _complex_powers(a_ref[:, re], a_ref[:, im], SCAN_ROWS)
            carry_mult = (_rows_to_tile([p[0] for p in powers]), _rows_to_tile([p[1] for p in powers]))
            consts.append((re, im, carry_mult, _round_multipliers(powers, rid, reverse=False)))

        def tile(i, carry):
            out = []
            for (re, im, (cr_t, ci_t), rounds), (cr, ci) in zip(consts, carry):
                xr, xi = _tile_scan(bu_ref[i, :, re], bu_ref[i, :, im], rounds, reverse=False)
                xs_ref[i, :, re] = xr + (cr_t * cr - ci_t * ci)
                xs_ref[i, :, im] = xi + (cr_t * ci + ci_t * cr)
                out.append((xs_ref[i, LAST, re], xs_ref[i, LAST, im]))
            return tuple(out)

        zero = jnp.zeros((1, SCAN_COLS), F32)
        lax.fori_loop(0, nt, tile, ((zero, zero),) * groups, unroll=SCAN_UNROLL)

        @pl.when(pl.program_id(1) == 0)
        def _():
            y_ref[...] = jnp.zeros_like(y_ref)

        cc = cc_ref[...].astype(BF16)
        for c in range(S // SCAN_CHUNK):
            x2 = xs_ref[c * tiles:(c + 1) * tiles].reshape(SCAN_CHUNK, width).astype(BF16)
            y_ref[c * SCAN_CHUNK:(c + 1) * SCAN_CHUNK, :] += jnp.dot(x2, cc, preferred_element_type=F32)

    col = pl.BlockSpec((None, nt, SCAN_ROWS, width), lambda b, j: (b, 0, 0, j))
    tok = pl.BlockSpec((None, S, SSM_WIDTH), lambda b, j: (b, 0, 0))
    xs, y = pl.pallas_call(
        body, name="s5_scan_fwd", grid=(B, nc),
        in_specs=[tok, pl.BlockSpec((SSM_WIDTH, width), lambda b, j: (0, j)),
                  pl.BlockSpec((1, width), lambda b, j: (0, j)), pl.BlockSpec((width, SSM_WIDTH), lambda b, j: (j, 0))],
        out_specs=[col, tok],
        out_shape=[jax.ShapeDtypeStruct((B, nt, SCAN_ROWS, 2 * SSM_COLS), F32),
                   jax.ShapeDtypeStruct((B, S, SSM_WIDTH), F32)],
        scratch_shapes=[pltpu.VMEM((nt, SCAN_ROWS, width), F32)],
        compiler_params=_params(2),
    )(us, bb_big, a_row, cc_big)
    return xs.reshape(B, S, 2 * SSM_COLS), y


def _scan_bwd(dy, cc_big, xs, a_row):
    B, S, _ = dy.shape
    width = 2 * SCAN_COLS
    nc = SSM_COLS // SCAN_COLS
    nt = S // SCAN_ROWS
    tiles = SCAN_CHUNK // SCAN_ROWS
    RE, IM = slice(0, SCAN_COLS), slice(SCAN_COLS, 2 * SCAN_COLS)
    FIRST, LAST = slice(0, 1), slice(SCAN_ROWS - 1, SCAN_ROWS)

    def body(dy_ref, cc_ref, x_ref, a_ref, lam_ref, ga_ref, d_ref):
        b = pl.program_id(1)
        cc = cc_ref[...].astype(BF16)
        for c in range(S // SCAN_CHUNK):
            part = lax.dot_general(dy_ref[c * SCAN_CHUNK:(c + 1) * SCAN_CHUNK, :].astype(BF16), cc, _NT,
                                   preferred_element_type=F32)
            d_ref[c * tiles:(c + 1) * tiles] = part.reshape(tiles, SCAN_ROWS, width)
        powers = _complex_powers(a_ref[:, RE], -a_ref[:, IM], SCAN_ROWS)
        rid = lax.broadcasted_iota(jnp.int32, (SCAN_ROWS, SCAN_COLS), 0)
        cr_t = _rows_to_tile([powers[SCAN_ROWS - 1 - r][0] for r in range(SCAN_ROWS)])
        ci_t = _rows_to_tile([powers[SCAN_ROWS - 1 - r][1] for r in range(SCAN_ROWS)])
        rounds = _round_multipliers(powers, rid, reverse=True)

        @pl.when(b == 0)
        def _():
            ga_ref[...] = jnp.zeros_like(ga_ref)

        def tile(j, carry):
            cr, ci, accr, acci = carry
            i = nt - 1 - j
            lr, li = _tile_scan(d_ref[i, :, RE], d_ref[i, :, IM], rounds, reverse=True)
            lam_r = lr + (cr_t * cr - ci_t * ci)
            lam_i = li + (cr_t * ci + ci_t * cr)
            lam_ref[i, :, RE] = lam_r
            lam_ref[i, :, IM] = lam_i
            ip = jnp.maximum(i - 1, 0)
            keep = (i > 0).astype(F32)
            xpr = jnp.where(rid == 0, x_ref[ip, LAST, RE] * keep, pltpu.roll(x_ref[i, :, RE], 1, 0))
            xpi = jnp.where(rid == 0, x_ref[ip, LAST, IM] * keep, pltpu.roll(x_ref[i, :, IM], 1, 0))
            accr = accr + lam_r * xpr + lam_i * xpi
            acci = acci + lam_i * xpr - lam_r * xpi
            return lam_ref[i, FIRST, RE], lam_ref[i, FIRST, IM], accr, acci

        z1 = jnp.zeros((1, SCAN_COLS), F32)
        z8 = jnp.zeros((SCAN_ROWS, SCAN_COLS), F32)
        _, _, accr, acci = lax.fori_loop(0, nt, tile, (z1, z1, z8, z8), unroll=SCAN_UNROLL)
        ga_ref[:, RE] += _col_sum(accr)
        ga_ref[:, IM] += _col_sum(acci)

    col = pl.BlockSpec((None, nt, SCAN_ROWS, width), lambda j, b: (b, 0, 0, j))
    par = pl.BlockSpec((1, width), lambda j, b: (0, j))
    lam, g_a = pl.pallas_call(
        body, name="s5_scan_bwd", grid=(nc, B),
        in_specs=[pl.BlockSpec((None, S, SSM_WIDTH), lambda j, b: (b, 0, 0)),
                  pl.BlockSpec((width, SSM_WIDTH), lambda j, b: (j, 0)), col, par],
        out_specs=[col, par],
        out_shape=[jax.ShapeDtypeStruct((B, nt, SCAN_ROWS, 2 * SSM_COLS), F32),
                   jax.ShapeDtypeStruct((1, 2 * SSM_COLS), F32)],
        scratch_shapes=[pltpu.VMEM((nt, SCAN_ROWS, width), F32)],
        compiler_params=_params(2),
    )(dy, cc_big, xs.reshape(B, nt, SCAN_ROWS, 2 * SSM_COLS), a_row)
    return lam.reshape(B, S, 2 * SSM_COLS), g_a


def _s5_discretise(lr, li, log_dt):
    dt = jnp.exp(log_dt)
    mag = jnp.exp(lr * dt)
    ang = li * dt
    ab_re, ab_im = mag * jnp.cos(ang), mag * jnp.sin(ang)
    nr, ni = ab_re - 1.0, ab_im
    den = lr * lr + li * li
    f_re = (nr * lr + ni * li) / den
    f_im = (ni * lr - nr * li) / den
    return dt, ab_re, ab_im, nr, ni, den, f_re, f_im


def _s5_params(a_re, a_im, log_dt):
    def body(lr_ref, li_ref, ld_ref, abr, abi, fr, fi):
        _, ab_re, ab_im, _, _, _, f_re, f_im = _s5_discretise(lr_ref[...], li_ref[...], ld_ref[...])
        abr[...] = ab_re
        abi[...] = ab_im
        fr[...] = f_re
        fi[...] = f_im

    return pl.pallas_call(body, name="s5_params",
                          out_shape=[jax.ShapeDtypeStruct(a_re.shape, F32)] * 4)(a_re, a_im, log_dt)


def _s5_input_matrix(f_re, f_im, b_re, b_im):
    def body(fr, fi, br, bi, o_re, o_im):
        o_re[...] = fr[...] * br[...] - fi[...] * bi[...]
        o_im[...] = fr[...] * bi[...] + fi[...] * br[...]

    return pl.pallas_call(body, name="s5_input_matrix",
                          out_shape=[jax.ShapeDtypeStruct(b_re.shape, F32)] * 2)(f_re, f_im, b_re, b_im)


def _s5_input_matrix_bwd(f_re, f_im, b_re, b_im, g_re, g_im):
    def body(fr, fi, br, bi, gr, gi, dbr, dbi, dfr, dfi):
        dbr[...] = fr[...] * gr[...] + fi[...] * gi[...]
        dbi[...] = fr[...] * gi[...] - fi[...] * gr[...]
        dfr[...] = jnp.sum(br[...] * gr[...] + bi[...] * gi[...], axis=1, keepdims=True)
        dfi[...] = jnp.sum(br[...] * gi[...] - bi[...] * gr[...], axis=1, keepdims=True)

    return pl.pallas_call(
        body, name="s5_input_matrix_bwd",
        out_shape=[jax.ShapeDtypeStruct(b_re.shape, F32)] * 2 + [jax.ShapeDtypeStruct(f_re.shape, F32)] * 2,
    )(f_re, f_im, b_re, b_im, g_re, g_im)


def _s5_params_bwd(a_re, a_im, log_dt, g_ab_re, g_ab_im, d_f_re, d_f_im):
    def body(lr_ref, li_ref, ld_ref, gar, gai, dfr, dfi, o_lr, o_li, o_ld):
        lr, li = lr_ref[...], li_ref[...]
        dt, ab_re, ab_im, nr, ni, den, f_re, f_im = _s5_discretise(lr, li, ld_ref[...])
        d_fr, d_fi = dfr[...], dfi[...]
        d_nr = (d_fr * lr - d_fi * li) / den
        d_ni = (d_fr * li + d_fi * lr) / den
        common = (d_fr * f_re + d_fi * f_im) * 2.0 / den
        d_lr = (d_fr * nr + d_fi * ni) / den - common * lr
        d_li = (d_fr * ni - d_fi * nr) / den - common * li
        d_abr = gar[...] + d_nr
        d_abi = gai[...] + d_ni
        d_mag_mag = d_abr * ab_re + d_abi * ab_im
        d_ang = d_abi * ab_re - d_abr * ab_im
        o_lr[...] = d_lr + d_mag_mag * dt
        o_li[...] = d_li + d_ang * dt
        o_ld[...] = jnp.sum(d_mag_mag * lr + d_ang * li, axis=1, keepdims=True) * dt

    return pl.pallas_call(
        body, name="s5_params_bwd",
        out_shape=[jax.ShapeDtypeStruct(a_re.shape, F32)] * 2 + [jax.ShapeDtypeStruct(log_dt.shape, F32)],
    )(a_re, a_im, log_dt, g_ab_re, g_ab_im, d_f_re, d_f_im)


CONV_COLS = 256


def _shift_down(v, j, row):
    return jnp.where(row >= j, pltpu.roll(v, j, 0), 0.0)


def _shift_up(v, j, row, seq):
    return jnp.where(row < seq - j, pltpu.roll(v, seq - j, 0), 0.0)


def _conv_fwd(up, w_conv, b_conv):
    B, S, _ = up.shape
    nj = D_FF // CONV_COLS

    def body(up_ref, w_ref, b_ref, ff_ref):
        a = up_ref[:, :CONV_COLS].astype(F32)
        val = up_ref[:, CONV_COLS:].astype(F32)
        row = lax.broadcasted_iota(jnp.int32, a.shape, 0)
        w0, w1, w2 = w_ref[0:1, :], w_ref[1:2, :], w_ref[2:3, :]
        conv = b_ref[...] + w0 * a + w1 * _shift_down(a, 1, row) + w2 * _shift_down(a, 2, row)
        ff_ref[...] = (conv * _sigmoid(conv) * val).astype(ff_ref.dtype)

    return pl.pallas_call(
        body, name="conv_gate_fwd", grid=(B, nj),
        in_specs=[pl.BlockSpec((None, S, 2 * CONV_COLS), lambda b, j: (b, 0, j)),
                  pl.BlockSpec((3, CONV_COLS), lambda b, j: (0, j)),
                  pl.BlockSpec((1, CONV_COLS), lambda b, j: (0, j))],
        out_specs=pl.BlockSpec((None, S, CONV_COLS), lambda b, j: (b, 0, j)),
        out_shape=jax.ShapeDtypeStruct((B, S, D_FF), BF16),
        compiler_params=_params(2),
    )(up, w_conv, b_conv)


def _conv_bwd(up, d_ff, w_conv, b_conv):
    B, S, _ = up.shape
    nj = D_FF // CONV_COLS

    def body(up_ref, dff_ref, w_ref, b_ref, dup_ref, dw_ref, db_ref):
        b = pl.program_id(1)
        a = up_ref[:, :CONV_COLS].astype(F32)
        val = up_ref[:, CONV_COLS:].astype(F32)
        row = lax.broadcasted_iota(jnp.int32, a.shape, 0)
        w0, w1, w2 = w_ref[0:1, :], w_ref[1:2, :], w_ref[2:3, :]
        a1, a2 = _shift_down(a, 1, row), _shift_down(a, 2, row)
        conv = b_ref[...] + w0 * a + w1 * a1 + w2 * a2
        sg = _sigmoid(conv)
        dff = dff_ref[...].astype(F32)
        d_val = dff * conv * sg
        dc = dff * val * (sg * (1.0 + conv * (1.0 - sg)))
        d_a = w0 * dc + w1 * _shift_up(dc, 1, row, S) + w2 * _shift_up(dc, 2, row, S)
        dup_ref[:, :CONV_COLS] = d_a.astype(dup_ref.dtype)
        dup_ref[:, CONV_COLS:] = d_val.astype(dup_ref.dtype)

        @pl.when(b == 0)
        def _():
            dw_ref[...] = jnp.zeros_like(dw_ref)
            db_ref[...] = jnp.zeros_like(db_ref)

        dw_ref[0:1, :] += _col_sum(dc * a)
        dw_ref[1:2, :] += _col_sum(dc * a1)
        dw_ref[2:3, :] += _col_sum(dc * a2)
        db_ref[...] += _col_sum(dc)

    return pl.pallas_call(
        body, name="conv_gate_bwd", grid=(nj, B),
        in_specs=[pl.BlockSpec((None, S, 2 * CONV_COLS), lambda j, b: (b, 0, j)),
                  pl.BlockSpec((None, S, CONV_COLS), lambda j, b: (b, 0, j)),
                  pl.BlockSpec((3, CONV_COLS), lambda j, b: (0, j)),
                  pl.BlockSpec((1, CONV_COLS), lambda j, b: (0, j))],
        out_specs=[pl.BlockSpec((None, S, 2 * CONV_COLS), lambda j, b: (b, 0, j)),
                   pl.BlockSpec((3, CONV_COLS), lambda j, b: (0, j)),
                   pl.BlockSpec((1, CONV_COLS), lambda j, b: (0, j))],
        out_shape=[jax.ShapeDtypeStruct((B, S, 2 * D_FF), BF16), jax.ShapeDtypeStruct((3, D_FF), F32),
                   jax.ShapeDtypeStruct((1, D_FF), F32)],
        compiler_params=_params(2),
    )(up, d_ff, w_conv, b_conv)


def _ada_fwd(c_all, w_ada, b_ada):
    def body(c_ref, w_ref, b_ref, o_ref):
        cv = c_ref[...]
        act = (cv * _sigmoid(cv)).astype(BF16)
        o_ref[...] = jnp.dot(act, w_ref[...].astype(BF16), preferred_element_type=F32) + b_ref[...]

    return pl.pallas_call(body, name="ada_fwd",
                          out_shape=jax.ShapeDtypeStruct((c_all.shape[0], w_ada.shape[1]), F32),
                          compiler_params=pltpu.CompilerParams(vmem_limit_bytes=V7X_VMEM_LIMIT))(c_all, w_ada, b_ada)


def _ada_bwd(c_all, dmod_all, dmod_cols):
    def body(c_ref, dm_ref, dmc_ref, dw_ref, db_ref):
        cv = c_ref[...]
        act = (cv * _sigmoid(cv)).astype(BF16)
        dw_ref[...] = lax.dot_general(act, dmc_ref[...].astype(BF16), _TN, preferred_element_type=F32)
        db_ref[...] = _col_sum(dm_ref[...])

    return pl.pallas_call(
        body, name="ada_bwd",
        out_shape=[jax.ShapeDtypeStruct((c_all.shape[1], dmod_cols.shape[1]), F32),
                   jax.ShapeDtypeStruct((1, dmod_all.shape[1]), F32)],
        compiler_params=pltpu.CompilerParams(vmem_limit_bytes=V7X_VMEM_LIMIT))(c_all, dmod_all, dmod_cols)


def _adamw(w, m, v, g_parts, name, own=None):
    R, C = w.shape
    P = g_parts.shape[0]
    tr = R
    for cand in (256, 128, 64, 32, 16, 8):
        if R % cand == 0 and cand * C * 4 * (P + 8) * 2 <= V7X_VMEM_LIMIT // 2:
            tr = cand
            break
    c1 = 1.0 / (1.0 - ADAM_B1 ** ADAM_STEP)
    c2 = 1.0 / (1.0 - ADAM_B2 ** ADAM_STEP)

    def update(w_ref, m_ref, v_ref, g, og, od, om, ov):
        m_new = ADAM_B1 * m_ref[...] + (1.0 - ADAM_B1) * g
        v_new = ADAM_B2 * v_ref[...] + (1.0 - ADAM_B2) * (g * g)
        og[...] = g
        om[...] = m_new
        ov[...] = v_new
        od[...] = -ADAM_LR * ((m_new * c1) / (jnp.sqrt(v_new * c2) + ADAM_EPS) + ADAM_WD * w_ref[...])

    def total(g_ref):
        g = g_ref[0].astype(F32)
        for p in range(1, P):
            g = g + g_ref[p].astype(F32)
        return g

    out_shape = [jax.ShapeDtypeStruct((R, C), F32)] * 4
    if own is None:
        def body(w_ref, m_ref, v_ref, g_ref, og, od, om, ov):
            update(w_ref, m_ref, v_ref, total(g_ref), og, od, om, ov)

        spec = pl.BlockSpec((tr, C), lambda i: (i, 0))
        return pl.pallas_call(
            body, name=name, grid=(R // tr,),
            in_specs=[spec, spec, spec, pl.BlockSpec((P, tr, C), lambda i: (0, i, 0))],
            out_specs=[spec] * 4, out_shape=out_shape, compiler_params=_params(1),
        )(w, m, v, g_parts)

    slots, me = own

    def body_own(me_ref, w_ref, m_ref, v_ref, g_ref, own_ref, og, od, om, ov):
        g = own_ref[...].astype(F32)
        for p in range(P):
            g = g + jnp.where(me_ref[0] == p, 0.0, g_ref[p].astype(F32))
        update(w_ref, m_ref, v_ref, g, og, od, om, ov)

    spec = pl.BlockSpec((tr, C), lambda i, me_ref: (i, 0))
    grid_spec = pltpu.PrefetchScalarGridSpec(
        num_scalar_prefetch=1, grid=(R // tr,),
        in_specs=[spec, spec, spec, pl.BlockSpec((P, tr, C), lambda i, me_ref: (0, i, 0)),
                  pl.BlockSpec((None, tr, C), lambda i, me_ref: (me_ref[0], i, 0))],
        out_specs=[spec] * 4)
    return pl.pallas_call(body_own, name=name, grid_spec=grid_spec, out_shape=out_shape,
                          compiler_params=_params(1))(me, w, m, v, g_parts, slots)


def _sum_parts(parts, loss_rows):
    P, R, C = parts.shape
    lo, hi = loss_rows

    def body(p_ref, o_ref, loss_ref):
        t = p_ref[0]
        for p in range(1, P):
            t = t + p_ref[p]
        o_ref[...] = t
        tot = jnp.sum(jnp.sum(o_ref[lo:hi, :], axis=1, keepdims=True), axis=0, keepdims=True)
        loss_ref[...] = jnp.broadcast_to(tot, loss_ref.shape)

    return pl.pallas_call(body, name="sum_small_grads",
                          out_shape=[jax.ShapeDtypeStruct((R, C), F32), jax.ShapeDtypeStruct((1, LANES), F32)],
                          compiler_params=pltpu.CompilerParams(vmem_limit_bytes=V7X_VMEM_LIMIT))(parts)


def _exchange(items, name):
    n = len(items)
    MESH = pl.DeviceIdType.MESH

    def body(*refs):
        src, dst = refs[:n], refs[n:2 * n]
        send_sems, recv_sems, local_sems = refs[2 * n:]
        x, y, c = lax.axis_index("x"), lax.axis_index("y"), lax.axis_index("c")
        me = 4 * x + 2 * y + c
        started = []
        for it, (_, per_peer) in enumerate(items):
            own = pltpu.make_async_copy(src[it].at[me] if per_peer else src[it], dst[it].at[me], local_sems.at[it])
            own.start()
            started.append(own)
        sends, recvs = [], []
        for k in range(1, N_DEV):
            px = 1 - x if k & 4 else x
            py = 1 - y if k & 2 else y
            pc = 1 - c if k & 1 else c
            peer = 4 * px + 2 * py + pc
            for it, (_, per_peer) in enumerate(items):
                s = src[it].at[peer] if per_peer else src[it]
                cp = pltpu.make_async_remote_copy(src_ref=s, dst_ref=dst[it].at[me], send_sem=send_sems.at[it, k - 1],
                                                  recv_sem=recv_sems.at[it, k - 1], device_id=(px, py, pc),
                                                  device_id_type=MESH)
                cp.start()
                sends.append(cp)
                recvs.append(pltpu.make_async_remote_copy(
                    src_ref=s, dst_ref=dst[it].at[peer], send_sem=send_sems.at[it, k - 1],
                    recv_sem=recv_sems.at[it, k - 1], device_id=(px, py, pc), device_id_type=MESH))
        for cp in recvs:
            cp.wait_recv()
        for cp in sends:
            cp.wait_send()
        for cp in started:
            cp.wait()

    any_spec = pl.BlockSpec(memory_space=pl.ANY)
    out_shape = []
    for a, per_peer in items:
        shp = a.shape if per_peer else (N_DEV,) + a.shape
        out_shape.append(jax.ShapeDtypeStruct(shp, a.dtype))
    return pl.pallas_call(
        body, name=name, in_specs=[any_spec] * n, out_specs=[any_spec] * n, out_shape=out_shape,
        scratch_shapes=[pltpu.SemaphoreType.DMA((n, N_DEV - 1)), pltpu.SemaphoreType.DMA((n, N_DEV - 1)),
                        pltpu.SemaphoreType.DMA((n,))],
    )(*[a for a, _ in items])


def _remote(src, dst, send_sem, recv_sem, device):
    return pltpu.make_async_remote_copy(src_ref=src, dst_ref=dst, send_sem=send_sem, recv_sem=recv_sem,
                                        device_id=device, device_id_type=pl.DeviceIdType.MESH)


def _mesh_place():
    x, y, c = lax.axis_index("x"), lax.axis_index("y"), lax.axis_index("c")
    other_chips = [(1 - x, y), (x, 1 - y), (1 - x, 1 - y)]
    return x, y, c, (x, y, 1 - c), other_chips


def _gather_all(items, name):
    n = len(items)

    def body(*refs):
        src, dst = refs[:n], refs[n:2 * n]
        send_sems, recv_sems, local_sems = refs[2 * n:]
        x, y, c, sibling, chips = _mesh_place()
        slot = lambda px, py, pc: 4 * px + 2 * py + pc
        me = slot(x, y, c)
        own = [pltpu.make_async_copy(src[it], dst[it].at[me], local_sems.at[it]) for it in range(n)]
        first = []
        for it in range(n):
            first.append(_remote(src[it], dst[it].at[me], send_sems.at[it, 0], recv_sems.at[it, 0], sibling))
            for j, chip in enumerate(chips):
                first.append(_remote(src[it], dst[it].at[me], send_sems.at[it, 1 + j], recv_sems.at[it, 1 + j],
                                     (*chip, c)))
        for cp in own + first:
            cp.start()
        passed = []
        for j, chip in enumerate(chips):
            blk = slot(*chip, c)
            for it in range(n):
                _remote(src[it], dst[it].at[blk], send_sems.at[it, 1 + j], recv_sems.at[it, 1 + j],
                        (*chip, c)).wait_recv()
                fwd = _remote(dst[it].at[blk], dst[it].at[blk], send_sems.at[it, 4 + j], recv_sems.at[it, 4 + j],
                              sibling)
                fwd.start()
                passed.append(fwd)
        for it in range(n):
            _remote(src[it], dst[it].at[slot(x, y, 1 - c)], send_sems.at[it, 0], recv_sems.at[it, 0],
                    sibling).wait_recv()
        for j, chip in enumerate(chips):
            for it in range(n):
                _remote(src[it], dst[it].at[slot(*chip, 1 - c)], send_sems.at[it, 4 + j], recv_sems.at[it, 4 + j],
                        sibling).wait_recv()
        for cp in first + passed:
            cp.wait_send()
        for cp in own:
            cp.wait()

    any_spec = pl.BlockSpec(memory_space=pl.ANY)
    return pl.pallas_call(
        body, name=name, in_specs=[any_spec] * n, out_specs=[any_spec] * n,
        out_shape=[jax.ShapeDtypeStruct((N_DEV,) + a.shape, a.dtype) for a in items],
        scratch_shapes=[pltpu.SemaphoreType.DMA((n, 7)), pltpu.SemaphoreType.DMA((n, 7)),
                        pltpu.SemaphoreType.DMA((n,))],
    )(*items)


def _peers():
    x, y, c = lax.axis_index("x"), lax.axis_index("y"), lax.axis_index("c")
    out = []
    for k in range(1, N_DEV):
        px = 1 - x if k & 4 else x
        py = 1 - y if k & 2 else y
        pc = 1 - c if k & 1 else c
        out.append((k, (px, py, pc), 4 * px + 2 * py + pc))
    return 4 * x + 2 * y + c, out


def _exchange_start(items, name, gather, carry=()):
    n, m = len(items), len(carry)

    def body(*refs):
        src, land = refs[:n], refs[n:2 * n]
        first_out = 2 * n + m
        send_sems, recv_sems = refs[first_out:first_out + n], refs[first_out + n:first_out + 2 * n]
        token = refs[-1]
        me, peers = _peers()
        for k, peer, slot in peers:
            for it in range(n):
                _remote(src[it] if gather else src[it].at[slot], land[it].at[me], send_sems[it], recv_sems[it],
                        peer).start()
        token[...] = jnp.zeros_like(token)

    hbm = pl.BlockSpec(memory_space=pltpu.HBM)
    sem = pl.BlockSpec(memory_space=pltpu.SEMAPHORE)
    land_shapes = [(N_DEV,) + (a.shape if gather else a.shape[1:]) for a in items]
    lands = [lax.empty(shp, a.dtype) for shp, a in zip(land_shapes, items)]
    through = list(items) + lands + list(carry)
    outs = pl.pallas_call(
        body, name=name,
        out_shape=(*[pltpu.SemaphoreType.DMA(())] * (2 * n), *[pltpu.HBM(a.shape, a.dtype) for a in through],
                   jax.ShapeDtypeStruct((8, LANES), F32)),
        in_specs=[hbm] * len(through),
        out_specs=(*[sem] * (2 * n), *[hbm] * len(through), pl.BlockSpec(memory_space=pltpu.VMEM)),
        input_output_aliases={i: 2 * n + i for i in range(len(through))},
        compiler_params=pltpu.CompilerParams(has_side_effects=pltpu.SideEffectType.DATAFLOW_SIDE_EFFECTING),
    )(*[pltpu.with_memory_space_constraint(a, pltpu.HBM) for a in through])
    return (list(outs[:n]), list(outs[n:2 * n]), list(outs[2 * n:3 * n]), list(outs[3 * n:4 * n]), outs[-1],
            list(outs[4 * n:4 * n + m]))


def _exchange_wait(send_sems, recv_sems, items, lands, after, name):
    n = len(items)

    def body(*refs):
        land = refs[n:2 * n]
        send_sems, recv_sems = refs[2 * n:3 * n], refs[3 * n:4 * n]
        me, peers = _peers()
        for it in range(n):
            seven = land[it].at[pl.ds(0, N_DEV - 1)]
            cp = _remote(seven, seven, send_sems[it], recv_sems[it], peers[0][1])
            cp.wait_send()
            cp.wait_recv()

    hbm = pl.BlockSpec(memory_space=pltpu.HBM)
    sem = pl.BlockSpec(memory_space=pltpu.SEMAPHORE)
    outs = pl.pallas_call(
        body, name=name,
        out_shape=tuple(pltpu.HBM(a.shape, a.dtype) for a in list(items) + list(lands)),
        in_specs=[hbm] * (2 * n) + [sem] * (2 * n) + [pl.BlockSpec(memory_space=pl.ANY)],
        out_specs=tuple([hbm] * (2 * n)),
        input_output_aliases={i: i for i in range(2 * n)},
        compiler_params=pltpu.CompilerParams(has_side_effects=pltpu.SideEffectType.DATAFLOW_SIDE_EFFECTING),
    )(*items, *lands, *send_sems, *recv_sems, after)
    return list(outs[:n]), list(outs[n:])


def _gelu_tanh(y):
    k = math.sqrt(2.0 / math.pi)
    t = jnp.tanh(k * (y + 0.044715 * y * y * y))
    return 0.5 * y * (1.0 + t), t


def _local_step(x, mod, target, W, late_weights, P, send_early):
    B, S, D = x.shape
    T = B * S
    TS = 512
    flat = lambda a: a.reshape(T, a.shape[-1])
    unflat = lambda a: a.reshape(B, S, a.shape[-1])
    mod_col = lambda i: (mod, D, i)

    def f_modnorm(xv, sc, sh, g):
        return (xv * _rms_scale(xv) * g) * (1.0 + sc) + sh

    (u1,) = _rowwise(f_modnorm, [(x, D, 0)], [mod_col(1), mod_col(0)], [P["g_mix"]],
                     [(D, BF16)], [], [], ts=TS, name="modnorm_mix")
    u1f = flat(u1)
    qkv = unflat(_matmul(u1f, W["w_qkv"], name="proj_qkv"))
    us = unflat(_matmul(u1f, W["w_us"], name="proj_ssm_in"))
    gates = unflat(_matmul(u1f, W["w_gates"], out_dtype=BF16, name="proj_gates"))

    o_att, lse = _attention_fwd(qkv, P["slopes"])
    more_w, more_p = late_weights(o_att)
    W, P = {**W, **more_w}, {**P, **more_p}
    y_att = unflat(_matmul(flat(o_att), W["w_proj_att"], out_dtype=BF16, name="proj_att"))

    xs, y_mm = _scan_fwd(us, P["bb_big"], P["a_row"], P["cc_big"])

    def f_glu(ymm, usv, dsk, wg, bg):
        yv = ymm + dsk * usv
        ge, _ = _gelu_tanh(yv)
        pre = jnp.dot(ge.astype(BF16), wg, preferred_element_type=F32) + bg
        return yv, ge * _sigmoid(pre)

    y_s5, z = _rowwise(f_glu, [(y_mm, SSM_WIDTH, 0), (us, SSM_WIDTH, 0)], [], [P["d_skip"], W["w_glu"], P["b_glu"]],
                       [(SSM_WIDTH, F32), (SSM_WIDTH, BF16)], [], [], ts=TS, name="s5_glu")
    y_ssm = unflat(_matmul(flat(z), W["w_proj_ssm"], out_dtype=BF16, name="proj_ssm"))

    def f_merge(ga, gs, ya, ys, bga, bgs):
        return _sigmoid(ga + bga) * ya + _sigmoid(gs + bgs) * ys

    bga, bgs = P["b_gate"][:, :D], P["b_gate"][:, D:]
    (merged,) = _rowwise(f_merge, [(gates, D, 0), (gates, D, 1), (y_att, D, 0), (y_ssm, D, 0)], [], [bga, bgs],
                         [(D, BF16)], [], [], ts=TS, name="gate_merge")
    mix = unflat(_matmul(flat(merged), W["w_out"], out_dtype=BF16, name="proj_out"))

    def f_res_modnorm(xv, mx, gt, sc, sh, g):
        h = xv + gt * mx
        return h, (h * _rms_scale(h) * g) * (1.0 + sc) + sh

    h1, u2 = _rowwise(f_res_modnorm, [(x, D, 0), (mix, D, 0)], [mod_col(2), mod_col(4), mod_col(3)], [P["g_ffn"]],
                      [(D, F32), (D, BF16)], [], [], ts=TS, name="residual_modnorm_ffn")
    up = unflat(_up_fwd(flat(u2), W["w_up"], name="ffn_up"))
    ff = _conv_fwd(up, P["w_conv"], P["b_conv"])
    down = unflat(_matmul(flat(ff), W["w_down"], out_dtype=BF16, name="ffn_down"))

    def f_head(h1v, dn, tg, gt, g):
        h2 = h1v + gt * dn
        r = _rms_scale(h2)
        nh = h2 * r
        e = nh * g - tg
        dy = e * (1.0 / D)
        gy = dy * g
        dh = r * (gy - nh * jnp.mean(gy * nh, axis=-1, keepdims=True))
        return (dh, dh * gt, _col_sum(dh * dn), _col_sum(dy * nh), _col_sum(e * e) * (0.5 / D))

    dh2, d_down, d_gt2, d_g_final, loss_cols = _rowwise(
        f_head, [(h1, D, 0), (down, D, 0), (target, D, 0)], [mod_col(5)], [P["g_final"]],
        [(D, BF16), (D, BF16)], [D], [(1, D), (1, D)], ts=TS, name="head_loss")

    d_downf = flat(d_down)
    d_ff = unflat(_matmul(d_downf, W["w_down"], tb=True, out_dtype=BF16, name="ffn_down_dx"))
    d_w_down = _matmul(flat(ff), d_downf, ta=True, out_dtype=BF16, name="ffn_down_dw")
    d_up, d_w_conv, d_b_conv = _conv_bwd(up, d_ff, P["w_conv"], P["b_conv"])
    d_upf = flat(d_up)
    d_u2 = unflat(_up_dx(d_upf, W["w_up"], name="ffn_up_dx"))
    d_w_up = _up_dw(flat(u2), d_upf, name="ffn_up_dw")
    token, _ = send_early(dict(w_down=d_w_down.reshape(N_DEV, D_FF // N_DEV, D), w_up=d_w_up))
    g_ffn_after = P["g_ffn"] + token[0:1, 0:1]

    def f_modnorm_bwd(du, h, dres, mx, sc, gt, g):
        r = _rms_scale(h)
        nh = h * r
        dn = du * (1.0 + sc)
        gy = dn * g
        dh = dres + r * (gy - nh * jnp.mean(gy * nh, axis=-1, keepdims=True))
        return (dh, dh * gt, _col_sum(du), _col_sum(du * nh * g), _col_sum(dh * mx), _col_sum(dn * nh))

    dh1, d_mix, d_sh2, d_sc2, d_gt1, d_g_ffn = _rowwise(
        f_modnorm_bwd, [(d_u2, D, 0), (h1, D, 0), (dh2, D, 0), (mix, D, 0)], [mod_col(4), mod_col(2)], [g_ffn_after],
        [(D, BF16), (D, BF16)], [D, D, D], [(1, D)], ts=TS, name="modnorm_ffn_bwd")

    d_mixf = flat(d_mix)
    d_merged = unflat(_matmul(d_mixf, W["w_out"], tb=True, out_dtype=BF16, name="proj_out_dx"))
    d_w_out = _matmul(flat(merged), d_mixf, ta=True, out_dtype=BF16, name="proj_out_dw")

    def f_merge_bwd(dm, ga, gs, ya, ys, bga_, bgs_):
        sa, ss = _sigmoid(ga + bga_), _sigmoid(gs + bgs_)
        dga = dm * ya * sa * (1.0 - sa)
        dgs = dm * ys * ss * (1.0 - ss)
        return dm * sa, dm * ss, jnp.concatenate([dga, dgs], axis=1), _col_sum(dga), _col_sum(dgs)

    d_y_att, d_y_ssm, d_gates, d_bga, d_bgs = _rowwise(
        f_merge_bwd, [(d_merged, D, 0), (gates, D, 0), (gates, D, 1), (y_att, D, 0), (y_ssm, D, 0)], [], [bga, bgs],
        [(D, BF16), (D, BF16), (2 * D, BF16)], [], [(1, D), (1, D)], ts=TS, name="gate_merge_bwd")

    d_yaf, d_ysf = flat(d_y_att), flat(d_y_ssm)
    d_o_att = unflat(_matmul(d_yaf, W["w_proj_att"], tb=True, name="proj_att_dx"))
    d_w_proj_att = _matmul(flat(o_att), d_yaf, ta=True, out_dtype=BF16, name="proj_att_dw")
    d_z = unflat(_matmul(d_ysf, W["w_proj_ssm"], tb=True, out_dtype=BF16, name="proj_ssm_dx"))
    d_w_proj_ssm = _matmul(flat(z), d_ysf, ta=True, out_dtype=BF16, name="proj_ssm_dw")

    def f_glu_bwd(yv, dz, usv, dsk, wg, bg):
        ge, t = _gelu_tanh(yv)
        pre = jnp.dot(ge.astype(BF16), wg, preferred_element_type=F32) + bg
        sg = _sigmoid(pre)
        dpre = dz * ge * sg * (1.0 - sg)
        dge = dz * sg + lax.dot_general(dpre.astype(BF16), wg, _NT, preferred_element_type=F32)
        k = math.sqrt(2.0 / math.pi)
        dgelu = 0.5 * (1.0 + t) + 0.5 * yv * (1.0 - t * t) * k * (1.0 + 3.0 * 0.044715 * yv * yv)
        dy = dge * dgelu
        dwg = lax.dot_general(ge.astype(BF16), dpre.astype(BF16), _TN, preferred_element_type=F32)
        return dy, dy * dsk, dwg, _col_sum(dpre), _col_sum(dy * usv)

    d_y_s5, d_us_skip, d_w_glu, d_b_glu, d_d_skip = _rowwise(
        f_glu_bwd, [(y_s5, SSM_WIDTH, 0), (d_z, SSM_WIDTH, 0), (us, SSM_WIDTH, 0)], [],
        [P["d_skip"], W["w_glu"], P["b_glu"]],
        [(SSM_WIDTH, BF16), (SSM_WIDTH, F32)], [], [(SSM_WIDTH, SSM_WIDTH), (1, SSM_WIDTH), (1, SSM_WIDTH)],
        ts=TS, name="s5_glu_bwd")
    d_ysf2 = flat(d_y_s5)
    d_cc = _matmul(flat(xs), d_ysf2, ta=True, name="s5_readout_dw")
    lam, g_ab = _scan_bwd(d_y_s5, P["cc_big"], xs, P["a_row"])
    lam = flat(lam)
    d_us_mm = unflat(_matmul(lam, P["bb_big"], tb=True, name="s5_bu_dx"))
    d_bb = _matmul(flat(us), lam, ta=True, name="s5_bu_dw")

    token, _ = send_early(dict(
        w_out=d_w_out.reshape(N_DEV, D // N_DEV, D), w_proj_att=_cols_to_slots(d_w_proj_att),
        w_proj_ssm=_cols_to_slots(d_w_proj_ssm),
        w_glu=d_w_glu.astype(BF16).reshape(N_DEV, SSM_WIDTH // N_DEV, SSM_WIDTH),
        w_conv=_cols_to_slots(d_w_conv.astype(BF16))))
    d_qkv = _attention_bwd(qkv, o_att, d_o_att, lse, P["slopes"] + token[0, 0])

    def f_add(a, b_):
        return a + b_

    (d_us,) = _rowwise(f_add, [(d_us_mm, SSM_WIDTH, 0), (d_us_skip, SSM_WIDTH, 0)], [], [],
                       [(SSM_WIDTH, BF16)], [], [], ts=TS, name="s5_input_grad")
    d_qkvf = flat(d_qkv)
    d_usf = flat(d_us)
    d_gatesf = flat(d_gates)
    d_w_in = jnp.concatenate(
        [_unpair_qkv_columns(_matmul(u1f, d_qkvf, ta=True, out_dtype=BF16, name="proj_qkv_dw")),
         _matmul(u1f, d_usf, ta=True, out_dtype=BF16, name="proj_ssm_in_dw"),
         _matmul(u1f, d_gatesf, ta=True, out_dtype=BF16, name="proj_gates_dw")], axis=1)
    token, (w_qkv, w_us, w_gates) = send_early(dict(w_in=_cols_to_slots(d_w_in)),
                                               carry=[W["w_qkv"], W["w_us"], W["w_gates"]])
    d_u1 = (_matmul(d_qkvf, w_qkv, tb=True, out_dtype=BF16, name="proj_qkv_dx"),
            _matmul(d_usf, w_us, tb=True, out_dtype=BF16, name="proj_ssm_in_dx"),
            _matmul(d_gatesf, w_gates, tb=True, out_dtype=BF16, name="proj_gates_dx"))

    def f_modnorm_bwd_in(du0, du1, du2, h, dres, sc, g):
        du = du0 + du1 + du2
        r = _rms_scale(h)
        nh = h * r
        dn = du * (1.0 + sc)
        gy = dn * g
        dh = dres + r * (gy - nh * jnp.mean(gy * nh, axis=-1, keepdims=True))
        return (dh, _col_sum(du), _col_sum(du * nh * g), _col_sum(dn * nh))

    grad_x, d_sh1, d_sc1, d_g_mix = _rowwise(
        f_modnorm_bwd_in, [(unflat(d_u1[0]), D, 0), (unflat(d_u1[1]), D, 0), (unflat(d_u1[2]), D, 0), (x, D, 0),
                           (dh1, D, 0)], [mod_col(1)], [P["g_mix"] + token[0:1, 0:1]],
        [(D, F32)], [D, D], [(1, D)], ts=TS, name="modnorm_mix_bwd")

    d_mod = jnp.concatenate([d_sh1, d_sc1, d_gt1, d_sh2, d_sc2, d_gt2], axis=-1)
    g_ab_re, g_ab_im = _deinterleave(g_ab)
    d_bb_re, d_bb_im = _deinterleave(d_bb)
    d_cc_re, d_cc_im = (t.T for t in _deinterleave(d_cc.T))
    small = dict(g_mix=d_g_mix, b_gate=jnp.concatenate([d_bga, d_bgs], axis=1), g_ab_re=g_ab_re, g_ab_im=g_ab_im,
                 d_bb_re=d_bb_re, d_bb_im=d_bb_im, d_cc_re=d_cc_re, d_cc_im=d_cc_im, d_skip=d_d_skip,
                 b_glu=d_b_glu, g_ffn=d_g_ffn, b_conv=d_b_conv, g_final=d_g_final, loss_cols=loss_cols)
    return grad_x, d_mod, small


def _block_diag_in(bb):
    t = bb.reshape(SSM_GROUPS, SSM_STATE, SSM_GROUP_CH)
    eye = jnp.eye(SSM_GROUPS, dtype=bb.dtype)
    return jnp.einsum("gnc,gh->gchn", t, eye).reshape(SSM_WIDTH, SSM_COLS)


def _block_diag_out(cm):
    eye = jnp.eye(SSM_GROUPS, dtype=cm.dtype)
    return jnp.einsum("gcn,gh->gnhc", cm, eye).reshape(SSM_COLS, SSM_WIDTH)


def _diag_blocks_in(m):
    t = m.reshape(SSM_GROUPS, SSM_GROUP_CH, SSM_GROUPS, SSM_STATE)
    idx = jnp.arange(SSM_GROUPS)
    return t[idx, :, idx, :].transpose(0, 2, 1).reshape(SSM_COLS, SSM_GROUP_CH)


def _diag_blocks_out(m):
    t = m.reshape(SSM_GROUPS, SSM_STATE, SSM_GROUPS, SSM_GROUP_CH)
    idx = jnp.arange(SSM_GROUPS)
    return t[idx, :, idx, :].transpose(0, 2, 1)


def _pair_qkv_columns(w):
    lead = w.shape[:-1]
    return w.reshape(lead + (3, N_HEADS // 2, LANES)).swapaxes(-3, -2).reshape(lead + (3 * ATT_WIDTH,))


def _unpair_qkv_columns(w):
    lead = w.shape[:-1]
    return w.reshape(lead + (N_HEADS // 2, 3, LANES)).swapaxes(-3, -2).reshape(lead + (3 * ATT_WIDTH,))


def _interleave(re, im):
    lead = re.shape[:-1]
    g = lambda a: a.reshape(lead + (SSM_COLS // SCAN_COLS, 1, SCAN_COLS))
    return jnp.concatenate([g(re), g(im)], axis=-2).reshape(lead + (2 * SSM_COLS,))


def _deinterleave(x):
    lead = x.shape[:-1]
    t = x.reshape(lead + (SSM_COLS // SCAN_COLS, 2, SCAN_COLS))
    return t[..., 0, :].reshape(lead + (SSM_COLS,)), t[..., 1, :].reshape(lead + (SSM_COLS,))


def _cols_to_slots(g):
    R = g.shape[0]
    return g.reshape(R, N_DEV, g.shape[1] // N_DEV).transpose(1, 0, 2)


def _slots_to_cols(g):
    return g.transpose(1, 0, 2).reshape(g.shape[1], N_DEV * g.shape[2])


SMALL_ORDER = ("b_ada", "g_mix", "b_gate", "a_re", "a_im", "log_dt", "b_re", "b_im", "c_re", "c_im", "d_skip",
               "b_glu", "g_ffn", "b_conv", "g_final")


def _pack(arrs):
    pieces, offs, row = [], [], 0
    for a in arrs:
        f = a.reshape(-1).astype(F32)
        n = f.shape[0]
        rows = -(-n // LANES)
        pieces.append(jnp.pad(f, (0, rows * LANES - n)))
        offs.append((row, n))
        row += rows
    return jnp.concatenate(pieces).reshape(row, LANES), offs


def _unpack(packed, offs, shapes):
    flat = packed.reshape(-1)
    return [flat[r * LANES:r * LANES + n].reshape(s) for (r, n), s in zip(offs, shapes)]


def kernel(x, c, w_ada, b_ada, g_mix, w_in, b_gate, a_re, a_im, log_dt, b_re, b_im, c_re, c_im, d_skip, w_glu, b_glu, w_proj_att, w_proj_ssm, w_out, g_ffn, w_up, w_conv, b_conv, w_down, g_final, loss_target, m_w_ada, m_b_ada, m_g_mix, m_w_in, m_b_gate, m_a_re, m_a_im, m_log_dt, m_b_re, m_b_im, m_c_re, m_c_im, m_d_skip, m_w_glu, m_b_glu, m_w_proj_att, m_w_proj_ssm, m_w_out, m_g_ffn, m_w_up, m_w_conv, m_b_conv, m_w_down, m_g_final, v_w_ada, v_b_ada, v_g_mix, v_w_in, v_b_gate, v_a_re, v_a_im, v_log_dt, v_b_re, v_b_im, v_c_re, v_c_im, v_d_skip, v_w_glu, v_b_glu, v_w_proj_att, v_w_proj_ssm, v_w_out, v_g_ffn, v_w_up, v_w_conv, v_b_conv, v_w_down, v_g_final):
    args = dict(locals())
    B, S, D = x.shape
    me = 4 * lax.axis_index("x") + 2 * lax.axis_index("y") + lax.axis_index("c")
    bf = lambda w: w[0].astype(BF16)

    c_slots, w_in_slots = _gather_all([c, bf(w_in)], name="gather_first_weights")
    c_all = c_slots.reshape(N_DEV * B, D)
    w_in_full = _slots_to_cols(w_in_slots)
    n_qkv = 3 * ATT_WIDTH
    W = dict(w_qkv=_pair_qkv_columns(w_in_full[:, :n_qkv]), w_us=w_in_full[:, n_qkv:n_qkv + SSM_WIDTH],
             w_gates=w_in_full[:, n_qkv + SSM_WIDTH:])

    n_ada = w_ada.shape[2]
    b_ada_cols = lax.dynamic_slice(b_ada, (0, me * n_ada), (1, n_ada))
    mod_part = _ada_fwd(c_all, w_ada[0], b_ada_cols)
    (mod_slots,) = _exchange([(mod_part.reshape(N_DEV, B, n_ada), True)], name="scatter_modulation")
    mod = mod_slots.transpose(1, 0, 2).reshape(B, 1, 6 * D)

    later = [bf(w_glu), bf(w_proj_att), bf(w_proj_ssm), bf(w_out), bf(w_up), w_conv[0], bf(w_down)]
    later_sems = _exchange_start(later, "start_later_weights", gather=True, carry=[mod])
    (mod,) = later_sems[5]

    def late_weights(after):
        _, lands = _exchange_wait(*later_sems[:4], after, name="wait_later_weights")
        g = [lax.dynamic_update_index_in_dim(land, a, me, 0) for land, a in zip(lands, later)]
        more_w = dict(w_glu=g[0].reshape(SSM_WIDTH, SSM_WIDTH), w_proj_att=_slots_to_cols(g[1]),
                      w_proj_ssm=_slots_to_cols(g[2]), w_out=g[3].reshape(D, D), w_up=g[4],
                      w_down=g[6].reshape(D_FF, D))
        return more_w, dict(w_conv=_slots_to_cols(g[5]))

    ab_re, ab_im, f_re, f_im = _s5_params(a_re[0], a_im[0], log_dt[0].reshape(SSM_GROUPS, 1))
    col = lambda a: a.reshape(SSM_COLS, 1)
    b_re2, b_im2 = b_re[0].reshape(SSM_COLS, SSM_GROUP_CH), b_im[0].reshape(SSM_COLS, SSM_GROUP_CH)
    bb_re, bb_im = _s5_input_matrix(col(f_re), col(f_im), b_re2, b_im2)
    slopes = jnp.asarray([2.0 ** (-8.0 * (h + 1) / N_HEADS) for h in range(N_HEADS)], F32)
    P = dict(g_mix=g_mix, g_ffn=g_ffn, g_final=g_final.reshape(1, D), b_gate=b_gate, d_skip=d_skip, b_glu=b_glu,
             b_conv=b_conv, slopes=slopes,
             a_row=_interleave(ab_re.reshape(1, SSM_COLS), ab_im.reshape(1, SSM_COLS)),
             bb_big=_interleave(_block_diag_in(bb_re), _block_diag_in(bb_im)),
             cc_big=_interleave(_block_diag_out(c_re[0]).T, -_block_diag_out(c_im[0]).T).T)

    in_flight = []

    def send_early(grads, carry=()):
        names = list(grads)
        handles = _exchange_start([grads[n] for n in names], "start_gradients_%d" % len(in_flight), gather=False,
                                  carry=carry)
        in_flight.append((names,) + handles[:4])
        return handles[4], handles[5]

    grad_x, d_mod, small = _local_step(x, mod, loss_target, W, late_weights, P, send_early)

    small_list = [small["loss_cols"], small["g_mix"], small["b_gate"], small["g_ab_re"], small["g_ab_im"],
                  _diag_blocks_in(small["d_bb_re"]), _diag_blocks_in(small["d_bb_im"]),
                  _diag_blocks_out(small["d_cc_re"]), -_diag_blocks_out(small["d_cc_im"]),
                  small["g_ffn"], small["b_conv"], small["g_final"], small["d_skip"], small["b_glu"]]
    small_packed, small_offs = _pack(small_list)
    small_all, dmod_slots = _gather_all([small_packed, d_mod.reshape(B, 6 * D)], name="gather_small_gradients")

    out = {}

    def update(name, parts, own=None):
        w2 = args[name][0]
        g, dl, mn, vn = _adamw(w2, args["m_" + name][0], args["v_" + name][0], parts, name="adamw_" + name, own=own)
        for key, val in (("grad_", g), ("delta_", dl), ("new_m_", mn), ("new_v_", vn)):
            out[key + name] = val[None]

    my_slot = me.astype(jnp.int32).reshape(1)
    for i, (names, send_sems, recv_sems, sent, lands) in enumerate(in_flight):
        sent, lands = _exchange_wait(send_sems, recv_sems, sent, lands, dmod_slots, name="wait_gradients_%d" % i)
        for name, own_slots, landed in zip(names, sent, lands):
            update(name, landed, own=(own_slots, my_slot))

    dmod_all = dmod_slots.reshape(N_DEV * B, 6 * D)
    dmod_cols = lax.dynamic_slice(dmod_all, (0, me * n_ada), (N_DEV * B, n_ada))
    d_w_ada, d_b_ada = _ada_bwd(c_all, dmod_all, dmod_cols)
    update("w_ada", d_w_ada[None])

    loss_row, loss_n = small_offs[0]
    small_sum, loss_vec = _sum_parts(small_all, (loss_row, loss_row + loss_n // LANES))
    shapes = [(1, D), (1, D), (1, 2 * D), (SSM_GROUPS, SSM_STATE), (SSM_GROUPS, SSM_STATE), (SSM_COLS, SSM_GROUP_CH),
              (SSM_COLS, SSM_GROUP_CH), (1, SSM_GROUPS, SSM_GROUP_CH, SSM_STATE),
              (1, SSM_GROUPS, SSM_GROUP_CH, SSM_STATE), (1, D), (1, D_FF), (D,), (1, SSM_WIDTH), (1, SSM_WIDTH)]
    (_, s_g_mix, s_b_gate, s_ab_re, s_ab_im, s_bb_re, s_bb_im, s_c_re, s_c_im, s_g_ffn, s_b_conv, s_g_final,
     s_d_skip, s_b_glu) = _unpack(small_sum, small_offs, shapes)
    d_b_re2, d_b_im2, d_f_re, d_f_im = _s5_input_matrix_bwd(col(f_re), col(f_im), b_re2, b_im2, s_bb_re, s_bb_im)
    d_a_re, d_a_im, d_log_dt = _s5_params_bwd(a_re[0], a_im[0], log_dt[0].reshape(SSM_GROUPS, 1), s_ab_re, s_ab_im,
                                              d_f_re.reshape(SSM_GROUPS, SSM_STATE),
                                              d_f_im.reshape(SSM_GROUPS, SSM_STATE))
    grads_small = dict(b_ada=d_b_ada, g_mix=s_g_mix, b_gate=s_b_gate, a_re=d_a_re[None], a_im=d_a_im[None],
                       log_dt=d_log_dt.reshape(1, SSM_GROUPS), b_re=d_b_re2.reshape(b_re.shape),
                       b_im=d_b_im2.reshape(b_im.shape), c_re=s_c_re, c_im=s_c_im, d_skip=s_d_skip, b_glu=s_b_glu,
                       g_ffn=s_g_ffn, b_conv=s_b_conv, g_final=s_g_final)
    w_pack, offs = _pack([args[n] for n in SMALL_ORDER])
    m_pack, _ = _pack([args["m_" + n] for n in SMALL_ORDER])
    v_pack, _ = _pack([args["v_" + n] for n in SMALL_ORDER])
    g_pack, _ = _pack([grads_small[n] for n in SMALL_ORDER])
    res = _adamw(w_pack, m_pack, v_pack, g_pack[None], name="adamw_small")
    shapes_small = [args[n].shape for n in SMALL_ORDER]
    for key, packed in zip(("grad_", "delta_", "new_m_", "new_v_"), res):
        for n, val in zip(SMALL_ORDER, _unpack(packed, offs, shapes_small)):
            out[key + n] = val

    order = ["w_ada", "b_ada", "g_mix", "w_in", "b_gate", "a_re", "a_im", "log_dt", "b_re", "b_im", "c_re", "c_im",
             "d_skip", "w_glu", "b_glu", "w_proj_att", "w_proj_ssm", "w_out", "g_ffn", "w_up", "w_conv", "b_conv",
             "w_down", "g_final"]
    loss = loss_vec[0, 0]
    return (loss, grad_x, *[out[k + n] for k in ("grad_", "delta_", "new_m_", "new_v_") for n in order])
```

```python
import math

import jax
import jax.numpy as jnp
from jax import lax
from jax.experimental import pallas as pl
from jax.experimental.pallas import tpu as pltpu

F32 = jnp.float32
BF16 = jnp.bfloat16

N_DEV = 8
D_MODEL = 1024
N_HEADS = 8
HEAD_DIM = 64
ATT_WIDTH = N_HEADS * HEAD_DIM
DILATIONS = (1, 4, 16)
WIN = 128
SSM_GROUPS = 16
SSM_GROUP_CH = 16
SSM_WIDTH = SSM_GROUPS * SSM_GROUP_CH
SSM_STATE = 64
SSM_COLS = SSM_GROUPS * SSM_STATE
D_FF = 2048
EPS = 1e-6
NEG_INF = -1e30
ADAM_LR, ADAM_B1, ADAM_B2, ADAM_EPS, ADAM_WD, ADAM_STEP = 0.001, 0.9, 0.999, 1e-08, 0.01, 10

V7X_VMEM_LIMIT = 56 * 1024 * 1024
LANES = 128


def _params(n_grid):
    return pltpu.CompilerParams(dimension_semantics=("arbitrary",) * n_grid,
                                vmem_limit_bytes=V7X_VMEM_LIMIT)


def _tile(n, pref):
    if n <= pref:
        return n
    t = (pref // LANES) * LANES
    while t > 0:
        if n % t == 0:
            return t
        t -= LANES
    return n


def _matmul(a, b, *, ta=False, tb=False, out_dtype=F32, name):
    if ta:
        K, M = a.shape
    else:
        M, K = a.shape
    if tb:
        N, K2 = b.shape
    else:
        K2, N = b.shape
    assert K == K2, (a.shape, b.shape)
    if ta:
        tm, tn, tk = _tile(M, 1024), _tile(N, 2048), _tile(K, 512)
    else:
        tm, tk = _tile(M, 512), _tile(K, 4096)
        tn = _tile(N, 2048 if K <= 2048 else 1024)
    nk = K // tk
    dn = (((0,) if ta else (1,), (1,) if tb else (0,)), ((), ()))

    def body(a_ref, b_ref, o_ref, acc_ref):
        k = pl.program_id(2)
        part = lax.dot_general(a_ref[...].astype(BF16), b_ref[...].astype(BF16), dn, preferred_element_type=F32)
        if nk == 1:
            o_ref[...] = part.astype(o_ref.dtype)
            return

        @pl.when(k == 0)
        def _():
            acc_ref[...] = jnp.zeros_like(acc_ref)

        acc_ref[...] += part

        @pl.when(k == nk - 1)
        def _():
            o_ref[...] = acc_ref[...].astype(o_ref.dtype)

    a_spec = (pl.BlockSpec((tk, tm), lambda j, i, k: (k, i)) if ta
              else pl.BlockSpec((tm, tk), lambda j, i, k: (i, k)))
    b_spec = (pl.BlockSpec((tn, tk), lambda j, i, k: (j, k)) if tb
              else pl.BlockSpec((tk, tn), lambda j, i, k: (k, j)))
    return pl.pallas_call(
        body, name=name, grid=(N // tn, M // tm, nk),
        in_specs=[a_spec, b_spec],
        out_specs=pl.BlockSpec((tm, tn), lambda j, i, k: (i, j)),
        out_shape=jax.ShapeDtypeStruct((M, N), out_dtype),
        scratch_shapes=[pltpu.VMEM((tm, tn) if nk > 1 else (8, LANES), F32)],
        compiler_params=_params(3),
    )(a, b)


HALF = 256
UP_SLOTS = N_DEV // 2
UP_GROUP = 4 * HALF


def _group_weight(w_ref):
    return jnp.concatenate([w_ref[0, :, :HALF], w_ref[1, :, :HALF], w_ref[0, :, HALF:], w_ref[1, :, HALF:]], axis=1)


def _up_weight_spec(K, index):
    return pl.BlockSpec((2, None, K, 2 * HALF), index)


def _up_fwd(a, w3, name):
    M, K = a.shape
    tm = _tile(M, 1024)

    def body(a_ref, w_ref, o_ref):
        o_ref[...] = jnp.dot(a_ref[...].astype(BF16), _group_weight(w_ref),
                             preferred_element_type=F32).astype(o_ref.dtype)

    return pl.pallas_call(
        body, name=name, grid=(UP_SLOTS, M // tm),
        in_specs=[pl.BlockSpec((tm, K), lambda j, i: (i, 0)), _up_weight_spec(K, lambda j, i: (0, j, 0, 0))],
        out_specs=pl.BlockSpec((tm, UP_GROUP), lambda j, i: (i, j)),
        out_shape=jax.ShapeDtypeStruct((M, UP_SLOTS * UP_GROUP), BF16), compiler_params=_params(2),
    )(a, w3.reshape(2, UP_SLOTS, K, 2 * HALF))


def _up_dx(d, w3, name):
    M = d.shape[0]
    K = w3.shape[1]
    tm = _tile(M, 1024)

    def body(d_ref, w_ref, o_ref, acc_ref):
        j = pl.program_id(1)

        @pl.when(j == 0)
        def _():
            acc_ref[...] = jnp.zeros_like(acc_ref)

        acc_ref[...] += lax.dot_general(d_ref[...], _group_weight(w_ref), _NT, preferred_element_type=F32)

        @pl.when(j == UP_SLOTS - 1)
        def _():
            o_ref[...] = acc_ref[...].astype(o_ref.dtype)

    return pl.pallas_call(
        body, name=name, grid=(M // tm, UP_SLOTS),
        in_specs=[pl.BlockSpec((tm, UP_GROUP), lambda i, j: (i, j)), _up_weight_spec(K, lambda i, j: (0, j, 0, 0))],
        out_specs=pl.BlockSpec((tm, K), lambda i, j: (i, 0)),
        out_shape=jax.ShapeDtypeStruct((M, K), BF16), scratch_shapes=[pltpu.VMEM((tm, K), F32)],
        compiler_params=_params(2),
    )(d, w3.reshape(2, UP_SLOTS, K, 2 * HALF))


def _up_dw(a, d, name):
    M, K = a.shape
    tk = _tile(M, 512)
    nk = M // tk

    def body(a_ref, d_ref, o_ref, acc_ref):
        k = pl.program_id(1)

        @pl.when(k == 0)
        def _():
            acc_ref[...] = jnp.zeros_like(acc_ref)

        acc_ref[...] += lax.dot_general(a_ref[...], d_ref[...], _TN, preferred_element_type=F32)

        @pl.when(k == nk - 1)
        def _():
            for half in range(2):
                for part in range(2):
                    lo = (2 * half + part) * HALF
                    o_ref[part, :, half * HALF:(half + 1) * HALF] = acc_ref[:, lo:lo + HALF].astype(o_ref.dtype)

    out = pl.pallas_call(
        body, name=name, grid=(UP_SLOTS, nk),
        in_specs=[pl.BlockSpec((tk, K), lambda j, k: (k, 0)), pl.BlockSpec((tk, UP_GROUP), lambda j, k: (k, j))],
        out_specs=_up_weight_spec(K, lambda j, k: (0, j, 0, 0)),
        out_shape=jax.ShapeDtypeStruct((2, UP_SLOTS, K, 2 * HALF), BF16),
        scratch_shapes=[pltpu.VMEM((K, UP_GROUP), F32)], compiler_params=_params(2),
    )(a, d)
    return out.reshape(N_DEV, K, 2 * HALF)


def _rowwise(fn, rows, bvecs, consts, out_rows, out_b, out_g, *, ts, name):
    B, S = rows[0][0].shape[:2]
    nin = len(rows) + len(bvecs) + len(consts)
    nr, nb, ng = len(out_rows), len(out_b), len(out_g)

    def body(*refs):
        b = pl.program_id(0)
        s = pl.program_id(1)
        vals = [r[...] for r in refs[:nin]]
        vals[:len(rows)] = [v.astype(F32) for v in vals[:len(rows)]]
        outs = fn(*vals)
        if not isinstance(outs, (tuple, list)):
            outs = (outs,)
        orefs = refs[nin:]
        for i in range(nr):
            orefs[i][...] = outs[i].astype(orefs[i].dtype)
        for i in range(nb):
            ref = orefs[nr + i]

            @pl.when(s == 0)
            def _(ref=ref):
                ref[...] = jnp.zeros_like(ref)

            ref[...] += outs[nr + i]
        for i in range(ng):
            ref = orefs[nr + nb + i]

            @pl.when((s == 0) & (b == 0))
            def _(ref=ref):
                ref[...] = jnp.zeros_like(ref)

            ref[...] += outs[nr + nb + i]

    in_specs = ([pl.BlockSpec((None, ts, cb), lambda b, s, ci=ci: (b, s, ci)) for (_, cb, ci) in rows]
                + [pl.BlockSpec((None, 1, cb), lambda b, s, ci=ci: (b, 0, ci)) for (_, cb, ci) in bvecs]
                + [pl.BlockSpec(a.shape, lambda b, s: (0, 0)) for a in consts])
    out_shape = ([jax.ShapeDtypeStruct((B, S, c), dt) for (c, dt) in out_rows]
                 + [jax.ShapeDtypeStruct((B, 1, c), F32) for c in out_b]
                 + [jax.ShapeDtypeStruct(rc, F32) for rc in out_g])
    out_specs = ([pl.BlockSpec((None, ts, c), lambda b, s: (b, s, 0)) for (c, _) in out_rows]
                 + [pl.BlockSpec((None, 1, c), lambda b, s: (b, 0, 0)) for c in out_b]
                 + [pl.BlockSpec(rc, lambda b, s: (0, 0)) for rc in out_g])
    args = [a for (a, _, _) in rows] + [a for (a, _, _) in bvecs] + list(consts)
    return pl.pallas_call(
        body, name=name, grid=(B, S // ts), in_specs=in_specs, out_specs=out_specs,
        out_shape=out_shape, compiler_params=_params(2),
    )(*args)


def _col_sum(v):
    return jnp.sum(v, axis=0, keepdims=True)


def _rms_scale(h):
    return lax.rsqrt(jnp.mean(h * h, axis=-1, keepdims=True) + EPS)


def _sigmoid(v):
    return 1.0 / (1.0 + jnp.exp(-v))


ATT_SCALE = HEAD_DIM ** -0.5
COPY_ROWS = 256
_NT = (((1,), (1,)), ((), ()))
_TN = (((0,), (0,)), ((), ()))


def _row_chunks(d, seq):
    sub = seq // d
    out = []
    for r in range(d):
        for c0 in range(0, sub, COPY_ROWS):
            n = min(COPY_ROWS, sub - c0)
            out.append((pl.ds(r + c0 * d, n, stride=d), r * sub + c0, n))
    return out


ATT_UNROLL = 8
KEYS = 2 * WIN


def _zero_once(refs):
    @pl.when((pl.program_id(0) == 0) & (pl.program_id(1) == 0))
    def _():
        for r in refs:
            r[...] = jnp.zeros_like(r)


def _pair_bias(bias_ref, slopes_ref, hp, d, key_major):
    shape = (KEYS, WIN) if key_major else (WIN, KEYS)
    qi = lax.broadcasted_iota(jnp.int32, shape, 1 if key_major else 0)
    kj = lax.broadcasted_iota(jnp.int32, shape, 0 if key_major else 1)
    dist = WIN + qi - kj
    valid = (dist >= 0) & (dist <= WIN)
    distf = dist.astype(F32)
    for h in range(2):
        slope_d = slopes_ref[2 * hp + h] * float(d)
        with_prev = jnp.where(valid, -(slope_d * distf), NEG_INF)
        no_prev = jnp.where(kj >= WIN, with_prev, NEG_INF)
        span = slice(h * KEYS, (h + 1) * KEYS)
        if key_major:
            bias_ref[1, span, :] = with_prev
            bias_ref[0, span, :] = no_prev
        else:
            bias_ref[1, :, span] = with_prev
            bias_ref[0, :, span] = no_prev


def _stack_heads(v):
    first = lax.broadcasted_iota(jnp.int32, v.shape, 1) < HEAD_DIM
    zero = jnp.zeros_like(v)
    return jnp.concatenate([jnp.where(first, v, zero), jnp.where(first, zero, v)], axis=0)


def _per_head(c0, c1, n):
    return jnp.where(lax.broadcasted_iota(jnp.int32, (n, LANES), 1) < HEAD_DIM, c0, c1)


def _qkv_spec(seq, j):
    return pl.BlockSpec((None, seq, LANES), lambda b, hp: (b, 0, 3 * hp + j))


def _attention_fwd(qkv, slopes):
    B, S, _ = qkv.shape
    n_blk = S // WIN
    n_pair = N_HEADS // 2

    def body(slopes_ref, q_ref, k_ref, v_ref, o_ref, lse_ref, qp, kp, vp, bias, acc, mx, sm, acc_n, mx_n, sm_n):
        hp = pl.program_id(1)
        _zero_once((kp, vp))
        for p, d in enumerate(DILATIONS):
            nb = n_blk // d
            chunks = _row_chunks(d, S)
            for src, dst, n in chunks:
                qp[dst:dst + n, :] = (q_ref[src, :] * ATT_SCALE).astype(BF16)
                kp[WIN + dst:WIN + dst + n, :] = k_ref[src, :].astype(BF16)
                vp[WIN + dst:WIN + dst + n, :] = v_ref[src, :].astype(BF16)
            _pair_bias(bias, slopes_ref, hp, d, key_major=False)
            acc_t, mx_t, sm_t = (acc_n, mx_n, sm_n) if d == 1 else (acc, mx, sm)

            def block(i, carry, p=p, nb=nb, acc_t=acc_t, mx_t=mx_t, sm_t=sm_t):
                cur = pl.ds(pl.multiple_of(i * WIN, WIN), WIN)
                keys = pl.ds(pl.multiple_of(i * WIN, WIN), KEYS)
                flag = ((i % nb) > 0).astype(jnp.int32)
                s = lax.dot_general(qp[cur, :], _stack_heads(kp[keys, :]), _NT, preferred_element_type=F32)
                s = s + bias[flag]
                es, ms, ls = [], [], []
                for h in range(2):
                    sh = s[:, h * KEYS:(h + 1) * KEYS]
                    m = jnp.max(jnp.maximum(sh[:, :WIN], sh[:, WIN:]), axis=1, keepdims=True)
                    e = jnp.exp(sh - m)
                    es.append(e.astype(BF16))
                    ms.append(m)
                    ls.append(jnp.sum(e[:, :WIN] + e[:, WIN:], axis=1, keepdims=True))
                acc_t[p, cur, :] = jnp.dot(jnp.concatenate(es, axis=1), _stack_heads(vp[keys, :]),
                                           preferred_element_type=F32)
                mx_t[p, cur, :] = _per_head(ms[0], ms[1], WIN)
                sm_t[p, cur, :] = _per_head(ls[0], ls[1], WIN)
                return carry

            lax.fori_loop(0, n_blk, block, 0, unroll=ATT_UNROLL)
            if d > 1:
                for src, dst, n in chunks:
                    acc_n[p, src, :] = acc[p, dst:dst + n, :]
                    mx_n[p, src, :] = mx[p, dst:dst + n, :]
                    sm_n[p, src, :] = sm[p, dst:dst + n, :]

        chunk = 256

        def merge(i, carry):
            rows = pl.ds(pl.multiple_of(i * chunk, chunk), chunk)
            ms = [mx_n[p, rows, :] for p in range(3)]
            m = jnp.maximum(jnp.maximum(ms[0], ms[1]), ms[2])
            ws = [jnp.exp(mp - m) for mp in ms]
            l = ws[0] * sm_n[0, rows, :] + ws[1] * sm_n[1, rows, :] + ws[2] * sm_n[2, rows, :]
            o = (ws[0] * acc_n[0, rows, :] + ws[1] * acc_n[1, rows, :] + ws[2] * acc_n[2, rows, :]) / l
            o_ref[rows, :] = o.astype(o_ref.dtype)
            lse = m + jnp.log(l)
            for h in range(2):
                lse_ref[rows, h:h + 1] = lse[:, h * HEAD_DIM:h * HEAD_DIM + 1]
            return carry

        lax.fori_loop(0, S // chunk, merge, 0)

    return pl.pallas_call(
        body, name="attention_fwd", grid=(B, n_pair),
        in_specs=[pl.BlockSpec(memory_space=pltpu.SMEM), _qkv_spec(S, 0), _qkv_spec(S, 1), _qkv_spec(S, 2)],
        out_specs=[pl.BlockSpec((None, S, LANES), lambda b, hp: (b, 0, hp)),
                   pl.BlockSpec((None, None, S, 2), lambda b, hp: (b, hp, 0, 0))],
        out_shape=[jax.ShapeDtypeStruct((B, S, ATT_WIDTH), BF16),
                   jax.ShapeDtypeStruct((B, n_pair, S, 2), F32)],
        scratch_shapes=[pltpu.VMEM((S, LANES), BF16), pltpu.VMEM((S + WIN, LANES), BF16),
                        pltpu.VMEM((S + WIN, LANES), BF16), pltpu.VMEM((2, WIN, 2 * KEYS), F32)]
        + [pltpu.VMEM((3, S, LANES), F32)] * 6,
        compiler_params=_params(2),
    )(slopes, qkv, qkv, qkv)


def _attention_bwd(qkv, o, do, lse, slopes):
    B, S, _ = qkv.shape
    n_blk = S // WIN
    n_pair = N_HEADS // 2

    def body(slopes_ref, q_ref, k_ref, v_ref, o_ref, do_ref, lse_ref, dx_ref,
             qp, dop, kp, vp, aux, auxp, aux_t, bias_t, dqp, dvk, dq_n, dk_n, dv_n):
        hp = pl.program_id(1)
        aux[...] = jnp.zeros_like(aux)
        for c0 in range(0, S, COPY_ROWS):
            rows = slice(c0, c0 + COPY_ROWS)
            prod = do_ref[rows, :] * o_ref[rows, :].astype(F32)
            for h in range(2):
                aux[rows, 2 * h:2 * h + 1] = lse_ref[rows, h:h + 1]
                aux[rows, 2 * h + 1:2 * h + 2] = jnp.sum(prod[:, h * HEAD_DIM:(h + 1) * HEAD_DIM], axis=1,
                                                         keepdims=True)
        dq_n[...] = jnp.zeros_like(dq_n)
        dk_n[...] = jnp.zeros_like(dk_n)
        dv_n[...] = jnp.zeros_like(dv_n)
        _zero_once((kp, vp))
        for p, d in enumerate(DILATIONS):
            nb = n_blk // d
            chunks = _row_chunks(d, S)
            for src, dst, n in chunks:
                auxp[dst:dst + n, :] = aux[src, :]
                qp[dst:dst + n, :] = (q_ref[src, :] * ATT_SCALE).astype(BF16)
                dop[dst:dst + n, :] = do_ref[src, :].astype(BF16)
                kp[WIN + dst:WIN + dst + n, :] = k_ref[src, :].astype(BF16)
                vp[WIN + dst:WIN + dst + n, :] = v_ref[src, :].astype(BF16)
            for i in range(n_blk):
                aux_t[i] = auxp[i * WIN:(i + 1) * WIN, :].T[0:8, :]
            _pair_bias(bias_t, slopes_ref, hp, d, key_major=True)
            dvk[...] = jnp.zeros_like(dvk)

            def block(i, carry, nb=nb):
                cur = pl.ds(pl.multiple_of(i * WIN, WIN), WIN)
                keys = pl.ds(pl.multiple_of(i * WIN, WIN), KEYS)
                flag = ((i % nb) > 0).astype(jnp.int32)
                q2, do2 = qp[cur, :], dop[cur, :]
                kc = _stack_heads(kp[keys, :])
                s_t = lax.dot_general(kc, q2, _NT, preferred_element_type=F32) + bias_t[flag]
                dp_t = lax.dot_general(_stack_heads(vp[keys, :]), do2, _NT, preferred_element_type=F32)
                ps, dss = [], []
                for h in range(2):
                    span = slice(h * KEYS, (h + 1) * KEYS)
                    p_t = jnp.exp(s_t[span, :] - aux_t[i, 2 * h:2 * h + 1, :])
                    ds_t = p_t * (dp_t[span, :] - aux_t[i, 2 * h + 1:2 * h + 2, :])
                    ps.append(p_t.astype(BF16))
                    dss.append(ds_t.astype(BF16))
                do_rows, q_rows = _stack_heads(do2), _stack_heads(q2)
                zr = jnp.zeros_like(do_rows)
                rhs = jnp.concatenate([jnp.concatenate([do_rows, zr], axis=1),
                                       jnp.concatenate([zr, q_rows], axis=1)], axis=0)
                dvk[keys, :] += jnp.dot(jnp.concatenate(ps + dss, axis=1), rhs, preferred_element_type=F32)
                dqp[cur, :] = lax.dot_general(jnp.concatenate(dss, axis=0), kc, _TN, preferred_element_type=F32)
                return carry

            lax.fori_loop(0, n_blk, block, 0, unroll=ATT_UNROLL)
            for src, dst, n in chunks:
                dq_n[src, :] += dqp[dst:dst + n, :]
                dv_n[src, :] += dvk[WIN + dst:WIN + dst + n, :LANES]
                dk_n[src, :] += dvk[WIN + dst:WIN + dst + n, LANES:]
        for c0 in range(0, S, COPY_ROWS):
            rows = slice(c0, c0 + COPY_ROWS)
            dx_ref[rows, 0:LANES] = (dq_n[rows, :] * ATT_SCALE).astype(dx_ref.dtype)
            dx_ref[rows, LANES:2 * LANES] = dk_n[rows, :].astype(dx_ref.dtype)
            dx_ref[rows, 2 * LANES:3 * LANES] = dv_n[rows, :].astype(dx_ref.dtype)

    pair = lambda width: pl.BlockSpec((None, S, width), lambda b, hp: (b, 0, hp))
    vm = lambda shape, dt: pltpu.VMEM(shape, dt)
    return pl.pallas_call(
        body, name="attention_bwd", grid=(B, n_pair),
        in_specs=[pl.BlockSpec(memory_space=pltpu.SMEM), _qkv_spec(S, 0), _qkv_spec(S, 1), _qkv_spec(S, 2),
                  pair(LANES), pair(LANES), pl.BlockSpec((None, None, S, 2), lambda b, hp: (b, hp, 0, 0))],
        out_specs=pair(3 * LANES),
        out_shape=jax.ShapeDtypeStruct((B, S, 3 * ATT_WIDTH), BF16),
        scratch_shapes=[vm((S, LANES), BF16), vm((S, LANES), BF16),
                        vm((S + WIN, LANES), BF16), vm((S + WIN, LANES), BF16),
                        vm((S, LANES), F32), vm((S, LANES), F32), vm((n_blk, 8, WIN), F32),
                        vm((2, 2 * KEYS, WIN), F32),
                        vm((S, LANES), F32), vm((S + WIN, 2 * LANES), F32),
                        vm((S, LANES), F32), vm((S, LANES), F32), vm((S, LANES), F32)],
        compiler_params=_params(2),
    )(slopes, qkv, qkv, qkv, o, do, lse)


SCAN_COLS = 256
SCAN_ROWS = 8


def _rows_to_tile(rows):
    rid = lax.broadcasted_iota(jnp.int32, (SCAN_ROWS, rows[0].shape[1]), 0)
    tile = jnp.broadcast_to(rows[0], rid.shape)
    for k in range(1, SCAN_ROWS):
        tile = jnp.where(rid == k, rows[k], tile)
    return tile


SCAN_UNROLL = 4


def _complex_powers(ar, ai, n):
    out = [(ar, ai)]
    for _ in range(n - 1):
        pr, pi = out[-1]
        out.append((pr * ar - pi * ai, pr * ai + pi * ar))
    return out


def _round_multipliers(powers, rid, reverse):
    out = []
    for s in (1, 2, 4):
        keep = (rid < SCAN_ROWS - s) if reverse else (rid >= s)
        out.append((jnp.where(keep, powers[s - 1][0], 0.0), jnp.where(keep, powers[s - 1][1], 0.0)))
    return out


def _tile_scan(xr, xi, multipliers, reverse):
    for s, (mr, mi) in zip((1, 2, 4), multipliers):
        shift = SCAN_ROWS - s if reverse else s
        sr, si = pltpu.roll(xr, shift, 0), pltpu.roll(xi, shift, 0)
        xr, xi = xr + (mr * sr - mi * si), xi + (mr * si + mi * sr)
    return xr, xi


SCAN_CHUNK = 256


def _scan_fwd(us, bb_big, a_row, cc_big):
    B, S, _ = us.shape
    groups = 2
    width = 2 * groups * SCAN_COLS
    nc = 2 * SSM_COLS // width
    nt = S // SCAN_ROWS
    tiles = SCAN_CHUNK // SCAN_ROWS
    LAST = slice(SCAN_ROWS - 1, SCAN_ROWS)

    def body(us_ref, bb_ref, a_ref, cc_ref, xs_ref, y_ref, bu_ref):
        bb = bb_ref[...].astype(BF16)
        for c in range(S // SCAN_CHUNK):
            part = jnp.dot(us_ref[c * SCAN_CHUNK:(c + 1) * SCAN_CHUNK, :].astype(BF16), bb,
                           preferred_element_type=F32)
            bu_ref[c * tiles:(c + 1) * tiles] = part.reshape(tiles, SCAN_ROWS, width)
        rid = lax.broadcasted_iota(jnp.int32, (SCAN_ROWS, SCAN_COLS), 0)
        consts = []
        for g in range(groups):
            re = slice(2 * g * SCAN_COLS, (2 * g + 1) * SCAN_COLS)
            im = slice((2 * g + 1) * SCAN_COLS, (2 * g + 2) * SCAN_COLS)
            powers = _complex_powers(a_ref[:, re], a_ref[:, im], SCAN_ROWS)
            carry_mult = (_rows_to_tile([p[0] for p in powers]), _rows_to_tile([p[1] for p in powers]))
            consts.append((re, im, carry_mult, _round_multipliers(powers, rid, reverse=False)))

        def tile(i, carry):
            out = []
            for (re, im, (cr_t, ci_t), rounds), (cr, ci) in zip(consts, carry):
                xr, xi = _tile_scan(bu_ref[i, :, re], bu_ref[i, :, im], rounds, reverse=False)
                xs_ref[i, :, re] = xr + (cr_t * cr - ci_t * ci)
                xs_ref[i, :, im] = xi + (cr_t * ci + ci_t * cr)
                out.append((xs_ref[i, LAST, re], xs_ref[i, LAST, im]))
            return tuple(out)

        zero = jnp.zeros((1, SCAN_COLS), F32)
        lax.fori_loop(0, nt, tile, ((zero, zero),) * groups, unroll=SCAN_UNROLL)

        @pl.when(pl.program_id(1) == 0)
        def _():
            y_ref[...] = jnp.zeros_like(y_ref)

        cc = cc_ref[...].astype(BF16)
        for c in range(S // SCAN_CHUNK):
            x2 = xs_ref[c * tiles:(c + 1) * tiles].reshape(SCAN_CHUNK, width).astype(BF16)
            y_ref[c * SCAN_CHUNK:(c + 1) * SCAN_CHUNK, :] += jnp.dot(x2, cc, preferred_element_type=F32)

    col = pl.BlockSpec((None, nt, SCAN_ROWS, width), lambda b, j: (b, 0, 0, j))
    tok = pl.BlockSpec((None, S, SSM_WIDTH), lambda b, j: (b, 0, 0))
    xs, y = pl.pallas_call(
        body, name="s5_scan_fwd", grid=(B, nc),
        in_specs=[tok, pl.BlockSpec((SSM_WIDTH, width), lambda b, j: (0, j)),
                  pl.BlockSpec((1, width), lambda b, j: (0, j)), pl.BlockSpec((width, SSM_WIDTH), lambda b, j: (j, 0))],
        out_specs=[col, tok],
        out_shape=[jax.ShapeDtypeStruct((B, nt, SCAN_ROWS, 2 * SSM_COLS), F32),
                   jax.ShapeDtypeStruct((B, S, SSM_WIDTH), F32)],
        scratch_shapes=[pltpu.VMEM((nt, SCAN_ROWS, width), F32)],
        compiler_params=_params(2),
    )(us, bb_big, a_row, cc_big)
    return xs.reshape(B, S, 2 * SSM_COLS), y


def _scan_bwd(dy, us, bb_big, cc_big, xs, a_row):
    B, S, _ = dy.shape
    width = 2 * SCAN_COLS
    nc = SSM_COLS // SCAN_COLS
    nt = S // SCAN_ROWS
    tiles = SCAN_CHUNK // SCAN_ROWS
    RE, IM = slice(0, SCAN_COLS), slice(SCAN_COLS, 2 * SCAN_COLS)
    FIRST, LAST = slice(0, 1), slice(SCAN_ROWS - 1, SCAN_ROWS)

    def body(dy_ref, us_ref, bb_ref, cc_ref, x_ref, a_ref, dus_ref, ga_ref, dbb_ref, dcc_ref, d_ref, lam_ref):
        b = pl.program_id(1)
        cc = cc_ref[...].astype(BF16)
        for c in range(S // SCAN_CHUNK):
            part = lax.dot_general(dy_ref[c * SCAN_CHUNK:(c + 1) * SCAN_CHUNK, :].astype(BF16), cc, _NT,
                                   preferred_element_type=F32)
            d_ref[c * tiles:(c + 1) * tiles] = part.reshape(tiles, SCAN_ROWS, width)
        powers = _complex_powers(a_ref[:, RE], -a_ref[:, IM], SCAN_ROWS)
        rid = lax.broadcasted_iota(jnp.int32, (SCAN_ROWS, SCAN_COLS), 0)
        cr_t = _rows_to_tile([powers[SCAN_ROWS - 1 - r][0] for r in range(SCAN_ROWS)])
        ci_t = _rows_to_tile([powers[SCAN_ROWS - 1 - r][1] for r in range(SCAN_ROWS)])
        rounds = _round_multipliers(powers, rid, reverse=True)

        @pl.when(b == 0)
        def _():
            ga_ref[...] = jnp.zeros_like(ga_ref)
            dbb_ref[...] = jnp.zeros_like(dbb_ref)
            dcc_ref[...] = jnp.zeros_like(dcc_ref)

        def tile(j, carry):
            cr, ci, accr, acci = carry
            i = nt - 1 - j
            lr, li = _tile_scan(d_ref[i, :, RE], d_ref[i, :, IM], rounds, reverse=True)
            lam_r = lr + (cr_t * cr - ci_t * ci)
            lam_i = li + (cr_t * ci + ci_t * cr)
            lam_ref[i, :, RE] = lam_r
            lam_ref[i, :, IM] = lam_i
            ip = jnp.maximum(i - 1, 0)
            keep = (i > 0).astype(F32)
            xpr = jnp.where(rid == 0, x_ref[ip, LAST, RE] * keep, pltpu.roll(x_ref[i, :, RE], 1, 0))
            xpi = jnp.where(rid == 0, x_ref[ip, LAST, IM] * keep, pltpu.roll(x_ref[i, :, IM], 1, 0))
            accr = accr + lam_r * xpr + lam_i * xpi
            acci = acci + lam_i * xpr - lam_r * xpi
            return lam_ref[i, FIRST, RE], lam_ref[i, FIRST, IM], accr, acci

        z1 = jnp.zeros((1, SCAN_COLS), F32)
        z8 = jnp.zeros((SCAN_ROWS, SCAN_COLS), F32)
        _, _, accr, acci = lax.fori_loop(0, nt, tile, (z1, z1, z8, z8), unroll=SCAN_UNROLL)
        ga_ref[:, RE] += _col_sum(accr)
        ga_ref[:, IM] += _col_sum(acci)

        bb = bb_ref[...].astype(BF16)
        for c in range(S // SCAN_CHUNK):
            rows = slice(c * SCAN_CHUNK, (c + 1) * SCAN_CHUNK)
            lam2 = lam_ref[c * tiles:(c + 1) * tiles].reshape(SCAN_CHUNK, width).astype(BF16)
            x2 = x_ref[c * tiles:(c + 1) * tiles].reshape(SCAN_CHUNK, width).astype(BF16)
            dus_ref[rows, :] = lax.dot_general(lam2, bb, _NT, preferred_element_type=F32)
            dbb_ref[...] += lax.dot_general(us_ref[rows, :].astype(BF16), lam2, _TN, preferred_element_type=F32)
            dcc_ref[...] += lax.dot_general(x2, dy_ref[rows, :].astype(BF16), _TN, preferred_element_type=F32)

    col = pl.BlockSpec((None, nt, SCAN_ROWS, width), lambda j, b: (b, 0, 0, j))
    tok = pl.BlockSpec((None, S, SSM_WIDTH), lambda j, b: (b, 0, 0))
    scratch = pltpu.VMEM((nt, SCAN_ROWS, width), F32)
    return pl.pallas_call(
        body, name="s5_scan_bwd", grid=(nc, B),
        in_specs=[tok, tok, pl.BlockSpec((SSM_WIDTH, width), lambda j, b: (0, j)),
                  pl.BlockSpec((width, SSM_WIDTH), lambda j, b: (j, 0)), col,
                  pl.BlockSpec((1, width), lambda j, b: (0, j))],
        out_specs=[pl.BlockSpec((None, None, S, SSM_WIDTH), lambda j, b: (j, b, 0, 0)),
                   pl.BlockSpec((1, width), lambda j, b: (0, j)),
                   pl.BlockSpec((SSM_WIDTH, width), lambda j, b: (0, j)),
                   pl.BlockSpec((width, SSM_WIDTH), lambda j, b: (j, 0))],
        out_shape=[jax.ShapeDtypeStruct((nc, B, S, SSM_WIDTH), F32), jax.ShapeDtypeStruct((1, 2 * SSM_COLS), F32),
                   jax.ShapeDtypeStruct((SSM_WIDTH, 2 * SSM_COLS), F32),
                   jax.ShapeDtypeStruct((2 * SSM_COLS, SSM_WIDTH), F32)],
        scratch_shapes=[scratch, scratch],
        compiler_params=_params(2),
    )(dy, us, bb_big, cc_big, xs.reshape(B, nt, SCAN_ROWS, 2 * SSM_COLS), a_row)


def _s5_discretise(lr, li, log_dt):
    dt = jnp.exp(log_dt)
    mag = jnp.exp(lr * dt)
    ang = li * dt
    ab_re, ab_im = mag * jnp.cos(ang), mag * jnp.sin(ang)
    nr, ni = ab_re - 1.0, ab_im
    den = lr * lr + li * li
    f_re = (nr * lr + ni * li) / den
    f_im = (ni * lr - nr * li) / den
    return dt, ab_re, ab_im, nr, ni, den, f_re, f_im


def _s5_params(a_re, a_im, log_dt):
    def body(lr_ref, li_ref, ld_ref, abr, abi, fr, fi):
        _, ab_re, ab_im, _, _, _, f_re, f_im = _s5_discretise(lr_ref[...], li_ref[...], ld_ref[...])
        abr[...] = ab_re
        abi[...] = ab_im
        fr[...] = f_re
        fi[...] = f_im

    return pl.pallas_call(body, name="s5_params",
                          out_shape=[jax.ShapeDtypeStruct(a_re.shape, F32)] * 4)(a_re, a_im, log_dt)


def _s5_input_matrix(f_re, f_im, b_re, b_im):
    def body(fr, fi, br, bi, o_re, o_im):
        o_re[...] = fr[...] * br[...] - fi[...] * bi[...]
        o_im[...] = fr[...] * bi[...] + fi[...] * br[...]

    return pl.pallas_call(body, name="s5_input_matrix",
                          out_shape=[jax.ShapeDtypeStruct(b_re.shape, F32)] * 2)(f_re, f_im, b_re, b_im)


def _s5_input_matrix_bwd(f_re, f_im, b_re, b_im, g_re, g_im):
    def body(fr, fi, br, bi, gr, gi, dbr, dbi, dfr, dfi):
        dbr[...] = fr[...] * gr[...] + fi[...] * gi[...]
        dbi[...] = fr[...] * gi[...] - fi[...] * gr[...]
        dfr[...] = jnp.sum(br[...] * gr[...] + bi[...] * gi[...], axis=1, keepdims=True)
        dfi[...] = jnp.sum(br[...] * gi[...] - bi[...] * gr[...], axis=1, keepdims=True)

    return pl.pallas_call(
        body, name="s5_input_matrix_bwd",
        out_shape=[jax.ShapeDtypeStruct(b_re.shape, F32)] * 2 + [jax.ShapeDtypeStruct(f_re.shape, F32)] * 2,
    )(f_re, f_im, b_re, b_im, g_re, g_im)


def _s5_params_bwd(a_re, a_im, log_dt, g_ab_re, g_ab_im, d_f_re, d_f_im):
    def body(lr_ref, li_ref, ld_ref, gar, gai, dfr, dfi, o_lr, o_li, o_ld):
        lr, li = lr_ref[...], li_ref[...]
        dt, ab_re, ab_im, nr, ni, den, f_re, f_im = _s5_discretise(lr, li, ld_ref[...])
        d_fr, d_fi = dfr[...], dfi[...]
        d_nr = (d_fr * lr - d_fi * li) / den
        d_ni = (d_fr * li + d_fi * lr) / den
        common = (d_fr * f_re + d_fi * f_im) * 2.0 / den
        d_lr = (d_fr * nr + d_fi * ni) / den - common * lr
        d_li = (d_fr * ni - d_fi * nr) / den - common * li
        d_abr = gar[...] + d_nr
        d_abi = gai[...] + d_ni
        d_mag_mag = d_abr * ab_re + d_abi * ab_im
        d_ang = d_abi * ab_re - d_abr * ab_im
        o_lr[...] = d_lr + d_mag_mag * dt
        o_li[...] = d_li + d_ang * dt
        o_ld[...] = jnp.sum(d_mag_mag * lr + d_ang * li, axis=1, keepdims=True) * dt

    return pl.pallas_call(
        body, name="s5_params_bwd",
        out_shape=[jax.ShapeDtypeStruct(a_re.shape, F32)] * 2 + [jax.ShapeDtypeStruct(log_dt.shape, F32)],
    )(a_re, a_im, log_dt, g_ab_re, g_ab_im, d_f_re, d_f_im)


CONV_COLS = 256


def _shift_down(v, j, row):
    return jnp.where(row >= j, pltpu.roll(v, j, 0), 0.0)


def _shift_up(v, j, row, seq):
    return jnp.where(row < seq - j, pltpu.roll(v, seq - j, 0), 0.0)


def _conv_fwd(up, w_conv, b_conv):
    B, S, _ = up.shape
    nj = D_FF // CONV_COLS

    def body(up_ref, w_ref, b_ref, ff_ref):
        a = up_ref[:, :CONV_COLS].astype(F32)
        val = up_ref[:, CONV_COLS:].astype(F32)
        row = lax.broadcasted_iota(jnp.int32, a.shape, 0)
        w0, w1, w2 = w_ref[0:1, :], w_ref[1:2, :], w_ref[2:3, :]
        conv = b_ref[...] + w0 * a + w1 * _shift_down(a, 1, row) + w2 * _shift_down(a, 2, row)
        ff_ref[...] = (conv * _sigmoid(conv) * val).astype(ff_ref.dtype)

    return pl.pallas_call(
        body, name="conv_gate_fwd", grid=(B, nj),
        in_specs=[pl.BlockSpec((None, S, 2 * CONV_COLS), lambda b, j: (b, 0, j)),
                  pl.BlockSpec((3, CONV_COLS), lambda b, j: (0, j)),
                  pl.BlockSpec((1, CONV_COLS), lambda b, j: (0, j))],
        out_specs=pl.BlockSpec((None, S, CONV_COLS), lambda b, j: (b, 0, j)),
        out_shape=jax.ShapeDtypeStruct((B, S, D_FF), BF16),
        compiler_params=_params(2),
    )(up, w_conv, b_conv)


def _conv_bwd(up, d_ff, w_conv, b_conv):
    B, S, _ = up.shape
    nj = D_FF // CONV_COLS

    def body(up_ref, dff_ref, w_ref, b_ref, dup_ref, dw_ref, db_ref):
        b = pl.program_id(1)
        a = up_ref[:, :CONV_COLS].astype(F32)
        val = up_ref[:, CONV_COLS:].astype(F32)
        row = lax.broadcasted_iota(jnp.int32, a.shape, 0)
        w0, w1, w2 = w_ref[0:1, :], w_ref[1:2, :], w_ref[2:3, :]
        a1, a2 = _shift_down(a, 1, row), _shift_down(a, 2, row)
        conv = b_ref[...] + w0 * a + w1 * a1 + w2 * a2
        sg = _sigmoid(conv)
        dff = dff_ref[...].astype(F32)
        d_val = dff * conv * sg
        dc = dff * val * (sg * (1.0 + conv * (1.0 - sg)))
        d_a = w0 * dc + w1 * _shift_up(dc, 1, row, S) + w2 * _shift_up(dc, 2, row, S)
        dup_ref[:, :CONV_COLS] = d_a.astype(dup_ref.dtype)
        dup_ref[:, CONV_COLS:] = d_val.astype(dup_ref.dtype)

        @pl.when(b == 0)
        def _():
            dw_ref[...] = jnp.zeros_like(dw_ref)
            db_ref[...] = jnp.zeros_like(db_ref)

        dw_ref[0:1, :] += _col_sum(dc * a)
        dw_ref[1:2, :] += _col_sum(dc * a1)
        dw_ref[2:3, :] += _col_sum(dc * a2)
        db_ref[...] += _col_sum(dc)

    return pl.pallas_call(
        body, name="conv_gate_bwd", grid=(nj, B),
        in_specs=[pl.BlockSpec((None, S, 2 * CONV_COLS), lambda j, b: (b, 0, j)),
                  pl.BlockSpec((None, S, CONV_COLS), lambda j, b: (b, 0, j)),
                  pl.BlockSpec((3, CONV_COLS), lambda j, b: (0, j)),
                  pl.BlockSpec((1, CONV_COLS), lambda j, b: (0, j))],
        out_specs=[pl.BlockSpec((None, S, 2 * CONV_COLS), lambda j, b: (b, 0, j)),
                   pl.BlockSpec((3, CONV_COLS), lambda j, b: (0, j)),
                   pl.BlockSpec((1, CONV_COLS), lambda j, b: (0, j))],
        out_shape=[jax.ShapeDtypeStruct((B, S, 2 * D_FF), BF16), jax.ShapeDtypeStruct((3, D_FF), F32),
                   jax.ShapeDtypeStruct((1, D_FF), F32)],
        compiler_params=_params(2),
    )(up, d_ff, w_conv, b_conv)


def _ada_fwd(c_all, w_ada, b_ada):
    def body(c_ref, w_ref, b_ref, o_ref):
        cv = c_ref[...]
        act = (cv * _sigmoid(cv)).astype(BF16)
        o_ref[...] = jnp.dot(act, w_ref[...].astype(BF16), preferred_element_type=F32) + b_ref[...]

    return pl.pallas_call(body, name="ada_fwd",
                          out_shape=jax.ShapeDtypeStruct((c_all.shape[0], w_ada.shape[1]), F32),
                          compiler_params=pltpu.CompilerParams(vmem_limit_bytes=V7X_VMEM_LIMIT))(c_all, w_ada, b_ada)


def _ada_bwd(c_all, dmod_all, dmod_cols):
    def body(c_ref, dm_ref, dmc_ref, dw_ref, db_ref):
        cv = c_ref[...]
        act = (cv * _sigmoid(cv)).astype(BF16)
        dw_ref[...] = lax.dot_general(act, dmc_ref[...].astype(BF16), _TN, preferred_element_type=F32)
        db_ref[...] = _col_sum(dm_ref[...])

    return pl.pallas_call(
        body, name="ada_bwd",
        out_shape=[jax.ShapeDtypeStruct((c_all.shape[1], dmod_cols.shape[1]), F32),
                   jax.ShapeDtypeStruct((1, dmod_all.shape[1]), F32)],
        compiler_params=pltpu.CompilerParams(vmem_limit_bytes=V7X_VMEM_LIMIT))(c_all, dmod_all, dmod_cols)


def _adamw(w, m, v, g_parts, name, own=None):
    R, C = w.shape
    P = g_parts.shape[0]
    tr = R
    for cand in (256, 128, 64, 32, 16, 8):
        if R % cand == 0 and cand * C * 4 * (P + 8) * 2 <= V7X_VMEM_LIMIT // 2:
            tr = cand
            break
    c1 = 1.0 / (1.0 - ADAM_B1 ** ADAM_STEP)
    c2 = 1.0 / (1.0 - ADAM_B2 ** ADAM_STEP)

    def update(w_ref, m_ref, v_ref, g, og, od, om, ov):
        m_new = ADAM_B1 * m_ref[...] + (1.0 - ADAM_B1) * g
        v_new = ADAM_B2 * v_ref[...] + (1.0 - ADAM_B2) * (g * g)
        og[...] = g
        om[...] = m_new
        ov[...] = v_new
        od[...] = -ADAM_LR * ((m_new * c1) / (jnp.sqrt(v_new * c2) + ADAM_EPS) + ADAM_WD * w_ref[...])

    def total(g_ref):
        g = g_ref[0].astype(F32)
        for p in range(1, P):
            g = g + g_ref[p].astype(F32)
        return g

    out_shape = [jax.ShapeDtypeStruct((R, C), F32)] * 4
    if own is None:
        def body(w_ref, m_ref, v_ref, g_ref, og, od, om, ov):
            update(w_ref, m_ref, v_ref, total(g_ref), og, od, om, ov)

        spec = pl.BlockSpec((tr, C), lambda i: (i, 0))
        return pl.pallas_call(
            body, name=name, grid=(R // tr,),
            in_specs=[spec, spec, spec, pl.BlockSpec((P, tr, C), lambda i: (0, i, 0))],
            out_specs=[spec] * 4, out_shape=out_shape, compiler_params=_params(1),
        )(w, m, v, g_parts)

    slots, me = own

    def body_own(me_ref, w_ref, m_ref, v_ref, g_ref, own_ref, og, od, om, ov):
        g = own_ref[...].astype(F32)
        for p in range(P):
            g = g + jnp.where(me_ref[0] == p, 0.0, g_ref[p].astype(F32))
        update(w_ref, m_ref, v_ref, g, og, od, om, ov)

    spec = pl.BlockSpec((tr, C), lambda i, me_ref: (i, 0))
    grid_spec = pltpu.PrefetchScalarGridSpec(
        num_scalar_prefetch=1, grid=(R // tr,),
        in_specs=[spec, spec, spec, pl.BlockSpec((P, tr, C), lambda i, me_ref: (0, i, 0)),
                  pl.BlockSpec((None, tr, C), lambda i, me_ref: (me_ref[0], i, 0))],
        out_specs=[spec] * 4)
    return pl.pallas_call(body_own, name=name, grid_spec=grid_spec, out_shape=out_shape,
                          compiler_params=_params(1))(me, w, m, v, g_parts, slots)


def _sum_parts(parts, loss_rows):
    P, R, C = parts.shape
    lo, hi = loss_rows

    def body(p_ref, o_ref, loss_ref):
        t = p_ref[0]
        for p in range(1, P):
            t = t + p_ref[p]
        o_ref[...] = t
        tot = jnp.sum(jnp.sum(o_ref[lo:hi, :], axis=1, keepdims=True), axis=0, keepdims=True)
        loss_ref[...] = jnp.broadcast_to(tot, loss_ref.shape)

    return pl.pallas_call(body, name="sum_small_grads",
                          out_shape=[jax.ShapeDtypeStruct((R, C), F32), jax.ShapeDtypeStruct((1, LANES), F32)],
                          compiler_params=pltpu.CompilerParams(vmem_limit_bytes=V7X_VMEM_LIMIT))(parts)


def _exchange(items, name):
    n = len(items)
    MESH = pl.DeviceIdType.MESH

    def body(*refs):
        src, dst = refs[:n], refs[n:2 * n]
        send_sems, recv_sems, local_sems = refs[2 * n:]
        x, y, c = lax.axis_index("x"), lax.axis_index("y"), lax.axis_index("c")
        me = 4 * x + 2 * y + c
        started = []
        for it, (_, per_peer) in enumerate(items):
            own = pltpu.make_async_copy(src[it].at[me] if per_peer else src[it], dst[it].at[me], local_sems.at[it])
            own.start()
            started.append(own)
        sends, recvs = [], []
        for k in range(1, N_DEV):
            px = 1 - x if k & 4 else x
            py = 1 - y if k & 2 else y
            pc = 1 - c if k & 1 else c
            peer = 4 * px + 2 * py + pc
            for it, (_, per_peer) in enumerate(items):
                s = src[it].at[peer] if per_peer else src[it]
                cp = pltpu.make_async_remote_copy(src_ref=s, dst_ref=dst[it].at[me], send_sem=send_sems.at[it, k - 1],
                                                  recv_sem=recv_sems.at[it, k - 1], device_id=(px, py, pc),
                                                  device_id_type=MESH)
                cp.start()
                sends.append(cp)
                recvs.append(pltpu.make_async_remote_copy(
                    src_ref=s, dst_ref=dst[it].at[peer], send_sem=send_sems.at[it, k - 1],
                    recv_sem=recv_sems.at[it, k - 1], device_id=(px, py, pc), device_id_type=MESH))
        for cp in recvs:
            cp.wait_recv()
        for cp in sends:
            cp.wait_send()
        for cp in started:
            cp.wait()

    any_spec = pl.BlockSpec(memory_space=pl.ANY)
    out_shape = []
    for a, per_peer in items:
        shp = a.shape if per_peer else (N_DEV,) + a.shape
        out_shape.append(jax.ShapeDtypeStruct(shp, a.dtype))
    return pl.pallas_call(
        body, name=name, in_specs=[any_spec] * n, out_specs=[any_spec] * n, out_shape=out_shape,
        scratch_shapes=[pltpu.SemaphoreType.DMA((n, N_DEV - 1)), pltpu.SemaphoreType.DMA((n, N_DEV - 1)),
                        pltpu.SemaphoreType.DMA((n,))],
    )(*[a for a, _ in items])


def _remote(src, dst, send_sem, recv_sem, device):
    return pltpu.make_async_remote_copy(src_ref=src, dst_ref=dst, send_sem=send_sem, recv_sem=recv_sem,
                                        device_id=device, device_id_type=pl.DeviceIdType.MESH)


def _mesh_place():
    x, y, c = lax.axis_index("x"), lax.axis_index("y"), lax.axis_index("c")
    other_chips = [(1 - x, y), (x, 1 - y), (1 - x, 1 - y)]
    return x, y, c, (x, y, 1 - c), other_chips


def _gather_all(items, name):
    n = len(items)

    def body(*refs):
        src, dst = refs[:n], refs[n:2 * n]
        send_sems, recv_sems, local_sems = refs[2 * n:]
        x, y, c, sibling, chips = _mesh_place()
        slot = lambda px, py, pc: 4 * px + 2 * py + pc
        me = slot(x, y, c)
        own = [pltpu.make_async_copy(src[it], dst[it].at[me], local_sems.at[it]) for it in range(n)]
        first = []
        for it in range(n):
            first.append(_remote(src[it], dst[it].at[me], send_sems.at[it, 0], recv_sems.at[it, 0], sibling))
            for j, chip in enumerate(chips):
                first.append(_remote(src[it], dst[it].at[me], send_sems.at[it, 1 + j], recv_sems.at[it, 1 + j],
                                     (*chip, c)))
        for cp in own + first:
            cp.start()
        passed = []
        for j, chip in enumerate(chips):
            blk = slot(*chip, c)
            for it in range(n):
                _remote(src[it], dst[it].at[blk], send_sems.at[it, 1 + j], recv_sems.at[it, 1 + j],
                        (*chip, c)).wait_recv()
                fwd = _remote(dst[it].at[blk], dst[it].at[blk], send_sems.at[it, 4 + j], recv_sems.at[it, 4 + j],
                              sibling)
                fwd.start()
                passed.append(fwd)
        for it in range(n):
            _remote(src[it], dst[it].at[slot(x, y, 1 - c)], send_sems.at[it, 0], recv_sems.at[it, 0],
                    sibling).wait_recv()
        for j, chip in enumerate(chips):
            for it in range(n):
                _remote(src[it], dst[it].at[slot(*chip, 1 - c)], send_sems.at[it, 4 + j], recv_sems.at[it, 4 + j],
                        sibling).wait_recv()
        for cp in first + passed:
            cp.wait_send()
        for cp in own:
            cp.wait()

    any_spec = pl.BlockSpec(memory_space=pl.ANY)
    return pl.pallas_call(
        body, name=name, in_specs=[any_spec] * n, out_specs=[any_spec] * n,
        out_shape=[jax.ShapeDtypeStruct((N_DEV,) + a.shape, a.dtype) for a in items],
        scratch_shapes=[pltpu.SemaphoreType.DMA((n, 7)), pltpu.SemaphoreType.DMA((n, 7)),
                        pltpu.SemaphoreType.DMA((n,))],
    )(*items)


def _peers():
    x, y, c = lax.axis_index("x"), lax.axis_index("y"), lax.axis_index("c")
    out = []
    for k in range(1, N_DEV):
        px = 1 - x if k & 4 else x
        py = 1 - y if k & 2 else y
        pc = 1 - c if k & 1 else c
        out.append((k, (px, py, pc), 4 * px + 2 * py + pc))
    return 4 * x + 2 * y + c, out


def _exchange_start(items, name, gather, carry=()):
    n, m = len(items), len(carry)

    def body(*refs):
        src, land = refs[:n], refs[n:2 * n]
        first_out = 2 * n + m
        send_sems, recv_sems = refs[first_out:first_out + n], refs[first_out + n:first_out + 2 * n]
        token = refs[-1]
        me, peers = _peers()
        for k, peer, slot in peers:
            for it in range(n):
                _remote(src[it] if gather else src[it].at[slot], land[it].at[me], send_sems[it], recv_sems[it],
                        peer).start()
        token[...] = jnp.zeros_like(token)

    hbm = pl.BlockSpec(memory_space=pltpu.HBM)
    sem = pl.BlockSpec(memory_space=pltpu.SEMAPHORE)
    land_shapes = [(N_DEV,) + (a.shape if gather else a.shape[1:]) for a in items]
    lands = [lax.empty(shp, a.dtype) for shp, a in zip(land_shapes, items)]
    through = list(items) + lands + list(carry)
    outs = pl.pallas_call(
        body, name=name,
        out_shape=(*[pltpu.SemaphoreType.DMA(())] * (2 * n), *[pltpu.HBM(a.shape, a.dtype) for a in through],
                   jax.ShapeDtypeStruct((8, LANES), F32)),
        in_specs=[hbm] * len(through),
        out_specs=(*[sem] * (2 * n), *[hbm] * len(through), pl.BlockSpec(memory_space=pltpu.VMEM)),
        input_output_aliases={i: 2 * n + i for i in range(len(through))},
        compiler_params=pltpu.CompilerParams(has_side_effects=pltpu.SideEffectType.DATAFLOW_SIDE_EFFECTING),
    )(*[pltpu.with_memory_space_constraint(a, pltpu.HBM) for a in through])
    return (list(outs[:n]), list(outs[n:2 * n]), list(outs[2 * n:3 * n]), list(outs[3 * n:4 * n]), outs[-1],
            list(outs[4 * n:4 * n + m]))


def _exchange_wait(send_sems, recv_sems, items, lands, after, name):
    n = len(items)

    def body(*refs):
        land = refs[n:2 * n]
        send_sems, recv_sems = refs[2 * n:3 * n], refs[3 * n:4 * n]
        me, peers = _peers()
        for it in range(n):
            seven = land[it].at[pl.ds(0, N_DEV - 1)]
            cp = _remote(seven, seven, send_sems[it], recv_sems[it], peers[0][1])
            cp.wait_send()
            cp.wait_recv()

    hbm = pl.BlockSpec(memory_space=pltpu.HBM)
    sem = pl.BlockSpec(memory_space=pltpu.SEMAPHORE)
    outs = pl.pallas_call(
        body, name=name,
        out_shape=tuple(pltpu.HBM(a.shape, a.dtype) for a in list(items) + list(lands)),
        in_specs=[hbm] * (2 * n) + [sem] * (2 * n) + [pl.BlockSpec(memory_space=pl.ANY)],
        out_specs=tuple([hbm] * (2 * n)),
        input_output_aliases={i: i for i in range(2 * n)},
        compiler_params=pltpu.CompilerParams(has_side_effects=pltpu.SideEffectType.DATAFLOW_SIDE_EFFECTING),
    )(*items, *lands, *send_sems, *recv_sems, after)
    return list(outs[:n]), list(outs[n:])


def _gelu_tanh(y):
    k = math.sqrt(2.0 / math.pi)
    t = jnp.tanh(k * (y + 0.044715 * y * y * y))
    return 0.5 * y * (1.0 + t), t


def _local_step(x, mod, target, W, late_weights, P, send_early):
    B, S, D = x.shape
    T = B * S
    TS = 512
    flat = lambda a: a.reshape(T, a.shape[-1])
    unflat = lambda a: a.reshape(B, S, a.shape[-1])
    mod_col = lambda i: (mod, D, i)

    def f_modnorm(xv, sc, sh, g):
        return (xv * _rms_scale(xv) * g) * (1.0 + sc) + sh

    (u1,) = _rowwise(f_modnorm, [(x, D, 0)], [mod_col(1), mod_col(0)], [P["g_mix"]],
                     [(D, BF16)], [], [], ts=TS, name="modnorm_mix")
    u1f = flat(u1)
    qkv = unflat(_matmul(u1f, W["w_qkv"], name="proj_qkv"))
    us = unflat(_matmul(u1f, W["w_us"], name="proj_ssm_in"))
    gates = unflat(_matmul(u1f, W["w_gates"], out_dtype=BF16, name="proj_gates"))

    o_att, lse = _attention_fwd(qkv, P["slopes"])
    more_w, more_p = late_weights(o_att)
    W, P = {**W, **more_w}, {**P, **more_p}
    y_att = unflat(_matmul(flat(o_att), W["w_proj_att"], out_dtype=BF16, name="proj_att"))

    xs, y_mm = _scan_fwd(us, P["bb_big"], P["a_row"], P["cc_big"])

    def f_glu(ymm, usv, dsk, wg, bg):
        yv = ymm + dsk * usv
        ge, _ = _gelu_tanh(yv)
        pre = jnp.dot(ge.astype(BF16), wg, preferred_element_type=F32) + bg
        return yv, ge * _sigmoid(pre)

    y_s5, z = _rowwise(f_glu, [(y_mm, SSM_WIDTH, 0), (us, SSM_WIDTH, 0)], [], [P["d_skip"], W["w_glu"], P["b_glu"]],
                       [(SSM_WIDTH, F32), (SSM_WIDTH, BF16)], [], [], ts=TS, name="s5_glu")
    y_ssm = unflat(_matmul(flat(z), W["w_proj_ssm"], out_dtype=BF16, name="proj_ssm"))

    def f_merge(ga, gs, ya, ys, bga, bgs):
        return _sigmoid(ga + bga) * ya + _sigmoid(gs + bgs) * ys

    bga, bgs = P["b_gate"][:, :D], P["b_gate"][:, D:]
    (merged,) = _rowwise(f_merge, [(gates, D, 0), (gates, D, 1), (y_att, D, 0), (y_ssm, D, 0)], [], [bga, bgs],
                         [(D, BF16)], [], [], ts=TS, name="gate_merge")
    mix = unflat(_matmul(flat(merged), W["w_out"], out_dtype=BF16, name="proj_out"))

    def f_res_modnorm(xv, mx, gt, sc, sh, g):
        h = xv + gt * mx
        return h, (h * _rms_scale(h) * g) * (1.0 + sc) + sh

    h1, u2 = _rowwise(f_res_modnorm, [(x, D, 0), (mix, D, 0)], [mod_col(2), mod_col(4), mod_col(3)], [P["g_ffn"]],
                      [(D, F32), (D, BF16)], [], [], ts=TS, name="residual_modnorm_ffn")
    up = unflat(_up_fwd(flat(u2), W["w_up"], name="ffn_up"))
    ff = _conv_fwd(up, P["w_conv"], P["b_conv"])
    down = unflat(_matmul(flat(ff), W["w_down"], out_dtype=BF16, name="ffn_down"))

    def f_head(h1v, dn, tg, gt, g):
        h2 = h1v + gt * dn
        r = _rms_scale(h2)
        nh = h2 * r
        e = nh * g - tg
        dy = e * (1.0 / D)
        gy = dy * g
        dh = r * (gy - nh * jnp.mean(gy * nh, axis=-1, keepdims=True))
        return (dh, dh * gt, _col_sum(dh * dn), _col_sum(dy * nh), _col_sum(e * e) * (0.5 / D))

    dh2, d_down, d_gt2, d_g_final, loss_cols = _rowwise(
        f_head, [(h1, D, 0), (down, D, 0), (target, D, 0)], [mod_col(5)], [P["g_final"]],
        [(D, BF16), (D, BF16)], [D], [(1, D), (1, D)], ts=TS, name="head_loss")

    d_downf = flat(d_down)
    d_ff = unflat(_matmul(d_downf, W["w_down"], tb=True, out_dtype=BF16, name="ffn_down_dx"))
    d_w_down = _matmul(flat(ff), d_downf, ta=True, out_dtype=BF16, name="ffn_down_dw")
    d_up, d_w_conv, d_b_conv = _conv_bwd(up, d_ff, P["w_conv"], P["b_conv"])
    d_upf = flat(d_up)
    d_u2 = unflat(_up_dx(d_upf, W["w_up"], name="ffn_up_dx"))
    d_w_up = _up_dw(flat(u2), d_upf, name="ffn_up_dw")
    token, _ = send_early(dict(w_down=d_w_down.reshape(N_DEV, D_FF // N_DEV, D), w_up=d_w_up))
    g_ffn_after = P["g_ffn"] + token[0:1, 0:1]

    def f_modnorm_bwd(du, h, dres, mx, sc, gt, g):
        r = _rms_scale(h)
        nh = h * r
        dn = du * (1.0 + sc)
        gy = dn * g
        dh = dres + r * (gy - nh * jnp.mean(gy * nh, axis=-1, keepdims=True))
        return (dh, dh * gt, _col_sum(du), _col_sum(du * nh * g), _col_sum(dh * mx), _col_sum(dn * nh))

    dh1, d_mix, d_sh2, d_sc2, d_gt1, d_g_ffn = _rowwise(
        f_modnorm_bwd, [(d_u2, D, 0), (h1, D, 0), (dh2, D, 0), (mix, D, 0)], [mod_col(4), mod_col(2)], [g_ffn_after],
        [(D, BF16), (D, BF16)], [D, D, D], [(1, D)], ts=TS, name="modnorm_ffn_bwd")

    d_mixf = flat(d_mix)
    d_merged = unflat(_matmul(d_mixf, W["w_out"], tb=True, out_dtype=BF16, name="proj_out_dx"))
    d_w_out = _matmul(flat(merged), d_mixf, ta=True, out_dtype=BF16, name="proj_out_dw")

    def f_merge_bwd(dm, ga, gs, ya, ys, bga_, bgs_):
        sa, ss = _sigmoid(ga + bga_), _sigmoid(gs + bgs_)
        dga = dm * ya * sa * (1.0 - sa)
        dgs = dm * ys * ss * (1.0 - ss)
        return dm * sa, dm * ss, jnp.concatenate([dga, dgs], axis=1), _col_sum(dga), _col_sum(dgs)

    d_y_att, d_y_ssm, d_gates, d_bga, d_bgs = _rowwise(
        f_merge_bwd, [(d_merged, D, 0), (gates, D, 0), (gates, D, 1), (y_att, D, 0), (y_ssm, D, 0)], [], [bga, bgs],
        [(D, BF16), (D, BF16), (2 * D, BF16)], [], [(1, D), (1, D)], ts=TS, name="gate_merge_bwd")

    d_yaf, d_ysf = flat(d_y_att), flat(d_y_ssm)
    d_o_att = unflat(_matmul(d_yaf, W["w_proj_att"], tb=True, name="proj_att_dx"))
    d_w_proj_att = _matmul(flat(o_att), d_yaf, ta=True, out_dtype=BF16, name="proj_att_dw")
    d_z = unflat(_matmul(d_ysf, W["w_proj_ssm"], tb=True, out_dtype=BF16, name="proj_ssm_dx"))
    d_w_proj_ssm = _matmul(flat(z), d_ysf, ta=True, out_dtype=BF16, name="proj_ssm_dw")

    def f_glu_bwd(yv, dz, usv, dsk, wg, bg):
        ge, t = _gelu_tanh(yv)
        pre = jnp.dot(ge.astype(BF16), wg, preferred_element_type=F32) + bg
        sg = _sigmoid(pre)
        dpre = dz * ge * sg * (1.0 - sg)
        dge = dz * sg + lax.dot_general(dpre.astype(BF16), wg, _NT, preferred_element_type=F32)
        k = math.sqrt(2.0 / math.pi)
        dgelu = 0.5 * (1.0 + t) + 0.5 * yv * (1.0 - t * t) * k * (1.0 + 3.0 * 0.044715 * yv * yv)
        dy = dge * dgelu
        dwg = lax.dot_general(ge.astype(BF16), dpre.astype(BF16), _TN, preferred_element_type=F32)
        return dy, dy * dsk, dwg, _col_sum(dpre), _col_sum(dy * usv)

    d_y_s5, d_us_skip, d_w_glu, d_b_glu, d_d_skip = _rowwise(
        f_glu_bwd, [(y_s5, SSM_WIDTH, 0), (d_z, SSM_WIDTH, 0), (us, SSM_WIDTH, 0)], [],
        [P["d_skip"], W["w_glu"], P["b_glu"]],
        [(SSM_WIDTH, BF16), (SSM_WIDTH, F32)], [], [(SSM_WIDTH, SSM_WIDTH), (1, SSM_WIDTH), (1, SSM_WIDTH)],
        ts=TS, name="s5_glu_bwd")
    d_us_parts, g_ab, d_bb, d_cc = _scan_bwd(d_y_s5, us, P["bb_big"], P["cc_big"], xs, P["a_row"])

    token, _ = send_early(dict(
        w_out=d_w_out.reshape(N_DEV, D // N_DEV, D), w_proj_att=_cols_to_slots(d_w_proj_att),
        w_proj_ssm=_cols_to_slots(d_w_proj_ssm),
        w_glu=d_w_glu.astype(BF16).reshape(N_DEV, SSM_WIDTH // N_DEV, SSM_WIDTH),
        w_conv=_cols_to_slots(d_w_conv.astype(BF16))))
    d_qkv = _attention_bwd(qkv, o_att, d_o_att, lse, P["slopes"] + token[0, 0])

    def f_add(*parts):
        return sum(parts[1:], parts[0])

    (d_us,) = _rowwise(f_add, [(d_us_parts[j], SSM_WIDTH, 0) for j in range(d_us_parts.shape[0])]
                       + [(d_us_skip, SSM_WIDTH, 0)], [], [],
                       [(SSM_WIDTH, BF16)], [], [], ts=TS, name="s5_input_grad")
    d_qkvf = flat(d_qkv)
    d_usf = flat(d_us)
    d_gatesf = flat(d_gates)
    d_w_in = jnp.concatenate(
        [_unpair_qkv_columns(_matmul(u1f, d_qkvf, ta=True, out_dtype=BF16, name="proj_qkv_dw")),
         _matmul(u1f, d_usf, ta=True, out_dtype=BF16, name="proj_ssm_in_dw"),
         _matmul(u1f, d_gatesf, ta=True, out_dtype=BF16, name="proj_gates_dw")], axis=1)
    token, (w_qkv, w_us, w_gates) = send_early(dict(w_in=_cols_to_slots(d_w_in)),
                                               carry=[W["w_qkv"], W["w_us"], W["w_gates"]])
    d_u1 = (_matmul(d_qkvf, w_qkv, tb=True, out_dtype=BF16, name="proj_qkv_dx"),
            _matmul(d_usf, w_us, tb=True, out_dtype=BF16, name="proj_ssm_in_dx"),
            _matmul(d_gatesf, w_gates, tb=True, out_dtype=BF16, name="proj_gates_dx"))

    def f_modnorm_bwd_in(du0, du1, du2, h, dres, sc, g):
        du = du0 + du1 + du2
        r = _rms_scale(h)
        nh = h * r
        dn = du * (1.0 + sc)
        gy = dn * g
        dh = dres + r * (gy - nh * jnp.mean(gy * nh, axis=-1, keepdims=True))
        return (dh, _col_sum(du), _col_sum(du * nh * g), _col_sum(dn * nh))

    grad_x, d_sh1, d_sc1, d_g_mix = _rowwise(
        f_modnorm_bwd_in, [(unflat(d_u1[0]), D, 0), (unflat(d_u1[1]), D, 0), (unflat(d_u1[2]), D, 0), (x, D, 0),
                           (dh1, D, 0)], [mod_col(1)], [P["g_mix"] + token[0:1, 0:1]],
        [(D, F32)], [D, D], [(1, D)], ts=TS, name="modnorm_mix_bwd")

    d_mod = jnp.concatenate([d_sh1, d_sc1, d_gt1, d_sh2, d_sc2, d_gt2], axis=-1)
    g_ab_re, g_ab_im = _deinterleave(g_ab)
    d_bb_re, d_bb_im = _deinterleave(d_bb)
    d_cc_re, d_cc_im = (t.T for t in _deinterleave(d_cc.T))
    small = dict(g_mix=d_g_mix, b_gate=jnp.concatenate([d_bga, d_bgs], axis=1), g_ab_re=g_ab_re, g_ab_im=g_ab_im,
                 d_bb_re=d_bb_re, d_bb_im=d_bb_im, d_cc_re=d_cc_re, d_cc_im=d_cc_im, d_skip=d_d_skip,
                 b_glu=d_b_glu, g_ffn=d_g_ffn, b_conv=d_b_conv, g_final=d_g_final, loss_cols=loss_cols)
    return grad_x, d_mod, small


def _block_diag_in(bb):
    t = bb.reshape(SSM_GROUPS, SSM_STATE, SSM_GROUP_CH)
    eye = jnp.eye(SSM_GROUPS, dtype=bb.dtype)
    return jnp.einsum("gnc,gh->gchn", t, eye).reshape(SSM_WIDTH, SSM_COLS)


def _block_diag_out(cm):
    eye = jnp.eye(SSM_GROUPS, dtype=cm.dtype)
    return jnp.einsum("gcn,gh->gnhc", cm, eye).reshape(SSM_COLS, SSM_WIDTH)


def _diag_blocks_in(m):
    t = m.reshape(SSM_GROUPS, SSM_GROUP_CH, SSM_GROUPS, SSM_STATE)
    idx = jnp.arange(SSM_GROUPS)
    return t[idx, :, idx, :].transpose(0, 2, 1).reshape(SSM_COLS, SSM_GROUP_CH)


def _diag_blocks_out(m):
    t = m.reshape(SSM_GROUPS, SSM_STATE, SSM_GROUPS, SSM_GROUP_CH)
    idx = jnp.arange(SSM_GROUPS)
    return t[idx, :, idx, :].transpose(0, 2, 1)


def _pair_qkv_columns(w):
    lead = w.shape[:-1]
    return w.reshape(lead + (3, N_HEADS // 2, LANES)).swapaxes(-3, -2).reshape(lead + (3 * ATT_WIDTH,))


def _unpair_qkv_columns(w):
    lead = w.shape[:-1]
    return w.reshape(lead + (N_HEADS // 2, 3, LANES)).swapaxes(-3, -2).reshape(lead + (3 * ATT_WIDTH,))


def _interleave(re, im):
    lead = re.shape[:-1]
    g = lambda a: a.reshape(lead + (SSM_COLS // SCAN_COLS, 1, SCAN_COLS))
    return jnp.concatenate([g(re), g(im)], axis=-2).reshape(lead + (2 * SSM_COLS,))


def _deinterleave(x):
    lead = x.shape[:-1]
    t = x.reshape(lead + (SSM_COLS // SCAN_COLS, 2, SCAN_COLS))
    return t[..., 0, :].reshape(lead + (SSM_COLS,)), t[..., 1, :].reshape(lead + (SSM_COLS,))


def _cols_to_slots(g):
    R = g.shape[0]
    return g.reshape(R, N_DEV, g.shape[1] // N_DEV).transpose(1, 0, 2)


def _slots_to_cols(g):
    return g.transpose(1, 0, 2).reshape(g.shape[1], N_DEV * g.shape[2])


SMALL_ORDER = ("b_ada", "g_mix", "b_gate", "a_re", "a_im", "log_dt", "b_re", "b_im", "c_re", "c_im", "d_skip",
               "b_glu", "g_ffn", "b_conv", "g_final")


def _pack(arrs):
    pieces, offs, row = [], [], 0
    for a in arrs:
        f = a.reshape(-1).astype(F32)
        n = f.shape[0]
        rows = -(-n // LANES)
        pieces.append(jnp.pad(f, (0, rows * LANES - n)))
        offs.append((row, n))
        row += rows
    return jnp.concatenate(pieces).reshape(row, LANES), offs


def _unpack(packed, offs, shapes):
    flat = packed.reshape(-1)
    return [flat[r * LANES:r * LANES + n].reshape(s) for (r, n), s in zip(offs, shapes)]


def kernel(x, c, w_ada, b_ada, g_mix, w_in, b_gate, a_re, a_im, log_dt, b_re, b_im, c_re, c_im, d_skip, w_glu, b_glu, w_proj_att, w_proj_ssm, w_out, g_ffn, w_up, w_conv, b_conv, w_down, g_final, loss_target, m_w_ada, m_b_ada, m_g_mix, m_w_in, m_b_gate, m_a_re, m_a_im, m_log_dt, m_b_re, m_b_im, m_c_re, m_c_im, m_d_skip, m_w_glu, m_b_glu, m_w_proj_att, m_w_proj_ssm, m_w_out, m_g_ffn, m_w_up, m_w_conv, m_b_conv, m_w_down, m_g_final, v_w_ada, v_b_ada, v_g_mix, v_w_in, v_b_gate, v_a_re, v_a_im, v_log_dt, v_b_re, v_b_im, v_c_re, v_c_im, v_d_skip, v_w_glu, v_b_glu, v_w_proj_att, v_w_proj_ssm, v_w_out, v_g_ffn, v_w_up, v_w_conv, v_b_conv, v_w_down, v_g_final):
    args = dict(locals())
    B, S, D = x.shape
    me = 4 * lax.axis_index("x") + 2 * lax.axis_index("y") + lax.axis_index("c")
    bf = lambda w: w[0].astype(BF16)

    c_slots, w_in_slots = _gather_all([c, bf(w_in)], name="gather_first_weights")
    c_all = c_slots.reshape(N_DEV * B, D)
    w_in_full = _slots_to_cols(w_in_slots)
    n_qkv = 3 * ATT_WIDTH
    W = dict(w_qkv=_pair_qkv_columns(w_in_full[:, :n_qkv]), w_us=w_in_full[:, n_qkv:n_qkv + SSM_WIDTH],
             w_gates=w_in_full[:, n_qkv + SSM_WIDTH:])

    n_ada = w_ada.shape[2]
    b_ada_cols = lax.dynamic_slice(b_ada, (0, me * n_ada), (1, n_ada))
    mod_part = _ada_fwd(c_all, w_ada[0], b_ada_cols)
    (mod_slots,) = _exchange([(mod_part.reshape(N_DEV, B, n_ada), True)], name="scatter_modulation")
    mod = mod_slots.transpose(1, 0, 2).reshape(B, 1, 6 * D)

    later = [bf(w_glu), bf(w_proj_att), bf(w_proj_ssm), bf(w_out), bf(w_up), w_conv[0], bf(w_down)]
    later_sems = _exchange_start(later, "start_later_weights", gather=True, carry=[mod])
    (mod,) = later_sems[5]

    def late_weights(after):
        _, lands = _exchange_wait(*later_sems[:4], after, name="wait_later_weights")
        g = [lax.dynamic_update_index_in_dim(land, a, me, 0) for land, a in zip(lands, later)]
        more_w = dict(w_glu=g[0].reshape(SSM_WIDTH, SSM_WIDTH), w_proj_att=_slots_to_cols(g[1]),
                      w_proj_ssm=_slots_to_cols(g[2]), w_out=g[3].reshape(D, D), w_up=g[4],
                      w_down=g[6].reshape(D_FF, D))
        return more_w, dict(w_conv=_slots_to_cols(g[5]))

    ab_re, ab_im, f_re, f_im = _s5_params(a_re[0], a_im[0], log_dt[0].reshape(SSM_GROUPS, 1))
    col = lambda a: a.reshape(SSM_COLS, 1)
    b_re2, b_im2 = b_re[0].reshape(SSM_COLS, SSM_GROUP_CH), b_im[0].reshape(SSM_COLS, SSM_GROUP_CH)
    bb_re, bb_im = _s5_input_matrix(col(f_re), col(f_im), b_re2, b_im2)
    slopes = jnp.asarray([2.0 ** (-8.0 * (h + 1) / N_HEADS) for h in range(N_HEADS)], F32)
    P = dict(g_mix=g_mix, g_ffn=g_ffn, g_final=g_final.reshape(1, D), b_gate=b_gate, d_skip=d_skip, b_glu=b_glu,
             b_conv=b_conv, slopes=slopes,
             a_row=_interleave(ab_re.reshape(1, SSM_COLS), ab_im.reshape(1, SSM_COLS)),
             bb_big=_interleave(_block_diag_in(bb_re), _block_diag_in(bb_im)),
             cc_big=_interleave(_block_diag_out(c_re[0]).T, -_block_diag_out(c_im[0]).T).T)

    in_flight = []

    def send_early(grads, carry=()):
        names = list(grads)
        handles = _exchange_start([grads[n] for n in names], "start_gradients_%d" % len(in_flight), gather=False,
                                  carry=carry)
        in_flight.append((names,) + handles[:4])
        return handles[4], handles[5]

    grad_x, d_mod, small = _local_step(x, mod, loss_target, W, late_weights, P, send_early)

    small_list = [small["loss_cols"], small["g_mix"], small["b_gate"], small["g_ab_re"], small["g_ab_im"],
                  _diag_blocks_in(small["d_bb_re"]), _diag_blocks_in(small["d_bb_im"]),
                  _diag_blocks_out(small["d_cc_re"]), -_diag_blocks_out(small["d_cc_im"]),
                  small["g_ffn"], small["b_conv"], small["g_final"], small["d_skip"], small["b_glu"]]
    small_packed, small_offs = _pack(small_list)
    small_all, dmod_slots = _gather_all([small_packed, d_mod.reshape(B, 6 * D)], name="gather_small_gradients")

    out = {}

    def update(name, parts, own=None):
        w2 = args[name][0]
        g, dl, mn, vn = _adamw(w2, args["m_" + name][0], args["v_" + name][0], parts, name="adamw_" + name, own=own)
        for key, val in (("grad_", g), ("delta_", dl), ("new_m_", mn), ("new_v_", vn)):
            out[key + name] = val[None]

    my_slot = me.astype(jnp.int32).reshape(1)
    for i, (names, send_sems, recv_sems, sent, lands) in enumerate(in_flight):
        sent, lands = _exchange_wait(send_sems, recv_sems, sent, lands, dmod_slots, name="wait_gradients_%d" % i)
        for name, own_slots, landed in zip(names, sent, lands):
            update(name, landed, own=(own_slots, my_slot))

    dmod_all = dmod_slots.reshape(N_DEV * B, 6 * D)
    dmod_cols = lax.dynamic_slice(dmod_all, (0, me * n_ada), (N_DEV * B, n_ada))
    d_w_ada, d_b_ada = _ada_bwd(c_all, dmod_all, dmod_cols)
    update("w_ada", d_w_ada[None])

    loss_row, loss_n = small_offs[0]
    small_sum, loss_vec = _sum_parts(small_all, (loss_row, loss_row + loss_n // LANES))
    shapes = [(1, D), (1, D), (1, 2 * D), (SSM_GROUPS, SSM_STATE), (SSM_GROUPS, SSM_STATE), (SSM_COLS, SSM_GROUP_CH),
              (SSM_COLS, SSM_GROUP_CH), (1, SSM_GROUPS, SSM_GROUP_CH, SSM_STATE),
              (1, SSM_GROUPS, SSM_GROUP_CH, SSM_STATE), (1, D), (1, D_FF), (D,), (1, SSM_WIDTH), (1, SSM_WIDTH)]
    (_, s_g_mix, s_b_gate, s_ab_re, s_ab_im, s_bb_re, s_bb_im, s_c_re, s_c_im, s_g_ffn, s_b_conv, s_g_final,
     s_d_skip, s_b_glu) = _unpack(small_sum, small_offs, shapes)
    d_b_re2, d_b_im2, d_f_re, d_f_im = _s5_input_matrix_bwd(col(f_re), col(f_im), b_re2, b_im2, s_bb_re, s_bb_im)
    d_a_re, d_a_im, d_log_dt = _s5_params_bwd(a_re[0], a_im[0], log_dt[0].reshape(SSM_GROUPS, 1), s_ab_re, s_ab_im,
                                              d_f_re.reshape(SSM_GROUPS, SSM_STATE),
                                              d_f_im.reshape(SSM_GROUPS, SSM_STATE))
    grads_small = dict(b_ada=d_b_ada, g_mix=s_g_mix, b_gate=s_b_gate, a_re=d_a_re[None], a_im=d_a_im[None],
                       log_dt=d_log_dt.reshape(1, SSM_GROUPS), b_re=d_b_re2.reshape(b_re.shape),
                       b_im=d_b_im2.reshape(b_im.shape), c_re=s_c_re, c_im=s_c_im, d_skip=s_d_skip, b_glu=s_b_glu,
                       g_ffn=s_g_ffn, b_conv=s_b_conv, g_final=s_g_final)
    w_pack, offs = _pack([args[n] for n in SMALL_ORDER])
    m_pack, _ = _pack([args["m_" + n] for n in SMALL_ORDER])
    v_pack, _ = _pack([args["v_" + n] for n in SMALL_ORDER])
    g_pack, _ = _pack([grads_small[n] for n in SMALL_ORDER])
    res = _adamw(w_pack, m_pack, v_pack, g_pack[None], name="adamw_small")
    shapes_small = [args[n].shape for n in SMALL_ORDER]
    for key, packed in zip(("grad_", "delta_", "new_m_", "new_v_"), res):
        for n, val in zip(SMALL_ORDER, _unpack(packed, offs, shapes_small)):
            out[key + n] = val

    order = ["w_ada", "b_ada", "g_mix", "w_in", "b_gate", "a_re", "a_im", "log_dt", "b_re", "b_im", "c_re", "c_im",
             "d_skip", "w_glu", "b_glu", "w_proj_att", "w_proj_ssm", "w_out", "g_ffn", "w_up", "w_conv", "b_conv",
             "w_down", "g_final"]
    loss = loss_vec[0, 0]
    return (loss, grad_x, *[out[k + n] for k in ("grad_", "delta_", "new_m_", "new_v_") for n in order])
```

```python
import math

import jax
import jax.numpy as jnp
from jax import lax
from jax.experimental import pallas as pl
from jax.experimental.pallas import tpu as pltpu

F32 = jnp.float32
BF16 = jnp.bfloat16

N_DEV = 8
D_MODEL = 1024
N_HEADS = 8
HEAD_DIM = 64
ATT_WIDTH = N_HEADS * HEAD_DIM
DILATIONS = (1, 4, 16)
WIN = 128
SSM_GROUPS = 16
SSM_GROUP_CH = 16
SSM_WIDTH = SSM_GROUPS * SSM_GROUP_CH
SSM_STATE = 64
SSM_COLS = SSM_GROUPS * SSM_STATE
D_FF = 2048
EPS = 1e-6
NEG_INF = -1e30
ADAM_LR, ADAM_B1, ADAM_B2, ADAM_EPS, ADAM_WD, ADAM_STEP = 0.001, 0.9, 0.999, 1e-08, 0.01, 10

V7X_VMEM_LIMIT = 56 * 1024 * 1024
LANES = 128


def _params(n_grid):
    return pltpu.CompilerParams(dimension_semantics=("arbitrary",) * n_grid,
                                vmem_limit_bytes=V7X_VMEM_LIMIT)


def _tile(n, pref):
    if n <= pref:
        return n
    t = (pref // LANES) * LANES
    while t > 0:
        if n % t == 0:
            return t
        t -= LANES
    return n


def _matmul(a, b, *, ta=False, tb=False, out_dtype=F32, name):
    if ta:
        K, M = a.shape
    else:
        M, K = a.shape
    if tb:
        N, K2 = b.shape
    else:
        K2, N = b.shape
    assert K == K2, (a.shape, b.shape)
    if ta:
        tm, tn, tk = _tile(M, 1024), _tile(N, 2048), _tile(K, 512)
    else:
        tm, tk = _tile(M, 512), _tile(K, 4096)
        tn = _tile(N, 2048 if K <= 2048 else 1024)
    nk = K // tk
    dn = (((0,) if ta else (1,), (1,) if tb else (0,)), ((), ()))

    def body(a_ref, b_ref, o_ref, acc_ref):
        k = pl.program_id(2)
        part = lax.dot_general(a_ref[...].astype(BF16), b_ref[...].astype(BF16), dn, preferred_element_type=F32)
        if nk == 1:
            o_ref[...] = part.astype(o_ref.dtype)
            return

        @pl.when(k == 0)
        def _():
            acc_ref[...] = jnp.zeros_like(acc_ref)

        acc_ref[...] += part

        @pl.when(k == nk - 1)
        def _():
            o_ref[...] = acc_ref[...].astype(o_ref.dtype)

    a_spec = (pl.BlockSpec((tk, tm), lambda j, i, k: (k, i)) if ta
              else pl.BlockSpec((tm, tk), lambda j, i, k: (i, k)))
    b_spec = (pl.BlockSpec((tn, tk), lambda j, i, k: (j, k)) if tb
              else pl.BlockSpec((tk, tn), lambda j, i, k: (k, j)))
    return pl.pallas_call(
        body, name=name, grid=(N // tn, M // tm, nk),
        in_specs=[a_spec, b_spec],
        out_specs=pl.BlockSpec((tm, tn), lambda j, i, k: (i, j)),
        out_shape=jax.ShapeDtypeStruct((M, N), out_dtype),
        scratch_shapes=[pltpu.VMEM((tm, tn) if nk > 1 else (8, LANES), F32)],
        compiler_params=_params(3),
    )(a, b)


HALF = 256
UP_SLOTS = N_DEV // 2
UP_GROUP = 4 * HALF


def _group_weight(w_ref):
    return jnp.concatenate([w_ref[0, :, :HALF], w_ref[1, :, :HALF], w_ref[0, :, HALF:], w_ref[1, :, HALF:]], axis=1)


def _up_weight_spec(K, index):
    return pl.BlockSpec((2, None, K, 2 * HALF), index)


def _up_fwd(a, w3, name):
    M, K = a.shape
    tm = _tile(M, 1024)

    def body(a_ref, w_ref, o_ref):
        o_ref[...] = jnp.dot(a_ref[...].astype(BF16), _group_weight(w_ref),
                             preferred_element_type=F32).astype(o_ref.dtype)

    return pl.pallas_call(
        body, name=name, grid=(UP_SLOTS, M // tm),
        in_specs=[pl.BlockSpec((tm, K), lambda j, i: (i, 0)), _up_weight_spec(K, lambda j, i: (0, j, 0, 0))],
        out_specs=pl.BlockSpec((tm, UP_GROUP), lambda j, i: (i, j)),
        out_shape=jax.ShapeDtypeStruct((M, UP_SLOTS * UP_GROUP), BF16), compiler_params=_params(2),
    )(a, w3.reshape(2, UP_SLOTS, K, 2 * HALF))


def _up_dx(d, w3, name):
    M = d.shape[0]
    K = w3.shape[1]
    tm = _tile(M, 1024)

    def body(d_ref, w_ref, o_ref, acc_ref):
        j = pl.program_id(1)

        @pl.when(j == 0)
        def _():
            acc_ref[...] = jnp.zeros_like(acc_ref)

        acc_ref[...] += lax.dot_general(d_ref[...], _group_weight(w_ref), _NT, preferred_element_type=F32)

        @pl.when(j == UP_SLOTS - 1)
        def _():
            o_ref[...] = acc_ref[...].astype(o_ref.dtype)

    return pl.pallas_call(
        body, name=name, grid=(M // tm, UP_SLOTS),
        in_specs=[pl.BlockSpec((tm, UP_GROUP), lambda i, j: (i, j)), _up_weight_spec(K, lambda i, j: (0, j, 0, 0))],
        out_specs=pl.BlockSpec((tm, K), lambda i, j: (i, 0)),
        out_shape=jax.ShapeDtypeStruct((M, K), BF16), scratch_shapes=[pltpu.VMEM((tm, K), F32)],
        compiler_params=_params(2),
    )(d, w3.reshape(2, UP_SLOTS, K, 2 * HALF))


def _up_dw(a, d, name):
    M, K = a.shape
    tk = _tile(M, 512)
    nk = M // tk

    def body(a_ref, d_ref, o_ref, acc_ref):
        k = pl.program_id(1)

        @pl.when(k == 0)
        def _():
            acc_ref[...] = jnp.zeros_like(acc_ref)

        acc_ref[...] += lax.dot_general(a_ref[...], d_ref[...], _TN, preferred_element_type=F32)

        @pl.when(k == nk - 1)
        def _():
            for half in range(2):
                for part in range(2):
                    lo = (2 * half + part) * HALF
                    o_ref[part, :, half * HALF:(half + 1) * HALF] = acc_ref[:, lo:lo + HALF].astype(o_ref.dtype)

    out = pl.pallas_call(
        body, name=name, grid=(UP_SLOTS, nk),
        in_specs=[pl.BlockSpec((tk, K), lambda j, k: (k, 0)), pl.BlockSpec((tk, UP_GROUP), lambda j, k: (k, j))],
        out_specs=_up_weight_spec(K, lambda j, k: (0, j, 0, 0)),
        out_shape=jax.ShapeDtypeStruct((2, UP_SLOTS, K, 2 * HALF), BF16),
        scratch_shapes=[pltpu.VMEM((K, UP_GROUP), F32)], compiler_params=_params(2),
    )(a, d)
    return out.reshape(N_DEV, K, 2 * HALF)


def _rowwise(fn, rows, bvecs, consts, out_rows, out_b, out_g, *, ts, name):
    B, S = rows[0][0].shape[:2]
    nin = len(rows) + len(bvecs) + len(consts)
    nr, nb, ng = len(out_rows), len(out_b), len(out_g)

    def body(*refs):
        b = pl.program_id(0)
        s = pl.program_id(1)
        vals = [r[...] for r in refs[:nin]]
        vals[:len(rows)] = [v.astype(F32) for v in vals[:len(rows)]]
        outs = fn(*vals)
        if not isinstance(outs, (tuple, list)):
            outs = (outs,)
        orefs = refs[nin:]
        for i in range(nr):
            orefs[i][...] = outs[i].astype(orefs[i].dtype)
        for i in range(nb):
            ref = orefs[nr + i]

            @pl.when(s == 0)
            def _(ref=ref):
                ref[...] = jnp.zeros_like(ref)

            ref[...] += outs[nr + i]
        for i in range(ng):
            ref = orefs[nr + nb + i]

            @pl.when((s == 0) & (b == 0))
            def _(ref=ref):
                ref[...] = jnp.zeros_like(ref)

            ref[...] += outs[nr + nb + i]

    rows = [r if len(r) == 4 else r + (0,) for r in rows]
    in_specs = ([pl.BlockSpec((None, ts, cb), lambda b, s, ci=ci, b0=b0: (b0 + b, s, ci)) for (_, cb, ci, b0) in rows]
                + [pl.BlockSpec((None, 1, cb), lambda b, s, ci=ci: (b, 0, ci)) for (_, cb, ci) in bvecs]
                + [pl.BlockSpec(a.shape, lambda b, s: (0, 0)) for a in consts])
    out_shape = ([jax.ShapeDtypeStruct((B, S, c), dt) for (c, dt) in out_rows]
                 + [jax.ShapeDtypeStruct((B, 1, c), F32) for c in out_b]
                 + [jax.ShapeDtypeStruct(rc, F32) for rc in out_g])
    out_specs = ([pl.BlockSpec((None, ts, c), lambda b, s: (b, s, 0)) for (c, _) in out_rows]
                 + [pl.BlockSpec((None, 1, c), lambda b, s: (b, 0, 0)) for c in out_b]
                 + [pl.BlockSpec(rc, lambda b, s: (0, 0)) for rc in out_g])
    args = [r[0] for r in rows] + [a for (a, _, _) in bvecs] + list(consts)
    return pl.pallas_call(
        body, name=name, grid=(B, S // ts), in_specs=in_specs, out_specs=out_specs,
        out_shape=out_shape, compiler_params=_params(2),
    )(*args)


def _col_sum(v):
    return jnp.sum(v, axis=0, keepdims=True)


def _rms_scale(h):
    return lax.rsqrt(jnp.mean(h * h, axis=-1, keepdims=True) + EPS)


def _sigmoid(v):
    return 1.0 / (1.0 + jnp.exp(-v))


ATT_SCALE = HEAD_DIM ** -0.5
COPY_ROWS = 256
_NT = (((1,), (1,)), ((), ()))
_TN = (((0,), (0,)), ((), ()))


def _row_chunks(d, seq):
    sub = seq // d
    out = []
    for r in range(d):
        for c0 in range(0, sub, COPY_ROWS):
            n = min(COPY_ROWS, sub - c0)
            out.append((pl.ds(r + c0 * d, n, stride=d), r * sub + c0, n))
    return out


ATT_UNROLL = 8
KEYS = 2 * WIN


def _zero_once(refs):
    @pl.when((pl.program_id(0) == 0) & (pl.program_id(1) == 0))
    def _():
        for r in refs:
            r[...] = jnp.zeros_like(r)


def _pair_bias(bias_ref, slopes_ref, hp, d, key_major):
    shape = (KEYS, WIN) if key_major else (WIN, KEYS)
    qi = lax.broadcasted_iota(jnp.int32, shape, 1 if key_major else 0)
    kj = lax.broadcasted_iota(jnp.int32, shape, 0 if key_major else 1)
    dist = WIN + qi - kj
    valid = (dist >= 0) & (dist <= WIN)
    distf = dist.astype(F32)
    for h in range(2):
        slope_d = slopes_ref[2 * hp + h] * float(d)
        with_prev = jnp.where(valid, -(slope_d * distf), NEG_INF)
        no_prev = jnp.where(kj >= WIN, with_prev, NEG_INF)
        span = slice(h * KEYS, (h + 1) * KEYS)
        if key_major:
            bias_ref[1, span, :] = with_prev
            bias_ref[0, span, :] = no_prev
        else:
            bias_ref[1, :, span] = with_prev
            bias_ref[0, :, span] = no_prev


def _stack_heads(v):
    first = lax.broadcasted_iota(jnp.int32, v.shape, 1) < HEAD_DIM
    zero = jnp.zeros_like(v)
    return jnp.concatenate([jnp.where(first, v, zero), jnp.where(first, zero, v)], axis=0)


def _per_head(c0, c1, n):
    return jnp.where(lax.broadcasted_iota(jnp.int32, (n, LANES), 1) < HEAD_DIM, c0, c1)


def _qkv_spec(seq, j):
    return pl.BlockSpec((None, seq, LANES), lambda b, hp: (b, 0, 3 * hp + j))


def _attention_fwd(qkv, slopes):
    B, S, _ = qkv.shape
    n_blk = S // WIN
    n_pair = N_HEADS // 2

    def body(slopes_ref, q_ref, k_ref, v_ref, o_ref, lse_ref, qp, kp, vp, bias, acc, mx, sm, acc_n, mx_n, sm_n):
        hp = pl.program_id(1)
        _zero_once((kp, vp))
        for p, d in enumerate(DILATIONS):
            nb = n_blk // d
            chunks = _row_chunks(d, S)
            for src, dst, n in chunks:
                qp[dst:dst + n, :] = (q_ref[src, :] * ATT_SCALE).astype(BF16)
                kp[WIN + dst:WIN + dst + n, :] = k_ref[src, :].astype(BF16)
                vp[WIN + dst:WIN + dst + n, :] = v_ref[src, :].astype(BF16)
            _pair_bias(bias, slopes_ref, hp, d, key_major=False)
            acc_t, mx_t, sm_t = (acc_n, mx_n, sm_n) if d == 1 else (acc, mx, sm)

            nk = WIN if nb == 1 else KEYS
            bias_cur = jnp.concatenate([bias[0, :, WIN:KEYS], bias[0, :, KEYS + WIN:]], axis=1) if nb == 1 else None

            def block(i, carry, p=p, nb=nb, nk=nk, bias_cur=bias_cur, acc_t=acc_t, mx_t=mx_t, sm_t=sm_t):
                cur = pl.ds(pl.multiple_of(i * WIN, WIN), WIN)
                keys = pl.ds(pl.multiple_of(i * WIN + (KEYS - nk), WIN), nk)
                s = lax.dot_general(qp[cur, :], _stack_heads(kp[keys, :]), _NT, preferred_element_type=F32)
                s = s + (bias_cur if nb == 1 else bias[((i % nb) > 0).astype(jnp.int32)])
                es, ms, ls = [], [], []
                for h in range(2):
                    sh = s[:, h * nk:(h + 1) * nk]
                    m = jnp.max(sh if nb == 1 else jnp.maximum(sh[:, :WIN], sh[:, WIN:]), axis=1, keepdims=True)
                    e = jnp.exp(sh - m)
                    es.append(e.astype(BF16))
                    ms.append(m)
                    ls.append(jnp.sum(e if nb == 1 else e[:, :WIN] + e[:, WIN:], axis=1, keepdims=True))
                acc_t[p, cur, :] = jnp.dot(jnp.concatenate(es, axis=1), _stack_heads(vp[keys, :]),
                                           preferred_element_type=F32)
                mx_t[p, cur, :] = _per_head(ms[0], ms[1], WIN)
                sm_t[p, cur, :] = _per_head(ls[0], ls[1], WIN)
                return carry

            lax.fori_loop(0, n_blk, block, 0, unroll=ATT_UNROLL)
            if d > 1:
                for src, dst, n in chunks:
                    acc_n[p, src, :] = acc[p, dst:dst + n, :]
                    mx_n[p, src, :] = mx[p, dst:dst + n, :]
                    sm_n[p, src, :] = sm[p, dst:dst + n, :]

        chunk = 256

        def merge(i, carry):
            rows = pl.ds(pl.multiple_of(i * chunk, chunk), chunk)
            ms = [mx_n[p, rows, :] for p in range(3)]
            m = jnp.maximum(jnp.maximum(ms[0], ms[1]), ms[2])
            ws = [jnp.exp(mp - m) for mp in ms]
            l = ws[0] * sm_n[0, rows, :] + ws[1] * sm_n[1, rows, :] + ws[2] * sm_n[2, rows, :]
            o = (ws[0] * acc_n[0, rows, :] + ws[1] * acc_n[1, rows, :] + ws[2] * acc_n[2, rows, :]) / l
            o_ref[rows, :] = o.astype(o_ref.dtype)
            lse = m + jnp.log(l)
            for h in range(2):
                lse_ref[rows, h:h + 1] = lse[:, h * HEAD_DIM:h * HEAD_DIM + 1]
            return carry

        lax.fori_loop(0, S // chunk, merge, 0)

    return pl.pallas_call(
        body, name="attention_fwd", grid=(B, n_pair),
        in_specs=[pl.BlockSpec(memory_space=pltpu.SMEM), _qkv_spec(S, 0), _qkv_spec(S, 1), _qkv_spec(S, 2)],
        out_specs=[pl.BlockSpec((None, S, LANES), lambda b, hp: (b, 0, hp)),
                   pl.BlockSpec((None, None, S, 2), lambda b, hp: (b, hp, 0, 0))],
        out_shape=[jax.ShapeDtypeStruct((B, S, ATT_WIDTH), BF16),
                   jax.ShapeDtypeStruct((B, n_pair, S, 2), F32)],
        scratch_shapes=[pltpu.VMEM((S, LANES), BF16), pltpu.VMEM((S + WIN, LANES), BF16),
                        pltpu.VMEM((S + WIN, LANES), BF16), pltpu.VMEM((2, WIN, 2 * KEYS), F32)]
        + [pltpu.VMEM((3, S, LANES), F32)] * 6,
        compiler_params=_params(2),
    )(slopes, qkv, qkv, qkv)


def _attention_bwd(qkv, o, do, lse, slopes):
    B, S, _ = qkv.shape
    n_blk = S // WIN
    n_pair = N_HEADS // 2

    def body(slopes_ref, q_ref, k_ref, v_ref, o_ref, do_ref, lse_ref, dx_ref,
             qp, dop, kp, vp, aux, auxp, aux_t, bias_t, dqp, dvk, dq_n, dk_n, dv_n):
        hp = pl.program_id(1)
        aux[...] = jnp.zeros_like(aux)
        for c0 in range(0, S, COPY_ROWS):
            rows = slice(c0, c0 + COPY_ROWS)
            prod = do_ref[rows, :] * o_ref[rows, :].astype(F32)
            for h in range(2):
                aux[rows, 2 * h:2 * h + 1] = lse_ref[rows, h:h + 1]
                aux[rows, 2 * h + 1:2 * h + 2] = jnp.sum(prod[:, h * HEAD_DIM:(h + 1) * HEAD_DIM], axis=1,
                                                         keepdims=True)
        dq_n[...] = jnp.zeros_like(dq_n)
        dk_n[...] = jnp.zeros_like(dk_n)
        dv_n[...] = jnp.zeros_like(dv_n)
        _zero_once((kp, vp))
        for p, d in enumerate(DILATIONS):
            nb = n_blk // d
            chunks = _row_chunks(d, S)
            for src, dst, n in chunks:
                auxp[dst:dst + n, :] = aux[src, :]
                qp[dst:dst + n, :] = (q_ref[src, :] * ATT_SCALE).astype(BF16)
                dop[dst:dst + n, :] = do_ref[src, :].astype(BF16)
                kp[WIN + dst:WIN + dst + n, :] = k_ref[src, :].astype(BF16)
                vp[WIN + dst:WIN + dst + n, :] = v_ref[src, :].astype(BF16)
            for i in range(n_blk):
                aux_t[i] = auxp[i * WIN:(i + 1) * WIN, :].T[0:8, :]
            _pair_bias(bias_t, slopes_ref, hp, d, key_major=True)
            dvk[...] = jnp.zeros_like(dvk)

            nk = WIN if nb == 1 else KEYS
            bias_cur = jnp.concatenate([bias_t[0, WIN:KEYS, :], bias_t[0, KEYS + WIN:, :]], axis=0) if nb == 1 else None

            def block(i, carry, nb=nb, nk=nk, bias_cur=bias_cur):
                cur = pl.ds(pl.multiple_of(i * WIN, WIN), WIN)
                keys = pl.ds(pl.multiple_of(i * WIN + (KEYS - nk), WIN), nk)
                q2, do2 = qp[cur, :], dop[cur, :]
                kc = _stack_heads(kp[keys, :])
                s_t = lax.dot_general(kc, q2, _NT, preferred_element_type=F32)
                s_t = s_t + (bias_cur if nb == 1 else bias_t[((i % nb) > 0).astype(jnp.int32)])
                dp_t = lax.dot_general(_stack_heads(vp[keys, :]), do2, _NT, preferred_element_type=F32)
                ps, dss = [], []
                for h in range(2):
                    span = slice(h * nk, (h + 1) * nk)
                    p_t = jnp.exp(s_t[span, :] - aux_t[i, 2 * h:2 * h + 1, :])
                    ds_t = p_t * (dp_t[span, :] - aux_t[i, 2 * h + 1:2 * h + 2, :])
                    ps.append(p_t.astype(BF16))
                    dss.append(ds_t.astype(BF16))
                do_rows, q_rows = _stack_heads(do2), _stack_heads(q2)
                zr = jnp.zeros_like(do_rows)
                rhs = jnp.concatenate([jnp.concatenate([do_rows, zr], axis=1),
                                       jnp.concatenate([zr, q_rows], axis=1)], axis=0)
                dvk[keys, :] += jnp.dot(jnp.concatenate(ps + dss, axis=1), rhs, preferred_element_type=F32)
                dqp[cur, :] = lax.dot_general(jnp.concatenate(dss, axis=0), kc, _TN, preferred_element_type=F32)
                return carry

            lax.fori_loop(0, n_blk, block, 0, unroll=ATT_UNROLL)
            for src, dst, n in chunks:
                dq_n[src, :] += dqp[dst:dst + n, :]
                dv_n[src, :] += dvk[WIN + dst:WIN + dst + n, :LANES]
                dk_n[src, :] += dvk[WIN + dst:WIN + dst + n, LANES:]
        for c0 in range(0, S, COPY_ROWS):
            rows = slice(c0, c0 + COPY_ROWS)
            dx_ref[rows, 0:LANES] = (dq_n[rows, :] * ATT_SCALE).astype(dx_ref.dtype)
            dx_ref[rows, LANES:2 * LANES] = dk_n[rows, :].astype(dx_ref.dtype)
            dx_ref[rows, 2 * LANES:3 * LANES] = dv_n[rows, :].astype(dx_ref.dtype)

    pair = lambda width: pl.BlockSpec((None, S, width), lambda b, hp: (b, 0, hp))
    vm = lambda shape, dt: pltpu.VMEM(shape, dt)
    return pl.pallas_call(
        body, name="attention_bwd", grid=(B, n_pair),
        in_specs=[pl.BlockSpec(memory_space=pltpu.SMEM), _qkv_spec(S, 0), _qkv_spec(S, 1), _qkv_spec(S, 2),
                  pair(LANES), pair(LANES), pl.BlockSpec((None, None, S, 2), lambda b, hp: (b, hp, 0, 0))],
        out_specs=pair(3 * LANES),
        out_shape=jax.ShapeDtypeStruct((B, S, 3 * ATT_WIDTH), BF16),
        scratch_shapes=[vm((S, LANES), BF16), vm((S, LANES), BF16),
                        vm((S + WIN, LANES), BF16), vm((S + WIN, LANES), BF16),
                        vm((S, LANES), F32), vm((S, LANES), F32), vm((n_blk, 8, WIN), F32),
                        vm((2, 2 * KEYS, WIN), F32),
                        vm((S, LANES), F32), vm((S + WIN, 2 * LANES), F32),
                        vm((S, LANES), F32), vm((S, LANES), F32), vm((S, LANES), F32)],
        compiler_params=_params(2),
    )(slopes, qkv, qkv, qkv, o, do, lse)


SCAN_COLS = 256
SCAN_ROWS = 8


def _rows_to_tile(rows):
    rid = lax.broadcasted_iota(jnp.int32, (SCAN_ROWS, rows[0].shape[1]), 0)
    tile = jnp.broadcast_to(rows[0], rid.shape)
    for k in range(1, SCAN_ROWS):
        tile = jnp.where(rid == k, rows[k], tile)
    return tile


SCAN_UNROLL = 4


def _complex_powers(ar, ai, n):
    out = [(ar, ai)]
    for _ in range(n - 1):
        pr, pi = out[-1]
        out.append((pr * ar - pi * ai, pr * ai + pi * ar))
    return out


def _round_multipliers(powers, rid, reverse):
    out = []
    for s in (1, 2, 4):
        keep = (rid < SCAN_ROWS - s) if reverse else (rid >= s)
        out.append((jnp.where(keep, powers[s - 1][0], 0.0), jnp.where(keep, powers[s - 1][1], 0.0)))
    return out


def _tile_scan(xr, xi, multipliers, reverse):
    for s, (mr, mi) in zip((1, 2, 4), multipliers):
        shift = SCAN_ROWS - s if reverse else s
        sr, si = pltpu.roll(xr, shift, 0), pltpu.roll(xi, shift, 0)
        xr, xi = xr + (mr * sr - mi * si), xi + (mr * si + mi * sr)
    return xr, xi


SCAN_CHUNK = 256


def _scan_fwd(us, bb_big, a_row, cc_big):
    B, S, _ = us.shape
    groups = 2
    width = 2 * groups * SCAN_COLS
    nc = 2 * SSM_COLS // width
    nt = S // SCAN_ROWS
    tiles = SCAN_CHUNK // SCAN_ROWS
    LAST = slice(SCAN_ROWS - 1, SCAN_ROWS)

    def body(us_ref, bb_ref, a_ref, cc_ref, xs_ref, y_ref, bu_ref):
        bb = bb_ref[...].astype(BF16)
        for c in range(S // SCAN_CHUNK):
            part = jnp.dot(us_ref[c * SCAN_CHUNK:(c + 1) * SCAN_CHUNK, :].astype(BF16), bb,
                           preferred_element_type=F32)
            bu_ref[c * tiles:(c + 1) * tiles] = part.reshape(tiles, SCAN_ROWS, width)
        rid = lax.broadcasted_iota(jnp.int32, (SCAN_ROWS, SCAN_COLS), 0)
        consts = []
        for g in range(groups):
            re = slice(2 * g * SCAN_COLS, (2 * g + 1) * SCAN_COLS)
            im = slice((2 * g + 1) * SCAN_COLS, (2 * g + 2) * SCAN_COLS)
            powers = _complex_powers(a_ref[:, re], a_ref[:, im], SCAN_ROWS)
            carry_mult = (_rows_to_tile([p[0] for p in powers]), _rows_to_tile([p[1] for p in powers]))
            consts.append((re, im, carry_mult, _round_multipliers(powers, rid, reverse=False)))

        def tile(i, carry):
            out = []
            for (re, im, (cr_t, ci_t), rounds), (cr, ci) in zip(consts, carry):
                xr, xi = _tile_scan(bu_ref[i, :, re], bu_ref[i, :, im], rounds, reverse=False)
                xs_ref[i, :, re] = xr + (cr_t * cr - ci_t * ci)
                xs_ref[i, :, im] = xi + (cr_t * ci + ci_t * cr)
                out.append((xs_ref[i, LAST, re], xs_ref[i, LAST, im]))
            return tuple(out)

        zero = jnp.zeros((1, SCAN_COLS), F32)
        lax.fori_loop(0, nt, tile, ((zero, zero),) * groups, unroll=SCAN_UNROLL)

        @pl.when(pl.program_id(1) == 0)
        def _():
            y_ref[...] = jnp.zeros_like(y_ref)

        cc = cc_ref[...].astype(BF16)
        for c in range(S // SCAN_CHUNK):
            x2 = xs_ref[c * tiles:(c + 1) * tiles].reshape(SCAN_CHUNK, width).astype(BF16)
            y_ref[c * SCAN_CHUNK:(c + 1) * SCAN_CHUNK, :] += jnp.dot(x2, cc, preferred_element_type=F32)

    col = pl.BlockSpec((None, nt, SCAN_ROWS, width), lambda b, j: (b, 0, 0, j))
    tok = pl.BlockSpec((None, S, SSM_WIDTH), lambda b, j: (b, 0, 0))
    xs, y = pl.pallas_call(
        body, name="s5_scan_fwd", grid=(B, nc),
        in_specs=[tok, pl.BlockSpec((SSM_WIDTH, width), lambda b, j: (0, j)),
                  pl.BlockSpec((1, width), lambda b, j: (0, j)), pl.BlockSpec((width, SSM_WIDTH), lambda b, j: (j, 0))],
        out_specs=[col, tok],
        out_shape=[jax.ShapeDtypeStruct((B, nt, SCAN_ROWS, 2 * SSM_COLS), F32),
                   jax.ShapeDtypeStruct((B, S, SSM_WIDTH), F32)],
        scratch_shapes=[pltpu.VMEM((nt, SCAN_ROWS, width), F32)],
        compiler_params=_params(2),
    )(us, bb_big, a_row, cc_big)
    return xs.reshape(B, S, 2 * SSM_COLS), y


def _scan_bwd(dy, us, bb_big, cc_big, xs, a_row):
    B, S, _ = dy.shape
    width = 2 * SCAN_COLS
    nc = SSM_COLS // SCAN_COLS
    nt = S // SCAN_ROWS
    tiles = SCAN_CHUNK // SCAN_ROWS
    RE, IM = slice(0, SCAN_COLS), slice(SCAN_COLS, 2 * SCAN_COLS)
    FIRST, LAST = slice(0, 1), slice(SCAN_ROWS - 1, SCAN_ROWS)

    def body(dy_ref, us_ref, bb_ref, cc_ref, x_ref, a_ref, dus_ref, ga_ref, dbb_ref, dcc_ref, d_ref, lam_ref):
        b = pl.program_id(1)
        cc = cc_ref[...].astype(BF16)
        for c in range(S // SCAN_CHUNK):
            part = lax.dot_general(dy_ref[c * SCAN_CHUNK:(c + 1) * SCAN_CHUNK, :].astype(BF16), cc, _NT,
                                   preferred_element_type=F32)
            d_ref[c * tiles:(c + 1) * tiles] = part.reshape(tiles, SCAN_ROWS, width)
        powers = _complex_powers(a_ref[:, RE], -a_ref[:, IM], SCAN_ROWS)
        rid = lax.broadcasted_iota(jnp.int32, (SCAN_ROWS, SCAN_COLS), 0)
        cr_t = _rows_to_tile([powers[SCAN_ROWS - 1 - r][0] for r in range(SCAN_ROWS)])
        ci_t = _rows_to_tile([powers[SCAN_ROWS - 1 - r][1] for r in range(SCAN_ROWS)])
        rounds = _round_multipliers(powers, rid, reverse=True)

        @pl.when(b == 0)
        def _():
            ga_ref[...] = jnp.zeros_like(ga_ref)
            dbb_ref[...] = jnp.zeros_like(dbb_ref)
            dcc_ref[...] = jnp.zeros_like(dcc_ref)

        def tile(j, carry):
            cr, ci, accr, acci = carry
            i = nt - 1 - j
            lr, li = _tile_scan(d_ref[i, :, RE], d_ref[i, :, IM], rounds, reverse=True)
            lam_r = lr + (cr_t * cr - ci_t * ci)
            lam_i = li + (cr_t * ci + ci_t * cr)
            lam_ref[i, :, RE] = lam_r
            lam_ref[i, :, IM] = lam_i
            ip = jnp.maximum(i - 1, 0)
            keep = (i > 0).astype(F32)
            xpr = jnp.where(rid == 0, x_ref[ip, LAST, RE] * keep, pltpu.roll(x_ref[i, :, RE], 1, 0))
            xpi = jnp.where(rid == 0, x_ref[ip, LAST, IM] * keep, pltpu.roll(x_ref[i, :, IM], 1, 0))
            accr = accr + lam_r * xpr + lam_i * xpi
            acci = acci + lam_i * xpr - lam_r * xpi
            return lam_ref[i, FIRST, RE], lam_ref[i, FIRST, IM], accr, acci

        z1 = jnp.zeros((1, SCAN_COLS), F32)
        z8 = jnp.zeros((SCAN_ROWS, SCAN_COLS), F32)
        _, _, accr, acci = lax.fori_loop(0, nt, tile, (z1, z1, z8, z8), unroll=SCAN_UNROLL)
        ga_ref[:, RE] += _col_sum(accr)
        ga_ref[:, IM] += _col_sum(acci)

        bb = bb_ref[...].astype(BF16)
        for c in range(S // SCAN_CHUNK):
            rows = slice(c * SCAN_CHUNK, (c + 1) * SCAN_CHUNK)
            lam2 = lam_ref[c * tiles:(c + 1) * tiles].reshape(SCAN_CHUNK, width).astype(BF16)
            x2 = x_ref[c * tiles:(c + 1) * tiles].reshape(SCAN_CHUNK, width).astype(BF16)
            dus_ref[rows, :] = lax.dot_general(lam2, bb, _NT, preferred_element_type=F32)
            dbb_ref[...] += lax.dot_general(us_ref[rows, :].astype(BF16), lam2, _TN, preferred_element_type=F32)
            dcc_ref[...] += lax.dot_general(x2, dy_ref[rows, :].astype(BF16), _TN, preferred_element_type=F32)

    col = pl.BlockSpec((None, nt, SCAN_ROWS, width), lambda j, b: (b, 0, 0, j))
    tok = pl.BlockSpec((None, S, SSM_WIDTH), lambda j, b: (b, 0, 0))
    scratch = pltpu.VMEM((nt, SCAN_ROWS, width), F32)
    return pl.pallas_call(
        body, name="s5_scan_bwd", grid=(nc, B),
        in_specs=[tok, tok, pl.BlockSpec((SSM_WIDTH, width), lambda j, b: (0, j)),
                  pl.BlockSpec((width, SSM_WIDTH), lambda j, b: (j, 0)), col,
                  pl.BlockSpec((1, width), lambda j, b: (0, j))],
        out_specs=[pl.BlockSpec((None, None, S, SSM_WIDTH), lambda j, b: (j, b, 0, 0)),
                   pl.BlockSpec((1, width), lambda j, b: (0, j)),
                   pl.BlockSpec((SSM_WIDTH, width), lambda j, b: (0, j)),
                   pl.BlockSpec((width, SSM_WIDTH), lambda j, b: (j, 0))],
        out_shape=[jax.ShapeDtypeStruct((nc, B, S, SSM_WIDTH), F32), jax.ShapeDtypeStruct((1, 2 * SSM_COLS), F32),
                   jax.ShapeDtypeStruct((SSM_WIDTH, 2 * SSM_COLS), F32),
                   jax.ShapeDtypeStruct((2 * SSM_COLS, SSM_WIDTH), F32)],
        scratch_shapes=[scratch, scratch],
        compiler_params=_params(2),
    )(dy, us, bb_big, cc_big, xs.reshape(B, nt, SCAN_ROWS, 2 * SSM_COLS), a_row)


def _s5_discretise(lr, li, log_dt):
    dt = jnp.exp(log_dt)
    mag = jnp.exp(lr * dt)
    ang = li * dt
    ab_re, ab_im = mag * jnp.cos(ang), mag * jnp.sin(ang)
    nr, ni = ab_re - 1.0, ab_im
    den = lr * lr + li * li
    f_re = (nr * lr + ni * li) / den
    f_im = (ni * lr - nr * li) / den
    return dt, ab_re, ab_im, nr, ni, den, f_re, f_im


def _s5_params(a_re, a_im, log_dt):
    def body(lr_ref, li_ref, ld_ref, abr, abi, fr, fi):
        _, ab_re, ab_im, _, _, _, f_re, f_im = _s5_discretise(lr_ref[...], li_ref[...], ld_ref[...])
        abr[...] = ab_re
        abi[...] = ab_im
        fr[...] = f_re
        fi[...] = f_im

    return pl.pallas_call(body, name="s5_params",
                          out_shape=[jax.ShapeDtypeStruct(a_re.shape, F32)] * 4)(a_re, a_im, log_dt)


def _s5_input_matrix(f_re, f_im, b_re, b_im):
    def body(fr, fi, br, bi, o_re, o_im):
        o_re[...] = fr[...] * br[...] - fi[...] * bi[...]
        o_im[...] = fr[...] * bi[...] + fi[...] * br[...]

    return pl.pallas_call(body, name="s5_input_matrix",
                          out_shape=[jax.ShapeDtypeStruct(b_re.shape, F32)] * 2)(f_re, f_im, b_re, b_im)


def _s5_input_matrix_bwd(f_re, f_im, b_re, b_im, g_re, g_im):
    def body(fr, fi, br, bi, gr, gi, dbr, dbi, dfr, dfi):
        dbr[...] = fr[...] * gr[...] + fi[...] * gi[...]
        dbi[...] = fr[...] * gi[...] - fi[...] * gr[...]
        dfr[...] = jnp.sum(br[...] * gr[...] + bi[...] * gi[...], axis=1, keepdims=True)
        dfi[...] = jnp.sum(br[...] * gi[...] - bi[...] * gr[...], axis=1, keepdims=True)

    return pl.pallas_call(
        body, name="s5_input_matrix_bwd",
        out_shape=[jax.ShapeDtypeStruct(b_re.shape, F32)] * 2 + [jax.ShapeDtypeStruct(f_re.shape, F32)] * 2,
    )(f_re, f_im, b_re, b_im, g_re, g_im)


def _s5_params_bwd(a_re, a_im, log_dt, g_ab_re, g_ab_im, d_f_re, d_f_im):
    def body(lr_ref, li_ref, ld_ref, gar, gai, dfr, dfi, o_lr, o_li, o_ld):
        lr, li = lr_ref[...], li_ref[...]
        dt, ab_re, ab_im, nr, ni, den, f_re, f_im = _s5_discretise(lr, li, ld_ref[...])
        d_fr, d_fi = dfr[...], dfi[...]
        d_nr = (d_fr * lr - d_fi * li) / den
        d_ni = (d_fr * li + d_fi * lr) / den
        common = (d_fr * f_re + d_fi * f_im) * 2.0 / den
        d_lr = (d_fr * nr + d_fi * ni) / den - common * lr
        d_li = (d_fr * ni - d_fi * nr) / den - common * li
        d_abr = gar[...] + d_nr
        d_abi = gai[...] + d_ni
        d_mag_mag = d_abr * ab_re + d_abi * ab_im
        d_ang = d_abi * ab_re - d_abr * ab_im
        o_lr[...] = d_lr + d_mag_mag * dt
        o_li[...] = d_li + d_ang * dt
        o_ld[...] = jnp.sum(d_mag_mag * lr + d_ang * li, axis=1, keepdims=True) * dt

    return pl.pallas_call(
        body, name="s5_params_bwd",
        out_shape=[jax.ShapeDtypeStruct(a_re.shape, F32)] * 2 + [jax.ShapeDtypeStruct(log_dt.shape, F32)],
    )(a_re, a_im, log_dt, g_ab_re, g_ab_im, d_f_re, d_f_im)


CONV_COLS = 256


def _shift_down(v, j, row):
    return jnp.where(row >= j, pltpu.roll(v, j, 0), 0.0)


def _shift_up(v, j, row, seq):
    return jnp.where(row < seq - j, pltpu.roll(v, seq - j, 0), 0.0)


def _conv_fwd(up, w_conv, b_conv):
    B, S, _ = up.shape
    nj = D_FF // CONV_COLS

    def body(up_ref, w_ref, b_ref, ff_ref):
        a = up_ref[:, :CONV_COLS].astype(F32)
        val = up_ref[:, CONV_COLS:].astype(F32)
        row = lax.broadcasted_iota(jnp.int32, a.shape, 0)
        w0, w1, w2 = w_ref[0:1, :], w_ref[1:2, :], w_ref[2:3, :]
        conv = b_ref[...] + w0 * a + w1 * _shift_down(a, 1, row) + w2 * _shift_down(a, 2, row)
        ff_ref[...] = (conv * _sigmoid(conv) * val).astype(ff_ref.dtype)

    return pl.pallas_call(
        body, name="conv_gate_fwd", grid=(B, nj),
        in_specs=[pl.BlockSpec((None, S, 2 * CONV_COLS), lambda b, j: (b, 0, j)),
                  pl.BlockSpec((3, CONV_COLS), lambda b, j: (0, j)),
                  pl.BlockSpec((1, CONV_COLS), lambda b, j: (0, j))],
        out_specs=pl.BlockSpec((None, S, CONV_COLS), lambda b, j: (b, 0, j)),
        out_shape=jax.ShapeDtypeStruct((B, S, D_FF), BF16),
        compiler_params=_params(2),
    )(up, w_conv, b_conv)


def _conv_bwd(up, d_ff, w_conv, b_conv):
    B, S, _ = up.shape
    nj = D_FF // CONV_COLS

    def body(up_ref, dff_ref, w_ref, b_ref, dup_ref, dw_ref, db_ref):
        b = pl.program_id(1)
        a = up_ref[:, :CONV_COLS].astype(F32)
        val = up_ref[:, CONV_COLS:].astype(F32)
        row = lax.broadcasted_iota(jnp.int32, a.shape, 0)
        w0, w1, w2 = w_ref[0:1, :], w_ref[1:2, :], w_ref[2:3, :]
        a1, a2 = _shift_down(a, 1, row), _shift_down(a, 2, row)
        conv = b_ref[...] + w0 * a + w1 * a1 + w2 * a2
        sg = _sigmoid(conv)
        dff = dff_ref[...].astype(F32)
        d_val = dff * conv * sg
        dc = dff * val * (sg * (1.0 + conv * (1.0 - sg)))
        d_a = w0 * dc + w1 * _shift_up(dc, 1, row, S) + w2 * _shift_up(dc, 2, row, S)
        dup_ref[:, :CONV_COLS] = d_a.astype(dup_ref.dtype)
        dup_ref[:, CONV_COLS:] = d_val.astype(dup_ref.dtype)

        @pl.when(b == 0)
        def _():
            dw_ref[...] = jnp.zeros_like(dw_ref)
            db_ref[...] = jnp.zeros_like(db_ref)

        dw_ref[0:1, :] += _col_sum(dc * a)
        dw_ref[1:2, :] += _col_sum(dc * a1)
        dw_ref[2:3, :] += _col_sum(dc * a2)
        db_ref[...] += _col_sum(dc)

    return pl.pallas_call(
        body, name="conv_gate_bwd", grid=(nj, B),
        in_specs=[pl.BlockSpec((None, S, 2 * CONV_COLS), lambda j, b: (b, 0, j)),
                  pl.BlockSpec((None, S, CONV_COLS), lambda j, b: (b, 0, j)),
                  pl.BlockSpec((3, CONV_COLS), lambda j, b: (0, j)),
                  pl.BlockSpec((1, CONV_COLS), lambda j, b: (0, j))],
        out_specs=[pl.BlockSpec((None, S, 2 * CONV_COLS), lambda j, b: (b, 0, j)),
                   pl.BlockSpec((3, CONV_COLS), lambda j, b: (0, j)),
                   pl.BlockSpec((1, CONV_COLS), lambda j, b: (0, j))],
        out_shape=[jax.ShapeDtypeStruct((B, S, 2 * D_FF), BF16), jax.ShapeDtypeStruct((3, D_FF), F32),
                   jax.ShapeDtypeStruct((1, D_FF), F32)],
        compiler_params=_params(2),
    )(up, d_ff, w_conv, b_conv)


def _ada_fwd(c_all, w_ada, b_ada):
    def body(c_ref, w_ref, b_ref, o_ref):
        cv = c_ref[...]
        act = (cv * _sigmoid(cv)).astype(BF16)
        o_ref[...] = jnp.dot(act, w_ref[...].astype(BF16), preferred_element_type=F32) + b_ref[...]

    return pl.pallas_call(body, name="ada_fwd",
                          out_shape=jax.ShapeDtypeStruct((c_all.shape[0], w_ada.shape[1]), F32),
                          compiler_params=pltpu.CompilerParams(vmem_limit_bytes=V7X_VMEM_LIMIT))(c_all, w_ada, b_ada)


def _ada_bwd(c_all, dmod_all, dmod_cols):
    def body(c_ref, dm_ref, dmc_ref, dw_ref, db_ref):
        cv = c_ref[...]
        act = (cv * _sigmoid(cv)).astype(BF16)
        dw_ref[...] = lax.dot_general(act, dmc_ref[...].astype(BF16), _TN, preferred_element_type=F32)
        db_ref[...] = _col_sum(dm_ref[...])

    return pl.pallas_call(
        body, name="ada_bwd",
        out_shape=[jax.ShapeDtypeStruct((c_all.shape[1], dmod_cols.shape[1]), F32),
                   jax.ShapeDtypeStruct((1, dmod_all.shape[1]), F32)],
        compiler_params=pltpu.CompilerParams(vmem_limit_bytes=V7X_VMEM_LIMIT))(c_all, dmod_all, dmod_cols)


def _adamw(w, m, v, g_parts, name, own=None):
    R, C = w.shape
    P = g_parts.shape[0]
    tr = R
    for cand in (256, 128, 64, 32, 16, 8):
        if R % cand == 0 and cand * C * 4 * (P + 8) * 2 <= V7X_VMEM_LIMIT // 2:
            tr = cand
            break
    c1 = 1.0 / (1.0 - ADAM_B1 ** ADAM_STEP)
    c2 = 1.0 / (1.0 - ADAM_B2 ** ADAM_STEP)

    def update(w_ref, m_ref, v_ref, g, og, od, om, ov):
        m_new = ADAM_B1 * m_ref[...] + (1.0 - ADAM_B1) * g
        v_new = ADAM_B2 * v_ref[...] + (1.0 - ADAM_B2) * (g * g)
        og[...] = g
        om[...] = m_new
        ov[...] = v_new
        od[...] = -ADAM_LR * ((m_new * c1) / (jnp.sqrt(v_new * c2) + ADAM_EPS) + ADAM_WD * w_ref[...])

    def total(g_ref):
        g = g_ref[0].astype(F32)
        for p in range(1, P):
            g = g + g_ref[p].astype(F32)
        return g

    out_shape = [jax.ShapeDtypeStruct((R, C), F32)] * 4
    if own is None:
        def body(w_ref, m_ref, v_ref, g_ref, og, od, om, ov):
            update(w_ref, m_ref, v_ref, total(g_ref), og, od, om, ov)

        spec = pl.BlockSpec((tr, C), lambda i: (i, 0))
        return pl.pallas_call(
            body, name=name, grid=(R // tr,),
            in_specs=[spec, spec, spec, pl.BlockSpec((P, tr, C), lambda i: (0, i, 0))],
            out_specs=[spec] * 4, out_shape=out_shape, compiler_params=_params(1),
        )(w, m, v, g_parts)

    slots, me = own

    def body_own(me_ref, w_ref, m_ref, v_ref, g_ref, own_ref, og, od, om, ov):
        g = own_ref[...].astype(F32)
        for p in range(P):
            g = g + jnp.where(me_ref[0] == p, 0.0, g_ref[p].astype(F32))
        update(w_ref, m_ref, v_ref, g, og, od, om, ov)

    spec = pl.BlockSpec((tr, C), lambda i, me_ref: (i, 0))
    grid_spec = pltpu.PrefetchScalarGridSpec(
        num_scalar_prefetch=1, grid=(R // tr,),
        in_specs=[spec, spec, spec, pl.BlockSpec((P, tr, C), lambda i, me_ref: (0, i, 0)),
                  pl.BlockSpec((None, tr, C), lambda i, me_ref: (me_ref[0], i, 0))],
        out_specs=[spec] * 4)
    return pl.pallas_call(body_own, name=name, grid_spec=grid_spec, out_shape=out_shape,
                          compiler_params=_params(1))(me, w, m, v, g_parts, slots)


def _sum_parts(parts, loss_rows):
    P, R, C = parts.shape
    lo, hi = loss_rows

    def body(p_ref, o_ref, loss_ref):
        t = p_ref[0]
        for p in range(1, P):
            t = t + p_ref[p]
        o_ref[...] = t
        tot = jnp.sum(jnp.sum(o_ref[lo:hi, :], axis=1, keepdims=True), axis=0, keepdims=True)
        loss_ref[...] = jnp.broadcast_to(tot, loss_ref.shape)

    return pl.pallas_call(body, name="sum_small_grads",
                          out_shape=[jax.ShapeDtypeStruct((R, C), F32), jax.ShapeDtypeStruct((1, LANES), F32)],
                          compiler_params=pltpu.CompilerParams(vmem_limit_bytes=V7X_VMEM_LIMIT))(parts)


def _exchange(items, name):
    n = len(items)
    MESH = pl.DeviceIdType.MESH

    def body(*refs):
        src, dst = refs[:n], refs[n:2 * n]
        send_sems, recv_sems, local_sems = refs[2 * n:]
        x, y, c = lax.axis_index("x"), lax.axis_index("y"), lax.axis_index("c")
        me = 4 * x + 2 * y + c
        started = []
        for it, (_, per_peer) in enumerate(items):
            own = pltpu.make_async_copy(src[it].at[me] if per_peer else src[it], dst[it].at[me], local_sems.at[it])
            own.start()
            started.append(own)
        sends, recvs = [], []
        for k in range(1, N_DEV):
            px = 1 - x if k & 4 else x
            py = 1 - y if k & 2 else y
            pc = 1 - c if k & 1 else c
            peer = 4 * px + 2 * py + pc
            for it, (_, per_peer) in enumerate(items):
                s = src[it].at[peer] if per_peer else src[it]
                cp = pltpu.make_async_remote_copy(src_ref=s, dst_ref=dst[it].at[me], send_sem=send_sems.at[it, k - 1],
                                                  recv_sem=recv_sems.at[it, k - 1], device_id=(px, py, pc),
                                                  device_id_type=MESH)
                cp.start()
                sends.append(cp)
                recvs.append(pltpu.make_async_remote_copy(
                    src_ref=s, dst_ref=dst[it].at[peer], send_sem=send_sems.at[it, k - 1],
                    recv_sem=recv_sems.at[it, k - 1], device_id=(px, py, pc), device_id_type=MESH))
        for cp in recvs:
            cp.wait_recv()
        for cp in sends:
            cp.wait_send()
        for cp in started:
            cp.wait()

    any_spec = pl.BlockSpec(memory_space=pl.ANY)
    out_shape = []
    for a, per_peer in items:
        shp = a.shape if per_peer else (N_DEV,) + a.shape
        out_shape.append(jax.ShapeDtypeStruct(shp, a.dtype))
    return pl.pallas_call(
        body, name=name, in_specs=[any_spec] * n, out_specs=[any_spec] * n, out_shape=out_shape,
        scratch_shapes=[pltpu.SemaphoreType.DMA((n, N_DEV - 1)), pltpu.SemaphoreType.DMA((n, N_DEV - 1)),
                        pltpu.SemaphoreType.DMA((n,))],
    )(*[a for a, _ in items])


def _remote(src, dst, send_sem, recv_sem, device):
    return pltpu.make_async_remote_copy(src_ref=src, dst_ref=dst, send_sem=send_sem, recv_sem=recv_sem,
                                        device_id=device, device_id_type=pl.DeviceIdType.MESH)


def _mesh_place():
    x, y, c = lax.axis_index("x"), lax.axis_index("y"), lax.axis_index("c")
    other_chips = [(1 - x, y), (x, 1 - y), (1 - x, 1 - y)]
    return x, y, c, (x, y, 1 - c), other_chips


def _gather_all(items, name):
    n = len(items)

    def body(*refs):
        src, dst = refs[:n], refs[n:2 * n]
        send_sems, recv_sems, local_sems = refs[2 * n:]
        x, y, c, sibling, chips = _mesh_place()
        slot = lambda px, py, pc: 4 * px + 2 * py + pc
        me = slot(x, y, c)
        own = [pltpu.make_async_copy(src[it], dst[it].at[me], local_sems.at[it]) for it in range(n)]
        first = []
        for it in range(n):
            first.append(_remote(src[it], dst[it].at[me], send_sems.at[it, 0], recv_sems.at[it, 0], sibling))
            for j, chip in enumerate(chips):
                first.append(_remote(src[it], dst[it].at[me], send_sems.at[it, 1 + j], recv_sems.at[it, 1 + j],
                                     (*chip, c)))
        for cp in own + first:
            cp.start()
        passed = []
        for j, chip in enumerate(chips):
            blk = slot(*chip, c)
            for it in range(n):
                _remote(src[it], dst[it].at[blk], send_sems.at[it, 1 + j], recv_sems.at[it, 1 + j],
                        (*chip, c)).wait_recv()
                fwd = _remote(dst[it].at[blk], dst[it].at[blk], send_sems.at[it, 4 + j], recv_sems.at[it, 4 + j],
                              sibling)
                fwd.start()
                passed.append(fwd)
        for it in range(n):
            _remote(src[it], dst[it].at[slot(x, y, 1 - c)], send_sems.at[it, 0], recv_sems.at[it, 0],
                    sibling).wait_recv()
        for j, chip in enumerate(chips):
            for it in range(n):
                _remote(src[it], dst[it].at[slot(*chip, 1 - c)], send_sems.at[it, 4 + j], recv_sems.at[it, 4 + j],
                        sibling).wait_recv()
        for cp in first + passed:
            cp.wait_send()
        for cp in own:
            cp.wait()

    any_spec = pl.BlockSpec(memory_space=pl.ANY)
    return pl.pallas_call(
        body, name=name, in_specs=[any_spec] * n, out_specs=[any_spec] * n,
        out_shape=[jax.ShapeDtypeStruct((N_DEV,) + a.shape, a.dtype) for a in items],
        scratch_shapes=[pltpu.SemaphoreType.DMA((n, 7)), pltpu.SemaphoreType.DMA((n, 7)),
                        pltpu.SemaphoreType.DMA((n,))],
    )(*items)


def _peers():
    x, y, c = lax.axis_index("x"), lax.axis_index("y"), lax.axis_index("c")
    out = []
    for k in range(1, N_DEV):
        px = 1 - x if k & 4 else x
        py = 1 - y if k & 2 else y
        pc = 1 - c if k & 1 else c
        out.append((k, (px, py, pc), 4 * px + 2 * py + pc))
    return 4 * x + 2 * y + c, out


def _exchange_start(items, name, gather, carry=()):
    n, m = len(items), len(carry)

    def body(*refs):
        src, land = refs[:n], refs[n:2 * n]
        first_out = 2 * n + m
        send_sems, recv_sems = refs[first_out:first_out + n], refs[first_out + n:first_out + 2 * n]
        token = refs[-1]
        me, peers = _peers()
        for k, peer, slot in peers:
            for it in range(n):
                _remote(src[it] if gather else src[it].at[slot], land[it].at[me], send_sems[it], recv_sems[it],
                        peer).start()
        token[...] = jnp.zeros_like(token)

    hbm = pl.BlockSpec(memory_space=pltpu.HBM)
    sem = pl.BlockSpec(memory_space=pltpu.SEMAPHORE)
    land_shapes = [(N_DEV,) + (a.shape if gather else a.shape[1:]) for a in items]
    lands = [lax.empty(shp, a.dtype) for shp, a in zip(land_shapes, items)]
    through = list(items) + lands + list(carry)
    outs = pl.pallas_call(
        body, name=name,
        out_shape=(*[pltpu.SemaphoreType.DMA(())] * (2 * n), *[pltpu.HBM(a.shape, a.dtype) for a in through],
                   jax.ShapeDtypeStruct((8, LANES), F32)),
        in_specs=[hbm] * len(through),
        out_specs=(*[sem] * (2 * n), *[hbm] * len(through), pl.BlockSpec(memory_space=pltpu.VMEM)),
        input_output_aliases={i: 2 * n + i for i in range(len(through))},
        compiler_params=pltpu.CompilerParams(has_side_effects=pltpu.SideEffectType.DATAFLOW_SIDE_EFFECTING),
    )(*[pltpu.with_memory_space_constraint(a, pltpu.HBM) for a in through])
    return (list(outs[:n]), list(outs[n:2 * n]), list(outs[2 * n:3 * n]), list(outs[3 * n:4 * n]), outs[-1],
            list(outs[4 * n:4 * n + m]))


def _exchange_wait(send_sems, recv_sems, items, lands, after, name):
    n = len(items)

    def body(*refs):
        land = refs[n:2 * n]
        send_sems, recv_sems = refs[2 * n:3 * n], refs[3 * n:4 * n]
        me, peers = _peers()
        for it in range(n):
            seven = land[it].at[pl.ds(0, N_DEV - 1)]
            cp = _remote(seven, seven, send_sems[it], recv_sems[it], peers[0][1])
            cp.wait_send()
            cp.wait_recv()

    hbm = pl.BlockSpec(memory_space=pltpu.HBM)
    sem = pl.BlockSpec(memory_space=pltpu.SEMAPHORE)
    outs = pl.pallas_call(
        body, name=name,
        out_shape=tuple(pltpu.HBM(a.shape, a.dtype) for a in list(items) + list(lands)),
        in_specs=[hbm] * (2 * n) + [sem] * (2 * n) + [pl.BlockSpec(memory_space=pl.ANY)],
        out_specs=tuple([hbm] * (2 * n)),
        input_output_aliases={i: i for i in range(2 * n)},
        compiler_params=pltpu.CompilerParams(has_side_effects=pltpu.SideEffectType.DATAFLOW_SIDE_EFFECTING),
    )(*items, *lands, *send_sems, *recv_sems, after)
    return list(outs[:n]), list(outs[n:])


def _gelu_tanh(y):
    k = math.sqrt(2.0 / math.pi)
    t = jnp.tanh(k * (y + 0.044715 * y * y * y))
    return 0.5 * y * (1.0 + t), t


def _local_step(x, mod, target, W, late_weights, P, send_early):
    B, S, D = x.shape
    T = B * S
    TS = 512
    flat = lambda a: a.reshape(T, a.shape[-1])
    unflat = lambda a: a.reshape(B, S, a.shape[-1])
    mod_col = lambda i: (mod, D, i)

    def f_modnorm(xv, sc, sh, g):
        return (xv * _rms_scale(xv) * g) * (1.0 + sc) + sh

    (u1,) = _rowwise(f_modnorm, [(x, D, 0)], [mod_col(1), mod_col(0)], [P["g_mix"]],
                     [(D, BF16)], [], [], ts=TS, name="modnorm_mix")
    u1f = flat(u1)
    qkv = unflat(_matmul(u1f, W["w_qkv"], name="proj_qkv"))
    us = unflat(_matmul(u1f, W["w_us"], name="proj_ssm_in"))
    gates = unflat(_matmul(u1f, W["w_gates"], out_dtype=BF16, name="proj_gates"))

    o_att, lse = _attention_fwd(qkv, P["slopes"])
    more_w, more_p = late_weights(o_att)
    W, P = {**W, **more_w}, {**P, **more_p}
    y_att = unflat(_matmul(flat(o_att), W["w_proj_att"], out_dtype=BF16, name="proj_att"))

    xs, y_mm = _scan_fwd(us, P["bb_big"], P["a_row"], P["cc_big"])

    def f_glu(ymm, usv, dsk, wg, bg):
        yv = ymm + dsk * usv
        ge, _ = _gelu_tanh(yv)
        pre = jnp.dot(ge.astype(BF16), wg, preferred_element_type=F32) + bg
        return yv, ge * _sigmoid(pre)

    y_s5, z = _rowwise(f_glu, [(y_mm, SSM_WIDTH, 0), (us, SSM_WIDTH, 0)], [], [P["d_skip"], W["w_glu"], P["b_glu"]],
                       [(SSM_WIDTH, F32), (SSM_WIDTH, BF16)], [], [], ts=TS, name="s5_glu")
    y_ssm = unflat(_matmul(flat(z), W["w_proj_ssm"], out_dtype=BF16, name="proj_ssm"))

    def f_merge(ga, gs, ya, ys, bga, bgs):
        return _sigmoid(ga + bga) * ya + _sigmoid(gs + bgs) * ys

    bga, bgs = P["b_gate"][:, :D], P["b_gate"][:, D:]
    (merged,) = _rowwise(f_merge, [(gates, D, 0), (gates, D, 1), (y_att, D, 0), (y_ssm, D, 0)], [], [bga, bgs],
                         [(D, BF16)], [], [], ts=TS, name="gate_merge")
    mix = unflat(_matmul(flat(merged), W["w_out"], out_dtype=BF16, name="proj_out"))

    def f_res_modnorm(xv, mx, gt, sc, sh, g):
        h = xv + gt * mx
        return h, (h * _rms_scale(h) * g) * (1.0 + sc) + sh

    h1, u2 = _rowwise(f_res_modnorm, [(x, D, 0), (mix, D, 0)], [mod_col(2), mod_col(4), mod_col(3)], [P["g_ffn"]],
                      [(D, F32), (D, BF16)], [], [], ts=TS, name="residual_modnorm_ffn")
    up = unflat(_up_fwd(flat(u2), W["w_up"], name="ffn_up"))
    ff = _conv_fwd(up, P["w_conv"], P["b_conv"])
    down = unflat(_matmul(flat(ff), W["w_down"], out_dtype=BF16, name="ffn_down"))

    def f_head(h1v, dn, tg, gt, g):
        h2 = h1v + gt * dn
        r = _rms_scale(h2)
        nh = h2 * r
        e = nh * g - tg
        dy = e * (1.0 / D)
        gy = dy * g
        dh = r * (gy - nh * jnp.mean(gy * nh, axis=-1, keepdims=True))
        return (dh, dh * gt, _col_sum(dh * dn), _col_sum(dy * nh), _col_sum(e * e) * (0.5 / D))

    dh2, d_down, d_gt2, d_g_final, loss_cols = _rowwise(
        f_head, [(h1, D, 0), (down, D, 0), (target, D, 0)], [mod_col(5)], [P["g_final"]],
        [(D, BF16), (D, BF16)], [D], [(1, D), (1, D)], ts=TS, name="head_loss")

    d_downf = flat(d_down)
    d_ff = unflat(_matmul(d_downf, W["w_down"], tb=True, out_dtype=BF16, name="ffn_down_dx"))
    d_w_down = _matmul(flat(ff), d_downf, ta=True, out_dtype=BF16, name="ffn_down_dw")
    d_up, d_w_conv, d_b_conv = _conv_bwd(up, d_ff, P["w_conv"], P["b_conv"])
    d_upf = flat(d_up)
    d_u2 = unflat(_up_dx(d_upf, W["w_up"], name="ffn_up_dx"))
    d_w_up = _up_dw(flat(u2), d_upf, name="ffn_up_dw")
    token, _ = send_early(dict(w_down=d_w_down.reshape(N_DEV, D_FF // N_DEV, D), w_up=d_w_up))
    g_ffn_after = P["g_ffn"] + token[0:1, 0:1]

    def f_modnorm_bwd(du, h, dres, mx, sc, gt, g):
        r = _rms_scale(h)
        nh = h * r
        dn = du * (1.0 + sc)
        gy = dn * g
        dh = dres + r * (gy - nh * jnp.mean(gy * nh, axis=-1, keepdims=True))
        return (dh, dh * gt, _col_sum(du), _col_sum(du * nh * g), _col_sum(dh * mx), _col_sum(dn * nh))

    dh1, d_mix, d_sh2, d_sc2, d_gt1, d_g_ffn = _rowwise(
        f_modnorm_bwd, [(d_u2, D, 0), (h1, D, 0), (dh2, D, 0), (mix, D, 0)], [mod_col(4), mod_col(2)], [g_ffn_after],
        [(D, BF16), (D, BF16)], [D, D, D], [(1, D)], ts=TS, name="modnorm_ffn_bwd")

    d_mixf = flat(d_mix)
    d_merged = unflat(_matmul(d_mixf, W["w_out"], tb=True, out_dtype=BF16, name="proj_out_dx"))
    d_w_out = _matmul(flat(merged), d_mixf, ta=True, out_dtype=BF16, name="proj_out_dw")

    def f_merge_bwd(dm, ga, gs, ya, ys, bga_, bgs_):
        sa, ss = _sigmoid(ga + bga_), _sigmoid(gs + bgs_)
        dga = dm * ya * sa * (1.0 - sa)
        dgs = dm * ys * ss * (1.0 - ss)
        return dm * sa, dm * ss, jnp.concatenate([dga, dgs], axis=1), _col_sum(dga), _col_sum(dgs)

    d_y_att, d_y_ssm, d_gates, d_bga, d_bgs = _rowwise(
        f_merge_bwd, [(d_merged, D, 0), (gates, D, 0), (gates, D, 1), (y_att, D, 0), (y_ssm, D, 0)], [], [bga, bgs],
        [(D, BF16), (D, BF16), (2 * D, BF16)], [], [(1, D), (1, D)], ts=TS, name="gate_merge_bwd")

    d_yaf, d_ysf = flat(d_y_att), flat(d_y_ssm)
    d_o_att = unflat(_matmul(d_yaf, W["w_proj_att"], tb=True, name="proj_att_dx"))
    d_w_proj_att = _matmul(flat(o_att), d_yaf, ta=True, out_dtype=BF16, name="proj_att_dw")
    d_z = unflat(_matmul(d_ysf, W["w_proj_ssm"], tb=True, out_dtype=BF16, name="proj_ssm_dx"))
    d_w_proj_ssm = _matmul(flat(z), d_ysf, ta=True, out_dtype=BF16, name="proj_ssm_dw")

    def f_glu_bwd(yv, dz, usv, dsk, wg, bg):
        ge, t = _gelu_tanh(yv)
        pre = jnp.dot(ge.astype(BF16), wg, preferred_element_type=F32) + bg
        sg = _sigmoid(pre)
        dpre = dz * ge * sg * (1.0 - sg)
        dge = dz * sg + lax.dot_general(dpre.astype(BF16), wg, _NT, preferred_element_type=F32)
        k = math.sqrt(2.0 / math.pi)
        dgelu = 0.5 * (1.0 + t) + 0.5 * yv * (1.0 - t * t) * k * (1.0 + 3.0 * 0.044715 * yv * yv)
        dy = dge * dgelu
        dwg = lax.dot_general(ge.astype(BF16), dpre.astype(BF16), _TN, preferred_element_type=F32)
        return dy, dy * dsk, dwg, _col_sum(dpre), _col_sum(dy * usv)

    d_y_s5, d_us_skip, d_w_glu, d_b_glu, d_d_skip = _rowwise(
        f_glu_bwd, [(y_s5, SSM_WIDTH, 0), (d_z, SSM_WIDTH, 0), (us, SSM_WIDTH, 0)], [],
        [P["d_skip"], W["w_glu"], P["b_glu"]],
        [(SSM_WIDTH, BF16), (SSM_WIDTH, F32)], [], [(SSM_WIDTH, SSM_WIDTH), (1, SSM_WIDTH), (1, SSM_WIDTH)],
        ts=TS, name="s5_glu_bwd")
    d_us_parts, g_ab, d_bb, d_cc = _scan_bwd(d_y_s5, us, P["bb_big"], P["cc_big"], xs, P["a_row"])

    token, _ = send_early(dict(
        w_out=d_w_out.reshape(N_DEV, D // N_DEV, D), w_proj_att=_cols_to_slots(d_w_proj_att),
        w_proj_ssm=_cols_to_slots(d_w_proj_ssm),
        w_glu=d_w_glu.astype(BF16).reshape(N_DEV, SSM_WIDTH // N_DEV, SSM_WIDTH),
        w_conv=_cols_to_slots(d_w_conv.astype(BF16))))
    d_qkv = _attention_bwd(qkv, o_att, d_o_att, lse, P["slopes"] + token[0, 0])

    def f_add(*parts):
        return sum(parts[1:], parts[0])

    n_parts = d_us_parts.shape[0]
    stacked = d_us_parts.reshape(n_parts * B, S, SSM_WIDTH)
    (d_us,) = _rowwise(f_add, [(d_us_skip, SSM_WIDTH, 0)] + [(stacked, SSM_WIDTH, 0, j * B) for j in range(n_parts)],
                       [], [],
                       [(SSM_WIDTH, BF16)], [], [], ts=TS, name="s5_input_grad")
    d_qkvf = flat(d_qkv)
    d_usf = flat(d_us)
    d_gatesf = flat(d_gates)
    d_w_in = jnp.concatenate(
        [_unpair_qkv_columns(_matmul(u1f, d_qkvf, ta=True, out_dtype=BF16, name="proj_qkv_dw")),
         _matmul(u1f, d_usf, ta=True, out_dtype=BF16, name="proj_ssm_in_dw"),
         _matmul(u1f, d_gatesf, ta=True, out_dtype=BF16, name="proj_gates_dw")], axis=1)
    token, (w_qkv, w_us, w_gates) = send_early(dict(w_in=_cols_to_slots(d_w_in)),
                                               carry=[W["w_qkv"], W["w_us"], W["w_gates"]])
    d_u1 = (_matmul(d_qkvf, w_qkv, tb=True, out_dtype=BF16, name="proj_qkv_dx"),
            _matmul(d_usf, w_us, tb=True, out_dtype=BF16, name="proj_ssm_in_dx"),
            _matmul(d_gatesf, w_gates, tb=True, out_dtype=BF16, name="proj_gates_dx"))

    def f_modnorm_bwd_in(du0, du1, du2, h, dres, sc, g):
        du = du0 + du1 + du2
        r = _rms_scale(h)
        nh = h * r
        dn = du * (1.0 + sc)
        gy = dn * g
        dh = dres + r * (gy - nh * jnp.mean(gy * nh, axis=-1, keepdims=True))
        return (dh, _col_sum(du), _col_sum(du * nh * g), _col_sum(dn * nh))

    grad_x, d_sh1, d_sc1, d_g_mix = _rowwise(
        f_modnorm_bwd_in, [(unflat(d_u1[0]), D, 0), (unflat(d_u1[1]), D, 0), (unflat(d_u1[2]), D, 0), (x, D, 0),
                           (dh1, D, 0)], [mod_col(1)], [P["g_mix"] + token[0:1, 0:1]],
        [(D, F32)], [D, D], [(1, D)], ts=TS, name="modnorm_mix_bwd")

    d_mod = jnp.concatenate([d_sh1, d_sc1, d_gt1, d_sh2, d_sc2, d_gt2], axis=-1)
    g_ab_re, g_ab_im = _deinterleave(g_ab)
    d_bb_re, d_bb_im = _deinterleave(d_bb)
    d_cc_re, d_cc_im = (t.T for t in _deinterleave(d_cc.T))
    small = dict(g_mix=d_g_mix, b_gate=jnp.concatenate([d_bga, d_bgs], axis=1), g_ab_re=g_ab_re, g_ab_im=g_ab_im,
                 d_bb_re=d_bb_re, d_bb_im=d_bb_im, d_cc_re=d_cc_re, d_cc_im=d_cc_im, d_skip=d_d_skip,
                 b_glu=d_b_glu, g_ffn=d_g_ffn, b_conv=d_b_conv, g_final=d_g_final, loss_cols=loss_cols)
    return grad_x, d_mod, small


def _block_diag_in(bb):
    t = bb.reshape(SSM_GROUPS, SSM_STATE, SSM_GROUP_CH)
    eye = jnp.eye(SSM_GROUPS, dtype=bb.dtype)
    return jnp.einsum("gnc,gh->gchn", t, eye).reshape(SSM_WIDTH, SSM_COLS)


def _block_diag_out(cm):
    eye = jnp.eye(SSM_GROUPS, dtype=cm.dtype)
    return jnp.einsum("gcn,gh->gnhc", cm, eye).reshape(SSM_COLS, SSM_WIDTH)


def _diag_blocks_in(m):
    t = m.reshape(SSM_GROUPS, SSM_GROUP_CH, SSM_GROUPS, SSM_STATE)
    idx = jnp.arange(SSM_GROUPS)
    return t[idx, :, idx, :].transpose(0, 2, 1).reshape(SSM_COLS, SSM_GROUP_CH)


def _diag_blocks_out(m):
    t = m.reshape(SSM_GROUPS, SSM_STATE, SSM_GROUPS, SSM_GROUP_CH)
    idx = jnp.arange(SSM_GROUPS)
    return t[idx, :, idx, :].transpose(0, 2, 1)


def _pair_qkv_columns(w):
    lead = w.shape[:-1]
    return w.reshape(lead + (3, N_HEADS // 2, LANES)).swapaxes(-3, -2).reshape(lead + (3 * ATT_WIDTH,))


def _unpair_qkv_columns(w):
    lead = w.shape[:-1]
    return w.reshape(lead + (N_HEADS // 2, 3, LANES)).swapaxes(-3, -2).reshape(lead + (3 * ATT_WIDTH,))


def _interleave(re, im):
    lead = re.shape[:-1]
    g = lambda a: a.reshape(lead + (SSM_COLS // SCAN_COLS, 1, SCAN_COLS))
    return jnp.concatenate([g(re), g(im)], axis=-2).reshape(lead + (2 * SSM_COLS,))


def _deinterleave(x):
    lead = x.shape[:-1]
    t = x.reshape(lead + (SSM_COLS // SCAN_COLS, 2, SCAN_COLS))
    return t[..., 0, :].reshape(lead + (SSM_COLS,)), t[..., 1, :].reshape(lead + (SSM_COLS,))


def _cols_to_slots(g):
    R = g.shape[0]
    return g.reshape(R, N_DEV, g.shape[1] // N_DEV).transpose(1, 0, 2)


def _slots_to_cols(g):
    return g.transpose(1, 0, 2).reshape(g.shape[1], N_DEV * g.shape[2])


SMALL_ORDER = ("b_ada", "g_mix", "b_gate", "a_re", "a_im", "log_dt", "b_re", "b_im", "c_re", "c_im", "d_skip",
               "b_glu", "g_ffn", "b_conv", "g_final")


def _pack(arrs):
    pieces, offs, row = [], [], 0
    for a in arrs:
        f = a.reshape(-1).astype(F32)
        n = f.shape[0]
        rows = -(-n // LANES)
        pieces.append(jnp.pad(f, (0, rows * LANES - n)))
        offs.append((row, n))
        row += rows
    return jnp.concatenate(pieces).reshape(row, LANES), offs


def _unpack(packed, offs, shapes):
    flat = packed.reshape(-1)
    return [flat[r * LANES:r * LANES + n].reshape(s) for (r, n), s in zip(offs, shapes)]


def kernel(x, c, w_ada, b_ada, g_mix, w_in, b_gate, a_re, a_im, log_dt, b_re, b_im, c_re, c_im, d_skip, w_glu, b_glu, w_proj_att, w_proj_ssm, w_out, g_ffn, w_up, w_conv, b_conv, w_down, g_final, loss_target, m_w_ada, m_b_ada, m_g_mix, m_w_in, m_b_gate, m_a_re, m_a_im, m_log_dt, m_b_re, m_b_im, m_c_re, m_c_im, m_d_skip, m_w_glu, m_b_glu, m_w_proj_att, m_w_proj_ssm, m_w_out, m_g_ffn, m_w_up, m_w_conv, m_b_conv, m_w_down, m_g_final, v_w_ada, v_b_ada, v_g_mix, v_w_in, v_b_gate, v_a_re, v_a_im, v_log_dt, v_b_re, v_b_im, v_c_re, v_c_im, v_d_skip, v_w_glu, v_b_glu, v_w_proj_att, v_w_proj_ssm, v_w_out, v_g_ffn, v_w_up, v_w_conv, v_b_conv, v_w_down, v_g_final):
    args = dict(locals())
    B, S, D = x.shape
    me = 4 * lax.axis_index("x") + 2 * lax.axis_index("y") + lax.axis_index("c")
    bf = lambda w: w[0].astype(BF16)

    c_slots, w_in_slots = _gather_all([c, bf(w_in)], name="gather_first_weights")
    c_all = c_slots.reshape(N_DEV * B, D)
    w_in_full = _slots_to_cols(w_in_slots)
    n_qkv = 3 * ATT_WIDTH
    W = dict(w_qkv=_pair_qkv_columns(w_in_full[:, :n_qkv]), w_us=w_in_full[:, n_qkv:n_qkv + SSM_WIDTH],
             w_gates=w_in_full[:, n_qkv + SSM_WIDTH:])

    n_ada = w_ada.shape[2]
    b_ada_cols = lax.dynamic_slice(b_ada, (0, me * n_ada), (1, n_ada))
    mod_part = _ada_fwd(c_all, w_ada[0], b_ada_cols)
    (mod_slots,) = _exchange([(mod_part.reshape(N_DEV, B, n_ada), True)], name="scatter_modulation")
    mod = mod_slots.transpose(1, 0, 2).reshape(B, 1, 6 * D)

    later = [bf(w_glu), bf(w_proj_att), bf(w_proj_ssm), bf(w_out), bf(w_up), w_conv[0], bf(w_down)]
    later_sems = _exchange_start(later, "start_later_weights", gather=True, carry=[mod])
    (mod,) = later_sems[5]

    def late_weights(after):
        _, lands = _exchange_wait(*later_sems[:4], after, name="wait_later_weights")
        g = [lax.dynamic_update_index_in_dim(land, a, me, 0) for land, a in zip(lands, later)]
        more_w = dict(w_glu=g[0].reshape(SSM_WIDTH, SSM_WIDTH), w_proj_att=_slots_to_cols(g[1]),
                      w_proj_ssm=_slots_to_cols(g[2]), w_out=g[3].reshape(D, D), w_up=g[4],
                      w_down=g[6].reshape(D_FF, D))
        return more_w, dict(w_conv=_slots_to_cols(g[5]))

    ab_re, ab_im, f_re, f_im = _s5_params(a_re[0], a_im[0], log_dt[0].reshape(SSM_GROUPS, 1))
    col = lambda a: a.reshape(SSM_COLS, 1)
    b_re2, b_im2 = b_re[0].reshape(SSM_COLS, SSM_GROUP_CH), b_im[0].reshape(SSM_COLS, SSM_GROUP_CH)
    bb_re, bb_im = _s5_input_matrix(col(f_re), col(f_im), b_re2, b_im2)
    slopes = jnp.asarray([2.0 ** (-8.0 * (h + 1) / N_HEADS) for h in range(N_HEADS)], F32)
    P = dict(g_mix=g_mix, g_ffn=g_ffn, g_final=g_final.reshape(1, D), b_gate=b_gate, d_skip=d_skip, b_glu=b_glu,
             b_conv=b_conv, slopes=slopes,
             a_row=_interleave(ab_re.reshape(1, SSM_COLS), ab_im.reshape(1, SSM_COLS)),
             bb_big=_interleave(_block_diag_in(bb_re), _block_diag_in(bb_im)),
             cc_big=_interleave(_block_diag_out(c_re[0]).T, -_block_diag_out(c_im[0]).T).T)

    in_flight = []

    def send_early(grads, carry=()):
        names = list(grads)
        handles = _exchange_start([grads[n] for n in names], "start_gradients_%d" % len(in_flight), gather=False,
                                  carry=carry)
        in_flight.append((names,) + handles[:4])
        return handles[4], handles[5]

    grad_x, d_mod, small = _local_step(x, mod, loss_target, W, late_weights, P, send_early)

    small_list = [small["loss_cols"], small["g_mix"], small["b_gate"], small["g_ab_re"], small["g_ab_im"],
                  _diag_blocks_in(small["d_bb_re"]), _diag_blocks_in(small["d_bb_im"]),
                  _diag_blocks_out(small["d_cc_re"]), -_diag_blocks_out(small["d_cc_im"]),
                  small["g_ffn"], small["b_conv"], small["g_final"], small["d_skip"], small["b_glu"]]
    small_packed, small_offs = _pack(small_list)
    small_all, dmod_slots = _gather_all([small_packed, d_mod.reshape(B, 6 * D)], name="gather_small_gradients")

    out = {}

    def update(name, parts, own=None):
        w2 = args[name][0]
        g, dl, mn, vn = _adamw(w2, args["m_" + name][0], args["v_" + name][0], parts, name="adamw_" + name, own=own)
        for key, val in (("grad_", g), ("delta_", dl), ("new_m_", mn), ("new_v_", vn)):
            out[key + name] = val[None]

    my_slot = me.astype(jnp.int32).reshape(1)
    for i, (names, send_sems, recv_sems, sent, lands) in enumerate(in_flight):
        sent, lands = _exchange_wait(send_sems, recv_sems, sent, lands, dmod_slots, name="wait_gradients_%d" % i)
        for name, own_slots, landed in zip(names, sent, lands):
            update(name, landed, own=(own_slots, my_slot))

    dmod_all = dmod_slots.reshape(N_DEV * B, 6 * D)
    dmod_cols = lax.dynamic_slice(dmod_all, (0, me * n_ada), (N_DEV * B, n_ada))
    d_w_ada, d_b_ada = _ada_bwd(c_all, dmod_all, dmod_cols)
    update("w_ada", d_w_ada[None])

    loss_row, loss_n = small_offs[0]
    small_sum, loss_vec = _sum_parts(small_all, (loss_row, loss_row + loss_n // LANES))
    shapes = [(1, D), (1, D), (1, 2 * D), (SSM_GROUPS, SSM_STATE), (SSM_GROUPS, SSM_STATE), (SSM_COLS, SSM_GROUP_CH),
              (SSM_COLS, SSM_GROUP_CH), (1, SSM_GROUPS, SSM_GROUP_CH, SSM_STATE),
              (1, SSM_GROUPS, SSM_GROUP_CH, SSM_STATE), (1, D), (1, D_FF), (D,), (1, SSM_WIDTH), (1, SSM_WIDTH)]
    (_, s_g_mix, s_b_gate, s_ab_re, s_ab_im, s_bb_re, s_bb_im, s_c_re, s_c_im, s_g_ffn, s_b_conv, s_g_final,
     s_d_skip, s_b_glu) = _unpack(small_sum, small_offs, shapes)
    d_b_re2, d_b_im2, d_f_re, d_f_im = _s5_input_matrix_bwd(col(f_re), col(f_im), b_re2, b_im2, s_bb_re, s_bb_im)
    d_a_re, d_a_im, d_log_dt = _s5_params_bwd(a_re[0], a_im[0], log_dt[0].reshape(SSM_GROUPS, 1), s_ab_re, s_ab_im,
                                              d_f_re.reshape(SSM_GROUPS, SSM_STATE),
                                              d_f_im.reshape(SSM_GROUPS, SSM_STATE))
    grads_small = dict(b_ada=d_b_ada, g_mix=s_g_mix, b_gate=s_b_gate, a_re=d_a_re[None], a_im=d_a_im[None],
                       log_dt=d_log_dt.reshape(1, SSM_GROUPS), b_re=d_b_re2.reshape(b_re.shape),
                       b_im=d_b_im2.reshape(b_im.shape), c_re=s_c_re, c_im=s_c_im, d_skip=s_d_skip, b_glu=s_b_glu,
                       g_ffn=s_g_ffn, b_conv=s_b_conv, g_final=s_g_final)
    w_pack, offs = _pack([args[n] for n in SMALL_ORDER])
    m_pack, _ = _pack([args["m_" + n] for n in SMALL_ORDER])
    v_pack, _ = _pack([args["v_" + n] for n in SMALL_ORDER])
    g_pack, _ = _pack([grads_small[n] for n in SMALL_ORDER])
    res = _adamw(w_pack, m_pack, v_pack, g_pack[None], name="adamw_small")
    shapes_small = [args[n].shape for n in SMALL_ORDER]
    for key, packed in zip(("grad_", "delta_", "new_m_", "new_v_"), res):
        for n, val in zip(SMALL_ORDER, _unpack(packed, offs, shapes_small)):
            out[key + n] = val

    order = ["w_ada", "b_ada", "g_mix", "w_in", "b_gate", "a_re", "a_im", "log_dt", "b_re", "b_im", "c_re", "c_im",
             "d_skip", "w_glu", "b_glu", "w_proj_att", "w_proj_ssm", "w_out", "g_ffn", "w_up", "w_conv", "b_conv",
             "w_down", "g_final"]
    loss = loss_vec[0, 0]
    return (loss, grad_x, *[out[k + n] for k in ("grad_", "delta_", "new_m_", "new_v_") for n in order])
```

```python
import math

import jax
import jax.numpy as jnp
from jax import lax
from jax.experimental import pallas as pl
from jax.experimental.pallas import tpu as pltpu

F32 = jnp.float32
BF16 = jnp.bfloat16

N_DEV = 8
D_MODEL = 1024
N_HEADS = 8
HEAD_DIM = 64
ATT_WIDTH = N_HEADS * HEAD_DIM
DILATIONS = (1, 4, 16)
WIN = 128
SSM_GROUPS = 16
SSM_GROUP_CH = 16
SSM_WIDTH = SSM_GROUPS * SSM_GROUP_CH
SSM_STATE = 64
SSM_COLS = SSM_GROUPS * SSM_STATE
D_FF = 2048
EPS = 1e-6
NEG_INF = -1e30
ADAM_LR, ADAM_B1, ADAM_B2, ADAM_EPS, ADAM_WD, ADAM_STEP = 0.001, 0.9, 0.999, 1e-08, 0.01, 10

V7X_VMEM_LIMIT = 56 * 1024 * 1024
LANES = 128


def _params(n_grid):
    return pltpu.CompilerParams(dimension_semantics=("arbitrary",) * n_grid,
                                vmem_limit_bytes=V7X_VMEM_LIMIT)


def _tile(n, pref):
    if n <= pref:
        return n
    t = (pref // LANES) * LANES
    while t > 0:
        if n % t == 0:
            return t
        t -= LANES
    return n


def _matmul(a, b, *, ta=False, tb=False, out_dtype=F32, name):
    if ta:
        K, M = a.shape
    else:
        M, K = a.shape
    if tb:
        N, K2 = b.shape
    else:
        K2, N = b.shape
    assert K == K2, (a.shape, b.shape)
    if ta:
        tm, tn, tk = _tile(M, 1024), _tile(N, 2048), _tile(K, 1024)
    else:
        tm, tk = _tile(M, 512), _tile(K, 4096)
        tn = _tile(N, 2048 if K <= 2048 else 1024)
    nk = K // tk
    dn = (((0,) if ta else (1,), (1,) if tb else (0,)), ((), ()))

    def body(a_ref, b_ref, o_ref, acc_ref):
        k = pl.program_id(2)
        part = lax.dot_general(a_ref[...].astype(BF16), b_ref[...].astype(BF16), dn, preferred_element_type=F32)
        if nk == 1:
            o_ref[...] = part.astype(o_ref.dtype)
            return

        @pl.when(k == 0)
        def _():
            acc_ref[...] = jnp.zeros_like(acc_ref)

        acc_ref[...] += part

        @pl.when(k == nk - 1)
        def _():
            o_ref[...] = acc_ref[...].astype(o_ref.dtype)

    a_spec = (pl.BlockSpec((tk, tm), lambda j, i, k: (k, i)) if ta
              else pl.BlockSpec((tm, tk), lambda j, i, k: (i, k)))
    b_spec = (pl.BlockSpec((tn, tk), lambda j, i, k: (j, k)) if tb
              else pl.BlockSpec((tk, tn), lambda j, i, k: (k, j)))
    return pl.pallas_call(
        body, name=name, grid=(N // tn, M // tm, nk),
        in_specs=[a_spec, b_spec],
        out_specs=pl.BlockSpec((tm, tn), lambda j, i, k: (i, j)),
        out_shape=jax.ShapeDtypeStruct((M, N), out_dtype),
        scratch_shapes=[pltpu.VMEM((tm, tn) if nk > 1 else (8, LANES), F32)],
        compiler_params=_params(3),
    )(a, b)


HALF = 256
UP_SLOTS = N_DEV // 2
UP_GROUP = 4 * HALF


def _group_weight(w_ref):
    return jnp.concatenate([w_ref[0, :, :HALF], w_ref[1, :, :HALF], w_ref[0, :, HALF:], w_ref[1, :, HALF:]], axis=1)


def _up_weight_spec(K, index):
    return pl.BlockSpec((2, None, K, 2 * HALF), index)


def _up_fwd(a, w3, name):
    M, K = a.shape
    tm = _tile(M, 1024)

    def body(a_ref, w_ref, o_ref):
        o_ref[...] = jnp.dot(a_ref[...].astype(BF16), _group_weight(w_ref),
                             preferred_element_type=F32).astype(o_ref.dtype)

    return pl.pallas_call(
        body, name=name, grid=(UP_SLOTS, M // tm),
        in_specs=[pl.BlockSpec((tm, K), lambda j, i: (i, 0)), _up_weight_spec(K, lambda j, i: (0, j, 0, 0))],
        out_specs=pl.BlockSpec((tm, UP_GROUP), lambda j, i: (i, j)),
        out_shape=jax.ShapeDtypeStruct((M, UP_SLOTS * UP_GROUP), BF16), compiler_params=_params(2),
    )(a, w3.reshape(2, UP_SLOTS, K, 2 * HALF))


def _up_dx(d, w3, name):
    M = d.shape[0]
    K = w3.shape[1]
    tm = _tile(M, 1024)

    def body(d_ref, w_ref, o_ref, acc_ref):
        j = pl.program_id(1)

        @pl.when(j == 0)
        def _():
            acc_ref[...] = jnp.zeros_like(acc_ref)

        acc_ref[...] += lax.dot_general(d_ref[...], _group_weight(w_ref), _NT, preferred_element_type=F32)

        @pl.when(j == UP_SLOTS - 1)
        def _():
            o_ref[...] = acc_ref[...].astype(o_ref.dtype)

    return pl.pallas_call(
        body, name=name, grid=(M // tm, UP_SLOTS),
        in_specs=[pl.BlockSpec((tm, UP_GROUP), lambda i, j: (i, j)), _up_weight_spec(K, lambda i, j: (0, j, 0, 0))],
        out_specs=pl.BlockSpec((tm, K), lambda i, j: (i, 0)),
        out_shape=jax.ShapeDtypeStruct((M, K), BF16), scratch_shapes=[pltpu.VMEM((tm, K), F32)],
        compiler_params=_params(2),
    )(d, w3.reshape(2, UP_SLOTS, K, 2 * HALF))


def _up_dw(a, d, name):
    M, K = a.shape
    tk = _tile(M, 1024)
    nk = M // tk

    def body(a_ref, d_ref, o_ref, acc_ref):
        k = pl.program_id(1)

        @pl.when(k == 0)
        def _():
            acc_ref[...] = jnp.zeros_like(acc_ref)

        acc_ref[...] += lax.dot_general(a_ref[...], d_ref[...], _TN, preferred_element_type=F32)

        @pl.when(k == nk - 1)
        def _():
            for half in range(2):
                for part in range(2):
                    lo = (2 * half + part) * HALF
                    o_ref[part, :, half * HALF:(half + 1) * HALF] = acc_ref[:, lo:lo + HALF].astype(o_ref.dtype)

    out = pl.pallas_call(
        body, name=name, grid=(UP_SLOTS, nk),
        in_specs=[pl.BlockSpec((tk, K), lambda j, k: (k, 0)), pl.BlockSpec((tk, UP_GROUP), lambda j, k: (k, j))],
        out_specs=_up_weight_spec(K, lambda j, k: (0, j, 0, 0)),
        out_shape=jax.ShapeDtypeStruct((2, UP_SLOTS, K, 2 * HALF), BF16),
        scratch_shapes=[pltpu.VMEM((K, UP_GROUP), F32)], compiler_params=_params(2),
    )(a, d)
    return out.reshape(N_DEV, K, 2 * HALF)


def _rowwise(fn, rows, bvecs, consts, out_rows, out_b, out_g, *, ts, name):
    B, S = rows[0][0].shape[:2]
    nin = len(rows) + len(bvecs) + len(consts)
    nr, nb, ng = len(out_rows), len(out_b), len(out_g)

    def body(*refs):
        b = pl.program_id(0)
        s = pl.program_id(1)
        vals = [r[...] for r in refs[:nin]]
        vals[:len(rows)] = [v.astype(F32) for v in vals[:len(rows)]]
        outs = fn(*vals)
        if not isinstance(outs, (tuple, list)):
            outs = (outs,)
        orefs = refs[nin:]
        for i in range(nr):
            orefs[i][...] = outs[i].astype(orefs[i].dtype)
        for i in range(nb):
            ref = orefs[nr + i]

            @pl.when(s == 0)
            def _(ref=ref):
                ref[...] = jnp.zeros_like(ref)

            ref[...] += outs[nr + i]
        for i in range(ng):
            ref = orefs[nr + nb + i]

            @pl.when((s == 0) & (b == 0))
            def _(ref=ref):
                ref[...] = jnp.zeros_like(ref)

            ref[...] += outs[nr + nb + i]

    rows = [r if len(r) == 4 else r + (0,) for r in rows]
    in_specs = ([pl.BlockSpec((None, ts, cb), lambda b, s, ci=ci, b0=b0: (b0 + b, s, ci)) for (_, cb, ci, b0) in rows]
                + [pl.BlockSpec((None, 1, cb), lambda b, s, ci=ci: (b, 0, ci)) for (_, cb, ci) in bvecs]
                + [pl.BlockSpec(a.shape, lambda b, s: (0, 0)) for a in consts])
    out_shape = ([jax.ShapeDtypeStruct((B, S, c), dt) for (c, dt) in out_rows]
                 + [jax.ShapeDtypeStruct((B, 1, c), F32) for c in out_b]
                 + [jax.ShapeDtypeStruct(rc, F32) for rc in out_g])
    out_specs = ([pl.BlockSpec((None, ts, c), lambda b, s: (b, s, 0)) for (c, _) in out_rows]
                 + [pl.BlockSpec((None, 1, c), lambda b, s: (b, 0, 0)) for c in out_b]
                 + [pl.BlockSpec(rc, lambda b, s: (0, 0)) for rc in out_g])
    args = [r[0] for r in rows] + [a for (a, _, _) in bvecs] + list(consts)
    return pl.pallas_call(
        body, name=name, grid=(B, S // ts), in_specs=in_specs, out_specs=out_specs,
        out_shape=out_shape, compiler_params=_params(2),
    )(*args)


def _col_sum(v):
    return jnp.sum(v, axis=0, keepdims=True)


def _rms_scale(h):
    return lax.rsqrt(jnp.mean(h * h, axis=-1, keepdims=True) + EPS)


def _sigmoid(v):
    return 1.0 / (1.0 + jnp.exp(-v))


ATT_SCALE = HEAD_DIM ** -0.5
COPY_ROWS = 256
_NT = (((1,), (1,)), ((), ()))
_TN = (((0,), (0,)), ((), ()))


def _row_chunks(d, seq):
    sub = seq // d
    out = []
    for r in range(d):
        for c0 in range(0, sub, COPY_ROWS):
            n = min(COPY_ROWS, sub - c0)
            out.append((pl.ds(r + c0 * d, n, stride=d), r * sub + c0, n))
    return out


ATT_UNROLL = 8
KEYS = 2 * WIN


def _zero_once(refs):
    @pl.when((pl.program_id(0) == 0) & (pl.program_id(1) == 0))
    def _():
        for r in refs:
            r[...] = jnp.zeros_like(r)


def _pair_bias(bias_ref, slopes_ref, hp, d, key_major):
    shape = (KEYS, WIN) if key_major else (WIN, KEYS)
    qi = lax.broadcasted_iota(jnp.int32, shape, 1 if key_major else 0)
    kj = lax.broadcasted_iota(jnp.int32, shape, 0 if key_major else 1)
    dist = WIN + qi - kj
    valid = (dist >= 0) & (dist <= WIN)
    distf = dist.astype(F32)
    for h in range(2):
        slope_d = slopes_ref[2 * hp + h] * float(d)
        with_prev = jnp.where(valid, -(slope_d * distf), NEG_INF)
        no_prev = jnp.where(kj >= WIN, with_prev, NEG_INF)
        span = slice(h * KEYS, (h + 1) * KEYS)
        if key_major:
            bias_ref[1, span, :] = with_prev
            bias_ref[0, span, :] = no_prev
        else:
            bias_ref[1, :, span] = with_prev
            bias_ref[0, :, span] = no_prev


def _stack_heads(v):
    first = lax.broadcasted_iota(jnp.int32, v.shape, 1) < HEAD_DIM
    zero = jnp.zeros_like(v)
    return jnp.concatenate([jnp.where(first, v, zero), jnp.where(first, zero, v)], axis=0)


def _per_head(c0, c1, n):
    return jnp.where(lax.broadcasted_iota(jnp.int32, (n, LANES), 1) < HEAD_DIM, c0, c1)


def _qkv_spec(seq, j):
    return pl.BlockSpec((None, seq, LANES), lambda b, hp: (b, 0, 3 * hp + j))


def _attention_fwd(qkv, slopes):
    B, S, _ = qkv.shape
    n_blk = S // WIN
    n_pair = N_HEADS // 2

    def body(slopes_ref, q_ref, k_ref, v_ref, o_ref, lse_ref, qp, kp, vp, bias, acc, mx, sm, acc_n, mx_n, sm_n):
        hp = pl.program_id(1)
        _zero_once((kp, vp))
        for p, d in enumerate(DILATIONS):
            nb = n_blk // d
            chunks = _row_chunks(d, S)
            for src, dst, n in chunks:
                qp[dst:dst + n, :] = (q_ref[src, :] * ATT_SCALE).astype(BF16)
                kp[WIN + dst:WIN + dst + n, :] = k_ref[src, :].astype(BF16)
                vp[WIN + dst:WIN + dst + n, :] = v_ref[src, :].astype(BF16)
            _pair_bias(bias, slopes_ref, hp, d, key_major=False)
            acc_t, mx_t, sm_t = (acc_n, mx_n, sm_n) if d == 1 else (acc, mx, sm)

            nk = WIN if nb == 1 else KEYS
            bias_cur = jnp.concatenate([bias[0, :, WIN:KEYS], bias[0, :, KEYS + WIN:]], axis=1) if nb == 1 else None

            def block(i, carry, p=p, nb=nb, nk=nk, bias_cur=bias_cur, acc_t=acc_t, mx_t=mx_t, sm_t=sm_t):
                cur = pl.ds(pl.multiple_of(i * WIN, WIN), WIN)
                keys = pl.ds(pl.multiple_of(i * WIN + (KEYS - nk), WIN), nk)
                s = lax.dot_general(qp[cur, :], _stack_heads(kp[keys, :]), _NT, preferred_element_type=F32)
                s = s + (bias_cur if nb == 1 else bias[((i % nb) > 0).astype(jnp.int32)])
                es, ms, ls = [], [], []
                for h in range(2):
                    sh = s[:, h * nk:(h + 1) * nk]
                    m = jnp.max(sh if nb == 1 else jnp.maximum(sh[:, :WIN], sh[:, WIN:]), axis=1, keepdims=True)
                    e = jnp.exp(sh - m)
                    es.append(e.astype(BF16))
                    ms.append(m)
                    ls.append(jnp.sum(e if nb == 1 else e[:, :WIN] + e[:, WIN:], axis=1, keepdims=True))
                acc_t[p, cur, :] = jnp.dot(jnp.concatenate(es, axis=1), _stack_heads(vp[keys, :]),
                                           preferred_element_type=F32)
                mx_t[p, cur, :] = _per_head(ms[0], ms[1], WIN)
                sm_t[p, cur, :] = _per_head(ls[0], ls[1], WIN)
                return carry

            lax.fori_loop(0, n_blk, block, 0, unroll=ATT_UNROLL)
            if d > 1:
                for src, dst, n in chunks:
                    acc_n[p, src, :] = acc[p, dst:dst + n, :]
                    mx_n[p, src, :] = mx[p, dst:dst + n, :]
                    sm_n[p, src, :] = sm[p, dst:dst + n, :]

        chunk = 256

        def merge(i, carry):
            rows = pl.ds(pl.multiple_of(i * chunk, chunk), chunk)
            ms = [mx_n[p, rows, :] for p in range(3)]
            m = jnp.maximum(jnp.maximum(ms[0], ms[1]), ms[2])
            ws = [jnp.exp(mp - m) for mp in ms]
            l = ws[0] * sm_n[0, rows, :] + ws[1] * sm_n[1, rows, :] + ws[2] * sm_n[2, rows, :]
            o = (ws[0] * acc_n[0, rows, :] + ws[1] * acc_n[1, rows, :] + ws[2] * acc_n[2, rows, :]) / l
            o_ref[rows, :] = o.astype(o_ref.dtype)
            lse = m + jnp.log(l)
            for h in range(2):
                lse_ref[rows, h:h + 1] = lse[:, h * HEAD_DIM:h * HEAD_DIM + 1]
            return carry

        lax.fori_loop(0, S // chunk, merge, 0)

    return pl.pallas_call(
        body, name="attention_fwd", grid=(B, n_pair),
        in_specs=[pl.BlockSpec(memory_space=pltpu.SMEM), _qkv_spec(S, 0), _qkv_spec(S, 1), _qkv_spec(S, 2)],
        out_specs=[pl.BlockSpec((None, S, LANES), lambda b, hp: (b, 0, hp)),
                   pl.BlockSpec((None, None, S, 2), lambda b, hp: (b, hp, 0, 0))],
        out_shape=[jax.ShapeDtypeStruct((B, S, ATT_WIDTH), BF16),
                   jax.ShapeDtypeStruct((B, n_pair, S, 2), F32)],
        scratch_shapes=[pltpu.VMEM((S, LANES), BF16), pltpu.VMEM((S + WIN, LANES), BF16),
                        pltpu.VMEM((S + WIN, LANES), BF16), pltpu.VMEM((2, WIN, 2 * KEYS), F32)]
        + [pltpu.VMEM((3, S, LANES), F32)] * 6,
        compiler_params=_params(2),
    )(slopes, qkv, qkv, qkv)


def _attention_bwd(qkv, o, do, lse, slopes):
    B, S, _ = qkv.shape
    n_blk = S // WIN
    n_pair = N_HEADS // 2

    def body(slopes_ref, q_ref, k_ref, v_ref, o_ref, do_ref, lse_ref, dx_ref,
             qp, dop, kp, vp, aux, auxp, aux_t, bias_t, dqp, dvk, dq_n, dk_n, dv_n):
        hp = pl.program_id(1)
        aux[...] = jnp.zeros_like(aux)
        for c0 in range(0, S, COPY_ROWS):
            rows = slice(c0, c0 + COPY_ROWS)
            prod = do_ref[rows, :] * o_ref[rows, :].astype(F32)
            for h in range(2):
                aux[rows, 2 * h:2 * h + 1] = lse_ref[rows, h:h + 1]
                aux[rows, 2 * h + 1:2 * h + 2] = jnp.sum(prod[:, h * HEAD_DIM:(h + 1) * HEAD_DIM], axis=1,
                                                         keepdims=True)
        dq_n[...] = jnp.zeros_like(dq_n)
        dk_n[...] = jnp.zeros_like(dk_n)
        dv_n[...] = jnp.zeros_like(dv_n)
        _zero_once((kp, vp))
        for p, d in enumerate(DILATIONS):
            nb = n_blk // d
            chunks = _row_chunks(d, S)
            for src, dst, n in chunks:
                auxp[dst:dst + n, :] = aux[src, :]
                qp[dst:dst + n, :] = (q_ref[src, :] * ATT_SCALE).astype(BF16)
                dop[dst:dst + n, :] = do_ref[src, :].astype(BF16)
                kp[WIN + dst:WIN + dst + n, :] = k_ref[src, :].astype(BF16)
                vp[WIN + dst:WIN + dst + n, :] = v_ref[src, :].astype(BF16)
            for i in range(n_blk):
                aux_t[i] = auxp[i * WIN:(i + 1) * WIN, :].T[0:8, :]
            _pair_bias(bias_t, slopes_ref, hp, d, key_major=True)
            dvk[...] = jnp.zeros_like(dvk)

            nk = WIN if nb == 1 else KEYS
            bias_cur = jnp.concatenate([bias_t[0, WIN:KEYS, :], bias_t[0, KEYS + WIN:, :]], axis=0) if nb == 1 else None

            def block(i, carry, nb=nb, nk=nk, bias_cur=bias_cur):
                cur = pl.ds(pl.multiple_of(i * WIN, WIN), WIN)
                keys = pl.ds(pl.multiple_of(i * WIN + (KEYS - nk), WIN), nk)
                q2, do2 = qp[cur, :], dop[cur, :]
                kc = _stack_heads(kp[keys, :])
                s_t = lax.dot_general(kc, q2, _NT, preferred_element_type=F32)
                s_t = s_t + (bias_cur if nb == 1 else bias_t[((i % nb) > 0).astype(jnp.int32)])
                dp_t = lax.dot_general(_stack_heads(vp[keys, :]), do2, _NT, preferred_element_type=F32)
                ps, dss = [], []
                for h in range(2):
                    span = slice(h * nk, (h + 1) * nk)
                    p_t = jnp.exp(s_t[span, :] - aux_t[i, 2 * h:2 * h + 1, :])
                    ds_t = p_t * (dp_t[span, :] - aux_t[i, 2 * h + 1:2 * h + 2, :])
                    ps.append(p_t.astype(BF16))
                    dss.append(ds_t.astype(BF16))
                do_rows, q_rows = _stack_heads(do2), _stack_heads(q2)
                zr = jnp.zeros_like(do_rows)
                rhs = jnp.concatenate([jnp.concatenate([do_rows, zr], axis=1),
                                       jnp.concatenate([zr, q_rows], axis=1)], axis=0)
                dvk[keys, :] += jnp.dot(jnp.concatenate(ps + dss, axis=1), rhs, preferred_element_type=F32)
                dqp[cur, :] = lax.dot_general(jnp.concatenate(dss, axis=0), kc, _TN, preferred_element_type=F32)
                return carry

            lax.fori_loop(0, n_blk, block, 0, unroll=ATT_UNROLL)
            for src, dst, n in chunks:
                dq_n[src, :] += dqp[dst:dst + n, :]
                dv_n[src, :] += dvk[WIN + dst:WIN + dst + n, :LANES]
                dk_n[src, :] += dvk[WIN + dst:WIN + dst + n, LANES:]
        for c0 in range(0, S, COPY_ROWS):
            rows = slice(c0, c0 + COPY_ROWS)
            dx_ref[rows, 0:LANES] = (dq_n[rows, :] * ATT_SCALE).astype(dx_ref.dtype)
            dx_ref[rows, LANES:2 * LANES] = dk_n[rows, :].astype(dx_ref.dtype)
            dx_ref[rows, 2 * LANES:3 * LANES] = dv_n[rows, :].astype(dx_ref.dtype)

    pair = lambda width: pl.BlockSpec((None, S, width), lambda b, hp: (b, 0, hp))
    vm = lambda shape, dt: pltpu.VMEM(shape, dt)
    return pl.pallas_call(
        body, name="attention_bwd", grid=(B, n_pair),
        in_specs=[pl.BlockSpec(memory_space=pltpu.SMEM), _qkv_spec(S, 0), _qkv_spec(S, 1), _qkv_spec(S, 2),
                  pair(LANES), pair(LANES), pl.BlockSpec((None, None, S, 2), lambda b, hp: (b, hp, 0, 0))],
        out_specs=pair(3 * LANES),
        out_shape=jax.ShapeDtypeStruct((B, S, 3 * ATT_WIDTH), BF16),
        scratch_shapes=[vm((S, LANES), BF16), vm((S, LANES), BF16),
                        vm((S + WIN, LANES), BF16), vm((S + WIN, LANES), BF16),
                        vm((S, LANES), F32), vm((S, LANES), F32), vm((n_blk, 8, WIN), F32),
                        vm((2, 2 * KEYS, WIN), F32),
                        vm((S, LANES), F32), vm((S + WIN, 2 * LANES), F32),
                        vm((S, LANES), F32), vm((S, LANES), F32), vm((S, LANES), F32)],
        compiler_params=_params(2),
    )(slopes, qkv, qkv, qkv, o, do, lse)


SCAN_COLS = 256
SCAN_ROWS = 8


def _rows_to_tile(rows):
    rid = lax.broadcasted_iota(jnp.int32, (SCAN_ROWS, rows[0].shape[1]), 0)
    tile = jnp.broadcast_to(rows[0], rid.shape)
    for k in range(1, SCAN_ROWS):
        tile = jnp.where(rid == k, rows[k], tile)
    return tile


SCAN_UNROLL = 4


def _complex_powers(ar, ai, n):
    out = [(ar, ai)]
    for _ in range(n - 1):
        pr, pi = out[-1]
        out.append((pr * ar - pi * ai, pr * ai + pi * ar))
    return out


def _round_multipliers(powers, rid, reverse):
    out = []
    for s in (1, 2, 4):
        keep = (rid < SCAN_ROWS - s) if reverse else (rid >= s)
        out.append((jnp.where(keep, powers[s - 1][0], 0.0), jnp.where(keep, powers[s - 1][1], 0.0)))
    return out


def _tile_scan(xr, xi, multipliers, reverse):
    for s, (mr, mi) in zip((1, 2, 4), multipliers):
        shift = SCAN_ROWS - s if reverse else s
        sr, si = pltpu.roll(xr, shift, 0), pltpu.roll(xi, shift, 0)
        xr, xi = xr + (mr * sr - mi * si), xi + (mr * si + mi * sr)
    return xr, xi


SCAN_CHUNK = 256


def _scan_fwd(us, bb_big, a_row, cc_big):
    B, S, _ = us.shape
    groups = 2
    width = 2 * groups * SCAN_COLS
    nc = 2 * SSM_COLS // width
    nt = S // SCAN_ROWS
    tiles = SCAN_CHUNK // SCAN_ROWS
    LAST = slice(SCAN_ROWS - 1, SCAN_ROWS)

    def body(us_ref, bb_ref, a_ref, cc_ref, xs_ref, y_ref, bu_ref):
        bb = bb_ref[...].astype(BF16)
        for c in range(S // SCAN_CHUNK):
            part = jnp.dot(us_ref[c * SCAN_CHUNK:(c + 1) * SCAN_CHUNK, :].astype(BF16), bb,
                           preferred_element_type=F32)
            bu_ref[c * tiles:(c + 1) * tiles] = part.reshape(tiles, SCAN_ROWS, width)
        rid = lax.broadcasted_iota(jnp.int32, (SCAN_ROWS, SCAN_COLS), 0)
        consts = []
        for g in range(groups):
            re = slice(2 * g * SCAN_COLS, (2 * g + 1) * SCAN_COLS)
            im = slice((2 * g + 1) * SCAN_COLS, (2 * g + 2) * SCAN_COLS)
            powers = _complex_powers(a_ref[:, re], a_ref[:, im], SCAN_ROWS)
            carry_mult = (_rows_to_tile([p[0] for p in powers]), _rows_to_tile([p[1] for p in powers]))
            consts.append((re, im, carry_mult, _round_multipliers(powers, rid, reverse=False)))

        def tile(i, carry):
            out = []
            for (re, im, (cr_t, ci_t), rounds), (cr, ci) in zip(consts, carry):
                xr, xi = _tile_scan(bu_ref[i, :, re], bu_ref[i, :, im], rounds, reverse=False)
                xs_ref[i, :, re] = xr + (cr_t * cr - ci_t * ci)
                xs_ref[i, :, im] = xi + (cr_t * ci + ci_t * cr)
                out.append((xs_ref[i, LAST, re], xs_ref[i, LAST, im]))
            return tuple(out)

        zero = jnp.zeros((1, SCAN_COLS), F32)
        lax.fori_loop(0, nt, tile, ((zero, zero),) * groups, unroll=SCAN_UNROLL)

        @pl.when(pl.program_id(1) == 0)
        def _():
            y_ref[...] = jnp.zeros_like(y_ref)

        cc = cc_ref[...].astype(BF16)
        for c in range(S // SCAN_CHUNK):
            x2 = xs_ref[c * tiles:(c + 1) * tiles].reshape(SCAN_CHUNK, width).astype(BF16)
            y_ref[c * SCAN_CHUNK:(c + 1) * SCAN_CHUNK, :] += jnp.dot(x2, cc, preferred_element_type=F32)

    col = pl.BlockSpec((None, nt, SCAN_ROWS, width), lambda b, j: (b, 0, 0, j))
    tok = pl.BlockSpec((None, S, SSM_WIDTH), lambda b, j: (b, 0, 0))
    xs, y = pl.pallas_call(
        body, name="s5_scan_fwd", grid=(B, nc),
        in_specs=[tok, pl.BlockSpec((SSM_WIDTH, width), lambda b, j: (0, j)),
                  pl.BlockSpec((1, width), lambda b, j: (0, j)), pl.BlockSpec((width, SSM_WIDTH), lambda b, j: (j, 0))],
        out_specs=[col, tok],
        out_shape=[jax.ShapeDtypeStruct((B, nt, SCAN_ROWS, 2 * SSM_COLS), F32),
                   jax.ShapeDtypeStruct((B, S, SSM_WIDTH), F32)],
        scratch_shapes=[pltpu.VMEM((nt, SCAN_ROWS, width), F32)],
        compiler_params=_params(2),
    )(us, bb_big, a_row, cc_big)
    return xs.reshape(B, S, 2 * SSM_COLS), y


def _scan_bwd(dy, us, bb_big, cc_big, xs, a_row):
    B, S, _ = dy.shape
    width = 2 * SCAN_COLS
    nc = SSM_COLS // SCAN_COLS
    nt = S // SCAN_ROWS
    tiles = SCAN_CHUNK // SCAN_ROWS
    RE, IM = slice(0, SCAN_COLS), slice(SCAN_COLS, 2 * SCAN_COLS)
    FIRST, LAST = slice(0, 1), slice(SCAN_ROWS - 1, SCAN_ROWS)

    def body(dy_ref, us_ref, bb_ref, cc_ref, x_ref, a_ref, dus_ref, ga_ref, dbb_ref, dcc_ref, d_ref, lam_ref):
        b = pl.program_id(1)
        cc = cc_ref[...].astype(BF16)
        for c in range(S // SCAN_CHUNK):
            part = lax.dot_general(dy_ref[c * SCAN_CHUNK:(c + 1) * SCAN_CHUNK, :].astype(BF16), cc, _NT,
                                   preferred_element_type=F32)
            d_ref[c * tiles:(c + 1) * tiles] = part.reshape(tiles, SCAN_ROWS, width)
        powers = _complex_powers(a_ref[:, RE], -a_ref[:, IM], SCAN_ROWS)
        rid = lax.broadcasted_iota(jnp.int32, (SCAN_ROWS, SCAN_COLS), 0)
        cr_t = _rows_to_tile([powers[SCAN_ROWS - 1 - r][0] for r in range(SCAN_ROWS)])
        ci_t = _rows_to_tile([powers[SCAN_ROWS - 1 - r][1] for r in range(SCAN_ROWS)])
        rounds = _round_multipliers(powers, rid, reverse=True)

        @pl.when(b == 0)
        def _():
            ga_ref[...] = jnp.zeros_like(ga_ref)
            dbb_ref[...] = jnp.zeros_like(dbb_ref)
            dcc_ref[...] = jnp.zeros_like(dcc_ref)

        def tile(j, carry):
            cr, ci, accr, acci = carry
            i = nt - 1 - j
            lr, li = _tile_scan(d_ref[i, :, RE], d_ref[i, :, IM], rounds, reverse=True)
            lam_r = lr + (cr_t * cr - ci_t * ci)
            lam_i = li + (cr_t * ci + ci_t * cr)
            lam_ref[i, :, RE] = lam_r
            lam_ref[i, :, IM] = lam_i
            ip = jnp.maximum(i - 1, 0)
            keep = (i > 0).astype(F32)
            xpr = jnp.where(rid == 0, x_ref[ip, LAST, RE] * keep, pltpu.roll(x_ref[i, :, RE], 1, 0))
            xpi = jnp.where(rid == 0, x_ref[ip, LAST, IM] * keep, pltpu.roll(x_ref[i, :, IM], 1, 0))
            accr = accr + lam_r * xpr + lam_i * xpi
            acci = acci + lam_i * xpr - lam_r * xpi
            return lam_ref[i, FIRST, RE], lam_ref[i, FIRST, IM], accr, acci

        z1 = jnp.zeros((1, SCAN_COLS), F32)
        z8 = jnp.zeros((SCAN_ROWS, SCAN_COLS), F32)
        _, _, accr, acci = lax.fori_loop(0, nt, tile, (z1, z1, z8, z8), unroll=SCAN_UNROLL)
        ga_ref[:, RE] += _col_sum(accr)
        ga_ref[:, IM] += _col_sum(acci)

        bb = bb_ref[...].astype(BF16)
        for c in range(S // SCAN_CHUNK):
            rows = slice(c * SCAN_CHUNK, (c + 1) * SCAN_CHUNK)
            lam2 = lam_ref[c * tiles:(c + 1) * tiles].reshape(SCAN_CHUNK, width).astype(BF16)
            x2 = x_ref[c * tiles:(c + 1) * tiles].reshape(SCAN_CHUNK, width).astype(BF16)
            dus_ref[rows, :] = lax.dot_general(lam2, bb, _NT, preferred_element_type=F32)
            dbb_ref[...] += lax.dot_general(us_ref[rows, :].astype(BF16), lam2, _TN, preferred_element_type=F32)
            dcc_ref[...] += lax.dot_general(x2, dy_ref[rows, :].astype(BF16), _TN, preferred_element_type=F32)

    col = pl.BlockSpec((None, nt, SCAN_ROWS, width), lambda j, b: (b, 0, 0, j))
    tok = pl.BlockSpec((None, S, SSM_WIDTH), lambda j, b: (b, 0, 0))
    scratch = pltpu.VMEM((nt, SCAN_ROWS, width), F32)
    return pl.pallas_call(
        body, name="s5_scan_bwd", grid=(nc, B),
        in_specs=[tok, tok, pl.BlockSpec((SSM_WIDTH, width), lambda j, b: (0, j)),
                  pl.BlockSpec((width, SSM_WIDTH), lambda j, b: (j, 0)), col,
                  pl.BlockSpec((1, width), lambda j, b: (0, j))],
        out_specs=[pl.BlockSpec((None, None, S, SSM_WIDTH), lambda j, b: (j, b, 0, 0)),
                   pl.BlockSpec((1, width), lambda j, b: (0, j)),
                   pl.BlockSpec((SSM_WIDTH, width), lambda j, b: (0, j)),
                   pl.BlockSpec((width, SSM_WIDTH), lambda j, b: (j, 0))],
        out_shape=[jax.ShapeDtypeStruct((nc, B, S, SSM_WIDTH), F32), jax.ShapeDtypeStruct((1, 2 * SSM_COLS), F32),
                   jax.ShapeDtypeStruct((SSM_WIDTH, 2 * SSM_COLS), F32),
                   jax.ShapeDtypeStruct((2 * SSM_COLS, SSM_WIDTH), F32)],
        scratch_shapes=[scratch, scratch],
        compiler_params=_params(2),
    )(dy, us, bb_big, cc_big, xs.reshape(B, nt, SCAN_ROWS, 2 * SSM_COLS), a_row)


def _s5_discretise(lr, li, log_dt):
    dt = jnp.exp(log_dt)
    mag = jnp.exp(lr * dt)
    ang = li * dt
    ab_re, ab_im = mag * jnp.cos(ang), mag * jnp.sin(ang)
    nr, ni = ab_re - 1.0, ab_im
    den = lr * lr + li * li
    f_re = (nr * lr + ni * li) / den
    f_im = (ni * lr - nr * li) / den
    return dt, ab_re, ab_im, nr, ni, den, f_re, f_im


def _s5_params(a_re, a_im, log_dt):
    def body(lr_ref, li_ref, ld_ref, abr, abi, fr, fi):
        _, ab_re, ab_im, _, _, _, f_re, f_im = _s5_discretise(lr_ref[...], li_ref[...], ld_ref[...])
        abr[...] = ab_re
        abi[...] = ab_im
        fr[...] = f_re
        fi[...] = f_im

    return pl.pallas_call(body, name="s5_params",
                          out_shape=[jax.ShapeDtypeStruct(a_re.shape, F32)] * 4)(a_re, a_im, log_dt)


def _s5_input_matrix(f_re, f_im, b_re, b_im):
    def body(fr, fi, br, bi, o_re, o_im):
        o_re[...] = fr[...] * br[...] - fi[...] * bi[...]
        o_im[...] = fr[...] * bi[...] + fi[...] * br[...]

    return pl.pallas_call(body, name="s5_input_matrix",
                          out_shape=[jax.ShapeDtypeStruct(b_re.shape, F32)] * 2)(f_re, f_im, b_re, b_im)


def _s5_input_matrix_bwd(f_re, f_im, b_re, b_im, g_re, g_im):
    def body(fr, fi, br, bi, gr, gi, dbr, dbi, dfr, dfi):
        dbr[...] = fr[...] * gr[...] + fi[...] * gi[...]
        dbi[...] = fr[...] * gi[...] - fi[...] * gr[...]
        dfr[...] = jnp.sum(br[...] * gr[...] + bi[...] * gi[...], axis=1, keepdims=True)
        dfi[...] = jnp.sum(br[...] * gi[...] - bi[...] * gr[...], axis=1, keepdims=True)

    return pl.pallas_call(
        body, name="s5_input_matrix_bwd",
        out_shape=[jax.ShapeDtypeStruct(b_re.shape, F32)] * 2 + [jax.ShapeDtypeStruct(f_re.shape, F32)] * 2,
    )(f_re, f_im, b_re, b_im, g_re, g_im)


def _s5_params_bwd(a_re, a_im, log_dt, g_ab_re, g_ab_im, d_f_re, d_f_im):
    def body(lr_ref, li_ref, ld_ref, gar, gai, dfr, dfi, o_lr, o_li, o_ld):
        lr, li = lr_ref[...], li_ref[...]
        dt, ab_re, ab_im, nr, ni, den, f_re, f_im = _s5_discretise(lr, li, ld_ref[...])
        d_fr, d_fi = dfr[...], dfi[...]
        d_nr = (d_fr * lr - d_fi * li) / den
        d_ni = (d_fr * li + d_fi * lr) / den
        common = (d_fr * f_re + d_fi * f_im) * 2.0 / den
        d_lr = (d_fr * nr + d_fi * ni) / den - common * lr
        d_li = (d_fr * ni - d_fi * nr) / den - common * li
        d_abr = gar[...] + d_nr
        d_abi = gai[...] + d_ni
        d_mag_mag = d_abr * ab_re + d_abi * ab_im
        d_ang = d_abi * ab_re - d_abr * ab_im
        o_lr[...] = d_lr + d_mag_mag * dt
        o_li[...] = d_li + d_ang * dt
        o_ld[...] = jnp.sum(d_mag_mag * lr + d_ang * li, axis=1, keepdims=True) * dt

    return pl.pallas_call(
        body, name="s5_params_bwd",
        out_shape=[jax.ShapeDtypeStruct(a_re.shape, F32)] * 2 + [jax.ShapeDtypeStruct(log_dt.shape, F32)],
    )(a_re, a_im, log_dt, g_ab_re, g_ab_im, d_f_re, d_f_im)


CONV_COLS = 256


def _shift_down(v, j, row):
    return jnp.where(row >= j, pltpu.roll(v, j, 0), 0.0)


def _shift_up(v, j, row, seq):
    return jnp.where(row < seq - j, pltpu.roll(v, seq - j, 0), 0.0)


def _conv_fwd(up, w_conv, b_conv):
    B, S, _ = up.shape
    nj = D_FF // CONV_COLS

    def body(up_ref, w_ref, b_ref, ff_ref):
        a = up_ref[:, :CONV_COLS].astype(F32)
        val = up_ref[:, CONV_COLS:].astype(F32)
        row = lax.broadcasted_iota(jnp.int32, a.shape, 0)
        w0, w1, w2 = w_ref[0:1, :], w_ref[1:2, :], w_ref[2:3, :]
        conv = b_ref[...] + w0 * a + w1 * _shift_down(a, 1, row) + w2 * _shift_down(a, 2, row)
        ff_ref[...] = (conv * _sigmoid(conv) * val).astype(ff_ref.dtype)

    return pl.pallas_call(
        body, name="conv_gate_fwd", grid=(B, nj),
        in_specs=[pl.BlockSpec((None, S, 2 * CONV_COLS), lambda b, j: (b, 0, j)),
                  pl.BlockSpec((3, CONV_COLS), lambda b, j: (0, j)),
                  pl.BlockSpec((1, CONV_COLS), lambda b, j: (0, j))],
        out_specs=pl.BlockSpec((None, S, CONV_COLS), lambda b, j: (b, 0, j)),
        out_shape=jax.ShapeDtypeStruct((B, S, D_FF), BF16),
        compiler_params=_params(2),
    )(up, w_conv, b_conv)


def _conv_bwd(up, d_ff, w_conv, b_conv):
    B, S, _ = up.shape
    nj = D_FF // CONV_COLS

    def body(up_ref, dff_ref, w_ref, b_ref, dup_ref, dw_ref, db_ref):
        b = pl.program_id(1)
        a = up_ref[:, :CONV_COLS].astype(F32)
        val = up_ref[:, CONV_COLS:].astype(F32)
        row = lax.broadcasted_iota(jnp.int32, a.shape, 0)
        w0, w1, w2 = w_ref[0:1, :], w_ref[1:2, :], w_ref[2:3, :]
        a1, a2 = _shift_down(a, 1, row), _shift_down(a, 2, row)
        conv = b_ref[...] + w0 * a + w1 * a1 + w2 * a2
        sg = _sigmoid(conv)
        dff = dff_ref[...].astype(F32)
        d_val = dff * conv * sg
        dc = dff * val * (sg * (1.0 + conv * (1.0 - sg)))
        d_a = w0 * dc + w1 * _shift_up(dc, 1, row, S) + w2 * _shift_up(dc, 2, row, S)
        dup_ref[:, :CONV_COLS] = d_a.astype(dup_ref.dtype)
        dup_ref[:, CONV_COLS:] = d_val.astype(dup_ref.dtype)

        @pl.when(b == 0)
        def _():
            dw_ref[...] = jnp.zeros_like(dw_ref)
            db_ref[...] = jnp.zeros_like(db_ref)

        dw_ref[0:1, :] += _col_sum(dc * a)
        dw_ref[1:2, :] += _col_sum(dc * a1)
        dw_ref[2:3, :] += _col_sum(dc * a2)
        db_ref[...] += _col_sum(dc)

    return pl.pallas_call(
        body, name="conv_gate_bwd", grid=(nj, B),
        in_specs=[pl.BlockSpec((None, S, 2 * CONV_COLS), lambda j, b: (b, 0, j)),
                  pl.BlockSpec((None, S, CONV_COLS), lambda j, b: (b, 0, j)),
                  pl.BlockSpec((3, CONV_COLS), lambda j, b: (0, j)),
                  pl.BlockSpec((1, CONV_COLS), lambda j, b: (0, j))],
        out_specs=[pl.BlockSpec((None, S, 2 * CONV_COLS), lambda j, b: (b, 0, j)),
                   pl.BlockSpec((3, CONV_COLS), lambda j, b: (0, j)),
                   pl.BlockSpec((1, CONV_COLS), lambda j, b: (0, j))],
        out_shape=[jax.ShapeDtypeStruct((B, S, 2 * D_FF), BF16), jax.ShapeDtypeStruct((3, D_FF), F32),
                   jax.ShapeDtypeStruct((1, D_FF), F32)],
        compiler_params=_params(2),
    )(up, d_ff, w_conv, b_conv)


def _ada_fwd(c_all, w_ada, b_ada):
    def body(c_ref, w_ref, b_ref, o_ref):
        cv = c_ref[...]
        act = (cv * _sigmoid(cv)).astype(BF16)
        o_ref[...] = jnp.dot(act, w_ref[...].astype(BF16), preferred_element_type=F32) + b_ref[...]

    return pl.pallas_call(body, name="ada_fwd",
                          out_shape=jax.ShapeDtypeStruct((c_all.shape[0], w_ada.shape[1]), F32),
                          compiler_params=pltpu.CompilerParams(vmem_limit_bytes=V7X_VMEM_LIMIT))(c_all, w_ada, b_ada)


def _ada_bwd(c_all, dmod_all, dmod_cols):
    def body(c_ref, dm_ref, dmc_ref, dw_ref, db_ref):
        cv = c_ref[...]
        act = (cv * _sigmoid(cv)).astype(BF16)
        dw_ref[...] = lax.dot_general(act, dmc_ref[...].astype(BF16), _TN, preferred_element_type=F32)
        db_ref[...] = _col_sum(dm_ref[...])

    return pl.pallas_call(
        body, name="ada_bwd",
        out_shape=[jax.ShapeDtypeStruct((c_all.shape[1], dmod_cols.shape[1]), F32),
                   jax.ShapeDtypeStruct((1, dmod_all.shape[1]), F32)],
        compiler_params=pltpu.CompilerParams(vmem_limit_bytes=V7X_VMEM_LIMIT))(c_all, dmod_all, dmod_cols)


def _adamw(w, m, v, g_parts, name, own=None):
    R, C = w.shape
    P = g_parts.shape[0]
    tr = R
    for cand in (256, 128, 64, 32, 16, 8):
        if R % cand == 0 and cand * C * 4 * (P + 8) * 2 <= V7X_VMEM_LIMIT // 2:
            tr = cand
            break
    c1 = 1.0 / (1.0 - ADAM_B1 ** ADAM_STEP)
    c2 = 1.0 / (1.0 - ADAM_B2 ** ADAM_STEP)

    def update(w_ref, m_ref, v_ref, g, og, od, om, ov):
        m_new = ADAM_B1 * m_ref[...] + (1.0 - ADAM_B1) * g
        v_new = ADAM_B2 * v_ref[...] + (1.0 - ADAM_B2) * (g * g)
        og[...] = g
        om[...] = m_new
        ov[...] = v_new
        od[...] = -ADAM_LR * ((m_new * c1) / (jnp.sqrt(v_new * c2) + ADAM_EPS) + ADAM_WD * w_ref[...])

    def total(g_ref):
        g = g_ref[0].astype(F32)
        for p in range(1, P):
            g = g + g_ref[p].astype(F32)
        return g

    out_shape = [jax.ShapeDtypeStruct((R, C), F32)] * 4
    if own is None:
        def body(w_ref, m_ref, v_ref, g_ref, og, od, om, ov):
            update(w_ref, m_ref, v_ref, total(g_ref), og, od, om, ov)

        spec = pl.BlockSpec((tr, C), lambda i: (i, 0))
        return pl.pallas_call(
            body, name=name, grid=(R // tr,),
            in_specs=[spec, spec, spec, pl.BlockSpec((P, tr, C), lambda i: (0, i, 0))],
            out_specs=[spec] * 4, out_shape=out_shape, compiler_params=_params(1),
        )(w, m, v, g_parts)

    slots, me = own

    def body_own(me_ref, w_ref, m_ref, v_ref, g_ref, own_ref, og, od, om, ov):
        g = own_ref[...].astype(F32)
        for p in range(P):
            g = g + jnp.where(me_ref[0] == p, 0.0, g_ref[p].astype(F32))
        update(w_ref, m_ref, v_ref, g, og, od, om, ov)

    spec = pl.BlockSpec((tr, C), lambda i, me_ref: (i, 0))
    grid_spec = pltpu.PrefetchScalarGridSpec(
        num_scalar_prefetch=1, grid=(R // tr,),
        in_specs=[spec, spec, spec, pl.BlockSpec((P, tr, C), lambda i, me_ref: (0, i, 0)),
                  pl.BlockSpec((None, tr, C), lambda i, me_ref: (me_ref[0], i, 0))],
        out_specs=[spec] * 4)
    return pl.pallas_call(body_own, name=name, grid_spec=grid_spec, out_shape=out_shape,
                          compiler_params=_params(1))(me, w, m, v, g_parts, slots)


def _adamw_small(ws, ms, vs, gs):
    n = len(ws)
    c1 = 1.0 / (1.0 - ADAM_B1 ** ADAM_STEP)
    c2 = 1.0 / (1.0 - ADAM_B2 ** ADAM_STEP)

    def body(*refs):
        ins, outs = refs[:4 * n], refs[4 * n:]
        for i in range(n):
            w, m, v, g = ins[i][...], ins[n + i][...], ins[2 * n + i][...], ins[3 * n + i][...]
            m_new = ADAM_B1 * m + (1.0 - ADAM_B1) * g
            v_new = ADAM_B2 * v + (1.0 - ADAM_B2) * (g * g)
            outs[4 * i][...] = g
            outs[4 * i + 1][...] = -ADAM_LR * ((m_new * c1) / (jnp.sqrt(v_new * c2) + ADAM_EPS) + ADAM_WD * w)
            outs[4 * i + 2][...] = m_new
            outs[4 * i + 3][...] = v_new

    out_shape = [jax.ShapeDtypeStruct(w.shape, F32) for w in ws for _ in range(4)]
    return pl.pallas_call(body, name="adamw_small", out_shape=out_shape,
                          compiler_params=pltpu.CompilerParams(vmem_limit_bytes=V7X_VMEM_LIMIT))(*ws, *ms, *vs, *gs)


def _sum_parts(parts, loss_rows):
    P, R, C = parts.shape
    lo, hi = loss_rows

    def body(p_ref, o_ref, loss_ref):
        t = p_ref[0]
        for p in range(1, P):
            t = t + p_ref[p]
        o_ref[...] = t
        tot = jnp.sum(jnp.sum(o_ref[lo:hi, :], axis=1, keepdims=True), axis=0, keepdims=True)
        loss_ref[...] = jnp.broadcast_to(tot, loss_ref.shape)

    return pl.pallas_call(body, name="sum_small_grads",
                          out_shape=[jax.ShapeDtypeStruct((R, C), F32), jax.ShapeDtypeStruct((1, LANES), F32)],
                          compiler_params=pltpu.CompilerParams(vmem_limit_bytes=V7X_VMEM_LIMIT))(parts)


def _exchange(items, name):
    n = len(items)
    MESH = pl.DeviceIdType.MESH

    def body(*refs):
        src, dst = refs[:n], refs[n:2 * n]
        send_sems, recv_sems, local_sems = refs[2 * n:]
        x, y, c = lax.axis_index("x"), lax.axis_index("y"), lax.axis_index("c")
        me = 4 * x + 2 * y + c
        started = []
        for it, (_, per_peer) in enumerate(items):
            own = pltpu.make_async_copy(src[it].at[me] if per_peer else src[it], dst[it].at[me], local_sems.at[it])
            own.start()
            started.append(own)
        sends, recvs = [], []
        for k in range(1, N_DEV):
            px = 1 - x if k & 4 else x
            py = 1 - y if k & 2 else y
            pc = 1 - c if k & 1 else c
            peer = 4 * px + 2 * py + pc
            for it, (_, per_peer) in enumerate(items):
                s = src[it].at[peer] if per_peer else src[it]
                cp = pltpu.make_async_remote_copy(src_ref=s, dst_ref=dst[it].at[me], send_sem=send_sems.at[it, k - 1],
                                                  recv_sem=recv_sems.at[it, k - 1], device_id=(px, py, pc),
                                                  device_id_type=MESH)
                cp.start()
                sends.append(cp)
                recvs.append(pltpu.make_async_remote_copy(
                    src_ref=s, dst_ref=dst[it].at[peer], send_sem=send_sems.at[it, k - 1],
                    recv_sem=recv_sems.at[it, k - 1], device_id=(px, py, pc), device_id_type=MESH))
        for cp in recvs:
            cp.wait_recv()
        for cp in sends:
            cp.wait_send()
        for cp in started:
            cp.wait()

    any_spec = pl.BlockSpec(memory_space=pl.ANY)
    out_shape = []
    for a, per_peer in items:
        shp = a.shape if per_peer else (N_DEV,) + a.shape
        out_shape.append(jax.ShapeDtypeStruct(shp, a.dtype))
    return pl.pallas_call(
        body, name=name, in_specs=[any_spec] * n, out_specs=[any_spec] * n, out_shape=out_shape,
        scratch_shapes=[pltpu.SemaphoreType.DMA((n, N_DEV - 1)), pltpu.SemaphoreType.DMA((n, N_DEV - 1)),
                        pltpu.SemaphoreType.DMA((n,))],
    )(*[a for a, _ in items])


def _remote(src, dst, send_sem, recv_sem, device):
    return pltpu.make_async_remote_copy(src_ref=src, dst_ref=dst, send_sem=send_sem, recv_sem=recv_sem,
                                        device_id=device, device_id_type=pl.DeviceIdType.MESH)


def _mesh_place():
    x, y, c = lax.axis_index("x"), lax.axis_index("y"), lax.axis_index("c")
    other_chips = [(1 - x, y), (x, 1 - y), (1 - x, 1 - y)]
    return x, y, c, (x, y, 1 - c), other_chips


def _gather_all(items, name):
    n = len(items)

    def body(*refs):
        src, dst = refs[:n], refs[n:2 * n]
        send_sems, recv_sems, local_sems = refs[2 * n:]
        x, y, c, sibling, chips = _mesh_place()
        slot = lambda px, py, pc: 4 * px + 2 * py + pc
        me = slot(x, y, c)
        own = [pltpu.make_async_copy(src[it], dst[it].at[me], local_sems.at[it]) for it in range(n)]
        first = []
        for it in range(n):
            first.append(_remote(src[it], dst[it].at[me], send_sems.at[it, 0], recv_sems.at[it, 0], sibling))
            for j, chip in enumerate(chips):
                first.append(_remote(src[it], dst[it].at[me], send_sems.at[it, 1 + j], recv_sems.at[it, 1 + j],
                                     (*chip, c)))
        for cp in own + first:
            cp.start()
        passed = []
        for j, chip in enumerate(chips):
            blk = slot(*chip, c)
            for it in range(n):
                _remote(src[it], dst[it].at[blk], send_sems.at[it, 1 + j], recv_sems.at[it, 1 + j],
                        (*chip, c)).wait_recv()
                fwd = _remote(dst[it].at[blk], dst[it].at[blk], send_sems.at[it, 4 + j], recv_sems.at[it, 4 + j],
                              sibling)
                fwd.start()
                passed.append(fwd)
        for it in range(n):
            _remote(src[it], dst[it].at[slot(x, y, 1 - c)], send_sems.at[it, 0], recv_sems.at[it, 0],
                    sibling).wait_recv()
        for j, chip in enumerate(chips):
            for it in range(n):
                _remote(src[it], dst[it].at[slot(*chip, 1 - c)], send_sems.at[it, 4 + j], recv_sems.at[it, 4 + j],
                        sibling).wait_recv()
        for cp in first + passed:
            cp.wait_send()
        for cp in own:
            cp.wait()

    any_spec = pl.BlockSpec(memory_space=pl.ANY)
    return pl.pallas_call(
        body, name=name, in_specs=[any_spec] * n, out_specs=[any_spec] * n,
        out_shape=[jax.ShapeDtypeStruct((N_DEV,) + a.shape, a.dtype) for a in items],
        scratch_shapes=[pltpu.SemaphoreType.DMA((n, 7)), pltpu.SemaphoreType.DMA((n, 7)),
                        pltpu.SemaphoreType.DMA((n,))],
    )(*items)


def _peers():
    x, y, c = lax.axis_index("x"), lax.axis_index("y"), lax.axis_index("c")
    out = []
    for k in range(1, N_DEV):
        px = 1 - x if k & 4 else x
        py = 1 - y if k & 2 else y
        pc = 1 - c if k & 1 else c
        out.append((k, (px, py, pc), 4 * px + 2 * py + pc))
    return 4 * x + 2 * y + c, out


def _exchange_start(items, name, gather, carry=()):
    n, m = len(items), len(carry)

    def body(*refs):
        src, land = refs[:n], refs[n:2 * n]
        first_out = 2 * n + m
        send_sems, recv_sems = refs[first_out:first_out + n], refs[first_out + n:first_out + 2 * n]
        token = refs[-1]
        me, peers = _peers()
        for k, peer, slot in peers:
            for it in range(n):
                _remote(src[it] if gather else src[it].at[slot], land[it].at[me], send_sems[it], recv_sems[it],
                        peer).start()
        token[...] = jnp.zeros_like(token)

    hbm = pl.BlockSpec(memory_space=pltpu.HBM)
    sem = pl.BlockSpec(memory_space=pltpu.SEMAPHORE)
    land_shapes = [(N_DEV,) + (a.shape if gather else a.shape[1:]) for a in items]
    lands = [lax.empty(shp, a.dtype) for shp, a in zip(land_shapes, items)]
    through = list(items) + lands + list(carry)
    outs = pl.pallas_call(
        body, name=name,
        out_shape=(*[pltpu.SemaphoreType.DMA(())] * (2 * n), *[pltpu.HBM(a.shape, a.dtype) for a in through],
                   jax.ShapeDtypeStruct((8, LANES), F32)),
        in_specs=[hbm] * len(through),
        out_specs=(*[sem] * (2 * n), *[hbm] * len(through), pl.BlockSpec(memory_space=pltpu.VMEM)),
        input_output_aliases={i: 2 * n + i for i in range(len(through))},
        compiler_params=pltpu.CompilerParams(has_side_effects=pltpu.SideEffectType.DATAFLOW_SIDE_EFFECTING),
    )(*[pltpu.with_memory_space_constraint(a, pltpu.HBM) for a in through])
    return (list(outs[:n]), list(outs[n:2 * n]), list(outs[2 * n:3 * n]), list(outs[3 * n:4 * n]), outs[-1],
            list(outs[4 * n:4 * n + m]))


def _exchange_wait(send_sems, recv_sems, items, lands, after, name):
    n = len(items)

    def body(*refs):
        land = refs[n:2 * n]
        send_sems, recv_sems = refs[2 * n:3 * n], refs[3 * n:4 * n]
        me, peers = _peers()
        for it in range(n):
            seven = land[it].at[pl.ds(0, N_DEV - 1)]
            cp = _remote(seven, seven, send_sems[it], recv_sems[it], peers[0][1])
            cp.wait_send()
            cp.wait_recv()

    hbm = pl.BlockSpec(memory_space=pltpu.HBM)
    sem = pl.BlockSpec(memory_space=pltpu.SEMAPHORE)
    outs = pl.pallas_call(
        body, name=name,
        out_shape=tuple(pltpu.HBM(a.shape, a.dtype) for a in list(items) + list(lands)),
        in_specs=[hbm] * (2 * n) + [sem] * (2 * n) + [pl.BlockSpec(memory_space=pl.ANY)],
        out_specs=tuple([hbm] * (2 * n)),
        input_output_aliases={i: i for i in range(2 * n)},
        compiler_params=pltpu.CompilerParams(has_side_effects=pltpu.SideEffectType.DATAFLOW_SIDE_EFFECTING),
    )(*items, *lands, *send_sems, *recv_sems, after)
    return list(outs[:n]), list(outs[n:])


def _gelu_tanh(y):
    k = math.sqrt(2.0 / math.pi)
    t = jnp.tanh(k * (y + 0.044715 * y * y * y))
    return 0.5 * y * (1.0 + t), t


def _local_step(x, mod, target, W, late_weights, P, send_early):
    B, S, D = x.shape
    T = B * S
    TS = 512
    flat = lambda a: a.reshape(T, a.shape[-1])
    unflat = lambda a: a.reshape(B, S, a.shape[-1])
    mod_col = lambda i: (mod, D, i)

    def f_modnorm(xv, sc, sh, g):
        return (xv * _rms_scale(xv) * g) * (1.0 + sc) + sh

    (u1,) = _rowwise(f_modnorm, [(x, D, 0)], [mod_col(1), mod_col(0)], [P["g_mix"]],
                     [(D, BF16)], [], [], ts=TS, name="modnorm_mix")
    u1f = flat(u1)
    qkv = unflat(_matmul(u1f, W["w_qkv"], name="proj_qkv"))
    us = unflat(_matmul(u1f, W["w_us"], name="proj_ssm_in"))
    gates = unflat(_matmul(u1f, W["w_gates"], out_dtype=BF16, name="proj_gates"))

    o_att, lse = _attention_fwd(qkv, P["slopes"])
    more_w, more_p = late_weights(o_att)
    W, P = {**W, **more_w}, {**P, **more_p}
    y_att = unflat(_matmul(flat(o_att), W["w_proj_att"], out_dtype=BF16, name="proj_att"))

    xs, y_mm = _scan_fwd(us, P["bb_big"], P["a_row"], P["cc_big"])

    def f_glu(ymm, usv, dsk, wg, bg):
        yv = ymm + dsk * usv
        ge, _ = _gelu_tanh(yv)
        pre = jnp.dot(ge.astype(BF16), wg, preferred_element_type=F32) + bg
        return yv, ge * _sigmoid(pre)

    y_s5, z = _rowwise(f_glu, [(y_mm, SSM_WIDTH, 0), (us, SSM_WIDTH, 0)], [], [P["d_skip"], W["w_glu"], P["b_glu"]],
                       [(SSM_WIDTH, F32), (SSM_WIDTH, BF16)], [], [], ts=TS, name="s5_glu")
    y_ssm = unflat(_matmul(flat(z), W["w_proj_ssm"], out_dtype=BF16, name="proj_ssm"))

    def f_merge(ga, gs, ya, ys, bga, bgs):
        return _sigmoid(ga + bga) * ya + _sigmoid(gs + bgs) * ys

    bga, bgs = P["b_gate"][:, :D], P["b_gate"][:, D:]
    (merged,) = _rowwise(f_merge, [(gates, D, 0), (gates, D, 1), (y_att, D, 0), (y_ssm, D, 0)], [], [bga, bgs],
                         [(D, BF16)], [], [], ts=TS, name="gate_merge")
    mix = unflat(_matmul(flat(merged), W["w_out"], out_dtype=BF16, name="proj_out"))

    def f_res_modnorm(xv, mx, gt, sc, sh, g):
        h = xv + gt * mx
        return h, (h * _rms_scale(h) * g) * (1.0 + sc) + sh

    h1, u2 = _rowwise(f_res_modnorm, [(x, D, 0), (mix, D, 0)], [mod_col(2), mod_col(4), mod_col(3)], [P["g_ffn"]],
                      [(D, F32), (D, BF16)], [], [], ts=TS, name="residual_modnorm_ffn")
    up = unflat(_up_fwd(flat(u2), W["w_up"], name="ffn_up"))
    ff = _conv_fwd(up, P["w_conv"], P["b_conv"])
    down = unflat(_matmul(flat(ff), W["w_down"], out_dtype=BF16, name="ffn_down"))

    def f_head(h1v, dn, tg, gt, g):
        h2 = h1v + gt * dn
        r = _rms_scale(h2)
        nh = h2 * r
        e = nh * g - tg
        dy = e * (1.0 / D)
        gy = dy * g
        dh = r * (gy - nh * jnp.mean(gy * nh, axis=-1, keepdims=True))
        return (dh, dh * gt, _col_sum(dh * dn), _col_sum(dy * nh), _col_sum(e * e) * (0.5 / D))

    dh2, d_down, d_gt2, d_g_final, loss_cols = _rowwise(
        f_head, [(h1, D, 0), (down, D, 0), (target, D, 0)], [mod_col(5)], [P["g_final"]],
        [(D, BF16), (D, BF16)], [D], [(1, D), (1, D)], ts=TS, name="head_loss")

    d_downf = flat(d_down)
    d_ff = unflat(_matmul(d_downf, W["w_down"], tb=True, out_dtype=BF16, name="ffn_down_dx"))
    d_w_down = _matmul(flat(ff), d_downf, ta=True, out_dtype=BF16, name="ffn_down_dw")
    d_up, d_w_conv, d_b_conv = _conv_bwd(up, d_ff, P["w_conv"], P["b_conv"])
    d_upf = flat(d_up)
    d_u2 = unflat(_up_dx(d_upf, W["w_up"], name="ffn_up_dx"))
    d_w_up = _up_dw(flat(u2), d_upf, name="ffn_up_dw")
    token, _ = send_early(dict(w_down=d_w_down.reshape(N_DEV, D_FF // N_DEV, D), w_up=d_w_up))
    g_ffn_after = P["g_ffn"] + token[0:1, 0:1]

    def f_modnorm_bwd(du, h, dres, mx, sc, gt, g):
        r = _rms_scale(h)
        nh = h * r
        dn = du * (1.0 + sc)
        gy = dn * g
        dh = dres + r * (gy - nh * jnp.mean(gy * nh, axis=-1, keepdims=True))
        return (dh, dh * gt, _col_sum(du), _col_sum(du * nh * g), _col_sum(dh * mx), _col_sum(dn * nh))

    dh1, d_mix, d_sh2, d_sc2, d_gt1, d_g_ffn = _rowwise(
        f_modnorm_bwd, [(d_u2, D, 0), (h1, D, 0), (dh2, D, 0), (mix, D, 0)], [mod_col(4), mod_col(2)], [g_ffn_after],
        [(D, BF16), (D, BF16)], [D, D, D], [(1, D)], ts=TS, name="modnorm_ffn_bwd")

    d_mixf = flat(d_mix)
    d_merged = unflat(_matmul(d_mixf, W["w_out"], tb=True, out_dtype=BF16, name="proj_out_dx"))
    d_w_out = _matmul(flat(merged), d_mixf, ta=True, out_dtype=BF16, name="proj_out_dw")

    def f_merge_bwd(dm, ga, gs, ya, ys, bga_, bgs_):
        sa, ss = _sigmoid(ga + bga_), _sigmoid(gs + bgs_)
        dga = dm * ya * sa * (1.0 - sa)
        dgs = dm * ys * ss * (1.0 - ss)
        return dm * sa, dm * ss, jnp.concatenate([dga, dgs], axis=1), _col_sum(dga), _col_sum(dgs)

    d_y_att, d_y_ssm, d_gates, d_bga, d_bgs = _rowwise(
        f_merge_bwd, [(d_merged, D, 0), (gates, D, 0), (gates, D, 1), (y_att, D, 0), (y_ssm, D, 0)], [], [bga, bgs],
        [(D, BF16), (D, BF16), (2 * D, BF16)], [], [(1, D), (1, D)], ts=TS, name="gate_merge_bwd")

    d_yaf, d_ysf = flat(d_y_att), flat(d_y_ssm)
    d_o_att = unflat(_matmul(d_yaf, W["w_proj_att"], tb=True, name="proj_att_dx"))
    d_w_proj_att = _matmul(flat(o_att), d_yaf, ta=True, out_dtype=BF16, name="proj_att_dw")
    d_z = unflat(_matmul(d_ysf, W["w_proj_ssm"], tb=True, out_dtype=BF16, name="proj_ssm_dx"))
    d_w_proj_ssm = _matmul(flat(z), d_ysf, ta=True, out_dtype=BF16, name="proj_ssm_dw")

    def f_glu_bwd(yv, dz, usv, dsk, wg, bg):
        ge, t = _gelu_tanh(yv)
        pre = jnp.dot(ge.astype(BF16), wg, preferred_element_type=F32) + bg
        sg = _sigmoid(pre)
        dpre = dz * ge * sg * (1.0 - sg)
        dge = dz * sg + lax.dot_general(dpre.astype(BF16), wg, _NT, preferred_element_type=F32)
        k = math.sqrt(2.0 / math.pi)
        dgelu = 0.5 * (1.0 + t) + 0.5 * yv * (1.0 - t * t) * k * (1.0 + 3.0 * 0.044715 * yv * yv)
        dy = dge * dgelu
        dwg = lax.dot_general(ge.astype(BF16), dpre.astype(BF16), _TN, preferred_element_type=F32)
        return dy, dy * dsk, dwg, _col_sum(dpre), _col_sum(dy * usv)

    d_y_s5, d_us_skip, d_w_glu, d_b_glu, d_d_skip = _rowwise(
        f_glu_bwd, [(y_s5, SSM_WIDTH, 0), (d_z, SSM_WIDTH, 0), (us, SSM_WIDTH, 0)], [],
        [P["d_skip"], W["w_glu"], P["b_glu"]],
        [(SSM_WIDTH, BF16), (SSM_WIDTH, F32)], [], [(SSM_WIDTH, SSM_WIDTH), (1, SSM_WIDTH), (1, SSM_WIDTH)],
        ts=TS, name="s5_glu_bwd")
    d_us_parts, g_ab, d_bb, d_cc = _scan_bwd(d_y_s5, us, P["bb_big"], P["cc_big"], xs, P["a_row"])

    token, _ = send_early(dict(
        w_out=d_w_out.reshape(N_DEV, D // N_DEV, D), w_proj_att=_cols_to_slots(d_w_proj_att),
        w_proj_ssm=_cols_to_slots(d_w_proj_ssm),
        w_glu=d_w_glu.astype(BF16).reshape(N_DEV, SSM_WIDTH // N_DEV, SSM_WIDTH),
        w_conv=_cols_to_slots(d_w_conv.astype(BF16))))
    d_qkv = _attention_bwd(qkv, o_att, d_o_att, lse, P["slopes"] + token[0, 0])

    def f_add(*parts):
        return sum(parts[1:], parts[0])

    n_parts = d_us_parts.shape[0]
    stacked = d_us_parts.reshape(n_parts * B, S, SSM_WIDTH)
    (d_us,) = _rowwise(f_add, [(d_us_skip, SSM_WIDTH, 0)] + [(stacked, SSM_WIDTH, 0, j * B) for j in range(n_parts)],
                       [], [],
                       [(SSM_WIDTH, BF16)], [], [], ts=TS, name="s5_input_grad")
    d_qkvf = flat(d_qkv)
    d_usf = flat(d_us)
    d_gatesf = flat(d_gates)
    d_w_in = jnp.concatenate(
        [_unpair_qkv_columns(_matmul(u1f, d_qkvf, ta=True, out_dtype=BF16, name="proj_qkv_dw")),
         _matmul(u1f, d_usf, ta=True, out_dtype=BF16, name="proj_ssm_in_dw"),
         _matmul(u1f, d_gatesf, ta=True, out_dtype=BF16, name="proj_gates_dw")], axis=1)
    token, (w_qkv, w_us, w_gates) = send_early(dict(w_in=_cols_to_slots(d_w_in)),
                                               carry=[W["w_qkv"], W["w_us"], W["w_gates"]])
    d_u1 = (_matmul(d_qkvf, w_qkv, tb=True, out_dtype=BF16, name="proj_qkv_dx"),
            _matmul(d_usf, w_us, tb=True, out_dtype=BF16, name="proj_ssm_in_dx"),
            _matmul(d_gatesf, w_gates, tb=True, out_dtype=BF16, name="proj_gates_dx"))

    def f_modnorm_bwd_in(du0, du1, du2, h, dres, sc, g):
        du = du0 + du1 + du2
        r = _rms_scale(h)
        nh = h * r
        dn = du * (1.0 + sc)
        gy = dn * g
        dh = dres + r * (gy - nh * jnp.mean(gy * nh, axis=-1, keepdims=True))
        return (dh, _col_sum(du), _col_sum(du * nh * g), _col_sum(dn * nh))

    grad_x, d_sh1, d_sc1, d_g_mix = _rowwise(
        f_modnorm_bwd_in, [(unflat(d_u1[0]), D, 0), (unflat(d_u1[1]), D, 0), (unflat(d_u1[2]), D, 0), (x, D, 0),
                           (dh1, D, 0)], [mod_col(1)], [P["g_mix"] + token[0:1, 0:1]],
        [(D, F32)], [D, D], [(1, D)], ts=TS, name="modnorm_mix_bwd")

    d_mod = jnp.concatenate([d_sh1, d_sc1, d_gt1, d_sh2, d_sc2, d_gt2], axis=-1)
    g_ab_re, g_ab_im = _deinterleave(g_ab)
    d_bb_re, d_bb_im = _deinterleave(d_bb)
    d_cc_re, d_cc_im = (t.T for t in _deinterleave(d_cc.T))
    small = dict(g_mix=d_g_mix, b_gate=jnp.concatenate([d_bga, d_bgs], axis=1), g_ab_re=g_ab_re, g_ab_im=g_ab_im,
                 d_bb_re=d_bb_re, d_bb_im=d_bb_im, d_cc_re=d_cc_re, d_cc_im=d_cc_im, d_skip=d_d_skip,
                 b_glu=d_b_glu, g_ffn=d_g_ffn, b_conv=d_b_conv, g_final=d_g_final, loss_cols=loss_cols)
    return grad_x, d_mod, small


def _block_diag_in(bb):
    t = bb.reshape(SSM_GROUPS, SSM_STATE, SSM_GROUP_CH)
    eye = jnp.eye(SSM_GROUPS, dtype=bb.dtype)
    return jnp.einsum("gnc,gh->gchn", t, eye).reshape(SSM_WIDTH, SSM_COLS)


def _block_diag_out(cm):
    eye = jnp.eye(SSM_GROUPS, dtype=cm.dtype)
    return jnp.einsum("gcn,gh->gnhc", cm, eye).reshape(SSM_COLS, SSM_WIDTH)


def _diag_blocks_in(m):
    t = m.reshape(SSM_GROUPS, SSM_GROUP_CH, SSM_GROUPS, SSM_STATE)
    idx = jnp.arange(SSM_GROUPS)
    return t[idx, :, idx, :].transpose(0, 2, 1).reshape(SSM_COLS, SSM_GROUP_CH)


def _diag_blocks_out(m):
    t = m.reshape(SSM_GROUPS, SSM_STATE, SSM_GROUPS, SSM_GROUP_CH)
    idx = jnp.arange(SSM_GROUPS)
    return t[idx, :, idx, :].transpose(0, 2, 1)


def _pair_qkv_columns(w):
    lead = w.shape[:-1]
    return w.reshape(lead + (3, N_HEADS // 2, LANES)).swapaxes(-3, -2).reshape(lead + (3 * ATT_WIDTH,))


def _unpair_qkv_columns(w):
    lead = w.shape[:-1]
    return w.reshape(lead + (N_HEADS // 2, 3, LANES)).swapaxes(-3, -2).reshape(lead + (3 * ATT_WIDTH,))


def _interleave(re, im):
    lead = re.shape[:-1]
    g = lambda a: a.reshape(lead + (SSM_COLS // SCAN_COLS, 1, SCAN_COLS))
    return jnp.concatenate([g(re), g(im)], axis=-2).reshape(lead + (2 * SSM_COLS,))


def _deinterleave(x):
    lead = x.shape[:-1]
    t = x.reshape(lead + (SSM_COLS // SCAN_COLS, 2, SCAN_COLS))
    return t[..., 0, :].reshape(lead + (SSM_COLS,)), t[..., 1, :].reshape(lead + (SSM_COLS,))


def _cols_to_slots(g):
    R = g.shape[0]
    return g.reshape(R, N_DEV, g.shape[1] // N_DEV).transpose(1, 0, 2)


def _slots_to_cols(g):
    return g.transpose(1, 0, 2).reshape(g.shape[1], N_DEV * g.shape[2])


SMALL_ORDER = ("b_ada", "g_mix", "b_gate", "a_re", "a_im", "log_dt", "b_re", "b_im", "c_re", "c_im", "d_skip",
               "b_glu", "g_ffn", "b_conv", "g_final")


def _pack(arrs):
    pieces, offs, row = [], [], 0
    for a in arrs:
        f = a.reshape(-1).astype(F32)
        n = f.shape[0]
        rows = -(-n // LANES)
        pieces.append(jnp.pad(f, (0, rows * LANES - n)))
        offs.append((row, n))
        row += rows
    return jnp.concatenate(pieces).reshape(row, LANES), offs


def _unpack(packed, offs, shapes):
    flat = packed.reshape(-1)
    return [flat[r * LANES:r * LANES + n].reshape(s) for (r, n), s in zip(offs, shapes)]


def kernel(x, c, w_ada, b_ada, g_mix, w_in, b_gate, a_re, a_im, log_dt, b_re, b_im, c_re, c_im, d_skip, w_glu, b_glu, w_proj_att, w_proj_ssm, w_out, g_ffn, w_up, w_conv, b_conv, w_down, g_final, loss_target, m_w_ada, m_b_ada, m_g_mix, m_w_in, m_b_gate, m_a_re, m_a_im, m_log_dt, m_b_re, m_b_im, m_c_re, m_c_im, m_d_skip, m_w_glu, m_b_glu, m_w_proj_att, m_w_proj_ssm, m_w_out, m_g_ffn, m_w_up, m_w_conv, m_b_conv, m_w_down, m_g_final, v_w_ada, v_b_ada, v_g_mix, v_w_in, v_b_gate, v_a_re, v_a_im, v_log_dt, v_b_re, v_b_im, v_c_re, v_c_im, v_d_skip, v_w_glu, v_b_glu, v_w_proj_att, v_w_proj_ssm, v_w_out, v_g_ffn, v_w_up, v_w_conv, v_b_conv, v_w_down, v_g_final):
    args = dict(locals())
    B, S, D = x.shape
    me = 4 * lax.axis_index("x") + 2 * lax.axis_index("y") + lax.axis_index("c")
    bf = lambda w: w[0].astype(BF16)

    c_slots, w_in_slots = _gather_all([c, bf(w_in)], name="gather_first_weights")
    c_all = c_slots.reshape(N_DEV * B, D)
    w_in_full = _slots_to_cols(w_in_slots)
    n_qkv = 3 * ATT_WIDTH
    W = dict(w_qkv=_pair_qkv_columns(w_in_full[:, :n_qkv]), w_us=w_in_full[:, n_qkv:n_qkv + SSM_WIDTH],
             w_gates=w_in_full[:, n_qkv + SSM_WIDTH:])

    n_ada = w_ada.shape[2]
    b_ada_cols = lax.dynamic_slice(b_ada, (0, me * n_ada), (1, n_ada))
    mod_part = _ada_fwd(c_all, w_ada[0], b_ada_cols)
    (mod_slots,) = _exchange([(mod_part.reshape(N_DEV, B, n_ada), True)], name="scatter_modulation")
    mod = mod_slots.transpose(1, 0, 2).reshape(B, 1, 6 * D)

    later = [bf(w_glu), bf(w_proj_att), bf(w_proj_ssm), bf(w_out), bf(w_up), w_conv[0], bf(w_down)]
    later_sems = _exchange_start(later, "start_later_weights", gather=True, carry=[mod])
    (mod,) = later_sems[5]

    def late_weights(after):
        _, lands = _exchange_wait(*later_sems[:4], after, name="wait_later_weights")
        g = [lax.dynamic_update_index_in_dim(land, a, me, 0) for land, a in zip(lands, later)]
        more_w = dict(w_glu=g[0].reshape(SSM_WIDTH, SSM_WIDTH), w_proj_att=_slots_to_cols(g[1]),
                      w_proj_ssm=_slots_to_cols(g[2]), w_out=g[3].reshape(D, D), w_up=g[4],
                      w_down=g[6].reshape(D_FF, D))
        return more_w, dict(w_conv=_slots_to_cols(g[5]))

    ab_re, ab_im, f_re, f_im = _s5_params(a_re[0], a_im[0], log_dt[0].reshape(SSM_GROUPS, 1))
    col = lambda a: a.reshape(SSM_COLS, 1)
    b_re2, b_im2 = b_re[0].reshape(SSM_COLS, SSM_GROUP_CH), b_im[0].reshape(SSM_COLS, SSM_GROUP_CH)
    bb_re, bb_im = _s5_input_matrix(col(f_re), col(f_im), b_re2, b_im2)
    slopes = jnp.asarray([2.0 ** (-8.0 * (h + 1) / N_HEADS) for h in range(N_HEADS)], F32)
    P = dict(g_mix=g_mix, g_ffn=g_ffn, g_final=g_final.reshape(1, D), b_gate=b_gate, d_skip=d_skip, b_glu=b_glu,
             b_conv=b_conv, slopes=slopes,
             a_row=_interleave(ab_re.reshape(1, SSM_COLS), ab_im.reshape(1, SSM_COLS)),
             bb_big=_interleave(_block_diag_in(bb_re), _block_diag_in(bb_im)),
             cc_big=_interleave(_block_diag_out(c_re[0]).T, -_block_diag_out(c_im[0]).T).T)

    in_flight = []

    def send_early(grads, carry=()):
        names = list(grads)
        handles = _exchange_start([grads[n] for n in names], "start_gradients_%d" % len(in_flight), gather=False,
                                  carry=carry)
        in_flight.append((names,) + handles[:4])
        return handles[4], handles[5]

    grad_x, d_mod, small = _local_step(x, mod, loss_target, W, late_weights, P, send_early)

    small_list = [small["loss_cols"], small["g_mix"], small["b_gate"], small["g_ab_re"], small["g_ab_im"],
                  _diag_blocks_in(small["d_bb_re"]), _diag_blocks_in(small["d_bb_im"]),
                  _diag_blocks_out(small["d_cc_re"]), -_diag_blocks_out(small["d_cc_im"]),
                  small["g_ffn"], small["b_conv"], small["g_final"], small["d_skip"], small["b_glu"]]
    small_packed, small_offs = _pack(small_list)
    small_all, dmod_slots = _gather_all([small_packed, d_mod.reshape(B, 6 * D)], name="gather_small_gradients")

    out = {}

    def update(name, parts, own=None):
        w2 = args[name][0]
        g, dl, mn, vn = _adamw(w2, args["m_" + name][0], args["v_" + name][0], parts, name="adamw_" + name, own=own)
        for key, val in (("grad_", g), ("delta_", dl), ("new_m_", mn), ("new_v_", vn)):
            out[key + name] = val[None]

    my_slot = me.astype(jnp.int32).reshape(1)
    for i, (names, send_sems, recv_sems, sent, lands) in enumerate(in_flight):
        sent, lands = _exchange_wait(send_sems, recv_sems, sent, lands, dmod_slots, name="wait_gradients_%d" % i)
        for name, own_slots, landed in zip(names, sent, lands):
            update(name, landed, own=(own_slots, my_slot))

    dmod_all = dmod_slots.reshape(N_DEV * B, 6 * D)
    dmod_cols = lax.dynamic_slice(dmod_all, (0, me * n_ada), (N_DEV * B, n_ada))
    d_w_ada, d_b_ada = _ada_bwd(c_all, dmod_all, dmod_cols)
    update("w_ada", d_w_ada[None])

    loss_row, loss_n = small_offs[0]
    small_sum, loss_vec = _sum_parts(small_all, (loss_row, loss_row + loss_n // LANES))
    shapes = [(1, D), (1, D), (1, 2 * D), (SSM_GROUPS, SSM_STATE), (SSM_GROUPS, SSM_STATE), (SSM_COLS, SSM_GROUP_CH),
              (SSM_COLS, SSM_GROUP_CH), (1, SSM_GROUPS, SSM_GROUP_CH, SSM_STATE),
              (1, SSM_GROUPS, SSM_GROUP_CH, SSM_STATE), (1, D), (1, D_FF), (D,), (1, SSM_WIDTH), (1, SSM_WIDTH)]
    (_, s_g_mix, s_b_gate, s_ab_re, s_ab_im, s_bb_re, s_bb_im, s_c_re, s_c_im, s_g_ffn, s_b_conv, s_g_final,
     s_d_skip, s_b_glu) = _unpack(small_sum, small_offs, shapes)
    d_b_re2, d_b_im2, d_f_re, d_f_im = _s5_input_matrix_bwd(col(f_re), col(f_im), b_re2, b_im2, s_bb_re, s_bb_im)
    d_a_re, d_a_im, d_log_dt = _s5_params_bwd(a_re[0], a_im[0], log_dt[0].reshape(SSM_GROUPS, 1), s_ab_re, s_ab_im,
                                              d_f_re.reshape(SSM_GROUPS, SSM_STATE),
                                              d_f_im.reshape(SSM_GROUPS, SSM_STATE))
    grads_small = dict(b_ada=d_b_ada, g_mix=s_g_mix, b_gate=s_b_gate, a_re=d_a_re[None], a_im=d_a_im[None],
                       log_dt=d_log_dt.reshape(1, SSM_GROUPS), b_re=d_b_re2.reshape(b_re.shape),
                       b_im=d_b_im2.reshape(b_im.shape), c_re=s_c_re, c_im=s_c_im, d_skip=s_d_skip, b_glu=s_b_glu,
                       g_ffn=s_g_ffn, b_conv=s_b_conv, g_final=s_g_final)
    flat2 = lambda a: a.reshape(-1, a.shape[-1])
    res = _adamw_small([flat2(args[n]) for n in SMALL_ORDER], [flat2(args["m_" + n]) for n in SMALL_ORDER],
                       [flat2(args["v_" + n]) for n in SMALL_ORDER],
                       [flat2(grads_small[n].reshape(args[n].shape)) for n in SMALL_ORDER])
    for i, n in enumerate(SMALL_ORDER):
        for k, key in enumerate(("grad_", "delta_", "new_m_", "new_v_")):
            out[key + n] = res[4 * i + k].reshape(args[n].shape)

    order = ["w_ada", "b_ada", "g_mix", "w_in", "b_gate", "a_re", "a_im", "log_dt", "b_re", "b_im", "c_re", "c_im",
             "d_skip", "w_glu", "b_glu", "w_proj_att", "w_proj_ssm", "w_out", "g_ffn", "w_up", "w_conv", "b_conv",
             "w_down", "g_final"]
    loss = loss_vec[0, 0]
    return (loss, grad_x, *[out[k + n] for k in ("grad_", "delta_", "new_m_", "new_v_") for n in order])
```

```python
import math

import jax
import jax.numpy as jnp
from jax import lax
from jax.experimental import pallas as pl
from jax.experimental.pallas import tpu as pltpu

F32 = jnp.float32
BF16 = jnp.bfloat16

N_DEV = 8
D_MODEL = 1024
N_HEADS = 8
HEAD_DIM = 64
ATT_WIDTH = N_HEADS * HEAD_DIM
DILATIONS = (1, 4, 16)
WIN = 128
SSM_GROUPS = 16
SSM_GROUP_CH = 16
SSM_WIDTH = SSM_GROUPS * SSM_GROUP_CH
SSM_STATE = 64
SSM_COLS = SSM_GROUPS * SSM_STATE
D_FF = 2048
EPS = 1e-6
NEG_INF = -1e30
ADAM_LR, ADAM_B1, ADAM_B2, ADAM_EPS, ADAM_WD, ADAM_STEP = 0.001, 0.9, 0.999, 1e-08, 0.01, 10

V7X_VMEM_LIMIT = 56 * 1024 * 1024
LANES = 128


def _params(n_grid):
    return pltpu.CompilerParams(dimension_semantics=("arbitrary",) * n_grid,
                                vmem_limit_bytes=V7X_VMEM_LIMIT)


def _tile(n, pref):
    if n <= pref:
        return n
    t = (pref // LANES) * LANES
    while t > 0:
        if n % t == 0:
            return t
        t -= LANES
    return n


def _matmul(a, b, *, ta=False, tb=False, out_dtype=F32, name):
    if ta:
        K, M = a.shape
    else:
        M, K = a.shape
    if tb:
        N, K2 = b.shape
    else:
        K2, N = b.shape
    assert K == K2, (a.shape, b.shape)
    if ta:
        tm, tn, tk = _tile(M, 1024), _tile(N, 2048), _tile(K, 1024)
    else:
        tm, tk = _tile(M, 512), _tile(K, 4096)
        tn = _tile(N, 2048 if K <= 2048 else 1024)
    nk = K // tk
    dn = (((0,) if ta else (1,), (1,) if tb else (0,)), ((), ()))

    def body(a_ref, b_ref, o_ref, acc_ref):
        k = pl.program_id(2)
        part = lax.dot_general(a_ref[...].astype(BF16), b_ref[...].astype(BF16), dn, preferred_element_type=F32)
        if nk == 1:
            o_ref[...] = part.astype(o_ref.dtype)
            return

        @pl.when(k == 0)
        def _():
            acc_ref[...] = jnp.zeros_like(acc_ref)

        acc_ref[...] += part

        @pl.when(k == nk - 1)
        def _():
            o_ref[...] = acc_ref[...].astype(o_ref.dtype)

    a_spec = (pl.BlockSpec((tk, tm), lambda j, i, k: (k, i)) if ta
              else pl.BlockSpec((tm, tk), lambda j, i, k: (i, k)))
    b_spec = (pl.BlockSpec((tn, tk), lambda j, i, k: (j, k)) if tb
              else pl.BlockSpec((tk, tn), lambda j, i, k: (k, j)))
    return pl.pallas_call(
        body, name=name, grid=(N // tn, M // tm, nk),
        in_specs=[a_spec, b_spec],
        out_specs=pl.BlockSpec((tm, tn), lambda j, i, k: (i, j)),
        out_shape=jax.ShapeDtypeStruct((M, N), out_dtype),
        scratch_shapes=[pltpu.VMEM((tm, tn) if nk > 1 else (8, LANES), F32)],
        compiler_params=_params(3),
    )(a, b)


def _project_in(a, weights, out_dtypes, name):
    M, K = a.shape
    tm = _tile(M, 512)
    n = len(weights)

    def body(a_ref, *refs):
        av = a_ref[...].astype(BF16)
        for w_ref, o_ref in zip(refs[:n], refs[n:]):
            o_ref[...] = jnp.dot(av, w_ref[...].astype(BF16), preferred_element_type=F32).astype(o_ref.dtype)

    return pl.pallas_call(
        body, name=name, grid=(M // tm,),
        in_specs=[pl.BlockSpec((tm, K), lambda i: (i, 0))] + [pl.BlockSpec(w.shape, lambda i: (0, 0)) for w in weights],
        out_specs=[pl.BlockSpec((tm, w.shape[1]), lambda i: (i, 0)) for w in weights],
        out_shape=[jax.ShapeDtypeStruct((M, w.shape[1]), dt) for w, dt in zip(weights, out_dtypes)],
        compiler_params=_params(1),
    )(a, *weights)


def _project_back(ds, weights, name):
    M = ds[0].shape[0]
    K = weights[0].shape[0]
    tm = _tile(M, 512)
    n = len(ds)

    def body(*refs):
        total = None
        for d_ref, w_ref in zip(refs[:n], refs[n:2 * n]):
            part = lax.dot_general(d_ref[...].astype(BF16), w_ref[...].astype(BF16), _NT, preferred_element_type=F32)
            total = part if total is None else total + part
        refs[2 * n][...] = total.astype(refs[2 * n].dtype)

    return pl.pallas_call(
        body, name=name, grid=(M // tm,),
        in_specs=[pl.BlockSpec((tm, d.shape[1]), lambda i: (i, 0)) for d in ds]
        + [pl.BlockSpec(w.shape, lambda i: (0, 0)) for w in weights],
        out_specs=pl.BlockSpec((tm, K), lambda i: (i, 0)),
        out_shape=jax.ShapeDtypeStruct((M, K), BF16), compiler_params=_params(1),
    )(*ds, *weights)


HALF = 256
UP_SLOTS = N_DEV // 2
UP_GROUP = 4 * HALF


def _group_weight(w_ref):
    return jnp.concatenate([w_ref[0, :, :HALF], w_ref[1, :, :HALF], w_ref[0, :, HALF:], w_ref[1, :, HALF:]], axis=1)


def _up_weight_spec(K, index):
    return pl.BlockSpec((2, None, K, 2 * HALF), index)


def _up_fwd(a, w3, name):
    M, K = a.shape
    tm = _tile(M, 1024)

    def body(a_ref, w_ref, o_ref):
        o_ref[...] = jnp.dot(a_ref[...].astype(BF16), _group_weight(w_ref),
                             preferred_element_type=F32).astype(o_ref.dtype)

    return pl.pallas_call(
        body, name=name, grid=(UP_SLOTS, M // tm),
        in_specs=[pl.BlockSpec((tm, K), lambda j, i: (i, 0)), _up_weight_spec(K, lambda j, i: (0, j, 0, 0))],
        out_specs=pl.BlockSpec((tm, UP_GROUP), lambda j, i: (i, j)),
        out_shape=jax.ShapeDtypeStruct((M, UP_SLOTS * UP_GROUP), BF16), compiler_params=_params(2),
    )(a, w3.reshape(2, UP_SLOTS, K, 2 * HALF))


def _up_dx(d, w3, name):
    M = d.shape[0]
    K = w3.shape[1]
    tm = _tile(M, 1024)

    def body(d_ref, w_ref, o_ref, acc_ref):
        j = pl.program_id(1)

        @pl.when(j == 0)
        def _():
            acc_ref[...] = jnp.zeros_like(acc_ref)

        acc_ref[...] += lax.dot_general(d_ref[...], _group_weight(w_ref), _NT, preferred_element_type=F32)

        @pl.when(j == UP_SLOTS - 1)
        def _():
            o_ref[...] = acc_ref[...].astype(o_ref.dtype)

    return pl.pallas_call(
        body, name=name, grid=(M // tm, UP_SLOTS),
        in_specs=[pl.BlockSpec((tm, UP_GROUP), lambda i, j: (i, j)), _up_weight_spec(K, lambda i, j: (0, j, 0, 0))],
        out_specs=pl.BlockSpec((tm, K), lambda i, j: (i, 0)),
        out_shape=jax.ShapeDtypeStruct((M, K), BF16), scratch_shapes=[pltpu.VMEM((tm, K), F32)],
        compiler_params=_params(2),
    )(d, w3.reshape(2, UP_SLOTS, K, 2 * HALF))


def _up_dw(a, d, name):
    M, K = a.shape
    tk = _tile(M, 1024)
    nk = M // tk

    def body(a_ref, d_ref, o_ref, acc_ref):
        k = pl.program_id(1)

        @pl.when(k == 0)
        def _():
            acc_ref[...] = jnp.zeros_like(acc_ref)

        acc_ref[...] += lax.dot_general(a_ref[...], d_ref[...], _TN, preferred_element_type=F32)

        @pl.when(k == nk - 1)
        def _():
            for half in range(2):
                for part in range(2):
                    lo = (2 * half + part) * HALF
                    o_ref[part, :, half * HALF:(half + 1) * HALF] = acc_ref[:, lo:lo + HALF].astype(o_ref.dtype)

    out = pl.pallas_call(
        body, name=name, grid=(UP_SLOTS, nk),
        in_specs=[pl.BlockSpec((tk, K), lambda j, k: (k, 0)), pl.BlockSpec((tk, UP_GROUP), lambda j, k: (k, j))],
        out_specs=_up_weight_spec(K, lambda j, k: (0, j, 0, 0)),
        out_shape=jax.ShapeDtypeStruct((2, UP_SLOTS, K, 2 * HALF), BF16),
        scratch_shapes=[pltpu.VMEM((K, UP_GROUP), F32)], compiler_params=_params(2),
    )(a, d)
    return out.reshape(N_DEV, K, 2 * HALF)


def _rowwise(fn, rows, bvecs, consts, out_rows, out_b, out_g, *, ts, name):
    B, S = rows[0][0].shape[:2]
    nin = len(rows) + len(bvecs) + len(consts)
    nr, nb, ng = len(out_rows), len(out_b), len(out_g)

    def body(*refs):
        b = pl.program_id(0)
        s = pl.program_id(1)
        vals = [r[...] for r in refs[:nin]]
        vals[:len(rows)] = [v.astype(F32) for v in vals[:len(rows)]]
        outs = fn(*vals)
        if not isinstance(outs, (tuple, list)):
            outs = (outs,)
        orefs = refs[nin:]
        for i in range(nr):
            orefs[i][...] = outs[i].astype(orefs[i].dtype)
        for i in range(nb):
            ref = orefs[nr + i]

            @pl.when(s == 0)
            def _(ref=ref):
                ref[...] = jnp.zeros_like(ref)

            ref[...] += outs[nr + i]
        for i in range(ng):
            ref = orefs[nr + nb + i]

            @pl.when((s == 0) & (b == 0))
            def _(ref=ref):
                ref[...] = jnp.zeros_like(ref)

            ref[...] += outs[nr + nb + i]

    rows = [r if len(r) == 4 else r + (0,) for r in rows]
    in_specs = ([pl.BlockSpec((None, ts, cb), lambda b, s, ci=ci, b0=b0: (b0 + b, s, ci)) for (_, cb, ci, b0) in rows]
                + [pl.BlockSpec((None, 1, cb), lambda b, s, ci=ci: (b, 0, ci)) for (_, cb, ci) in bvecs]
                + [pl.BlockSpec(a.shape, lambda b, s: (0, 0)) for a in consts])
    out_shape = ([jax.ShapeDtypeStruct((B, S, c), dt) for (c, dt) in out_rows]
                 + [jax.ShapeDtypeStruct((B, 1, c), F32) for c in out_b]
                 + [jax.ShapeDtypeStruct(rc, F32) for rc in out_g])
    out_specs = ([pl.BlockSpec((None, ts, c), lambda b, s: (b, s, 0)) for (c, _) in out_rows]
                 + [pl.BlockSpec((None, 1, c), lambda b, s: (b, 0, 0)) for c in out_b]
                 + [pl.BlockSpec(rc, lambda b, s: (0, 0)) for rc in out_g])
    args = [r[0] for r in rows] + [a for (a, _, _) in bvecs] + list(consts)
    return pl.pallas_call(
        body, name=name, grid=(B, S // ts), in_specs=in_specs, out_specs=out_specs,
        out_shape=out_shape, compiler_params=_params(2),
    )(*args)


def _col_sum(v):
    return jnp.sum(v, axis=0, keepdims=True)


def _rms_scale(h):
    return lax.rsqrt(jnp.mean(h * h, axis=-1, keepdims=True) + EPS)


def _sigmoid(v):
    return 0.5 * (1.0 + jnp.tanh(0.5 * v))


ATT_SCALE = HEAD_DIM ** -0.5
COPY_ROWS = 256
_NT = (((1,), (1,)), ((), ()))
_TN = (((0,), (0,)), ((), ()))


def _row_chunks(d, seq):
    sub = seq // d
    out = []
    for r in range(d):
        for c0 in range(0, sub, COPY_ROWS):
            n = min(COPY_ROWS, sub - c0)
            out.append((pl.ds(r + c0 * d, n, stride=d), r * sub + c0, n))
    return out


ATT_UNROLL = 8
KEYS = 2 * WIN


def _zero_once(refs):
    @pl.when((pl.program_id(0) == 0) & (pl.program_id(1) == 0))
    def _():
        for r in refs:
            r[...] = jnp.zeros_like(r)


def _pair_bias(bias_ref, slopes_ref, hp, d, key_major):
    shape = (KEYS, WIN) if key_major else (WIN, KEYS)
    qi = lax.broadcasted_iota(jnp.int32, shape, 1 if key_major else 0)
    kj = lax.broadcasted_iota(jnp.int32, shape, 0 if key_major else 1)
    dist = WIN + qi - kj
    valid = (dist >= 0) & (dist <= WIN)
    distf = dist.astype(F32)
    for h in range(2):
        slope_d = slopes_ref[2 * hp + h] * float(d)
        with_prev = jnp.where(valid, -(slope_d * distf), NEG_INF)
        no_prev = jnp.where(kj >= WIN, with_prev, NEG_INF)
        span = slice(h * KEYS, (h + 1) * KEYS)
        if key_major:
            bias_ref[1, span, :] = with_prev
            bias_ref[0, span, :] = no_prev
        else:
            bias_ref[1, :, span] = with_prev
            bias_ref[0, :, span] = no_prev


def _stack_heads(v):
    first = lax.broadcasted_iota(jnp.int32, v.shape, 1) < HEAD_DIM
    zero = jnp.zeros_like(v)
    return jnp.concatenate([jnp.where(first, v, zero), jnp.where(first, zero, v)], axis=0)


def _per_head(c0, c1, n):
    return jnp.where(lax.broadcasted_iota(jnp.int32, (n, LANES), 1) < HEAD_DIM, c0, c1)


def _qkv_spec(seq, j):
    return pl.BlockSpec((None, seq, LANES), lambda b, hp: (b, 0, 3 * hp + j))


def _attention_fwd(qkv, slopes):
    B, S, _ = qkv.shape
    n_blk = S // WIN
    n_pair = N_HEADS // 2

    def body(slopes_ref, q_ref, k_ref, v_ref, o_ref, lse_ref, qp, kp, vp, bias, acc, mx, sm, acc_n, mx_n, sm_n):
        hp = pl.program_id(1)
        _zero_once((kp, vp))
        for p, d in enumerate(DILATIONS):
            nb = n_blk // d
            chunks = _row_chunks(d, S)
            for src, dst, n in chunks:
                qp[dst:dst + n, :] = (q_ref[src, :] * ATT_SCALE).astype(BF16)
                kp[WIN + dst:WIN + dst + n, :] = k_ref[src, :].astype(BF16)
                vp[WIN + dst:WIN + dst + n, :] = v_ref[src, :].astype(BF16)
            _pair_bias(bias, slopes_ref, hp, d, key_major=False)
            acc_t, mx_t, sm_t = (acc_n, mx_n, sm_n) if d == 1 else (acc, mx, sm)

            nk = WIN if nb == 1 else KEYS
            bias_cur = jnp.concatenate([bias[0, :, WIN:KEYS], bias[0, :, KEYS + WIN:]], axis=1) if nb == 1 else None

            def block(i, carry, p=p, nb=nb, nk=nk, bias_cur=bias_cur, acc_t=acc_t, mx_t=mx_t, sm_t=sm_t):
                cur = pl.ds(pl.multiple_of(i * WIN, WIN), WIN)
                keys = pl.ds(pl.multiple_of(i * WIN + (KEYS - nk), WIN), nk)
                s = lax.dot_general(qp[cur, :], _stack_heads(kp[keys, :]), _NT, preferred_element_type=F32)
                s = s + (bias_cur if nb == 1 else bias[((i % nb) > 0).astype(jnp.int32)])
                es, ms, ls = [], [], []
                for h in range(2):
                    sh = s[:, h * nk:(h + 1) * nk]
                    m = jnp.max(sh if nb == 1 else jnp.maximum(sh[:, :WIN], sh[:, WIN:]), axis=1, keepdims=True)
                    e = jnp.exp(sh - m)
                    es.append(e.astype(BF16))
                    ms.append(m)
                    ls.append(jnp.sum(e if nb == 1 else e[:, :WIN] + e[:, WIN:], axis=1, keepdims=True))
                acc_t[p, cur, :] = jnp.dot(jnp.concatenate(es, axis=1), _stack_heads(vp[keys, :]),
                                           preferred_element_type=F32)
                mx_t[p, cur, :] = _per_head(ms[0], ms[1], WIN)
                sm_t[p, cur, :] = _per_head(ls[0], ls[1], WIN)
                return carry

            lax.fori_loop(0, n_blk, block, 0, unroll=ATT_UNROLL)
            if d > 1:
                for src, dst, n in chunks:
                    acc_n[p, src, :] = acc[p, dst:dst + n, :]
                    mx_n[p, src, :] = mx[p, dst:dst + n, :]
                    sm_n[p, src, :] = sm[p, dst:dst + n, :]

        chunk = 256

        def merge(i, carry):
            rows = pl.ds(pl.multiple_of(i * chunk, chunk), chunk)
            ms = [mx_n[p, rows, :] for p in range(3)]
            m = jnp.maximum(jnp.maximum(ms[0], ms[1]), ms[2])
            ws = [jnp.exp(mp - m) for mp in ms]
            l = ws[0] * sm_n[0, rows, :] + ws[1] * sm_n[1, rows, :] + ws[2] * sm_n[2, rows, :]
            o = (ws[0] * acc_n[0, rows, :] + ws[1] * acc_n[1, rows, :] + ws[2] * acc_n[2, rows, :]) / l
            o_ref[rows, :] = o.astype(o_ref.dtype)
            lse = m + jnp.log(l)
            for h in range(2):
                lse_ref[rows, h:h + 1] = lse[:, h * HEAD_DIM:h * HEAD_DIM + 1]
            return carry

        lax.fori_loop(0, S // chunk, merge, 0)

    return pl.pallas_call(
        body, name="attention_fwd", grid=(B, n_pair),
        in_specs=[pl.BlockSpec(memory_space=pltpu.SMEM), _qkv_spec(S, 0), _qkv_spec(S, 1), _qkv_spec(S, 2)],
        out_specs=[pl.BlockSpec((None, S, LANES), lambda b, hp: (b, 0, hp)),
                   pl.BlockSpec((None, None, S, 2), lambda b, hp: (b, hp, 0, 0))],
        out_shape=[jax.ShapeDtypeStruct((B, S, ATT_WIDTH), BF16),
                   jax.ShapeDtypeStruct((B, n_pair, S, 2), F32)],
        scratch_shapes=[pltpu.VMEM((S, LANES), BF16), pltpu.VMEM((S + WIN, LANES), BF16),
                        pltpu.VMEM((S + WIN, LANES), BF16), pltpu.VMEM((2, WIN, 2 * KEYS), F32)]
        + [pltpu.VMEM((3, S, LANES), F32)] * 6,
        compiler_params=_params(2),
    )(slopes, qkv, qkv, qkv)


def _attention_bwd(qkv, o, do, lse, slopes):
    B, S, _ = qkv.shape
    n_blk = S // WIN
    n_pair = N_HEADS // 2

    def body(slopes_ref, q_ref, k_ref, v_ref, o_ref, do_ref, lse_ref, dx_ref,
             qp, dop, kp, vp, aux, auxp, aux_t, bias_t, dqp, dvk, dq_n, dk_n, dv_n):
        hp = pl.program_id(1)
        aux[...] = jnp.zeros_like(aux)
        for c0 in range(0, S, COPY_ROWS):
            rows = slice(c0, c0 + COPY_ROWS)
            prod = do_ref[rows, :] * o_ref[rows, :].astype(F32)
            for h in range(2):
                aux[rows, 2 * h:2 * h + 1] = lse_ref[rows, h:h + 1]
                aux[rows, 2 * h + 1:2 * h + 2] = jnp.sum(prod[:, h * HEAD_DIM:(h + 1) * HEAD_DIM], axis=1,
                                                         keepdims=True)
        dq_n[...] = jnp.zeros_like(dq_n)
        dk_n[...] = jnp.zeros_like(dk_n)
        dv_n[...] = jnp.zeros_like(dv_n)
        _zero_once((kp, vp))
        for p, d in enumerate(DILATIONS):
            nb = n_blk // d
            chunks = _row_chunks(d, S)
            for src, dst, n in chunks:
                auxp[dst:dst + n, :] = aux[src, :]
                qp[dst:dst + n, :] = (q_ref[src, :] * ATT_SCALE).astype(BF16)
                dop[dst:dst + n, :] = do_ref[src, :].astype(BF16)
                kp[WIN + dst:WIN + dst + n, :] = k_ref[src, :].astype(BF16)
                vp[WIN + dst:WIN + dst + n, :] = v_ref[src, :].astype(BF16)
            for i in range(n_blk):
                aux_t[i] = auxp[i * WIN:(i + 1) * WIN, :].T[0:8, :]
            _pair_bias(bias_t, slopes_ref, hp, d, key_major=True)
            dvk[...] = jnp.zeros_like(dvk)

            nk = WIN if nb == 1 else KEYS
            bias_cur = jnp.concatenate([bias_t[0, WIN:KEYS, :], bias_t[0, KEYS + WIN:, :]], axis=0) if nb == 1 else None

            def block(i, carry, nb=nb, nk=nk, bias_cur=bias_cur):
                cur = pl.ds(pl.multiple_of(i * WIN, WIN), WIN)
                keys = pl.ds(pl.multiple_of(i * WIN + (KEYS - nk), WIN), nk)
                q2, do2 = qp[cur, :], dop[cur, :]
                kc = _stack_heads(kp[keys, :])
                s_t = lax.dot_general(kc, q2, _NT, preferred_element_type=F32)
                s_t = s_t + (bias_cur if nb == 1 else bias_t[((i % nb) > 0).astype(jnp.int32)])
                dp_t = lax.dot_general(_stack_heads(vp[keys, :]), do2, _NT, preferred_element_type=F32)
                ps, dss = [], []
                for h in range(2):
                    span = slice(h * nk, (h + 1) * nk)
                    p_t = jnp.exp(s_t[span, :] - aux_t[i, 2 * h:2 * h + 1, :])
                    ds_t = p_t * (dp_t[span, :] - aux_t[i, 2 * h + 1:2 * h + 2, :])
                    ps.append(p_t.astype(BF16))
                    dss.append(ds_t.astype(BF16))
                do_rows, q_rows = _stack_heads(do2), _stack_heads(q2)
                zr = jnp.zeros_like(do_rows)
                rhs = jnp.concatenate([jnp.concatenate([do_rows, zr], axis=1),
                                       jnp.concatenate([zr, q_rows], axis=1)], axis=0)
                dvk[keys, :] += jnp.dot(jnp.concatenate(ps + dss, axis=1), rhs, preferred_element_type=F32)
                dqp[cur, :] = lax.dot_general(jnp.concatenate(dss, axis=0), kc, _TN, preferred_element_type=F32)
                return carry

            lax.fori_loop(0, n_blk, block, 0, unroll=ATT_UNROLL)
            for src, dst, n in chunks:
                dq_n[src, :] += dqp[dst:dst + n, :]
                dv_n[src, :] += dvk[WIN + dst:WIN + dst + n, :LANES]
                dk_n[src, :] += dvk[WIN + dst:WIN + dst + n, LANES:]
        for c0 in range(0, S, COPY_ROWS):
            rows = slice(c0, c0 + COPY_ROWS)
            dx_ref[rows, 0:LANES] = (dq_n[rows, :] * ATT_SCALE).astype(dx_ref.dtype)
            dx_ref[rows, LANES:2 * LANES] = dk_n[rows, :].astype(dx_ref.dtype)
            dx_ref[rows, 2 * LANES:3 * LANES] = dv_n[rows, :].astype(dx_ref.dtype)

    pair = lambda width: pl.BlockSpec((None, S, width), lambda b, hp: (b, 0, hp))
    vm = lambda shape, dt: pltpu.VMEM(shape, dt)
    return pl.pallas_call(
        body, name="attention_bwd", grid=(B, n_pair),
        in_specs=[pl.BlockSpec(memory_space=pltpu.SMEM), _qkv_spec(S, 0), _qkv_spec(S, 1), _qkv_spec(S, 2),
                  pair(LANES), pair(LANES), pl.BlockSpec((None, None, S, 2), lambda b, hp: (b, hp, 0, 0))],
        out_specs=pair(3 * LANES),
        out_shape=jax.ShapeDtypeStruct((B, S, 3 * ATT_WIDTH), BF16),
        scratch_shapes=[vm((S, LANES), BF16), vm((S, LANES), BF16),
                        vm((S + WIN, LANES), BF16), vm((S + WIN, LANES), BF16),
                        vm((S, LANES), F32), vm((S, LANES), F32), vm((n_blk, 8, WIN), F32),
                        vm((2, 2 * KEYS, WIN), F32),
                        vm((S, LANES), F32), vm((S + WIN, 2 * LANES), F32),
                        vm((S, LANES), F32), vm((S, LANES), F32), vm((S, LANES), F32)],
        compiler_params=_params(2),
    )(slopes, qkv, qkv, qkv, o, do, lse)


SCAN_COLS = 256
SCAN_ROWS = 8


def _rows_to_tile(rows):
    rid = lax.broadcasted_iota(jnp.int32, (SCAN_ROWS, rows[0].shape[1]), 0)
    tile = jnp.broadcast_to(rows[0], rid.shape)
    for k in range(1, SCAN_ROWS):
        tile = jnp.where(rid == k, rows[k], tile)
    return tile


SCAN_UNROLL = 4


def _complex_powers(ar, ai, n):
    out = [(ar, ai)]
    for _ in range(n - 1):
        pr, pi = out[-1]
        out.append((pr * ar - pi * ai, pr * ai + pi * ar))
    return out


def _round_multipliers(powers, rid, reverse):
    out = []
    for s in (1, 2, 4):
        keep = (rid < SCAN_ROWS - s) if reverse else (rid >= s)
        out.append((jnp.where(keep, powers[s - 1][0], 0.0), jnp.where(keep, powers[s - 1][1], 0.0)))
    return out


def _tile_scan(xr, xi, multipliers, reverse):
    for s, (mr, mi) in zip((1, 2, 4), multipliers):
        shift = SCAN_ROWS - s if reverse else s
        sr, si = pltpu.roll(xr, shift, 0), pltpu.roll(xi, shift, 0)
        xr, xi = xr + (mr * sr - mi * si), xi + (mr * si + mi * sr)
    return xr, xi


SCAN_CHUNK = 256


def _scan_fwd(us, bb_big, a_row, cc_big):
    B, S, _ = us.shape
    groups = 2
    width = 2 * groups * SCAN_COLS
    nc = 2 * SSM_COLS // width
    nt = S // SCAN_ROWS
    tiles = SCAN_CHUNK // SCAN_ROWS
    LAST = slice(SCAN_ROWS - 1, SCAN_ROWS)

    def body(us_ref, bb_ref, a_ref, cc_ref, xs_ref, y_ref, bu_ref):
        bb = bb_ref[...].astype(BF16)
        for c in range(S // SCAN_CHUNK):
            part = jnp.dot(us_ref[c * SCAN_CHUNK:(c + 1) * SCAN_CHUNK, :].astype(BF16), bb,
                           preferred_element_type=F32)
            bu_ref[c * tiles:(c + 1) * tiles] = part.reshape(tiles, SCAN_ROWS, width)
        rid = lax.broadcasted_iota(jnp.int32, (SCAN_ROWS, SCAN_COLS), 0)
        consts = []
        for g in range(groups):
            re = slice(2 * g * SCAN_COLS, (2 * g + 1) * SCAN_COLS)
            im = slice((2 * g + 1) * SCAN_COLS, (2 * g + 2) * SCAN_COLS)
            powers = _complex_powers(a_ref[:, re], a_ref[:, im], SCAN_ROWS)
            carry_mult = (_rows_to_tile([p[0] for p in powers]), _rows_to_tile([p[1] for p in powers]))
            consts.append((re, im, carry_mult, _round_multipliers(powers, rid, reverse=False)))

        def tile(i, carry):
            out = []
            for (re, im, (cr_t, ci_t), rounds), (cr, ci) in zip(consts, carry):
                xr, xi = _tile_scan(bu_ref[i, :, re], bu_ref[i, :, im], rounds, reverse=False)
                xs_ref[i, :, re] = xr + (cr_t * cr - ci_t * ci)
                xs_ref[i, :, im] = xi + (cr_t * ci + ci_t * cr)
                out.append((xs_ref[i, LAST, re], xs_ref[i, LAST, im]))
            return tuple(out)

        zero = jnp.zeros((1, SCAN_COLS), F32)
        lax.fori_loop(0, nt, tile, ((zero, zero),) * groups, unroll=SCAN_UNROLL)

        @pl.when(pl.program_id(1) == 0)
        def _():
            y_ref[...] = jnp.zeros_like(y_ref)

        cc = cc_ref[...].astype(BF16)
        for c in range(S // SCAN_CHUNK):
            x2 = xs_ref[c * tiles:(c + 1) * tiles].reshape(SCAN_CHUNK, width).astype(BF16)
            y_ref[c * SCAN_CHUNK:(c + 1) * SCAN_CHUNK, :] += jnp.dot(x2, cc, preferred_element_type=F32)

    col = pl.BlockSpec((None, nt, SCAN_ROWS, width), lambda b, j: (b, 0, 0, j))
    tok = pl.BlockSpec((None, S, SSM_WIDTH), lambda b, j: (b, 0, 0))
    xs, y = pl.pallas_call(
        body, name="s5_scan_fwd", grid=(B, nc),
        in_specs=[tok, pl.BlockSpec((SSM_WIDTH, width), lambda b, j: (0, j)),
                  pl.BlockSpec((1, width), lambda b, j: (0, j)), pl.BlockSpec((width, SSM_WIDTH), lambda b, j: (j, 0))],
        out_specs=[col, tok],
        out_shape=[jax.ShapeDtypeStruct((B, nt, SCAN_ROWS, 2 * SSM_COLS), F32),
                   jax.ShapeDtypeStruct((B, S, SSM_WIDTH), F32)],
        scratch_shapes=[pltpu.VMEM((nt, SCAN_ROWS, width), F32)],
        compiler_params=_params(2),
    )(us, bb_big, a_row, cc_big)
    return xs.reshape(B, S, 2 * SSM_COLS), y


def _scan_bwd(dy, us, bb_big, cc_big, xs, a_row):
    B, S, _ = dy.shape
    width = 2 * SCAN_COLS
    nc = SSM_COLS // SCAN_COLS
    nt = S // SCAN_ROWS
    tiles = SCAN_CHUNK // SCAN_ROWS
    RE, IM = slice(0, SCAN_COLS), slice(SCAN_COLS, 2 * SCAN_COLS)
    FIRST, LAST = slice(0, 1), slice(SCAN_ROWS - 1, SCAN_ROWS)

    def body(dy_ref, us_ref, bb_ref, cc_ref, x_ref, a_ref, dus_ref, ga_ref, dbb_ref, dcc_ref, d_ref, lam_ref):
        b = pl.program_id(1)
        cc = cc_ref[...].astype(BF16)
        for c in range(S // SCAN_CHUNK):
            part = lax.dot_general(dy_ref[c * SCAN_CHUNK:(c + 1) * SCAN_CHUNK, :].astype(BF16), cc, _NT,
                                   preferred_element_type=F32)
            d_ref[c * tiles:(c + 1) * tiles] = part.reshape(tiles, SCAN_ROWS, width)
        powers = _complex_powers(a_ref[:, RE], -a_ref[:, IM], SCAN_ROWS)
        rid = lax.broadcasted_iota(jnp.int32, (SCAN_ROWS, SCAN_COLS), 0)
        cr_t = _rows_to_tile([powers[SCAN_ROWS - 1 - r][0] for r in range(SCAN_ROWS)])
        ci_t = _rows_to_tile([powers[SCAN_ROWS - 1 - r][1] for r in range(SCAN_ROWS)])
        rounds = _round_multipliers(powers, rid, reverse=True)

        @pl.when(b == 0)
        def _():
            ga_ref[...] = jnp.zeros_like(ga_ref)
            dbb_ref[...] = jnp.zeros_like(dbb_ref)
            dcc_ref[...] = jnp.zeros_like(dcc_ref)

        def tile(j, carry):
            cr, ci, accr, acci = carry
            i = nt - 1 - j
            lr, li = _tile_scan(d_ref[i, :, RE], d_ref[i, :, IM], rounds, reverse=True)
            lam_r = lr + (cr_t * cr - ci_t * ci)
            lam_i = li + (cr_t * ci + ci_t * cr)
            lam_ref[i, :, RE] = lam_r
            lam_ref[i, :, IM] = lam_i
            ip = jnp.maximum(i - 1, 0)
            keep = (i > 0).astype(F32)
            xpr = jnp.where(rid == 0, x_ref[ip, LAST, RE] * keep, pltpu.roll(x_ref[i, :, RE], 1, 0))
            xpi = jnp.where(rid == 0, x_ref[ip, LAST, IM] * keep, pltpu.roll(x_ref[i, :, IM], 1, 0))
            accr = accr + lam_r * xpr + lam_i * xpi
            acci = acci + lam_i * xpr - lam_r * xpi
            return lam_ref[i, FIRST, RE], lam_ref[i, FIRST, IM], accr, acci

        z1 = jnp.zeros((1, SCAN_COLS), F32)
        z8 = jnp.zeros((SCAN_ROWS, SCAN_COLS), F32)
        _, _, accr, acci = lax.fori_loop(0, nt, tile, (z1, z1, z8, z8), unroll=SCAN_UNROLL)
        ga_ref[:, RE] += _col_sum(accr)
        ga_ref[:, IM] += _col_sum(acci)

        bb = bb_ref[...].astype(BF16)
        for c in range(S // SCAN_CHUNK):
            rows = slice(c * SCAN_CHUNK, (c + 1) * SCAN_CHUNK)
            lam2 = lam_ref[c * tiles:(c + 1) * tiles].reshape(SCAN_CHUNK, width).astype(BF16)
            x2 = x_ref[c * tiles:(c + 1) * tiles].reshape(SCAN_CHUNK, width).astype(BF16)
            dus_ref[rows, :] = lax.dot_general(lam2, bb, _NT, preferred_element_type=F32)
            dbb_ref[...] += lax.dot_general(us_ref[rows, :].astype(BF16), lam2, _TN, preferred_element_type=F32)
            dcc_ref[...] += lax.dot_general(x2, dy_ref[rows, :].astype(BF16), _TN, preferred_element_type=F32)

    col = pl.BlockSpec((None, nt, SCAN_ROWS, width), lambda j, b: (b, 0, 0, j))
    tok = pl.BlockSpec((None, S, SSM_WIDTH), lambda j, b: (b, 0, 0))
    scratch = pltpu.VMEM((nt, SCAN_ROWS, width), F32)
    return pl.pallas_call(
        body, name="s5_scan_bwd", grid=(nc, B),
        in_specs=[tok, tok, pl.BlockSpec((SSM_WIDTH, width), lambda j, b: (0, j)),
                  pl.BlockSpec((width, SSM_WIDTH), lambda j, b: (j, 0)), col,
                  pl.BlockSpec((1, width), lambda j, b: (0, j))],
        out_specs=[pl.BlockSpec((None, None, S, SSM_WIDTH), lambda j, b: (j, b, 0, 0)),
                   pl.BlockSpec((1, width), lambda j, b: (0, j)),
                   pl.BlockSpec((SSM_WIDTH, width), lambda j, b: (0, j)),
                   pl.BlockSpec((width, SSM_WIDTH), lambda j, b: (j, 0))],
        out_shape=[jax.ShapeDtypeStruct((nc, B, S, SSM_WIDTH), F32), jax.ShapeDtypeStruct((1, 2 * SSM_COLS), F32),
                   jax.ShapeDtypeStruct((SSM_WIDTH, 2 * SSM_COLS), F32),
                   jax.ShapeDtypeStruct((2 * SSM_COLS, SSM_WIDTH), F32)],
        scratch_shapes=[scratch, scratch],
        compiler_params=_params(2),
    )(dy, us, bb_big, cc_big, xs.reshape(B, nt, SCAN_ROWS, 2 * SSM_COLS), a_row)


def _s5_discretise(lr, li, log_dt):
    dt = jnp.exp(log_dt)
    mag = jnp.exp(lr * dt)
    ang = li * dt
    ab_re, ab_im = mag * jnp.cos(ang), mag * jnp.sin(ang)
    nr, ni = ab_re - 1.0, ab_im
    den = lr * lr + li * li
    f_re = (nr * lr + ni * li) / den
    f_im = (ni * lr - nr * li) / den
    return dt, ab_re, ab_im, nr, ni, den, f_re, f_im


def _s5_params(a_re, a_im, log_dt):
    def body(lr_ref, li_ref, ld_ref, abr, abi, fr, fi):
        _, ab_re, ab_im, _, _, _, f_re, f_im = _s5_discretise(lr_ref[...], li_ref[...], ld_ref[...])
        abr[...] = ab_re
        abi[...] = ab_im
        fr[...] = f_re
        fi[...] = f_im

    return pl.pallas_call(body, name="s5_params",
                          out_shape=[jax.ShapeDtypeStruct(a_re.shape, F32)] * 4)(a_re, a_im, log_dt)


def _s5_input_matrix(f_re, f_im, b_re, b_im):
    def body(fr, fi, br, bi, o_re, o_im):
        o_re[...] = fr[...] * br[...] - fi[...] * bi[...]
        o_im[...] = fr[...] * bi[...] + fi[...] * br[...]

    return pl.pallas_call(body, name="s5_input_matrix",
                          out_shape=[jax.ShapeDtypeStruct(b_re.shape, F32)] * 2)(f_re, f_im, b_re, b_im)


def _s5_input_matrix_bwd(f_re, f_im, b_re, b_im, g_re, g_im):
    def body(fr, fi, br, bi, gr, gi, dbr, dbi, dfr, dfi):
        dbr[...] = fr[...] * gr[...] + fi[...] * gi[...]
        dbi[...] = fr[...] * gi[...] - fi[...] * gr[...]
        dfr[...] = jnp.sum(br[...] * gr[...] + bi[...] * gi[...], axis=1, keepdims=True)
        dfi[...] = jnp.sum(br[...] * gi[...] - bi[...] * gr[...], axis=1, keepdims=True)

    return pl.pallas_call(
        body, name="s5_input_matrix_bwd",
        out_shape=[jax.ShapeDtypeStruct(b_re.shape, F32)] * 2 + [jax.ShapeDtypeStruct(f_re.shape, F32)] * 2,
    )(f_re, f_im, b_re, b_im, g_re, g_im)


def _s5_params_bwd(a_re, a_im, log_dt, g_ab_re, g_ab_im, d_f_re, d_f_im):
    def body(lr_ref, li_ref, ld_ref, gar, gai, dfr, dfi, o_lr, o_li, o_ld):
        lr, li = lr_ref[...], li_ref[...]
        dt, ab_re, ab_im, nr, ni, den, f_re, f_im = _s5_discretise(lr, li, ld_ref[...])
        d_fr, d_fi = dfr[...], dfi[...]
        d_nr = (d_fr * lr - d_fi * li) / den
        d_ni = (d_fr * li + d_fi * lr) / den
        common = (d_fr * f_re + d_fi * f_im) * 2.0 / den
        d_lr = (d_fr * nr + d_fi * ni) / den - common * lr
        d_li = (d_fr * ni - d_fi * nr) / den - common * li
        d_abr = gar[...] + d_nr
        d_abi = gai[...] + d_ni
        d_mag_mag = d_abr * ab_re + d_abi * ab_im
        d_ang = d_abi * ab_re - d_abr * ab_im
        o_lr[...] = d_lr + d_mag_mag * dt
        o_li[...] = d_li + d_ang * dt
        o_ld[...] = jnp.sum(d_mag_mag * lr + d_ang * li, axis=1, keepdims=True) * dt

    return pl.pallas_call(
        body, name="s5_params_bwd",
        out_shape=[jax.ShapeDtypeStruct(a_re.shape, F32)] * 2 + [jax.ShapeDtypeStruct(log_dt.shape, F32)],
    )(a_re, a_im, log_dt, g_ab_re, g_ab_im, d_f_re, d_f_im)


CONV_COLS = 256


def _shift_down(v, j, row):
    return jnp.where(row >= j, pltpu.roll(v, j, 0), 0.0)


def _shift_up(v, j, row, seq):
    return jnp.where(row < seq - j, pltpu.roll(v, seq - j, 0), 0.0)


def _conv_fwd(up, w_conv, b_conv):
    B, S, _ = up.shape
    nj = D_FF // CONV_COLS

    def body(up_ref, w_ref, b_ref, ff_ref):
        a = up_ref[:, :CONV_COLS].astype(F32)
        val = up_ref[:, CONV_COLS:].astype(F32)
        row = lax.broadcasted_iota(jnp.int32, a.shape, 0)
        w0, w1, w2 = w_ref[0:1, :], w_ref[1:2, :], w_ref[2:3, :]
        conv = b_ref[...] + w0 * a + w1 * _shift_down(a, 1, row) + w2 * _shift_down(a, 2, row)
        ff_ref[...] = (conv * _sigmoid(conv) * val).astype(ff_ref.dtype)

    return pl.pallas_call(
        body, name="conv_gate_fwd", grid=(B, nj),
        in_specs=[pl.BlockSpec((None, S, 2 * CONV_COLS), lambda b, j: (b, 0, j)),
                  pl.BlockSpec((3, CONV_COLS), lambda b, j: (0, j)),
                  pl.BlockSpec((1, CONV_COLS), lambda b, j: (0, j))],
        out_specs=pl.BlockSpec((None, S, CONV_COLS), lambda b, j: (b, 0, j)),
        out_shape=jax.ShapeDtypeStruct((B, S, D_FF), BF16),
        compiler_params=_params(2),
    )(up, w_conv, b_conv)


def _conv_bwd(up, d_ff, w_conv, b_conv):
    B, S, _ = up.shape
    nj = D_FF // CONV_COLS

    def body(up_ref, dff_ref, w_ref, b_ref, dup_ref, dw_ref, db_ref):
        b = pl.program_id(1)
        a = up_ref[:, :CONV_COLS].astype(F32)
        val = up_ref[:, CONV_COLS:].astype(F32)
        row = lax.broadcasted_iota(jnp.int32, a.shape, 0)
        w0, w1, w2 = w_ref[0:1, :], w_ref[1:2, :], w_ref[2:3, :]
        a1, a2 = _shift_down(a, 1, row), _shift_down(a, 2, row)
        conv = b_ref[...] + w0 * a + w1 * a1 + w2 * a2
        sg = _sigmoid(conv)
        dff = dff_ref[...].astype(F32)
        d_val = dff * conv * sg
        dc = dff * val * (sg * (1.0 + conv * (1.0 - sg)))
        d_a = w0 * dc + w1 * _shift_up(dc, 1, row, S) + w2 * _shift_up(dc, 2, row, S)
        dup_ref[:, :CONV_COLS] = d_a.astype(dup_ref.dtype)
        dup_ref[:, CONV_COLS:] = d_val.astype(dup_ref.dtype)

        @pl.when(b == 0)
        def _():
            dw_ref[...] = jnp.zeros_like(dw_ref)
            db_ref[...] = jnp.zeros_like(db_ref)

        dw_ref[0:1, :] += _col_sum(dc * a)
        dw_ref[1:2, :] += _col_sum(dc * a1)
        dw_ref[2:3, :] += _col_sum(dc * a2)
        db_ref[...] += _col_sum(dc)

    return pl.pallas_call(
        body, name="conv_gate_bwd", grid=(nj, B),
        in_specs=[pl.BlockSpec((None, S, 2 * CONV_COLS), lambda j, b: (b, 0, j)),
                  pl.BlockSpec((None, S, CONV_COLS), lambda j, b: (b, 0, j)),
                  pl.BlockSpec((3, CONV_COLS), lambda j, b: (0, j)),
                  pl.BlockSpec((1, CONV_COLS), lambda j, b: (0, j))],
        out_specs=[pl.BlockSpec((None, S, 2 * CONV_COLS), lambda j, b: (b, 0, j)),
                   pl.BlockSpec((3, CONV_COLS), lambda j, b: (0, j)),
                   pl.BlockSpec((1, CONV_COLS), lambda j, b: (0, j))],
        out_shape=[jax.ShapeDtypeStruct((B, S, 2 * D_FF), BF16), jax.ShapeDtypeStruct((3, D_FF), F32),
                   jax.ShapeDtypeStruct((1, D_FF), F32)],
        compiler_params=_params(2),
    )(up, d_ff, w_conv, b_conv)


def _ada_fwd(c_all, w_ada, b_ada):
    def body(c_ref, w_ref, b_ref, o_ref):
        cv = c_ref[...]
        act = (cv * _sigmoid(cv)).astype(BF16)
        o_ref[...] = jnp.dot(act, w_ref[...].astype(BF16), preferred_element_type=F32) + b_ref[...]

    return pl.pallas_call(body, name="ada_fwd",
                          out_shape=jax.ShapeDtypeStruct((c_all.shape[0], w_ada.shape[1]), F32),
                          compiler_params=pltpu.CompilerParams(vmem_limit_bytes=V7X_VMEM_LIMIT))(c_all, w_ada, b_ada)


def _ada_bwd(c_all, dmod_all, dmod_cols):
    def body(c_ref, dm_ref, dmc_ref, dw_ref, db_ref):
        cv = c_ref[...]
        act = (cv * _sigmoid(cv)).astype(BF16)
        dw_ref[...] = lax.dot_general(act, dmc_ref[...].astype(BF16), _TN, preferred_element_type=F32)
        db_ref[...] = _col_sum(dm_ref[...])

    return pl.pallas_call(
        body, name="ada_bwd",
        out_shape=[jax.ShapeDtypeStruct((c_all.shape[1], dmod_cols.shape[1]), F32),
                   jax.ShapeDtypeStruct((1, dmod_all.shape[1]), F32)],
        compiler_params=pltpu.CompilerParams(vmem_limit_bytes=V7X_VMEM_LIMIT))(c_all, dmod_all, dmod_cols)


def _adamw(w, m, v, g_parts, name, own=None):
    R, C = w.shape
    P = g_parts.shape[0]
    tr = R
    for cand in (256, 128, 64, 32, 16, 8):
        if R % cand == 0 and cand * C * 4 * (P + 8) * 2 <= V7X_VMEM_LIMIT // 2:
            tr = cand
            break
    c1 = 1.0 / (1.0 - ADAM_B1 ** ADAM_STEP)
    c2 = 1.0 / (1.0 - ADAM_B2 ** ADAM_STEP)

    def update(w_ref, m_ref, v_ref, g, og, od, om, ov):
        m_new = ADAM_B1 * m_ref[...] + (1.0 - ADAM_B1) * g
        v_new = ADAM_B2 * v_ref[...] + (1.0 - ADAM_B2) * (g * g)
        og[...] = g
        om[...] = m_new
        ov[...] = v_new
        od[...] = -ADAM_LR * ((m_new * c1) / (jnp.sqrt(v_new * c2) + ADAM_EPS) + ADAM_WD * w_ref[...])

    def total(g_ref):
        g = g_ref[0].astype(F32)
        for p in range(1, P):
            g = g + g_ref[p].astype(F32)
        return g

    out_shape = [jax.ShapeDtypeStruct((R, C), F32)] * 4
    if own is None:
        def body(w_ref, m_ref, v_ref, g_ref, og, od, om, ov):
            update(w_ref, m_ref, v_ref, total(g_ref), og, od, om, ov)

        spec = pl.BlockSpec((tr, C), lambda i: (i, 0))
        return pl.pallas_call(
            body, name=name, grid=(R // tr,),
            in_specs=[spec, spec, spec, pl.BlockSpec((P, tr, C), lambda i: (0, i, 0))],
            out_specs=[spec] * 4, out_shape=out_shape, compiler_params=_params(1),
        )(w, m, v, g_parts)

    slots, me = own

    def body_own(me_ref, w_ref, m_ref, v_ref, g_ref, own_ref, og, od, om, ov):
        g = own_ref[...].astype(F32)
        for p in range(P):
            g = g + jnp.where(me_ref[0] == p, 0.0, g_ref[p].astype(F32))
        update(w_ref, m_ref, v_ref, g, og, od, om, ov)

    spec = pl.BlockSpec((tr, C), lambda i, me_ref: (i, 0))
    grid_spec = pltpu.PrefetchScalarGridSpec(
        num_scalar_prefetch=1, grid=(R // tr,),
        in_specs=[spec, spec, spec, pl.BlockSpec((P, tr, C), lambda i, me_ref: (0, i, 0)),
                  pl.BlockSpec((None, tr, C), lambda i, me_ref: (me_ref[0], i, 0))],
        out_specs=[spec] * 4)
    return pl.pallas_call(body_own, name=name, grid_spec=grid_spec, out_shape=out_shape,
                          compiler_params=_params(1))(me, w, m, v, g_parts, slots)


def _adamw_small(ws, ms, vs, gs):
    n = len(ws)
    c1 = 1.0 / (1.0 - ADAM_B1 ** ADAM_STEP)
    c2 = 1.0 / (1.0 - ADAM_B2 ** ADAM_STEP)

    def body(*refs):
        ins, outs = refs[:4 * n], refs[4 * n:]
        for i in range(n):
            w, m, v, g = ins[i][...], ins[n + i][...], ins[2 * n + i][...], ins[3 * n + i][...]
            m_new = ADAM_B1 * m + (1.0 - ADAM_B1) * g
            v_new = ADAM_B2 * v + (1.0 - ADAM_B2) * (g * g)
            outs[4 * i][...] = g
            outs[4 * i + 1][...] = -ADAM_LR * ((m_new * c1) / (jnp.sqrt(v_new * c2) + ADAM_EPS) + ADAM_WD * w)
            outs[4 * i + 2][...] = m_new
            outs[4 * i + 3][...] = v_new

    out_shape = [jax.ShapeDtypeStruct(w.shape, F32) for w in ws for _ in range(4)]
    return pl.pallas_call(body, name="adamw_small", out_shape=out_shape,
                          compiler_params=pltpu.CompilerParams(vmem_limit_bytes=V7X_VMEM_LIMIT))(*ws, *ms, *vs, *gs)


def _sum_parts(parts, loss_rows):
    P, R, C = parts.shape
    lo, hi = loss_rows

    def body(p_ref, o_ref, loss_ref):
        t = p_ref[0]
        for p in range(1, P):
            t = t + p_ref[p]
        o_ref[...] = t
        tot = jnp.sum(jnp.sum(o_ref[lo:hi, :], axis=1, keepdims=True), axis=0, keepdims=True)
        loss_ref[...] = jnp.broadcast_to(tot, loss_ref.shape)

    return pl.pallas_call(body, name="sum_small_grads",
                          out_shape=[jax.ShapeDtypeStruct((R, C), F32), jax.ShapeDtypeStruct((1, LANES), F32)],
                          compiler_params=pltpu.CompilerParams(vmem_limit_bytes=V7X_VMEM_LIMIT))(parts)


def _exchange(items, name):
    n = len(items)
    MESH = pl.DeviceIdType.MESH

    def body(*refs):
        src, dst = refs[:n], refs[n:2 * n]
        send_sems, recv_sems, local_sems = refs[2 * n:]
        x, y, c = lax.axis_index("x"), lax.axis_index("y"), lax.axis_index("c")
        me = 4 * x + 2 * y + c
        started = []
        for it, (_, per_peer) in enumerate(items):
            own = pltpu.make_async_copy(src[it].at[me] if per_peer else src[it], dst[it].at[me], local_sems.at[it])
            own.start()
            started.append(own)
        sends, recvs = [], []
        for k in range(1, N_DEV):
            px = 1 - x if k & 4 else x
            py = 1 - y if k & 2 else y
            pc = 1 - c if k & 1 else c
            peer = 4 * px + 2 * py + pc
            for it, (_, per_peer) in enumerate(items):
                s = src[it].at[peer] if per_peer else src[it]
                cp = pltpu.make_async_remote_copy(src_ref=s, dst_ref=dst[it].at[me], send_sem=send_sems.at[it, k - 1],
                                                  recv_sem=recv_sems.at[it, k - 1], device_id=(px, py, pc),
                                                  device_id_type=MESH)
                cp.start()
                sends.append(cp)
                recvs.append(pltpu.make_async_remote_copy(
                    src_ref=s, dst_ref=dst[it].at[peer], send_sem=send_sems.at[it, k - 1],
                    recv_sem=recv_sems.at[it, k - 1], device_id=(px, py, pc), device_id_type=MESH))
        for cp in recvs:
            cp.wait_recv()
        for cp in sends:
            cp.wait_send()
        for cp in started:
            cp.wait()

    any_spec = pl.BlockSpec(memory_space=pl.ANY)
    out_shape = []
    for a, per_peer in items:
        shp = a.shape if per_peer else (N_DEV,) + a.shape
        out_shape.append(jax.ShapeDtypeStruct(shp, a.dtype))
    return pl.pallas_call(
        body, name=name, in_specs=[any_spec] * n, out_specs=[any_spec] * n, out_shape=out_shape,
        scratch_shapes=[pltpu.SemaphoreType.DMA((n, N_DEV - 1)), pltpu.SemaphoreType.DMA((n, N_DEV - 1)),
                        pltpu.SemaphoreType.DMA((n,))],
    )(*[a for a, _ in items])


def _remote(src, dst, send_sem, recv_sem, device):
    return pltpu.make_async_remote_copy(src_ref=src, dst_ref=dst, send_sem=send_sem, recv_sem=recv_sem,
                                        device_id=device, device_id_type=pl.DeviceIdType.MESH)


def _mesh_place():
    x, y, c = lax.axis_index("x"), lax.axis_index("y"), lax.axis_index("c")
    other_chips = [(1 - x, y), (x, 1 - y), (1 - x, 1 - y)]
    return x, y, c, (x, y, 1 - c), other_chips


def _gather_all(items, name):
    n = len(items)

    def body(*refs):
        src, dst = refs[:n], refs[n:2 * n]
        send_sems, recv_sems, local_sems = refs[2 * n:]
        x, y, c, sibling, chips = _mesh_place()
        slot = lambda px, py, pc: 4 * px + 2 * py + pc
        me = slot(x, y, c)
        own = [pltpu.make_async_copy(src[it], dst[it].at[me], local_sems.at[it]) for it in range(n)]
        first = []
        for it in range(n):
            first.append(_remote(src[it], dst[it].at[me], send_sems.at[it, 0], recv_sems.at[it, 0], sibling))
            for j, chip in enumerate(chips):
                first.append(_remote(src[it], dst[it].at[me], send_sems.at[it, 1 + j], recv_sems.at[it, 1 + j],
                                     (*chip, c)))
        for cp in own + first:
            cp.start()
        passed = []
        for j, chip in enumerate(chips):
            blk = slot(*chip, c)
            for it in range(n):
                _remote(src[it], dst[it].at[blk], send_sems.at[it, 1 + j], recv_sems.at[it, 1 + j],
                        (*chip, c)).wait_recv()
                fwd = _remote(dst[it].at[blk], dst[it].at[blk], send_sems.at[it, 4 + j], recv_sems.at[it, 4 + j],
                              sibling)
                fwd.start()
                passed.append(fwd)
        for it in range(n):
            _remote(src[it], dst[it].at[slot(x, y, 1 - c)], send_sems.at[it, 0], recv_sems.at[it, 0],
                    sibling).wait_recv()
        for j, chip in enumerate(chips):
            for it in range(n):
                _remote(src[it], dst[it].at[slot(*chip, 1 - c)], send_sems.at[it, 4 + j], recv_sems.at[it, 4 + j],
                        sibling).wait_recv()
        for cp in first + passed:
            cp.wait_send()
        for cp in own:
            cp.wait()

    any_spec = pl.BlockSpec(memory_space=pl.ANY)
    return pl.pallas_call(
        body, name=name, in_specs=[any_spec] * n, out_specs=[any_spec] * n,
        out_shape=[jax.ShapeDtypeStruct((N_DEV,) + a.shape, a.dtype) for a in items],
        scratch_shapes=[pltpu.SemaphoreType.DMA((n, 7)), pltpu.SemaphoreType.DMA((n, 7)),
                        pltpu.SemaphoreType.DMA((n,))],
    )(*items)


def _peers():
    x, y, c = lax.axis_index("x"), lax.axis_index("y"), lax.axis_index("c")
    out = []
    for k in range(1, N_DEV):
        px = 1 - x if k & 4 else x
        py = 1 - y if k & 2 else y
        pc = 1 - c if k & 1 else c
        out.append((k, (px, py, pc), 4 * px + 2 * py + pc))
    return 4 * x + 2 * y + c, out


def _exchange_start(items, name, gather, carry=()):
    n, m = len(items), len(carry)

    def body(*refs):
        src, land = refs[:n], refs[n:2 * n]
        first_out = 2 * n + m
        send_sems, recv_sems = refs[first_out:first_out + n], refs[first_out + n:first_out + 2 * n]
        token = refs[-1]
        me, peers = _peers()
        for k, peer, slot in peers:
            for it in range(n):
                _remote(src[it] if gather else src[it].at[slot], land[it].at[me], send_sems[it], recv_sems[it],
                        peer).start()
        token[...] = jnp.zeros_like(token)

    hbm = pl.BlockSpec(memory_space=pltpu.HBM)
    sem = pl.BlockSpec(memory_space=pltpu.SEMAPHORE)
    land_shapes = [(N_DEV,) + (a.shape if gather else a.shape[1:]) for a in items]
    lands = [lax.empty(shp, a.dtype) for shp, a in zip(land_shapes, items)]
    through = list(items) + lands + list(carry)
    outs = pl.pallas_call(
        body, name=name,
        out_shape=(*[pltpu.SemaphoreType.DMA(())] * (2 * n), *[pltpu.HBM(a.shape, a.dtype) for a in through],
                   jax.ShapeDtypeStruct((8, LANES), F32)),
        in_specs=[hbm] * len(through),
        out_specs=(*[sem] * (2 * n), *[hbm] * len(through), pl.BlockSpec(memory_space=pltpu.VMEM)),
        input_output_aliases={i: 2 * n + i for i in range(len(through))},
        compiler_params=pltpu.CompilerParams(has_side_effects=pltpu.SideEffectType.DATAFLOW_SIDE_EFFECTING),
    )(*[pltpu.with_memory_space_constraint(a, pltpu.HBM) for a in through])
    return (list(outs[:n]), list(outs[n:2 * n]), list(outs[2 * n:3 * n]), list(outs[3 * n:4 * n]), outs[-1],
            list(outs[4 * n:4 * n + m]))


def _exchange_wait(send_sems, recv_sems, items, lands, after, name):
    n = len(items)

    def body(*refs):
        land = refs[n:2 * n]
        send_sems, recv_sems = refs[2 * n:3 * n], refs[3 * n:4 * n]
        me, peers = _peers()
        for it in range(n):
            seven = land[it].at[pl.ds(0, N_DEV - 1)]
            cp = _remote(seven, seven, send_sems[it], recv_sems[it], peers[0][1])
            cp.wait_send()
            cp.wait_recv()

    hbm = pl.BlockSpec(memory_space=pltpu.HBM)
    sem = pl.BlockSpec(memory_space=pltpu.SEMAPHORE)
    outs = pl.pallas_call(
        body, name=name,
        out_shape=tuple(pltpu.HBM(a.shape, a.dtype) for a in list(items) + list(lands)),
        in_specs=[hbm] * (2 * n) + [sem] * (2 * n) + [pl.BlockSpec(memory_space=pl.ANY)],
        out_specs=tuple([hbm] * (2 * n)),
        input_output_aliases={i: i for i in range(2 * n)},
        compiler_params=pltpu.CompilerParams(has_side_effects=pltpu.SideEffectType.DATAFLOW_SIDE_EFFECTING),
    )(*items, *lands, *send_sems, *recv_sems, after)
    return list(outs[:n]), list(outs[n:])


def _gelu_tanh(y):
    k = math.sqrt(2.0 / math.pi)
    t = jnp.tanh(k * (y + 0.044715 * y * y * y))
    return 0.5 * y * (1.0 + t), t


def _local_step(x, mod, target, W, late_weights, P, send_early):
    B, S, D = x.shape
    T = B * S
    TS = 512
    flat = lambda a: a.reshape(T, a.shape[-1])
    unflat = lambda a: a.reshape(B, S, a.shape[-1])
    mod_col = lambda i: (mod, D, i)

    def f_modnorm(xv, sc, sh, g):
        return (xv * _rms_scale(xv) * g) * (1.0 + sc) + sh

    (u1,) = _rowwise(f_modnorm, [(x, D, 0)], [mod_col(1), mod_col(0)], [P["g_mix"]],
                     [(D, BF16)], [], [], ts=TS, name="modnorm_mix")
    u1f = flat(u1)
    qkv, us, gates = (unflat(t) for t in _project_in(u1f, [W["w_qkv"], W["w_us"], W["w_gates"]], [F32, F32, BF16],
                                                      name="proj_in"))

    o_att, lse = _attention_fwd(qkv, P["slopes"])
    more_w, more_p = late_weights(o_att)
    W, P = {**W, **more_w}, {**P, **more_p}
    y_att = unflat(_matmul(flat(o_att), W["w_proj_att"], out_dtype=BF16, name="proj_att"))

    xs, y_mm = _scan_fwd(us, P["bb_big"], P["a_row"], P["cc_big"])

    def f_glu(ymm, usv, dsk, wg, bg):
        yv = ymm + dsk * usv
        ge, _ = _gelu_tanh(yv)
        pre = jnp.dot(ge.astype(BF16), wg, preferred_element_type=F32) + bg
        return yv, ge * _sigmoid(pre)

    y_s5, z = _rowwise(f_glu, [(y_mm, SSM_WIDTH, 0), (us, SSM_WIDTH, 0)], [], [P["d_skip"], W["w_glu"], P["b_glu"]],
                       [(SSM_WIDTH, F32), (SSM_WIDTH, BF16)], [], [], ts=TS, name="s5_glu")
    y_ssm = unflat(_matmul(flat(z), W["w_proj_ssm"], out_dtype=BF16, name="proj_ssm"))

    def f_merge(ga, gs, ya, ys, bga, bgs):
        return _sigmoid(ga + bga) * ya + _sigmoid(gs + bgs) * ys

    bga, bgs = P["b_gate"][:, :D], P["b_gate"][:, D:]
    (merged,) = _rowwise(f_merge, [(gates, D, 0), (gates, D, 1), (y_att, D, 0), (y_ssm, D, 0)], [], [bga, bgs],
                         [(D, BF16)], [], [], ts=TS, name="gate_merge")
    mix = unflat(_matmul(flat(merged), W["w_out"], out_dtype=BF16, name="proj_out"))

    def f_res_modnorm(xv, mx, gt, sc, sh, g):
        h = xv + gt * mx
        return h, (h * _rms_scale(h) * g) * (1.0 + sc) + sh

    h1, u2 = _rowwise(f_res_modnorm, [(x, D, 0), (mix, D, 0)], [mod_col(2), mod_col(4), mod_col(3)], [P["g_ffn"]],
                      [(D, F32), (D, BF16)], [], [], ts=TS, name="residual_modnorm_ffn")
    up = unflat(_up_fwd(flat(u2), W["w_up"], name="ffn_up"))
    ff = _conv_fwd(up, P["w_conv"], P["b_conv"])
    down = unflat(_matmul(flat(ff), W["w_down"], out_dtype=BF16, name="ffn_down"))

    def f_head(h1v, dn, tg, gt, g):
        h2 = h1v + gt * dn
        r = _rms_scale(h2)
        nh = h2 * r
        e = nh * g - tg
        dy = e * (1.0 / D)
        gy = dy * g
        dh = r * (gy - nh * jnp.mean(gy * nh, axis=-1, keepdims=True))
        return (dh, dh * gt, _col_sum(dh * dn), _col_sum(dy * nh), _col_sum(e * e) * (0.5 / D))

    dh2, d_down, d_gt2, d_g_final, loss_cols = _rowwise(
        f_head, [(h1, D, 0), (down, D, 0), (target, D, 0)], [mod_col(5)], [P["g_final"]],
        [(D, BF16), (D, BF16)], [D], [(1, D), (1, D)], ts=TS, name="head_loss")

    d_downf = flat(d_down)
    d_ff = unflat(_matmul(d_downf, W["w_down"], tb=True, out_dtype=BF16, name="ffn_down_dx"))
    d_w_down = _matmul(flat(ff), d_downf, ta=True, out_dtype=BF16, name="ffn_down_dw")
    d_up, d_w_conv, d_b_conv = _conv_bwd(up, d_ff, P["w_conv"], P["b_conv"])
    d_upf = flat(d_up)
    d_u2 = unflat(_up_dx(d_upf, W["w_up"], name="ffn_up_dx"))
    d_w_up = _up_dw(flat(u2), d_upf, name="ffn_up_dw")
    token, _ = send_early(dict(w_down=d_w_down.reshape(N_DEV, D_FF // N_DEV, D), w_up=d_w_up))
    g_ffn_after = P["g_ffn"] + token[0:1, 0:1]

    def f_modnorm_bwd(du, h, dres, mx, sc, gt, g):
        r = _rms_scale(h)
        nh = h * r
        dn = du * (1.0 + sc)
        gy = dn * g
        dh = dres + r * (gy - nh * jnp.mean(gy * nh, axis=-1, keepdims=True))
        return (dh, dh * gt, _col_sum(du), _col_sum(du * nh * g), _col_sum(dh * mx), _col_sum(dn * nh))

    dh1, d_mix, d_sh2, d_sc2, d_gt1, d_g_ffn = _rowwise(
        f_modnorm_bwd, [(d_u2, D, 0), (h1, D, 0), (dh2, D, 0), (mix, D, 0)], [mod_col(4), mod_col(2)], [g_ffn_after],
        [(D, BF16), (D, BF16)], [D, D, D], [(1, D)], ts=TS, name="modnorm_ffn_bwd")

    d_mixf = flat(d_mix)
    d_merged = unflat(_matmul(d_mixf, W["w_out"], tb=True, out_dtype=BF16, name="proj_out_dx"))
    d_w_out = _matmul(flat(merged), d_mixf, ta=True, out_dtype=BF16, name="proj_out_dw")

    def f_merge_bwd(dm, ga, gs, ya, ys, bga_, bgs_):
        sa, ss = _sigmoid(ga + bga_), _sigmoid(gs + bgs_)
        dga = dm * ya * sa * (1.0 - sa)
        dgs = dm * ys * ss * (1.0 - ss)
        return dm * sa, dm * ss, jnp.concatenate([dga, dgs], axis=1), _col_sum(dga), _col_sum(dgs)

    d_y_att, d_y_ssm, d_gates, d_bga, d_bgs = _rowwise(
        f_merge_bwd, [(d_merged, D, 0), (gates, D, 0), (gates, D, 1), (y_att, D, 0), (y_ssm, D, 0)], [], [bga, bgs],
        [(D, BF16), (D, BF16), (2 * D, BF16)], [], [(1, D), (1, D)], ts=TS, name="gate_merge_bwd")

    d_yaf, d_ysf = flat(d_y_att), flat(d_y_ssm)
    d_o_att = unflat(_matmul(d_yaf, W["w_proj_att"], tb=True, name="proj_att_dx"))
    d_w_proj_att = _matmul(flat(o_att), d_yaf, ta=True, out_dtype=BF16, name="proj_att_dw")
    d_z = unflat(_matmul(d_ysf, W["w_proj_ssm"], tb=True, out_dtype=BF16, name="proj_ssm_dx"))
    d_w_proj_ssm = _matmul(flat(z), d_ysf, ta=True, out_dtype=BF16, name="proj_ssm_dw")

    def f_glu_bwd(yv, dz, usv, dsk, wg, bg):
        ge, t = _gelu_tanh(yv)
        pre = jnp.dot(ge.astype(BF16), wg, preferred_element_type=F32) + bg
        sg = _sigmoid(pre)
        dpre = dz * ge * sg * (1.0 - sg)
        dge = dz * sg + lax.dot_general(dpre.astype(BF16), wg, _NT, preferred_element_type=F32)
        k = math.sqrt(2.0 / math.pi)
        dgelu = 0.5 * (1.0 + t) + 0.5 * yv * (1.0 - t * t) * k * (1.0 + 3.0 * 0.044715 * yv * yv)
        dy = dge * dgelu
        dwg = lax.dot_general(ge.astype(BF16), dpre.astype(BF16), _TN, preferred_element_type=F32)
        return dy, dy * dsk, dwg, _col_sum(dpre), _col_sum(dy * usv)

    d_y_s5, d_us_skip, d_w_glu, d_b_glu, d_d_skip = _rowwise(
        f_glu_bwd, [(y_s5, SSM_WIDTH, 0), (d_z, SSM_WIDTH, 0), (us, SSM_WIDTH, 0)], [],
        [P["d_skip"], W["w_glu"], P["b_glu"]],
        [(SSM_WIDTH, BF16), (SSM_WIDTH, F32)], [], [(SSM_WIDTH, SSM_WIDTH), (1, SSM_WIDTH), (1, SSM_WIDTH)],
        ts=TS, name="s5_glu_bwd")
    d_us_parts, g_ab, d_bb, d_cc = _scan_bwd(d_y_s5, us, P["bb_big"], P["cc_big"], xs, P["a_row"])

    token, _ = send_early(dict(
        w_out=d_w_out.reshape(N_DEV, D // N_DEV, D), w_proj_att=_cols_to_slots(d_w_proj_att),
        w_proj_ssm=_cols_to_slots(d_w_proj_ssm),
        w_glu=d_w_glu.astype(BF16).reshape(N_DEV, SSM_WIDTH // N_DEV, SSM_WIDTH),
        w_conv=_cols_to_slots(d_w_conv.astype(BF16))))
    d_qkv = _attention_bwd(qkv, o_att, d_o_att, lse, P["slopes"] + token[0, 0])

    def f_add(*parts):
        return sum(parts[1:], parts[0])

    n_parts = d_us_parts.shape[0]
    stacked = d_us_parts.reshape(n_parts * B, S, SSM_WIDTH)
    (d_us,) = _rowwise(f_add, [(d_us_skip, SSM_WIDTH, 0)] + [(stacked, SSM_WIDTH, 0, j * B) for j in range(n_parts)],
                       [], [],
                       [(SSM_WIDTH, BF16)], [], [], ts=TS, name="s5_input_grad")
    d_qkvf = flat(d_qkv)
    d_usf = flat(d_us)
    d_gatesf = flat(d_gates)
    d_w_in = jnp.concatenate(
        [_unpair_qkv_columns(_matmul(u1f, d_qkvf, ta=True, out_dtype=BF16, name="proj_qkv_dw")),
         _matmul(u1f, d_usf, ta=True, out_dtype=BF16, name="proj_ssm_in_dw"),
         _matmul(u1f, d_gatesf, ta=True, out_dtype=BF16, name="proj_gates_dw")], axis=1)
    token, (w_qkv, w_us, w_gates) = send_early(dict(w_in=_cols_to_slots(d_w_in)),
                                               carry=[W["w_qkv"], W["w_us"], W["w_gates"]])
    d_u1 = unflat(_project_back([d_qkvf, d_usf, d_gatesf], [w_qkv, w_us, w_gates], name="proj_in_dx"))

    def f_modnorm_bwd_in(du, h, dres, sc, g):
        r = _rms_scale(h)
        nh = h * r
        dn = du * (1.0 + sc)
        gy = dn * g
        dh = dres + r * (gy - nh * jnp.mean(gy * nh, axis=-1, keepdims=True))
        return (dh, _col_sum(du), _col_sum(du * nh * g), _col_sum(dn * nh))

    grad_x, d_sh1, d_sc1, d_g_mix = _rowwise(
        f_modnorm_bwd_in, [(d_u1, D, 0), (x, D, 0), (dh1, D, 0)], [mod_col(1)], [P["g_mix"] + token[0:1, 0:1]],
        [(D, F32)], [D, D], [(1, D)], ts=TS, name="modnorm_mix_bwd")

    d_mod = jnp.concatenate([d_sh1, d_sc1, d_gt1, d_sh2, d_sc2, d_gt2], axis=-1)
    g_ab_re, g_ab_im = _deinterleave(g_ab)
    d_bb_re, d_bb_im = _deinterleave(d_bb)
    d_cc_re, d_cc_im = (t.T for t in _deinterleave(d_cc.T))
    small = dict(g_mix=d_g_mix, b_gate=jnp.concatenate([d_bga, d_bgs], axis=1), g_ab_re=g_ab_re, g_ab_im=g_ab_im,
                 d_bb_re=d_bb_re, d_bb_im=d_bb_im, d_cc_re=d_cc_re, d_cc_im=d_cc_im, d_skip=d_d_skip,
                 b_glu=d_b_glu, g_ffn=d_g_ffn, b_conv=d_b_conv, g_final=d_g_final, loss_cols=loss_cols)
    return grad_x, d_mod, small


def _block_diag_in(bb):
    t = bb.reshape(SSM_GROUPS, SSM_STATE, SSM_GROUP_CH)
    eye = jnp.eye(SSM_GROUPS, dtype=bb.dtype)
    return jnp.einsum("gnc,gh->gchn", t, eye).reshape(SSM_WIDTH, SSM_COLS)


def _block_diag_out(cm):
    eye = jnp.eye(SSM_GROUPS, dtype=cm.dtype)
    return jnp.einsum("gcn,gh->gnhc", cm, eye).reshape(SSM_COLS, SSM_WIDTH)


def _diag_blocks_in(m):
    t = m.reshape(SSM_GROUPS, SSM_GROUP_CH, SSM_GROUPS, SSM_STATE)
    idx = jnp.arange(SSM_GROUPS)
    return t[idx, :, idx, :].transpose(0, 2, 1).reshape(SSM_COLS, SSM_GROUP_CH)


def _diag_blocks_out(m):
    t = m.reshape(SSM_GROUPS, SSM_STATE, SSM_GROUPS, SSM_GROUP_CH)
    idx = jnp.arange(SSM_GROUPS)
    return t[idx, :, idx, :].transpose(0, 2, 1)


def _pair_qkv_columns(w):
    lead = w.shape[:-1]
    return w.reshape(lead + (3, N_HEADS // 2, LANES)).swapaxes(-3, -2).reshape(lead + (3 * ATT_WIDTH,))


def _unpair_qkv_columns(w):
    lead = w.shape[:-1]
    return w.reshape(lead + (N_HEADS // 2, 3, LANES)).swapaxes(-3, -2).reshape(lead + (3 * ATT_WIDTH,))


def _interleave(re, im):
    lead = re.shape[:-1]
    g = lambda a: a.reshape(lead + (SSM_COLS // SCAN_COLS, 1, SCAN_COLS))
    return jnp.concatenate([g(re), g(im)], axis=-2).reshape(lead + (2 * SSM_COLS,))


def _deinterleave(x):
    lead = x.shape[:-1]
    t = x.reshape(lead + (SSM_COLS // SCAN_COLS, 2, SCAN_COLS))
    return t[..., 0, :].reshape(lead + (SSM_COLS,)), t[..., 1, :].reshape(lead + (SSM_COLS,))


def _cols_to_slots(g):
    R = g.shape[0]
    return g.reshape(R, N_DEV, g.shape[1] // N_DEV).transpose(1, 0, 2)


def _slots_to_cols(g):
    return g.transpose(1, 0, 2).reshape(g.shape[1], N_DEV * g.shape[2])


SMALL_ORDER = ("b_ada", "g_mix", "b_gate", "a_re", "a_im", "log_dt", "b_re", "b_im", "c_re", "c_im", "d_skip",
               "b_glu", "g_ffn", "b_conv", "g_final")


def _pack(arrs):
    pieces, offs, row = [], [], 0
    for a in arrs:
        f = a.reshape(-1).astype(F32)
        n = f.shape[0]
        rows = -(-n // LANES)
        pieces.append(jnp.pad(f, (0, rows * LANES - n)))
        offs.append((row, n))
        row += rows
    return jnp.concatenate(pieces).reshape(row, LANES), offs


def _unpack(packed, offs, shapes):
    flat = packed.reshape(-1)
    return [flat[r * LANES:r * LANES + n].reshape(s) for (r, n), s in zip(offs, shapes)]


def kernel(x, c, w_ada, b_ada, g_mix, w_in, b_gate, a_re, a_im, log_dt, b_re, b_im, c_re, c_im, d_skip, w_glu, b_glu, w_proj_att, w_proj_ssm, w_out, g_ffn, w_up, w_conv, b_conv, w_down, g_final, loss_target, m_w_ada, m_b_ada, m_g_mix, m_w_in, m_b_gate, m_a_re, m_a_im, m_log_dt, m_b_re, m_b_im, m_c_re, m_c_im, m_d_skip, m_w_glu, m_b_glu, m_w_proj_att, m_w_proj_ssm, m_w_out, m_g_ffn, m_w_up, m_w_conv, m_b_conv, m_w_down, m_g_final, v_w_ada, v_b_ada, v_g_mix, v_w_in, v_b_gate, v_a_re, v_a_im, v_log_dt, v_b_re, v_b_im, v_c_re, v_c_im, v_d_skip, v_w_glu, v_b_glu, v_w_proj_att, v_w_proj_ssm, v_w_out, v_g_ffn, v_w_up, v_w_conv, v_b_conv, v_w_down, v_g_final):
    args = dict(locals())
    B, S, D = x.shape
    me = 4 * lax.axis_index("x") + 2 * lax.axis_index("y") + lax.axis_index("c")
    bf = lambda w: w[0].astype(BF16)

    c_slots, w_in_slots = _gather_all([c, bf(w_in)], name="gather_first_weights")
    c_all = c_slots.reshape(N_DEV * B, D)
    w_in_full = _slots_to_cols(w_in_slots)
    n_qkv = 3 * ATT_WIDTH
    W = dict(w_qkv=_pair_qkv_columns(w_in_full[:, :n_qkv]), w_us=w_in_full[:, n_qkv:n_qkv + SSM_WIDTH],
             w_gates=w_in_full[:, n_qkv + SSM_WIDTH:])

    n_ada = w_ada.shape[2]
    b_ada_cols = lax.dynamic_slice(b_ada, (0, me * n_ada), (1, n_ada))
    mod_part = _ada_fwd(c_all, w_ada[0], b_ada_cols)
    (mod_slots,) = _exchange([(mod_part.reshape(N_DEV, B, n_ada), True)], name="scatter_modulation")
    mod = mod_slots.transpose(1, 0, 2).reshape(B, 1, 6 * D)

    later = [bf(w_glu), bf(w_proj_att), bf(w_proj_ssm), bf(w_out), bf(w_up), w_conv[0], bf(w_down)]
    later_sems = _exchange_start(later, "start_later_weights", gather=True, carry=[mod])
    (mod,) = later_sems[5]

    def late_weights(after):
        _, lands = _exchange_wait(*later_sems[:4], after, name="wait_later_weights")
        g = [lax.dynamic_update_index_in_dim(land, a, me, 0) for land, a in zip(lands, later)]
        more_w = dict(w_glu=g[0].reshape(SSM_WIDTH, SSM_WIDTH), w_proj_att=_slots_to_cols(g[1]),
                      w_proj_ssm=_slots_to_cols(g[2]), w_out=g[3].reshape(D, D), w_up=g[4],
                      w_down=g[6].reshape(D_FF, D))
        return more_w, dict(w_conv=_slots_to_cols(g[5]))

    ab_re, ab_im, f_re, f_im = _s5_params(a_re[0], a_im[0], log_dt[0].reshape(SSM_GROUPS, 1))
    col = lambda a: a.reshape(SSM_COLS, 1)
    b_re2, b_im2 = b_re[0].reshape(SSM_COLS, SSM_GROUP_CH), b_im[0].reshape(SSM_COLS, SSM_GROUP_CH)
    bb_re, bb_im = _s5_input_matrix(col(f_re), col(f_im), b_re2, b_im2)
    slopes = jnp.asarray([2.0 ** (-8.0 * (h + 1) / N_HEADS) for h in range(N_HEADS)], F32)
    P = dict(g_mix=g_mix, g_ffn=g_ffn, g_final=g_final.reshape(1, D), b_gate=b_gate, d_skip=d_skip, b_glu=b_glu,
             b_conv=b_conv, slopes=slopes,
             a_row=_interleave(ab_re.reshape(1, SSM_COLS), ab_im.reshape(1, SSM_COLS)),
             bb_big=_interleave(_block_diag_in(bb_re), _block_diag_in(bb_im)),
             cc_big=_interleave(_block_diag_out(c_re[0]).T, -_block_diag_out(c_im[0]).T).T)

    in_flight = []

    def send_early(grads, carry=()):
        names = list(grads)
        handles = _exchange_start([grads[n] for n in names], "start_gradients_%d" % len(in_flight), gather=False,
                                  carry=carry)
        in_flight.append((names,) + handles[:4])
        return handles[4], handles[5]

    grad_x, d_mod, small = _local_step(x, mod, loss_target, W, late_weights, P, send_early)

    small_list = [small["loss_cols"], small["g_mix"], small["b_gate"], small["g_ab_re"], small["g_ab_im"],
                  _diag_blocks_in(small["d_bb_re"]), _diag_blocks_in(small["d_bb_im"]),
                  _diag_blocks_out(small["d_cc_re"]), -_diag_blocks_out(small["d_cc_im"]),
                  small["g_ffn"], small["b_conv"], small["g_final"], small["d_skip"], small["b_glu"]]
    small_packed, small_offs = _pack(small_list)
    small_all, dmod_slots = _gather_all([small_packed, d_mod.reshape(B, 6 * D)], name="gather_small_gradients")

    out = {}

    def update(name, parts, own=None):
        w2 = args[name][0]
        g, dl, mn, vn = _adamw(w2, args["m_" + name][0], args["v_" + name][0], parts, name="adamw_" + name, own=own)
        for key, val in (("grad_", g), ("delta_", dl), ("new_m_", mn), ("new_v_", vn)):
            out[key + name] = val[None]

    my_slot = me.astype(jnp.int32).reshape(1)
    for i, (names, send_sems, recv_sems, sent, lands) in enumerate(in_flight):
        sent, lands = _exchange_wait(send_sems, recv_sems, sent, lands, dmod_slots, name="wait_gradients_%d" % i)
        for name, own_slots, landed in zip(names, sent, lands):
            update(name, landed, own=(own_slots, my_slot))

    dmod_all = dmod_slots.reshape(N_DEV * B, 6 * D)
    dmod_cols = lax.dynamic_slice(dmod_all, (0, me * n_ada), (N_DEV * B, n_ada))
    d_w_ada, d_b_ada = _ada_bwd(c_all, dmod_all, dmod_cols)
    update("w_ada", d_w_ada[None])

    loss_row, loss_n = small_offs[0]
    small_sum, loss_vec = _sum_parts(small_all, (loss_row, loss_row + loss_n // LANES))
    shapes = [(1, D), (1, D), (1, 2 * D), (SSM_GROUPS, SSM_STATE), (SSM_GROUPS, SSM_STATE), (SSM_COLS, SSM_GROUP_CH),
              (SSM_COLS, SSM_GROUP_CH), (1, SSM_GROUPS, SSM_GROUP_CH, SSM_STATE),
              (1, SSM_GROUPS, SSM_GROUP_CH, SSM_STATE), (1, D), (1, D_FF), (D,), (1, SSM_WIDTH), (1, SSM_WIDTH)]
    (_, s_g_mix, s_b_gate, s_ab_re, s_ab_im, s_bb_re, s_bb_im, s_c_re, s_c_im, s_g_ffn, s_b_conv, s_g_final,
     s_d_skip, s_b_glu) = _unpack(small_sum, small_offs, shapes)
    d_b_re2, d_b_im2, d_f_re, d_f_im = _s5_input_matrix_bwd(col(f_re), col(f_im), b_re2, b_im2, s_bb_re, s_bb_im)
    d_a_re, d_a_im, d_log_dt = _s5_params_bwd(a_re[0], a_im[0], log_dt[0].reshape(SSM_GROUPS, 1), s_ab_re, s_ab_im,
                                              d_f_re.reshape(SSM_GROUPS, SSM_STATE),
                                              d_f_im.reshape(SSM_GROUPS, SSM_STATE))
    grads_small = dict(b_ada=d_b_ada, g_mix=s_g_mix, b_gate=s_b_gate, a_re=d_a_re[None], a_im=d_a_im[None],
                       log_dt=d_log_dt.reshape(1, SSM_GROUPS), b_re=d_b_re2.reshape(b_re.shape),
                       b_im=d_b_im2.reshape(b_im.shape), c_re=s_c_re, c_im=s_c_im, d_skip=s_d_skip, b_glu=s_b_glu,
                       g_ffn=s_g_ffn, b_conv=s_b_conv, g_final=s_g_final)
    flat2 = lambda a: a.reshape(-1, a.shape[-1])
    res = _adamw_small([flat2(args[n]) for n in SMALL_ORDER], [flat2(args["m_" + n]) for n in SMALL_ORDER],
                       [flat2(args["v_" + n]) for n in SMALL_ORDER],
                       [flat2(grads_small[n].reshape(args[n].shape)) for n in SMALL_ORDER])
    for i, n in enumerate(SMALL_ORDER):
        for k, key in enumerate(("grad_", "delta_", "new_m_", "new_v_")):
            out[key + n] = res[4 * i + k].reshape(args[n].shape)

    order = ["w_ada", "b_ada", "g_mix", "w_in", "b_gate", "a_re", "a_im", "log_dt", "b_re", "b_im", "c_re", "c_im",
             "d_skip", "w_glu", "b_glu", "w_proj_att", "w_proj_ssm", "w_out", "g_ffn", "w_up", "w_conv", "b_conv",
             "w_down", "g_final"]
    loss = loss_vec[0, 0]
    return (loss, grad_x, *[out[k + n] for k in ("grad_", "delta_", "new_m_", "new_v_") for n in order])
```

```python
import math

import jax
import jax.numpy as jnp
from jax import lax
from jax.experimental import pallas as pl
from jax.experimental.pallas import tpu as pltpu

F32 = jnp.float32
BF16 = jnp.bfloat16

N_DEV = 8
D_MODEL = 1024
N_HEADS = 8
HEAD_DIM = 64
ATT_WIDTH = N_HEADS * HEAD_DIM
DILATIONS = (1, 4, 16)
WIN = 128
SSM_GROUPS = 16
SSM_GROUP_CH = 16
SSM_WIDTH = SSM_GROUPS * SSM_GROUP_CH
SSM_STATE = 64
SSM_COLS = SSM_GROUPS * SSM_STATE
D_FF = 2048
EPS = 1e-6
NEG_INF = -1e30
ADAM_LR, ADAM_B1, ADAM_B2, ADAM_EPS, ADAM_WD, ADAM_STEP = 0.001, 0.9, 0.999, 1e-08, 0.01, 10

V7X_VMEM_LIMIT = 56 * 1024 * 1024
LANES = 128


def _params(n_grid):
    return pltpu.CompilerParams(dimension_semantics=("arbitrary",) * n_grid,
                                vmem_limit_bytes=V7X_VMEM_LIMIT)


def _tile(n, pref):
    if n <= pref:
        return n
    t = (pref // LANES) * LANES
    while t > 0:
        if n % t == 0:
            return t
        t -= LANES
    return n


def _matmul(a, b, *, ta=False, tb=False, out_dtype=F32, name):
    if ta:
        K, M = a.shape
    else:
        M, K = a.shape
    if tb:
        N, K2 = b.shape
    else:
        K2, N = b.shape
    assert K == K2, (a.shape, b.shape)
    if ta:
        tm, tn, tk = _tile(M, 1024), _tile(N, 2048), _tile(K, 1024)
    else:
        tm, tk = _tile(M, 512), _tile(K, 4096)
        tn = _tile(N, 2048 if K <= 2048 else 1024)
    nk = K // tk
    dn = (((0,) if ta else (1,), (1,) if tb else (0,)), ((), ()))

    def body(a_ref, b_ref, o_ref, acc_ref):
        k = pl.program_id(2)
        part = lax.dot_general(a_ref[...].astype(BF16), b_ref[...].astype(BF16), dn, preferred_element_type=F32)
        if nk == 1:
            o_ref[...] = part.astype(o_ref.dtype)
            return

        @pl.when(k == 0)
        def _():
            acc_ref[...] = jnp.zeros_like(acc_ref)

        acc_ref[...] += part

        @pl.when(k == nk - 1)
        def _():
            o_ref[...] = acc_ref[...].astype(o_ref.dtype)

    a_spec = (pl.BlockSpec((tk, tm), lambda j, i, k: (k, i)) if ta
              else pl.BlockSpec((tm, tk), lambda j, i, k: (i, k)))
    b_spec = (pl.BlockSpec((tn, tk), lambda j, i, k: (j, k)) if tb
              else pl.BlockSpec((tk, tn), lambda j, i, k: (k, j)))
    return pl.pallas_call(
        body, name=name, grid=(N // tn, M // tm, nk),
        in_specs=[a_spec, b_spec],
        out_specs=pl.BlockSpec((tm, tn), lambda j, i, k: (i, j)),
        out_shape=jax.ShapeDtypeStruct((M, N), out_dtype),
        scratch_shapes=[pltpu.VMEM((tm, tn) if nk > 1 else (8, LANES), F32)],
        compiler_params=_params(3),
    )(a, b)


def _project_in(a, weights_t, out_dtypes, name):
    M, K = a.shape
    tm = _tile(M, 512)
    n = len(weights_t)

    def body(a_ref, *refs):
        av = a_ref[...].astype(BF16)
        for w_ref, o_ref in zip(refs[:n], refs[n:]):
            o_ref[...] = lax.dot_general(av, w_ref[...].astype(BF16), _NT,
                                         preferred_element_type=F32).astype(o_ref.dtype)

    return pl.pallas_call(
        body, name=name, grid=(M // tm,),
        in_specs=[pl.BlockSpec((tm, K), lambda i: (i, 0))] + [pl.BlockSpec(w.shape, lambda i: (0, 0)) for w in weights_t],
        out_specs=[pl.BlockSpec((tm, w.shape[0]), lambda i: (i, 0)) for w in weights_t],
        out_shape=[jax.ShapeDtypeStruct((M, w.shape[0]), dt) for w, dt in zip(weights_t, out_dtypes)],
        compiler_params=_params(1),
    )(a, *weights_t)


def _project_back(ds, weights_t, name):
    M = ds[0].shape[0]
    K = weights_t[0].shape[1]
    tm = _tile(M, 512)
    n = len(ds)
    weights = weights_t

    def body(*refs):
        total = None
        for d_ref, w_ref in zip(refs[:n], refs[n:2 * n]):
            part = jnp.dot(d_ref[...].astype(BF16), w_ref[...].astype(BF16), preferred_element_type=F32)
            total = part if total is None else total + part
        refs[2 * n][...] = total.astype(refs[2 * n].dtype)

    return pl.pallas_call(
        body, name=name, grid=(M // tm,),
        in_specs=[pl.BlockSpec((tm, d.shape[1]), lambda i: (i, 0)) for d in ds]
        + [pl.BlockSpec(w.shape, lambda i: (0, 0)) for w in weights],
        out_specs=pl.BlockSpec((tm, K), lambda i: (i, 0)),
        out_shape=jax.ShapeDtypeStruct((M, K), BF16), compiler_params=_params(1),
    )(*ds, *weights)


HALF = 256
UP_SLOTS = N_DEV // 2
UP_GROUP = 4 * HALF


def _group_weight(w_ref):
    return jnp.concatenate([w_ref[0, :, :HALF], w_ref[1, :, :HALF], w_ref[0, :, HALF:], w_ref[1, :, HALF:]], axis=1)


def _up_weight_spec(K, index):
    return pl.BlockSpec((2, None, K, 2 * HALF), index)


def _up_fwd(a, w3, name):
    M, K = a.shape
    tm = _tile(M, 1024)

    def body(a_ref, w_ref, o_ref):
        o_ref[...] = jnp.dot(a_ref[...].astype(BF16), _group_weight(w_ref),
                             preferred_element_type=F32).astype(o_ref.dtype)

    return pl.pallas_call(
        body, name=name, grid=(UP_SLOTS, M // tm),
        in_specs=[pl.BlockSpec((tm, K), lambda j, i: (i, 0)), _up_weight_spec(K, lambda j, i: (0, j, 0, 0))],
        out_specs=pl.BlockSpec((tm, UP_GROUP), lambda j, i: (i, j)),
        out_shape=jax.ShapeDtypeStruct((M, UP_SLOTS * UP_GROUP), BF16), compiler_params=_params(2),
    )(a, w3.reshape(2, UP_SLOTS, K, 2 * HALF))


def _up_dx(d, w3, name):
    M = d.shape[0]
    K = w3.shape[1]
    tm = _tile(M, 1024)

    def body(d_ref, w_ref, o_ref, acc_ref):
        j = pl.program_id(1)

        @pl.when(j == 0)
        def _():
            acc_ref[...] = jnp.zeros_like(acc_ref)

        acc_ref[...] += lax.dot_general(d_ref[...], _group_weight(w_ref), _NT, preferred_element_type=F32)

        @pl.when(j == UP_SLOTS - 1)
        def _():
            o_ref[...] = acc_ref[...].astype(o_ref.dtype)

    return pl.pallas_call(
        body, name=name, grid=(M // tm, UP_SLOTS),
        in_specs=[pl.BlockSpec((tm, UP_GROUP), lambda i, j: (i, j)), _up_weight_spec(K, lambda i, j: (0, j, 0, 0))],
        out_specs=pl.BlockSpec((tm, K), lambda i, j: (i, 0)),
        out_shape=jax.ShapeDtypeStruct((M, K), BF16), scratch_shapes=[pltpu.VMEM((tm, K), F32)],
        compiler_params=_params(2),
    )(d, w3.reshape(2, UP_SLOTS, K, 2 * HALF))


def _up_dw(a, d, name):
    M, K = a.shape
    tk = _tile(M, 1024)
    nk = M // tk

    def body(a_ref, d_ref, o_ref, acc_ref):
        k = pl.program_id(1)

        @pl.when(k == 0)
        def _():
            acc_ref[...] = jnp.zeros_like(acc_ref)

        acc_ref[...] += lax.dot_general(a_ref[...], d_ref[...], _TN, preferred_element_type=F32)

        @pl.when(k == nk - 1)
        def _():
            for half in range(2):
                for part in range(2):
                    lo = (2 * half + part) * HALF
                    o_ref[part, :, half * HALF:(half + 1) * HALF] = acc_ref[:, lo:lo + HALF].astype(o_ref.dtype)

    out = pl.pallas_call(
        body, name=name, grid=(UP_SLOTS, nk),
        in_specs=[pl.BlockSpec((tk, K), lambda j, k: (k, 0)), pl.BlockSpec((tk, UP_GROUP), lambda j, k: (k, j))],
        out_specs=_up_weight_spec(K, lambda j, k: (0, j, 0, 0)),
        out_shape=jax.ShapeDtypeStruct((2, UP_SLOTS, K, 2 * HALF), BF16),
        scratch_shapes=[pltpu.VMEM((K, UP_GROUP), F32)], compiler_params=_params(2),
    )(a, d)
    return out.reshape(N_DEV, K, 2 * HALF)


def _rowwise(fn, rows, bvecs, consts, out_rows, out_b, out_g, *, ts, name):
    B, S = rows[0][0].shape[:2]
    nin = len(rows) + len(bvecs) + len(consts)
    nr, nb, ng = len(out_rows), len(out_b), len(out_g)

    def body(*refs):
        b = pl.program_id(0)
        s = pl.program_id(1)
        vals = [r[...] for r in refs[:nin]]
        vals[:len(rows)] = [v.astype(F32) for v in vals[:len(rows)]]
        outs = fn(*vals)
        if not isinstance(outs, (tuple, list)):
            outs = (outs,)
        orefs = refs[nin:]
        for i in range(nr):
            orefs[i][...] = outs[i].astype(orefs[i].dtype)
        for i in range(nb):
            ref = orefs[nr + i]

            @pl.when(s == 0)
            def _(ref=ref):
                ref[...] = jnp.zeros_like(ref)

            ref[...] += outs[nr + i]
        for i in range(ng):
            ref = orefs[nr + nb + i]

            @pl.when((s == 0) & (b == 0))
            def _(ref=ref):
                ref[...] = jnp.zeros_like(ref)

            ref[...] += outs[nr + nb + i]

    rows = [r if len(r) == 4 else r + (0,) for r in rows]
    in_specs = ([pl.BlockSpec((None, ts, cb), lambda b, s, ci=ci, b0=b0: (b0 + b, s, ci)) for (_, cb, ci, b0) in rows]
                + [pl.BlockSpec((None, 1, cb), lambda b, s, ci=ci: (b, 0, ci)) for (_, cb, ci) in bvecs]
                + [pl.BlockSpec(a.shape, lambda b, s: (0, 0)) for a in consts])
    out_shape = ([jax.ShapeDtypeStruct((B, S, c), dt) for (c, dt) in out_rows]
                 + [jax.ShapeDtypeStruct((B, 1, c), F32) for c in out_b]
                 + [jax.ShapeDtypeStruct(rc, F32) for rc in out_g])
    out_specs = ([pl.BlockSpec((None, ts, c), lambda b, s: (b, s, 0)) for (c, _) in out_rows]
                 + [pl.BlockSpec((None, 1, c), lambda b, s: (b, 0, 0)) for c in out_b]
                 + [pl.BlockSpec(rc, lambda b, s: (0, 0)) for rc in out_g])
    args = [r[0] for r in rows] + [a for (a, _, _) in bvecs] + list(consts)
    return pl.pallas_call(
        body, name=name, grid=(B, S // ts), in_specs=in_specs, out_specs=out_specs,
        out_shape=out_shape, compiler_params=_params(2),
    )(*args)


def _col_sum(v):
    return jnp.sum(v, axis=0, keepdims=True)


def _rms_scale(h):
    return lax.rsqrt(jnp.mean(h * h, axis=-1, keepdims=True) + EPS)


def _sigmoid(v):
    return 0.5 * (1.0 + jnp.tanh(0.5 * v))


ATT_SCALE = HEAD_DIM ** -0.5
COPY_ROWS = 256
_NT = (((1,), (1,)), ((), ()))
_TN = (((0,), (0,)), ((), ()))


def _row_chunks(d, seq):
    sub = seq // d
    out = []
    for r in range(d):
        for c0 in range(0, sub, COPY_ROWS):
            n = min(COPY_ROWS, sub - c0)
            out.append((pl.ds(r + c0 * d, n, stride=d), r * sub + c0, n))
    return out


ATT_UNROLL = 8
KEYS = 2 * WIN


def _zero_once(refs):
    @pl.when((pl.program_id(0) == 0) & (pl.program_id(1) == 0))
    def _():
        for r in refs:
            r[...] = jnp.zeros_like(r)


def _pair_bias(bias_ref, slopes_ref, hp, d, key_major):
    shape = (KEYS, WIN) if key_major else (WIN, KEYS)
    qi = lax.broadcasted_iota(jnp.int32, shape, 1 if key_major else 0)
    kj = lax.broadcasted_iota(jnp.int32, shape, 0 if key_major else 1)
    dist = WIN + qi - kj
    valid = (dist >= 0) & (dist <= WIN)
    distf = dist.astype(F32)
    for h in range(2):
        slope_d = slopes_ref[2 * hp + h] * float(d)
        with_prev = jnp.where(valid, -(slope_d * distf), NEG_INF)
        no_prev = jnp.where(kj >= WIN, with_prev, NEG_INF)
        span = slice(h * KEYS, (h + 1) * KEYS)
        if key_major:
            bias_ref[1, span, :] = with_prev
            bias_ref[0, span, :] = no_prev
        else:
            bias_ref[1, :, span] = with_prev
            bias_ref[0, :, span] = no_prev


def _stack_heads(v):
    first = lax.broadcasted_iota(jnp.int32, v.shape, 1) < HEAD_DIM
    zero = jnp.zeros_like(v)
    return jnp.concatenate([jnp.where(first, v, zero), jnp.where(first, zero, v)], axis=0)


def _per_head(c0, c1, n):
    return jnp.where(lax.broadcasted_iota(jnp.int32, (n, LANES), 1) < HEAD_DIM, c0, c1)


def _qkv_spec(seq, j):
    return pl.BlockSpec((None, seq, LANES), lambda b, hp: (b, 0, 3 * hp + j))


def _attention_fwd(qkv, slopes):
    B, S, _ = qkv.shape
    n_blk = S // WIN
    n_pair = N_HEADS // 2

    def body(slopes_ref, q_ref, k_ref, v_ref, o_ref, lse_ref, qp, kp, vp, bias, acc, mx, sm, acc_n, mx_n, sm_n):
        hp = pl.program_id(1)
        _zero_once((kp, vp))
        for p, d in enumerate(DILATIONS):
            nb = n_blk // d
            chunks = _row_chunks(d, S)
            for src, dst, n in chunks:
                qp[dst:dst + n, :] = (q_ref[src, :] * ATT_SCALE).astype(BF16)
                kp[WIN + dst:WIN + dst + n, :] = k_ref[src, :].astype(BF16)
                vp[WIN + dst:WIN + dst + n, :] = v_ref[src, :].astype(BF16)
            _pair_bias(bias, slopes_ref, hp, d, key_major=False)
            acc_t, mx_t, sm_t = (acc_n, mx_n, sm_n) if d == 1 else (acc, mx, sm)

            nk = WIN if nb == 1 else KEYS
            bias_cur = jnp.concatenate([bias[0, :, WIN:KEYS], bias[0, :, KEYS + WIN:]], axis=1) if nb == 1 else None

            def block(i, carry, p=p, nb=nb, nk=nk, bias_cur=bias_cur, acc_t=acc_t, mx_t=mx_t, sm_t=sm_t):
                cur = pl.ds(pl.multiple_of(i * WIN, WIN), WIN)
                keys = pl.ds(pl.multiple_of(i * WIN + (KEYS - nk), WIN), nk)
                s = lax.dot_general(qp[cur, :], _stack_heads(kp[keys, :]), _NT, preferred_element_type=F32)
                s = s + (bias_cur if nb == 1 else bias[((i % nb) > 0).astype(jnp.int32)])
                es, ms, ls = [], [], []
                for h in range(2):
                    sh = s[:, h * nk:(h + 1) * nk]
                    m = jnp.max(sh if nb == 1 else jnp.maximum(sh[:, :WIN], sh[:, WIN:]), axis=1, keepdims=True)
                    e = jnp.exp(sh - m)
                    es.append(e.astype(BF16))
                    ms.append(m)
                    ls.append(jnp.sum(e if nb == 1 else e[:, :WIN] + e[:, WIN:], axis=1, keepdims=True))
                acc_t[p, cur, :] = jnp.dot(jnp.concatenate(es, axis=1), _stack_heads(vp[keys, :]),
                                           preferred_element_type=F32)
                mx_t[p, cur, :] = _per_head(ms[0], ms[1], WIN)
                sm_t[p, cur, :] = _per_head(ls[0], ls[1], WIN)
                return carry

            lax.fori_loop(0, n_blk, block, 0, unroll=ATT_UNROLL)
            if d > 1:
                for src, dst, n in chunks:
                    acc_n[p, src, :] = acc[p, dst:dst + n, :]
                    mx_n[p, src, :] = mx[p, dst:dst + n, :]
                    sm_n[p, src, :] = sm[p, dst:dst + n, :]

        chunk = 256

        def merge(i, carry):
            rows = pl.ds(pl.multiple_of(i * chunk, chunk), chunk)
            ms = [mx_n[p, rows, :] for p in range(3)]
            m = jnp.maximum(jnp.maximum(ms[0], ms[1]), ms[2])
            ws = [jnp.exp(mp - m) for mp in ms]
            l = ws[0] * sm_n[0, rows, :] + ws[1] * sm_n[1, rows, :] + ws[2] * sm_n[2, rows, :]
            o = (ws[0] * acc_n[0, rows, :] + ws[1] * acc_n[1, rows, :] + ws[2] * acc_n[2, rows, :]) / l
            o_ref[rows, :] = o.astype(o_ref.dtype)
            lse = m + jnp.log(l)
            for h in range(2):
                lse_ref[rows, h:h + 1] = lse[:, h * HEAD_DIM:h * HEAD_DIM + 1]
            return carry

        lax.fori_loop(0, S // chunk, merge, 0)

    return pl.pallas_call(
        body, name="attention_fwd", grid=(B, n_pair),
        in_specs=[pl.BlockSpec(memory_space=pltpu.SMEM), _qkv_spec(S, 0), _qkv_spec(S, 1), _qkv_spec(S, 2)],
        out_specs=[pl.BlockSpec((None, S, LANES), lambda b, hp: (b, 0, hp)),
                   pl.BlockSpec((None, None, S, 2), lambda b, hp: (b, hp, 0, 0))],
        out_shape=[jax.ShapeDtypeStruct((B, S, ATT_WIDTH), BF16),
                   jax.ShapeDtypeStruct((B, n_pair, S, 2), F32)],
        scratch_shapes=[pltpu.VMEM((S, LANES), BF16), pltpu.VMEM((S + WIN, LANES), BF16),
                        pltpu.VMEM((S + WIN, LANES), BF16), pltpu.VMEM((2, WIN, 2 * KEYS), F32)]
        + [pltpu.VMEM((3, S, LANES), F32)] * 6,
        compiler_params=_params(2),
    )(slopes, qkv, qkv, qkv)


def _attention_bwd(qkv, o, do, lse, slopes):
    B, S, _ = qkv.shape
    n_blk = S // WIN
    n_pair = N_HEADS // 2

    def body(slopes_ref, q_ref, k_ref, v_ref, o_ref, do_ref, lse_ref, dx_ref,
             qp, dop, kp, vp, aux, auxp, aux_t, bias_t, dqp, dvk, dq_n, dk_n, dv_n):
        hp = pl.program_id(1)
        aux[...] = jnp.zeros_like(aux)
        for c0 in range(0, S, COPY_ROWS):
            rows = slice(c0, c0 + COPY_ROWS)
            prod = do_ref[rows, :] * o_ref[rows, :].astype(F32)
            for h in range(2):
                aux[rows, 2 * h:2 * h + 1] = lse_ref[rows, h:h + 1]
                aux[rows, 2 * h + 1:2 * h + 2] = jnp.sum(prod[:, h * HEAD_DIM:(h + 1) * HEAD_DIM], axis=1,
                                                         keepdims=True)
        dq_n[...] = jnp.zeros_like(dq_n)
        dk_n[...] = jnp.zeros_like(dk_n)
        dv_n[...] = jnp.zeros_like(dv_n)
        _zero_once((kp, vp))
        for p, d in enumerate(DILATIONS):
            nb = n_blk // d
            chunks = _row_chunks(d, S)
            for src, dst, n in chunks:
                auxp[dst:dst + n, :] = aux[src, :]
                qp[dst:dst + n, :] = (q_ref[src, :] * ATT_SCALE).astype(BF16)
                dop[dst:dst + n, :] = do_ref[src, :].astype(BF16)
                kp[WIN + dst:WIN + dst + n, :] = k_ref[src, :].astype(BF16)
                vp[WIN + dst:WIN + dst + n, :] = v_ref[src, :].astype(BF16)
            for i in range(n_blk):
                aux_t[i] = auxp[i * WIN:(i + 1) * WIN, :].T[0:8, :]
            _pair_bias(bias_t, slopes_ref, hp, d, key_major=True)
            dvk[...] = jnp.zeros_like(dvk)

            nk = WIN if nb == 1 else KEYS
            bias_cur = jnp.concatenate([bias_t[0, WIN:KEYS, :], bias_t[0, KEYS + WIN:, :]], axis=0) if nb == 1 else None

            def block(i, carry, nb=nb, nk=nk, bias_cur=bias_cur):
                cur = pl.ds(pl.multiple_of(i * WIN, WIN), WIN)
                keys = pl.ds(pl.multiple_of(i * WIN + (KEYS - nk), WIN), nk)
                q2, do2 = qp[cur, :], dop[cur, :]
                kc = _stack_heads(kp[keys, :])
                s_t = lax.dot_general(kc, q2, _NT, preferred_element_type=F32)
                s_t = s_t + (bias_cur if nb == 1 else bias_t[((i % nb) > 0).astype(jnp.int32)])
                dp_t = lax.dot_general(_stack_heads(vp[keys, :]), do2, _NT, preferred_element_type=F32)
                ps, dss = [], []
                for h in range(2):
                    span = slice(h * nk, (h + 1) * nk)
                    p_t = jnp.exp(s_t[span, :] - aux_t[i, 2 * h:2 * h + 1, :])
                    ds_t = p_t * (dp_t[span, :] - aux_t[i, 2 * h + 1:2 * h + 2, :])
                    ps.append(p_t.astype(BF16))
                    dss.append(ds_t.astype(BF16))
                do_rows, q_rows = _stack_heads(do2), _stack_heads(q2)
                zr = jnp.zeros_like(do_rows)
                rhs = jnp.concatenate([jnp.concatenate([do_rows, zr], axis=1),
                                       jnp.concatenate([zr, q_rows], axis=1)], axis=0)
                dvk[keys, :] += jnp.dot(jnp.concatenate(ps + dss, axis=1), rhs, preferred_element_type=F32)
                dqp[cur, :] = lax.dot_general(jnp.concatenate(dss, axis=0), kc, _TN, preferred_element_type=F32)
                return carry

            lax.fori_loop(0, n_blk, block, 0, unroll=ATT_UNROLL)
            for src, dst, n in chunks:
                dq_n[src, :] += dqp[dst:dst + n, :]
                dv_n[src, :] += dvk[WIN + dst:WIN + dst + n, :LANES]
                dk_n[src, :] += dvk[WIN + dst:WIN + dst + n, LANES:]
        for c0 in range(0, S, COPY_ROWS):
            rows = slice(c0, c0 + COPY_ROWS)
            dx_ref[rows, 0:LANES] = (dq_n[rows, :] * ATT_SCALE).astype(dx_ref.dtype)
            dx_ref[rows, LANES:2 * LANES] = dk_n[rows, :].astype(dx_ref.dtype)
            dx_ref[rows, 2 * LANES:3 * LANES] = dv_n[rows, :].astype(dx_ref.dtype)

    pair = lambda width: pl.BlockSpec((None, S, width), lambda b, hp: (b, 0, hp))
    vm = lambda shape, dt: pltpu.VMEM(shape, dt)
    return pl.pallas_call(
        body, name="attention_bwd", grid=(B, n_pair),
        in_specs=[pl.BlockSpec(memory_space=pltpu.SMEM), _qkv_spec(S, 0), _qkv_spec(S, 1), _qkv_spec(S, 2),
                  pair(LANES), pair(LANES), pl.BlockSpec((None, None, S, 2), lambda b, hp: (b, hp, 0, 0))],
        out_specs=pair(3 * LANES),
        out_shape=jax.ShapeDtypeStruct((B, S, 3 * ATT_WIDTH), BF16),
        scratch_shapes=[vm((S, LANES), BF16), vm((S, LANES), BF16),
                        vm((S + WIN, LANES), BF16), vm((S + WIN, LANES), BF16),
                        vm((S, LANES), F32), vm((S, LANES), F32), vm((n_blk, 8, WIN), F32),
                        vm((2, 2 * KEYS, WIN), F32),
                        vm((S, LANES), F32), vm((S + WIN, 2 * LANES), F32),
                        vm((S, LANES), F32), vm((S, LANES), F32), vm((S, LANES), F32)],
        compiler_params=_params(2),
    )(slopes, qkv, qkv, qkv, o, do, lse)


SCAN_COLS = 256
SCAN_ROWS = 8


def _rows_to_tile(rows):
    rid = lax.broadcasted_iota(jnp.int32, (SCAN_ROWS, rows[0].shape[1]), 0)
    tile = jnp.broadcast_to(rows[0], rid.shape)
    for k in range(1, SCAN_ROWS):
        tile = jnp.where(rid == k, rows[k], tile)
    return tile


SCAN_UNROLL = 4


def _complex_powers(ar, ai, n):
    out = [(ar, ai)]
    for _ in range(n - 1):
        pr, pi = out[-1]
        out.append((pr * ar - pi * ai, pr * ai + pi * ar))
    return out


def _round_multipliers(powers, rid, reverse):
    out = []
    for s in (1, 2, 4):
        keep = (rid < SCAN_ROWS - s) if reverse else (rid >= s)
        out.append((jnp.where(keep, powers[s - 1][0], 0.0), jnp.where(keep, powers[s - 1][1], 0.0)))
    return out


def _tile_scan(xr, xi, multipliers, reverse):
    for s, (mr, mi) in zip((1, 2, 4), multipliers):
        shift = SCAN_ROWS - s if reverse else s
        sr, si = pltpu.roll(xr, shift, 0), pltpu.roll(xi, shift, 0)
        xr, xi = xr + (mr * sr - mi * si), xi + (mr * si + mi * sr)
    return xr, xi


SCAN_CHUNK = 256


def _scan_fwd(us, bb_big, a_row, cc_big):
    B, S, _ = us.shape
    groups = 2
    width = 2 * groups * SCAN_COLS
    nc = 2 * SSM_COLS // width
    nt = S // SCAN_ROWS
    tiles = SCAN_CHUNK // SCAN_ROWS
    LAST = slice(SCAN_ROWS - 1, SCAN_ROWS)

    def body(us_ref, bb_ref, a_ref, cc_ref, xs_ref, y_ref, bu_ref):
        bb = bb_ref[...].astype(BF16)
        for c in range(S // SCAN_CHUNK):
            part = jnp.dot(us_ref[c * SCAN_CHUNK:(c + 1) * SCAN_CHUNK, :].astype(BF16), bb,
                           preferred_element_type=F32)
            bu_ref[c * tiles:(c + 1) * tiles] = part.reshape(tiles, SCAN_ROWS, width)
        rid = lax.broadcasted_iota(jnp.int32, (SCAN_ROWS, SCAN_COLS), 0)
        consts = []
        for g in range(groups):
            re = slice(2 * g * SCAN_COLS, (2 * g + 1) * SCAN_COLS)
            im = slice((2 * g + 1) * SCAN_COLS, (2 * g + 2) * SCAN_COLS)
            powers = _complex_powers(a_ref[:, re], a_ref[:, im], SCAN_ROWS)
            carry_mult = (_rows_to_tile([p[0] for p in powers]), _rows_to_tile([p[1] for p in powers]))
            consts.append((re, im, carry_mult, _round_multipliers(powers, rid, reverse=False)))

        def tile(i, carry):
            out = []
            for (re, im, (cr_t, ci_t), rounds), (cr, ci) in zip(consts, carry):
                xr, xi = _tile_scan(bu_ref[i, :, re], bu_ref[i, :, im], rounds, reverse=False)
                xs_ref[i, :, re] = xr + (cr_t * cr - ci_t * ci)
                xs_ref[i, :, im] = xi + (cr_t * ci + ci_t * cr)
                out.append((xs_ref[i, LAST, re], xs_ref[i, LAST, im]))
            return tuple(out)

        zero = jnp.zeros((1, SCAN_COLS), F32)
        lax.fori_loop(0, nt, tile, ((zero, zero),) * groups, unroll=SCAN_UNROLL)

        @pl.when(pl.program_id(1) == 0)
        def _():
            y_ref[...] = jnp.zeros_like(y_ref)

        cc = cc_ref[...].astype(BF16)
        for c in range(S // SCAN_CHUNK):
            x2 = xs_ref[c * tiles:(c + 1) * tiles].reshape(SCAN_CHUNK, width).astype(BF16)
            y_ref[c * SCAN_CHUNK:(c + 1) * SCAN_CHUNK, :] += jnp.dot(x2, cc, preferred_element_type=F32)

    col = pl.BlockSpec((None, nt, SCAN_ROWS, width), lambda b, j: (b, 0, 0, j))
    tok = pl.BlockSpec((None, S, SSM_WIDTH), lambda b, j: (b, 0, 0))
    xs, y = pl.pallas_call(
        body, name="s5_scan_fwd", grid=(B, nc),
        in_specs=[tok, pl.BlockSpec((SSM_WIDTH, width), lambda b, j: (0, j)),
                  pl.BlockSpec((1, width), lambda b, j: (0, j)), pl.BlockSpec((width, SSM_WIDTH), lambda b, j: (j, 0))],
        out_specs=[col, tok],
        out_shape=[jax.ShapeDtypeStruct((B, nt, SCAN_ROWS, 2 * SSM_COLS), F32),
                   jax.ShapeDtypeStruct((B, S, SSM_WIDTH), F32)],
        scratch_shapes=[pltpu.VMEM((nt, SCAN_ROWS, width), F32)],
        compiler_params=_params(2),
    )(us, bb_big, a_row, cc_big)
    return xs.reshape(B, S, 2 * SSM_COLS), y


def _scan_bwd(dy, us, bb_big, cc_big, xs, a_row):
    B, S, _ = dy.shape
    width = 2 * SCAN_COLS
    nc = SSM_COLS // SCAN_COLS
    nt = S // SCAN_ROWS
    tiles = SCAN_CHUNK // SCAN_ROWS
    RE, IM = slice(0, SCAN_COLS), slice(SCAN_COLS, 2 * SCAN_COLS)
    FIRST, LAST = slice(0, 1), slice(SCAN_ROWS - 1, SCAN_ROWS)

    def body(dy_ref, us_ref, bb_ref, cc_ref, x_ref, a_ref, dus_ref, ga_ref, dbb_ref, dcc_ref, d_ref, lam_ref):
        b = pl.program_id(1)
        cc = cc_ref[...].astype(BF16)
        for c in range(S // SCAN_CHUNK):
            part = lax.dot_general(dy_ref[c * SCAN_CHUNK:(c + 1) * SCAN_CHUNK, :].astype(BF16), cc, _NT,
                                   preferred_element_type=F32)
            d_ref[c * tiles:(c + 1) * tiles] = part.reshape(tiles, SCAN_ROWS, width)
        powers = _complex_powers(a_ref[:, RE], -a_ref[:, IM], SCAN_ROWS)
        rid = lax.broadcasted_iota(jnp.int32, (SCAN_ROWS, SCAN_COLS), 0)
        cr_t = _rows_to_tile([powers[SCAN_ROWS - 1 - r][0] for r in range(SCAN_ROWS)])
        ci_t = _rows_to_tile([powers[SCAN_ROWS - 1 - r][1] for r in range(SCAN_ROWS)])
        rounds = _round_multipliers(powers, rid, reverse=True)

        @pl.when(b == 0)
        def _():
            ga_ref[...] = jnp.zeros_like(ga_ref)
            dbb_ref[...] = jnp.zeros_like(dbb_ref)
            dcc_ref[...] = jnp.zeros_like(dcc_ref)

        def tile(j, carry):
            cr, ci, accr, acci = carry
            i = nt - 1 - j
            lr, li = _tile_scan(d_ref[i, :, RE], d_ref[i, :, IM], rounds, reverse=True)
            lam_r = lr + (cr_t * cr - ci_t * ci)
            lam_i = li + (cr_t * ci + ci_t * cr)
            lam_ref[i, :, RE] = lam_r
            lam_ref[i, :, IM] = lam_i
            ip = jnp.maximum(i - 1, 0)
            keep = (i > 0).astype(F32)
            xpr = jnp.where(rid == 0, x_ref[ip, LAST, RE] * keep, pltpu.roll(x_ref[i, :, RE], 1, 0))
            xpi = jnp.where(rid == 0, x_ref[ip, LAST, IM] * keep, pltpu.roll(x_ref[i, :, IM], 1, 0))
            accr = accr + lam_r * xpr + lam_i * xpi
            acci = acci + lam_i * xpr - lam_r * xpi
            return lam_ref[i, FIRST, RE], lam_ref[i, FIRST, IM], accr, acci

        z1 = jnp.zeros((1, SCAN_COLS), F32)
        z8 = jnp.zeros((SCAN_ROWS, SCAN_COLS), F32)
        _, _, accr, acci = lax.fori_loop(0, nt, tile, (z1, z1, z8, z8), unroll=SCAN_UNROLL)
        ga_ref[:, RE] += _col_sum(accr)
        ga_ref[:, IM] += _col_sum(acci)

        bb = bb_ref[...].astype(BF16)
        for c in range(S // SCAN_CHUNK):
            rows = slice(c * SCAN_CHUNK, (c + 1) * SCAN_CHUNK)
            lam2 = lam_ref[c * tiles:(c + 1) * tiles].reshape(SCAN_CHUNK, width).astype(BF16)
            x2 = x_ref[c * tiles:(c + 1) * tiles].reshape(SCAN_CHUNK, width).astype(BF16)
            dus_ref[rows, :] = lax.dot_general(lam2, bb, _NT, preferred_element_type=F32)
            dbb_ref[...] += lax.dot_general(us_ref[rows, :].astype(BF16), lam2, _TN, preferred_element_type=F32)
            dcc_ref[...] += lax.dot_general(x2, dy_ref[rows, :].astype(BF16), _TN, preferred_element_type=F32)

    col = pl.BlockSpec((None, nt, SCAN_ROWS, width), lambda j, b: (b, 0, 0, j))
    tok = pl.BlockSpec((None, S, SSM_WIDTH), lambda j, b: (b, 0, 0))
    scratch = pltpu.VMEM((nt, SCAN_ROWS, width), F32)
    return pl.pallas_call(
        body, name="s5_scan_bwd", grid=(nc, B),
        in_specs=[tok, tok, pl.BlockSpec((SSM_WIDTH, width), lambda j, b: (0, j)),
                  pl.BlockSpec((width, SSM_WIDTH), lambda j, b: (j, 0)), col,
                  pl.BlockSpec((1, width), lambda j, b: (0, j))],
        out_specs=[pl.BlockSpec((None, None, S, SSM_WIDTH), lambda j, b: (j, b, 0, 0)),
                   pl.BlockSpec((1, width), lambda j, b: (0, j)),
                   pl.BlockSpec((SSM_WIDTH, width), lambda j, b: (0, j)),
                   pl.BlockSpec((width, SSM_WIDTH), lambda j, b: (j, 0))],
        out_shape=[jax.ShapeDtypeStruct((nc, B, S, SSM_WIDTH), F32), jax.ShapeDtypeStruct((1, 2 * SSM_COLS), F32),
                   jax.ShapeDtypeStruct((SSM_WIDTH, 2 * SSM_COLS), F32),
                   jax.ShapeDtypeStruct((2 * SSM_COLS, SSM_WIDTH), F32)],
        scratch_shapes=[scratch, scratch],
        compiler_params=_params(2),
    )(dy, us, bb_big, cc_big, xs.reshape(B, nt, SCAN_ROWS, 2 * SSM_COLS), a_row)


def _s5_discretise(lr, li, log_dt):
    dt = jnp.exp(log_dt)
    mag = jnp.exp(lr * dt)
    ang = li * dt
    ab_re, ab_im = mag * jnp.cos(ang), mag * jnp.sin(ang)
    nr, ni = ab_re - 1.0, ab_im
    den = lr * lr + li * li
    f_re = (nr * lr + ni * li) / den
    f_im = (ni * lr - nr * li) / den
    return dt, ab_re, ab_im, nr, ni, den, f_re, f_im


def _s5_params(a_re, a_im, log_dt):
    def body(lr_ref, li_ref, ld_ref, abr, abi, fr, fi):
        _, ab_re, ab_im, _, _, _, f_re, f_im = _s5_discretise(lr_ref[...], li_ref[...], ld_ref[...])
        abr[...] = ab_re
        abi[...] = ab_im
        fr[...] = f_re
        fi[...] = f_im

    return pl.pallas_call(body, name="s5_params",
                          out_shape=[jax.ShapeDtypeStruct(a_re.shape, F32)] * 4)(a_re, a_im, log_dt)


def _s5_input_matrix(f_re, f_im, b_re, b_im):
    def body(fr, fi, br, bi, o_re, o_im):
        o_re[...] = fr[...] * br[...] - fi[...] * bi[...]
        o_im[...] = fr[...] * bi[...] + fi[...] * br[...]

    return pl.pallas_call(body, name="s5_input_matrix",
                          out_shape=[jax.ShapeDtypeStruct(b_re.shape, F32)] * 2)(f_re, f_im, b_re, b_im)


def _s5_input_matrix_bwd(f_re, f_im, b_re, b_im, g_re, g_im):
    def body(fr, fi, br, bi, gr, gi, dbr, dbi, dfr, dfi):
        dbr[...] = fr[...] * gr[...] + fi[...] * gi[...]
        dbi[...] = fr[...] * gi[...] - fi[...] * gr[...]
        dfr[...] = jnp.sum(br[...] * gr[...] + bi[...] * gi[...], axis=1, keepdims=True)
        dfi[...] = jnp.sum(br[...] * gi[...] - bi[...] * gr[...], axis=1, keepdims=True)

    return pl.pallas_call(
        body, name="s5_input_matrix_bwd",
        out_shape=[jax.ShapeDtypeStruct(b_re.shape, F32)] * 2 + [jax.ShapeDtypeStruct(f_re.shape, F32)] * 2,
    )(f_re, f_im, b_re, b_im, g_re, g_im)


def _s5_params_bwd(a_re, a_im, log_dt, g_ab_re, g_ab_im, d_f_re, d_f_im):
    def body(lr_ref, li_ref, ld_ref, gar, gai, dfr, dfi, o_lr, o_li, o_ld):
        lr, li = lr_ref[...], li_ref[...]
        dt, ab_re, ab_im, nr, ni, den, f_re, f_im = _s5_discretise(lr, li, ld_ref[...])
        d_fr, d_fi = dfr[...], dfi[...]
        d_nr = (d_fr * lr - d_fi * li) / den
        d_ni = (d_fr * li + d_fi * lr) / den
        common = (d_fr * f_re + d_fi * f_im) * 2.0 / den
        d_lr = (d_fr * nr + d_fi * ni) / den - common * lr
        d_li = (d_fr * ni - d_fi * nr) / den - common * li
        d_abr = gar[...] + d_nr
        d_abi = gai[...] + d_ni
        d_mag_mag = d_abr * ab_re + d_abi * ab_im
        d_ang = d_abi * ab_re - d_abr * ab_im
        o_lr[...] = d_lr + d_mag_mag * dt
        o_li[...] = d_li + d_ang * dt
        o_ld[...] = jnp.sum(d_mag_mag * lr + d_ang * li, axis=1, keepdims=True) * dt

    return pl.pallas_call(
        body, name="s5_params_bwd",
        out_shape=[jax.ShapeDtypeStruct(a_re.shape, F32)] * 2 + [jax.ShapeDtypeStruct(log_dt.shape, F32)],
    )(a_re, a_im, log_dt, g_ab_re, g_ab_im, d_f_re, d_f_im)


CONV_COLS = 256


def _shift_down(v, j, row):
    return jnp.where(row >= j, pltpu.roll(v, j, 0), 0.0)


def _shift_up(v, j, row, seq):
    return jnp.where(row < seq - j, pltpu.roll(v, seq - j, 0), 0.0)


def _conv_fwd(up, w_conv, b_conv):
    B, S, _ = up.shape
    nj = D_FF // CONV_COLS

    def body(up_ref, w_ref, b_ref, ff_ref):
        a = up_ref[:, :CONV_COLS].astype(F32)
        val = up_ref[:, CONV_COLS:].astype(F32)
        row = lax.broadcasted_iota(jnp.int32, a.shape, 0)
        w0, w1, w2 = w_ref[0:1, :], w_ref[1:2, :], w_ref[2:3, :]
        conv = b_ref[...] + w0 * a + w1 * _shift_down(a, 1, row) + w2 * _shift_down(a, 2, row)
        ff_ref[...] = (conv * _sigmoid(conv) * val).astype(ff_ref.dtype)

    return pl.pallas_call(
        body, name="conv_gate_fwd", grid=(B, nj),
        in_specs=[pl.BlockSpec((None, S, 2 * CONV_COLS), lambda b, j: (b, 0, j)),
                  pl.BlockSpec((3, CONV_COLS), lambda b, j: (0, j)),
                  pl.BlockSpec((1, CONV_COLS), lambda b, j: (0, j))],
        out_specs=pl.BlockSpec((None, S, CONV_COLS), lambda b, j: (b, 0, j)),
        out_shape=jax.ShapeDtypeStruct((B, S, D_FF), BF16),
        compiler_params=_params(2),
    )(up, w_conv, b_conv)


def _conv_bwd(up, d_ff, w_conv, b_conv):
    B, S, _ = up.shape
    nj = D_FF // CONV_COLS

    def body(up_ref, dff_ref, w_ref, b_ref, dup_ref, dw_ref, db_ref):
        b = pl.program_id(1)
        a = up_ref[:, :CONV_COLS].astype(F32)
        val = up_ref[:, CONV_COLS:].astype(F32)
        row = lax.broadcasted_iota(jnp.int32, a.shape, 0)
        w0, w1, w2 = w_ref[0:1, :], w_ref[1:2, :], w_ref[2:3, :]
        a1, a2 = _shift_down(a, 1, row), _shift_down(a, 2, row)
        conv = b_ref[...] + w0 * a + w1 * a1 + w2 * a2
        sg = _sigmoid(conv)
        dff = dff_ref[...].astype(F32)
        d_val = dff * conv * sg
        dc = dff * val * (sg * (1.0 + conv * (1.0 - sg)))
        d_a = w0 * dc + w1 * _shift_up(dc, 1, row, S) + w2 * _shift_up(dc, 2, row, S)
        dup_ref[:, :CONV_COLS] = d_a.astype(dup_ref.dtype)
        dup_ref[:, CONV_COLS:] = d_val.astype(dup_ref.dtype)

        @pl.when(b == 0)
        def _():
            dw_ref[...] = jnp.zeros_like(dw_ref)
            db_ref[...] = jnp.zeros_like(db_ref)

        dw_ref[0:1, :] += _col_sum(dc * a)
        dw_ref[1:2, :] += _col_sum(dc * a1)
        dw_ref[2:3, :] += _col_sum(dc * a2)
        db_ref[...] += _col_sum(dc)

    return pl.pallas_call(
        body, name="conv_gate_bwd", grid=(nj, B),
        in_specs=[pl.BlockSpec((None, S, 2 * CONV_COLS), lambda j, b: (b, 0, j)),
                  pl.BlockSpec((None, S, CONV_COLS), lambda j, b: (b, 0, j)),
                  pl.BlockSpec((3, CONV_COLS), lambda j, b: (0, j)),
                  pl.BlockSpec((1, CONV_COLS), lambda j, b: (0, j))],
        out_specs=[pl.BlockSpec((None, S, 2 * CONV_COLS), lambda j, b: (b, 0, j)),
                   pl.BlockSpec((3, CONV_COLS), lambda j, b: (0, j)),
                   pl.BlockSpec((1, CONV_COLS), lambda j, b: (0, j))],
        out_shape=[jax.ShapeDtypeStruct((B, S, 2 * D_FF), BF16), jax.ShapeDtypeStruct((3, D_FF), F32),
                   jax.ShapeDtypeStruct((1, D_FF), F32)],
        compiler_params=_params(2),
    )(up, d_ff, w_conv, b_conv)


def _ada_fwd(c_all, w_ada, b_ada):
    def body(c_ref, w_ref, b_ref, o_ref):
        cv = c_ref[...]
        act = (cv * _sigmoid(cv)).astype(BF16)
        o_ref[...] = jnp.dot(act, w_ref[...].astype(BF16), preferred_element_type=F32) + b_ref[...]

    return pl.pallas_call(body, name="ada_fwd",
                          out_shape=jax.ShapeDtypeStruct((c_all.shape[0], w_ada.shape[1]), F32),
                          compiler_params=pltpu.CompilerParams(vmem_limit_bytes=V7X_VMEM_LIMIT))(c_all, w_ada, b_ada)


def _ada_bwd(c_all, dmod_all, dmod_cols):
    def body(c_ref, dm_ref, dmc_ref, dw_ref, db_ref):
        cv = c_ref[...]
        act = (cv * _sigmoid(cv)).astype(BF16)
        dw_ref[...] = lax.dot_general(act, dmc_ref[...].astype(BF16), _TN, preferred_element_type=F32)
        db_ref[...] = _col_sum(dm_ref[...])

    return pl.pallas_call(
        body, name="ada_bwd",
        out_shape=[jax.ShapeDtypeStruct((c_all.shape[1], dmod_cols.shape[1]), F32),
                   jax.ShapeDtypeStruct((1, dmod_all.shape[1]), F32)],
        compiler_params=pltpu.CompilerParams(vmem_limit_bytes=V7X_VMEM_LIMIT))(c_all, dmod_all, dmod_cols)


def _adamw(w, m, v, g_parts, name, own=None):
    R, C = w.shape
    P = g_parts.shape[0]
    tr = R
    for cand in (256, 128, 64, 32, 16, 8):
        if R % cand == 0 and cand * C * 4 * (P + 8) * 2 <= V7X_VMEM_LIMIT // 2:
            tr = cand
            break
    c1 = 1.0 / (1.0 - ADAM_B1 ** ADAM_STEP)
    c2 = 1.0 / (1.0 - ADAM_B2 ** ADAM_STEP)

    def update(w_ref, m_ref, v_ref, g, og, od, om, ov):
        m_new = ADAM_B1 * m_ref[...] + (1.0 - ADAM_B1) * g
        v_new = ADAM_B2 * v_ref[...] + (1.0 - ADAM_B2) * (g * g)
        og[...] = g
        om[...] = m_new
        ov[...] = v_new
        od[...] = -ADAM_LR * ((m_new * c1) / (jnp.sqrt(v_new * c2) + ADAM_EPS) + ADAM_WD * w_ref[...])

    def total(g_ref):
        g = g_ref[0].astype(F32)
        for p in range(1, P):
            g = g + g_ref[p].astype(F32)
        return g

    out_shape = [jax.ShapeDtypeStruct((R, C), F32)] * 4
    if own is None:
        def body(w_ref, m_ref, v_ref, g_ref, og, od, om, ov):
            update(w_ref, m_ref, v_ref, total(g_ref), og, od, om, ov)

        spec = pl.BlockSpec((tr, C), lambda i: (i, 0))
        return pl.pallas_call(
            body, name=name, grid=(R // tr,),
            in_specs=[spec, spec, spec, pl.BlockSpec((P, tr, C), lambda i: (0, i, 0))],
            out_specs=[spec] * 4, out_shape=out_shape, compiler_params=_params(1),
        )(w, m, v, g_parts)

    slots, me = own

    def body_own(me_ref, w_ref, m_ref, v_ref, g_ref, own_ref, og, od, om, ov):
        g = own_ref[...].astype(F32)
        for p in range(P):
            g = g + jnp.where(me_ref[0] == p, 0.0, g_ref[p].astype(F32))
        update(w_ref, m_ref, v_ref, g, og, od, om, ov)

    spec = pl.BlockSpec((tr, C), lambda i, me_ref: (i, 0))
    grid_spec = pltpu.PrefetchScalarGridSpec(
        num_scalar_prefetch=1, grid=(R // tr,),
        in_specs=[spec, spec, spec, pl.BlockSpec((P, tr, C), lambda i, me_ref: (0, i, 0)),
                  pl.BlockSpec((None, tr, C), lambda i, me_ref: (me_ref[0], i, 0))],
        out_specs=[spec] * 4)
    return pl.pallas_call(body_own, name=name, grid_spec=grid_spec, out_shape=out_shape,
                          compiler_params=_params(1))(me, w, m, v, g_parts, slots)


def _adamw_small(ws, ms, vs, gs):
    n = len(ws)
    c1 = 1.0 / (1.0 - ADAM_B1 ** ADAM_STEP)
    c2 = 1.0 / (1.0 - ADAM_B2 ** ADAM_STEP)

    def body(*refs):
        ins, outs = refs[:4 * n], refs[4 * n:]
        for i in range(n):
            w, m, v, g = ins[i][...], ins[n + i][...], ins[2 * n + i][...], ins[3 * n + i][...]
            m_new = ADAM_B1 * m + (1.0 - ADAM_B1) * g
            v_new = ADAM_B2 * v + (1.0 - ADAM_B2) * (g * g)
            outs[4 * i][...] = g
            outs[4 * i + 1][...] = -ADAM_LR * ((m_new * c1) / (jnp.sqrt(v_new * c2) + ADAM_EPS) + ADAM_WD * w)
            outs[4 * i + 2][...] = m_new
            outs[4 * i + 3][...] = v_new

    out_shape = [jax.ShapeDtypeStruct(w.shape, F32) for w in ws for _ in range(4)]
    return pl.pallas_call(body, name="adamw_small", out_shape=out_shape,
                          compiler_params=pltpu.CompilerParams(vmem_limit_bytes=V7X_VMEM_LIMIT))(*ws, *ms, *vs, *gs)


def _sum_parts(parts, loss_rows):
    P, R, C = parts.shape
    lo, hi = loss_rows

    def body(p_ref, o_ref, loss_ref):
        t = p_ref[0]
        for p in range(1, P):
            t = t + p_ref[p]
        o_ref[...] = t
        tot = jnp.sum(jnp.sum(o_ref[lo:hi, :], axis=1, keepdims=True), axis=0, keepdims=True)
        loss_ref[...] = jnp.broadcast_to(tot, loss_ref.shape)

    return pl.pallas_call(body, name="sum_small_grads",
                          out_shape=[jax.ShapeDtypeStruct((R, C), F32), jax.ShapeDtypeStruct((1, LANES), F32)],
                          compiler_params=pltpu.CompilerParams(vmem_limit_bytes=V7X_VMEM_LIMIT))(parts)


def _exchange(items, name):
    n = len(items)
    MESH = pl.DeviceIdType.MESH

    def body(*refs):
        src, dst = refs[:n], refs[n:2 * n]
        send_sems, recv_sems, local_sems = refs[2 * n:]
        x, y, c = lax.axis_index("x"), lax.axis_index("y"), lax.axis_index("c")
        me = 4 * x + 2 * y + c
        started = []
        for it, (_, per_peer) in enumerate(items):
            own = pltpu.make_async_copy(src[it].at[me] if per_peer else src[it], dst[it].at[me], local_sems.at[it])
            own.start()
            started.append(own)
        sends, recvs = [], []
        for k in range(1, N_DEV):
            px = 1 - x if k & 4 else x
            py = 1 - y if k & 2 else y
            pc = 1 - c if k & 1 else c
            peer = 4 * px + 2 * py + pc
            for it, (_, per_peer) in enumerate(items):
                s = src[it].at[peer] if per_peer else src[it]
                cp = pltpu.make_async_remote_copy(src_ref=s, dst_ref=dst[it].at[me], send_sem=send_sems.at[it, k - 1],
                                                  recv_sem=recv_sems.at[it, k - 1], device_id=(px, py, pc),
                                                  device_id_type=MESH)
                cp.start()
                sends.append(cp)
                recvs.append(pltpu.make_async_remote_copy(
                    src_ref=s, dst_ref=dst[it].at[peer], send_sem=send_sems.at[it, k - 1],
                    recv_sem=recv_sems.at[it, k - 1], device_id=(px, py, pc), device_id_type=MESH))
        for cp in recvs:
            cp.wait_recv()
        for cp in sends:
            cp.wait_send()
        for cp in started:
            cp.wait()

    any_spec = pl.BlockSpec(memory_space=pl.ANY)
    out_shape = []
    for a, per_peer in items:
        shp = a.shape if per_peer else (N_DEV,) + a.shape
        out_shape.append(jax.ShapeDtypeStruct(shp, a.dtype))
    return pl.pallas_call(
        body, name=name, in_specs=[any_spec] * n, out_specs=[any_spec] * n, out_shape=out_shape,
        scratch_shapes=[pltpu.SemaphoreType.DMA((n, N_DEV - 1)), pltpu.SemaphoreType.DMA((n, N_DEV - 1)),
                        pltpu.SemaphoreType.DMA((n,))],
    )(*[a for a, _ in items])


def _remote(src, dst, send_sem, recv_sem, device):
    return pltpu.make_async_remote_copy(src_ref=src, dst_ref=dst, send_sem=send_sem, recv_sem=recv_sem,
                                        device_id=device, device_id_type=pl.DeviceIdType.MESH)


def _mesh_place():
    x, y, c = lax.axis_index("x"), lax.axis_index("y"), lax.axis_index("c")
    other_chips = [(1 - x, y), (x, 1 - y), (1 - x, 1 - y)]
    return x, y, c, (x, y, 1 - c), other_chips


def _gather_all(items, name):
    n = len(items)

    def body(*refs):
        src, dst = refs[:n], refs[n:2 * n]
        send_sems, recv_sems, local_sems = refs[2 * n:]
        x, y, c, sibling, chips = _mesh_place()
        slot = lambda px, py, pc: 4 * px + 2 * py + pc
        me = slot(x, y, c)
        own = [pltpu.make_async_copy(src[it], dst[it].at[me], local_sems.at[it]) for it in range(n)]
        first = []
        for it in range(n):
            first.append(_remote(src[it], dst[it].at[me], send_sems.at[it, 0], recv_sems.at[it, 0], sibling))
            for j, chip in enumerate(chips):
                first.append(_remote(src[it], dst[it].at[me], send_sems.at[it, 1 + j], recv_sems.at[it, 1 + j],
                                     (*chip, c)))
        for cp in own + first:
            cp.start()
        passed = []
        for j, chip in enumerate(chips):
            blk = slot(*chip, c)
            for it in range(n):
                _remote(src[it], dst[it].at[blk], send_sems.at[it, 1 + j], recv_sems.at[it, 1 + j],
                        (*chip, c)).wait_recv()
                fwd = _remote(dst[it].at[blk], dst[it].at[blk], send_sems.at[it, 4 + j], recv_sems.at[it, 4 + j],
                              sibling)
                fwd.start()
                passed.append(fwd)
        for it in range(n):
            _remote(src[it], dst[it].at[slot(x, y, 1 - c)], send_sems.at[it, 0], recv_sems.at[it, 0],
                    sibling).wait_recv()
        for j, chip in enumerate(chips):
            for it in range(n):
                _remote(src[it], dst[it].at[slot(*chip, 1 - c)], send_sems.at[it, 4 + j], recv_sems.at[it, 4 + j],
                        sibling).wait_recv()
        for cp in first + passed:
            cp.wait_send()
        for cp in own:
            cp.wait()

    any_spec = pl.BlockSpec(memory_space=pl.ANY)
    return pl.pallas_call(
        body, name=name, in_specs=[any_spec] * n, out_specs=[any_spec] * n,
        out_shape=[jax.ShapeDtypeStruct((N_DEV,) + a.shape, a.dtype) for a in items],
        scratch_shapes=[pltpu.SemaphoreType.DMA((n, 7)), pltpu.SemaphoreType.DMA((n, 7)),
                        pltpu.SemaphoreType.DMA((n,))],
    )(*items)


def _peers():
    x, y, c = lax.axis_index("x"), lax.axis_index("y"), lax.axis_index("c")
    out = []
    for k in range(1, N_DEV):
        px = 1 - x if k & 4 else x
        py = 1 - y if k & 2 else y
        pc = 1 - c if k & 1 else c
        out.append((k, (px, py, pc), 4 * px + 2 * py + pc))
    return 4 * x + 2 * y + c, out


def _exchange_start(items, name, gather, carry=()):
    n, m = len(items), len(carry)

    def body(*refs):
        src, land = refs[:n], refs[n:2 * n]
        first_out = 2 * n + m
        send_sems, recv_sems = refs[first_out:first_out + n], refs[first_out + n:first_out + 2 * n]
        token = refs[-1]
        me, peers = _peers()
        for k, peer, slot in peers:
            for it in range(n):
                _remote(src[it] if gather else src[it].at[slot], land[it].at[me], send_sems[it], recv_sems[it],
                        peer).start()
        token[...] = jnp.zeros_like(token)

    hbm = pl.BlockSpec(memory_space=pltpu.HBM)
    sem = pl.BlockSpec(memory_space=pltpu.SEMAPHORE)
    land_shapes = [(N_DEV,) + (a.shape if gather else a.shape[1:]) for a in items]
    lands = [lax.empty(shp, a.dtype) for shp, a in zip(land_shapes, items)]
    through = list(items) + lands + list(carry)
    outs = pl.pallas_call(
        body, name=name,
        out_shape=(*[pltpu.SemaphoreType.DMA(())] * (2 * n), *[pltpu.HBM(a.shape, a.dtype) for a in through],
                   jax.ShapeDtypeStruct((8, LANES), F32)),
        in_specs=[hbm] * len(through),
        out_specs=(*[sem] * (2 * n), *[hbm] * len(through), pl.BlockSpec(memory_space=pltpu.VMEM)),
        input_output_aliases={i: 2 * n + i for i in range(len(through))},
        compiler_params=pltpu.CompilerParams(has_side_effects=pltpu.SideEffectType.DATAFLOW_SIDE_EFFECTING),
    )(*[pltpu.with_memory_space_constraint(a, pltpu.HBM) for a in through])
    return (list(outs[:n]), list(outs[n:2 * n]), list(outs[2 * n:3 * n]), list(outs[3 * n:4 * n]), outs[-1],
            list(outs[4 * n:4 * n + m]))


def _exchange_wait(send_sems, recv_sems, items, lands, after, name):
    n = len(items)

    def body(*refs):
        land = refs[n:2 * n]
        send_sems, recv_sems = refs[2 * n:3 * n], refs[3 * n:4 * n]
        me, peers = _peers()
        for it in range(n):
            seven = land[it].at[pl.ds(0, N_DEV - 1)]
            cp = _remote(seven, seven, send_sems[it], recv_sems[it], peers[0][1])
            cp.wait_send()
            cp.wait_recv()

    hbm = pl.BlockSpec(memory_space=pltpu.HBM)
    sem = pl.BlockSpec(memory_space=pltpu.SEMAPHORE)
    outs = pl.pallas_call(
        body, name=name,
        out_shape=tuple(pltpu.HBM(a.shape, a.dtype) for a in list(items) + list(lands)),
        in_specs=[hbm] * (2 * n) + [sem] * (2 * n) + [pl.BlockSpec(memory_space=pl.ANY)],
        out_specs=tuple([hbm] * (2 * n)),
        input_output_aliases={i: i for i in range(2 * n)},
        compiler_params=pltpu.CompilerParams(has_side_effects=pltpu.SideEffectType.DATAFLOW_SIDE_EFFECTING),
    )(*items, *lands, *send_sems, *recv_sems, after)
    return list(outs[:n]), list(outs[n:])


def _gelu_tanh(y):
    k = math.sqrt(2.0 / math.pi)
    t = jnp.tanh(k * (y + 0.044715 * y * y * y))
    return 0.5 * y * (1.0 + t), t


def _local_step(x, mod, target, W, late_weights, P, send_early):
    B, S, D = x.shape
    T = B * S
    TS = 512
    flat = lambda a: a.reshape(T, a.shape[-1])
    unflat = lambda a: a.reshape(B, S, a.shape[-1])
    mod_col = lambda i: (mod, D, i)

    def f_modnorm(xv, sc, sh, g):
        return (xv * _rms_scale(xv) * g) * (1.0 + sc) + sh

    (u1,) = _rowwise(f_modnorm, [(x, D, 0)], [mod_col(1), mod_col(0)], [P["g_mix"]],
                     [(D, BF16)], [], [], ts=TS, name="modnorm_mix")
    u1f = flat(u1)
    qkv, us, gates = (unflat(t) for t in _project_in(u1f, [W["w_qkv"], W["w_us"], W["w_gates"]], [F32, F32, BF16],
                                                      name="proj_in"))

    o_att, lse = _attention_fwd(qkv, P["slopes"])
    more_w, more_p = late_weights(o_att)
    W, P = {**W, **more_w}, {**P, **more_p}
    y_att = unflat(_matmul(flat(o_att), W["w_proj_att"], out_dtype=BF16, name="proj_att"))

    xs, y_mm = _scan_fwd(us, P["bb_big"], P["a_row"], P["cc_big"])

    def f_glu(ymm, usv, dsk, wg, bg):
        yv = ymm + dsk * usv
        ge, _ = _gelu_tanh(yv)
        pre = jnp.dot(ge.astype(BF16), wg, preferred_element_type=F32) + bg
        return yv, ge * _sigmoid(pre)

    y_s5, z = _rowwise(f_glu, [(y_mm, SSM_WIDTH, 0), (us, SSM_WIDTH, 0)], [], [P["d_skip"], W["w_glu"], P["b_glu"]],
                       [(SSM_WIDTH, F32), (SSM_WIDTH, BF16)], [], [], ts=TS, name="s5_glu")
    y_ssm = unflat(_matmul(flat(z), W["w_proj_ssm"], out_dtype=BF16, name="proj_ssm"))

    def f_merge(ga, gs, ya, ys, bga, bgs):
        return _sigmoid(ga + bga) * ya + _sigmoid(gs + bgs) * ys

    bga, bgs = P["b_gate"][:, :D], P["b_gate"][:, D:]
    (merged,) = _rowwise(f_merge, [(gates, D, 0), (gates, D, 1), (y_att, D, 0), (y_ssm, D, 0)], [], [bga, bgs],
                         [(D, BF16)], [], [], ts=TS, name="gate_merge")
    mix = unflat(_matmul(flat(merged), W["w_out"], out_dtype=BF16, name="proj_out"))

    def f_res_modnorm(xv, mx, gt, sc, sh, g):
        h = xv + gt * mx
        return h, (h * _rms_scale(h) * g) * (1.0 + sc) + sh

    h1, u2 = _rowwise(f_res_modnorm, [(x, D, 0), (mix, D, 0)], [mod_col(2), mod_col(4), mod_col(3)], [P["g_ffn"]],
                      [(D, F32), (D, BF16)], [], [], ts=TS, name="residual_modnorm_ffn")
    up = unflat(_up_fwd(flat(u2), W["w_up"], name="ffn_up"))
    ff = _conv_fwd(up, P["w_conv"], P["b_conv"])
    down = unflat(_matmul(flat(ff), W["w_down"], out_dtype=BF16, name="ffn_down"))

    def f_head(h1v, dn, tg, gt, g):
        h2 = h1v + gt * dn
        r = _rms_scale(h2)
        nh = h2 * r
        e = nh * g - tg
        dy = e * (1.0 / D)
        gy = dy * g
        dh = r * (gy - nh * jnp.mean(gy * nh, axis=-1, keepdims=True))
        return (dh, dh * gt, _col_sum(dh * dn), _col_sum(dy * nh), _col_sum(e * e) * (0.5 / D))

    dh2, d_down, d_gt2, d_g_final, loss_cols = _rowwise(
        f_head, [(h1, D, 0), (down, D, 0), (target, D, 0)], [mod_col(5)], [P["g_final"]],
        [(D, BF16), (D, BF16)], [D], [(1, D), (1, D)], ts=TS, name="head_loss")

    d_downf = flat(d_down)
    d_ff = unflat(_matmul(d_downf, W["w_down"], tb=True, out_dtype=BF16, name="ffn_down_dx"))
    d_w_down = _matmul(flat(ff), d_downf, ta=True, out_dtype=BF16, name="ffn_down_dw")
    d_up, d_w_conv, d_b_conv = _conv_bwd(up, d_ff, P["w_conv"], P["b_conv"])
    d_upf = flat(d_up)
    d_u2 = unflat(_up_dx(d_upf, W["w_up"], name="ffn_up_dx"))
    d_w_up = _up_dw(flat(u2), d_upf, name="ffn_up_dw")
    token, _ = send_early(dict(w_down=d_w_down.reshape(N_DEV, D_FF // N_DEV, D), w_up=d_w_up))
    g_ffn_after = P["g_ffn"] + token[0:1, 0:1]

    def f_modnorm_bwd(du, h, dres, mx, sc, gt, g):
        r = _rms_scale(h)
        nh = h * r
        dn = du * (1.0 + sc)
        gy = dn * g
        dh = dres + r * (gy - nh * jnp.mean(gy * nh, axis=-1, keepdims=True))
        return (dh, dh * gt, _col_sum(du), _col_sum(du * nh * g), _col_sum(dh * mx), _col_sum(dn * nh))

    dh1, d_mix, d_sh2, d_sc2, d_gt1, d_g_ffn = _rowwise(
        f_modnorm_bwd, [(d_u2, D, 0), (h1, D, 0), (dh2, D, 0), (mix, D, 0)], [mod_col(4), mod_col(2)], [g_ffn_after],
        [(D, BF16), (D, BF16)], [D, D, D], [(1, D)], ts=TS, name="modnorm_ffn_bwd")

    d_mixf = flat(d_mix)
    d_merged = unflat(_matmul(d_mixf, W["w_out"], tb=True, out_dtype=BF16, name="proj_out_dx"))
    d_w_out = _matmul(flat(merged), d_mixf, ta=True, out_dtype=BF16, name="proj_out_dw")

    def f_merge_bwd(dm, ga, gs, ya, ys, bga_, bgs_):
        sa, ss = _sigmoid(ga + bga_), _sigmoid(gs + bgs_)
        dga = dm * ya * sa * (1.0 - sa)
        dgs = dm * ys * ss * (1.0 - ss)
        return dm * sa, dm * ss, jnp.concatenate([dga, dgs], axis=1), _col_sum(dga), _col_sum(dgs)

    d_y_att, d_y_ssm, d_gates, d_bga, d_bgs = _rowwise(
        f_merge_bwd, [(d_merged, D, 0), (gates, D, 0), (gates, D, 1), (y_att, D, 0), (y_ssm, D, 0)], [], [bga, bgs],
        [(D, BF16), (D, BF16), (2 * D, BF16)], [], [(1, D), (1, D)], ts=TS, name="gate_merge_bwd")

    d_yaf, d_ysf = flat(d_y_att), flat(d_y_ssm)
    d_o_att = unflat(_matmul(d_yaf, W["w_proj_att"], tb=True, name="proj_att_dx"))
    d_w_proj_att = _matmul(flat(o_att), d_yaf, ta=True, out_dtype=BF16, name="proj_att_dw")
    d_z = unflat(_matmul(d_ysf, W["w_proj_ssm"], tb=True, out_dtype=BF16, name="proj_ssm_dx"))
    d_w_proj_ssm = _matmul(flat(z), d_ysf, ta=True, out_dtype=BF16, name="proj_ssm_dw")

    def f_glu_bwd(yv, dz, usv, dsk, wg, bg):
        ge, t = _gelu_tanh(yv)
        pre = jnp.dot(ge.astype(BF16), wg, preferred_element_type=F32) + bg
        sg = _sigmoid(pre)
        dpre = dz * ge * sg * (1.0 - sg)
        dge = dz * sg + lax.dot_general(dpre.astype(BF16), wg, _NT, preferred_element_type=F32)
        k = math.sqrt(2.0 / math.pi)
        dgelu = 0.5 * (1.0 + t) + 0.5 * yv * (1.0 - t * t) * k * (1.0 + 3.0 * 0.044715 * yv * yv)
        dy = dge * dgelu
        dwg = lax.dot_general(ge.astype(BF16), dpre.astype(BF16), _TN, preferred_element_type=F32)
        return dy, dy * dsk, dwg, _col_sum(dpre), _col_sum(dy * usv)

    d_y_s5, d_us_skip, d_w_glu, d_b_glu, d_d_skip = _rowwise(
        f_glu_bwd, [(y_s5, SSM_WIDTH, 0), (d_z, SSM_WIDTH, 0), (us, SSM_WIDTH, 0)], [],
        [P["d_skip"], W["w_glu"], P["b_glu"]],
        [(SSM_WIDTH, BF16), (SSM_WIDTH, F32)], [], [(SSM_WIDTH, SSM_WIDTH), (1, SSM_WIDTH), (1, SSM_WIDTH)],
        ts=TS, name="s5_glu_bwd")
    d_us_parts, g_ab, d_bb, d_cc = _scan_bwd(d_y_s5, us, P["bb_big"], P["cc_big"], xs, P["a_row"])

    token, _ = send_early(dict(
        w_out=d_w_out.reshape(N_DEV, D // N_DEV, D), w_proj_att=_cols_to_slots(d_w_proj_att),
        w_proj_ssm=_cols_to_slots(d_w_proj_ssm),
        w_glu=d_w_glu.astype(BF16).reshape(N_DEV, SSM_WIDTH // N_DEV, SSM_WIDTH),
        w_conv=_cols_to_slots(d_w_conv.astype(BF16))))
    d_qkv = _attention_bwd(qkv, o_att, d_o_att, lse, P["slopes"] + token[0, 0])

    def f_add(*parts):
        return sum(parts[1:], parts[0])

    n_parts = d_us_parts.shape[0]
    stacked = d_us_parts.reshape(n_parts * B, S, SSM_WIDTH)
    (d_us,) = _rowwise(f_add, [(d_us_skip, SSM_WIDTH, 0)] + [(stacked, SSM_WIDTH, 0, j * B) for j in range(n_parts)],
                       [], [],
                       [(SSM_WIDTH, BF16)], [], [], ts=TS, name="s5_input_grad")
    d_qkvf = flat(d_qkv)
    d_usf = flat(d_us)
    d_gatesf = flat(d_gates)
    d_w_in_t = jnp.concatenate(
        [_unpair_qkv_rows(_matmul(d_qkvf, u1f, ta=True, out_dtype=BF16, name="proj_qkv_dw")),
         _matmul(d_usf, u1f, ta=True, out_dtype=BF16, name="proj_ssm_in_dw"),
         _matmul(d_gatesf, u1f, ta=True, out_dtype=BF16, name="proj_gates_dw")], axis=0)
    token, (w_qkv, w_us, w_gates) = send_early(dict(w_in=d_w_in_t.reshape(N_DEV, -1, D)),
                                               carry=[W["w_qkv"], W["w_us"], W["w_gates"]])
    d_u1 = unflat(_project_back([d_qkvf, d_usf, d_gatesf], [w_qkv, w_us, w_gates], name="proj_in_dx"))

    def f_modnorm_bwd_in(du, h, dres, sc, g):
        r = _rms_scale(h)
        nh = h * r
        dn = du * (1.0 + sc)
        gy = dn * g
        dh = dres + r * (gy - nh * jnp.mean(gy * nh, axis=-1, keepdims=True))
        return (dh, _col_sum(du), _col_sum(du * nh * g), _col_sum(dn * nh))

    grad_x, d_sh1, d_sc1, d_g_mix = _rowwise(
        f_modnorm_bwd_in, [(d_u1, D, 0), (x, D, 0), (dh1, D, 0)], [mod_col(1)], [P["g_mix"] + token[0:1, 0:1]],
        [(D, F32)], [D, D], [(1, D)], ts=TS, name="modnorm_mix_bwd")

    d_mod = jnp.concatenate([d_sh1, d_sc1, d_gt1, d_sh2, d_sc2, d_gt2], axis=-1)
    g_ab_re, g_ab_im = _deinterleave(g_ab)
    d_bb_re, d_bb_im = _deinterleave(d_bb)
    d_cc_re, d_cc_im = (t.T for t in _deinterleave(d_cc.T))
    small = dict(g_mix=d_g_mix, b_gate=jnp.concatenate([d_bga, d_bgs], axis=1), g_ab_re=g_ab_re, g_ab_im=g_ab_im,
                 d_bb_re=d_bb_re, d_bb_im=d_bb_im, d_cc_re=d_cc_re, d_cc_im=d_cc_im, d_skip=d_d_skip,
                 b_glu=d_b_glu, g_ffn=d_g_ffn, b_conv=d_b_conv, g_final=d_g_final, loss_cols=loss_cols)
    return grad_x, d_mod, small


def _block_diag_in(bb):
    t = bb.reshape(SSM_GROUPS, SSM_STATE, SSM_GROUP_CH)
    eye = jnp.eye(SSM_GROUPS, dtype=bb.dtype)
    return jnp.einsum("gnc,gh->gchn", t, eye).reshape(SSM_WIDTH, SSM_COLS)


def _block_diag_out(cm):
    eye = jnp.eye(SSM_GROUPS, dtype=cm.dtype)
    return jnp.einsum("gcn,gh->gnhc", cm, eye).reshape(SSM_COLS, SSM_WIDTH)


def _diag_blocks_in(m):
    t = m.reshape(SSM_GROUPS, SSM_GROUP_CH, SSM_GROUPS, SSM_STATE)
    idx = jnp.arange(SSM_GROUPS)
    return t[idx, :, idx, :].transpose(0, 2, 1).reshape(SSM_COLS, SSM_GROUP_CH)


def _diag_blocks_out(m):
    t = m.reshape(SSM_GROUPS, SSM_STATE, SSM_GROUPS, SSM_GROUP_CH)
    idx = jnp.arange(SSM_GROUPS)
    return t[idx, :, idx, :].transpose(0, 2, 1)


def _pair_qkv_rows(w):
    return w.reshape(3, N_HEADS // 2, LANES, w.shape[1]).swapaxes(0, 1).reshape(w.shape)


def _unpair_qkv_rows(w):
    return w.reshape(N_HEADS // 2, 3, LANES, w.shape[1]).swapaxes(0, 1).reshape(w.shape)


def _interleave(re, im):
    lead = re.shape[:-1]
    g = lambda a: a.reshape(lead + (SSM_COLS // SCAN_COLS, 1, SCAN_COLS))
    return jnp.concatenate([g(re), g(im)], axis=-2).reshape(lead + (2 * SSM_COLS,))


def _deinterleave(x):
    lead = x.shape[:-1]
    t = x.reshape(lead + (SSM_COLS // SCAN_COLS, 2, SCAN_COLS))
    return t[..., 0, :].reshape(lead + (SSM_COLS,)), t[..., 1, :].reshape(lead + (SSM_COLS,))


def _cols_to_slots(g):
    R = g.shape[0]
    return g.reshape(R, N_DEV, g.shape[1] // N_DEV).transpose(1, 0, 2)


def _slots_to_cols(g):
    return g.transpose(1, 0, 2).reshape(g.shape[1], N_DEV * g.shape[2])


SMALL_ORDER = ("b_ada", "g_mix", "b_gate", "a_re", "a_im", "log_dt", "b_re", "b_im", "c_re", "c_im", "d_skip",
               "b_glu", "g_ffn", "b_conv", "g_final")


def _pack(arrs):
    pieces, offs, row = [], [], 0
    for a in arrs:
        f = a.reshape(-1).astype(F32)
        n = f.shape[0]
        rows = -(-n // LANES)
        pieces.append(jnp.pad(f, (0, rows * LANES - n)))
        offs.append((row, n))
        row += rows
    return jnp.concatenate(pieces).reshape(row, LANES), offs


def _unpack(packed, offs, shapes):
    flat = packed.reshape(-1)
    return [flat[r * LANES:r * LANES + n].reshape(s) for (r, n), s in zip(offs, shapes)]


def kernel(x, c, w_ada, b_ada, g_mix, w_in, b_gate, a_re, a_im, log_dt, b_re, b_im, c_re, c_im, d_skip, w_glu, b_glu, w_proj_att, w_proj_ssm, w_out, g_ffn, w_up, w_conv, b_conv, w_down, g_final, loss_target, m_w_ada, m_b_ada, m_g_mix, m_w_in, m_b_gate, m_a_re, m_a_im, m_log_dt, m_b_re, m_b_im, m_c_re, m_c_im, m_d_skip, m_w_glu, m_b_glu, m_w_proj_att, m_w_proj_ssm, m_w_out, m_g_ffn, m_w_up, m_w_conv, m_b_conv, m_w_down, m_g_final, v_w_ada, v_b_ada, v_g_mix, v_w_in, v_b_gate, v_a_re, v_a_im, v_log_dt, v_b_re, v_b_im, v_c_re, v_c_im, v_d_skip, v_w_glu, v_b_glu, v_w_proj_att, v_w_proj_ssm, v_w_out, v_g_ffn, v_w_up, v_w_conv, v_b_conv, v_w_down, v_g_final):
    args = dict(locals())
    B, S, D = x.shape
    me = 4 * lax.axis_index("x") + 2 * lax.axis_index("y") + lax.axis_index("c")
    bf = lambda w: w[0].astype(BF16)

    c_slots, w_in_slots = _gather_all([c, w_in[0].T.astype(BF16)], name="gather_first_weights")
    c_all = c_slots.reshape(N_DEV * B, D)
    w_in_t = w_in_slots.reshape(-1, D)
    n_qkv = 3 * ATT_WIDTH
    W = dict(w_qkv=_pair_qkv_rows(w_in_t[:n_qkv]), w_us=w_in_t[n_qkv:n_qkv + SSM_WIDTH],
             w_gates=w_in_t[n_qkv + SSM_WIDTH:])

    n_ada = w_ada.shape[2]
    b_ada_cols = lax.dynamic_slice(b_ada, (0, me * n_ada), (1, n_ada))
    mod_part = _ada_fwd(c_all, w_ada[0], b_ada_cols)
    (mod_slots,) = _exchange([(mod_part.reshape(N_DEV, B, n_ada), True)], name="scatter_modulation")
    mod = mod_slots.transpose(1, 0, 2).reshape(B, 1, 6 * D)

    later = [bf(w_glu), bf(w_proj_att), bf(w_proj_ssm), bf(w_out), bf(w_up), w_conv[0], bf(w_down)]
    later_sems = _exchange_start(later, "start_later_weights", gather=True, carry=[mod])
    (mod,) = later_sems[5]

    def late_weights(after):
        _, lands = _exchange_wait(*later_sems[:4], after, name="wait_later_weights")
        g = [lax.dynamic_update_index_in_dim(land, a, me, 0) for land, a in zip(lands, later)]
        more_w = dict(w_glu=g[0].reshape(SSM_WIDTH, SSM_WIDTH), w_proj_att=_slots_to_cols(g[1]),
                      w_proj_ssm=_slots_to_cols(g[2]), w_out=g[3].reshape(D, D), w_up=g[4],
                      w_down=g[6].reshape(D_FF, D))
        return more_w, dict(w_conv=_slots_to_cols(g[5]))

    ab_re, ab_im, f_re, f_im = _s5_params(a_re[0], a_im[0], log_dt[0].reshape(SSM_GROUPS, 1))
    col = lambda a: a.reshape(SSM_COLS, 1)
    b_re2, b_im2 = b_re[0].reshape(SSM_COLS, SSM_GROUP_CH), b_im[0].reshape(SSM_COLS, SSM_GROUP_CH)
    bb_re, bb_im = _s5_input_matrix(col(f_re), col(f_im), b_re2, b_im2)
    slopes = jnp.asarray([2.0 ** (-8.0 * (h + 1) / N_HEADS) for h in range(N_HEADS)], F32)
    P = dict(g_mix=g_mix, g_ffn=g_ffn, g_final=g_final.reshape(1, D), b_gate=b_gate, d_skip=d_skip, b_glu=b_glu,
             b_conv=b_conv, slopes=slopes,
             a_row=_interleave(ab_re.reshape(1, SSM_COLS), ab_im.reshape(1, SSM_COLS)),
             bb_big=_interleave(_block_diag_in(bb_re), _block_diag_in(bb_im)),
             cc_big=_interleave(_block_diag_out(c_re[0]).T, -_block_diag_out(c_im[0]).T).T)

    in_flight = []

    def send_early(grads, carry=()):
        names = list(grads)
        handles = _exchange_start([grads[n] for n in names], "start_gradients_%d" % len(in_flight), gather=False,
                                  carry=carry)
        in_flight.append((names,) + handles[:4])
        return handles[4], handles[5]

    grad_x, d_mod, small = _local_step(x, mod, loss_target, W, late_weights, P, send_early)

    small_list = [small["loss_cols"], small["g_mix"], small["b_gate"], small["g_ab_re"], small["g_ab_im"],
                  _diag_blocks_in(small["d_bb_re"]), _diag_blocks_in(small["d_bb_im"]),
                  _diag_blocks_out(small["d_cc_re"]), -_diag_blocks_out(small["d_cc_im"]),
                  small["g_ffn"], small["b_conv"], small["g_final"], small["d_skip"], small["b_glu"]]
    small_packed, small_offs = _pack(small_list)
    small_all, dmod_slots = _gather_all([small_packed, d_mod.reshape(B, 6 * D)], name="gather_small_gradients")

    out = {}

    def update(name, parts, own=None):
        view = (lambda a: a[0].T) if name == "w_in" else (lambda a: a[0])
        back = (lambda a: a.T[None]) if name == "w_in" else (lambda a: a[None])
        g, dl, mn, vn = _adamw(view(args[name]), view(args["m_" + name]), view(args["v_" + name]), parts,
                               name="adamw_" + name, own=own)
        for key, val in (("grad_", g), ("delta_", dl), ("new_m_", mn), ("new_v_", vn)):
            out[key + name] = back(val)

    my_slot = me.astype(jnp.int32).reshape(1)
    for i, (names, send_sems, recv_sems, sent, lands) in enumerate(in_flight):
        sent, lands = _exchange_wait(send_sems, recv_sems, sent, lands, dmod_slots, name="wait_gradients_%d" % i)
        for name, own_slots, landed in zip(names, sent, lands):
            update(name, landed, own=(own_slots, my_slot))

    dmod_all = dmod_slots.reshape(N_DEV * B, 6 * D)
    dmod_cols = lax.dynamic_slice(dmod_all, (0, me * n_ada), (N_DEV * B, n_ada))
    d_w_ada, d_b_ada = _ada_bwd(c_all, dmod_all, dmod_cols)
    update("w_ada", d_w_ada[None])

    loss_row, loss_n = small_offs[0]
    small_sum, loss_vec = _sum_parts(small_all, (loss_row, loss_row + loss_n // LANES))
    shapes = [(1, D), (1, D), (1, 2 * D), (SSM_GROUPS, SSM_STATE), (SSM_GROUPS, SSM_STATE), (SSM_COLS, SSM_GROUP_CH),
              (SSM_COLS, SSM_GROUP_CH), (1, SSM_GROUPS, SSM_GROUP_CH, SSM_STATE),
              (1, SSM_GROUPS, SSM_GROUP_CH, SSM_STATE), (1, D), (1, D_FF), (D,), (1, SSM_WIDTH), (1, SSM_WIDTH)]
    (_, s_g_mix, s_b_gate, s_ab_re, s_ab_im, s_bb_re, s_bb_im, s_c_re, s_c_im, s_g_ffn, s_b_conv, s_g_final,
     s_d_skip, s_b_glu) = _unpack(small_sum, small_offs, shapes)
    d_b_re2, d_b_im2, d_f_re, d_f_im = _s5_input_matrix_bwd(col(f_re), col(f_im), b_re2, b_im2, s_bb_re, s_bb_im)
    d_a_re, d_a_im, d_log_dt = _s5_params_bwd(a_re[0], a_im[0], log_dt[0].reshape(SSM_GROUPS, 1), s_ab_re, s_ab_im,
                                              d_f_re.reshape(SSM_GROUPS, SSM_STATE),
                                              d_f_im.reshape(SSM_GROUPS, SSM_STATE))
    grads_small = dict(b_ada=d_b_ada, g_mix=s_g_mix, b_gate=s_b_gate, a_re=d_a_re[None], a_im=d_a_im[None],
                       log_dt=d_log_dt.reshape(1, SSM_GROUPS), b_re=d_b_re2.reshape(b_re.shape),
                       b_im=d_b_im2.reshape(b_im.shape), c_re=s_c_re, c_im=s_c_im, d_skip=s_d_skip, b_glu=s_b_glu,
                       g_ffn=s_g_ffn, b_conv=s_b_conv, g_final=s_g_final)
    flat2 = lambda a: a.reshape(-1, a.shape[-1])
    res = _adamw_small([flat2(args[n]) for n in SMALL_ORDER], [flat2(args["m_" + n]) for n in SMALL_ORDER],
                       [flat2(args["v_" + n]) for n in SMALL_ORDER],
                       [flat2(grads_small[n].reshape(args[n].shape)) for n in SMALL_ORDER])
    for i, n in enumerate(SMALL_ORDER):
        for k, key in enumerate(("grad_", "delta_", "new_m_", "new_v_")):
            out[key + n] = res[4 * i + k].reshape(args[n].shape)

    order = ["w_ada", "b_ada", "g_mix", "w_in", "b_gate", "a_re", "a_im", "log_dt", "b_re", "b_im", "c_re", "c_im",
             "d_skip", "w_glu", "b_glu", "w_proj_att", "w_proj_ssm", "w_out", "g_ffn", "w_up", "w_conv", "b_conv",
             "w_down", "g_final"]
    loss = loss_vec[0, 0]
    return (loss, grad_x, *[out[k + n] for k in ("grad_", "delta_", "new_m_", "new_v_") for n in order])
```

```python
import math

import jax
import jax.numpy as jnp
from jax import lax
from jax.experimental import pallas as pl
from jax.experimental.pallas import tpu as pltpu

F32 = jnp.float32
BF16 = jnp.bfloat16

N_DEV = 8
D_MODEL = 1024
N_HEADS = 8
HEAD_DIM = 64
ATT_WIDTH = N_HEADS * HEAD_DIM
DILATIONS = (1, 4, 16)
WIN = 128
SSM_GROUPS = 16
SSM_GROUP_CH = 16
SSM_WIDTH = SSM_GROUPS * SSM_GROUP_CH
SSM_STATE = 64
SSM_COLS = SSM_GROUPS * SSM_STATE
D_FF = 2048
EPS = 1e-6
NEG_INF = -1e30
ADAM_LR, ADAM_B1, ADAM_B2, ADAM_EPS, ADAM_WD, ADAM_STEP = 0.001, 0.9, 0.999, 1e-08, 0.01, 10

V7X_VMEM_LIMIT = 56 * 1024 * 1024
LANES = 128


def _params(n_grid):
    return pltpu.CompilerParams(dimension_semantics=("arbitrary",) * n_grid,
                                vmem_limit_bytes=V7X_VMEM_LIMIT)


def _tile(n, pref):
    if n <= pref:
        return n
    t = (pref // LANES) * LANES
    while t > 0:
        if n % t == 0:
            return t
        t -= LANES
    return n


def _matmul(a, b, *, ta=False, tb=False, out_dtype=F32, name):
    if ta:
        K, M = a.shape
    else:
        M, K = a.shape
    if tb:
        N, K2 = b.shape
    else:
        K2, N = b.shape
    assert K == K2, (a.shape, b.shape)
    if ta:
        tm, tn, tk = _tile(M, 1024), _tile(N, 2048), _tile(K, 1024)
    else:
        tm, tk = _tile(M, 512), _tile(K, 4096)
        tn = _tile(N, 2048 if K <= 2048 else 1024)
    nk = K // tk
    dn = (((0,) if ta else (1,), (1,) if tb else (0,)), ((), ()))

    def body(a_ref, b_ref, o_ref, acc_ref):
        k = pl.program_id(2)
        part = lax.dot_general(a_ref[...].astype(BF16), b_ref[...].astype(BF16), dn, preferred_element_type=F32)
        if nk == 1:
            o_ref[...] = part.astype(o_ref.dtype)
            return

        @pl.when(k == 0)
        def _():
            acc_ref[...] = jnp.zeros_like(acc_ref)

        acc_ref[...] += part

        @pl.when(k == nk - 1)
        def _():
            o_ref[...] = acc_ref[...].astype(o_ref.dtype)

    a_spec = (pl.BlockSpec((tk, tm), lambda j, i, k: (k, i)) if ta
              else pl.BlockSpec((tm, tk), lambda j, i, k: (i, k)))
    b_spec = (pl.BlockSpec((tn, tk), lambda j, i, k: (j, k)) if tb
              else pl.BlockSpec((tk, tn), lambda j, i, k: (k, j)))
    return pl.pallas_call(
        body, name=name, grid=(N // tn, M // tm, nk),
        in_specs=[a_spec, b_spec],
        out_specs=pl.BlockSpec((tm, tn), lambda j, i, k: (i, j)),
        out_shape=jax.ShapeDtypeStruct((M, N), out_dtype),
        scratch_shapes=[pltpu.VMEM((tm, tn) if nk > 1 else (8, LANES), F32)],
        compiler_params=_params(3),
    )(a, b)


def _project_in(a, weights_t, out_dtypes, name):
    M, K = a.shape
    tm = _tile(M, 512)
    n = len(weights_t)

    def body(a_ref, *refs):
        av = a_ref[...].astype(BF16)
        for w_ref, o_ref in zip(refs[:n], refs[n:]):
            o_ref[...] = lax.dot_general(av, w_ref[...].astype(BF16), _NT,
                                         preferred_element_type=F32).astype(o_ref.dtype)

    return pl.pallas_call(
        body, name=name, grid=(M // tm,),
        in_specs=[pl.BlockSpec((tm, K), lambda i: (i, 0))] + [pl.BlockSpec(w.shape, lambda i: (0, 0)) for w in weights_t],
        out_specs=[pl.BlockSpec((tm, w.shape[0]), lambda i: (i, 0)) for w in weights_t],
        out_shape=[jax.ShapeDtypeStruct((M, w.shape[0]), dt) for w, dt in zip(weights_t, out_dtypes)],
        compiler_params=_params(1),
    )(a, *weights_t)


def _project_back(ds, weights_t, name):
    M = ds[0].shape[0]
    K = weights_t[0].shape[1]
    tm = _tile(M, 512)
    n = len(ds)
    weights = weights_t

    def body(*refs):
        total = None
        for d_ref, w_ref in zip(refs[:n], refs[n:2 * n]):
            part = jnp.dot(d_ref[...].astype(BF16), w_ref[...].astype(BF16), preferred_element_type=F32)
            total = part if total is None else total + part
        refs[2 * n][...] = total.astype(refs[2 * n].dtype)

    return pl.pallas_call(
        body, name=name, grid=(M // tm,),
        in_specs=[pl.BlockSpec((tm, d.shape[1]), lambda i: (i, 0)) for d in ds]
        + [pl.BlockSpec(w.shape, lambda i: (0, 0)) for w in weights],
        out_specs=pl.BlockSpec((tm, K), lambda i: (i, 0)),
        out_shape=jax.ShapeDtypeStruct((M, K), BF16), compiler_params=_params(1),
    )(*ds, *weights)


HALF = 256
UP_SLOTS = N_DEV // 2
UP_GROUP = 4 * HALF


def _group_weight(w_ref):
    return jnp.concatenate([w_ref[0, :, :HALF], w_ref[1, :, :HALF], w_ref[0, :, HALF:], w_ref[1, :, HALF:]], axis=1)


def _up_weight_spec(K, index):
    return pl.BlockSpec((2, None, K, 2 * HALF), index)


def _up_fwd(a, w3, name):
    M, K = a.shape
    tm = _tile(M, 1024)

    def body(a_ref, w_ref, o_ref):
        o_ref[...] = jnp.dot(a_ref[...].astype(BF16), _group_weight(w_ref),
                             preferred_element_type=F32).astype(o_ref.dtype)

    return pl.pallas_call(
        body, name=name, grid=(UP_SLOTS, M // tm),
        in_specs=[pl.BlockSpec((tm, K), lambda j, i: (i, 0)), _up_weight_spec(K, lambda j, i: (0, j, 0, 0))],
        out_specs=pl.BlockSpec((tm, UP_GROUP), lambda j, i: (i, j)),
        out_shape=jax.ShapeDtypeStruct((M, UP_SLOTS * UP_GROUP), BF16), compiler_params=_params(2),
    )(a, w3.reshape(2, UP_SLOTS, K, 2 * HALF))


def _up_dx(d, w3, name):
    M = d.shape[0]
    K = w3.shape[1]
    tm = _tile(M, 1024)

    def body(d_ref, w_ref, o_ref, acc_ref):
        j = pl.program_id(1)

        @pl.when(j == 0)
        def _():
            acc_ref[...] = jnp.zeros_like(acc_ref)

        acc_ref[...] += lax.dot_general(d_ref[...], _group_weight(w_ref), _NT, preferred_element_type=F32)

        @pl.when(j == UP_SLOTS - 1)
        def _():
            o_ref[...] = acc_ref[...].astype(o_ref.dtype)

    return pl.pallas_call(
        body, name=name, grid=(M // tm, UP_SLOTS),
        in_specs=[pl.BlockSpec((tm, UP_GROUP), lambda i, j: (i, j)), _up_weight_spec(K, lambda i, j: (0, j, 0, 0))],
        out_specs=pl.BlockSpec((tm, K), lambda i, j: (i, 0)),
        out_shape=jax.ShapeDtypeStruct((M, K), BF16), scratch_shapes=[pltpu.VMEM((tm, K), F32)],
        compiler_params=_params(2),
    )(d, w3.reshape(2, UP_SLOTS, K, 2 * HALF))


def _up_dw(a, d, name):
    M, K = a.shape
    tk = _tile(M, 1024)
    nk = M // tk

    def body(a_ref, d_ref, o_ref, acc_ref):
        k = pl.program_id(1)

        @pl.when(k == 0)
        def _():
            acc_ref[...] = jnp.zeros_like(acc_ref)

        acc_ref[...] += lax.dot_general(a_ref[...], d_ref[...], _TN, preferred_element_type=F32)

        @pl.when(k == nk - 1)
        def _():
            for half in range(2):
                for part in range(2):
                    lo = (2 * half + part) * HALF
                    o_ref[part, :, half * HALF:(half + 1) * HALF] = acc_ref[:, lo:lo + HALF].astype(o_ref.dtype)

    out = pl.pallas_call(
        body, name=name, grid=(UP_SLOTS, nk),
        in_specs=[pl.BlockSpec((tk, K), lambda j, k: (k, 0)), pl.BlockSpec((tk, UP_GROUP), lambda j, k: (k, j))],
        out_specs=_up_weight_spec(K, lambda j, k: (0, j, 0, 0)),
        out_shape=jax.ShapeDtypeStruct((2, UP_SLOTS, K, 2 * HALF), BF16),
        scratch_shapes=[pltpu.VMEM((K, UP_GROUP), F32)], compiler_params=_params(2),
    )(a, d)
    return out.reshape(N_DEV, K, 2 * HALF)


def _rowwise(fn, rows, bvecs, consts, out_rows, out_b, out_g, *, ts, name, raw=()):
    B, S = rows[0][0].shape[:2]
    nin = len(rows) + len(bvecs) + len(consts)
    nr, nb, ng = len(out_rows), len(out_b), len(out_g)

    def body(*refs):
        b = pl.program_id(0)
        s = pl.program_id(1)
        vals = [r[...] for r in refs[:nin]]
        vals[:len(rows)] = [v if i in raw else v.astype(F32) for i, v in enumerate(vals[:len(rows)])]
        outs = fn(*vals)
        if not isinstance(outs, (tuple, list)):
            outs = (outs,)
        orefs = refs[nin:]
        for i in range(nr):
            orefs[i][...] = outs[i].astype(orefs[i].dtype)
        for i in range(nb):
            ref = orefs[nr + i]

            @pl.when(s == 0)
            def _(ref=ref):
                ref[...] = jnp.zeros_like(ref)

            ref[...] += outs[nr + i]
        for i in range(ng):
            ref = orefs[nr + nb + i]

            @pl.when((s == 0) & (b == 0))
            def _(ref=ref):
                ref[...] = jnp.zeros_like(ref)

            ref[...] += outs[nr + nb + i]

    rows = [r if len(r) == 4 else r + (0,) for r in rows]
    in_specs = ([pl.BlockSpec((None, ts, cb), lambda b, s, ci=ci, b0=b0: (b0 + b, s, ci)) for (_, cb, ci, b0) in rows]
                + [pl.BlockSpec((None, 1, cb), lambda b, s, ci=ci: (b, 0, ci)) for (_, cb, ci) in bvecs]
                + [pl.BlockSpec(a.shape, lambda b, s: (0, 0)) for a in consts])
    out_shape = ([jax.ShapeDtypeStruct((B, S, c), dt) for (c, dt) in out_rows]
                 + [jax.ShapeDtypeStruct((B, 1, c), F32) for c in out_b]
                 + [jax.ShapeDtypeStruct(rc, F32) for rc in out_g])
    out_specs = ([pl.BlockSpec((None, ts, c), lambda b, s: (b, s, 0)) for (c, _) in out_rows]
                 + [pl.BlockSpec((None, 1, c), lambda b, s: (b, 0, 0)) for c in out_b]
                 + [pl.BlockSpec(rc, lambda b, s: (0, 0)) for rc in out_g])
    args = [r[0] for r in rows] + [a for (a, _, _) in bvecs] + list(consts)
    return pl.pallas_call(
        body, name=name, grid=(B, S // ts), in_specs=in_specs, out_specs=out_specs,
        out_shape=out_shape, compiler_params=_params(2),
    )(*args)


def _col_sum(v):
    return jnp.sum(v, axis=0, keepdims=True)


def _rms_scale(h):
    return lax.rsqrt(jnp.mean(h * h, axis=-1, keepdims=True) + EPS)


def _sigmoid(v):
    return 0.5 * (1.0 + jnp.tanh(0.5 * v))


ATT_SCALE = HEAD_DIM ** -0.5
COPY_ROWS = 256
_NT = (((1,), (1,)), ((), ()))
_TN = (((0,), (0,)), ((), ()))


def _row_chunks(d, seq):
    sub = seq // d
    out = []
    for r in range(d):
        for c0 in range(0, sub, COPY_ROWS):
            n = min(COPY_ROWS, sub - c0)
            out.append((pl.ds(r + c0 * d, n, stride=d), r * sub + c0, n))
    return out


ATT_UNROLL = 8
KEYS = 2 * WIN


def _zero_once(refs):
    @pl.when((pl.program_id(0) == 0) & (pl.program_id(1) == 0))
    def _():
        for r in refs:
            r[...] = jnp.zeros_like(r)


def _pair_bias(bias_ref, slopes_ref, hp, d, key_major):
    shape = (KEYS, WIN) if key_major else (WIN, KEYS)
    qi = lax.broadcasted_iota(jnp.int32, shape, 1 if key_major else 0)
    kj = lax.broadcasted_iota(jnp.int32, shape, 0 if key_major else 1)
    dist = WIN + qi - kj
    valid = (dist >= 0) & (dist <= WIN)
    distf = dist.astype(F32)
    for h in range(2):
        slope_d = slopes_ref[2 * hp + h] * float(d)
        with_prev = jnp.where(valid, -(slope_d * distf), NEG_INF)
        no_prev = jnp.where(kj >= WIN, with_prev, NEG_INF)
        span = slice(h * KEYS, (h + 1) * KEYS)
        if key_major:
            bias_ref[1, span, :] = with_prev
            bias_ref[0, span, :] = no_prev
        else:
            bias_ref[1, :, span] = with_prev
            bias_ref[0, :, span] = no_prev


def _stack_heads(v):
    first = lax.broadcasted_iota(jnp.int32, v.shape, 1) < HEAD_DIM
    zero = jnp.zeros_like(v)
    return jnp.concatenate([jnp.where(first, v, zero), jnp.where(first, zero, v)], axis=0)


def _per_head(c0, c1, n):
    return jnp.where(lax.broadcasted_iota(jnp.int32, (n, LANES), 1) < HEAD_DIM, c0, c1)


def _qkv_spec(seq, j):
    return pl.BlockSpec((None, seq, LANES), lambda b, hp: (b, 0, 3 * hp + j))


def _attention_fwd(qkv, slopes):
    B, S, _ = qkv.shape
    n_blk = S // WIN
    n_pair = N_HEADS // 2

    def body(slopes_ref, q_ref, k_ref, v_ref, o_ref, lse_ref, qp, kp, vp, bias, acc, mx, sm, acc_n, mx_n, sm_n):
        hp = pl.program_id(1)
        _zero_once((kp, vp))
        for p, d in enumerate(DILATIONS):
            nb = n_blk // d
            chunks = _row_chunks(d, S)
            for src, dst, n in chunks:
                qp[dst:dst + n, :] = (q_ref[src, :] * ATT_SCALE).astype(BF16)
                kp[WIN + dst:WIN + dst + n, :] = k_ref[src, :].astype(BF16)
                vp[WIN + dst:WIN + dst + n, :] = v_ref[src, :].astype(BF16)
            _pair_bias(bias, slopes_ref, hp, d, key_major=False)
            acc_t, mx_t, sm_t = (acc_n, mx_n, sm_n) if d == 1 else (acc, mx, sm)

            nk = WIN if nb == 1 else KEYS
            bias_cur = jnp.concatenate([bias[0, :, WIN:KEYS], bias[0, :, KEYS + WIN:]], axis=1) if nb == 1 else None

            def block(i, carry, p=p, nb=nb, nk=nk, bias_cur=bias_cur, acc_t=acc_t, mx_t=mx_t, sm_t=sm_t):
                cur = pl.ds(pl.multiple_of(i * WIN, WIN), WIN)
                keys = pl.ds(pl.multiple_of(i * WIN + (KEYS - nk), WIN), nk)
                s = lax.dot_general(qp[cur, :], _stack_heads(kp[keys, :]), _NT, preferred_element_type=F32)
                s = s + (bias_cur if nb == 1 else bias[((i % nb) > 0).astype(jnp.int32)])
                es, ms, ls = [], [], []
                for h in range(2):
                    sh = s[:, h * nk:(h + 1) * nk]
                    m = jnp.max(sh if nb == 1 else jnp.maximum(sh[:, :WIN], sh[:, WIN:]), axis=1, keepdims=True)
                    e = jnp.exp(sh - m)
                    es.append(e.astype(BF16))
                    ms.append(m)
                    ls.append(jnp.sum(e if nb == 1 else e[:, :WIN] + e[:, WIN:], axis=1, keepdims=True))
                acc_t[p, cur, :] = jnp.dot(jnp.concatenate(es, axis=1), _stack_heads(vp[keys, :]),
                                           preferred_element_type=F32)
                mx_t[p, cur, :] = _per_head(ms[0], ms[1], WIN)
                sm_t[p, cur, :] = _per_head(ls[0], ls[1], WIN)
                return carry

            lax.fori_loop(0, n_blk, block, 0, unroll=ATT_UNROLL)
            if d > 1:
                for src, dst, n in chunks:
                    acc_n[p, src, :] = acc[p, dst:dst + n, :]
                    mx_n[p, src, :] = mx[p, dst:dst + n, :]
                    sm_n[p, src, :] = sm[p, dst:dst + n, :]

        chunk = 256

        def merge(i, carry):
            rows = pl.ds(pl.multiple_of(i * chunk, chunk), chunk)
            ms = [mx_n[p, rows, :] for p in range(3)]
            m = jnp.maximum(jnp.maximum(ms[0], ms[1]), ms[2])
            ws = [jnp.exp(mp - m) for mp in ms]
            l = ws[0] * sm_n[0, rows, :] + ws[1] * sm_n[1, rows, :] + ws[2] * sm_n[2, rows, :]
            o = (ws[0] * acc_n[0, rows, :] + ws[1] * acc_n[1, rows, :] + ws[2] * acc_n[2, rows, :]) / l
            o_ref[rows, :] = o.astype(o_ref.dtype)
            lse = m + jnp.log(l)
            for h in range(2):
                lse_ref[rows, h:h + 1] = lse[:, h * HEAD_DIM:h * HEAD_DIM + 1]
            return carry

        lax.fori_loop(0, S // chunk, merge, 0)

    return pl.pallas_call(
        body, name="attention_fwd", grid=(B, n_pair),
        in_specs=[pl.BlockSpec(memory_space=pltpu.SMEM), _qkv_spec(S, 0), _qkv_spec(S, 1), _qkv_spec(S, 2)],
        out_specs=[pl.BlockSpec((None, S, LANES), lambda b, hp: (b, 0, hp)),
                   pl.BlockSpec((None, None, S, 2), lambda b, hp: (b, hp, 0, 0))],
        out_shape=[jax.ShapeDtypeStruct((B, S, ATT_WIDTH), BF16),
                   jax.ShapeDtypeStruct((B, n_pair, S, 2), F32)],
        scratch_shapes=[pltpu.VMEM((S, LANES), BF16), pltpu.VMEM((S + WIN, LANES), BF16),
                        pltpu.VMEM((S + WIN, LANES), BF16), pltpu.VMEM((2, WIN, 2 * KEYS), F32)]
        + [pltpu.VMEM((3, S, LANES), F32)] * 6,
        compiler_params=_params(2),
    )(slopes, qkv, qkv, qkv)


def _attention_bwd(qkv, o, do, lse, slopes):
    B, S, _ = qkv.shape
    n_blk = S // WIN
    n_pair = N_HEADS // 2

    def body(slopes_ref, q_ref, k_ref, v_ref, o_ref, do_ref, lse_ref, dx_ref,
             qp, dop, kp, vp, aux, auxp, aux_t, bias_t, dqp, dvk, dq_n, dk_n, dv_n):
        hp = pl.program_id(1)
        aux[...] = jnp.zeros_like(aux)
        for c0 in range(0, S, COPY_ROWS):
            rows = slice(c0, c0 + COPY_ROWS)
            prod = do_ref[rows, :] * o_ref[rows, :].astype(F32)
            for h in range(2):
                aux[rows, 2 * h:2 * h + 1] = lse_ref[rows, h:h + 1]
                aux[rows, 2 * h + 1:2 * h + 2] = jnp.sum(prod[:, h * HEAD_DIM:(h + 1) * HEAD_DIM], axis=1,
                                                         keepdims=True)
        dq_n[...] = jnp.zeros_like(dq_n)
        dk_n[...] = jnp.zeros_like(dk_n)
        dv_n[...] = jnp.zeros_like(dv_n)
        _zero_once((kp, vp))
        for p, d in enumerate(DILATIONS):
            nb = n_blk // d
            chunks = _row_chunks(d, S)
            for src, dst, n in chunks:
                auxp[dst:dst + n, :] = aux[src, :]
                qp[dst:dst + n, :] = (q_ref[src, :] * ATT_SCALE).astype(BF16)
                dop[dst:dst + n, :] = do_ref[src, :].astype(BF16)
                kp[WIN + dst:WIN + dst + n, :] = k_ref[src, :].astype(BF16)
                vp[WIN + dst:WIN + dst + n, :] = v_ref[src, :].astype(BF16)
            for i in range(n_blk):
                aux_t[i] = auxp[i * WIN:(i + 1) * WIN, :].T[0:8, :]
            _pair_bias(bias_t, slopes_ref, hp, d, key_major=True)
            dvk[...] = jnp.zeros_like(dvk)

            nk = WIN if nb == 1 else KEYS
            bias_cur = jnp.concatenate([bias_t[0, WIN:KEYS, :], bias_t[0, KEYS + WIN:, :]], axis=0) if nb == 1 else None

            def block(i, carry, nb=nb, nk=nk, bias_cur=bias_cur):
                cur = pl.ds(pl.multiple_of(i * WIN, WIN), WIN)
                keys = pl.ds(pl.multiple_of(i * WIN + (KEYS - nk), WIN), nk)
                q2, do2 = qp[cur, :], dop[cur, :]
                kc = _stack_heads(kp[keys, :])
                s_t = lax.dot_general(kc, q2, _NT, preferred_element_type=F32)
                s_t = s_t + (bias_cur if nb == 1 else bias_t[((i % nb) > 0).astype(jnp.int32)])
                dp_t = lax.dot_general(_stack_heads(vp[keys, :]), do2, _NT, preferred_element_type=F32)
                ps, dss = [], []
                for h in range(2):
                    span = slice(h * nk, (h + 1) * nk)
                    p_t = jnp.exp(s_t[span, :] - aux_t[i, 2 * h:2 * h + 1, :])
                    ds_t = p_t * (dp_t[span, :] - aux_t[i, 2 * h + 1:2 * h + 2, :])
                    ps.append(p_t.astype(BF16))
                    dss.append(ds_t.astype(BF16))
                do_rows, q_rows = _stack_heads(do2), _stack_heads(q2)
                zr = jnp.zeros_like(do_rows)
                rhs = jnp.concatenate([jnp.concatenate([do_rows, zr], axis=1),
                                       jnp.concatenate([zr, q_rows], axis=1)], axis=0)
                dvk[keys, :] += jnp.dot(jnp.concatenate(ps + dss, axis=1), rhs, preferred_element_type=F32)
                dqp[cur, :] = lax.dot_general(jnp.concatenate(dss, axis=0), kc, _TN, preferred_element_type=F32)
                return carry

            lax.fori_loop(0, n_blk, block, 0, unroll=ATT_UNROLL)
            for src, dst, n in chunks:
                dq_n[src, :] += dqp[dst:dst + n, :]
                dv_n[src, :] += dvk[WIN + dst:WIN + dst + n, :LANES]
                dk_n[src, :] += dvk[WIN + dst:WIN + dst + n, LANES:]
        for c0 in range(0, S, COPY_ROWS):
            rows = slice(c0, c0 + COPY_ROWS)
            dx_ref[rows, 0:LANES] = (dq_n[rows, :] * ATT_SCALE).astype(dx_ref.dtype)
            dx_ref[rows, LANES:2 * LANES] = dk_n[rows, :].astype(dx_ref.dtype)
            dx_ref[rows, 2 * LANES:3 * LANES] = dv_n[rows, :].astype(dx_ref.dtype)

    pair = lambda width: pl.BlockSpec((None, S, width), lambda b, hp: (b, 0, hp))
    vm = lambda shape, dt: pltpu.VMEM(shape, dt)
    return pl.pallas_call(
        body, name="attention_bwd", grid=(B, n_pair),
        in_specs=[pl.BlockSpec(memory_space=pltpu.SMEM), _qkv_spec(S, 0), _qkv_spec(S, 1), _qkv_spec(S, 2),
                  pair(LANES), pair(LANES), pl.BlockSpec((None, None, S, 2), lambda b, hp: (b, hp, 0, 0))],
        out_specs=pair(3 * LANES),
        out_shape=jax.ShapeDtypeStruct((B, S, 3 * ATT_WIDTH), BF16),
        scratch_shapes=[vm((S, LANES), BF16), vm((S, LANES), BF16),
                        vm((S + WIN, LANES), BF16), vm((S + WIN, LANES), BF16),
                        vm((S, LANES), F32), vm((S, LANES), F32), vm((n_blk, 8, WIN), F32),
                        vm((2, 2 * KEYS, WIN), F32),
                        vm((S, LANES), F32), vm((S + WIN, 2 * LANES), F32),
                        vm((S, LANES), F32), vm((S, LANES), F32), vm((S, LANES), F32)],
        compiler_params=_params(2),
    )(slopes, qkv, qkv, qkv, o, do, lse)


SCAN_COLS = 256
SCAN_ROWS = 8


def _rows_to_tile(rows):
    rid = lax.broadcasted_iota(jnp.int32, (SCAN_ROWS, rows[0].shape[1]), 0)
    tile = jnp.broadcast_to(rows[0], rid.shape)
    for k in range(1, SCAN_ROWS):
        tile = jnp.where(rid == k, rows[k], tile)
    return tile


SCAN_UNROLL = 4


def _complex_powers(ar, ai, n):
    out = [(ar, ai)]
    for _ in range(n - 1):
        pr, pi = out[-1]
        out.append((pr * ar - pi * ai, pr * ai + pi * ar))
    return out


def _round_multipliers(powers, rid, reverse):
    out = []
    for s in (1, 2, 4):
        keep = (rid < SCAN_ROWS - s) if reverse else (rid >= s)
        out.append((jnp.where(keep, powers[s - 1][0], 0.0), jnp.where(keep, powers[s - 1][1], 0.0)))
    return out


def _tile_scan(xr, xi, multipliers, reverse):
    for s, (mr, mi) in zip((1, 2, 4), multipliers):
        shift = SCAN_ROWS - s if reverse else s
        sr, si = pltpu.roll(xr, shift, 0), pltpu.roll(xi, shift, 0)
        xr, xi = xr + (mr * sr - mi * si), xi + (mr * si + mi * sr)
    return xr, xi


SCAN_CHUNK = 256


def _scan_fwd(us, bb_big, a_row, cc_big):
    B, S, _ = us.shape
    groups = 2
    width = 2 * groups * SCAN_COLS
    nc = 2 * SSM_COLS // width
    nt = S // SCAN_ROWS
    tiles = SCAN_CHUNK // SCAN_ROWS
    LAST = slice(SCAN_ROWS - 1, SCAN_ROWS)

    def body(us_ref, bb_ref, a_ref, cc_ref, xs_ref, y_ref, bu_ref):
        bb = bb_ref[...].astype(BF16)
        for c in range(S // SCAN_CHUNK):
            part = jnp.dot(us_ref[c * SCAN_CHUNK:(c + 1) * SCAN_CHUNK, :].astype(BF16), bb,
                           preferred_element_type=F32)
            bu_ref[c * tiles:(c + 1) * tiles] = part.reshape(tiles, SCAN_ROWS, width)
        rid = lax.broadcasted_iota(jnp.int32, (SCAN_ROWS, SCAN_COLS), 0)
        consts = []
        for g in range(groups):
            re = slice(2 * g * SCAN_COLS, (2 * g + 1) * SCAN_COLS)
            im = slice((2 * g + 1) * SCAN_COLS, (2 * g + 2) * SCAN_COLS)
            powers = _complex_powers(a_ref[:, re], a_ref[:, im], SCAN_ROWS)
            carry_mult = (_rows_to_tile([p[0] for p in powers]), _rows_to_tile([p[1] for p in powers]))
            consts.append((re, im, carry_mult, _round_multipliers(powers, rid, reverse=False)))

        def tile(i, carry):
            out = []
            for (re, im, (cr_t, ci_t), rounds), (cr, ci) in zip(consts, carry):
                xr, xi = _tile_scan(bu_ref[i, :, re], bu_ref[i, :, im], rounds, reverse=False)
                xs_ref[i, :, re] = xr + (cr_t * cr - ci_t * ci)
                xs_ref[i, :, im] = xi + (cr_t * ci + ci_t * cr)
                out.append((xs_ref[i, LAST, re], xs_ref[i, LAST, im]))
            return tuple(out)

        zero = jnp.zeros((1, SCAN_COLS), F32)
        lax.fori_loop(0, nt, tile, ((zero, zero),) * groups, unroll=SCAN_UNROLL)

        @pl.when(pl.program_id(1) == 0)
        def _():
            y_ref[...] = jnp.zeros_like(y_ref)

        cc = cc_ref[...].astype(BF16)
        for c in range(S // SCAN_CHUNK):
            x2 = xs_ref[c * tiles:(c + 1) * tiles].reshape(SCAN_CHUNK, width).astype(BF16)
            y_ref[c * SCAN_CHUNK:(c + 1) * SCAN_CHUNK, :] += jnp.dot(x2, cc, preferred_element_type=F32)

    col = pl.BlockSpec((None, nt, SCAN_ROWS, width), lambda b, j: (b, 0, 0, j))
    tok = pl.BlockSpec((None, S, SSM_WIDTH), lambda b, j: (b, 0, 0))
    xs, y = pl.pallas_call(
        body, name="s5_scan_fwd", grid=(B, nc),
        in_specs=[tok, pl.BlockSpec((SSM_WIDTH, width), lambda b, j: (0, j)),
                  pl.BlockSpec((1, width), lambda b, j: (0, j)), pl.BlockSpec((width, SSM_WIDTH), lambda b, j: (j, 0))],
        out_specs=[col, tok],
        out_shape=[jax.ShapeDtypeStruct((B, nt, SCAN_ROWS, 2 * SSM_COLS), F32),
                   jax.ShapeDtypeStruct((B, S, SSM_WIDTH), F32)],
        scratch_shapes=[pltpu.VMEM((nt, SCAN_ROWS, width), F32)],
        compiler_params=_params(2),
    )(us, bb_big, a_row, cc_big)
    return xs.reshape(B, S, 2 * SSM_COLS), y


def _scan_bwd(dy, us, bb_big, cc_big, xs, a_row):
    B, S, _ = dy.shape
    width = 2 * SCAN_COLS
    nc = SSM_COLS // SCAN_COLS
    nt = S // SCAN_ROWS
    tiles = SCAN_CHUNK // SCAN_ROWS
    RE, IM = slice(0, SCAN_COLS), slice(SCAN_COLS, 2 * SCAN_COLS)
    FIRST, LAST = slice(0, 1), slice(SCAN_ROWS - 1, SCAN_ROWS)

    def body(dy_ref, us_ref, bb_ref, cc_ref, x_ref, a_ref, dus_ref, ga_ref, dbb_ref, dcc_ref, d_ref, lam_ref):
        b = pl.program_id(1)
        cc = cc_ref[...].astype(BF16)
        for c in range(S // SCAN_CHUNK):
            part = lax.dot_general(dy_ref[c * SCAN_CHUNK:(c + 1) * SCAN_CHUNK, :].astype(BF16), cc, _NT,
                                   preferred_element_type=F32)
            d_ref[c * tiles:(c + 1) * tiles] = part.reshape(tiles, SCAN_ROWS, width)
        powers = _complex_powers(a_ref[:, RE], -a_ref[:, IM], SCAN_ROWS)
        rid = lax.broadcasted_iota(jnp.int32, (SCAN_ROWS, SCAN_COLS), 0)
        cr_t = _rows_to_tile([powers[SCAN_ROWS - 1 - r][0] for r in range(SCAN_ROWS)])
        ci_t = _rows_to_tile([powers[SCAN_ROWS - 1 - r][1] for r in range(SCAN_ROWS)])
        rounds = _round_multipliers(powers, rid, reverse=True)

        @pl.when(b == 0)
        def _():
            ga_ref[...] = jnp.zeros_like(ga_ref)
            dbb_ref[...] = jnp.zeros_like(dbb_ref)
            dcc_ref[...] = jnp.zeros_like(dcc_ref)

        def tile(j, carry):
            cr, ci, accr, acci = carry
            i = nt - 1 - j
            lr, li = _tile_scan(d_ref[i, :, RE], d_ref[i, :, IM], rounds, reverse=True)
            lam_r = lr + (cr_t * cr - ci_t * ci)
            lam_i = li + (cr_t * ci + ci_t * cr)
            lam_ref[i, :, RE] = lam_r
            lam_ref[i, :, IM] = lam_i
            ip = jnp.maximum(i - 1, 0)
            keep = (i > 0).astype(F32)
            xpr = jnp.where(rid == 0, x_ref[ip, LAST, RE] * keep, pltpu.roll(x_ref[i, :, RE], 1, 0))
            xpi = jnp.where(rid == 0, x_ref[ip, LAST, IM] * keep, pltpu.roll(x_ref[i, :, IM], 1, 0))
            accr = accr + lam_r * xpr + lam_i * xpi
            acci = acci + lam_i * xpr - lam_r * xpi
            return lam_ref[i, FIRST, RE], lam_ref[i, FIRST, IM], accr, acci

        z1 = jnp.zeros((1, SCAN_COLS), F32)
        z8 = jnp.zeros((SCAN_ROWS, SCAN_COLS), F32)
        _, _, accr, acci = lax.fori_loop(0, nt, tile, (z1, z1, z8, z8), unroll=SCAN_UNROLL)
        ga_ref[:, RE] += _col_sum(accr)
        ga_ref[:, IM] += _col_sum(acci)

        bb = bb_ref[...].astype(BF16)
        for c in range(S // SCAN_CHUNK):
            rows = slice(c * SCAN_CHUNK, (c + 1) * SCAN_CHUNK)
            lam2 = lam_ref[c * tiles:(c + 1) * tiles].reshape(SCAN_CHUNK, width).astype(BF16)
            x2 = x_ref[c * tiles:(c + 1) * tiles].reshape(SCAN_CHUNK, width).astype(BF16)
            dus_ref[rows, :] = lax.dot_general(lam2, bb, _NT, preferred_element_type=F32)
            dbb_ref[...] += lax.dot_general(us_ref[rows, :].astype(BF16), lam2, _TN, preferred_element_type=F32)
            dcc_ref[...] += lax.dot_general(x2, dy_ref[rows, :].astype(BF16), _TN, preferred_element_type=F32)

    col = pl.BlockSpec((None, nt, SCAN_ROWS, width), lambda j, b: (b, 0, 0, j))
    tok = pl.BlockSpec((None, S, SSM_WIDTH), lambda j, b: (b, 0, 0))
    scratch = pltpu.VMEM((nt, SCAN_ROWS, width), F32)
    return pl.pallas_call(
        body, name="s5_scan_bwd", grid=(nc, B),
        in_specs=[tok, tok, pl.BlockSpec((SSM_WIDTH, width), lambda j, b: (0, j)),
                  pl.BlockSpec((width, SSM_WIDTH), lambda j, b: (j, 0)), col,
                  pl.BlockSpec((1, width), lambda j, b: (0, j))],
        out_specs=[pl.BlockSpec((None, None, S, SSM_WIDTH), lambda j, b: (j, b, 0, 0)),
                   pl.BlockSpec((1, width), lambda j, b: (0, j)),
                   pl.BlockSpec((SSM_WIDTH, width), lambda j, b: (0, j)),
                   pl.BlockSpec((width, SSM_WIDTH), lambda j, b: (j, 0))],
        out_shape=[jax.ShapeDtypeStruct((nc, B, S, SSM_WIDTH), F32), jax.ShapeDtypeStruct((1, 2 * SSM_COLS), F32),
                   jax.ShapeDtypeStruct((SSM_WIDTH, 2 * SSM_COLS), F32),
                   jax.ShapeDtypeStruct((2 * SSM_COLS, SSM_WIDTH), F32)],
        scratch_shapes=[scratch, scratch],
        compiler_params=_params(2),
    )(dy, us, bb_big, cc_big, xs.reshape(B, nt, SCAN_ROWS, 2 * SSM_COLS), a_row)


def _s5_discretise(lr, li, log_dt):
    dt = jnp.exp(log_dt)
    mag = jnp.exp(lr * dt)
    ang = li * dt
    ab_re, ab_im = mag * jnp.cos(ang), mag * jnp.sin(ang)
    nr, ni = ab_re - 1.0, ab_im
    den = lr * lr + li * li
    f_re = (nr * lr + ni * li) / den
    f_im = (ni * lr - nr * li) / den
    return dt, ab_re, ab_im, nr, ni, den, f_re, f_im


def _s5_params(a_re, a_im, log_dt):
    def body(lr_ref, li_ref, ld_ref, abr, abi, fr, fi):
        _, ab_re, ab_im, _, _, _, f_re, f_im = _s5_discretise(lr_ref[...], li_ref[...], ld_ref[...])
        abr[...] = ab_re
        abi[...] = ab_im
        fr[...] = f_re
        fi[...] = f_im

    return pl.pallas_call(body, name="s5_params",
                          out_shape=[jax.ShapeDtypeStruct(a_re.shape, F32)] * 4)(a_re, a_im, log_dt)


def _s5_input_matrix(f_re, f_im, b_re, b_im):
    def body(fr, fi, br, bi, o_re, o_im):
        o_re[...] = fr[...] * br[...] - fi[...] * bi[...]
        o_im[...] = fr[...] * bi[...] + fi[...] * br[...]

    return pl.pallas_call(body, name="s5_input_matrix",
                          out_shape=[jax.ShapeDtypeStruct(b_re.shape, F32)] * 2)(f_re, f_im, b_re, b_im)


def _s5_input_matrix_bwd(f_re, f_im, b_re, b_im, g_re, g_im):
    def body(fr, fi, br, bi, gr, gi, dbr, dbi, dfr, dfi):
        dbr[...] = fr[...] * gr[...] + fi[...] * gi[...]
        dbi[...] = fr[...] * gi[...] - fi[...] * gr[...]
        dfr[...] = jnp.sum(br[...] * gr[...] + bi[...] * gi[...], axis=1, keepdims=True)
        dfi[...] = jnp.sum(br[...] * gi[...] - bi[...] * gr[...], axis=1, keepdims=True)

    return pl.pallas_call(
        body, name="s5_input_matrix_bwd",
        out_shape=[jax.ShapeDtypeStruct(b_re.shape, F32)] * 2 + [jax.ShapeDtypeStruct(f_re.shape, F32)] * 2,
    )(f_re, f_im, b_re, b_im, g_re, g_im)


def _s5_params_bwd(a_re, a_im, log_dt, g_ab_re, g_ab_im, d_f_re, d_f_im):
    def body(lr_ref, li_ref, ld_ref, gar, gai, dfr, dfi, o_lr, o_li, o_ld):
        lr, li = lr_ref[...], li_ref[...]
        dt, ab_re, ab_im, nr, ni, den, f_re, f_im = _s5_discretise(lr, li, ld_ref[...])
        d_fr, d_fi = dfr[...], dfi[...]
        d_nr = (d_fr * lr - d_fi * li) / den
        d_ni = (d_fr * li + d_fi * lr) / den
        common = (d_fr * f_re + d_fi * f_im) * 2.0 / den
        d_lr = (d_fr * nr + d_fi * ni) / den - common * lr
        d_li = (d_fr * ni - d_fi * nr) / den - common * li
        d_abr = gar[...] + d_nr
        d_abi = gai[...] + d_ni
        d_mag_mag = d_abr * ab_re + d_abi * ab_im
        d_ang = d_abi * ab_re - d_abr * ab_im
        o_lr[...] = d_lr + d_mag_mag * dt
        o_li[...] = d_li + d_ang * dt
        o_ld[...] = jnp.sum(d_mag_mag * lr + d_ang * li, axis=1, keepdims=True) * dt

    return pl.pallas_call(
        body, name="s5_params_bwd",
        out_shape=[jax.ShapeDtypeStruct(a_re.shape, F32)] * 2 + [jax.ShapeDtypeStruct(log_dt.shape, F32)],
    )(a_re, a_im, log_dt, g_ab_re, g_ab_im, d_f_re, d_f_im)


CONV_COLS = 256


def _shift_down(v, j, row):
    return jnp.where(row >= j, pltpu.roll(v, j, 0), 0.0)


def _shift_up(v, j, row, seq):
    return jnp.where(row < seq - j, pltpu.roll(v, seq - j, 0), 0.0)


def _conv_fwd(up, w_conv, b_conv):
    B, S, _ = up.shape
    nj = D_FF // CONV_COLS

    def body(up_ref, w_ref, b_ref, ff_ref):
        a = up_ref[:, :CONV_COLS].astype(F32)
        val = up_ref[:, CONV_COLS:].astype(F32)
        row = lax.broadcasted_iota(jnp.int32, a.shape, 0)
        w0, w1, w2 = w_ref[0:1, :], w_ref[1:2, :], w_ref[2:3, :]
        conv = b_ref[...] + w0 * a + w1 * _shift_down(a, 1, row) + w2 * _shift_down(a, 2, row)
        ff_ref[...] = (conv * _sigmoid(conv) * val).astype(ff_ref.dtype)

    return pl.pallas_call(
        body, name="conv_gate_fwd", grid=(B, nj),
        in_specs=[pl.BlockSpec((None, S, 2 * CONV_COLS), lambda b, j: (b, 0, j)),
                  pl.BlockSpec((3, CONV_COLS), lambda b, j: (0, j)),
                  pl.BlockSpec((1, CONV_COLS), lambda b, j: (0, j))],
        out_specs=pl.BlockSpec((None, S, CONV_COLS), lambda b, j: (b, 0, j)),
        out_shape=jax.ShapeDtypeStruct((B, S, D_FF), BF16),
        compiler_params=_params(2),
    )(up, w_conv, b_conv)


def _conv_bwd(up, d_ff, w_conv, b_conv):
    B, S, _ = up.shape
    nj = D_FF // CONV_COLS

    def body(up_ref, dff_ref, w_ref, b_ref, dup_ref, dw_ref, db_ref):
        b = pl.program_id(1)
        a = up_ref[:, :CONV_COLS].astype(F32)
        val = up_ref[:, CONV_COLS:].astype(F32)
        row = lax.broadcasted_iota(jnp.int32, a.shape, 0)
        w0, w1, w2 = w_ref[0:1, :], w_ref[1:2, :], w_ref[2:3, :]
        a1, a2 = _shift_down(a, 1, row), _shift_down(a, 2, row)
        conv = b_ref[...] + w0 * a + w1 * a1 + w2 * a2
        sg = _sigmoid(conv)
        dff = dff_ref[...].astype(F32)
        d_val = dff * conv * sg
        dc = dff * val * (sg * (1.0 + conv * (1.0 - sg)))
        d_a = w0 * dc + w1 * _shift_up(dc, 1, row, S) + w2 * _shift_up(dc, 2, row, S)
        dup_ref[:, :CONV_COLS] = d_a.astype(dup_ref.dtype)
        dup_ref[:, CONV_COLS:] = d_val.astype(dup_ref.dtype)

        @pl.when(b == 0)
        def _():
            dw_ref[...] = jnp.zeros_like(dw_ref)
            db_ref[...] = jnp.zeros_like(db_ref)

        dw_ref[0:1, :] += _col_sum(dc * a)
        dw_ref[1:2, :] += _col_sum(dc * a1)
        dw_ref[2:3, :] += _col_sum(dc * a2)
        db_ref[...] += _col_sum(dc)

    return pl.pallas_call(
        body, name="conv_gate_bwd", grid=(nj, B),
        in_specs=[pl.BlockSpec((None, S, 2 * CONV_COLS), lambda j, b: (b, 0, j)),
                  pl.BlockSpec((None, S, CONV_COLS), lambda j, b: (b, 0, j)),
                  pl.BlockSpec((3, CONV_COLS), lambda j, b: (0, j)),
                  pl.BlockSpec((1, CONV_COLS), lambda j, b: (0, j))],
        out_specs=[pl.BlockSpec((None, S, 2 * CONV_COLS), lambda j, b: (b, 0, j)),
                   pl.BlockSpec((3, CONV_COLS), lambda j, b: (0, j)),
                   pl.BlockSpec((1, CONV_COLS), lambda j, b: (0, j))],
        out_shape=[jax.ShapeDtypeStruct((B, S, 2 * D_FF), BF16), jax.ShapeDtypeStruct((3, D_FF), F32),
                   jax.ShapeDtypeStruct((1, D_FF), F32)],
        compiler_params=_params(2),
    )(up, d_ff, w_conv, b_conv)


def _ada_fwd(c_all, w_ada, b_ada):
    def body(c_ref, w_ref, b_ref, o_ref):
        cv = c_ref[...]
        act = (cv * _sigmoid(cv)).astype(BF16)
        o_ref[...] = jnp.dot(act, w_ref[...].astype(BF16), preferred_element_type=F32) + b_ref[...]

    return pl.pallas_call(body, name="ada_fwd",
                          out_shape=jax.ShapeDtypeStruct((c_all.shape[0], w_ada.shape[1]), F32),
                          compiler_params=pltpu.CompilerParams(vmem_limit_bytes=V7X_VMEM_LIMIT))(c_all, w_ada, b_ada)


def _ada_bwd(c_all, dmod_all, dmod_cols):
    def body(c_ref, dm_ref, dmc_ref, dw_ref, db_ref):
        cv = c_ref[...]
        act = (cv * _sigmoid(cv)).astype(BF16)
        dw_ref[...] = lax.dot_general(act, dmc_ref[...].astype(BF16), _TN, preferred_element_type=F32)
        db_ref[...] = _col_sum(dm_ref[...])

    return pl.pallas_call(
        body, name="ada_bwd",
        out_shape=[jax.ShapeDtypeStruct((c_all.shape[1], dmod_cols.shape[1]), F32),
                   jax.ShapeDtypeStruct((1, dmod_all.shape[1]), F32)],
        compiler_params=pltpu.CompilerParams(vmem_limit_bytes=V7X_VMEM_LIMIT))(c_all, dmod_all, dmod_cols)


def _adamw(w, m, v, g_parts, name, own=None):
    R, C = w.shape
    P = g_parts.shape[0]
    tr = R
    for cand in (256, 128, 64, 32, 16, 8):
        if R % cand == 0 and cand * C * 4 * (P + 8) * 2 <= V7X_VMEM_LIMIT // 2:
            tr = cand
            break
    c1 = 1.0 / (1.0 - ADAM_B1 ** ADAM_STEP)
    c2 = 1.0 / (1.0 - ADAM_B2 ** ADAM_STEP)

    def update(w_ref, m_ref, v_ref, g, og, od, om, ov):
        m_new = ADAM_B1 * m_ref[...] + (1.0 - ADAM_B1) * g
        v_new = ADAM_B2 * v_ref[...] + (1.0 - ADAM_B2) * (g * g)
        og[...] = g
        om[...] = m_new
        ov[...] = v_new
        od[...] = -ADAM_LR * ((m_new * c1) / (jnp.sqrt(v_new * c2) + ADAM_EPS) + ADAM_WD * w_ref[...])

    def total(g_ref):
        g = g_ref[0].astype(F32)
        for p in range(1, P):
            g = g + g_ref[p].astype(F32)
        return g

    out_shape = [jax.ShapeDtypeStruct((R, C), F32)] * 4
    if own is None:
        def body(w_ref, m_ref, v_ref, g_ref, og, od, om, ov):
            update(w_ref, m_ref, v_ref, total(g_ref), og, od, om, ov)

        spec = pl.BlockSpec((tr, C), lambda i: (i, 0))
        return pl.pallas_call(
            body, name=name, grid=(R // tr,),
            in_specs=[spec, spec, spec, pl.BlockSpec((P, tr, C), lambda i: (0, i, 0))],
            out_specs=[spec] * 4, out_shape=out_shape, compiler_params=_params(1),
        )(w, m, v, g_parts)

    slots, me = own

    def body_own(me_ref, w_ref, m_ref, v_ref, g_ref, own_ref, og, od, om, ov):
        g = own_ref[...].astype(F32)
        for p in range(P):
            g = g + jnp.where(me_ref[0] == p, 0.0, g_ref[p].astype(F32))
        update(w_ref, m_ref, v_ref, g, og, od, om, ov)

    spec = pl.BlockSpec((tr, C), lambda i, me_ref: (i, 0))
    grid_spec = pltpu.PrefetchScalarGridSpec(
        num_scalar_prefetch=1, grid=(R // tr,),
        in_specs=[spec, spec, spec, pl.BlockSpec((P, tr, C), lambda i, me_ref: (0, i, 0)),
                  pl.BlockSpec((None, tr, C), lambda i, me_ref: (me_ref[0], i, 0))],
        out_specs=[spec] * 4)
    return pl.pallas_call(body_own, name=name, grid_spec=grid_spec, out_shape=out_shape,
                          compiler_params=_params(1))(me, w, m, v, g_parts, slots)


def _adamw_small(ws, ms, vs, gs):
    n = len(ws)
    c1 = 1.0 / (1.0 - ADAM_B1 ** ADAM_STEP)
    c2 = 1.0 / (1.0 - ADAM_B2 ** ADAM_STEP)

    def body(*refs):
        ins, outs = refs[:4 * n], refs[4 * n:]
        for i in range(n):
            w, m, v, g = ins[i][...], ins[n + i][...], ins[2 * n + i][...], ins[3 * n + i][...]
            m_new = ADAM_B1 * m + (1.0 - ADAM_B1) * g
            v_new = ADAM_B2 * v + (1.0 - ADAM_B2) * (g * g)
            outs[4 * i][...] = g
            outs[4 * i + 1][...] = -ADAM_LR * ((m_new * c1) / (jnp.sqrt(v_new * c2) + ADAM_EPS) + ADAM_WD * w)
            outs[4 * i + 2][...] = m_new
            outs[4 * i + 3][...] = v_new

    out_shape = [jax.ShapeDtypeStruct(w.shape, F32) for w in ws for _ in range(4)]
    return pl.pallas_call(body, name="adamw_small", out_shape=out_shape,
                          compiler_params=pltpu.CompilerParams(vmem_limit_bytes=V7X_VMEM_LIMIT))(*ws, *ms, *vs, *gs)


def _sum_parts(parts, loss_rows):
    P, R, C = parts.shape
    lo, hi = loss_rows

    def body(p_ref, o_ref, loss_ref):
        t = p_ref[0]
        for p in range(1, P):
            t = t + p_ref[p]
        o_ref[...] = t
        tot = jnp.sum(jnp.sum(o_ref[lo:hi, :], axis=1, keepdims=True), axis=0, keepdims=True)
        loss_ref[...] = jnp.broadcast_to(tot, loss_ref.shape)

    return pl.pallas_call(body, name="sum_small_grads",
                          out_shape=[jax.ShapeDtypeStruct((R, C), F32), jax.ShapeDtypeStruct((1, LANES), F32)],
                          compiler_params=pltpu.CompilerParams(vmem_limit_bytes=V7X_VMEM_LIMIT))(parts)


def _exchange(items, name):
    n = len(items)
    MESH = pl.DeviceIdType.MESH

    def body(*refs):
        src, dst = refs[:n], refs[n:2 * n]
        send_sems, recv_sems, local_sems = refs[2 * n:]
        x, y, c = lax.axis_index("x"), lax.axis_index("y"), lax.axis_index("c")
        me = 4 * x + 2 * y + c
        started = []
        for it, (_, per_peer) in enumerate(items):
            own = pltpu.make_async_copy(src[it].at[me] if per_peer else src[it], dst[it].at[me], local_sems.at[it])
            own.start()
            started.append(own)
        sends, recvs = [], []
        for k in range(1, N_DEV):
            px = 1 - x if k & 4 else x
            py = 1 - y if k & 2 else y
            pc = 1 - c if k & 1 else c
            peer = 4 * px + 2 * py + pc
            for it, (_, per_peer) in enumerate(items):
                s = src[it].at[peer] if per_peer else src[it]
                cp = pltpu.make_async_remote_copy(src_ref=s, dst_ref=dst[it].at[me], send_sem=send_sems.at[it, k - 1],
                                                  recv_sem=recv_sems.at[it, k - 1], device_id=(px, py, pc),
                                                  device_id_type=MESH)
                cp.start()
                sends.append(cp)
                recvs.append(pltpu.make_async_remote_copy(
                    src_ref=s, dst_ref=dst[it].at[peer], send_sem=send_sems.at[it, k - 1],
                    recv_sem=recv_sems.at[it, k - 1], device_id=(px, py, pc), device_id_type=MESH))
        for cp in recvs:
            cp.wait_recv()
        for cp in sends:
            cp.wait_send()
        for cp in started:
            cp.wait()

    any_spec = pl.BlockSpec(memory_space=pl.ANY)
    out_shape = []
    for a, per_peer in items:
        shp = a.shape if per_peer else (N_DEV,) + a.shape
        out_shape.append(jax.ShapeDtypeStruct(shp, a.dtype))
    return pl.pallas_call(
        body, name=name, in_specs=[any_spec] * n, out_specs=[any_spec] * n, out_shape=out_shape,
        scratch_shapes=[pltpu.SemaphoreType.DMA((n, N_DEV - 1)), pltpu.SemaphoreType.DMA((n, N_DEV - 1)),
                        pltpu.SemaphoreType.DMA((n,))],
    )(*[a for a, _ in items])


def _remote(src, dst, send_sem, recv_sem, device):
    return pltpu.make_async_remote_copy(src_ref=src, dst_ref=dst, send_sem=send_sem, recv_sem=recv_sem,
                                        device_id=device, device_id_type=pl.DeviceIdType.MESH)


def _mesh_place():
    x, y, c = lax.axis_index("x"), lax.axis_index("y"), lax.axis_index("c")
    other_chips = [(1 - x, y), (x, 1 - y), (1 - x, 1 - y)]
    return x, y, c, (x, y, 1 - c), other_chips


def _gather_all(items, name):
    n = len(items)

    def body(*refs):
        src, dst = refs[:n], refs[n:2 * n]
        send_sems, recv_sems, local_sems = refs[2 * n:]
        x, y, c, sibling, chips = _mesh_place()
        slot = lambda px, py, pc: 4 * px + 2 * py + pc
        me = slot(x, y, c)
        own = [pltpu.make_async_copy(src[it], dst[it].at[me], local_sems.at[it]) for it in range(n)]
        first = []
        for it in range(n):
            first.append(_remote(src[it], dst[it].at[me], send_sems.at[it, 0], recv_sems.at[it, 0], sibling))
            for j, chip in enumerate(chips):
                first.append(_remote(src[it], dst[it].at[me], send_sems.at[it, 1 + j], recv_sems.at[it, 1 + j],
                                     (*chip, c)))
        for cp in own + first:
            cp.start()
        passed = []
        for j, chip in enumerate(chips):
            blk = slot(*chip, c)
            for it in range(n):
                _remote(src[it], dst[it].at[blk], send_sems.at[it, 1 + j], recv_sems.at[it, 1 + j],
                        (*chip, c)).wait_recv()
                fwd = _remote(dst[it].at[blk], dst[it].at[blk], send_sems.at[it, 4 + j], recv_sems.at[it, 4 + j],
                              sibling)
                fwd.start()
                passed.append(fwd)
        for it in range(n):
            _remote(src[it], dst[it].at[slot(x, y, 1 - c)], send_sems.at[it, 0], recv_sems.at[it, 0],
                    sibling).wait_recv()
        for j, chip in enumerate(chips):
            for it in range(n):
                _remote(src[it], dst[it].at[slot(*chip, 1 - c)], send_sems.at[it, 4 + j], recv_sems.at[it, 4 + j],
                        sibling).wait_recv()
        for cp in first + passed:
            cp.wait_send()
        for cp in own:
            cp.wait()

    any_spec = pl.BlockSpec(memory_space=pl.ANY)
    return pl.pallas_call(
        body, name=name, in_specs=[any_spec] * n, out_specs=[any_spec] * n,
        out_shape=[jax.ShapeDtypeStruct((N_DEV,) + a.shape, a.dtype) for a in items],
        scratch_shapes=[pltpu.SemaphoreType.DMA((n, 7)), pltpu.SemaphoreType.DMA((n, 7)),
                        pltpu.SemaphoreType.DMA((n,))],
    )(*items)


def _peers():
    x, y, c = lax.axis_index("x"), lax.axis_index("y"), lax.axis_index("c")
    out = []
    for k in range(1, N_DEV):
        px = 1 - x if k & 4 else x
        py = 1 - y if k & 2 else y
        pc = 1 - c if k & 1 else c
        out.append((k, (px, py, pc), 4 * px + 2 * py + pc))
    return 4 * x + 2 * y + c, out


def _exchange_start(items, name, gather, carry=()):
    n, m = len(items), len(carry)

    def body(*refs):
        src, land = refs[:n], refs[n:2 * n]
        first_out = 2 * n + m
        send_sems, recv_sems = refs[first_out:first_out + n], refs[first_out + n:first_out + 2 * n]
        token = refs[-1]
        me, peers = _peers()
        for k, peer, slot in peers:
            for it in range(n):
                _remote(src[it] if gather else src[it].at[slot], land[it].at[me], send_sems[it], recv_sems[it],
                        peer).start()
        token[...] = jnp.zeros_like(token)

    hbm = pl.BlockSpec(memory_space=pltpu.HBM)
    sem = pl.BlockSpec(memory_space=pltpu.SEMAPHORE)
    land_shapes = [(N_DEV,) + (a.shape if gather else a.shape[1:]) for a in items]
    lands = [lax.empty(shp, a.dtype) for shp, a in zip(land_shapes, items)]
    through = list(items) + lands + list(carry)
    outs = pl.pallas_call(
        body, name=name,
        out_shape=(*[pltpu.SemaphoreType.DMA(())] * (2 * n), *[pltpu.HBM(a.shape, a.dtype) for a in through],
                   jax.ShapeDtypeStruct((8, LANES), F32)),
        in_specs=[hbm] * len(through),
        out_specs=(*[sem] * (2 * n), *[hbm] * len(through), pl.BlockSpec(memory_space=pltpu.VMEM)),
        input_output_aliases={i: 2 * n + i for i in range(len(through))},
        compiler_params=pltpu.CompilerParams(has_side_effects=pltpu.SideEffectType.DATAFLOW_SIDE_EFFECTING),
    )(*[pltpu.with_memory_space_constraint(a, pltpu.HBM) for a in through])
    return (list(outs[:n]), list(outs[n:2 * n]), list(outs[2 * n:3 * n]), list(outs[3 * n:4 * n]), outs[-1],
            list(outs[4 * n:4 * n + m]))


def _exchange_wait(send_sems, recv_sems, items, lands, after, name):
    n = len(items)

    def body(*refs):
        land = refs[n:2 * n]
        send_sems, recv_sems = refs[2 * n:3 * n], refs[3 * n:4 * n]
        me, peers = _peers()
        for it in range(n):
            seven = land[it].at[pl.ds(0, N_DEV - 1)]
            cp = _remote(seven, seven, send_sems[it], recv_sems[it], peers[0][1])
            cp.wait_send()
            cp.wait_recv()

    hbm = pl.BlockSpec(memory_space=pltpu.HBM)
    sem = pl.BlockSpec(memory_space=pltpu.SEMAPHORE)
    outs = pl.pallas_call(
        body, name=name,
        out_shape=tuple(pltpu.HBM(a.shape, a.dtype) for a in list(items) + list(lands)),
        in_specs=[hbm] * (2 * n) + [sem] * (2 * n) + [pl.BlockSpec(memory_space=pl.ANY)],
        out_specs=tuple([hbm] * (2 * n)),
        input_output_aliases={i: i for i in range(2 * n)},
        compiler_params=pltpu.CompilerParams(has_side_effects=pltpu.SideEffectType.DATAFLOW_SIDE_EFFECTING),
    )(*items, *lands, *send_sems, *recv_sems, after)
    return list(outs[:n]), list(outs[n:])


def _gelu_tanh(y):
    k = math.sqrt(2.0 / math.pi)
    t = jnp.tanh(k * (y + 0.044715 * y * y * y))
    return 0.5 * y * (1.0 + t), t


def _local_step(x, mod, target, W, late_weights, P, send_early):
    B, S, D = x.shape
    T = B * S
    TS = 512
    flat = lambda a: a.reshape(T, a.shape[-1])
    unflat = lambda a: a.reshape(B, S, a.shape[-1])
    mod_col = lambda i: (mod, D, i)

    def f_modnorm(xv, sc, sh, g):
        return (xv * _rms_scale(xv) * g) * (1.0 + sc) + sh

    (u1,) = _rowwise(f_modnorm, [(x, D, 0)], [mod_col(1), mod_col(0)], [P["g_mix"]],
                     [(D, BF16)], [], [], ts=TS, name="modnorm_mix")
    u1f = flat(u1)
    qkv, us, gates = (unflat(t) for t in _project_in(u1f, [W["w_qkv"], W["w_us"], W["w_gates"]], [F32, F32, BF16],
                                                      name="proj_in"))

    o_att, lse = _attention_fwd(qkv, P["slopes"])
    more_w, more_p = late_weights(o_att)
    W, P = {**W, **more_w}, {**P, **more_p}

    xs, y_mm = _scan_fwd(us, P["bb_big"], P["a_row"], P["cc_big"])

    def f_glu(ymm, usv, dsk, wg, bg):
        yv = ymm + dsk * usv
        ge, _ = _gelu_tanh(yv)
        pre = jnp.dot(ge.astype(BF16), wg, preferred_element_type=F32) + bg
        return yv, ge * _sigmoid(pre)

    y_s5, z = _rowwise(f_glu, [(y_mm, SSM_WIDTH, 0), (us, SSM_WIDTH, 0)], [], [P["d_skip"], W["w_glu"], P["b_glu"]],
                       [(SSM_WIDTH, F32), (SSM_WIDTH, BF16)], [], [], ts=TS, name="s5_glu")

    bga, bgs = P["b_gate"][:, :D], P["b_gate"][:, D:]

    def f_mixer_tail(ov, zv, ga, gs, xv, gt, sc, sh, w_att, w_ssm, w_o, bga_, bgs_, g):
        ya = jnp.dot(ov, w_att, preferred_element_type=F32)
        ys = jnp.dot(zv, w_ssm, preferred_element_type=F32)
        mg = (_sigmoid(ga + bga_) * ya + _sigmoid(gs + bgs_) * ys).astype(BF16)
        mx = jnp.dot(mg, w_o, preferred_element_type=F32)
        h = xv + gt * mx
        return ya, ys, mg, mx, h, (h * _rms_scale(h) * g) * (1.0 + sc) + sh

    y_att, y_ssm, merged, mix, h1, u2 = _rowwise(
        f_mixer_tail, [(o_att, ATT_WIDTH, 0), (z, SSM_WIDTH, 0), (gates, D, 0), (gates, D, 1), (x, D, 0)],
        [mod_col(2), mod_col(4), mod_col(3)], [W["w_proj_att"], W["w_proj_ssm"], W["w_out"], bga, bgs, P["g_ffn"]],
        [(D, BF16), (D, BF16), (D, BF16), (D, BF16), (D, F32), (D, BF16)], [], [], ts=TS, raw=(0, 1),
        name="mixer_tail")

    up = unflat(_up_fwd(flat(u2), W["w_up"], name="ffn_up"))
    ff = _conv_fwd(up, P["w_conv"], P["b_conv"])

    def f_head(ffv, h1v, tg, gt, g, w_dn):
        dn = jnp.dot(ffv, w_dn, preferred_element_type=F32)
        h2 = h1v + gt * dn
        r = _rms_scale(h2)
        nh = h2 * r
        e = nh * g - tg
        dy = e * (1.0 / D)
        gy = dy * g
        dh = r * (gy - nh * jnp.mean(gy * nh, axis=-1, keepdims=True))
        return (dh, dh * gt, _col_sum(dh * dn), _col_sum(dy * nh), _col_sum(e * e) * (0.5 / D))

    dh2, d_down, d_gt2, d_g_final, loss_cols = _rowwise(
        f_head, [(ff, D_FF, 0), (h1, D, 0), (target, D, 0)], [mod_col(5)], [P["g_final"], W["w_down"]],
        [(D, BF16), (D, BF16)], [D], [(1, D), (1, D)], ts=TS, raw=(0,), name="ffn_down_head_loss")

    d_downf = flat(d_down)
    d_ff = unflat(_matmul(d_downf, W["w_down"], tb=True, out_dtype=BF16, name="ffn_down_dx"))
    d_w_down = _matmul(flat(ff), d_downf, ta=True, out_dtype=BF16, name="ffn_down_dw")
    d_up, d_w_conv, d_b_conv = _conv_bwd(up, d_ff, P["w_conv"], P["b_conv"])
    d_upf = flat(d_up)
    d_u2 = unflat(_up_dx(d_upf, W["w_up"], name="ffn_up_dx"))
    d_w_up = _up_dw(flat(u2), d_upf, name="ffn_up_dw")
    token, _ = send_early(dict(w_down=d_w_down.reshape(N_DEV, D_FF // N_DEV, D), w_up=d_w_up))
    g_ffn_after = P["g_ffn"] + token[0:1, 0:1]

    def f_modnorm_bwd(du, h, dres, mx, sc, gt, g):
        r = _rms_scale(h)
        nh = h * r
        dn = du * (1.0 + sc)
        gy = dn * g
        dh = dres + r * (gy - nh * jnp.mean(gy * nh, axis=-1, keepdims=True))
        return (dh, dh * gt, _col_sum(du), _col_sum(du * nh * g), _col_sum(dh * mx), _col_sum(dn * nh))

    dh1, d_mix, d_sh2, d_sc2, d_gt1, d_g_ffn = _rowwise(
        f_modnorm_bwd, [(d_u2, D, 0), (h1, D, 0), (dh2, D, 0), (mix, D, 0)], [mod_col(4), mod_col(2)], [g_ffn_after],
        [(D, BF16), (D, BF16)], [D, D, D], [(1, D)], ts=TS, name="modnorm_ffn_bwd")

    d_mixf = flat(d_mix)
    d_w_out = _matmul(flat(merged), d_mixf, ta=True, out_dtype=BF16, name="proj_out_dw")

    def f_mixer_tail_bwd(dmx, ga, gs, ya, ys, w_o, w_att, w_ssm, bga_, bgs_):
        dm = lax.dot_general(dmx, w_o, _NT, preferred_element_type=F32)
        sa, ss = _sigmoid(ga + bga_), _sigmoid(gs + bgs_)
        dga = dm * ya * sa * (1.0 - sa)
        dgs = dm * ys * ss * (1.0 - ss)
        dya, dys = (dm * sa).astype(BF16), (dm * ss).astype(BF16)
        return (dya, dys, jnp.concatenate([dga, dgs], axis=1),
                lax.dot_general(dya, w_att, _NT, preferred_element_type=F32),
                lax.dot_general(dys, w_ssm, _NT, preferred_element_type=F32), _col_sum(dga), _col_sum(dgs))

    d_y_att, d_y_ssm, d_gates, d_o_att, d_z, d_bga, d_bgs = _rowwise(
        f_mixer_tail_bwd, [(d_mix, D, 0), (gates, D, 0), (gates, D, 1), (y_att, D, 0), (y_ssm, D, 0)], [],
        [W["w_out"], W["w_proj_att"], W["w_proj_ssm"], bga, bgs],
        [(D, BF16), (D, BF16), (2 * D, BF16), (ATT_WIDTH, F32), (SSM_WIDTH, BF16)], [], [(1, D), (1, D)], ts=TS,
        raw=(0,), name="mixer_tail_bwd")

    d_yaf, d_ysf = flat(d_y_att), flat(d_y_ssm)
    d_w_proj_att = _matmul(flat(o_att), d_yaf, ta=True, out_dtype=BF16, name="proj_att_dw")
    d_w_proj_ssm = _matmul(flat(z), d_ysf, ta=True, out_dtype=BF16, name="proj_ssm_dw")

    def f_glu_bwd(yv, dz, usv, dsk, wg, bg):
        ge, t = _gelu_tanh(yv)
        pre = jnp.dot(ge.astype(BF16), wg, preferred_element_type=F32) + bg
        sg = _sigmoid(pre)
        dpre = dz * ge * sg * (1.0 - sg)
        dge = dz * sg + lax.dot_general(dpre.astype(BF16), wg, _NT, preferred_element_type=F32)
        k = math.sqrt(2.0 / math.pi)
        dgelu = 0.5 * (1.0 + t) + 0.5 * yv * (1.0 - t * t) * k * (1.0 + 3.0 * 0.044715 * yv * yv)
        dy = dge * dgelu
        dwg = lax.dot_general(ge.astype(BF16), dpre.astype(BF16), _TN, preferred_element_type=F32)
        return dy, dy * dsk, dwg, _col_sum(dpre), _col_sum(dy * usv)

    d_y_s5, d_us_skip, d_w_glu, d_b_glu, d_d_skip = _rowwise(
        f_glu_bwd, [(y_s5, SSM_WIDTH, 0), (d_z, SSM_WIDTH, 0), (us, SSM_WIDTH, 0)], [],
        [P["d_skip"], W["w_glu"], P["b_glu"]],
        [(SSM_WIDTH, BF16), (SSM_WIDTH, F32)], [], [(SSM_WIDTH, SSM_WIDTH), (1, SSM_WIDTH), (1, SSM_WIDTH)],
        ts=TS, name="s5_glu_bwd")
    d_us_parts, g_ab, d_bb, d_cc = _scan_bwd(d_y_s5, us, P["bb_big"], P["cc_big"], xs, P["a_row"])

    token, _ = send_early(dict(
        w_out=d_w_out.reshape(N_DEV, D // N_DEV, D), w_proj_att=_cols_to_slots(d_w_proj_att),
        w_proj_ssm=_cols_to_slots(d_w_proj_ssm),
        w_glu=d_w_glu.astype(BF16).reshape(N_DEV, SSM_WIDTH // N_DEV, SSM_WIDTH),
        w_conv=_cols_to_slots(d_w_conv.astype(BF16))))
    d_qkv = _attention_bwd(qkv, o_att, d_o_att, lse, P["slopes"] + token[0, 0])

    def f_add(*parts):
        return sum(parts[1:], parts[0])

    n_parts = d_us_parts.shape[0]
    stacked = d_us_parts.reshape(n_parts * B, S, SSM_WIDTH)
    (d_us,) = _rowwise(f_add, [(d_us_skip, SSM_WIDTH, 0)] + [(stacked, SSM_WIDTH, 0, j * B) for j in range(n_parts)],
                       [], [],
                       [(SSM_WIDTH, BF16)], [], [], ts=TS, name="s5_input_grad")
    d_qkvf = flat(d_qkv)
    d_usf = flat(d_us)
    d_gatesf = flat(d_gates)
    d_w_in_t = jnp.concatenate(
        [_unpair_qkv_rows(_matmul(d_qkvf, u1f, ta=True, out_dtype=BF16, name="proj_qkv_dw")),
         _matmul(d_usf, u1f, ta=True, out_dtype=BF16, name="proj_ssm_in_dw"),
         _matmul(d_gatesf, u1f, ta=True, out_dtype=BF16, name="proj_gates_dw")], axis=0)
    token, (w_qkv, w_us, w_gates) = send_early(dict(w_in=d_w_in_t.reshape(N_DEV, -1, D)),
                                               carry=[W["w_qkv"], W["w_us"], W["w_gates"]])
    d_u1 = unflat(_project_back([d_qkvf, d_usf, d_gatesf], [w_qkv, w_us, w_gates], name="proj_in_dx"))

    def f_modnorm_bwd_in(du, h, dres, sc, g):
        r = _rms_scale(h)
        nh = h * r
        dn = du * (1.0 + sc)
        gy = dn * g
        dh = dres + r * (gy - nh * jnp.mean(gy * nh, axis=-1, keepdims=True))
        return (dh, _col_sum(du), _col_sum(du * nh * g), _col_sum(dn * nh))

    grad_x, d_sh1, d_sc1, d_g_mix = _rowwise(
        f_modnorm_bwd_in, [(d_u1, D, 0), (x, D, 0), (dh1, D, 0)], [mod_col(1)], [P["g_mix"] + token[0:1, 0:1]],
        [(D, F32)], [D, D], [(1, D)], ts=TS, name="modnorm_mix_bwd")

    d_mod = jnp.concatenate([d_sh1, d_sc1, d_gt1, d_sh2, d_sc2, d_gt2], axis=-1)
    g_ab_re, g_ab_im = _deinterleave(g_ab)
    d_bb_re, d_bb_im = _deinterleave(d_bb)
    d_cc_re, d_cc_im = (t.T for t in _deinterleave(d_cc.T))
    small = dict(g_mix=d_g_mix, b_gate=jnp.concatenate([d_bga, d_bgs], axis=1), g_ab_re=g_ab_re, g_ab_im=g_ab_im,
                 d_bb_re=d_bb_re, d_bb_im=d_bb_im, d_cc_re=d_cc_re, d_cc_im=d_cc_im, d_skip=d_d_skip,
                 b_glu=d_b_glu, g_ffn=d_g_ffn, b_conv=d_b_conv, g_final=d_g_final, loss_cols=loss_cols)
    return grad_x, d_mod, small


def _block_diag_in(bb):
    t = bb.reshape(SSM_GROUPS, SSM_STATE, SSM_GROUP_CH)
    eye = jnp.eye(SSM_GROUPS, dtype=bb.dtype)
    return jnp.einsum("gnc,gh->gchn", t, eye).reshape(SSM_WIDTH, SSM_COLS)


def _block_diag_out(cm):
    eye = jnp.eye(SSM_GROUPS, dtype=cm.dtype)
    return jnp.einsum("gcn,gh->gnhc", cm, eye).reshape(SSM_COLS, SSM_WIDTH)


def _diag_blocks_in(m):
    t = m.reshape(SSM_GROUPS, SSM_GROUP_CH, SSM_GROUPS, SSM_STATE)
    idx = jnp.arange(SSM_GROUPS)
    return t[idx, :, idx, :].transpose(0, 2, 1).reshape(SSM_COLS, SSM_GROUP_CH)


def _diag_blocks_out(m):
    t = m.reshape(SSM_GROUPS, SSM_STATE, SSM_GROUPS, SSM_GROUP_CH)
    idx = jnp.arange(SSM_GROUPS)
    return t[idx, :, idx, :].transpose(0, 2, 1)


def _pair_qkv_rows(w):
    return w.reshape(3, N_HEADS // 2, LANES, w.shape[1]).swapaxes(0, 1).reshape(w.shape)


def _unpair_qkv_rows(w):
    return w.reshape(N_HEADS // 2, 3, LANES, w.shape[1]).swapaxes(0, 1).reshape(w.shape)


def _interleave(re, im):
    lead = re.shape[:-1]
    g = lambda a: a.reshape(lead + (SSM_COLS // SCAN_COLS, 1, SCAN_COLS))
    return jnp.concatenate([g(re), g(im)], axis=-2).reshape(lead + (2 * SSM_COLS,))


def _deinterleave(x):
    lead = x.shape[:-1]
    t = x.reshape(lead + (SSM_COLS // SCAN_COLS, 2, SCAN_COLS))
    return t[..., 0, :].reshape(lead + (SSM_COLS,)), t[..., 1, :].reshape(lead + (SSM_COLS,))


def _cols_to_slots(g):
    R = g.shape[0]
    return g.reshape(R, N_DEV, g.shape[1] // N_DEV).transpose(1, 0, 2)


def _slots_to_cols(g):
    return g.transpose(1, 0, 2).reshape(g.shape[1], N_DEV * g.shape[2])


SMALL_ORDER = ("b_ada", "g_mix", "b_gate", "a_re", "a_im", "log_dt", "b_re", "b_im", "c_re", "c_im", "d_skip",
               "b_glu", "g_ffn", "b_conv", "g_final")


def _pack(arrs):
    pieces, offs, row = [], [], 0
    for a in arrs:
        f = a.reshape(-1).astype(F32)
        n = f.shape[0]
        rows = -(-n // LANES)
        pieces.append(jnp.pad(f, (0, rows * LANES - n)))
        offs.append((row, n))
        row += rows
    return jnp.concatenate(pieces).reshape(row, LANES), offs


def _unpack(packed, offs, shapes):
    flat = packed.reshape(-1)
    return [flat[r * LANES:r * LANES + n].reshape(s) for (r, n), s in zip(offs, shapes)]


def kernel(x, c, w_ada, b_ada, g_mix, w_in, b_gate, a_re, a_im, log_dt, b_re, b_im, c_re, c_im, d_skip, w_glu, b_glu, w_proj_att, w_proj_ssm, w_out, g_ffn, w_up, w_conv, b_conv, w_down, g_final, loss_target, m_w_ada, m_b_ada, m_g_mix, m_w_in, m_b_gate, m_a_re, m_a_im, m_log_dt, m_b_re, m_b_im, m_c_re, m_c_im, m_d_skip, m_w_glu, m_b_glu, m_w_proj_att, m_w_proj_ssm, m_w_out, m_g_ffn, m_w_up, m_w_conv, m_b_conv, m_w_down, m_g_final, v_w_ada, v_b_ada, v_g_mix, v_w_in, v_b_gate, v_a_re, v_a_im, v_log_dt, v_b_re, v_b_im, v_c_re, v_c_im, v_d_skip, v_w_glu, v_b_glu, v_w_proj_att, v_w_proj_ssm, v_w_out, v_g_ffn, v_w_up, v_w_conv, v_b_conv, v_w_down, v_g_final):
    args = dict(locals())
    B, S, D = x.shape
    me = 4 * lax.axis_index("x") + 2 * lax.axis_index("y") + lax.axis_index("c")
    bf = lambda w: w[0].astype(BF16)

    c_slots, w_in_slots = _gather_all([c, w_in[0].T.astype(BF16)], name="gather_first_weights")
    c_all = c_slots.reshape(N_DEV * B, D)
    w_in_t = w_in_slots.reshape(-1, D)
    n_qkv = 3 * ATT_WIDTH
    W = dict(w_qkv=_pair_qkv_rows(w_in_t[:n_qkv]), w_us=w_in_t[n_qkv:n_qkv + SSM_WIDTH],
             w_gates=w_in_t[n_qkv + SSM_WIDTH:])

    n_ada = w_ada.shape[2]
    b_ada_cols = lax.dynamic_slice(b_ada, (0, me * n_ada), (1, n_ada))
    mod_part = _ada_fwd(c_all, w_ada[0], b_ada_cols)
    (mod_slots,) = _exchange([(mod_part.reshape(N_DEV, B, n_ada), True)], name="scatter_modulation")
    mod = mod_slots.transpose(1, 0, 2).reshape(B, 1, 6 * D)

    later = [bf(w_glu), bf(w_proj_att), bf(w_proj_ssm), bf(w_out), bf(w_up), w_conv[0], bf(w_down)]
    later_sems = _exchange_start(later, "start_later_weights", gather=True, carry=[mod])
    (mod,) = later_sems[5]

    def late_weights(after):
        _, lands = _exchange_wait(*later_sems[:4], after, name="wait_later_weights")
        g = [lax.dynamic_update_index_in_dim(land, a, me, 0) for land, a in zip(lands, later)]
        more_w = dict(w_glu=g[0].reshape(SSM_WIDTH, SSM_WIDTH), w_proj_att=_slots_to_cols(g[1]),
                      w_proj_ssm=_slots_to_cols(g[2]), w_out=g[3].reshape(D, D), w_up=g[4],
                      w_down=g[6].reshape(D_FF, D))
        return more_w, dict(w_conv=_slots_to_cols(g[5]))

    ab_re, ab_im, f_re, f_im = _s5_params(a_re[0], a_im[0], log_dt[0].reshape(SSM_GROUPS, 1))
    col = lambda a: a.reshape(SSM_COLS, 1)
    b_re2, b_im2 = b_re[0].reshape(SSM_COLS, SSM_GROUP_CH), b_im[0].reshape(SSM_COLS, SSM_GROUP_CH)
    bb_re, bb_im = _s5_input_matrix(col(f_re), col(f_im), b_re2, b_im2)
    slopes = jnp.asarray([2.0 ** (-8.0 * (h + 1) / N_HEADS) for h in range(N_HEADS)], F32)
    P = dict(g_mix=g_mix, g_ffn=g_ffn, g_final=g_final.reshape(1, D), b_gate=b_gate, d_skip=d_skip, b_glu=b_glu,
             b_conv=b_conv, slopes=slopes,
             a_row=_interleave(ab_re.reshape(1, SSM_COLS), ab_im.reshape(1, SSM_COLS)),
             bb_big=_interleave(_block_diag_in(bb_re), _block_diag_in(bb_im)),
             cc_big=_interleave(_block_diag_out(c_re[0]).T, -_block_diag_out(c_im[0]).T).T)

    in_flight = []

    def send_early(grads, carry=()):
        names = list(grads)
        handles = _exchange_start([grads[n] for n in names], "start_gradients_%d" % len(in_flight), gather=False,
                                  carry=carry)
        in_flight.append((names,) + handles[:4])
        return handles[4], handles[5]

    grad_x, d_mod, small = _local_step(x, mod, loss_target, W, late_weights, P, send_early)

    small_list = [small["loss_cols"], small["g_mix"], small["b_gate"], small["g_ab_re"], small["g_ab_im"],
                  _diag_blocks_in(small["d_bb_re"]), _diag_blocks_in(small["d_bb_im"]),
                  _diag_blocks_out(small["d_cc_re"]), -_diag_blocks_out(small["d_cc_im"]),
                  small["g_ffn"], small["b_conv"], small["g_final"], small["d_skip"], small["b_glu"]]
    small_packed, small_offs = _pack(small_list)
    small_all, dmod_slots = _gather_all([small_packed, d_mod.reshape(B, 6 * D)], name="gather_small_gradients")

    out = {}

    def update(name, parts, own=None):
        view = (lambda a: a[0].T) if name == "w_in" else (lambda a: a[0])
        back = (lambda a: a.T[None]) if name == "w_in" else (lambda a: a[None])
        g, dl, mn, vn = _adamw(view(args[name]), view(args["m_" + name]), view(args["v_" + name]), parts,
                               name="adamw_" + name, own=own)
        for key, val in (("grad_", g), ("delta_", dl), ("new_m_", mn), ("new_v_", vn)):
            out[key + name] = back(val)

    my_slot = me.astype(jnp.int32).reshape(1)
    for i, (names, send_sems, recv_sems, sent, lands) in enumerate(in_flight):
        sent, lands = _exchange_wait(send_sems, recv_sems, sent, lands, dmod_slots, name="wait_gradients_%d" % i)
        for name, own_slots, landed in zip(names, sent, lands):
            update(name, landed, own=(own_slots, my_slot))

    dmod_all = dmod_slots.reshape(N_DEV * B, 6 * D)
    dmod_cols = lax.dynamic_slice(dmod_all, (0, me * n_ada), (N_DEV * B, n_ada))
    d_w_ada, d_b_ada = _ada_bwd(c_all, dmod_all, dmod_cols)
    update("w_ada", d_w_ada[None])

    loss_row, loss_n = small_offs[0]
    small_sum, loss_vec = _sum_parts(small_all, (loss_row, loss_row + loss_n // LANES))
    shapes = [(1, D), (1, D), (1, 2 * D), (SSM_GROUPS, SSM_STATE), (SSM_GROUPS, SSM_STATE), (SSM_COLS, SSM_GROUP_CH),
              (SSM_COLS, SSM_GROUP_CH), (1, SSM_GROUPS, SSM_GROUP_CH, SSM_STATE),
              (1, SSM_GROUPS, SSM_GROUP_CH, SSM_STATE), (1, D), (1, D_FF), (D,), (1, SSM_WIDTH), (1, SSM_WIDTH)]
    (_, s_g_mix, s_b_gate, s_ab_re, s_ab_im, s_bb_re, s_bb_im, s_c_re, s_c_im, s_g_ffn, s_b_conv, s_g_final,
     s_d_skip, s_b_glu) = _unpack(small_sum, small_offs, shapes)
    d_b_re2, d_b_im2, d_f_re, d_f_im = _s5_input_matrix_bwd(col(f_re), col(f_im), b_re2, b_im2, s_bb_re, s_bb_im)
    d_a_re, d_a_im, d_log_dt = _s5_params_bwd(a_re[0], a_im[0], log_dt[0].reshape(SSM_GROUPS, 1), s_ab_re, s_ab_im,
                                              d_f_re.reshape(SSM_GROUPS, SSM_STATE),
                                              d_f_im.reshape(SSM_GROUPS, SSM_STATE))
    grads_small = dict(b_ada=d_b_ada, g_mix=s_g_mix, b_gate=s_b_gate, a_re=d_a_re[None], a_im=d_a_im[None],
                       log_dt=d_log_dt.reshape(1, SSM_GROUPS), b_re=d_b_re2.reshape(b_re.shape),
                       b_im=d_b_im2.reshape(b_im.shape), c_re=s_c_re, c_im=s_c_im, d_skip=s_d_skip, b_glu=s_b_glu,
                       g_ffn=s_g_ffn, b_conv=s_b_conv, g_final=s_g_final)
    flat2 = lambda a: a.reshape(-1, a.shape[-1])
    res = _adamw_small([flat2(args[n]) for n in SMALL_ORDER], [flat2(args["m_" + n]) for n in SMALL_ORDER],
                       [flat2(args["v_" + n]) for n in SMALL_ORDER],
                       [flat2(grads_small[n].reshape(args[n].shape)) for n in SMALL_ORDER])
    for i, n in enumerate(SMALL_ORDER):
        for k, key in enumerate(("grad_", "delta_", "new_m_", "new_v_")):
            out[key + n] = res[4 * i + k].reshape(args[n].shape)

    order = ["w_ada", "b_ada", "g_mix", "w_in", "b_gate", "a_re", "a_im", "log_dt", "b_re", "b_im", "c_re", "c_im",
             "d_skip", "w_glu", "b_glu", "w_proj_att", "w_proj_ssm", "w_out", "g_ffn", "w_up", "w_conv", "b_conv",
             "w_down", "g_final"]
    loss = loss_vec[0, 0]
    return (loss, grad_x, *[out[k + n] for k in ("grad_", "delta_", "new_m_", "new_v_") for n in order])
```

```python
import math

import jax
import jax.numpy as jnp
from jax import lax
from jax.experimental import pallas as pl
from jax.experimental.pallas import tpu as pltpu

F32 = jnp.float32
BF16 = jnp.bfloat16

N_DEV = 8
D_MODEL = 1024
N_HEADS = 8
HEAD_DIM = 64
ATT_WIDTH = N_HEADS * HEAD_DIM
DILATIONS = (1, 4, 16)
WIN = 128
SSM_GROUPS = 16
SSM_GROUP_CH = 16
SSM_WIDTH = SSM_GROUPS * SSM_GROUP_CH
SSM_STATE = 64
SSM_COLS = SSM_GROUPS * SSM_STATE
D_FF = 2048
EPS = 1e-6
NEG_INF = -1e30
ADAM_LR, ADAM_B1, ADAM_B2, ADAM_EPS, ADAM_WD, ADAM_STEP = 0.001, 0.9, 0.999, 1e-08, 0.01, 10

V7X_VMEM_LIMIT = 56 * 1024 * 1024
LANES = 128


def _params(n_grid):
    return pltpu.CompilerParams(dimension_semantics=("arbitrary",) * n_grid,
                                vmem_limit_bytes=V7X_VMEM_LIMIT)


def _tile(n, pref):
    if n <= pref:
        return n
    t = (pref // LANES) * LANES
    while t > 0:
        if n % t == 0:
            return t
        t -= LANES
    return n


def _matmul(a, b, *, ta=False, tb=False, out_dtype=F32, name):
    if ta:
        K, M = a.shape
    else:
        M, K = a.shape
    if tb:
        N, K2 = b.shape
    else:
        K2, N = b.shape
    assert K == K2, (a.shape, b.shape)
    if ta:
        tm, tn, tk = _tile(M, 1024), _tile(N, 2048), _tile(K, 1024)
    else:
        tm, tk = _tile(M, 512), _tile(K, 4096)
        tn = _tile(N, 2048 if K <= 2048 else 1024)
    nk = K // tk
    dn = (((0,) if ta else (1,), (1,) if tb else (0,)), ((), ()))

    def body(a_ref, b_ref, o_ref, acc_ref):
        k = pl.program_id(2)
        part = lax.dot_general(a_ref[...].astype(BF16), b_ref[...].astype(BF16), dn, preferred_element_type=F32)
        if nk == 1:
            o_ref[...] = part.astype(o_ref.dtype)
            return

        @pl.when(k == 0)
        def _():
            acc_ref[...] = jnp.zeros_like(acc_ref)

        acc_ref[...] += part

        @pl.when(k == nk - 1)
        def _():
            o_ref[...] = acc_ref[...].astype(o_ref.dtype)

    a_spec = (pl.BlockSpec((tk, tm), lambda j, i, k: (k, i)) if ta
              else pl.BlockSpec((tm, tk), lambda j, i, k: (i, k)))
    b_spec = (pl.BlockSpec((tn, tk), lambda j, i, k: (j, k)) if tb
              else pl.BlockSpec((tk, tn), lambda j, i, k: (k, j)))
    return pl.pallas_call(
        body, name=name, grid=(N // tn, M // tm, nk),
        in_specs=[a_spec, b_spec],
        out_specs=pl.BlockSpec((tm, tn), lambda j, i, k: (i, j)),
        out_shape=jax.ShapeDtypeStruct((M, N), out_dtype),
        scratch_shapes=[pltpu.VMEM((tm, tn) if nk > 1 else (8, LANES), F32)],
        compiler_params=_params(3),
    )(a, b)


HALF = 256
UP_SLOTS = N_DEV // 2
UP_GROUP = 4 * HALF


def _group_weight(w_ref):
    return jnp.concatenate([w_ref[0, :, :HALF], w_ref[1, :, :HALF], w_ref[0, :, HALF:], w_ref[1, :, HALF:]], axis=1)


def _up_weight_spec(K, index):
    return pl.BlockSpec((2, None, K, 2 * HALF), index)


def _up_fwd(a, w3, name):
    M, K = a.shape
    tm = _tile(M, 1024)

    def body(a_ref, w_ref, o_ref):
        o_ref[...] = jnp.dot(a_ref[...].astype(BF16), _group_weight(w_ref),
                             preferred_element_type=F32).astype(o_ref.dtype)

    return pl.pallas_call(
        body, name=name, grid=(UP_SLOTS, M // tm),
        in_specs=[pl.BlockSpec((tm, K), lambda j, i: (i, 0)), _up_weight_spec(K, lambda j, i: (0, j, 0, 0))],
        out_specs=pl.BlockSpec((tm, UP_GROUP), lambda j, i: (i, j)),
        out_shape=jax.ShapeDtypeStruct((M, UP_SLOTS * UP_GROUP), BF16), compiler_params=_params(2),
    )(a, w3.reshape(2, UP_SLOTS, K, 2 * HALF))


def _up_dx(d, w3, name):
    M = d.shape[0]
    K = w3.shape[1]
    tm = _tile(M, 1024)

    def body(d_ref, w_ref, o_ref, acc_ref):
        j = pl.program_id(1)

        @pl.when(j == 0)
        def _():
            acc_ref[...] = jnp.zeros_like(acc_ref)

        acc_ref[...] += lax.dot_general(d_ref[...], _group_weight(w_ref), _NT, preferred_element_type=F32)

        @pl.when(j == UP_SLOTS - 1)
        def _():
            o_ref[...] = acc_ref[...].astype(o_ref.dtype)

    return pl.pallas_call(
        body, name=name, grid=(M // tm, UP_SLOTS),
        in_specs=[pl.BlockSpec((tm, UP_GROUP), lambda i, j: (i, j)), _up_weight_spec(K, lambda i, j: (0, j, 0, 0))],
        out_specs=pl.BlockSpec((tm, K), lambda i, j: (i, 0)),
        out_shape=jax.ShapeDtypeStruct((M, K), BF16), scratch_shapes=[pltpu.VMEM((tm, K), F32)],
        compiler_params=_params(2),
    )(d, w3.reshape(2, UP_SLOTS, K, 2 * HALF))


def _up_dw(a, d, name):
    M, K = a.shape
    tk = _tile(M, 1024)
    nk = M // tk

    def body(a_ref, d_ref, o_ref, acc_ref):
        k = pl.program_id(1)

        @pl.when(k == 0)
        def _():
            acc_ref[...] = jnp.zeros_like(acc_ref)

        acc_ref[...] += lax.dot_general(a_ref[...], d_ref[...], _TN, preferred_element_type=F32)

        @pl.when(k == nk - 1)
        def _():
            for half in range(2):
                for part in range(2):
                    lo = (2 * half + part) * HALF
                    o_ref[part, :, half * HALF:(half + 1) * HALF] = acc_ref[:, lo:lo + HALF].astype(o_ref.dtype)

    out = pl.pallas_call(
        body, name=name, grid=(UP_SLOTS, nk),
        in_specs=[pl.BlockSpec((tk, K), lambda j, k: (k, 0)), pl.BlockSpec((tk, UP_GROUP), lambda j, k: (k, j))],
        out_specs=_up_weight_spec(K, lambda j, k: (0, j, 0, 0)),
        out_shape=jax.ShapeDtypeStruct((2, UP_SLOTS, K, 2 * HALF), BF16),
        scratch_shapes=[pltpu.VMEM((K, UP_GROUP), F32)], compiler_params=_params(2),
    )(a, d)
    return out.reshape(N_DEV, K, 2 * HALF)


def _rowwise(fn, rows, bvecs, consts, out_rows, out_b, out_g, *, ts, name, raw=()):
    B, S = rows[0][0].shape[:2]
    nin = len(rows) + len(bvecs) + len(consts)
    nr, nb, ng = len(out_rows), len(out_b), len(out_g)

    def body(*refs):
        b = pl.program_id(0)
        s = pl.program_id(1)
        vals = [r[...] for r in refs[:nin]]
        vals[:len(rows)] = [v if i in raw else v.astype(F32) for i, v in enumerate(vals[:len(rows)])]
        outs = fn(*vals)
        if not isinstance(outs, (tuple, list)):
            outs = (outs,)
        orefs = refs[nin:]
        for i in range(nr):
            orefs[i][...] = outs[i].astype(orefs[i].dtype)
        for i in range(nb):
            ref = orefs[nr + i]

            @pl.when(s == 0)
            def _(ref=ref):
                ref[...] = jnp.zeros_like(ref)

            ref[...] += outs[nr + i]
        for i in range(ng):
            ref = orefs[nr + nb + i]

            @pl.when((s == 0) & (b == 0))
            def _(ref=ref):
                ref[...] = jnp.zeros_like(ref)

            ref[...] += outs[nr + nb + i]

    rows = [r if len(r) == 4 else r + (0,) for r in rows]
    in_specs = ([pl.BlockSpec((None, ts, cb), lambda b, s, ci=ci, b0=b0: (b0 + b, s, ci)) for (_, cb, ci, b0) in rows]
                + [pl.BlockSpec((None, 1, cb), lambda b, s, ci=ci: (b, 0, ci)) for (_, cb, ci) in bvecs]
                + [pl.BlockSpec(a.shape, lambda b, s: (0, 0)) for a in consts])
    out_shape = ([jax.ShapeDtypeStruct((B, S, c), dt) for (c, dt) in out_rows]
                 + [jax.ShapeDtypeStruct((B, 1, c), F32) for c in out_b]
                 + [jax.ShapeDtypeStruct(rc, F32) for rc in out_g])
    out_specs = ([pl.BlockSpec((None, ts, c), lambda b, s: (b, s, 0)) for (c, _) in out_rows]
                 + [pl.BlockSpec((None, 1, c), lambda b, s: (b, 0, 0)) for c in out_b]
                 + [pl.BlockSpec(rc, lambda b, s: (0, 0)) for rc in out_g])
    args = [r[0] for r in rows] + [a for (a, _, _) in bvecs] + list(consts)
    return pl.pallas_call(
        body, name=name, grid=(B, S // ts), in_specs=in_specs, out_specs=out_specs,
        out_shape=out_shape, compiler_params=_params(2),
    )(*args)


def _col_sum(v):
    return jnp.sum(v, axis=0, keepdims=True)


def _rms_scale(h):
    return lax.rsqrt(jnp.mean(h * h, axis=-1, keepdims=True) + EPS)


def _sigmoid(v):
    return 0.5 * (1.0 + jnp.tanh(0.5 * v))


ATT_SCALE = HEAD_DIM ** -0.5
COPY_ROWS = 256
_NT = (((1,), (1,)), ((), ()))
_TN = (((0,), (0,)), ((), ()))


def _row_chunks(d, seq):
    sub = seq // d
    out = []
    for r in range(d):
        for c0 in range(0, sub, COPY_ROWS):
            n = min(COPY_ROWS, sub - c0)
            out.append((pl.ds(r + c0 * d, n, stride=d), r * sub + c0, n))
    return out


ATT_UNROLL = 8
KEYS = 2 * WIN


def _zero_once(refs):
    @pl.when((pl.program_id(0) == 0) & (pl.program_id(1) == 0))
    def _():
        for r in refs:
            r[...] = jnp.zeros_like(r)


def _pair_bias(bias_ref, slopes_ref, hp, d, key_major):
    shape = (KEYS, WIN) if key_major else (WIN, KEYS)
    qi = lax.broadcasted_iota(jnp.int32, shape, 1 if key_major else 0)
    kj = lax.broadcasted_iota(jnp.int32, shape, 0 if key_major else 1)
    dist = WIN + qi - kj
    valid = (dist >= 0) & (dist <= WIN)
    distf = dist.astype(F32)
    for h in range(2):
        slope_d = slopes_ref[2 * hp + h] * float(d)
        with_prev = jnp.where(valid, -(slope_d * distf), NEG_INF)
        no_prev = jnp.where(kj >= WIN, with_prev, NEG_INF)
        span = slice(h * KEYS, (h + 1) * KEYS)
        if key_major:
            bias_ref[1, span, :] = with_prev
            bias_ref[0, span, :] = no_prev
        else:
            bias_ref[1, :, span] = with_prev
            bias_ref[0, :, span] = no_prev


def _stack_heads(v):
    first = lax.broadcasted_iota(jnp.int32, v.shape, 1) < HEAD_DIM
    zero = jnp.zeros_like(v)
    return jnp.concatenate([jnp.where(first, v, zero), jnp.where(first, zero, v)], axis=0)


def _per_head(c0, c1, n):
    return jnp.where(lax.broadcasted_iota(jnp.int32, (n, LANES), 1) < HEAD_DIM, c0, c1)


def _qkv_spec(seq, j):
    return pl.BlockSpec((None, seq, LANES), lambda b, hp: (b, 0, 3 * hp + j))


def _attention_fwd(qkv, slopes):
    B, S, _ = qkv.shape
    n_blk = S // WIN
    n_pair = N_HEADS // 2

    def body(slopes_ref, q_ref, k_ref, v_ref, o_ref, lse_ref, qp, kp, vp, bias, acc, mx, sm, acc_n, mx_n, sm_n):
        hp = pl.program_id(1)
        _zero_once((kp, vp))
        for p, d in enumerate(DILATIONS):
            nb = n_blk // d
            chunks = _row_chunks(d, S)
            for src, dst, n in chunks:
                qp[dst:dst + n, :] = (q_ref[src, :] * ATT_SCALE).astype(BF16)
                kp[WIN + dst:WIN + dst + n, :] = k_ref[src, :].astype(BF16)
                vp[WIN + dst:WIN + dst + n, :] = v_ref[src, :].astype(BF16)
            _pair_bias(bias, slopes_ref, hp, d, key_major=False)
            acc_t, mx_t, sm_t = (acc_n, mx_n, sm_n) if d == 1 else (acc, mx, sm)

            nk = WIN if nb == 1 else KEYS
            bias_cur = jnp.concatenate([bias[0, :, WIN:KEYS], bias[0, :, KEYS + WIN:]], axis=1) if nb == 1 else None

            def block(i, carry, p=p, nb=nb, nk=nk, bias_cur=bias_cur, acc_t=acc_t, mx_t=mx_t, sm_t=sm_t):
                cur = pl.ds(pl.multiple_of(i * WIN, WIN), WIN)
                keys = pl.ds(pl.multiple_of(i * WIN + (KEYS - nk), WIN), nk)
                s = lax.dot_general(qp[cur, :], _stack_heads(kp[keys, :]), _NT, preferred_element_type=F32)
                s = s + (bias_cur if nb == 1 else bias[((i % nb) > 0).astype(jnp.int32)])
                es, ms, ls = [], [], []
                for h in range(2):
                    sh = s[:, h * nk:(h + 1) * nk]
                    m = jnp.max(sh if nb == 1 else jnp.maximum(sh[:, :WIN], sh[:, WIN:]), axis=1, keepdims=True)
                    e = jnp.exp(sh - m)
                    es.append(e.astype(BF16))
                    ms.append(m)
                    ls.append(jnp.sum(e if nb == 1 else e[:, :WIN] + e[:, WIN:], axis=1, keepdims=True))
                acc_t[p, cur, :] = jnp.dot(jnp.concatenate(es, axis=1), _stack_heads(vp[keys, :]),
                                           preferred_element_type=F32)
                mx_t[p, cur, :] = _per_head(ms[0], ms[1], WIN)
                sm_t[p, cur, :] = _per_head(ls[0], ls[1], WIN)
                return carry

            lax.fori_loop(0, n_blk, block, 0, unroll=ATT_UNROLL)
            if d > 1:
                for src, dst, n in chunks:
                    acc_n[p, src, :] = acc[p, dst:dst + n, :]
                    mx_n[p, src, :] = mx[p, dst:dst + n, :]
                    sm_n[p, src, :] = sm[p, dst:dst + n, :]

        chunk = 256

        def merge(i, carry):
            rows = pl.ds(pl.multiple_of(i * chunk, chunk), chunk)
            ms = [mx_n[p, rows, :] for p in range(3)]
            m = jnp.maximum(jnp.maximum(ms[0], ms[1]), ms[2])
            ws = [jnp.exp(mp - m) for mp in ms]
            l = ws[0] * sm_n[0, rows, :] + ws[1] * sm_n[1, rows, :] + ws[2] * sm_n[2, rows, :]
            o = (ws[0] * acc_n[0, rows, :] + ws[1] * acc_n[1, rows, :] + ws[2] * acc_n[2, rows, :]) / l
            o_ref[rows, :] = o.astype(o_ref.dtype)
            lse = m + jnp.log(l)
            for h in range(2):
                lse_ref[rows, h:h + 1] = lse[:, h * HEAD_DIM:h * HEAD_DIM + 1]
            return carry

        lax.fori_loop(0, S // chunk, merge, 0)

    return pl.pallas_call(
        body, name="attention_fwd", grid=(B, n_pair),
        in_specs=[pl.BlockSpec(memory_space=pltpu.SMEM), _qkv_spec(S, 0), _qkv_spec(S, 1), _qkv_spec(S, 2)],
        out_specs=[pl.BlockSpec((None, S, LANES), lambda b, hp: (b, 0, hp)),
                   pl.BlockSpec((None, None, S, 2), lambda b, hp: (b, hp, 0, 0))],
        out_shape=[jax.ShapeDtypeStruct((B, S, ATT_WIDTH), BF16),
                   jax.ShapeDtypeStruct((B, n_pair, S, 2), F32)],
        scratch_shapes=[pltpu.VMEM((S, LANES), BF16), pltpu.VMEM((S + WIN, LANES), BF16),
                        pltpu.VMEM((S + WIN, LANES), BF16), pltpu.VMEM((2, WIN, 2 * KEYS), F32)]
        + [pltpu.VMEM((3, S, LANES), F32)] * 6,
        compiler_params=_params(2),
    )(slopes, qkv, qkv, qkv)


def _attention_bwd(qkv, o, do, lse, slopes):
    B, S, _ = qkv.shape
    n_blk = S // WIN
    n_pair = N_HEADS // 2

    def body(slopes_ref, q_ref, k_ref, v_ref, o_ref, do_ref, lse_ref, dx_ref,
             qp, dop, kp, vp, aux, auxp, aux_t, bias_t, dqp, dvk, dq_n, dk_n, dv_n):
        hp = pl.program_id(1)
        aux[...] = jnp.zeros_like(aux)
        for c0 in range(0, S, COPY_ROWS):
            rows = slice(c0, c0 + COPY_ROWS)
            prod = do_ref[rows, :] * o_ref[rows, :].astype(F32)
            for h in range(2):
                aux[rows, 2 * h:2 * h + 1] = lse_ref[rows, h:h + 1]
                aux[rows, 2 * h + 1:2 * h + 2] = jnp.sum(prod[:, h * HEAD_DIM:(h + 1) * HEAD_DIM], axis=1,
                                                         keepdims=True)
        dq_n[...] = jnp.zeros_like(dq_n)
        dk_n[...] = jnp.zeros_like(dk_n)
        dv_n[...] = jnp.zeros_like(dv_n)
        _zero_once((kp, vp))
        for p, d in enumerate(DILATIONS):
            nb = n_blk // d
            chunks = _row_chunks(d, S)
            for src, dst, n in chunks:
                auxp[dst:dst + n, :] = aux[src, :]
                qp[dst:dst + n, :] = (q_ref[src, :] * ATT_SCALE).astype(BF16)
                dop[dst:dst + n, :] = do_ref[src, :].astype(BF16)
                kp[WIN + dst:WIN + dst + n, :] = k_ref[src, :].astype(BF16)
                vp[WIN + dst:WIN + dst + n, :] = v_ref[src, :].astype(BF16)
            for i in range(n_blk):
                aux_t[i] = auxp[i * WIN:(i + 1) * WIN, :].T[0:8, :]
            _pair_bias(bias_t, slopes_ref, hp, d, key_major=True)
            dvk[...] = jnp.zeros_like(dvk)

            nk = WIN if nb == 1 else KEYS
            bias_cur = jnp.concatenate([bias_t[0, WIN:KEYS, :], bias_t[0, KEYS + WIN:, :]], axis=0) if nb == 1 else None

            def block(i, carry, nb=nb, nk=nk, bias_cur=bias_cur):
                cur = pl.ds(pl.multiple_of(i * WIN, WIN), WIN)
                keys = pl.ds(pl.multiple_of(i * WIN + (KEYS - nk), WIN), nk)
                q2, do2 = qp[cur, :], dop[cur, :]
                kc = _stack_heads(kp[keys, :])
                s_t = lax.dot_general(kc, q2, _NT, preferred_element_type=F32)
                s_t = s_t + (bias_cur if nb == 1 else bias_t[((i % nb) > 0).astype(jnp.int32)])
                dp_t = lax.dot_general(_stack_heads(vp[keys, :]), do2, _NT, preferred_element_type=F32)
                ps, dss = [], []
                for h in range(2):
                    span = slice(h * nk, (h + 1) * nk)
                    p_t = jnp.exp(s_t[span, :] - aux_t[i, 2 * h:2 * h + 1, :])
                    ds_t = p_t * (dp_t[span, :] - aux_t[i, 2 * h + 1:2 * h + 2, :])
                    ps.append(p_t.astype(BF16))
                    dss.append(ds_t.astype(BF16))
                do_rows, q_rows = _stack_heads(do2), _stack_heads(q2)
                zr = jnp.zeros_like(do_rows)
                rhs = jnp.concatenate([jnp.concatenate([do_rows, zr], axis=1),
                                       jnp.concatenate([zr, q_rows], axis=1)], axis=0)
                dvk[keys, :] += jnp.dot(jnp.concatenate(ps + dss, axis=1), rhs, preferred_element_type=F32)
                dqp[cur, :] = lax.dot_general(jnp.concatenate(dss, axis=0), kc, _TN, preferred_element_type=F32)
                return carry

            lax.fori_loop(0, n_blk, block, 0, unroll=ATT_UNROLL)
            for src, dst, n in chunks:
                dq_n[src, :] += dqp[dst:dst + n, :]
                dv_n[src, :] += dvk[WIN + dst:WIN + dst + n, :LANES]
                dk_n[src, :] += dvk[WIN + dst:WIN + dst + n, LANES:]
        for c0 in range(0, S, COPY_ROWS):
            rows = slice(c0, c0 + COPY_ROWS)
            dx_ref[rows, 0:LANES] = (dq_n[rows, :] * ATT_SCALE).astype(dx_ref.dtype)
            dx_ref[rows, LANES:2 * LANES] = dk_n[rows, :].astype(dx_ref.dtype)
            dx_ref[rows, 2 * LANES:3 * LANES] = dv_n[rows, :].astype(dx_ref.dtype)

    pair = lambda width: pl.BlockSpec((None, S, width), lambda b, hp: (b, 0, hp))
    vm = lambda shape, dt: pltpu.VMEM(shape, dt)
    return pl.pallas_call(
        body, name="attention_bwd", grid=(B, n_pair),
        in_specs=[pl.BlockSpec(memory_space=pltpu.SMEM), _qkv_spec(S, 0), _qkv_spec(S, 1), _qkv_spec(S, 2),
                  pair(LANES), pair(LANES), pl.BlockSpec((None, None, S, 2), lambda b, hp: (b, hp, 0, 0))],
        out_specs=pair(3 * LANES),
        out_shape=jax.ShapeDtypeStruct((B, S, 3 * ATT_WIDTH), BF16),
        scratch_shapes=[vm((S, LANES), BF16), vm((S, LANES), BF16),
                        vm((S + WIN, LANES), BF16), vm((S + WIN, LANES), BF16),
                        vm((S, LANES), F32), vm((S, LANES), F32), vm((n_blk, 8, WIN), F32),
                        vm((2, 2 * KEYS, WIN), F32),
                        vm((S, LANES), F32), vm((S + WIN, 2 * LANES), F32),
                        vm((S, LANES), F32), vm((S, LANES), F32), vm((S, LANES), F32)],
        compiler_params=_params(2),
    )(slopes, qkv, qkv, qkv, o, do, lse)


SCAN_COLS = 256
SCAN_ROWS = 8


def _rows_to_tile(rows):
    rid = lax.broadcasted_iota(jnp.int32, (SCAN_ROWS, rows[0].shape[1]), 0)
    tile = jnp.broadcast_to(rows[0], rid.shape)
    for k in range(1, SCAN_ROWS):
        tile = jnp.where(rid == k, rows[k], tile)
    return tile


SCAN_UNROLL = 4


def _complex_powers(ar, ai, n):
    out = [(ar, ai)]
    for _ in range(n - 1):
        pr, pi = out[-1]
        out.append((pr * ar - pi * ai, pr * ai + pi * ar))
    return out


def _round_multipliers(powers, rid, reverse):
    out = []
    for s in (1, 2, 4):
        keep = (rid < SCAN_ROWS - s) if reverse else (rid >= s)
        out.append((jnp.where(keep, powers[s - 1][0], 0.0), jnp.where(keep, powers[s - 1][1], 0.0)))
    return out


def _tile_scan(xr, xi, multipliers, reverse):
    for s, (mr, mi) in zip((1, 2, 4), multipliers):
        shift = SCAN_ROWS - s if reverse else s
        sr, si = pltpu.roll(xr, shift, 0), pltpu.roll(xi, shift, 0)
        xr, xi = xr + (mr * sr - mi * si), xi + (mr * si + mi * sr)
    return xr, xi


SCAN_CHUNK = 256


def _scan_fwd(us, bb_big, a_row, cc_big):
    B, S, _ = us.shape
    groups = 2
    width = 2 * groups * SCAN_COLS
    nc = 2 * SSM_COLS // width
    nt = S // SCAN_ROWS
    tiles = SCAN_CHUNK // SCAN_ROWS
    LAST = slice(SCAN_ROWS - 1, SCAN_ROWS)

    def body(us_ref, bb_ref, a_ref, cc_ref, xs_ref, y_ref, bu_ref):
        bb = bb_ref[...].astype(BF16)
        for c in range(S // SCAN_CHUNK):
            part = jnp.dot(us_ref[c * SCAN_CHUNK:(c + 1) * SCAN_CHUNK, :].astype(BF16), bb,
                           preferred_element_type=F32)
            bu_ref[c * tiles:(c + 1) * tiles] = part.reshape(tiles, SCAN_ROWS, width)
        rid = lax.broadcasted_iota(jnp.int32, (SCAN_ROWS, SCAN_COLS), 0)
        consts = []
        for g in range(groups):
            re = slice(2 * g * SCAN_COLS, (2 * g + 1) * SCAN_COLS)
            im = slice((2 * g + 1) * SCAN_COLS, (2 * g + 2) * SCAN_COLS)
            powers = _complex_powers(a_ref[:, re], a_ref[:, im], SCAN_ROWS)
            carry_mult = (_rows_to_tile([p[0] for p in powers]), _rows_to_tile([p[1] for p in powers]))
            consts.append((re, im, carry_mult, _round_multipliers(powers, rid, reverse=False)))

        def tile(i, carry):
            out = []
            for (re, im, (cr_t, ci_t), rounds), (cr, ci) in zip(consts, carry):
                xr, xi = _tile_scan(bu_ref[i, :, re], bu_ref[i, :, im], rounds, reverse=False)
                xs_ref[i, :, re] = xr + (cr_t * cr - ci_t * ci)
                xs_ref[i, :, im] = xi + (cr_t * ci + ci_t * cr)
                out.append((xs_ref[i, LAST, re], xs_ref[i, LAST, im]))
            return tuple(out)

        zero = jnp.zeros((1, SCAN_COLS), F32)
        lax.fori_loop(0, nt, tile, ((zero, zero),) * groups, unroll=SCAN_UNROLL)

        @pl.when(pl.program_id(1) == 0)
        def _():
            y_ref[...] = jnp.zeros_like(y_ref)

        cc = cc_ref[...].astype(BF16)
        for c in range(S // SCAN_CHUNK):
            x2 = xs_ref[c * tiles:(c + 1) * tiles].reshape(SCAN_CHUNK, width).astype(BF16)
            y_ref[c * SCAN_CHUNK:(c + 1) * SCAN_CHUNK, :] += jnp.dot(x2, cc, preferred_element_type=F32)

    col = pl.BlockSpec((None, nt, SCAN_ROWS, width), lambda b, j: (b, 0, 0, j))
    tok = pl.BlockSpec((None, S, SSM_WIDTH), lambda b, j: (b, 0, 0))
    xs, y = pl.pallas_call(
        body, name="s5_scan_fwd", grid=(B, nc),
        in_specs=[tok, pl.BlockSpec((SSM_WIDTH, width), lambda b, j: (0, j)),
                  pl.BlockSpec((1, width), lambda b, j: (0, j)), pl.BlockSpec((width, SSM_WIDTH), lambda b, j: (j, 0))],
        out_specs=[col, tok],
        out_shape=[jax.ShapeDtypeStruct((B, nt, SCAN_ROWS, 2 * SSM_COLS), F32),
                   jax.ShapeDtypeStruct((B, S, SSM_WIDTH), F32)],
        scratch_shapes=[pltpu.VMEM((nt, SCAN_ROWS, width), F32)],
        compiler_params=_params(2),
    )(us, bb_big, a_row, cc_big)
    return xs.reshape(B, S, 2 * SSM_COLS), y


def _scan_bwd(dy, us, bb_big, cc_big, xs, a_row):
    B, S, _ = dy.shape
    width = 2 * SCAN_COLS
    nc = SSM_COLS // SCAN_COLS
    nt = S // SCAN_ROWS
    tiles = SCAN_CHUNK // SCAN_ROWS
    RE, IM = slice(0, SCAN_COLS), slice(SCAN_COLS, 2 * SCAN_COLS)
    FIRST, LAST = slice(0, 1), slice(SCAN_ROWS - 1, SCAN_ROWS)

    def body(dy_ref, us_ref, bb_ref, cc_ref, x_ref, a_ref, dus_ref, ga_ref, dbb_ref, dcc_ref, d_ref, lam_ref):
        b = pl.program_id(1)
        cc = cc_ref[...].astype(BF16)
        for c in range(S // SCAN_CHUNK):
            part = lax.dot_general(dy_ref[c * SCAN_CHUNK:(c + 1) * SCAN_CHUNK, :].astype(BF16), cc, _NT,
                                   preferred_element_type=F32)
            d_ref[c * tiles:(c + 1) * tiles] = part.reshape(tiles, SCAN_ROWS, width)
        powers = _complex_powers(a_ref[:, RE], -a_ref[:, IM], SCAN_ROWS)
        rid = lax.broadcasted_iota(jnp.int32, (SCAN_ROWS, SCAN_COLS), 0)
        cr_t = _rows_to_tile([powers[SCAN_ROWS - 1 - r][0] for r in range(SCAN_ROWS)])
        ci_t = _rows_to_tile([powers[SCAN_ROWS - 1 - r][1] for r in range(SCAN_ROWS)])
        rounds = _round_multipliers(powers, rid, reverse=True)

        @pl.when(b == 0)
        def _():
            ga_ref[...] = jnp.zeros_like(ga_ref)
            dbb_ref[...] = jnp.zeros_like(dbb_ref)
            dcc_ref[...] = jnp.zeros_like(dcc_ref)

        def tile(j, carry):
            cr, ci, accr, acci = carry
            i = nt - 1 - j
            lr, li = _tile_scan(d_ref[i, :, RE], d_ref[i, :, IM], rounds, reverse=True)
            lam_r = lr + (cr_t * cr - ci_t * ci)
            lam_i = li + (cr_t * ci + ci_t * cr)
            lam_ref[i, :, RE] = lam_r
            lam_ref[i, :, IM] = lam_i
            ip = jnp.maximum(i - 1, 0)
            keep = (i > 0).astype(F32)
            xpr = jnp.where(rid == 0, x_ref[ip, LAST, RE] * keep, pltpu.roll(x_ref[i, :, RE], 1, 0))
            xpi = jnp.where(rid == 0, x_ref[ip, LAST, IM] * keep, pltpu.roll(x_ref[i, :, IM], 1, 0))
            accr = accr + lam_r * xpr + lam_i * xpi
            acci = acci + lam_i * xpr - lam_r * xpi
            return lam_ref[i, FIRST, RE], lam_ref[i, FIRST, IM], accr, acci

        z1 = jnp.zeros((1, SCAN_COLS), F32)
        z8 = jnp.zeros((SCAN_ROWS, SCAN_COLS), F32)
        _, _, accr, acci = lax.fori_loop(0, nt, tile, (z1, z1, z8, z8), unroll=SCAN_UNROLL)
        ga_ref[:, RE] += _col_sum(accr)
        ga_ref[:, IM] += _col_sum(acci)

        bb = bb_ref[...].astype(BF16)
        for c in range(S // SCAN_CHUNK):
            rows = slice(c * SCAN_CHUNK, (c + 1) * SCAN_CHUNK)
            lam2 = lam_ref[c * tiles:(c + 1) * tiles].reshape(SCAN_CHUNK, width).astype(BF16)
            x2 = x_ref[c * tiles:(c + 1) * tiles].reshape(SCAN_CHUNK, width).astype(BF16)
            dus_ref[rows, :] = lax.dot_general(lam2, bb, _NT, preferred_element_type=F32)
            dbb_ref[...] += lax.dot_general(us_ref[rows, :].astype(BF16), lam2, _TN, preferred_element_type=F32)
            dcc_ref[...] += lax.dot_general(x2, dy_ref[rows, :].astype(BF16), _TN, preferred_element_type=F32)

    col = pl.BlockSpec((None, nt, SCAN_ROWS, width), lambda j, b: (b, 0, 0, j))
    tok = pl.BlockSpec((None, S, SSM_WIDTH), lambda j, b: (b, 0, 0))
    scratch = pltpu.VMEM((nt, SCAN_ROWS, width), F32)
    return pl.pallas_call(
        body, name="s5_scan_bwd", grid=(nc, B),
        in_specs=[tok, tok, pl.BlockSpec((SSM_WIDTH, width), lambda j, b: (0, j)),
                  pl.BlockSpec((width, SSM_WIDTH), lambda j, b: (j, 0)), col,
                  pl.BlockSpec((1, width), lambda j, b: (0, j))],
        out_specs=[pl.BlockSpec((None, None, S, SSM_WIDTH), lambda j, b: (j, b, 0, 0)),
                   pl.BlockSpec((1, width), lambda j, b: (0, j)),
                   pl.BlockSpec((SSM_WIDTH, width), lambda j, b: (0, j)),
                   pl.BlockSpec((width, SSM_WIDTH), lambda j, b: (j, 0))],
        out_shape=[jax.ShapeDtypeStruct((nc, B, S, SSM_WIDTH), F32), jax.ShapeDtypeStruct((1, 2 * SSM_COLS), F32),
                   jax.ShapeDtypeStruct((SSM_WIDTH, 2 * SSM_COLS), F32),
                   jax.ShapeDtypeStruct((2 * SSM_COLS, SSM_WIDTH), F32)],
        scratch_shapes=[scratch, scratch],
        compiler_params=_params(2),
    )(dy, us, bb_big, cc_big, xs.reshape(B, nt, SCAN_ROWS, 2 * SSM_COLS), a_row)


def _s5_discretise(lr, li, log_dt):
    dt = jnp.exp(log_dt)
    mag = jnp.exp(lr * dt)
    ang = li * dt
    ab_re, ab_im = mag * jnp.cos(ang), mag * jnp.sin(ang)
    nr, ni = ab_re - 1.0, ab_im
    den = lr * lr + li * li
    f_re = (nr * lr + ni * li) / den
    f_im = (ni * lr - nr * li) / den
    return dt, ab_re, ab_im, nr, ni, den, f_re, f_im


def _s5_params(a_re, a_im, log_dt):
    def body(lr_ref, li_ref, ld_ref, abr, abi, fr, fi):
        _, ab_re, ab_im, _, _, _, f_re, f_im = _s5_discretise(lr_ref[...], li_ref[...], ld_ref[...])
        abr[...] = ab_re
        abi[...] = ab_im
        fr[...] = f_re
        fi[...] = f_im

    return pl.pallas_call(body, name="s5_params",
                          out_shape=[jax.ShapeDtypeStruct(a_re.shape, F32)] * 4)(a_re, a_im, log_dt)


def _s5_input_matrix(f_re, f_im, b_re, b_im):
    def body(fr, fi, br, bi, o_re, o_im):
        o_re[...] = fr[...] * br[...] - fi[...] * bi[...]
        o_im[...] = fr[...] * bi[...] + fi[...] * br[...]

    return pl.pallas_call(body, name="s5_input_matrix",
                          out_shape=[jax.ShapeDtypeStruct(b_re.shape, F32)] * 2)(f_re, f_im, b_re, b_im)


def _s5_input_matrix_bwd(f_re, f_im, b_re, b_im, g_re, g_im):
    def body(fr, fi, br, bi, gr, gi, dbr, dbi, dfr, dfi):
        dbr[...] = fr[...] * gr[...] + fi[...] * gi[...]
        dbi[...] = fr[...] * gi[...] - fi[...] * gr[...]
        dfr[...] = jnp.sum(br[...] * gr[...] + bi[...] * gi[...], axis=1, keepdims=True)
        dfi[...] = jnp.sum(br[...] * gi[...] - bi[...] * gr[...], axis=1, keepdims=True)

    return pl.pallas_call(
        body, name="s5_input_matrix_bwd",
        out_shape=[jax.ShapeDtypeStruct(b_re.shape, F32)] * 2 + [jax.ShapeDtypeStruct(f_re.shape, F32)] * 2,
    )(f_re, f_im, b_re, b_im, g_re, g_im)


def _s5_params_bwd(a_re, a_im, log_dt, g_ab_re, g_ab_im, d_f_re, d_f_im):
    def body(lr_ref, li_ref, ld_ref, gar, gai, dfr, dfi, o_lr, o_li, o_ld):
        lr, li = lr_ref[...], li_ref[...]
        dt, ab_re, ab_im, nr, ni, den, f_re, f_im = _s5_discretise(lr, li, ld_ref[...])
        d_fr, d_fi = dfr[...], dfi[...]
        d_nr = (d_fr * lr - d_fi * li) / den
        d_ni = (d_fr * li + d_fi * lr) / den
        common = (d_fr * f_re + d_fi * f_im) * 2.0 / den
        d_lr = (d_fr * nr + d_fi * ni) / den - common * lr
        d_li = (d_fr * ni - d_fi * nr) / den - common * li
        d_abr = gar[...] + d_nr
        d_abi = gai[...] + d_ni
        d_mag_mag = d_abr * ab_re + d_abi * ab_im
        d_ang = d_abi * ab_re - d_abr * ab_im
        o_lr[...] = d_lr + d_mag_mag * dt
        o_li[...] = d_li + d_ang * dt
        o_ld[...] = jnp.sum(d_mag_mag * lr + d_ang * li, axis=1, keepdims=True) * dt

    return pl.pallas_call(
        body, name="s5_params_bwd",
        out_shape=[jax.ShapeDtypeStruct(a_re.shape, F32)] * 2 + [jax.ShapeDtypeStruct(log_dt.shape, F32)],
    )(a_re, a_im, log_dt, g_ab_re, g_ab_im, d_f_re, d_f_im)


CONV_COLS = 256


def _shift_down(v, j, row):
    return jnp.where(row >= j, pltpu.roll(v, j, 0), 0.0)


def _shift_up(v, j, row, seq):
    return jnp.where(row < seq - j, pltpu.roll(v, seq - j, 0), 0.0)


def _conv_fwd(up, w_conv, b_conv):
    B, S, _ = up.shape
    nj = D_FF // CONV_COLS

    def body(up_ref, w_ref, b_ref, ff_ref):
        a = up_ref[:, :CONV_COLS].astype(F32)
        val = up_ref[:, CONV_COLS:].astype(F32)
        row = lax.broadcasted_iota(jnp.int32, a.shape, 0)
        w0, w1, w2 = w_ref[0:1, :], w_ref[1:2, :], w_ref[2:3, :]
        conv = b_ref[...] + w0 * a + w1 * _shift_down(a, 1, row) + w2 * _shift_down(a, 2, row)
        ff_ref[...] = (conv * _sigmoid(conv) * val).astype(ff_ref.dtype)

    return pl.pallas_call(
        body, name="conv_gate_fwd", grid=(B, nj),
        in_specs=[pl.BlockSpec((None, S, 2 * CONV_COLS), lambda b, j: (b, 0, j)),
                  pl.BlockSpec((3, CONV_COLS), lambda b, j: (0, j)),
                  pl.BlockSpec((1, CONV_COLS), lambda b, j: (0, j))],
        out_specs=pl.BlockSpec((None, S, CONV_COLS), lambda b, j: (b, 0, j)),
        out_shape=jax.ShapeDtypeStruct((B, S, D_FF), BF16),
        compiler_params=_params(2),
    )(up, w_conv, b_conv)


def _conv_bwd(up, d_ff, w_conv, b_conv):
    B, S, _ = up.shape
    nj = D_FF // CONV_COLS

    def body(up_ref, dff_ref, w_ref, b_ref, dup_ref, dw_ref, db_ref):
        b = pl.program_id(1)
        a = up_ref[:, :CONV_COLS].astype(F32)
        val = up_ref[:, CONV_COLS:].astype(F32)
        row = lax.broadcasted_iota(jnp.int32, a.shape, 0)
        w0, w1, w2 = w_ref[0:1, :], w_ref[1:2, :], w_ref[2:3, :]
        a1, a2 = _shift_down(a, 1, row), _shift_down(a, 2, row)
        conv = b_ref[...] + w0 * a + w1 * a1 + w2 * a2
        sg = _sigmoid(conv)
        dff = dff_ref[...].astype(F32)
        d_val = dff * conv * sg
        dc = dff * val * (sg * (1.0 + conv * (1.0 - sg)))
        d_a = w0 * dc + w1 * _shift_up(dc, 1, row, S) + w2 * _shift_up(dc, 2, row, S)
        dup_ref[:, :CONV_COLS] = d_a.astype(dup_ref.dtype)
        dup_ref[:, CONV_COLS:] = d_val.astype(dup_ref.dtype)

        @pl.when(b == 0)
        def _():
            dw_ref[...] = jnp.zeros_like(dw_ref)
            db_ref[...] = jnp.zeros_like(db_ref)

        dw_ref[0:1, :] += _col_sum(dc * a)
        dw_ref[1:2, :] += _col_sum(dc * a1)
        dw_ref[2:3, :] += _col_sum(dc * a2)
        db_ref[...] += _col_sum(dc)

    return pl.pallas_call(
        body, name="conv_gate_bwd", grid=(nj, B),
        in_specs=[pl.BlockSpec((None, S, 2 * CONV_COLS), lambda j, b: (b, 0, j)),
                  pl.BlockSpec((None, S, CONV_COLS), lambda j, b: (b, 0, j)),
                  pl.BlockSpec((3, CONV_COLS), lambda j, b: (0, j)),
                  pl.BlockSpec((1, CONV_COLS), lambda j, b: (0, j))],
        out_specs=[pl.BlockSpec((None, S, 2 * CONV_COLS), lambda j, b: (b, 0, j)),
                   pl.BlockSpec((3, CONV_COLS), lambda j, b: (0, j)),
                   pl.BlockSpec((1, CONV_COLS), lambda j, b: (0, j))],
        out_shape=[jax.ShapeDtypeStruct((B, S, 2 * D_FF), BF16), jax.ShapeDtypeStruct((3, D_FF), F32),
                   jax.ShapeDtypeStruct((1, D_FF), F32)],
        compiler_params=_params(2),
    )(up, d_ff, w_conv, b_conv)


def _ada_fwd(c_all, w_ada, b_ada):
    def body(c_ref, w_ref, b_ref, o_ref):
        cv = c_ref[...]
        act = (cv * _sigmoid(cv)).astype(BF16)
        o_ref[...] = jnp.dot(act, w_ref[...].astype(BF16), preferred_element_type=F32) + b_ref[...]

    return pl.pallas_call(body, name="ada_fwd",
                          out_shape=jax.ShapeDtypeStruct((c_all.shape[0], w_ada.shape[1]), F32),
                          compiler_params=pltpu.CompilerParams(vmem_limit_bytes=V7X_VMEM_LIMIT))(c_all, w_ada, b_ada)


def _ada_bwd(c_all, dmod_all, dmod_cols):
    def body(c_ref, dm_ref, dmc_ref, dw_ref, db_ref):
        cv = c_ref[...]
        act = (cv * _sigmoid(cv)).astype(BF16)
        dw_ref[...] = lax.dot_general(act, dmc_ref[...].astype(BF16), _TN, preferred_element_type=F32)
        db_ref[...] = _col_sum(dm_ref[...])

    return pl.pallas_call(
        body, name="ada_bwd",
        out_shape=[jax.ShapeDtypeStruct((c_all.shape[1], dmod_cols.shape[1]), F32),
                   jax.ShapeDtypeStruct((1, dmod_all.shape[1]), F32)],
        compiler_params=pltpu.CompilerParams(vmem_limit_bytes=V7X_VMEM_LIMIT))(c_all, dmod_all, dmod_cols)


def _adamw(w, m, v, g_parts, name, own=None):
    R, C = w.shape
    P = g_parts.shape[0]
    tr = R
    for cand in (256, 128, 64, 32, 16, 8):
        if R % cand == 0 and cand * C * 4 * (P + 8) * 2 <= V7X_VMEM_LIMIT // 2:
            tr = cand
            break
    c1 = 1.0 / (1.0 - ADAM_B1 ** ADAM_STEP)
    c2 = 1.0 / (1.0 - ADAM_B2 ** ADAM_STEP)

    def update(w_ref, m_ref, v_ref, g, og, od, om, ov):
        m_new = ADAM_B1 * m_ref[...] + (1.0 - ADAM_B1) * g
        v_new = ADAM_B2 * v_ref[...] + (1.0 - ADAM_B2) * (g * g)
        og[...] = g
        om[...] = m_new
        ov[...] = v_new
        od[...] = -ADAM_LR * ((m_new * c1) / (jnp.sqrt(v_new * c2) + ADAM_EPS) + ADAM_WD * w_ref[...])

    def total(g_ref):
        g = g_ref[0].astype(F32)
        for p in range(1, P):
            g = g + g_ref[p].astype(F32)
        return g

    out_shape = [jax.ShapeDtypeStruct((R, C), F32)] * 4
    if own is None:
        def body(w_ref, m_ref, v_ref, g_ref, og, od, om, ov):
            update(w_ref, m_ref, v_ref, total(g_ref), og, od, om, ov)

        spec = pl.BlockSpec((tr, C), lambda i: (i, 0))
        return pl.pallas_call(
            body, name=name, grid=(R // tr,),
            in_specs=[spec, spec, spec, pl.BlockSpec((P, tr, C), lambda i: (0, i, 0))],
            out_specs=[spec] * 4, out_shape=out_shape, compiler_params=_params(1),
        )(w, m, v, g_parts)

    slots, me = own

    def body_own(me_ref, w_ref, m_ref, v_ref, g_ref, own_ref, og, od, om, ov):
        g = own_ref[...].astype(F32)
        for p in range(P):
            g = g + jnp.where(me_ref[0] == p, 0.0, g_ref[p].astype(F32))
        update(w_ref, m_ref, v_ref, g, og, od, om, ov)

    spec = pl.BlockSpec((tr, C), lambda i, me_ref: (i, 0))
    grid_spec = pltpu.PrefetchScalarGridSpec(
        num_scalar_prefetch=1, grid=(R // tr,),
        in_specs=[spec, spec, spec, pl.BlockSpec((P, tr, C), lambda i, me_ref: (0, i, 0)),
                  pl.BlockSpec((None, tr, C), lambda i, me_ref: (me_ref[0], i, 0))],
        out_specs=[spec] * 4)
    return pl.pallas_call(body_own, name=name, grid_spec=grid_spec, out_shape=out_shape,
                          compiler_params=_params(1))(me, w, m, v, g_parts, slots)


def _adamw_small(ws, ms, vs, gs):
    n = len(ws)
    c1 = 1.0 / (1.0 - ADAM_B1 ** ADAM_STEP)
    c2 = 1.0 / (1.0 - ADAM_B2 ** ADAM_STEP)

    def body(*refs):
        ins, outs = refs[:4 * n], refs[4 * n:]
        for i in range(n):
            w, m, v, g = ins[i][...], ins[n + i][...], ins[2 * n + i][...], ins[3 * n + i][...]
            m_new = ADAM_B1 * m + (1.0 - ADAM_B1) * g
            v_new = ADAM_B2 * v + (1.0 - ADAM_B2) * (g * g)
            outs[4 * i][...] = g
            outs[4 * i + 1][...] = -ADAM_LR * ((m_new * c1) / (jnp.sqrt(v_new * c2) + ADAM_EPS) + ADAM_WD * w)
            outs[4 * i + 2][...] = m_new
            outs[4 * i + 3][...] = v_new

    out_shape = [jax.ShapeDtypeStruct(w.shape, F32) for w in ws for _ in range(4)]
    return pl.pallas_call(body, name="adamw_small", out_shape=out_shape,
                          compiler_params=pltpu.CompilerParams(vmem_limit_bytes=V7X_VMEM_LIMIT))(*ws, *ms, *vs, *gs)


def _sum_parts(parts, loss_rows):
    P, R, C = parts.shape
    lo, hi = loss_rows

    def body(p_ref, o_ref, loss_ref):
        t = p_ref[0]
        for p in range(1, P):
            t = t + p_ref[p]
        o_ref[...] = t
        tot = jnp.sum(jnp.sum(o_ref[lo:hi, :], axis=1, keepdims=True), axis=0, keepdims=True)
        loss_ref[...] = jnp.broadcast_to(tot, loss_ref.shape)

    return pl.pallas_call(body, name="sum_small_grads",
                          out_shape=[jax.ShapeDtypeStruct((R, C), F32), jax.ShapeDtypeStruct((1, LANES), F32)],
                          compiler_params=pltpu.CompilerParams(vmem_limit_bytes=V7X_VMEM_LIMIT))(parts)


def _exchange(items, name):
    n = len(items)
    MESH = pl.DeviceIdType.MESH

    def body(*refs):
        src, dst = refs[:n], refs[n:2 * n]
        send_sems, recv_sems, local_sems = refs[2 * n:]
        x, y, c = lax.axis_index("x"), lax.axis_index("y"), lax.axis_index("c")
        me = 4 * x + 2 * y + c
        started = []
        for it, (_, per_peer) in enumerate(items):
            own = pltpu.make_async_copy(src[it].at[me] if per_peer else src[it], dst[it].at[me], local_sems.at[it])
            own.start()
            started.append(own)
        sends, recvs = [], []
        for k in range(1, N_DEV):
            px = 1 - x if k & 4 else x
            py = 1 - y if k & 2 else y
            pc = 1 - c if k & 1 else c
            peer = 4 * px + 2 * py + pc
            for it, (_, per_peer) in enumerate(items):
                s = src[it].at[peer] if per_peer else src[it]
                cp = pltpu.make_async_remote_copy(src_ref=s, dst_ref=dst[it].at[me], send_sem=send_sems.at[it, k - 1],
                                                  recv_sem=recv_sems.at[it, k - 1], device_id=(px, py, pc),
                                                  device_id_type=MESH)
                cp.start()
                sends.append(cp)
                recvs.append(pltpu.make_async_remote_copy(
                    src_ref=s, dst_ref=dst[it].at[peer], send_sem=send_sems.at[it, k - 1],
                    recv_sem=recv_sems.at[it, k - 1], device_id=(px, py, pc), device_id_type=MESH))
        for cp in recvs:
            cp.wait_recv()
        for cp in sends:
            cp.wait_send()
        for cp in started:
            cp.wait()

    any_spec = pl.BlockSpec(memory_space=pl.ANY)
    out_shape = []
    for a, per_peer in items:
        shp = a.shape if per_peer else (N_DEV,) + a.shape
        out_shape.append(jax.ShapeDtypeStruct(shp, a.dtype))
    return pl.pallas_call(
        body, name=name, in_specs=[any_spec] * n, out_specs=[any_spec] * n, out_shape=out_shape,
        scratch_shapes=[pltpu.SemaphoreType.DMA((n, N_DEV - 1)), pltpu.SemaphoreType.DMA((n, N_DEV - 1)),
                        pltpu.SemaphoreType.DMA((n,))],
    )(*[a for a, _ in items])


def _remote(src, dst, send_sem, recv_sem, device):
    return pltpu.make_async_remote_copy(src_ref=src, dst_ref=dst, send_sem=send_sem, recv_sem=recv_sem,
                                        device_id=device, device_id_type=pl.DeviceIdType.MESH)


def _mesh_place():
    x, y, c = lax.axis_index("x"), lax.axis_index("y"), lax.axis_index("c")
    other_chips = [(1 - x, y), (x, 1 - y), (1 - x, 1 - y)]
    return x, y, c, (x, y, 1 - c), other_chips


def _gather_all(items, name):
    n = len(items)

    def body(*refs):
        src, dst = refs[:n], refs[n:2 * n]
        send_sems, recv_sems, local_sems = refs[2 * n:]
        x, y, c, sibling, chips = _mesh_place()
        slot = lambda px, py, pc: 4 * px + 2 * py + pc
        me = slot(x, y, c)
        own = [pltpu.make_async_copy(src[it], dst[it].at[me], local_sems.at[it]) for it in range(n)]
        first = []
        for it in range(n):
            first.append(_remote(src[it], dst[it].at[me], send_sems.at[it, 0], recv_sems.at[it, 0], sibling))
            for j, chip in enumerate(chips):
                first.append(_remote(src[it], dst[it].at[me], send_sems.at[it, 1 + j], recv_sems.at[it, 1 + j],
                                     (*chip, c)))
        for cp in own + first:
            cp.start()
        passed = []
        for j, chip in enumerate(chips):
            blk = slot(*chip, c)
            for it in range(n):
                _remote(src[it], dst[it].at[blk], send_sems.at[it, 1 + j], recv_sems.at[it, 1 + j],
                        (*chip, c)).wait_recv()
                fwd = _remote(dst[it].at[blk], dst[it].at[blk], send_sems.at[it, 4 + j], recv_sems.at[it, 4 + j],
                              sibling)
                fwd.start()
                passed.append(fwd)
        for it in range(n):
            _remote(src[it], dst[it].at[slot(x, y, 1 - c)], send_sems.at[it, 0], recv_sems.at[it, 0],
                    sibling).wait_recv()
        for j, chip in enumerate(chips):
            for it in range(n):
                _remote(src[it], dst[it].at[slot(*chip, 1 - c)], send_sems.at[it, 4 + j], recv_sems.at[it, 4 + j],
                        sibling).wait_recv()
        for cp in first + passed:
            cp.wait_send()
        for cp in own:
            cp.wait()

    any_spec = pl.BlockSpec(memory_space=pl.ANY)
    return pl.pallas_call(
        body, name=name, in_specs=[any_spec] * n, out_specs=[any_spec] * n,
        out_shape=[jax.ShapeDtypeStruct((N_DEV,) + a.shape, a.dtype) for a in items],
        scratch_shapes=[pltpu.SemaphoreType.DMA((n, 7)), pltpu.SemaphoreType.DMA((n, 7)),
                        pltpu.SemaphoreType.DMA((n,))],
    )(*items)


def _peers():
    x, y, c = lax.axis_index("x"), lax.axis_index("y"), lax.axis_index("c")
    out = []
    for k in range(1, N_DEV):
        px = 1 - x if k & 4 else x
        py = 1 - y if k & 2 else y
        pc = 1 - c if k & 1 else c
        out.append((k, (px, py, pc), 4 * px + 2 * py + pc))
    return 4 * x + 2 * y + c, out


def _exchange_start(items, name, gather, carry=()):
    n, m = len(items), len(carry)

    def body(*refs):
        src, land = refs[:n], refs[n:2 * n]
        first_out = 2 * n + m
        send_sems, recv_sems = refs[first_out:first_out + n], refs[first_out + n:first_out + 2 * n]
        token = refs[-1]
        me, peers = _peers()
        for k, peer, slot in peers:
            for it in range(n):
                _remote(src[it] if gather else src[it].at[slot], land[it].at[me], send_sems[it], recv_sems[it],
                        peer).start()
        token[...] = jnp.zeros_like(token)

    hbm = pl.BlockSpec(memory_space=pltpu.HBM)
    sem = pl.BlockSpec(memory_space=pltpu.SEMAPHORE)
    land_shapes = [(N_DEV,) + (a.shape if gather else a.shape[1:]) for a in items]
    lands = [lax.empty(shp, a.dtype) for shp, a in zip(land_shapes, items)]
    through = list(items) + lands + list(carry)
    outs = pl.pallas_call(
        body, name=name,
        out_shape=(*[pltpu.SemaphoreType.DMA(())] * (2 * n), *[pltpu.HBM(a.shape, a.dtype) for a in through],
                   jax.ShapeDtypeStruct((8, LANES), F32)),
        in_specs=[hbm] * len(through),
        out_specs=(*[sem] * (2 * n), *[hbm] * len(through), pl.BlockSpec(memory_space=pltpu.VMEM)),
        input_output_aliases={i: 2 * n + i for i in range(len(through))},
        compiler_params=pltpu.CompilerParams(has_side_effects=pltpu.SideEffectType.DATAFLOW_SIDE_EFFECTING),
    )(*[pltpu.with_memory_space_constraint(a, pltpu.HBM) for a in through])
    return (list(outs[:n]), list(outs[n:2 * n]), list(outs[2 * n:3 * n]), list(outs[3 * n:4 * n]), outs[-1],
            list(outs[4 * n:4 * n + m]))


def _exchange_wait(send_sems, recv_sems, items, lands, after, name):
    n = len(items)

    def body(*refs):
        land = refs[n:2 * n]
        send_sems, recv_sems = refs[2 * n:3 * n], refs[3 * n:4 * n]
        me, peers = _peers()
        for it in range(n):
            seven = land[it].at[pl.ds(0, N_DEV - 1)]
            cp = _remote(seven, seven, send_sems[it], recv_sems[it], peers[0][1])
            cp.wait_send()
            cp.wait_recv()

    hbm = pl.BlockSpec(memory_space=pltpu.HBM)
    sem = pl.BlockSpec(memory_space=pltpu.SEMAPHORE)
    outs = pl.pallas_call(
        body, name=name,
        out_shape=tuple(pltpu.HBM(a.shape, a.dtype) for a in list(items) + list(lands)),
        in_specs=[hbm] * (2 * n) + [sem] * (2 * n) + [pl.BlockSpec(memory_space=pl.ANY)],
        out_specs=tuple([hbm] * (2 * n)),
        input_output_aliases={i: i for i in range(2 * n)},
        compiler_params=pltpu.CompilerParams(has_side_effects=pltpu.SideEffectType.DATAFLOW_SIDE_EFFECTING),
    )(*items, *lands, *send_sems, *recv_sems, after)
    return list(outs[:n]), list(outs[n:])


def _gelu_tanh(y):
    k = math.sqrt(2.0 / math.pi)
    t = jnp.tanh(k * (y + 0.044715 * y * y * y))
    return 0.5 * y * (1.0 + t), t


def _local_step(x, mod, target, W, late_weights, P, send_early):
    B, S, D = x.shape
    T = B * S
    TS = 512
    flat = lambda a: a.reshape(T, a.shape[-1])
    unflat = lambda a: a.reshape(B, S, a.shape[-1])
    mod_col = lambda i: (mod, D, i)

    def f_modnorm_project(xv, sc, sh, g, wq, wu, wg):
        u = ((xv * _rms_scale(xv) * g) * (1.0 + sc) + sh).astype(BF16)
        return (u, lax.dot_general(u, wq, _NT, preferred_element_type=F32),
                lax.dot_general(u, wu, _NT, preferred_element_type=F32),
                lax.dot_general(u, wg, _NT, preferred_element_type=F32))

    u1, qkv, us, gates = _rowwise(
        f_modnorm_project, [(x, D, 0)], [mod_col(1), mod_col(0)], [P["g_mix"], W["w_qkv"], W["w_us"], W["w_gates"]],
        [(D, BF16), (3 * ATT_WIDTH, F32), (SSM_WIDTH, F32), (2 * D, BF16)], [], [], ts=TS, name="modnorm_project_in")
    u1f = flat(u1)

    o_att, lse = _attention_fwd(qkv, P["slopes"])
    more_w, more_p = late_weights(o_att)
    W, P = {**W, **more_w}, {**P, **more_p}

    xs, y_mm = _scan_fwd(us, P["bb_big"], P["a_row"], P["cc_big"])

    bga, bgs = P["b_gate"][:, :D], P["b_gate"][:, D:]

    def f_mixer_tail(ov, ymm, usv, ga, gs, xv, gt, sc, sh, w_att, w_ssm, w_o, bga_, bgs_, g, dsk, wg, bg):
        yv = ymm + dsk * usv
        ge, _ = _gelu_tanh(yv)
        zv = (ge * _sigmoid(jnp.dot(ge.astype(BF16), wg, preferred_element_type=F32) + bg)).astype(BF16)
        ya = jnp.dot(ov, w_att, preferred_element_type=F32)
        ys = jnp.dot(zv, w_ssm, preferred_element_type=F32)
        mg = (_sigmoid(ga + bga_) * ya + _sigmoid(gs + bgs_) * ys).astype(BF16)
        mx = jnp.dot(mg, w_o, preferred_element_type=F32)
        h = xv + gt * mx
        return yv, zv, ya, ys, mg, mx, h, (h * _rms_scale(h) * g) * (1.0 + sc) + sh

    y_s5, z, y_att, y_ssm, merged, mix, h1, u2 = _rowwise(
        f_mixer_tail,
        [(o_att, ATT_WIDTH, 0), (y_mm, SSM_WIDTH, 0), (us, SSM_WIDTH, 0), (gates, D, 0), (gates, D, 1), (x, D, 0)],
        [mod_col(2), mod_col(4), mod_col(3)],
        [W["w_proj_att"], W["w_proj_ssm"], W["w_out"], bga, bgs, P["g_ffn"], P["d_skip"], W["w_glu"], P["b_glu"]],
        [(SSM_WIDTH, F32), (SSM_WIDTH, BF16), (D, BF16), (D, BF16), (D, BF16), (D, BF16), (D, F32), (D, BF16)],
        [], [], ts=TS, raw=(0,), name="mixer_tail")

    up = unflat(_up_fwd(flat(u2), W["w_up"], name="ffn_up"))
    ff = _conv_fwd(up, P["w_conv"], P["b_conv"])

    def f_head(ffv, h1v, tg, gt, g, w_dn):
        dn = jnp.dot(ffv, w_dn, preferred_element_type=F32)
        h2 = h1v + gt * dn
        r = _rms_scale(h2)
        nh = h2 * r
        e = nh * g - tg
        dy = e * (1.0 / D)
        gy = dy * g
        dh = r * (gy - nh * jnp.mean(gy * nh, axis=-1, keepdims=True))
        return (dh, dh * gt, _col_sum(dh * dn), _col_sum(dy * nh), _col_sum(e * e) * (0.5 / D))

    dh2, d_down, d_gt2, d_g_final, loss_cols = _rowwise(
        f_head, [(ff, D_FF, 0), (h1, D, 0), (target, D, 0)], [mod_col(5)], [P["g_final"], W["w_down"]],
        [(D, BF16), (D, BF16)], [D], [(1, D), (1, D)], ts=TS, raw=(0,), name="ffn_down_head_loss")

    d_downf = flat(d_down)
    d_ff = unflat(_matmul(d_downf, W["w_down"], tb=True, out_dtype=BF16, name="ffn_down_dx"))
    d_w_down = _matmul(flat(ff), d_downf, ta=True, out_dtype=BF16, name="ffn_down_dw")
    d_up, d_w_conv, d_b_conv = _conv_bwd(up, d_ff, P["w_conv"], P["b_conv"])
    d_upf = flat(d_up)
    d_u2 = unflat(_up_dx(d_upf, W["w_up"], name="ffn_up_dx"))
    d_w_up = _up_dw(flat(u2), d_upf, name="ffn_up_dw")
    token, _ = send_early(dict(w_down=d_w_down.reshape(N_DEV, D_FF // N_DEV, D), w_up=d_w_up))
    g_ffn_after = P["g_ffn"] + token[0:1, 0:1]

    def f_modnorm_bwd(du, h, dres, mx, sc, gt, g):
        r = _rms_scale(h)
        nh = h * r
        dn = du * (1.0 + sc)
        gy = dn * g
        dh = dres + r * (gy - nh * jnp.mean(gy * nh, axis=-1, keepdims=True))
        return (dh, dh * gt, _col_sum(du), _col_sum(du * nh * g), _col_sum(dh * mx), _col_sum(dn * nh))

    dh1, d_mix, d_sh2, d_sc2, d_gt1, d_g_ffn = _rowwise(
        f_modnorm_bwd, [(d_u2, D, 0), (h1, D, 0), (dh2, D, 0), (mix, D, 0)], [mod_col(4), mod_col(2)], [g_ffn_after],
        [(D, BF16), (D, BF16)], [D, D, D], [(1, D)], ts=TS, name="modnorm_ffn_bwd")

    d_mixf = flat(d_mix)
    d_w_out = _matmul(flat(merged), d_mixf, ta=True, out_dtype=BF16, name="proj_out_dw")

    def f_mixer_tail_bwd(dmx, ga, gs, ya, ys, w_o, w_att, w_ssm, bga_, bgs_):
        dm = lax.dot_general(dmx, w_o, _NT, preferred_element_type=F32)
        sa, ss = _sigmoid(ga + bga_), _sigmoid(gs + bgs_)
        dga = dm * ya * sa * (1.0 - sa)
        dgs = dm * ys * ss * (1.0 - ss)
        dya, dys = (dm * sa).astype(BF16), (dm * ss).astype(BF16)
        return (dya, dys, jnp.concatenate([dga, dgs], axis=1),
                lax.dot_general(dya, w_att, _NT, preferred_element_type=F32),
                lax.dot_general(dys, w_ssm, _NT, preferred_element_type=F32), _col_sum(dga), _col_sum(dgs))

    d_y_att, d_y_ssm, d_gates, d_o_att, d_z, d_bga, d_bgs = _rowwise(
        f_mixer_tail_bwd, [(d_mix, D, 0), (gates, D, 0), (gates, D, 1), (y_att, D, 0), (y_ssm, D, 0)], [],
        [W["w_out"], W["w_proj_att"], W["w_proj_ssm"], bga, bgs],
        [(D, BF16), (D, BF16), (2 * D, BF16), (ATT_WIDTH, F32), (SSM_WIDTH, BF16)], [], [(1, D), (1, D)], ts=TS,
        raw=(0,), name="mixer_tail_bwd")

    d_yaf, d_ysf = flat(d_y_att), flat(d_y_ssm)
    d_w_proj_att = _matmul(flat(o_att), d_yaf, ta=True, out_dtype=BF16, name="proj_att_dw")
    d_w_proj_ssm = _matmul(flat(z), d_ysf, ta=True, out_dtype=BF16, name="proj_ssm_dw")

    def f_glu_bwd(yv, dz, usv, dsk, wg, bg):
        ge, t = _gelu_tanh(yv)
        pre = jnp.dot(ge.astype(BF16), wg, preferred_element_type=F32) + bg
        sg = _sigmoid(pre)
        dpre = dz * ge * sg * (1.0 - sg)
        dge = dz * sg + lax.dot_general(dpre.astype(BF16), wg, _NT, preferred_element_type=F32)
        k = math.sqrt(2.0 / math.pi)
        dgelu = 0.5 * (1.0 + t) + 0.5 * yv * (1.0 - t * t) * k * (1.0 + 3.0 * 0.044715 * yv * yv)
        dy = dge * dgelu
        dwg = lax.dot_general(ge.astype(BF16), dpre.astype(BF16), _TN, preferred_element_type=F32)
        return dy, dy * dsk, dwg, _col_sum(dpre), _col_sum(dy * usv)

    d_y_s5, d_us_skip, d_w_glu, d_b_glu, d_d_skip = _rowwise(
        f_glu_bwd, [(y_s5, SSM_WIDTH, 0), (d_z, SSM_WIDTH, 0), (us, SSM_WIDTH, 0)], [],
        [P["d_skip"], W["w_glu"], P["b_glu"]],
        [(SSM_WIDTH, BF16), (SSM_WIDTH, F32)], [], [(SSM_WIDTH, SSM_WIDTH), (1, SSM_WIDTH), (1, SSM_WIDTH)],
        ts=TS, name="s5_glu_bwd")
    d_us_parts, g_ab, d_bb, d_cc = _scan_bwd(d_y_s5, us, P["bb_big"], P["cc_big"], xs, P["a_row"])

    token, _ = send_early(dict(
        w_out=d_w_out.reshape(N_DEV, D // N_DEV, D), w_proj_att=_cols_to_slots(d_w_proj_att),
        w_proj_ssm=_cols_to_slots(d_w_proj_ssm),
        w_glu=d_w_glu.astype(BF16).reshape(N_DEV, SSM_WIDTH // N_DEV, SSM_WIDTH),
        w_conv=_cols_to_slots(d_w_conv.astype(BF16))))
    d_qkv = _attention_bwd(qkv, o_att, d_o_att, lse, P["slopes"] + token[0, 0])

    def f_add(*parts):
        return sum(parts[1:], parts[0])

    n_parts = d_us_parts.shape[0]
    stacked = d_us_parts.reshape(n_parts * B, S, SSM_WIDTH)
    (d_us,) = _rowwise(f_add, [(d_us_skip, SSM_WIDTH, 0)] + [(stacked, SSM_WIDTH, 0, j * B) for j in range(n_parts)],
                       [], [],
                       [(SSM_WIDTH, BF16)], [], [], ts=TS, name="s5_input_grad")
    d_qkvf = flat(d_qkv)
    d_usf = flat(d_us)
    d_gatesf = flat(d_gates)
    d_w_in_t = jnp.concatenate(
        [_unpair_qkv_rows(_matmul(d_qkvf, u1f, ta=True, out_dtype=BF16, name="proj_qkv_dw")),
         _matmul(d_usf, u1f, ta=True, out_dtype=BF16, name="proj_ssm_in_dw"),
         _matmul(d_gatesf, u1f, ta=True, out_dtype=BF16, name="proj_gates_dw")], axis=0)
    token, (w_qkv, w_us, w_gates) = send_early(dict(w_in=d_w_in_t.reshape(N_DEV, -1, D)),
                                               carry=[W["w_qkv"], W["w_us"], W["w_gates"]])
    def f_project_back_modnorm(dq, du_, dg, h, dres, sc, g, wq, wu, wg):
        du = (jnp.dot(dq, wq, preferred_element_type=F32) + jnp.dot(du_, wu, preferred_element_type=F32)
              + jnp.dot(dg, wg, preferred_element_type=F32))
        r = _rms_scale(h)
        nh = h * r
        dn = du * (1.0 + sc)
        gy = dn * g
        dh = dres + r * (gy - nh * jnp.mean(gy * nh, axis=-1, keepdims=True))
        return (dh, _col_sum(du), _col_sum(du * nh * g), _col_sum(dn * nh))

    grad_x, d_sh1, d_sc1, d_g_mix = _rowwise(
        f_project_back_modnorm,
        [(d_qkv, 3 * ATT_WIDTH, 0), (d_us, SSM_WIDTH, 0), (d_gates, 2 * D, 0), (x, D, 0), (dh1, D, 0)], [mod_col(1)],
        [P["g_mix"] + token[0:1, 0:1], w_qkv, w_us, w_gates],
        [(D, F32)], [D, D], [(1, D)], ts=TS, raw=(0, 1, 2), name="project_in_back_modnorm")

    d_mod = jnp.concatenate([d_sh1, d_sc1, d_gt1, d_sh2, d_sc2, d_gt2], axis=-1)
    g_ab_re, g_ab_im = _deinterleave(g_ab)
    d_bb_re, d_bb_im = _deinterleave(d_bb)
    d_cc_re, d_cc_im = (t.T for t in _deinterleave(d_cc.T))
    small = dict(g_mix=d_g_mix, b_gate=jnp.concatenate([d_bga, d_bgs], axis=1), g_ab_re=g_ab_re, g_ab_im=g_ab_im,
                 d_bb_re=d_bb_re, d_bb_im=d_bb_im, d_cc_re=d_cc_re, d_cc_im=d_cc_im, d_skip=d_d_skip,
                 b_glu=d_b_glu, g_ffn=d_g_ffn, b_conv=d_b_conv, g_final=d_g_final, loss_cols=loss_cols)
    return grad_x, d_mod, small


def _block_diag_in(bb):
    t = bb.reshape(SSM_GROUPS, SSM_STATE, SSM_GROUP_CH)
    eye = jnp.eye(SSM_GROUPS, dtype=bb.dtype)
    return jnp.einsum("gnc,gh->gchn", t, eye).reshape(SSM_WIDTH, SSM_COLS)


def _block_diag_out(cm):
    eye = jnp.eye(SSM_GROUPS, dtype=cm.dtype)
    return jnp.einsum("gcn,gh->gnhc", cm, eye).reshape(SSM_COLS, SSM_WIDTH)


def _diag_blocks_in(m):
    t = m.reshape(SSM_GROUPS, SSM_GROUP_CH, SSM_GROUPS, SSM_STATE)
    idx = jnp.arange(SSM_GROUPS)
    return t[idx, :, idx, :].transpose(0, 2, 1).reshape(SSM_COLS, SSM_GROUP_CH)


def _diag_blocks_out(m):
    t = m.reshape(SSM_GROUPS, SSM_STATE, SSM_GROUPS, SSM_GROUP_CH)
    idx = jnp.arange(SSM_GROUPS)
    return t[idx, :, idx, :].transpose(0, 2, 1)


def _pair_qkv_rows(w):
    return w.reshape(3, N_HEADS // 2, LANES, w.shape[1]).swapaxes(0, 1).reshape(w.shape)


def _unpair_qkv_rows(w):
    return w.reshape(N_HEADS // 2, 3, LANES, w.shape[1]).swapaxes(0, 1).reshape(w.shape)


def _interleave(re, im):
    lead = re.shape[:-1]
    g = lambda a: a.reshape(lead + (SSM_COLS // SCAN_COLS, 1, SCAN_COLS))
    return jnp.concatenate([g(re), g(im)], axis=-2).reshape(lead + (2 * SSM_COLS,))


def _deinterleave(x):
    lead = x.shape[:-1]
    t = x.reshape(lead + (SSM_COLS // SCAN_COLS, 2, SCAN_COLS))
    return t[..., 0, :].reshape(lead + (SSM_COLS,)), t[..., 1, :].reshape(lead + (SSM_COLS,))


def _cols_to_slots(g):
    R = g.shape[0]
    return g.reshape(R, N_DEV, g.shape[1] // N_DEV).transpose(1, 0, 2)


def _slots_to_cols(g):
    return g.transpose(1, 0, 2).reshape(g.shape[1], N_DEV * g.shape[2])


SMALL_ORDER = ("b_ada", "g_mix", "b_gate", "a_re", "a_im", "log_dt", "b_re", "b_im", "c_re", "c_im", "d_skip",
               "b_glu", "g_ffn", "b_conv", "g_final")


def _pack(arrs):
    pieces, offs, row = [], [], 0
    for a in arrs:
        f = a.reshape(-1).astype(F32)
        n = f.shape[0]
        rows = -(-n // LANES)
        pieces.append(jnp.pad(f, (0, rows * LANES - n)))
        offs.append((row, n))
        row += rows
    return jnp.concatenate(pieces).reshape(row, LANES), offs


def _unpack(packed, offs, shapes):
    flat = packed.reshape(-1)
    return [flat[r * LANES:r * LANES + n].reshape(s) for (r, n), s in zip(offs, shapes)]


def kernel(x, c, w_ada, b_ada, g_mix, w_in, b_gate, a_re, a_im, log_dt, b_re, b_im, c_re, c_im, d_skip, w_glu, b_glu, w_proj_att, w_proj_ssm, w_out, g_ffn, w_up, w_conv, b_conv, w_down, g_final, loss_target, m_w_ada, m_b_ada, m_g_mix, m_w_in, m_b_gate, m_a_re, m_a_im, m_log_dt, m_b_re, m_b_im, m_c_re, m_c_im, m_d_skip, m_w_glu, m_b_glu, m_w_proj_att, m_w_proj_ssm, m_w_out, m_g_ffn, m_w_up, m_w_conv, m_b_conv, m_w_down, m_g_final, v_w_ada, v_b_ada, v_g_mix, v_w_in, v_b_gate, v_a_re, v_a_im, v_log_dt, v_b_re, v_b_im, v_c_re, v_c_im, v_d_skip, v_w_glu, v_b_glu, v_w_proj_att, v_w_proj_ssm, v_w_out, v_g_ffn, v_w_up, v_w_conv, v_b_conv, v_w_down, v_g_final):
    args = dict(locals())
    B, S, D = x.shape
    me = 4 * lax.axis_index("x") + 2 * lax.axis_index("y") + lax.axis_index("c")
    bf = lambda w: w[0].astype(BF16)

    c_slots, w_in_slots = _gather_all([c, w_in[0].T.astype(BF16)], name="gather_first_weights")
    c_all = c_slots.reshape(N_DEV * B, D)
    w_in_t = w_in_slots.reshape(-1, D)
    n_qkv = 3 * ATT_WIDTH
    W = dict(w_qkv=_pair_qkv_rows(w_in_t[:n_qkv]), w_us=w_in_t[n_qkv:n_qkv + SSM_WIDTH],
             w_gates=w_in_t[n_qkv + SSM_WIDTH:])

    n_ada = w_ada.shape[2]
    b_ada_cols = lax.dynamic_slice(b_ada, (0, me * n_ada), (1, n_ada))
    mod_part = _ada_fwd(c_all, w_ada[0], b_ada_cols)
    (mod_slots,) = _exchange([(mod_part.reshape(N_DEV, B, n_ada), True)], name="scatter_modulation")
    mod = mod_slots.transpose(1, 0, 2).reshape(B, 1, 6 * D)

    later = [bf(w_glu), bf(w_proj_att), bf(w_proj_ssm), bf(w_out), bf(w_up), w_conv[0], bf(w_down)]
    later_sems = _exchange_start(later, "start_later_weights", gather=True, carry=[mod])
    (mod,) = later_sems[5]

    def late_weights(after):
        _, lands = _exchange_wait(*later_sems[:4], after, name="wait_later_weights")
        g = [lax.dynamic_update_index_in_dim(land, a, me, 0) for land, a in zip(lands, later)]
        more_w = dict(w_glu=g[0].reshape(SSM_WIDTH, SSM_WIDTH), w_proj_att=_slots_to_cols(g[1]),
                      w_proj_ssm=_slots_to_cols(g[2]), w_out=g[3].reshape(D, D), w_up=g[4],
                      w_down=g[6].reshape(D_FF, D))
        return more_w, dict(w_conv=_slots_to_cols(g[5]))

    ab_re, ab_im, f_re, f_im = _s5_params(a_re[0], a_im[0], log_dt[0].reshape(SSM_GROUPS, 1))
    col = lambda a: a.reshape(SSM_COLS, 1)
    b_re2, b_im2 = b_re[0].reshape(SSM_COLS, SSM_GROUP_CH), b_im[0].reshape(SSM_COLS, SSM_GROUP_CH)
    bb_re, bb_im = _s5_input_matrix(col(f_re), col(f_im), b_re2, b_im2)
    slopes = jnp.asarray([2.0 ** (-8.0 * (h + 1) / N_HEADS) for h in range(N_HEADS)], F32)
    P = dict(g_mix=g_mix, g_ffn=g_ffn, g_final=g_final.reshape(1, D), b_gate=b_gate, d_skip=d_skip, b_glu=b_glu,
             b_conv=b_conv, slopes=slopes,
             a_row=_interleave(ab_re.reshape(1, SSM_COLS), ab_im.reshape(1, SSM_COLS)),
             bb_big=_interleave(_block_diag_in(bb_re), _block_diag_in(bb_im)),
             cc_big=_interleave(_block_diag_out(c_re[0]).T, -_block_diag_out(c_im[0]).T).T)

    in_flight = []

    def send_early(grads, carry=()):
        names = list(grads)
        handles = _exchange_start([grads[n] for n in names], "start_gradients_%d" % len(in_flight), gather=False,
                                  carry=carry)
        in_flight.append((names,) + handles[:4])
        return handles[4], handles[5]

    grad_x, d_mod, small = _local_step(x, mod, loss_target, W, late_weights, P, send_early)

    small_list = [small["loss_cols"], small["g_mix"], small["b_gate"], small["g_ab_re"], small["g_ab_im"],
                  _diag_blocks_in(small["d_bb_re"]), _diag_blocks_in(small["d_bb_im"]),
                  _diag_blocks_out(small["d_cc_re"]), -_diag_blocks_out(small["d_cc_im"]),
                  small["g_ffn"], small["b_conv"], small["g_final"], small["d_skip"], small["b_glu"]]
    small_packed, small_offs = _pack(small_list)
    small_all, dmod_slots = _gather_all([small_packed, d_mod.reshape(B, 6 * D)], name="gather_small_gradients")

    out = {}

    def update(name, parts, own=None):
        view = (lambda a: a[0].T) if name == "w_in" else (lambda a: a[0])
        back = (lambda a: a.T[None]) if name == "w_in" else (lambda a: a[None])
        g, dl, mn, vn = _adamw(view(args[name]), view(args["m_" + name]), view(args["v_" + name]), parts,
                               name="adamw_" + name, own=own)
        for key, val in (("grad_", g), ("delta_", dl), ("new_m_", mn), ("new_v_", vn)):
            out[key + name] = back(val)

    my_slot = me.astype(jnp.int32).reshape(1)
    for i, (names, send_sems, recv_sems, sent, lands) in enumerate(in_flight):
        sent, lands = _exchange_wait(send_sems, recv_sems, sent, lands, dmod_slots, name="wait_gradients_%d" % i)
        for name, own_slots, landed in zip(names, sent, lands):
            update(name, landed, own=(own_slots, my_slot))

    dmod_all = dmod_slots.reshape(N_DEV * B, 6 * D)
    dmod_cols = lax.dynamic_slice(dmod_all, (0, me * n_ada), (N_DEV * B, n_ada))
    d_w_ada, d_b_ada = _ada_bwd(c_all, dmod_all, dmod_cols)
    update("w_ada", d_w_ada[None])

    loss_row, loss_n = small_offs[0]
    small_sum, loss_vec = _sum_parts(small_all, (loss_row, loss_row + loss_n // LANES))
    shapes = [(1, D), (1, D), (1, 2 * D), (SSM_GROUPS, SSM_STATE), (SSM_GROUPS, SSM_STATE), (SSM_COLS, SSM_GROUP_CH),
              (SSM_COLS, SSM_GROUP_CH), (1, SSM_GROUPS, SSM_GROUP_CH, SSM_STATE),
              (1, SSM_GROUPS, SSM_GROUP_CH, SSM_STATE), (1, D), (1, D_FF), (D,), (1, SSM_WIDTH), (1, SSM_WIDTH)]
    (_, s_g_mix, s_b_gate, s_ab_re, s_ab_im, s_bb_re, s_bb_im, s_c_re, s_c_im, s_g_ffn, s_b_conv, s_g_final,
     s_d_skip, s_b_glu) = _unpack(small_sum, small_offs, shapes)
    d_b_re2, d_b_im2, d_f_re, d_f_im = _s5_input_matrix_bwd(col(f_re), col(f_im), b_re2, b_im2, s_bb_re, s_bb_im)
    d_a_re, d_a_im, d_log_dt = _s5_params_bwd(a_re[0], a_im[0], log_dt[0].reshape(SSM_GROUPS, 1), s_ab_re, s_ab_im,
                                              d_f_re.reshape(SSM_GROUPS, SSM_STATE),
                                              d_f_im.reshape(SSM_GROUPS, SSM_STATE))
    grads_small = dict(b_ada=d_b_ada, g_mix=s_g_mix, b_gate=s_b_gate, a_re=d_a_re[None], a_im=d_a_im[None],
                       log_dt=d_log_dt.reshape(1, SSM_GROUPS), b_re=d_b_re2.reshape(b_re.shape),
                       b_im=d_b_im2.reshape(b_im.shape), c_re=s_c_re, c_im=s_c_im, d_skip=s_d_skip, b_glu=s_b_glu,
                       g_ffn=s_g_ffn, b_conv=s_b_conv, g_final=s_g_final)
    flat2 = lambda a: a.reshape(-1, a.shape[-1])
    res = _adamw_small([flat2(args[n]) for n in SMALL_ORDER], [flat2(args["m_" + n]) for n in SMALL_ORDER],
                       [flat2(args["v_" + n]) for n in SMALL_ORDER],
                       [flat2(grads_small[n].reshape(args[n].shape)) for n in SMALL_ORDER])
    for i, n in enumerate(SMALL_ORDER):
        for k, key in enumerate(("grad_", "delta_", "new_m_", "new_v_")):
            out[key + n] = res[4 * i + k].reshape(args[n].shape)

    order = ["w_ada", "b_ada", "g_mix", "w_in", "b_gate", "a_re", "a_im", "log_dt", "b_re", "b_im", "c_re", "c_im",
             "d_skip", "w_glu", "b_glu", "w_proj_att", "w_proj_ssm", "w_out", "g_ffn", "w_up", "w_conv", "b_conv",
             "w_down", "g_final"]
    loss = loss_vec[0, 0]
    return (loss, grad_x, *[out[k + n] for k in ("grad_", "delta_", "new_m_", "new_v_") for n in order])
```

```python
import math

import jax
import jax.numpy as jnp
from jax import lax
from jax.experimental import pallas as pl
from jax.experimental.pallas import tpu as pltpu

F32 = jnp.float32
BF16 = jnp.bfloat16

N_DEV = 8
D_MODEL = 1024
N_HEADS = 8
HEAD_DIM = 64
ATT_WIDTH = N_HEADS * HEAD_DIM
DILATIONS = (1, 4, 16)
WIN = 128
SSM_GROUPS = 16
SSM_GROUP_CH = 16
SSM_WIDTH = SSM_GROUPS * SSM_GROUP_CH
SSM_STATE = 64
SSM_COLS = SSM_GROUPS * SSM_STATE
D_FF = 2048
EPS = 1e-6
NEG_INF = -1e30
ADAM_LR, ADAM_B1, ADAM_B2, ADAM_EPS, ADAM_WD, ADAM_STEP = 0.001, 0.9, 0.999, 1e-08, 0.01, 10

V7X_VMEM_LIMIT = 56 * 1024 * 1024
LANES = 128


def _params(n_grid):
    return pltpu.CompilerParams(dimension_semantics=("arbitrary",) * n_grid,
                                vmem_limit_bytes=V7X_VMEM_LIMIT)


def _tile(n, pref):
    if n <= pref:
        return n
    t = (pref // LANES) * LANES
    while t > 0:
        if n % t == 0:
            return t
        t -= LANES
    return n


def _matmul(a, b, *, ta=False, tb=False, out_dtype=F32, name):
    if ta:
        K, M = a.shape
    else:
        M, K = a.shape
    if tb:
        N, K2 = b.shape
    else:
        K2, N = b.shape
    assert K == K2, (a.shape, b.shape)
    if ta:
        tm, tn, tk = _tile(M, 1024), _tile(N, 2048), _tile(K, 1024)
    else:
        tm, tk = _tile(M, 512), _tile(K, 4096)
        tn = _tile(N, 2048 if K <= 2048 else 1024)
    nk = K // tk
    dn = (((0,) if ta else (1,), (1,) if tb else (0,)), ((), ()))

    def body(a_ref, b_ref, o_ref, acc_ref):
        k = pl.program_id(2)
        part = lax.dot_general(a_ref[...].astype(BF16), b_ref[...].astype(BF16), dn, preferred_element_type=F32)
        if nk == 1:
            o_ref[...] = part.astype(o_ref.dtype)
            return

        @pl.when(k == 0)
        def _():
            acc_ref[...] = jnp.zeros_like(acc_ref)

        acc_ref[...] += part

        @pl.when(k == nk - 1)
        def _():
            o_ref[...] = acc_ref[...].astype(o_ref.dtype)

    a_spec = (pl.BlockSpec((tk, tm), lambda j, i, k: (k, i)) if ta
              else pl.BlockSpec((tm, tk), lambda j, i, k: (i, k)))
    b_spec = (pl.BlockSpec((tn, tk), lambda j, i, k: (j, k)) if tb
              else pl.BlockSpec((tk, tn), lambda j, i, k: (k, j)))
    return pl.pallas_call(
        body, name=name, grid=(N // tn, M // tm, nk),
        in_specs=[a_spec, b_spec],
        out_specs=pl.BlockSpec((tm, tn), lambda j, i, k: (i, j)),
        out_shape=jax.ShapeDtypeStruct((M, N), out_dtype),
        scratch_shapes=[pltpu.VMEM((tm, tn) if nk > 1 else (8, LANES), F32)],
        compiler_params=_params(3),
    )(a, b)


HALF = 256
UP_SLOTS = N_DEV // 2
UP_GROUP = 4 * HALF


def _group_weight(w_ref):
    return jnp.concatenate([w_ref[0, :, :HALF], w_ref[1, :, :HALF], w_ref[0, :, HALF:], w_ref[1, :, HALF:]], axis=1)


def _up_weight_spec(K, index):
    return pl.BlockSpec((2, None, K, 2 * HALF), index)


def _up_fwd(a, w3, name):
    M, K = a.shape
    tm = _tile(M, 1024)

    def body(a_ref, w_ref, o_ref):
        o_ref[...] = jnp.dot(a_ref[...].astype(BF16), _group_weight(w_ref),
                             preferred_element_type=F32).astype(o_ref.dtype)

    return pl.pallas_call(
        body, name=name, grid=(UP_SLOTS, M // tm),
        in_specs=[pl.BlockSpec((tm, K), lambda j, i: (i, 0)), _up_weight_spec(K, lambda j, i: (0, j, 0, 0))],
        out_specs=pl.BlockSpec((tm, UP_GROUP), lambda j, i: (i, j)),
        out_shape=jax.ShapeDtypeStruct((M, UP_SLOTS * UP_GROUP), BF16), compiler_params=_params(2),
    )(a, w3.reshape(2, UP_SLOTS, K, 2 * HALF))


def _up_dx(d, w3, name):
    M = d.shape[0]
    K = w3.shape[1]
    tm = _tile(M, 1024)

    def body(d_ref, w_ref, o_ref, acc_ref):
        j = pl.program_id(1)

        @pl.when(j == 0)
        def _():
            acc_ref[...] = jnp.zeros_like(acc_ref)

        acc_ref[...] += lax.dot_general(d_ref[...], _group_weight(w_ref), _NT, preferred_element_type=F32)

        @pl.when(j == UP_SLOTS - 1)
        def _():
            o_ref[...] = acc_ref[...].astype(o_ref.dtype)

    return pl.pallas_call(
        body, name=name, grid=(M // tm, UP_SLOTS),
        in_specs=[pl.BlockSpec((tm, UP_GROUP), lambda i, j: (i, j)), _up_weight_spec(K, lambda i, j: (0, j, 0, 0))],
        out_specs=pl.BlockSpec((tm, K), lambda i, j: (i, 0)),
        out_shape=jax.ShapeDtypeStruct((M, K), BF16), scratch_shapes=[pltpu.VMEM((tm, K), F32)],
        compiler_params=_params(2),
    )(d, w3.reshape(2, UP_SLOTS, K, 2 * HALF))


def _up_dw(a, d, name):
    M, K = a.shape
    tk = _tile(M, 1024)
    nk = M // tk

    def body(a_ref, d_ref, o_ref, acc_ref):
        k = pl.program_id(1)

        @pl.when(k == 0)
        def _():
            acc_ref[...] = jnp.zeros_like(acc_ref)

        acc_ref[...] += lax.dot_general(a_ref[...], d_ref[...], _TN, preferred_element_type=F32)

        @pl.when(k == nk - 1)
        def _():
            for half in range(2):
                for part in range(2):
                    lo = (2 * half + part) * HALF
                    o_ref[part, :, half * HALF:(half + 1) * HALF] = acc_ref[:, lo:lo + HALF].astype(o_ref.dtype)

    out = pl.pallas_call(
        body, name=name, grid=(UP_SLOTS, nk),
        in_specs=[pl.BlockSpec((tk, K), lambda j, k: (k, 0)), pl.BlockSpec((tk, UP_GROUP), lambda j, k: (k, j))],
        out_specs=_up_weight_spec(K, lambda j, k: (0, j, 0, 0)),
        out_shape=jax.ShapeDtypeStruct((2, UP_SLOTS, K, 2 * HALF), BF16),
        scratch_shapes=[pltpu.VMEM((K, UP_GROUP), F32)], compiler_params=_params(2),
    )(a, d)
    return out.reshape(N_DEV, K, 2 * HALF)


def _rowwise(fn, rows, bvecs, consts, out_rows, out_b, out_g, *, ts, name, raw=()):
    B, S = rows[0][0].shape[:2]
    nin = len(rows) + len(bvecs) + len(consts)
    nr, nb, ng = len(out_rows), len(out_b), len(out_g)

    def body(*refs):
        b = pl.program_id(0)
        s = pl.program_id(1)
        vals = [r[...] for r in refs[:nin]]
        vals[:len(rows)] = [v if i in raw else v.astype(F32) for i, v in enumerate(vals[:len(rows)])]
        outs = fn(*vals)
        if not isinstance(outs, (tuple, list)):
            outs = (outs,)
        orefs = refs[nin:]
        for i in range(nr):
            orefs[i][...] = outs[i].astype(orefs[i].dtype)
        for i in range(nb):
            ref = orefs[nr + i]

            @pl.when(s == 0)
            def _(ref=ref):
                ref[...] = jnp.zeros_like(ref)

            ref[...] += outs[nr + i]
        for i in range(ng):
            ref = orefs[nr + nb + i]

            @pl.when((s == 0) & (b == 0))
            def _(ref=ref):
                ref[...] = jnp.zeros_like(ref)

            ref[...] += outs[nr + nb + i]

    rows = [r if len(r) == 4 else r + (0,) for r in rows]
    in_specs = ([pl.BlockSpec((None, ts, cb), lambda b, s, ci=ci, b0=b0: (b0 + b, s, ci)) for (_, cb, ci, b0) in rows]
                + [pl.BlockSpec((None, 1, cb), lambda b, s, ci=ci: (b, 0, ci)) for (_, cb, ci) in bvecs]
                + [pl.BlockSpec(a.shape, lambda b, s: (0, 0)) for a in consts])
    out_shape = ([jax.ShapeDtypeStruct((B, S, c), dt) for (c, dt) in out_rows]
                 + [jax.ShapeDtypeStruct((B, 1, c), F32) for c in out_b]
                 + [jax.ShapeDtypeStruct(rc, F32) for rc in out_g])
    out_specs = ([pl.BlockSpec((None, ts, c), lambda b, s: (b, s, 0)) for (c, _) in out_rows]
                 + [pl.BlockSpec((None, 1, c), lambda b, s: (b, 0, 0)) for c in out_b]
                 + [pl.BlockSpec(rc, lambda b, s: (0, 0)) for rc in out_g])
    args = [r[0] for r in rows] + [a for (a, _, _) in bvecs] + list(consts)
    return pl.pallas_call(
        body, name=name, grid=(B, S // ts), in_specs=in_specs, out_specs=out_specs,
        out_shape=out_shape, compiler_params=_params(2),
    )(*args)


def _col_sum(v):
    return jnp.sum(v, axis=0, keepdims=True)


def _rms_scale(h):
    return lax.rsqrt(jnp.mean(h * h, axis=-1, keepdims=True) + EPS)


def _sigmoid(v):
    return 0.5 * (1.0 + jnp.tanh(0.5 * v))


ATT_SCALE = HEAD_DIM ** -0.5
COPY_ROWS = 256
_NT = (((1,), (1,)), ((), ()))
_TN = (((0,), (0,)), ((), ()))


def _row_chunks(d, seq):
    sub = seq // d
    out = []
    for r in range(d):
        for c0 in range(0, sub, COPY_ROWS):
            n = min(COPY_ROWS, sub - c0)
            out.append((pl.ds(r + c0 * d, n, stride=d), r * sub + c0, n))
    return out


ATT_UNROLL = 16
KEYS = 2 * WIN


def _zero_once(refs):
    @pl.when((pl.program_id(0) == 0) & (pl.program_id(1) == 0))
    def _():
        for r in refs:
            r[...] = jnp.zeros_like(r)


def _pair_bias(bias_ref, slopes_ref, hp, d, key_major):
    shape = (KEYS, WIN) if key_major else (WIN, KEYS)
    qi = lax.broadcasted_iota(jnp.int32, shape, 1 if key_major else 0)
    kj = lax.broadcasted_iota(jnp.int32, shape, 0 if key_major else 1)
    dist = WIN + qi - kj
    valid = (dist >= 0) & (dist <= WIN)
    distf = dist.astype(F32)
    for h in range(2):
        slope_d = slopes_ref[2 * hp + h] * float(d)
        with_prev = jnp.where(valid, -(slope_d * distf), NEG_INF)
        no_prev = jnp.where(kj >= WIN, with_prev, NEG_INF)
        span = slice(h * KEYS, (h + 1) * KEYS)
        if key_major:
            bias_ref[1, span, :] = with_prev
            bias_ref[0, span, :] = no_prev
        else:
            bias_ref[1, :, span] = with_prev
            bias_ref[0, :, span] = no_prev


def _stack_heads(v):
    first = lax.broadcasted_iota(jnp.int32, v.shape, 1) < HEAD_DIM
    zero = jnp.zeros_like(v)
    return jnp.concatenate([jnp.where(first, v, zero), jnp.where(first, zero, v)], axis=0)


def _per_head(c0, c1, n):
    return jnp.where(lax.broadcasted_iota(jnp.int32, (n, LANES), 1) < HEAD_DIM, c0, c1)


def _qkv_spec(seq, j):
    return pl.BlockSpec((None, seq, LANES), lambda b, hp: (b, 0, 3 * hp + j))


def _attention_fwd(qkv, slopes):
    B, S, _ = qkv.shape
    n_blk = S // WIN
    n_pair = N_HEADS // 2

    def body(slopes_ref, q_ref, k_ref, v_ref, o_ref, lse_ref, qp, kp, vp, bias, acc, mx, sm, acc_n, mx_n, sm_n):
        hp = pl.program_id(1)
        _zero_once((kp, vp))
        for p, d in enumerate(DILATIONS):
            nb = n_blk // d
            chunks = _row_chunks(d, S)
            for src, dst, n in chunks:
                qp[dst:dst + n, :] = (q_ref[src, :] * ATT_SCALE).astype(BF16)
                kp[WIN + dst:WIN + dst + n, :] = k_ref[src, :].astype(BF16)
                vp[WIN + dst:WIN + dst + n, :] = v_ref[src, :].astype(BF16)
            _pair_bias(bias, slopes_ref, hp, d, key_major=False)
            acc_t, mx_t, sm_t = (acc_n, mx_n, sm_n) if d == 1 else (acc, mx, sm)

            nk = WIN if nb == 1 else KEYS
            bias_cur = jnp.concatenate([bias[0, :, WIN:KEYS], bias[0, :, KEYS + WIN:]], axis=1) if nb == 1 else None

            def block(i, carry, p=p, nb=nb, nk=nk, bias_cur=bias_cur, acc_t=acc_t, mx_t=mx_t, sm_t=sm_t):
                cur = pl.ds(pl.multiple_of(i * WIN, WIN), WIN)
                keys = pl.ds(pl.multiple_of(i * WIN + (KEYS - nk), WIN), nk)
                s = lax.dot_general(qp[cur, :], _stack_heads(kp[keys, :]), _NT, preferred_element_type=F32)
                s = s + (bias_cur if nb == 1 else bias[((i % nb) > 0).astype(jnp.int32)])
                es, ms, ls = [], [], []
                for h in range(2):
                    sh = s[:, h * nk:(h + 1) * nk]
                    m = jnp.max(sh if nb == 1 else jnp.maximum(sh[:, :WIN], sh[:, WIN:]), axis=1, keepdims=True)
                    e = jnp.exp(sh - m)
                    es.append(e.astype(BF16))
                    ms.append(m)
                    ls.append(jnp.sum(e if nb == 1 else e[:, :WIN] + e[:, WIN:], axis=1, keepdims=True))
                acc_t[p, cur, :] = jnp.dot(jnp.concatenate(es, axis=1), _stack_heads(vp[keys, :]),
                                           preferred_element_type=F32)
                mx_t[p, cur, :] = _per_head(ms[0], ms[1], WIN)
                sm_t[p, cur, :] = _per_head(ls[0], ls[1], WIN)
                return carry

            lax.fori_loop(0, n_blk, block, 0, unroll=ATT_UNROLL)
            if d > 1:
                for src, dst, n in chunks:
                    acc_n[p, src, :] = acc[p, dst:dst + n, :]
                    mx_n[p, src, :] = mx[p, dst:dst + n, :]
                    sm_n[p, src, :] = sm[p, dst:dst + n, :]

        chunk = 256

        def merge(i, carry):
            rows = pl.ds(pl.multiple_of(i * chunk, chunk), chunk)
            ms = [mx_n[p, rows, :] for p in range(3)]
            m = jnp.maximum(jnp.maximum(ms[0], ms[1]), ms[2])
            ws = [jnp.exp(mp - m) for mp in ms]
            l = ws[0] * sm_n[0, rows, :] + ws[1] * sm_n[1, rows, :] + ws[2] * sm_n[2, rows, :]
            o = (ws[0] * acc_n[0, rows, :] + ws[1] * acc_n[1, rows, :] + ws[2] * acc_n[2, rows, :]) / l
            o_ref[rows, :] = o.astype(o_ref.dtype)
            lse = m + jnp.log(l)
            for h in range(2):
                lse_ref[rows, h:h + 1] = lse[:, h * HEAD_DIM:h * HEAD_DIM + 1]
            return carry

        lax.fori_loop(0, S // chunk, merge, 0)

    return pl.pallas_call(
        body, name="attention_fwd", grid=(B, n_pair),
        in_specs=[pl.BlockSpec(memory_space=pltpu.SMEM), _qkv_spec(S, 0), _qkv_spec(S, 1), _qkv_spec(S, 2)],
        out_specs=[pl.BlockSpec((None, S, LANES), lambda b, hp: (b, 0, hp)),
                   pl.BlockSpec((None, None, S, 2), lambda b, hp: (b, hp, 0, 0))],
        out_shape=[jax.ShapeDtypeStruct((B, S, ATT_WIDTH), BF16),
                   jax.ShapeDtypeStruct((B, n_pair, S, 2), F32)],
        scratch_shapes=[pltpu.VMEM((S, LANES), BF16), pltpu.VMEM((S + WIN, LANES), BF16),
                        pltpu.VMEM((S + WIN, LANES), BF16), pltpu.VMEM((2, WIN, 2 * KEYS), F32)]
        + [pltpu.VMEM((3, S, LANES), F32)] * 6,
        compiler_params=_params(2),
    )(slopes, qkv, qkv, qkv)


def _attention_bwd(qkv, o, do, lse, slopes):
    B, S, _ = qkv.shape
    n_blk = S // WIN
    n_pair = N_HEADS // 2

    def body(slopes_ref, q_ref, k_ref, v_ref, o_ref, do_ref, lse_ref, dx_ref,
             qp, dop, kp, vp, aux, auxp, aux_t, bias_t, dqp, dvk, dq_n, dk_n, dv_n):
        hp = pl.program_id(1)
        aux[...] = jnp.zeros_like(aux)
        for c0 in range(0, S, COPY_ROWS):
            rows = slice(c0, c0 + COPY_ROWS)
            prod = do_ref[rows, :] * o_ref[rows, :].astype(F32)
            for h in range(2):
                aux[rows, 2 * h:2 * h + 1] = lse_ref[rows, h:h + 1]
                aux[rows, 2 * h + 1:2 * h + 2] = jnp.sum(prod[:, h * HEAD_DIM:(h + 1) * HEAD_DIM], axis=1,
                                                         keepdims=True)
        dq_n[...] = jnp.zeros_like(dq_n)
        dk_n[...] = jnp.zeros_like(dk_n)
        dv_n[...] = jnp.zeros_like(dv_n)
        _zero_once((kp, vp))
        for p, d in enumerate(DILATIONS):
            nb = n_blk // d
            chunks = _row_chunks(d, S)
            for src, dst, n in chunks:
                auxp[dst:dst + n, :] = aux[src, :]
                qp[dst:dst + n, :] = (q_ref[src, :] * ATT_SCALE).astype(BF16)
                dop[dst:dst + n, :] = do_ref[src, :].astype(BF16)
                kp[WIN + dst:WIN + dst + n, :] = k_ref[src, :].astype(BF16)
                vp[WIN + dst:WIN + dst + n, :] = v_ref[src, :].astype(BF16)
            for i in range(n_blk):
                aux_t[i] = auxp[i * WIN:(i + 1) * WIN, :].T[0:8, :]
            _pair_bias(bias_t, slopes_ref, hp, d, key_major=True)
            dvk[...] = jnp.zeros_like(dvk)

            nk = WIN if nb == 1 else KEYS
            bias_cur = jnp.concatenate([bias_t[0, WIN:KEYS, :], bias_t[0, KEYS + WIN:, :]], axis=0) if nb == 1 else None

            def block(i, carry, nb=nb, nk=nk, bias_cur=bias_cur):
                cur = pl.ds(pl.multiple_of(i * WIN, WIN), WIN)
                keys = pl.ds(pl.multiple_of(i * WIN + (KEYS - nk), WIN), nk)
                q2, do2 = qp[cur, :], dop[cur, :]
                kc = _stack_heads(kp[keys, :])
                s_t = lax.dot_general(kc, q2, _NT, preferred_element_type=F32)
                s_t = s_t + (bias_cur if nb == 1 else bias_t[((i % nb) > 0).astype(jnp.int32)])
                dp_t = lax.dot_general(_stack_heads(vp[keys, :]), do2, _NT, preferred_element_type=F32)
                ps, dss = [], []
                for h in range(2):
                    span = slice(h * nk, (h + 1) * nk)
                    p_t = jnp.exp(s_t[span, :] - aux_t[i, 2 * h:2 * h + 1, :])
                    ds_t = p_t * (dp_t[span, :] - aux_t[i, 2 * h + 1:2 * h + 2, :])
                    ps.append(p_t.astype(BF16))
                    dss.append(ds_t.astype(BF16))
                do_rows, q_rows = _stack_heads(do2), _stack_heads(q2)
                zr = jnp.zeros_like(do_rows)
                rhs = jnp.concatenate([jnp.concatenate([do_rows, zr], axis=1),
                                       jnp.concatenate([zr, q_rows], axis=1)], axis=0)
                dvk[keys, :] += jnp.dot(jnp.concatenate(ps + dss, axis=1), rhs, preferred_element_type=F32)
                dqp[cur, :] = lax.dot_general(jnp.concatenate(dss, axis=0), kc, _TN, preferred_element_type=F32)
                return carry

            lax.fori_loop(0, n_blk, block, 0, unroll=ATT_UNROLL)
            for src, dst, n in chunks:
                dq_n[src, :] += dqp[dst:dst + n, :]
                dv_n[src, :] += dvk[WIN + dst:WIN + dst + n, :LANES]
                dk_n[src, :] += dvk[WIN + dst:WIN + dst + n, LANES:]
        for c0 in range(0, S, COPY_ROWS):
            rows = slice(c0, c0 + COPY_ROWS)
            dx_ref[rows, 0:LANES] = (dq_n[rows, :] * ATT_SCALE).astype(dx_ref.dtype)
            dx_ref[rows, LANES:2 * LANES] = dk_n[rows, :].astype(dx_ref.dtype)
            dx_ref[rows, 2 * LANES:3 * LANES] = dv_n[rows, :].astype(dx_ref.dtype)

    pair = lambda width: pl.BlockSpec((None, S, width), lambda b, hp: (b, 0, hp))
    vm = lambda shape, dt: pltpu.VMEM(shape, dt)
    return pl.pallas_call(
        body, name="attention_bwd", grid=(B, n_pair),
        in_specs=[pl.BlockSpec(memory_space=pltpu.SMEM), _qkv_spec(S, 0), _qkv_spec(S, 1), _qkv_spec(S, 2),
                  pair(LANES), pair(LANES), pl.BlockSpec((None, None, S, 2), lambda b, hp: (b, hp, 0, 0))],
        out_specs=pair(3 * LANES),
        out_shape=jax.ShapeDtypeStruct((B, S, 3 * ATT_WIDTH), BF16),
        scratch_shapes=[vm((S, LANES), BF16), vm((S, LANES), BF16),
                        vm((S + WIN, LANES), BF16), vm((S + WIN, LANES), BF16),
                        vm((S, LANES), F32), vm((S, LANES), F32), vm((n_blk, 8, WIN), F32),
                        vm((2, 2 * KEYS, WIN), F32),
                        vm((S, LANES), F32), vm((S + WIN, 2 * LANES), F32),
                        vm((S, LANES), F32), vm((S, LANES), F32), vm((S, LANES), F32)],
        compiler_params=_params(2),
    )(slopes, qkv, qkv, qkv, o, do, lse)


SCAN_COLS = 256
SCAN_ROWS = 8


def _rows_to_tile(rows):
    rid = lax.broadcasted_iota(jnp.int32, (SCAN_ROWS, rows[0].shape[1]), 0)
    tile = jnp.broadcast_to(rows[0], rid.shape)
    for k in range(1, SCAN_ROWS):
        tile = jnp.where(rid == k, rows[k], tile)
    return tile


SCAN_UNROLL = 4


def _complex_powers(ar, ai, n):
    out = [(ar, ai)]
    for _ in range(n - 1):
        pr, pi = out[-1]
        out.append((pr * ar - pi * ai, pr * ai + pi * ar))
    return out


def _round_multipliers(powers, rid, reverse):
    out = []
    for s in (1, 2, 4):
        keep = (rid < SCAN_ROWS - s) if reverse else (rid >= s)
        out.append((jnp.where(keep, powers[s - 1][0], 0.0), jnp.where(keep, powers[s - 1][1], 0.0)))
    return out


def _tile_scan(xr, xi, multipliers, reverse):
    for s, (mr, mi) in zip((1, 2, 4), multipliers):
        shift = SCAN_ROWS - s if reverse else s
        sr, si = pltpu.roll(xr, shift, 0), pltpu.roll(xi, shift, 0)
        xr, xi = xr + (mr * sr - mi * si), xi + (mr * si + mi * sr)
    return xr, xi


SCAN_CHUNK = 256


def _scan_fwd(us, bb_big, a_row, cc_big):
    B, S, _ = us.shape
    groups = 2
    width = 2 * groups * SCAN_COLS
    nc = 2 * SSM_COLS // width
    nt = S // SCAN_ROWS
    tiles = SCAN_CHUNK // SCAN_ROWS
    LAST = slice(SCAN_ROWS - 1, SCAN_ROWS)

    def body(us_ref, bb_ref, a_ref, cc_ref, xs_ref, y_ref, bu_ref):
        bb = bb_ref[...].astype(BF16)
        for c in range(S // SCAN_CHUNK):
            part = jnp.dot(us_ref[c * SCAN_CHUNK:(c + 1) * SCAN_CHUNK, :].astype(BF16), bb,
                           preferred_element_type=F32)
            bu_ref[c * tiles:(c + 1) * tiles] = part.reshape(tiles, SCAN_ROWS, width)
        rid = lax.broadcasted_iota(jnp.int32, (SCAN_ROWS, SCAN_COLS), 0)
        consts = []
        for g in range(groups):
            re = slice(2 * g * SCAN_COLS, (2 * g + 1) * SCAN_COLS)
            im = slice((2 * g + 1) * SCAN_COLS, (2 * g + 2) * SCAN_COLS)
            powers = _complex_powers(a_ref[:, re], a_ref[:, im], SCAN_ROWS)
            carry_mult = (_rows_to_tile([p[0] for p in powers]), _rows_to_tile([p[1] for p in powers]))
            consts.append((re, im, carry_mult, _round_multipliers(powers, rid, reverse=False)))

        def tile(i, carry):
            out = []
            for (re, im, (cr_t, ci_t), rounds), (cr, ci) in zip(consts, carry):
                xr, xi = _tile_scan(bu_ref[i, :, re], bu_ref[i, :, im], rounds, reverse=False)
                xs_ref[i, :, re] = xr + (cr_t * cr - ci_t * ci)
                xs_ref[i, :, im] = xi + (cr_t * ci + ci_t * cr)
                out.append((xs_ref[i, LAST, re], xs_ref[i, LAST, im]))
            return tuple(out)

        zero = jnp.zeros((1, SCAN_COLS), F32)
        lax.fori_loop(0, nt, tile, ((zero, zero),) * groups, unroll=SCAN_UNROLL)

        @pl.when(pl.program_id(1) == 0)
        def _():
            y_ref[...] = jnp.zeros_like(y_ref)

        cc = cc_ref[...].astype(BF16)
        for c in range(S // SCAN_CHUNK):
            x2 = xs_ref[c * tiles:(c + 1) * tiles].reshape(SCAN_CHUNK, width).astype(BF16)
            y_ref[c * SCAN_CHUNK:(c + 1) * SCAN_CHUNK, :] += jnp.dot(x2, cc, preferred_element_type=F32)

    col = pl.BlockSpec((None, nt, SCAN_ROWS, width), lambda b, j: (b, 0, 0, j))
    tok = pl.BlockSpec((None, S, SSM_WIDTH), lambda b, j: (b, 0, 0))
    xs, y = pl.pallas_call(
        body, name="s5_scan_fwd", grid=(B, nc),
        in_specs=[tok, pl.BlockSpec((SSM_WIDTH, width), lambda b, j: (0, j)),
                  pl.BlockSpec((1, width), lambda b, j: (0, j)), pl.BlockSpec((width, SSM_WIDTH), lambda b, j: (j, 0))],
        out_specs=[col, tok],
        out_shape=[jax.ShapeDtypeStruct((B, nt, SCAN_ROWS, 2 * SSM_COLS), F32),
                   jax.ShapeDtypeStruct((B, S, SSM_WIDTH), F32)],
        scratch_shapes=[pltpu.VMEM((nt, SCAN_ROWS, width), F32)],
        compiler_params=_params(2),
    )(us, bb_big, a_row, cc_big)
    return xs.reshape(B, S, 2 * SSM_COLS), y


def _scan_bwd(dy, us, bb_big, cc_big, xs, a_row):
    B, S, _ = dy.shape
    width = 2 * SCAN_COLS
    nc = SSM_COLS // SCAN_COLS
    nt = S // SCAN_ROWS
    tiles = SCAN_CHUNK // SCAN_ROWS
    RE, IM = slice(0, SCAN_COLS), slice(SCAN_COLS, 2 * SCAN_COLS)
    FIRST, LAST = slice(0, 1), slice(SCAN_ROWS - 1, SCAN_ROWS)

    def body(dy_ref, us_ref, bb_ref, cc_ref, x_ref, a_ref, dus_ref, ga_ref, dbb_ref, dcc_ref, d_ref, lam_ref):
        b = pl.program_id(1)
        cc = cc_ref[...].astype(BF16)
        for c in range(S // SCAN_CHUNK):
            part = lax.dot_general(dy_ref[c * SCAN_CHUNK:(c + 1) * SCAN_CHUNK, :].astype(BF16), cc, _NT,
                                   preferred_element_type=F32)
            d_ref[c * tiles:(c + 1) * tiles] = part.reshape(tiles, SCAN_ROWS, width)
        powers = _complex_powers(a_ref[:, RE], -a_ref[:, IM], SCAN_ROWS)
        rid = lax.broadcasted_iota(jnp.int32, (SCAN_ROWS, SCAN_COLS), 0)
        cr_t = _rows_to_tile([powers[SCAN_ROWS - 1 - r][0] for r in range(SCAN_ROWS)])
        ci_t = _rows_to_tile([powers[SCAN_ROWS - 1 - r][1] for r in range(SCAN_ROWS)])
        rounds = _round_multipliers(powers, rid, reverse=True)

        @pl.when(b == 0)
        def _():
            ga_ref[...] = jnp.zeros_like(ga_ref)
            dbb_ref[...] = jnp.zeros_like(dbb_ref)
            dcc_ref[...] = jnp.zeros_like(dcc_ref)

        def tile(j, carry):
            cr, ci, accr, acci = carry
            i = nt - 1 - j
            lr, li = _tile_scan(d_ref[i, :, RE], d_ref[i, :, IM], rounds, reverse=True)
            lam_r = lr + (cr_t * cr - ci_t * ci)
            lam_i = li + (cr_t * ci + ci_t * cr)
            lam_ref[i, :, RE] = lam_r
            lam_ref[i, :, IM] = lam_i
            ip = jnp.maximum(i - 1, 0)
            keep = (i > 0).astype(F32)
            xpr = jnp.where(rid == 0, x_ref[ip, LAST, RE] * keep, pltpu.roll(x_ref[i, :, RE], 1, 0))
            xpi = jnp.where(rid == 0, x_ref[ip, LAST, IM] * keep, pltpu.roll(x_ref[i, :, IM], 1, 0))
            accr = accr + lam_r * xpr + lam_i * xpi
            acci = acci + lam_i * xpr - lam_r * xpi
            return lam_ref[i, FIRST, RE], lam_ref[i, FIRST, IM], accr, acci

        z1 = jnp.zeros((1, SCAN_COLS), F32)
        z8 = jnp.zeros((SCAN_ROWS, SCAN_COLS), F32)
        _, _, accr, acci = lax.fori_loop(0, nt, tile, (z1, z1, z8, z8), unroll=SCAN_UNROLL)
        ga_ref[:, RE] += _col_sum(accr)
        ga_ref[:, IM] += _col_sum(acci)

        bb = bb_ref[...].astype(BF16)
        for c in range(S // SCAN_CHUNK):
            rows = slice(c * SCAN_CHUNK, (c + 1) * SCAN_CHUNK)
            lam2 = lam_ref[c * tiles:(c + 1) * tiles].reshape(SCAN_CHUNK, width).astype(BF16)
            x2 = x_ref[c * tiles:(c + 1) * tiles].reshape(SCAN_CHUNK, width).astype(BF16)
            dus_ref[rows, :] = lax.dot_general(lam2, bb, _NT, preferred_element_type=F32)
            dbb_ref[...] += lax.dot_general(us_ref[rows, :].astype(BF16), lam2, _TN, preferred_element_type=F32)
            dcc_ref[...] += lax.dot_general(x2, dy_ref[rows, :].astype(BF16), _TN, preferred_element_type=F32)

    col = pl.BlockSpec((None, nt, SCAN_ROWS, width), lambda j, b: (b, 0, 0, j))
    tok = pl.BlockSpec((None, S, SSM_WIDTH), lambda j, b: (b, 0, 0))
    scratch = pltpu.VMEM((nt, SCAN_ROWS, width), F32)
    return pl.pallas_call(
        body, name="s5_scan_bwd", grid=(nc, B),
        in_specs=[tok, tok, pl.BlockSpec((SSM_WIDTH, width), lambda j, b: (0, j)),
                  pl.BlockSpec((width, SSM_WIDTH), lambda j, b: (j, 0)), col,
                  pl.BlockSpec((1, width), lambda j, b: (0, j))],
        out_specs=[pl.BlockSpec((None, None, S, SSM_WIDTH), lambda j, b: (j, b, 0, 0)),
                   pl.BlockSpec((1, width), lambda j, b: (0, j)),
                   pl.BlockSpec((SSM_WIDTH, width), lambda j, b: (0, j)),
                   pl.BlockSpec((width, SSM_WIDTH), lambda j, b: (j, 0))],
        out_shape=[jax.ShapeDtypeStruct((nc, B, S, SSM_WIDTH), F32), jax.ShapeDtypeStruct((1, 2 * SSM_COLS), F32),
                   jax.ShapeDtypeStruct((SSM_WIDTH, 2 * SSM_COLS), F32),
                   jax.ShapeDtypeStruct((2 * SSM_COLS, SSM_WIDTH), F32)],
        scratch_shapes=[scratch, scratch],
        compiler_params=_params(2),
    )(dy, us, bb_big, cc_big, xs.reshape(B, nt, SCAN_ROWS, 2 * SSM_COLS), a_row)


def _s5_discretise(lr, li, log_dt):
    dt = jnp.exp(log_dt)
    mag = jnp.exp(lr * dt)
    ang = li * dt
    ab_re, ab_im = mag * jnp.cos(ang), mag * jnp.sin(ang)
    nr, ni = ab_re - 1.0, ab_im
    den = lr * lr + li * li
    f_re = (nr * lr + ni * li) / den
    f_im = (ni * lr - nr * li) / den
    return dt, ab_re, ab_im, nr, ni, den, f_re, f_im


def _s5_params(a_re, a_im, log_dt):
    def body(lr_ref, li_ref, ld_ref, abr, abi, fr, fi):
        _, ab_re, ab_im, _, _, _, f_re, f_im = _s5_discretise(lr_ref[...], li_ref[...], ld_ref[...])
        abr[...] = ab_re
        abi[...] = ab_im
        fr[...] = f_re
        fi[...] = f_im

    return pl.pallas_call(body, name="s5_params",
                          out_shape=[jax.ShapeDtypeStruct(a_re.shape, F32)] * 4)(a_re, a_im, log_dt)


def _s5_input_matrix(f_re, f_im, b_re, b_im):
    def body(fr, fi, br, bi, o_re, o_im):
        o_re[...] = fr[...] * br[...] - fi[...] * bi[...]
        o_im[...] = fr[...] * bi[...] + fi[...] * br[...]

    return pl.pallas_call(body, name="s5_input_matrix",
                          out_shape=[jax.ShapeDtypeStruct(b_re.shape, F32)] * 2)(f_re, f_im, b_re, b_im)


def _s5_input_matrix_bwd(f_re, f_im, b_re, b_im, g_re, g_im):
    def body(fr, fi, br, bi, gr, gi, dbr, dbi, dfr, dfi):
        dbr[...] = fr[...] * gr[...] + fi[...] * gi[...]
        dbi[...] = fr[...] * gi[...] - fi[...] * gr[...]
        dfr[...] = jnp.sum(br[...] * gr[...] + bi[...] * gi[...], axis=1, keepdims=True)
        dfi[...] = jnp.sum(br[...] * gi[...] - bi[...] * gr[...], axis=1, keepdims=True)

    return pl.pallas_call(
        body, name="s5_input_matrix_bwd",
        out_shape=[jax.ShapeDtypeStruct(b_re.shape, F32)] * 2 + [jax.ShapeDtypeStruct(f_re.shape, F32)] * 2,
    )(f_re, f_im, b_re, b_im, g_re, g_im)


def _s5_params_bwd(a_re, a_im, log_dt, g_ab_re, g_ab_im, d_f_re, d_f_im):
    def body(lr_ref, li_ref, ld_ref, gar, gai, dfr, dfi, o_lr, o_li, o_ld):
        lr, li = lr_ref[...], li_ref[...]
        dt, ab_re, ab_im, nr, ni, den, f_re, f_im = _s5_discretise(lr, li, ld_ref[...])
        d_fr, d_fi = dfr[...], dfi[...]
        d_nr = (d_fr * lr - d_fi * li) / den
        d_ni = (d_fr * li + d_fi * lr) / den
        common = (d_fr * f_re + d_fi * f_im) * 2.0 / den
        d_lr = (d_fr * nr + d_fi * ni) / den - common * lr
        d_li = (d_fr * ni - d_fi * nr) / den - common * li
        d_abr = gar[...] + d_nr
        d_abi = gai[...] + d_ni
        d_mag_mag = d_abr * ab_re + d_abi * ab_im
        d_ang = d_abi * ab_re - d_abr * ab_im
        o_lr[...] = d_lr + d_mag_mag * dt
        o_li[...] = d_li + d_ang * dt
        o_ld[...] = jnp.sum(d_mag_mag * lr + d_ang * li, axis=1, keepdims=True) * dt

    return pl.pallas_call(
        body, name="s5_params_bwd",
        out_shape=[jax.ShapeDtypeStruct(a_re.shape, F32)] * 2 + [jax.ShapeDtypeStruct(log_dt.shape, F32)],
    )(a_re, a_im, log_dt, g_ab_re, g_ab_im, d_f_re, d_f_im)


CONV_COLS = 256


def _shift_down(v, j, row):
    return jnp.where(row >= j, pltpu.roll(v, j, 0), 0.0)


def _shift_up(v, j, row, seq):
    return jnp.where(row < seq - j, pltpu.roll(v, seq - j, 0), 0.0)


def _conv_fwd(up, w_conv, b_conv):
    B, S, _ = up.shape
    nj = D_FF // CONV_COLS

    def body(up_ref, w_ref, b_ref, ff_ref):
        a = up_ref[:, :CONV_COLS].astype(F32)
        val = up_ref[:, CONV_COLS:].astype(F32)
        row = lax.broadcasted_iota(jnp.int32, a.shape, 0)
        w0, w1, w2 = w_ref[0:1, :], w_ref[1:2, :], w_ref[2:3, :]
        conv = b_ref[...] + w0 * a + w1 * _shift_down(a, 1, row) + w2 * _shift_down(a, 2, row)
        ff_ref[...] = (conv * _sigmoid(conv) * val).astype(ff_ref.dtype)

    return pl.pallas_call(
        body, name="conv_gate_fwd", grid=(B, nj),
        in_specs=[pl.BlockSpec((None, S, 2 * CONV_COLS), lambda b, j: (b, 0, j)),
                  pl.BlockSpec((3, CONV_COLS), lambda b, j: (0, j)),
                  pl.BlockSpec((1, CONV_COLS), lambda b, j: (0, j))],
        out_specs=pl.BlockSpec((None, S, CONV_COLS), lambda b, j: (b, 0, j)),
        out_shape=jax.ShapeDtypeStruct((B, S, D_FF), BF16),
        compiler_params=_params(2),
    )(up, w_conv, b_conv)


def _conv_bwd(up, d_ff, w_conv, b_conv):
    B, S, _ = up.shape
    nj = D_FF // CONV_COLS

    def body(up_ref, dff_ref, w_ref, b_ref, dup_ref, dw_ref, db_ref):
        b = pl.program_id(1)
        a = up_ref[:, :CONV_COLS].astype(F32)
        val = up_ref[:, CONV_COLS:].astype(F32)
        row = lax.broadcasted_iota(jnp.int32, a.shape, 0)
        w0, w1, w2 = w_ref[0:1, :], w_ref[1:2, :], w_ref[2:3, :]
        a1, a2 = _shift_down(a, 1, row), _shift_down(a, 2, row)
        conv = b_ref[...] + w0 * a + w1 * a1 + w2 * a2
        sg = _sigmoid(conv)
        dff = dff_ref[...].astype(F32)
        d_val = dff * conv * sg
        dc = dff * val * (sg * (1.0 + conv * (1.0 - sg)))
        d_a = w0 * dc + w1 * _shift_up(dc, 1, row, S) + w2 * _shift_up(dc, 2, row, S)
        dup_ref[:, :CONV_COLS] = d_a.astype(dup_ref.dtype)
        dup_ref[:, CONV_COLS:] = d_val.astype(dup_ref.dtype)

        @pl.when(b == 0)
        def _():
            dw_ref[...] = jnp.zeros_like(dw_ref)
            db_ref[...] = jnp.zeros_like(db_ref)

        dw_ref[0:1, :] += _col_sum(dc * a)
        dw_ref[1:2, :] += _col_sum(dc * a1)
        dw_ref[2:3, :] += _col_sum(dc * a2)
        db_ref[...] += _col_sum(dc)

    return pl.pallas_call(
        body, name="conv_gate_bwd", grid=(nj, B),
        in_specs=[pl.BlockSpec((None, S, 2 * CONV_COLS), lambda j, b: (b, 0, j)),
                  pl.BlockSpec((None, S, CONV_COLS), lambda j, b: (b, 0, j)),
                  pl.BlockSpec((3, CONV_COLS), lambda j, b: (0, j)),
                  pl.BlockSpec((1, CONV_COLS), lambda j, b: (0, j))],
        out_specs=[pl.BlockSpec((None, S, 2 * CONV_COLS), lambda j, b: (b, 0, j)),
                   pl.BlockSpec((3, CONV_COLS), lambda j, b: (0, j)),
                   pl.BlockSpec((1, CONV_COLS), lambda j, b: (0, j))],
        out_shape=[jax.ShapeDtypeStruct((B, S, 2 * D_FF), BF16), jax.ShapeDtypeStruct((3, D_FF), F32),
                   jax.ShapeDtypeStruct((1, D_FF), F32)],
        compiler_params=_params(2),
    )(up, d_ff, w_conv, b_conv)


def _ada_fwd(c_all, w_ada, b_ada):
    def body(c_ref, w_ref, b_ref, o_ref):
        cv = c_ref[...]
        act = (cv * _sigmoid(cv)).astype(BF16)
        o_ref[...] = jnp.dot(act, w_ref[...].astype(BF16), preferred_element_type=F32) + b_ref[...]

    return pl.pallas_call(body, name="ada_fwd",
                          out_shape=jax.ShapeDtypeStruct((c_all.shape[0], w_ada.shape[1]), F32),
                          compiler_params=pltpu.CompilerParams(vmem_limit_bytes=V7X_VMEM_LIMIT))(c_all, w_ada, b_ada)


def _ada_bwd(c_all, dmod_all, dmod_cols):
    def body(c_ref, dm_ref, dmc_ref, dw_ref, db_ref):
        cv = c_ref[...]
        act = (cv * _sigmoid(cv)).astype(BF16)
        dw_ref[...] = lax.dot_general(act, dmc_ref[...].astype(BF16), _TN, preferred_element_type=F32)
        db_ref[...] = _col_sum(dm_ref[...])

    return pl.pallas_call(
        body, name="ada_bwd",
        out_shape=[jax.ShapeDtypeStruct((c_all.shape[1], dmod_cols.shape[1]), F32),
                   jax.ShapeDtypeStruct((1, dmod_all.shape[1]), F32)],
        compiler_params=pltpu.CompilerParams(vmem_limit_bytes=V7X_VMEM_LIMIT))(c_all, dmod_all, dmod_cols)


def _adamw(w, m, v, g_parts, name, own=None):
    R, C = w.shape
    P = g_parts.shape[0]
    tr = R
    for cand in (256, 128, 64, 32, 16, 8):
        if R % cand == 0 and cand * C * 4 * (P + 8) * 2 <= V7X_VMEM_LIMIT // 2:
            tr = cand
            break
    c1 = 1.0 / (1.0 - ADAM_B1 ** ADAM_STEP)
    c2 = 1.0 / (1.0 - ADAM_B2 ** ADAM_STEP)

    def update(w_ref, m_ref, v_ref, g, og, od, om, ov):
        m_new = ADAM_B1 * m_ref[...] + (1.0 - ADAM_B1) * g
        v_new = ADAM_B2 * v_ref[...] + (1.0 - ADAM_B2) * (g * g)
        og[...] = g
        om[...] = m_new
        ov[...] = v_new
        od[...] = -ADAM_LR * ((m_new * c1) / (jnp.sqrt(v_new * c2) + ADAM_EPS) + ADAM_WD * w_ref[...])

    def total(g_ref):
        g = g_ref[0].astype(F32)
        for p in range(1, P):
            g = g + g_ref[p].astype(F32)
        return g

    out_shape = [jax.ShapeDtypeStruct((R, C), F32)] * 4
    if own is None:
        def body(w_ref, m_ref, v_ref, g_ref, og, od, om, ov):
            update(w_ref, m_ref, v_ref, total(g_ref), og, od, om, ov)

        spec = pl.BlockSpec((tr, C), lambda i: (i, 0))
        return pl.pallas_call(
            body, name=name, grid=(R // tr,),
            in_specs=[spec, spec, spec, pl.BlockSpec((P, tr, C), lambda i: (0, i, 0))],
            out_specs=[spec] * 4, out_shape=out_shape, compiler_params=_params(1),
        )(w, m, v, g_parts)

    slots, me = own

    def body_own(me_ref, w_ref, m_ref, v_ref, g_ref, own_ref, og, od, om, ov):
        g = own_ref[...].astype(F32)
        for p in range(P):
            g = g + jnp.where(me_ref[0] == p, 0.0, g_ref[p].astype(F32))
        update(w_ref, m_ref, v_ref, g, og, od, om, ov)

    spec = pl.BlockSpec((tr, C), lambda i, me_ref: (i, 0))
    grid_spec = pltpu.PrefetchScalarGridSpec(
        num_scalar_prefetch=1, grid=(R // tr,),
        in_specs=[spec, spec, spec, pl.BlockSpec((P, tr, C), lambda i, me_ref: (0, i, 0)),
                  pl.BlockSpec((None, tr, C), lambda i, me_ref: (me_ref[0], i, 0))],
        out_specs=[spec] * 4)
    return pl.pallas_call(body_own, name=name, grid_spec=grid_spec, out_shape=out_shape,
                          compiler_params=_params(1))(me, w, m, v, g_parts, slots)


def _adamw_small(ws, ms, vs, gs):
    n = len(ws)
    c1 = 1.0 / (1.0 - ADAM_B1 ** ADAM_STEP)
    c2 = 1.0 / (1.0 - ADAM_B2 ** ADAM_STEP)

    def body(*refs):
        ins, outs = refs[:4 * n], refs[4 * n:]
        for i in range(n):
            w, m, v, g = ins[i][...], ins[n + i][...], ins[2 * n + i][...], ins[3 * n + i][...]
            m_new = ADAM_B1 * m + (1.0 - ADAM_B1) * g
            v_new = ADAM_B2 * v + (1.0 - ADAM_B2) * (g * g)
            outs[4 * i][...] = g
            outs[4 * i + 1][...] = -ADAM_LR * ((m_new * c1) / (jnp.sqrt(v_new * c2) + ADAM_EPS) + ADAM_WD * w)
            outs[4 * i + 2][...] = m_new
            outs[4 * i + 3][...] = v_new

    out_shape = [jax.ShapeDtypeStruct(w.shape, F32) for w in ws for _ in range(4)]
    return pl.pallas_call(body, name="adamw_small", out_shape=out_shape,
                          compiler_params=pltpu.CompilerParams(vmem_limit_bytes=V7X_VMEM_LIMIT))(*ws, *ms, *vs, *gs)


def _sum_parts(parts, loss_rows):
    P, R, C = parts.shape
    lo, hi = loss_rows

    def body(p_ref, o_ref, loss_ref):
        t = p_ref[0]
        for p in range(1, P):
            t = t + p_ref[p]
        o_ref[...] = t
        tot = jnp.sum(jnp.sum(o_ref[lo:hi, :], axis=1, keepdims=True), axis=0, keepdims=True)
        loss_ref[...] = jnp.broadcast_to(tot, loss_ref.shape)

    return pl.pallas_call(body, name="sum_small_grads",
                          out_shape=[jax.ShapeDtypeStruct((R, C), F32), jax.ShapeDtypeStruct((1, LANES), F32)],
                          compiler_params=pltpu.CompilerParams(vmem_limit_bytes=V7X_VMEM_LIMIT))(parts)


def _exchange(items, name):
    n = len(items)
    MESH = pl.DeviceIdType.MESH

    def body(*refs):
        src, dst = refs[:n], refs[n:2 * n]
        send_sems, recv_sems, local_sems = refs[2 * n:]
        x, y, c = lax.axis_index("x"), lax.axis_index("y"), lax.axis_index("c")
        me = 4 * x + 2 * y + c
        started = []
        for it, (_, per_peer) in enumerate(items):
            own = pltpu.make_async_copy(src[it].at[me] if per_peer else src[it], dst[it].at[me], local_sems.at[it])
            own.start()
            started.append(own)
        sends, recvs = [], []
        for k in range(1, N_DEV):
            px = 1 - x if k & 4 else x
            py = 1 - y if k & 2 else y
            pc = 1 - c if k & 1 else c
            peer = 4 * px + 2 * py + pc
            for it, (_, per_peer) in enumerate(items):
                s = src[it].at[peer] if per_peer else src[it]
                cp = pltpu.make_async_remote_copy(src_ref=s, dst_ref=dst[it].at[me], send_sem=send_sems.at[it, k - 1],
                                                  recv_sem=recv_sems.at[it, k - 1], device_id=(px, py, pc),
                                                  device_id_type=MESH)
                cp.start()
                sends.append(cp)
                recvs.append(pltpu.make_async_remote_copy(
                    src_ref=s, dst_ref=dst[it].at[peer], send_sem=send_sems.at[it, k - 1],
                    recv_sem=recv_sems.at[it, k - 1], device_id=(px, py, pc), device_id_type=MESH))
        for cp in recvs:
            cp.wait_recv()
        for cp in sends:
            cp.wait_send()
        for cp in started:
            cp.wait()

    any_spec = pl.BlockSpec(memory_space=pl.ANY)
    out_shape = []
    for a, per_peer in items:
        shp = a.shape if per_peer else (N_DEV,) + a.shape
        out_shape.append(jax.ShapeDtypeStruct(shp, a.dtype))
    return pl.pallas_call(
        body, name=name, in_specs=[any_spec] * n, out_specs=[any_spec] * n, out_shape=out_shape,
        scratch_shapes=[pltpu.SemaphoreType.DMA((n, N_DEV - 1)), pltpu.SemaphoreType.DMA((n, N_DEV - 1)),
                        pltpu.SemaphoreType.DMA((n,))],
    )(*[a for a, _ in items])


def _remote(src, dst, send_sem, recv_sem, device):
    return pltpu.make_async_remote_copy(src_ref=src, dst_ref=dst, send_sem=send_sem, recv_sem=recv_sem,
                                        device_id=device, device_id_type=pl.DeviceIdType.MESH)


def _mesh_place():
    x, y, c = lax.axis_index("x"), lax.axis_index("y"), lax.axis_index("c")
    other_chips = [(1 - x, y), (x, 1 - y), (1 - x, 1 - y)]
    return x, y, c, (x, y, 1 - c), other_chips


def _gather_all(items, name):
    n = len(items)

    def body(*refs):
        src, dst = refs[:n], refs[n:2 * n]
        send_sems, recv_sems, local_sems = refs[2 * n:]
        x, y, c, sibling, chips = _mesh_place()
        slot = lambda px, py, pc: 4 * px + 2 * py + pc
        me = slot(x, y, c)
        own = [pltpu.make_async_copy(src[it], dst[it].at[me], local_sems.at[it]) for it in range(n)]
        first = []
        for it in range(n):
            first.append(_remote(src[it], dst[it].at[me], send_sems.at[it, 0], recv_sems.at[it, 0], sibling))
            for j, chip in enumerate(chips):
                first.append(_remote(src[it], dst[it].at[me], send_sems.at[it, 1 + j], recv_sems.at[it, 1 + j],
                                     (*chip, c)))
        for cp in own + first:
            cp.start()
        passed = []
        for j, chip in enumerate(chips):
            blk = slot(*chip, c)
            for it in range(n):
                _remote(src[it], dst[it].at[blk], send_sems.at[it, 1 + j], recv_sems.at[it, 1 + j],
                        (*chip, c)).wait_recv()
                fwd = _remote(dst[it].at[blk], dst[it].at[blk], send_sems.at[it, 4 + j], recv_sems.at[it, 4 + j],
                              sibling)
                fwd.start()
                passed.append(fwd)
        for it in range(n):
            _remote(src[it], dst[it].at[slot(x, y, 1 - c)], send_sems.at[it, 0], recv_sems.at[it, 0],
                    sibling).wait_recv()
        for j, chip in enumerate(chips):
            for it in range(n):
                _remote(src[it], dst[it].at[slot(*chip, 1 - c)], send_sems.at[it, 4 + j], recv_sems.at[it, 4 + j],
                        sibling).wait_recv()
        for cp in first + passed:
            cp.wait_send()
        for cp in own:
            cp.wait()

    any_spec = pl.BlockSpec(memory_space=pl.ANY)
    return pl.pallas_call(
        body, name=name, in_specs=[any_spec] * n, out_specs=[any_spec] * n,
        out_shape=[jax.ShapeDtypeStruct((N_DEV,) + a.shape, a.dtype) for a in items],
        scratch_shapes=[pltpu.SemaphoreType.DMA((n, 7)), pltpu.SemaphoreType.DMA((n, 7)),
                        pltpu.SemaphoreType.DMA((n,))],
    )(*items)


def _peers():
    x, y, c = lax.axis_index("x"), lax.axis_index("y"), lax.axis_index("c")
    out = []
    for k in range(1, N_DEV):
        px = 1 - x if k & 4 else x
        py = 1 - y if k & 2 else y
        pc = 1 - c if k & 1 else c
        out.append((k, (px, py, pc), 4 * px + 2 * py + pc))
    return 4 * x + 2 * y + c, out


def _exchange_start(items, name, gather, carry=()):
    n, m = len(items), len(carry)

    def body(*refs):
        src, land = refs[:n], refs[n:2 * n]
        first_out = 2 * n + m
        send_sems, recv_sems = refs[first_out:first_out + n], refs[first_out + n:first_out + 2 * n]
        token = refs[-1]
        me, peers = _peers()
        for k, peer, slot in peers:
            for it in range(n):
                _remote(src[it] if gather else src[it].at[slot], land[it].at[me], send_sems[it], recv_sems[it],
                        peer).start()
        token[...] = jnp.zeros_like(token)

    hbm = pl.BlockSpec(memory_space=pltpu.HBM)
    sem = pl.BlockSpec(memory_space=pltpu.SEMAPHORE)
    land_shapes = [(N_DEV,) + (a.shape if gather else a.shape[1:]) for a in items]
    lands = [lax.empty(shp, a.dtype) for shp, a in zip(land_shapes, items)]
    through = list(items) + lands + list(carry)
    outs = pl.pallas_call(
        body, name=name,
        out_shape=(*[pltpu.SemaphoreType.DMA(())] * (2 * n), *[pltpu.HBM(a.shape, a.dtype) for a in through],
                   jax.ShapeDtypeStruct((8, LANES), F32)),
        in_specs=[hbm] * len(through),
        out_specs=(*[sem] * (2 * n), *[hbm] * len(through), pl.BlockSpec(memory_space=pltpu.VMEM)),
        input_output_aliases={i: 2 * n + i for i in range(len(through))},
        compiler_params=pltpu.CompilerParams(has_side_effects=pltpu.SideEffectType.DATAFLOW_SIDE_EFFECTING),
    )(*[pltpu.with_memory_space_constraint(a, pltpu.HBM) for a in through])
    return (list(outs[:n]), list(outs[n:2 * n]), list(outs[2 * n:3 * n]), list(outs[3 * n:4 * n]), outs[-1],
            list(outs[4 * n:4 * n + m]))


def _exchange_wait(send_sems, recv_sems, items, lands, after, name):
    n = len(items)

    def body(*refs):
        land = refs[n:2 * n]
        send_sems, recv_sems = refs[2 * n:3 * n], refs[3 * n:4 * n]
        me, peers = _peers()
        for it in range(n):
            seven = land[it].at[pl.ds(0, N_DEV - 1)]
            cp = _remote(seven, seven, send_sems[it], recv_sems[it], peers[0][1])
            cp.wait_send()
            cp.wait_recv()

    hbm = pl.BlockSpec(memory_space=pltpu.HBM)
    sem = pl.BlockSpec(memory_space=pltpu.SEMAPHORE)
    outs = pl.pallas_call(
        body, name=name,
        out_shape=tuple(pltpu.HBM(a.shape, a.dtype) for a in list(items) + list(lands)),
        in_specs=[hbm] * (2 * n) + [sem] * (2 * n) + [pl.BlockSpec(memory_space=pl.ANY)],
        out_specs=tuple([hbm] * (2 * n)),
        input_output_aliases={i: i for i in range(2 * n)},
        compiler_params=pltpu.CompilerParams(has_side_effects=pltpu.SideEffectType.DATAFLOW_SIDE_EFFECTING),
    )(*items, *lands, *send_sems, *recv_sems, after)
    return list(outs[:n]), list(outs[n:])


def _gelu_tanh(y):
    k = math.sqrt(2.0 / math.pi)
    t = jnp.tanh(k * (y + 0.044715 * y * y * y))
    return 0.5 * y * (1.0 + t), t


def _local_step(x, mod, target, W, late_weights, P, send_early):
    B, S, D = x.shape
    T = B * S
    TS = 512
    flat = lambda a: a.reshape(T, a.shape[-1])
    unflat = lambda a: a.reshape(B, S, a.shape[-1])
    mod_col = lambda i: (mod, D, i)

    def f_modnorm_project(xv, sc, sh, g, wq, wu, wg):
        u = ((xv * _rms_scale(xv) * g) * (1.0 + sc) + sh).astype(BF16)
        return (u, lax.dot_general(u, wq, _NT, preferred_element_type=F32),
                lax.dot_general(u, wu, _NT, preferred_element_type=F32),
                lax.dot_general(u, wg, _NT, preferred_element_type=F32))

    u1, qkv, us, gates = _rowwise(
        f_modnorm_project, [(x, D, 0)], [mod_col(1), mod_col(0)], [P["g_mix"], W["w_qkv"], W["w_us"], W["w_gates"]],
        [(D, BF16), (3 * ATT_WIDTH, F32), (SSM_WIDTH, F32), (2 * D, BF16)], [], [], ts=TS, name="modnorm_project_in")
    u1f = flat(u1)

    o_att, lse = _attention_fwd(qkv, P["slopes"])
    more_w, more_p = late_weights(o_att)
    W, P = {**W, **more_w}, {**P, **more_p}

    xs, y_mm = _scan_fwd(us, P["bb_big"], P["a_row"], P["cc_big"])

    bga, bgs = P["b_gate"][:, :D], P["b_gate"][:, D:]

    def f_mixer_tail(ov, ymm, usv, ga, gs, xv, gt, sc, sh, w_att, w_ssm, w_o, bga_, bgs_, g, dsk, wg, bg):
        yv = ymm + dsk * usv
        ge, _ = _gelu_tanh(yv)
        zv = (ge * _sigmoid(jnp.dot(ge.astype(BF16), wg, preferred_element_type=F32) + bg)).astype(BF16)
        ya = jnp.dot(ov, w_att, preferred_element_type=F32)
        ys = jnp.dot(zv, w_ssm, preferred_element_type=F32)
        mg = (_sigmoid(ga + bga_) * ya + _sigmoid(gs + bgs_) * ys).astype(BF16)
        mx = jnp.dot(mg, w_o, preferred_element_type=F32)
        h = xv + gt * mx
        return yv, zv, ya, ys, mg, mx, h, (h * _rms_scale(h) * g) * (1.0 + sc) + sh

    y_s5, z, y_att, y_ssm, merged, mix, h1, u2 = _rowwise(
        f_mixer_tail,
        [(o_att, ATT_WIDTH, 0), (y_mm, SSM_WIDTH, 0), (us, SSM_WIDTH, 0), (gates, D, 0), (gates, D, 1), (x, D, 0)],
        [mod_col(2), mod_col(4), mod_col(3)],
        [W["w_proj_att"], W["w_proj_ssm"], W["w_out"], bga, bgs, P["g_ffn"], P["d_skip"], W["w_glu"], P["b_glu"]],
        [(SSM_WIDTH, F32), (SSM_WIDTH, BF16), (D, BF16), (D, BF16), (D, BF16), (D, BF16), (D, F32), (D, BF16)],
        [], [], ts=TS, raw=(0,), name="mixer_tail")

    up = unflat(_up_fwd(flat(u2), W["w_up"], name="ffn_up"))
    ff = _conv_fwd(up, P["w_conv"], P["b_conv"])

    def f_head(ffv, h1v, tg, gt, g, w_dn):
        dn = jnp.dot(ffv, w_dn, preferred_element_type=F32)
        h2 = h1v + gt * dn
        r = _rms_scale(h2)
        nh = h2 * r
        e = nh * g - tg
        dy = e * (1.0 / D)
        gy = dy * g
        dh = r * (gy - nh * jnp.mean(gy * nh, axis=-1, keepdims=True))
        return (dh, dh * gt, _col_sum(dh * dn), _col_sum(dy * nh), _col_sum(e * e) * (0.5 / D))

    dh2, d_down, d_gt2, d_g_final, loss_cols = _rowwise(
        f_head, [(ff, D_FF, 0), (h1, D, 0), (target, D, 0)], [mod_col(5)], [P["g_final"], W["w_down"]],
        [(D, BF16), (D, BF16)], [D], [(1, D), (1, D)], ts=TS, raw=(0,), name="ffn_down_head_loss")

    d_downf = flat(d_down)
    d_ff = unflat(_matmul(d_downf, W["w_down"], tb=True, out_dtype=BF16, name="ffn_down_dx"))
    d_w_down = _matmul(flat(ff), d_downf, ta=True, out_dtype=BF16, name="ffn_down_dw")
    d_up, d_w_conv, d_b_conv = _conv_bwd(up, d_ff, P["w_conv"], P["b_conv"])
    d_upf = flat(d_up)
    d_u2 = unflat(_up_dx(d_upf, W["w_up"], name="ffn_up_dx"))
    d_w_up = _up_dw(flat(u2), d_upf, name="ffn_up_dw")
    token, _ = send_early(dict(w_down=d_w_down.reshape(N_DEV, D_FF // N_DEV, D), w_up=d_w_up))
    g_ffn_after = P["g_ffn"] + token[0:1, 0:1]

    def f_modnorm_bwd(du, h, dres, mx, sc, gt, g):
        r = _rms_scale(h)
        nh = h * r
        dn = du * (1.0 + sc)
        gy = dn * g
        dh = dres + r * (gy - nh * jnp.mean(gy * nh, axis=-1, keepdims=True))
        return (dh, dh * gt, _col_sum(du), _col_sum(du * nh * g), _col_sum(dh * mx), _col_sum(dn * nh))

    dh1, d_mix, d_sh2, d_sc2, d_gt1, d_g_ffn = _rowwise(
        f_modnorm_bwd, [(d_u2, D, 0), (h1, D, 0), (dh2, D, 0), (mix, D, 0)], [mod_col(4), mod_col(2)], [g_ffn_after],
        [(D, BF16), (D, BF16)], [D, D, D], [(1, D)], ts=TS, name="modnorm_ffn_bwd")

    d_mixf = flat(d_mix)
    d_w_out = _matmul(flat(merged), d_mixf, ta=True, out_dtype=BF16, name="proj_out_dw")

    def f_mixer_tail_bwd(dmx, ga, gs, ya, ys, w_o, w_att, w_ssm, bga_, bgs_):
        dm = lax.dot_general(dmx, w_o, _NT, preferred_element_type=F32)
        sa, ss = _sigmoid(ga + bga_), _sigmoid(gs + bgs_)
        dga = dm * ya * sa * (1.0 - sa)
        dgs = dm * ys * ss * (1.0 - ss)
        dya, dys = (dm * sa).astype(BF16), (dm * ss).astype(BF16)
        return (dya, dys, jnp.concatenate([dga, dgs], axis=1),
                lax.dot_general(dya, w_att, _NT, preferred_element_type=F32),
                lax.dot_general(dys, w_ssm, _NT, preferred_element_type=F32), _col_sum(dga), _col_sum(dgs))

    d_y_att, d_y_ssm, d_gates, d_o_att, d_z, d_bga, d_bgs = _rowwise(
        f_mixer_tail_bwd, [(d_mix, D, 0), (gates, D, 0), (gates, D, 1), (y_att, D, 0), (y_ssm, D, 0)], [],
        [W["w_out"], W["w_proj_att"], W["w_proj_ssm"], bga, bgs],
        [(D, BF16), (D, BF16), (2 * D, BF16), (ATT_WIDTH, F32), (SSM_WIDTH, BF16)], [], [(1, D), (1, D)], ts=TS,
        raw=(0,), name="mixer_tail_bwd")

    d_yaf, d_ysf = flat(d_y_att), flat(d_y_ssm)
    d_w_proj_att = _matmul(flat(o_att), d_yaf, ta=True, out_dtype=BF16, name="proj_att_dw")
    d_w_proj_ssm = _matmul(flat(z), d_ysf, ta=True, out_dtype=BF16, name="proj_ssm_dw")

    def f_glu_bwd(yv, dz, usv, dsk, wg, bg):
        ge, t = _gelu_tanh(yv)
        pre = jnp.dot(ge.astype(BF16), wg, preferred_element_type=F32) + bg
        sg = _sigmoid(pre)
        dpre = dz * ge * sg * (1.0 - sg)
        dge = dz * sg + lax.dot_general(dpre.astype(BF16), wg, _NT, preferred_element_type=F32)
        k = math.sqrt(2.0 / math.pi)
        dgelu = 0.5 * (1.0 + t) + 0.5 * yv * (1.0 - t * t) * k * (1.0 + 3.0 * 0.044715 * yv * yv)
        dy = dge * dgelu
        dwg = lax.dot_general(ge.astype(BF16), dpre.astype(BF16), _TN, preferred_element_type=F32)
        return dy, dy * dsk, dwg, _col_sum(dpre), _col_sum(dy * usv)

    d_y_s5, d_us_skip, d_w_glu, d_b_glu, d_d_skip = _rowwise(
        f_glu_bwd, [(y_s5, SSM_WIDTH, 0), (d_z, SSM_WIDTH, 0), (us, SSM_WIDTH, 0)], [],
        [P["d_skip"], W["w_glu"], P["b_glu"]],
        [(SSM_WIDTH, BF16), (SSM_WIDTH, F32)], [], [(SSM_WIDTH, SSM_WIDTH), (1, SSM_WIDTH), (1, SSM_WIDTH)],
        ts=TS, name="s5_glu_bwd")
    d_us_parts, g_ab, d_bb, d_cc = _scan_bwd(d_y_s5, us, P["bb_big"], P["cc_big"], xs, P["a_row"])

    token, _ = send_early(dict(
        w_out=d_w_out.reshape(N_DEV, D // N_DEV, D), w_proj_att=_cols_to_slots(d_w_proj_att),
        w_proj_ssm=_cols_to_slots(d_w_proj_ssm),
        w_glu=d_w_glu.astype(BF16).reshape(N_DEV, SSM_WIDTH // N_DEV, SSM_WIDTH),
        w_conv=_cols_to_slots(d_w_conv.astype(BF16))))
    d_qkv = _attention_bwd(qkv, o_att, d_o_att, lse, P["slopes"] + token[0, 0])

    def f_add(*parts):
        return sum(parts[1:], parts[0])

    n_parts = d_us_parts.shape[0]
    stacked = d_us_parts.reshape(n_parts * B, S, SSM_WIDTH)
    (d_us,) = _rowwise(f_add, [(d_us_skip, SSM_WIDTH, 0)] + [(stacked, SSM_WIDTH, 0, j * B) for j in range(n_parts)],
                       [], [],
                       [(SSM_WIDTH, BF16)], [], [], ts=TS, name="s5_input_grad")
    d_qkvf = flat(d_qkv)
    d_usf = flat(d_us)
    d_gatesf = flat(d_gates)
    d_w_in_t = jnp.concatenate(
        [_unpair_qkv_rows(_matmul(d_qkvf, u1f, ta=True, out_dtype=BF16, name="proj_qkv_dw")),
         _matmul(d_usf, u1f, ta=True, out_dtype=BF16, name="proj_ssm_in_dw"),
         _matmul(d_gatesf, u1f, ta=True, out_dtype=BF16, name="proj_gates_dw")], axis=0)
    token, (w_qkv, w_us, w_gates) = send_early(dict(w_in=d_w_in_t.reshape(N_DEV, -1, D)),
                                               carry=[W["w_qkv"], W["w_us"], W["w_gates"]])
    def f_project_back_modnorm(dq, du_, dg, h, dres, sc, g, wq, wu, wg):
        du = (jnp.dot(dq, wq, preferred_element_type=F32) + jnp.dot(du_, wu, preferred_element_type=F32)
              + jnp.dot(dg, wg, preferred_element_type=F32))
        r = _rms_scale(h)
        nh = h * r
        dn = du * (1.0 + sc)
        gy = dn * g
        dh = dres + r * (gy - nh * jnp.mean(gy * nh, axis=-1, keepdims=True))
        return (dh, _col_sum(du), _col_sum(du * nh * g), _col_sum(dn * nh))

    grad_x, d_sh1, d_sc1, d_g_mix = _rowwise(
        f_project_back_modnorm,
        [(d_qkv, 3 * ATT_WIDTH, 0), (d_us, SSM_WIDTH, 0), (d_gates, 2 * D, 0), (x, D, 0), (dh1, D, 0)], [mod_col(1)],
        [P["g_mix"] + token[0:1, 0:1], w_qkv, w_us, w_gates],
        [(D, F32)], [D, D], [(1, D)], ts=TS, raw=(0, 1, 2), name="project_in_back_modnorm")

    d_mod = jnp.concatenate([d_sh1, d_sc1, d_gt1, d_sh2, d_sc2, d_gt2], axis=-1)
    g_ab_re, g_ab_im = _deinterleave(g_ab)
    d_bb_re, d_bb_im = _deinterleave(d_bb)
    d_cc_re, d_cc_im = (t.T for t in _deinterleave(d_cc.T))
    small = dict(g_mix=d_g_mix, b_gate=jnp.concatenate([d_bga, d_bgs], axis=1), g_ab_re=g_ab_re, g_ab_im=g_ab_im,
                 d_bb_re=d_bb_re, d_bb_im=d_bb_im, d_cc_re=d_cc_re, d_cc_im=d_cc_im, d_skip=d_d_skip,
                 b_glu=d_b_glu, g_ffn=d_g_ffn, b_conv=d_b_conv, g_final=d_g_final, loss_cols=loss_cols)
    return grad_x, d_mod, small


def _block_diag_in(bb):
    t = bb.reshape(SSM_GROUPS, SSM_STATE, SSM_GROUP_CH)
    eye = jnp.eye(SSM_GROUPS, dtype=bb.dtype)
    return jnp.einsum("gnc,gh->gchn", t, eye).reshape(SSM_WIDTH, SSM_COLS)


def _block_diag_out(cm):
    eye = jnp.eye(SSM_GROUPS, dtype=cm.dtype)
    return jnp.einsum("gcn,gh->gnhc", cm, eye).reshape(SSM_COLS, SSM_WIDTH)


def _diag_blocks_in(m):
    t = m.reshape(SSM_GROUPS, SSM_GROUP_CH, SSM_GROUPS, SSM_STATE)
    idx = jnp.arange(SSM_GROUPS)
    return t[idx, :, idx, :].transpose(0, 2, 1).reshape(SSM_COLS, SSM_GROUP_CH)


def _diag_blocks_out(m):
    t = m.reshape(SSM_GROUPS, SSM_STATE, SSM_GROUPS, SSM_GROUP_CH)
    idx = jnp.arange(SSM_GROUPS)
    return t[idx, :, idx, :].transpose(0, 2, 1)


def _pair_qkv_rows(w):
    return w.reshape(3, N_HEADS // 2, LANES, w.shape[1]).swapaxes(0, 1).reshape(w.shape)


def _unpair_qkv_rows(w):
    return w.reshape(N_HEADS // 2, 3, LANES, w.shape[1]).swapaxes(0, 1).reshape(w.shape)


def _interleave(re, im):
    lead = re.shape[:-1]
    g = lambda a: a.reshape(lead + (SSM_COLS // SCAN_COLS, 1, SCAN_COLS))
    return jnp.concatenate([g(re), g(im)], axis=-2).reshape(lead + (2 * SSM_COLS,))


def _deinterleave(x):
    lead = x.shape[:-1]
    t = x.reshape(lead + (SSM_COLS // SCAN_COLS, 2, SCAN_COLS))
    return t[..., 0, :].reshape(lead + (SSM_COLS,)), t[..., 1, :].reshape(lead + (SSM_COLS,))


def _cols_to_slots(g):
    R = g.shape[0]
    return g.reshape(R, N_DEV, g.shape[1] // N_DEV).transpose(1, 0, 2)


def _slots_to_cols(g):
    return g.transpose(1, 0, 2).reshape(g.shape[1], N_DEV * g.shape[2])


SMALL_ORDER = ("b_ada", "g_mix", "b_gate", "a_re", "a_im", "log_dt", "b_re", "b_im", "c_re", "c_im", "d_skip",
               "b_glu", "g_ffn", "b_conv", "g_final")


def _pack(arrs):
    pieces, offs, row = [], [], 0
    for a in arrs:
        f = a.reshape(-1).astype(F32)
        n = f.shape[0]
        rows = -(-n // LANES)
        pieces.append(jnp.pad(f, (0, rows * LANES - n)))
        offs.append((row, n))
        row += rows
    return jnp.concatenate(pieces).reshape(row, LANES), offs


def _unpack(packed, offs, shapes):
    flat = packed.reshape(-1)
    return [flat[r * LANES:r * LANES + n].reshape(s) for (r, n), s in zip(offs, shapes)]


def kernel(x, c, w_ada, b_ada, g_mix, w_in, b_gate, a_re, a_im, log_dt, b_re, b_im, c_re, c_im, d_skip, w_glu, b_glu, w_proj_att, w_proj_ssm, w_out, g_ffn, w_up, w_conv, b_conv, w_down, g_final, loss_target, m_w_ada, m_b_ada, m_g_mix, m_w_in, m_b_gate, m_a_re, m_a_im, m_log_dt, m_b_re, m_b_im, m_c_re, m_c_im, m_d_skip, m_w_glu, m_b_glu, m_w_proj_att, m_w_proj_ssm, m_w_out, m_g_ffn, m_w_up, m_w_conv, m_b_conv, m_w_down, m_g_final, v_w_ada, v_b_ada, v_g_mix, v_w_in, v_b_gate, v_a_re, v_a_im, v_log_dt, v_b_re, v_b_im, v_c_re, v_c_im, v_d_skip, v_w_glu, v_b_glu, v_w_proj_att, v_w_proj_ssm, v_w_out, v_g_ffn, v_w_up, v_w_conv, v_b_conv, v_w_down, v_g_final):
    args = dict(locals())
    B, S, D = x.shape
    me = 4 * lax.axis_index("x") + 2 * lax.axis_index("y") + lax.axis_index("c")
    bf = lambda w: w[0].astype(BF16)

    c_slots, w_in_slots = _gather_all([c, w_in[0].T.astype(BF16)], name="gather_first_weights")
    c_all = c_slots.reshape(N_DEV * B, D)
    w_in_t = w_in_slots.reshape(-1, D)
    n_qkv = 3 * ATT_WIDTH
    W = dict(w_qkv=_pair_qkv_rows(w_in_t[:n_qkv]), w_us=w_in_t[n_qkv:n_qkv + SSM_WIDTH],
             w_gates=w_in_t[n_qkv + SSM_WIDTH:])

    n_ada = w_ada.shape[2]
    b_ada_cols = lax.dynamic_slice(b_ada, (0, me * n_ada), (1, n_ada))
    mod_part = _ada_fwd(c_all, w_ada[0], b_ada_cols)
    (mod_slots,) = _exchange([(mod_part.reshape(N_DEV, B, n_ada), True)], name="scatter_modulation")
    mod = mod_slots.transpose(1, 0, 2).reshape(B, 1, 6 * D)

    later = [bf(w_glu), bf(w_proj_att), bf(w_proj_ssm), bf(w_out), bf(w_up), w_conv[0], bf(w_down)]
    later_sems = _exchange_start(later, "start_later_weights", gather=True, carry=[mod])
    (mod,) = later_sems[5]

    def late_weights(after):
        _, lands = _exchange_wait(*later_sems[:4], after, name="wait_later_weights")
        g = [lax.dynamic_update_index_in_dim(land, a, me, 0) for land, a in zip(lands, later)]
        more_w = dict(w_glu=g[0].reshape(SSM_WIDTH, SSM_WIDTH), w_proj_att=_slots_to_cols(g[1]),
                      w_proj_ssm=_slots_to_cols(g[2]), w_out=g[3].reshape(D, D), w_up=g[4],
                      w_down=g[6].reshape(D_FF, D))
        return more_w, dict(w_conv=_slots_to_cols(g[5]))

    ab_re, ab_im, f_re, f_im = _s5_params(a_re[0], a_im[0], log_dt[0].reshape(SSM_GROUPS, 1))
    col = lambda a: a.reshape(SSM_COLS, 1)
    b_re2, b_im2 = b_re[0].reshape(SSM_COLS, SSM_GROUP_CH), b_im[0].reshape(SSM_COLS, SSM_GROUP_CH)
    bb_re, bb_im = _s5_input_matrix(col(f_re), col(f_im), b_re2, b_im2)
    slopes = jnp.asarray([2.0 ** (-8.0 * (h + 1) / N_HEADS) for h in range(N_HEADS)], F32)
    P = dict(g_mix=g_mix, g_ffn=g_ffn, g_final=g_final.reshape(1, D), b_gate=b_gate, d_skip=d_skip, b_glu=b_glu,
             b_conv=b_conv, slopes=slopes,
             a_row=_interleave(ab_re.reshape(1, SSM_COLS), ab_im.reshape(1, SSM_COLS)),
             bb_big=_interleave(_block_diag_in(bb_re), _block_diag_in(bb_im)),
             cc_big=_interleave(_block_diag_out(c_re[0]).T, -_block_diag_out(c_im[0]).T).T)

    in_flight = []

    def send_early(grads, carry=()):
        names = list(grads)
        handles = _exchange_start([grads[n] for n in names], "start_gradients_%d" % len(in_flight), gather=False,
                                  carry=carry)
        in_flight.append((names,) + handles[:4])
        return handles[4], handles[5]

    grad_x, d_mod, small = _local_step(x, mod, loss_target, W, late_weights, P, send_early)

    small_list = [small["loss_cols"], small["g_mix"], small["b_gate"], small["g_ab_re"], small["g_ab_im"],
                  _diag_blocks_in(small["d_bb_re"]), _diag_blocks_in(small["d_bb_im"]),
                  _diag_blocks_out(small["d_cc_re"]), -_diag_blocks_out(small["d_cc_im"]),
                  small["g_ffn"], small["b_conv"], small["g_final"], small["d_skip"], small["b_glu"]]
    small_packed, small_offs = _pack(small_list)
    small_all, dmod_slots = _gather_all([small_packed, d_mod.reshape(B, 6 * D)], name="gather_small_gradients")

    out = {}

    def update(name, parts, own=None):
        view = (lambda a: a[0].T) if name == "w_in" else (lambda a: a[0])
        back = (lambda a: a.T[None]) if name == "w_in" else (lambda a: a[None])
        g, dl, mn, vn = _adamw(view(args[name]), view(args["m_" + name]), view(args["v_" + name]), parts,
                               name="adamw_" + name, own=own)
        for key, val in (("grad_", g), ("delta_", dl), ("new_m_", mn), ("new_v_", vn)):
            out[key + name] = back(val)

    my_slot = me.astype(jnp.int32).reshape(1)
    for i, (names, send_sems, recv_sems, sent, lands) in enumerate(in_flight):
        sent, lands = _exchange_wait(send_sems, recv_sems, sent, lands, dmod_slots, name="wait_gradients_%d" % i)
        for name, own_slots, landed in zip(names, sent, lands):
            update(name, landed, own=(own_slots, my_slot))

    dmod_all = dmod_slots.reshape(N_DEV * B, 6 * D)
    dmod_cols = lax.dynamic_slice(dmod_all, (0, me * n_ada), (N_DEV * B, n_ada))
    d_w_ada, d_b_ada = _ada_bwd(c_all, dmod_all, dmod_cols)
    update("w_ada", d_w_ada[None])

    loss_row, loss_n = small_offs[0]
    small_sum, loss_vec = _sum_parts(small_all, (loss_row, loss_row + loss_n // LANES))
    shapes = [(1, D), (1, D), (1, 2 * D), (SSM_GROUPS, SSM_STATE), (SSM_GROUPS, SSM_STATE), (SSM_COLS, SSM_GROUP_CH),
              (SSM_COLS, SSM_GROUP_CH), (1, SSM_GROUPS, SSM_GROUP_CH, SSM_STATE),
              (1, SSM_GROUPS, SSM_GROUP_CH, SSM_STATE), (1, D), (1, D_FF), (D,), (1, SSM_WIDTH), (1, SSM_WIDTH)]
    (_, s_g_mix, s_b_gate, s_ab_re, s_ab_im, s_bb_re, s_bb_im, s_c_re, s_c_im, s_g_ffn, s_b_conv, s_g_final,
     s_d_skip, s_b_glu) = _unpack(small_sum, small_offs, shapes)
    d_b_re2, d_b_im2, d_f_re, d_f_im = _s5_input_matrix_bwd(col(f_re), col(f_im), b_re2, b_im2, s_bb_re, s_bb_im)
    d_a_re, d_a_im, d_log_dt = _s5_params_bwd(a_re[0], a_im[0], log_dt[0].reshape(SSM_GROUPS, 1), s_ab_re, s_ab_im,
                                              d_f_re.reshape(SSM_GROUPS, SSM_STATE),
                                              d_f_im.reshape(SSM_GROUPS, SSM_STATE))
    grads_small = dict(b_ada=d_b_ada, g_mix=s_g_mix, b_gate=s_b_gate, a_re=d_a_re[None], a_im=d_a_im[None],
                       log_dt=d_log_dt.reshape(1, SSM_GROUPS), b_re=d_b_re2.reshape(b_re.shape),
                       b_im=d_b_im2.reshape(b_im.shape), c_re=s_c_re, c_im=s_c_im, d_skip=s_d_skip, b_glu=s_b_glu,
                       g_ffn=s_g_ffn, b_conv=s_b_conv, g_final=s_g_final)
    flat2 = lambda a: a.reshape(-1, a.shape[-1])
    res = _adamw_small([flat2(args[n]) for n in SMALL_ORDER], [flat2(args["m_" + n]) for n in SMALL_ORDER],
                       [flat2(args["v_" + n]) for n in SMALL_ORDER],
                       [flat2(grads_small[n].reshape(args[n].shape)) for n in SMALL_ORDER])
    for i, n in enumerate(SMALL_ORDER):
        for k, key in enumerate(("grad_", "delta_", "new_m_", "new_v_")):
            out[key + n] = res[4 * i + k].reshape(args[n].shape)

    order = ["w_ada", "b_ada", "g_mix", "w_in", "b_gate", "a_re", "a_im", "log_dt", "b_re", "b_im", "c_re", "c_im",
             "d_skip", "w_glu", "b_glu", "w_proj_att", "w_proj_ssm", "w_out", "g_ffn", "w_up", "w_conv", "b_conv",
             "w_down", "g_final"]
    loss = loss_vec[0, 0]
    return (loss, grad_x, *[out[k + n] for k in ("grad_", "delta_", "new_m_", "new_v_") for n in order])
```

```python
import math

import jax
import jax.numpy as jnp
from jax import lax
from jax.experimental import pallas as pl
from jax.experimental.pallas import tpu as pltpu

F32 = jnp.float32
BF16 = jnp.bfloat16

N_DEV = 8
D_MODEL = 1024
N_HEADS = 8
HEAD_DIM = 64
ATT_WIDTH = N_HEADS * HEAD_DIM
DILATIONS = (1, 4, 16)
WIN = 128
SSM_GROUPS = 16
SSM_GROUP_CH = 16
SSM_WIDTH = SSM_GROUPS * SSM_GROUP_CH
SSM_STATE = 64
SSM_COLS = SSM_GROUPS * SSM_STATE
D_FF = 2048
EPS = 1e-6
NEG_INF = -1e30
ADAM_LR, ADAM_B1, ADAM_B2, ADAM_EPS, ADAM_WD, ADAM_STEP = 0.001, 0.9, 0.999, 1e-08, 0.01, 10

V7X_VMEM_LIMIT = 56 * 1024 * 1024
LANES = 128


def _params(n_grid):
    return pltpu.CompilerParams(dimension_semantics=("arbitrary",) * n_grid,
                                vmem_limit_bytes=V7X_VMEM_LIMIT)


def _tile(n, pref):
    if n <= pref:
        return n
    t = (pref // LANES) * LANES
    while t > 0:
        if n % t == 0:
            return t
        t -= LANES
    return n


def _matmul(a, b, *, ta=False, tb=False, out_dtype=F32, name):
    if ta:
        K, M = a.shape
    else:
        M, K = a.shape
    if tb:
        N, K2 = b.shape
    else:
        K2, N = b.shape
    assert K == K2, (a.shape, b.shape)
    if ta:
        tm, tn, tk = _tile(M, 1024), _tile(N, 2048), _tile(K, 1024)
    else:
        tm, tk = _tile(M, 512), _tile(K, 4096)
        tn = _tile(N, 2048 if K <= 2048 else 1024)
    nk = K // tk
    dn = (((0,) if ta else (1,), (1,) if tb else (0,)), ((), ()))

    def body(a_ref, b_ref, o_ref, acc_ref):
        k = pl.program_id(2)
        part = lax.dot_general(a_ref[...].astype(BF16), b_ref[...].astype(BF16), dn, preferred_element_type=F32)
        if nk == 1:
            o_ref[...] = part.astype(o_ref.dtype)
            return

        @pl.when(k == 0)
        def _():
            acc_ref[...] = jnp.zeros_like(acc_ref)

        acc_ref[...] += part

        @pl.when(k == nk - 1)
        def _():
            o_ref[...] = acc_ref[...].astype(o_ref.dtype)

    a_spec = (pl.BlockSpec((tk, tm), lambda j, i, k: (k, i)) if ta
              else pl.BlockSpec((tm, tk), lambda j, i, k: (i, k)))
    b_spec = (pl.BlockSpec((tn, tk), lambda j, i, k: (j, k)) if tb
              else pl.BlockSpec((tk, tn), lambda j, i, k: (k, j)))
    return pl.pallas_call(
        body, name=name, grid=(N // tn, M // tm, nk),
        in_specs=[a_spec, b_spec],
        out_specs=pl.BlockSpec((tm, tn), lambda j, i, k: (i, j)),
        out_shape=jax.ShapeDtypeStruct((M, N), out_dtype),
        scratch_shapes=[pltpu.VMEM((tm, tn) if nk > 1 else (8, LANES), F32)],
        compiler_params=_params(3),
    )(a, b)


HALF = 256
UP_SLOTS = N_DEV // 2
UP_GROUP = 4 * HALF


def _group_weight(w_ref):
    return jnp.concatenate([w_ref[0, :, :HALF], w_ref[1, :, :HALF], w_ref[0, :, HALF:], w_ref[1, :, HALF:]], axis=1)


def _up_weight_spec(K, index):
    return pl.BlockSpec((2, None, K, 2 * HALF), index)


def _up_fwd(a, w3, name):
    M, K = a.shape
    tm = _tile(M, 1024)

    def body(a_ref, w_ref, o_ref):
        o_ref[...] = jnp.dot(a_ref[...].astype(BF16), _group_weight(w_ref),
                             preferred_element_type=F32).astype(o_ref.dtype)

    return pl.pallas_call(
        body, name=name, grid=(UP_SLOTS, M // tm),
        in_specs=[pl.BlockSpec((tm, K), lambda j, i: (i, 0)), _up_weight_spec(K, lambda j, i: (0, j, 0, 0))],
        out_specs=pl.BlockSpec((tm, UP_GROUP), lambda j, i: (i, j)),
        out_shape=jax.ShapeDtypeStruct((M, UP_SLOTS * UP_GROUP), BF16), compiler_params=_params(2),
    )(a, w3.reshape(2, UP_SLOTS, K, 2 * HALF))


def _up_dx(d, w3, name):
    M = d.shape[0]
    K = w3.shape[1]
    tm = _tile(M, 1024)

    def body(d_ref, w_ref, o_ref, acc_ref):
        j = pl.program_id(1)

        @pl.when(j == 0)
        def _():
            acc_ref[...] = jnp.zeros_like(acc_ref)

        acc_ref[...] += lax.dot_general(d_ref[...], _group_weight(w_ref), _NT, preferred_element_type=F32)

        @pl.when(j == UP_SLOTS - 1)
        def _():
            o_ref[...] = acc_ref[...].astype(o_ref.dtype)

    return pl.pallas_call(
        body, name=name, grid=(M // tm, UP_SLOTS),
        in_specs=[pl.BlockSpec((tm, UP_GROUP), lambda i, j: (i, j)), _up_weight_spec(K, lambda i, j: (0, j, 0, 0))],
        out_specs=pl.BlockSpec((tm, K), lambda i, j: (i, 0)),
        out_shape=jax.ShapeDtypeStruct((M, K), BF16), scratch_shapes=[pltpu.VMEM((tm, K), F32)],
        compiler_params=_params(2),
    )(d, w3.reshape(2, UP_SLOTS, K, 2 * HALF))


def _up_dw(a, d, name):
    M, K = a.shape
    tk = _tile(M, 1024)
    nk = M // tk

    def body(a_ref, d_ref, o_ref, acc_ref):
        k = pl.program_id(1)

        @pl.when(k == 0)
        def _():
            acc_ref[...] = jnp.zeros_like(acc_ref)

        acc_ref[...] += lax.dot_general(a_ref[...], d_ref[...], _TN, preferred_element_type=F32)

        @pl.when(k == nk - 1)
        def _():
            for half in range(2):
                for part in range(2):
                    lo = (2 * half + part) * HALF
                    o_ref[part, :, half * HALF:(half + 1) * HALF] = acc_ref[:, lo:lo + HALF].astype(o_ref.dtype)

    out = pl.pallas_call(
        body, name=name, grid=(UP_SLOTS, nk),
        in_specs=[pl.BlockSpec((tk, K), lambda j, k: (k, 0)), pl.BlockSpec((tk, UP_GROUP), lambda j, k: (k, j))],
        out_specs=_up_weight_spec(K, lambda j, k: (0, j, 0, 0)),
        out_shape=jax.ShapeDtypeStruct((2, UP_SLOTS, K, 2 * HALF), BF16),
        scratch_shapes=[pltpu.VMEM((K, UP_GROUP), F32)], compiler_params=_params(2),
    )(a, d)
    return out.reshape(N_DEV, K, 2 * HALF)


def _rowwise(fn, rows, bvecs, consts, out_rows, out_b, out_g, *, ts, name, raw=()):
    B, S = rows[0][0].shape[:2]
    nin = len(rows) + len(bvecs) + len(consts)
    nr, nb, ng = len(out_rows), len(out_b), len(out_g)

    def body(*refs):
        b = pl.program_id(0)
        s = pl.program_id(1)
        vals = [r[...] for r in refs[:nin]]
        vals[:len(rows)] = [v if i in raw else v.astype(F32) for i, v in enumerate(vals[:len(rows)])]
        outs = fn(*vals)
        if not isinstance(outs, (tuple, list)):
            outs = (outs,)
        orefs = refs[nin:]
        for i in range(nr):
            orefs[i][...] = outs[i].astype(orefs[i].dtype)
        for i in range(nb):
            ref = orefs[nr + i]

            @pl.when(s == 0)
            def _(ref=ref):
                ref[...] = jnp.zeros_like(ref)

            ref[...] += outs[nr + i]
        for i in range(ng):
            ref = orefs[nr + nb + i]

            @pl.when((s == 0) & (b == 0))
            def _(ref=ref):
                ref[...] = jnp.zeros_like(ref)

            ref[...] += outs[nr + nb + i]

    rows = [r if len(r) == 4 else r + (0,) for r in rows]
    in_specs = ([pl.BlockSpec((None, ts, cb), lambda b, s, ci=ci, b0=b0: (b0 + b, s, ci)) for (_, cb, ci, b0) in rows]
                + [pl.BlockSpec((None, 1, cb), lambda b, s, ci=ci: (b, 0, ci)) for (_, cb, ci) in bvecs]
                + [pl.BlockSpec(a.shape, lambda b, s: (0, 0)) for a in consts])
    out_shape = ([jax.ShapeDtypeStruct((B, S, c), dt) for (c, dt) in out_rows]
                 + [jax.ShapeDtypeStruct((B, 1, c), F32) for c in out_b]
                 + [jax.ShapeDtypeStruct(rc, F32) for rc in out_g])
    out_specs = ([pl.BlockSpec((None, ts, c), lambda b, s: (b, s, 0)) for (c, _) in out_rows]
                 + [pl.BlockSpec((None, 1, c), lambda b, s: (b, 0, 0)) for c in out_b]
                 + [pl.BlockSpec(rc, lambda b, s: (0, 0)) for rc in out_g])
    args = [r[0] for r in rows] + [a for (a, _, _) in bvecs] + list(consts)
    return pl.pallas_call(
        body, name=name, grid=(B, S // ts), in_specs=in_specs, out_specs=out_specs,
        out_shape=out_shape, compiler_params=_params(2),
    )(*args)


def _col_sum(v):
    return jnp.sum(v, axis=0, keepdims=True)


def _rms_scale(h):
    return lax.rsqrt(jnp.mean(h * h, axis=-1, keepdims=True) + EPS)


def _sigmoid(v):
    return 0.5 * (1.0 + jnp.tanh(0.5 * v))


ATT_SCALE = HEAD_DIM ** -0.5
COPY_ROWS = 256
_NT = (((1,), (1,)), ((), ()))
_TN = (((0,), (0,)), ((), ()))


def _row_chunks(d, seq):
    sub = seq // d
    out = []
    for r in range(d):
        for c0 in range(0, sub, COPY_ROWS):
            n = min(COPY_ROWS, sub - c0)
            out.append((pl.ds(r + c0 * d, n, stride=d), r * sub + c0, n))
    return out


ATT_UNROLL = 16
KEYS = 2 * WIN


def _zero_once(refs):
    @pl.when((pl.program_id(0) == 0) & (pl.program_id(1) == 0))
    def _():
        for r in refs:
            r[...] = jnp.zeros_like(r)


def _pair_bias(bias_ref, slopes_ref, hp, d, key_major):
    shape = (KEYS, WIN) if key_major else (WIN, KEYS)
    qi = lax.broadcasted_iota(jnp.int32, shape, 1 if key_major else 0)
    kj = lax.broadcasted_iota(jnp.int32, shape, 0 if key_major else 1)
    dist = WIN + qi - kj
    valid = (dist >= 0) & (dist <= WIN)
    distf = dist.astype(F32)
    for h in range(2):
        slope_d = slopes_ref[2 * hp + h] * float(d)
        with_prev = jnp.where(valid, -(slope_d * distf), NEG_INF)
        no_prev = jnp.where(kj >= WIN, with_prev, NEG_INF)
        span = slice(h * KEYS, (h + 1) * KEYS)
        if key_major:
            bias_ref[1, span, :] = with_prev
            bias_ref[0, span, :] = no_prev
        else:
            bias_ref[1, :, span] = with_prev
            bias_ref[0, :, span] = no_prev


def _stack_heads(v):
    first = lax.broadcasted_iota(jnp.int32, v.shape, 1) < HEAD_DIM
    zero = jnp.zeros_like(v)
    return jnp.concatenate([jnp.where(first, v, zero), jnp.where(first, zero, v)], axis=0)


def _per_head(c0, c1, n):
    return jnp.where(lax.broadcasted_iota(jnp.int32, (n, LANES), 1) < HEAD_DIM, c0, c1)


def _qkv_spec(seq, j):
    return pl.BlockSpec((None, seq, LANES), lambda b, hp: (b, 0, 3 * hp + j))


def _attention_fwd(qkv, slopes):
    B, S, _ = qkv.shape
    n_blk = S // WIN
    n_pair = N_HEADS // 2

    def body(slopes_ref, q_ref, k_ref, v_ref, o_ref, lse_ref, qp, kp, vp, bias, acc, mx, sm, acc_n, mx_n, sm_n):
        hp = pl.program_id(1)
        _zero_once((kp, vp))
        for p, d in enumerate(DILATIONS):
            nb = n_blk // d
            chunks = _row_chunks(d, S)
            for src, dst, n in chunks:
                qp[dst:dst + n, :] = (q_ref[src, :] * ATT_SCALE).astype(BF16)
                kp[WIN + dst:WIN + dst + n, :] = k_ref[src, :].astype(BF16)
                vp[WIN + dst:WIN + dst + n, :] = v_ref[src, :].astype(BF16)
            _pair_bias(bias, slopes_ref, hp, d, key_major=False)
            acc_t, mx_t, sm_t = (acc_n, mx_n, sm_n) if d == 1 else (acc, mx, sm)

            nk = WIN if nb == 1 else KEYS
            bias_cur = jnp.concatenate([bias[0, :, WIN:KEYS], bias[0, :, KEYS + WIN:]], axis=1) if nb == 1 else None

            def block(i, carry, p=p, nb=nb, nk=nk, bias_cur=bias_cur, acc_t=acc_t, mx_t=mx_t, sm_t=sm_t):
                cur = pl.ds(pl.multiple_of(i * WIN, WIN), WIN)
                keys = pl.ds(pl.multiple_of(i * WIN + (KEYS - nk), WIN), nk)
                s = lax.dot_general(qp[cur, :], _stack_heads(kp[keys, :]), _NT, preferred_element_type=F32)
                s = s + (bias_cur if nb == 1 else bias[((i % nb) > 0).astype(jnp.int32)])
                es, ms, ls = [], [], []
                for h in range(2):
                    sh = s[:, h * nk:(h + 1) * nk]
                    m = jnp.max(sh if nb == 1 else jnp.maximum(sh[:, :WIN], sh[:, WIN:]), axis=1, keepdims=True)
                    e = jnp.exp(sh - m)
                    es.append(e.astype(BF16))
                    ms.append(m)
                    ls.append(jnp.sum(e if nb == 1 else e[:, :WIN] + e[:, WIN:], axis=1, keepdims=True))
                acc_t[p, cur, :] = jnp.dot(jnp.concatenate(es, axis=1), _stack_heads(vp[keys, :]),
                                           preferred_element_type=F32)
                mx_t[p, cur, :] = _per_head(ms[0], ms[1], WIN)
                sm_t[p, cur, :] = _per_head(ls[0], ls[1], WIN)
                return carry

            lax.fori_loop(0, n_blk, block, 0, unroll=ATT_UNROLL)
            if d > 1:
                for src, dst, n in chunks:
                    acc_n[p, src, :] = acc[p, dst:dst + n, :]
                    mx_n[p, src, :] = mx[p, dst:dst + n, :]
                    sm_n[p, src, :] = sm[p, dst:dst + n, :]

        chunk = 256

        def merge(i, carry):
            rows = pl.ds(pl.multiple_of(i * chunk, chunk), chunk)
            ms = [mx_n[p, rows, :] for p in range(3)]
            m = jnp.maximum(jnp.maximum(ms[0], ms[1]), ms[2])
            ws = [jnp.exp(mp - m) for mp in ms]
            l = ws[0] * sm_n[0, rows, :] + ws[1] * sm_n[1, rows, :] + ws[2] * sm_n[2, rows, :]
            o = (ws[0] * acc_n[0, rows, :] + ws[1] * acc_n[1, rows, :] + ws[2] * acc_n[2, rows, :]) / l
            o_ref[rows, :] = o.astype(o_ref.dtype)
            lse = m + jnp.log(l)
            for h in range(2):
                lse_ref[rows, h:h + 1] = lse[:, h * HEAD_DIM:h * HEAD_DIM + 1]
            return carry

        lax.fori_loop(0, S // chunk, merge, 0)

    return pl.pallas_call(
        body, name="attention_fwd", grid=(B, n_pair),
        in_specs=[pl.BlockSpec(memory_space=pltpu.SMEM), _qkv_spec(S, 0), _qkv_spec(S, 1), _qkv_spec(S, 2)],
        out_specs=[pl.BlockSpec((None, S, LANES), lambda b, hp: (b, 0, hp)),
                   pl.BlockSpec((None, None, S, 2), lambda b, hp: (b, hp, 0, 0))],
        out_shape=[jax.ShapeDtypeStruct((B, S, ATT_WIDTH), BF16),
                   jax.ShapeDtypeStruct((B, n_pair, S, 2), F32)],
        scratch_shapes=[pltpu.VMEM((S, LANES), BF16), pltpu.VMEM((S + WIN, LANES), BF16),
                        pltpu.VMEM((S + WIN, LANES), BF16), pltpu.VMEM((2, WIN, 2 * KEYS), F32)]
        + [pltpu.VMEM((3, S, LANES), F32)] * 6,
        compiler_params=_params(2),
    )(slopes, qkv, qkv, qkv)


def _attention_bwd(qkv, o, do, lse, slopes):
    B, S, _ = qkv.shape
    n_blk = S // WIN
    n_pair = N_HEADS // 2

    def body(slopes_ref, q_ref, k_ref, v_ref, o_ref, do_ref, lse_ref, dx_ref,
             qp, dop, kp, vp, aux, auxp, aux_t, bias_t, dqp, dvk, dq_n, dk_n, dv_n):
        hp = pl.program_id(1)
        aux[...] = jnp.zeros_like(aux)
        for c0 in range(0, S, COPY_ROWS):
            rows = slice(c0, c0 + COPY_ROWS)
            prod = do_ref[rows, :] * o_ref[rows, :].astype(F32)
            for h in range(2):
                aux[rows, 2 * h:2 * h + 1] = lse_ref[rows, h:h + 1]
                aux[rows, 2 * h + 1:2 * h + 2] = jnp.sum(prod[:, h * HEAD_DIM:(h + 1) * HEAD_DIM], axis=1,
                                                         keepdims=True)
        dq_n[...] = jnp.zeros_like(dq_n)
        dk_n[...] = jnp.zeros_like(dk_n)
        dv_n[...] = jnp.zeros_like(dv_n)
        _zero_once((kp, vp))
        for p, d in enumerate(DILATIONS):
            nb = n_blk // d
            chunks = _row_chunks(d, S)
            for src, dst, n in chunks:
                auxp[dst:dst + n, :] = aux[src, :]
                qp[dst:dst + n, :] = (q_ref[src, :] * ATT_SCALE).astype(BF16)
                dop[dst:dst + n, :] = do_ref[src, :].astype(BF16)
                kp[WIN + dst:WIN + dst + n, :] = k_ref[src, :].astype(BF16)
                vp[WIN + dst:WIN + dst + n, :] = v_ref[src, :].astype(BF16)
            for i in range(n_blk):
                aux_t[i] = auxp[i * WIN:(i + 1) * WIN, :].T[0:8, :]
            _pair_bias(bias_t, slopes_ref, hp, d, key_major=True)
            dvk[...] = jnp.zeros_like(dvk)

            nk = WIN if nb == 1 else KEYS
            bias_cur = jnp.concatenate([bias_t[0, WIN:KEYS, :], bias_t[0, KEYS + WIN:, :]], axis=0) if nb == 1 else None

            def block(i, carry, nb=nb, nk=nk, bias_cur=bias_cur):
                cur = pl.ds(pl.multiple_of(i * WIN, WIN), WIN)
                keys = pl.ds(pl.multiple_of(i * WIN + (KEYS - nk), WIN), nk)
                q2, do2 = qp[cur, :], dop[cur, :]
                kc = _stack_heads(kp[keys, :])
                s_t = lax.dot_general(kc, q2, _NT, preferred_element_type=F32)
                s_t = s_t + (bias_cur if nb == 1 else bias_t[((i % nb) > 0).astype(jnp.int32)])
                dp_t = lax.dot_general(_stack_heads(vp[keys, :]), do2, _NT, preferred_element_type=F32)
                ps, dss = [], []
                for h in range(2):
                    span = slice(h * nk, (h + 1) * nk)
                    p_t = jnp.exp(s_t[span, :] - aux_t[i, 2 * h:2 * h + 1, :])
                    ds_t = p_t * (dp_t[span, :] - aux_t[i, 2 * h + 1:2 * h + 2, :])
                    ps.append(p_t.astype(BF16))
                    dss.append(ds_t.astype(BF16))
                do_rows, q_rows = _stack_heads(do2), _stack_heads(q2)
                zr = jnp.zeros_like(do_rows)
                rhs = jnp.concatenate([jnp.concatenate([do_rows, zr], axis=1),
                                       jnp.concatenate([zr, q_rows], axis=1)], axis=0)
                dvk[keys, :] += jnp.dot(jnp.concatenate(ps + dss, axis=1), rhs, preferred_element_type=F32)
                dqp[cur, :] = lax.dot_general(jnp.concatenate(dss, axis=0), kc, _TN, preferred_element_type=F32)
                return carry

            lax.fori_loop(0, n_blk, block, 0, unroll=ATT_UNROLL)
            for src, dst, n in chunks:
                dq_n[src, :] += dqp[dst:dst + n, :]
                dv_n[src, :] += dvk[WIN + dst:WIN + dst + n, :LANES]
                dk_n[src, :] += dvk[WIN + dst:WIN + dst + n, LANES:]
        for c0 in range(0, S, COPY_ROWS):
            rows = slice(c0, c0 + COPY_ROWS)
            dx_ref[rows, 0:LANES] = (dq_n[rows, :] * ATT_SCALE).astype(dx_ref.dtype)
            dx_ref[rows, LANES:2 * LANES] = dk_n[rows, :].astype(dx_ref.dtype)
            dx_ref[rows, 2 * LANES:3 * LANES] = dv_n[rows, :].astype(dx_ref.dtype)

    pair = lambda width: pl.BlockSpec((None, S, width), lambda b, hp: (b, 0, hp))
    vm = lambda shape, dt: pltpu.VMEM(shape, dt)
    return pl.pallas_call(
        body, name="attention_bwd", grid=(B, n_pair),
        in_specs=[pl.BlockSpec(memory_space=pltpu.SMEM), _qkv_spec(S, 0), _qkv_spec(S, 1), _qkv_spec(S, 2),
                  pair(LANES), pair(LANES), pl.BlockSpec((None, None, S, 2), lambda b, hp: (b, hp, 0, 0))],
        out_specs=pair(3 * LANES),
        out_shape=jax.ShapeDtypeStruct((B, S, 3 * ATT_WIDTH), BF16),
        scratch_shapes=[vm((S, LANES), BF16), vm((S, LANES), BF16),
                        vm((S + WIN, LANES), BF16), vm((S + WIN, LANES), BF16),
                        vm((S, LANES), F32), vm((S, LANES), F32), vm((n_blk, 8, WIN), F32),
                        vm((2, 2 * KEYS, WIN), F32),
                        vm((S, LANES), F32), vm((S + WIN, 2 * LANES), F32),
                        vm((S, LANES), F32), vm((S, LANES), F32), vm((S, LANES), F32)],
        compiler_params=_params(2),
    )(slopes, qkv, qkv, qkv, o, do, lse)


SCAN_COLS = 256
SCAN_ROWS = 8


def _rows_to_tile(rows):
    rid = lax.broadcasted_iota(jnp.int32, (SCAN_ROWS, rows[0].shape[1]), 0)
    tile = jnp.broadcast_to(rows[0], rid.shape)
    for k in range(1, SCAN_ROWS):
        tile = jnp.where(rid == k, rows[k], tile)
    return tile


SCAN_UNROLL = 4


def _complex_powers(ar, ai, n):
    out = [(ar, ai)]
    for _ in range(n - 1):
        pr, pi = out[-1]
        out.append((pr * ar - pi * ai, pr * ai + pi * ar))
    return out


def _round_multipliers(powers, rid, reverse):
    out = []
    for s in (1, 2, 4):
        keep = (rid < SCAN_ROWS - s) if reverse else (rid >= s)
        out.append((jnp.where(keep, powers[s - 1][0], 0.0), jnp.where(keep, powers[s - 1][1], 0.0)))
    return out


def _tile_scan(xr, xi, multipliers, reverse):
    for s, (mr, mi) in zip((1, 2, 4), multipliers):
        shift = SCAN_ROWS - s if reverse else s
        sr, si = pltpu.roll(xr, shift, 0), pltpu.roll(xi, shift, 0)
        xr, xi = xr + (mr * sr - mi * si), xi + (mr * si + mi * sr)
    return xr, xi


SCAN_CHUNK = 256


def _scan_fwd(us, bb_big, a_row, cc_big):
    B, S, _ = us.shape
    groups = 2
    width = 2 * groups * SCAN_COLS
    nc = 2 * SSM_COLS // width
    nt = S // SCAN_ROWS
    tiles = SCAN_CHUNK // SCAN_ROWS
    LAST = slice(SCAN_ROWS - 1, SCAN_ROWS)

    def body(us_ref, bb_ref, a_ref, cc_ref, xs_ref, y_ref, bu_ref):
        bb = bb_ref[...].astype(BF16)
        for c in range(S // SCAN_CHUNK):
            part = jnp.dot(us_ref[c * SCAN_CHUNK:(c + 1) * SCAN_CHUNK, :].astype(BF16), bb,
                           preferred_element_type=F32)
            bu_ref[c * tiles:(c + 1) * tiles] = part.reshape(tiles, SCAN_ROWS, width)
        rid = lax.broadcasted_iota(jnp.int32, (SCAN_ROWS, SCAN_COLS), 0)
        consts = []
        for g in range(groups):
            re = slice(2 * g * SCAN_COLS, (2 * g + 1) * SCAN_COLS)
            im = slice((2 * g + 1) * SCAN_COLS, (2 * g + 2) * SCAN_COLS)
            powers = _complex_powers(a_ref[:, re], a_ref[:, im], SCAN_ROWS)
            carry_mult = (_rows_to_tile([p[0] for p in powers]), _rows_to_tile([p[1] for p in powers]))
            consts.append((re, im, carry_mult, _round_multipliers(powers, rid, reverse=False)))

        def tile(i, carry):
            out = []
            for (re, im, (cr_t, ci_t), rounds), (cr, ci) in zip(consts, carry):
                xr, xi = _tile_scan(bu_ref[i, :, re], bu_ref[i, :, im], rounds, reverse=False)
                xs_ref[i, :, re] = xr + (cr_t * cr - ci_t * ci)
                xs_ref[i, :, im] = xi + (cr_t * ci + ci_t * cr)
                out.append((xs_ref[i, LAST, re], xs_ref[i, LAST, im]))
            return tuple(out)

        zero = jnp.zeros((1, SCAN_COLS), F32)
        lax.fori_loop(0, nt, tile, ((zero, zero),) * groups, unroll=SCAN_UNROLL)

        @pl.when(pl.program_id(1) == 0)
        def _():
            y_ref[...] = jnp.zeros_like(y_ref)

        cc = cc_ref[...].astype(BF16)
        for c in range(S // SCAN_CHUNK):
            x2 = xs_ref[c * tiles:(c + 1) * tiles].reshape(SCAN_CHUNK, width).astype(BF16)
            y_ref[c * SCAN_CHUNK:(c + 1) * SCAN_CHUNK, :] += jnp.dot(x2, cc, preferred_element_type=F32)

    col = pl.BlockSpec((None, nt, SCAN_ROWS, width), lambda b, j: (b, 0, 0, j))
    tok = pl.BlockSpec((None, S, SSM_WIDTH), lambda b, j: (b, 0, 0))
    xs, y = pl.pallas_call(
        body, name="s5_scan_fwd", grid=(B, nc),
        in_specs=[tok, pl.BlockSpec((SSM_WIDTH, width), lambda b, j: (0, j)),
                  pl.BlockSpec((1, width), lambda b, j: (0, j)), pl.BlockSpec((width, SSM_WIDTH), lambda b, j: (j, 0))],
        out_specs=[col, tok],
        out_shape=[jax.ShapeDtypeStruct((B, nt, SCAN_ROWS, 2 * SSM_COLS), F32),
                   jax.ShapeDtypeStruct((B, S, SSM_WIDTH), F32)],
        scratch_shapes=[pltpu.VMEM((nt, SCAN_ROWS, width), F32)],
        compiler_params=_params(2),
    )(us, bb_big, a_row, cc_big)
    return xs.reshape(B, S, 2 * SSM_COLS), y


def _scan_bwd(dy, us, bb_big, cc_big, xs, a_row):
    B, S, _ = dy.shape
    width = 2 * SCAN_COLS
    nc = SSM_COLS // SCAN_COLS
    nt = S // SCAN_ROWS
    tiles = SCAN_CHUNK // SCAN_ROWS
    RE, IM = slice(0, SCAN_COLS), slice(SCAN_COLS, 2 * SCAN_COLS)
    FIRST, LAST = slice(0, 1), slice(SCAN_ROWS - 1, SCAN_ROWS)

    def body(dy_ref, us_ref, bb_ref, cc_ref, x_ref, a_ref, dus_ref, ga_ref, dbb_ref, dcc_ref, d_ref, lam_ref):
        b = pl.program_id(1)
        cc = cc_ref[...].astype(BF16)
        for c in range(S // SCAN_CHUNK):
            part = lax.dot_general(dy_ref[c * SCAN_CHUNK:(c + 1) * SCAN_CHUNK, :].astype(BF16), cc, _NT,
                                   preferred_element_type=F32)
            d_ref[c * tiles:(c + 1) * tiles] = part.reshape(tiles, SCAN_ROWS, width)
        powers = _complex_powers(a_ref[:, RE], -a_ref[:, IM], SCAN_ROWS)
        rid = lax.broadcasted_iota(jnp.int32, (SCAN_ROWS, SCAN_COLS), 0)
        cr_t = _rows_to_tile([powers[SCAN_ROWS - 1 - r][0] for r in range(SCAN_ROWS)])
        ci_t = _rows_to_tile([powers[SCAN_ROWS - 1 - r][1] for r in range(SCAN_ROWS)])
        rounds = _round_multipliers(powers, rid, reverse=True)

        @pl.when(b == 0)
        def _():
            ga_ref[...] = jnp.zeros_like(ga_ref)
            dbb_ref[...] = jnp.zeros_like(dbb_ref)
            dcc_ref[...] = jnp.zeros_like(dcc_ref)

        def tile(j, carry):
            cr, ci, accr, acci = carry
            i = nt - 1 - j
            lr, li = _tile_scan(d_ref[i, :, RE], d_ref[i, :, IM], rounds, reverse=True)
            lam_r = lr + (cr_t * cr - ci_t * ci)
            lam_i = li + (cr_t * ci + ci_t * cr)
            lam_ref[i, :, RE] = lam_r
            lam_ref[i, :, IM] = lam_i
            ip = jnp.maximum(i - 1, 0)
            keep = (i > 0).astype(F32)
            xpr = jnp.where(rid == 0, x_ref[ip, LAST, RE] * keep, pltpu.roll(x_ref[i, :, RE], 1, 0))
            xpi = jnp.where(rid == 0, x_ref[ip, LAST, IM] * keep, pltpu.roll(x_ref[i, :, IM], 1, 0))
            accr = accr + lam_r * xpr + lam_i * xpi
            acci = acci + lam_i * xpr - lam_r * xpi
            return lam_ref[i, FIRST, RE], lam_ref[i, FIRST, IM], accr, acci

        z1 = jnp.zeros((1, SCAN_COLS), F32)
        z8 = jnp.zeros((SCAN_ROWS, SCAN_COLS), F32)
        _, _, accr, acci = lax.fori_loop(0, nt, tile, (z1, z1, z8, z8), unroll=SCAN_UNROLL)
        ga_ref[:, RE] += _col_sum(accr)
        ga_ref[:, IM] += _col_sum(acci)

        bb = bb_ref[...].astype(BF16)
        for c in range(S // SCAN_CHUNK):
            rows = slice(c * SCAN_CHUNK, (c + 1) * SCAN_CHUNK)
            lam2 = lam_ref[c * tiles:(c + 1) * tiles].reshape(SCAN_CHUNK, width).astype(BF16)
            x2 = x_ref[c * tiles:(c + 1) * tiles].reshape(SCAN_CHUNK, width).astype(BF16)
            dus_ref[rows, :] = lax.dot_general(lam2, bb, _NT, preferred_element_type=F32)
            dbb_ref[...] += lax.dot_general(us_ref[rows, :].astype(BF16), lam2, _TN, preferred_element_type=F32)
            dcc_ref[...] += lax.dot_general(x2, dy_ref[rows, :].astype(BF16), _TN, preferred_element_type=F32)

    col = pl.BlockSpec((None, nt, SCAN_ROWS, width), lambda j, b: (b, 0, 0, j))
    tok = pl.BlockSpec((None, S, SSM_WIDTH), lambda j, b: (b, 0, 0))
    scratch = pltpu.VMEM((nt, SCAN_ROWS, width), F32)
    return pl.pallas_call(
        body, name="s5_scan_bwd", grid=(nc, B),
        in_specs=[tok, tok, pl.BlockSpec((SSM_WIDTH, width), lambda j, b: (0, j)),
                  pl.BlockSpec((width, SSM_WIDTH), lambda j, b: (j, 0)), col,
                  pl.BlockSpec((1, width), lambda j, b: (0, j))],
        out_specs=[pl.BlockSpec((None, None, S, SSM_WIDTH), lambda j, b: (j, b, 0, 0)),
                   pl.BlockSpec((1, width), lambda j, b: (0, j)),
                   pl.BlockSpec((SSM_WIDTH, width), lambda j, b: (0, j)),
                   pl.BlockSpec((width, SSM_WIDTH), lambda j, b: (j, 0))],
        out_shape=[jax.ShapeDtypeStruct((nc, B, S, SSM_WIDTH), F32), jax.ShapeDtypeStruct((1, 2 * SSM_COLS), F32),
                   jax.ShapeDtypeStruct((SSM_WIDTH, 2 * SSM_COLS), F32),
                   jax.ShapeDtypeStruct((2 * SSM_COLS, SSM_WIDTH), F32)],
        scratch_shapes=[scratch, scratch],
        compiler_params=_params(2),
    )(dy, us, bb_big, cc_big, xs.reshape(B, nt, SCAN_ROWS, 2 * SSM_COLS), a_row)


def _s5_discretise(lr, li, log_dt):
    dt = jnp.exp(log_dt)
    mag = jnp.exp(lr * dt)
    ang = li * dt
    ab_re, ab_im = mag * jnp.cos(ang), mag * jnp.sin(ang)
    nr, ni = ab_re - 1.0, ab_im
    den = lr * lr + li * li
    f_re = (nr * lr + ni * li) / den
    f_im = (ni * lr - nr * li) / den
    return dt, ab_re, ab_im, nr, ni, den, f_re, f_im


def _s5_params(a_re, a_im, log_dt):
    def body(lr_ref, li_ref, ld_ref, abr, abi, fr, fi):
        _, ab_re, ab_im, _, _, _, f_re, f_im = _s5_discretise(lr_ref[...], li_ref[...], ld_ref[...])
        abr[...] = ab_re
        abi[...] = ab_im
        fr[...] = f_re
        fi[...] = f_im

    return pl.pallas_call(body, name="s5_params",
                          out_shape=[jax.ShapeDtypeStruct(a_re.shape, F32)] * 4)(a_re, a_im, log_dt)


def _s5_input_matrix(f_re, f_im, b_re, b_im):
    def body(fr, fi, br, bi, o_re, o_im):
        o_re[...] = fr[...] * br[...] - fi[...] * bi[...]
        o_im[...] = fr[...] * bi[...] + fi[...] * br[...]

    return pl.pallas_call(body, name="s5_input_matrix",
                          out_shape=[jax.ShapeDtypeStruct(b_re.shape, F32)] * 2)(f_re, f_im, b_re, b_im)


def _s5_input_matrix_bwd(f_re, f_im, b_re, b_im, g_re, g_im):
    def body(fr, fi, br, bi, gr, gi, dbr, dbi, dfr, dfi):
        dbr[...] = fr[...] * gr[...] + fi[...] * gi[...]
        dbi[...] = fr[...] * gi[...] - fi[...] * gr[...]
        dfr[...] = jnp.sum(br[...] * gr[...] + bi[...] * gi[...], axis=1, keepdims=True)
        dfi[...] = jnp.sum(br[...] * gi[...] - bi[...] * gr[...], axis=1, keepdims=True)

    return pl.pallas_call(
        body, name="s5_input_matrix_bwd",
        out_shape=[jax.ShapeDtypeStruct(b_re.shape, F32)] * 2 + [jax.ShapeDtypeStruct(f_re.shape, F32)] * 2,
    )(f_re, f_im, b_re, b_im, g_re, g_im)


def _s5_params_bwd(a_re, a_im, log_dt, g_ab_re, g_ab_im, d_f_re, d_f_im):
    def body(lr_ref, li_ref, ld_ref, gar, gai, dfr, dfi, o_lr, o_li, o_ld):
        lr, li = lr_ref[...], li_ref[...]
        dt, ab_re, ab_im, nr, ni, den, f_re, f_im = _s5_discretise(lr, li, ld_ref[...])
        d_fr, d_fi = dfr[...], dfi[...]
        d_nr = (d_fr * lr - d_fi * li) / den
        d_ni = (d_fr * li + d_fi * lr) / den
        common = (d_fr * f_re + d_fi * f_im) * 2.0 / den
        d_lr = (d_fr * nr + d_fi * ni) / den - common * lr
        d_li = (d_fr * ni - d_fi * nr) / den - common * li
        d_abr = gar[...] + d_nr
        d_abi = gai[...] + d_ni
        d_mag_mag = d_abr * ab_re + d_abi * ab_im
        d_ang = d_abi * ab_re - d_abr * ab_im
        o_lr[...] = d_lr + d_mag_mag * dt
        o_li[...] = d_li + d_ang * dt
        o_ld[...] = jnp.sum(d_mag_mag * lr + d_ang * li, axis=1, keepdims=True) * dt

    return pl.pallas_call(
        body, name="s5_params_bwd",
        out_shape=[jax.ShapeDtypeStruct(a_re.shape, F32)] * 2 + [jax.ShapeDtypeStruct(log_dt.shape, F32)],
    )(a_re, a_im, log_dt, g_ab_re, g_ab_im, d_f_re, d_f_im)


CONV_COLS = 256


def _shift_down(v, j, row):
    return jnp.where(row >= j, pltpu.roll(v, j, 0), 0.0)


def _shift_up(v, j, row, seq):
    return jnp.where(row < seq - j, pltpu.roll(v, seq - j, 0), 0.0)


def _conv_fwd(up, w_conv, b_conv):
    B, S, _ = up.shape
    nj = D_FF // CONV_COLS

    def body(up_ref, w_ref, b_ref, ff_ref):
        a = up_ref[:, :CONV_COLS].astype(F32)
        val = up_ref[:, CONV_COLS:].astype(F32)
        row = lax.broadcasted_iota(jnp.int32, a.shape, 0)
        w0, w1, w2 = w_ref[0:1, :], w_ref[1:2, :], w_ref[2:3, :]
        conv = b_ref[...] + w0 * a + w1 * _shift_down(a, 1, row) + w2 * _shift_down(a, 2, row)
        ff_ref[...] = (conv * _sigmoid(conv) * val).astype(ff_ref.dtype)

    return pl.pallas_call(
        body, name="conv_gate_fwd", grid=(B, nj),
        in_specs=[pl.BlockSpec((None, S, 2 * CONV_COLS), lambda b, j: (b, 0, j)),
                  pl.BlockSpec((3, CONV_COLS), lambda b, j: (0, j)),
                  pl.BlockSpec((1, CONV_COLS), lambda b, j: (0, j))],
        out_specs=pl.BlockSpec((None, S, CONV_COLS), lambda b, j: (b, 0, j)),
        out_shape=jax.ShapeDtypeStruct((B, S, D_FF), BF16),
        compiler_params=_params(2),
    )(up, w_conv, b_conv)


def _conv_bwd(up, d_ff, w_conv, b_conv):
    B, S, _ = up.shape
    nj = D_FF // CONV_COLS

    def body(up_ref, dff_ref, w_ref, b_ref, dup_ref, dw_ref, db_ref):
        b = pl.program_id(1)
        a = up_ref[:, :CONV_COLS].astype(F32)
        val = up_ref[:, CONV_COLS:].astype(F32)
        row = lax.broadcasted_iota(jnp.int32, a.shape, 0)
        w0, w1, w2 = w_ref[0:1, :], w_ref[1:2, :], w_ref[2:3, :]
        a1, a2 = _shift_down(a, 1, row), _shift_down(a, 2, row)
        conv = b_ref[...] + w0 * a + w1 * a1 + w2 * a2
        sg = _sigmoid(conv)
        dff = dff_ref[...].astype(F32)
        d_val = dff * conv * sg
        dc = dff * val * (sg * (1.0 + conv * (1.0 - sg)))
        d_a = w0 * dc + w1 * _shift_up(dc, 1, row, S) + w2 * _shift_up(dc, 2, row, S)
        dup_ref[:, :CONV_COLS] = d_a.astype(dup_ref.dtype)
        dup_ref[:, CONV_COLS:] = d_val.astype(dup_ref.dtype)

        @pl.when(b == 0)
        def _():
            dw_ref[...] = jnp.zeros_like(dw_ref)
            db_ref[...] = jnp.zeros_like(db_ref)

        dw_ref[0:1, :] += _col_sum(dc * a)
        dw_ref[1:2, :] += _col_sum(dc * a1)
        dw_ref[2:3, :] += _col_sum(dc * a2)
        db_ref[...] += _col_sum(dc)

    return pl.pallas_call(
        body, name="conv_gate_bwd", grid=(nj, B),
        in_specs=[pl.BlockSpec((None, S, 2 * CONV_COLS), lambda j, b: (b, 0, j)),
                  pl.BlockSpec((None, S, CONV_COLS), lambda j, b: (b, 0, j)),
                  pl.BlockSpec((3, CONV_COLS), lambda j, b: (0, j)),
                  pl.BlockSpec((1, CONV_COLS), lambda j, b: (0, j))],
        out_specs=[pl.BlockSpec((None, S, 2 * CONV_COLS), lambda j, b: (b, 0, j)),
                   pl.BlockSpec((3, CONV_COLS), lambda j, b: (0, j)),
                   pl.BlockSpec((1, CONV_COLS), lambda j, b: (0, j))],
        out_shape=[jax.ShapeDtypeStruct((B, S, 2 * D_FF), BF16), jax.ShapeDtypeStruct((3, D_FF), F32),
                   jax.ShapeDtypeStruct((1, D_FF), F32)],
        compiler_params=_params(2),
    )(up, d_ff, w_conv, b_conv)


def _ada_fwd(c_all, w_ada, b_ada):
    def body(c_ref, w_ref, b_ref, o_ref):
        cv = c_ref[...]
        act = (cv * _sigmoid(cv)).astype(BF16)
        o_ref[...] = jnp.dot(act, w_ref[...].astype(BF16), preferred_element_type=F32) + b_ref[...]

    return pl.pallas_call(body, name="ada_fwd",
                          out_shape=jax.ShapeDtypeStruct((c_all.shape[0], w_ada.shape[1]), F32),
                          compiler_params=pltpu.CompilerParams(vmem_limit_bytes=V7X_VMEM_LIMIT))(c_all, w_ada, b_ada)


def _ada_bwd(c_all, dmod_all, dmod_cols):
    def body(c_ref, dm_ref, dmc_ref, dw_ref, db_ref):
        cv = c_ref[...]
        act = (cv * _sigmoid(cv)).astype(BF16)
        dw_ref[...] = lax.dot_general(act, dmc_ref[...].astype(BF16), _TN, preferred_element_type=F32)
        db_ref[...] = _col_sum(dm_ref[...])

    return pl.pallas_call(
        body, name="ada_bwd",
        out_shape=[jax.ShapeDtypeStruct((c_all.shape[1], dmod_cols.shape[1]), F32),
                   jax.ShapeDtypeStruct((1, dmod_all.shape[1]), F32)],
        compiler_params=pltpu.CompilerParams(vmem_limit_bytes=V7X_VMEM_LIMIT))(c_all, dmod_all, dmod_cols)


def _adamw(w, m, v, g_parts, name, own=None):
    R, C = w.shape
    P = g_parts.shape[0]
    tr = R
    for cand in (256, 128, 64, 32, 16, 8):
        if R % cand == 0 and cand * C * 4 * (P + 8) * 2 <= V7X_VMEM_LIMIT // 2:
            tr = cand
            break
    c1 = 1.0 / (1.0 - ADAM_B1 ** ADAM_STEP)
    c2 = 1.0 / (1.0 - ADAM_B2 ** ADAM_STEP)

    def update(w_ref, m_ref, v_ref, g, og, od, om, ov):
        m_new = ADAM_B1 * m_ref[...] + (1.0 - ADAM_B1) * g
        v_new = ADAM_B2 * v_ref[...] + (1.0 - ADAM_B2) * (g * g)
        og[...] = g
        om[...] = m_new
        ov[...] = v_new
        od[...] = -ADAM_LR * ((m_new * c1) / (jnp.sqrt(v_new * c2) + ADAM_EPS) + ADAM_WD * w_ref[...])

    def total(g_ref):
        g = g_ref[0].astype(F32)
        for p in range(1, P):
            g = g + g_ref[p].astype(F32)
        return g

    out_shape = [jax.ShapeDtypeStruct((R, C), F32)] * 4
    if own is None:
        def body(w_ref, m_ref, v_ref, g_ref, og, od, om, ov):
            update(w_ref, m_ref, v_ref, total(g_ref), og, od, om, ov)

        spec = pl.BlockSpec((tr, C), lambda i: (i, 0))
        return pl.pallas_call(
            body, name=name, grid=(R // tr,),
            in_specs=[spec, spec, spec, pl.BlockSpec((P, tr, C), lambda i: (0, i, 0))],
            out_specs=[spec] * 4, out_shape=out_shape, compiler_params=_params(1),
        )(w, m, v, g_parts)

    slots, me = own

    def body_own(me_ref, w_ref, m_ref, v_ref, g_ref, own_ref, og, od, om, ov):
        g = own_ref[...].astype(F32)
        for p in range(P):
            g = g + jnp.where(me_ref[0] == p, 0.0, g_ref[p].astype(F32))
        update(w_ref, m_ref, v_ref, g, og, od, om, ov)

    spec = pl.BlockSpec((tr, C), lambda i, me_ref: (i, 0))
    grid_spec = pltpu.PrefetchScalarGridSpec(
        num_scalar_prefetch=1, grid=(R // tr,),
        in_specs=[spec, spec, spec, pl.BlockSpec((P, tr, C), lambda i, me_ref: (0, i, 0)),
                  pl.BlockSpec((None, tr, C), lambda i, me_ref: (me_ref[0], i, 0))],
        out_specs=[spec] * 4)
    return pl.pallas_call(body_own, name=name, grid_spec=grid_spec, out_shape=out_shape,
                          compiler_params=_params(1))(me, w, m, v, g_parts, slots)


def _adamw_small(ws, ms, vs, gs):
    n = len(ws)
    c1 = 1.0 / (1.0 - ADAM_B1 ** ADAM_STEP)
    c2 = 1.0 / (1.0 - ADAM_B2 ** ADAM_STEP)

    def body(*refs):
        ins, outs = refs[:4 * n], refs[4 * n:]
        for i in range(n):
            w, m, v, g = ins[i][...], ins[n + i][...], ins[2 * n + i][...], ins[3 * n + i][...]
            m_new = ADAM_B1 * m + (1.0 - ADAM_B1) * g
            v_new = ADAM_B2 * v + (1.0 - ADAM_B2) * (g * g)
            outs[4 * i][...] = g
            outs[4 * i + 1][...] = -ADAM_LR * ((m_new * c1) / (jnp.sqrt(v_new * c2) + ADAM_EPS) + ADAM_WD * w)
            outs[4 * i + 2][...] = m_new
            outs[4 * i + 3][...] = v_new

    out_shape = [jax.ShapeDtypeStruct(w.shape, F32) for w in ws for _ in range(4)]
    return pl.pallas_call(body, name="adamw_small", out_shape=out_shape,
                          compiler_params=pltpu.CompilerParams(vmem_limit_bytes=V7X_VMEM_LIMIT))(*ws, *ms, *vs, *gs)


def _sum_parts(parts, loss_rows):
    P, R, C = parts.shape
    lo, hi = loss_rows

    def body(p_ref, o_ref, loss_ref):
        t = p_ref[0]
        for p in range(1, P):
            t = t + p_ref[p]
        o_ref[...] = t
        tot = jnp.sum(jnp.sum(o_ref[lo:hi, :], axis=1, keepdims=True), axis=0, keepdims=True)
        loss_ref[...] = jnp.broadcast_to(tot, loss_ref.shape)

    return pl.pallas_call(body, name="sum_small_grads",
                          out_shape=[jax.ShapeDtypeStruct((R, C), F32), jax.ShapeDtypeStruct((1, LANES), F32)],
                          compiler_params=pltpu.CompilerParams(vmem_limit_bytes=V7X_VMEM_LIMIT))(parts)


def _exchange(items, name):
    n = len(items)
    MESH = pl.DeviceIdType.MESH

    def body(*refs):
        src, dst = refs[:n], refs[n:2 * n]
        send_sems, recv_sems, local_sems = refs[2 * n:]
        x, y, c = lax.axis_index("x"), lax.axis_index("y"), lax.axis_index("c")
        me = 4 * x + 2 * y + c
        started = []
        for it, (_, per_peer) in enumerate(items):
            own = pltpu.make_async_copy(src[it].at[me] if per_peer else src[it], dst[it].at[me], local_sems.at[it])
            own.start()
            started.append(own)
        sends, recvs = [], []
        for k in range(1, N_DEV):
            px = 1 - x if k & 4 else x
            py = 1 - y if k & 2 else y
            pc = 1 - c if k & 1 else c
            peer = 4 * px + 2 * py + pc
            for it, (_, per_peer) in enumerate(items):
                s = src[it].at[peer] if per_peer else src[it]
                cp = pltpu.make_async_remote_copy(src_ref=s, dst_ref=dst[it].at[me], send_sem=send_sems.at[it, k - 1],
                                                  recv_sem=recv_sems.at[it, k - 1], device_id=(px, py, pc),
                                                  device_id_type=MESH)
                cp.start()
                sends.append(cp)
                recvs.append(pltpu.make_async_remote_copy(
                    src_ref=s, dst_ref=dst[it].at[peer], send_sem=send_sems.at[it, k - 1],
                    recv_sem=recv_sems.at[it, k - 1], device_id=(px, py, pc), device_id_type=MESH))
        for cp in recvs:
            cp.wait_recv()
        for cp in sends:
            cp.wait_send()
        for cp in started:
            cp.wait()

    any_spec = pl.BlockSpec(memory_space=pl.ANY)
    out_shape = []
    for a, per_peer in items:
        shp = a.shape if per_peer else (N_DEV,) + a.shape
        out_shape.append(jax.ShapeDtypeStruct(shp, a.dtype))
    return pl.pallas_call(
        body, name=name, in_specs=[any_spec] * n, out_specs=[any_spec] * n, out_shape=out_shape,
        scratch_shapes=[pltpu.SemaphoreType.DMA((n, N_DEV - 1)), pltpu.SemaphoreType.DMA((n, N_DEV - 1)),
                        pltpu.SemaphoreType.DMA((n,))],
    )(*[a for a, _ in items])


def _remote(src, dst, send_sem, recv_sem, device):
    return pltpu.make_async_remote_copy(src_ref=src, dst_ref=dst, send_sem=send_sem, recv_sem=recv_sem,
                                        device_id=device, device_id_type=pl.DeviceIdType.MESH)


def _mesh_place():
    x, y, c = lax.axis_index("x"), lax.axis_index("y"), lax.axis_index("c")
    other_chips = [(1 - x, y), (x, 1 - y), (1 - x, 1 - y)]
    return x, y, c, (x, y, 1 - c), other_chips


def _gather_all(items, name):
    n = len(items)

    def body(*refs):
        src, dst = refs[:n], refs[n:2 * n]
        send_sems, recv_sems, local_sems = refs[2 * n:]
        x, y, c, sibling, chips = _mesh_place()
        slot = lambda px, py, pc: 4 * px + 2 * py + pc
        me = slot(x, y, c)
        own = [pltpu.make_async_copy(src[it], dst[it].at[me], local_sems.at[it]) for it in range(n)]
        first = []
        for it in range(n):
            first.append(_remote(src[it], dst[it].at[me], send_sems.at[it, 0], recv_sems.at[it, 0], sibling))
            for j, chip in enumerate(chips):
                first.append(_remote(src[it], dst[it].at[me], send_sems.at[it, 1 + j], recv_sems.at[it, 1 + j],
                                     (*chip, c)))
        for cp in own + first:
            cp.start()
        passed = []
        for j, chip in enumerate(chips):
            blk = slot(*chip, c)
            for it in range(n):
                _remote(src[it], dst[it].at[blk], send_sems.at[it, 1 + j], recv_sems.at[it, 1 + j],
                        (*chip, c)).wait_recv()
                fwd = _remote(dst[it].at[blk], dst[it].at[blk], send_sems.at[it, 4 + j], recv_sems.at[it, 4 + j],
                              sibling)
                fwd.start()
                passed.append(fwd)
        for it in range(n):
            _remote(src[it], dst[it].at[slot(x, y, 1 - c)], send_sems.at[it, 0], recv_sems.at[it, 0],
                    sibling).wait_recv()
        for j, chip in enumerate(chips):
            for it in range(n):
                _remote(src[it], dst[it].at[slot(*chip, 1 - c)], send_sems.at[it, 4 + j], recv_sems.at[it, 4 + j],
                        sibling).wait_recv()
        for cp in first + passed:
            cp.wait_send()
        for cp in own:
            cp.wait()

    any_spec = pl.BlockSpec(memory_space=pl.ANY)
    return pl.pallas_call(
        body, name=name, in_specs=[any_spec] * n, out_specs=[any_spec] * n,
        out_shape=[jax.ShapeDtypeStruct((N_DEV,) + a.shape, a.dtype) for a in items],
        scratch_shapes=[pltpu.SemaphoreType.DMA((n, 7)), pltpu.SemaphoreType.DMA((n, 7)),
                        pltpu.SemaphoreType.DMA((n,))],
    )(*items)


def _peers():
    x, y, c = lax.axis_index("x"), lax.axis_index("y"), lax.axis_index("c")
    out = []
    for k in range(1, N_DEV):
        px = 1 - x if k & 4 else x
        py = 1 - y if k & 2 else y
        pc = 1 - c if k & 1 else c
        out.append((k, (px, py, pc), 4 * px + 2 * py + pc))
    return 4 * x + 2 * y + c, out


def _exchange_start(items, name, gather, carry=()):
    n, m = len(items), len(carry)

    def body(*refs):
        src, land = refs[:n], refs[n:2 * n]
        first_out = 2 * n + m
        send_sems, recv_sems = refs[first_out:first_out + n], refs[first_out + n:first_out + 2 * n]
        token = refs[-1]
        me, peers = _peers()
        for k, peer, slot in peers:
            for it in range(n):
                _remote(src[it] if gather else src[it].at[slot], land[it].at[me], send_sems[it], recv_sems[it],
                        peer).start()
        token[...] = jnp.zeros_like(token)

    hbm = pl.BlockSpec(memory_space=pltpu.HBM)
    sem = pl.BlockSpec(memory_space=pltpu.SEMAPHORE)
    land_shapes = [(N_DEV,) + (a.shape if gather else a.shape[1:]) for a in items]
    lands = [lax.empty(shp, a.dtype) for shp, a in zip(land_shapes, items)]
    through = list(items) + lands + list(carry)
    outs = pl.pallas_call(
        body, name=name,
        out_shape=(*[pltpu.SemaphoreType.DMA(())] * (2 * n), *[pltpu.HBM(a.shape, a.dtype) for a in through],
                   jax.ShapeDtypeStruct((8, LANES), F32)),
        in_specs=[hbm] * len(through),
        out_specs=(*[sem] * (2 * n), *[hbm] * len(through), pl.BlockSpec(memory_space=pltpu.VMEM)),
        input_output_aliases={i: 2 * n + i for i in range(len(through))},
        compiler_params=pltpu.CompilerParams(has_side_effects=pltpu.SideEffectType.DATAFLOW_SIDE_EFFECTING),
    )(*[pltpu.with_memory_space_constraint(a, pltpu.HBM) for a in through])
    return (list(outs[:n]), list(outs[n:2 * n]), list(outs[2 * n:3 * n]), list(outs[3 * n:4 * n]), outs[-1],
            list(outs[4 * n:4 * n + m]))


def _exchange_wait(send_sems, recv_sems, items, lands, after, name):
    n = len(items)

    def body(*refs):
        land = refs[n:2 * n]
        send_sems, recv_sems = refs[2 * n:3 * n], refs[3 * n:4 * n]
        me, peers = _peers()
        for it in range(n):
            seven = land[it].at[pl.ds(0, N_DEV - 1)]
            cp = _remote(seven, seven, send_sems[it], recv_sems[it], peers[0][1])
            cp.wait_send()
            cp.wait_recv()

    hbm = pl.BlockSpec(memory_space=pltpu.HBM)
    sem = pl.BlockSpec(memory_space=pltpu.SEMAPHORE)
    outs = pl.pallas_call(
        body, name=name,
        out_shape=tuple(pltpu.HBM(a.shape, a.dtype) for a in list(items) + list(lands)),
        in_specs=[hbm] * (2 * n) + [sem] * (2 * n) + [pl.BlockSpec(memory_space=pl.ANY)],
        out_specs=tuple([hbm] * (2 * n)),
        input_output_aliases={i: i for i in range(2 * n)},
        compiler_params=pltpu.CompilerParams(has_side_effects=pltpu.SideEffectType.DATAFLOW_SIDE_EFFECTING),
    )(*items, *lands, *send_sems, *recv_sems, after)
    return list(outs[:n]), list(outs[n:])


def _gelu_tanh(y):
    k = math.sqrt(2.0 / math.pi)
    t = jnp.tanh(k * (y + 0.044715 * y * y * y))
    return 0.5 * y * (1.0 + t), t


def _local_step(x, mod, target, W, late_weights, P, send_early):
    B, S, D = x.shape
    T = B * S
    TS = 512
    flat = lambda a: a.reshape(T, a.shape[-1])
    unflat = lambda a: a.reshape(B, S, a.shape[-1])
    mod_col = lambda i: (mod, D, i)

    def f_modnorm_project(xv, sc, sh, g, wq, wu, wg):
        u = ((xv * _rms_scale(xv) * g) * (1.0 + sc) + sh).astype(BF16)
        return (u, lax.dot_general(u, wq, _NT, preferred_element_type=F32),
                lax.dot_general(u, wu, _NT, preferred_element_type=F32),
                lax.dot_general(u, wg, _NT, preferred_element_type=F32))

    u1, qkv, us, gates = _rowwise(
        f_modnorm_project, [(x, D, 0)], [mod_col(1), mod_col(0)], [P["g_mix"], W["w_qkv"], W["w_us"], W["w_gates"]],
        [(D, BF16), (3 * ATT_WIDTH, F32), (SSM_WIDTH, F32), (2 * D, BF16)], [], [], ts=TS, name="modnorm_project_in")
    u1f = flat(u1)

    o_att, lse = _attention_fwd(qkv, P["slopes"])
    more_w, more_p = late_weights(o_att)
    W, P = {**W, **more_w}, {**P, **more_p}

    xs, y_mm = _scan_fwd(us, P["bb_big"], P["a_row"], P["cc_big"])

    bga, bgs = P["b_gate"][:, :D], P["b_gate"][:, D:]

    def f_mixer_tail(ov, ymm, usv, ga, gs, xv, gt, sc, sh, w_att, w_ssm, w_o, bga_, bgs_, g, dsk, wg, bg):
        yv = ymm + dsk * usv
        ge, _ = _gelu_tanh(yv)
        zv = (ge * _sigmoid(jnp.dot(ge.astype(BF16), wg, preferred_element_type=F32) + bg)).astype(BF16)
        ya = jnp.dot(ov, w_att, preferred_element_type=F32)
        ys = jnp.dot(zv, w_ssm, preferred_element_type=F32)
        mg = (_sigmoid(ga + bga_) * ya + _sigmoid(gs + bgs_) * ys).astype(BF16)
        mx = jnp.dot(mg, w_o, preferred_element_type=F32)
        h = xv + gt * mx
        return yv, zv, mg, mx, h, (h * _rms_scale(h) * g) * (1.0 + sc) + sh

    y_s5, z, merged, mix, h1, u2 = _rowwise(
        f_mixer_tail,
        [(o_att, ATT_WIDTH, 0), (y_mm, SSM_WIDTH, 0), (us, SSM_WIDTH, 0), (gates, D, 0), (gates, D, 1), (x, D, 0)],
        [mod_col(2), mod_col(4), mod_col(3)],
        [W["w_proj_att"], W["w_proj_ssm"], W["w_out"], bga, bgs, P["g_ffn"], P["d_skip"], W["w_glu"], P["b_glu"]],
        [(SSM_WIDTH, F32), (SSM_WIDTH, BF16), (D, BF16), (D, BF16), (D, F32), (D, BF16)],
        [], [], ts=TS, raw=(0,), name="mixer_tail")

    up = unflat(_up_fwd(flat(u2), W["w_up"], name="ffn_up"))
    ff = _conv_fwd(up, P["w_conv"], P["b_conv"])

    def f_head(ffv, h1v, tg, gt, g, w_dn):
        dn = jnp.dot(ffv, w_dn, preferred_element_type=F32)
        h2 = h1v + gt * dn
        r = _rms_scale(h2)
        nh = h2 * r
        e = nh * g - tg
        dy = e * (1.0 / D)
        gy = dy * g
        dh = r * (gy - nh * jnp.mean(gy * nh, axis=-1, keepdims=True))
        return (dh, dh * gt, _col_sum(dh * dn), _col_sum(dy * nh), _col_sum(e * e) * (0.5 / D))

    dh2, d_down, d_gt2, d_g_final, loss_cols = _rowwise(
        f_head, [(ff, D_FF, 0), (h1, D, 0), (target, D, 0)], [mod_col(5)], [P["g_final"], W["w_down"]],
        [(D, BF16), (D, BF16)], [D], [(1, D), (1, D)], ts=TS, raw=(0,), name="ffn_down_head_loss")

    d_downf = flat(d_down)
    d_ff = unflat(_matmul(d_downf, W["w_down"], tb=True, out_dtype=BF16, name="ffn_down_dx"))
    d_w_down = _matmul(flat(ff), d_downf, ta=True, out_dtype=BF16, name="ffn_down_dw")
    d_up, d_w_conv, d_b_conv = _conv_bwd(up, d_ff, P["w_conv"], P["b_conv"])
    d_upf = flat(d_up)
    d_u2 = unflat(_up_dx(d_upf, W["w_up"], name="ffn_up_dx"))
    d_w_up = _up_dw(flat(u2), d_upf, name="ffn_up_dw")
    token, _ = send_early(dict(w_down=d_w_down.reshape(N_DEV, D_FF // N_DEV, D), w_up=d_w_up))
    g_ffn_after = P["g_ffn"] + token[0:1, 0:1]

    def f_modnorm_bwd(du, h, dres, mx, sc, gt, g):
        r = _rms_scale(h)
        nh = h * r
        dn = du * (1.0 + sc)
        gy = dn * g
        dh = dres + r * (gy - nh * jnp.mean(gy * nh, axis=-1, keepdims=True))
        return (dh, dh * gt, _col_sum(du), _col_sum(du * nh * g), _col_sum(dh * mx), _col_sum(dn * nh))

    dh1, d_mix, d_sh2, d_sc2, d_gt1, d_g_ffn = _rowwise(
        f_modnorm_bwd, [(d_u2, D, 0), (h1, D, 0), (dh2, D, 0), (mix, D, 0)], [mod_col(4), mod_col(2)], [g_ffn_after],
        [(D, BF16), (D, BF16)], [D, D, D], [(1, D)], ts=TS, name="modnorm_ffn_bwd")

    d_mixf = flat(d_mix)
    d_w_out = _matmul(flat(merged), d_mixf, ta=True, out_dtype=BF16, name="proj_out_dw")

    def f_mixer_tail_bwd(dmx, ov, zv, ga, gs, yv, usv, w_o, w_att, w_ssm, bga_, bgs_, dsk, wg, bg):
        dm = lax.dot_general(dmx, w_o, _NT, preferred_element_type=F32)
        ya = jnp.dot(ov, w_att, preferred_element_type=F32)
        ys = jnp.dot(zv, w_ssm, preferred_element_type=F32)
        sa, ss = _sigmoid(ga + bga_), _sigmoid(gs + bgs_)
        dga = dm * ya * sa * (1.0 - sa)
        dgs = dm * ys * ss * (1.0 - ss)
        dya, dys = (dm * sa).astype(BF16), (dm * ss).astype(BF16)
        d_o = lax.dot_general(dya, w_att, _NT, preferred_element_type=F32)
        dz = lax.dot_general(dys, w_ssm, _NT, preferred_element_type=F32)
        ge, t = _gelu_tanh(yv)
        sg = _sigmoid(jnp.dot(ge.astype(BF16), wg, preferred_element_type=F32) + bg)
        dpre = dz * ge * sg * (1.0 - sg)
        dge = dz * sg + lax.dot_general(dpre.astype(BF16), wg, _NT, preferred_element_type=F32)
        k = math.sqrt(2.0 / math.pi)
        dgelu = 0.5 * (1.0 + t) + 0.5 * yv * (1.0 - t * t) * k * (1.0 + 3.0 * 0.044715 * yv * yv)
        dy = dge * dgelu
        dwg = lax.dot_general(ge.astype(BF16), dpre.astype(BF16), _TN, preferred_element_type=F32)
        return (dya, dys, jnp.concatenate([dga, dgs], axis=1), d_o, dy, dy * dsk,
                _col_sum(dga), _col_sum(dgs), dwg, _col_sum(dpre), _col_sum(dy * usv))

    (d_y_att, d_y_ssm, d_gates, d_o_att, d_y_s5, d_us_skip, d_bga, d_bgs, d_w_glu, d_b_glu, d_d_skip) = _rowwise(
        f_mixer_tail_bwd,
        [(d_mix, D, 0), (o_att, ATT_WIDTH, 0), (z, SSM_WIDTH, 0), (gates, D, 0), (gates, D, 1),
         (y_s5, SSM_WIDTH, 0), (us, SSM_WIDTH, 0)], [],
        [W["w_out"], W["w_proj_att"], W["w_proj_ssm"], bga, bgs, P["d_skip"], W["w_glu"], P["b_glu"]],
        [(D, BF16), (D, BF16), (2 * D, BF16), (ATT_WIDTH, F32), (SSM_WIDTH, BF16), (SSM_WIDTH, F32)], [],
        [(1, D), (1, D), (SSM_WIDTH, SSM_WIDTH), (1, SSM_WIDTH), (1, SSM_WIDTH)], ts=TS, raw=(0, 1, 2),
        name="mixer_tail_bwd")

    d_yaf, d_ysf = flat(d_y_att), flat(d_y_ssm)
    d_w_proj_att = _matmul(flat(o_att), d_yaf, ta=True, out_dtype=BF16, name="proj_att_dw")
    d_w_proj_ssm = _matmul(flat(z), d_ysf, ta=True, out_dtype=BF16, name="proj_ssm_dw")
    d_us_parts, g_ab, d_bb, d_cc = _scan_bwd(d_y_s5, us, P["bb_big"], P["cc_big"], xs, P["a_row"])

    token, _ = send_early(dict(
        w_out=d_w_out.reshape(N_DEV, D // N_DEV, D), w_proj_att=_cols_to_slots(d_w_proj_att),
        w_proj_ssm=_cols_to_slots(d_w_proj_ssm),
        w_glu=d_w_glu.astype(BF16).reshape(N_DEV, SSM_WIDTH // N_DEV, SSM_WIDTH),
        w_conv=_cols_to_slots(d_w_conv.astype(BF16))))
    d_qkv = _attention_bwd(qkv, o_att, d_o_att, lse, P["slopes"] + token[0, 0])

    def f_add(*parts):
        return sum(parts[1:], parts[0])

    n_parts = d_us_parts.shape[0]
    stacked = d_us_parts.reshape(n_parts * B, S, SSM_WIDTH)
    (d_us,) = _rowwise(f_add, [(d_us_skip, SSM_WIDTH, 0)] + [(stacked, SSM_WIDTH, 0, j * B) for j in range(n_parts)],
                       [], [],
                       [(SSM_WIDTH, BF16)], [], [], ts=TS, name="s5_input_grad")
    d_qkvf = flat(d_qkv)
    d_usf = flat(d_us)
    d_gatesf = flat(d_gates)
    d_w_in_t = jnp.concatenate(
        [_unpair_qkv_rows(_matmul(d_qkvf, u1f, ta=True, out_dtype=BF16, name="proj_qkv_dw")),
         _matmul(d_usf, u1f, ta=True, out_dtype=BF16, name="proj_ssm_in_dw"),
         _matmul(d_gatesf, u1f, ta=True, out_dtype=BF16, name="proj_gates_dw")], axis=0)
    token, (w_qkv, w_us, w_gates) = send_early(dict(w_in=d_w_in_t.reshape(N_DEV, -1, D)),
                                               carry=[W["w_qkv"], W["w_us"], W["w_gates"]])
    def f_project_back_modnorm(dq, du_, dg, h, dres, sc, g, wq, wu, wg):
        du = (jnp.dot(dq, wq, preferred_element_type=F32) + jnp.dot(du_, wu, preferred_element_type=F32)
              + jnp.dot(dg, wg, preferred_element_type=F32))
        r = _rms_scale(h)
        nh = h * r
        dn = du * (1.0 + sc)
        gy = dn * g
        dh = dres + r * (gy - nh * jnp.mean(gy * nh, axis=-1, keepdims=True))
        return (dh, _col_sum(du), _col_sum(du * nh * g), _col_sum(dn * nh))

    grad_x, d_sh1, d_sc1, d_g_mix = _rowwise(
        f_project_back_modnorm,
        [(d_qkv, 3 * ATT_WIDTH, 0), (d_us, SSM_WIDTH, 0), (d_gates, 2 * D, 0), (x, D, 0), (dh1, D, 0)], [mod_col(1)],
        [P["g_mix"] + token[0:1, 0:1], w_qkv, w_us, w_gates],
        [(D, F32)], [D, D], [(1, D)], ts=TS, raw=(0, 1, 2), name="project_in_back_modnorm")

    d_mod = jnp.concatenate([d_sh1, d_sc1, d_gt1, d_sh2, d_sc2, d_gt2], axis=-1)
    g_ab_re, g_ab_im = _deinterleave(g_ab)
    d_bb_re, d_bb_im = _deinterleave(d_bb)
    d_cc_re, d_cc_im = (t.T for t in _deinterleave(d_cc.T))
    small = dict(g_mix=d_g_mix, b_gate=jnp.concatenate([d_bga, d_bgs], axis=1), g_ab_re=g_ab_re, g_ab_im=g_ab_im,
                 d_bb_re=d_bb_re, d_bb_im=d_bb_im, d_cc_re=d_cc_re, d_cc_im=d_cc_im, d_skip=d_d_skip,
                 b_glu=d_b_glu, g_ffn=d_g_ffn, b_conv=d_b_conv, g_final=d_g_final, loss_cols=loss_cols)
    return grad_x, d_mod, small


def _block_diag_in(bb):
    t = bb.reshape(SSM_GROUPS, SSM_STATE, SSM_GROUP_CH)
    eye = jnp.eye(SSM_GROUPS, dtype=bb.dtype)
    return jnp.einsum("gnc,gh->gchn", t, eye).reshape(SSM_WIDTH, SSM_COLS)


def _block_diag_out(cm):
    eye = jnp.eye(SSM_GROUPS, dtype=cm.dtype)
    return jnp.einsum("gcn,gh->gnhc", cm, eye).reshape(SSM_COLS, SSM_WIDTH)


def _diag_blocks_in(m):
    t = m.reshape(SSM_GROUPS, SSM_GROUP_CH, SSM_GROUPS, SSM_STATE)
    idx = jnp.arange(SSM_GROUPS)
    return t[idx, :, idx, :].transpose(0, 2, 1).reshape(SSM_COLS, SSM_GROUP_CH)


def _diag_blocks_out(m):
    t = m.reshape(SSM_GROUPS, SSM_STATE, SSM_GROUPS, SSM_GROUP_CH)
    idx = jnp.arange(SSM_GROUPS)
    return t[idx, :, idx, :].transpose(0, 2, 1)


def _pair_qkv_rows(w):
    return w.reshape(3, N_HEADS // 2, LANES, w.shape[1]).swapaxes(0, 1).reshape(w.shape)


def _unpair_qkv_rows(w):
    return w.reshape(N_HEADS // 2, 3, LANES, w.shape[1]).swapaxes(0, 1).reshape(w.shape)


def _interleave(re, im):
    lead = re.shape[:-1]
    g = lambda a: a.reshape(lead + (SSM_COLS // SCAN_COLS, 1, SCAN_COLS))
    return jnp.concatenate([g(re), g(im)], axis=-2).reshape(lead + (2 * SSM_COLS,))


def _deinterleave(x):
    lead = x.shape[:-1]
    t = x.reshape(lead + (SSM_COLS // SCAN_COLS, 2, SCAN_COLS))
    return t[..., 0, :].reshape(lead + (SSM_COLS,)), t[..., 1, :].reshape(lead + (SSM_COLS,))


def _cols_to_slots(g):
    R = g.shape[0]
    return g.reshape(R, N_DEV, g.shape[1] // N_DEV).transpose(1, 0, 2)


def _slots_to_cols(g):
    return g.transpose(1, 0, 2).reshape(g.shape[1], N_DEV * g.shape[2])


SMALL_ORDER = ("b_ada", "g_mix", "b_gate", "a_re", "a_im", "log_dt", "b_re", "b_im", "c_re", "c_im", "d_skip",
               "b_glu", "g_ffn", "b_conv", "g_final")


def _pack(arrs):
    pieces, offs, row = [], [], 0
    for a in arrs:
        f = a.reshape(-1).astype(F32)
        n = f.shape[0]
        rows = -(-n // LANES)
        pieces.append(jnp.pad(f, (0, rows * LANES - n)))
        offs.append((row, n))
        row += rows
    return jnp.concatenate(pieces).reshape(row, LANES), offs


def _unpack(packed, offs, shapes):
    flat = packed.reshape(-1)
    return [flat[r * LANES:r * LANES + n].reshape(s) for (r, n), s in zip(offs, shapes)]


def kernel(x, c, w_ada, b_ada, g_mix, w_in, b_gate, a_re, a_im, log_dt, b_re, b_im, c_re, c_im, d_skip, w_glu, b_glu, w_proj_att, w_proj_ssm, w_out, g_ffn, w_up, w_conv, b_conv, w_down, g_final, loss_target, m_w_ada, m_b_ada, m_g_mix, m_w_in, m_b_gate, m_a_re, m_a_im, m_log_dt, m_b_re, m_b_im, m_c_re, m_c_im, m_d_skip, m_w_glu, m_b_glu, m_w_proj_att, m_w_proj_ssm, m_w_out, m_g_ffn, m_w_up, m_w_conv, m_b_conv, m_w_down, m_g_final, v_w_ada, v_b_ada, v_g_mix, v_w_in, v_b_gate, v_a_re, v_a_im, v_log_dt, v_b_re, v_b_im, v_c_re, v_c_im, v_d_skip, v_w_glu, v_b_glu, v_w_proj_att, v_w_proj_ssm, v_w_out, v_g_ffn, v_w_up, v_w_conv, v_b_conv, v_w_down, v_g_final):
    args = dict(locals())
    B, S, D = x.shape
    me = 4 * lax.axis_index("x") + 2 * lax.axis_index("y") + lax.axis_index("c")
    bf = lambda w: w[0].astype(BF16)

    c_slots, w_in_slots = _gather_all([c, w_in[0].T.astype(BF16)], name="gather_first_weights")
    c_all = c_slots.reshape(N_DEV * B, D)
    w_in_t = w_in_slots.reshape(-1, D)
    n_qkv = 3 * ATT_WIDTH
    W = dict(w_qkv=_pair_qkv_rows(w_in_t[:n_qkv]), w_us=w_in_t[n_qkv:n_qkv + SSM_WIDTH],
             w_gates=w_in_t[n_qkv + SSM_WIDTH:])

    n_ada = w_ada.shape[2]
    b_ada_cols = lax.dynamic_slice(b_ada, (0, me * n_ada), (1, n_ada))
    mod_part = _ada_fwd(c_all, w_ada[0], b_ada_cols)
    (mod_slots,) = _exchange([(mod_part.reshape(N_DEV, B, n_ada), True)], name="scatter_modulation")
    mod = mod_slots.transpose(1, 0, 2).reshape(B, 1, 6 * D)

    later = [bf(w_glu), bf(w_proj_att), bf(w_proj_ssm), bf(w_out), bf(w_up), w_conv[0], bf(w_down)]
    later_sems = _exchange_start(later, "start_later_weights", gather=True, carry=[mod])
    (mod,) = later_sems[5]

    def late_weights(after):
        _, lands = _exchange_wait(*later_sems[:4], after, name="wait_later_weights")
        g = [lax.dynamic_update_index_in_dim(land, a, me, 0) for land, a in zip(lands, later)]
        more_w = dict(w_glu=g[0].reshape(SSM_WIDTH, SSM_WIDTH), w_proj_att=_slots_to_cols(g[1]),
                      w_proj_ssm=_slots_to_cols(g[2]), w_out=g[3].reshape(D, D), w_up=g[4],
                      w_down=g[6].reshape(D_FF, D))
        return more_w, dict(w_conv=_slots_to_cols(g[5]))

    ab_re, ab_im, f_re, f_im = _s5_params(a_re[0], a_im[0], log_dt[0].reshape(SSM_GROUPS, 1))
    col = lambda a: a.reshape(SSM_COLS, 1)
    b_re2, b_im2 = b_re[0].reshape(SSM_COLS, SSM_GROUP_CH), b_im[0].reshape(SSM_COLS, SSM_GROUP_CH)
    bb_re, bb_im = _s5_input_matrix(col(f_re), col(f_im), b_re2, b_im2)
    slopes = jnp.asarray([2.0 ** (-8.0 * (h + 1) / N_HEADS) for h in range(N_HEADS)], F32)
    P = dict(g_mix=g_mix, g_ffn=g_ffn, g_final=g_final.reshape(1, D), b_gate=b_gate, d_skip=d_skip, b_glu=b_glu,
             b_conv=b_conv, slopes=slopes,
             a_row=_interleave(ab_re.reshape(1, SSM_COLS), ab_im.reshape(1, SSM_COLS)),
             bb_big=_interleave(_block_diag_in(bb_re), _block_diag_in(bb_im)),
             cc_big=_interleave(_block_diag_out(c_re[0]).T, -_block_diag_out(c_im[0]).T).T)

    in_flight = []

    def send_early(grads, carry=()):
        names = list(grads)
        handles = _exchange_start([grads[n] for n in names], "start_gradients_%d" % len(in_flight), gather=False,
                                  carry=carry)
        in_flight.append((names,) + handles[:4])
        return handles[4], handles[5]

    grad_x, d_mod, small = _local_step(x, mod, loss_target, W, late_weights, P, send_early)

    small_list = [small["loss_cols"], small["g_mix"], small["b_gate"], small["g_ab_re"], small["g_ab_im"],
                  _diag_blocks_in(small["d_bb_re"]), _diag_blocks_in(small["d_bb_im"]),
                  _diag_blocks_out(small["d_cc_re"]), -_diag_blocks_out(small["d_cc_im"]),
                  small["g_ffn"], small["b_conv"], small["g_final"], small["d_skip"], small["b_glu"]]
    small_packed, small_offs = _pack(small_list)
    small_all, dmod_slots = _gather_all([small_packed, d_mod.reshape(B, 6 * D)], name="gather_small_gradients")

    out = {}

    def update(name, parts, own=None):
        view = (lambda a: a[0].T) if name == "w_in" else (lambda a: a[0])
        back = (lambda a: a.T[None]) if name == "w_in" else (lambda a: a[None])
        g, dl, mn, vn = _adamw(view(args[name]), view(args["m_" + name]), view(args["v_" + name]), parts,
                               name="adamw_" + name, own=own)
        for key, val in (("grad_", g), ("delta_", dl), ("new_m_", mn), ("new_v_", vn)):
            out[key + name] = back(val)

    my_slot = me.astype(jnp.int32).reshape(1)
    for i, (names, send_sems, recv_sems, sent, lands) in enumerate(in_flight):
        sent, lands = _exchange_wait(send_sems, recv_sems, sent, lands, dmod_slots, name="wait_gradients_%d" % i)
        for name, own_slots, landed in zip(names, sent, lands):
            update(name, landed, own=(own_slots, my_slot))

    dmod_all = dmod_slots.reshape(N_DEV * B, 6 * D)
    dmod_cols = lax.dynamic_slice(dmod_all, (0, me * n_ada), (N_DEV * B, n_ada))
    d_w_ada, d_b_ada = _ada_bwd(c_all, dmod_all, dmod_cols)
    update("w_ada", d_w_ada[None])

    loss_row, loss_n = small_offs[0]
    small_sum, loss_vec = _sum_parts(small_all, (loss_row, loss_row + loss_n // LANES))
    shapes = [(1, D), (1, D), (1, 2 * D), (SSM_GROUPS, SSM_STATE), (SSM_GROUPS, SSM_STATE), (SSM_COLS, SSM_GROUP_CH),
              (SSM_COLS, SSM_GROUP_CH), (1, SSM_GROUPS, SSM_GROUP_CH, SSM_STATE),
              (1, SSM_GROUPS, SSM_GROUP_CH, SSM_STATE), (1, D), (1, D_FF), (D,), (1, SSM_WIDTH), (1, SSM_WIDTH)]
    (_, s_g_mix, s_b_gate, s_ab_re, s_ab_im, s_bb_re, s_bb_im, s_c_re, s_c_im, s_g_ffn, s_b_conv, s_g_final,
     s_d_skip, s_b_glu) = _unpack(small_sum, small_offs, shapes)
    d_b_re2, d_b_im2, d_f_re, d_f_im = _s5_input_matrix_bwd(col(f_re), col(f_im), b_re2, b_im2, s_bb_re, s_bb_im)
    d_a_re, d_a_im, d_log_dt = _s5_params_bwd(a_re[0], a_im[0], log_dt[0].reshape(SSM_GROUPS, 1), s_ab_re, s_ab_im,
                                              d_f_re.reshape(SSM_GROUPS, SSM_STATE),
                                              d_f_im.reshape(SSM_GROUPS, SSM_STATE))
    grads_small = dict(b_ada=d_b_ada, g_mix=s_g_mix, b_gate=s_b_gate, a_re=d_a_re[None], a_im=d_a_im[None],
                       log_dt=d_log_dt.reshape(1, SSM_GROUPS), b_re=d_b_re2.reshape(b_re.shape),
                       b_im=d_b_im2.reshape(b_im.shape), c_re=s_c_re, c_im=s_c_im, d_skip=s_d_skip, b_glu=s_b_glu,
                       g_ffn=s_g_ffn, b_conv=s_b_conv, g_final=s_g_final)
    flat2 = lambda a: a.reshape(-1, a.shape[-1])
    res = _adamw_small([flat2(args[n]) for n in SMALL_ORDER], [flat2(args["m_" + n]) for n in SMALL_ORDER],
                       [flat2(args["v_" + n]) for n in SMALL_ORDER],
                       [flat2(grads_small[n].reshape(args[n].shape)) for n in SMALL_ORDER])
    for i, n in enumerate(SMALL_ORDER):
        for k, key in enumerate(("grad_", "delta_", "new_m_", "new_v_")):
            out[key + n] = res[4 * i + k].reshape(args[n].shape)

    order = ["w_ada", "b_ada", "g_mix", "w_in", "b_gate", "a_re", "a_im", "log_dt", "b_re", "b_im", "c_re", "c_im",
             "d_skip", "w_glu", "b_glu", "w_proj_att", "w_proj_ssm", "w_out", "g_ffn", "w_up", "w_conv", "b_conv",
             "w_down", "g_final"]
    loss = loss_vec[0, 0]
    return (loss, grad_x, *[out[k + n] for k in ("grad_", "delta_", "new_m_", "new_v_") for n in order])
```

```python
import math

import jax
import jax.numpy as jnp
from jax import lax
from jax.experimental import pallas as pl
from jax.experimental.pallas import tpu as pltpu

F32 = jnp.float32
BF16 = jnp.bfloat16

N_DEV = 8
D_MODEL = 1024
N_HEADS = 8
HEAD_DIM = 64
ATT_WIDTH = N_HEADS * HEAD_DIM
DILATIONS = (1, 4, 16)
WIN = 128
SSM_GROUPS = 16
SSM_GROUP_CH = 16
SSM_WIDTH = SSM_GROUPS * SSM_GROUP_CH
SSM_STATE = 64
SSM_COLS = SSM_GROUPS * SSM_STATE
D_FF = 2048
EPS = 1e-6
NEG_INF = -1e30
ADAM_LR, ADAM_B1, ADAM_B2, ADAM_EPS, ADAM_WD, ADAM_STEP = 0.001, 0.9, 0.999, 1e-08, 0.01, 10

V7X_VMEM_LIMIT = 56 * 1024 * 1024
LANES = 128


def _params(n_grid):
    return pltpu.CompilerParams(dimension_semantics=("arbitrary",) * n_grid,
                                vmem_limit_bytes=V7X_VMEM_LIMIT)


def _tile(n, pref):
    if n <= pref:
        return n
    t = (pref // LANES) * LANES
    while t > 0:
        if n % t == 0:
            return t
        t -= LANES
    return n


def _matmul(a, b, *, ta=False, tb=False, out_dtype=F32, name):
    if ta:
        K, M = a.shape
    else:
        M, K = a.shape
    if tb:
        N, K2 = b.shape
    else:
        K2, N = b.shape
    assert K == K2, (a.shape, b.shape)
    if ta:
        tm, tn, tk = _tile(M, 1024), _tile(N, 2048), _tile(K, 1024)
    else:
        tm, tk = _tile(M, 512), _tile(K, 4096)
        tn = _tile(N, 2048 if K <= 2048 else 1024)
    nk = K // tk
    dn = (((0,) if ta else (1,), (1,) if tb else (0,)), ((), ()))

    def body(a_ref, b_ref, o_ref, acc_ref):
        k = pl.program_id(2)
        part = lax.dot_general(a_ref[...].astype(BF16), b_ref[...].astype(BF16), dn, preferred_element_type=F32)
        if nk == 1:
            o_ref[...] = part.astype(o_ref.dtype)
            return

        @pl.when(k == 0)
        def _():
            acc_ref[...] = jnp.zeros_like(acc_ref)

        acc_ref[...] += part

        @pl.when(k == nk - 1)
        def _():
            o_ref[...] = acc_ref[...].astype(o_ref.dtype)

    a_spec = (pl.BlockSpec((tk, tm), lambda j, i, k: (k, i)) if ta
              else pl.BlockSpec((tm, tk), lambda j, i, k: (i, k)))
    b_spec = (pl.BlockSpec((tn, tk), lambda j, i, k: (j, k)) if tb
              else pl.BlockSpec((tk, tn), lambda j, i, k: (k, j)))
    return pl.pallas_call(
        body, name=name, grid=(N // tn, M // tm, nk),
        in_specs=[a_spec, b_spec],
        out_specs=pl.BlockSpec((tm, tn), lambda j, i, k: (i, j)),
        out_shape=jax.ShapeDtypeStruct((M, N), out_dtype),
        scratch_shapes=[pltpu.VMEM((tm, tn) if nk > 1 else (8, LANES), F32)],
        compiler_params=_params(3),
    )(a, b)


HALF = 256
UP_SLOTS = N_DEV // 2
UP_GROUP = 4 * HALF


def _group_weight(w_ref):
    return jnp.concatenate([w_ref[0, :, :HALF], w_ref[1, :, :HALF], w_ref[0, :, HALF:], w_ref[1, :, HALF:]], axis=1)


def _up_weight_spec(K, index):
    return pl.BlockSpec((2, None, K, 2 * HALF), index)


def _up_fwd(a, w3, name):
    M, K = a.shape
    tm = _tile(M, 1024)

    def body(a_ref, w_ref, o_ref):
        o_ref[...] = jnp.dot(a_ref[...].astype(BF16), _group_weight(w_ref),
                             preferred_element_type=F32).astype(o_ref.dtype)

    return pl.pallas_call(
        body, name=name, grid=(UP_SLOTS, M // tm),
        in_specs=[pl.BlockSpec((tm, K), lambda j, i: (i, 0)), _up_weight_spec(K, lambda j, i: (0, j, 0, 0))],
        out_specs=pl.BlockSpec((tm, UP_GROUP), lambda j, i: (i, j)),
        out_shape=jax.ShapeDtypeStruct((M, UP_SLOTS * UP_GROUP), BF16), compiler_params=_params(2),
    )(a, w3.reshape(2, UP_SLOTS, K, 2 * HALF))


def _up_dw(a, d, name):
    M, K = a.shape
    tk = _tile(M, 1024)
    nk = M // tk

    def body(a_ref, d_ref, o_ref, acc_ref):
        k = pl.program_id(1)

        @pl.when(k == 0)
        def _():
            acc_ref[...] = jnp.zeros_like(acc_ref)

        acc_ref[...] += lax.dot_general(a_ref[...], d_ref[...], _TN, preferred_element_type=F32)

        @pl.when(k == nk - 1)
        def _():
            for half in range(2):
                for part in range(2):
                    lo = (2 * half + part) * HALF
                    o_ref[part, :, half * HALF:(half + 1) * HALF] = acc_ref[:, lo:lo + HALF].astype(o_ref.dtype)

    out = pl.pallas_call(
        body, name=name, grid=(UP_SLOTS, nk),
        in_specs=[pl.BlockSpec((tk, K), lambda j, k: (k, 0)), pl.BlockSpec((tk, UP_GROUP), lambda j, k: (k, j))],
        out_specs=_up_weight_spec(K, lambda j, k: (0, j, 0, 0)),
        out_shape=jax.ShapeDtypeStruct((2, UP_SLOTS, K, 2 * HALF), BF16),
        scratch_shapes=[pltpu.VMEM((K, UP_GROUP), F32)], compiler_params=_params(2),
    )(a, d)
    return out.reshape(N_DEV, K, 2 * HALF)


def _rowwise(fn, rows, bvecs, consts, out_rows, out_b, out_g, *, ts, name, raw=()):
    B, S = rows[0][0].shape[:2]
    nin = len(rows) + len(bvecs) + len(consts)
    nr, nb, ng = len(out_rows), len(out_b), len(out_g)

    def body(*refs):
        b = pl.program_id(0)
        s = pl.program_id(1)
        vals = [r[...] for r in refs[:nin]]
        vals[:len(rows)] = [v if i in raw else v.astype(F32) for i, v in enumerate(vals[:len(rows)])]
        outs = fn(*vals)
        if not isinstance(outs, (tuple, list)):
            outs = (outs,)
        orefs = refs[nin:]
        for i in range(nr):
            orefs[i][...] = outs[i].astype(orefs[i].dtype)
        for i in range(nb):
            ref = orefs[nr + i]

            @pl.when(s == 0)
            def _(ref=ref):
                ref[...] = jnp.zeros_like(ref)

            ref[...] += outs[nr + i]
        for i in range(ng):
            ref = orefs[nr + nb + i]

            @pl.when((s == 0) & (b == 0))
            def _(ref=ref):
                ref[...] = jnp.zeros_like(ref)

            ref[...] += outs[nr + nb + i]

    rows = [r if len(r) == 4 else r + (0,) for r in rows]
    in_specs = ([pl.BlockSpec((None, ts, cb), lambda b, s, ci=ci, b0=b0: (b0 + b, s, ci)) for (_, cb, ci, b0) in rows]
                + [pl.BlockSpec((None, 1, cb), lambda b, s, ci=ci: (b, 0, ci)) for (_, cb, ci) in bvecs]
                + [pl.BlockSpec(a.shape, lambda b, s: (0, 0)) for a in consts])
    out_shape = ([jax.ShapeDtypeStruct((B, S, c), dt) for (c, dt) in out_rows]
                 + [jax.ShapeDtypeStruct((B, 1, c), F32) for c in out_b]
                 + [jax.ShapeDtypeStruct(rc, F32) for rc in out_g])
    out_specs = ([pl.BlockSpec((None, ts, c), lambda b, s: (b, s, 0)) for (c, _) in out_rows]
                 + [pl.BlockSpec((None, 1, c), lambda b, s: (b, 0, 0)) for c in out_b]
                 + [pl.BlockSpec(rc, lambda b, s: (0, 0)) for rc in out_g])
    args = [r[0] for r in rows] + [a for (a, _, _) in bvecs] + list(consts)
    return pl.pallas_call(
        body, name=name, grid=(B, S // ts), in_specs=in_specs, out_specs=out_specs,
        out_shape=out_shape, compiler_params=_params(2),
    )(*args)


def _col_sum(v):
    return jnp.sum(v, axis=0, keepdims=True)


def _rms_scale(h):
    return lax.rsqrt(jnp.mean(h * h, axis=-1, keepdims=True) + EPS)


def _sigmoid(v):
    return 0.5 * (1.0 + jnp.tanh(0.5 * v))


ATT_SCALE = HEAD_DIM ** -0.5
COPY_ROWS = 256
_NT = (((1,), (1,)), ((), ()))
_TN = (((0,), (0,)), ((), ()))


def _row_chunks(d, seq):
    sub = seq // d
    out = []
    for r in range(d):
        for c0 in range(0, sub, COPY_ROWS):
            n = min(COPY_ROWS, sub - c0)
            out.append((pl.ds(r + c0 * d, n, stride=d), r * sub + c0, n))
    return out


ATT_UNROLL = 16
KEYS = 2 * WIN


def _zero_once(refs):
    @pl.when((pl.program_id(0) == 0) & (pl.program_id(1) == 0))
    def _():
        for r in refs:
            r[...] = jnp.zeros_like(r)


def _pair_bias(bias_ref, slopes_ref, hp, d, key_major):
    shape = (KEYS, WIN) if key_major else (WIN, KEYS)
    qi = lax.broadcasted_iota(jnp.int32, shape, 1 if key_major else 0)
    kj = lax.broadcasted_iota(jnp.int32, shape, 0 if key_major else 1)
    dist = WIN + qi - kj
    valid = (dist >= 0) & (dist <= WIN)
    distf = dist.astype(F32)
    for h in range(2):
        slope_d = slopes_ref[2 * hp + h] * float(d)
        with_prev = jnp.where(valid, -(slope_d * distf), NEG_INF)
        no_prev = jnp.where(kj >= WIN, with_prev, NEG_INF)
        span = slice(h * KEYS, (h + 1) * KEYS)
        if key_major:
            bias_ref[1, span, :] = with_prev
            bias_ref[0, span, :] = no_prev
        else:
            bias_ref[1, :, span] = with_prev
            bias_ref[0, :, span] = no_prev


def _stack_heads(v):
    first = lax.broadcasted_iota(jnp.int32, v.shape, 1) < HEAD_DIM
    zero = jnp.zeros_like(v)
    return jnp.concatenate([jnp.where(first, v, zero), jnp.where(first, zero, v)], axis=0)


def _per_head(c0, c1, n):
    return jnp.where(lax.broadcasted_iota(jnp.int32, (n, LANES), 1) < HEAD_DIM, c0, c1)


def _qkv_spec(seq, j):
    return pl.BlockSpec((None, seq, LANES), lambda b, hp: (b, 0, 3 * hp + j))


def _attention_fwd(qkv, slopes):
    B, S, _ = qkv.shape
    n_blk = S // WIN
    n_pair = N_HEADS // 2

    def body(slopes_ref, q_ref, k_ref, v_ref, o_ref, lse_ref, qp, kp, vp, bias, acc, mx, sm, acc_n, mx_n, sm_n):
        hp = pl.program_id(1)
        _zero_once((kp, vp))
        for p, d in enumerate(DILATIONS):
            nb = n_blk // d
            chunks = _row_chunks(d, S)
            for src, dst, n in chunks:
                qp[dst:dst + n, :] = (q_ref[src, :] * ATT_SCALE).astype(BF16)
                kp[WIN + dst:WIN + dst + n, :] = k_ref[src, :].astype(BF16)
                vp[WIN + dst:WIN + dst + n, :] = v_ref[src, :].astype(BF16)
            _pair_bias(bias, slopes_ref, hp, d, key_major=False)
            acc_t, mx_t, sm_t = (acc_n, mx_n, sm_n) if d == 1 else (acc, mx, sm)

            nk = WIN if nb == 1 else KEYS
            bias_cur = jnp.concatenate([bias[0, :, WIN:KEYS], bias[0, :, KEYS + WIN:]], axis=1) if nb == 1 else None

            def block(i, carry, p=p, nb=nb, nk=nk, bias_cur=bias_cur, acc_t=acc_t, mx_t=mx_t, sm_t=sm_t):
                cur = pl.ds(pl.multiple_of(i * WIN, WIN), WIN)
                keys = pl.ds(pl.multiple_of(i * WIN + (KEYS - nk), WIN), nk)
                s = lax.dot_general(qp[cur, :], _stack_heads(kp[keys, :]), _NT, preferred_element_type=F32)
                s = s + (bias_cur if nb == 1 else bias[((i % nb) > 0).astype(jnp.int32)])
                es, ms, ls = [], [], []
                for h in range(2):
                    sh = s[:, h * nk:(h + 1) * nk]
                    m = jnp.max(sh if nb == 1 else jnp.maximum(sh[:, :WIN], sh[:, WIN:]), axis=1, keepdims=True)
                    e = jnp.exp(sh - m)
                    es.append(e.astype(BF16))
                    ms.append(m)
                    ls.append(jnp.sum(e if nb == 1 else e[:, :WIN] + e[:, WIN:], axis=1, keepdims=True))
                acc_t[p, cur, :] = jnp.dot(jnp.concatenate(es, axis=1), _stack_heads(vp[keys, :]),
                                           preferred_element_type=F32)
                mx_t[p, cur, :] = _per_head(ms[0], ms[1], WIN)
                sm_t[p, cur, :] = _per_head(ls[0], ls[1], WIN)
                return carry

            lax.fori_loop(0, n_blk, block, 0, unroll=ATT_UNROLL)
            if d > 1:
                for src, dst, n in chunks:
                    acc_n[p, src, :] = acc[p, dst:dst + n, :]
                    mx_n[p, src, :] = mx[p, dst:dst + n, :]
                    sm_n[p, src, :] = sm[p, dst:dst + n, :]

        chunk = 256

        def merge(i, carry):
            rows = pl.ds(pl.multiple_of(i * chunk, chunk), chunk)
            ms = [mx_n[p, rows, :] for p in range(3)]
            m = jnp.maximum(jnp.maximum(ms[0], ms[1]), ms[2])
            ws = [jnp.exp(mp - m) for mp in ms]
            l = ws[0] * sm_n[0, rows, :] + ws[1] * sm_n[1, rows, :] + ws[2] * sm_n[2, rows, :]
            o = (ws[0] * acc_n[0, rows, :] + ws[1] * acc_n[1, rows, :] + ws[2] * acc_n[2, rows, :]) / l
            o_ref[rows, :] = o.astype(o_ref.dtype)
            lse = m + jnp.log(l)
            for h in range(2):
                lse_ref[rows, h:h + 1] = lse[:, h * HEAD_DIM:h * HEAD_DIM + 1]
            return carry

        lax.fori_loop(0, S // chunk, merge, 0)

    return pl.pallas_call(
        body, name="attention_fwd", grid=(B, n_pair),
        in_specs=[pl.BlockSpec(memory_space=pltpu.SMEM), _qkv_spec(S, 0), _qkv_spec(S, 1), _qkv_spec(S, 2)],
        out_specs=[pl.BlockSpec((None, S, LANES), lambda b, hp: (b, 0, hp)),
                   pl.BlockSpec((None, None, S, 2), lambda b, hp: (b, hp, 0, 0))],
        out_shape=[jax.ShapeDtypeStruct((B, S, ATT_WIDTH), BF16),
                   jax.ShapeDtypeStruct((B, n_pair, S, 2), F32)],
        scratch_shapes=[pltpu.VMEM((S, LANES), BF16), pltpu.VMEM((S + WIN, LANES), BF16),
                        pltpu.VMEM((S + WIN, LANES), BF16), pltpu.VMEM((2, WIN, 2 * KEYS), F32)]
        + [pltpu.VMEM((3, S, LANES), F32)] * 6,
        compiler_params=_params(2),
    )(slopes, qkv, qkv, qkv)


def _attention_bwd(qkv, o, do, lse, slopes):
    B, S, _ = qkv.shape
    n_blk = S // WIN
    n_pair = N_HEADS // 2

    def body(slopes_ref, q_ref, k_ref, v_ref, o_ref, do_ref, lse_ref, dx_ref,
             qp, dop, kp, vp, aux, auxp, aux_t, bias_t, dqp, dvk, dq_n, dk_n, dv_n):
        hp = pl.program_id(1)
        aux[...] = jnp.zeros_like(aux)
        for c0 in range(0, S, COPY_ROWS):
            rows = slice(c0, c0 + COPY_ROWS)
            prod = do_ref[rows, :] * o_ref[rows, :].astype(F32)
            for h in range(2):
                aux[rows, 2 * h:2 * h + 1] = lse_ref[rows, h:h + 1]
                aux[rows, 2 * h + 1:2 * h + 2] = jnp.sum(prod[:, h * HEAD_DIM:(h + 1) * HEAD_DIM], axis=1,
                                                         keepdims=True)
        dq_n[...] = jnp.zeros_like(dq_n)
        dk_n[...] = jnp.zeros_like(dk_n)
        dv_n[...] = jnp.zeros_like(dv_n)
        _zero_once((kp, vp))
        for p, d in enumerate(DILATIONS):
            nb = n_blk // d
            chunks = _row_chunks(d, S)
            for src, dst, n in chunks:
                auxp[dst:dst + n, :] = aux[src, :]
                qp[dst:dst + n, :] = (q_ref[src, :] * ATT_SCALE).astype(BF16)
                dop[dst:dst + n, :] = do_ref[src, :].astype(BF16)
                kp[WIN + dst:WIN + dst + n, :] = k_ref[src, :].astype(BF16)
                vp[WIN + dst:WIN + dst + n, :] = v_ref[src, :].astype(BF16)
            for i in range(n_blk):
                aux_t[i] = auxp[i * WIN:(i + 1) * WIN, :].T[0:8, :]
            _pair_bias(bias_t, slopes_ref, hp, d, key_major=True)
            dvk[...] = jnp.zeros_like(dvk)

            nk = WIN if nb == 1 else KEYS
            bias_cur = jnp.concatenate([bias_t[0, WIN:KEYS, :], bias_t[0, KEYS + WIN:, :]], axis=0) if nb == 1 else None

            def block(i, carry, nb=nb, nk=nk, bias_cur=bias_cur):
                cur = pl.ds(pl.multiple_of(i * WIN, WIN), WIN)
                keys = pl.ds(pl.multiple_of(i * WIN + (KEYS - nk), WIN), nk)
                q2, do2 = qp[cur, :], dop[cur, :]
                kc = _stack_heads(kp[keys, :])
                s_t = lax.dot_general(kc, q2, _NT, preferred_element_type=F32)
                s_t = s_t + (bias_cur if nb == 1 else bias_t[((i % nb) > 0).astype(jnp.int32)])
                dp_t = lax.dot_general(_stack_heads(vp[keys, :]), do2, _NT, preferred_element_type=F32)
                ps, dss = [], []
                for h in range(2):
                    span = slice(h * nk, (h + 1) * nk)
                    p_t = jnp.exp(s_t[span, :] - aux_t[i, 2 * h:2 * h + 1, :])
                    ds_t = p_t * (dp_t[span, :] - aux_t[i, 2 * h + 1:2 * h + 2, :])
                    ps.append(p_t.astype(BF16))
                    dss.append(ds_t.astype(BF16))
                do_rows, q_rows = _stack_heads(do2), _stack_heads(q2)
                zr = jnp.zeros_like(do_rows)
                rhs = jnp.concatenate([jnp.concatenate([do_rows, zr], axis=1),
                                       jnp.concatenate([zr, q_rows], axis=1)], axis=0)
                dvk[keys, :] += jnp.dot(jnp.concatenate(ps + dss, axis=1), rhs, preferred_element_type=F32)
                dqp[cur, :] = lax.dot_general(jnp.concatenate(dss, axis=0), kc, _TN, preferred_element_type=F32)
                return carry

            lax.fori_loop(0, n_blk, block, 0, unroll=ATT_UNROLL)
            for src, dst, n in chunks:
                dq_n[src, :] += dqp[dst:dst + n, :]
                dv_n[src, :] += dvk[WIN + dst:WIN + dst + n, :LANES]
                dk_n[src, :] += dvk[WIN + dst:WIN + dst + n, LANES:]
        for c0 in range(0, S, COPY_ROWS):
            rows = slice(c0, c0 + COPY_ROWS)
            dx_ref[rows, 0:LANES] = (dq_n[rows, :] * ATT_SCALE).astype(dx_ref.dtype)
            dx_ref[rows, LANES:2 * LANES] = dk_n[rows, :].astype(dx_ref.dtype)
            dx_ref[rows, 2 * LANES:3 * LANES] = dv_n[rows, :].astype(dx_ref.dtype)

    pair = lambda width: pl.BlockSpec((None, S, width), lambda b, hp: (b, 0, hp))
    vm = lambda shape, dt: pltpu.VMEM(shape, dt)
    return pl.pallas_call(
        body, name="attention_bwd", grid=(B, n_pair),
        in_specs=[pl.BlockSpec(memory_space=pltpu.SMEM), _qkv_spec(S, 0), _qkv_spec(S, 1), _qkv_spec(S, 2),
                  pair(LANES), pair(LANES), pl.BlockSpec((None, None, S, 2), lambda b, hp: (b, hp, 0, 0))],
        out_specs=pair(3 * LANES),
        out_shape=jax.ShapeDtypeStruct((B, S, 3 * ATT_WIDTH), BF16),
        scratch_shapes=[vm((S, LANES), BF16), vm((S, LANES), BF16),
                        vm((S + WIN, LANES), BF16), vm((S + WIN, LANES), BF16),
                        vm((S, LANES), F32), vm((S, LANES), F32), vm((n_blk, 8, WIN), F32),
                        vm((2, 2 * KEYS, WIN), F32),
                        vm((S, LANES), F32), vm((S + WIN, 2 * LANES), F32),
                        vm((S, LANES), F32), vm((S, LANES), F32), vm((S, LANES), F32)],
        compiler_params=_params(2),
    )(slopes, qkv, qkv, qkv, o, do, lse)


SCAN_COLS = 256
SCAN_ROWS = 8


def _rows_to_tile(rows):
    rid = lax.broadcasted_iota(jnp.int32, (SCAN_ROWS, rows[0].shape[1]), 0)
    tile = jnp.broadcast_to(rows[0], rid.shape)
    for k in range(1, SCAN_ROWS):
        tile = jnp.where(rid == k, rows[k], tile)
    return tile


SCAN_UNROLL = 4


def _complex_powers(ar, ai, n):
    out = [(ar, ai)]
    for _ in range(n - 1):
        pr, pi = out[-1]
        out.append((pr * ar - pi * ai, pr * ai + pi * ar))
    return out


def _round_multipliers(powers, rid, reverse):
    out = []
    for s in (1, 2, 4):
        keep = (rid < SCAN_ROWS - s) if reverse else (rid >= s)
        out.append((jnp.where(keep, powers[s - 1][0], 0.0), jnp.where(keep, powers[s - 1][1], 0.0)))
    return out


def _tile_scan(xr, xi, multipliers, reverse):
    for s, (mr, mi) in zip((1, 2, 4), multipliers):
        shift = SCAN_ROWS - s if reverse else s
        sr, si = pltpu.roll(xr, shift, 0), pltpu.roll(xi, shift, 0)
        xr, xi = xr + (mr * sr - mi * si), xi + (mr * si + mi * sr)
    return xr, xi


SCAN_CHUNK = 256


def _scan_fwd(us, bb_big, a_row, cc_big):
    B, S, _ = us.shape
    groups = 2
    width = 2 * groups * SCAN_COLS
    nc = 2 * SSM_COLS // width
    nt = S // SCAN_ROWS
    tiles = SCAN_CHUNK // SCAN_ROWS
    LAST = slice(SCAN_ROWS - 1, SCAN_ROWS)

    def body(us_ref, bb_ref, a_ref, cc_ref, xs_ref, y_ref, bu_ref):
        bb = bb_ref[...].astype(BF16)
        for c in range(S // SCAN_CHUNK):
            part = jnp.dot(us_ref[c * SCAN_CHUNK:(c + 1) * SCAN_CHUNK, :].astype(BF16), bb,
                           preferred_element_type=F32)
            bu_ref[c * tiles:(c + 1) * tiles] = part.reshape(tiles, SCAN_ROWS, width)
        rid = lax.broadcasted_iota(jnp.int32, (SCAN_ROWS, SCAN_COLS), 0)
        consts = []
        for g in range(groups):
            re = slice(2 * g * SCAN_COLS, (2 * g + 1) * SCAN_COLS)
            im = slice((2 * g + 1) * SCAN_COLS, (2 * g + 2) * SCAN_COLS)
            powers = _complex_powers(a_ref[:, re], a_ref[:, im], SCAN_ROWS)
            carry_mult = (_rows_to_tile([p[0] for p in powers]), _rows_to_tile([p[1] for p in powers]))
            consts.append((re, im, carry_mult, _round_multipliers(powers, rid, reverse=False)))

        def tile(i, carry):
            out = []
            for (re, im, (cr_t, ci_t), rounds), (cr, ci) in zip(consts, carry):
                xr, xi = _tile_scan(bu_ref[i, :, re], bu_ref[i, :, im], rounds, reverse=False)
                xs_ref[i, :, re] = xr + (cr_t * cr - ci_t * ci)
                xs_ref[i, :, im] = xi + (cr_t * ci + ci_t * cr)
                out.append((xs_ref[i, LAST, re], xs_ref[i, LAST, im]))
            return tuple(out)

        zero = jnp.zeros((1, SCAN_COLS), F32)
        lax.fori_loop(0, nt, tile, ((zero, zero),) * groups, unroll=SCAN_UNROLL)

        @pl.when(pl.program_id(1) == 0)
        def _():
            y_ref[...] = jnp.zeros_like(y_ref)

        cc = cc_ref[...].astype(BF16)
        for c in range(S // SCAN_CHUNK):
            x2 = xs_ref[c * tiles:(c + 1) * tiles].reshape(SCAN_CHUNK, width).astype(BF16)
            y_ref[c * SCAN_CHUNK:(c + 1) * SCAN_CHUNK, :] += jnp.dot(x2, cc, preferred_element_type=F32)

    col = pl.BlockSpec((None, nt, SCAN_ROWS, width), lambda b, j: (b, 0, 0, j))
    tok = pl.BlockSpec((None, S, SSM_WIDTH), lambda b, j: (b, 0, 0))
    xs, y = pl.pallas_call(
        body, name="s5_scan_fwd", grid=(B, nc),
        in_specs=[tok, pl.BlockSpec((SSM_WIDTH, width), lambda b, j: (0, j)),
                  pl.BlockSpec((1, width), lambda b, j: (0, j)), pl.BlockSpec((width, SSM_WIDTH), lambda b, j: (j, 0))],
        out_specs=[col, tok],
        out_shape=[jax.ShapeDtypeStruct((B, nt, SCAN_ROWS, 2 * SSM_COLS), F32),
                   jax.ShapeDtypeStruct((B, S, SSM_WIDTH), F32)],
        scratch_shapes=[pltpu.VMEM((nt, SCAN_ROWS, width), F32)],
        compiler_params=_params(2),
    )(us, bb_big, a_row, cc_big)
    return xs.reshape(B, S, 2 * SSM_COLS), y


def _scan_bwd(dy, us, bb_big, cc_big, xs, a_row):
    B, S, _ = dy.shape
    width = 2 * SCAN_COLS
    nc = SSM_COLS // SCAN_COLS
    nt = S // SCAN_ROWS
    tiles = SCAN_CHUNK // SCAN_ROWS
    RE, IM = slice(0, SCAN_COLS), slice(SCAN_COLS, 2 * SCAN_COLS)
    FIRST, LAST = slice(0, 1), slice(SCAN_ROWS - 1, SCAN_ROWS)

    def body(dy_ref, us_ref, bb_ref, cc_ref, x_ref, a_ref, dus_ref, ga_ref, dbb_ref, dcc_ref, d_ref, lam_ref):
        b = pl.program_id(1)
        cc = cc_ref[...].astype(BF16)
        for c in range(S // SCAN_CHUNK):
            part = lax.dot_general(dy_ref[c * SCAN_CHUNK:(c + 1) * SCAN_CHUNK, :].astype(BF16), cc, _NT,
                                   preferred_element_type=F32)
            d_ref[c * tiles:(c + 1) * tiles] = part.reshape(tiles, SCAN_ROWS, width)
        powers = _complex_powers(a_ref[:, RE], -a_ref[:, IM], SCAN_ROWS)
        rid = lax.broadcasted_iota(jnp.int32, (SCAN_ROWS, SCAN_COLS), 0)
        cr_t = _rows_to_tile([powers[SCAN_ROWS - 1 - r][0] for r in range(SCAN_ROWS)])
        ci_t = _rows_to_tile([powers[SCAN_ROWS - 1 - r][1] for r in range(SCAN_ROWS)])
        rounds = _round_multipliers(powers, rid, reverse=True)

        @pl.when(b == 0)
        def _():
            ga_ref[...] = jnp.zeros_like(ga_ref)
            dbb_ref[...] = jnp.zeros_like(dbb_ref)
            dcc_ref[...] = jnp.zeros_like(dcc_ref)

        def tile(j, carry):
            cr, ci, accr, acci = carry
            i = nt - 1 - j
            lr, li = _tile_scan(d_ref[i, :, RE], d_ref[i, :, IM], rounds, reverse=True)
            lam_r = lr + (cr_t * cr - ci_t * ci)
            lam_i = li + (cr_t * ci + ci_t * cr)
            lam_ref[i, :, RE] = lam_r
            lam_ref[i, :, IM] = lam_i
            ip = jnp.maximum(i - 1, 0)
            keep = (i > 0).astype(F32)
            xpr = jnp.where(rid == 0, x_ref[ip, LAST, RE] * keep, pltpu.roll(x_ref[i, :, RE], 1, 0))
            xpi = jnp.where(rid == 0, x_ref[ip, LAST, IM] * keep, pltpu.roll(x_ref[i, :, IM], 1, 0))
            accr = accr + lam_r * xpr + lam_i * xpi
            acci = acci + lam_i * xpr - lam_r * xpi
            return lam_ref[i, FIRST, RE], lam_ref[i, FIRST, IM], accr, acci

        z1 = jnp.zeros((1, SCAN_COLS), F32)
        z8 = jnp.zeros((SCAN_ROWS, SCAN_COLS), F32)
        _, _, accr, acci = lax.fori_loop(0, nt, tile, (z1, z1, z8, z8), unroll=SCAN_UNROLL)
        ga_ref[:, RE] += _col_sum(accr)
        ga_ref[:, IM] += _col_sum(acci)

        bb = bb_ref[...].astype(BF16)
        for c in range(S // SCAN_CHUNK):
            rows = slice(c * SCAN_CHUNK, (c + 1) * SCAN_CHUNK)
            lam2 = lam_ref[c * tiles:(c + 1) * tiles].reshape(SCAN_CHUNK, width).astype(BF16)
            x2 = x_ref[c * tiles:(c + 1) * tiles].reshape(SCAN_CHUNK, width).astype(BF16)
            dus_ref[rows, :] = lax.dot_general(lam2, bb, _NT, preferred_element_type=F32)
            dbb_ref[...] += lax.dot_general(us_ref[rows, :].astype(BF16), lam2, _TN, preferred_element_type=F32)
            dcc_ref[...] += lax.dot_general(x2, dy_ref[rows, :].astype(BF16), _TN, preferred_element_type=F32)

    col = pl.BlockSpec((None, nt, SCAN_ROWS, width), lambda j, b: (b, 0, 0, j))
    tok = pl.BlockSpec((None, S, SSM_WIDTH), lambda j, b: (b, 0, 0))
    scratch = pltpu.VMEM((nt, SCAN_ROWS, width), F32)
    return pl.pallas_call(
        body, name="s5_scan_bwd", grid=(nc, B),
        in_specs=[tok, tok, pl.BlockSpec((SSM_WIDTH, width), lambda j, b: (0, j)),
                  pl.BlockSpec((width, SSM_WIDTH), lambda j, b: (j, 0)), col,
                  pl.BlockSpec((1, width), lambda j, b: (0, j))],
        out_specs=[pl.BlockSpec((None, None, S, SSM_WIDTH), lambda j, b: (j, b, 0, 0)),
                   pl.BlockSpec((1, width), lambda j, b: (0, j)),
                   pl.BlockSpec((SSM_WIDTH, width), lambda j, b: (0, j)),
                   pl.BlockSpec((width, SSM_WIDTH), lambda j, b: (j, 0))],
        out_shape=[jax.ShapeDtypeStruct((nc, B, S, SSM_WIDTH), F32), jax.ShapeDtypeStruct((1, 2 * SSM_COLS), F32),
                   jax.ShapeDtypeStruct((SSM_WIDTH, 2 * SSM_COLS), F32),
                   jax.ShapeDtypeStruct((2 * SSM_COLS, SSM_WIDTH), F32)],
        scratch_shapes=[scratch, scratch],
        compiler_params=_params(2),
    )(dy, us, bb_big, cc_big, xs.reshape(B, nt, SCAN_ROWS, 2 * SSM_COLS), a_row)


def _s5_discretise(lr, li, log_dt):
    dt = jnp.exp(log_dt)
    mag = jnp.exp(lr * dt)
    ang = li * dt
    ab_re, ab_im = mag * jnp.cos(ang), mag * jnp.sin(ang)
    nr, ni = ab_re - 1.0, ab_im
    den = lr * lr + li * li
    f_re = (nr * lr + ni * li) / den
    f_im = (ni * lr - nr * li) / den
    return dt, ab_re, ab_im, nr, ni, den, f_re, f_im


def _s5_params(a_re, a_im, log_dt):
    def body(lr_ref, li_ref, ld_ref, abr, abi, fr, fi):
        _, ab_re, ab_im, _, _, _, f_re, f_im = _s5_discretise(lr_ref[...], li_ref[...], ld_ref[...])
        abr[...] = ab_re
        abi[...] = ab_im
        fr[...] = f_re
        fi[...] = f_im

    return pl.pallas_call(body, name="s5_params",
                          out_shape=[jax.ShapeDtypeStruct(a_re.shape, F32)] * 4)(a_re, a_im, log_dt)


def _s5_input_matrix(f_re, f_im, b_re, b_im):
    def body(fr, fi, br, bi, o_re, o_im):
        o_re[...] = fr[...] * br[...] - fi[...] * bi[...]
        o_im[...] = fr[...] * bi[...] + fi[...] * br[...]

    return pl.pallas_call(body, name="s5_input_matrix",
                          out_shape=[jax.ShapeDtypeStruct(b_re.shape, F32)] * 2)(f_re, f_im, b_re, b_im)


def _s5_input_matrix_bwd(f_re, f_im, b_re, b_im, g_re, g_im):
    def body(fr, fi, br, bi, gr, gi, dbr, dbi, dfr, dfi):
        dbr[...] = fr[...] * gr[...] + fi[...] * gi[...]
        dbi[...] = fr[...] * gi[...] - fi[...] * gr[...]
        dfr[...] = jnp.sum(br[...] * gr[...] + bi[...] * gi[...], axis=1, keepdims=True)
        dfi[...] = jnp.sum(br[...] * gi[...] - bi[...] * gr[...], axis=1, keepdims=True)

    return pl.pallas_call(
        body, name="s5_input_matrix_bwd",
        out_shape=[jax.ShapeDtypeStruct(b_re.shape, F32)] * 2 + [jax.ShapeDtypeStruct(f_re.shape, F32)] * 2,
    )(f_re, f_im, b_re, b_im, g_re, g_im)


def _s5_params_bwd(a_re, a_im, log_dt, g_ab_re, g_ab_im, d_f_re, d_f_im):
    def body(lr_ref, li_ref, ld_ref, gar, gai, dfr, dfi, o_lr, o_li, o_ld):
        lr, li = lr_ref[...], li_ref[...]
        dt, ab_re, ab_im, nr, ni, den, f_re, f_im = _s5_discretise(lr, li, ld_ref[...])
        d_fr, d_fi = dfr[...], dfi[...]
        d_nr = (d_fr * lr - d_fi * li) / den
        d_ni = (d_fr * li + d_fi * lr) / den
        common = (d_fr * f_re + d_fi * f_im) * 2.0 / den
        d_lr = (d_fr * nr + d_fi * ni) / den - common * lr
        d_li = (d_fr * ni - d_fi * nr) / den - common * li
        d_abr = gar[...] + d_nr
        d_abi = gai[...] + d_ni
        d_mag_mag = d_abr * ab_re + d_abi * ab_im
        d_ang = d_abi * ab_re - d_abr * ab_im
        o_lr[...] = d_lr + d_mag_mag * dt
        o_li[...] = d_li + d_ang * dt
        o_ld[...] = jnp.sum(d_mag_mag * lr + d_ang * li, axis=1, keepdims=True) * dt

    return pl.pallas_call(
        body, name="s5_params_bwd",
        out_shape=[jax.ShapeDtypeStruct(a_re.shape, F32)] * 2 + [jax.ShapeDtypeStruct(log_dt.shape, F32)],
    )(a_re, a_im, log_dt, g_ab_re, g_ab_im, d_f_re, d_f_im)


CONV_COLS = 256


def _shift_down(v, j, row):
    return jnp.where(row >= j, pltpu.roll(v, j, 0), 0.0)


def _shift_up(v, j, row, seq):
    return jnp.where(row < seq - j, pltpu.roll(v, seq - j, 0), 0.0)


def _conv_fwd(up, w_conv, b_conv):
    B, S, _ = up.shape
    nj = D_FF // CONV_COLS

    def body(up_ref, w_ref, b_ref, ff_ref):
        a = up_ref[:, :CONV_COLS].astype(F32)
        val = up_ref[:, CONV_COLS:].astype(F32)
        row = lax.broadcasted_iota(jnp.int32, a.shape, 0)
        w0, w1, w2 = w_ref[0:1, :], w_ref[1:2, :], w_ref[2:3, :]
        conv = b_ref[...] + w0 * a + w1 * _shift_down(a, 1, row) + w2 * _shift_down(a, 2, row)
        ff_ref[...] = (conv * _sigmoid(conv) * val).astype(ff_ref.dtype)

    return pl.pallas_call(
        body, name="conv_gate_fwd", grid=(B, nj),
        in_specs=[pl.BlockSpec((None, S, 2 * CONV_COLS), lambda b, j: (b, 0, j)),
                  pl.BlockSpec((3, CONV_COLS), lambda b, j: (0, j)),
                  pl.BlockSpec((1, CONV_COLS), lambda b, j: (0, j))],
        out_specs=pl.BlockSpec((None, S, CONV_COLS), lambda b, j: (b, 0, j)),
        out_shape=jax.ShapeDtypeStruct((B, S, D_FF), BF16),
        compiler_params=_params(2),
    )(up, w_conv, b_conv)


def _conv_bwd(up, d_ff, w_conv, b_conv):
    B, S, _ = up.shape
    nj = D_FF // CONV_COLS

    def body(up_ref, dff_ref, w_ref, b_ref, dup_ref, dw_ref, db_ref):
        b = pl.program_id(1)
        a = up_ref[:, :CONV_COLS].astype(F32)
        val = up_ref[:, CONV_COLS:].astype(F32)
        row = lax.broadcasted_iota(jnp.int32, a.shape, 0)
        w0, w1, w2 = w_ref[0:1, :], w_ref[1:2, :], w_ref[2:3, :]
        a1, a2 = _shift_down(a, 1, row), _shift_down(a, 2, row)
        conv = b_ref[...] + w0 * a + w1 * a1 + w2 * a2
        sg = _sigmoid(conv)
        dff = dff_ref[...].astype(F32)
        d_val = dff * conv * sg
        dc = dff * val * (sg * (1.0 + conv * (1.0 - sg)))
        d_a = w0 * dc + w1 * _shift_up(dc, 1, row, S) + w2 * _shift_up(dc, 2, row, S)
        dup_ref[:, :CONV_COLS] = d_a.astype(dup_ref.dtype)
        dup_ref[:, CONV_COLS:] = d_val.astype(dup_ref.dtype)

        @pl.when(b == 0)
        def _():
            dw_ref[...] = jnp.zeros_like(dw_ref)
            db_ref[...] = jnp.zeros_like(db_ref)

        dw_ref[0:1, :] += _col_sum(dc * a)
        dw_ref[1:2, :] += _col_sum(dc * a1)
        dw_ref[2:3, :] += _col_sum(dc * a2)
        db_ref[...] += _col_sum(dc)

    return pl.pallas_call(
        body, name="conv_gate_bwd", grid=(nj, B),
        in_specs=[pl.BlockSpec((None, S, 2 * CONV_COLS), lambda j, b: (b, 0, j)),
                  pl.BlockSpec((None, S, CONV_COLS), lambda j, b: (b, 0, j)),
                  pl.BlockSpec((3, CONV_COLS), lambda j, b: (0, j)),
                  pl.BlockSpec((1, CONV_COLS), lambda j, b: (0, j))],
        out_specs=[pl.BlockSpec((None, S, 2 * CONV_COLS), lambda j, b: (b, 0, j)),
                   pl.BlockSpec((3, CONV_COLS), lambda j, b: (0, j)),
                   pl.BlockSpec((1, CONV_COLS), lambda j, b: (0, j))],
        out_shape=[jax.ShapeDtypeStruct((B, S, 2 * D_FF), BF16), jax.ShapeDtypeStruct((3, D_FF), F32),
                   jax.ShapeDtypeStruct((1, D_FF), F32)],
        compiler_params=_params(2),
    )(up, d_ff, w_conv, b_conv)


def _ada_fwd(c_all, w_ada, b_ada):
    def body(c_ref, w_ref, b_ref, o_ref):
        cv = c_ref[...]
        act = (cv * _sigmoid(cv)).astype(BF16)
        o_ref[...] = jnp.dot(act, w_ref[...].astype(BF16), preferred_element_type=F32) + b_ref[...]

    return pl.pallas_call(body, name="ada_fwd",
                          out_shape=jax.ShapeDtypeStruct((c_all.shape[0], w_ada.shape[1]), F32),
                          compiler_params=pltpu.CompilerParams(vmem_limit_bytes=V7X_VMEM_LIMIT))(c_all, w_ada, b_ada)


def _ada_bwd(c_all, dmod_all, dmod_cols):
    def body(c_ref, dm_ref, dmc_ref, dw_ref, db_ref):
        cv = c_ref[...]
        act = (cv * _sigmoid(cv)).astype(BF16)
        dw_ref[...] = lax.dot_general(act, dmc_ref[...].astype(BF16), _TN, preferred_element_type=F32)
        db_ref[...] = _col_sum(dm_ref[...])

    return pl.pallas_call(
        body, name="ada_bwd",
        out_shape=[jax.ShapeDtypeStruct((c_all.shape[1], dmod_cols.shape[1]), F32),
                   jax.ShapeDtypeStruct((1, dmod_all.shape[1]), F32)],
        compiler_params=pltpu.CompilerParams(vmem_limit_bytes=V7X_VMEM_LIMIT))(c_all, dmod_all, dmod_cols)


def _adamw(w, m, v, g_parts, name, own=None):
    R, C = w.shape
    P = g_parts.shape[0]
    tr = R
    for cand in (256, 128, 64, 32, 16, 8):
        if R % cand == 0 and cand * C * 4 * (P + 8) * 2 <= V7X_VMEM_LIMIT // 2:
            tr = cand
            break
    c1 = 1.0 / (1.0 - ADAM_B1 ** ADAM_STEP)
    c2 = 1.0 / (1.0 - ADAM_B2 ** ADAM_STEP)

    def update(w_ref, m_ref, v_ref, g, og, od, om, ov):
        m_new = ADAM_B1 * m_ref[...] + (1.0 - ADAM_B1) * g
        v_new = ADAM_B2 * v_ref[...] + (1.0 - ADAM_B2) * (g * g)
        og[...] = g
        om[...] = m_new
        ov[...] = v_new
        od[...] = -ADAM_LR * ((m_new * c1) / (jnp.sqrt(v_new * c2) + ADAM_EPS) + ADAM_WD * w_ref[...])

    def total(g_ref):
        g = g_ref[0].astype(F32)
        for p in range(1, P):
            g = g + g_ref[p].astype(F32)
        return g

    out_shape = [jax.ShapeDtypeStruct((R, C), F32)] * 4
    if own is None:
        def body(w_ref, m_ref, v_ref, g_ref, og, od, om, ov):
            update(w_ref, m_ref, v_ref, total(g_ref), og, od, om, ov)

        spec = pl.BlockSpec((tr, C), lambda i: (i, 0))
        return pl.pallas_call(
            body, name=name, grid=(R // tr,),
            in_specs=[spec, spec, spec, pl.BlockSpec((P, tr, C), lambda i: (0, i, 0))],
            out_specs=[spec] * 4, out_shape=out_shape, compiler_params=_params(1),
        )(w, m, v, g_parts)

    slots, me = own

    def body_own(me_ref, w_ref, m_ref, v_ref, g_ref, own_ref, og, od, om, ov):
        g = own_ref[...].astype(F32)
        for p in range(P):
            g = g + jnp.where(me_ref[0] == p, 0.0, g_ref[p].astype(F32))
        update(w_ref, m_ref, v_ref, g, og, od, om, ov)

    spec = pl.BlockSpec((tr, C), lambda i, me_ref: (i, 0))
    grid_spec = pltpu.PrefetchScalarGridSpec(
        num_scalar_prefetch=1, grid=(R // tr,),
        in_specs=[spec, spec, spec, pl.BlockSpec((P, tr, C), lambda i, me_ref: (0, i, 0)),
                  pl.BlockSpec((None, tr, C), lambda i, me_ref: (me_ref[0], i, 0))],
        out_specs=[spec] * 4)
    return pl.pallas_call(body_own, name=name, grid_spec=grid_spec, out_shape=out_shape,
                          compiler_params=_params(1))(me, w, m, v, g_parts, slots)


def _adamw_small(ws, ms, vs, gs):
    n = len(ws)
    c1 = 1.0 / (1.0 - ADAM_B1 ** ADAM_STEP)
    c2 = 1.0 / (1.0 - ADAM_B2 ** ADAM_STEP)

    def body(*refs):
        ins, outs = refs[:4 * n], refs[4 * n:]
        for i in range(n):
            w, m, v, g = ins[i][...], ins[n + i][...], ins[2 * n + i][...], ins[3 * n + i][...]
            m_new = ADAM_B1 * m + (1.0 - ADAM_B1) * g
            v_new = ADAM_B2 * v + (1.0 - ADAM_B2) * (g * g)
            outs[4 * i][...] = g
            outs[4 * i + 1][...] = -ADAM_LR * ((m_new * c1) / (jnp.sqrt(v_new * c2) + ADAM_EPS) + ADAM_WD * w)
            outs[4 * i + 2][...] = m_new
            outs[4 * i + 3][...] = v_new

    out_shape = [jax.ShapeDtypeStruct(w.shape, F32) for w in ws for _ in range(4)]
    return pl.pallas_call(body, name="adamw_small", out_shape=out_shape,
                          compiler_params=pltpu.CompilerParams(vmem_limit_bytes=V7X_VMEM_LIMIT))(*ws, *ms, *vs, *gs)


def _sum_parts(parts, loss_rows):
    P, R, C = parts.shape
    lo, hi = loss_rows

    def body(p_ref, o_ref, loss_ref):
        t = p_ref[0]
        for p in range(1, P):
            t = t + p_ref[p]
        o_ref[...] = t
        tot = jnp.sum(jnp.sum(o_ref[lo:hi, :], axis=1, keepdims=True), axis=0, keepdims=True)
        loss_ref[...] = jnp.broadcast_to(tot, loss_ref.shape)

    return pl.pallas_call(body, name="sum_small_grads",
                          out_shape=[jax.ShapeDtypeStruct((R, C), F32), jax.ShapeDtypeStruct((1, LANES), F32)],
                          compiler_params=pltpu.CompilerParams(vmem_limit_bytes=V7X_VMEM_LIMIT))(parts)


def _exchange(items, name):
    n = len(items)
    MESH = pl.DeviceIdType.MESH

    def body(*refs):
        src, dst = refs[:n], refs[n:2 * n]
        send_sems, recv_sems, local_sems = refs[2 * n:]
        x, y, c = lax.axis_index("x"), lax.axis_index("y"), lax.axis_index("c")
        me = 4 * x + 2 * y + c
        started = []
        for it, (_, per_peer) in enumerate(items):
            own = pltpu.make_async_copy(src[it].at[me] if per_peer else src[it], dst[it].at[me], local_sems.at[it])
            own.start()
            started.append(own)
        sends, recvs = [], []
        for k in range(1, N_DEV):
            px = 1 - x if k & 4 else x
            py = 1 - y if k & 2 else y
            pc = 1 - c if k & 1 else c
            peer = 4 * px + 2 * py + pc
            for it, (_, per_peer) in enumerate(items):
                s = src[it].at[peer] if per_peer else src[it]
                cp = pltpu.make_async_remote_copy(src_ref=s, dst_ref=dst[it].at[me], send_sem=send_sems.at[it, k - 1],
                                                  recv_sem=recv_sems.at[it, k - 1], device_id=(px, py, pc),
                                                  device_id_type=MESH)
                cp.start()
                sends.append(cp)
                recvs.append(pltpu.make_async_remote_copy(
                    src_ref=s, dst_ref=dst[it].at[peer], send_sem=send_sems.at[it, k - 1],
                    recv_sem=recv_sems.at[it, k - 1], device_id=(px, py, pc), device_id_type=MESH))
        for cp in recvs:
            cp.wait_recv()
        for cp in sends:
            cp.wait_send()
        for cp in started:
            cp.wait()

    any_spec = pl.BlockSpec(memory_space=pl.ANY)
    out_shape = []
    for a, per_peer in items:
        shp = a.shape if per_peer else (N_DEV,) + a.shape
        out_shape.append(jax.ShapeDtypeStruct(shp, a.dtype))
    return pl.pallas_call(
        body, name=name, in_specs=[any_spec] * n, out_specs=[any_spec] * n, out_shape=out_shape,
        scratch_shapes=[pltpu.SemaphoreType.DMA((n, N_DEV - 1)), pltpu.SemaphoreType.DMA((n, N_DEV - 1)),
                        pltpu.SemaphoreType.DMA((n,))],
    )(*[a for a, _ in items])


def _remote(src, dst, send_sem, recv_sem, device):
    return pltpu.make_async_remote_copy(src_ref=src, dst_ref=dst, send_sem=send_sem, recv_sem=recv_sem,
                                        device_id=device, device_id_type=pl.DeviceIdType.MESH)


def _mesh_place():
    x, y, c = lax.axis_index("x"), lax.axis_index("y"), lax.axis_index("c")
    other_chips = [(1 - x, y), (x, 1 - y), (1 - x, 1 - y)]
    return x, y, c, (x, y, 1 - c), other_chips


def _gather_all(items, name):
    n = len(items)

    def body(*refs):
        src, dst = refs[:n], refs[n:2 * n]
        send_sems, recv_sems, local_sems = refs[2 * n:]
        x, y, c, sibling, chips = _mesh_place()
        slot = lambda px, py, pc: 4 * px + 2 * py + pc
        me = slot(x, y, c)
        own = [pltpu.make_async_copy(src[it], dst[it].at[me], local_sems.at[it]) for it in range(n)]
        first = []
        for it in range(n):
            first.append(_remote(src[it], dst[it].at[me], send_sems.at[it, 0], recv_sems.at[it, 0], sibling))
            for j, chip in enumerate(chips):
                first.append(_remote(src[it], dst[it].at[me], send_sems.at[it, 1 + j], recv_sems.at[it, 1 + j],
                                     (*chip, c)))
        for cp in own + first:
            cp.start()
        passed = []
        for j, chip in enumerate(chips):
            blk = slot(*chip, c)
            for it in range(n):
                _remote(src[it], dst[it].at[blk], send_sems.at[it, 1 + j], recv_sems.at[it, 1 + j],
                        (*chip, c)).wait_recv()
                fwd = _remote(dst[it].at[blk], dst[it].at[blk], send_sems.at[it, 4 + j], recv_sems.at[it, 4 + j],
                              sibling)
                fwd.start()
                passed.append(fwd)
        for it in range(n):
            _remote(src[it], dst[it].at[slot(x, y, 1 - c)], send_sems.at[it, 0], recv_sems.at[it, 0],
                    sibling).wait_recv()
        for j, chip in enumerate(chips):
            for it in range(n):
                _remote(src[it], dst[it].at[slot(*chip, 1 - c)], send_sems.at[it, 4 + j], recv_sems.at[it, 4 + j],
                        sibling).wait_recv()
        for cp in first + passed:
            cp.wait_send()
        for cp in own:
            cp.wait()

    any_spec = pl.BlockSpec(memory_space=pl.ANY)
    return pl.pallas_call(
        body, name=name, in_specs=[any_spec] * n, out_specs=[any_spec] * n,
        out_shape=[jax.ShapeDtypeStruct((N_DEV,) + a.shape, a.dtype) for a in items],
        scratch_shapes=[pltpu.SemaphoreType.DMA((n, 7)), pltpu.SemaphoreType.DMA((n, 7)),
                        pltpu.SemaphoreType.DMA((n,))],
    )(*items)


def _peers():
    x, y, c = lax.axis_index("x"), lax.axis_index("y"), lax.axis_index("c")
    out = []
    for k in range(1, N_DEV):
        px = 1 - x if k & 4 else x
        py = 1 - y if k & 2 else y
        pc = 1 - c if k & 1 else c
        out.append((k, (px, py, pc), 4 * px + 2 * py + pc))
    return 4 * x + 2 * y + c, out


def _exchange_start(items, name, gather, carry=()):
    n, m = len(items), len(carry)

    def body(*refs):
        src, land = refs[:n], refs[n:2 * n]
        first_out = 2 * n + m
        send_sems, recv_sems = refs[first_out:first_out + n], refs[first_out + n:first_out + 2 * n]
        token = refs[-1]
        me, peers = _peers()
        for k, peer, slot in peers:
            for it in range(n):
                _remote(src[it] if gather else src[it].at[slot], land[it].at[me], send_sems[it], recv_sems[it],
                        peer).start()
        token[...] = jnp.zeros_like(token)

    hbm = pl.BlockSpec(memory_space=pltpu.HBM)
    sem = pl.BlockSpec(memory_space=pltpu.SEMAPHORE)
    land_shapes = [(N_DEV,) + (a.shape if gather else a.shape[1:]) for a in items]
    lands = [lax.empty(shp, a.dtype) for shp, a in zip(land_shapes, items)]
    through = list(items) + lands + list(carry)
    outs = pl.pallas_call(
        body, name=name,
        out_shape=(*[pltpu.SemaphoreType.DMA(())] * (2 * n), *[pltpu.HBM(a.shape, a.dtype) for a in through],
                   jax.ShapeDtypeStruct((8, LANES), F32)),
        in_specs=[hbm] * len(through),
        out_specs=(*[sem] * (2 * n), *[hbm] * len(through), pl.BlockSpec(memory_space=pltpu.VMEM)),
        input_output_aliases={i: 2 * n + i for i in range(len(through))},
        compiler_params=pltpu.CompilerParams(has_side_effects=pltpu.SideEffectType.DATAFLOW_SIDE_EFFECTING),
    )(*[pltpu.with_memory_space_constraint(a, pltpu.HBM) for a in through])
    return (list(outs[:n]), list(outs[n:2 * n]), list(outs[2 * n:3 * n]), list(outs[3 * n:4 * n]), outs[-1],
            list(outs[4 * n:4 * n + m]))


def _exchange_wait(send_sems, recv_sems, items, lands, after, name):
    n = len(items)

    def body(*refs):
        land = refs[n:2 * n]
        send_sems, recv_sems = refs[2 * n:3 * n], refs[3 * n:4 * n]
        me, peers = _peers()
        for it in range(n):
            seven = land[it].at[pl.ds(0, N_DEV - 1)]
            cp = _remote(seven, seven, send_sems[it], recv_sems[it], peers[0][1])
            cp.wait_send()
            cp.wait_recv()

    hbm = pl.BlockSpec(memory_space=pltpu.HBM)
    sem = pl.BlockSpec(memory_space=pltpu.SEMAPHORE)
    outs = pl.pallas_call(
        body, name=name,
        out_shape=tuple(pltpu.HBM(a.shape, a.dtype) for a in list(items) + list(lands)),
        in_specs=[hbm] * (2 * n) + [sem] * (2 * n) + [pl.BlockSpec(memory_space=pl.ANY)],
        out_specs=tuple([hbm] * (2 * n)),
        input_output_aliases={i: i for i in range(2 * n)},
        compiler_params=pltpu.CompilerParams(has_side_effects=pltpu.SideEffectType.DATAFLOW_SIDE_EFFECTING),
    )(*items, *lands, *send_sems, *recv_sems, after)
    return list(outs[:n]), list(outs[n:])


def _gelu_tanh(y):
    k = math.sqrt(2.0 / math.pi)
    t = jnp.tanh(k * (y + 0.044715 * y * y * y))
    return 0.5 * y * (1.0 + t), t


def _local_step(x, mod, target, W, late_weights, P, send_early):
    B, S, D = x.shape
    T = B * S
    TS = 512
    flat = lambda a: a.reshape(T, a.shape[-1])
    unflat = lambda a: a.reshape(B, S, a.shape[-1])
    mod_col = lambda i: (mod, D, i)

    def f_modnorm_project(xv, sc, sh, g, wq, wu, wg):
        u = ((xv * _rms_scale(xv) * g) * (1.0 + sc) + sh).astype(BF16)
        return (u, lax.dot_general(u, wq, _NT, preferred_element_type=F32),
                lax.dot_general(u, wu, _NT, preferred_element_type=F32),
                lax.dot_general(u, wg, _NT, preferred_element_type=F32))

    u1, qkv, us, gates = _rowwise(
        f_modnorm_project, [(x, D, 0)], [mod_col(1), mod_col(0)], [P["g_mix"], W["w_qkv"], W["w_us"], W["w_gates"]],
        [(D, BF16), (3 * ATT_WIDTH, F32), (SSM_WIDTH, F32), (2 * D, BF16)], [], [], ts=TS, name="modnorm_project_in")
    u1f = flat(u1)

    o_att, lse = _attention_fwd(qkv, P["slopes"])
    more_w, more_p = late_weights(o_att)
    W, P = {**W, **more_w}, {**P, **more_p}

    xs, y_mm = _scan_fwd(us, P["bb_big"], P["a_row"], P["cc_big"])

    bga, bgs = P["b_gate"][:, :D], P["b_gate"][:, D:]

    def f_mixer_tail(ov, ymm, usv, ga, gs, xv, gt, sc, sh, w_att, w_ssm, w_o, bga_, bgs_, g, dsk, wg, bg):
        yv = ymm + dsk * usv
        ge, _ = _gelu_tanh(yv)
        zv = (ge * _sigmoid(jnp.dot(ge.astype(BF16), wg, preferred_element_type=F32) + bg)).astype(BF16)
        ya = jnp.dot(ov, w_att, preferred_element_type=F32)
        ys = jnp.dot(zv, w_ssm, preferred_element_type=F32)
        mg = (_sigmoid(ga + bga_) * ya + _sigmoid(gs + bgs_) * ys).astype(BF16)
        mx = jnp.dot(mg, w_o, preferred_element_type=F32)
        h = xv + gt * mx
        return yv, zv, mg, mx, h, (h * _rms_scale(h) * g) * (1.0 + sc) + sh

    y_s5, z, merged, mix, h1, u2 = _rowwise(
        f_mixer_tail,
        [(o_att, ATT_WIDTH, 0), (y_mm, SSM_WIDTH, 0), (us, SSM_WIDTH, 0), (gates, D, 0), (gates, D, 1), (x, D, 0)],
        [mod_col(2), mod_col(4), mod_col(3)],
        [W["w_proj_att"], W["w_proj_ssm"], W["w_out"], bga, bgs, P["g_ffn"], P["d_skip"], W["w_glu"], P["b_glu"]],
        [(SSM_WIDTH, F32), (SSM_WIDTH, BF16), (D, BF16), (D, BF16), (D, F32), (D, BF16)],
        [], [], ts=TS, raw=(0,), name="mixer_tail")

    up = unflat(_up_fwd(flat(u2), W["w_up"], name="ffn_up"))
    ff = _conv_fwd(up, P["w_conv"], P["b_conv"])

    def f_head(ffv, h1v, tg, gt, g, w_dn):
        dn = jnp.dot(ffv, w_dn, preferred_element_type=F32)
        h2 = h1v + gt * dn
        r = _rms_scale(h2)
        nh = h2 * r
        e = nh * g - tg
        dy = e * (1.0 / D)
        gy = dy * g
        dh = r * (gy - nh * jnp.mean(gy * nh, axis=-1, keepdims=True))
        return (dh, dh * gt, _col_sum(dh * dn), _col_sum(dy * nh), _col_sum(e * e) * (0.5 / D))

    dh2, d_down, d_gt2, d_g_final, loss_cols = _rowwise(
        f_head, [(ff, D_FF, 0), (h1, D, 0), (target, D, 0)], [mod_col(5)], [P["g_final"], W["w_down"]],
        [(D, BF16), (D, BF16)], [D], [(1, D), (1, D)], ts=TS, raw=(0,), name="ffn_down_head_loss")

    d_downf = flat(d_down)
    d_ff = unflat(_matmul(d_downf, W["w_down"], tb=True, out_dtype=BF16, name="ffn_down_dx"))
    d_w_down = _matmul(flat(ff), d_downf, ta=True, out_dtype=BF16, name="ffn_down_dw")
    d_up, d_w_conv, d_b_conv = _conv_bwd(up, d_ff, P["w_conv"], P["b_conv"])
    d_upf = flat(d_up)
    d_w_up = _up_dw(flat(u2), d_upf, name="ffn_up_dw")
    token, _ = send_early(dict(w_down=d_w_down.reshape(N_DEV, D_FF // N_DEV, D), w_up=d_w_up))
    g_ffn_after = P["g_ffn"] + token[0:1, 0:1]

    def f_up_back_modnorm(dup, h, dres, mx, sc, gt, g, w2):
        du = None
        for j in range(UP_SLOTS):
            wa, wv = w2[j * D:(j + 1) * D, :], w2[(j + UP_SLOTS) * D:(j + UP_SLOTS + 1) * D, :]
            wj = jnp.concatenate([wa[:, :HALF], wv[:, :HALF], wa[:, HALF:], wv[:, HALF:]], axis=1)
            part = lax.dot_general(dup[:, j * UP_GROUP:(j + 1) * UP_GROUP], wj, _NT, preferred_element_type=F32)
            du = part if du is None else du + part
        r = _rms_scale(h)
        nh = h * r
        dn = du * (1.0 + sc)
        gy = dn * g
        dh = dres + r * (gy - nh * jnp.mean(gy * nh, axis=-1, keepdims=True))
        return (dh, dh * gt, _col_sum(du), _col_sum(du * nh * g), _col_sum(dh * mx), _col_sum(dn * nh))

    dh1, d_mix, d_sh2, d_sc2, d_gt1, d_g_ffn = _rowwise(
        f_up_back_modnorm, [(d_up, 2 * D_FF, 0), (h1, D, 0), (dh2, D, 0), (mix, D, 0)], [mod_col(4), mod_col(2)],
        [g_ffn_after, W["w_up"].reshape(N_DEV * D, 2 * HALF)],
        [(D, BF16), (D, BF16)], [D, D, D], [(1, D)], ts=TS, raw=(0,), name="ffn_up_back_modnorm")

    d_mixf = flat(d_mix)
    d_w_out = _matmul(flat(merged), d_mixf, ta=True, out_dtype=BF16, name="proj_out_dw")

    def f_mixer_tail_bwd(dmx, ov, zv, ga, gs, yv, usv, w_o, w_att, w_ssm, bga_, bgs_, dsk, wg, bg):
        dm = lax.dot_general(dmx, w_o, _NT, preferred_element_type=F32)
        ya = jnp.dot(ov, w_att, preferred_element_type=F32)
        ys = jnp.dot(zv, w_ssm, preferred_element_type=F32)
        sa, ss = _sigmoid(ga + bga_), _sigmoid(gs + bgs_)
        dga = dm * ya * sa * (1.0 - sa)
        dgs = dm * ys * ss * (1.0 - ss)
        dya, dys = (dm * sa).astype(BF16), (dm * ss).astype(BF16)
        d_o = lax.dot_general(dya, w_att, _NT, preferred_element_type=F32)
        dz = lax.dot_general(dys, w_ssm, _NT, preferred_element_type=F32)
        ge, t = _gelu_tanh(yv)
        sg = _sigmoid(jnp.dot(ge.astype(BF16), wg, preferred_element_type=F32) + bg)
        dpre = dz * ge * sg * (1.0 - sg)
        dge = dz * sg + lax.dot_general(dpre.astype(BF16), wg, _NT, preferred_element_type=F32)
        k = math.sqrt(2.0 / math.pi)
        dgelu = 0.5 * (1.0 + t) + 0.5 * yv * (1.0 - t * t) * k * (1.0 + 3.0 * 0.044715 * yv * yv)
        dy = dge * dgelu
        dwg = lax.dot_general(ge.astype(BF16), dpre.astype(BF16), _TN, preferred_element_type=F32)
        return (dya, dys, jnp.concatenate([dga, dgs], axis=1), d_o, dy, dy * dsk,
                _col_sum(dga), _col_sum(dgs), dwg, _col_sum(dpre), _col_sum(dy * usv))

    (d_y_att, d_y_ssm, d_gates, d_o_att, d_y_s5, d_us_skip, d_bga, d_bgs, d_w_glu, d_b_glu, d_d_skip) = _rowwise(
        f_mixer_tail_bwd,
        [(d_mix, D, 0), (o_att, ATT_WIDTH, 0), (z, SSM_WIDTH, 0), (gates, D, 0), (gates, D, 1),
         (y_s5, SSM_WIDTH, 0), (us, SSM_WIDTH, 0)], [],
        [W["w_out"], W["w_proj_att"], W["w_proj_ssm"], bga, bgs, P["d_skip"], W["w_glu"], P["b_glu"]],
        [(D, BF16), (D, BF16), (2 * D, BF16), (ATT_WIDTH, F32), (SSM_WIDTH, BF16), (SSM_WIDTH, F32)], [],
        [(1, D), (1, D), (SSM_WIDTH, SSM_WIDTH), (1, SSM_WIDTH), (1, SSM_WIDTH)], ts=TS, raw=(0, 1, 2),
        name="mixer_tail_bwd")

    d_yaf, d_ysf = flat(d_y_att), flat(d_y_ssm)
    d_w_proj_att = _matmul(flat(o_att), d_yaf, ta=True, out_dtype=BF16, name="proj_att_dw")
    d_w_proj_ssm = _matmul(flat(z), d_ysf, ta=True, out_dtype=BF16, name="proj_ssm_dw")
    d_us_parts, g_ab, d_bb, d_cc = _scan_bwd(d_y_s5, us, P["bb_big"], P["cc_big"], xs, P["a_row"])

    token, _ = send_early(dict(
        w_out=d_w_out.reshape(N_DEV, D // N_DEV, D), w_proj_att=_cols_to_slots(d_w_proj_att),
        w_proj_ssm=_cols_to_slots(d_w_proj_ssm),
        w_glu=d_w_glu.astype(BF16).reshape(N_DEV, SSM_WIDTH // N_DEV, SSM_WIDTH),
        w_conv=_cols_to_slots(d_w_conv.astype(BF16))))
    d_qkv = _attention_bwd(qkv, o_att, d_o_att, lse, P["slopes"] + token[0, 0])

    def f_add(*parts):
        return sum(parts[1:], parts[0])

    n_parts = d_us_parts.shape[0]
    stacked = d_us_parts.reshape(n_parts * B, S, SSM_WIDTH)
    (d_us,) = _rowwise(f_add, [(d_us_skip, SSM_WIDTH, 0)] + [(stacked, SSM_WIDTH, 0, j * B) for j in range(n_parts)],
                       [], [],
                       [(SSM_WIDTH, BF16)], [], [], ts=TS, name="s5_input_grad")
    d_qkvf = flat(d_qkv)
    d_usf = flat(d_us)
    d_gatesf = flat(d_gates)
    d_w_in_t = jnp.concatenate(
        [_unpair_qkv_rows(_matmul(d_qkvf, u1f, ta=True, out_dtype=BF16, name="proj_qkv_dw")),
         _matmul(d_usf, u1f, ta=True, out_dtype=BF16, name="proj_ssm_in_dw"),
         _matmul(d_gatesf, u1f, ta=True, out_dtype=BF16, name="proj_gates_dw")], axis=0)
    token, (w_qkv, w_us, w_gates) = send_early(dict(w_in=d_w_in_t.reshape(N_DEV, -1, D)),
                                               carry=[W["w_qkv"], W["w_us"], W["w_gates"]])
    def f_project_back_modnorm(dq, du_, dg, h, dres, sc, g, wq, wu, wg):
        du = (jnp.dot(dq, wq, preferred_element_type=F32) + jnp.dot(du_, wu, preferred_element_type=F32)
              + jnp.dot(dg, wg, preferred_element_type=F32))
        r = _rms_scale(h)
        nh = h * r
        dn = du * (1.0 + sc)
        gy = dn * g
        dh = dres + r * (gy - nh * jnp.mean(gy * nh, axis=-1, keepdims=True))
        return (dh, _col_sum(du), _col_sum(du * nh * g), _col_sum(dn * nh))

    grad_x, d_sh1, d_sc1, d_g_mix = _rowwise(
        f_project_back_modnorm,
        [(d_qkv, 3 * ATT_WIDTH, 0), (d_us, SSM_WIDTH, 0), (d_gates, 2 * D, 0), (x, D, 0), (dh1, D, 0)], [mod_col(1)],
        [P["g_mix"] + token[0:1, 0:1], w_qkv, w_us, w_gates],
        [(D, F32)], [D, D], [(1, D)], ts=TS, raw=(0, 1, 2), name="project_in_back_modnorm")

    d_mod = jnp.concatenate([d_sh1, d_sc1, d_gt1, d_sh2, d_sc2, d_gt2], axis=-1)
    g_ab_re, g_ab_im = _deinterleave(g_ab)
    d_bb_re, d_bb_im = _deinterleave(d_bb)
    d_cc_re, d_cc_im = (t.T for t in _deinterleave(d_cc.T))
    small = dict(g_mix=d_g_mix, b_gate=jnp.concatenate([d_bga, d_bgs], axis=1), g_ab_re=g_ab_re, g_ab_im=g_ab_im,
                 d_bb_re=d_bb_re, d_bb_im=d_bb_im, d_cc_re=d_cc_re, d_cc_im=d_cc_im, d_skip=d_d_skip,
                 b_glu=d_b_glu, g_ffn=d_g_ffn, b_conv=d_b_conv, g_final=d_g_final, loss_cols=loss_cols)
    return grad_x, d_mod, small


def _block_diag_in(bb):
    t = bb.reshape(SSM_GROUPS, SSM_STATE, SSM_GROUP_CH)
    eye = jnp.eye(SSM_GROUPS, dtype=bb.dtype)
    return jnp.einsum("gnc,gh->gchn", t, eye).reshape(SSM_WIDTH, SSM_COLS)


def _block_diag_out(cm):
    eye = jnp.eye(SSM_GROUPS, dtype=cm.dtype)
    return jnp.einsum("gcn,gh->gnhc", cm, eye).reshape(SSM_COLS, SSM_WIDTH)


def _diag_blocks_in(m):
    t = m.reshape(SSM_GROUPS, SSM_GROUP_CH, SSM_GROUPS, SSM_STATE)
    idx = jnp.arange(SSM_GROUPS)
    return t[idx, :, idx, :].transpose(0, 2, 1).reshape(SSM_COLS, SSM_GROUP_CH)


def _diag_blocks_out(m):
    t = m.reshape(SSM_GROUPS, SSM_STATE, SSM_GROUPS, SSM_GROUP_CH)
    idx = jnp.arange(SSM_GROUPS)
    return t[idx, :, idx, :].transpose(0, 2, 1)


def _pair_qkv_rows(w):
    return w.reshape(3, N_HEADS // 2, LANES, w.shape[1]).swapaxes(0, 1).reshape(w.shape)


def _unpair_qkv_rows(w):
    return w.reshape(N_HEADS // 2, 3, LANES, w.shape[1]).swapaxes(0, 1).reshape(w.shape)


def _interleave(re, im):
    lead = re.shape[:-1]
    g = lambda a: a.reshape(lead + (SSM_COLS // SCAN_COLS, 1, SCAN_COLS))
    return jnp.concatenate([g(re), g(im)], axis=-2).reshape(lead + (2 * SSM_COLS,))


def _deinterleave(x):
    lead = x.shape[:-1]
    t = x.reshape(lead + (SSM_COLS // SCAN_COLS, 2, SCAN_COLS))
    return t[..., 0, :].reshape(lead + (SSM_COLS,)), t[..., 1, :].reshape(lead + (SSM_COLS,))


def _cols_to_slots(g):
    R = g.shape[0]
    return g.reshape(R, N_DEV, g.shape[1] // N_DEV).transpose(1, 0, 2)


def _slots_to_cols(g):
    return g.transpose(1, 0, 2).reshape(g.shape[1], N_DEV * g.shape[2])


SMALL_ORDER = ("b_ada", "g_mix", "b_gate", "a_re", "a_im", "log_dt", "b_re", "b_im", "c_re", "c_im", "d_skip",
               "b_glu", "g_ffn", "b_conv", "g_final")


def _pack(arrs):
    pieces, offs, row = [], [], 0
    for a in arrs:
        f = a.reshape(-1).astype(F32)
        n = f.shape[0]
        rows = -(-n // LANES)
        pieces.append(jnp.pad(f, (0, rows * LANES - n)))
        offs.append((row, n))
        row += rows
    return jnp.concatenate(pieces).reshape(row, LANES), offs


def _unpack(packed, offs, shapes):
    flat = packed.reshape(-1)
    return [flat[r * LANES:r * LANES + n].reshape(s) for (r, n), s in zip(offs, shapes)]


def kernel(x, c, w_ada, b_ada, g_mix, w_in, b_gate, a_re, a_im, log_dt, b_re, b_im, c_re, c_im, d_skip, w_glu, b_glu, w_proj_att, w_proj_ssm, w_out, g_ffn, w_up, w_conv, b_conv, w_down, g_final, loss_target, m_w_ada, m_b_ada, m_g_mix, m_w_in, m_b_gate, m_a_re, m_a_im, m_log_dt, m_b_re, m_b_im, m_c_re, m_c_im, m_d_skip, m_w_glu, m_b_glu, m_w_proj_att, m_w_proj_ssm, m_w_out, m_g_ffn, m_w_up, m_w_conv, m_b_conv, m_w_down, m_g_final, v_w_ada, v_b_ada, v_g_mix, v_w_in, v_b_gate, v_a_re, v_a_im, v_log_dt, v_b_re, v_b_im, v_c_re, v_c_im, v_d_skip, v_w_glu, v_b_glu, v_w_proj_att, v_w_proj_ssm, v_w_out, v_g_ffn, v_w_up, v_w_conv, v_b_conv, v_w_down, v_g_final):
    args = dict(locals())
    B, S, D = x.shape
    me = 4 * lax.axis_index("x") + 2 * lax.axis_index("y") + lax.axis_index("c")
    bf = lambda w: w[0].astype(BF16)

    c_slots, w_in_slots = _gather_all([c, w_in[0].T.astype(BF16)], name="gather_first_weights")
    c_all = c_slots.reshape(N_DEV * B, D)
    w_in_t = w_in_slots.reshape(-1, D)
    n_qkv = 3 * ATT_WIDTH
    W = dict(w_qkv=_pair_qkv_rows(w_in_t[:n_qkv]), w_us=w_in_t[n_qkv:n_qkv + SSM_WIDTH],
             w_gates=w_in_t[n_qkv + SSM_WIDTH:])

    n_ada = w_ada.shape[2]
    b_ada_cols = lax.dynamic_slice(b_ada, (0, me * n_ada), (1, n_ada))
    mod_part = _ada_fwd(c_all, w_ada[0], b_ada_cols)
    (mod_slots,) = _exchange([(mod_part.reshape(N_DEV, B, n_ada), True)], name="scatter_modulation")
    mod = mod_slots.transpose(1, 0, 2).reshape(B, 1, 6 * D)

    later = [bf(w_glu), bf(w_proj_att), bf(w_proj_ssm), bf(w_out), bf(w_up), w_conv[0], bf(w_down)]
    later_sems = _exchange_start(later, "start_later_weights", gather=True, carry=[mod])
    (mod,) = later_sems[5]

    def late_weights(after):
        _, lands = _exchange_wait(*later_sems[:4], after, name="wait_later_weights")
        g = [lax.dynamic_update_index_in_dim(land, a, me, 0) for land, a in zip(lands, later)]
        more_w = dict(w_glu=g[0].reshape(SSM_WIDTH, SSM_WIDTH), w_proj_att=_slots_to_cols(g[1]),
                      w_proj_ssm=_slots_to_cols(g[2]), w_out=g[3].reshape(D, D), w_up=g[4],
                      w_down=g[6].reshape(D_FF, D))
        return more_w, dict(w_conv=_slots_to_cols(g[5]))

    ab_re, ab_im, f_re, f_im = _s5_params(a_re[0], a_im[0], log_dt[0].reshape(SSM_GROUPS, 1))
    col = lambda a: a.reshape(SSM_COLS, 1)
    b_re2, b_im2 = b_re[0].reshape(SSM_COLS, SSM_GROUP_CH), b_im[0].reshape(SSM_COLS, SSM_GROUP_CH)
    bb_re, bb_im = _s5_input_matrix(col(f_re), col(f_im), b_re2, b_im2)
    slopes = jnp.asarray([2.0 ** (-8.0 * (h + 1) / N_HEADS) for h in range(N_HEADS)], F32)
    P = dict(g_mix=g_mix, g_ffn=g_ffn, g_final=g_final.reshape(1, D), b_gate=b_gate, d_skip=d_skip, b_glu=b_glu,
             b_conv=b_conv, slopes=slopes,
             a_row=_interleave(ab_re.reshape(1, SSM_COLS), ab_im.reshape(1, SSM_COLS)),
             bb_big=_interleave(_block_diag_in(bb_re), _block_diag_in(bb_im)),
             cc_big=_interleave(_block_diag_out(c_re[0]).T, -_block_diag_out(c_im[0]).T).T)

    in_flight = []

    def send_early(grads, carry=()):
        names = list(grads)
        handles = _exchange_start([grads[n] for n in names], "start_gradients_%d" % len(in_flight), gather=False,
                                  carry=carry)
        in_flight.append((names,) + handles[:4])
        return handles[4], handles[5]

    grad_x, d_mod, small = _local_step(x, mod, loss_target, W, late_weights, P, send_early)

    small_list = [small["loss_cols"], small["g_mix"], small["b_gate"], small["g_ab_re"], small["g_ab_im"],
                  _diag_blocks_in(small["d_bb_re"]), _diag_blocks_in(small["d_bb_im"]),
                  _diag_blocks_out(small["d_cc_re"]), -_diag_blocks_out(small["d_cc_im"]),
                  small["g_ffn"], small["b_conv"], small["g_final"], small["d_skip"], small["b_glu"]]
    small_packed, small_offs = _pack(small_list)
    small_all, dmod_slots = _gather_all([small_packed, d_mod.reshape(B, 6 * D)], name="gather_small_gradients")

    out = {}

    def update(name, parts, own=None):
        view = (lambda a: a[0].T) if name == "w_in" else (lambda a: a[0])
        back = (lambda a: a.T[None]) if name == "w_in" else (lambda a: a[None])
        g, dl, mn, vn = _adamw(view(args[name]), view(args["m_" + name]), view(args["v_" + name]), parts,
                               name="adamw_" + name, own=own)
        for key, val in (("grad_", g), ("delta_", dl), ("new_m_", mn), ("new_v_", vn)):
            out[key + name] = back(val)

    my_slot = me.astype(jnp.int32).reshape(1)
    for i, (names, send_sems, recv_sems, sent, lands) in enumerate(in_flight):
        sent, lands = _exchange_wait(send_sems, recv_sems, sent, lands, dmod_slots, name="wait_gradients_%d" % i)
        for name, own_slots, landed in zip(names, sent, lands):
            update(name, landed, own=(own_slots, my_slot))

    dmod_all = dmod_slots.reshape(N_DEV * B, 6 * D)
    dmod_cols = lax.dynamic_slice(dmod_all, (0, me * n_ada), (N_DEV * B, n_ada))
    d_w_ada, d_b_ada = _ada_bwd(c_all, dmod_all, dmod_cols)
    update("w_ada", d_w_ada[None])

    loss_row, loss_n = small_offs[0]
    small_sum, loss_vec = _sum_parts(small_all, (loss_row, loss_row + loss_n // LANES))
    shapes = [(1, D), (1, D), (1, 2 * D), (SSM_GROUPS, SSM_STATE), (SSM_GROUPS, SSM_STATE), (SSM_COLS, SSM_GROUP_CH),
              (SSM_COLS, SSM_GROUP_CH), (1, SSM_GROUPS, SSM_GROUP_CH, SSM_STATE),
              (1, SSM_GROUPS, SSM_GROUP_CH, SSM_STATE), (1, D), (1, D_FF), (D,), (1, SSM_WIDTH), (1, SSM_WIDTH)]
    (_, s_g_mix, s_b_gate, s_ab_re, s_ab_im, s_bb_re, s_bb_im, s_c_re, s_c_im, s_g_ffn, s_b_conv, s_g_final,
     s_d_skip, s_b_glu) = _unpack(small_sum, small_offs, shapes)
    d_b_re2, d_b_im2, d_f_re, d_f_im = _s5_input_matrix_bwd(col(f_re), col(f_im), b_re2, b_im2, s_bb_re, s_bb_im)
    d_a_re, d_a_im, d_log_dt = _s5_params_bwd(a_re[0], a_im[0], log_dt[0].reshape(SSM_GROUPS, 1), s_ab_re, s_ab_im,
                                              d_f_re.reshape(SSM_GROUPS, SSM_STATE),
                                              d_f_im.reshape(SSM_GROUPS, SSM_STATE))
    grads_small = dict(b_ada=d_b_ada, g_mix=s_g_mix, b_gate=s_b_gate, a_re=d_a_re[None], a_im=d_a_im[None],
                       log_dt=d_log_dt.reshape(1, SSM_GROUPS), b_re=d_b_re2.reshape(b_re.shape),
                       b_im=d_b_im2.reshape(b_im.shape), c_re=s_c_re, c_im=s_c_im, d_skip=s_d_skip, b_glu=s_b_glu,
                       g_ffn=s_g_ffn, b_conv=s_b_conv, g_final=s_g_final)
    flat2 = lambda a: a.reshape(-1, a.shape[-1])
    res = _adamw_small([flat2(args[n]) for n in SMALL_ORDER], [flat2(args["m_" + n]) for n in SMALL_ORDER],
                       [flat2(args["v_" + n]) for n in SMALL_ORDER],
                       [flat2(grads_small[n].reshape(args[n].shape)) for n in SMALL_ORDER])
    for i, n in enumerate(SMALL_ORDER):
        for k, key in enumerate(("grad_", "delta_", "new_m_", "new_v_")):
            out[key + n] = res[4 * i + k].reshape(args[n].shape)

    order = ["w_ada", "b_ada", "g_mix", "w_in", "b_gate", "a_re", "a_im", "log_dt", "b_re", "b_im", "c_re", "c_im",
             "d_skip", "w_glu", "b_glu", "w_proj_att", "w_proj_ssm", "w_out", "g_ffn", "w_up", "w_conv", "b_conv",
             "w_down", "g_final"]
    loss = loss_vec[0, 0]
    return (loss, grad_x, *[out[k + n] for k in ("grad_", "delta_", "new_m_", "new_v_") for n in order])
```

```python
import math

import jax
import jax.numpy as jnp
from jax import lax
from jax.experimental import pallas as pl
from jax.experimental.pallas import tpu as pltpu

F32 = jnp.float32
BF16 = jnp.bfloat16

N_DEV = 8
D_MODEL = 1024
N_HEADS = 8
HEAD_DIM = 64
ATT_WIDTH = N_HEADS * HEAD_DIM
DILATIONS = (1, 4, 16)
WIN = 128
SSM_GROUPS = 16
SSM_GROUP_CH = 16
SSM_WIDTH = SSM_GROUPS * SSM_GROUP_CH
SSM_STATE = 64
SSM_COLS = SSM_GROUPS * SSM_STATE
D_FF = 2048
EPS = 1e-6
NEG_INF = -1e30
ADAM_LR, ADAM_B1, ADAM_B2, ADAM_EPS, ADAM_WD, ADAM_STEP = 0.001, 0.9, 0.999, 1e-08, 0.01, 10

V7X_VMEM_LIMIT = 56 * 1024 * 1024
LANES = 128


def _params(n_grid):
    return pltpu.CompilerParams(dimension_semantics=("arbitrary",) * n_grid,
                                vmem_limit_bytes=V7X_VMEM_LIMIT)


def _tile(n, pref):
    if n <= pref:
        return n
    t = (pref // LANES) * LANES
    while t > 0:
        if n % t == 0:
            return t
        t -= LANES
    return n


def _matmul(a, b, *, ta=False, tb=False, out_dtype=F32, name):
    if ta:
        K, M = a.shape
    else:
        M, K = a.shape
    if tb:
        N, K2 = b.shape
    else:
        K2, N = b.shape
    assert K == K2, (a.shape, b.shape)
    if ta:
        tm, tn, tk = _tile(M, 1024), _tile(N, 2048), _tile(K, 1024)
    else:
        tm, tk = _tile(M, 512), _tile(K, 4096)
        tn = _tile(N, 2048 if K <= 2048 else 1024)
    nk = K // tk
    dn = (((0,) if ta else (1,), (1,) if tb else (0,)), ((), ()))

    def body(a_ref, b_ref, o_ref, acc_ref):
        k = pl.program_id(2)
        part = lax.dot_general(a_ref[...].astype(BF16), b_ref[...].astype(BF16), dn, preferred_element_type=F32)
        if nk == 1:
            o_ref[...] = part.astype(o_ref.dtype)
            return

        @pl.when(k == 0)
        def _():
            acc_ref[...] = jnp.zeros_like(acc_ref)

        acc_ref[...] += part

        @pl.when(k == nk - 1)
        def _():
            o_ref[...] = acc_ref[...].astype(o_ref.dtype)

    a_spec = (pl.BlockSpec((tk, tm), lambda j, i, k: (k, i)) if ta
              else pl.BlockSpec((tm, tk), lambda j, i, k: (i, k)))
    b_spec = (pl.BlockSpec((tn, tk), lambda j, i, k: (j, k)) if tb
              else pl.BlockSpec((tk, tn), lambda j, i, k: (k, j)))
    return pl.pallas_call(
        body, name=name, grid=(N // tn, M // tm, nk),
        in_specs=[a_spec, b_spec],
        out_specs=pl.BlockSpec((tm, tn), lambda j, i, k: (i, j)),
        out_shape=jax.ShapeDtypeStruct((M, N), out_dtype),
        scratch_shapes=[pltpu.VMEM((tm, tn) if nk > 1 else (8, LANES), F32)],
        compiler_params=_params(3),
    )(a, b)


HALF = 256
UP_SLOTS = N_DEV // 2
UP_GROUP = 4 * HALF


def _up_weight_spec(K, index):
    return pl.BlockSpec((2, None, K, 2 * HALF), index)


def _up_dw(a, d, name):
    M, K = a.shape
    tk = _tile(M, 1024)
    nk = M // tk

    def body(a_ref, d_ref, o_ref, acc_ref):
        k = pl.program_id(1)

        @pl.when(k == 0)
        def _():
            acc_ref[...] = jnp.zeros_like(acc_ref)

        acc_ref[...] += lax.dot_general(a_ref[...], d_ref[...], _TN, preferred_element_type=F32)

        @pl.when(k == nk - 1)
        def _():
            for half in range(2):
                for part in range(2):
                    lo = (2 * half + part) * HALF
                    o_ref[part, :, half * HALF:(half + 1) * HALF] = acc_ref[:, lo:lo + HALF].astype(o_ref.dtype)

    out = pl.pallas_call(
        body, name=name, grid=(UP_SLOTS, nk),
        in_specs=[pl.BlockSpec((tk, K), lambda j, k: (k, 0)), pl.BlockSpec((tk, UP_GROUP), lambda j, k: (k, j))],
        out_specs=_up_weight_spec(K, lambda j, k: (0, j, 0, 0)),
        out_shape=jax.ShapeDtypeStruct((2, UP_SLOTS, K, 2 * HALF), BF16),
        scratch_shapes=[pltpu.VMEM((K, UP_GROUP), F32)], compiler_params=_params(2),
    )(a, d)
    return out.reshape(N_DEV, K, 2 * HALF)


def _rowwise(fn, rows, bvecs, consts, out_rows, out_b, out_g, *, ts, name, raw=()):
    B, S = rows[0][0].shape[:2]
    nin = len(rows) + len(bvecs) + len(consts)
    nr, nb, ng = len(out_rows), len(out_b), len(out_g)

    def body(*refs):
        b = pl.program_id(0)
        s = pl.program_id(1)
        vals = [r[...] for r in refs[:nin]]
        vals[:len(rows)] = [v if i in raw else v.astype(F32) for i, v in enumerate(vals[:len(rows)])]
        outs = fn(*vals)
        if not isinstance(outs, (tuple, list)):
            outs = (outs,)
        orefs = refs[nin:]
        for i in range(nr):
            orefs[i][...] = outs[i].astype(orefs[i].dtype)
        for i in range(nb):
            ref = orefs[nr + i]

            @pl.when(s == 0)
            def _(ref=ref):
                ref[...] = jnp.zeros_like(ref)

            ref[...] += outs[nr + i]
        for i in range(ng):
            ref = orefs[nr + nb + i]

            @pl.when((s == 0) & (b == 0))
            def _(ref=ref):
                ref[...] = jnp.zeros_like(ref)

            ref[...] += outs[nr + nb + i]

    rows = [r if len(r) == 4 else r + (0,) for r in rows]
    in_specs = ([pl.BlockSpec((None, ts, cb), lambda b, s, ci=ci, b0=b0: (b0 + b, s, ci)) for (_, cb, ci, b0) in rows]
                + [pl.BlockSpec((None, 1, cb), lambda b, s, ci=ci: (b, 0, ci)) for (_, cb, ci) in bvecs]
                + [pl.BlockSpec(a.shape, lambda b, s: (0, 0)) for a in consts])
    out_shape = ([jax.ShapeDtypeStruct((B, S, c), dt) for (c, dt) in out_rows]
                 + [jax.ShapeDtypeStruct((B, 1, c), F32) for c in out_b]
                 + [jax.ShapeDtypeStruct(rc, F32) for rc in out_g])
    out_specs = ([pl.BlockSpec((None, ts, c), lambda b, s: (b, s, 0)) for (c, _) in out_rows]
                 + [pl.BlockSpec((None, 1, c), lambda b, s: (b, 0, 0)) for c in out_b]
                 + [pl.BlockSpec(rc, lambda b, s: (0, 0)) for rc in out_g])
    args = [r[0] for r in rows] + [a for (a, _, _) in bvecs] + list(consts)
    return pl.pallas_call(
        body, name=name, grid=(B, S // ts), in_specs=in_specs, out_specs=out_specs,
        out_shape=out_shape, compiler_params=_params(2),
    )(*args)


def _col_sum(v):
    return jnp.sum(v, axis=0, keepdims=True)


def _rms_scale(h):
    return lax.rsqrt(jnp.mean(h * h, axis=-1, keepdims=True) + EPS)


def _sigmoid(v):
    return 0.5 * (1.0 + jnp.tanh(0.5 * v))


ATT_SCALE = HEAD_DIM ** -0.5
COPY_ROWS = 256
_NT = (((1,), (1,)), ((), ()))
_TN = (((0,), (0,)), ((), ()))


def _row_chunks(d, seq):
    sub = seq // d
    out = []
    for r in range(d):
        for c0 in range(0, sub, COPY_ROWS):
            n = min(COPY_ROWS, sub - c0)
            out.append((pl.ds(r + c0 * d, n, stride=d), r * sub + c0, n))
    return out


ATT_UNROLL = 16
KEYS = 2 * WIN


def _zero_once(refs):
    @pl.when((pl.program_id(0) == 0) & (pl.program_id(1) == 0))
    def _():
        for r in refs:
            r[...] = jnp.zeros_like(r)


def _pair_bias(bias_ref, slopes_ref, hp, d, key_major):
    shape = (KEYS, WIN) if key_major else (WIN, KEYS)
    qi = lax.broadcasted_iota(jnp.int32, shape, 1 if key_major else 0)
    kj = lax.broadcasted_iota(jnp.int32, shape, 0 if key_major else 1)
    dist = WIN + qi - kj
    valid = (dist >= 0) & (dist <= WIN)
    distf = dist.astype(F32)
    for h in range(2):
        slope_d = slopes_ref[2 * hp + h] * float(d)
        with_prev = jnp.where(valid, -(slope_d * distf), NEG_INF)
        no_prev = jnp.where(kj >= WIN, with_prev, NEG_INF)
        span = slice(h * KEYS, (h + 1) * KEYS)
        if key_major:
            bias_ref[1, span, :] = with_prev
            bias_ref[0, span, :] = no_prev
        else:
            bias_ref[1, :, span] = with_prev
            bias_ref[0, :, span] = no_prev


def _stack_heads(v):
    first = lax.broadcasted_iota(jnp.int32, v.shape, 1) < HEAD_DIM
    zero = jnp.zeros_like(v)
    return jnp.concatenate([jnp.where(first, v, zero), jnp.where(first, zero, v)], axis=0)


def _per_head(c0, c1, n):
    return jnp.where(lax.broadcasted_iota(jnp.int32, (n, LANES), 1) < HEAD_DIM, c0, c1)


def _qkv_spec(seq, j):
    return pl.BlockSpec((None, seq, LANES), lambda b, hp: (b, 0, 3 * hp + j))


def _attention_fwd(qkv, slopes):
    B, S, _ = qkv.shape
    n_blk = S // WIN
    n_pair = N_HEADS // 2

    def body(slopes_ref, q_ref, k_ref, v_ref, o_ref, lse_ref, qp, kp, vp, bias, acc, mx, sm, acc_n, mx_n, sm_n):
        hp = pl.program_id(1)
        _zero_once((kp, vp))
        for p, d in enumerate(DILATIONS):
            nb = n_blk // d
            chunks = _row_chunks(d, S)
            for src, dst, n in chunks:
                qp[dst:dst + n, :] = (q_ref[src, :] * ATT_SCALE).astype(BF16)
                kp[WIN + dst:WIN + dst + n, :] = k_ref[src, :].astype(BF16)
                vp[WIN + dst:WIN + dst + n, :] = v_ref[src, :].astype(BF16)
            _pair_bias(bias, slopes_ref, hp, d, key_major=False)
            acc_t, mx_t, sm_t = (acc_n, mx_n, sm_n) if d == 1 else (acc, mx, sm)

            nk = WIN if nb == 1 else KEYS
            bias_cur = jnp.concatenate([bias[0, :, WIN:KEYS], bias[0, :, KEYS + WIN:]], axis=1) if nb == 1 else None

            def block(i, carry, p=p, nb=nb, nk=nk, bias_cur=bias_cur, acc_t=acc_t, mx_t=mx_t, sm_t=sm_t):
                cur = pl.ds(pl.multiple_of(i * WIN, WIN), WIN)
                keys = pl.ds(pl.multiple_of(i * WIN + (KEYS - nk), WIN), nk)
                s = lax.dot_general(qp[cur, :], _stack_heads(kp[keys, :]), _NT, preferred_element_type=F32)
                s = s + (bias_cur if nb == 1 else bias[((i % nb) > 0).astype(jnp.int32)])
                es, ms, ls = [], [], []
                for h in range(2):
                    sh = s[:, h * nk:(h + 1) * nk]
                    m = jnp.max(sh if nb == 1 else jnp.maximum(sh[:, :WIN], sh[:, WIN:]), axis=1, keepdims=True)
                    e = jnp.exp(sh - m)
                    es.append(e.astype(BF16))
                    ms.append(m)
                    ls.append(jnp.sum(e if nb == 1 else e[:, :WIN] + e[:, WIN:], axis=1, keepdims=True))
                acc_t[p, cur, :] = jnp.dot(jnp.concatenate(es, axis=1), _stack_heads(vp[keys, :]),
                                           preferred_element_type=F32)
                mx_t[p, cur, :] = _per_head(ms[0], ms[1], WIN)
                sm_t[p, cur, :] = _per_head(ls[0], ls[1], WIN)
                return carry

            lax.fori_loop(0, n_blk, block, 0, unroll=ATT_UNROLL)
            if d > 1:
                for src, dst, n in chunks:
                    acc_n[p, src, :] = acc[p, dst:dst + n, :]
                    mx_n[p, src, :] = mx[p, dst:dst + n, :]
                    sm_n[p, src, :] = sm[p, dst:dst + n, :]

        chunk = 256

        def merge(i, carry):
            rows = pl.ds(pl.multiple_of(i * chunk, chunk), chunk)
            ms = [mx_n[p, rows, :] for p in range(3)]
            m = jnp.maximum(jnp.maximum(ms[0], ms[1]), ms[2])
            ws = [jnp.exp(mp - m) for mp in ms]
            l = ws[0] * sm_n[0, rows, :] + ws[1] * sm_n[1, rows, :] + ws[2] * sm_n[2, rows, :]
            o = (ws[0] * acc_n[0, rows, :] + ws[1] * acc_n[1, rows, :] + ws[2] * acc_n[2, rows, :]) / l
            o_ref[rows, :] = o.astype(o_ref.dtype)
            lse = m + jnp.log(l)
            for h in range(2):
                lse_ref[rows, h:h + 1] = lse[:, h * HEAD_DIM:h * HEAD_DIM + 1]
            return carry

        lax.fori_loop(0, S // chunk, merge, 0)

    return pl.pallas_call(
        body, name="attention_fwd", grid=(B, n_pair),
        in_specs=[pl.BlockSpec(memory_space=pltpu.SMEM), _qkv_spec(S, 0), _qkv_spec(S, 1), _qkv_spec(S, 2)],
        out_specs=[pl.BlockSpec((None, S, LANES), lambda b, hp: (b, 0, hp)),
                   pl.BlockSpec((None, None, S, 2), lambda b, hp: (b, hp, 0, 0))],
        out_shape=[jax.ShapeDtypeStruct((B, S, ATT_WIDTH), BF16),
                   jax.ShapeDtypeStruct((B, n_pair, S, 2), F32)],
        scratch_shapes=[pltpu.VMEM((S, LANES), BF16), pltpu.VMEM((S + WIN, LANES), BF16),
                        pltpu.VMEM((S + WIN, LANES), BF16), pltpu.VMEM((2, WIN, 2 * KEYS), F32)]
        + [pltpu.VMEM((3, S, LANES), F32)] * 6,
        compiler_params=_params(2),
    )(slopes, qkv, qkv, qkv)


def _attention_bwd(qkv, o, do, lse, slopes):
    B, S, _ = qkv.shape
    n_blk = S // WIN
    n_pair = N_HEADS // 2

    def body(slopes_ref, q_ref, k_ref, v_ref, o_ref, do_ref, lse_ref, dx_ref,
             qp, dop, kp, vp, aux, auxp, aux_t, bias_t, dqp, dvk, dq_n, dk_n, dv_n):
        hp = pl.program_id(1)
        aux[...] = jnp.zeros_like(aux)
        for c0 in range(0, S, COPY_ROWS):
            rows = slice(c0, c0 + COPY_ROWS)
            prod = do_ref[rows, :] * o_ref[rows, :].astype(F32)
            for h in range(2):
                aux[rows, 2 * h:2 * h + 1] = lse_ref[rows, h:h + 1]
                aux[rows, 2 * h + 1:2 * h + 2] = jnp.sum(prod[:, h * HEAD_DIM:(h + 1) * HEAD_DIM], axis=1,
                                                         keepdims=True)
        dq_n[...] = jnp.zeros_like(dq_n)
        dk_n[...] = jnp.zeros_like(dk_n)
        dv_n[...] = jnp.zeros_like(dv_n)
        _zero_once((kp, vp))
        for p, d in enumerate(DILATIONS):
            nb = n_blk // d
            chunks = _row_chunks(d, S)
            for src, dst, n in chunks:
                auxp[dst:dst + n, :] = aux[src, :]
                qp[dst:dst + n, :] = (q_ref[src, :] * ATT_SCALE).astype(BF16)
                dop[dst:dst + n, :] = do_ref[src, :].astype(BF16)
                kp[WIN + dst:WIN + dst + n, :] = k_ref[src, :].astype(BF16)
                vp[WIN + dst:WIN + dst + n, :] = v_ref[src, :].astype(BF16)
            for i in range(n_blk):
                aux_t[i] = auxp[i * WIN:(i + 1) * WIN, :].T[0:8, :]
            _pair_bias(bias_t, slopes_ref, hp, d, key_major=True)
            dvk[...] = jnp.zeros_like(dvk)

            nk = WIN if nb == 1 else KEYS
            bias_cur = jnp.concatenate([bias_t[0, WIN:KEYS, :], bias_t[0, KEYS + WIN:, :]], axis=0) if nb == 1 else None

            def block(i, carry, nb=nb, nk=nk, bias_cur=bias_cur):
                cur = pl.ds(pl.multiple_of(i * WIN, WIN), WIN)
                keys = pl.ds(pl.multiple_of(i * WIN + (KEYS - nk), WIN), nk)
                q2, do2 = qp[cur, :], dop[cur, :]
                kc = _stack_heads(kp[keys, :])
                s_t = lax.dot_general(kc, q2, _NT, preferred_element_type=F32)
                s_t = s_t + (bias_cur if nb == 1 else bias_t[((i % nb) > 0).astype(jnp.int32)])
                dp_t = lax.dot_general(_stack_heads(vp[keys, :]), do2, _NT, preferred_element_type=F32)
                ps, dss = [], []
                for h in range(2):
                    span = slice(h * nk, (h + 1) * nk)
                    p_t = jnp.exp(s_t[span, :] - aux_t[i, 2 * h:2 * h + 1, :])
                    ds_t = p_t * (dp_t[span, :] - aux_t[i, 2 * h + 1:2 * h + 2, :])
                    ps.append(p_t.astype(BF16))
                    dss.append(ds_t.astype(BF16))
                do_rows, q_rows = _stack_heads(do2), _stack_heads(q2)
                zr = jnp.zeros_like(do_rows)
                rhs = jnp.concatenate([jnp.concatenate([do_rows, zr], axis=1),
                                       jnp.concatenate([zr, q_rows], axis=1)], axis=0)
                dvk[keys, :] += jnp.dot(jnp.concatenate(ps + dss, axis=1), rhs, preferred_element_type=F32)
                dqp[cur, :] = lax.dot_general(jnp.concatenate(dss, axis=0), kc, _TN, preferred_element_type=F32)
                return carry

            lax.fori_loop(0, n_blk, block, 0, unroll=ATT_UNROLL)
            for src, dst, n in chunks:
                dq_n[src, :] += dqp[dst:dst + n, :]
                dv_n[src, :] += dvk[WIN + dst:WIN + dst + n, :LANES]
                dk_n[src, :] += dvk[WIN + dst:WIN + dst + n, LANES:]
        for c0 in range(0, S, COPY_ROWS):
            rows = slice(c0, c0 + COPY_ROWS)
            dx_ref[rows, 0:LANES] = (dq_n[rows, :] * ATT_SCALE).astype(dx_ref.dtype)
            dx_ref[rows, LANES:2 * LANES] = dk_n[rows, :].astype(dx_ref.dtype)
            dx_ref[rows, 2 * LANES:3 * LANES] = dv_n[rows, :].astype(dx_ref.dtype)

    pair = lambda width: pl.BlockSpec((None, S, width), lambda b, hp: (b, 0, hp))
    vm = lambda shape, dt: pltpu.VMEM(shape, dt)
    return pl.pallas_call(
        body, name="attention_bwd", grid=(B, n_pair),
        in_specs=[pl.BlockSpec(memory_space=pltpu.SMEM), _qkv_spec(S, 0), _qkv_spec(S, 1), _qkv_spec(S, 2),
                  pair(LANES), pair(LANES), pl.BlockSpec((None, None, S, 2), lambda b, hp: (b, hp, 0, 0))],
        out_specs=pair(3 * LANES),
        out_shape=jax.ShapeDtypeStruct((B, S, 3 * ATT_WIDTH), BF16),
        scratch_shapes=[vm((S, LANES), BF16), vm((S, LANES), BF16),
                        vm((S + WIN, LANES), BF16), vm((S + WIN, LANES), BF16),
                        vm((S, LANES), F32), vm((S, LANES), F32), vm((n_blk, 8, WIN), F32),
                        vm((2, 2 * KEYS, WIN), F32),
                        vm((S, LANES), F32), vm((S + WIN, 2 * LANES), F32),
                        vm((S, LANES), F32), vm((S, LANES), F32), vm((S, LANES), F32)],
        compiler_params=_params(2),
    )(slopes, qkv, qkv, qkv, o, do, lse)


SCAN_COLS = 256
SCAN_ROWS = 8


def _rows_to_tile(rows):
    rid = lax.broadcasted_iota(jnp.int32, (SCAN_ROWS, rows[0].shape[1]), 0)
    tile = jnp.broadcast_to(rows[0], rid.shape)
    for k in range(1, SCAN_ROWS):
        tile = jnp.where(rid == k, rows[k], tile)
    return tile


SCAN_UNROLL = 4


def _complex_powers(ar, ai, n):
    out = [(ar, ai)]
    for _ in range(n - 1):
        pr, pi = out[-1]
        out.append((pr * ar - pi * ai, pr * ai + pi * ar))
    return out


def _round_multipliers(powers, rid, reverse):
    out = []
    for s in (1, 2, 4):
        keep = (rid < SCAN_ROWS - s) if reverse else (rid >= s)
        out.append((jnp.where(keep, powers[s - 1][0], 0.0), jnp.where(keep, powers[s - 1][1], 0.0)))
    return out


def _tile_scan(xr, xi, multipliers, reverse):
    for s, (mr, mi) in zip((1, 2, 4), multipliers):
        shift = SCAN_ROWS - s if reverse else s
        sr, si = pltpu.roll(xr, shift, 0), pltpu.roll(xi, shift, 0)
        xr, xi = xr + (mr * sr - mi * si), xi + (mr * si + mi * sr)
    return xr, xi


SCAN_CHUNK = 256


def _scan_fwd(us, bb_big, a_row, cc_big):
    B, S, _ = us.shape
    groups = 2
    width = 2 * groups * SCAN_COLS
    nc = 2 * SSM_COLS // width
    nt = S // SCAN_ROWS
    tiles = SCAN_CHUNK // SCAN_ROWS
    LAST = slice(SCAN_ROWS - 1, SCAN_ROWS)

    def body(us_ref, bb_ref, a_ref, cc_ref, xs_ref, y_ref, bu_ref):
        bb = bb_ref[...].astype(BF16)
        for c in range(S // SCAN_CHUNK):
            part = jnp.dot(us_ref[c * SCAN_CHUNK:(c + 1) * SCAN_CHUNK, :].astype(BF16), bb,
                           preferred_element_type=F32)
            bu_ref[c * tiles:(c + 1) * tiles] = part.reshape(tiles, SCAN_ROWS, width)
        rid = lax.broadcasted_iota(jnp.int32, (SCAN_ROWS, SCAN_COLS), 0)
        consts = []
        for g in range(groups):
            re = slice(2 * g * SCAN_COLS, (2 * g + 1) * SCAN_COLS)
            im = slice((2 * g + 1) * SCAN_COLS, (2 * g + 2) * SCAN_COLS)
            powers = _complex_powers(a_ref[:, re], a_ref[:, im], SCAN_ROWS)
            carry_mult = (_rows_to_tile([p[0] for p in powers]), _rows_to_tile([p[1] for p in powers]))
            consts.append((re, im, carry_mult, _round_multipliers(powers, rid, reverse=False)))

        def tile(i, carry):
            out = []
            for (re, im, (cr_t, ci_t), rounds), (cr, ci) in zip(consts, carry):
                xr, xi = _tile_scan(bu_ref[i, :, re], bu_ref[i, :, im], rounds, reverse=False)
                xs_ref[i, :, re] = xr + (cr_t * cr - ci_t * ci)
                xs_ref[i, :, im] = xi + (cr_t * ci + ci_t * cr)
                out.append((xs_ref[i, LAST, re], xs_ref[i, LAST, im]))
            return tuple(out)

        zero = jnp.zeros((1, SCAN_COLS), F32)
        lax.fori_loop(0, nt, tile, ((zero, zero),) * groups, unroll=SCAN_UNROLL)

        @pl.when(pl.program_id(1) == 0)
        def _():
            y_ref[...] = jnp.zeros_like(y_ref)

        cc = cc_ref[...].astype(BF16)
        for c in range(S // SCAN_CHUNK):
            x2 = xs_ref[c * tiles:(c + 1) * tiles].reshape(SCAN_CHUNK, width).astype(BF16)
            y_ref[c * SCAN_CHUNK:(c + 1) * SCAN_CHUNK, :] += jnp.dot(x2, cc, preferred_element_type=F32)

    col = pl.BlockSpec((None, nt, SCAN_ROWS, width), lambda b, j: (b, 0, 0, j))
    tok = pl.BlockSpec((None, S, SSM_WIDTH), lambda b, j: (b, 0, 0))
    xs, y = pl.pallas_call(
        body, name="s5_scan_fwd", grid=(B, nc),
        in_specs=[tok, pl.BlockSpec((SSM_WIDTH, width), lambda b, j: (0, j)),
                  pl.BlockSpec((1, width), lambda b, j: (0, j)), pl.BlockSpec((width, SSM_WIDTH), lambda b, j: (j, 0))],
        out_specs=[col, tok],
        out_shape=[jax.ShapeDtypeStruct((B, nt, SCAN_ROWS, 2 * SSM_COLS), F32),
                   jax.ShapeDtypeStruct((B, S, SSM_WIDTH), F32)],
        scratch_shapes=[pltpu.VMEM((nt, SCAN_ROWS, width), F32)],
        compiler_params=_params(2),
    )(us, bb_big, a_row, cc_big)
    return xs.reshape(B, S, 2 * SSM_COLS), y


def _scan_bwd(dy, us, bb_big, cc_big, xs, a_row):
    B, S, _ = dy.shape
    width = 2 * SCAN_COLS
    nc = SSM_COLS // SCAN_COLS
    nt = S // SCAN_ROWS
    tiles = SCAN_CHUNK // SCAN_ROWS
    RE, IM = slice(0, SCAN_COLS), slice(SCAN_COLS, 2 * SCAN_COLS)
    FIRST, LAST = slice(0, 1), slice(SCAN_ROWS - 1, SCAN_ROWS)

    def body(dy_ref, us_ref, bb_ref, cc_ref, x_ref, a_ref, dus_ref, ga_ref, dbb_ref, dcc_ref, d_ref, lam_ref):
        b = pl.program_id(1)
        cc = cc_ref[...].astype(BF16)
        for c in range(S // SCAN_CHUNK):
            part = lax.dot_general(dy_ref[c * SCAN_CHUNK:(c + 1) * SCAN_CHUNK, :].astype(BF16), cc, _NT,
                                   preferred_element_type=F32)
            d_ref[c * tiles:(c + 1) * tiles] = part.reshape(tiles, SCAN_ROWS, width)
        powers = _complex_powers(a_ref[:, RE], -a_ref[:, IM], SCAN_ROWS)
        rid = lax.broadcasted_iota(jnp.int32, (SCAN_ROWS, SCAN_COLS), 0)
        cr_t = _rows_to_tile([powers[SCAN_ROWS - 1 - r][0] for r in range(SCAN_ROWS)])
        ci_t = _rows_to_tile([powers[SCAN_ROWS - 1 - r][1] for r in range(SCAN_ROWS)])
        rounds = _round_multipliers(powers, rid, reverse=True)

        @pl.when(b == 0)
        def _():
            ga_ref[...] = jnp.zeros_like(ga_ref)
            dbb_ref[...] = jnp.zeros_like(dbb_ref)
            dcc_ref[...] = jnp.zeros_like(dcc_ref)

        def tile(j, carry):
            cr, ci, accr, acci = carry
            i = nt - 1 - j
            lr, li = _tile_scan(d_ref[i, :, RE], d_ref[i, :, IM], rounds, reverse=True)
            lam_r = lr + (cr_t * cr - ci_t * ci)
            lam_i = li + (cr_t * ci + ci_t * cr)
            lam_ref[i, :, RE] = lam_r
            lam_ref[i, :, IM] = lam_i
            ip = jnp.maximum(i - 1, 0)
            keep = (i > 0).astype(F32)
            xpr = jnp.where(rid == 0, x_ref[ip, LAST, RE] * keep, pltpu.roll(x_ref[i, :, RE], 1, 0))
            xpi = jnp.where(rid == 0, x_ref[ip, LAST, IM] * keep, pltpu.roll(x_ref[i, :, IM], 1, 0))
            accr = accr + lam_r * xpr + lam_i * xpi
            acci = acci + lam_i * xpr - lam_r * xpi
            return lam_ref[i, FIRST, RE], lam_ref[i, FIRST, IM], accr, acci

        z1 = jnp.zeros((1, SCAN_COLS), F32)
        z8 = jnp.zeros((SCAN_ROWS, SCAN_COLS), F32)
        _, _, accr, acci = lax.fori_loop(0, nt, tile, (z1, z1, z8, z8), unroll=SCAN_UNROLL)
        ga_ref[:, RE] += _col_sum(accr)
        ga_ref[:, IM] += _col_sum(acci)

        bb = bb_ref[...].astype(BF16)
        for c in range(S // SCAN_CHUNK):
            rows = slice(c * SCAN_CHUNK, (c + 1) * SCAN_CHUNK)
            lam2 = lam_ref[c * tiles:(c + 1) * tiles].reshape(SCAN_CHUNK, width).astype(BF16)
            x2 = x_ref[c * tiles:(c + 1) * tiles].reshape(SCAN_CHUNK, width).astype(BF16)
            dus_ref[rows, :] = lax.dot_general(lam2, bb, _NT, preferred_element_type=F32)
            dbb_ref[...] += lax.dot_general(us_ref[rows, :].astype(BF16), lam2, _TN, preferred_element_type=F32)
            dcc_ref[...] += lax.dot_general(x2, dy_ref[rows, :].astype(BF16), _TN, preferred_element_type=F32)

    col = pl.BlockSpec((None, nt, SCAN_ROWS, width), lambda j, b: (b, 0, 0, j))
    tok = pl.BlockSpec((None, S, SSM_WIDTH), lambda j, b: (b, 0, 0))
    scratch = pltpu.VMEM((nt, SCAN_ROWS, width), F32)
    return pl.pallas_call(
        body, name="s5_scan_bwd", grid=(nc, B),
        in_specs=[tok, tok, pl.BlockSpec((SSM_WIDTH, width), lambda j, b: (0, j)),
                  pl.BlockSpec((width, SSM_WIDTH), lambda j, b: (j, 0)), col,
                  pl.BlockSpec((1, width), lambda j, b: (0, j))],
        out_specs=[pl.BlockSpec((None, None, S, SSM_WIDTH), lambda j, b: (j, b, 0, 0)),
                   pl.BlockSpec((1, width), lambda j, b: (0, j)),
                   pl.BlockSpec((SSM_WIDTH, width), lambda j, b: (0, j)),
                   pl.BlockSpec((width, SSM_WIDTH), lambda j, b: (j, 0))],
        out_shape=[jax.ShapeDtypeStruct((nc, B, S, SSM_WIDTH), F32), jax.ShapeDtypeStruct((1, 2 * SSM_COLS), F32),
                   jax.ShapeDtypeStruct((SSM_WIDTH, 2 * SSM_COLS), F32),
                   jax.ShapeDtypeStruct((2 * SSM_COLS, SSM_WIDTH), F32)],
        scratch_shapes=[scratch, scratch],
        compiler_params=_params(2),
    )(dy, us, bb_big, cc_big, xs.reshape(B, nt, SCAN_ROWS, 2 * SSM_COLS), a_row)


def _s5_discretise(lr, li, log_dt):
    dt = jnp.exp(log_dt)
    mag = jnp.exp(lr * dt)
    ang = li * dt
    ab_re, ab_im = mag * jnp.cos(ang), mag * jnp.sin(ang)
    nr, ni = ab_re - 1.0, ab_im
    den = lr * lr + li * li
    f_re = (nr * lr + ni * li) / den
    f_im = (ni * lr - nr * li) / den
    return dt, ab_re, ab_im, nr, ni, den, f_re, f_im


def _s5_params(a_re, a_im, log_dt):
    def body(lr_ref, li_ref, ld_ref, abr, abi, fr, fi):
        _, ab_re, ab_im, _, _, _, f_re, f_im = _s5_discretise(lr_ref[...], li_ref[...], ld_ref[...])
        abr[...] = ab_re
        abi[...] = ab_im
        fr[...] = f_re
        fi[...] = f_im

    return pl.pallas_call(body, name="s5_params",
                          out_shape=[jax.ShapeDtypeStruct(a_re.shape, F32)] * 4)(a_re, a_im, log_dt)


def _s5_input_matrix(f_re, f_im, b_re, b_im):
    def body(fr, fi, br, bi, o_re, o_im):
        o_re[...] = fr[...] * br[...] - fi[...] * bi[...]
        o_im[...] = fr[...] * bi[...] + fi[...] * br[...]

    return pl.pallas_call(body, name="s5_input_matrix",
                          out_shape=[jax.ShapeDtypeStruct(b_re.shape, F32)] * 2)(f_re, f_im, b_re, b_im)


def _s5_input_matrix_bwd(f_re, f_im, b_re, b_im, g_re, g_im):
    def body(fr, fi, br, bi, gr, gi, dbr, dbi, dfr, dfi):
        dbr[...] = fr[...] * gr[...] + fi[...] * gi[...]
        dbi[...] = fr[...] * gi[...] - fi[...] * gr[...]
        dfr[...] = jnp.sum(br[...] * gr[...] + bi[...] * gi[...], axis=1, keepdims=True)
        dfi[...] = jnp.sum(br[...] * gi[...] - bi[...] * gr[...], axis=1, keepdims=True)

    return pl.pallas_call(
        body, name="s5_input_matrix_bwd",
        out_shape=[jax.ShapeDtypeStruct(b_re.shape, F32)] * 2 + [jax.ShapeDtypeStruct(f_re.shape, F32)] * 2,
    )(f_re, f_im, b_re, b_im, g_re, g_im)


def _s5_params_bwd(a_re, a_im, log_dt, g_ab_re, g_ab_im, d_f_re, d_f_im):
    def body(lr_ref, li_ref, ld_ref, gar, gai, dfr, dfi, o_lr, o_li, o_ld):
        lr, li = lr_ref[...], li_ref[...]
        dt, ab_re, ab_im, nr, ni, den, f_re, f_im = _s5_discretise(lr, li, ld_ref[...])
        d_fr, d_fi = dfr[...], dfi[...]
        d_nr = (d_fr * lr - d_fi * li) / den
        d_ni = (d_fr * li + d_fi * lr) / den
        common = (d_fr * f_re + d_fi * f_im) * 2.0 / den
        d_lr = (d_fr * nr + d_fi * ni) / den - common * lr
        d_li = (d_fr * ni - d_fi * nr) / den - common * li
        d_abr = gar[...] + d_nr
        d_abi = gai[...] + d_ni
        d_mag_mag = d_abr * ab_re + d_abi * ab_im
        d_ang = d_abi * ab_re - d_abr * ab_im
        o_lr[...] = d_lr + d_mag_mag * dt
        o_li[...] = d_li + d_ang * dt
        o_ld[...] = jnp.sum(d_mag_mag * lr + d_ang * li, axis=1, keepdims=True) * dt

    return pl.pallas_call(
        body, name="s5_params_bwd",
        out_shape=[jax.ShapeDtypeStruct(a_re.shape, F32)] * 2 + [jax.ShapeDtypeStruct(log_dt.shape, F32)],
    )(a_re, a_im, log_dt, g_ab_re, g_ab_im, d_f_re, d_f_im)


CONV_COLS = 256


def _shift_down(v, j, row):
    return jnp.where(row >= j, pltpu.roll(v, j, 0), 0.0)


def _shift_up(v, j, row, seq):
    return jnp.where(row < seq - j, pltpu.roll(v, seq - j, 0), 0.0)


def _up_conv_fwd(u, w3, w_conv, b_conv):
    B, S, K = u.shape

    def body(u_ref, w_ref, wc_ref, bc_ref, up_ref, ff_ref):
        uv = u_ref[...]
        row = lax.broadcasted_iota(jnp.int32, (S, CONV_COLS), 0)
        for half in range(2):
            cols = slice(half * HALF, (half + 1) * HALF)
            pair = jnp.dot(uv, jnp.concatenate([w_ref[0, :, cols], w_ref[1, :, cols]], axis=1),
                           preferred_element_type=F32)
            up_ref[:, half * 2 * HALF:(half + 1) * 2 * HALF] = pair.astype(up_ref.dtype)
            a, val = pair[:, :HALF], pair[:, HALF:]
            conv = (bc_ref[:, cols] + wc_ref[0:1, cols] * a + wc_ref[1:2, cols] * _shift_down(a, 1, row)
                    + wc_ref[2:3, cols] * _shift_down(a, 2, row))
            ff_ref[:, cols] = (conv * _sigmoid(conv) * val).astype(ff_ref.dtype)

    return pl.pallas_call(
        body, name="ffn_up_conv_gate", grid=(UP_SLOTS, B),
        in_specs=[pl.BlockSpec((None, S, K), lambda j, b: (b, 0, 0)), _up_weight_spec(K, lambda j, b: (0, j, 0, 0)),
                  pl.BlockSpec((3, 2 * HALF), lambda j, b: (0, j)), pl.BlockSpec((1, 2 * HALF), lambda j, b: (0, j))],
        out_specs=[pl.BlockSpec((None, S, UP_GROUP), lambda j, b: (b, 0, j)),
                   pl.BlockSpec((None, S, 2 * HALF), lambda j, b: (b, 0, j))],
        out_shape=[jax.ShapeDtypeStruct((B, S, UP_SLOTS * UP_GROUP), BF16), jax.ShapeDtypeStruct((B, S, D_FF), BF16)],
        compiler_params=_params(2),
    )(u, w3.reshape(2, UP_SLOTS, K, 2 * HALF), w_conv, b_conv)


def _conv_bwd(up, d_down, w_down, w_conv, b_conv):
    B, S, _ = up.shape
    nj = D_FF // CONV_COLS

    def body(up_ref, dd_ref, wd_ref, w_ref, b_ref, dup_ref, dw_ref, db_ref):
        b = pl.program_id(1)
        a = up_ref[:, :CONV_COLS].astype(F32)
        val = up_ref[:, CONV_COLS:].astype(F32)
        row = lax.broadcasted_iota(jnp.int32, a.shape, 0)
        w0, w1, w2 = w_ref[0:1, :], w_ref[1:2, :], w_ref[2:3, :]
        a1, a2 = _shift_down(a, 1, row), _shift_down(a, 2, row)
        conv = b_ref[...] + w0 * a + w1 * a1 + w2 * a2
        sg = _sigmoid(conv)
        dff = lax.dot_general(dd_ref[...], wd_ref[...], _NT, preferred_element_type=F32)
        d_val = dff * conv * sg
        dc = dff * val * (sg * (1.0 + conv * (1.0 - sg)))
        d_a = w0 * dc + w1 * _shift_up(dc, 1, row, S) + w2 * _shift_up(dc, 2, row, S)
        dup_ref[:, :CONV_COLS] = d_a.astype(dup_ref.dtype)
        dup_ref[:, CONV_COLS:] = d_val.astype(dup_ref.dtype)

        @pl.when(b == 0)
        def _():
            dw_ref[...] = jnp.zeros_like(dw_ref)
            db_ref[...] = jnp.zeros_like(db_ref)

        dw_ref[0:1, :] += _col_sum(dc * a)
        dw_ref[1:2, :] += _col_sum(dc * a1)
        dw_ref[2:3, :] += _col_sum(dc * a2)
        db_ref[...] += _col_sum(dc)

    return pl.pallas_call(
        body, name="conv_gate_bwd", grid=(nj, B),
        in_specs=[pl.BlockSpec((None, S, 2 * CONV_COLS), lambda j, b: (b, 0, j)),
                  pl.BlockSpec((None, S, D_MODEL), lambda j, b: (b, 0, 0)),
                  pl.BlockSpec((CONV_COLS, D_MODEL), lambda j, b: (j, 0)),
                  pl.BlockSpec((3, CONV_COLS), lambda j, b: (0, j)),
                  pl.BlockSpec((1, CONV_COLS), lambda j, b: (0, j))],
        out_specs=[pl.BlockSpec((None, S, 2 * CONV_COLS), lambda j, b: (b, 0, j)),
                   pl.BlockSpec((3, CONV_COLS), lambda j, b: (0, j)),
                   pl.BlockSpec((1, CONV_COLS), lambda j, b: (0, j))],
        out_shape=[jax.ShapeDtypeStruct((B, S, 2 * D_FF), BF16), jax.ShapeDtypeStruct((3, D_FF), F32),
                   jax.ShapeDtypeStruct((1, D_FF), F32)],
        compiler_params=_params(2),
    )(up, d_down, w_down, w_conv, b_conv)


def _ada_fwd(c_all, w_ada, b_ada):
    def body(c_ref, w_ref, b_ref, o_ref):
        cv = c_ref[...]
        act = (cv * _sigmoid(cv)).astype(BF16)
        o_ref[...] = jnp.dot(act, w_ref[...].astype(BF16), preferred_element_type=F32) + b_ref[...]

    return pl.pallas_call(body, name="ada_fwd",
                          out_shape=jax.ShapeDtypeStruct((c_all.shape[0], w_ada.shape[1]), F32),
                          compiler_params=pltpu.CompilerParams(vmem_limit_bytes=V7X_VMEM_LIMIT))(c_all, w_ada, b_ada)


def _ada_bwd(c_all, dmod_all, dmod_cols):
    def body(c_ref, dm_ref, dmc_ref, dw_ref, db_ref):
        cv = c_ref[...]
        act = (cv * _sigmoid(cv)).astype(BF16)
        dw_ref[...] = lax.dot_general(act, dmc_ref[...].astype(BF16), _TN, preferred_element_type=F32)
        db_ref[...] = _col_sum(dm_ref[...])

    return pl.pallas_call(
        body, name="ada_bwd",
        out_shape=[jax.ShapeDtypeStruct((c_all.shape[1], dmod_cols.shape[1]), F32),
                   jax.ShapeDtypeStruct((1, dmod_all.shape[1]), F32)],
        compiler_params=pltpu.CompilerParams(vmem_limit_bytes=V7X_VMEM_LIMIT))(c_all, dmod_all, dmod_cols)


def _adamw(w, m, v, g_parts, name, own=None):
    R, C = w.shape
    P = g_parts.shape[0]
    tr = R
    for cand in (256, 128, 64, 32, 16, 8):
        if R % cand == 0 and cand * C * 4 * (P + 8) * 2 <= V7X_VMEM_LIMIT // 2:
            tr = cand
            break
    c1 = 1.0 / (1.0 - ADAM_B1 ** ADAM_STEP)
    c2 = 1.0 / (1.0 - ADAM_B2 ** ADAM_STEP)

    def update(w_ref, m_ref, v_ref, g, og, od, om, ov):
        m_new = ADAM_B1 * m_ref[...] + (1.0 - ADAM_B1) * g
        v_new = ADAM_B2 * v_ref[...] + (1.0 - ADAM_B2) * (g * g)
        og[...] = g
        om[...] = m_new
        ov[...] = v_new
        od[...] = -ADAM_LR * ((m_new * c1) / (jnp.sqrt(v_new * c2) + ADAM_EPS) + ADAM_WD * w_ref[...])

    def total(g_ref):
        g = g_ref[0].astype(F32)
        for p in range(1, P):
            g = g + g_ref[p].astype(F32)
        return g

    out_shape = [jax.ShapeDtypeStruct((R, C), F32)] * 4
    if own is None:
        def body(w_ref, m_ref, v_ref, g_ref, og, od, om, ov):
            update(w_ref, m_ref, v_ref, total(g_ref), og, od, om, ov)

        spec = pl.BlockSpec((tr, C), lambda i: (i, 0))
        return pl.pallas_call(
            body, name=name, grid=(R // tr,),
            in_specs=[spec, spec, spec, pl.BlockSpec((P, tr, C), lambda i: (0, i, 0))],
            out_specs=[spec] * 4, out_shape=out_shape, compiler_params=_params(1),
        )(w, m, v, g_parts)

    slots, me = own

    def body_own(me_ref, w_ref, m_ref, v_ref, g_ref, own_ref, og, od, om, ov):
        g = own_ref[...].astype(F32)
        for p in range(P):
            g = g + jnp.where(me_ref[0] == p, 0.0, g_ref[p].astype(F32))
        update(w_ref, m_ref, v_ref, g, og, od, om, ov)

    spec = pl.BlockSpec((tr, C), lambda i, me_ref: (i, 0))
    grid_spec = pltpu.PrefetchScalarGridSpec(
        num_scalar_prefetch=1, grid=(R // tr,),
        in_specs=[spec, spec, spec, pl.BlockSpec((P, tr, C), lambda i, me_ref: (0, i, 0)),
                  pl.BlockSpec((None, tr, C), lambda i, me_ref: (me_ref[0], i, 0))],
        out_specs=[spec] * 4)
    return pl.pallas_call(body_own, name=name, grid_spec=grid_spec, out_shape=out_shape,
                          compiler_params=_params(1))(me, w, m, v, g_parts, slots)


def _adamw_small(ws, ms, vs, gs):
    n = len(ws)
    c1 = 1.0 / (1.0 - ADAM_B1 ** ADAM_STEP)
    c2 = 1.0 / (1.0 - ADAM_B2 ** ADAM_STEP)

    def body(*refs):
        ins, outs = refs[:4 * n], refs[4 * n:]
        for i in range(n):
            w, m, v, g = ins[i][...], ins[n + i][...], ins[2 * n + i][...], ins[3 * n + i][...]
            m_new = ADAM_B1 * m + (1.0 - ADAM_B1) * g
            v_new = ADAM_B2 * v + (1.0 - ADAM_B2) * (g * g)
            outs[4 * i][...] = g
            outs[4 * i + 1][...] = -ADAM_LR * ((m_new * c1) / (jnp.sqrt(v_new * c2) + ADAM_EPS) + ADAM_WD * w)
            outs[4 * i + 2][...] = m_new
            outs[4 * i + 3][...] = v_new

    out_shape = [jax.ShapeDtypeStruct(w.shape, F32) for w in ws for _ in range(4)]
    return pl.pallas_call(body, name="adamw_small", out_shape=out_shape,
                          compiler_params=pltpu.CompilerParams(vmem_limit_bytes=V7X_VMEM_LIMIT))(*ws, *ms, *vs, *gs)


def _sum_parts(parts, loss_rows):
    P, R, C = parts.shape
    lo, hi = loss_rows

    def body(p_ref, o_ref, loss_ref):
        t = p_ref[0]
        for p in range(1, P):
            t = t + p_ref[p]
        o_ref[...] = t
        tot = jnp.sum(jnp.sum(o_ref[lo:hi, :], axis=1, keepdims=True), axis=0, keepdims=True)
        loss_ref[...] = jnp.broadcast_to(tot, loss_ref.shape)

    return pl.pallas_call(body, name="sum_small_grads",
                          out_shape=[jax.ShapeDtypeStruct((R, C), F32), jax.ShapeDtypeStruct((1, LANES), F32)],
                          compiler_params=pltpu.CompilerParams(vmem_limit_bytes=V7X_VMEM_LIMIT))(parts)


def _exchange(items, name):
    n = len(items)
    MESH = pl.DeviceIdType.MESH

    def body(*refs):
        src, dst = refs[:n], refs[n:2 * n]
        send_sems, recv_sems, local_sems = refs[2 * n:]
        x, y, c = lax.axis_index("x"), lax.axis_index("y"), lax.axis_index("c")
        me = 4 * x + 2 * y + c
        started = []
        for it, (_, per_peer) in enumerate(items):
            own = pltpu.make_async_copy(src[it].at[me] if per_peer else src[it], dst[it].at[me], local_sems.at[it])
            own.start()
            started.append(own)
        sends, recvs = [], []
        for k in range(1, N_DEV):
            px = 1 - x if k & 4 else x
            py = 1 - y if k & 2 else y
            pc = 1 - c if k & 1 else c
            peer = 4 * px + 2 * py + pc
            for it, (_, per_peer) in enumerate(items):
                s = src[it].at[peer] if per_peer else src[it]
                cp = pltpu.make_async_remote_copy(src_ref=s, dst_ref=dst[it].at[me], send_sem=send_sems.at[it, k - 1],
                                                  recv_sem=recv_sems.at[it, k - 1], device_id=(px, py, pc),
                                                  device_id_type=MESH)
                cp.start()
                sends.append(cp)
                recvs.append(pltpu.make_async_remote_copy(
                    src_ref=s, dst_ref=dst[it].at[peer], send_sem=send_sems.at[it, k - 1],
                    recv_sem=recv_sems.at[it, k - 1], device_id=(px, py, pc), device_id_type=MESH))
        for cp in recvs:
            cp.wait_recv()
        for cp in sends:
            cp.wait_send()
        for cp in started:
            cp.wait()

    any_spec = pl.BlockSpec(memory_space=pl.ANY)
    out_shape = []
    for a, per_peer in items:
        shp = a.shape if per_peer else (N_DEV,) + a.shape
        out_shape.append(jax.ShapeDtypeStruct(shp, a.dtype))
    return pl.pallas_call(
        body, name=name, in_specs=[any_spec] * n, out_specs=[any_spec] * n, out_shape=out_shape,
        scratch_shapes=[pltpu.SemaphoreType.DMA((n, N_DEV - 1)), pltpu.SemaphoreType.DMA((n, N_DEV - 1)),
                        pltpu.SemaphoreType.DMA((n,))],
    )(*[a for a, _ in items])


def _remote(src, dst, send_sem, recv_sem, device):
    return pltpu.make_async_remote_copy(src_ref=src, dst_ref=dst, send_sem=send_sem, recv_sem=recv_sem,
                                        device_id=device, device_id_type=pl.DeviceIdType.MESH)


def _mesh_place():
    x, y, c = lax.axis_index("x"), lax.axis_index("y"), lax.axis_index("c")
    other_chips = [(1 - x, y), (x, 1 - y), (1 - x, 1 - y)]
    return x, y, c, (x, y, 1 - c), other_chips


def _gather_all(items, name):
    n = len(items)

    def body(*refs):
        src, dst = refs[:n], refs[n:2 * n]
        send_sems, recv_sems, local_sems = refs[2 * n:]
        x, y, c, sibling, chips = _mesh_place()
        slot = lambda px, py, pc: 4 * px + 2 * py + pc
        me = slot(x, y, c)
        own = [pltpu.make_async_copy(src[it], dst[it].at[me], local_sems.at[it]) for it in range(n)]
        first = []
        for it in range(n):
            first.append(_remote(src[it], dst[it].at[me], send_sems.at[it, 0], recv_sems.at[it, 0], sibling))
            for j, chip in enumerate(chips):
                first.append(_remote(src[it], dst[it].at[me], send_sems.at[it, 1 + j], recv_sems.at[it, 1 + j],
                                     (*chip, c)))
        for cp in own + first:
            cp.start()
        passed = []
        for j, chip in enumerate(chips):
            blk = slot(*chip, c)
            for it in range(n):
                _remote(src[it], dst[it].at[blk], send_sems.at[it, 1 + j], recv_sems.at[it, 1 + j],
                        (*chip, c)).wait_recv()
                fwd = _remote(dst[it].at[blk], dst[it].at[blk], send_sems.at[it, 4 + j], recv_sems.at[it, 4 + j],
                              sibling)
                fwd.start()
                passed.append(fwd)
        for it in range(n):
            _remote(src[it], dst[it].at[slot(x, y, 1 - c)], send_sems.at[it, 0], recv_sems.at[it, 0],
                    sibling).wait_recv()
        for j, chip in enumerate(chips):
            for it in range(n):
                _remote(src[it], dst[it].at[slot(*chip, 1 - c)], send_sems.at[it, 4 + j], recv_sems.at[it, 4 + j],
                        sibling).wait_recv()
        for cp in first + passed:
            cp.wait_send()
        for cp in own:
            cp.wait()

    any_spec = pl.BlockSpec(memory_space=pl.ANY)
    return pl.pallas_call(
        body, name=name, in_specs=[any_spec] * n, out_specs=[any_spec] * n,
        out_shape=[jax.ShapeDtypeStruct((N_DEV,) + a.shape, a.dtype) for a in items],
        scratch_shapes=[pltpu.SemaphoreType.DMA((n, 7)), pltpu.SemaphoreType.DMA((n, 7)),
                        pltpu.SemaphoreType.DMA((n,))],
    )(*items)


def _peers():
    x, y, c = lax.axis_index("x"), lax.axis_index("y"), lax.axis_index("c")
    out = []
    for k in range(1, N_DEV):
        px = 1 - x if k & 4 else x
        py = 1 - y if k & 2 else y
        pc = 1 - c if k & 1 else c
        out.append((k, (px, py, pc), 4 * px + 2 * py + pc))
    return 4 * x + 2 * y + c, out


def _exchange_start(items, name, gather, carry=()):
    n, m = len(items), len(carry)

    def body(*refs):
        src, land = refs[:n], refs[n:2 * n]
        first_out = 2 * n + m
        send_sems, recv_sems = refs[first_out:first_out + n], refs[first_out + n:first_out + 2 * n]
        token = refs[-1]
        me, peers = _peers()
        for k, peer, slot in peers:
            for it in range(n):
                _remote(src[it] if gather else src[it].at[slot], land[it].at[me], send_sems[it], recv_sems[it],
                        peer).start()
        token[...] = jnp.zeros_like(token)

    hbm = pl.BlockSpec(memory_space=pltpu.HBM)
    sem = pl.BlockSpec(memory_space=pltpu.SEMAPHORE)
    land_shapes = [(N_DEV,) + (a.shape if gather else a.shape[1:]) for a in items]
    lands = [lax.empty(shp, a.dtype) for shp, a in zip(land_shapes, items)]
    through = list(items) + lands + list(carry)
    outs = pl.pallas_call(
        body, name=name,
        out_shape=(*[pltpu.SemaphoreType.DMA(())] * (2 * n), *[pltpu.HBM(a.shape, a.dtype) for a in through],
                   jax.ShapeDtypeStruct((8, LANES), F32)),
        in_specs=[hbm] * len(through),
        out_specs=(*[sem] * (2 * n), *[hbm] * len(through), pl.BlockSpec(memory_space=pltpu.VMEM)),
        input_output_aliases={i: 2 * n + i for i in range(len(through))},
        compiler_params=pltpu.CompilerParams(has_side_effects=pltpu.SideEffectType.DATAFLOW_SIDE_EFFECTING),
    )(*[pltpu.with_memory_space_constraint(a, pltpu.HBM) for a in through])
    return (list(outs[:n]), list(outs[n:2 * n]), list(outs[2 * n:3 * n]), list(outs[3 * n:4 * n]), outs[-1],
            list(outs[4 * n:4 * n + m]))


def _exchange_wait(send_sems, recv_sems, items, lands, after, name):
    n = len(items)

    def body(*refs):
        land = refs[n:2 * n]
        send_sems, recv_sems = refs[2 * n:3 * n], refs[3 * n:4 * n]
        me, peers = _peers()
        for it in range(n):
            seven = land[it].at[pl.ds(0, N_DEV - 1)]
            cp = _remote(seven, seven, send_sems[it], recv_sems[it], peers[0][1])
            cp.wait_send()
            cp.wait_recv()

    hbm = pl.BlockSpec(memory_space=pltpu.HBM)
    sem = pl.BlockSpec(memory_space=pltpu.SEMAPHORE)
    outs = pl.pallas_call(
        body, name=name,
        out_shape=tuple(pltpu.HBM(a.shape, a.dtype) for a in list(items) + list(lands)),
        in_specs=[hbm] * (2 * n) + [sem] * (2 * n) + [pl.BlockSpec(memory_space=pl.ANY)],
        out_specs=tuple([hbm] * (2 * n)),
        input_output_aliases={i: i for i in range(2 * n)},
        compiler_params=pltpu.CompilerParams(has_side_effects=pltpu.SideEffectType.DATAFLOW_SIDE_EFFECTING),
    )(*items, *lands, *send_sems, *recv_sems, after)
    return list(outs[:n]), list(outs[n:])


def _gelu_tanh(y):
    k = math.sqrt(2.0 / math.pi)
    t = jnp.tanh(k * (y + 0.044715 * y * y * y))
    return 0.5 * y * (1.0 + t), t


def _local_step(x, mod, target, W, late_weights, P, send_early):
    B, S, D = x.shape
    T = B * S
    TS = 512
    flat = lambda a: a.reshape(T, a.shape[-1])
    unflat = lambda a: a.reshape(B, S, a.shape[-1])
    mod_col = lambda i: (mod, D, i)

    def f_modnorm_project(xv, sc, sh, g, wq, wu, wg):
        u = ((xv * _rms_scale(xv) * g) * (1.0 + sc) + sh).astype(BF16)
        return (u, lax.dot_general(u, wq, _NT, preferred_element_type=F32),
                lax.dot_general(u, wu, _NT, preferred_element_type=F32),
                lax.dot_general(u, wg, _NT, preferred_element_type=F32))

    u1, qkv, us, gates = _rowwise(
        f_modnorm_project, [(x, D, 0)], [mod_col(1), mod_col(0)], [P["g_mix"], W["w_qkv"], W["w_us"], W["w_gates"]],
        [(D, BF16), (3 * ATT_WIDTH, F32), (SSM_WIDTH, F32), (2 * D, BF16)], [], [], ts=TS, name="modnorm_project_in")
    u1f = flat(u1)

    o_att, lse = _attention_fwd(qkv, P["slopes"])
    more_w, more_p = late_weights(o_att)
    W, P = {**W, **more_w}, {**P, **more_p}

    xs, y_mm = _scan_fwd(us, P["bb_big"], P["a_row"], P["cc_big"])

    bga, bgs = P["b_gate"][:, :D], P["b_gate"][:, D:]

    def f_mixer_tail(ov, ymm, usv, ga, gs, xv, gt, sc, sh, w_att, w_ssm, w_o, bga_, bgs_, g, dsk, wg, bg):
        yv = ymm + dsk * usv
        ge, _ = _gelu_tanh(yv)
        zv = (ge * _sigmoid(jnp.dot(ge.astype(BF16), wg, preferred_element_type=F32) + bg)).astype(BF16)
        ya = jnp.dot(ov, w_att, preferred_element_type=F32)
        ys = jnp.dot(zv, w_ssm, preferred_element_type=F32)
        mg = (_sigmoid(ga + bga_) * ya + _sigmoid(gs + bgs_) * ys).astype(BF16)
        mx = jnp.dot(mg, w_o, preferred_element_type=F32)
        h = xv + gt * mx
        return yv, zv, mg, mx, h, (h * _rms_scale(h) * g) * (1.0 + sc) + sh

    y_s5, z, merged, mix, h1, u2 = _rowwise(
        f_mixer_tail,
        [(o_att, ATT_WIDTH, 0), (y_mm, SSM_WIDTH, 0), (us, SSM_WIDTH, 0), (gates, D, 0), (gates, D, 1), (x, D, 0)],
        [mod_col(2), mod_col(4), mod_col(3)],
        [W["w_proj_att"], W["w_proj_ssm"], W["w_out"], bga, bgs, P["g_ffn"], P["d_skip"], W["w_glu"], P["b_glu"]],
        [(SSM_WIDTH, F32), (SSM_WIDTH, BF16), (D, BF16), (D, BF16), (D, F32), (D, BF16)],
        [], [], ts=TS, raw=(0,), name="mixer_tail")

    up, ff = _up_conv_fwd(u2, W["w_up"], P["w_conv"], P["b_conv"])

    def f_head(ffv, h1v, tg, gt, g, w_dn):
        dn = jnp.dot(ffv, w_dn, preferred_element_type=F32)
        h2 = h1v + gt * dn
        r = _rms_scale(h2)
        nh = h2 * r
        e = nh * g - tg
        dy = e * (1.0 / D)
        gy = dy * g
        dh = r * (gy - nh * jnp.mean(gy * nh, axis=-1, keepdims=True))
        return (dh, dh * gt, _col_sum(dh * dn), _col_sum(dy * nh), _col_sum(e * e) * (0.5 / D))

    dh2, d_down, d_gt2, d_g_final, loss_cols = _rowwise(
        f_head, [(ff, D_FF, 0), (h1, D, 0), (target, D, 0)], [mod_col(5)], [P["g_final"], W["w_down"]],
        [(D, BF16), (D, BF16)], [D], [(1, D), (1, D)], ts=TS, raw=(0,), name="ffn_down_head_loss")

    d_downf = flat(d_down)
    d_w_down = _matmul(flat(ff), d_downf, ta=True, out_dtype=BF16, name="ffn_down_dw")
    d_up, d_w_conv, d_b_conv = _conv_bwd(up, d_down, W["w_down"], P["w_conv"], P["b_conv"])
    d_upf = flat(d_up)
    d_w_up = _up_dw(flat(u2), d_upf, name="ffn_up_dw")
    token, _ = send_early(dict(w_down=d_w_down.reshape(N_DEV, D_FF // N_DEV, D), w_up=d_w_up))
    g_ffn_after = P["g_ffn"] + token[0:1, 0:1]

    def f_up_back_modnorm(dup, h, dres, mx, sc, gt, g, w2):
        du = None
        for j in range(UP_SLOTS):
            wa, wv = w2[j * D:(j + 1) * D, :], w2[(j + UP_SLOTS) * D:(j + UP_SLOTS + 1) * D, :]
            wj = jnp.concatenate([wa[:, :HALF], wv[:, :HALF], wa[:, HALF:], wv[:, HALF:]], axis=1)
            part = lax.dot_general(dup[:, j * UP_GROUP:(j + 1) * UP_GROUP], wj, _NT, preferred_element_type=F32)
            du = part if du is None else du + part
        r = _rms_scale(h)
        nh = h * r
        dn = du * (1.0 + sc)
        gy = dn * g
        dh = dres + r * (gy - nh * jnp.mean(gy * nh, axis=-1, keepdims=True))
        return (dh, dh * gt, _col_sum(du), _col_sum(du * nh * g), _col_sum(dh * mx), _col_sum(dn * nh))

    dh1, d_mix, d_sh2, d_sc2, d_gt1, d_g_ffn = _rowwise(
        f_up_back_modnorm, [(d_up, 2 * D_FF, 0), (h1, D, 0), (dh2, D, 0), (mix, D, 0)], [mod_col(4), mod_col(2)],
        [g_ffn_after, W["w_up"].reshape(N_DEV * D, 2 * HALF)],
        [(D, BF16), (D, BF16)], [D, D, D], [(1, D)], ts=TS, raw=(0,), name="ffn_up_back_modnorm")

    d_mixf = flat(d_mix)
    d_w_out = _matmul(flat(merged), d_mixf, ta=True, out_dtype=BF16, name="proj_out_dw")

    def f_mixer_tail_bwd(dmx, ov, zv, ga, gs, yv, usv, w_o, w_att, w_ssm, bga_, bgs_, dsk, wg, bg):
        dm = lax.dot_general(dmx, w_o, _NT, preferred_element_type=F32)
        ya = jnp.dot(ov, w_att, preferred_element_type=F32)
        ys = jnp.dot(zv, w_ssm, preferred_element_type=F32)
        sa, ss = _sigmoid(ga + bga_), _sigmoid(gs + bgs_)
        dga = dm * ya * sa * (1.0 - sa)
        dgs = dm * ys * ss * (1.0 - ss)
        dya, dys = (dm * sa).astype(BF16), (dm * ss).astype(BF16)
        d_o = lax.dot_general(dya, w_att, _NT, preferred_element_type=F32)
        dz = lax.dot_general(dys, w_ssm, _NT, preferred_element_type=F32)
        ge, t = _gelu_tanh(yv)
        sg = _sigmoid(jnp.dot(ge.astype(BF16), wg, preferred_element_type=F32) + bg)
        dpre = dz * ge * sg * (1.0 - sg)
        dge = dz * sg + lax.dot_general(dpre.astype(BF16), wg, _NT, preferred_element_type=F32)
        k = math.sqrt(2.0 / math.pi)
        dgelu = 0.5 * (1.0 + t) + 0.5 * yv * (1.0 - t * t) * k * (1.0 + 3.0 * 0.044715 * yv * yv)
        dy = dge * dgelu
        dwg = lax.dot_general(ge.astype(BF16), dpre.astype(BF16), _TN, preferred_element_type=F32)
        return (dya, dys, jnp.concatenate([dga, dgs], axis=1), d_o, dy, dy * dsk,
                _col_sum(dga), _col_sum(dgs), dwg, _col_sum(dpre), _col_sum(dy * usv))

    (d_y_att, d_y_ssm, d_gates, d_o_att, d_y_s5, d_us_skip, d_bga, d_bgs, d_w_glu, d_b_glu, d_d_skip) = _rowwise(
        f_mixer_tail_bwd,
        [(d_mix, D, 0), (o_att, ATT_WIDTH, 0), (z, SSM_WIDTH, 0), (gates, D, 0), (gates, D, 1),
         (y_s5, SSM_WIDTH, 0), (us, SSM_WIDTH, 0)], [],
        [W["w_out"], W["w_proj_att"], W["w_proj_ssm"], bga, bgs, P["d_skip"], W["w_glu"], P["b_glu"]],
        [(D, BF16), (D, BF16), (2 * D, BF16), (ATT_WIDTH, F32), (SSM_WIDTH, BF16), (SSM_WIDTH, F32)], [],
        [(1, D), (1, D), (SSM_WIDTH, SSM_WIDTH), (1, SSM_WIDTH), (1, SSM_WIDTH)], ts=TS, raw=(0, 1, 2),
        name="mixer_tail_bwd")

    d_yaf, d_ysf = flat(d_y_att), flat(d_y_ssm)
    d_w_proj_att = _matmul(flat(o_att), d_yaf, ta=True, out_dtype=BF16, name="proj_att_dw")
    d_w_proj_ssm = _matmul(flat(z), d_ysf, ta=True, out_dtype=BF16, name="proj_ssm_dw")
    d_us_parts, g_ab, d_bb, d_cc = _scan_bwd(d_y_s5, us, P["bb_big"], P["cc_big"], xs, P["a_row"])

    token, _ = send_early(dict(
        w_out=d_w_out.reshape(N_DEV, D // N_DEV, D), w_proj_att=_cols_to_slots(d_w_proj_att),
        w_proj_ssm=_cols_to_slots(d_w_proj_ssm),
        w_glu=d_w_glu.astype(BF16).reshape(N_DEV, SSM_WIDTH // N_DEV, SSM_WIDTH),
        w_conv=_cols_to_slots(d_w_conv.astype(BF16))))
    d_qkv = _attention_bwd(qkv, o_att, d_o_att, lse, P["slopes"] + token[0, 0])

    def f_add(*parts):
        return sum(parts[1:], parts[0])

    n_parts = d_us_parts.shape[0]
    stacked = d_us_parts.reshape(n_parts * B, S, SSM_WIDTH)
    (d_us,) = _rowwise(f_add, [(d_us_skip, SSM_WIDTH, 0)] + [(stacked, SSM_WIDTH, 0, j * B) for j in range(n_parts)],
                       [], [],
                       [(SSM_WIDTH, BF16)], [], [], ts=TS, name="s5_input_grad")
    d_qkvf = flat(d_qkv)
    d_usf = flat(d_us)
    d_gatesf = flat(d_gates)
    d_w_in_t = jnp.concatenate(
        [_unpair_qkv_rows(_matmul(d_qkvf, u1f, ta=True, out_dtype=BF16, name="proj_qkv_dw")),
         _matmul(d_usf, u1f, ta=True, out_dtype=BF16, name="proj_ssm_in_dw"),
         _matmul(d_gatesf, u1f, ta=True, out_dtype=BF16, name="proj_gates_dw")], axis=0)
    token, (w_qkv, w_us, w_gates) = send_early(dict(w_in=d_w_in_t.reshape(N_DEV, -1, D)),
                                               carry=[W["w_qkv"], W["w_us"], W["w_gates"]])
    def f_project_back_modnorm(dq, du_, dg, h, dres, sc, g, wq, wu, wg):
        du = (jnp.dot(dq, wq, preferred_element_type=F32) + jnp.dot(du_, wu, preferred_element_type=F32)
              + jnp.dot(dg, wg, preferred_element_type=F32))
        r = _rms_scale(h)
        nh = h * r
        dn = du * (1.0 + sc)
        gy = dn * g
        dh = dres + r * (gy - nh * jnp.mean(gy * nh, axis=-1, keepdims=True))
        return (dh, _col_sum(du), _col_sum(du * nh * g), _col_sum(dn * nh))

    grad_x, d_sh1, d_sc1, d_g_mix = _rowwise(
        f_project_back_modnorm,
        [(d_qkv, 3 * ATT_WIDTH, 0), (d_us, SSM_WIDTH, 0), (d_gates, 2 * D, 0), (x, D, 0), (dh1, D, 0)], [mod_col(1)],
        [P["g_mix"] + token[0:1, 0:1], w_qkv, w_us, w_gates],
        [(D, F32)], [D, D], [(1, D)], ts=TS, raw=(0, 1, 2), name="project_in_back_modnorm")

    d_mod = jnp.concatenate([d_sh1, d_sc1, d_gt1, d_sh2, d_sc2, d_gt2], axis=-1)
    g_ab_re, g_ab_im = _deinterleave(g_ab)
    d_bb_re, d_bb_im = _deinterleave(d_bb)
    d_cc_re, d_cc_im = (t.T for t in _deinterleave(d_cc.T))
    small = dict(g_mix=d_g_mix, b_gate=jnp.concatenate([d_bga, d_bgs], axis=1), g_ab_re=g_ab_re, g_ab_im=g_ab_im,
                 d_bb_re=d_bb_re, d_bb_im=d_bb_im, d_cc_re=d_cc_re, d_cc_im=d_cc_im, d_skip=d_d_skip,
                 b_glu=d_b_glu, g_ffn=d_g_ffn, b_conv=d_b_conv, g_final=d_g_final, loss_cols=loss_cols)
    return grad_x, d_mod, small


def _block_diag_in(bb):
    t = bb.reshape(SSM_GROUPS, SSM_STATE, SSM_GROUP_CH)
    eye = jnp.eye(SSM_GROUPS, dtype=bb.dtype)
    return jnp.einsum("gnc,gh->gchn", t, eye).reshape(SSM_WIDTH, SSM_COLS)


def _block_diag_out(cm):
    eye = jnp.eye(SSM_GROUPS, dtype=cm.dtype)
    return jnp.einsum("gcn,gh->gnhc", cm, eye).reshape(SSM_COLS, SSM_WIDTH)


def _diag_blocks_in(m):
    t = m.reshape(SSM_GROUPS, SSM_GROUP_CH, SSM_GROUPS, SSM_STATE)
    idx = jnp.arange(SSM_GROUPS)
    return t[idx, :, idx, :].transpose(0, 2, 1).reshape(SSM_COLS, SSM_GROUP_CH)


def _diag_blocks_out(m):
    t = m.reshape(SSM_GROUPS, SSM_STATE, SSM_GROUPS, SSM_GROUP_CH)
    idx = jnp.arange(SSM_GROUPS)
    return t[idx, :, idx, :].transpose(0, 2, 1)


def _pair_qkv_rows(w):
    return w.reshape(3, N_HEADS // 2, LANES, w.shape[1]).swapaxes(0, 1).reshape(w.shape)


def _unpair_qkv_rows(w):
    return w.reshape(N_HEADS // 2, 3, LANES, w.shape[1]).swapaxes(0, 1).reshape(w.shape)


def _interleave(re, im):
    lead = re.shape[:-1]
    g = lambda a: a.reshape(lead + (SSM_COLS // SCAN_COLS, 1, SCAN_COLS))
    return jnp.concatenate([g(re), g(im)], axis=-2).reshape(lead + (2 * SSM_COLS,))


def _deinterleave(x):
    lead = x.shape[:-1]
    t = x.reshape(lead + (SSM_COLS // SCAN_COLS, 2, SCAN_COLS))
    return t[..., 0, :].reshape(lead + (SSM_COLS,)), t[..., 1, :].reshape(lead + (SSM_COLS,))


def _cols_to_slots(g):
    R = g.shape[0]
    return g.reshape(R, N_DEV, g.shape[1] // N_DEV).transpose(1, 0, 2)


def _slots_to_cols(g):
    return g.transpose(1, 0, 2).reshape(g.shape[1], N_DEV * g.shape[2])


SMALL_ORDER = ("b_ada", "g_mix", "b_gate", "a_re", "a_im", "log_dt", "b_re", "b_im", "c_re", "c_im", "d_skip",
               "b_glu", "g_ffn", "b_conv", "g_final")


def _pack(arrs):
    pieces, offs, row = [], [], 0
    for a in arrs:
        f = a.reshape(-1).astype(F32)
        n = f.shape[0]
        rows = -(-n // LANES)
        pieces.append(jnp.pad(f, (0, rows * LANES - n)))
        offs.append((row, n))
        row += rows
    return jnp.concatenate(pieces).reshape(row, LANES), offs


def _unpack(packed, offs, shapes):
    flat = packed.reshape(-1)
    return [flat[r * LANES:r * LANES + n].reshape(s) for (r, n), s in zip(offs, shapes)]


def kernel(x, c, w_ada, b_ada, g_mix, w_in, b_gate, a_re, a_im, log_dt, b_re, b_im, c_re, c_im, d_skip, w_glu, b_glu, w_proj_att, w_proj_ssm, w_out, g_ffn, w_up, w_conv, b_conv, w_down, g_final, loss_target, m_w_ada, m_b_ada, m_g_mix, m_w_in, m_b_gate, m_a_re, m_a_im, m_log_dt, m_b_re, m_b_im, m_c_re, m_c_im, m_d_skip, m_w_glu, m_b_glu, m_w_proj_att, m_w_proj_ssm, m_w_out, m_g_ffn, m_w_up, m_w_conv, m_b_conv, m_w_down, m_g_final, v_w_ada, v_b_ada, v_g_mix, v_w_in, v_b_gate, v_a_re, v_a_im, v_log_dt, v_b_re, v_b_im, v_c_re, v_c_im, v_d_skip, v_w_glu, v_b_glu, v_w_proj_att, v_w_proj_ssm, v_w_out, v_g_ffn, v_w_up, v_w_conv, v_b_conv, v_w_down, v_g_final):
    args = dict(locals())
    B, S, D = x.shape
    me = 4 * lax.axis_index("x") + 2 * lax.axis_index("y") + lax.axis_index("c")
    bf = lambda w: w[0].astype(BF16)

    c_slots, w_in_slots = _gather_all([c, w_in[0].T.astype(BF16)], name="gather_first_weights")
    c_all = c_slots.reshape(N_DEV * B, D)
    w_in_t = w_in_slots.reshape(-1, D)
    n_qkv = 3 * ATT_WIDTH
    W = dict(w_qkv=_pair_qkv_rows(w_in_t[:n_qkv]), w_us=w_in_t[n_qkv:n_qkv + SSM_WIDTH],
             w_gates=w_in_t[n_qkv + SSM_WIDTH:])

    n_ada = w_ada.shape[2]
    b_ada_cols = lax.dynamic_slice(b_ada, (0, me * n_ada), (1, n_ada))
    mod_part = _ada_fwd(c_all, w_ada[0], b_ada_cols)
    (mod_slots,) = _exchange([(mod_part.reshape(N_DEV, B, n_ada), True)], name="scatter_modulation")
    mod = mod_slots.transpose(1, 0, 2).reshape(B, 1, 6 * D)

    later = [bf(w_glu), bf(w_proj_att), bf(w_proj_ssm), bf(w_out), bf(w_up), w_conv[0], bf(w_down)]
    later_sems = _exchange_start(later, "start_later_weights", gather=True, carry=[mod])
    (mod,) = later_sems[5]

    def late_weights(after):
        _, lands = _exchange_wait(*later_sems[:4], after, name="wait_later_weights")
        g = [lax.dynamic_update_index_in_dim(land, a, me, 0) for land, a in zip(lands, later)]
        more_w = dict(w_glu=g[0].reshape(SSM_WIDTH, SSM_WIDTH), w_proj_att=_slots_to_cols(g[1]),
                      w_proj_ssm=_slots_to_cols(g[2]), w_out=g[3].reshape(D, D), w_up=g[4],
                      w_down=g[6].reshape(D_FF, D))
        return more_w, dict(w_conv=_slots_to_cols(g[5]))

    ab_re, ab_im, f_re, f_im = _s5_params(a_re[0], a_im[0], log_dt[0].reshape(SSM_GROUPS, 1))
    col = lambda a: a.reshape(SSM_COLS, 1)
    b_re2, b_im2 = b_re[0].reshape(SSM_COLS, SSM_GROUP_CH), b_im[0].reshape(SSM_COLS, SSM_GROUP_CH)
    bb_re, bb_im = _s5_input_matrix(col(f_re), col(f_im), b_re2, b_im2)
    slopes = jnp.asarray([2.0 ** (-8.0 * (h + 1) / N_HEADS) for h in range(N_HEADS)], F32)
    P = dict(g_mix=g_mix, g_ffn=g_ffn, g_final=g_final.reshape(1, D), b_gate=b_gate, d_skip=d_skip, b_glu=b_glu,
             b_conv=b_conv, slopes=slopes,
             a_row=_interleave(ab_re.reshape(1, SSM_COLS), ab_im.reshape(1, SSM_COLS)),
             bb_big=_interleave(_block_diag_in(bb_re), _block_diag_in(bb_im)),
             cc_big=_interleave(_block_diag_out(c_re[0]).T, -_block_diag_out(c_im[0]).T).T)

    in_flight = []

    def send_early(grads, carry=()):
        names = list(grads)
        handles = _exchange_start([grads[n] for n in names], "start_gradients_%d" % len(in_flight), gather=False,
                                  carry=carry)
        in_flight.append((names,) + handles[:4])
        return handles[4], handles[5]

    grad_x, d_mod, small = _local_step(x, mod, loss_target, W, late_weights, P, send_early)

    small_list = [small["loss_cols"], small["g_mix"], small["b_gate"], small["g_ab_re"], small["g_ab_im"],
                  _diag_blocks_in(small["d_bb_re"]), _diag_blocks_in(small["d_bb_im"]),
                  _diag_blocks_out(small["d_cc_re"]), -_diag_blocks_out(small["d_cc_im"]),
                  small["g_ffn"], small["b_conv"], small["g_final"], small["d_skip"], small["b_glu"]]
    small_packed, small_offs = _pack(small_list)
    small_all, dmod_slots = _gather_all([small_packed, d_mod.reshape(B, 6 * D)], name="gather_small_gradients")

    out = {}

    def update(name, parts, own=None):
        view = (lambda a: a[0].T) if name == "w_in" else (lambda a: a[0])
        back = (lambda a: a.T[None]) if name == "w_in" else (lambda a: a[None])
        g, dl, mn, vn = _adamw(view(args[name]), view(args["m_" + name]), view(args["v_" + name]), parts,
                               name="adamw_" + name, own=own)
        for key, val in (("grad_", g), ("delta_", dl), ("new_m_", mn), ("new_v_", vn)):
            out[key + name] = back(val)

    my_slot = me.astype(jnp.int32).reshape(1)
    for i, (names, send_sems, recv_sems, sent, lands) in enumerate(in_flight):
        sent, lands = _exchange_wait(send_sems, recv_sems, sent, lands, dmod_slots, name="wait_gradients_%d" % i)
        for name, own_slots, landed in zip(names, sent, lands):
            update(name, landed, own=(own_slots, my_slot))

    dmod_all = dmod_slots.reshape(N_DEV * B, 6 * D)
    dmod_cols = lax.dynamic_slice(dmod_all, (0, me * n_ada), (N_DEV * B, n_ada))
    d_w_ada, d_b_ada = _ada_bwd(c_all, dmod_all, dmod_cols)
    update("w_ada", d_w_ada[None])

    loss_row, loss_n = small_offs[0]
    small_sum, loss_vec = _sum_parts(small_all, (loss_row, loss_row + loss_n // LANES))
    shapes = [(1, D), (1, D), (1, 2 * D), (SSM_GROUPS, SSM_STATE), (SSM_GROUPS, SSM_STATE), (SSM_COLS, SSM_GROUP_CH),
              (SSM_COLS, SSM_GROUP_CH), (1, SSM_GROUPS, SSM_GROUP_CH, SSM_STATE),
              (1, SSM_GROUPS, SSM_GROUP_CH, SSM_STATE), (1, D), (1, D_FF), (D,), (1, SSM_WIDTH), (1, SSM_WIDTH)]
    (_, s_g_mix, s_b_gate, s_ab_re, s_ab_im, s_bb_re, s_bb_im, s_c_re, s_c_im, s_g_ffn, s_b_conv, s_g_final,
     s_d_skip, s_b_glu) = _unpack(small_sum, small_offs, shapes)
    d_b_re2, d_b_im2, d_f_re, d_f_im = _s5_input_matrix_bwd(col(f_re), col(f_im), b_re2, b_im2, s_bb_re, s_bb_im)
    d_a_re, d_a_im, d_log_dt = _s5_params_bwd(a_re[0], a_im[0], log_dt[0].reshape(SSM_GROUPS, 1), s_ab_re, s_ab_im,
                                              d_f_re.reshape(SSM_GROUPS, SSM_STATE),
                                              d_f_im.reshape(SSM_GROUPS, SSM_STATE))
    grads_small = dict(b_ada=d_b_ada, g_mix=s_g_mix, b_gate=s_b_gate, a_re=d_a_re[None], a_im=d_a_im[None],
                       log_dt=d_log_dt.reshape(1, SSM_GROUPS), b_re=d_b_re2.reshape(b_re.shape),
                       b_im=d_b_im2.reshape(b_im.shape), c_re=s_c_re, c_im=s_c_im, d_skip=s_d_skip, b_glu=s_b_glu,
                       g_ffn=s_g_ffn, b_conv=s_b_conv, g_final=s_g_final)
    flat2 = lambda a: a.reshape(-1, a.shape[-1])
    res = _adamw_small([flat2(args[n]) for n in SMALL_ORDER], [flat2(args["m_" + n]) for n in SMALL_ORDER],
                       [flat2(args["v_" + n]) for n in SMALL_ORDER],
                       [flat2(grads_small[n].reshape(args[n].shape)) for n in SMALL_ORDER])
    for i, n in enumerate(SMALL_ORDER):
        for k, key in enumerate(("grad_", "delta_", "new_m_", "new_v_")):
            out[key + n] = res[4 * i + k].reshape(args[n].shape)

    order = ["w_ada", "b_ada", "g_mix", "w_in", "b_gate", "a_re", "a_im", "log_dt", "b_re", "b_im", "c_re", "c_im",
             "d_skip", "w_glu", "b_glu", "w_proj_att", "w_proj_ssm", "w_out", "g_ffn", "w_up", "w_conv", "b_conv",
             "w_down", "g_final"]
    loss = loss_vec[0, 0]
    return (loss, grad_x, *[out[k + n] for k in ("grad_", "delta_", "new_m_", "new_v_") for n in order])
```

```python
import math

import jax
import jax.numpy as jnp
from jax import lax
from jax.experimental import pallas as pl
from jax.experimental.pallas import tpu as pltpu

F32 = jnp.float32
BF16 = jnp.bfloat16

N_DEV = 8
D_MODEL = 1024
N_HEADS = 8
HEAD_DIM = 64
ATT_WIDTH = N_HEADS * HEAD_DIM
DILATIONS = (1, 4, 16)
WIN = 128
SSM_GROUPS = 16
SSM_GROUP_CH = 16
SSM_WIDTH = SSM_GROUPS * SSM_GROUP_CH
SSM_STATE = 64
SSM_COLS = SSM_GROUPS * SSM_STATE
D_FF = 2048
EPS = 1e-6
NEG_INF = -1e30
ADAM_LR, ADAM_B1, ADAM_B2, ADAM_EPS, ADAM_WD, ADAM_STEP = 0.001, 0.9, 0.999, 1e-08, 0.01, 10

V7X_VMEM_LIMIT = 56 * 1024 * 1024
LANES = 128


def _params(n_grid):
    return pltpu.CompilerParams(dimension_semantics=("arbitrary",) * n_grid,
                                vmem_limit_bytes=V7X_VMEM_LIMIT)


def _tile(n, pref):
    if n <= pref:
        return n
    t = (pref // LANES) * LANES
    while t > 0:
        if n % t == 0:
            return t
        t -= LANES
    return n


def _matmul(a, b, *, ta=False, tb=False, out_dtype=F32, name):
    if ta:
        K, M = a.shape
    else:
        M, K = a.shape
    if tb:
        N, K2 = b.shape
    else:
        K2, N = b.shape
    assert K == K2, (a.shape, b.shape)
    if ta:
        tm, tn, tk = _tile(M, 1024), _tile(N, 2048), _tile(K, 1024)
    else:
        tm, tk = _tile(M, 512), _tile(K, 4096)
        tn = _tile(N, 2048 if K <= 2048 else 1024)
    nk = K // tk
    dn = (((0,) if ta else (1,), (1,) if tb else (0,)), ((), ()))

    def body(a_ref, b_ref, o_ref, acc_ref):
        k = pl.program_id(2)
        part = lax.dot_general(a_ref[...].astype(BF16), b_ref[...].astype(BF16), dn, preferred_element_type=F32)
        if nk == 1:
            o_ref[...] = part.astype(o_ref.dtype)
            return

        @pl.when(k == 0)
        def _():
            acc_ref[...] = jnp.zeros_like(acc_ref)

        acc_ref[...] += part

        @pl.when(k == nk - 1)
        def _():
            o_ref[...] = acc_ref[...].astype(o_ref.dtype)

    a_spec = (pl.BlockSpec((tk, tm), lambda j, i, k: (k, i)) if ta
              else pl.BlockSpec((tm, tk), lambda j, i, k: (i, k)))
    b_spec = (pl.BlockSpec((tn, tk), lambda j, i, k: (j, k)) if tb
              else pl.BlockSpec((tk, tn), lambda j, i, k: (k, j)))
    return pl.pallas_call(
        body, name=name, grid=(N // tn, M // tm, nk),
        in_specs=[a_spec, b_spec],
        out_specs=pl.BlockSpec((tm, tn), lambda j, i, k: (i, j)),
        out_shape=jax.ShapeDtypeStruct((M, N), out_dtype),
        scratch_shapes=[pltpu.VMEM((tm, tn) if nk > 1 else (8, LANES), F32)],
        compiler_params=_params(3),
    )(a, b)


HALF = 256
UP_SLOTS = N_DEV // 2
UP_GROUP = 4 * HALF


def _up_weight_spec(K, index):
    return pl.BlockSpec((2, None, K, 2 * HALF), index)


def _up_dw(a, d, name):
    M, K = a.shape
    tk = _tile(M, 1024)
    nk = M // tk

    def body(a_ref, d_ref, o_ref, acc_ref):
        k = pl.program_id(1)

        @pl.when(k == 0)
        def _():
            acc_ref[...] = jnp.zeros_like(acc_ref)

        acc_ref[...] += lax.dot_general(a_ref[...], d_ref[...], _TN, preferred_element_type=F32)

        @pl.when(k == nk - 1)
        def _():
            for half in range(2):
                for part in range(2):
                    lo = (2 * half + part) * HALF
                    o_ref[part, :, half * HALF:(half + 1) * HALF] = acc_ref[:, lo:lo + HALF].astype(o_ref.dtype)

    out = pl.pallas_call(
        body, name=name, grid=(UP_SLOTS, nk),
        in_specs=[pl.BlockSpec((tk, K), lambda j, k: (k, 0)), pl.BlockSpec((tk, UP_GROUP), lambda j, k: (k, j))],
        out_specs=_up_weight_spec(K, lambda j, k: (0, j, 0, 0)),
        out_shape=jax.ShapeDtypeStruct((2, UP_SLOTS, K, 2 * HALF), BF16),
        scratch_shapes=[pltpu.VMEM((K, UP_GROUP), F32)], compiler_params=_params(2),
    )(a, d)
    return out.reshape(N_DEV, K, 2 * HALF)


def _rowwise(fn, rows, bvecs, consts, out_rows, out_b, out_g, *, ts, name, raw=(), split=1):
    B, S = rows[0][0].shape[:2]
    nin = len(rows) + len(bvecs) + len(consts)
    nr, nb, ng = len(out_rows), len(out_b), len(out_g)
    sub = ts // split

    def body(*refs):
        b = pl.program_id(0)
        s = pl.program_id(1)
        orefs = refs[nin:]
        fixed = [r[...] for r in refs[len(rows):nin]]
        sums = None
        for h in range(split):
            sec = slice(h * sub, (h + 1) * sub)
            vals = [r[sec, :] if i in raw else r[sec, :].astype(F32) for i, r in enumerate(refs[:len(rows)])]
            outs = fn(*vals, *fixed)
            if not isinstance(outs, (tuple, list)):
                outs = (outs,)
            for i in range(nr):
                orefs[i][sec, :] = outs[i].astype(orefs[i].dtype)
            part = list(outs[nr:])
            sums = part if sums is None else [a + c for a, c in zip(sums, part)]
        for i in range(nb):
            ref = orefs[nr + i]

            @pl.when(s == 0)
            def _(ref=ref):
                ref[...] = jnp.zeros_like(ref)

            ref[...] += sums[i]
        for i in range(ng):
            ref = orefs[nr + nb + i]

            @pl.when((s == 0) & (b == 0))
            def _(ref=ref):
                ref[...] = jnp.zeros_like(ref)

            ref[...] += sums[nb + i]

    rows = [r if len(r) == 4 else r + (0,) for r in rows]
    in_specs = ([pl.BlockSpec((None, ts, cb), lambda b, s, ci=ci, b0=b0: (b0 + b, s, ci)) for (_, cb, ci, b0) in rows]
                + [pl.BlockSpec((None, 1, cb), lambda b, s, ci=ci: (b, 0, ci)) for (_, cb, ci) in bvecs]
                + [pl.BlockSpec(a.shape, lambda b, s: (0, 0)) for a in consts])
    out_shape = ([jax.ShapeDtypeStruct((B, S, c), dt) for (c, dt) in out_rows]
                 + [jax.ShapeDtypeStruct((B, 1, c), F32) for c in out_b]
                 + [jax.ShapeDtypeStruct(rc, F32) for rc in out_g])
    out_specs = ([pl.BlockSpec((None, ts, c), lambda b, s: (b, s, 0)) for (c, _) in out_rows]
                 + [pl.BlockSpec((None, 1, c), lambda b, s: (b, 0, 0)) for c in out_b]
                 + [pl.BlockSpec(rc, lambda b, s: (0, 0)) for rc in out_g])
    args = [r[0] for r in rows] + [a for (a, _, _) in bvecs] + list(consts)
    return pl.pallas_call(
        body, name=name, grid=(B, S // ts), in_specs=in_specs, out_specs=out_specs,
        out_shape=out_shape, compiler_params=_params(2),
    )(*args)


def _col_sum(v):
    return jnp.sum(v, axis=0, keepdims=True)


def _rms_scale(h):
    return lax.rsqrt(jnp.mean(h * h, axis=-1, keepdims=True) + EPS)


def _sigmoid(v):
    return 0.5 * (1.0 + jnp.tanh(0.5 * v))


ATT_SCALE = HEAD_DIM ** -0.5
COPY_ROWS = 256
_NT = (((1,), (1,)), ((), ()))
_TN = (((0,), (0,)), ((), ()))


def _row_chunks(d, seq):
    sub = seq // d
    out = []
    for r in range(d):
        for c0 in range(0, sub, COPY_ROWS):
            n = min(COPY_ROWS, sub - c0)
            out.append((pl.ds(r + c0 * d, n, stride=d), r * sub + c0, n))
    return out


ATT_UNROLL = 16
KEYS = 2 * WIN


def _zero_once(refs):
    @pl.when((pl.program_id(0) == 0) & (pl.program_id(1) == 0))
    def _():
        for r in refs:
            r[...] = jnp.zeros_like(r)


def _pair_bias(bias_ref, slopes_ref, hp, d, key_major):
    shape = (KEYS, WIN) if key_major else (WIN, KEYS)
    qi = lax.broadcasted_iota(jnp.int32, shape, 1 if key_major else 0)
    kj = lax.broadcasted_iota(jnp.int32, shape, 0 if key_major else 1)
    dist = WIN + qi - kj
    valid = (dist >= 0) & (dist <= WIN)
    distf = dist.astype(F32)
    for h in range(2):
        slope_d = slopes_ref[2 * hp + h] * float(d)
        with_prev = jnp.where(valid, -(slope_d * distf), NEG_INF)
        no_prev = jnp.where(kj >= WIN, with_prev, NEG_INF)
        span = slice(h * KEYS, (h + 1) * KEYS)
        if key_major:
            bias_ref[1, span, :] = with_prev
            bias_ref[0, span, :] = no_prev
        else:
            bias_ref[1, :, span] = with_prev
            bias_ref[0, :, span] = no_prev


def _stack_heads(v):
    first = lax.broadcasted_iota(jnp.int32, v.shape, 1) < HEAD_DIM
    zero = jnp.zeros_like(v)
    return jnp.concatenate([jnp.where(first, v, zero), jnp.where(first, zero, v)], axis=0)


def _per_head(c0, c1, n):
    return jnp.where(lax.broadcasted_iota(jnp.int32, (n, LANES), 1) < HEAD_DIM, c0, c1)


def _qkv_spec(seq, j):
    return pl.BlockSpec((None, seq, LANES), lambda b, hp: (b, 0, 3 * hp + j))


def _attention_fwd(qkv, slopes):
    B, S, _ = qkv.shape
    n_blk = S // WIN
    n_pair = N_HEADS // 2

    def body(slopes_ref, q_ref, k_ref, v_ref, o_ref, lse_ref, qp, kp, vp, bias, acc, mx, sm, acc_n, mx_n, sm_n):
        hp = pl.program_id(1)
        _zero_once((kp, vp))
        for p, d in enumerate(DILATIONS):
            nb = n_blk // d
            chunks = _row_chunks(d, S)
            for src, dst, n in chunks:
                qp[dst:dst + n, :] = (q_ref[src, :] * ATT_SCALE).astype(BF16)
                kp[WIN + dst:WIN + dst + n, :] = k_ref[src, :].astype(BF16)
                vp[WIN + dst:WIN + dst + n, :] = v_ref[src, :].astype(BF16)
            _pair_bias(bias, slopes_ref, hp, d, key_major=False)
            acc_t, mx_t, sm_t = (acc_n, mx_n, sm_n) if d == 1 else (acc, mx, sm)

            nk = WIN if nb == 1 else KEYS
            bias_cur = jnp.concatenate([bias[0, :, WIN:KEYS], bias[0, :, KEYS + WIN:]], axis=1) if nb == 1 else None

            def block(i, carry, p=p, nb=nb, nk=nk, bias_cur=bias_cur, acc_t=acc_t, mx_t=mx_t, sm_t=sm_t):
                cur = pl.ds(pl.multiple_of(i * WIN, WIN), WIN)
                keys = pl.ds(pl.multiple_of(i * WIN + (KEYS - nk), WIN), nk)
                s = lax.dot_general(qp[cur, :], _stack_heads(kp[keys, :]), _NT, preferred_element_type=F32)
                s = s + (bias_cur if nb == 1 else bias[((i % nb) > 0).astype(jnp.int32)])
                es, ms, ls = [], [], []
                for h in range(2):
                    sh = s[:, h * nk:(h + 1) * nk]
                    m = jnp.max(sh if nb == 1 else jnp.maximum(sh[:, :WIN], sh[:, WIN:]), axis=1, keepdims=True)
                    e = jnp.exp(sh - m)
                    es.append(e.astype(BF16))
                    ms.append(m)
                    ls.append(jnp.sum(e if nb == 1 else e[:, :WIN] + e[:, WIN:], axis=1, keepdims=True))
                acc_t[p, cur, :] = jnp.dot(jnp.concatenate(es, axis=1), _stack_heads(vp[keys, :]),
                                           preferred_element_type=F32)
                mx_t[p, cur, :] = _per_head(ms[0], ms[1], WIN)
                sm_t[p, cur, :] = _per_head(ls[0], ls[1], WIN)
                return carry

            lax.fori_loop(0, n_blk, block, 0, unroll=ATT_UNROLL)
            if d > 1:
                for src, dst, n in chunks:
                    acc_n[p, src, :] = acc[p, dst:dst + n, :]
                    mx_n[p, src, :] = mx[p, dst:dst + n, :]
                    sm_n[p, src, :] = sm[p, dst:dst + n, :]

        chunk = 256

        def merge(i, carry):
            rows = pl.ds(pl.multiple_of(i * chunk, chunk), chunk)
            ms = [mx_n[p, rows, :] for p in range(3)]
            m = jnp.maximum(jnp.maximum(ms[0], ms[1]), ms[2])
            ws = [jnp.exp(mp - m) for mp in ms]
            l = ws[0] * sm_n[0, rows, :] + ws[1] * sm_n[1, rows, :] + ws[2] * sm_n[2, rows, :]
            o = (ws[0] * acc_n[0, rows, :] + ws[1] * acc_n[1, rows, :] + ws[2] * acc_n[2, rows, :]) / l
            o_ref[rows, :] = o.astype(o_ref.dtype)
            lse = m + jnp.log(l)
            for h in range(2):
                lse_ref[rows, h:h + 1] = lse[:, h * HEAD_DIM:h * HEAD_DIM + 1]
            return carry

        lax.fori_loop(0, S // chunk, merge, 0)

    return pl.pallas_call(
        body, name="attention_fwd", grid=(B, n_pair),
        in_specs=[pl.BlockSpec(memory_space=pltpu.SMEM), _qkv_spec(S, 0), _qkv_spec(S, 1), _qkv_spec(S, 2)],
        out_specs=[pl.BlockSpec((None, S, LANES), lambda b, hp: (b, 0, hp)),
                   pl.BlockSpec((None, None, S, 2), lambda b, hp: (b, hp, 0, 0))],
        out_shape=[jax.ShapeDtypeStruct((B, S, ATT_WIDTH), BF16),
                   jax.ShapeDtypeStruct((B, n_pair, S, 2), F32)],
        scratch_shapes=[pltpu.VMEM((S, LANES), BF16), pltpu.VMEM((S + WIN, LANES), BF16),
                        pltpu.VMEM((S + WIN, LANES), BF16), pltpu.VMEM((2, WIN, 2 * KEYS), F32)]
        + [pltpu.VMEM((3, S, LANES), F32)] * 6,
        compiler_params=_params(2),
    )(slopes, qkv, qkv, qkv)


def _attention_bwd(qkv, o, do, lse, slopes):
    B, S, _ = qkv.shape
    n_blk = S // WIN
    n_pair = N_HEADS // 2

    def body(slopes_ref, q_ref, k_ref, v_ref, o_ref, do_ref, lse_ref, dx_ref,
             qp, dop, kp, vp, aux, auxp, aux_t, bias_t, dqp, dvk, dq_n, dk_n, dv_n):
        hp = pl.program_id(1)
        aux[...] = jnp.zeros_like(aux)
        for c0 in range(0, S, COPY_ROWS):
            rows = slice(c0, c0 + COPY_ROWS)
            prod = do_ref[rows, :] * o_ref[rows, :].astype(F32)
            for h in range(2):
                aux[rows, 2 * h:2 * h + 1] = lse_ref[rows, h:h + 1]
                aux[rows, 2 * h + 1:2 * h + 2] = jnp.sum(prod[:, h * HEAD_DIM:(h + 1) * HEAD_DIM], axis=1,
                                                         keepdims=True)
        dq_n[...] = jnp.zeros_like(dq_n)
        dk_n[...] = jnp.zeros_like(dk_n)
        dv_n[...] = jnp.zeros_like(dv_n)
        _zero_once((kp, vp))
        for p, d in enumerate(DILATIONS):
            nb = n_blk // d
            chunks = _row_chunks(d, S)
            for src, dst, n in chunks:
                auxp[dst:dst + n, :] = aux[src, :]
                qp[dst:dst + n, :] = (q_ref[src, :] * ATT_SCALE).astype(BF16)
                dop[dst:dst + n, :] = do_ref[src, :].astype(BF16)
                kp[WIN + dst:WIN + dst + n, :] = k_ref[src, :].astype(BF16)
                vp[WIN + dst:WIN + dst + n, :] = v_ref[src, :].astype(BF16)
            for i in range(n_blk):
                aux_t[i] = auxp[i * WIN:(i + 1) * WIN, :].T[0:8, :]
            _pair_bias(bias_t, slopes_ref, hp, d, key_major=True)
            dvk[...] = jnp.zeros_like(dvk)

            nk = WIN if nb == 1 else KEYS
            bias_cur = jnp.concatenate([bias_t[0, WIN:KEYS, :], bias_t[0, KEYS + WIN:, :]], axis=0) if nb == 1 else None

            def block(i, carry, nb=nb, nk=nk, bias_cur=bias_cur):
                cur = pl.ds(pl.multiple_of(i * WIN, WIN), WIN)
                keys = pl.ds(pl.multiple_of(i * WIN + (KEYS - nk), WIN), nk)
                q2, do2 = qp[cur, :], dop[cur, :]
                kc = _stack_heads(kp[keys, :])
                s_t = lax.dot_general(kc, q2, _NT, preferred_element_type=F32)
                s_t = s_t + (bias_cur if nb == 1 else bias_t[((i % nb) > 0).astype(jnp.int32)])
                dp_t = lax.dot_general(_stack_heads(vp[keys, :]), do2, _NT, preferred_element_type=F32)
                ps, dss = [], []
                for h in range(2):
                    span = slice(h * nk, (h + 1) * nk)
                    p_t = jnp.exp(s_t[span, :] - aux_t[i, 2 * h:2 * h + 1, :])
                    ds_t = p_t * (dp_t[span, :] - aux_t[i, 2 * h + 1:2 * h + 2, :])
                    ps.append(p_t.astype(BF16))
                    dss.append(ds_t.astype(BF16))
                do_rows, q_rows = _stack_heads(do2), _stack_heads(q2)
                zr = jnp.zeros_like(do_rows)
                rhs = jnp.concatenate([jnp.concatenate([do_rows, zr], axis=1),
                                       jnp.concatenate([zr, q_rows], axis=1)], axis=0)
                dvk[keys, :] += jnp.dot(jnp.concatenate(ps + dss, axis=1), rhs, preferred_element_type=F32)
                dqp[cur, :] = lax.dot_general(jnp.concatenate(dss, axis=0), kc, _TN, preferred_element_type=F32)
                return carry

            lax.fori_loop(0, n_blk, block, 0, unroll=ATT_UNROLL)
            for src, dst, n in chunks:
                dq_n[src, :] += dqp[dst:dst + n, :]
                dv_n[src, :] += dvk[WIN + dst:WIN + dst + n, :LANES]
                dk_n[src, :] += dvk[WIN + dst:WIN + dst + n, LANES:]
        for c0 in range(0, S, COPY_ROWS):
            rows = slice(c0, c0 + COPY_ROWS)
            dx_ref[rows, 0:LANES] = (dq_n[rows, :] * ATT_SCALE).astype(dx_ref.dtype)
            dx_ref[rows, LANES:2 * LANES] = dk_n[rows, :].astype(dx_ref.dtype)
            dx_ref[rows, 2 * LANES:3 * LANES] = dv_n[rows, :].astype(dx_ref.dtype)

    pair = lambda width: pl.BlockSpec((None, S, width), lambda b, hp: (b, 0, hp))
    vm = lambda shape, dt: pltpu.VMEM(shape, dt)
    return pl.pallas_call(
        body, name="attention_bwd", grid=(B, n_pair),
        in_specs=[pl.BlockSpec(memory_space=pltpu.SMEM), _qkv_spec(S, 0), _qkv_spec(S, 1), _qkv_spec(S, 2),
                  pair(LANES), pair(LANES), pl.BlockSpec((None, None, S, 2), lambda b, hp: (b, hp, 0, 0))],
        out_specs=pair(3 * LANES),
        out_shape=jax.ShapeDtypeStruct((B, S, 3 * ATT_WIDTH), BF16),
        scratch_shapes=[vm((S, LANES), BF16), vm((S, LANES), BF16),
                        vm((S + WIN, LANES), BF16), vm((S + WIN, LANES), BF16),
                        vm((S, LANES), F32), vm((S, LANES), F32), vm((n_blk, 8, WIN), F32),
                        vm((2, 2 * KEYS, WIN), F32),
                        vm((S, LANES), F32), vm((S + WIN, 2 * LANES), F32),
                        vm((S, LANES), F32), vm((S, LANES), F32), vm((S, LANES), F32)],
        compiler_params=_params(2),
    )(slopes, qkv, qkv, qkv, o, do, lse)


SCAN_COLS = 256
SCAN_ROWS = 8


def _rows_to_tile(rows):
    rid = lax.broadcasted_iota(jnp.int32, (SCAN_ROWS, rows[0].shape[1]), 0)
    tile = jnp.broadcast_to(rows[0], rid.shape)
    for k in range(1, SCAN_ROWS):
        tile = jnp.where(rid == k, rows[k], tile)
    return tile


SCAN_UNROLL = 4


def _complex_powers(ar, ai, n):
    out = [(ar, ai)]
    for _ in range(n - 1):
        pr, pi = out[-1]
        out.append((pr * ar - pi * ai, pr * ai + pi * ar))
    return out


def _round_multipliers(powers, rid, reverse):
    out = []
    for s in (1, 2, 4):
        keep = (rid < SCAN_ROWS - s) if reverse else (rid >= s)
        out.append((jnp.where(keep, powers[s - 1][0], 0.0), jnp.where(keep, powers[s - 1][1], 0.0)))
    return out


def _tile_scan(xr, xi, multipliers, reverse):
    for s, (mr, mi) in zip((1, 2, 4), multipliers):
        shift = SCAN_ROWS - s if reverse else s
        sr, si = pltpu.roll(xr, shift, 0), pltpu.roll(xi, shift, 0)
        xr, xi = xr + (mr * sr - mi * si), xi + (mr * si + mi * sr)
    return xr, xi


SCAN_CHUNK = 256


def _scan_fwd(us, bb_big, a_row, cc_big):
    B, S, _ = us.shape
    groups = 2
    width = 2 * groups * SCAN_COLS
    nc = 2 * SSM_COLS // width
    nt = S // SCAN_ROWS
    tiles = SCAN_CHUNK // SCAN_ROWS
    LAST = slice(SCAN_ROWS - 1, SCAN_ROWS)

    def body(us_ref, bb_ref, a_ref, cc_ref, xs_ref, y_ref, bu_ref):
        bb = bb_ref[...].astype(BF16)
        for c in range(S // SCAN_CHUNK):
            part = jnp.dot(us_ref[c * SCAN_CHUNK:(c + 1) * SCAN_CHUNK, :].astype(BF16), bb,
                           preferred_element_type=F32)
            bu_ref[c * tiles:(c + 1) * tiles] = part.reshape(tiles, SCAN_ROWS, width)
        rid = lax.broadcasted_iota(jnp.int32, (SCAN_ROWS, SCAN_COLS), 0)
        consts = []
        for g in range(groups):
            re = slice(2 * g * SCAN_COLS, (2 * g + 1) * SCAN_COLS)
            im = slice((2 * g + 1) * SCAN_COLS, (2 * g + 2) * SCAN_COLS)
            powers = _complex_powers(a_ref[:, re], a_ref[:, im], SCAN_ROWS)
            carry_mult = (_rows_to_tile([p[0] for p in powers]), _rows_to_tile([p[1] for p in powers]))
            consts.append((re, im, carry_mult, _round_multipliers(powers, rid, reverse=False)))

        def tile(i, carry):
            out = []
            for (re, im, (cr_t, ci_t), rounds), (cr, ci) in zip(consts, carry):
                xr, xi = _tile_scan(bu_ref[i, :, re], bu_ref[i, :, im], rounds, reverse=False)
                xs_ref[i, :, re] = xr + (cr_t * cr - ci_t * ci)
                xs_ref[i, :, im] = xi + (cr_t * ci + ci_t * cr)
                out.append((xs_ref[i, LAST, re], xs_ref[i, LAST, im]))
            return tuple(out)

        zero = jnp.zeros((1, SCAN_COLS), F32)
        lax.fori_loop(0, nt, tile, ((zero, zero),) * groups, unroll=SCAN_UNROLL)

        @pl.when(pl.program_id(1) == 0)
        def _():
            y_ref[...] = jnp.zeros_like(y_ref)

        cc = cc_ref[...].astype(BF16)
        for c in range(S // SCAN_CHUNK):
            x2 = xs_ref[c * tiles:(c + 1) * tiles].reshape(SCAN_CHUNK, width).astype(BF16)
            y_ref[c * SCAN_CHUNK:(c + 1) * SCAN_CHUNK, :] += jnp.dot(x2, cc, preferred_element_type=F32)

    col = pl.BlockSpec((None, nt, SCAN_ROWS, width), lambda b, j: (b, 0, 0, j))
    tok = pl.BlockSpec((None, S, SSM_WIDTH), lambda b, j: (b, 0, 0))
    xs, y = pl.pallas_call(
        body, name="s5_scan_fwd", grid=(B, nc),
        in_specs=[tok, pl.BlockSpec((SSM_WIDTH, width), lambda b, j: (0, j)),
                  pl.BlockSpec((1, width), lambda b, j: (0, j)), pl.BlockSpec((width, SSM_WIDTH), lambda b, j: (j, 0))],
        out_specs=[col, tok],
        out_shape=[jax.ShapeDtypeStruct((B, nt, SCAN_ROWS, 2 * SSM_COLS), F32),
                   jax.ShapeDtypeStruct((B, S, SSM_WIDTH), F32)],
        scratch_shapes=[pltpu.VMEM((nt, SCAN_ROWS, width), F32)],
        compiler_params=_params(2),
    )(us, bb_big, a_row, cc_big)
    return xs.reshape(B, S, 2 * SSM_COLS), y


def _scan_bwd(dy, us, bb_big, cc_big, xs, a_row):
    B, S, _ = dy.shape
    width = 2 * SCAN_COLS
    nc = SSM_COLS // SCAN_COLS
    nt = S // SCAN_ROWS
    tiles = SCAN_CHUNK // SCAN_ROWS
    RE, IM = slice(0, SCAN_COLS), slice(SCAN_COLS, 2 * SCAN_COLS)
    FIRST, LAST = slice(0, 1), slice(SCAN_ROWS - 1, SCAN_ROWS)

    def body(dy_ref, us_ref, bb_ref, cc_ref, x_ref, a_ref, dus_ref, ga_ref, dbb_ref, dcc_ref, d_ref, lam_ref):
        b = pl.program_id(1)
        cc = cc_ref[...].astype(BF16)
        for c in range(S // SCAN_CHUNK):
            part = lax.dot_general(dy_ref[c * SCAN_CHUNK:(c + 1) * SCAN_CHUNK, :].astype(BF16), cc, _NT,
                                   preferred_element_type=F32)
            d_ref[c * tiles:(c + 1) * tiles] = part.reshape(tiles, SCAN_ROWS, width)
        powers = _complex_powers(a_ref[:, RE], -a_ref[:, IM], SCAN_ROWS)
        rid = lax.broadcasted_iota(jnp.int32, (SCAN_ROWS, SCAN_COLS), 0)
        cr_t = _rows_to_tile([powers[SCAN_ROWS - 1 - r][0] for r in range(SCAN_ROWS)])
        ci_t = _rows_to_tile([powers[SCAN_ROWS - 1 - r][1] for r in range(SCAN_ROWS)])
        rounds = _round_multipliers(powers, rid, reverse=True)

        @pl.when(b == 0)
        def _():
            ga_ref[...] = jnp.zeros_like(ga_ref)
            dbb_ref[...] = jnp.zeros_like(dbb_ref)
            dcc_ref[...] = jnp.zeros_like(dcc_ref)

        def tile(j, carry):
            cr, ci, accr, acci = carry
            i = nt - 1 - j
            lr, li = _tile_scan(d_ref[i, :, RE], d_ref[i, :, IM], rounds, reverse=True)
            lam_r = lr + (cr_t * cr - ci_t * ci)
            lam_i = li + (cr_t * ci + ci_t * cr)
            lam_ref[i, :, RE] = lam_r
            lam_ref[i, :, IM] = lam_i
            ip = jnp.maximum(i - 1, 0)
            keep = (i > 0).astype(F32)
            xpr = jnp.where(rid == 0, x_ref[ip, LAST, RE] * keep, pltpu.roll(x_ref[i, :, RE], 1, 0))
            xpi = jnp.where(rid == 0, x_ref[ip, LAST, IM] * keep, pltpu.roll(x_ref[i, :, IM], 1, 0))
            accr = accr + lam_r * xpr + lam_i * xpi
            acci = acci + lam_i * xpr - lam_r * xpi
            return lam_ref[i, FIRST, RE], lam_ref[i, FIRST, IM], accr, acci

        z1 = jnp.zeros((1, SCAN_COLS), F32)
        z8 = jnp.zeros((SCAN_ROWS, SCAN_COLS), F32)
        _, _, accr, acci = lax.fori_loop(0, nt, tile, (z1, z1, z8, z8), unroll=SCAN_UNROLL)
        ga_ref[:, RE] += _col_sum(accr)
        ga_ref[:, IM] += _col_sum(acci)

        bb = bb_ref[...].astype(BF16)
        for c in range(S // SCAN_CHUNK):
            rows = slice(c * SCAN_CHUNK, (c + 1) * SCAN_CHUNK)
            lam2 = lam_ref[c * tiles:(c + 1) * tiles].reshape(SCAN_CHUNK, width).astype(BF16)
            x2 = x_ref[c * tiles:(c + 1) * tiles].reshape(SCAN_CHUNK, width).astype(BF16)
            dus_ref[rows, :] = lax.dot_general(lam2, bb, _NT, preferred_element_type=F32)
            dbb_ref[...] += lax.dot_general(us_ref[rows, :].astype(BF16), lam2, _TN, preferred_element_type=F32)
            dcc_ref[...] += lax.dot_general(x2, dy_ref[rows, :].astype(BF16), _TN, preferred_element_type=F32)

    col = pl.BlockSpec((None, nt, SCAN_ROWS, width), lambda j, b: (b, 0, 0, j))
    tok = pl.BlockSpec((None, S, SSM_WIDTH), lambda j, b: (b, 0, 0))
    scratch = pltpu.VMEM((nt, SCAN_ROWS, width), F32)
    return pl.pallas_call(
        body, name="s5_scan_bwd", grid=(nc, B),
        in_specs=[tok, tok, pl.BlockSpec((SSM_WIDTH, width), lambda j, b: (0, j)),
                  pl.BlockSpec((width, SSM_WIDTH), lambda j, b: (j, 0)), col,
                  pl.BlockSpec((1, width), lambda j, b: (0, j))],
        out_specs=[pl.BlockSpec((None, None, S, SSM_WIDTH), lambda j, b: (j, b, 0, 0)),
                   pl.BlockSpec((1, width), lambda j, b: (0, j)),
                   pl.BlockSpec((SSM_WIDTH, width), lambda j, b: (0, j)),
                   pl.BlockSpec((width, SSM_WIDTH), lambda j, b: (j, 0))],
        out_shape=[jax.ShapeDtypeStruct((nc, B, S, SSM_WIDTH), F32), jax.ShapeDtypeStruct((1, 2 * SSM_COLS), F32),
                   jax.ShapeDtypeStruct((SSM_WIDTH, 2 * SSM_COLS), F32),
                   jax.ShapeDtypeStruct((2 * SSM_COLS, SSM_WIDTH), F32)],
        scratch_shapes=[scratch, scratch],
        compiler_params=_params(2),
    )(dy, us, bb_big, cc_big, xs.reshape(B, nt, SCAN_ROWS, 2 * SSM_COLS), a_row)


def _s5_discretise(lr, li, log_dt):
    dt = jnp.exp(log_dt)
    mag = jnp.exp(lr * dt)
    ang = li * dt
    ab_re, ab_im = mag * jnp.cos(ang), mag * jnp.sin(ang)
    nr, ni = ab_re - 1.0, ab_im
    den = lr * lr + li * li
    f_re = (nr * lr + ni * li) / den
    f_im = (ni * lr - nr * li) / den
    return dt, ab_re, ab_im, nr, ni, den, f_re, f_im


def _s5_params(a_re, a_im, log_dt):
    def body(lr_ref, li_ref, ld_ref, abr, abi, fr, fi):
        _, ab_re, ab_im, _, _, _, f_re, f_im = _s5_discretise(lr_ref[...], li_ref[...], ld_ref[...])
        abr[...] = ab_re
        abi[...] = ab_im
        fr[...] = f_re
        fi[...] = f_im

    return pl.pallas_call(body, name="s5_params",
                          out_shape=[jax.ShapeDtypeStruct(a_re.shape, F32)] * 4)(a_re, a_im, log_dt)


def _s5_input_matrix(f_re, f_im, b_re, b_im):
    def body(fr, fi, br, bi, o_re, o_im):
        o_re[...] = fr[...] * br[...] - fi[...] * bi[...]
        o_im[...] = fr[...] * bi[...] + fi[...] * br[...]

    return pl.pallas_call(body, name="s5_input_matrix",
                          out_shape=[jax.ShapeDtypeStruct(b_re.shape, F32)] * 2)(f_re, f_im, b_re, b_im)


def _s5_input_matrix_bwd(f_re, f_im, b_re, b_im, g_re, g_im):
    def body(fr, fi, br, bi, gr, gi, dbr, dbi, dfr, dfi):
        dbr[...] = fr[...] * gr[...] + fi[...] * gi[...]
        dbi[...] = fr[...] * gi[...] - fi[...] * gr[...]
        dfr[...] = jnp.sum(br[...] * gr[...] + bi[...] * gi[...], axis=1, keepdims=True)
        dfi[...] = jnp.sum(br[...] * gi[...] - bi[...] * gr[...], axis=1, keepdims=True)

    return pl.pallas_call(
        body, name="s5_input_matrix_bwd",
        out_shape=[jax.ShapeDtypeStruct(b_re.shape, F32)] * 2 + [jax.ShapeDtypeStruct(f_re.shape, F32)] * 2,
    )(f_re, f_im, b_re, b_im, g_re, g_im)


def _s5_params_bwd(a_re, a_im, log_dt, g_ab_re, g_ab_im, d_f_re, d_f_im):
    def body(lr_ref, li_ref, ld_ref, gar, gai, dfr, dfi, o_lr, o_li, o_ld):
        lr, li = lr_ref[...], li_ref[...]
        dt, ab_re, ab_im, nr, ni, den, f_re, f_im = _s5_discretise(lr, li, ld_ref[...])
        d_fr, d_fi = dfr[...], dfi[...]
        d_nr = (d_fr * lr - d_fi * li) / den
        d_ni = (d_fr * li + d_fi * lr) / den
        common = (d_fr * f_re + d_fi * f_im) * 2.0 / den
        d_lr = (d_fr * nr + d_fi * ni) / den - common * lr
        d_li = (d_fr * ni - d_fi * nr) / den - common * li
        d_abr = gar[...] + d_nr
        d_abi = gai[...] + d_ni
        d_mag_mag = d_abr * ab_re + d_abi * ab_im
        d_ang = d_abi * ab_re - d_abr * ab_im
        o_lr[...] = d_lr + d_mag_mag * dt
        o_li[...] = d_li + d_ang * dt
        o_ld[...] = jnp.sum(d_mag_mag * lr + d_ang * li, axis=1, keepdims=True) * dt

    return pl.pallas_call(
        body, name="s5_params_bwd",
        out_shape=[jax.ShapeDtypeStruct(a_re.shape, F32)] * 2 + [jax.ShapeDtypeStruct(log_dt.shape, F32)],
    )(a_re, a_im, log_dt, g_ab_re, g_ab_im, d_f_re, d_f_im)


CONV_COLS = 256


def _shift_down(v, j, row):
    return jnp.where(row >= j, pltpu.roll(v, j, 0), 0.0)


def _shift_up(v, j, row, seq):
    return jnp.where(row < seq - j, pltpu.roll(v, seq - j, 0), 0.0)


def _up_conv_fwd(u, w3, w_conv, b_conv):
    B, S, K = u.shape

    def body(u_ref, w_ref, wc_ref, bc_ref, up_ref, ff_ref):
        uv = u_ref[...]
        row = lax.broadcasted_iota(jnp.int32, (S, CONV_COLS), 0)
        for half in range(2):
            cols = slice(half * HALF, (half + 1) * HALF)
            pair = jnp.dot(uv, jnp.concatenate([w_ref[0, :, cols], w_ref[1, :, cols]], axis=1),
                           preferred_element_type=F32)
            up_ref[:, half * 2 * HALF:(half + 1) * 2 * HALF] = pair.astype(up_ref.dtype)
            a, val = pair[:, :HALF], pair[:, HALF:]
            conv = (bc_ref[:, cols] + wc_ref[0:1, cols] * a + wc_ref[1:2, cols] * _shift_down(a, 1, row)
                    + wc_ref[2:3, cols] * _shift_down(a, 2, row))
            ff_ref[:, cols] = (conv * _sigmoid(conv) * val).astype(ff_ref.dtype)

    return pl.pallas_call(
        body, name="ffn_up_conv_gate", grid=(UP_SLOTS, B),
        in_specs=[pl.BlockSpec((None, S, K), lambda j, b: (b, 0, 0)), _up_weight_spec(K, lambda j, b: (0, j, 0, 0)),
                  pl.BlockSpec((3, 2 * HALF), lambda j, b: (0, j)), pl.BlockSpec((1, 2 * HALF), lambda j, b: (0, j))],
        out_specs=[pl.BlockSpec((None, S, UP_GROUP), lambda j, b: (b, 0, j)),
                   pl.BlockSpec((None, S, 2 * HALF), lambda j, b: (b, 0, j))],
        out_shape=[jax.ShapeDtypeStruct((B, S, UP_SLOTS * UP_GROUP), BF16), jax.ShapeDtypeStruct((B, S, D_FF), BF16)],
        compiler_params=_params(2),
    )(u, w3.reshape(2, UP_SLOTS, K, 2 * HALF), w_conv, b_conv)


def _conv_bwd(up, d_down, w_down, w_conv, b_conv):
    B, S, _ = up.shape
    nj = D_FF // CONV_COLS

    def body(up_ref, dd_ref, wd_ref, w_ref, b_ref, dup_ref, dw_ref, db_ref):
        b = pl.program_id(1)
        a = up_ref[:, :CONV_COLS].astype(F32)
        val = up_ref[:, CONV_COLS:].astype(F32)
        row = lax.broadcasted_iota(jnp.int32, a.shape, 0)
        w0, w1, w2 = w_ref[0:1, :], w_ref[1:2, :], w_ref[2:3, :]
        a1, a2 = _shift_down(a, 1, row), _shift_down(a, 2, row)
        conv = b_ref[...] + w0 * a + w1 * a1 + w2 * a2
        sg = _sigmoid(conv)
        dff = lax.dot_general(dd_ref[...], wd_ref[...], _NT, preferred_element_type=F32)
        d_val = dff * conv * sg
        dc = dff * val * (sg * (1.0 + conv * (1.0 - sg)))
        d_a = w0 * dc + w1 * _shift_up(dc, 1, row, S) + w2 * _shift_up(dc, 2, row, S)
        dup_ref[:, :CONV_COLS] = d_a.astype(dup_ref.dtype)
        dup_ref[:, CONV_COLS:] = d_val.astype(dup_ref.dtype)

        @pl.when(b == 0)
        def _():
            dw_ref[...] = jnp.zeros_like(dw_ref)
            db_ref[...] = jnp.zeros_like(db_ref)

        dw_ref[0:1, :] += _col_sum(dc * a)
        dw_ref[1:2, :] += _col_sum(dc * a1)
        dw_ref[2:3, :] += _col_sum(dc * a2)
        db_ref[...] += _col_sum(dc)

    return pl.pallas_call(
        body, name="conv_gate_bwd", grid=(nj, B),
        in_specs=[pl.BlockSpec((None, S, 2 * CONV_COLS), lambda j, b: (b, 0, j)),
                  pl.BlockSpec((None, S, D_MODEL), lambda j, b: (b, 0, 0)),
                  pl.BlockSpec((CONV_COLS, D_MODEL), lambda j, b: (j, 0)),
                  pl.BlockSpec((3, CONV_COLS), lambda j, b: (0, j)),
                  pl.BlockSpec((1, CONV_COLS), lambda j, b: (0, j))],
        out_specs=[pl.BlockSpec((None, S, 2 * CONV_COLS), lambda j, b: (b, 0, j)),
                   pl.BlockSpec((3, CONV_COLS), lambda j, b: (0, j)),
                   pl.BlockSpec((1, CONV_COLS), lambda j, b: (0, j))],
        out_shape=[jax.ShapeDtypeStruct((B, S, 2 * D_FF), BF16), jax.ShapeDtypeStruct((3, D_FF), F32),
                   jax.ShapeDtypeStruct((1, D_FF), F32)],
        compiler_params=_params(2),
    )(up, d_down, w_down, w_conv, b_conv)


def _ada_fwd(c_all, w_ada, b_ada):
    def body(c_ref, w_ref, b_ref, o_ref):
        cv = c_ref[...]
        act = (cv * _sigmoid(cv)).astype(BF16)
        o_ref[...] = jnp.dot(act, w_ref[...].astype(BF16), preferred_element_type=F32) + b_ref[...]

    return pl.pallas_call(body, name="ada_fwd",
                          out_shape=jax.ShapeDtypeStruct((c_all.shape[0], w_ada.shape[1]), F32),
                          compiler_params=pltpu.CompilerParams(vmem_limit_bytes=V7X_VMEM_LIMIT))(c_all, w_ada, b_ada)


def _ada_bwd(c_all, dmod_all, dmod_cols):
    def body(c_ref, dm_ref, dmc_ref, dw_ref, db_ref):
        cv = c_ref[...]
        act = (cv * _sigmoid(cv)).astype(BF16)
        dw_ref[...] = lax.dot_general(act, dmc_ref[...].astype(BF16), _TN, preferred_element_type=F32)
        db_ref[...] = _col_sum(dm_ref[...])

    return pl.pallas_call(
        body, name="ada_bwd",
        out_shape=[jax.ShapeDtypeStruct((c_all.shape[1], dmod_cols.shape[1]), F32),
                   jax.ShapeDtypeStruct((1, dmod_all.shape[1]), F32)],
        compiler_params=pltpu.CompilerParams(vmem_limit_bytes=V7X_VMEM_LIMIT))(c_all, dmod_all, dmod_cols)


def _adamw(w, m, v, g_parts, name, own=None):
    R, C = w.shape
    P = g_parts.shape[0]
    tr = R
    for cand in (256, 128, 64, 32, 16, 8):
        if R % cand == 0 and cand * C * 4 * (P + 8) * 2 <= V7X_VMEM_LIMIT // 2:
            tr = cand
            break
    c1 = 1.0 / (1.0 - ADAM_B1 ** ADAM_STEP)
    c2 = 1.0 / (1.0 - ADAM_B2 ** ADAM_STEP)

    def update(w_ref, m_ref, v_ref, g, og, od, om, ov):
        m_new = ADAM_B1 * m_ref[...] + (1.0 - ADAM_B1) * g
        v_new = ADAM_B2 * v_ref[...] + (1.0 - ADAM_B2) * (g * g)
        og[...] = g
        om[...] = m_new
        ov[...] = v_new
        od[...] = -ADAM_LR * ((m_new * c1) / (jnp.sqrt(v_new * c2) + ADAM_EPS) + ADAM_WD * w_ref[...])

    def total(g_ref):
        g = g_ref[0].astype(F32)
        for p in range(1, P):
            g = g + g_ref[p].astype(F32)
        return g

    out_shape = [jax.ShapeDtypeStruct((R, C), F32)] * 4
    if own is None:
        def body(w_ref, m_ref, v_ref, g_ref, og, od, om, ov):
            update(w_ref, m_ref, v_ref, total(g_ref), og, od, om, ov)

        spec = pl.BlockSpec((tr, C), lambda i: (i, 0))
        return pl.pallas_call(
            body, name=name, grid=(R // tr,),
            in_specs=[spec, spec, spec, pl.BlockSpec((P, tr, C), lambda i: (0, i, 0))],
            out_specs=[spec] * 4, out_shape=out_shape, compiler_params=_params(1),
        )(w, m, v, g_parts)

    slots, me = own

    def body_own(me_ref, w_ref, m_ref, v_ref, g_ref, own_ref, og, od, om, ov):
        g = own_ref[...].astype(F32)
        for p in range(P):
            g = g + jnp.where(me_ref[0] == p, 0.0, g_ref[p].astype(F32))
        update(w_ref, m_ref, v_ref, g, og, od, om, ov)

    spec = pl.BlockSpec((tr, C), lambda i, me_ref: (i, 0))
    grid_spec = pltpu.PrefetchScalarGridSpec(
        num_scalar_prefetch=1, grid=(R // tr,),
        in_specs=[spec, spec, spec, pl.BlockSpec((P, tr, C), lambda i, me_ref: (0, i, 0)),
                  pl.BlockSpec((None, tr, C), lambda i, me_ref: (me_ref[0], i, 0))],
        out_specs=[spec] * 4)
    return pl.pallas_call(body_own, name=name, grid_spec=grid_spec, out_shape=out_shape,
                          compiler_params=_params(1))(me, w, m, v, g_parts, slots)


def _adamw_small(ws, ms, vs, gs):
    n = len(ws)
    c1 = 1.0 / (1.0 - ADAM_B1 ** ADAM_STEP)
    c2 = 1.0 / (1.0 - ADAM_B2 ** ADAM_STEP)

    def body(*refs):
        ins, outs = refs[:4 * n], refs[4 * n:]
        for i in range(n):
            w, m, v, g = ins[i][...], ins[n + i][...], ins[2 * n + i][...], ins[3 * n + i][...]
            m_new = ADAM_B1 * m + (1.0 - ADAM_B1) * g
            v_new = ADAM_B2 * v + (1.0 - ADAM_B2) * (g * g)
            outs[4 * i][...] = g
            outs[4 * i + 1][...] = -ADAM_LR * ((m_new * c1) / (jnp.sqrt(v_new * c2) + ADAM_EPS) + ADAM_WD * w)
            outs[4 * i + 2][...] = m_new
            outs[4 * i + 3][...] = v_new

    out_shape = [jax.ShapeDtypeStruct(w.shape, F32) for w in ws for _ in range(4)]
    return pl.pallas_call(body, name="adamw_small", out_shape=out_shape,
                          compiler_params=pltpu.CompilerParams(vmem_limit_bytes=V7X_VMEM_LIMIT))(*ws, *ms, *vs, *gs)


def _sum_parts(parts, loss_rows):
    P, R, C = parts.shape
    lo, hi = loss_rows

    def body(p_ref, o_ref, loss_ref):
        t = p_ref[0]
        for p in range(1, P):
            t = t + p_ref[p]
        o_ref[...] = t
        tot = jnp.sum(jnp.sum(o_ref[lo:hi, :], axis=1, keepdims=True), axis=0, keepdims=True)
        loss_ref[...] = jnp.broadcast_to(tot, loss_ref.shape)

    return pl.pallas_call(body, name="sum_small_grads",
                          out_shape=[jax.ShapeDtypeStruct((R, C), F32), jax.ShapeDtypeStruct((1, LANES), F32)],
                          compiler_params=pltpu.CompilerParams(vmem_limit_bytes=V7X_VMEM_LIMIT))(parts)


def _exchange(items, name):
    n = len(items)
    MESH = pl.DeviceIdType.MESH

    def body(*refs):
        src, dst = refs[:n], refs[n:2 * n]
        send_sems, recv_sems, local_sems = refs[2 * n:]
        x, y, c = lax.axis_index("x"), lax.axis_index("y"), lax.axis_index("c")
        me = 4 * x + 2 * y + c
        started = []
        for it, (_, per_peer) in enumerate(items):
            own = pltpu.make_async_copy(src[it].at[me] if per_peer else src[it], dst[it].at[me], local_sems.at[it])
            own.start()
            started.append(own)
        sends, recvs = [], []
        for k in range(1, N_DEV):
            px = 1 - x if k & 4 else x
            py = 1 - y if k & 2 else y
            pc = 1 - c if k & 1 else c
            peer = 4 * px + 2 * py + pc
            for it, (_, per_peer) in enumerate(items):
                s = src[it].at[peer] if per_peer else src[it]
                cp = pltpu.make_async_remote_copy(src_ref=s, dst_ref=dst[it].at[me], send_sem=send_sems.at[it, k - 1],
                                                  recv_sem=recv_sems.at[it, k - 1], device_id=(px, py, pc),
                                                  device_id_type=MESH)
                cp.start()
                sends.append(cp)
                recvs.append(pltpu.make_async_remote_copy(
                    src_ref=s, dst_ref=dst[it].at[peer], send_sem=send_sems.at[it, k - 1],
                    recv_sem=recv_sems.at[it, k - 1], device_id=(px, py, pc), device_id_type=MESH))
        for cp in recvs:
            cp.wait_recv()
        for cp in sends:
            cp.wait_send()
        for cp in started:
            cp.wait()

    any_spec = pl.BlockSpec(memory_space=pl.ANY)
    out_shape = []
    for a, per_peer in items:
        shp = a.shape if per_peer else (N_DEV,) + a.shape
        out_shape.append(jax.ShapeDtypeStruct(shp, a.dtype))
    return pl.pallas_call(
        body, name=name, in_specs=[any_spec] * n, out_specs=[any_spec] * n, out_shape=out_shape,
        scratch_shapes=[pltpu.SemaphoreType.DMA((n, N_DEV - 1)), pltpu.SemaphoreType.DMA((n, N_DEV - 1)),
                        pltpu.SemaphoreType.DMA((n,))],
    )(*[a for a, _ in items])


def _remote(src, dst, send_sem, recv_sem, device):
    return pltpu.make_async_remote_copy(src_ref=src, dst_ref=dst, send_sem=send_sem, recv_sem=recv_sem,
                                        device_id=device, device_id_type=pl.DeviceIdType.MESH)


def _mesh_place():
    x, y, c = lax.axis_index("x"), lax.axis_index("y"), lax.axis_index("c")
    other_chips = [(1 - x, y), (x, 1 - y), (1 - x, 1 - y)]
    return x, y, c, (x, y, 1 - c), other_chips


def _gather_all(items, name):
    n = len(items)

    def body(*refs):
        src, dst = refs[:n], refs[n:2 * n]
        send_sems, recv_sems, local_sems = refs[2 * n:]
        x, y, c, sibling, chips = _mesh_place()
        slot = lambda px, py, pc: 4 * px + 2 * py + pc
        me = slot(x, y, c)
        own = [pltpu.make_async_copy(src[it], dst[it].at[me], local_sems.at[it]) for it in range(n)]
        first = []
        for it in range(n):
            first.append(_remote(src[it], dst[it].at[me], send_sems.at[it, 0], recv_sems.at[it, 0], sibling))
            for j, chip in enumerate(chips):
                first.append(_remote(src[it], dst[it].at[me], send_sems.at[it, 1 + j], recv_sems.at[it, 1 + j],
                                     (*chip, c)))
        for cp in own + first:
            cp.start()
        passed = []
        for j, chip in enumerate(chips):
            blk = slot(*chip, c)
            for it in range(n):
                _remote(src[it], dst[it].at[blk], send_sems.at[it, 1 + j], recv_sems.at[it, 1 + j],
                        (*chip, c)).wait_recv()
                fwd = _remote(dst[it].at[blk], dst[it].at[blk], send_sems.at[it, 4 + j], recv_sems.at[it, 4 + j],
                              sibling)
                fwd.start()
                passed.append(fwd)
        for it in range(n):
            _remote(src[it], dst[it].at[slot(x, y, 1 - c)], send_sems.at[it, 0], recv_sems.at[it, 0],
                    sibling).wait_recv()
        for j, chip in enumerate(chips):
            for it in range(n):
                _remote(src[it], dst[it].at[slot(*chip, 1 - c)], send_sems.at[it, 4 + j], recv_sems.at[it, 4 + j],
                        sibling).wait_recv()
        for cp in first + passed:
            cp.wait_send()
        for cp in own:
            cp.wait()

    any_spec = pl.BlockSpec(memory_space=pl.ANY)
    return pl.pallas_call(
        body, name=name, in_specs=[any_spec] * n, out_specs=[any_spec] * n,
        out_shape=[jax.ShapeDtypeStruct((N_DEV,) + a.shape, a.dtype) for a in items],
        scratch_shapes=[pltpu.SemaphoreType.DMA((n, 7)), pltpu.SemaphoreType.DMA((n, 7)),
                        pltpu.SemaphoreType.DMA((n,))],
    )(*items)


def _peers():
    x, y, c = lax.axis_index("x"), lax.axis_index("y"), lax.axis_index("c")
    out = []
    for k in range(1, N_DEV):
        px = 1 - x if k & 4 else x
        py = 1 - y if k & 2 else y
        pc = 1 - c if k & 1 else c
        out.append((k, (px, py, pc), 4 * px + 2 * py + pc))
    return 4 * x + 2 * y + c, out


def _exchange_start(items, name, gather, carry=()):
    n, m = len(items), len(carry)

    def body(*refs):
        src, land = refs[:n], refs[n:2 * n]
        first_out = 2 * n + m
        send_sems, recv_sems = refs[first_out:first_out + n], refs[first_out + n:first_out + 2 * n]
        token = refs[-1]
        me, peers = _peers()
        for k, peer, slot in peers:
            for it in range(n):
                _remote(src[it] if gather else src[it].at[slot], land[it].at[me], send_sems[it], recv_sems[it],
                        peer).start()
        token[...] = jnp.zeros_like(token)

    hbm = pl.BlockSpec(memory_space=pltpu.HBM)
    sem = pl.BlockSpec(memory_space=pltpu.SEMAPHORE)
    land_shapes = [(N_DEV,) + (a.shape if gather else a.shape[1:]) for a in items]
    lands = [lax.empty(shp, a.dtype) for shp, a in zip(land_shapes, items)]
    through = list(items) + lands + list(carry)
    outs = pl.pallas_call(
        body, name=name,
        out_shape=(*[pltpu.SemaphoreType.DMA(())] * (2 * n), *[pltpu.HBM(a.shape, a.dtype) for a in through],
                   jax.ShapeDtypeStruct((8, LANES), F32)),
        in_specs=[hbm] * len(through),
        out_specs=(*[sem] * (2 * n), *[hbm] * len(through), pl.BlockSpec(memory_space=pltpu.VMEM)),
        input_output_aliases={i: 2 * n + i for i in range(len(through))},
        compiler_params=pltpu.CompilerParams(has_side_effects=pltpu.SideEffectType.DATAFLOW_SIDE_EFFECTING),
    )(*[pltpu.with_memory_space_constraint(a, pltpu.HBM) for a in through])
    return (list(outs[:n]), list(outs[n:2 * n]), list(outs[2 * n:3 * n]), list(outs[3 * n:4 * n]), outs[-1],
            list(outs[4 * n:4 * n + m]))


def _exchange_wait(send_sems, recv_sems, items, lands, after, name):
    n = len(items)

    def body(*refs):
        land = refs[n:2 * n]
        send_sems, recv_sems = refs[2 * n:3 * n], refs[3 * n:4 * n]
        me, peers = _peers()
        for it in range(n):
            seven = land[it].at[pl.ds(0, N_DEV - 1)]
            cp = _remote(seven, seven, send_sems[it], recv_sems[it], peers[0][1])
            cp.wait_send()
            cp.wait_recv()

    hbm = pl.BlockSpec(memory_space=pltpu.HBM)
    sem = pl.BlockSpec(memory_space=pltpu.SEMAPHORE)
    outs = pl.pallas_call(
        body, name=name,
        out_shape=tuple(pltpu.HBM(a.shape, a.dtype) for a in list(items) + list(lands)),
        in_specs=[hbm] * (2 * n) + [sem] * (2 * n) + [pl.BlockSpec(memory_space=pl.ANY)],
        out_specs=tuple([hbm] * (2 * n)),
        input_output_aliases={i: i for i in range(2 * n)},
        compiler_params=pltpu.CompilerParams(has_side_effects=pltpu.SideEffectType.DATAFLOW_SIDE_EFFECTING),
    )(*items, *lands, *send_sems, *recv_sems, after)
    return list(outs[:n]), list(outs[n:])


def _gelu_tanh(y):
    k = math.sqrt(2.0 / math.pi)
    t = jnp.tanh(k * (y + 0.044715 * y * y * y))
    return 0.5 * y * (1.0 + t), t


def _local_step(x, mod, target, W, late_weights, P, send_early):
    B, S, D = x.shape
    T = B * S
    TS = 512
    flat = lambda a: a.reshape(T, a.shape[-1])
    unflat = lambda a: a.reshape(B, S, a.shape[-1])
    mod_col = lambda i: (mod, D, i)

    def f_modnorm_project(xv, sc, sh, g, wq, wu, wg):
        u = ((xv * _rms_scale(xv) * g) * (1.0 + sc) + sh).astype(BF16)
        return (u, lax.dot_general(u, wq, _NT, preferred_element_type=F32),
                lax.dot_general(u, wu, _NT, preferred_element_type=F32),
                lax.dot_general(u, wg, _NT, preferred_element_type=F32))

    u1, qkv, us, gates = _rowwise(
        f_modnorm_project, [(x, D, 0)], [mod_col(1), mod_col(0)], [P["g_mix"], W["w_qkv"], W["w_us"], W["w_gates"]],
        [(D, BF16), (3 * ATT_WIDTH, F32), (SSM_WIDTH, F32), (2 * D, BF16)], [], [], ts=TS, name="modnorm_project_in",
        split=2)
    u1f = flat(u1)

    o_att, lse = _attention_fwd(qkv, P["slopes"])
    more_w, more_p = late_weights(o_att)
    W, P = {**W, **more_w}, {**P, **more_p}

    xs, y_mm = _scan_fwd(us, P["bb_big"], P["a_row"], P["cc_big"])

    bga, bgs = P["b_gate"][:, :D], P["b_gate"][:, D:]

    def f_mixer_tail(ov, ymm, usv, ga, gs, xv, gt, sc, sh, w_att, w_ssm, w_o, bga_, bgs_, g, dsk, wg, bg):
        yv = ymm + dsk * usv
        ge, _ = _gelu_tanh(yv)
        zv = (ge * _sigmoid(jnp.dot(ge.astype(BF16), wg, preferred_element_type=F32) + bg)).astype(BF16)
        ya = jnp.dot(ov, w_att, preferred_element_type=F32)
        ys = jnp.dot(zv, w_ssm, preferred_element_type=F32)
        mg = (_sigmoid(ga + bga_) * ya + _sigmoid(gs + bgs_) * ys).astype(BF16)
        mx = jnp.dot(mg, w_o, preferred_element_type=F32)
        h = xv + gt * mx
        return yv, zv, mg, mx, h, (h * _rms_scale(h) * g) * (1.0 + sc) + sh

    y_s5, z, merged, mix, h1, u2 = _rowwise(
        f_mixer_tail,
        [(o_att, ATT_WIDTH, 0), (y_mm, SSM_WIDTH, 0), (us, SSM_WIDTH, 0), (gates, D, 0), (gates, D, 1), (x, D, 0)],
        [mod_col(2), mod_col(4), mod_col(3)],
        [W["w_proj_att"], W["w_proj_ssm"], W["w_out"], bga, bgs, P["g_ffn"], P["d_skip"], W["w_glu"], P["b_glu"]],
        [(SSM_WIDTH, F32), (SSM_WIDTH, BF16), (D, BF16), (D, BF16), (D, F32), (D, BF16)],
        [], [], ts=TS, raw=(0,), name="mixer_tail", split=2)

    up, ff = _up_conv_fwd(u2, W["w_up"], P["w_conv"], P["b_conv"])

    def f_head(ffv, h1v, tg, gt, g, w_dn):
        dn = jnp.dot(ffv, w_dn, preferred_element_type=F32)
        h2 = h1v + gt * dn
        r = _rms_scale(h2)
        nh = h2 * r
        e = nh * g - tg
        dy = e * (1.0 / D)
        gy = dy * g
        dh = r * (gy - nh * jnp.mean(gy * nh, axis=-1, keepdims=True))
        return (dh, dh * gt, _col_sum(dh * dn), _col_sum(dy * nh), _col_sum(e * e) * (0.5 / D))

    dh2, d_down, d_gt2, d_g_final, loss_cols = _rowwise(
        f_head, [(ff, D_FF, 0), (h1, D, 0), (target, D, 0)], [mod_col(5)], [P["g_final"], W["w_down"]],
        [(D, BF16), (D, BF16)], [D], [(1, D), (1, D)], ts=TS, raw=(0,),
        name="ffn_down_head_loss", split=2)

    d_downf = flat(d_down)
    d_w_down = _matmul(flat(ff), d_downf, ta=True, out_dtype=BF16, name="ffn_down_dw")
    d_up, d_w_conv, d_b_conv = _conv_bwd(up, d_down, W["w_down"], P["w_conv"], P["b_conv"])
    d_upf = flat(d_up)
    d_w_up = _up_dw(flat(u2), d_upf, name="ffn_up_dw")
    token, _ = send_early(dict(w_down=d_w_down.reshape(N_DEV, D_FF // N_DEV, D), w_up=d_w_up))
    g_ffn_after = P["g_ffn"] + token[0:1, 0:1]

    def f_up_back_modnorm(dup, h, dres, mx, sc, gt, g, w2):
        du = None
        for j in range(UP_SLOTS):
            wa, wv = w2[j * D:(j + 1) * D, :], w2[(j + UP_SLOTS) * D:(j + UP_SLOTS + 1) * D, :]
            wj = jnp.concatenate([wa[:, :HALF], wv[:, :HALF], wa[:, HALF:], wv[:, HALF:]], axis=1)
            part = lax.dot_general(dup[:, j * UP_GROUP:(j + 1) * UP_GROUP], wj, _NT, preferred_element_type=F32)
            du = part if du is None else du + part
        r = _rms_scale(h)
        nh = h * r
        dn = du * (1.0 + sc)
        gy = dn * g
        dh = dres + r * (gy - nh * jnp.mean(gy * nh, axis=-1, keepdims=True))
        return (dh, dh * gt, _col_sum(du), _col_sum(du * nh * g), _col_sum(dh * mx), _col_sum(dn * nh))

    dh1, d_mix, d_sh2, d_sc2, d_gt1, d_g_ffn = _rowwise(
        f_up_back_modnorm, [(d_up, 2 * D_FF, 0), (h1, D, 0), (dh2, D, 0), (mix, D, 0)], [mod_col(4), mod_col(2)],
        [g_ffn_after, W["w_up"].reshape(N_DEV * D, 2 * HALF)],
        [(D, BF16), (D, BF16)], [D, D, D], [(1, D)], ts=TS, raw=(0,),
        name="ffn_up_back_modnorm", split=2)

    d_mixf = flat(d_mix)
    d_w_out = _matmul(flat(merged), d_mixf, ta=True, out_dtype=BF16, name="proj_out_dw")

    def f_mixer_tail_bwd(dmx, ov, zv, ga, gs, yv, usv, w_o, w_att, w_ssm, bga_, bgs_, dsk, wg, bg):
        dm = lax.dot_general(dmx, w_o, _NT, preferred_element_type=F32)
        ya = jnp.dot(ov, w_att, preferred_element_type=F32)
        ys = jnp.dot(zv, w_ssm, preferred_element_type=F32)
        sa, ss = _sigmoid(ga + bga_), _sigmoid(gs + bgs_)
        dga = dm * ya * sa * (1.0 - sa)
        dgs = dm * ys * ss * (1.0 - ss)
        dya, dys = (dm * sa).astype(BF16), (dm * ss).astype(BF16)
        d_o = lax.dot_general(dya, w_att, _NT, preferred_element_type=F32)
        dz = lax.dot_general(dys, w_ssm, _NT, preferred_element_type=F32)
        ge, t = _gelu_tanh(yv)
        sg = _sigmoid(jnp.dot(ge.astype(BF16), wg, preferred_element_type=F32) + bg)
        dpre = dz * ge * sg * (1.0 - sg)
        dge = dz * sg + lax.dot_general(dpre.astype(BF16), wg, _NT, preferred_element_type=F32)
        k = math.sqrt(2.0 / math.pi)
        dgelu = 0.5 * (1.0 + t) + 0.5 * yv * (1.0 - t * t) * k * (1.0 + 3.0 * 0.044715 * yv * yv)
        dy = dge * dgelu
        dwg = lax.dot_general(ge.astype(BF16), dpre.astype(BF16), _TN, preferred_element_type=F32)
        return (dya, dys, jnp.concatenate([dga, dgs], axis=1), d_o, dy, dy * dsk,
                _col_sum(dga), _col_sum(dgs), dwg, _col_sum(dpre), _col_sum(dy * usv))

    (d_y_att, d_y_ssm, d_gates, d_o_att, d_y_s5, d_us_skip, d_bga, d_bgs, d_w_glu, d_b_glu, d_d_skip) = _rowwise(
        f_mixer_tail_bwd,
        [(d_mix, D, 0), (o_att, ATT_WIDTH, 0), (z, SSM_WIDTH, 0), (gates, D, 0), (gates, D, 1),
         (y_s5, SSM_WIDTH, 0), (us, SSM_WIDTH, 0)], [],
        [W["w_out"], W["w_proj_att"], W["w_proj_ssm"], bga, bgs, P["d_skip"], W["w_glu"], P["b_glu"]],
        [(D, BF16), (D, BF16), (2 * D, BF16), (ATT_WIDTH, F32), (SSM_WIDTH, BF16), (SSM_WIDTH, F32)], [],
        [(1, D), (1, D), (SSM_WIDTH, SSM_WIDTH), (1, SSM_WIDTH), (1, SSM_WIDTH)], ts=TS, raw=(0, 1, 2),
        name="mixer_tail_bwd", split=2)

    d_yaf, d_ysf = flat(d_y_att), flat(d_y_ssm)
    d_w_proj_att = _matmul(flat(o_att), d_yaf, ta=True, out_dtype=BF16, name="proj_att_dw")
    d_w_proj_ssm = _matmul(flat(z), d_ysf, ta=True, out_dtype=BF16, name="proj_ssm_dw")
    d_us_parts, g_ab, d_bb, d_cc = _scan_bwd(d_y_s5, us, P["bb_big"], P["cc_big"], xs, P["a_row"])

    token, _ = send_early(dict(
        w_out=d_w_out.reshape(N_DEV, D // N_DEV, D), w_proj_att=_cols_to_slots(d_w_proj_att),
        w_proj_ssm=_cols_to_slots(d_w_proj_ssm),
        w_glu=d_w_glu.astype(BF16).reshape(N_DEV, SSM_WIDTH // N_DEV, SSM_WIDTH),
        w_conv=_cols_to_slots(d_w_conv.astype(BF16))))
    d_qkv = _attention_bwd(qkv, o_att, d_o_att, lse, P["slopes"] + token[0, 0])

    def f_add(*parts):
        return sum(parts[1:], parts[0])

    n_parts = d_us_parts.shape[0]
    stacked = d_us_parts.reshape(n_parts * B, S, SSM_WIDTH)
    (d_us,) = _rowwise(f_add, [(d_us_skip, SSM_WIDTH, 0)] + [(stacked, SSM_WIDTH, 0, j * B) for j in range(n_parts)],
                       [], [],
                       [(SSM_WIDTH, BF16)], [], [], ts=TS, name="s5_input_grad")
    d_qkvf = flat(d_qkv)
    d_usf = flat(d_us)
    d_gatesf = flat(d_gates)
    d_w_in_t = jnp.concatenate(
        [_unpair_qkv_rows(_matmul(d_qkvf, u1f, ta=True, out_dtype=BF16, name="proj_qkv_dw")),
         _matmul(d_usf, u1f, ta=True, out_dtype=BF16, name="proj_ssm_in_dw"),
         _matmul(d_gatesf, u1f, ta=True, out_dtype=BF16, name="proj_gates_dw")], axis=0)
    token, (w_qkv, w_us, w_gates) = send_early(dict(w_in=d_w_in_t.reshape(N_DEV, -1, D)),
                                               carry=[W["w_qkv"], W["w_us"], W["w_gates"]])
    def f_project_back_modnorm(dq, du_, dg, h, dres, sc, g, wq, wu, wg):
        du = (jnp.dot(dq, wq, preferred_element_type=F32) + jnp.dot(du_, wu, preferred_element_type=F32)
              + jnp.dot(dg, wg, preferred_element_type=F32))
        r = _rms_scale(h)
        nh = h * r
        dn = du * (1.0 + sc)
        gy = dn * g
        dh = dres + r * (gy - nh * jnp.mean(gy * nh, axis=-1, keepdims=True))
        return (dh, _col_sum(du), _col_sum(du * nh * g), _col_sum(dn * nh))

    grad_x, d_sh1, d_sc1, d_g_mix = _rowwise(
        f_project_back_modnorm,
        [(d_qkv, 3 * ATT_WIDTH, 0), (d_us, SSM_WIDTH, 0), (d_gates, 2 * D, 0), (x, D, 0), (dh1, D, 0)], [mod_col(1)],
        [P["g_mix"] + token[0:1, 0:1], w_qkv, w_us, w_gates],
        [(D, F32)], [D, D], [(1, D)], ts=TS, raw=(0, 1, 2),
        name="project_in_back_modnorm", split=2)

    d_mod = jnp.concatenate([d_sh1, d_sc1, d_gt1, d_sh2, d_sc2, d_gt2], axis=-1)
    g_ab_re, g_ab_im = _deinterleave(g_ab)
    d_bb_re, d_bb_im = _deinterleave(d_bb)
    d_cc_re, d_cc_im = (t.T for t in _deinterleave(d_cc.T))
    small = dict(g_mix=d_g_mix, b_gate=jnp.concatenate([d_bga, d_bgs], axis=1), g_ab_re=g_ab_re, g_ab_im=g_ab_im,
                 d_bb_re=d_bb_re, d_bb_im=d_bb_im, d_cc_re=d_cc_re, d_cc_im=d_cc_im, d_skip=d_d_skip,
                 b_glu=d_b_glu, g_ffn=d_g_ffn, b_conv=d_b_conv, g_final=d_g_final, loss_cols=loss_cols)
    return grad_x, d_mod, small


def _block_diag_in(bb):
    t = bb.reshape(SSM_GROUPS, SSM_STATE, SSM_GROUP_CH)
    eye = jnp.eye(SSM_GROUPS, dtype=bb.dtype)
    return jnp.einsum("gnc,gh->gchn", t, eye).reshape(SSM_WIDTH, SSM_COLS)


def _block_diag_out(cm):
    eye = jnp.eye(SSM_GROUPS, dtype=cm.dtype)
    return jnp.einsum("gcn,gh->gnhc", cm, eye).reshape(SSM_COLS, SSM_WIDTH)


def _diag_blocks_in(m):
    t = m.reshape(SSM_GROUPS, SSM_GROUP_CH, SSM_GROUPS, SSM_STATE)
    idx = jnp.arange(SSM_GROUPS)
    return t[idx, :, idx, :].transpose(0, 2, 1).reshape(SSM_COLS, SSM_GROUP_CH)


def _diag_blocks_out(m):
    t = m.reshape(SSM_GROUPS, SSM_STATE, SSM_GROUPS, SSM_GROUP_CH)
    idx = jnp.arange(SSM_GROUPS)
    return t[idx, :, idx, :].transpose(0, 2, 1)


def _pair_qkv_rows(w):
    return w.reshape(3, N_HEADS // 2, LANES, w.shape[1]).swapaxes(0, 1).reshape(w.shape)


def _unpair_qkv_rows(w):
    return w.reshape(N_HEADS // 2, 3, LANES, w.shape[1]).swapaxes(0, 1).reshape(w.shape)


def _interleave(re, im):
    lead = re.shape[:-1]
    g = lambda a: a.reshape(lead + (SSM_COLS // SCAN_COLS, 1, SCAN_COLS))
    return jnp.concatenate([g(re), g(im)], axis=-2).reshape(lead + (2 * SSM_COLS,))


def _deinterleave(x):
    lead = x.shape[:-1]
    t = x.reshape(lead + (SSM_COLS // SCAN_COLS, 2, SCAN_COLS))
    return t[..., 0, :].reshape(lead + (SSM_COLS,)), t[..., 1, :].reshape(lead + (SSM_COLS,))


def _cols_to_slots(g):
    R = g.shape[0]
    return g.reshape(R, N_DEV, g.shape[1] // N_DEV).transpose(1, 0, 2)


def _slots_to_cols(g):
    return g.transpose(1, 0, 2).reshape(g.shape[1], N_DEV * g.shape[2])


SMALL_ORDER = ("b_ada", "g_mix", "b_gate", "a_re", "a_im", "log_dt", "b_re", "b_im", "c_re", "c_im", "d_skip",
               "b_glu", "g_ffn", "b_conv", "g_final")


def _pack(arrs):
    pieces, offs, row = [], [], 0
    for a in arrs:
        f = a.reshape(-1).astype(F32)
        n = f.shape[0]
        rows = -(-n // LANES)
        pieces.append(jnp.pad(f, (0, rows * LANES - n)))
        offs.append((row, n))
        row += rows
    return jnp.concatenate(pieces).reshape(row, LANES), offs


def _unpack(packed, offs, shapes):
    flat = packed.reshape(-1)
    return [flat[r * LANES:r * LANES + n].reshape(s) for (r, n), s in zip(offs, shapes)]


def kernel(x, c, w_ada, b_ada, g_mix, w_in, b_gate, a_re, a_im, log_dt, b_re, b_im, c_re, c_im, d_skip, w_glu, b_glu, w_proj_att, w_proj_ssm, w_out, g_ffn, w_up, w_conv, b_conv, w_down, g_final, loss_target, m_w_ada, m_b_ada, m_g_mix, m_w_in, m_b_gate, m_a_re, m_a_im, m_log_dt, m_b_re, m_b_im, m_c_re, m_c_im, m_d_skip, m_w_glu, m_b_glu, m_w_proj_att, m_w_proj_ssm, m_w_out, m_g_ffn, m_w_up, m_w_conv, m_b_conv, m_w_down, m_g_final, v_w_ada, v_b_ada, v_g_mix, v_w_in, v_b_gate, v_a_re, v_a_im, v_log_dt, v_b_re, v_b_im, v_c_re, v_c_im, v_d_skip, v_w_glu, v_b_glu, v_w_proj_att, v_w_proj_ssm, v_w_out, v_g_ffn, v_w_up, v_w_conv, v_b_conv, v_w_down, v_g_final):
    args = dict(locals())
    B, S, D = x.shape
    me = 4 * lax.axis_index("x") + 2 * lax.axis_index("y") + lax.axis_index("c")
    bf = lambda w: w[0].astype(BF16)

    c_slots, w_in_slots = _gather_all([c, w_in[0].T.astype(BF16)], name="gather_first_weights")
    c_all = c_slots.reshape(N_DEV * B, D)
    w_in_t = w_in_slots.reshape(-1, D)
    n_qkv = 3 * ATT_WIDTH
    W = dict(w_qkv=_pair_qkv_rows(w_in_t[:n_qkv]), w_us=w_in_t[n_qkv:n_qkv + SSM_WIDTH],
             w_gates=w_in_t[n_qkv + SSM_WIDTH:])

    n_ada = w_ada.shape[2]
    b_ada_cols = lax.dynamic_slice(b_ada, (0, me * n_ada), (1, n_ada))
    mod_part = _ada_fwd(c_all, w_ada[0], b_ada_cols)
    (mod_slots,) = _exchange([(mod_part.reshape(N_DEV, B, n_ada), True)], name="scatter_modulation")
    mod = mod_slots.transpose(1, 0, 2).reshape(B, 1, 6 * D)

    later = [bf(w_glu), bf(w_proj_att), bf(w_proj_ssm), bf(w_out), bf(w_up), w_conv[0], bf(w_down)]
    later_sems = _exchange_start(later, "start_later_weights", gather=True, carry=[mod])
    (mod,) = later_sems[5]

    def late_weights(after):
        _, lands = _exchange_wait(*later_sems[:4], after, name="wait_later_weights")
        g = [lax.dynamic_update_index_in_dim(land, a, me, 0) for land, a in zip(lands, later)]
        more_w = dict(w_glu=g[0].reshape(SSM_WIDTH, SSM_WIDTH), w_proj_att=_slots_to_cols(g[1]),
                      w_proj_ssm=_slots_to_cols(g[2]), w_out=g[3].reshape(D, D), w_up=g[4],
                      w_down=g[6].reshape(D_FF, D))
        return more_w, dict(w_conv=_slots_to_cols(g[5]))

    ab_re, ab_im, f_re, f_im = _s5_params(a_re[0], a_im[0], log_dt[0].reshape(SSM_GROUPS, 1))
    col = lambda a: a.reshape(SSM_COLS, 1)
    b_re2, b_im2 = b_re[0].reshape(SSM_COLS, SSM_GROUP_CH), b_im[0].reshape(SSM_COLS, SSM_GROUP_CH)
    bb_re, bb_im = _s5_input_matrix(col(f_re), col(f_im), b_re2, b_im2)
    slopes = jnp.asarray([2.0 ** (-8.0 * (h + 1) / N_HEADS) for h in range(N_HEADS)], F32)
    P = dict(g_mix=g_mix, g_ffn=g_ffn, g_final=g_final.reshape(1, D), b_gate=b_gate, d_skip=d_skip, b_glu=b_glu,
             b_conv=b_conv, slopes=slopes,
             a_row=_interleave(ab_re.reshape(1, SSM_COLS), ab_im.reshape(1, SSM_COLS)),
             bb_big=_interleave(_block_diag_in(bb_re), _block_diag_in(bb_im)),
             cc_big=_interleave(_block_diag_out(c_re[0]).T, -_block_diag_out(c_im[0]).T).T)

    in_flight = []

    def send_early(grads, carry=()):
        names = list(grads)
        handles = _exchange_start([grads[n] for n in names], "start_gradients_%d" % len(in_flight), gather=False,
                                  carry=carry)
        in_flight.append((names,) + handles[:4])
        return handles[4], handles[5]

    grad_x, d_mod, small = _local_step(x, mod, loss_target, W, late_weights, P, send_early)

    small_list = [small["loss_cols"], small["g_mix"], small["b_gate"], small["g_ab_re"], small["g_ab_im"],
                  _diag_blocks_in(small["d_bb_re"]), _diag_blocks_in(small["d_bb_im"]),
                  _diag_blocks_out(small["d_cc_re"]), -_diag_blocks_out(small["d_cc_im"]),
                  small["g_ffn"], small["b_conv"], small["g_final"], small["d_skip"], small["b_glu"]]
    small_packed, small_offs = _pack(small_list)
    small_all, dmod_slots = _gather_all([small_packed, d_mod.reshape(B, 6 * D)], name="gather_small_gradients")

    out = {}

    def update(name, parts, own=None):
        view = (lambda a: a[0].T) if name == "w_in" else (lambda a: a[0])
        back = (lambda a: a.T[None]) if name == "w_in" else (lambda a: a[None])
        g, dl, mn, vn = _adamw(view(args[name]), view(args["m_" + name]), view(args["v_" + name]), parts,
                               name="adamw_" + name, own=own)
        for key, val in (("grad_", g), ("delta_", dl), ("new_m_", mn), ("new_v_", vn)):
            out[key + name] = back(val)

    my_slot = me.astype(jnp.int32).reshape(1)
    for i, (names, send_sems, recv_sems, sent, lands) in enumerate(in_flight):
        sent, lands = _exchange_wait(send_sems, recv_sems, sent, lands, dmod_slots, name="wait_gradients_%d" % i)
        for name, own_slots, landed in zip(names, sent, lands):
            update(name, landed, own=(own_slots, my_slot))

    dmod_all = dmod_slots.reshape(N_DEV * B, 6 * D)
    dmod_cols = lax.dynamic_slice(dmod_all, (0, me * n_ada), (N_DEV * B, n_ada))
    d_w_ada, d_b_ada = _ada_bwd(c_all, dmod_all, dmod_cols)
    update("w_ada", d_w_ada[None])

    loss_row, loss_n = small_offs[0]
    small_sum, loss_vec = _sum_parts(small_all, (loss_row, loss_row + loss_n // LANES))
    shapes = [(1, D), (1, D), (1, 2 * D), (SSM_GROUPS, SSM_STATE), (SSM_GROUPS, SSM_STATE), (SSM_COLS, SSM_GROUP_CH),
              (SSM_COLS, SSM_GROUP_CH), (1, SSM_GROUPS, SSM_GROUP_CH, SSM_STATE),
              (1, SSM_GROUPS, SSM_GROUP_CH, SSM_STATE), (1, D), (1, D_FF), (D,), (1, SSM_WIDTH), (1, SSM_WIDTH)]
    (_, s_g_mix, s_b_gate, s_ab_re, s_ab_im, s_bb_re, s_bb_im, s_c_re, s_c_im, s_g_ffn, s_b_conv, s_g_final,
     s_d_skip, s_b_glu) = _unpack(small_sum, small_offs, shapes)
    d_b_re2, d_b_im2, d_f_re, d_f_im = _s5_input_matrix_bwd(col(f_re), col(f_im), b_re2, b_im2, s_bb_re, s_bb_im)
    d_a_re, d_a_im, d_log_dt = _s5_params_bwd(a_re[0], a_im[0], log_dt[0].reshape(SSM_GROUPS, 1), s_ab_re, s_ab_im,
                                              d_f_re.reshape(SSM_GROUPS, SSM_STATE),
                                              d_f_im.reshape(SSM_GROUPS, SSM_STATE))
    grads_small = dict(b_ada=d_b_ada, g_mix=s_g_mix, b_gate=s_b_gate, a_re=d_a_re[None], a_im=d_a_im[None],
                       log_dt=d_log_dt.reshape(1, SSM_GROUPS), b_re=d_b_re2.reshape(b_re.shape),
                       b_im=d_b_im2.reshape(b_im.shape), c_re=s_c_re, c_im=s_c_im, d_skip=s_d_skip, b_glu=s_b_glu,
                       g_ffn=s_g_ffn, b_conv=s_b_conv, g_final=s_g_final)
    flat2 = lambda a: a.reshape(-1, a.shape[-1])
    res = _adamw_small([flat2(args[n]) for n in SMALL_ORDER], [flat2(args["m_" + n]) for n in SMALL_ORDER],
                       [flat2(args["v_" + n]) for n in SMALL_ORDER],
                       [flat2(grads_small[n].reshape(args[n].shape)) for n in SMALL_ORDER])
    for i, n in enumerate(SMALL_ORDER):
        for k, key in enumerate(("grad_", "delta_", "new_m_", "new_v_")):
            out[key + n] = res[4 * i + k].reshape(args[n].shape)

    order = ["w_ada", "b_ada", "g_mix", "w_in", "b_gate", "a_re", "a_im", "log_dt", "b_re", "b_im", "c_re", "c_im",
             "d_skip", "w_glu", "b_glu", "w_proj_att", "w_proj_ssm", "w_out", "g_ffn", "w_up", "w_conv", "b_conv",
             "w_down", "g_final"]
    loss = loss_vec[0, 0]
    return (loss, grad_x, *[out[k + n] for k in ("grad_", "delta_", "new_m_", "new_v_") for n in order])
```

```python
import math

import jax
import jax.numpy as jnp
from jax import lax
from jax.experimental import pallas as pl
from jax.experimental.pallas import tpu as pltpu

F32 = jnp.float32
BF16 = jnp.bfloat16

N_DEV = 8
D_MODEL = 1024
N_HEADS = 8
HEAD_DIM = 64
ATT_WIDTH = N_HEADS * HEAD_DIM
DILATIONS = (1, 4, 16)
WIN = 128
SSM_GROUPS = 16
SSM_GROUP_CH = 16
SSM_WIDTH = SSM_GROUPS * SSM_GROUP_CH
SSM_STATE = 64
SSM_COLS = SSM_GROUPS * SSM_STATE
D_FF = 2048
EPS = 1e-6
NEG_INF = -1e30
ADAM_LR, ADAM_B1, ADAM_B2, ADAM_EPS, ADAM_WD, ADAM_STEP = 0.001, 0.9, 0.999, 1e-08, 0.01, 10

V7X_VMEM_LIMIT = 56 * 1024 * 1024
LANES = 128


def _params(n_grid):
    return pltpu.CompilerParams(dimension_semantics=("arbitrary",) * n_grid,
                                vmem_limit_bytes=V7X_VMEM_LIMIT)


def _tile(n, pref):
    if n <= pref:
        return n
    t = (pref // LANES) * LANES
    while t > 0:
        if n % t == 0:
            return t
        t -= LANES
    return n


def _matmul(a, b, *, ta=False, tb=False, out_dtype=F32, name):
    if ta:
        K, M = a.shape
    else:
        M, K = a.shape
    if tb:
        N, K2 = b.shape
    else:
        K2, N = b.shape
    assert K == K2, (a.shape, b.shape)
    if ta:
        tm, tn, tk = _tile(M, 1024), _tile(N, 2048), _tile(K, 1024)
    else:
        tm, tk = _tile(M, 512), _tile(K, 4096)
        tn = _tile(N, 2048 if K <= 2048 else 1024)
    nk = K // tk
    dn = (((0,) if ta else (1,), (1,) if tb else (0,)), ((), ()))

    def body(a_ref, b_ref, o_ref, acc_ref):
        k = pl.program_id(2)
        part = lax.dot_general(a_ref[...].astype(BF16), b_ref[...].astype(BF16), dn, preferred_element_type=F32)
        if nk == 1:
            o_ref[...] = part.astype(o_ref.dtype)
            return

        @pl.when(k == 0)
        def _():
            acc_ref[...] = jnp.zeros_like(acc_ref)

        acc_ref[...] += part

        @pl.when(k == nk - 1)
        def _():
            o_ref[...] = acc_ref[...].astype(o_ref.dtype)

    a_spec = (pl.BlockSpec((tk, tm), lambda j, i, k: (k, i)) if ta
              else pl.BlockSpec((tm, tk), lambda j, i, k: (i, k)))
    b_spec = (pl.BlockSpec((tn, tk), lambda j, i, k: (j, k)) if tb
              else pl.BlockSpec((tk, tn), lambda j, i, k: (k, j)))
    return pl.pallas_call(
        body, name=name, grid=(N // tn, M // tm, nk),
        in_specs=[a_spec, b_spec],
        out_specs=pl.BlockSpec((tm, tn), lambda j, i, k: (i, j)),
        out_shape=jax.ShapeDtypeStruct((M, N), out_dtype),
        scratch_shapes=[pltpu.VMEM((tm, tn) if nk > 1 else (8, LANES), F32)],
        compiler_params=_params(3),
    )(a, b)


HALF = 256
UP_SLOTS = N_DEV // 2
UP_GROUP = 4 * HALF


def _up_weight_spec(K, index):
    return pl.BlockSpec((2, None, K, 2 * HALF), index)


def _up_dw(a, d, name):
    M, K = a.shape
    tk = _tile(M, 1024)
    nk = M // tk

    def body(a_ref, d_ref, o_ref, acc_ref):
        k = pl.program_id(1)

        @pl.when(k == 0)
        def _():
            acc_ref[...] = jnp.zeros_like(acc_ref)

        acc_ref[...] += lax.dot_general(a_ref[...], d_ref[...], _TN, preferred_element_type=F32)

        @pl.when(k == nk - 1)
        def _():
            for half in range(2):
                for part in range(2):
                    lo = (2 * half + part) * HALF
                    o_ref[part, :, half * HALF:(half + 1) * HALF] = acc_ref[:, lo:lo + HALF].astype(o_ref.dtype)

    out = pl.pallas_call(
        body, name=name, grid=(UP_SLOTS, nk),
        in_specs=[pl.BlockSpec((tk, K), lambda j, k: (k, 0)), pl.BlockSpec((tk, UP_GROUP), lambda j, k: (k, j))],
        out_specs=_up_weight_spec(K, lambda j, k: (0, j, 0, 0)),
        out_shape=jax.ShapeDtypeStruct((2, UP_SLOTS, K, 2 * HALF), BF16),
        scratch_shapes=[pltpu.VMEM((K, UP_GROUP), F32)], compiler_params=_params(2),
    )(a, d)
    return out.reshape(N_DEV, K, 2 * HALF)


def _rowwise(fn, rows, bvecs, consts, out_rows, out_b, out_g, *, ts, name, raw=(), split=1):
    B, S = rows[0][0].shape[:2]
    nin = len(rows) + len(bvecs) + len(consts)
    nr, nb, ng = len(out_rows), len(out_b), len(out_g)
    sub = ts // split

    def body(*refs):
        b = pl.program_id(0)
        s = pl.program_id(1)
        orefs = refs[nin:]
        fixed = [r[...] for r in refs[len(rows):nin]]
        sums = None
        for h in range(split):
            sec = slice(h * sub, (h + 1) * sub)
            vals = [r[sec, :] if i in raw else r[sec, :].astype(F32) for i, r in enumerate(refs[:len(rows)])]
            outs = fn(*vals, *fixed)
            if not isinstance(outs, (tuple, list)):
                outs = (outs,)
            for i in range(nr):
                orefs[i][sec, :] = outs[i].astype(orefs[i].dtype)
            part = list(outs[nr:])
            sums = part if sums is None else [a + c for a, c in zip(sums, part)]
        for i in range(nb):
            ref = orefs[nr + i]

            @pl.when(s == 0)
            def _(ref=ref):
                ref[...] = jnp.zeros_like(ref)

            ref[...] += sums[i]
        for i in range(ng):
            ref = orefs[nr + nb + i]

            @pl.when((s == 0) & (b == 0))
            def _(ref=ref):
                ref[...] = jnp.zeros_like(ref)

            ref[...] += sums[nb + i]

    rows = [r if len(r) == 4 else r + (0,) for r in rows]
    in_specs = ([pl.BlockSpec((None, ts, cb), lambda b, s, ci=ci, b0=b0: (b0 + b, s, ci)) for (_, cb, ci, b0) in rows]
                + [pl.BlockSpec((None, 1, cb), lambda b, s, ci=ci: (b, 0, ci)) for (_, cb, ci) in bvecs]
                + [pl.BlockSpec(a.shape, lambda b, s: (0, 0)) for a in consts])
    out_shape = ([jax.ShapeDtypeStruct((B, S, c), dt) for (c, dt) in out_rows]
                 + [jax.ShapeDtypeStruct((B, 1, c), F32) for c in out_b]
                 + [jax.ShapeDtypeStruct(rc, F32) for rc in out_g])
    out_specs = ([pl.BlockSpec((None, ts, c), lambda b, s: (b, s, 0)) for (c, _) in out_rows]
                 + [pl.BlockSpec((None, 1, c), lambda b, s: (b, 0, 0)) for c in out_b]
                 + [pl.BlockSpec(rc, lambda b, s: (0, 0)) for rc in out_g])
    args = [r[0] for r in rows] + [a for (a, _, _) in bvecs] + list(consts)
    return pl.pallas_call(
        body, name=name, grid=(B, S // ts), in_specs=in_specs, out_specs=out_specs,
        out_shape=out_shape, compiler_params=_params(2),
    )(*args)


def _col_sum(v):
    return jnp.sum(v, axis=0, keepdims=True)


def _rms_scale(h):
    return lax.rsqrt(jnp.mean(h * h, axis=-1, keepdims=True) + EPS)


def _sigmoid(v):
    return 0.5 * (1.0 + jnp.tanh(0.5 * v))


ATT_SCALE = HEAD_DIM ** -0.5
COPY_ROWS = 256
_NT = (((1,), (1,)), ((), ()))
_TN = (((0,), (0,)), ((), ()))


def _row_chunks(d, seq):
    sub = seq // d
    out = []
    for r in range(d):
        for c0 in range(0, sub, COPY_ROWS):
            n = min(COPY_ROWS, sub - c0)
            out.append((pl.ds(r + c0 * d, n, stride=d), r * sub + c0, n))
    return out


ATT_UNROLL = 16
KEYS = 2 * WIN


def _zero_once(refs):
    @pl.when((pl.program_id(0) == 0) & (pl.program_id(1) == 0))
    def _():
        for r in refs:
            r[...] = jnp.zeros_like(r)


def _pair_bias(bias_ref, slopes_ref, hp, d, key_major):
    shape = (KEYS, WIN) if key_major else (WIN, KEYS)
    qi = lax.broadcasted_iota(jnp.int32, shape, 1 if key_major else 0)
    kj = lax.broadcasted_iota(jnp.int32, shape, 0 if key_major else 1)
    dist = WIN + qi - kj
    valid = (dist >= 0) & (dist <= WIN)
    distf = dist.astype(F32)
    for h in range(2):
        slope_d = slopes_ref[2 * hp + h] * float(d)
        with_prev = jnp.where(valid, -(slope_d * distf), NEG_INF)
        no_prev = jnp.where(kj >= WIN, with_prev, NEG_INF)
        span = slice(h * KEYS, (h + 1) * KEYS)
        if key_major:
            bias_ref[1, span, :] = with_prev
            bias_ref[0, span, :] = no_prev
        else:
            bias_ref[1, :, span] = with_prev
            bias_ref[0, :, span] = no_prev


def _stack_heads(v):
    first = lax.broadcasted_iota(jnp.int32, v.shape, 1) < HEAD_DIM
    zero = jnp.zeros_like(v)
    return jnp.concatenate([jnp.where(first, v, zero), jnp.where(first, zero, v)], axis=0)


def _per_head(c0, c1, n):
    return jnp.where(lax.broadcasted_iota(jnp.int32, (n, LANES), 1) < HEAD_DIM, c0, c1)


def _qkv_spec(seq, j):
    return pl.BlockSpec((None, seq, LANES), lambda b, hp: (b, 0, 3 * hp + j))


def _attention_fwd(qkv, slopes):
    B, S, _ = qkv.shape
    n_blk = S // WIN
    n_pair = N_HEADS // 2

    def body(slopes_ref, q_ref, k_ref, v_ref, o_ref, lse_ref, qp, kp, vp, bias, acc, mx, sm, acc_n, mx_n, sm_n):
        hp = pl.program_id(1)
        _zero_once((kp, vp))
        for p, d in enumerate(DILATIONS):
            nb = n_blk // d
            chunks = _row_chunks(d, S)
            for src, dst, n in chunks:
                qp[dst:dst + n, :] = (q_ref[src, :] * ATT_SCALE).astype(BF16)
                kp[WIN + dst:WIN + dst + n, :] = k_ref[src, :].astype(BF16)
                vp[WIN + dst:WIN + dst + n, :] = v_ref[src, :].astype(BF16)
            _pair_bias(bias, slopes_ref, hp, d, key_major=False)
            acc_t, mx_t, sm_t = (acc_n, mx_n, sm_n) if d == 1 else (acc, mx, sm)

            nk = WIN if nb == 1 else KEYS
            bias_cur = jnp.concatenate([bias[0, :, WIN:KEYS], bias[0, :, KEYS + WIN:]], axis=1) if nb == 1 else None

            def block(i, carry, p=p, nb=nb, nk=nk, bias_cur=bias_cur, acc_t=acc_t, mx_t=mx_t, sm_t=sm_t):
                cur = pl.ds(pl.multiple_of(i * WIN, WIN), WIN)
                keys = pl.ds(pl.multiple_of(i * WIN + (KEYS - nk), WIN), nk)
                s = lax.dot_general(qp[cur, :], _stack_heads(kp[keys, :]), _NT, preferred_element_type=F32)
                s = s + (bias_cur if nb == 1 else bias[((i % nb) > 0).astype(jnp.int32)])
                es, ms, ls = [], [], []
                for h in range(2):
                    sh = s[:, h * nk:(h + 1) * nk]
                    m = jnp.max(sh if nb == 1 else jnp.maximum(sh[:, :WIN], sh[:, WIN:]), axis=1, keepdims=True)
                    e = jnp.exp(sh - m)
                    es.append(e.astype(BF16))
                    ms.append(m)
                    ls.append(jnp.sum(e if nb == 1 else e[:, :WIN] + e[:, WIN:], axis=1, keepdims=True))
                acc_t[p, cur, :] = jnp.dot(jnp.concatenate(es, axis=1), _stack_heads(vp[keys, :]),
                                           preferred_element_type=F32)
                mx_t[p, cur, :] = _per_head(ms[0], ms[1], WIN)
                sm_t[p, cur, :] = _per_head(ls[0], ls[1], WIN)
                return carry

            lax.fori_loop(0, n_blk, block, 0, unroll=ATT_UNROLL)
            if d > 1:
                for src, dst, n in chunks:
                    acc_n[p, src, :] = acc[p, dst:dst + n, :]
                    mx_n[p, src, :] = mx[p, dst:dst + n, :]
                    sm_n[p, src, :] = sm[p, dst:dst + n, :]

        chunk = 256

        def merge(i, carry):
            rows = pl.ds(pl.multiple_of(i * chunk, chunk), chunk)
            ms = [mx_n[p, rows, :] for p in range(3)]
            m = jnp.maximum(jnp.maximum(ms[0], ms[1]), ms[2])
            ws = [jnp.exp(mp - m) for mp in ms]
            l = ws[0] * sm_n[0, rows, :] + ws[1] * sm_n[1, rows, :] + ws[2] * sm_n[2, rows, :]
            o = (ws[0] * acc_n[0, rows, :] + ws[1] * acc_n[1, rows, :] + ws[2] * acc_n[2, rows, :]) / l
            o_ref[rows, :] = o.astype(o_ref.dtype)
            lse = m + jnp.log(l)
            for h in range(2):
                lse_ref[rows, h:h + 1] = lse[:, h * HEAD_DIM:h * HEAD_DIM + 1]
            return carry

        lax.fori_loop(0, S // chunk, merge, 0)

    return pl.pallas_call(
        body, name="attention_fwd", grid=(B, n_pair),
        in_specs=[pl.BlockSpec(memory_space=pltpu.SMEM), _qkv_spec(S, 0), _qkv_spec(S, 1), _qkv_spec(S, 2)],
        out_specs=[pl.BlockSpec((None, S, LANES), lambda b, hp: (b, 0, hp)),
                   pl.BlockSpec((None, None, S, 2), lambda b, hp: (b, hp, 0, 0))],
        out_shape=[jax.ShapeDtypeStruct((B, S, ATT_WIDTH), BF16),
                   jax.ShapeDtypeStruct((B, n_pair, S, 2), F32)],
        scratch_shapes=[pltpu.VMEM((S, LANES), BF16), pltpu.VMEM((S + WIN, LANES), BF16),
                        pltpu.VMEM((S + WIN, LANES), BF16), pltpu.VMEM((2, WIN, 2 * KEYS), F32)]
        + [pltpu.VMEM((3, S, LANES), F32)] * 6,
        compiler_params=_params(2),
    )(slopes, qkv, qkv, qkv)


def _attention_bwd(qkv, o, do, lse, slopes):
    B, S, _ = qkv.shape
    n_blk = S // WIN
    n_pair = N_HEADS // 2

    def body(slopes_ref, q_ref, k_ref, v_ref, o_ref, do_ref, lse_ref, dx_ref,
             qp, dop, kp, vp, aux, auxp, aux_t, bias_t, dqp, dvk, dq_n, dk_n, dv_n):
        hp = pl.program_id(1)
        aux[...] = jnp.zeros_like(aux)
        for c0 in range(0, S, COPY_ROWS):
            rows = slice(c0, c0 + COPY_ROWS)
            prod = do_ref[rows, :] * o_ref[rows, :].astype(F32)
            for h in range(2):
                aux[rows, 2 * h:2 * h + 1] = lse_ref[rows, h:h + 1]
                aux[rows, 2 * h + 1:2 * h + 2] = jnp.sum(prod[:, h * HEAD_DIM:(h + 1) * HEAD_DIM], axis=1,
                                                         keepdims=True)
        dq_n[...] = jnp.zeros_like(dq_n)
        dk_n[...] = jnp.zeros_like(dk_n)
        dv_n[...] = jnp.zeros_like(dv_n)
        _zero_once((kp, vp))
        for p, d in enumerate(DILATIONS):
            nb = n_blk // d
            chunks = _row_chunks(d, S)
            for src, dst, n in chunks:
                auxp[dst:dst + n, :] = aux[src, :]
                qp[dst:dst + n, :] = (q_ref[src, :] * ATT_SCALE).astype(BF16)
                dop[dst:dst + n, :] = do_ref[src, :].astype(BF16)
                kp[WIN + dst:WIN + dst + n, :] = k_ref[src, :].astype(BF16)
                vp[WIN + dst:WIN + dst + n, :] = v_ref[src, :].astype(BF16)
            for i in range(n_blk):
                aux_t[i] = auxp[i * WIN:(i + 1) * WIN, :].T[0:8, :]
            _pair_bias(bias_t, slopes_ref, hp, d, key_major=True)
            dvk[...] = jnp.zeros_like(dvk)

            nk = WIN if nb == 1 else KEYS
            bias_cur = jnp.concatenate([bias_t[0, WIN:KEYS, :], bias_t[0, KEYS + WIN:, :]], axis=0) if nb == 1 else None

            def block(i, carry, nb=nb, nk=nk, bias_cur=bias_cur):
                cur = pl.ds(pl.multiple_of(i * WIN, WIN), WIN)
                keys = pl.ds(pl.multiple_of(i * WIN + (KEYS - nk), WIN), nk)
                q2, do2 = qp[cur, :], dop[cur, :]
                kc = _stack_heads(kp[keys, :])
                s_t = lax.dot_general(kc, q2, _NT, preferred_element_type=F32)
                s_t = s_t + (bias_cur if nb == 1 else bias_t[((i % nb) > 0).astype(jnp.int32)])
                dp_t = lax.dot_general(_stack_heads(vp[keys, :]), do2, _NT, preferred_element_type=F32)
                ps, dss = [], []
                for h in range(2):
                    span = slice(h * nk, (h + 1) * nk)
                    p_t = jnp.exp(s_t[span, :] - aux_t[i, 2 * h:2 * h + 1, :])
                    ds_t = p_t * (dp_t[span, :] - aux_t[i, 2 * h + 1:2 * h + 2, :])
                    ps.append(p_t.astype(BF16))
                    dss.append(ds_t.astype(BF16))
                do_rows, q_rows = _stack_heads(do2), _stack_heads(q2)
                zr = jnp.zeros_like(do_rows)
                rhs = jnp.concatenate([jnp.concatenate([do_rows, zr], axis=1),
                                       jnp.concatenate([zr, q_rows], axis=1)], axis=0)
                dvk[keys, :] += jnp.dot(jnp.concatenate(ps + dss, axis=1), rhs, preferred_element_type=F32)
                dqp[cur, :] = lax.dot_general(jnp.concatenate(dss, axis=0), kc, _TN, preferred_element_type=F32)
                return carry

            lax.fori_loop(0, n_blk, block, 0, unroll=ATT_UNROLL)
            for src, dst, n in chunks:
                dq_n[src, :] += dqp[dst:dst + n, :]
                dv_n[src, :] += dvk[WIN + dst:WIN + dst + n, :LANES]
                dk_n[src, :] += dvk[WIN + dst:WIN + dst + n, LANES:]
        for c0 in range(0, S, COPY_ROWS):
            rows = slice(c0, c0 + COPY_ROWS)
            dx_ref[rows, 0:LANES] = (dq_n[rows, :] * ATT_SCALE).astype(dx_ref.dtype)
            dx_ref[rows, LANES:2 * LANES] = dk_n[rows, :].astype(dx_ref.dtype)
            dx_ref[rows, 2 * LANES:3 * LANES] = dv_n[rows, :].astype(dx_ref.dtype)

    pair = lambda width: pl.BlockSpec((None, S, width), lambda b, hp: (b, 0, hp))
    vm = lambda shape, dt: pltpu.VMEM(shape, dt)
    return pl.pallas_call(
        body, name="attention_bwd", grid=(B, n_pair),
        in_specs=[pl.BlockSpec(memory_space=pltpu.SMEM), _qkv_spec(S, 0), _qkv_spec(S, 1), _qkv_spec(S, 2),
                  pair(LANES), pair(LANES), pl.BlockSpec((None, None, S, 2), lambda b, hp: (b, hp, 0, 0))],
        out_specs=pair(3 * LANES),
        out_shape=jax.ShapeDtypeStruct((B, S, 3 * ATT_WIDTH), BF16),
        scratch_shapes=[vm((S, LANES), BF16), vm((S, LANES), BF16),
                        vm((S + WIN, LANES), BF16), vm((S + WIN, LANES), BF16),
                        vm((S, LANES), F32), vm((S, LANES), F32), vm((n_blk, 8, WIN), F32),
                        vm((2, 2 * KEYS, WIN), F32),
                        vm((S, LANES), F32), vm((S + WIN, 2 * LANES), F32),
                        vm((S, LANES), F32), vm((S, LANES), F32), vm((S, LANES), F32)],
        compiler_params=_params(2),
    )(slopes, qkv, qkv, qkv, o, do, lse)


SCAN_COLS = 256
SCAN_ROWS = 8


def _rows_to_tile(rows):
    rid = lax.broadcasted_iota(jnp.int32, (SCAN_ROWS, rows[0].shape[1]), 0)
    tile = jnp.broadcast_to(rows[0], rid.shape)
    for k in range(1, SCAN_ROWS):
        tile = jnp.where(rid == k, rows[k], tile)
    return tile


SCAN_UNROLL = 4


def _complex_powers(ar, ai, n):
    out = [(ar, ai)]
    for _ in range(n - 1):
        pr, pi = out[-1]
        out.append((pr * ar - pi * ai, pr * ai + pi * ar))
    return out


def _round_multipliers(powers, rid, reverse):
    out = []
    for s in (1, 2, 4):
        keep = (rid < SCAN_ROWS - s) if reverse else (rid >= s)
        out.append((jnp.where(keep, powers[s - 1][0], 0.0), jnp.where(keep, powers[s - 1][1], 0.0)))
    return out


def _tile_scan(xr, xi, multipliers, reverse):
    for s, (mr, mi) in zip((1, 2, 4), multipliers):
        shift = SCAN_ROWS - s if reverse else s
        sr, si = pltpu.roll(xr, shift, 0), pltpu.roll(xi, shift, 0)
        xr, xi = xr + (mr * sr - mi * si), xi + (mr * si + mi * sr)
    return xr, xi


SCAN_CHUNK = 256


def _scan_fwd(us, bb_big, a_row, cc_big):
    B, S, _ = us.shape
    groups = 2
    width = 2 * groups * SCAN_COLS
    nc = 2 * SSM_COLS // width
    nt = S // SCAN_ROWS
    tiles = SCAN_CHUNK // SCAN_ROWS
    LAST = slice(SCAN_ROWS - 1, SCAN_ROWS)

    def body(us_ref, bb_ref, a_ref, cc_ref, xs_ref, y_ref, bu_ref):
        bb = bb_ref[...].astype(BF16)
        for c in range(S // SCAN_CHUNK):
            part = jnp.dot(us_ref[c * SCAN_CHUNK:(c + 1) * SCAN_CHUNK, :].astype(BF16), bb,
                           preferred_element_type=F32)
            bu_ref[c * tiles:(c + 1) * tiles] = part.reshape(tiles, SCAN_ROWS, width)
        rid = lax.broadcasted_iota(jnp.int32, (SCAN_ROWS, SCAN_COLS), 0)
        consts = []
        for g in range(groups):
            re = slice(2 * g * SCAN_COLS, (2 * g + 1) * SCAN_COLS)
            im = slice((2 * g + 1) * SCAN_COLS, (2 * g + 2) * SCAN_COLS)
            powers = _complex_powers(a_ref[:, re], a_ref[:, im], SCAN_ROWS)
            carry_mult = (_rows_to_tile([p[0] for p in powers]), _rows_to_tile([p[1] for p in powers]))
            consts.append((re, im, carry_mult, _round_multipliers(powers, rid, reverse=False)))

        def tile(i, carry):
            out = []
            for (re, im, (cr_t, ci_t), rounds), (cr, ci) in zip(consts, carry):
                xr, xi = _tile_scan(bu_ref[i, :, re], bu_ref[i, :, im], rounds, reverse=False)
                xs_ref[i, :, re] = xr + (cr_t * cr - ci_t * ci)
                xs_ref[i, :, im] = xi + (cr_t * ci + ci_t * cr)
                out.append((xs_ref[i, LAST, re], xs_ref[i, LAST, im]))
            return tuple(out)

        zero = jnp.zeros((1, SCAN_COLS), F32)
        lax.fori_loop(0, nt, tile, ((zero, zero),) * groups, unroll=SCAN_UNROLL)

        @pl.when(pl.program_id(1) == 0)
        def _():
            y_ref[...] = jnp.zeros_like(y_ref)

        cc = cc_ref[...].astype(BF16)
        for c in range(S // SCAN_CHUNK):
            x2 = xs_ref[c * tiles:(c + 1) * tiles].reshape(SCAN_CHUNK, width).astype(BF16)
            y_ref[c * SCAN_CHUNK:(c + 1) * SCAN_CHUNK, :] += jnp.dot(x2, cc, preferred_element_type=F32)

    col = pl.BlockSpec((None, nt, SCAN_ROWS, width), lambda b, j: (b, 0, 0, j))
    tok = pl.BlockSpec((None, S, SSM_WIDTH), lambda b, j: (b, 0, 0))
    xs, y = pl.pallas_call(
        body, name="s5_scan_fwd", grid=(B, nc),
        in_specs=[tok, pl.BlockSpec((SSM_WIDTH, width), lambda b, j: (0, j)),
                  pl.BlockSpec((1, width), lambda b, j: (0, j)), pl.BlockSpec((width, SSM_WIDTH), lambda b, j: (j, 0))],
        out_specs=[col, tok],
        out_shape=[jax.ShapeDtypeStruct((B, nt, SCAN_ROWS, 2 * SSM_COLS), F32),
                   jax.ShapeDtypeStruct((B, S, SSM_WIDTH), F32)],
        scratch_shapes=[pltpu.VMEM((nt, SCAN_ROWS, width), F32)],
        compiler_params=_params(2),
    )(us, bb_big, a_row, cc_big)
    return xs.reshape(B, S, 2 * SSM_COLS), y


def _scan_bwd(dy, us, bb_big, cc_big, xs, a_row):
    B, S, _ = dy.shape
    width = 2 * SCAN_COLS
    nc = SSM_COLS // SCAN_COLS
    nt = S // SCAN_ROWS
    tiles = SCAN_CHUNK // SCAN_ROWS
    RE, IM = slice(0, SCAN_COLS), slice(SCAN_COLS, 2 * SCAN_COLS)
    FIRST, LAST = slice(0, 1), slice(SCAN_ROWS - 1, SCAN_ROWS)

    def body(dy_ref, us_ref, bb_ref, cc_ref, x_ref, a_ref, dus_ref, ga_ref, dbb_ref, dcc_ref, d_ref, lam_ref):
        b = pl.program_id(1)
        cc = cc_ref[...].astype(BF16)
        for c in range(S // SCAN_CHUNK):
            part = lax.dot_general(dy_ref[c * SCAN_CHUNK:(c + 1) * SCAN_CHUNK, :].astype(BF16), cc, _NT,
                                   preferred_element_type=F32)
            d_ref[c * tiles:(c + 1) * tiles] = part.reshape(tiles, SCAN_ROWS, width)
        powers = _complex_powers(a_ref[:, RE], -a_ref[:, IM], SCAN_ROWS)
        rid = lax.broadcasted_iota(jnp.int32, (SCAN_ROWS, SCAN_COLS), 0)
        cr_t = _rows_to_tile([powers[SCAN_ROWS - 1 - r][0] for r in range(SCAN_ROWS)])
        ci_t = _rows_to_tile([powers[SCAN_ROWS - 1 - r][1] for r in range(SCAN_ROWS)])
        rounds = _round_multipliers(powers, rid, reverse=True)

        @pl.when(b == 0)
        def _():
            ga_ref[...] = jnp.zeros_like(ga_ref)
            dbb_ref[...] = jnp.zeros_like(dbb_ref)
            dcc_ref[...] = jnp.zeros_like(dcc_ref)

        def tile(j, carry):
            cr, ci, accr, acci = carry
            i = nt - 1 - j
            lr, li = _tile_scan(d_ref[i, :, RE], d_ref[i, :, IM], rounds, reverse=True)
            lam_r = lr + (cr_t * cr - ci_t * ci)
            lam_i = li + (cr_t * ci + ci_t * cr)
            lam_ref[i, :, RE] = lam_r
            lam_ref[i, :, IM] = lam_i
            ip = jnp.maximum(i - 1, 0)
            keep = (i > 0).astype(F32)
            xpr = jnp.where(rid == 0, x_ref[ip, LAST, RE] * keep, pltpu.roll(x_ref[i, :, RE], 1, 0))
            xpi = jnp.where(rid == 0, x_ref[ip, LAST, IM] * keep, pltpu.roll(x_ref[i, :, IM], 1, 0))
            accr = accr + lam_r * xpr + lam_i * xpi
            acci = acci + lam_i * xpr - lam_r * xpi
            return lam_ref[i, FIRST, RE], lam_ref[i, FIRST, IM], accr, acci

        z1 = jnp.zeros((1, SCAN_COLS), F32)
        z8 = jnp.zeros((SCAN_ROWS, SCAN_COLS), F32)
        _, _, accr, acci = lax.fori_loop(0, nt, tile, (z1, z1, z8, z8), unroll=SCAN_UNROLL)
        ga_ref[:, RE] += _col_sum(accr)
        ga_ref[:, IM] += _col_sum(acci)

        bb = bb_ref[...].astype(BF16)
        for c in range(S // SCAN_CHUNK):
            rows = slice(c * SCAN_CHUNK, (c + 1) * SCAN_CHUNK)
            lam2 = lam_ref[c * tiles:(c + 1) * tiles].reshape(SCAN_CHUNK, width).astype(BF16)
            x2 = x_ref[c * tiles:(c + 1) * tiles].reshape(SCAN_CHUNK, width).astype(BF16)
            dus_ref[rows, :] = lax.dot_general(lam2, bb, _NT, preferred_element_type=F32)
            dbb_ref[...] += lax.dot_general(us_ref[rows, :].astype(BF16), lam2, _TN, preferred_element_type=F32)
            dcc_ref[...] += lax.dot_general(x2, dy_ref[rows, :].astype(BF16), _TN, preferred_element_type=F32)

    col = pl.BlockSpec((None, nt, SCAN_ROWS, width), lambda j, b: (b, 0, 0, j))
    tok = pl.BlockSpec((None, S, SSM_WIDTH), lambda j, b: (b, 0, 0))
    scratch = pltpu.VMEM((nt, SCAN_ROWS, width), F32)
    return pl.pallas_call(
        body, name="s5_scan_bwd", grid=(nc, B),
        in_specs=[tok, tok, pl.BlockSpec((SSM_WIDTH, width), lambda j, b: (0, j)),
                  pl.BlockSpec((width, SSM_WIDTH), lambda j, b: (j, 0)), col,
                  pl.BlockSpec((1, width), lambda j, b: (0, j))],
        out_specs=[pl.BlockSpec((None, None, S, SSM_WIDTH), lambda j, b: (j, b, 0, 0)),
                   pl.BlockSpec((1, width), lambda j, b: (0, j)),
                   pl.BlockSpec((SSM_WIDTH, width), lambda j, b: (0, j)),
                   pl.BlockSpec((width, SSM_WIDTH), lambda j, b: (j, 0))],
        out_shape=[jax.ShapeDtypeStruct((nc, B, S, SSM_WIDTH), F32), jax.ShapeDtypeStruct((1, 2 * SSM_COLS), F32),
                   jax.ShapeDtypeStruct((SSM_WIDTH, 2 * SSM_COLS), F32),
                   jax.ShapeDtypeStruct((2 * SSM_COLS, SSM_WIDTH), F32)],
        scratch_shapes=[scratch, scratch],
        compiler_params=_params(2),
    )(dy, us, bb_big, cc_big, xs.reshape(B, nt, SCAN_ROWS, 2 * SSM_COLS), a_row)


def _s5_discretise(lr, li, log_dt):
    dt = jnp.exp(log_dt)
    mag = jnp.exp(lr * dt)
    ang = li * dt
    ab_re, ab_im = mag * jnp.cos(ang), mag * jnp.sin(ang)
    nr, ni = ab_re - 1.0, ab_im
    den = lr * lr + li * li
    f_re = (nr * lr + ni * li) / den
    f_im = (ni * lr - nr * li) / den
    return dt, ab_re, ab_im, nr, ni, den, f_re, f_im


def _s5_params(a_re, a_im, log_dt):
    def body(lr_ref, li_ref, ld_ref, abr, abi, fr, fi):
        _, ab_re, ab_im, _, _, _, f_re, f_im = _s5_discretise(lr_ref[...], li_ref[...], ld_ref[...])
        abr[...] = ab_re
        abi[...] = ab_im
        fr[...] = f_re
        fi[...] = f_im

    return pl.pallas_call(body, name="s5_params",
                          out_shape=[jax.ShapeDtypeStruct(a_re.shape, F32)] * 4)(a_re, a_im, log_dt)


def _s5_input_matrix(f_re, f_im, b_re, b_im):
    def body(fr, fi, br, bi, o_re, o_im):
        o_re[...] = fr[...] * br[...] - fi[...] * bi[...]
        o_im[...] = fr[...] * bi[...] + fi[...] * br[...]

    return pl.pallas_call(body, name="s5_input_matrix",
                          out_shape=[jax.ShapeDtypeStruct(b_re.shape, F32)] * 2)(f_re, f_im, b_re, b_im)


def _s5_input_matrix_bwd(f_re, f_im, b_re, b_im, g_re, g_im):
    def body(fr, fi, br, bi, gr, gi, dbr, dbi, dfr, dfi):
        dbr[...] = fr[...] * gr[...] + fi[...] * gi[...]
        dbi[...] = fr[...] * gi[...] - fi[...] * gr[...]
        dfr[...] = jnp.sum(br[...] * gr[...] + bi[...] * gi[...], axis=1, keepdims=True)
        dfi[...] = jnp.sum(br[...] * gi[...] - bi[...] * gr[...], axis=1, keepdims=True)

    return pl.pallas_call(
        body, name="s5_input_matrix_bwd",
        out_shape=[jax.ShapeDtypeStruct(b_re.shape, F32)] * 2 + [jax.ShapeDtypeStruct(f_re.shape, F32)] * 2,
    )(f_re, f_im, b_re, b_im, g_re, g_im)


def _s5_params_bwd(a_re, a_im, log_dt, g_ab_re, g_ab_im, d_f_re, d_f_im):
    def body(lr_ref, li_ref, ld_ref, gar, gai, dfr, dfi, o_lr, o_li, o_ld):
        lr, li = lr_ref[...], li_ref[...]
        dt, ab_re, ab_im, nr, ni, den, f_re, f_im = _s5_discretise(lr, li, ld_ref[...])
        d_fr, d_fi = dfr[...], dfi[...]
        d_nr = (d_fr * lr - d_fi * li) / den
        d_ni = (d_fr * li + d_fi * lr) / den
        common = (d_fr * f_re + d_fi * f_im) * 2.0 / den
        d_lr = (d_fr * nr + d_fi * ni) / den - common * lr
        d_li = (d_fr * ni - d_fi * nr) / den - common * li
        d_abr = gar[...] + d_nr
        d_abi = gai[...] + d_ni
        d_mag_mag = d_abr * ab_re + d_abi * ab_im
        d_ang = d_abi * ab_re - d_abr * ab_im
        o_lr[...] = d_lr + d_mag_mag * dt
        o_li[...] = d_li + d_ang * dt
        o_ld[...] = jnp.sum(d_mag_mag * lr + d_ang * li, axis=1, keepdims=True) * dt

    return pl.pallas_call(
        body, name="s5_params_bwd",
        out_shape=[jax.ShapeDtypeStruct(a_re.shape, F32)] * 2 + [jax.ShapeDtypeStruct(log_dt.shape, F32)],
    )(a_re, a_im, log_dt, g_ab_re, g_ab_im, d_f_re, d_f_im)


CONV_COLS = 256


def _shift_down(v, j, row):
    return jnp.where(row >= j, pltpu.roll(v, j, 0), 0.0)


def _shift_up(v, j, row, seq):
    return jnp.where(row < seq - j, pltpu.roll(v, seq - j, 0), 0.0)


def _up_conv_fwd(u, w3, w_conv, b_conv):
    B, S, K = u.shape

    def body(u_ref, w_ref, wc_ref, bc_ref, up_ref, ff_ref):
        uv = u_ref[...]
        row = lax.broadcasted_iota(jnp.int32, (S, CONV_COLS), 0)
        for half in range(2):
            cols = slice(half * HALF, (half + 1) * HALF)
            pair = jnp.dot(uv, jnp.concatenate([w_ref[0, :, cols], w_ref[1, :, cols]], axis=1),
                           preferred_element_type=F32)
            up_ref[:, half * 2 * HALF:(half + 1) * 2 * HALF] = pair.astype(up_ref.dtype)
            a, val = pair[:, :HALF], pair[:, HALF:]
            conv = (bc_ref[:, cols] + wc_ref[0:1, cols] * a + wc_ref[1:2, cols] * _shift_down(a, 1, row)
                    + wc_ref[2:3, cols] * _shift_down(a, 2, row))
            ff_ref[:, cols] = (conv * _sigmoid(conv) * val).astype(ff_ref.dtype)

    return pl.pallas_call(
        body, name="ffn_up_conv_gate", grid=(UP_SLOTS, B),
        in_specs=[pl.BlockSpec((None, S, K), lambda j, b: (b, 0, 0)), _up_weight_spec(K, lambda j, b: (0, j, 0, 0)),
                  pl.BlockSpec((3, 2 * HALF), lambda j, b: (0, j)), pl.BlockSpec((1, 2 * HALF), lambda j, b: (0, j))],
        out_specs=[pl.BlockSpec((None, S, UP_GROUP), lambda j, b: (b, 0, j)),
                   pl.BlockSpec((None, S, 2 * HALF), lambda j, b: (b, 0, j))],
        out_shape=[jax.ShapeDtypeStruct((B, S, UP_SLOTS * UP_GROUP), BF16), jax.ShapeDtypeStruct((B, S, D_FF), BF16)],
        compiler_params=_params(2),
    )(u, w3.reshape(2, UP_SLOTS, K, 2 * HALF), w_conv, b_conv)


def _conv_bwd(up, d_down, w_down, w_conv, b_conv):
    B, S, _ = up.shape
    nj = D_FF // CONV_COLS

    def body(up_ref, dd_ref, wd_ref, w_ref, b_ref, dup_ref, dw_ref, db_ref):
        b = pl.program_id(1)
        a = up_ref[:, :CONV_COLS].astype(F32)
        val = up_ref[:, CONV_COLS:].astype(F32)
        row = lax.broadcasted_iota(jnp.int32, a.shape, 0)
        w0, w1, w2 = w_ref[0:1, :], w_ref[1:2, :], w_ref[2:3, :]
        a1, a2 = _shift_down(a, 1, row), _shift_down(a, 2, row)
        conv = b_ref[...] + w0 * a + w1 * a1 + w2 * a2
        sg = _sigmoid(conv)
        dff = lax.dot_general(dd_ref[...], wd_ref[...], _NT, preferred_element_type=F32)
        d_val = dff * conv * sg
        dc = dff * val * (sg * (1.0 + conv * (1.0 - sg)))
        d_a = w0 * dc + w1 * _shift_up(dc, 1, row, S) + w2 * _shift_up(dc, 2, row, S)
        dup_ref[:, :CONV_COLS] = d_a.astype(dup_ref.dtype)
        dup_ref[:, CONV_COLS:] = d_val.astype(dup_ref.dtype)

        @pl.when(b == 0)
        def _():
            dw_ref[...] = jnp.zeros_like(dw_ref)
            db_ref[...] = jnp.zeros_like(db_ref)

        dw_ref[0:1, :] += _col_sum(dc * a)
        dw_ref[1:2, :] += _col_sum(dc * a1)
        dw_ref[2:3, :] += _col_sum(dc * a2)
        db_ref[...] += _col_sum(dc)

    return pl.pallas_call(
        body, name="conv_gate_bwd", grid=(nj, B),
        in_specs=[pl.BlockSpec((None, S, 2 * CONV_COLS), lambda j, b: (b, 0, j)),
                  pl.BlockSpec((None, S, D_MODEL), lambda j, b: (b, 0, 0)),
                  pl.BlockSpec((CONV_COLS, D_MODEL), lambda j, b: (j, 0)),
                  pl.BlockSpec((3, CONV_COLS), lambda j, b: (0, j)),
                  pl.BlockSpec((1, CONV_COLS), lambda j, b: (0, j))],
        out_specs=[pl.BlockSpec((None, S, 2 * CONV_COLS), lambda j, b: (b, 0, j)),
                   pl.BlockSpec((3, CONV_COLS), lambda j, b: (0, j)),
                   pl.BlockSpec((1, CONV_COLS), lambda j, b: (0, j))],
        out_shape=[jax.ShapeDtypeStruct((B, S, 2 * D_FF), BF16), jax.ShapeDtypeStruct((3, D_FF), F32),
                   jax.ShapeDtypeStruct((1, D_FF), F32)],
        compiler_params=_params(2),
    )(up, d_down, w_down, w_conv, b_conv)


def _ada_fwd(c_all, w_ada, b_ada):
    def body(c_ref, w_ref, b_ref, o_ref):
        cv = c_ref[...]
        act = (cv * _sigmoid(cv)).astype(BF16)
        o_ref[...] = jnp.dot(act, w_ref[...].astype(BF16), preferred_element_type=F32) + b_ref[...]

    return pl.pallas_call(body, name="ada_fwd",
                          out_shape=jax.ShapeDtypeStruct((c_all.shape[0], w_ada.shape[1]), F32),
                          compiler_params=pltpu.CompilerParams(vmem_limit_bytes=V7X_VMEM_LIMIT))(c_all, w_ada, b_ada)


def _ada_bwd(c_all, dmod_all, dmod_cols):
    def body(c_ref, dm_ref, dmc_ref, dw_ref, db_ref):
        cv = c_ref[...]
        act = (cv * _sigmoid(cv)).astype(BF16)
        dw_ref[...] = lax.dot_general(act, dmc_ref[...].astype(BF16), _TN, preferred_element_type=F32)
        db_ref[...] = _col_sum(dm_ref[...])

    return pl.pallas_call(
        body, name="ada_bwd",
        out_shape=[jax.ShapeDtypeStruct((c_all.shape[1], dmod_cols.shape[1]), F32),
                   jax.ShapeDtypeStruct((1, dmod_all.shape[1]), F32)],
        compiler_params=pltpu.CompilerParams(vmem_limit_bytes=V7X_VMEM_LIMIT))(c_all, dmod_all, dmod_cols)


def _adamw(w, m, v, g_parts, name, own=None):
    R, C = w.shape
    P = g_parts.shape[0]
    tr = R
    for cand in (256, 128, 64, 32, 16, 8):
        if R % cand == 0 and cand * C * 4 * (P + 8) * 2 <= V7X_VMEM_LIMIT // 2:
            tr = cand
            break
    c1 = 1.0 / (1.0 - ADAM_B1 ** ADAM_STEP)
    c2 = 1.0 / (1.0 - ADAM_B2 ** ADAM_STEP)

    def update(w_ref, m_ref, v_ref, g, og, od, om, ov):
        m_new = ADAM_B1 * m_ref[...] + (1.0 - ADAM_B1) * g
        v_new = ADAM_B2 * v_ref[...] + (1.0 - ADAM_B2) * (g * g)
        og[...] = g
        om[...] = m_new
        ov[...] = v_new
        od[...] = -ADAM_LR * ((m_new * c1) / (jnp.sqrt(v_new * c2) + ADAM_EPS) + ADAM_WD * w_ref[...])

    def total(g_ref):
        g = g_ref[0].astype(F32)
        for p in range(1, P):
            g = g + g_ref[p].astype(F32)
        return g

    out_shape = [jax.ShapeDtypeStruct((R, C), F32)] * 4
    if own is None:
        def body(w_ref, m_ref, v_ref, g_ref, og, od, om, ov):
            update(w_ref, m_ref, v_ref, total(g_ref), og, od, om, ov)

        spec = pl.BlockSpec((tr, C), lambda i: (i, 0))
        return pl.pallas_call(
            body, name=name, grid=(R // tr,),
            in_specs=[spec, spec, spec, pl.BlockSpec((P, tr, C), lambda i: (0, i, 0))],
            out_specs=[spec] * 4, out_shape=out_shape, compiler_params=_params(1),
        )(w, m, v, g_parts)

    slots, me = own

    def body_own(me_ref, w_ref, m_ref, v_ref, g_ref, own_ref, og, od, om, ov):
        g = own_ref[...].astype(F32)
        for p in range(P):
            g = g + jnp.where(me_ref[0] == p, 0.0, g_ref[p].astype(F32))
        update(w_ref, m_ref, v_ref, g, og, od, om, ov)

    spec = pl.BlockSpec((tr, C), lambda i, me_ref: (i, 0))
    grid_spec = pltpu.PrefetchScalarGridSpec(
        num_scalar_prefetch=1, grid=(R // tr,),
        in_specs=[spec, spec, spec, pl.BlockSpec((P, tr, C), lambda i, me_ref: (0, i, 0)),
                  pl.BlockSpec((None, tr, C), lambda i, me_ref: (me_ref[0], i, 0))],
        out_specs=[spec] * 4)
    return pl.pallas_call(body_own, name=name, grid_spec=grid_spec, out_shape=out_shape,
                          compiler_params=_params(1))(me, w, m, v, g_parts, slots)


def _adamw_small(ws, ms, vs, gs):
    n = len(ws)
    c1 = 1.0 / (1.0 - ADAM_B1 ** ADAM_STEP)
    c2 = 1.0 / (1.0 - ADAM_B2 ** ADAM_STEP)

    def body(*refs):
        ins, outs = refs[:4 * n], refs[4 * n:]
        for i in range(n):
            w, m, v, g = ins[i][...], ins[n + i][...], ins[2 * n + i][...], ins[3 * n + i][...]
            m_new = ADAM_B1 * m + (1.0 - ADAM_B1) * g
            v_new = ADAM_B2 * v + (1.0 - ADAM_B2) * (g * g)
            outs[4 * i][...] = g
            outs[4 * i + 1][...] = -ADAM_LR * ((m_new * c1) / (jnp.sqrt(v_new * c2) + ADAM_EPS) + ADAM_WD * w)
            outs[4 * i + 2][...] = m_new
            outs[4 * i + 3][...] = v_new

    out_shape = [jax.ShapeDtypeStruct(w.shape, F32) for w in ws for _ in range(4)]
    return pl.pallas_call(body, name="adamw_small", out_shape=out_shape,
                          compiler_params=pltpu.CompilerParams(vmem_limit_bytes=V7X_VMEM_LIMIT))(*ws, *ms, *vs, *gs)


def _sum_parts(parts, loss_rows):
    P, R, C = parts.shape
    lo, hi = loss_rows

    def body(p_ref, o_ref, loss_ref):
        t = p_ref[0]
        for p in range(1, P):
            t = t + p_ref[p]
        o_ref[...] = t
        tot = jnp.sum(jnp.sum(o_ref[lo:hi, :], axis=1, keepdims=True), axis=0, keepdims=True)
        loss_ref[...] = jnp.broadcast_to(tot, loss_ref.shape)

    return pl.pallas_call(body, name="sum_small_grads",
                          out_shape=[jax.ShapeDtypeStruct((R, C), F32), jax.ShapeDtypeStruct((1, LANES), F32)],
                          compiler_params=pltpu.CompilerParams(vmem_limit_bytes=V7X_VMEM_LIMIT))(parts)


def _exchange(items, name):
    n = len(items)
    MESH = pl.DeviceIdType.MESH

    def body(*refs):
        src, dst = refs[:n], refs[n:2 * n]
        send_sems, recv_sems, local_sems = refs[2 * n:]
        x, y, c = lax.axis_index("x"), lax.axis_index("y"), lax.axis_index("c")
        me = 4 * x + 2 * y + c
        started = []
        for it, (_, per_peer) in enumerate(items):
            own = pltpu.make_async_copy(src[it].at[me] if per_peer else src[it], dst[it].at[me], local_sems.at[it])
            own.start()
            started.append(own)
        sends, recvs = [], []
        for k in range(1, N_DEV):
            px = 1 - x if k & 4 else x
            py = 1 - y if k & 2 else y
            pc = 1 - c if k & 1 else c
            peer = 4 * px + 2 * py + pc
            for it, (_, per_peer) in enumerate(items):
                s = src[it].at[peer] if per_peer else src[it]
                cp = pltpu.make_async_remote_copy(src_ref=s, dst_ref=dst[it].at[me], send_sem=send_sems.at[it, k - 1],
                                                  recv_sem=recv_sems.at[it, k - 1], device_id=(px, py, pc),
                                                  device_id_type=MESH)
                cp.start()
                sends.append(cp)
                recvs.append(pltpu.make_async_remote_copy(
                    src_ref=s, dst_ref=dst[it].at[peer], send_sem=send_sems.at[it, k - 1],
                    recv_sem=recv_sems.at[it, k - 1], device_id=(px, py, pc), device_id_type=MESH))
        for cp in recvs:
            cp.wait_recv()
        for cp in sends:
            cp.wait_send()
        for cp in started:
            cp.wait()

    any_spec = pl.BlockSpec(memory_space=pl.ANY)
    out_shape = []
    for a, per_peer in items:
        shp = a.shape if per_peer else (N_DEV,) + a.shape
        out_shape.append(jax.ShapeDtypeStruct(shp, a.dtype))
    return pl.pallas_call(
        body, name=name, in_specs=[any_spec] * n, out_specs=[any_spec] * n, out_shape=out_shape,
        scratch_shapes=[pltpu.SemaphoreType.DMA((n, N_DEV - 1)), pltpu.SemaphoreType.DMA((n, N_DEV - 1)),
                        pltpu.SemaphoreType.DMA((n,))],
    )(*[a for a, _ in items])


def _remote(src, dst, send_sem, recv_sem, device):
    return pltpu.make_async_remote_copy(src_ref=src, dst_ref=dst, send_sem=send_sem, recv_sem=recv_sem,
                                        device_id=device, device_id_type=pl.DeviceIdType.MESH)


def _mesh_place():
    x, y, c = lax.axis_index("x"), lax.axis_index("y"), lax.axis_index("c")
    other_chips = [(1 - x, y), (x, 1 - y), (1 - x, 1 - y)]
    return x, y, c, (x, y, 1 - c), other_chips


def _gather_all(items, name):
    n = len(items)

    def body(*refs):
        src, dst = refs[:n], refs[n:2 * n]
        send_sems, recv_sems, local_sems = refs[2 * n:]
        x, y, c, sibling, chips = _mesh_place()
        slot = lambda px, py, pc: 4 * px + 2 * py + pc
        me = slot(x, y, c)
        own = [pltpu.make_async_copy(src[it], dst[it].at[me], local_sems.at[it]) for it in range(n)]
        first = []
        for it in range(n):
            first.append(_remote(src[it], dst[it].at[me], send_sems.at[it, 0], recv_sems.at[it, 0], sibling))
            for j, chip in enumerate(chips):
                first.append(_remote(src[it], dst[it].at[me], send_sems.at[it, 1 + j], recv_sems.at[it, 1 + j],
                                     (*chip, c)))
        for cp in own + first:
            cp.start()
        passed = []
        for j, chip in enumerate(chips):
            blk = slot(*chip, c)
            for it in range(n):
                _remote(src[it], dst[it].at[blk], send_sems.at[it, 1 + j], recv_sems.at[it, 1 + j],
                        (*chip, c)).wait_recv()
                fwd = _remote(dst[it].at[blk], dst[it].at[blk], send_sems.at[it, 4 + j], recv_sems.at[it, 4 + j],
                              sibling)
                fwd.start()
                passed.append(fwd)
        for it in range(n):
            _remote(src[it], dst[it].at[slot(x, y, 1 - c)], send_sems.at[it, 0], recv_sems.at[it, 0],
                    sibling).wait_recv()
        for j, chip in enumerate(chips):
            for it in range(n):
                _remote(src[it], dst[it].at[slot(*chip, 1 - c)], send_sems.at[it, 4 + j], recv_sems.at[it, 4 + j],
                        sibling).wait_recv()
        for cp in first + passed:
            cp.wait_send()
        for cp in own:
            cp.wait()

    any_spec = pl.BlockSpec(memory_space=pl.ANY)
    return pl.pallas_call(
        body, name=name, in_specs=[any_spec] * n, out_specs=[any_spec] * n,
        out_shape=[jax.ShapeDtypeStruct((N_DEV,) + a.shape, a.dtype) for a in items],
        scratch_shapes=[pltpu.SemaphoreType.DMA((n, 7)), pltpu.SemaphoreType.DMA((n, 7)),
                        pltpu.SemaphoreType.DMA((n,))],
    )(*items)


def _peers():
    x, y, c = lax.axis_index("x"), lax.axis_index("y"), lax.axis_index("c")
    out = []
    for k in range(1, N_DEV):
        px = 1 - x if k & 4 else x
        py = 1 - y if k & 2 else y
        pc = 1 - c if k & 1 else c
        out.append((k, (px, py, pc), 4 * px + 2 * py + pc))
    return 4 * x + 2 * y + c, out


def _exchange_start(items, name, gather, carry=()):
    n, m = len(items), len(carry)

    def body(*refs):
        src, land = refs[:n], refs[n:2 * n]
        first_out = 2 * n + m
        send_sems, recv_sems = refs[first_out:first_out + n], refs[first_out + n:first_out + 2 * n]
        token = refs[-1]
        me, peers = _peers()
        for k, peer, slot in peers:
            for it in range(n):
                _remote(src[it] if gather else src[it].at[slot], land[it].at[me], send_sems[it], recv_sems[it],
                        peer).start()
        token[...] = jnp.zeros_like(token)

    hbm = pl.BlockSpec(memory_space=pltpu.HBM)
    sem = pl.BlockSpec(memory_space=pltpu.SEMAPHORE)
    land_shapes = [(N_DEV,) + (a.shape if gather else a.shape[1:]) for a in items]
    lands = [lax.empty(shp, a.dtype) for shp, a in zip(land_shapes, items)]
    through = list(items) + lands + list(carry)
    outs = pl.pallas_call(
        body, name=name,
        out_shape=(*[pltpu.SemaphoreType.DMA(())] * (2 * n), *[pltpu.HBM(a.shape, a.dtype) for a in through],
                   jax.ShapeDtypeStruct((8, LANES), F32)),
        in_specs=[hbm] * len(through),
        out_specs=(*[sem] * (2 * n), *[hbm] * len(through), pl.BlockSpec(memory_space=pltpu.VMEM)),
        input_output_aliases={i: 2 * n + i for i in range(len(through))},
        compiler_params=pltpu.CompilerParams(has_side_effects=pltpu.SideEffectType.DATAFLOW_SIDE_EFFECTING),
    )(*[pltpu.with_memory_space_constraint(a, pltpu.HBM) for a in through])
    return (list(outs[:n]), list(outs[n:2 * n]), list(outs[2 * n:3 * n]), list(outs[3 * n:4 * n]), outs[-1],
            list(outs[4 * n:4 * n + m]))


def _exchange_wait(send_sems, recv_sems, items, lands, after, name):
    n = len(items)

    def body(*refs):
        land = refs[n:2 * n]
        send_sems, recv_sems = refs[2 * n:3 * n], refs[3 * n:4 * n]
        me, peers = _peers()
        for it in range(n):
            seven = land[it].at[pl.ds(0, N_DEV - 1)]
            cp = _remote(seven, seven, send_sems[it], recv_sems[it], peers[0][1])
            cp.wait_send()
            cp.wait_recv()

    hbm = pl.BlockSpec(memory_space=pltpu.HBM)
    sem = pl.BlockSpec(memory_space=pltpu.SEMAPHORE)
    outs = pl.pallas_call(
        body, name=name,
        out_shape=tuple(pltpu.HBM(a.shape, a.dtype) for a in list(items) + list(lands)),
        in_specs=[hbm] * (2 * n) + [sem] * (2 * n) + [pl.BlockSpec(memory_space=pl.ANY)],
        out_specs=tuple([hbm] * (2 * n)),
        input_output_aliases={i: i for i in range(2 * n)},
        compiler_params=pltpu.CompilerParams(has_side_effects=pltpu.SideEffectType.DATAFLOW_SIDE_EFFECTING),
    )(*items, *lands, *send_sems, *recv_sems, after)
    return list(outs[:n]), list(outs[n:])


def _gelu_tanh(y):
    k = math.sqrt(2.0 / math.pi)
    t = jnp.tanh(k * (y + 0.044715 * y * y * y))
    return 0.5 * y * (1.0 + t), t


def _local_step(x, mod, target, W, late_weights, P, send_early):
    B, S, D = x.shape
    T = B * S
    TS = 512
    flat = lambda a: a.reshape(T, a.shape[-1])
    unflat = lambda a: a.reshape(B, S, a.shape[-1])
    mod_col = lambda i: (mod, D, i)

    def f_modnorm_project(xv, sc, sh, g, wq, wu, wg):
        u = ((xv * _rms_scale(xv) * g) * (1.0 + sc) + sh).astype(BF16)
        return (u, lax.dot_general(u, wq, _NT, preferred_element_type=F32),
                lax.dot_general(u, wu, _NT, preferred_element_type=F32),
                lax.dot_general(u, wg, _NT, preferred_element_type=F32))

    u1, qkv, us, gates = _rowwise(
        f_modnorm_project, [(x, D, 0)], [mod_col(1), mod_col(0)], [P["g_mix"], W["w_qkv"], W["w_us"], W["w_gates"]],
        [(D, BF16), (3 * ATT_WIDTH, F32), (SSM_WIDTH, F32), (2 * D, BF16)], [], [], ts=TS, name="modnorm_project_in")
    u1f = flat(u1)

    o_att, lse = _attention_fwd(qkv, P["slopes"])
    more_w, more_p = late_weights(o_att)
    W, P = {**W, **more_w}, {**P, **more_p}

    xs, y_mm = _scan_fwd(us, P["bb_big"], P["a_row"], P["cc_big"])

    bga, bgs = P["b_gate"][:, :D], P["b_gate"][:, D:]

    def f_mixer_tail(ov, ymm, usv, ga, gs, xv, gt, sc, sh, w_att, w_ssm, w_o, bga_, bgs_, g, dsk, wg, bg):
        yv = ymm + dsk * usv
        ge, _ = _gelu_tanh(yv)
        zv = (ge * _sigmoid(jnp.dot(ge.astype(BF16), wg, preferred_element_type=F32) + bg)).astype(BF16)
        ya = jnp.dot(ov, w_att, preferred_element_type=F32)
        ys = jnp.dot(zv, w_ssm, preferred_element_type=F32)
        mg = (_sigmoid(ga + bga_) * ya + _sigmoid(gs + bgs_) * ys).astype(BF16)
        mx = jnp.dot(mg, w_o, preferred_element_type=F32)
        h = xv + gt * mx
        return yv, zv, mg, mx, h, (h * _rms_scale(h) * g) * (1.0 + sc) + sh

    y_s5, z, merged, mix, h1, u2 = _rowwise(
        f_mixer_tail,
        [(o_att, ATT_WIDTH, 0), (y_mm, SSM_WIDTH, 0), (us, SSM_WIDTH, 0), (gates, D, 0), (gates, D, 1), (x, D, 0)],
        [mod_col(2), mod_col(4), mod_col(3)],
        [W["w_proj_att"], W["w_proj_ssm"], W["w_out"], bga, bgs, P["g_ffn"], P["d_skip"], W["w_glu"], P["b_glu"]],
        [(SSM_WIDTH, F32), (SSM_WIDTH, BF16), (D, BF16), (D, BF16), (D, F32), (D, BF16)],
        [], [], ts=TS, raw=(0,), name="mixer_tail")

    up, ff = _up_conv_fwd(u2, W["w_up"], P["w_conv"], P["b_conv"])

    def f_head(ffv, h1v, tg, gt, g, w_dn):
        dn = jnp.dot(ffv, w_dn, preferred_element_type=F32)
        h2 = h1v + gt * dn
        r = _rms_scale(h2)
        nh = h2 * r
        e = nh * g - tg
        dy = e * (1.0 / D)
        gy = dy * g
        dh = r * (gy - nh * jnp.mean(gy * nh, axis=-1, keepdims=True))
        return (dh, dh * gt, _col_sum(dh * dn), _col_sum(dy * nh), _col_sum(e * e) * (0.5 / D))

    dh2, d_down, d_gt2, d_g_final, loss_cols = _rowwise(
        f_head, [(ff, D_FF, 0), (h1, D, 0), (target, D, 0)], [mod_col(5)], [P["g_final"], W["w_down"]],
        [(D, BF16), (D, BF16)], [D], [(1, D), (1, D)], ts=TS, raw=(0,), name="ffn_down_head_loss")

    d_downf = flat(d_down)
    d_w_down = _matmul(flat(ff), d_downf, ta=True, out_dtype=BF16, name="ffn_down_dw")
    d_up, d_w_conv, d_b_conv = _conv_bwd(up, d_down, W["w_down"], P["w_conv"], P["b_conv"])
    d_upf = flat(d_up)
    d_w_up = _up_dw(flat(u2), d_upf, name="ffn_up_dw")
    token, _ = send_early(dict(w_down=d_w_down.reshape(N_DEV, D_FF // N_DEV, D), w_up=d_w_up))
    g_ffn_after = P["g_ffn"] + token[0:1, 0:1]

    def f_up_back_modnorm(dup, h, dres, mx, sc, gt, g, w2):
        du = None
        for j in range(UP_SLOTS):
            wa, wv = w2[j * D:(j + 1) * D, :], w2[(j + UP_SLOTS) * D:(j + UP_SLOTS + 1) * D, :]
            wj = jnp.concatenate([wa[:, :HALF], wv[:, :HALF], wa[:, HALF:], wv[:, HALF:]], axis=1)
            part = lax.dot_general(dup[:, j * UP_GROUP:(j + 1) * UP_GROUP], wj, _NT, preferred_element_type=F32)
            du = part if du is None else du + part
        r = _rms_scale(h)
        nh = h * r
        dn = du * (1.0 + sc)
        gy = dn * g
        dh = dres + r * (gy - nh * jnp.mean(gy * nh, axis=-1, keepdims=True))
        return (dh, dh * gt, _col_sum(du), _col_sum(du * nh * g), _col_sum(dh * mx), _col_sum(dn * nh))

    dh1, d_mix, d_sh2, d_sc2, d_gt1, d_g_ffn = _rowwise(
        f_up_back_modnorm, [(d_up, 2 * D_FF, 0), (h1, D, 0), (dh2, D, 0), (mix, D, 0)], [mod_col(4), mod_col(2)],
        [g_ffn_after, W["w_up"].reshape(N_DEV * D, 2 * HALF)],
        [(D, BF16), (D, BF16)], [D, D, D], [(1, D)], ts=TS, raw=(0,),
        name="ffn_up_back_modnorm", split=2)

    d_mixf = flat(d_mix)
    d_w_out = _matmul(flat(merged), d_mixf, ta=True, out_dtype=BF16, name="proj_out_dw")

    def f_mixer_tail_bwd(dmx, ov, zv, ga, gs, yv, usv, w_o, w_att, w_ssm, bga_, bgs_, dsk, wg, bg):
        dm = lax.dot_general(dmx, w_o, _NT, preferred_element_type=F32)
        ya = jnp.dot(ov, w_att, preferred_element_type=F32)
        ys = jnp.dot(zv, w_ssm, preferred_element_type=F32)
        sa, ss = _sigmoid(ga + bga_), _sigmoid(gs + bgs_)
        dga = dm * ya * sa * (1.0 - sa)
        dgs = dm * ys * ss * (1.0 - ss)
        dya, dys = (dm * sa).astype(BF16), (dm * ss).astype(BF16)
        d_o = lax.dot_general(dya, w_att, _NT, preferred_element_type=F32)
        dz = lax.dot_general(dys, w_ssm, _NT, preferred_element_type=F32)
        ge, t = _gelu_tanh(yv)
        sg = _sigmoid(jnp.dot(ge.astype(BF16), wg, preferred_element_type=F32) + bg)
        dpre = dz * ge * sg * (1.0 - sg)
        dge = dz * sg + lax.dot_general(dpre.astype(BF16), wg, _NT, preferred_element_type=F32)
        k = math.sqrt(2.0 / math.pi)
        dgelu = 0.5 * (1.0 + t) + 0.5 * yv * (1.0 - t * t) * k * (1.0 + 3.0 * 0.044715 * yv * yv)
        dy = dge * dgelu
        dwg = lax.dot_general(ge.astype(BF16), dpre.astype(BF16), _TN, preferred_element_type=F32)
        return (dya, dys, jnp.concatenate([dga, dgs], axis=1), d_o, dy, dy * dsk,
                _col_sum(dga), _col_sum(dgs), dwg, _col_sum(dpre), _col_sum(dy * usv))

    (d_y_att, d_y_ssm, d_gates, d_o_att, d_y_s5, d_us_skip, d_bga, d_bgs, d_w_glu, d_b_glu, d_d_skip) = _rowwise(
        f_mixer_tail_bwd,
        [(d_mix, D, 0), (o_att, ATT_WIDTH, 0), (z, SSM_WIDTH, 0), (gates, D, 0), (gates, D, 1),
         (y_s5, SSM_WIDTH, 0), (us, SSM_WIDTH, 0)], [],
        [W["w_out"], W["w_proj_att"], W["w_proj_ssm"], bga, bgs, P["d_skip"], W["w_glu"], P["b_glu"]],
        [(D, BF16), (D, BF16), (2 * D, BF16), (ATT_WIDTH, F32), (SSM_WIDTH, BF16), (SSM_WIDTH, F32)], [],
        [(1, D), (1, D), (SSM_WIDTH, SSM_WIDTH), (1, SSM_WIDTH), (1, SSM_WIDTH)], ts=TS, raw=(0, 1, 2),
        name="mixer_tail_bwd")

    d_yaf, d_ysf = flat(d_y_att), flat(d_y_ssm)
    d_w_proj_att = _matmul(flat(o_att), d_yaf, ta=True, out_dtype=BF16, name="proj_att_dw")
    d_w_proj_ssm = _matmul(flat(z), d_ysf, ta=True, out_dtype=BF16, name="proj_ssm_dw")
    d_us_parts, g_ab, d_bb, d_cc = _scan_bwd(d_y_s5, us, P["bb_big"], P["cc_big"], xs, P["a_row"])

    token, _ = send_early(dict(
        w_out=d_w_out.reshape(N_DEV, D // N_DEV, D), w_proj_att=_cols_to_slots(d_w_proj_att),
        w_proj_ssm=_cols_to_slots(d_w_proj_ssm),
        w_glu=d_w_glu.astype(BF16).reshape(N_DEV, SSM_WIDTH // N_DEV, SSM_WIDTH),
        w_conv=_cols_to_slots(d_w_conv.astype(BF16))))
    d_qkv = _attention_bwd(qkv, o_att, d_o_att, lse, P["slopes"] + token[0, 0])

    def f_add(*parts):
        return sum(parts[1:], parts[0])

    n_parts = d_us_parts.shape[0]
    stacked = d_us_parts.reshape(n_parts * B, S, SSM_WIDTH)
    (d_us,) = _rowwise(f_add, [(d_us_skip, SSM_WIDTH, 0)] + [(stacked, SSM_WIDTH, 0, j * B) for j in range(n_parts)],
                       [], [],
                       [(SSM_WIDTH, BF16)], [], [], ts=TS, name="s5_input_grad")
    d_qkvf = flat(d_qkv)
    d_usf = flat(d_us)
    d_gatesf = flat(d_gates)
    d_w_in_t = jnp.concatenate(
        [_unpair_qkv_rows(_matmul(d_qkvf, u1f, ta=True, out_dtype=BF16, name="proj_qkv_dw")),
         _matmul(d_usf, u1f, ta=True, out_dtype=BF16, name="proj_ssm_in_dw"),
         _matmul(d_gatesf, u1f, ta=True, out_dtype=BF16, name="proj_gates_dw")], axis=0)
    token, (w_qkv, w_us, w_gates) = send_early(dict(w_in=d_w_in_t.reshape(N_DEV, -1, D)),
                                               carry=[W["w_qkv"], W["w_us"], W["w_gates"]])
    def f_project_back_modnorm(dq, du_, dg, h, dres, sc, g, wq, wu, wg):
        du = (jnp.dot(dq, wq, preferred_element_type=F32) + jnp.dot(du_, wu, preferred_element_type=F32)
              + jnp.dot(dg, wg, preferred_element_type=F32))
        r = _rms_scale(h)
        nh = h * r
        dn = du * (1.0 + sc)
        gy = dn * g
        dh = dres + r * (gy - nh * jnp.mean(gy * nh, axis=-1, keepdims=True))
        return (dh, _col_sum(du), _col_sum(du * nh * g), _col_sum(dn * nh))

    grad_x, d_sh1, d_sc1, d_g_mix = _rowwise(
        f_project_back_modnorm,
        [(d_qkv, 3 * ATT_WIDTH, 0), (d_us, SSM_WIDTH, 0), (d_gates, 2 * D, 0), (x, D, 0), (dh1, D, 0)], [mod_col(1)],
        [P["g_mix"] + token[0:1, 0:1], w_qkv, w_us, w_gates],
        [(D, F32)], [D, D], [(1, D)], ts=TS, raw=(0, 1, 2),
        name="project_in_back_modnorm", split=2)

    d_mod = jnp.concatenate([d_sh1, d_sc1, d_gt1, d_sh2, d_sc2, d_gt2], axis=-1)
    g_ab_re, g_ab_im = _deinterleave(g_ab)
    d_bb_re, d_bb_im = _deinterleave(d_bb)
    d_cc_re, d_cc_im = (t.T for t in _deinterleave(d_cc.T))
    small = dict(g_mix=d_g_mix, b_gate=jnp.concatenate([d_bga, d_bgs], axis=1), g_ab_re=g_ab_re, g_ab_im=g_ab_im,
                 d_bb_re=d_bb_re, d_bb_im=d_bb_im, d_cc_re=d_cc_re, d_cc_im=d_cc_im, d_skip=d_d_skip,
                 b_glu=d_b_glu, g_ffn=d_g_ffn, b_conv=d_b_conv, g_final=d_g_final, loss_cols=loss_cols)
    return grad_x, d_mod, small


def _block_diag_in(bb):
    t = bb.reshape(SSM_GROUPS, SSM_STATE, SSM_GROUP_CH)
    eye = jnp.eye(SSM_GROUPS, dtype=bb.dtype)
    return jnp.einsum("gnc,gh->gchn", t, eye).reshape(SSM_WIDTH, SSM_COLS)


def _block_diag_out(cm):
    eye = jnp.eye(SSM_GROUPS, dtype=cm.dtype)
    return jnp.einsum("gcn,gh->gnhc", cm, eye).reshape(SSM_COLS, SSM_WIDTH)


def _diag_blocks_in(m):
    t = m.reshape(SSM_GROUPS, SSM_GROUP_CH, SSM_GROUPS, SSM_STATE)
    idx = jnp.arange(SSM_GROUPS)
    return t[idx, :, idx, :].transpose(0, 2, 1).reshape(SSM_COLS, SSM_GROUP_CH)


def _diag_blocks_out(m):
    t = m.reshape(SSM_GROUPS, SSM_STATE, SSM_GROUPS, SSM_GROUP_CH)
    idx = jnp.arange(SSM_GROUPS)
    return t[idx, :, idx, :].transpose(0, 2, 1)


def _pair_qkv_rows(w):
    return w.reshape(3, N_HEADS // 2, LANES, w.shape[1]).swapaxes(0, 1).reshape(w.shape)


def _unpair_qkv_rows(w):
    return w.reshape(N_HEADS // 2, 3, LANES, w.shape[1]).swapaxes(0, 1).reshape(w.shape)


def _interleave(re, im):
    lead = re.shape[:-1]
    g = lambda a: a.reshape(lead + (SSM_COLS // SCAN_COLS, 1, SCAN_COLS))
    return jnp.concatenate([g(re), g(im)], axis=-2).reshape(lead + (2 * SSM_COLS,))


def _deinterleave(x):
    lead = x.shape[:-1]
    t = x.reshape(lead + (SSM_COLS // SCAN_COLS, 2, SCAN_COLS))
    return t[..., 0, :].reshape(lead + (SSM_COLS,)), t[..., 1, :].reshape(lead + (SSM_COLS,))


def _cols_to_slots(g):
    R = g.shape[0]
    return g.reshape(R, N_DEV, g.shape[1] // N_DEV).transpose(1, 0, 2)


def _slots_to_cols(g):
    return g.transpose(1, 0, 2).reshape(g.shape[1], N_DEV * g.shape[2])


SMALL_ORDER = ("b_ada", "g_mix", "b_gate", "a_re", "a_im", "log_dt", "b_re", "b_im", "c_re", "c_im", "d_skip",
               "b_glu", "g_ffn", "b_conv", "g_final")


def _pack(arrs):
    pieces, offs, row = [], [], 0
    for a in arrs:
        f = a.reshape(-1).astype(F32)
        n = f.shape[0]
        rows = -(-n // LANES)
        pieces.append(jnp.pad(f, (0, rows * LANES - n)))
        offs.append((row, n))
        row += rows
    return jnp.concatenate(pieces).reshape(row, LANES), offs


def _unpack(packed, offs, shapes):
    flat = packed.reshape(-1)
    return [flat[r * LANES:r * LANES + n].reshape(s) for (r, n), s in zip(offs, shapes)]


def kernel(x, c, w_ada, b_ada, g_mix, w_in, b_gate, a_re, a_im, log_dt, b_re, b_im, c_re, c_im, d_skip, w_glu, b_glu, w_proj_att, w_proj_ssm, w_out, g_ffn, w_up, w_conv, b_conv, w_down, g_final, loss_target, m_w_ada, m_b_ada, m_g_mix, m_w_in, m_b_gate, m_a_re, m_a_im, m_log_dt, m_b_re, m_b_im, m_c_re, m_c_im, m_d_skip, m_w_glu, m_b_glu, m_w_proj_att, m_w_proj_ssm, m_w_out, m_g_ffn, m_w_up, m_w_conv, m_b_conv, m_w_down, m_g_final, v_w_ada, v_b_ada, v_g_mix, v_w_in, v_b_gate, v_a_re, v_a_im, v_log_dt, v_b_re, v_b_im, v_c_re, v_c_im, v_d_skip, v_w_glu, v_b_glu, v_w_proj_att, v_w_proj_ssm, v_w_out, v_g_ffn, v_w_up, v_w_conv, v_b_conv, v_w_down, v_g_final):
    args = dict(locals())
    B, S, D = x.shape
    me = 4 * lax.axis_index("x") + 2 * lax.axis_index("y") + lax.axis_index("c")
    bf = lambda w: w[0].astype(BF16)

    c_slots, w_in_slots = _gather_all([c, w_in[0].T.astype(BF16)], name="gather_first_weights")
    c_all = c_slots.reshape(N_DEV * B, D)
    w_in_t = w_in_slots.reshape(-1, D)
    n_qkv = 3 * ATT_WIDTH
    W = dict(w_qkv=_pair_qkv_rows(w_in_t[:n_qkv]), w_us=w_in_t[n_qkv:n_qkv + SSM_WIDTH],
             w_gates=w_in_t[n_qkv + SSM_WIDTH:])

    n_ada = w_ada.shape[2]
    b_ada_cols = lax.dynamic_slice(b_ada, (0, me * n_ada), (1, n_ada))
    mod_part = _ada_fwd(c_all, w_ada[0], b_ada_cols)
    (mod_slots,) = _exchange([(mod_part.reshape(N_DEV, B, n_ada), True)], name="scatter_modulation")
    mod = mod_slots.transpose(1, 0, 2).reshape(B, 1, 6 * D)

    later = [bf(w_glu), bf(w_proj_att), bf(w_proj_ssm), bf(w_out), bf(w_up), w_conv[0], bf(w_down)]
    later_sems = _exchange_start(later, "start_later_weights", gather=True, carry=[mod])
    (mod,) = later_sems[5]

    def late_weights(after):
        _, lands = _exchange_wait(*later_sems[:4], after, name="wait_later_weights")
        g = [lax.dynamic_update_index_in_dim(land, a, me, 0) for land, a in zip(lands, later)]
        more_w = dict(w_glu=g[0].reshape(SSM_WIDTH, SSM_WIDTH), w_proj_att=_slots_to_cols(g[1]),
                      w_proj_ssm=_slots_to_cols(g[2]), w_out=g[3].reshape(D, D), w_up=g[4],
                      w_down=g[6].reshape(D_FF, D))
        return more_w, dict(w_conv=_slots_to_cols(g[5]))

    ab_re, ab_im, f_re, f_im = _s5_params(a_re[0], a_im[0], log_dt[0].reshape(SSM_GROUPS, 1))
    col = lambda a: a.reshape(SSM_COLS, 1)
    b_re2, b_im2 = b_re[0].reshape(SSM_COLS, SSM_GROUP_CH), b_im[0].reshape(SSM_COLS, SSM_GROUP_CH)
    bb_re, bb_im = _s5_input_matrix(col(f_re), col(f_im), b_re2, b_im2)
    slopes = jnp.asarray([2.0 ** (-8.0 * (h + 1) / N_HEADS) for h in range(N_HEADS)], F32)
    P = dict(g_mix=g_mix, g_ffn=g_ffn, g_final=g_final.reshape(1, D), b_gate=b_gate, d_skip=d_skip, b_glu=b_glu,
             b_conv=b_conv, slopes=slopes,
             a_row=_interleave(ab_re.reshape(1, SSM_COLS), ab_im.reshape(1, SSM_COLS)),
             bb_big=_interleave(_block_diag_in(bb_re), _block_diag_in(bb_im)),
             cc_big=_interleave(_block_diag_out(c_re[0]).T, -_block_diag_out(c_im[0]).T).T)

    in_flight = []

    def send_early(grads, carry=()):
        names = list(grads)
        handles = _exchange_start([grads[n] for n in names], "start_gradients_%d" % len(in_flight), gather=False,
                                  carry=carry)
        in_flight.append((names,) + handles[:4])
        return handles[4], handles[5]

    grad_x, d_mod, small = _local_step(x, mod, loss_target, W, late_weights, P, send_early)

    small_list = [small["loss_cols"], small["g_mix"], small["b_gate"], small["g_ab_re"], small["g_ab_im"],
                  _diag_blocks_in(small["d_bb_re"]), _diag_blocks_in(small["d_bb_im"]),
                  _diag_blocks_out(small["d_cc_re"]), -_diag_blocks_out(small["d_cc_im"]),
                  small["g_ffn"], small["b_conv"], small["g_final"], small["d_skip"], small["b_glu"]]
    small_packed, small_offs = _pack(small_list)
    small_sems = _exchange_start([small_packed, d_mod.reshape(B, 6 * D)], "start_small_gradients", gather=True)
    updated = [small_sems[4]]

    out = {}

    def update(name, parts, own=None):
        view = (lambda a: a[0].T) if name == "w_in" else (lambda a: a[0])
        back = (lambda a: a.T[None]) if name == "w_in" else (lambda a: a[None])
        g, dl, mn, vn = _adamw(view(args[name]), view(args["m_" + name]), view(args["v_" + name]), parts,
                               name="adamw_" + name, own=own)
        updated.append(vn)
        for key, val in (("grad_", g), ("delta_", dl), ("new_m_", mn), ("new_v_", vn)):
            out[key + name] = back(val)

    my_slot = me.astype(jnp.int32).reshape(1)
    for i, (names, send_sems, recv_sems, sent, lands) in enumerate(in_flight):
        sent, lands = _exchange_wait(send_sems, recv_sems, sent, lands, updated[-1], name="wait_gradients_%d" % i)
        for name, own_slots, landed in zip(names, sent, lands):
            update(name, landed, own=(own_slots, my_slot))

    own_small, small_lands = _exchange_wait(*small_sems[:4], updated[-1], name="wait_small_gradients")
    small_all, dmod_slots = [lax.dynamic_update_index_in_dim(land, a, me, 0)
                             for land, a in zip(small_lands, own_small)]
    dmod_all = dmod_slots.reshape(N_DEV * B, 6 * D)
    dmod_cols = lax.dynamic_slice(dmod_all, (0, me * n_ada), (N_DEV * B, n_ada))
    d_w_ada, d_b_ada = _ada_bwd(c_all, dmod_all, dmod_cols)
    update("w_ada", d_w_ada[None])

    loss_row, loss_n = small_offs[0]
    small_sum, loss_vec = _sum_parts(small_all, (loss_row, loss_row + loss_n // LANES))
    shapes = [(1, D), (1, D), (1, 2 * D), (SSM_GROUPS, SSM_STATE), (SSM_GROUPS, SSM_STATE), (SSM_COLS, SSM_GROUP_CH),
              (SSM_COLS, SSM_GROUP_CH), (1, SSM_GROUPS, SSM_GROUP_CH, SSM_STATE),
              (1, SSM_GROUPS, SSM_GROUP_CH, SSM_STATE), (1, D), (1, D_FF), (D,), (1, SSM_WIDTH), (1, SSM_WIDTH)]
    (_, s_g_mix, s_b_gate, s_ab_re, s_ab_im, s_bb_re, s_bb_im, s_c_re, s_c_im, s_g_ffn, s_b_conv, s_g_final,
     s_d_skip, s_b_glu) = _unpack(small_sum, small_offs, shapes)
    d_b_re2, d_b_im2, d_f_re, d_f_im = _s5_input_matrix_bwd(col(f_re), col(f_im), b_re2, b_im2, s_bb_re, s_bb_im)
    d_a_re, d_a_im, d_log_dt = _s5_params_bwd(a_re[0], a_im[0], log_dt[0].reshape(SSM_GROUPS, 1), s_ab_re, s_ab_im,
                                              d_f_re.reshape(SSM_GROUPS, SSM_STATE),
                                              d_f_im.reshape(SSM_GROUPS, SSM_STATE))
    grads_small = dict(b_ada=d_b_ada, g_mix=s_g_mix, b_gate=s_b_gate, a_re=d_a_re[None], a_im=d_a_im[None],
                       log_dt=d_log_dt.reshape(1, SSM_GROUPS), b_re=d_b_re2.reshape(b_re.shape),
                       b_im=d_b_im2.reshape(b_im.shape), c_re=s_c_re, c_im=s_c_im, d_skip=s_d_skip, b_glu=s_b_glu,
                       g_ffn=s_g_ffn, b_conv=s_b_conv, g_final=s_g_final)
    flat2 = lambda a: a.reshape(-1, a.shape[-1])
    res = _adamw_small([flat2(args[n]) for n in SMALL_ORDER], [flat2(args["m_" + n]) for n in SMALL_ORDER],
                       [flat2(args["v_" + n]) for n in SMALL_ORDER],
                       [flat2(grads_small[n].reshape(args[n].shape)) for n in SMALL_ORDER])
    for i, n in enumerate(SMALL_ORDER):
        for k, key in enumerate(("grad_", "delta_", "new_m_", "new_v_")):
            out[key + n] = res[4 * i + k].reshape(args[n].shape)

    order = ["w_ada", "b_ada", "g_mix", "w_in", "b_gate", "a_re", "a_im", "log_dt", "b_re", "b_im", "c_re", "c_im",
             "d_skip", "w_glu", "b_glu", "w_proj_att", "w_proj_ssm", "w_out", "g_ffn", "w_up", "w_conv", "b_conv",
             "w_down", "g_final"]
    loss = loss_vec[0, 0]
    return (loss, grad_x, *[out[k + n] for k in ("grad_", "delta_", "new_m_", "new_v_") for n in order])
```

```python
import math

import jax
import jax.numpy as jnp
from jax import lax
from jax.experimental import pallas as pl
from jax.experimental.pallas import tpu as pltpu

F32 = jnp.float32
BF16 = jnp.bfloat16

N_DEV = 8
D_MODEL = 1024
N_HEADS = 8
HEAD_DIM = 64
ATT_WIDTH = N_HEADS * HEAD_DIM
DILATIONS = (1, 4, 16)
WIN = 128
SSM_GROUPS = 16
SSM_GROUP_CH = 16
SSM_WIDTH = SSM_GROUPS * SSM_GROUP_CH
SSM_STATE = 64
SSM_COLS = SSM_GROUPS * SSM_STATE
D_FF = 2048
EPS = 1e-6
NEG_INF = -1e30
ADAM_LR, ADAM_B1, ADAM_B2, ADAM_EPS, ADAM_WD, ADAM_STEP = 0.001, 0.9, 0.999, 1e-08, 0.01, 10

V7X_VMEM_LIMIT = 56 * 1024 * 1024
LANES = 128


def _params(n_grid):
    return pltpu.CompilerParams(dimension_semantics=("arbitrary",) * n_grid,
                                vmem_limit_bytes=V7X_VMEM_LIMIT)


def _tile(n, pref):
    if n <= pref:
        return n
    t = (pref // LANES) * LANES
    while t > 0:
        if n % t == 0:
            return t
        t -= LANES
    return n


def _matmul(a, b, *, ta=False, tb=False, out_dtype=F32, name):
    if ta:
        K, M = a.shape
    else:
        M, K = a.shape
    if tb:
        N, K2 = b.shape
    else:
        K2, N = b.shape
    assert K == K2, (a.shape, b.shape)
    if ta:
        tm, tn, tk = _tile(M, 1024), _tile(N, 2048), _tile(K, 1024)
    else:
        tm, tk = _tile(M, 512), _tile(K, 4096)
        tn = _tile(N, 2048 if K <= 2048 else 1024)
    nk = K // tk
    dn = (((0,) if ta else (1,), (1,) if tb else (0,)), ((), ()))

    def body(a_ref, b_ref, o_ref, acc_ref):
        k = pl.program_id(2)
        part = lax.dot_general(a_ref[...].astype(BF16), b_ref[...].astype(BF16), dn, preferred_element_type=F32)
        if nk == 1:
            o_ref[...] = part.astype(o_ref.dtype)
            return

        @pl.when(k == 0)
        def _():
            acc_ref[...] = jnp.zeros_like(acc_ref)

        acc_ref[...] += part

        @pl.when(k == nk - 1)
        def _():
            o_ref[...] = acc_ref[...].astype(o_ref.dtype)

    a_spec = (pl.BlockSpec((tk, tm), lambda j, i, k: (k, i)) if ta
              else pl.BlockSpec((tm, tk), lambda j, i, k: (i, k)))
    b_spec = (pl.BlockSpec((tn, tk), lambda j, i, k: (j, k)) if tb
              else pl.BlockSpec((tk, tn), lambda j, i, k: (k, j)))
    return pl.pallas_call(
        body, name=name, grid=(N // tn, M // tm, nk),
        in_specs=[a_spec, b_spec],
        out_specs=pl.BlockSpec((tm, tn), lambda j, i, k: (i, j)),
        out_shape=jax.ShapeDtypeStruct((M, N), out_dtype),
        scratch_shapes=[pltpu.VMEM((tm, tn) if nk > 1 else (8, LANES), F32)],
        compiler_params=_params(3),
    )(a, b)


HALF = 256
UP_SLOTS = N_DEV // 2
UP_GROUP = 4 * HALF


def _up_weight_spec(K, index):
    return pl.BlockSpec((2, None, K, 2 * HALF), index)


def _up_dw(a, d, name):
    M, K = a.shape
    tk = _tile(M, 1024)
    nk = M // tk

    def body(a_ref, d_ref, o_ref, acc_ref):
        k = pl.program_id(1)

        @pl.when(k == 0)
        def _():
            acc_ref[...] = jnp.zeros_like(acc_ref)

        acc_ref[...] += lax.dot_general(a_ref[...], d_ref[...], _TN, preferred_element_type=F32)

        @pl.when(k == nk - 1)
        def _():
            for half in range(2):
                for part in range(2):
                    lo = (2 * half + part) * HALF
                    o_ref[part, :, half * HALF:(half + 1) * HALF] = acc_ref[:, lo:lo + HALF].astype(o_ref.dtype)

    out = pl.pallas_call(
        body, name=name, grid=(UP_SLOTS, nk),
        in_specs=[pl.BlockSpec((tk, K), lambda j, k: (k, 0)), pl.BlockSpec((tk, UP_GROUP), lambda j, k: (k, j))],
        out_specs=_up_weight_spec(K, lambda j, k: (0, j, 0, 0)),
        out_shape=jax.ShapeDtypeStruct((2, UP_SLOTS, K, 2 * HALF), BF16),
        scratch_shapes=[pltpu.VMEM((K, UP_GROUP), F32)], compiler_params=_params(2),
    )(a, d)
    return out.reshape(N_DEV, K, 2 * HALF)


def _rowwise(fn, rows, bvecs, consts, out_rows, out_b, out_g, *, ts, name, raw=(), split=1):
    B, S = rows[0][0].shape[:2]
    nin = len(rows) + len(bvecs) + len(consts)
    nr, nb, ng = len(out_rows), len(out_b), len(out_g)
    sub = ts // split

    def body(*refs):
        b = pl.program_id(0)
        s = pl.program_id(1)
        orefs = refs[nin:]
        fixed = [r[...] for r in refs[len(rows):nin]]
        sums = None
        for h in range(split):
            sec = slice(h * sub, (h + 1) * sub)
            vals = [r[sec, :] if i in raw else r[sec, :].astype(F32) for i, r in enumerate(refs[:len(rows)])]
            outs = fn(*vals, *fixed)
            if not isinstance(outs, (tuple, list)):
                outs = (outs,)
            for i in range(nr):
                orefs[i][sec, :] = outs[i].astype(orefs[i].dtype)
            part = list(outs[nr:])
            sums = part if sums is None else [a + c for a, c in zip(sums, part)]
        for i in range(nb):
            ref = orefs[nr + i]

            @pl.when(s == 0)
            def _(ref=ref):
                ref[...] = jnp.zeros_like(ref)

            ref[...] += sums[i]
        for i in range(ng):
            ref = orefs[nr + nb + i]

            @pl.when((s == 0) & (b == 0))
            def _(ref=ref):
                ref[...] = jnp.zeros_like(ref)

            ref[...] += sums[nb + i]

    rows = [r if len(r) == 4 else r + (0,) for r in rows]
    in_specs = ([pl.BlockSpec((None, ts, cb), lambda b, s, ci=ci, b0=b0: (b0 + b, s, ci)) for (_, cb, ci, b0) in rows]
                + [pl.BlockSpec((None, 1, cb), lambda b, s, ci=ci: (b, 0, ci)) for (_, cb, ci) in bvecs]
                + [pl.BlockSpec(a.shape, lambda b, s: (0, 0)) for a in consts])
    out_shape = ([jax.ShapeDtypeStruct((B, S, c), dt) for (c, dt) in out_rows]
                 + [jax.ShapeDtypeStruct((B, 1, c), F32) for c in out_b]
                 + [jax.ShapeDtypeStruct(rc, F32) for rc in out_g])
    out_specs = ([pl.BlockSpec((None, ts, c), lambda b, s: (b, s, 0)) for (c, _) in out_rows]
                 + [pl.BlockSpec((None, 1, c), lambda b, s: (b, 0, 0)) for c in out_b]
                 + [pl.BlockSpec(rc, lambda b, s: (0, 0)) for rc in out_g])
    args = [r[0] for r in rows] + [a for (a, _, _) in bvecs] + list(consts)
    return pl.pallas_call(
        body, name=name, grid=(B, S // ts), in_specs=in_specs, out_specs=out_specs,
        out_shape=out_shape, compiler_params=_params(2),
    )(*args)


def _col_sum(v):
    return jnp.sum(v, axis=0, keepdims=True)


def _rms_scale(h):
    return lax.rsqrt(jnp.mean(h * h, axis=-1, keepdims=True) + EPS)


def _sigmoid(v):
    return 0.5 * (1.0 + jnp.tanh(0.5 * v))


ATT_SCALE = HEAD_DIM ** -0.5
COPY_ROWS = 256
_NT = (((1,), (1,)), ((), ()))
_TN = (((0,), (0,)), ((), ()))


def _row_chunks(d, seq):
    sub = seq // d
    out = []
    for r in range(d):
        for c0 in range(0, sub, COPY_ROWS):
            n = min(COPY_ROWS, sub - c0)
            out.append((pl.ds(r + c0 * d, n, stride=d), r * sub + c0, n))
    return out


ATT_UNROLL = 16
KEYS = 2 * WIN


def _zero_once(refs):
    @pl.when((pl.program_id(0) == 0) & (pl.program_id(1) == 0))
    def _():
        for r in refs:
            r[...] = jnp.zeros_like(r)


def _pair_bias(bias_ref, slopes_ref, hp, d, key_major):
    shape = (KEYS, WIN) if key_major else (WIN, KEYS)
    qi = lax.broadcasted_iota(jnp.int32, shape, 1 if key_major else 0)
    kj = lax.broadcasted_iota(jnp.int32, shape, 0 if key_major else 1)
    dist = WIN + qi - kj
    valid = (dist >= 0) & (dist <= WIN)
    distf = dist.astype(F32)
    for h in range(2):
        slope_d = slopes_ref[2 * hp + h] * float(d)
        with_prev = jnp.where(valid, -(slope_d * distf), NEG_INF)
        no_prev = jnp.where(kj >= WIN, with_prev, NEG_INF)
        span = slice(h * KEYS, (h + 1) * KEYS)
        if key_major:
            bias_ref[1, span, :] = with_prev
            bias_ref[0, span, :] = no_prev
        else:
            bias_ref[1, :, span] = with_prev
            bias_ref[0, :, span] = no_prev


def _stack_heads(v):
    first = lax.broadcasted_iota(jnp.int32, v.shape, 1) < HEAD_DIM
    zero = jnp.zeros_like(v)
    return jnp.concatenate([jnp.where(first, v, zero), jnp.where(first, zero, v)], axis=0)


def _per_head(c0, c1, n):
    return jnp.where(lax.broadcasted_iota(jnp.int32, (n, LANES), 1) < HEAD_DIM, c0, c1)


def _qkv_spec(seq, j):
    return pl.BlockSpec((None, seq, LANES), lambda b, hp: (b, 0, 3 * hp + j))


def _attention_fwd(qkv, slopes):
    B, S, _ = qkv.shape
    n_blk = S // WIN
    n_pair = N_HEADS // 2

    def body(slopes_ref, q_ref, k_ref, v_ref, o_ref, lse_ref, qp, kp, vp, bias, acc, mx, sm, acc_n, mx_n, sm_n):
        hp = pl.program_id(1)
        _zero_once((kp, vp))
        for p, d in enumerate(DILATIONS):
            nb = n_blk // d
            chunks = _row_chunks(d, S)
            for src, dst, n in chunks:
                qp[dst:dst + n, :] = (q_ref[src, :] * ATT_SCALE).astype(BF16)
                kp[WIN + dst:WIN + dst + n, :] = k_ref[src, :].astype(BF16)
                vp[WIN + dst:WIN + dst + n, :] = v_ref[src, :].astype(BF16)
            _pair_bias(bias, slopes_ref, hp, d, key_major=False)
            acc_t, mx_t, sm_t = (acc_n, mx_n, sm_n) if d == 1 else (acc, mx, sm)

            nk = WIN if nb == 1 else KEYS
            bias_cur = jnp.concatenate([bias[0, :, WIN:KEYS], bias[0, :, KEYS + WIN:]], axis=1) if nb == 1 else None

            def block(i, carry, p=p, nb=nb, nk=nk, bias_cur=bias_cur, acc_t=acc_t, mx_t=mx_t, sm_t=sm_t):
                cur = pl.ds(pl.multiple_of(i * WIN, WIN), WIN)
                keys = pl.ds(pl.multiple_of(i * WIN + (KEYS - nk), WIN), nk)
                s = lax.dot_general(qp[cur, :], _stack_heads(kp[keys, :]), _NT, preferred_element_type=F32)
                s = s + (bias_cur if nb == 1 else bias[((i % nb) > 0).astype(jnp.int32)])
                es, ms, ls = [], [], []
                for h in range(2):
                    sh = s[:, h * nk:(h + 1) * nk]
                    m = jnp.max(sh if nb == 1 else jnp.maximum(sh[:, :WIN], sh[:, WIN:]), axis=1, keepdims=True)
                    e = jnp.exp(sh - m)
                    es.append(e.astype(BF16))
                    ms.append(m)
                    ls.append(jnp.sum(e if nb == 1 else e[:, :WIN] + e[:, WIN:], axis=1, keepdims=True))
                acc_t[p, cur, :] = jnp.dot(jnp.concatenate(es, axis=1), _stack_heads(vp[keys, :]),
                                           preferred_element_type=F32)
                mx_t[p, cur, :] = _per_head(ms[0], ms[1], WIN)
                sm_t[p, cur, :] = _per_head(ls[0], ls[1], WIN)
                return carry

            lax.fori_loop(0, n_blk, block, 0, unroll=ATT_UNROLL)
            if d > 1:
                for src, dst, n in chunks:
                    acc_n[p, src, :] = acc[p, dst:dst + n, :]
                    mx_n[p, src, :] = mx[p, dst:dst + n, :]
                    sm_n[p, src, :] = sm[p, dst:dst + n, :]

        chunk = 256

        def merge(i, carry):
            rows = pl.ds(pl.multiple_of(i * chunk, chunk), chunk)
            ms = [mx_n[p, rows, :] for p in range(3)]
            m = jnp.maximum(jnp.maximum(ms[0], ms[1]), ms[2])
            ws = [jnp.exp(mp - m) for mp in ms]
            l = ws[0] * sm_n[0, rows, :] + ws[1] * sm_n[1, rows, :] + ws[2] * sm_n[2, rows, :]
            o = (ws[0] * acc_n[0, rows, :] + ws[1] * acc_n[1, rows, :] + ws[2] * acc_n[2, rows, :]) / l
            o_ref[rows, :] = o.astype(o_ref.dtype)
            lse = m + jnp.log(l)
            for h in range(2):
                lse_ref[rows, h:h + 1] = lse[:, h * HEAD_DIM:h * HEAD_DIM + 1]
            return carry

        lax.fori_loop(0, S // chunk, merge, 0)

    return pl.pallas_call(
        body, name="attention_fwd", grid=(B, n_pair),
        in_specs=[pl.BlockSpec(memory_space=pltpu.SMEM), _qkv_spec(S, 0), _qkv_spec(S, 1), _qkv_spec(S, 2)],
        out_specs=[pl.BlockSpec((None, S, LANES), lambda b, hp: (b, 0, hp)),
                   pl.BlockSpec((None, None, S, 2), lambda b, hp: (b, hp, 0, 0))],
        out_shape=[jax.ShapeDtypeStruct((B, S, ATT_WIDTH), BF16),
                   jax.ShapeDtypeStruct((B, n_pair, S, 2), F32)],
        scratch_shapes=[pltpu.VMEM((S, LANES), BF16), pltpu.VMEM((S + WIN, LANES), BF16),
                        pltpu.VMEM((S + WIN, LANES), BF16), pltpu.VMEM((2, WIN, 2 * KEYS), F32)]
        + [pltpu.VMEM((3, S, LANES), F32)] * 6,
        compiler_params=_params(2),
    )(slopes, qkv, qkv, qkv)


def _attention_bwd(qkv, o, do, lse, slopes):
    B, S, _ = qkv.shape
    n_blk = S // WIN
    n_pair = N_HEADS // 2

    def body(slopes_ref, q_ref, k_ref, v_ref, o_ref, do_ref, lse_ref, dx_ref,
             qp, dop, kp, vp, aux, auxp, aux_t, bias_t, dqp, dvk, dq_n, dk_n, dv_n):
        hp = pl.program_id(1)
        aux[...] = jnp.zeros_like(aux)
        for c0 in range(0, S, COPY_ROWS):
            rows = slice(c0, c0 + COPY_ROWS)
            prod = do_ref[rows, :] * o_ref[rows, :].astype(F32)
            for h in range(2):
                aux[rows, 2 * h:2 * h + 1] = lse_ref[rows, h:h + 1]
                aux[rows, 2 * h + 1:2 * h + 2] = jnp.sum(prod[:, h * HEAD_DIM:(h + 1) * HEAD_DIM], axis=1,
                                                         keepdims=True)
        dq_n[...] = jnp.zeros_like(dq_n)
        dk_n[...] = jnp.zeros_like(dk_n)
        dv_n[...] = jnp.zeros_like(dv_n)
        _zero_once((kp, vp))
        for p, d in enumerate(DILATIONS):
            nb = n_blk // d
            chunks = _row_chunks(d, S)
            for src, dst, n in chunks:
                auxp[dst:dst + n, :] = aux[src, :]
                qp[dst:dst + n, :] = (q_ref[src, :] * ATT_SCALE).astype(BF16)
                dop[dst:dst + n, :] = do_ref[src, :].astype(BF16)
                kp[WIN + dst:WIN + dst + n, :] = k_ref[src, :].astype(BF16)
                vp[WIN + dst:WIN + dst + n, :] = v_ref[src, :].astype(BF16)
            for i in range(n_blk):
                aux_t[i] = auxp[i * WIN:(i + 1) * WIN, :].T[0:8, :]
            _pair_bias(bias_t, slopes_ref, hp, d, key_major=True)
            dvk[...] = jnp.zeros_like(dvk)

            nk = WIN if nb == 1 else KEYS
            bias_cur = jnp.concatenate([bias_t[0, WIN:KEYS, :], bias_t[0, KEYS + WIN:, :]], axis=0) if nb == 1 else None

            def block(i, carry, nb=nb, nk=nk, bias_cur=bias_cur):
                cur = pl.ds(pl.multiple_of(i * WIN, WIN), WIN)
                keys = pl.ds(pl.multiple_of(i * WIN + (KEYS - nk), WIN), nk)
                q2, do2 = qp[cur, :], dop[cur, :]
                kc = _stack_heads(kp[keys, :])
                s_t = lax.dot_general(kc, q2, _NT, preferred_element_type=F32)
                s_t = s_t + (bias_cur if nb == 1 else bias_t[((i % nb) > 0).astype(jnp.int32)])
                dp_t = lax.dot_general(_stack_heads(vp[keys, :]), do2, _NT, preferred_element_type=F32)
                ps, dss = [], []
                for h in range(2):
                    span = slice(h * nk, (h + 1) * nk)
                    p_t = jnp.exp(s_t[span, :] - aux_t[i, 2 * h:2 * h + 1, :])
                    ds_t = p_t * (dp_t[span, :] - aux_t[i, 2 * h + 1:2 * h + 2, :])
                    ps.append(p_t.astype(BF16))
                    dss.append(ds_t.astype(BF16))
                do_rows, q_rows = _stack_heads(do2), _stack_heads(q2)
                zr = jnp.zeros_like(do_rows)
                rhs = jnp.concatenate([jnp.concatenate([do_rows, zr], axis=1),
                                       jnp.concatenate([zr, q_rows], axis=1)], axis=0)
                dvk[keys, :] += jnp.dot(jnp.concatenate(ps + dss, axis=1), rhs, preferred_element_type=F32)
                dqp[cur, :] = lax.dot_general(jnp.concatenate(dss, axis=0), kc, _TN, preferred_element_type=F32)
                return carry

            lax.fori_loop(0, n_blk, block, 0, unroll=ATT_UNROLL)
            for src, dst, n in chunks:
                dq_n[src, :] += dqp[dst:dst + n, :]
                dv_n[src, :] += dvk[WIN + dst:WIN + dst + n, :LANES]
                dk_n[src, :] += dvk[WIN + dst:WIN + dst + n, LANES:]
        for c0 in range(0, S, COPY_ROWS):
            rows = slice(c0, c0 + COPY_ROWS)
            dx_ref[rows, 0:LANES] = (dq_n[rows, :] * ATT_SCALE).astype(dx_ref.dtype)
            dx_ref[rows, LANES:2 * LANES] = dk_n[rows, :].astype(dx_ref.dtype)
            dx_ref[rows, 2 * LANES:3 * LANES] = dv_n[rows, :].astype(dx_ref.dtype)

    pair = lambda width: pl.BlockSpec((None, S, width), lambda b, hp: (b, 0, hp))
    vm = lambda shape, dt: pltpu.VMEM(shape, dt)
    return pl.pallas_call(
        body, name="attention_bwd", grid=(B, n_pair),
        in_specs=[pl.BlockSpec(memory_space=pltpu.SMEM), _qkv_spec(S, 0), _qkv_spec(S, 1), _qkv_spec(S, 2),
                  pair(LANES), pair(LANES), pl.BlockSpec((None, None, S, 2), lambda b, hp: (b, hp, 0, 0))],
        out_specs=pair(3 * LANES),
        out_shape=jax.ShapeDtypeStruct((B, S, 3 * ATT_WIDTH), BF16),
        scratch_shapes=[vm((S, LANES), BF16), vm((S, LANES), BF16),
                        vm((S + WIN, LANES), BF16), vm((S + WIN, LANES), BF16),
                        vm((S, LANES), F32), vm((S, LANES), F32), vm((n_blk, 8, WIN), F32),
                        vm((2, 2 * KEYS, WIN), F32),
                        vm((S, LANES), F32), vm((S + WIN, 2 * LANES), F32),
                        vm((S, LANES), F32), vm((S, LANES), F32), vm((S, LANES), F32)],
        compiler_params=_params(2),
    )(slopes, qkv, qkv, qkv, o, do, lse)


SCAN_COLS = 256
SCAN_ROWS = 8


def _rows_to_tile(rows):
    rid = lax.broadcasted_iota(jnp.int32, (SCAN_ROWS, rows[0].shape[1]), 0)
    tile = jnp.broadcast_to(rows[0], rid.shape)
    for k in range(1, SCAN_ROWS):
        tile = jnp.where(rid == k, rows[k], tile)
    return tile


SCAN_UNROLL = 4


def _complex_powers(ar, ai, n):
    out = [(ar, ai)]
    for _ in range(n - 1):
        pr, pi = out[-1]
        out.append((pr * ar - pi * ai, pr * ai + pi * ar))
    return out


def _round_multipliers(powers, rid, reverse):
    out = []
    for s in (1, 2, 4):
        keep = (rid < SCAN_ROWS - s) if reverse else (rid >= s)
        out.append((jnp.where(keep, powers[s - 1][0], 0.0), jnp.where(keep, powers[s - 1][1], 0.0)))
    return out


def _tile_scan(xr, xi, multipliers, reverse):
    for s, (mr, mi) in zip((1, 2, 4), multipliers):
        shift = SCAN_ROWS - s if reverse else s
        sr, si = pltpu.roll(xr, shift, 0), pltpu.roll(xi, shift, 0)
        xr, xi = xr + (mr * sr - mi * si), xi + (mr * si + mi * sr)
    return xr, xi


SCAN_CHUNK = 256


def _scan_fwd(us, bb_big, a_row, cc_big):
    B, S, _ = us.shape
    groups = 2
    width = 2 * groups * SCAN_COLS
    nc = 2 * SSM_COLS // width
    nt = S // SCAN_ROWS
    tiles = SCAN_CHUNK // SCAN_ROWS
    LAST = slice(SCAN_ROWS - 1, SCAN_ROWS)

    def body(us_ref, bb_ref, a_ref, cc_ref, xs_ref, y_ref, bu_ref):
        bb = bb_ref[...].astype(BF16)
        for c in range(S // SCAN_CHUNK):
            part = jnp.dot(us_ref[c * SCAN_CHUNK:(c + 1) * SCAN_CHUNK, :].astype(BF16), bb,
                           preferred_element_type=F32)
            bu_ref[c * tiles:(c + 1) * tiles] = part.reshape(tiles, SCAN_ROWS, width)
        rid = lax.broadcasted_iota(jnp.int32, (SCAN_ROWS, SCAN_COLS), 0)
        consts = []
        for g in range(groups):
            re = slice(2 * g * SCAN_COLS, (2 * g + 1) * SCAN_COLS)
            im = slice((2 * g + 1) * SCAN_COLS, (2 * g + 2) * SCAN_COLS)
            powers = _complex_powers(a_ref[:, re], a_ref[:, im], SCAN_ROWS)
            carry_mult = (_rows_to_tile([p[0] for p in powers]), _rows_to_tile([p[1] for p in powers]))
            consts.append((re, im, carry_mult, _round_multipliers(powers, rid, reverse=False)))

        def tile(i, carry):
            out = []
            for (re, im, (cr_t, ci_t), rounds), (cr, ci) in zip(consts, carry):
                xr, xi = _tile_scan(bu_ref[i, :, re], bu_ref[i, :, im], rounds, reverse=False)
                xs_ref[i, :, re] = xr + (cr_t * cr - ci_t * ci)
                xs_ref[i, :, im] = xi + (cr_t * ci + ci_t * cr)
                out.append((xs_ref[i, LAST, re], xs_ref[i, LAST, im]))
            return tuple(out)

        zero = jnp.zeros((1, SCAN_COLS), F32)
        lax.fori_loop(0, nt, tile, ((zero, zero),) * groups, unroll=SCAN_UNROLL)

        @pl.when(pl.program_id(1) == 0)
        def _():
            y_ref[...] = jnp.zeros_like(y_ref)

        cc = cc_ref[...].astype(BF16)
        for c in range(S // SCAN_CHUNK):
            x2 = xs_ref[c * tiles:(c + 1) * tiles].reshape(SCAN_CHUNK, width).astype(BF16)
            y_ref[c * SCAN_CHUNK:(c + 1) * SCAN_CHUNK, :] += jnp.dot(x2, cc, preferred_element_type=F32)

    col = pl.BlockSpec((None, nt, SCAN_ROWS, width), lambda b, j: (b, 0, 0, j))
    tok = pl.BlockSpec((None, S, SSM_WIDTH), lambda b, j: (b, 0, 0))
    xs, y = pl.pallas_call(
        body, name="s5_scan_fwd", grid=(B, nc),
        in_specs=[tok, pl.BlockSpec((SSM_WIDTH, width), lambda b, j: (0, j)),
                  pl.BlockSpec((1, width), lambda b, j: (0, j)), pl.BlockSpec((width, SSM_WIDTH), lambda b, j: (j, 0))],
        out_specs=[col, tok],
        out_shape=[jax.ShapeDtypeStruct((B, nt, SCAN_ROWS, 2 * SSM_COLS), F32),
                   jax.ShapeDtypeStruct((B, S, SSM_WIDTH), F32)],
        scratch_shapes=[pltpu.VMEM((nt, SCAN_ROWS, width), F32)],
        compiler_params=_params(2),
    )(us, bb_big, a_row, cc_big)
    return xs.reshape(B, S, 2 * SSM_COLS), y


def _scan_bwd(dy, us, bb_big, cc_big, xs, a_row):
    B, S, _ = dy.shape
    width = 2 * SCAN_COLS
    nc = SSM_COLS // SCAN_COLS
    nt = S // SCAN_ROWS
    tiles = SCAN_CHUNK // SCAN_ROWS
    RE, IM = slice(0, SCAN_COLS), slice(SCAN_COLS, 2 * SCAN_COLS)
    FIRST, LAST = slice(0, 1), slice(SCAN_ROWS - 1, SCAN_ROWS)

    def body(dy_ref, us_ref, bb_ref, cc_ref, x_ref, a_ref, dus_ref, ga_ref, dbb_ref, dcc_ref, d_ref, lam_ref):
        b = pl.program_id(1)
        cc = cc_ref[...].astype(BF16)
        for c in range(S // SCAN_CHUNK):
            part = lax.dot_general(dy_ref[c * SCAN_CHUNK:(c + 1) * SCAN_CHUNK, :].astype(BF16), cc, _NT,
                                   preferred_element_type=F32)
            d_ref[c * tiles:(c + 1) * tiles] = part.reshape(tiles, SCAN_ROWS, width)
        powers = _complex_powers(a_ref[:, RE], -a_ref[:, IM], SCAN_ROWS)
        rid = lax.broadcasted_iota(jnp.int32, (SCAN_ROWS, SCAN_COLS), 0)
        cr_t = _rows_to_tile([powers[SCAN_ROWS - 1 - r][0] for r in range(SCAN_ROWS)])
        ci_t = _rows_to_tile([powers[SCAN_ROWS - 1 - r][1] for r in range(SCAN_ROWS)])
        rounds = _round_multipliers(powers, rid, reverse=True)

        @pl.when(b == 0)
        def _():
            ga_ref[...] = jnp.zeros_like(ga_ref)
            dbb_ref[...] = jnp.zeros_like(dbb_ref)
            dcc_ref[...] = jnp.zeros_like(dcc_ref)

        def tile(j, carry):
            cr, ci, accr, acci = carry
            i = nt - 1 - j
            lr, li = _tile_scan(d_ref[i, :, RE], d_ref[i, :, IM], rounds, reverse=True)
            lam_r = lr + (cr_t * cr - ci_t * ci)
            lam_i = li + (cr_t * ci + ci_t * cr)
            lam_ref[i, :, RE] = lam_r
            lam_ref[i, :, IM] = lam_i
            ip = jnp.maximum(i - 1, 0)
            keep = (i > 0).astype(F32)
            xpr = jnp.where(rid == 0, x_ref[ip, LAST, RE] * keep, pltpu.roll(x_ref[i, :, RE], 1, 0))
            xpi = jnp.where(rid == 0, x_ref[ip, LAST, IM] * keep, pltpu.roll(x_ref[i, :, IM], 1, 0))
            accr = accr + lam_r * xpr + lam_i * xpi
            acci = acci + lam_i * xpr - lam_r * xpi
            return lam_ref[i, FIRST, RE], lam_ref[i, FIRST, IM], accr, acci

        z1 = jnp.zeros((1, SCAN_COLS), F32)
        z8 = jnp.zeros((SCAN_ROWS, SCAN_COLS), F32)
        _, _, accr, acci = lax.fori_loop(0, nt, tile, (z1, z1, z8, z8), unroll=SCAN_UNROLL)
        ga_ref[:, RE] += _col_sum(accr)
        ga_ref[:, IM] += _col_sum(acci)

        bb = bb_ref[...].astype(BF16)
        for c in range(S // SCAN_CHUNK):
            rows = slice(c * SCAN_CHUNK, (c + 1) * SCAN_CHUNK)
            lam2 = lam_ref[c * tiles:(c + 1) * tiles].reshape(SCAN_CHUNK, width).astype(BF16)
            x2 = x_ref[c * tiles:(c + 1) * tiles].reshape(SCAN_CHUNK, width).astype(BF16)
            dus_ref[rows, :] = lax.dot_general(lam2, bb, _NT, preferred_element_type=F32)
            dbb_ref[...] += lax.dot_general(us_ref[rows, :].astype(BF16), lam2, _TN, preferred_element_type=F32)
            dcc_ref[...] += lax.dot_general(x2, dy_ref[rows, :].astype(BF16), _TN, preferred_element_type=F32)

    col = pl.BlockSpec((None, nt, SCAN_ROWS, width), lambda j, b: (b, 0, 0, j))
    tok = pl.BlockSpec((None, S, SSM_WIDTH), lambda j, b: (b, 0, 0))
    scratch = pltpu.VMEM((nt, SCAN_ROWS, width), F32)
    return pl.pallas_call(
        body, name="s5_scan_bwd", grid=(nc, B),
        in_specs=[tok, tok, pl.BlockSpec((SSM_WIDTH, width), lambda j, b: (0, j)),
                  pl.BlockSpec((width, SSM_WIDTH), lambda j, b: (j, 0)), col,
                  pl.BlockSpec((1, width), lambda j, b: (0, j))],
        out_specs=[pl.BlockSpec((None, None, S, SSM_WIDTH), lambda j, b: (j, b, 0, 0)),
                   pl.BlockSpec((1, width), lambda j, b: (0, j)),
                   pl.BlockSpec((SSM_WIDTH, width), lambda j, b: (0, j)),
                   pl.BlockSpec((width, SSM_WIDTH), lambda j, b: (j, 0))],
        out_shape=[jax.ShapeDtypeStruct((nc, B, S, SSM_WIDTH), F32), jax.ShapeDtypeStruct((1, 2 * SSM_COLS), F32),
                   jax.ShapeDtypeStruct((SSM_WIDTH, 2 * SSM_COLS), F32),
                   jax.ShapeDtypeStruct((2 * SSM_COLS, SSM_WIDTH), F32)],
        scratch_shapes=[scratch, scratch],
        compiler_params=_params(2),
    )(dy, us, bb_big, cc_big, xs.reshape(B, nt, SCAN_ROWS, 2 * SSM_COLS), a_row)


def _s5_discretise(lr, li, log_dt):
    dt = jnp.exp(log_dt)
    mag = jnp.exp(lr * dt)
    ang = li * dt
    ab_re, ab_im = mag * jnp.cos(ang), mag * jnp.sin(ang)
    nr, ni = ab_re - 1.0, ab_im
    den = lr * lr + li * li
    f_re = (nr * lr + ni * li) / den
    f_im = (ni * lr - nr * li) / den
    return dt, ab_re, ab_im, nr, ni, den, f_re, f_im


def _s5_params(a_re, a_im, log_dt):
    def body(lr_ref, li_ref, ld_ref, abr, abi, fr, fi):
        _, ab_re, ab_im, _, _, _, f_re, f_im = _s5_discretise(lr_ref[...], li_ref[...], ld_ref[...])
        abr[...] = ab_re
        abi[...] = ab_im
        fr[...] = f_re
        fi[...] = f_im

    return pl.pallas_call(body, name="s5_params",
                          out_shape=[jax.ShapeDtypeStruct(a_re.shape, F32)] * 4)(a_re, a_im, log_dt)


def _s5_input_matrix(f_re, f_im, b_re, b_im):
    def body(fr, fi, br, bi, o_re, o_im):
        o_re[...] = fr[...] * br[...] - fi[...] * bi[...]
        o_im[...] = fr[...] * bi[...] + fi[...] * br[...]

    return pl.pallas_call(body, name="s5_input_matrix",
                          out_shape=[jax.ShapeDtypeStruct(b_re.shape, F32)] * 2)(f_re, f_im, b_re, b_im)


def _s5_input_matrix_bwd(f_re, f_im, b_re, b_im, g_re, g_im):
    def body(fr, fi, br, bi, gr, gi, dbr, dbi, dfr, dfi):
        dbr[...] = fr[...] * gr[...] + fi[...] * gi[...]
        dbi[...] = fr[...] * gi[...] - fi[...] * gr[...]
        dfr[...] = jnp.sum(br[...] * gr[...] + bi[...] * gi[...], axis=1, keepdims=True)
        dfi[...] = jnp.sum(br[...] * gi[...] - bi[...] * gr[...], axis=1, keepdims=True)

    return pl.pallas_call(
        body, name="s5_input_matrix_bwd",
        out_shape=[jax.ShapeDtypeStruct(b_re.shape, F32)] * 2 + [jax.ShapeDtypeStruct(f_re.shape, F32)] * 2,
    )(f_re, f_im, b_re, b_im, g_re, g_im)


def _s5_params_bwd(a_re, a_im, log_dt, g_ab_re, g_ab_im, d_f_re, d_f_im):
    def body(lr_ref, li_ref, ld_ref, gar, gai, dfr, dfi, o_lr, o_li, o_ld):
        lr, li = lr_ref[...], li_ref[...]
        dt, ab_re, ab_im, nr, ni, den, f_re, f_im = _s5_discretise(lr, li, ld_ref[...])
        d_fr, d_fi = dfr[...], dfi[...]
        d_nr = (d_fr * lr - d_fi * li) / den
        d_ni = (d_fr * li + d_fi * lr) / den
        common = (d_fr * f_re + d_fi * f_im) * 2.0 / den
        d_lr = (d_fr * nr + d_fi * ni) / den - common * lr
        d_li = (d_fr * ni - d_fi * nr) / den - common * li
        d_abr = gar[...] + d_nr
        d_abi = gai[...] + d_ni
        d_mag_mag = d_abr * ab_re + d_abi * ab_im
        d_ang = d_abi * ab_re - d_abr * ab_im
        o_lr[...] = d_lr + d_mag_mag * dt
        o_li[...] = d_li + d_ang * dt
        o_ld[...] = jnp.sum(d_mag_mag * lr + d_ang * li, axis=1, keepdims=True) * dt

    return pl.pallas_call(
        body, name="s5_params_bwd",
        out_shape=[jax.ShapeDtypeStruct(a_re.shape, F32)] * 2 + [jax.ShapeDtypeStruct(log_dt.shape, F32)],
    )(a_re, a_im, log_dt, g_ab_re, g_ab_im, d_f_re, d_f_im)


CONV_COLS = 256


def _shift_down(v, j, row):
    return jnp.where(row >= j, pltpu.roll(v, j, 0), 0.0)


def _shift_up(v, j, row, seq):
    return jnp.where(row < seq - j, pltpu.roll(v, seq - j, 0), 0.0)


def _up_conv_fwd(u, w3, w_conv, b_conv):
    B, S, K = u.shape

    def body(u_ref, w_ref, wc_ref, bc_ref, up_ref, ff_ref):
        uv = u_ref[...]
        row = lax.broadcasted_iota(jnp.int32, (S, CONV_COLS), 0)
        for half in range(2):
            cols = slice(half * HALF, (half + 1) * HALF)
            pair = jnp.dot(uv, jnp.concatenate([w_ref[0, :, cols], w_ref[1, :, cols]], axis=1),
                           preferred_element_type=F32)
            up_ref[:, half * 2 * HALF:(half + 1) * 2 * HALF] = pair.astype(up_ref.dtype)
            a, val = pair[:, :HALF], pair[:, HALF:]
            conv = (bc_ref[:, cols] + wc_ref[0:1, cols] * a + wc_ref[1:2, cols] * _shift_down(a, 1, row)
                    + wc_ref[2:3, cols] * _shift_down(a, 2, row))
            ff_ref[:, cols] = (conv * _sigmoid(conv) * val).astype(ff_ref.dtype)

    return pl.pallas_call(
        body, name="ffn_up_conv_gate", grid=(UP_SLOTS, B),
        in_specs=[pl.BlockSpec((None, S, K), lambda j, b: (b, 0, 0)), _up_weight_spec(K, lambda j, b: (0, j, 0, 0)),
                  pl.BlockSpec((3, 2 * HALF), lambda j, b: (0, j)), pl.BlockSpec((1, 2 * HALF), lambda j, b: (0, j))],
        out_specs=[pl.BlockSpec((None, S, UP_GROUP), lambda j, b: (b, 0, j)),
                   pl.BlockSpec((None, S, 2 * HALF), lambda j, b: (b, 0, j))],
        out_shape=[jax.ShapeDtypeStruct((B, S, UP_SLOTS * UP_GROUP), BF16), jax.ShapeDtypeStruct((B, S, D_FF), BF16)],
        compiler_params=_params(2),
    )(u, w3.reshape(2, UP_SLOTS, K, 2 * HALF), w_conv, b_conv)


def _conv_bwd(up, d_down, w_down, w_conv, b_conv):
    B, S, _ = up.shape
    nj = D_FF // CONV_COLS

    def body(up_ref, dd_ref, wd_ref, w_ref, b_ref, dup_ref, dw_ref, db_ref):
        b = pl.program_id(1)
        a = up_ref[:, :CONV_COLS].astype(F32)
        val = up_ref[:, CONV_COLS:].astype(F32)
        row = lax.broadcasted_iota(jnp.int32, a.shape, 0)
        w0, w1, w2 = w_ref[0:1, :], w_ref[1:2, :], w_ref[2:3, :]
        a1, a2 = _shift_down(a, 1, row), _shift_down(a, 2, row)
        conv = b_ref[...] + w0 * a + w1 * a1 + w2 * a2
        sg = _sigmoid(conv)
        dff = lax.dot_general(dd_ref[...], wd_ref[...], _NT, preferred_element_type=F32)
        d_val = dff * conv * sg
        dc = dff * val * (sg * (1.0 + conv * (1.0 - sg)))
        d_a = w0 * dc + w1 * _shift_up(dc, 1, row, S) + w2 * _shift_up(dc, 2, row, S)
        dup_ref[:, :CONV_COLS] = d_a.astype(dup_ref.dtype)
        dup_ref[:, CONV_COLS:] = d_val.astype(dup_ref.dtype)

        @pl.when(b == 0)
        def _():
            dw_ref[...] = jnp.zeros_like(dw_ref)
            db_ref[...] = jnp.zeros_like(db_ref)

        dw_ref[0:1, :] += _col_sum(dc * a)
        dw_ref[1:2, :] += _col_sum(dc * a1)
        dw_ref[2:3, :] += _col_sum(dc * a2)
        db_ref[...] += _col_sum(dc)

    return pl.pallas_call(
        body, name="conv_gate_bwd", grid=(nj, B),
        in_specs=[pl.BlockSpec((None, S, 2 * CONV_COLS), lambda j, b: (b, 0, j)),
                  pl.BlockSpec((None, S, D_MODEL), lambda j, b: (b, 0, 0)),
                  pl.BlockSpec((CONV_COLS, D_MODEL), lambda j, b: (j, 0)),
                  pl.BlockSpec((3, CONV_COLS), lambda j, b: (0, j)),
                  pl.BlockSpec((1, CONV_COLS), lambda j, b: (0, j))],
        out_specs=[pl.BlockSpec((None, S, 2 * CONV_COLS), lambda j, b: (b, 0, j)),
                   pl.BlockSpec((3, CONV_COLS), lambda j, b: (0, j)),
                   pl.BlockSpec((1, CONV_COLS), lambda j, b: (0, j))],
        out_shape=[jax.ShapeDtypeStruct((B, S, 2 * D_FF), BF16), jax.ShapeDtypeStruct((3, D_FF), F32),
                   jax.ShapeDtypeStruct((1, D_FF), F32)],
        compiler_params=_params(2),
    )(up, d_down, w_down, w_conv, b_conv)


def _ada_fwd(c_all, w_ada, b_ada):
    def body(c_ref, w_ref, b_ref, o_ref):
        cv = c_ref[...]
        act = (cv * _sigmoid(cv)).astype(BF16)
        o_ref[...] = jnp.dot(act, w_ref[...].astype(BF16), preferred_element_type=F32) + b_ref[...]

    return pl.pallas_call(body, name="ada_fwd",
                          out_shape=jax.ShapeDtypeStruct((c_all.shape[0], w_ada.shape[1]), F32),
                          compiler_params=pltpu.CompilerParams(vmem_limit_bytes=V7X_VMEM_LIMIT))(c_all, w_ada, b_ada)


def _ada_bwd(c_all, dmod_all, dmod_cols):
    def body(c_ref, dm_ref, dmc_ref, dw_ref, db_ref):
        cv = c_ref[...]
        act = (cv * _sigmoid(cv)).astype(BF16)
        dw_ref[...] = lax.dot_general(act, dmc_ref[...].astype(BF16), _TN, preferred_element_type=F32)
        db_ref[...] = _col_sum(dm_ref[...])

    return pl.pallas_call(
        body, name="ada_bwd",
        out_shape=[jax.ShapeDtypeStruct((c_all.shape[1], dmod_cols.shape[1]), F32),
                   jax.ShapeDtypeStruct((1, dmod_all.shape[1]), F32)],
        compiler_params=pltpu.CompilerParams(vmem_limit_bytes=V7X_VMEM_LIMIT))(c_all, dmod_all, dmod_cols)


def _adamw(w, m, v, g_parts, name, own=None):
    R, C = w.shape
    P = g_parts.shape[0]
    tr = R
    for cand in (256, 128, 64, 32, 16, 8):
        if R % cand == 0 and cand * C * 4 * (P + 8) * 2 <= V7X_VMEM_LIMIT // 2:
            tr = cand
            break
    c1 = 1.0 / (1.0 - ADAM_B1 ** ADAM_STEP)
    c2 = 1.0 / (1.0 - ADAM_B2 ** ADAM_STEP)

    def update(w_ref, m_ref, v_ref, g, og, od, om, ov):
        m_new = ADAM_B1 * m_ref[...] + (1.0 - ADAM_B1) * g
        v_new = ADAM_B2 * v_ref[...] + (1.0 - ADAM_B2) * (g * g)
        og[...] = g
        om[...] = m_new
        ov[...] = v_new
        od[...] = -ADAM_LR * ((m_new * c1) / (jnp.sqrt(v_new * c2) + ADAM_EPS) + ADAM_WD * w_ref[...])

    def total(g_ref):
        g = g_ref[0].astype(F32)
        for p in range(1, P):
            g = g + g_ref[p].astype(F32)
        return g

    out_shape = [jax.ShapeDtypeStruct((R, C), F32)] * 4
    if own is None:
        def body(w_ref, m_ref, v_ref, g_ref, og, od, om, ov):
            update(w_ref, m_ref, v_ref, total(g_ref), og, od, om, ov)

        spec = pl.BlockSpec((tr, C), lambda i: (i, 0))
        return pl.pallas_call(
            body, name=name, grid=(R // tr,),
            in_specs=[spec, spec, spec, pl.BlockSpec((P, tr, C), lambda i: (0, i, 0))],
            out_specs=[spec] * 4, out_shape=out_shape, compiler_params=_params(1),
        )(w, m, v, g_parts)

    slots, me = own

    def body_own(me_ref, w_ref, m_ref, v_ref, g_ref, own_ref, og, od, om, ov):
        g = own_ref[...].astype(F32)
        for p in range(P):
            g = g + jnp.where(me_ref[0] == p, 0.0, g_ref[p].astype(F32))
        update(w_ref, m_ref, v_ref, g, og, od, om, ov)

    spec = pl.BlockSpec((tr, C), lambda i, me_ref: (i, 0))
    grid_spec = pltpu.PrefetchScalarGridSpec(
        num_scalar_prefetch=1, grid=(R // tr,),
        in_specs=[spec, spec, spec, pl.BlockSpec((P, tr, C), lambda i, me_ref: (0, i, 0)),
                  pl.BlockSpec((None, tr, C), lambda i, me_ref: (me_ref[0], i, 0))],
        out_specs=[spec] * 4)
    return pl.pallas_call(body_own, name=name, grid_spec=grid_spec, out_shape=out_shape,
                          compiler_params=_params(1))(me, w, m, v, g_parts, slots)


def _adamw_small(ws, ms, vs, gs):
    n = len(ws)
    c1 = 1.0 / (1.0 - ADAM_B1 ** ADAM_STEP)
    c2 = 1.0 / (1.0 - ADAM_B2 ** ADAM_STEP)

    def body(*refs):
        ins, outs = refs[:4 * n], refs[4 * n:]
        for i in range(n):
            w, m, v, g = ins[i][...], ins[n + i][...], ins[2 * n + i][...], ins[3 * n + i][...]
            m_new = ADAM_B1 * m + (1.0 - ADAM_B1) * g
            v_new = ADAM_B2 * v + (1.0 - ADAM_B2) * (g * g)
            outs[4 * i][...] = g
            outs[4 * i + 1][...] = -ADAM_LR * ((m_new * c1) / (jnp.sqrt(v_new * c2) + ADAM_EPS) + ADAM_WD * w)
            outs[4 * i + 2][...] = m_new
            outs[4 * i + 3][...] = v_new

    out_shape = [jax.ShapeDtypeStruct(w.shape, F32) for w in ws for _ in range(4)]
    return pl.pallas_call(body, name="adamw_small", out_shape=out_shape,
                          compiler_params=pltpu.CompilerParams(vmem_limit_bytes=V7X_VMEM_LIMIT))(*ws, *ms, *vs, *gs)


def _sum_parts(parts, loss_rows):
    P, R, C = parts.shape
    lo, hi = loss_rows

    def body(p_ref, o_ref, loss_ref):
        t = p_ref[0]
        for p in range(1, P):
            t = t + p_ref[p]
        o_ref[...] = t
        tot = jnp.sum(jnp.sum(o_ref[lo:hi, :], axis=1, keepdims=True), axis=0, keepdims=True)
        loss_ref[...] = jnp.broadcast_to(tot, loss_ref.shape)

    return pl.pallas_call(body, name="sum_small_grads",
                          out_shape=[jax.ShapeDtypeStruct((R, C), F32), jax.ShapeDtypeStruct((1, LANES), F32)],
                          compiler_params=pltpu.CompilerParams(vmem_limit_bytes=V7X_VMEM_LIMIT))(parts)


def _exchange(items, name):
    n = len(items)
    MESH = pl.DeviceIdType.MESH

    def body(*refs):
        src, dst = refs[:n], refs[n:2 * n]
        send_sems, recv_sems, local_sems = refs[2 * n:]
        x, y, c = lax.axis_index("x"), lax.axis_index("y"), lax.axis_index("c")
        me = 4 * x + 2 * y + c
        started = []
        for it, (_, per_peer) in enumerate(items):
            own = pltpu.make_async_copy(src[it].at[me] if per_peer else src[it], dst[it].at[me], local_sems.at[it])
            own.start()
            started.append(own)
        sends, recvs = [], []
        for k in range(1, N_DEV):
            px = 1 - x if k & 4 else x
            py = 1 - y if k & 2 else y
            pc = 1 - c if k & 1 else c
            peer = 4 * px + 2 * py + pc
            for it, (_, per_peer) in enumerate(items):
                s = src[it].at[peer] if per_peer else src[it]
                cp = pltpu.make_async_remote_copy(src_ref=s, dst_ref=dst[it].at[me], send_sem=send_sems.at[it, k - 1],
                                                  recv_sem=recv_sems.at[it, k - 1], device_id=(px, py, pc),
                                                  device_id_type=MESH)
                cp.start()
                sends.append(cp)
                recvs.append(pltpu.make_async_remote_copy(
                    src_ref=s, dst_ref=dst[it].at[peer], send_sem=send_sems.at[it, k - 1],
                    recv_sem=recv_sems.at[it, k - 1], device_id=(px, py, pc), device_id_type=MESH))
        for cp in recvs:
            cp.wait_recv()
        for cp in sends:
            cp.wait_send()
        for cp in started:
            cp.wait()

    any_spec = pl.BlockSpec(memory_space=pl.ANY)
    out_shape = []
    for a, per_peer in items:
        shp = a.shape if per_peer else (N_DEV,) + a.shape
        out_shape.append(jax.ShapeDtypeStruct(shp, a.dtype))
    return pl.pallas_call(
        body, name=name, in_specs=[any_spec] * n, out_specs=[any_spec] * n, out_shape=out_shape,
        scratch_shapes=[pltpu.SemaphoreType.DMA((n, N_DEV - 1)), pltpu.SemaphoreType.DMA((n, N_DEV - 1)),
                        pltpu.SemaphoreType.DMA((n,))],
    )(*[a for a, _ in items])


def _remote(src, dst, send_sem, recv_sem, device):
    return pltpu.make_async_remote_copy(src_ref=src, dst_ref=dst, send_sem=send_sem, recv_sem=recv_sem,
                                        device_id=device, device_id_type=pl.DeviceIdType.MESH)


def _mesh_place():
    x, y, c = lax.axis_index("x"), lax.axis_index("y"), lax.axis_index("c")
    other_chips = [(1 - x, y), (x, 1 - y), (1 - x, 1 - y)]
    return x, y, c, (x, y, 1 - c), other_chips


def _gather_all(items, name):
    n = len(items)

    def body(*refs):
        src, dst = refs[:n], refs[n:2 * n]
        send_sems, recv_sems, local_sems = refs[2 * n:]
        x, y, c, sibling, chips = _mesh_place()
        slot = lambda px, py, pc: 4 * px + 2 * py + pc
        me = slot(x, y, c)
        own = [pltpu.make_async_copy(src[it], dst[it].at[me], local_sems.at[it]) for it in range(n)]
        first = []
        for it in range(n):
            first.append(_remote(src[it], dst[it].at[me], send_sems.at[it, 0], recv_sems.at[it, 0], sibling))
            for j, chip in enumerate(chips):
                first.append(_remote(src[it], dst[it].at[me], send_sems.at[it, 1 + j], recv_sems.at[it, 1 + j],
                                     (*chip, c)))
        for cp in own + first:
            cp.start()
        passed = []
        for j, chip in enumerate(chips):
            blk = slot(*chip, c)
            for it in range(n):
                _remote(src[it], dst[it].at[blk], send_sems.at[it, 1 + j], recv_sems.at[it, 1 + j],
                        (*chip, c)).wait_recv()
                fwd = _remote(dst[it].at[blk], dst[it].at[blk], send_sems.at[it, 4 + j], recv_sems.at[it, 4 + j],
                              sibling)
                fwd.start()
                passed.append(fwd)
        for it in range(n):
            _remote(src[it], dst[it].at[slot(x, y, 1 - c)], send_sems.at[it, 0], recv_sems.at[it, 0],
                    sibling).wait_recv()
        for j, chip in enumerate(chips):
            for it in range(n):
                _remote(src[it], dst[it].at[slot(*chip, 1 - c)], send_sems.at[it, 4 + j], recv_sems.at[it, 4 + j],
                        sibling).wait_recv()
        for cp in first + passed:
            cp.wait_send()
        for cp in own:
            cp.wait()

    any_spec = pl.BlockSpec(memory_space=pl.ANY)
    return pl.pallas_call(
        body, name=name, in_specs=[any_spec] * n, out_specs=[any_spec] * n,
        out_shape=[jax.ShapeDtypeStruct((N_DEV,) + a.shape, a.dtype) for a in items],
        scratch_shapes=[pltpu.SemaphoreType.DMA((n, 7)), pltpu.SemaphoreType.DMA((n, 7)),
                        pltpu.SemaphoreType.DMA((n,))],
    )(*items)


def _peers():
    x, y, c = lax.axis_index("x"), lax.axis_index("y"), lax.axis_index("c")
    out = []
    for k in range(1, N_DEV):
        px = 1 - x if k & 4 else x
        py = 1 - y if k & 2 else y
        pc = 1 - c if k & 1 else c
        out.append((k, (px, py, pc), 4 * px + 2 * py + pc))
    return 4 * x + 2 * y + c, out


def _exchange_start(items, name, gather, carry=()):
    n, m = len(items), len(carry)

    def body(*refs):
        src, land = refs[:n], refs[n:2 * n]
        first_out = 2 * n + m
        send_sems, recv_sems = refs[first_out:first_out + n], refs[first_out + n:first_out + 2 * n]
        token = refs[-1]
        me, peers = _peers()
        for k, peer, slot in peers:
            for it in range(n):
                _remote(src[it] if gather else src[it].at[slot], land[it].at[me], send_sems[it], recv_sems[it],
                        peer).start()
        token[...] = jnp.zeros_like(token)

    hbm = pl.BlockSpec(memory_space=pltpu.HBM)
    sem = pl.BlockSpec(memory_space=pltpu.SEMAPHORE)
    land_shapes = [(N_DEV,) + (a.shape if gather else a.shape[1:]) for a in items]
    lands = [lax.empty(shp, a.dtype) for shp, a in zip(land_shapes, items)]
    through = list(items) + lands + list(carry)
    outs = pl.pallas_call(
        body, name=name,
        out_shape=(*[pltpu.SemaphoreType.DMA(())] * (2 * n), *[pltpu.HBM(a.shape, a.dtype) for a in through],
                   jax.ShapeDtypeStruct((8, LANES), F32)),
        in_specs=[hbm] * len(through),
        out_specs=(*[sem] * (2 * n), *[hbm] * len(through), pl.BlockSpec(memory_space=pltpu.VMEM)),
        input_output_aliases={i: 2 * n + i for i in range(len(through))},
        compiler_params=pltpu.CompilerParams(has_side_effects=pltpu.SideEffectType.DATAFLOW_SIDE_EFFECTING),
    )(*[pltpu.with_memory_space_constraint(a, pltpu.HBM) for a in through])
    return (list(outs[:n]), list(outs[n:2 * n]), list(outs[2 * n:3 * n]), list(outs[3 * n:4 * n]), outs[-1],
            list(outs[4 * n:4 * n + m]))


def _exchange_wait(send_sems, recv_sems, items, lands, after, name):
    n = len(items)

    def body(*refs):
        land = refs[n:2 * n]
        send_sems, recv_sems = refs[2 * n:3 * n], refs[3 * n:4 * n]
        me, peers = _peers()
        for it in range(n):
            seven = land[it].at[pl.ds(0, N_DEV - 1)]
            cp = _remote(seven, seven, send_sems[it], recv_sems[it], peers[0][1])
            cp.wait_send()
            cp.wait_recv()

    hbm = pl.BlockSpec(memory_space=pltpu.HBM)
    sem = pl.BlockSpec(memory_space=pltpu.SEMAPHORE)
    outs = pl.pallas_call(
        body, name=name,
        out_shape=tuple(pltpu.HBM(a.shape, a.dtype) for a in list(items) + list(lands)),
        in_specs=[hbm] * (2 * n) + [sem] * (2 * n) + [pl.BlockSpec(memory_space=pl.ANY)],
        out_specs=tuple([hbm] * (2 * n)),
        input_output_aliases={i: i for i in range(2 * n)},
        compiler_params=pltpu.CompilerParams(has_side_effects=pltpu.SideEffectType.DATAFLOW_SIDE_EFFECTING),
    )(*items, *lands, *send_sems, *recv_sems, after)
    return list(outs[:n]), list(outs[n:])


def _gelu_tanh(y):
    k = math.sqrt(2.0 / math.pi)
    t = jnp.tanh(k * (y + 0.044715 * y * y * y))
    return 0.5 * y * (1.0 + t), t


def _local_step(x, mod, target, W, late_weights, P, send_early):
    B, S, D = x.shape
    T = B * S
    TS = 512
    flat = lambda a: a.reshape(T, a.shape[-1])
    unflat = lambda a: a.reshape(B, S, a.shape[-1])
    mod_col = lambda i: (mod, D, i)

    def f_modnorm_project(xv, sc, sh, g, wq, wu, wg):
        u = ((xv * _rms_scale(xv) * g) * (1.0 + sc) + sh).astype(BF16)
        return (u, lax.dot_general(u, wq, _NT, preferred_element_type=F32),
                lax.dot_general(u, wu, _NT, preferred_element_type=F32),
                lax.dot_general(u, wg, _NT, preferred_element_type=F32))

    u1, qkv, us, gates = _rowwise(
        f_modnorm_project, [(x, D, 0)], [mod_col(1), mod_col(0)], [P["g_mix"], W["w_qkv"], W["w_us"], W["w_gates"]],
        [(D, BF16), (3 * ATT_WIDTH, F32), (SSM_WIDTH, F32), (2 * D, BF16)], [], [], ts=TS, name="modnorm_project_in")
    u1f = flat(u1)

    o_att, lse = _attention_fwd(qkv, P["slopes"])
    more_w, more_p = late_weights(o_att)
    W, P = {**W, **more_w}, {**P, **more_p}

    xs, y_mm = _scan_fwd(us, P["bb_big"], P["a_row"], P["cc_big"])

    bga, bgs = P["b_gate"][:, :D], P["b_gate"][:, D:]

    def f_mixer_tail(ov, ymm, usv, ga, gs, xv, gt, sc, sh, w_att, w_ssm, w_o, bga_, bgs_, g, dsk, wg, bg):
        yv = ymm + dsk * usv
        ge, _ = _gelu_tanh(yv)
        zv = (ge * _sigmoid(jnp.dot(ge.astype(BF16), wg, preferred_element_type=F32) + bg)).astype(BF16)
        ya = jnp.dot(ov, w_att, preferred_element_type=F32)
        ys = jnp.dot(zv, w_ssm, preferred_element_type=F32)
        mg = (_sigmoid(ga + bga_) * ya + _sigmoid(gs + bgs_) * ys).astype(BF16)
        mx = jnp.dot(mg, w_o, preferred_element_type=F32)
        h = xv + gt * mx
        return yv, zv, mg, mx, h, (h * _rms_scale(h) * g) * (1.0 + sc) + sh

    y_s5, z, merged, mix, h1, u2 = _rowwise(
        f_mixer_tail,
        [(o_att, ATT_WIDTH, 0), (y_mm, SSM_WIDTH, 0), (us, SSM_WIDTH, 0), (gates, D, 0), (gates, D, 1), (x, D, 0)],
        [mod_col(2), mod_col(4), mod_col(3)],
        [W["w_proj_att"], W["w_proj_ssm"], W["w_out"], bga, bgs, P["g_ffn"], P["d_skip"], W["w_glu"], P["b_glu"]],
        [(SSM_WIDTH, F32), (SSM_WIDTH, BF16), (D, BF16), (D, BF16), (D, F32), (D, BF16)],
        [], [], ts=TS, raw=(0,), name="mixer_tail")

    up, ff = _up_conv_fwd(u2, W["w_up"], P["w_conv"], P["b_conv"])

    def f_head(ffv, h1v, tg, gt, g, w_dn):
        dn = jnp.dot(ffv, w_dn, preferred_element_type=F32)
        h2 = h1v + gt * dn
        r = _rms_scale(h2)
        nh = h2 * r
        e = nh * g - tg
        dy = e * (1.0 / D)
        gy = dy * g
        dh = r * (gy - nh * jnp.mean(gy * nh, axis=-1, keepdims=True))
        return (dh, dh * gt, _col_sum(dh * dn), _col_sum(dy * nh), _col_sum(e * e) * (0.5 / D))

    dh2, d_down, d_gt2, d_g_final, loss_cols = _rowwise(
        f_head, [(ff, D_FF, 0), (h1, D, 0), (target, D, 0)], [mod_col(5)], [P["g_final"], W["w_down"]],
        [(D, BF16), (D, BF16)], [D], [(1, D), (1, D)], ts=TS, raw=(0,), name="ffn_down_head_loss")

    d_downf = flat(d_down)
    d_w_down = _matmul(flat(ff), d_downf, ta=True, out_dtype=BF16, name="ffn_down_dw")
    d_up, d_w_conv, d_b_conv = _conv_bwd(up, d_down, W["w_down"], P["w_conv"], P["b_conv"])
    d_upf = flat(d_up)
    d_w_up = _up_dw(flat(u2), d_upf, name="ffn_up_dw")
    token, _ = send_early(dict(w_down=d_w_down.reshape(N_DEV, D_FF // N_DEV, D), w_up=d_w_up))
    g_ffn_after = P["g_ffn"] + token[0:1, 0:1]

    def f_up_back_modnorm(dup, h, dres, mx, sc, gt, g, w2):
        du = None
        for j in range(UP_SLOTS):
            wa, wv = w2[j * D:(j + 1) * D, :], w2[(j + UP_SLOTS) * D:(j + UP_SLOTS + 1) * D, :]
            wj = jnp.concatenate([wa[:, :HALF], wv[:, :HALF], wa[:, HALF:], wv[:, HALF:]], axis=1)
            part = lax.dot_general(dup[:, j * UP_GROUP:(j + 1) * UP_GROUP], wj, _NT, preferred_element_type=F32)
            du = part if du is None else du + part
        r = _rms_scale(h)
        nh = h * r
        dn = du * (1.0 + sc)
        gy = dn * g
        dh = dres + r * (gy - nh * jnp.mean(gy * nh, axis=-1, keepdims=True))
        return (dh, dh * gt, _col_sum(du), _col_sum(du * nh * g), _col_sum(dh * mx), _col_sum(dn * nh))

    dh1, d_mix, d_sh2, d_sc2, d_gt1, d_g_ffn = _rowwise(
        f_up_back_modnorm, [(d_up, 2 * D_FF, 0), (h1, D, 0), (dh2, D, 0), (mix, D, 0)], [mod_col(4), mod_col(2)],
        [g_ffn_after, W["w_up"].reshape(N_DEV * D, 2 * HALF)],
        [(D, BF16), (D, BF16)], [D, D, D], [(1, D)], ts=TS, raw=(0,),
        name="ffn_up_back_modnorm", split=2)

    d_mixf = flat(d_mix)
    d_w_out = _matmul(flat(merged), d_mixf, ta=True, out_dtype=BF16, name="proj_out_dw")

    def f_mixer_tail_bwd(dmx, ov, zv, ga, gs, yv, usv, w_o, w_att, w_ssm, bga_, bgs_, dsk, wg, bg):
        dm = lax.dot_general(dmx, w_o, _NT, preferred_element_type=F32)
        ya = jnp.dot(ov, w_att, preferred_element_type=F32)
        ys = jnp.dot(zv, w_ssm, preferred_element_type=F32)
        sa, ss = _sigmoid(ga + bga_), _sigmoid(gs + bgs_)
        dga = dm * ya * sa * (1.0 - sa)
        dgs = dm * ys * ss * (1.0 - ss)
        dya, dys = (dm * sa).astype(BF16), (dm * ss).astype(BF16)
        d_o = lax.dot_general(dya, w_att, _NT, preferred_element_type=F32)
        dz = lax.dot_general(dys, w_ssm, _NT, preferred_element_type=F32)
        ge, t = _gelu_tanh(yv)
        sg = _sigmoid(jnp.dot(ge.astype(BF16), wg, preferred_element_type=F32) + bg)
        dpre = dz * ge * sg * (1.0 - sg)
        dge = dz * sg + lax.dot_general(dpre.astype(BF16), wg, _NT, preferred_element_type=F32)
        k = math.sqrt(2.0 / math.pi)
        dgelu = 0.5 * (1.0 + t) + 0.5 * yv * (1.0 - t * t) * k * (1.0 + 3.0 * 0.044715 * yv * yv)
        dy = dge * dgelu
        dwg = lax.dot_general(ge.astype(BF16), dpre.astype(BF16), _TN, preferred_element_type=F32)
        return (dya, dys, jnp.concatenate([dga, dgs], axis=1), d_o, dy, dy * dsk,
                _col_sum(dga), _col_sum(dgs), dwg, _col_sum(dpre), _col_sum(dy * usv))

    (d_y_att, d_y_ssm, d_gates, d_o_att, d_y_s5, d_us_skip, d_bga, d_bgs, d_w_glu, d_b_glu, d_d_skip) = _rowwise(
        f_mixer_tail_bwd,
        [(d_mix, D, 0), (o_att, ATT_WIDTH, 0), (z, SSM_WIDTH, 0), (gates, D, 0), (gates, D, 1),
         (y_s5, SSM_WIDTH, 0), (us, SSM_WIDTH, 0)], [],
        [W["w_out"], W["w_proj_att"], W["w_proj_ssm"], bga, bgs, P["d_skip"], W["w_glu"], P["b_glu"]],
        [(D, BF16), (D, BF16), (2 * D, BF16), (ATT_WIDTH, F32), (SSM_WIDTH, BF16), (SSM_WIDTH, F32)], [],
        [(1, D), (1, D), (SSM_WIDTH, SSM_WIDTH), (1, SSM_WIDTH), (1, SSM_WIDTH)], ts=TS, raw=(0, 1, 2),
        name="mixer_tail_bwd")

    d_yaf, d_ysf = flat(d_y_att), flat(d_y_ssm)
    d_w_proj_att = _matmul(flat(o_att), d_yaf, ta=True, out_dtype=BF16, name="proj_att_dw")
    d_w_proj_ssm = _matmul(flat(z), d_ysf, ta=True, out_dtype=BF16, name="proj_ssm_dw")
    d_us_parts, g_ab, d_bb, d_cc = _scan_bwd(d_y_s5, us, P["bb_big"], P["cc_big"], xs, P["a_row"])

    token, _ = send_early(dict(
        w_out=d_w_out.reshape(N_DEV, D // N_DEV, D), w_proj_att=_cols_to_slots(d_w_proj_att),
        w_proj_ssm=_cols_to_slots(d_w_proj_ssm),
        w_glu=d_w_glu.astype(BF16).reshape(N_DEV, SSM_WIDTH // N_DEV, SSM_WIDTH),
        w_conv=_cols_to_slots(d_w_conv.astype(BF16))))
    d_qkv = _attention_bwd(qkv, o_att, d_o_att, lse, P["slopes"] + token[0, 0])

    def f_add(*parts):
        return sum(parts[1:], parts[0])

    n_parts = d_us_parts.shape[0]
    stacked = d_us_parts.reshape(n_parts * B, S, SSM_WIDTH)
    (d_us,) = _rowwise(f_add, [(d_us_skip, SSM_WIDTH, 0)] + [(stacked, SSM_WIDTH, 0, j * B) for j in range(n_parts)],
                       [], [],
                       [(SSM_WIDTH, BF16)], [], [], ts=TS, name="s5_input_grad")
    d_qkvf = flat(d_qkv)
    d_usf = flat(d_us)
    d_gatesf = flat(d_gates)
    d_w_in_t = jnp.concatenate(
        [_unpair_qkv_rows(_matmul(d_qkvf, u1f, ta=True, out_dtype=BF16, name="proj_qkv_dw")),
         _matmul(d_usf, u1f, ta=True, out_dtype=BF16, name="proj_ssm_in_dw"),
         _matmul(d_gatesf, u1f, ta=True, out_dtype=BF16, name="proj_gates_dw")], axis=0)
    token, (w_qkv, w_us, w_gates) = send_early(dict(w_in=d_w_in_t.reshape(N_DEV, -1, D)),
                                               carry=[W["w_qkv"], W["w_us"], W["w_gates"]])
    def f_project_back_modnorm(dq, du_, dg, h, dres, sc, g, wq, wu, wg):
        du = (jnp.dot(dq, wq, preferred_element_type=F32) + jnp.dot(du_, wu, preferred_element_type=F32)
              + jnp.dot(dg, wg, preferred_element_type=F32))
        r = _rms_scale(h)
        nh = h * r
        dn = du * (1.0 + sc)
        gy = dn * g
        dh = dres + r * (gy - nh * jnp.mean(gy * nh, axis=-1, keepdims=True))
        return (dh, _col_sum(du), _col_sum(du * nh * g), _col_sum(dn * nh))

    grad_x, d_sh1, d_sc1, d_g_mix = _rowwise(
        f_project_back_modnorm,
        [(d_qkv, 3 * ATT_WIDTH, 0), (d_us, SSM_WIDTH, 0), (d_gates, 2 * D, 0), (x, D, 0), (dh1, D, 0)], [mod_col(1)],
        [P["g_mix"] + token[0:1, 0:1], w_qkv, w_us, w_gates],
        [(D, F32)], [D, D], [(1, D)], ts=TS, raw=(0, 1, 2),
        name="project_in_back_modnorm", split=2)

    d_mod = jnp.concatenate([d_sh1, d_sc1, d_gt1, d_sh2, d_sc2, d_gt2], axis=-1)
    g_ab_re, g_ab_im = _deinterleave(g_ab)
    d_bb_re, d_bb_im = _deinterleave(d_bb)
    d_cc_re, d_cc_im = (t.T for t in _deinterleave(d_cc.T))
    small = dict(g_mix=d_g_mix, b_gate=jnp.concatenate([d_bga, d_bgs], axis=1), g_ab_re=g_ab_re, g_ab_im=g_ab_im,
                 d_bb_re=d_bb_re, d_bb_im=d_bb_im, d_cc_re=d_cc_re, d_cc_im=d_cc_im, d_skip=d_d_skip,
                 b_glu=d_b_glu, g_ffn=d_g_ffn, b_conv=d_b_conv, g_final=d_g_final, loss_cols=loss_cols)
    return grad_x, d_mod, small


def _block_diag_in(bb):
    t = bb.reshape(SSM_GROUPS, SSM_STATE, SSM_GROUP_CH)
    eye = jnp.eye(SSM_GROUPS, dtype=bb.dtype)
    return jnp.einsum("gnc,gh->gchn", t, eye).reshape(SSM_WIDTH, SSM_COLS)


def _block_diag_out(cm):
    eye = jnp.eye(SSM_GROUPS, dtype=cm.dtype)
    return jnp.einsum("gcn,gh->gnhc", cm, eye).reshape(SSM_COLS, SSM_WIDTH)


def _diag_blocks_in(m):
    t = m.reshape(SSM_GROUPS, SSM_GROUP_CH, SSM_GROUPS, SSM_STATE)
    idx = jnp.arange(SSM_GROUPS)
    return t[idx, :, idx, :].transpose(0, 2, 1).reshape(SSM_COLS, SSM_GROUP_CH)


def _diag_blocks_out(m):
    t = m.reshape(SSM_GROUPS, SSM_STATE, SSM_GROUPS, SSM_GROUP_CH)
    idx = jnp.arange(SSM_GROUPS)
    return t[idx, :, idx, :].transpose(0, 2, 1)


def _pair_qkv_rows(w):
    return w.reshape(3, N_HEADS // 2, LANES, w.shape[1]).swapaxes(0, 1).reshape(w.shape)


def _unpair_qkv_rows(w):
    return w.reshape(N_HEADS // 2, 3, LANES, w.shape[1]).swapaxes(0, 1).reshape(w.shape)


def _interleave(re, im):
    lead = re.shape[:-1]
    g = lambda a: a.reshape(lead + (SSM_COLS // SCAN_COLS, 1, SCAN_COLS))
    return jnp.concatenate([g(re), g(im)], axis=-2).reshape(lead + (2 * SSM_COLS,))


def _deinterleave(x):
    lead = x.shape[:-1]
    t = x.reshape(lead + (SSM_COLS // SCAN_COLS, 2, SCAN_COLS))
    return t[..., 0, :].reshape(lead + (SSM_COLS,)), t[..., 1, :].reshape(lead + (SSM_COLS,))


def _cols_to_slots(g):
    R = g.shape[0]
    return g.reshape(R, N_DEV, g.shape[1] // N_DEV).transpose(1, 0, 2)


def _slots_to_cols(g):
    return g.transpose(1, 0, 2).reshape(g.shape[1], N_DEV * g.shape[2])


SMALL_ORDER = ("b_ada", "g_mix", "b_gate", "a_re", "a_im", "log_dt", "b_re", "b_im", "c_re", "c_im", "d_skip",
               "b_glu", "g_ffn", "b_conv", "g_final")


def _pack(arrs):
    pieces, offs, row = [], [], 0
    for a in arrs:
        f = a.reshape(-1).astype(F32)
        n = f.shape[0]
        rows = -(-n // LANES)
        pieces.append(jnp.pad(f, (0, rows * LANES - n)))
        offs.append((row, n))
        row += rows
    return jnp.concatenate(pieces).reshape(row, LANES), offs


def _unpack(packed, offs, shapes):
    flat = packed.reshape(-1)
    return [flat[r * LANES:r * LANES + n].reshape(s) for (r, n), s in zip(offs, shapes)]


def kernel(x, c, w_ada, b_ada, g_mix, w_in, b_gate, a_re, a_im, log_dt, b_re, b_im, c_re, c_im, d_skip, w_glu, b_glu, w_proj_att, w_proj_ssm, w_out, g_ffn, w_up, w_conv, b_conv, w_down, g_final, loss_target, m_w_ada, m_b_ada, m_g_mix, m_w_in, m_b_gate, m_a_re, m_a_im, m_log_dt, m_b_re, m_b_im, m_c_re, m_c_im, m_d_skip, m_w_glu, m_b_glu, m_w_proj_att, m_w_proj_ssm, m_w_out, m_g_ffn, m_w_up, m_w_conv, m_b_conv, m_w_down, m_g_final, v_w_ada, v_b_ada, v_g_mix, v_w_in, v_b_gate, v_a_re, v_a_im, v_log_dt, v_b_re, v_b_im, v_c_re, v_c_im, v_d_skip, v_w_glu, v_b_glu, v_w_proj_att, v_w_proj_ssm, v_w_out, v_g_ffn, v_w_up, v_w_conv, v_b_conv, v_w_down, v_g_final):
    args = dict(locals())
    B, S, D = x.shape
    me = 4 * lax.axis_index("x") + 2 * lax.axis_index("y") + lax.axis_index("c")
    bf = lambda w: w[0].astype(BF16)

    first = [c, w_in[0].T.astype(BF16)]
    first_sems = _exchange_start(first, "start_first_weights", gather=True, carry=[log_dt[0]])
    (log_dt_sent,) = first_sems[5]

    ab_re, ab_im, f_re, f_im = _s5_params(a_re[0], a_im[0], log_dt_sent.reshape(SSM_GROUPS, 1))
    col = lambda a: a.reshape(SSM_COLS, 1)
    b_re2, b_im2 = b_re[0].reshape(SSM_COLS, SSM_GROUP_CH), b_im[0].reshape(SSM_COLS, SSM_GROUP_CH)
    bb_re, bb_im = _s5_input_matrix(col(f_re), col(f_im), b_re2, b_im2)
    slopes = jnp.asarray([2.0 ** (-8.0 * (h + 1) / N_HEADS) for h in range(N_HEADS)], F32)
    P = dict(g_mix=g_mix, g_ffn=g_ffn, g_final=g_final.reshape(1, D), b_gate=b_gate, d_skip=d_skip, b_glu=b_glu,
             b_conv=b_conv, slopes=slopes,
             a_row=_interleave(ab_re.reshape(1, SSM_COLS), ab_im.reshape(1, SSM_COLS)),
             cc_big=_interleave(_block_diag_out(c_re[0]).T, -_block_diag_out(c_im[0]).T).T,
             bb_big=_interleave(_block_diag_in(bb_re), _block_diag_in(bb_im)))

    own_first, first_lands = _exchange_wait(*first_sems[:4], P["bb_big"], name="wait_first_weights")
    c_slots, w_in_slots = [lax.dynamic_update_index_in_dim(land, a, me, 0) for land, a in zip(first_lands, own_first)]
    c_all = c_slots.reshape(N_DEV * B, D)
    w_in_t = w_in_slots.reshape(-1, D)
    n_qkv = 3 * ATT_WIDTH
    W = dict(w_qkv=_pair_qkv_rows(w_in_t[:n_qkv]), w_us=w_in_t[n_qkv:n_qkv + SSM_WIDTH],
             w_gates=w_in_t[n_qkv + SSM_WIDTH:])

    n_ada = w_ada.shape[2]
    b_ada_cols = lax.dynamic_slice(b_ada, (0, me * n_ada), (1, n_ada))
    mod_part = _ada_fwd(c_all, w_ada[0], b_ada_cols)
    (mod_slots,) = _exchange([(mod_part.reshape(N_DEV, B, n_ada), True)], name="scatter_modulation")
    mod = mod_slots.transpose(1, 0, 2).reshape(B, 1, 6 * D)

    later = [bf(w_glu), bf(w_proj_att), bf(w_proj_ssm), bf(w_out), bf(w_up), w_conv[0], bf(w_down)]
    later_sems = _exchange_start(later, "start_later_weights", gather=True, carry=[mod])
    (mod,) = later_sems[5]

    def late_weights(after):
        _, lands = _exchange_wait(*later_sems[:4], after, name="wait_later_weights")
        g = [lax.dynamic_update_index_in_dim(land, a, me, 0) for land, a in zip(lands, later)]
        more_w = dict(w_glu=g[0].reshape(SSM_WIDTH, SSM_WIDTH), w_proj_att=_slots_to_cols(g[1]),
                      w_proj_ssm=_slots_to_cols(g[2]), w_out=g[3].reshape(D, D), w_up=g[4],
                      w_down=g[6].reshape(D_FF, D))
        return more_w, dict(w_conv=_slots_to_cols(g[5]))

    in_flight = []

    def send_early(grads, carry=()):
        names = list(grads)
        handles = _exchange_start([grads[n] for n in names], "start_gradients_%d" % len(in_flight), gather=False,
                                  carry=carry)
        in_flight.append((names,) + handles[:4])
        return handles[4], handles[5]

    grad_x, d_mod, small = _local_step(x, mod, loss_target, W, late_weights, P, send_early)

    small_list = [small["loss_cols"], small["g_mix"], small["b_gate"], small["g_ab_re"], small["g_ab_im"],
                  _diag_blocks_in(small["d_bb_re"]), _diag_blocks_in(small["d_bb_im"]),
                  _diag_blocks_out(small["d_cc_re"]), -_diag_blocks_out(small["d_cc_im"]),
                  small["g_ffn"], small["b_conv"], small["g_final"], small["d_skip"], small["b_glu"]]
    small_packed, small_offs = _pack(small_list)
    small_sems = _exchange_start([small_packed, d_mod.reshape(B, 6 * D)], "start_small_gradients", gather=True)
    updated = [small_sems[4]]

    out = {}

    def update(name, parts, own=None):
        view = (lambda a: a[0].T) if name == "w_in" else (lambda a: a[0])
        back = (lambda a: a.T[None]) if name == "w_in" else (lambda a: a[None])
        g, dl, mn, vn = _adamw(view(args[name]), view(args["m_" + name]), view(args["v_" + name]), parts,
                               name="adamw_" + name, own=own)
        updated.append(vn)
        for key, val in (("grad_", g), ("delta_", dl), ("new_m_", mn), ("new_v_", vn)):
            out[key + name] = back(val)

    my_slot = me.astype(jnp.int32).reshape(1)
    for i, (names, send_sems, recv_sems, sent, lands) in enumerate(in_flight):
        sent, lands = _exchange_wait(send_sems, recv_sems, sent, lands, updated[-1], name="wait_gradients_%d" % i)
        for name, own_slots, landed in zip(names, sent, lands):
            update(name, landed, own=(own_slots, my_slot))

    own_small, small_lands = _exchange_wait(*small_sems[:4], updated[-1], name="wait_small_gradients")
    small_all, dmod_slots = [lax.dynamic_update_index_in_dim(land, a, me, 0)
                             for land, a in zip(small_lands, own_small)]
    dmod_all = dmod_slots.reshape(N_DEV * B, 6 * D)
    dmod_cols = lax.dynamic_slice(dmod_all, (0, me * n_ada), (N_DEV * B, n_ada))
    d_w_ada, d_b_ada = _ada_bwd(c_all, dmod_all, dmod_cols)
    update("w_ada", d_w_ada[None])

    loss_row, loss_n = small_offs[0]
    small_sum, loss_vec = _sum_parts(small_all, (loss_row, loss_row + loss_n // LANES))
    shapes = [(1, D), (1, D), (1, 2 * D), (SSM_GROUPS, SSM_STATE), (SSM_GROUPS, SSM_STATE), (SSM_COLS, SSM_GROUP_CH),
              (SSM_COLS, SSM_GROUP_CH), (1, SSM_GROUPS, SSM_GROUP_CH, SSM_STATE),
              (1, SSM_GROUPS, SSM_GROUP_CH, SSM_STATE), (1, D), (1, D_FF), (D,), (1, SSM_WIDTH), (1, SSM_WIDTH)]
    (_, s_g_mix, s_b_gate, s_ab_re, s_ab_im, s_bb_re, s_bb_im, s_c_re, s_c_im, s_g_ffn, s_b_conv, s_g_final,
     s_d_skip, s_b_glu) = _unpack(small_sum, small_offs, shapes)
    d_b_re2, d_b_im2, d_f_re, d_f_im = _s5_input_matrix_bwd(col(f_re), col(f_im), b_re2, b_im2, s_bb_re, s_bb_im)
    d_a_re, d_a_im, d_log_dt = _s5_params_bwd(a_re[0], a_im[0], log_dt[0].reshape(SSM_GROUPS, 1), s_ab_re, s_ab_im,
                                              d_f_re.reshape(SSM_GROUPS, SSM_STATE),
                                              d_f_im.reshape(SSM_GROUPS, SSM_STATE))
    grads_small = dict(b_ada=d_b_ada, g_mix=s_g_mix, b_gate=s_b_gate, a_re=d_a_re[None], a_im=d_a_im[None],
                       log_dt=d_log_dt.reshape(1, SSM_GROUPS), b_re=d_b_re2.reshape(b_re.shape),
                       b_im=d_b_im2.reshape(b_im.shape), c_re=s_c_re, c_im=s_c_im, d_skip=s_d_skip, b_glu=s_b_glu,
                       g_ffn=s_g_ffn, b_conv=s_b_conv, g_final=s_g_final)
    flat2 = lambda a: a.reshape(-1, a.shape[-1])
    res = _adamw_small([flat2(args[n]) for n in SMALL_ORDER], [flat2(args["m_" + n]) for n in SMALL_ORDER],
                       [flat2(args["v_" + n]) for n in SMALL_ORDER],
                       [flat2(grads_small[n].reshape(args[n].shape)) for n in SMALL_ORDER])
    for i, n in enumerate(SMALL_ORDER):
        for k, key in enumerate(("grad_", "delta_", "new_m_", "new_v_")):
            out[key + n] = res[4 * i + k].reshape(args[n].shape)

    order = ["w_ada", "b_ada", "g_mix", "w_in", "b_gate", "a_re", "a_im", "log_dt", "b_re", "b_im", "c_re", "c_im",
             "d_skip", "w_glu", "b_glu", "w_proj_att", "w_proj_ssm", "w_out", "g_ffn", "w_up", "w_conv", "b_conv",
             "w_down", "g_final"]
    loss = loss_vec[0, 0]
    return (loss, grad_x, *[out[k + n] for k in ("grad_", "delta_", "new_m_", "new_v_") for n in order])
```

```python
import math

import jax
import jax.numpy as jnp
from jax import lax
from jax.experimental import pallas as pl
from jax.experimental.pallas import tpu as pltpu

F32 = jnp.float32
BF16 = jnp.bfloat16

N_DEV = 8
D_MODEL = 1024
N_HEADS = 8
HEAD_DIM = 64
ATT_WIDTH = N_HEADS * HEAD_DIM
DILATIONS = (1, 4, 16)
WIN = 128
SSM_GROUPS = 16
SSM_GROUP_CH = 16
SSM_WIDTH = SSM_GROUPS * SSM_GROUP_CH
SSM_STATE = 64
SSM_COLS = SSM_GROUPS * SSM_STATE
D_FF = 2048
EPS = 1e-6
NEG_INF = -1e30
ADAM_LR, ADAM_B1, ADAM_B2, ADAM_EPS, ADAM_WD, ADAM_STEP = 0.001, 0.9, 0.999, 1e-08, 0.01, 10

V7X_VMEM_LIMIT = 56 * 1024 * 1024
LANES = 128


def _params(n_grid):
    return pltpu.CompilerParams(dimension_semantics=("arbitrary",) * n_grid,
                                vmem_limit_bytes=V7X_VMEM_LIMIT)


def _tile(n, pref):
    if n <= pref:
        return n
    t = (pref // LANES) * LANES
    while t > 0:
        if n % t == 0:
            return t
        t -= LANES
    return n


def _matmul(a, b, *, ta=False, tb=False, out_dtype=F32, name):
    if ta:
        K, M = a.shape
    else:
        M, K = a.shape
    if tb:
        N, K2 = b.shape
    else:
        K2, N = b.shape
    assert K == K2, (a.shape, b.shape)
    if ta:
        tm, tn, tk = _tile(M, 1024), _tile(N, 2048), _tile(K, 1024)
    else:
        tm, tk = _tile(M, 512), _tile(K, 4096)
        tn = _tile(N, 2048 if K <= 2048 else 1024)
    nk = K // tk
    dn = (((0,) if ta else (1,), (1,) if tb else (0,)), ((), ()))

    def body(a_ref, b_ref, o_ref, acc_ref):
        k = pl.program_id(2)
        part = lax.dot_general(a_ref[...].astype(BF16), b_ref[...].astype(BF16), dn, preferred_element_type=F32)
        if nk == 1:
            o_ref[...] = part.astype(o_ref.dtype)
            return

        @pl.when(k == 0)
        def _():
            acc_ref[...] = jnp.zeros_like(acc_ref)

        acc_ref[...] += part

        @pl.when(k == nk - 1)
        def _():
            o_ref[...] = acc_ref[...].astype(o_ref.dtype)

    a_spec = (pl.BlockSpec((tk, tm), lambda j, i, k: (k, i)) if ta
              else pl.BlockSpec((tm, tk), lambda j, i, k: (i, k)))
    b_spec = (pl.BlockSpec((tn, tk), lambda j, i, k: (j, k)) if tb
              else pl.BlockSpec((tk, tn), lambda j, i, k: (k, j)))
    return pl.pallas_call(
        body, name=name, grid=(N // tn, M // tm, nk),
        in_specs=[a_spec, b_spec],
        out_specs=pl.BlockSpec((tm, tn), lambda j, i, k: (i, j)),
        out_shape=jax.ShapeDtypeStruct((M, N), out_dtype),
        scratch_shapes=[pltpu.VMEM((tm, tn) if nk > 1 else (8, LANES), F32)],
        compiler_params=_params(3),
    )(a, b)


HALF = 256
UP_SLOTS = N_DEV // 2
UP_GROUP = 4 * HALF


def _up_weight_spec(K, index):
    return pl.BlockSpec((2, None, K, 2 * HALF), index)


def _up_dw(a, d, name):
    M, K = a.shape
    tk = _tile(M, 1024)
    nk = M // tk

    def body(a_ref, d_ref, o_ref, acc_ref):
        k = pl.program_id(1)

        @pl.when(k == 0)
        def _():
            acc_ref[...] = jnp.zeros_like(acc_ref)

        acc_ref[...] += lax.dot_general(a_ref[...], d_ref[...], _TN, preferred_element_type=F32)

        @pl.when(k == nk - 1)
        def _():
            for half in range(2):
                for part in range(2):
                    lo = (2 * half + part) * HALF
                    o_ref[part, :, half * HALF:(half + 1) * HALF] = acc_ref[:, lo:lo + HALF].astype(o_ref.dtype)

    out = pl.pallas_call(
        body, name=name, grid=(UP_SLOTS, nk),
        in_specs=[pl.BlockSpec((tk, K), lambda j, k: (k, 0)), pl.BlockSpec((tk, UP_GROUP), lambda j, k: (k, j))],
        out_specs=_up_weight_spec(K, lambda j, k: (0, j, 0, 0)),
        out_shape=jax.ShapeDtypeStruct((2, UP_SLOTS, K, 2 * HALF), BF16),
        scratch_shapes=[pltpu.VMEM((K, UP_GROUP), F32)], compiler_params=_params(2),
    )(a, d)
    return out.reshape(N_DEV, K, 2 * HALF)


def _rowwise(fn, rows, bvecs, consts, out_rows, out_b, out_g, *, ts, name, raw=(), split=1):
    B, S = rows[0][0].shape[:2]
    nin = len(rows) + len(bvecs) + len(consts)
    nr, nb, ng = len(out_rows), len(out_b), len(out_g)
    sub = ts // split

    def body(*refs):
        b = pl.program_id(0)
        s = pl.program_id(1)
        orefs = refs[nin:]
        fixed = [r[...] for r in refs[len(rows):nin]]
        sums = None
        for h in range(split):
            sec = slice(h * sub, (h + 1) * sub)
            vals = [r[sec, :] if i in raw else r[sec, :].astype(F32) for i, r in enumerate(refs[:len(rows)])]
            outs = fn(*vals, *fixed)
            if not isinstance(outs, (tuple, list)):
                outs = (outs,)
            for i in range(nr):
                orefs[i][sec, :] = outs[i].astype(orefs[i].dtype)
            part = list(outs[nr:])
            sums = part if sums is None else [a + c for a, c in zip(sums, part)]
        for i in range(nb):
            ref = orefs[nr + i]

            @pl.when(s == 0)
            def _(ref=ref):
                ref[...] = jnp.zeros_like(ref)

            ref[...] += sums[i]
        for i in range(ng):
            ref = orefs[nr + nb + i]

            @pl.when((s == 0) & (b == 0))
            def _(ref=ref):
                ref[...] = jnp.zeros_like(ref)

            ref[...] += sums[nb + i]

    rows = [r if len(r) == 4 else r + (0,) for r in rows]
    in_specs = ([pl.BlockSpec((None, ts, cb), lambda b, s, ci=ci, b0=b0: (b0 + b, s, ci)) for (_, cb, ci, b0) in rows]
                + [pl.BlockSpec((None, 1, cb), lambda b, s, ci=ci: (b, 0, ci)) for (_, cb, ci) in bvecs]
                + [pl.BlockSpec(a.shape, lambda b, s: (0, 0)) for a in consts])
    out_shape = ([jax.ShapeDtypeStruct((B, S, c), dt) for (c, dt) in out_rows]
                 + [jax.ShapeDtypeStruct((B, 1, c), F32) for c in out_b]
                 + [jax.ShapeDtypeStruct(rc, F32) for rc in out_g])
    out_specs = ([pl.BlockSpec((None, ts, c), lambda b, s: (b, s, 0)) for (c, _) in out_rows]
                 + [pl.BlockSpec((None, 1, c), lambda b, s: (b, 0, 0)) for c in out_b]
                 + [pl.BlockSpec(rc, lambda b, s: (0, 0)) for rc in out_g])
    args = [r[0] for r in rows] + [a for (a, _, _) in bvecs] + list(consts)
    return pl.pallas_call(
        body, name=name, grid=(B, S // ts), in_specs=in_specs, out_specs=out_specs,
        out_shape=out_shape, compiler_params=_params(2),
    )(*args)


def _col_sum(v):
    return jnp.sum(v, axis=0, keepdims=True)


def _rms_scale(h):
    return lax.rsqrt(jnp.mean(h * h, axis=-1, keepdims=True) + EPS)


def _sigmoid(v):
    return 0.5 * (1.0 + jnp.tanh(0.5 * v))


ATT_SCALE = HEAD_DIM ** -0.5
COPY_ROWS = 256
_NT = (((1,), (1,)), ((), ()))
_TN = (((0,), (0,)), ((), ()))


def _row_chunks(d, seq):
    sub = seq // d
    out = []
    for r in range(d):
        for c0 in range(0, sub, COPY_ROWS):
            n = min(COPY_ROWS, sub - c0)
            out.append((pl.ds(r + c0 * d, n, stride=d), r * sub + c0, n))
    return out


ATT_UNROLL = 16
KEYS = 2 * WIN


def _zero_once(refs):
    @pl.when((pl.program_id(0) == 0) & (pl.program_id(1) == 0))
    def _():
        for r in refs:
            r[...] = jnp.zeros_like(r)


def _pair_bias(bias_ref, slopes_ref, hp, d, key_major):
    shape = (KEYS, WIN) if key_major else (WIN, KEYS)
    qi = lax.broadcasted_iota(jnp.int32, shape, 1 if key_major else 0)
    kj = lax.broadcasted_iota(jnp.int32, shape, 0 if key_major else 1)
    dist = WIN + qi - kj
    valid = (dist >= 0) & (dist <= WIN)
    distf = dist.astype(F32)
    for h in range(2):
        slope_d = slopes_ref[2 * hp + h] * float(d)
        with_prev = jnp.where(valid, -(slope_d * distf), NEG_INF)
        no_prev = jnp.where(kj >= WIN, with_prev, NEG_INF)
        span = slice(h * KEYS, (h + 1) * KEYS)
        if key_major:
            bias_ref[1, span, :] = with_prev
            bias_ref[0, span, :] = no_prev
        else:
            bias_ref[1, :, span] = with_prev
            bias_ref[0, :, span] = no_prev


def _stack_heads(v):
    first = lax.broadcasted_iota(jnp.int32, v.shape, 1) < HEAD_DIM
    zero = jnp.zeros_like(v)
    return jnp.concatenate([jnp.where(first, v, zero), jnp.where(first, zero, v)], axis=0)


def _per_head(c0, c1, n):
    return jnp.where(lax.broadcasted_iota(jnp.int32, (n, LANES), 1) < HEAD_DIM, c0, c1)


def _qkv_spec(seq, j):
    return pl.BlockSpec((None, seq, LANES), lambda b, hp: (b, 0, 3 * hp + j))


def _attention_fwd(qkv, slopes):
    B, S, _ = qkv.shape
    n_blk = S // WIN
    n_pair = N_HEADS // 2

    def body(slopes_ref, q_ref, k_ref, v_ref, o_ref, lse_ref, qp, kp, vp, bias, acc, mx, sm, acc_n, mx_n, sm_n):
        hp = pl.program_id(1)
        _zero_once((kp, vp))
        for p, d in enumerate(DILATIONS):
            nb = n_blk // d
            chunks = _row_chunks(d, S)
            for src, dst, n in chunks:
                qp[dst:dst + n, :] = (q_ref[src, :] * ATT_SCALE).astype(BF16)
                kp[WIN + dst:WIN + dst + n, :] = k_ref[src, :].astype(BF16)
                vp[WIN + dst:WIN + dst + n, :] = v_ref[src, :].astype(BF16)
            _pair_bias(bias, slopes_ref, hp, d, key_major=False)
            acc_t, mx_t, sm_t = (acc_n, mx_n, sm_n) if d == 1 else (acc, mx, sm)

            nk = WIN if nb == 1 else KEYS
            bias_cur = jnp.concatenate([bias[0, :, WIN:KEYS], bias[0, :, KEYS + WIN:]], axis=1) if nb == 1 else None

            def block(i, carry, p=p, nb=nb, nk=nk, bias_cur=bias_cur, acc_t=acc_t, mx_t=mx_t, sm_t=sm_t):
                cur = pl.ds(pl.multiple_of(i * WIN, WIN), WIN)
                keys = pl.ds(pl.multiple_of(i * WIN + (KEYS - nk), WIN), nk)
                s = lax.dot_general(qp[cur, :], _stack_heads(kp[keys, :]), _NT, preferred_element_type=F32)
                s = s + (bias_cur if nb == 1 else bias[((i % nb) > 0).astype(jnp.int32)])
                es, ms, ls = [], [], []
                for h in range(2):
                    sh = s[:, h * nk:(h + 1) * nk]
                    m = jnp.max(sh if nb == 1 else jnp.maximum(sh[:, :WIN], sh[:, WIN:]), axis=1, keepdims=True)
                    e = jnp.exp(sh - m)
                    es.append(e.astype(BF16))
                    ms.append(m)
                    ls.append(jnp.sum(e if nb == 1 else e[:, :WIN] + e[:, WIN:], axis=1, keepdims=True))
                acc_t[p, cur, :] = jnp.dot(jnp.concatenate(es, axis=1), _stack_heads(vp[keys, :]),
                                           preferred_element_type=F32)
                mx_t[p, cur, :] = _per_head(ms[0], ms[1], WIN)
                sm_t[p, cur, :] = _per_head(ls[0], ls[1], WIN)
                return carry

            lax.fori_loop(0, n_blk, block, 0, unroll=ATT_UNROLL)
            if d > 1:
                for src, dst, n in chunks:
                    acc_n[p, src, :] = acc[p, dst:dst + n, :]
                    mx_n[p, src, :] = mx[p, dst:dst + n, :]
                    sm_n[p, src, :] = sm[p, dst:dst + n, :]

        chunk = 256

        def merge(i, carry):
            rows = pl.ds(pl.multiple_of(i * chunk, chunk), chunk)
            ms = [mx_n[p, rows, :] for p in range(3)]
            m = jnp.maximum(jnp.maximum(ms[0], ms[1]), ms[2])
            ws = [jnp.exp(mp - m) for mp in ms]
            l = ws[0] * sm_n[0, rows, :] + ws[1] * sm_n[1, rows, :] + ws[2] * sm_n[2, rows, :]
            o = (ws[0] * acc_n[0, rows, :] + ws[1] * acc_n[1, rows, :] + ws[2] * acc_n[2, rows, :]) / l
            o_ref[rows, :] = o.astype(o_ref.dtype)
            lse = m + jnp.log(l)
            for h in range(2):
                lse_ref[rows, h:h + 1] = lse[:, h * HEAD_DIM:h * HEAD_DIM + 1]
            return carry

        lax.fori_loop(0, S // chunk, merge, 0)

    return pl.pallas_call(
        body, name="attention_fwd", grid=(B, n_pair),
        in_specs=[pl.BlockSpec(memory_space=pltpu.SMEM), _qkv_spec(S, 0), _qkv_spec(S, 1), _qkv_spec(S, 2)],
        out_specs=[pl.BlockSpec((None, S, LANES), lambda b, hp: (b, 0, hp)),
                   pl.BlockSpec((None, None, S, 2), lambda b, hp: (b, hp, 0, 0))],
        out_shape=[jax.ShapeDtypeStruct((B, S, ATT_WIDTH), BF16),
                   jax.ShapeDtypeStruct((B, n_pair, S, 2), F32)],
        scratch_shapes=[pltpu.VMEM((S, LANES), BF16), pltpu.VMEM((S + WIN, LANES), BF16),
                        pltpu.VMEM((S + WIN, LANES), BF16), pltpu.VMEM((2, WIN, 2 * KEYS), F32)]
        + [pltpu.VMEM((3, S, LANES), F32)] * 6,
        compiler_params=_params(2),
    )(slopes, qkv, qkv, qkv)


def _attention_bwd(qkv, o, do, lse, slopes):
    B, S, _ = qkv.shape
    n_blk = S // WIN
    n_pair = N_HEADS // 2

    def body(slopes_ref, q_ref, k_ref, v_ref, o_ref, do_ref, lse_ref, dx_ref,
             qp, dop, kp, vp, aux, auxp, aux_t, bias_t, dqp, dvk, dq_n, dk_n, dv_n):
        hp = pl.program_id(1)
        aux[...] = jnp.zeros_like(aux)
        for c0 in range(0, S, COPY_ROWS):
            rows = slice(c0, c0 + COPY_ROWS)
            prod = do_ref[rows, :] * o_ref[rows, :].astype(F32)
            for h in range(2):
                aux[rows, 2 * h:2 * h + 1] = lse_ref[rows, h:h + 1]
                aux[rows, 2 * h + 1:2 * h + 2] = jnp.sum(prod[:, h * HEAD_DIM:(h + 1) * HEAD_DIM], axis=1,
                                                         keepdims=True)
        dq_n[...] = jnp.zeros_like(dq_n)
        dk_n[...] = jnp.zeros_like(dk_n)
        dv_n[...] = jnp.zeros_like(dv_n)
        _zero_once((kp, vp))
        for p, d in enumerate(DILATIONS):
            nb = n_blk // d
            chunks = _row_chunks(d, S)
            for src, dst, n in chunks:
                auxp[dst:dst + n, :] = aux[src, :]
                qp[dst:dst + n, :] = (q_ref[src, :] * ATT_SCALE).astype(BF16)
                dop[dst:dst + n, :] = do_ref[src, :].astype(BF16)
                kp[WIN + dst:WIN + dst + n, :] = k_ref[src, :].astype(BF16)
                vp[WIN + dst:WIN + dst + n, :] = v_ref[src, :].astype(BF16)
            for i in range(n_blk):
                aux_t[i] = auxp[i * WIN:(i + 1) * WIN, :].T[0:8, :]
            _pair_bias(bias_t, slopes_ref, hp, d, key_major=True)
            dvk[...] = jnp.zeros_like(dvk)

            nk = WIN if nb == 1 else KEYS
            bias_cur = jnp.concatenate([bias_t[0, WIN:KEYS, :], bias_t[0, KEYS + WIN:, :]], axis=0) if nb == 1 else None

            def block(i, carry, nb=nb, nk=nk, bias_cur=bias_cur):
                cur = pl.ds(pl.multiple_of(i * WIN, WIN), WIN)
                keys = pl.ds(pl.multiple_of(i * WIN + (KEYS - nk), WIN), nk)
                q2, do2 = qp[cur, :], dop[cur, :]
                kc = _stack_heads(kp[keys, :])
                s_t = lax.dot_general(kc, q2, _NT, preferred_element_type=F32)
                s_t = s_t + (bias_cur if nb == 1 else bias_t[((i % nb) > 0).astype(jnp.int32)])
                dp_t = lax.dot_general(_stack_heads(vp[keys, :]), do2, _NT, preferred_element_type=F32)
                ps, dss = [], []
                for h in range(2):
                    span = slice(h * nk, (h + 1) * nk)
                    p_t = jnp.exp(s_t[span, :] - aux_t[i, 2 * h:2 * h + 1, :])
                    ds_t = p_t * (dp_t[span, :] - aux_t[i, 2 * h + 1:2 * h + 2, :])
                    ps.append(p_t.astype(BF16))
                    dss.append(ds_t.astype(BF16))
                do_rows, q_rows = _stack_heads(do2), _stack_heads(q2)
                zr = jnp.zeros_like(do_rows)
                rhs = jnp.concatenate([jnp.concatenate([do_rows, zr], axis=1),
                                       jnp.concatenate([zr, q_rows], axis=1)], axis=0)
                dvk[keys, :] += jnp.dot(jnp.concatenate(ps + dss, axis=1), rhs, preferred_element_type=F32)
                dqp[cur, :] = lax.dot_general(jnp.concatenate(dss, axis=0), kc, _TN, preferred_element_type=F32)
                return carry

            lax.fori_loop(0, n_blk, block, 0, unroll=ATT_UNROLL)
            for src, dst, n in chunks:
                dq_n[src, :] += dqp[dst:dst + n, :]
                dv_n[src, :] += dvk[WIN + dst:WIN + dst + n, :LANES]
                dk_n[src, :] += dvk[WIN + dst:WIN + dst + n, LANES:]
        for c0 in range(0, S, COPY_ROWS):
            rows = slice(c0, c0 + COPY_ROWS)
            dx_ref[rows, 0:LANES] = (dq_n[rows, :] * ATT_SCALE).astype(dx_ref.dtype)
            dx_ref[rows, LANES:2 * LANES] = dk_n[rows, :].astype(dx_ref.dtype)
            dx_ref[rows, 2 * LANES:3 * LANES] = dv_n[rows, :].astype(dx_ref.dtype)

    pair = lambda width: pl.BlockSpec((None, S, width), lambda b, hp: (b, 0, hp))
    vm = lambda shape, dt: pltpu.VMEM(shape, dt)
    return pl.pallas_call(
        body, name="attention_bwd", grid=(B, n_pair),
        in_specs=[pl.BlockSpec(memory_space=pltpu.SMEM), _qkv_spec(S, 0), _qkv_spec(S, 1), _qkv_spec(S, 2),
                  pair(LANES), pair(LANES), pl.BlockSpec((None, None, S, 2), lambda b, hp: (b, hp, 0, 0))],
        out_specs=pair(3 * LANES),
        out_shape=jax.ShapeDtypeStruct((B, S, 3 * ATT_WIDTH), BF16),
        scratch_shapes=[vm((S, LANES), BF16), vm((S, LANES), BF16),
                        vm((S + WIN, LANES), BF16), vm((S + WIN, LANES), BF16),
                        vm((S, LANES), F32), vm((S, LANES), F32), vm((n_blk, 8, WIN), F32),
                        vm((2, 2 * KEYS, WIN), F32),
                        vm((S, LANES), F32), vm((S + WIN, 2 * LANES), F32),
                        vm((S, LANES), F32), vm((S, LANES), F32), vm((S, LANES), F32)],
        compiler_params=_params(2),
    )(slopes, qkv, qkv, qkv, o, do, lse)


SCAN_COLS = 256
SCAN_ROWS = 8


def _rows_to_tile(rows):
    rid = lax.broadcasted_iota(jnp.int32, (SCAN_ROWS, rows[0].shape[1]), 0)
    tile = jnp.broadcast_to(rows[0], rid.shape)
    for k in range(1, SCAN_ROWS):
        tile = jnp.where(rid == k, rows[k], tile)
    return tile


SCAN_UNROLL = 4


def _complex_powers(ar, ai, n):
    out = [(ar, ai)]
    for _ in range(n - 1):
        pr, pi = out[-1]
        out.append((pr * ar - pi * ai, pr * ai + pi * ar))
    return out


def _round_multipliers(powers, rid, reverse):
    out = []
    for s in (1, 2, 4):
        keep = (rid < SCAN_ROWS - s) if reverse else (rid >= s)
        out.append((jnp.where(keep, powers[s - 1][0], 0.0), jnp.where(keep, powers[s - 1][1], 0.0)))
    return out


def _tile_scan(xr, xi, multipliers, reverse):
    for s, (mr, mi) in zip((1, 2, 4), multipliers):
        shift = SCAN_ROWS - s if reverse else s
        sr, si = pltpu.roll(xr, shift, 0), pltpu.roll(xi, shift, 0)
        xr, xi = xr + (mr * sr - mi * si), xi + (mr * si + mi * sr)
    return xr, xi


SCAN_CHUNK = 256


def _scan_fwd(us, bb_big, a_row, cc_big):
    B, S, _ = us.shape
    groups = 2
    width = 2 * groups * SCAN_COLS
    nc = 2 * SSM_COLS // width
    nt = S // SCAN_ROWS
    tiles = SCAN_CHUNK // SCAN_ROWS
    LAST = slice(SCAN_ROWS - 1, SCAN_ROWS)

    def body(us_ref, bb_ref, a_ref, cc_ref, xs_ref, y_ref, bu_ref):
        bb = bb_ref[...].astype(BF16)
        for c in range(S // SCAN_CHUNK):
            part = jnp.dot(us_ref[c * SCAN_CHUNK:(c + 1) * SCAN_CHUNK, :].astype(BF16), bb,
                           preferred_element_type=F32)
            bu_ref[c * tiles:(c + 1) * tiles] = part.reshape(tiles, SCAN_ROWS, width)
        rid = lax.broadcasted_iota(jnp.int32, (SCAN_ROWS, SCAN_COLS), 0)
        consts = []
        for g in range(groups):
            re = slice(2 * g * SCAN_COLS, (2 * g + 1) * SCAN_COLS)
            im = slice((2 * g + 1) * SCAN_COLS, (2 * g + 2) * SCAN_COLS)
            powers = _complex_powers(a_ref[:, re], a_ref[:, im], SCAN_ROWS)
            carry_mult = (_rows_to_tile([p[0] for p in powers]), _rows_to_tile([p[1] for p in powers]))
            consts.append((re, im, carry_mult, _round_multipliers(powers, rid, reverse=False)))

        def tile(i, carry):
            out = []
            for (re, im, (cr_t, ci_t), rounds), (cr, ci) in zip(consts, carry):
                xr, xi = _tile_scan(bu_ref[i, :, re], bu_ref[i, :, im], rounds, reverse=False)
                xs_ref[i, :, re] = xr + (cr_t * cr - ci_t * ci)
                xs_ref[i, :, im] = xi + (cr_t * ci + ci_t * cr)
                out.append((xs_ref[i, LAST, re], xs_ref[i, LAST, im]))
            return tuple(out)

        zero = jnp.zeros((1, SCAN_COLS), F32)
        lax.fori_loop(0, nt, tile, ((zero, zero),) * groups, unroll=SCAN_UNROLL)

        @pl.when(pl.program_id(1) == 0)
        def _():
            y_ref[...] = jnp.zeros_like(y_ref)

        cc = cc_ref[...].astype(BF16)
        for c in range(S // SCAN_CHUNK):
            x2 = xs_ref[c * tiles:(c + 1) * tiles].reshape(SCAN_CHUNK, width).astype(BF16)
            y_ref[c * SCAN_CHUNK:(c + 1) * SCAN_CHUNK, :] += jnp.dot(x2, cc, preferred_element_type=F32)

    col = pl.BlockSpec((None, nt, SCAN_ROWS, width), lambda b, j: (b, 0, 0, j))
    tok = pl.BlockSpec((None, S, SSM_WIDTH), lambda b, j: (b, 0, 0))
    xs, y = pl.pallas_call(
        body, name="s5_scan_fwd", grid=(B, nc),
        in_specs=[tok, pl.BlockSpec((SSM_WIDTH, width), lambda b, j: (0, j)),
                  pl.BlockSpec((1, width), lambda b, j: (0, j)), pl.BlockSpec((width, SSM_WIDTH), lambda b, j: (j, 0))],
        out_specs=[col, tok],
        out_shape=[jax.ShapeDtypeStruct((B, nt, SCAN_ROWS, 2 * SSM_COLS), F32),
                   jax.ShapeDtypeStruct((B, S, SSM_WIDTH), F32)],
        scratch_shapes=[pltpu.VMEM((nt, SCAN_ROWS, width), F32)],
        compiler_params=_params(2),
    )(us, bb_big, a_row, cc_big)
    return xs.reshape(B, S, 2 * SSM_COLS), y


def _scan_bwd(dy, us, bb_big, cc_big, xs, a_row):
    B, S, _ = dy.shape
    width = 2 * SCAN_COLS
    nc = SSM_COLS // SCAN_COLS
    nt = S // SCAN_ROWS
    tiles = SCAN_CHUNK // SCAN_ROWS
    RE, IM = slice(0, SCAN_COLS), slice(SCAN_COLS, 2 * SCAN_COLS)
    FIRST, LAST = slice(0, 1), slice(SCAN_ROWS - 1, SCAN_ROWS)

    def body(dy_ref, us_ref, bb_ref, cc_ref, x_ref, a_ref, dus_ref, ga_ref, dbb_ref, dcc_ref, d_ref, lam_ref):
        b = pl.program_id(1)
        cc = cc_ref[...].astype(BF16)
        for c in range(S // SCAN_CHUNK):
            part = lax.dot_general(dy_ref[c * SCAN_CHUNK:(c + 1) * SCAN_CHUNK, :].astype(BF16), cc, _NT,
                                   preferred_element_type=F32)
            d_ref[c * tiles:(c + 1) * tiles] = part.reshape(tiles, SCAN_ROWS, width)
        powers = _complex_powers(a_ref[:, RE], -a_ref[:, IM], SCAN_ROWS)
        rid = lax.broadcasted_iota(jnp.int32, (SCAN_ROWS, SCAN_COLS), 0)
        cr_t = _rows_to_tile([powers[SCAN_ROWS - 1 - r][0] for r in range(SCAN_ROWS)])
        ci_t = _rows_to_tile([powers[SCAN_ROWS - 1 - r][1] for r in range(SCAN_ROWS)])
        rounds = _round_multipliers(powers, rid, reverse=True)

        @pl.when(b == 0)
        def _():
            ga_ref[...] = jnp.zeros_like(ga_ref)
            dbb_ref[...] = jnp.zeros_like(dbb_ref)
            dcc_ref[...] = jnp.zeros_like(dcc_ref)

        def tile(j, carry):
            cr, ci, accr, acci = carry
            i = nt - 1 - j
            lr, li = _tile_scan(d_ref[i, :, RE], d_ref[i, :, IM], rounds, reverse=True)
            lam_r = lr + (cr_t * cr - ci_t * ci)
            lam_i = li + (cr_t * ci + ci_t * cr)
            lam_ref[i, :, RE] = lam_r
            lam_ref[i, :, IM] = lam_i
            ip = jnp.maximum(i - 1, 0)
            keep = (i > 0).astype(F32)
            xpr = jnp.where(rid == 0, x_ref[ip, LAST, RE] * keep, pltpu.roll(x_ref[i, :, RE], 1, 0))
            xpi = jnp.where(rid == 0, x_ref[ip, LAST, IM] * keep, pltpu.roll(x_ref[i, :, IM], 1, 0))
            accr = accr + lam_r * xpr + lam_i * xpi
            acci = acci + lam_i * xpr - lam_r * xpi
            return lam_ref[i, FIRST, RE], lam_ref[i, FIRST, IM], accr, acci

        z1 = jnp.zeros((1, SCAN_COLS), F32)
        z8 = jnp.zeros((SCAN_ROWS, SCAN_COLS), F32)
        _, _, accr, acci = lax.fori_loop(0, nt, tile, (z1, z1, z8, z8), unroll=SCAN_UNROLL)
        ga_ref[:, RE] += _col_sum(accr)
        ga_ref[:, IM] += _col_sum(acci)

        bb = bb_ref[...].astype(BF16)
        for c in range(S // SCAN_CHUNK):
            rows = slice(c * SCAN_CHUNK, (c + 1) * SCAN_CHUNK)
            lam2 = lam_ref[c * tiles:(c + 1) * tiles].reshape(SCAN_CHUNK, width).astype(BF16)
            x2 = x_ref[c * tiles:(c + 1) * tiles].reshape(SCAN_CHUNK, width).astype(BF16)
            dus_ref[rows, :] = lax.dot_general(lam2, bb, _NT, preferred_element_type=F32)
            dbb_ref[...] += lax.dot_general(us_ref[rows, :].astype(BF16), lam2, _TN, preferred_element_type=F32)
            dcc_ref[...] += lax.dot_general(x2, dy_ref[rows, :].astype(BF16), _TN, preferred_element_type=F32)

    col = pl.BlockSpec((None, nt, SCAN_ROWS, width), lambda j, b: (b, 0, 0, j))
    tok = pl.BlockSpec((None, S, SSM_WIDTH), lambda j, b: (b, 0, 0))
    scratch = pltpu.VMEM((nt, SCAN_ROWS, width), F32)
    return pl.pallas_call(
        body, name="s5_scan_bwd", grid=(nc, B),
        in_specs=[tok, tok, pl.BlockSpec((SSM_WIDTH, width), lambda j, b: (0, j)),
                  pl.BlockSpec((width, SSM_WIDTH), lambda j, b: (j, 0)), col,
                  pl.BlockSpec((1, width), lambda j, b: (0, j))],
        out_specs=[pl.BlockSpec((None, None, S, SSM_WIDTH), lambda j, b: (j, b, 0, 0)),
                   pl.BlockSpec((1, width), lambda j, b: (0, j)),
                   pl.BlockSpec((SSM_WIDTH, width), lambda j, b: (0, j)),
                   pl.BlockSpec((width, SSM_WIDTH), lambda j, b: (j, 0))],
        out_shape=[jax.ShapeDtypeStruct((nc, B, S, SSM_WIDTH), F32), jax.ShapeDtypeStruct((1, 2 * SSM_COLS), F32),
                   jax.ShapeDtypeStruct((SSM_WIDTH, 2 * SSM_COLS), F32),
                   jax.ShapeDtypeStruct((2 * SSM_COLS, SSM_WIDTH), F32)],
        scratch_shapes=[scratch, scratch],
        compiler_params=_params(2),
    )(dy, us, bb_big, cc_big, xs.reshape(B, nt, SCAN_ROWS, 2 * SSM_COLS), a_row)


def _s5_discretise(lr, li, log_dt):
    dt = jnp.exp(log_dt)
    mag = jnp.exp(lr * dt)
    ang = li * dt
    ab_re, ab_im = mag * jnp.cos(ang), mag * jnp.sin(ang)
    nr, ni = ab_re - 1.0, ab_im
    den = lr * lr + li * li
    f_re = (nr * lr + ni * li) / den
    f_im = (ni * lr - nr * li) / den
    return dt, ab_re, ab_im, nr, ni, den, f_re, f_im


def _s5_params(a_re, a_im, log_dt):
    def body(lr_ref, li_ref, ld_ref, abr, abi, fr, fi):
        _, ab_re, ab_im, _, _, _, f_re, f_im = _s5_discretise(lr_ref[...], li_ref[...], ld_ref[...])
        abr[...] = ab_re
        abi[...] = ab_im
        fr[...] = f_re
        fi[...] = f_im

    return pl.pallas_call(body, name="s5_params",
                          out_shape=[jax.ShapeDtypeStruct(a_re.shape, F32)] * 4)(a_re, a_im, log_dt)


def _s5_input_matrix(f_re, f_im, b_re, b_im):
    def body(fr, fi, br, bi, o_re, o_im):
        o_re[...] = fr[...] * br[...] - fi[...] * bi[...]
        o_im[...] = fr[...] * bi[...] + fi[...] * br[...]

    return pl.pallas_call(body, name="s5_input_matrix",
                          out_shape=[jax.ShapeDtypeStruct(b_re.shape, F32)] * 2)(f_re, f_im, b_re, b_im)


def _s5_input_matrix_bwd(f_re, f_im, b_re, b_im, g_re, g_im):
    def body(fr, fi, br, bi, gr, gi, dbr, dbi, dfr, dfi):
        dbr[...] = fr[...] * gr[...] + fi[...] * gi[...]
        dbi[...] = fr[...] * gi[...] - fi[...] * gr[...]
        dfr[...] = jnp.sum(br[...] * gr[...] + bi[...] * gi[...], axis=1, keepdims=True)
        dfi[...] = jnp.sum(br[...] * gi[...] - bi[...] * gr[...], axis=1, keepdims=True)

    return pl.pallas_call(
        body, name="s5_input_matrix_bwd",
        out_shape=[jax.ShapeDtypeStruct(b_re.shape, F32)] * 2 + [jax.ShapeDtypeStruct(f_re.shape, F32)] * 2,
    )(f_re, f_im, b_re, b_im, g_re, g_im)


def _s5_params_bwd(a_re, a_im, log_dt, g_ab_re, g_ab_im, d_f_re, d_f_im):
    def body(lr_ref, li_ref, ld_ref, gar, gai, dfr, dfi, o_lr, o_li, o_ld):
        lr, li = lr_ref[...], li_ref[...]
        dt, ab_re, ab_im, nr, ni, den, f_re, f_im = _s5_discretise(lr, li, ld_ref[...])
        d_fr, d_fi = dfr[...], dfi[...]
        d_nr = (d_fr * lr - d_fi * li) / den
        d_ni = (d_fr * li + d_fi * lr) / den
        common = (d_fr * f_re + d_fi * f_im) * 2.0 / den
        d_lr = (d_fr * nr + d_fi * ni) / den - common * lr
        d_li = (d_fr * ni - d_fi * nr) / den - common * li
        d_abr = gar[...] + d_nr
        d_abi = gai[...] + d_ni
        d_mag_mag = d_abr * ab_re + d_abi * ab_im
        d_ang = d_abi * ab_re - d_abr * ab_im
        o_lr[...] = d_lr + d_mag_mag * dt
        o_li[...] = d_li + d_ang * dt
        o_ld[...] = jnp.sum(d_mag_mag * lr + d_ang * li, axis=1, keepdims=True) * dt

    return pl.pallas_call(
        body, name="s5_params_bwd",
        out_shape=[jax.ShapeDtypeStruct(a_re.shape, F32)] * 2 + [jax.ShapeDtypeStruct(log_dt.shape, F32)],
    )(a_re, a_im, log_dt, g_ab_re, g_ab_im, d_f_re, d_f_im)


CONV_COLS = 256


def _shift_down(v, j, row):
    return jnp.where(row >= j, pltpu.roll(v, j, 0), 0.0)


def _shift_up(v, j, row, seq):
    return jnp.where(row < seq - j, pltpu.roll(v, seq - j, 0), 0.0)


def _up_conv_fwd(u, w3, w_conv, b_conv):
    B, S, K = u.shape

    def body(u_ref, w_ref, wc_ref, bc_ref, up_ref, ff_ref):
        uv = u_ref[...]
        row = lax.broadcasted_iota(jnp.int32, (S, CONV_COLS), 0)
        for half in range(2):
            cols = slice(half * HALF, (half + 1) * HALF)
            pair = jnp.dot(uv, jnp.concatenate([w_ref[0, :, cols], w_ref[1, :, cols]], axis=1),
                           preferred_element_type=F32)
            up_ref[:, half * 2 * HALF:(half + 1) * 2 * HALF] = pair.astype(up_ref.dtype)
            a, val = pair[:, :HALF], pair[:, HALF:]
            conv = (bc_ref[:, cols] + wc_ref[0:1, cols] * a + wc_ref[1:2, cols] * _shift_down(a, 1, row)
                    + wc_ref[2:3, cols] * _shift_down(a, 2, row))
            ff_ref[:, cols] = (conv * _sigmoid(conv) * val).astype(ff_ref.dtype)

    return pl.pallas_call(
        body, name="ffn_up_conv_gate", grid=(UP_SLOTS, B),
        in_specs=[pl.BlockSpec((None, S, K), lambda j, b: (b, 0, 0)), _up_weight_spec(K, lambda j, b: (0, j, 0, 0)),
                  pl.BlockSpec((3, 2 * HALF), lambda j, b: (0, j)), pl.BlockSpec((1, 2 * HALF), lambda j, b: (0, j))],
        out_specs=[pl.BlockSpec((None, S, UP_GROUP), lambda j, b: (b, 0, j)),
                   pl.BlockSpec((None, S, 2 * HALF), lambda j, b: (b, 0, j))],
        out_shape=[jax.ShapeDtypeStruct((B, S, UP_SLOTS * UP_GROUP), BF16), jax.ShapeDtypeStruct((B, S, D_FF), BF16)],
        compiler_params=_params(2),
    )(u, w3.reshape(2, UP_SLOTS, K, 2 * HALF), w_conv, b_conv)


def _conv_bwd(up, d_down, w_down, w_conv, b_conv):
    B, S, _ = up.shape
    nj = D_FF // CONV_COLS

    def body(up_ref, dd_ref, wd_ref, w_ref, b_ref, dup_ref, dw_ref, db_ref):
        b = pl.program_id(1)
        a = up_ref[:, :CONV_COLS].astype(F32)
        val = up_ref[:, CONV_COLS:].astype(F32)
        row = lax.broadcasted_iota(jnp.int32, a.shape, 0)
        w0, w1, w2 = w_ref[0:1, :], w_ref[1:2, :], w_ref[2:3, :]
        a1, a2 = _shift_down(a, 1, row), _shift_down(a, 2, row)
        conv = b_ref[...] + w0 * a + w1 * a1 + w2 * a2
        sg = _sigmoid(conv)
        dff = lax.dot_general(dd_ref[...], wd_ref[...], _NT, preferred_element_type=F32)
        d_val = dff * conv * sg
        dc = dff * val * (sg * (1.0 + conv * (1.0 - sg)))
        d_a = w0 * dc + w1 * _shift_up(dc, 1, row, S) + w2 * _shift_up(dc, 2, row, S)
        dup_ref[:, :CONV_COLS] = d_a.astype(dup_ref.dtype)
        dup_ref[:, CONV_COLS:] = d_val.astype(dup_ref.dtype)

        @pl.when(b == 0)
        def _():
            dw_ref[...] = jnp.zeros_like(dw_ref)
            db_ref[...] = jnp.zeros_like(db_ref)

        dw_ref[0:1, :] += _col_sum(dc * a)
        dw_ref[1:2, :] += _col_sum(dc * a1)
        dw_ref[2:3, :] += _col_sum(dc * a2)
        db_ref[...] += _col_sum(dc)

    return pl.pallas_call(
        body, name="conv_gate_bwd", grid=(nj, B),
        in_specs=[pl.BlockSpec((None, S, 2 * CONV_COLS), lambda j, b: (b, 0, j)),
                  pl.BlockSpec((None, S, D_MODEL), lambda j, b: (b, 0, 0)),
                  pl.BlockSpec((CONV_COLS, D_MODEL), lambda j, b: (j, 0)),
                  pl.BlockSpec((3, CONV_COLS), lambda j, b: (0, j)),
                  pl.BlockSpec((1, CONV_COLS), lambda j, b: (0, j))],
        out_specs=[pl.BlockSpec((None, S, 2 * CONV_COLS), lambda j, b: (b, 0, j)),
                   pl.BlockSpec((3, CONV_COLS), lambda j, b: (0, j)),
                   pl.BlockSpec((1, CONV_COLS), lambda j, b: (0, j))],
        out_shape=[jax.ShapeDtypeStruct((B, S, 2 * D_FF), BF16), jax.ShapeDtypeStruct((3, D_FF), F32),
                   jax.ShapeDtypeStruct((1, D_FF), F32)],
        compiler_params=_params(2),
    )(up, d_down, w_down, w_conv, b_conv)


def _ada_fwd(c_all, w_ada, b_ada):
    def body(c_ref, w_ref, b_ref, o_ref):
        cv = c_ref[...]
        act = (cv * _sigmoid(cv)).astype(BF16)
        o_ref[...] = jnp.dot(act, w_ref[...].astype(BF16), preferred_element_type=F32) + b_ref[...]

    return pl.pallas_call(body, name="ada_fwd",
                          out_shape=jax.ShapeDtypeStruct((c_all.shape[0], w_ada.shape[1]), F32),
                          compiler_params=pltpu.CompilerParams(vmem_limit_bytes=V7X_VMEM_LIMIT))(c_all, w_ada, b_ada)


def _ada_bwd(c_all, dmod_all, dmod_cols):
    def body(c_ref, dm_ref, dmc_ref, dw_ref, db_ref):
        cv = c_ref[...]
        act = (cv * _sigmoid(cv)).astype(BF16)
        dw_ref[...] = lax.dot_general(act, dmc_ref[...].astype(BF16), _TN, preferred_element_type=F32)
        db_ref[...] = _col_sum(dm_ref[...])

    return pl.pallas_call(
        body, name="ada_bwd",
        out_shape=[jax.ShapeDtypeStruct((c_all.shape[1], dmod_cols.shape[1]), F32),
                   jax.ShapeDtypeStruct((1, dmod_all.shape[1]), F32)],
        compiler_params=pltpu.CompilerParams(vmem_limit_bytes=V7X_VMEM_LIMIT))(c_all, dmod_all, dmod_cols)


def _adamw(w, m, v, g_parts, name, own=None):
    R, C = w.shape
    P = g_parts.shape[0]
    tr = R
    for cand in (256, 128, 64, 32, 16, 8):
        if R % cand == 0 and cand * C * 4 * (P + 8) * 2 <= V7X_VMEM_LIMIT // 2:
            tr = cand
            break
    c1 = 1.0 / (1.0 - ADAM_B1 ** ADAM_STEP)
    c2 = 1.0 / (1.0 - ADAM_B2 ** ADAM_STEP)

    def update(w_ref, m_ref, v_ref, g, og, od, om, ov):
        m_new = ADAM_B1 * m_ref[...] + (1.0 - ADAM_B1) * g
        v_new = ADAM_B2 * v_ref[...] + (1.0 - ADAM_B2) * (g * g)
        og[...] = g
        om[...] = m_new
        ov[...] = v_new
        od[...] = -ADAM_LR * ((m_new * c1) / (jnp.sqrt(v_new * c2) + ADAM_EPS) + ADAM_WD * w_ref[...])

    def total(g_ref):
        g = g_ref[0].astype(F32)
        for p in range(1, P):
            g = g + g_ref[p].astype(F32)
        return g

    out_shape = [jax.ShapeDtypeStruct((R, C), F32)] * 4
    if own is None:
        def body(w_ref, m_ref, v_ref, g_ref, og, od, om, ov):
            update(w_ref, m_ref, v_ref, total(g_ref), og, od, om, ov)

        spec = pl.BlockSpec((tr, C), lambda i: (i, 0))
        return pl.pallas_call(
            body, name=name, grid=(R // tr,),
            in_specs=[spec, spec, spec, pl.BlockSpec((P, tr, C), lambda i: (0, i, 0))],
            out_specs=[spec] * 4, out_shape=out_shape, compiler_params=_params(1),
        )(w, m, v, g_parts)

    slots, me = own

    def body_own(me_ref, w_ref, m_ref, v_ref, g_ref, own_ref, og, od, om, ov):
        g = own_ref[...].astype(F32)
        for p in range(P):
            g = g + jnp.where(me_ref[0] == p, 0.0, g_ref[p].astype(F32))
        update(w_ref, m_ref, v_ref, g, og, od, om, ov)

    spec = pl.BlockSpec((tr, C), lambda i, me_ref: (i, 0))
    grid_spec = pltpu.PrefetchScalarGridSpec(
        num_scalar_prefetch=1, grid=(R // tr,),
        in_specs=[spec, spec, spec, pl.BlockSpec((P, tr, C), lambda i, me_ref: (0, i, 0)),
                  pl.BlockSpec((None, tr, C), lambda i, me_ref: (me_ref[0], i, 0))],
        out_specs=[spec] * 4)
    return pl.pallas_call(body_own, name=name, grid_spec=grid_spec, out_shape=out_shape,
                          compiler_params=_params(1))(me, w, m, v, g_parts, slots)


def _adamw_small(ws, ms, vs, gs):
    n = len(ws)
    c1 = 1.0 / (1.0 - ADAM_B1 ** ADAM_STEP)
    c2 = 1.0 / (1.0 - ADAM_B2 ** ADAM_STEP)

    def body(*refs):
        ins, outs = refs[:4 * n], refs[4 * n:]
        for i in range(n):
            w, m, v, g = ins[i][...], ins[n + i][...], ins[2 * n + i][...], ins[3 * n + i][...]
            m_new = ADAM_B1 * m + (1.0 - ADAM_B1) * g
            v_new = ADAM_B2 * v + (1.0 - ADAM_B2) * (g * g)
            outs[4 * i][...] = g
            outs[4 * i + 1][...] = -ADAM_LR * ((m_new * c1) / (jnp.sqrt(v_new * c2) + ADAM_EPS) + ADAM_WD * w)
            outs[4 * i + 2][...] = m_new
            outs[4 * i + 3][...] = v_new

    out_shape = [jax.ShapeDtypeStruct(w.shape, F32) for w in ws for _ in range(4)]
    return pl.pallas_call(body, name="adamw_small", out_shape=out_shape,
                          compiler_params=pltpu.CompilerParams(vmem_limit_bytes=V7X_VMEM_LIMIT))(*ws, *ms, *vs, *gs)


def _sum_parts(parts, loss_rows):
    P, R, C = parts.shape
    lo, hi = loss_rows

    def body(p_ref, o_ref, loss_ref):
        t = p_ref[0]
        for p in range(1, P):
            t = t + p_ref[p]
        o_ref[...] = t
        tot = jnp.sum(jnp.sum(o_ref[lo:hi, :], axis=1, keepdims=True), axis=0, keepdims=True)
        loss_ref[...] = jnp.broadcast_to(tot, loss_ref.shape)

    return pl.pallas_call(body, name="sum_small_grads",
                          out_shape=[jax.ShapeDtypeStruct((R, C), F32), jax.ShapeDtypeStruct((1, LANES), F32)],
                          compiler_params=pltpu.CompilerParams(vmem_limit_bytes=V7X_VMEM_LIMIT))(parts)


def _exchange(items, name):
    n = len(items)
    MESH = pl.DeviceIdType.MESH

    def body(*refs):
        src, dst = refs[:n], refs[n:2 * n]
        send_sems, recv_sems, local_sems = refs[2 * n:]
        x, y, c = lax.axis_index("x"), lax.axis_index("y"), lax.axis_index("c")
        me = 4 * x + 2 * y + c
        started = []
        for it, (_, per_peer) in enumerate(items):
            own = pltpu.make_async_copy(src[it].at[me] if per_peer else src[it], dst[it].at[me], local_sems.at[it])
            own.start()
            started.append(own)
        sends, recvs = [], []
        for k in range(1, N_DEV):
            px = 1 - x if k & 4 else x
            py = 1 - y if k & 2 else y
            pc = 1 - c if k & 1 else c
            peer = 4 * px + 2 * py + pc
            for it, (_, per_peer) in enumerate(items):
                s = src[it].at[peer] if per_peer else src[it]
                cp = pltpu.make_async_remote_copy(src_ref=s, dst_ref=dst[it].at[me], send_sem=send_sems.at[it, k - 1],
                                                  recv_sem=recv_sems.at[it, k - 1], device_id=(px, py, pc),
                                                  device_id_type=MESH)
                cp.start()
                sends.append(cp)
                recvs.append(pltpu.make_async_remote_copy(
                    src_ref=s, dst_ref=dst[it].at[peer], send_sem=send_sems.at[it, k - 1],
                    recv_sem=recv_sems.at[it, k - 1], device_id=(px, py, pc), device_id_type=MESH))
        for cp in recvs:
            cp.wait_recv()
        for cp in sends:
            cp.wait_send()
        for cp in started:
            cp.wait()

    any_spec = pl.BlockSpec(memory_space=pl.ANY)
    out_shape = []
    for a, per_peer in items:
        shp = a.shape if per_peer else (N_DEV,) + a.shape
        out_shape.append(jax.ShapeDtypeStruct(shp, a.dtype))
    return pl.pallas_call(
        body, name=name, in_specs=[any_spec] * n, out_specs=[any_spec] * n, out_shape=out_shape,
        scratch_shapes=[pltpu.SemaphoreType.DMA((n, N_DEV - 1)), pltpu.SemaphoreType.DMA((n, N_DEV - 1)),
                        pltpu.SemaphoreType.DMA((n,))],
    )(*[a for a, _ in items])


def _remote(src, dst, send_sem, recv_sem, device):
    return pltpu.make_async_remote_copy(src_ref=src, dst_ref=dst, send_sem=send_sem, recv_sem=recv_sem,
                                        device_id=device, device_id_type=pl.DeviceIdType.MESH)


def _mesh_place():
    x, y, c = lax.axis_index("x"), lax.axis_index("y"), lax.axis_index("c")
    other_chips = [(1 - x, y), (x, 1 - y), (1 - x, 1 - y)]
    return x, y, c, (x, y, 1 - c), other_chips


def _peers():
    x, y, c = lax.axis_index("x"), lax.axis_index("y"), lax.axis_index("c")
    out = []
    for k in range(1, N_DEV):
        px = 1 - x if k & 4 else x
        py = 1 - y if k & 2 else y
        pc = 1 - c if k & 1 else c
        out.append((k, (px, py, pc), 4 * px + 2 * py + pc))
    return 4 * x + 2 * y + c, out


FIRST_LEVEL = (1, 2, 4, 6)


def _exchange_start(items, name, gather, carry=(), first_level=False):
    n, m = len(items), len(carry)

    def body(*refs):
        src, land = refs[:n], refs[n:2 * n]
        first_out = 2 * n + m
        send_sems, recv_sems = refs[first_out:first_out + n], refs[first_out + n:first_out + 2 * n]
        token = refs[-1]
        me, peers = _peers()
        if first_level:
            peers = [p for p in peers if p[0] in FIRST_LEVEL]
        for k, peer, slot in peers:
            for it in range(n):
                _remote(src[it] if gather else src[it].at[slot], land[it].at[me], send_sems[it], recv_sems[it],
                        peer).start()
        token[...] = jnp.zeros_like(token)

    hbm = pl.BlockSpec(memory_space=pltpu.HBM)
    sem = pl.BlockSpec(memory_space=pltpu.SEMAPHORE)
    land_shapes = [(N_DEV,) + (a.shape if gather else a.shape[1:]) for a in items]
    lands = [lax.empty(shp, a.dtype) for shp, a in zip(land_shapes, items)]
    through = list(items) + lands + list(carry)
    outs = pl.pallas_call(
        body, name=name,
        out_shape=(*[pltpu.SemaphoreType.DMA(())] * (2 * n), *[pltpu.HBM(a.shape, a.dtype) for a in through],
                   jax.ShapeDtypeStruct((8, LANES), F32)),
        in_specs=[hbm] * len(through),
        out_specs=(*[sem] * (2 * n), *[hbm] * len(through), pl.BlockSpec(memory_space=pltpu.VMEM)),
        input_output_aliases={i: 2 * n + i for i in range(len(through))},
        compiler_params=pltpu.CompilerParams(has_side_effects=pltpu.SideEffectType.DATAFLOW_SIDE_EFFECTING),
    )(*[pltpu.with_memory_space_constraint(a, pltpu.HBM) for a in through])
    return (list(outs[:n]), list(outs[n:2 * n]), list(outs[2 * n:3 * n]), list(outs[3 * n:4 * n]), outs[-1],
            list(outs[4 * n:4 * n + m]))


def _exchange_wait(send_sems, recv_sems, items, lands, after, name, copies=N_DEV - 1):
    n = len(items)

    def body(*refs):
        land = refs[n:2 * n]
        send_sems, recv_sems = refs[2 * n:3 * n], refs[3 * n:4 * n]
        me, peers = _peers()
        for it in range(n):
            seven = land[it].at[pl.ds(0, copies)]
            cp = _remote(seven, seven, send_sems[it], recv_sems[it], peers[0][1])
            cp.wait_send()
            cp.wait_recv()

    hbm = pl.BlockSpec(memory_space=pltpu.HBM)
    sem = pl.BlockSpec(memory_space=pltpu.SEMAPHORE)
    outs = pl.pallas_call(
        body, name=name,
        out_shape=tuple(pltpu.HBM(a.shape, a.dtype) for a in list(items) + list(lands)),
        in_specs=[hbm] * (2 * n) + [sem] * (2 * n) + [pl.BlockSpec(memory_space=pl.ANY)],
        out_specs=tuple([hbm] * (2 * n)),
        input_output_aliases={i: i for i in range(2 * n)},
        compiler_params=pltpu.CompilerParams(has_side_effects=pltpu.SideEffectType.DATAFLOW_SIDE_EFFECTING),
    )(*items, *lands, *send_sems, *recv_sems, after)
    return list(outs[:n]), list(outs[n:])


def _gather_finish(lands, name):
    n = len(lands)

    def body(*refs):
        land = refs[n:2 * n]
        pass_send, pass_recv = refs[-2], refs[-1]
        x, y, c, sibling, chips = _mesh_place()
        passed = []
        for j, (px, py) in enumerate(chips):
            for it in range(n):
                blk = land[it].at[4 * px + 2 * py + c]
                cp = _remote(blk, blk, pass_send.at[it, j], pass_recv.at[it, j], sibling)
                cp.start()
                passed.append(cp)
        for j, (px, py) in enumerate(chips):
            for it in range(n):
                blk = land[it].at[4 * px + 2 * py + 1 - c]
                _remote(blk, blk, pass_send.at[it, j], pass_recv.at[it, j], sibling).wait_recv()
        for cp in passed:
            cp.wait_send()

    any_spec = pl.BlockSpec(memory_space=pl.ANY)
    return pl.pallas_call(
        body, name=name, in_specs=[any_spec] * n, out_specs=[any_spec] * n,
        out_shape=[jax.ShapeDtypeStruct(a.shape, a.dtype) for a in lands],
        scratch_shapes=[pltpu.SemaphoreType.DMA((n, 3)), pltpu.SemaphoreType.DMA((n, 3))],
        input_output_aliases={i: i for i in range(n)},
    )(*lands)


def _gelu_tanh(y):
    k = math.sqrt(2.0 / math.pi)
    t = jnp.tanh(k * (y + 0.044715 * y * y * y))
    return 0.5 * y * (1.0 + t), t


def _local_step(x, mod, target, W, late_weights, P, send_early):
    B, S, D = x.shape
    T = B * S
    TS = 512
    flat = lambda a: a.reshape(T, a.shape[-1])
    unflat = lambda a: a.reshape(B, S, a.shape[-1])
    mod_col = lambda i: (mod, D, i)

    def f_modnorm_project(xv, sc, sh, g, wq, wu, wg):
        u = ((xv * _rms_scale(xv) * g) * (1.0 + sc) + sh).astype(BF16)
        return (u, lax.dot_general(u, wq, _NT, preferred_element_type=F32),
                lax.dot_general(u, wu, _NT, preferred_element_type=F32),
                lax.dot_general(u, wg, _NT, preferred_element_type=F32))

    u1, qkv, us, gates = _rowwise(
        f_modnorm_project, [(x, D, 0)], [mod_col(1), mod_col(0)], [P["g_mix"], W["w_qkv"], W["w_us"], W["w_gates"]],
        [(D, BF16), (3 * ATT_WIDTH, F32), (SSM_WIDTH, F32), (2 * D, BF16)], [], [], ts=TS, name="modnorm_project_in")
    u1f = flat(u1)

    o_att, lse = _attention_fwd(qkv, P["slopes"])
    more_w, more_p = late_weights(o_att)
    W, P = {**W, **more_w}, {**P, **more_p}

    xs, y_mm = _scan_fwd(us, P["bb_big"], P["a_row"], P["cc_big"])

    bga, bgs = P["b_gate"][:, :D], P["b_gate"][:, D:]

    def f_mixer_tail(ov, ymm, usv, ga, gs, xv, gt, sc, sh, w_att, w_ssm, w_o, bga_, bgs_, g, dsk, wg, bg):
        yv = ymm + dsk * usv
        ge, _ = _gelu_tanh(yv)
        zv = (ge * _sigmoid(jnp.dot(ge.astype(BF16), wg, preferred_element_type=F32) + bg)).astype(BF16)
        ya = jnp.dot(ov, w_att, preferred_element_type=F32)
        ys = jnp.dot(zv, w_ssm, preferred_element_type=F32)
        mg = (_sigmoid(ga + bga_) * ya + _sigmoid(gs + bgs_) * ys).astype(BF16)
        mx = jnp.dot(mg, w_o, preferred_element_type=F32)
        h = xv + gt * mx
        return yv, zv, mg, mx, h, (h * _rms_scale(h) * g) * (1.0 + sc) + sh

    y_s5, z, merged, mix, h1, u2 = _rowwise(
        f_mixer_tail,
        [(o_att, ATT_WIDTH, 0), (y_mm, SSM_WIDTH, 0), (us, SSM_WIDTH, 0), (gates, D, 0), (gates, D, 1), (x, D, 0)],
        [mod_col(2), mod_col(4), mod_col(3)],
        [W["w_proj_att"], W["w_proj_ssm"], W["w_out"], bga, bgs, P["g_ffn"], P["d_skip"], W["w_glu"], P["b_glu"]],
        [(SSM_WIDTH, F32), (SSM_WIDTH, BF16), (D, BF16), (D, BF16), (D, F32), (D, BF16)],
        [], [], ts=TS, raw=(0,), name="mixer_tail")

    up, ff = _up_conv_fwd(u2, W["w_up"], P["w_conv"], P["b_conv"])

    def f_head(ffv, h1v, tg, gt, g, w_dn):
        dn = jnp.dot(ffv, w_dn, preferred_element_type=F32)
        h2 = h1v + gt * dn
        r = _rms_scale(h2)
        nh = h2 * r
        e = nh * g - tg
        dy = e * (1.0 / D)
        gy = dy * g
        dh = r * (gy - nh * jnp.mean(gy * nh, axis=-1, keepdims=True))
        return (dh, dh * gt, _col_sum(dh * dn), _col_sum(dy * nh), _col_sum(e * e) * (0.5 / D))

    dh2, d_down, d_gt2, d_g_final, loss_cols = _rowwise(
        f_head, [(ff, D_FF, 0), (h1, D, 0), (target, D, 0)], [mod_col(5)], [P["g_final"], W["w_down"]],
        [(D, BF16), (D, BF16)], [D], [(1, D), (1, D)], ts=TS, raw=(0,), name="ffn_down_head_loss")

    d_downf = flat(d_down)
    d_w_down = _matmul(flat(ff), d_downf, ta=True, out_dtype=BF16, name="ffn_down_dw")
    d_up, d_w_conv, d_b_conv = _conv_bwd(up, d_down, W["w_down"], P["w_conv"], P["b_conv"])
    d_upf = flat(d_up)
    d_w_up = _up_dw(flat(u2), d_upf, name="ffn_up_dw")
    token, _ = send_early(dict(w_down=d_w_down.reshape(N_DEV, D_FF // N_DEV, D), w_up=d_w_up))
    g_ffn_after = P["g_ffn"] + token[0:1, 0:1]

    def f_up_back_modnorm(dup, h, dres, mx, sc, gt, g, w2):
        du = None
        for j in range(UP_SLOTS):
            wa, wv = w2[j * D:(j + 1) * D, :], w2[(j + UP_SLOTS) * D:(j + UP_SLOTS + 1) * D, :]
            wj = jnp.concatenate([wa[:, :HALF], wv[:, :HALF], wa[:, HALF:], wv[:, HALF:]], axis=1)
            part = lax.dot_general(dup[:, j * UP_GROUP:(j + 1) * UP_GROUP], wj, _NT, preferred_element_type=F32)
            du = part if du is None else du + part
        r = _rms_scale(h)
        nh = h * r
        dn = du * (1.0 + sc)
        gy = dn * g
        dh = dres + r * (gy - nh * jnp.mean(gy * nh, axis=-1, keepdims=True))
        return (dh, dh * gt, _col_sum(du), _col_sum(du * nh * g), _col_sum(dh * mx), _col_sum(dn * nh))

    dh1, d_mix, d_sh2, d_sc2, d_gt1, d_g_ffn = _rowwise(
        f_up_back_modnorm, [(d_up, 2 * D_FF, 0), (h1, D, 0), (dh2, D, 0), (mix, D, 0)], [mod_col(4), mod_col(2)],
        [g_ffn_after, W["w_up"].reshape(N_DEV * D, 2 * HALF)],
        [(D, BF16), (D, BF16)], [D, D, D], [(1, D)], ts=TS, raw=(0,),
        name="ffn_up_back_modnorm", split=2)

    d_mixf = flat(d_mix)
    d_w_out = _matmul(flat(merged), d_mixf, ta=True, out_dtype=BF16, name="proj_out_dw")

    def f_mixer_tail_bwd(dmx, ov, zv, ga, gs, yv, usv, w_o, w_att, w_ssm, bga_, bgs_, dsk, wg, bg):
        dm = lax.dot_general(dmx, w_o, _NT, preferred_element_type=F32)
        ya = jnp.dot(ov, w_att, preferred_element_type=F32)
        ys = jnp.dot(zv, w_ssm, preferred_element_type=F32)
        sa, ss = _sigmoid(ga + bga_), _sigmoid(gs + bgs_)
        dga = dm * ya * sa * (1.0 - sa)
        dgs = dm * ys * ss * (1.0 - ss)
        dya, dys = (dm * sa).astype(BF16), (dm * ss).astype(BF16)
        d_o = lax.dot_general(dya, w_att, _NT, preferred_element_type=F32)
        dz = lax.dot_general(dys, w_ssm, _NT, preferred_element_type=F32)
        ge, t = _gelu_tanh(yv)
        sg = _sigmoid(jnp.dot(ge.astype(BF16), wg, preferred_element_type=F32) + bg)
        dpre = dz * ge * sg * (1.0 - sg)
        dge = dz * sg + lax.dot_general(dpre.astype(BF16), wg, _NT, preferred_element_type=F32)
        k = math.sqrt(2.0 / math.pi)
        dgelu = 0.5 * (1.0 + t) + 0.5 * yv * (1.0 - t * t) * k * (1.0 + 3.0 * 0.044715 * yv * yv)
        dy = dge * dgelu
        dwg = lax.dot_general(ge.astype(BF16), dpre.astype(BF16), _TN, preferred_element_type=F32)
        return (dya, dys, jnp.concatenate([dga, dgs], axis=1), d_o, dy, dy * dsk,
                _col_sum(dga), _col_sum(dgs), dwg, _col_sum(dpre), _col_sum(dy * usv))

    (d_y_att, d_y_ssm, d_gates, d_o_att, d_y_s5, d_us_skip, d_bga, d_bgs, d_w_glu, d_b_glu, d_d_skip) = _rowwise(
        f_mixer_tail_bwd,
        [(d_mix, D, 0), (o_att, ATT_WIDTH, 0), (z, SSM_WIDTH, 0), (gates, D, 0), (gates, D, 1),
         (y_s5, SSM_WIDTH, 0), (us, SSM_WIDTH, 0)], [],
        [W["w_out"], W["w_proj_att"], W["w_proj_ssm"], bga, bgs, P["d_skip"], W["w_glu"], P["b_glu"]],
        [(D, BF16), (D, BF16), (2 * D, BF16), (ATT_WIDTH, F32), (SSM_WIDTH, BF16), (SSM_WIDTH, F32)], [],
        [(1, D), (1, D), (SSM_WIDTH, SSM_WIDTH), (1, SSM_WIDTH), (1, SSM_WIDTH)], ts=TS, raw=(0, 1, 2),
        name="mixer_tail_bwd")

    d_yaf, d_ysf = flat(d_y_att), flat(d_y_ssm)
    d_w_proj_att = _matmul(flat(o_att), d_yaf, ta=True, out_dtype=BF16, name="proj_att_dw")
    d_w_proj_ssm = _matmul(flat(z), d_ysf, ta=True, out_dtype=BF16, name="proj_ssm_dw")
    d_us_parts, g_ab, d_bb, d_cc = _scan_bwd(d_y_s5, us, P["bb_big"], P["cc_big"], xs, P["a_row"])

    token, _ = send_early(dict(
        w_out=d_w_out.reshape(N_DEV, D // N_DEV, D), w_proj_att=_cols_to_slots(d_w_proj_att),
        w_proj_ssm=_cols_to_slots(d_w_proj_ssm),
        w_glu=d_w_glu.astype(BF16).reshape(N_DEV, SSM_WIDTH // N_DEV, SSM_WIDTH),
        w_conv=_cols_to_slots(d_w_conv.astype(BF16))))
    d_qkv = _attention_bwd(qkv, o_att, d_o_att, lse, P["slopes"] + token[0, 0])

    def f_add(*parts):
        return sum(parts[1:], parts[0])

    n_parts = d_us_parts.shape[0]
    stacked = d_us_parts.reshape(n_parts * B, S, SSM_WIDTH)
    (d_us,) = _rowwise(f_add, [(d_us_skip, SSM_WIDTH, 0)] + [(stacked, SSM_WIDTH, 0, j * B) for j in range(n_parts)],
                       [], [],
                       [(SSM_WIDTH, BF16)], [], [], ts=TS, name="s5_input_grad")
    d_qkvf = flat(d_qkv)
    d_usf = flat(d_us)
    d_gatesf = flat(d_gates)
    d_w_in_t = jnp.concatenate(
        [_unpair_qkv_rows(_matmul(d_qkvf, u1f, ta=True, out_dtype=BF16, name="proj_qkv_dw")),
         _matmul(d_usf, u1f, ta=True, out_dtype=BF16, name="proj_ssm_in_dw"),
         _matmul(d_gatesf, u1f, ta=True, out_dtype=BF16, name="proj_gates_dw")], axis=0)
    token, (w_qkv, w_us, w_gates) = send_early(dict(w_in=d_w_in_t.reshape(N_DEV, -1, D)),
                                               carry=[W["w_qkv"], W["w_us"], W["w_gates"]])
    def f_project_back_modnorm(dq, du_, dg, h, dres, sc, g, wq, wu, wg):
        du = (jnp.dot(dq, wq, preferred_element_type=F32) + jnp.dot(du_, wu, preferred_element_type=F32)
              + jnp.dot(dg, wg, preferred_element_type=F32))
        r = _rms_scale(h)
        nh = h * r
        dn = du * (1.0 + sc)
        gy = dn * g
        dh = dres + r * (gy - nh * jnp.mean(gy * nh, axis=-1, keepdims=True))
        return (dh, _col_sum(du), _col_sum(du * nh * g), _col_sum(dn * nh))

    grad_x, d_sh1, d_sc1, d_g_mix = _rowwise(
        f_project_back_modnorm,
        [(d_qkv, 3 * ATT_WIDTH, 0), (d_us, SSM_WIDTH, 0), (d_gates, 2 * D, 0), (x, D, 0), (dh1, D, 0)], [mod_col(1)],
        [P["g_mix"] + token[0:1, 0:1], w_qkv, w_us, w_gates],
        [(D, F32)], [D, D], [(1, D)], ts=TS, raw=(0, 1, 2),
        name="project_in_back_modnorm", split=2)

    d_mod = jnp.concatenate([d_sh1, d_sc1, d_gt1, d_sh2, d_sc2, d_gt2], axis=-1)
    g_ab_re, g_ab_im = _deinterleave(g_ab)
    d_bb_re, d_bb_im = _deinterleave(d_bb)
    d_cc_re, d_cc_im = (t.T for t in _deinterleave(d_cc.T))
    small = dict(g_mix=d_g_mix, b_gate=jnp.concatenate([d_bga, d_bgs], axis=1), g_ab_re=g_ab_re, g_ab_im=g_ab_im,
                 d_bb_re=d_bb_re, d_bb_im=d_bb_im, d_cc_re=d_cc_re, d_cc_im=d_cc_im, d_skip=d_d_skip,
                 b_glu=d_b_glu, g_ffn=d_g_ffn, b_conv=d_b_conv, g_final=d_g_final, loss_cols=loss_cols)
    return grad_x, d_mod, small


def _block_diag_in(bb):
    t = bb.reshape(SSM_GROUPS, SSM_STATE, SSM_GROUP_CH)
    eye = jnp.eye(SSM_GROUPS, dtype=bb.dtype)
    return jnp.einsum("gnc,gh->gchn", t, eye).reshape(SSM_WIDTH, SSM_COLS)


def _block_diag_out(cm):
    eye = jnp.eye(SSM_GROUPS, dtype=cm.dtype)
    return jnp.einsum("gcn,gh->gnhc", cm, eye).reshape(SSM_COLS, SSM_WIDTH)


def _diag_blocks_in(m):
    t = m.reshape(SSM_GROUPS, SSM_GROUP_CH, SSM_GROUPS, SSM_STATE)
    idx = jnp.arange(SSM_GROUPS)
    return t[idx, :, idx, :].transpose(0, 2, 1).reshape(SSM_COLS, SSM_GROUP_CH)


def _diag_blocks_out(m):
    t = m.reshape(SSM_GROUPS, SSM_STATE, SSM_GROUPS, SSM_GROUP_CH)
    idx = jnp.arange(SSM_GROUPS)
    return t[idx, :, idx, :].transpose(0, 2, 1)


def _pair_qkv_rows(w):
    return w.reshape(3, N_HEADS // 2, LANES, w.shape[1]).swapaxes(0, 1).reshape(w.shape)


def _unpair_qkv_rows(w):
    return w.reshape(N_HEADS // 2, 3, LANES, w.shape[1]).swapaxes(0, 1).reshape(w.shape)


def _interleave(re, im):
    lead = re.shape[:-1]
    g = lambda a: a.reshape(lead + (SSM_COLS // SCAN_COLS, 1, SCAN_COLS))
    return jnp.concatenate([g(re), g(im)], axis=-2).reshape(lead + (2 * SSM_COLS,))


def _deinterleave(x):
    lead = x.shape[:-1]
    t = x.reshape(lead + (SSM_COLS // SCAN_COLS, 2, SCAN_COLS))
    return t[..., 0, :].reshape(lead + (SSM_COLS,)), t[..., 1, :].reshape(lead + (SSM_COLS,))


def _cols_to_slots(g):
    R = g.shape[0]
    return g.reshape(R, N_DEV, g.shape[1] // N_DEV).transpose(1, 0, 2)


def _slots_to_cols(g):
    return g.transpose(1, 0, 2).reshape(g.shape[1], N_DEV * g.shape[2])


SMALL_ORDER = ("b_ada", "g_mix", "b_gate", "a_re", "a_im", "log_dt", "b_re", "b_im", "c_re", "c_im", "d_skip",
               "b_glu", "g_ffn", "b_conv", "g_final")


def _pack(arrs):
    pieces, offs, row = [], [], 0
    for a in arrs:
        f = a.reshape(-1).astype(F32)
        n = f.shape[0]
        rows = -(-n // LANES)
        pieces.append(jnp.pad(f, (0, rows * LANES - n)))
        offs.append((row, n))
        row += rows
    return jnp.concatenate(pieces).reshape(row, LANES), offs


def _unpack(packed, offs, shapes):
    flat = packed.reshape(-1)
    return [flat[r * LANES:r * LANES + n].reshape(s) for (r, n), s in zip(offs, shapes)]


def kernel(x, c, w_ada, b_ada, g_mix, w_in, b_gate, a_re, a_im, log_dt, b_re, b_im, c_re, c_im, d_skip, w_glu, b_glu, w_proj_att, w_proj_ssm, w_out, g_ffn, w_up, w_conv, b_conv, w_down, g_final, loss_target, m_w_ada, m_b_ada, m_g_mix, m_w_in, m_b_gate, m_a_re, m_a_im, m_log_dt, m_b_re, m_b_im, m_c_re, m_c_im, m_d_skip, m_w_glu, m_b_glu, m_w_proj_att, m_w_proj_ssm, m_w_out, m_g_ffn, m_w_up, m_w_conv, m_b_conv, m_w_down, m_g_final, v_w_ada, v_b_ada, v_g_mix, v_w_in, v_b_gate, v_a_re, v_a_im, v_log_dt, v_b_re, v_b_im, v_c_re, v_c_im, v_d_skip, v_w_glu, v_b_glu, v_w_proj_att, v_w_proj_ssm, v_w_out, v_g_ffn, v_w_up, v_w_conv, v_b_conv, v_w_down, v_g_final):
    args = dict(locals())
    B, S, D = x.shape
    me = 4 * lax.axis_index("x") + 2 * lax.axis_index("y") + lax.axis_index("c")
    bf = lambda w: w[0].astype(BF16)

    first = [c, w_in[0].T.astype(BF16)]
    first_sems = _exchange_start(first, "start_first_weights", gather=True, carry=[log_dt[0]], first_level=True)
    (log_dt_sent,) = first_sems[5]

    ab_re, ab_im, f_re, f_im = _s5_params(a_re[0], a_im[0], log_dt_sent.reshape(SSM_GROUPS, 1))
    col = lambda a: a.reshape(SSM_COLS, 1)
    b_re2, b_im2 = b_re[0].reshape(SSM_COLS, SSM_GROUP_CH), b_im[0].reshape(SSM_COLS, SSM_GROUP_CH)
    bb_re, bb_im = _s5_input_matrix(col(f_re), col(f_im), b_re2, b_im2)
    slopes = jnp.asarray([2.0 ** (-8.0 * (h + 1) / N_HEADS) for h in range(N_HEADS)], F32)
    P = dict(g_mix=g_mix, g_ffn=g_ffn, g_final=g_final.reshape(1, D), b_gate=b_gate, d_skip=d_skip, b_glu=b_glu,
             b_conv=b_conv, slopes=slopes,
             a_row=_interleave(ab_re.reshape(1, SSM_COLS), ab_im.reshape(1, SSM_COLS)),
             cc_big=_interleave(_block_diag_out(c_re[0]).T, -_block_diag_out(c_im[0]).T).T,
             bb_big=_interleave(_block_diag_in(bb_re), _block_diag_in(bb_im)))

    own_first, first_lands = _exchange_wait(*first_sems[:4], P["bb_big"], name="wait_first_weights",
                                            copies=len(FIRST_LEVEL))
    first_lands = _gather_finish(first_lands, name="pass_on_first_weights")
    c_slots, w_in_slots = [lax.dynamic_update_index_in_dim(land, a, me, 0) for land, a in zip(first_lands, own_first)]
    c_all = c_slots.reshape(N_DEV * B, D)
    w_in_t = w_in_slots.reshape(-1, D)
    n_qkv = 3 * ATT_WIDTH
    W = dict(w_qkv=_pair_qkv_rows(w_in_t[:n_qkv]), w_us=w_in_t[n_qkv:n_qkv + SSM_WIDTH],
             w_gates=w_in_t[n_qkv + SSM_WIDTH:])

    n_ada = w_ada.shape[2]
    b_ada_cols = lax.dynamic_slice(b_ada, (0, me * n_ada), (1, n_ada))
    mod_part = _ada_fwd(c_all, w_ada[0], b_ada_cols)
    (mod_slots,) = _exchange([(mod_part.reshape(N_DEV, B, n_ada), True)], name="scatter_modulation")
    mod = mod_slots.transpose(1, 0, 2).reshape(B, 1, 6 * D)

    later = [bf(w_glu), bf(w_proj_att), bf(w_proj_ssm), bf(w_out), bf(w_up), w_conv[0], bf(w_down)]
    later_sems = _exchange_start(later, "start_later_weights", gather=True, carry=[mod])
    (mod,) = later_sems[5]

    def late_weights(after):
        _, lands = _exchange_wait(*later_sems[:4], after, name="wait_later_weights")
        g = [lax.dynamic_update_index_in_dim(land, a, me, 0) for land, a in zip(lands, later)]
        more_w = dict(w_glu=g[0].reshape(SSM_WIDTH, SSM_WIDTH), w_proj_att=_slots_to_cols(g[1]),
                      w_proj_ssm=_slots_to_cols(g[2]), w_out=g[3].reshape(D, D), w_up=g[4],
                      w_down=g[6].reshape(D_FF, D))
        return more_w, dict(w_conv=_slots_to_cols(g[5]))

    in_flight = []

    def send_early(grads, carry=()):
        names = list(grads)
        handles = _exchange_start([grads[n] for n in names], "start_gradients_%d" % len(in_flight), gather=False,
                                  carry=carry)
        in_flight.append((names,) + handles[:4])
        return handles[4], handles[5]

    grad_x, d_mod, small = _local_step(x, mod, loss_target, W, late_weights, P, send_early)

    small_list = [small["loss_cols"], small["g_mix"], small["b_gate"], small["g_ab_re"], small["g_ab_im"],
                  _diag_blocks_in(small["d_bb_re"]), _diag_blocks_in(small["d_bb_im"]),
                  _diag_blocks_out(small["d_cc_re"]), -_diag_blocks_out(small["d_cc_im"]),
                  small["g_ffn"], small["b_conv"], small["g_final"], small["d_skip"], small["b_glu"]]
    small_packed, small_offs = _pack(small_list)
    small_sems = _exchange_start([small_packed, d_mod.reshape(B, 6 * D)], "start_small_gradients", gather=True)
    updated = [small_sems[4]]

    out = {}

    def update(name, parts, own=None):
        view = (lambda a: a[0].T) if name == "w_in" else (lambda a: a[0])
        back = (lambda a: a.T[None]) if name == "w_in" else (lambda a: a[None])
        g, dl, mn, vn = _adamw(view(args[name]), view(args["m_" + name]), view(args["v_" + name]), parts,
                               name="adamw_" + name, own=own)
        updated.append(vn)
        for key, val in (("grad_", g), ("delta_", dl), ("new_m_", mn), ("new_v_", vn)):
            out[key + name] = back(val)

    my_slot = me.astype(jnp.int32).reshape(1)
    for i, (names, send_sems, recv_sems, sent, lands) in enumerate(in_flight):
        sent, lands = _exchange_wait(send_sems, recv_sems, sent, lands, updated[-1], name="wait_gradients_%d" % i)
        for name, own_slots, landed in zip(names, sent, lands):
            update(name, landed, own=(own_slots, my_slot))

    own_small, small_lands = _exchange_wait(*small_sems[:4], updated[-1], name="wait_small_gradients")
    small_all, dmod_slots = [lax.dynamic_update_index_in_dim(land, a, me, 0)
                             for land, a in zip(small_lands, own_small)]
    dmod_all = dmod_slots.reshape(N_DEV * B, 6 * D)
    dmod_cols = lax.dynamic_slice(dmod_all, (0, me * n_ada), (N_DEV * B, n_ada))
    d_w_ada, d_b_ada = _ada_bwd(c_all, dmod_all, dmod_cols)
    update("w_ada", d_w_ada[None])

    loss_row, loss_n = small_offs[0]
    small_sum, loss_vec = _sum_parts(small_all, (loss_row, loss_row + loss_n // LANES))
    shapes = [(1, D), (1, D), (1, 2 * D), (SSM_GROUPS, SSM_STATE), (SSM_GROUPS, SSM_STATE), (SSM_COLS, SSM_GROUP_CH),
              (SSM_COLS, SSM_GROUP_CH), (1, SSM_GROUPS, SSM_GROUP_CH, SSM_STATE),
              (1, SSM_GROUPS, SSM_GROUP_CH, SSM_STATE), (1, D), (1, D_FF), (D,), (1, SSM_WIDTH), (1, SSM_WIDTH)]
    (_, s_g_mix, s_b_gate, s_ab_re, s_ab_im, s_bb_re, s_bb_im, s_c_re, s_c_im, s_g_ffn, s_b_conv, s_g_final,
     s_d_skip, s_b_glu) = _unpack(small_sum, small_offs, shapes)
    d_b_re2, d_b_im2, d_f_re, d_f_im = _s5_input_matrix_bwd(col(f_re), col(f_im), b_re2, b_im2, s_bb_re, s_bb_im)
    d_a_re, d_a_im, d_log_dt = _s5_params_bwd(a_re[0], a_im[0], log_dt[0].reshape(SSM_GROUPS, 1), s_ab_re, s_ab_im,
                                              d_f_re.reshape(SSM_GROUPS, SSM_STATE),
                                              d_f_im.reshape(SSM_GROUPS, SSM_STATE))
    grads_small = dict(b_ada=d_b_ada, g_mix=s_g_mix, b_gate=s_b_gate, a_re=d_a_re[None], a_im=d_a_im[None],
                       log_dt=d_log_dt.reshape(1, SSM_GROUPS), b_re=d_b_re2.reshape(b_re.shape),
                       b_im=d_b_im2.reshape(b_im.shape), c_re=s_c_re, c_im=s_c_im, d_skip=s_d_skip, b_glu=s_b_glu,
                       g_ffn=s_g_ffn, b_conv=s_b_conv, g_final=s_g_final)
    flat2 = lambda a: a.reshape(-1, a.shape[-1])
    res = _adamw_small([flat2(args[n]) for n in SMALL_ORDER], [flat2(args["m_" + n]) for n in SMALL_ORDER],
                       [flat2(args["v_" + n]) for n in SMALL_ORDER],
                       [flat2(grads_small[n].reshape(args[n].shape)) for n in SMALL_ORDER])
    for i, n in enumerate(SMALL_ORDER):
        for k, key in enumerate(("grad_", "delta_", "new_m_", "new_v_")):
            out[key + n] = res[4 * i + k].reshape(args[n].shape)

    order = ["w_ada", "b_ada", "g_mix", "w_in", "b_gate", "a_re", "a_im", "log_dt", "b_re", "b_im", "c_re", "c_im",
             "d_skip", "w_glu", "b_glu", "w_proj_att", "w_proj_ssm", "w_out", "g_ffn", "w_up", "w_conv", "b_conv",
             "w_down", "g_final"]
    loss = loss_vec[0, 0]
    return (loss, grad_x, *[out[k + n] for k in ("grad_", "delta_", "new_m_", "new_v_") for n in order])
```

```python
import math

import jax
import jax.numpy as jnp
from jax import lax
from jax.experimental import pallas as pl
from jax.experimental.pallas import tpu as pltpu

F32 = jnp.float32
BF16 = jnp.bfloat16

N_DEV = 8
D_MODEL = 1024
N_HEADS = 8
HEAD_DIM = 64
ATT_WIDTH = N_HEADS * HEAD_DIM
DILATIONS = (1, 4, 16)
WIN = 128
SSM_GROUPS = 16
SSM_GROUP_CH = 16
SSM_WIDTH = SSM_GROUPS * SSM_GROUP_CH
SSM_STATE = 64
SSM_COLS = SSM_GROUPS * SSM_STATE
D_FF = 2048
EPS = 1e-6
NEG_INF = -1e30
ADAM_LR, ADAM_B1, ADAM_B2, ADAM_EPS, ADAM_WD, ADAM_STEP = 0.001, 0.9, 0.999, 1e-08, 0.01, 10

V7X_VMEM_LIMIT = 56 * 1024 * 1024
LANES = 128


def _params(n_grid):
    return pltpu.CompilerParams(dimension_semantics=("arbitrary",) * n_grid,
                                vmem_limit_bytes=V7X_VMEM_LIMIT)


def _tile(n, pref):
    if n <= pref:
        return n
    t = (pref // LANES) * LANES
    while t > 0:
        if n % t == 0:
            return t
        t -= LANES
    return n


def _matmul(a, b, *, ta=False, tb=False, out_dtype=F32, name):
    if ta:
        K, M = a.shape
    else:
        M, K = a.shape
    if tb:
        N, K2 = b.shape
    else:
        K2, N = b.shape
    assert K == K2, (a.shape, b.shape)
    if ta:
        tm, tn, tk = _tile(M, 1024), _tile(N, 2048), _tile(K, 1024)
    else:
        tm, tk = _tile(M, 512), _tile(K, 4096)
        tn = _tile(N, 2048 if K <= 2048 else 1024)
    nk = K // tk
    dn = (((0,) if ta else (1,), (1,) if tb else (0,)), ((), ()))

    def body(a_ref, b_ref, o_ref, acc_ref):
        k = pl.program_id(2)
        part = lax.dot_general(a_ref[...].astype(BF16), b_ref[...].astype(BF16), dn, preferred_element_type=F32)
        if nk == 1:
            o_ref[...] = part.astype(o_ref.dtype)
            return

        @pl.when(k == 0)
        def _():
            acc_ref[...] = jnp.zeros_like(acc_ref)

        acc_ref[...] += part

        @pl.when(k == nk - 1)
        def _():
            o_ref[...] = acc_ref[...].astype(o_ref.dtype)

    a_spec = (pl.BlockSpec((tk, tm), lambda j, i, k: (k, i)) if ta
              else pl.BlockSpec((tm, tk), lambda j, i, k: (i, k)))
    b_spec = (pl.BlockSpec((tn, tk), lambda j, i, k: (j, k)) if tb
              else pl.BlockSpec((tk, tn), lambda j, i, k: (k, j)))
    return pl.pallas_call(
        body, name=name, grid=(N // tn, M // tm, nk),
        in_specs=[a_spec, b_spec],
        out_specs=pl.BlockSpec((tm, tn), lambda j, i, k: (i, j)),
        out_shape=jax.ShapeDtypeStruct((M, N), out_dtype),
        scratch_shapes=[pltpu.VMEM((tm, tn) if nk > 1 else (8, LANES), F32)],
        compiler_params=_params(3),
    )(a, b)


HALF = 256
UP_SLOTS = N_DEV // 2
UP_GROUP = 4 * HALF


def _up_weight_spec(K, index):
    return pl.BlockSpec((2, None, K, 2 * HALF), index)


def _up_dw(a, d, name):
    M, K = a.shape
    tk = _tile(M, 1024)
    nk = M // tk

    def body(a_ref, d_ref, o_ref, acc_ref):
        k = pl.program_id(1)

        @pl.when(k == 0)
        def _():
            acc_ref[...] = jnp.zeros_like(acc_ref)

        acc_ref[...] += lax.dot_general(a_ref[...], d_ref[...], _TN, preferred_element_type=F32)

        @pl.when(k == nk - 1)
        def _():
            for half in range(2):
                for part in range(2):
                    lo = (2 * half + part) * HALF
                    o_ref[part, :, half * HALF:(half + 1) * HALF] = acc_ref[:, lo:lo + HALF].astype(o_ref.dtype)

    out = pl.pallas_call(
        body, name=name, grid=(UP_SLOTS, nk),
        in_specs=[pl.BlockSpec((tk, K), lambda j, k: (k, 0)), pl.BlockSpec((tk, UP_GROUP), lambda j, k: (k, j))],
        out_specs=_up_weight_spec(K, lambda j, k: (0, j, 0, 0)),
        out_shape=jax.ShapeDtypeStruct((2, UP_SLOTS, K, 2 * HALF), BF16),
        scratch_shapes=[pltpu.VMEM((K, UP_GROUP), F32)], compiler_params=_params(2),
    )(a, d)
    return out.reshape(N_DEV, K, 2 * HALF)


def _rowwise(fn, rows, bvecs, consts, out_rows, out_b, out_g, *, ts, name, raw=(), split=1):
    B, S = rows[0][0].shape[:2]
    nin = len(rows) + len(bvecs) + len(consts)
    nr, nb, ng = len(out_rows), len(out_b), len(out_g)
    sub = ts // split

    def body(*refs):
        b = pl.program_id(0)
        s = pl.program_id(1)
        orefs = refs[nin:]
        fixed = [r[...] for r in refs[len(rows):nin]]
        sums = None
        for h in range(split):
            sec = slice(h * sub, (h + 1) * sub)
            vals = [r[sec, :] if i in raw else r[sec, :].astype(F32) for i, r in enumerate(refs[:len(rows)])]
            outs = fn(*vals, *fixed)
            if not isinstance(outs, (tuple, list)):
                outs = (outs,)
            for i in range(nr):
                orefs[i][sec, :] = outs[i].astype(orefs[i].dtype)
            part = list(outs[nr:])
            sums = part if sums is None else [a + c for a, c in zip(sums, part)]
        for i in range(nb):
            ref = orefs[nr + i]

            @pl.when(s == 0)
            def _(ref=ref):
                ref[...] = jnp.zeros_like(ref)

            ref[...] += sums[i]
        for i in range(ng):
            ref = orefs[nr + nb + i]

            @pl.when((s == 0) & (b == 0))
            def _(ref=ref):
                ref[...] = jnp.zeros_like(ref)

            ref[...] += sums[nb + i]

    rows = [r if len(r) == 4 else r + (0,) for r in rows]
    in_specs = ([pl.BlockSpec((None, ts, cb), lambda b, s, ci=ci, b0=b0: (b0 + b, s, ci)) for (_, cb, ci, b0) in rows]
                + [pl.BlockSpec((None, 1, cb), lambda b, s, ci=ci: (b, 0, ci)) for (_, cb, ci) in bvecs]
                + [pl.BlockSpec(a.shape, lambda b, s: (0, 0)) for a in consts])
    out_shape = ([jax.ShapeDtypeStruct((B, S, c), dt) for (c, dt) in out_rows]
                 + [jax.ShapeDtypeStruct((B, 1, c), F32) for c in out_b]
                 + [jax.ShapeDtypeStruct(rc, F32) for rc in out_g])
    out_specs = ([pl.BlockSpec((None, ts, c), lambda b, s: (b, s, 0)) for (c, _) in out_rows]
                 + [pl.BlockSpec((None, 1, c), lambda b, s: (b, 0, 0)) for c in out_b]
                 + [pl.BlockSpec(rc, lambda b, s: (0, 0)) for rc in out_g])
    args = [r[0] for r in rows] + [a for (a, _, _) in bvecs] + list(consts)
    return pl.pallas_call(
        body, name=name, grid=(B, S // ts), in_specs=in_specs, out_specs=out_specs,
        out_shape=out_shape, compiler_params=_params(2),
    )(*args)


def _col_sum(v):
    return jnp.sum(v, axis=0, keepdims=True)


def _rms_scale(h):
    return lax.rsqrt(jnp.mean(h * h, axis=-1, keepdims=True) + EPS)


def _sigmoid(v):
    return 0.5 * (1.0 + jnp.tanh(0.5 * v))


ATT_SCALE = HEAD_DIM ** -0.5
COPY_ROWS = 256
_NT = (((1,), (1,)), ((), ()))
_TN = (((0,), (0,)), ((), ()))


def _row_chunks(d, seq):
    sub = seq // d
    out = []
    for r in range(d):
        for c0 in range(0, sub, COPY_ROWS):
            n = min(COPY_ROWS, sub - c0)
            out.append((pl.ds(r + c0 * d, n, stride=d), r * sub + c0, n))
    return out


ATT_UNROLL = 16
KEYS = 2 * WIN


def _zero_once(refs):
    @pl.when((pl.program_id(0) == 0) & (pl.program_id(1) == 0))
    def _():
        for r in refs:
            r[...] = jnp.zeros_like(r)


def _pair_bias(bias_ref, slopes_ref, hp, d, key_major):
    shape = (KEYS, WIN) if key_major else (WIN, KEYS)
    qi = lax.broadcasted_iota(jnp.int32, shape, 1 if key_major else 0)
    kj = lax.broadcasted_iota(jnp.int32, shape, 0 if key_major else 1)
    dist = WIN + qi - kj
    valid = (dist >= 0) & (dist <= WIN)
    distf = dist.astype(F32)
    for h in range(2):
        slope_d = slopes_ref[2 * hp + h] * float(d)
        with_prev = jnp.where(valid, -(slope_d * distf), NEG_INF)
        no_prev = jnp.where(kj >= WIN, with_prev, NEG_INF)
        span = slice(h * KEYS, (h + 1) * KEYS)
        if key_major:
            bias_ref[1, span, :] = with_prev
            bias_ref[0, span, :] = no_prev
        else:
            bias_ref[1, :, span] = with_prev
            bias_ref[0, :, span] = no_prev


def _stack_heads(v):
    first = lax.broadcasted_iota(jnp.int32, v.shape, 1) < HEAD_DIM
    zero = jnp.zeros_like(v)
    return jnp.concatenate([jnp.where(first, v, zero), jnp.where(first, zero, v)], axis=0)


def _per_head(c0, c1, n):
    return jnp.where(lax.broadcasted_iota(jnp.int32, (n, LANES), 1) < HEAD_DIM, c0, c1)


def _qkv_spec(seq, j):
    return pl.BlockSpec((None, seq, LANES), lambda b, hp: (b, 0, 3 * hp + j))


def _attention_fwd(qkv, slopes):
    B, S, _ = qkv.shape
    n_blk = S // WIN
    n_pair = N_HEADS // 2

    def body(slopes_ref, q_ref, k_ref, v_ref, o_ref, lse_ref, qp, kp, vp, bias, acc, mx, sm, acc_n, mx_n, sm_n):
        hp = pl.program_id(1)
        _zero_once((kp, vp))
        for p, d in enumerate(DILATIONS):
            nb = n_blk // d
            chunks = _row_chunks(d, S)
            for src, dst, n in chunks:
                qp[dst:dst + n, :] = (q_ref[src, :] * ATT_SCALE).astype(BF16)
                kp[WIN + dst:WIN + dst + n, :] = k_ref[src, :].astype(BF16)
                vp[WIN + dst:WIN + dst + n, :] = v_ref[src, :].astype(BF16)
            _pair_bias(bias, slopes_ref, hp, d, key_major=False)
            acc_t, mx_t, sm_t = (acc_n, mx_n, sm_n) if d == 1 else (acc, mx, sm)

            nk = WIN if nb == 1 else KEYS
            bias_cur = jnp.concatenate([bias[0, :, WIN:KEYS], bias[0, :, KEYS + WIN:]], axis=1) if nb == 1 else None

            def block(i, carry, p=p, nb=nb, nk=nk, bias_cur=bias_cur, acc_t=acc_t, mx_t=mx_t, sm_t=sm_t):
                cur = pl.ds(pl.multiple_of(i * WIN, WIN), WIN)
                keys = pl.ds(pl.multiple_of(i * WIN + (KEYS - nk), WIN), nk)
                s = lax.dot_general(qp[cur, :], _stack_heads(kp[keys, :]), _NT, preferred_element_type=F32)
                s = s + (bias_cur if nb == 1 else bias[((i % nb) > 0).astype(jnp.int32)])
                es, ms, ls = [], [], []
                for h in range(2):
                    sh = s[:, h * nk:(h + 1) * nk]
                    m = jnp.max(sh if nb == 1 else jnp.maximum(sh[:, :WIN], sh[:, WIN:]), axis=1, keepdims=True)
                    e = jnp.exp(sh - m)
                    es.append(e.astype(BF16))
                    ms.append(m)
                    ls.append(jnp.sum(e if nb == 1 else e[:, :WIN] + e[:, WIN:], axis=1, keepdims=True))
                acc_t[p, cur, :] = jnp.dot(jnp.concatenate(es, axis=1), _stack_heads(vp[keys, :]),
                                           preferred_element_type=F32)
                mx_t[p, cur, :] = _per_head(ms[0], ms[1], WIN)
                sm_t[p, cur, :] = _per_head(ls[0], ls[1], WIN)
                return carry

            lax.fori_loop(0, n_blk, block, 0, unroll=ATT_UNROLL)
            if d > 1:
                for src, dst, n in chunks:
                    acc_n[p, src, :] = acc[p, dst:dst + n, :]
                    mx_n[p, src, :] = mx[p, dst:dst + n, :]
                    sm_n[p, src, :] = sm[p, dst:dst + n, :]

        chunk = 256

        def merge(i, carry):
            rows = pl.ds(pl.multiple_of(i * chunk, chunk), chunk)
            ms = [mx_n[p, rows, :] for p in range(3)]
            m = jnp.maximum(jnp.maximum(ms[0], ms[1]), ms[2])
            ws = [jnp.exp(mp - m) for mp in ms]
            l = ws[0] * sm_n[0, rows, :] + ws[1] * sm_n[1, rows, :] + ws[2] * sm_n[2, rows, :]
            o = (ws[0] * acc_n[0, rows, :] + ws[1] * acc_n[1, rows, :] + ws[2] * acc_n[2, rows, :]) / l
            o_ref[rows, :] = o.astype(o_ref.dtype)
            lse = m + jnp.log(l)
            for h in range(2):
                lse_ref[rows, h:h + 1] = lse[:, h * HEAD_DIM:h * HEAD_DIM + 1]
            return carry

        lax.fori_loop(0, S // chunk, merge, 0)

    return pl.pallas_call(
        body, name="attention_fwd", grid=(B, n_pair),
        in_specs=[pl.BlockSpec(memory_space=pltpu.SMEM), _qkv_spec(S, 0), _qkv_spec(S, 1), _qkv_spec(S, 2)],
        out_specs=[pl.BlockSpec((None, S, LANES), lambda b, hp: (b, 0, hp)),
                   pl.BlockSpec((None, None, S, 2), lambda b, hp: (b, hp, 0, 0))],
        out_shape=[jax.ShapeDtypeStruct((B, S, ATT_WIDTH), BF16),
                   jax.ShapeDtypeStruct((B, n_pair, S, 2), F32)],
        scratch_shapes=[pltpu.VMEM((S, LANES), BF16), pltpu.VMEM((S + WIN, LANES), BF16),
                        pltpu.VMEM((S + WIN, LANES), BF16), pltpu.VMEM((2, WIN, 2 * KEYS), F32)]
        + [pltpu.VMEM((3, S, LANES), F32)] * 6,
        compiler_params=_params(2),
    )(slopes, qkv, qkv, qkv)


def _attention_bwd(qkv, o, do, lse, slopes):
    B, S, _ = qkv.shape
    n_blk = S // WIN
    n_pair = N_HEADS // 2

    def body(slopes_ref, q_ref, k_ref, v_ref, o_ref, do_ref, lse_ref, dx_ref,
             qp, dop, kp, vp, aux, auxp, aux_t, bias_t, dqp, dvk, dq_n, dk_n, dv_n):
        hp = pl.program_id(1)
        aux[...] = jnp.zeros_like(aux)
        for c0 in range(0, S, COPY_ROWS):
            rows = slice(c0, c0 + COPY_ROWS)
            prod = do_ref[rows, :] * o_ref[rows, :].astype(F32)
            for h in range(2):
                aux[rows, 2 * h:2 * h + 1] = lse_ref[rows, h:h + 1]
                aux[rows, 2 * h + 1:2 * h + 2] = jnp.sum(prod[:, h * HEAD_DIM:(h + 1) * HEAD_DIM], axis=1,
                                                         keepdims=True)
        dq_n[...] = jnp.zeros_like(dq_n)
        dk_n[...] = jnp.zeros_like(dk_n)
        dv_n[...] = jnp.zeros_like(dv_n)
        _zero_once((kp, vp))
        for p, d in enumerate(DILATIONS):
            nb = n_blk // d
            chunks = _row_chunks(d, S)
            for src, dst, n in chunks:
                auxp[dst:dst + n, :] = aux[src, :]
                qp[dst:dst + n, :] = (q_ref[src, :] * ATT_SCALE).astype(BF16)
                dop[dst:dst + n, :] = do_ref[src, :].astype(BF16)
                kp[WIN + dst:WIN + dst + n, :] = k_ref[src, :].astype(BF16)
                vp[WIN + dst:WIN + dst + n, :] = v_ref[src, :].astype(BF16)
            for i in range(n_blk):
                aux_t[i] = auxp[i * WIN:(i + 1) * WIN, :].T[0:8, :]
            _pair_bias(bias_t, slopes_ref, hp, d, key_major=True)
            dvk[...] = jnp.zeros_like(dvk)

            nk = WIN if nb == 1 else KEYS
            bias_cur = jnp.concatenate([bias_t[0, WIN:KEYS, :], bias_t[0, KEYS + WIN:, :]], axis=0) if nb == 1 else None

            def block(i, carry, nb=nb, nk=nk, bias_cur=bias_cur):
                cur = pl.ds(pl.multiple_of(i * WIN, WIN), WIN)
                keys = pl.ds(pl.multiple_of(i * WIN + (KEYS - nk), WIN), nk)
                q2, do2 = qp[cur, :], dop[cur, :]
                kc = _stack_heads(kp[keys, :])
                s_t = lax.dot_general(kc, q2, _NT, preferred_element_type=F32)
                s_t = s_t + (bias_cur if nb == 1 else bias_t[((i % nb) > 0).astype(jnp.int32)])
                dp_t = lax.dot_general(_stack_heads(vp[keys, :]), do2, _NT, preferred_element_type=F32)
                ps, dss = [], []
                for h in range(2):
                    span = slice(h * nk, (h + 1) * nk)
                    p_t = jnp.exp(s_t[span, :] - aux_t[i, 2 * h:2 * h + 1, :])
                    ds_t = p_t * (dp_t[span, :] - aux_t[i, 2 * h + 1:2 * h + 2, :])
                    ps.append(p_t.astype(BF16))
                    dss.append(ds_t.astype(BF16))
                do_rows, q_rows = _stack_heads(do2), _stack_heads(q2)
                zr = jnp.zeros_like(do_rows)
                rhs = jnp.concatenate([jnp.concatenate([do_rows, zr], axis=1),
                                       jnp.concatenate([zr, q_rows], axis=1)], axis=0)
                dvk[keys, :] += jnp.dot(jnp.concatenate(ps + dss, axis=1), rhs, preferred_element_type=F32)
                dqp[cur, :] = lax.dot_general(jnp.concatenate(dss, axis=0), kc, _TN, preferred_element_type=F32)
                return carry

            lax.fori_loop(0, n_blk, block, 0, unroll=ATT_UNROLL)
            for src, dst, n in chunks:
                dq_n[src, :] += dqp[dst:dst + n, :]
                dv_n[src, :] += dvk[WIN + dst:WIN + dst + n, :LANES]
                dk_n[src, :] += dvk[WIN + dst:WIN + dst + n, LANES:]
        for c0 in range(0, S, COPY_ROWS):
            rows = slice(c0, c0 + COPY_ROWS)
            dx_ref[rows, 0:LANES] = (dq_n[rows, :] * ATT_SCALE).astype(dx_ref.dtype)
            dx_ref[rows, LANES:2 * LANES] = dk_n[rows, :].astype(dx_ref.dtype)
            dx_ref[rows, 2 * LANES:3 * LANES] = dv_n[rows, :].astype(dx_ref.dtype)

    pair = lambda width: pl.BlockSpec((None, S, width), lambda b, hp: (b, 0, hp))
    vm = lambda shape, dt: pltpu.VMEM(shape, dt)
    return pl.pallas_call(
        body, name="attention_bwd", grid=(B, n_pair),
        in_specs=[pl.BlockSpec(memory_space=pltpu.SMEM), _qkv_spec(S, 0), _qkv_spec(S, 1), _qkv_spec(S, 2),
                  pair(LANES), pair(LANES), pl.BlockSpec((None, None, S, 2), lambda b, hp: (b, hp, 0, 0))],
        out_specs=pair(3 * LANES),
        out_shape=jax.ShapeDtypeStruct((B, S, 3 * ATT_WIDTH), BF16),
        scratch_shapes=[vm((S, LANES), BF16), vm((S, LANES), BF16),
                        vm((S + WIN, LANES), BF16), vm((S + WIN, LANES), BF16),
                        vm((S, LANES), F32), vm((S, LANES), F32), vm((n_blk, 8, WIN), F32),
                        vm((2, 2 * KEYS, WIN), F32),
                        vm((S, LANES), F32), vm((S + WIN, 2 * LANES), F32),
                        vm((S, LANES), F32), vm((S, LANES), F32), vm((S, LANES), F32)],
        compiler_params=_params(2),
    )(slopes, qkv, qkv, qkv, o, do, lse)


SCAN_COLS = 256
SCAN_ROWS = 8


def _rows_to_tile(rows):
    rid = lax.broadcasted_iota(jnp.int32, (SCAN_ROWS, rows[0].shape[1]), 0)
    tile = jnp.broadcast_to(rows[0], rid.shape)
    for k in range(1, SCAN_ROWS):
        tile = jnp.where(rid == k, rows[k], tile)
    return tile


SCAN_UNROLL = 4


def _complex_powers(ar, ai, n):
    out = [(ar, ai)]
    for _ in range(n - 1):
        pr, pi = out[-1]
        out.append((pr * ar - pi * ai, pr * ai + pi * ar))
    return out


def _round_multipliers(powers, rid, reverse):
    out = []
    for s in (1, 2, 4):
        keep = (rid < SCAN_ROWS - s) if reverse else (rid >= s)
        out.append((jnp.where(keep, powers[s - 1][0], 0.0), jnp.where(keep, powers[s - 1][1], 0.0)))
    return out


def _tile_scan(xr, xi, multipliers, reverse):
    for s, (mr, mi) in zip((1, 2, 4), multipliers):
        shift = SCAN_ROWS - s if reverse else s
        sr, si = pltpu.roll(xr, shift, 0), pltpu.roll(xi, shift, 0)
        xr, xi = xr + (mr * sr - mi * si), xi + (mr * si + mi * sr)
    return xr, xi


SCAN_CHUNK = 256


def _scan_fwd(us, bb_big, a_row, cc_big):
    B, S, _ = us.shape
    groups = 2
    width = 2 * groups * SCAN_COLS
    nc = 2 * SSM_COLS // width
    nt = S // SCAN_ROWS
    tiles = SCAN_CHUNK // SCAN_ROWS
    LAST = slice(SCAN_ROWS - 1, SCAN_ROWS)

    def body(us_ref, bb_ref, a_ref, cc_ref, xs_ref, y_ref, bu_ref):
        bb = bb_ref[...].astype(BF16)
        for c in range(S // SCAN_CHUNK):
            part = jnp.dot(us_ref[c * SCAN_CHUNK:(c + 1) * SCAN_CHUNK, :].astype(BF16), bb,
                           preferred_element_type=F32)
            bu_ref[c * tiles:(c + 1) * tiles] = part.reshape(tiles, SCAN_ROWS, width)
        rid = lax.broadcasted_iota(jnp.int32, (SCAN_ROWS, SCAN_COLS), 0)
        consts = []
        for g in range(groups):
            re = slice(2 * g * SCAN_COLS, (2 * g + 1) * SCAN_COLS)
            im = slice((2 * g + 1) * SCAN_COLS, (2 * g + 2) * SCAN_COLS)
            powers = _complex_powers(a_ref[:, re], a_ref[:, im], SCAN_ROWS)
            carry_mult = (_rows_to_tile([p[0] for p in powers]), _rows_to_tile([p[1] for p in powers]))
            consts.append((re, im, carry_mult, _round_multipliers(powers, rid, reverse=False)))

        def tile(i, carry):
            out = []
            for (re, im, (cr_t, ci_t), rounds), (cr, ci) in zip(consts, carry):
                xr, xi = _tile_scan(bu_ref[i, :, re], bu_ref[i, :, im], rounds, reverse=False)
                xs_ref[i, :, re] = xr + (cr_t * cr - ci_t * ci)
                xs_ref[i, :, im] = xi + (cr_t * ci + ci_t * cr)
                out.append((xs_ref[i, LAST, re], xs_ref[i, LAST, im]))
            return tuple(out)

        zero = jnp.zeros((1, SCAN_COLS), F32)
        lax.fori_loop(0, nt, tile, ((zero, zero),) * groups, unroll=SCAN_UNROLL)

        @pl.when(pl.program_id(1) == 0)
        def _():
            y_ref[...] = jnp.zeros_like(y_ref)

        cc = cc_ref[...].astype(BF16)
        for c in range(S // SCAN_CHUNK):
            x2 = xs_ref[c * tiles:(c + 1) * tiles].reshape(SCAN_CHUNK, width).astype(BF16)
            y_ref[c * SCAN_CHUNK:(c + 1) * SCAN_CHUNK, :] += jnp.dot(x2, cc, preferred_element_type=F32)

    col = pl.BlockSpec((None, nt, SCAN_ROWS, width), lambda b, j: (b, 0, 0, j))
    tok = pl.BlockSpec((None, S, SSM_WIDTH), lambda b, j: (b, 0, 0))
    xs, y = pl.pallas_call(
        body, name="s5_scan_fwd", grid=(B, nc),
        in_specs=[tok, pl.BlockSpec((SSM_WIDTH, width), lambda b, j: (0, j)),
                  pl.BlockSpec((1, width), lambda b, j: (0, j)), pl.BlockSpec((width, SSM_WIDTH), lambda b, j: (j, 0))],
        out_specs=[col, tok],
        out_shape=[jax.ShapeDtypeStruct((B, nt, SCAN_ROWS, 2 * SSM_COLS), F32),
                   jax.ShapeDtypeStruct((B, S, SSM_WIDTH), F32)],
        scratch_shapes=[pltpu.VMEM((nt, SCAN_ROWS, width), F32)],
        compiler_params=_params(2),
    )(us, bb_big, a_row, cc_big)
    return xs.reshape(B, S, 2 * SSM_COLS), y


def _scan_bwd(dy, us, bb_big, cc_big, xs, a_row):
    B, S, _ = dy.shape
    width = 2 * SCAN_COLS
    nc = SSM_COLS // SCAN_COLS
    nt = S // SCAN_ROWS
    tiles = SCAN_CHUNK // SCAN_ROWS
    RE, IM = slice(0, SCAN_COLS), slice(SCAN_COLS, 2 * SCAN_COLS)
    FIRST, LAST = slice(0, 1), slice(SCAN_ROWS - 1, SCAN_ROWS)

    def body(dy_ref, us_ref, bb_ref, cc_ref, x_ref, a_ref, dus_ref, ga_ref, dbb_ref, dcc_ref, d_ref, lam_ref):
        b = pl.program_id(1)
        cc = cc_ref[...].astype(BF16)
        for c in range(S // SCAN_CHUNK):
            part = lax.dot_general(dy_ref[c * SCAN_CHUNK:(c + 1) * SCAN_CHUNK, :].astype(BF16), cc, _NT,
                                   preferred_element_type=F32)
            d_ref[c * tiles:(c + 1) * tiles] = part.reshape(tiles, SCAN_ROWS, width)
        powers = _complex_powers(a_ref[:, RE], -a_ref[:, IM], SCAN_ROWS)
        rid = lax.broadcasted_iota(jnp.int32, (SCAN_ROWS, SCAN_COLS), 0)
        cr_t = _rows_to_tile([powers[SCAN_ROWS - 1 - r][0] for r in range(SCAN_ROWS)])
        ci_t = _rows_to_tile([powers[SCAN_ROWS - 1 - r][1] for r in range(SCAN_ROWS)])
        rounds = _round_multipliers(powers, rid, reverse=True)

        @pl.when(b == 0)
        def _():
            ga_ref[...] = jnp.zeros_like(ga_ref)
            dbb_ref[...] = jnp.zeros_like(dbb_ref)
            dcc_ref[...] = jnp.zeros_like(dcc_ref)

        def tile(j, carry):
            cr, ci, accr, acci = carry
            i = nt - 1 - j
            lr, li = _tile_scan(d_ref[i, :, RE], d_ref[i, :, IM], rounds, reverse=True)
            lam_r = lr + (cr_t * cr - ci_t * ci)
            lam_i = li + (cr_t * ci + ci_t * cr)
            lam_ref[i, :, RE] = lam_r
            lam_ref[i, :, IM] = lam_i
            ip = jnp.maximum(i - 1, 0)
            keep = (i > 0).astype(F32)
            xpr = jnp.where(rid == 0, x_ref[ip, LAST, RE] * keep, pltpu.roll(x_ref[i, :, RE], 1, 0))
            xpi = jnp.where(rid == 0, x_ref[ip, LAST, IM] * keep, pltpu.roll(x_ref[i, :, IM], 1, 0))
            accr = accr + lam_r * xpr + lam_i * xpi
            acci = acci + lam_i * xpr - lam_r * xpi
            return lam_ref[i, FIRST, RE], lam_ref[i, FIRST, IM], accr, acci

        z1 = jnp.zeros((1, SCAN_COLS), F32)
        z8 = jnp.zeros((SCAN_ROWS, SCAN_COLS), F32)
        _, _, accr, acci = lax.fori_loop(0, nt, tile, (z1, z1, z8, z8), unroll=SCAN_UNROLL)
        ga_ref[:, RE] += _col_sum(accr)
        ga_ref[:, IM] += _col_sum(acci)

        bb = bb_ref[...].astype(BF16)
        for c in range(S // SCAN_CHUNK):
            rows = slice(c * SCAN_CHUNK, (c + 1) * SCAN_CHUNK)
            lam2 = lam_ref[c * tiles:(c + 1) * tiles].reshape(SCAN_CHUNK, width).astype(BF16)
            x2 = x_ref[c * tiles:(c + 1) * tiles].reshape(SCAN_CHUNK, width).astype(BF16)
            dus_ref[rows, :] = lax.dot_general(lam2, bb, _NT, preferred_element_type=F32)
            dbb_ref[...] += lax.dot_general(us_ref[rows, :].astype(BF16), lam2, _TN, preferred_element_type=F32)
            dcc_ref[...] += lax.dot_general(x2, dy_ref[rows, :].astype(BF16), _TN, preferred_element_type=F32)

    col = pl.BlockSpec((None, nt, SCAN_ROWS, width), lambda j, b: (b, 0, 0, j))
    tok = pl.BlockSpec((None, S, SSM_WIDTH), lambda j, b: (b, 0, 0))
    scratch = pltpu.VMEM((nt, SCAN_ROWS, width), F32)
    return pl.pallas_call(
        body, name="s5_scan_bwd", grid=(nc, B),
        in_specs=[tok, tok, pl.BlockSpec((SSM_WIDTH, width), lambda j, b: (0, j)),
                  pl.BlockSpec((width, SSM_WIDTH), lambda j, b: (j, 0)), col,
                  pl.BlockSpec((1, width), lambda j, b: (0, j))],
        out_specs=[pl.BlockSpec((None, None, S, SSM_WIDTH), lambda j, b: (j, b, 0, 0)),
                   pl.BlockSpec((1, width), lambda j, b: (0, j)),
                   pl.BlockSpec((SSM_WIDTH, width), lambda j, b: (0, j)),
                   pl.BlockSpec((width, SSM_WIDTH), lambda j, b: (j, 0))],
        out_shape=[jax.ShapeDtypeStruct((nc, B, S, SSM_WIDTH), F32), jax.ShapeDtypeStruct((1, 2 * SSM_COLS), F32),
                   jax.ShapeDtypeStruct((SSM_WIDTH, 2 * SSM_COLS), F32),
                   jax.ShapeDtypeStruct((2 * SSM_COLS, SSM_WIDTH), F32)],
        scratch_shapes=[scratch, scratch],
        compiler_params=_params(2),
    )(dy, us, bb_big, cc_big, xs.reshape(B, nt, SCAN_ROWS, 2 * SSM_COLS), a_row)


def _s5_discretise(lr, li, log_dt):
    dt = jnp.exp(log_dt)
    mag = jnp.exp(lr * dt)
    ang = li * dt
    ab_re, ab_im = mag * jnp.cos(ang), mag * jnp.sin(ang)
    nr, ni = ab_re - 1.0, ab_im
    den = lr * lr + li * li
    f_re = (nr * lr + ni * li) / den
    f_im = (ni * lr - nr * li) / den
    return dt, ab_re, ab_im, nr, ni, den, f_re, f_im


def _s5_params(a_re, a_im, log_dt):
    def body(lr_ref, li_ref, ld_ref, abr, abi, fr, fi):
        _, ab_re, ab_im, _, _, _, f_re, f_im = _s5_discretise(lr_ref[...], li_ref[...], ld_ref[...])
        abr[...] = ab_re
        abi[...] = ab_im
        fr[...] = f_re
        fi[...] = f_im

    return pl.pallas_call(body, name="s5_params",
                          out_shape=[jax.ShapeDtypeStruct(a_re.shape, F32)] * 4)(a_re, a_im, log_dt)


def _s5_input_matrix(f_re, f_im, b_re, b_im):
    def body(fr, fi, br, bi, o_re, o_im):
        o_re[...] = fr[...] * br[...] - fi[...] * bi[...]
        o_im[...] = fr[...] * bi[...] + fi[...] * br[...]

    return pl.pallas_call(body, name="s5_input_matrix",
                          out_shape=[jax.ShapeDtypeStruct(b_re.shape, F32)] * 2)(f_re, f_im, b_re, b_im)


def _s5_input_matrix_bwd(f_re, f_im, b_re, b_im, g_re, g_im):
    def body(fr, fi, br, bi, gr, gi, dbr, dbi, dfr, dfi):
        dbr[...] = fr[...] * gr[...] + fi[...] * gi[...]
        dbi[...] = fr[...] * gi[...] - fi[...] * gr[...]
        dfr[...] = jnp.sum(br[...] * gr[...] + bi[...] * gi[...], axis=1, keepdims=True)
        dfi[...] = jnp.sum(br[...] * gi[...] - bi[...] * gr[...], axis=1, keepdims=True)

    return pl.pallas_call(
        body, name="s5_input_matrix_bwd",
        out_shape=[jax.ShapeDtypeStruct(b_re.shape, F32)] * 2 + [jax.ShapeDtypeStruct(f_re.shape, F32)] * 2,
    )(f_re, f_im, b_re, b_im, g_re, g_im)


def _s5_params_bwd(a_re, a_im, log_dt, g_ab_re, g_ab_im, d_f_re, d_f_im):
    def body(lr_ref, li_ref, ld_ref, gar, gai, dfr, dfi, o_lr, o_li, o_ld):
        lr, li = lr_ref[...], li_ref[...]
        dt, ab_re, ab_im, nr, ni, den, f_re, f_im = _s5_discretise(lr, li, ld_ref[...])
        d_fr, d_fi = dfr[...], dfi[...]
        d_nr = (d_fr * lr - d_fi * li) / den
        d_ni = (d_fr * li + d_fi * lr) / den
        common = (d_fr * f_re + d_fi * f_im) * 2.0 / den
        d_lr = (d_fr * nr + d_fi * ni) / den - common * lr
        d_li = (d_fr * ni - d_fi * nr) / den - common * li
        d_abr = gar[...] + d_nr
        d_abi = gai[...] + d_ni
        d_mag_mag = d_abr * ab_re + d_abi * ab_im
        d_ang = d_abi * ab_re - d_abr * ab_im
        o_lr[...] = d_lr + d_mag_mag * dt
        o_li[...] = d_li + d_ang * dt
        o_ld[...] = jnp.sum(d_mag_mag * lr + d_ang * li, axis=1, keepdims=True) * dt

    return pl.pallas_call(
        body, name="s5_params_bwd",
        out_shape=[jax.ShapeDtypeStruct(a_re.shape, F32)] * 2 + [jax.ShapeDtypeStruct(log_dt.shape, F32)],
    )(a_re, a_im, log_dt, g_ab_re, g_ab_im, d_f_re, d_f_im)


CONV_COLS = 256


def _shift_down(v, j, row):
    return jnp.where(row >= j, pltpu.roll(v, j, 0), 0.0)


def _shift_up(v, j, row, seq):
    return jnp.where(row < seq - j, pltpu.roll(v, seq - j, 0), 0.0)


def _up_conv_fwd(u, w3, w_conv, b_conv):
    B, S, K = u.shape

    def body(u_ref, w_ref, wc_ref, bc_ref, up_ref, ff_ref):
        uv = u_ref[...]
        row = lax.broadcasted_iota(jnp.int32, (S, CONV_COLS), 0)
        for half in range(2):
            cols = slice(half * HALF, (half + 1) * HALF)
            pair = jnp.dot(uv, jnp.concatenate([w_ref[0, :, cols], w_ref[1, :, cols]], axis=1),
                           preferred_element_type=F32)
            up_ref[:, half * 2 * HALF:(half + 1) * 2 * HALF] = pair.astype(up_ref.dtype)
            a, val = pair[:, :HALF], pair[:, HALF:]
            conv = (bc_ref[:, cols] + wc_ref[0:1, cols] * a + wc_ref[1:2, cols] * _shift_down(a, 1, row)
                    + wc_ref[2:3, cols] * _shift_down(a, 2, row))
            ff_ref[:, cols] = (conv * _sigmoid(conv) * val).astype(ff_ref.dtype)

    return pl.pallas_call(
        body, name="ffn_up_conv_gate", grid=(UP_SLOTS, B),
        in_specs=[pl.BlockSpec((None, S, K), lambda j, b: (b, 0, 0)), _up_weight_spec(K, lambda j, b: (0, j, 0, 0)),
                  pl.BlockSpec((3, 2 * HALF), lambda j, b: (0, j)), pl.BlockSpec((1, 2 * HALF), lambda j, b: (0, j))],
        out_specs=[pl.BlockSpec((None, S, UP_GROUP), lambda j, b: (b, 0, j)),
                   pl.BlockSpec((None, S, 2 * HALF), lambda j, b: (b, 0, j))],
        out_shape=[jax.ShapeDtypeStruct((B, S, UP_SLOTS * UP_GROUP), BF16), jax.ShapeDtypeStruct((B, S, D_FF), BF16)],
        compiler_params=_params(2),
    )(u, w3.reshape(2, UP_SLOTS, K, 2 * HALF), w_conv, b_conv)


def _conv_bwd(up, d_down, w_down, w_conv, b_conv):
    B, S, _ = up.shape
    nj = D_FF // CONV_COLS

    def body(up_ref, dd_ref, wd_ref, w_ref, b_ref, dup_ref, dw_ref, db_ref):
        b = pl.program_id(1)
        a = up_ref[:, :CONV_COLS].astype(F32)
        val = up_ref[:, CONV_COLS:].astype(F32)
        row = lax.broadcasted_iota(jnp.int32, a.shape, 0)
        w0, w1, w2 = w_ref[0:1, :], w_ref[1:2, :], w_ref[2:3, :]
        a1, a2 = _shift_down(a, 1, row), _shift_down(a, 2, row)
        conv = b_ref[...] + w0 * a + w1 * a1 + w2 * a2
        sg = _sigmoid(conv)
        dff = lax.dot_general(dd_ref[...], wd_ref[...], _NT, preferred_element_type=F32)
        d_val = dff * conv * sg
        dc = dff * val * (sg * (1.0 + conv * (1.0 - sg)))
        d_a = w0 * dc + w1 * _shift_up(dc, 1, row, S) + w2 * _shift_up(dc, 2, row, S)
        dup_ref[:, :CONV_COLS] = d_a.astype(dup_ref.dtype)
        dup_ref[:, CONV_COLS:] = d_val.astype(dup_ref.dtype)

        @pl.when(b == 0)
        def _():
            dw_ref[...] = jnp.zeros_like(dw_ref)
            db_ref[...] = jnp.zeros_like(db_ref)

        dw_ref[0:1, :] += _col_sum(dc * a)
        dw_ref[1:2, :] += _col_sum(dc * a1)
        dw_ref[2:3, :] += _col_sum(dc * a2)
        db_ref[...] += _col_sum(dc)

    return pl.pallas_call(
        body, name="conv_gate_bwd", grid=(nj, B),
        in_specs=[pl.BlockSpec((None, S, 2 * CONV_COLS), lambda j, b: (b, 0, j)),
                  pl.BlockSpec((None, S, D_MODEL), lambda j, b: (b, 0, 0)),
                  pl.BlockSpec((CONV_COLS, D_MODEL), lambda j, b: (j, 0)),
                  pl.BlockSpec((3, CONV_COLS), lambda j, b: (0, j)),
                  pl.BlockSpec((1, CONV_COLS), lambda j, b: (0, j))],
        out_specs=[pl.BlockSpec((None, S, 2 * CONV_COLS), lambda j, b: (b, 0, j)),
                   pl.BlockSpec((3, CONV_COLS), lambda j, b: (0, j)),
                   pl.BlockSpec((1, CONV_COLS), lambda j, b: (0, j))],
        out_shape=[jax.ShapeDtypeStruct((B, S, 2 * D_FF), BF16), jax.ShapeDtypeStruct((3, D_FF), F32),
                   jax.ShapeDtypeStruct((1, D_FF), F32)],
        compiler_params=_params(2),
    )(up, d_down, w_down, w_conv, b_conv)


def _ada_fwd(c_all, w_ada, b_ada):
    def body(c_ref, w_ref, b_ref, o_ref):
        cv = c_ref[...]
        act = (cv * _sigmoid(cv)).astype(BF16)
        o_ref[...] = jnp.dot(act, w_ref[...].astype(BF16), preferred_element_type=F32) + b_ref[...]

    return pl.pallas_call(body, name="ada_fwd",
                          out_shape=jax.ShapeDtypeStruct((c_all.shape[0], w_ada.shape[1]), F32),
                          compiler_params=pltpu.CompilerParams(vmem_limit_bytes=V7X_VMEM_LIMIT))(c_all, w_ada, b_ada)


def _ada_bwd(c_all, dmod_all, dmod_cols):
    def body(c_ref, dm_ref, dmc_ref, dw_ref, db_ref):
        cv = c_ref[...]
        act = (cv * _sigmoid(cv)).astype(BF16)
        dw_ref[...] = lax.dot_general(act, dmc_ref[...].astype(BF16), _TN, preferred_element_type=F32)
        db_ref[...] = _col_sum(dm_ref[...])

    return pl.pallas_call(
        body, name="ada_bwd",
        out_shape=[jax.ShapeDtypeStruct((c_all.shape[1], dmod_cols.shape[1]), F32),
                   jax.ShapeDtypeStruct((1, dmod_all.shape[1]), F32)],
        compiler_params=pltpu.CompilerParams(vmem_limit_bytes=V7X_VMEM_LIMIT))(c_all, dmod_all, dmod_cols)


def _adamw(w, m, v, g_parts, name, own=None):
    R, C = w.shape
    P = g_parts.shape[0]
    tr = R
    for cand in (256, 128, 64, 32, 16, 8):
        if R % cand == 0 and cand * C * 4 * (P + 8) * 2 <= V7X_VMEM_LIMIT // 2:
            tr = cand
            break
    c1 = 1.0 / (1.0 - ADAM_B1 ** ADAM_STEP)
    c2 = 1.0 / (1.0 - ADAM_B2 ** ADAM_STEP)

    def update(w_ref, m_ref, v_ref, g, og, od, om, ov):
        m_new = ADAM_B1 * m_ref[...] + (1.0 - ADAM_B1) * g
        v_new = ADAM_B2 * v_ref[...] + (1.0 - ADAM_B2) * (g * g)
        og[...] = g
        om[...] = m_new
        ov[...] = v_new
        od[...] = -ADAM_LR * ((m_new * c1) / (jnp.sqrt(v_new * c2) + ADAM_EPS) + ADAM_WD * w_ref[...])

    def total(g_ref):
        g = g_ref[0].astype(F32)
        for p in range(1, P):
            g = g + g_ref[p].astype(F32)
        return g

    out_shape = [jax.ShapeDtypeStruct((R, C), F32)] * 4
    if own is None:
        def body(w_ref, m_ref, v_ref, g_ref, og, od, om, ov):
            update(w_ref, m_ref, v_ref, total(g_ref), og, od, om, ov)

        spec = pl.BlockSpec((tr, C), lambda i: (i, 0))
        return pl.pallas_call(
            body, name=name, grid=(R // tr,),
            in_specs=[spec, spec, spec, pl.BlockSpec((P, tr, C), lambda i: (0, i, 0))],
            out_specs=[spec] * 4, out_shape=out_shape, compiler_params=_params(1),
        )(w, m, v, g_parts)

    slots, me = own

    def body_own(me_ref, w_ref, m_ref, v_ref, g_ref, own_ref, og, od, om, ov):
        g = own_ref[...].astype(F32)
        for p in range(P):
            g = g + jnp.where(me_ref[0] == p, 0.0, g_ref[p].astype(F32))
        update(w_ref, m_ref, v_ref, g, og, od, om, ov)

    spec = pl.BlockSpec((tr, C), lambda i, me_ref: (i, 0))
    grid_spec = pltpu.PrefetchScalarGridSpec(
        num_scalar_prefetch=1, grid=(R // tr,),
        in_specs=[spec, spec, spec, pl.BlockSpec((P, tr, C), lambda i, me_ref: (0, i, 0)),
                  pl.BlockSpec((None, tr, C), lambda i, me_ref: (me_ref[0], i, 0))],
        out_specs=[spec] * 4)
    return pl.pallas_call(body_own, name=name, grid_spec=grid_spec, out_shape=out_shape,
                          compiler_params=_params(1))(me, w, m, v, g_parts, slots)


def _adamw_small(ws, ms, vs, gs):
    n = len(ws)
    c1 = 1.0 / (1.0 - ADAM_B1 ** ADAM_STEP)
    c2 = 1.0 / (1.0 - ADAM_B2 ** ADAM_STEP)

    def body(*refs):
        ins, outs = refs[:4 * n], refs[4 * n:]
        for i in range(n):
            w, m, v, g = ins[i][...], ins[n + i][...], ins[2 * n + i][...], ins[3 * n + i][...]
            m_new = ADAM_B1 * m + (1.0 - ADAM_B1) * g
            v_new = ADAM_B2 * v + (1.0 - ADAM_B2) * (g * g)
            outs[4 * i][...] = g
            outs[4 * i + 1][...] = -ADAM_LR * ((m_new * c1) / (jnp.sqrt(v_new * c2) + ADAM_EPS) + ADAM_WD * w)
            outs[4 * i + 2][...] = m_new
            outs[4 * i + 3][...] = v_new

    out_shape = [jax.ShapeDtypeStruct(w.shape, F32) for w in ws for _ in range(4)]
    return pl.pallas_call(body, name="adamw_small", out_shape=out_shape,
                          compiler_params=pltpu.CompilerParams(vmem_limit_bytes=V7X_VMEM_LIMIT))(*ws, *ms, *vs, *gs)


def _sum_parts(parts, loss_rows):
    P, R, C = parts.shape
    lo, hi = loss_rows

    def body(p_ref, o_ref, loss_ref):
        t = p_ref[0]
        for p in range(1, P):
            t = t + p_ref[p]
        o_ref[...] = t
        tot = jnp.sum(jnp.sum(o_ref[lo:hi, :], axis=1, keepdims=True), axis=0, keepdims=True)
        loss_ref[...] = jnp.broadcast_to(tot, loss_ref.shape)

    return pl.pallas_call(body, name="sum_small_grads",
                          out_shape=[jax.ShapeDtypeStruct((R, C), F32), jax.ShapeDtypeStruct((1, LANES), F32)],
                          compiler_params=pltpu.CompilerParams(vmem_limit_bytes=V7X_VMEM_LIMIT))(parts)


def _exchange(items, name):
    n = len(items)
    MESH = pl.DeviceIdType.MESH

    def body(*refs):
        src, dst = refs[:n], refs[n:2 * n]
        send_sems, recv_sems, local_sems = refs[2 * n:]
        x, y, c = lax.axis_index("x"), lax.axis_index("y"), lax.axis_index("c")
        me = 4 * x + 2 * y + c
        started = []
        for it, (_, per_peer) in enumerate(items):
            own = pltpu.make_async_copy(src[it].at[me] if per_peer else src[it], dst[it].at[me], local_sems.at[it])
            own.start()
            started.append(own)
        sends, recvs = [], []
        for k in range(1, N_DEV):
            px = 1 - x if k & 4 else x
            py = 1 - y if k & 2 else y
            pc = 1 - c if k & 1 else c
            peer = 4 * px + 2 * py + pc
            for it, (_, per_peer) in enumerate(items):
                s = src[it].at[peer] if per_peer else src[it]
                cp = pltpu.make_async_remote_copy(src_ref=s, dst_ref=dst[it].at[me], send_sem=send_sems.at[it, k - 1],
                                                  recv_sem=recv_sems.at[it, k - 1], device_id=(px, py, pc),
                                                  device_id_type=MESH)
                cp.start()
                sends.append(cp)
                recvs.append(pltpu.make_async_remote_copy(
                    src_ref=s, dst_ref=dst[it].at[peer], send_sem=send_sems.at[it, k - 1],
                    recv_sem=recv_sems.at[it, k - 1], device_id=(px, py, pc), device_id_type=MESH))
        for cp in recvs:
            cp.wait_recv()
        for cp in sends:
            cp.wait_send()
        for cp in started:
            cp.wait()

    any_spec = pl.BlockSpec(memory_space=pl.ANY)
    out_shape = []
    for a, per_peer in items:
        shp = a.shape if per_peer else (N_DEV,) + a.shape
        out_shape.append(jax.ShapeDtypeStruct(shp, a.dtype))
    return pl.pallas_call(
        body, name=name, in_specs=[any_spec] * n, out_specs=[any_spec] * n, out_shape=out_shape,
        scratch_shapes=[pltpu.SemaphoreType.DMA((n, N_DEV - 1)), pltpu.SemaphoreType.DMA((n, N_DEV - 1)),
                        pltpu.SemaphoreType.DMA((n,))],
    )(*[a for a, _ in items])


def _remote(src, dst, send_sem, recv_sem, device):
    return pltpu.make_async_remote_copy(src_ref=src, dst_ref=dst, send_sem=send_sem, recv_sem=recv_sem,
                                        device_id=device, device_id_type=pl.DeviceIdType.MESH)


def _mesh_place():
    x, y, c = lax.axis_index("x"), lax.axis_index("y"), lax.axis_index("c")
    other_chips = [(1 - x, y), (x, 1 - y), (1 - x, 1 - y)]
    return x, y, c, (x, y, 1 - c), other_chips


def _peers():
    x, y, c = lax.axis_index("x"), lax.axis_index("y"), lax.axis_index("c")
    out = []
    for k in range(1, N_DEV):
        px = 1 - x if k & 4 else x
        py = 1 - y if k & 2 else y
        pc = 1 - c if k & 1 else c
        out.append((k, (px, py, pc), 4 * px + 2 * py + pc))
    return 4 * x + 2 * y + c, out


FIRST_LEVEL = (1, 2, 4, 6)


def _exchange_start(items, name, gather, carry=(), first_level=False):
    n, m = len(items), len(carry)

    def body(*refs):
        src, land = refs[:n], refs[n:2 * n]
        first_out = 2 * n + m
        send_sems, recv_sems = refs[first_out:first_out + n], refs[first_out + n:first_out + 2 * n]
        token = refs[-1]
        me, peers = _peers()
        if first_level:
            peers = [p for p in peers if p[0] in FIRST_LEVEL]
        for k, peer, slot in peers:
            for it in range(n):
                _remote(src[it] if gather else src[it].at[slot], land[it].at[me], send_sems[it], recv_sems[it],
                        peer).start()
        token[...] = jnp.zeros_like(token)

    hbm = pl.BlockSpec(memory_space=pltpu.HBM)
    sem = pl.BlockSpec(memory_space=pltpu.SEMAPHORE)
    land_shapes = [(N_DEV,) + (a.shape if gather else a.shape[1:]) for a in items]
    lands = [lax.empty(shp, a.dtype) for shp, a in zip(land_shapes, items)]
    through = list(items) + lands + list(carry)
    outs = pl.pallas_call(
        body, name=name,
        out_shape=(*[pltpu.SemaphoreType.DMA(())] * (2 * n), *[pltpu.HBM(a.shape, a.dtype) for a in through],
                   jax.ShapeDtypeStruct((8, LANES), F32)),
        in_specs=[hbm] * len(through),
        out_specs=(*[sem] * (2 * n), *[hbm] * len(through), pl.BlockSpec(memory_space=pltpu.VMEM)),
        input_output_aliases={i: 2 * n + i for i in range(len(through))},
        compiler_params=pltpu.CompilerParams(has_side_effects=pltpu.SideEffectType.DATAFLOW_SIDE_EFFECTING),
    )(*[pltpu.with_memory_space_constraint(a, pltpu.HBM) for a in through])
    return (list(outs[:n]), list(outs[n:2 * n]), list(outs[2 * n:3 * n]), list(outs[3 * n:4 * n]), outs[-1],
            list(outs[4 * n:4 * n + m]))


def _exchange_wait(send_sems, recv_sems, items, lands, after, name, copies=N_DEV - 1):
    n = len(items)
    after = list(after) if isinstance(after, (list, tuple)) else [after]

    def body(*refs):
        land = refs[n:2 * n]
        send_sems, recv_sems = refs[2 * n:3 * n], refs[3 * n:4 * n]
        me, peers = _peers()
        for it in range(n):
            seven = land[it].at[pl.ds(0, copies)]
            cp = _remote(seven, seven, send_sems[it], recv_sems[it], peers[0][1])
            cp.wait_send()
            cp.wait_recv()

    hbm = pl.BlockSpec(memory_space=pltpu.HBM)
    sem = pl.BlockSpec(memory_space=pltpu.SEMAPHORE)
    outs = pl.pallas_call(
        body, name=name,
        out_shape=tuple(pltpu.HBM(a.shape, a.dtype) for a in list(items) + list(lands)),
        in_specs=[hbm] * (2 * n) + [sem] * (2 * n) + [pl.BlockSpec(memory_space=pl.ANY)] * len(after),
        out_specs=tuple([hbm] * (2 * n)),
        input_output_aliases={i: i for i in range(2 * n)},
        compiler_params=pltpu.CompilerParams(has_side_effects=pltpu.SideEffectType.DATAFLOW_SIDE_EFFECTING),
    )(*items, *lands, *send_sems, *recv_sems, *after)
    return list(outs[:n]), list(outs[n:])


def _gather_finish(lands, name):
    n = len(lands)

    def body(*refs):
        land = refs[n:2 * n]
        pass_send, pass_recv = refs[-2], refs[-1]
        x, y, c, sibling, chips = _mesh_place()
        passed = []
        for j, (px, py) in enumerate(chips):
            for it in range(n):
                blk = land[it].at[4 * px + 2 * py + c]
                cp = _remote(blk, blk, pass_send.at[it, j], pass_recv.at[it, j], sibling)
                cp.start()
                passed.append(cp)
        for j, (px, py) in enumerate(chips):
            for it in range(n):
                blk = land[it].at[4 * px + 2 * py + 1 - c]
                _remote(blk, blk, pass_send.at[it, j], pass_recv.at[it, j], sibling).wait_recv()
        for cp in passed:
            cp.wait_send()

    any_spec = pl.BlockSpec(memory_space=pl.ANY)
    return pl.pallas_call(
        body, name=name, in_specs=[any_spec] * n, out_specs=[any_spec] * n,
        out_shape=[jax.ShapeDtypeStruct(a.shape, a.dtype) for a in lands],
        scratch_shapes=[pltpu.SemaphoreType.DMA((n, 3)), pltpu.SemaphoreType.DMA((n, 3))],
        input_output_aliases={i: i for i in range(n)},
    )(*lands)


def _gelu_tanh(y):
    k = math.sqrt(2.0 / math.pi)
    t = jnp.tanh(k * (y + 0.044715 * y * y * y))
    return 0.5 * y * (1.0 + t), t


def _local_step(x, mod, target, W, late_weights, P, send_early):
    B, S, D = x.shape
    T = B * S
    TS = 512
    flat = lambda a: a.reshape(T, a.shape[-1])
    unflat = lambda a: a.reshape(B, S, a.shape[-1])
    mod_col = lambda i: (mod, D, i)

    def f_modnorm_project(xv, sc, sh, g, wq, wu, wg):
        u = ((xv * _rms_scale(xv) * g) * (1.0 + sc) + sh).astype(BF16)
        return (u, lax.dot_general(u, wq, _NT, preferred_element_type=F32),
                lax.dot_general(u, wu, _NT, preferred_element_type=F32),
                lax.dot_general(u, wg, _NT, preferred_element_type=F32))

    u1, qkv, us, gates = _rowwise(
        f_modnorm_project, [(x, D, 0)], [mod_col(1), mod_col(0)], [P["g_mix"], W["w_qkv"], W["w_us"], W["w_gates"]],
        [(D, BF16), (3 * ATT_WIDTH, F32), (SSM_WIDTH, F32), (2 * D, BF16)], [], [], ts=TS, name="modnorm_project_in")
    u1f = flat(u1)

    o_att, lse = _attention_fwd(qkv, P["slopes"])
    more_w, more_p = late_weights(o_att)
    W, P = {**W, **more_w}, {**P, **more_p}

    xs, y_mm = _scan_fwd(us, P["bb_big"], P["a_row"], P["cc_big"])

    bga, bgs = P["b_gate"][:, :D], P["b_gate"][:, D:]

    def f_mixer_tail(ov, ymm, usv, ga, gs, xv, gt, sc, sh, w_att, w_ssm, w_o, bga_, bgs_, g, dsk, wg, bg):
        yv = ymm + dsk * usv
        ge, _ = _gelu_tanh(yv)
        zv = (ge * _sigmoid(jnp.dot(ge.astype(BF16), wg, preferred_element_type=F32) + bg)).astype(BF16)
        ya = jnp.dot(ov, w_att, preferred_element_type=F32)
        ys = jnp.dot(zv, w_ssm, preferred_element_type=F32)
        mg = (_sigmoid(ga + bga_) * ya + _sigmoid(gs + bgs_) * ys).astype(BF16)
        mx = jnp.dot(mg, w_o, preferred_element_type=F32)
        h = xv + gt * mx
        return yv, zv, mg, mx, h, (h * _rms_scale(h) * g) * (1.0 + sc) + sh

    y_s5, z, merged, mix, h1, u2 = _rowwise(
        f_mixer_tail,
        [(o_att, ATT_WIDTH, 0), (y_mm, SSM_WIDTH, 0), (us, SSM_WIDTH, 0), (gates, D, 0), (gates, D, 1), (x, D, 0)],
        [mod_col(2), mod_col(4), mod_col(3)],
        [W["w_proj_att"], W["w_proj_ssm"], W["w_out"], bga, bgs, P["g_ffn"], P["d_skip"], W["w_glu"], P["b_glu"]],
        [(SSM_WIDTH, F32), (SSM_WIDTH, BF16), (D, BF16), (D, BF16), (D, F32), (D, BF16)],
        [], [], ts=TS, raw=(0,), name="mixer_tail")

    up, ff = _up_conv_fwd(u2, W["w_up"], P["w_conv"], P["b_conv"])

    def f_head(ffv, h1v, tg, gt, g, w_dn):
        dn = jnp.dot(ffv, w_dn, preferred_element_type=F32)
        h2 = h1v + gt * dn
        r = _rms_scale(h2)
        nh = h2 * r
        e = nh * g - tg
        dy = e * (1.0 / D)
        gy = dy * g
        dh = r * (gy - nh * jnp.mean(gy * nh, axis=-1, keepdims=True))
        return (dh, dh * gt, _col_sum(dh * dn), _col_sum(dy * nh), _col_sum(e * e) * (0.5 / D))

    dh2, d_down, d_gt2, d_g_final, loss_cols = _rowwise(
        f_head, [(ff, D_FF, 0), (h1, D, 0), (target, D, 0)], [mod_col(5)], [P["g_final"], W["w_down"]],
        [(D, BF16), (D, BF16)], [D], [(1, D), (1, D)], ts=TS, raw=(0,), name="ffn_down_head_loss")

    d_downf = flat(d_down)
    d_w_down = _matmul(flat(ff), d_downf, ta=True, out_dtype=BF16, name="ffn_down_dw")
    d_up, d_w_conv, d_b_conv = _conv_bwd(up, d_down, W["w_down"], P["w_conv"], P["b_conv"])
    d_upf = flat(d_up)
    d_w_up = _up_dw(flat(u2), d_upf, name="ffn_up_dw")
    token, _ = send_early(dict(w_down=d_w_down.reshape(N_DEV, D_FF // N_DEV, D), w_up=d_w_up))
    g_ffn_after = P["g_ffn"] + token[0:1, 0:1]

    def f_up_back_modnorm(dup, h, dres, mx, sc, gt, g, w2):
        du = None
        for j in range(UP_SLOTS):
            wa, wv = w2[j * D:(j + 1) * D, :], w2[(j + UP_SLOTS) * D:(j + UP_SLOTS + 1) * D, :]
            wj = jnp.concatenate([wa[:, :HALF], wv[:, :HALF], wa[:, HALF:], wv[:, HALF:]], axis=1)
            part = lax.dot_general(dup[:, j * UP_GROUP:(j + 1) * UP_GROUP], wj, _NT, preferred_element_type=F32)
            du = part if du is None else du + part
        r = _rms_scale(h)
        nh = h * r
        dn = du * (1.0 + sc)
        gy = dn * g
        dh = dres + r * (gy - nh * jnp.mean(gy * nh, axis=-1, keepdims=True))
        return (dh, dh * gt, _col_sum(du), _col_sum(du * nh * g), _col_sum(dh * mx), _col_sum(dn * nh))

    dh1, d_mix, d_sh2, d_sc2, d_gt1, d_g_ffn = _rowwise(
        f_up_back_modnorm, [(d_up, 2 * D_FF, 0), (h1, D, 0), (dh2, D, 0), (mix, D, 0)], [mod_col(4), mod_col(2)],
        [g_ffn_after, W["w_up"].reshape(N_DEV * D, 2 * HALF)],
        [(D, BF16), (D, BF16)], [D, D, D], [(1, D)], ts=TS, raw=(0,),
        name="ffn_up_back_modnorm", split=2)

    d_mixf = flat(d_mix)
    d_w_out = _matmul(flat(merged), d_mixf, ta=True, out_dtype=BF16, name="proj_out_dw")

    def f_mixer_tail_bwd(dmx, ov, zv, ga, gs, yv, usv, w_o, w_att, w_ssm, bga_, bgs_, dsk, wg, bg):
        dm = lax.dot_general(dmx, w_o, _NT, preferred_element_type=F32)
        ya = jnp.dot(ov, w_att, preferred_element_type=F32)
        ys = jnp.dot(zv, w_ssm, preferred_element_type=F32)
        sa, ss = _sigmoid(ga + bga_), _sigmoid(gs + bgs_)
        dga = dm * ya * sa * (1.0 - sa)
        dgs = dm * ys * ss * (1.0 - ss)
        dya, dys = (dm * sa).astype(BF16), (dm * ss).astype(BF16)
        d_o = lax.dot_general(dya, w_att, _NT, preferred_element_type=F32)
        dz = lax.dot_general(dys, w_ssm, _NT, preferred_element_type=F32)
        ge, t = _gelu_tanh(yv)
        sg = _sigmoid(jnp.dot(ge.astype(BF16), wg, preferred_element_type=F32) + bg)
        dpre = dz * ge * sg * (1.0 - sg)
        dge = dz * sg + lax.dot_general(dpre.astype(BF16), wg, _NT, preferred_element_type=F32)
        k = math.sqrt(2.0 / math.pi)
        dgelu = 0.5 * (1.0 + t) + 0.5 * yv * (1.0 - t * t) * k * (1.0 + 3.0 * 0.044715 * yv * yv)
        dy = dge * dgelu
        dwg = lax.dot_general(ge.astype(BF16), dpre.astype(BF16), _TN, preferred_element_type=F32)
        return (dya, dys, jnp.concatenate([dga, dgs], axis=1), d_o, dy, dy * dsk,
                _col_sum(dga), _col_sum(dgs), dwg, _col_sum(dpre), _col_sum(dy * usv))

    (d_y_att, d_y_ssm, d_gates, d_o_att, d_y_s5, d_us_skip, d_bga, d_bgs, d_w_glu, d_b_glu, d_d_skip) = _rowwise(
        f_mixer_tail_bwd,
        [(d_mix, D, 0), (o_att, ATT_WIDTH, 0), (z, SSM_WIDTH, 0), (gates, D, 0), (gates, D, 1),
         (y_s5, SSM_WIDTH, 0), (us, SSM_WIDTH, 0)], [],
        [W["w_out"], W["w_proj_att"], W["w_proj_ssm"], bga, bgs, P["d_skip"], W["w_glu"], P["b_glu"]],
        [(D, BF16), (D, BF16), (2 * D, BF16), (ATT_WIDTH, F32), (SSM_WIDTH, BF16), (SSM_WIDTH, F32)], [],
        [(1, D), (1, D), (SSM_WIDTH, SSM_WIDTH), (1, SSM_WIDTH), (1, SSM_WIDTH)], ts=TS, raw=(0, 1, 2),
        name="mixer_tail_bwd")

    d_yaf, d_ysf = flat(d_y_att), flat(d_y_ssm)
    d_w_proj_att = _matmul(flat(o_att), d_yaf, ta=True, out_dtype=BF16, name="proj_att_dw")
    d_w_proj_ssm = _matmul(flat(z), d_ysf, ta=True, out_dtype=BF16, name="proj_ssm_dw")
    d_us_parts, g_ab, d_bb, d_cc = _scan_bwd(d_y_s5, us, P["bb_big"], P["cc_big"], xs, P["a_row"])

    token, _ = send_early(dict(
        w_out=d_w_out.reshape(N_DEV, D // N_DEV, D), w_proj_att=_cols_to_slots(d_w_proj_att),
        w_proj_ssm=_cols_to_slots(d_w_proj_ssm),
        w_glu=d_w_glu.astype(BF16).reshape(N_DEV, SSM_WIDTH // N_DEV, SSM_WIDTH),
        w_conv=_cols_to_slots(d_w_conv.astype(BF16))))
    d_qkv = _attention_bwd(qkv, o_att, d_o_att, lse, P["slopes"] + token[0, 0])

    def f_add(*parts):
        return sum(parts[1:], parts[0])

    n_parts = d_us_parts.shape[0]
    stacked = d_us_parts.reshape(n_parts * B, S, SSM_WIDTH)
    (d_us,) = _rowwise(f_add, [(d_us_skip, SSM_WIDTH, 0)] + [(stacked, SSM_WIDTH, 0, j * B) for j in range(n_parts)],
                       [], [],
                       [(SSM_WIDTH, BF16)], [], [], ts=TS, name="s5_input_grad")
    d_qkvf = flat(d_qkv)
    d_usf = flat(d_us)
    d_gatesf = flat(d_gates)
    d_w_in_t = jnp.concatenate(
        [_unpair_qkv_rows(_matmul(d_qkvf, u1f, ta=True, out_dtype=BF16, name="proj_qkv_dw")),
         _matmul(d_usf, u1f, ta=True, out_dtype=BF16, name="proj_ssm_in_dw"),
         _matmul(d_gatesf, u1f, ta=True, out_dtype=BF16, name="proj_gates_dw")], axis=0)
    token, (w_qkv, w_us, w_gates) = send_early(dict(w_in=d_w_in_t.reshape(N_DEV, -1, D)),
                                               carry=[W["w_qkv"], W["w_us"], W["w_gates"]])
    def f_project_back_modnorm(dq, du_, dg, h, dres, sc, g, wq, wu, wg):
        du = (jnp.dot(dq, wq, preferred_element_type=F32) + jnp.dot(du_, wu, preferred_element_type=F32)
              + jnp.dot(dg, wg, preferred_element_type=F32))
        r = _rms_scale(h)
        nh = h * r
        dn = du * (1.0 + sc)
        gy = dn * g
        dh = dres + r * (gy - nh * jnp.mean(gy * nh, axis=-1, keepdims=True))
        return (dh, _col_sum(du), _col_sum(du * nh * g), _col_sum(dn * nh))

    grad_x, d_sh1, d_sc1, d_g_mix = _rowwise(
        f_project_back_modnorm,
        [(d_qkv, 3 * ATT_WIDTH, 0), (d_us, SSM_WIDTH, 0), (d_gates, 2 * D, 0), (x, D, 0), (dh1, D, 0)], [mod_col(1)],
        [P["g_mix"] + token[0:1, 0:1], w_qkv, w_us, w_gates],
        [(D, F32)], [D, D], [(1, D)], ts=TS, raw=(0, 1, 2),
        name="project_in_back_modnorm", split=2)

    d_mod = jnp.concatenate([d_sh1, d_sc1, d_gt1, d_sh2, d_sc2, d_gt2], axis=-1)
    g_ab_re, g_ab_im = _deinterleave(g_ab)
    d_bb_re, d_bb_im = _deinterleave(d_bb)
    d_cc_re, d_cc_im = (t.T for t in _deinterleave(d_cc.T))
    small = dict(g_mix=d_g_mix, b_gate=jnp.concatenate([d_bga, d_bgs], axis=1), g_ab_re=g_ab_re, g_ab_im=g_ab_im,
                 d_bb_re=d_bb_re, d_bb_im=d_bb_im, d_cc_re=d_cc_re, d_cc_im=d_cc_im, d_skip=d_d_skip,
                 b_glu=d_b_glu, g_ffn=d_g_ffn, b_conv=d_b_conv, g_final=d_g_final, loss_cols=loss_cols)
    return grad_x, d_mod, small


def _block_diag_in(bb):
    t = bb.reshape(SSM_GROUPS, SSM_STATE, SSM_GROUP_CH)
    eye = jnp.eye(SSM_GROUPS, dtype=bb.dtype)
    return jnp.einsum("gnc,gh->gchn", t, eye).reshape(SSM_WIDTH, SSM_COLS)


def _block_diag_out(cm):
    eye = jnp.eye(SSM_GROUPS, dtype=cm.dtype)
    return jnp.einsum("gcn,gh->gnhc", cm, eye).reshape(SSM_COLS, SSM_WIDTH)


def _diag_blocks_in(m):
    t = m.reshape(SSM_GROUPS, SSM_GROUP_CH, SSM_GROUPS, SSM_STATE)
    idx = jnp.arange(SSM_GROUPS)
    return t[idx, :, idx, :].transpose(0, 2, 1).reshape(SSM_COLS, SSM_GROUP_CH)


def _diag_blocks_out(m):
    t = m.reshape(SSM_GROUPS, SSM_STATE, SSM_GROUPS, SSM_GROUP_CH)
    idx = jnp.arange(SSM_GROUPS)
    return t[idx, :, idx, :].transpose(0, 2, 1)


def _pair_qkv_rows(w):
    return w.reshape(3, N_HEADS // 2, LANES, w.shape[1]).swapaxes(0, 1).reshape(w.shape)


def _unpair_qkv_rows(w):
    return w.reshape(N_HEADS // 2, 3, LANES, w.shape[1]).swapaxes(0, 1).reshape(w.shape)


def _interleave(re, im):
    lead = re.shape[:-1]
    g = lambda a: a.reshape(lead + (SSM_COLS // SCAN_COLS, 1, SCAN_COLS))
    return jnp.concatenate([g(re), g(im)], axis=-2).reshape(lead + (2 * SSM_COLS,))


def _deinterleave(x):
    lead = x.shape[:-1]
    t = x.reshape(lead + (SSM_COLS // SCAN_COLS, 2, SCAN_COLS))
    return t[..., 0, :].reshape(lead + (SSM_COLS,)), t[..., 1, :].reshape(lead + (SSM_COLS,))


def _cols_to_slots(g):
    R = g.shape[0]
    return g.reshape(R, N_DEV, g.shape[1] // N_DEV).transpose(1, 0, 2)


def _slots_to_cols(g):
    return g.transpose(1, 0, 2).reshape(g.shape[1], N_DEV * g.shape[2])


SMALL_ORDER = ("b_ada", "g_mix", "b_gate", "a_re", "a_im", "log_dt", "b_re", "b_im", "c_re", "c_im", "d_skip",
               "b_glu", "g_ffn", "b_conv", "g_final")


def _pack(arrs):
    pieces, offs, row = [], [], 0
    for a in arrs:
        f = a.reshape(-1).astype(F32)
        n = f.shape[0]
        rows = -(-n // LANES)
        pieces.append(jnp.pad(f, (0, rows * LANES - n)))
        offs.append((row, n))
        row += rows
    return jnp.concatenate(pieces).reshape(row, LANES), offs


def _unpack(packed, offs, shapes):
    flat = packed.reshape(-1)
    return [flat[r * LANES:r * LANES + n].reshape(s) for (r, n), s in zip(offs, shapes)]


def kernel(x, c, w_ada, b_ada, g_mix, w_in, b_gate, a_re, a_im, log_dt, b_re, b_im, c_re, c_im, d_skip, w_glu, b_glu, w_proj_att, w_proj_ssm, w_out, g_ffn, w_up, w_conv, b_conv, w_down, g_final, loss_target, m_w_ada, m_b_ada, m_g_mix, m_w_in, m_b_gate, m_a_re, m_a_im, m_log_dt, m_b_re, m_b_im, m_c_re, m_c_im, m_d_skip, m_w_glu, m_b_glu, m_w_proj_att, m_w_proj_ssm, m_w_out, m_g_ffn, m_w_up, m_w_conv, m_b_conv, m_w_down, m_g_final, v_w_ada, v_b_ada, v_g_mix, v_w_in, v_b_gate, v_a_re, v_a_im, v_log_dt, v_b_re, v_b_im, v_c_re, v_c_im, v_d_skip, v_w_glu, v_b_glu, v_w_proj_att, v_w_proj_ssm, v_w_out, v_g_ffn, v_w_up, v_w_conv, v_b_conv, v_w_down, v_g_final):
    args = dict(locals())
    B, S, D = x.shape
    me = 4 * lax.axis_index("x") + 2 * lax.axis_index("y") + lax.axis_index("c")
    bf = lambda w: w[0].astype(BF16)

    first = [c, w_in[0].T.astype(BF16)]
    first_sems = _exchange_start(first, "start_first_weights", gather=True, carry=[log_dt[0]], first_level=True)
    (log_dt_sent,) = first_sems[5]

    ab_re, ab_im, f_re, f_im = _s5_params(a_re[0], a_im[0], log_dt_sent.reshape(SSM_GROUPS, 1))
    col = lambda a: a.reshape(SSM_COLS, 1)
    b_re2, b_im2 = b_re[0].reshape(SSM_COLS, SSM_GROUP_CH), b_im[0].reshape(SSM_COLS, SSM_GROUP_CH)
    bb_re, bb_im = _s5_input_matrix(col(f_re), col(f_im), b_re2, b_im2)
    slopes = jnp.asarray([2.0 ** (-8.0 * (h + 1) / N_HEADS) for h in range(N_HEADS)], F32)
    P = dict(g_mix=g_mix, g_ffn=g_ffn, g_final=g_final.reshape(1, D), b_gate=b_gate, d_skip=d_skip, b_glu=b_glu,
             b_conv=b_conv, slopes=slopes,
             a_row=_interleave(ab_re.reshape(1, SSM_COLS), ab_im.reshape(1, SSM_COLS)),
             cc_big=_interleave(_block_diag_out(c_re[0]).T, -_block_diag_out(c_im[0]).T).T,
             bb_big=_interleave(_block_diag_in(bb_re), _block_diag_in(bb_im)))

    later = [bf(w_glu), bf(w_proj_att), bf(w_proj_ssm), bf(w_out), bf(w_up), w_conv[0], bf(w_down)]

    prepared = [P["bb_big"], P["cc_big"], P["a_row"]] + later
    own_first, first_lands = _exchange_wait(*first_sems[:4], prepared, name="wait_first_weights",
                                            copies=len(FIRST_LEVEL))
    first_lands = _gather_finish(first_lands, name="pass_on_first_weights")
    c_slots, w_in_slots = [lax.dynamic_update_index_in_dim(land, a, me, 0) for land, a in zip(first_lands, own_first)]
    c_all = c_slots.reshape(N_DEV * B, D)
    w_in_t = w_in_slots.reshape(-1, D)
    n_qkv = 3 * ATT_WIDTH
    W = dict(w_qkv=_pair_qkv_rows(w_in_t[:n_qkv]), w_us=w_in_t[n_qkv:n_qkv + SSM_WIDTH],
             w_gates=w_in_t[n_qkv + SSM_WIDTH:])

    n_ada = w_ada.shape[2]
    b_ada_cols = lax.dynamic_slice(b_ada, (0, me * n_ada), (1, n_ada))
    mod_part = _ada_fwd(c_all, w_ada[0], b_ada_cols)
    (mod_slots,) = _exchange([(mod_part.reshape(N_DEV, B, n_ada), True)], name="scatter_modulation")
    mod = mod_slots.transpose(1, 0, 2).reshape(B, 1, 6 * D)

    later_sems = _exchange_start(later, "start_later_weights", gather=True, carry=[mod])
    (mod,) = later_sems[5]

    def late_weights(after):
        _, lands = _exchange_wait(*later_sems[:4], after, name="wait_later_weights")
        g = [lax.dynamic_update_index_in_dim(land, a, me, 0) for land, a in zip(lands, later)]
        more_w = dict(w_glu=g[0].reshape(SSM_WIDTH, SSM_WIDTH), w_proj_att=_slots_to_cols(g[1]),
                      w_proj_ssm=_slots_to_cols(g[2]), w_out=g[3].reshape(D, D), w_up=g[4],
                      w_down=g[6].reshape(D_FF, D))
        return more_w, dict(w_conv=_slots_to_cols(g[5]))

    in_flight = []

    def send_early(grads, carry=()):
        names = list(grads)
        handles = _exchange_start([grads[n] for n in names], "start_gradients_%d" % len(in_flight), gather=False,
                                  carry=carry)
        in_flight.append((names,) + handles[:4])
        return handles[4], handles[5]

    grad_x, d_mod, small = _local_step(x, mod, loss_target, W, late_weights, P, send_early)

    small_list = [small["loss_cols"], small["g_mix"], small["b_gate"], small["g_ab_re"], small["g_ab_im"],
                  _diag_blocks_in(small["d_bb_re"]), _diag_blocks_in(small["d_bb_im"]),
                  _diag_blocks_out(small["d_cc_re"]), -_diag_blocks_out(small["d_cc_im"]),
                  small["g_ffn"], small["b_conv"], small["g_final"], small["d_skip"], small["b_glu"]]
    small_packed, small_offs = _pack(small_list)
    small_sems = _exchange_start([small_packed, d_mod.reshape(B, 6 * D)], "start_small_gradients", gather=True)
    updated = [small_sems[4]]

    out = {}

    def update(name, parts, own=None):
        view = (lambda a: a[0].T) if name == "w_in" else (lambda a: a[0])
        back = (lambda a: a.T[None]) if name == "w_in" else (lambda a: a[None])
        g, dl, mn, vn = _adamw(view(args[name]), view(args["m_" + name]), view(args["v_" + name]), parts,
                               name="adamw_" + name, own=own)
        updated.append(vn)
        for key, val in (("grad_", g), ("delta_", dl), ("new_m_", mn), ("new_v_", vn)):
            out[key + name] = back(val)

    my_slot = me.astype(jnp.int32).reshape(1)
    for i, (names, send_sems, recv_sems, sent, lands) in enumerate(in_flight):
        sent, lands = _exchange_wait(send_sems, recv_sems, sent, lands, updated[-1], name="wait_gradients_%d" % i)
        for name, own_slots, landed in zip(names, sent, lands):
            update(name, landed, own=(own_slots, my_slot))

    own_small, small_lands = _exchange_wait(*small_sems[:4], updated[-1], name="wait_small_gradients")
    small_all, dmod_slots = [lax.dynamic_update_index_in_dim(land, a, me, 0)
                             for land, a in zip(small_lands, own_small)]
    dmod_all = dmod_slots.reshape(N_DEV * B, 6 * D)
    dmod_cols = lax.dynamic_slice(dmod_all, (0, me * n_ada), (N_DEV * B, n_ada))
    d_w_ada, d_b_ada = _ada_bwd(c_all, dmod_all, dmod_cols)
    update("w_ada", d_w_ada[None])

    loss_row, loss_n = small_offs[0]
    small_sum, loss_vec = _sum_parts(small_all, (loss_row, loss_row + loss_n // LANES))
    shapes = [(1, D), (1, D), (1, 2 * D), (SSM_GROUPS, SSM_STATE), (SSM_GROUPS, SSM_STATE), (SSM_COLS, SSM_GROUP_CH),
              (SSM_COLS, SSM_GROUP_CH), (1, SSM_GROUPS, SSM_GROUP_CH, SSM_STATE),
              (1, SSM_GROUPS, SSM_GROUP_CH, SSM_STATE), (1, D), (1, D_FF), (D,), (1, SSM_WIDTH), (1, SSM_WIDTH)]
    (_, s_g_mix, s_b_gate, s_ab_re, s_ab_im, s_bb_re, s_bb_im, s_c_re, s_c_im, s_g_ffn, s_b_conv, s_g_final,
     s_d_skip, s_b_glu) = _unpack(small_sum, small_offs, shapes)
    d_b_re2, d_b_im2, d_f_re, d_f_im = _s5_input_matrix_bwd(col(f_re), col(f_im), b_re2, b_im2, s_bb_re, s_bb_im)
    d_a_re, d_a_im, d_log_dt = _s5_params_bwd(a_re[0], a_im[0], log_dt[0].reshape(SSM_GROUPS, 1), s_ab_re, s_ab_im,
                                              d_f_re.reshape(SSM_GROUPS, SSM_STATE),
                                              d_f_im.reshape(SSM_GROUPS, SSM_STATE))
    grads_small = dict(b_ada=d_b_ada, g_mix=s_g_mix, b_gate=s_b_gate, a_re=d_a_re[None], a_im=d_a_im[None],
                       log_dt=d_log_dt.reshape(1, SSM_GROUPS), b_re=d_b_re2.reshape(b_re.shape),
                       b_im=d_b_im2.reshape(b_im.shape), c_re=s_c_re, c_im=s_c_im, d_skip=s_d_skip, b_glu=s_b_glu,
                       g_ffn=s_g_ffn, b_conv=s_b_conv, g_final=s_g_final)
    flat2 = lambda a: a.reshape(-1, a.shape[-1])
    res = _adamw_small([flat2(args[n]) for n in SMALL_ORDER], [flat2(args["m_" + n]) for n in SMALL_ORDER],
                       [flat2(args["v_" + n]) for n in SMALL_ORDER],
                       [flat2(grads_small[n].reshape(args[n].shape)) for n in SMALL_ORDER])
    for i, n in enumerate(SMALL_ORDER):
        for k, key in enumerate(("grad_", "delta_", "new_m_", "new_v_")):
            out[key + n] = res[4 * i + k].reshape(args[n].shape)

    order = ["w_ada", "b_ada", "g_mix", "w_in", "b_gate", "a_re", "a_im", "log_dt", "b_re", "b_im", "c_re", "c_im",
             "d_skip", "w_glu", "b_glu", "w_proj_att", "w_proj_ssm", "w_out", "g_ffn", "w_up", "w_conv", "b_conv",
             "w_down", "g_final"]
    loss = loss_vec[0, 0]
    return (loss, grad_x, *[out[k + n] for k in ("grad_", "delta_", "new_m_", "new_v_") for n in order])
```
